```python
import jax, jax.numpy as jnp
from jax import lax
import numpy as np

D_MODEL = 1024
BATCH = 8
SEQ = 4096
DEPTH = 1

D_MIX = D_MODEL
D_A = D_MIX // 2
D_B = D_MIX - D_A
N_HEADS_A = 4
HEAD_DIM_A = D_A // N_HEADS_A
CHUNK = 128
POOL_WINDOWS = (2, 4, 8, 16)
N_POOL_GROUPS = len(POOL_WINDOWS)
POOL_GROUP_DIM = D_B // N_POOL_GROUPS
D_FF = 4 * D_MODEL
N_MOD = 6
EPS = 1e-6

kernel_name = "hybrid_gmlp_pool_sqrelu_block"


def rms_norm(x, g):
    xf = x.astype(jnp.float32)
    y = xf * lax.rsqrt(jnp.mean(xf * xf, axis=-1, keepdims=True) + EPS)
    return (y * g.astype(jnp.float32)).astype(x.dtype)


def layer_norm(x, g, b):
    xf = x.astype(jnp.float32)
    mu = jnp.mean(xf, axis=-1, keepdims=True)
    var = jnp.mean(jnp.square(xf - mu), axis=-1, keepdims=True)
    y = (xf - mu) * lax.rsqrt(var + EPS)
    return (y * g.astype(jnp.float32) + b.astype(jnp.float32)).astype(x.dtype)


def spatial_gating(z_a, w_spatial, b_spatial, ln_v_gain, ln_v_bias):
    b, s, _ = z_a.shape
    z_a = jax.nn.gelu(z_a)
    u, v = z_a[..., :D_A], z_a[..., D_A:]
    v = layer_norm(v, ln_v_gain, ln_v_bias)
    v = v.reshape(b, s // CHUNK, CHUNK, N_HEADS_A, HEAD_DIM_A)
    mask = jnp.tril(jnp.ones((CHUNK, CHUNK), dtype=w_spatial.dtype))
    w_causal = w_spatial * mask[None]
    mixed = jnp.einsum("hts,bnshd->bnthd", w_causal, v)
    mixed = mixed + b_spatial.T[:, :, None]
    return u * mixed.reshape(b, s, D_A)


def multiscale_pool(z_b, w_pool, b_pool, pool_scale):
    b, s, _ = z_b.shape
    zg = z_b.reshape(b, s, N_POOL_GROUPS, POOL_GROUP_DIM)
    zf = zg.astype(jnp.float32)
    cs = jnp.concatenate(
        [jnp.zeros((b, 1, N_POOL_GROUPS, POOL_GROUP_DIM), jnp.float32), jnp.cumsum(zf, axis=1)],
        axis=1)
    pos = jnp.arange(s, dtype=jnp.float32)
    pooled = []
    for g, w in enumerate(POOL_WINDOWS):
        csg = cs[:, :, g]
        lower = jnp.concatenate(
            [jnp.zeros((b, w - 1, POOL_GROUP_DIM), jnp.float32), csg[:, : s + 1 - w]], axis=1)
        count = jnp.minimum(pos + 1.0, float(w))[None, :, None]
        pooled.append((csg[:, 1:] - lower) / count)
    pooled = jnp.stack(pooled, axis=2)
    diff = (pooled - zf).astype(z_b.dtype)
    y = jnp.einsum("bsgc,gcd->bsgd", diff, w_pool) + b_pool
    return y.reshape(b, s, D_B) * pool_scale


def _fwd_setup_inputs(seed: int = 0) -> dict:
    key = jax.random.key(seed)
    ks = jax.random.split(key, 20)
    f32 = jnp.float32
    nrm = lambda k, shape, scale: jax.random.normal(k, shape, f32) * scale
    return {
        "x": nrm(ks[0], (BATCH, SEQ, D_MODEL), 1.0),
        "c": nrm(ks[1], (BATCH, D_MODEL), 1.0),
        "w_ada": nrm(ks[2], (D_MODEL, N_MOD * D_MODEL), 0.5 * D_MODEL ** -0.5),
        "b_ada": nrm(ks[3], (N_MOD * D_MODEL,), 0.01),
        "norm1_pre": 1.0 + nrm(ks[4], (D_MODEL,), 0.02),
        "norm1_post": 1.0 + nrm(ks[5], (D_MODEL,), 0.02),
        "w_in": nrm(ks[6], (D_MODEL, 2 * D_A + D_B), D_MODEL ** -0.5),
        "w_spatial": nrm(ks[7], (N_HEADS_A, CHUNK, CHUNK), 0.5 * CHUNK ** -0.5),
        "b_spatial": 1.0 + nrm(ks[8], (N_HEADS_A, CHUNK), 0.02),
        "ln_v_gain": 1.0 + nrm(ks[9], (D_A,), 0.02),
        "ln_v_bias": nrm(ks[10], (D_A,), 0.02),
        "w_pool": nrm(ks[11], (N_POOL_GROUPS, POOL_GROUP_DIM, POOL_GROUP_DIM), POOL_GROUP_DIM ** -0.5),
        "b_pool": nrm(ks[12], (N_POOL_GROUPS, POOL_GROUP_DIM), 0.02),
        "pool_scale": 1.0 + nrm(ks[13], (D_B,), 0.02),
        "w_out": nrm(ks[14], (D_MIX, D_MODEL), D_MIX ** -0.5),
        "norm2_pre": 1.0 + nrm(ks[15], (D_MODEL,), 0.02),
        "norm2_post": 1.0 + nrm(ks[16], (D_MODEL,), 0.02),
        "w_fc1": nrm(ks[17], (DEPTH, D_MODEL, D_FF), D_MODEL ** -0.5)[0],
        "w_fc2": nrm(ks[18], (D_FF, D_MODEL), D_FF ** -0.5),
    }


def _fwd_reference(x, c, w_ada, b_ada, norm1_pre, norm1_post, w_in, w_spatial, b_spatial,
              ln_v_gain, ln_v_bias, w_pool, b_pool, pool_scale, w_out,
              norm2_pre, norm2_post, w_fc1, w_fc2):
    mod = jax.nn.silu(c) @ w_ada + b_ada
    shift1, scale1, gate1, shift2, scale2, gate2 = [
        m[:, None, :] for m in jnp.split(mod, N_MOD, axis=-1)]

    for _ in range(DEPTH):
        h = rms_norm(x, norm1_pre) * (1.0 + scale1) + shift1
        z = h @ w_in
        y_a = spatial_gating(z[..., : 2 * D_A], w_spatial, b_spatial, ln_v_gain, ln_v_bias)
        y_b = multiscale_pool(z[..., 2 * D_A:], w_pool, b_pool, pool_scale)
        mix = jnp.concatenate([y_a, y_b], axis=-1) @ w_out
        x = x + gate1 * rms_norm(mix, norm1_post)

        h = rms_norm(x, norm2_pre) * (1.0 + scale2) + shift2
        f = jnp.square(jax.nn.relu(h @ w_fc1)) @ w_fc2
        x = x + gate2 * rms_norm(f, norm2_post)
    return x


import jax as _jax
import jax.numpy as _jnp

TWIN_FORMAT = 'train_step'
FWD_PARAMS = ['x', 'c', 'w_ada', 'b_ada', 'norm1_pre', 'norm1_post', 'w_in', 'w_spatial', 'b_spatial', 'ln_v_gain', 'ln_v_bias', 'w_pool', 'b_pool', 'pool_scale', 'w_out', 'norm2_pre', 'norm2_post', 'w_fc1', 'w_fc2']
TWIN_WEIGHTS = ['w_ada', 'b_ada', 'norm1_pre', 'norm1_post', 'w_in', 'w_spatial', 'b_spatial', 'ln_v_gain', 'ln_v_bias', 'w_pool', 'b_pool', 'pool_scale', 'w_out', 'norm2_pre', 'norm2_post', 'w_fc1', 'w_fc2']
TWIN_DIFF_INPUT = 'x'
TWIN_INPUTS = ['x', 'c', 'w_ada', 'b_ada', 'norm1_pre', 'norm1_post', 'w_in', 'w_spatial', 'b_spatial', 'ln_v_gain', 'ln_v_bias', 'w_pool', 'b_pool', 'pool_scale', 'w_out', 'norm2_pre', 'norm2_post', 'w_fc1', 'w_fc2', 'loss_target', 'm_w_ada', 'm_b_ada', 'm_norm1_pre', 'm_norm1_post', 'm_w_in', 'm_w_spatial', 'm_b_spatial', 'm_ln_v_gain', 'm_ln_v_bias', 'm_w_pool', 'm_b_pool', 'm_pool_scale', 'm_w_out', 'm_norm2_pre', 'm_norm2_post', 'm_w_fc1', 'm_w_fc2', 'v_w_ada', 'v_b_ada', 'v_norm1_pre', 'v_norm1_post', 'v_w_in', 'v_w_spatial', 'v_b_spatial', 'v_ln_v_gain', 'v_ln_v_bias', 'v_w_pool', 'v_b_pool', 'v_pool_scale', 'v_w_out', 'v_norm2_pre', 'v_norm2_post', 'v_w_fc1', 'v_w_fc2']
TWIN_OUTPUTS = ['loss', 'grad_x', 'grad_w_ada', 'grad_b_ada', 'grad_norm1_pre', 'grad_norm1_post', 'grad_w_in', 'grad_w_spatial', 'grad_b_spatial', 'grad_ln_v_gain', 'grad_ln_v_bias', 'grad_w_pool', 'grad_b_pool', 'grad_pool_scale', 'grad_w_out', 'grad_norm2_pre', 'grad_norm2_post', 'grad_w_fc1', 'grad_w_fc2', 'delta_w_ada', 'delta_b_ada', 'delta_norm1_pre', 'delta_norm1_post', 'delta_w_in', 'delta_w_spatial', 'delta_b_spatial', 'delta_ln_v_gain', 'delta_ln_v_bias', 'delta_w_pool', 'delta_b_pool', 'delta_pool_scale', 'delta_w_out', 'delta_norm2_pre', 'delta_norm2_post', 'delta_w_fc1', 'delta_w_fc2', 'new_m_w_ada', 'new_m_b_ada', 'new_m_norm1_pre', 'new_m_norm1_post', 'new_m_w_in', 'new_m_w_spatial', 'new_m_b_spatial', 'new_m_ln_v_gain', 'new_m_ln_v_bias', 'new_m_w_pool', 'new_m_b_pool', 'new_m_pool_scale', 'new_m_w_out', 'new_m_norm2_pre', 'new_m_norm2_post', 'new_m_w_fc1', 'new_m_w_fc2', 'new_v_w_ada', 'new_v_b_ada', 'new_v_norm1_pre', 'new_v_norm1_post', 'new_v_w_in', 'new_v_w_spatial', 'new_v_b_spatial', 'new_v_ln_v_gain', 'new_v_ln_v_bias', 'new_v_w_pool', 'new_v_b_pool', 'new_v_pool_scale', 'new_v_w_out', 'new_v_norm2_pre', 'new_v_norm2_post', 'new_v_w_fc1', 'new_v_w_fc2']
TWIN_LEAF_KINDS = {'loss': 'loss', 'grad_x': 'grad_x', 'grad_w_ada': 'grad_w', 'grad_b_ada': 'grad_w', 'grad_norm1_pre': 'grad_w', 'grad_norm1_post': 'grad_w', 'grad_w_in': 'grad_w', 'grad_w_spatial': 'grad_w', 'grad_b_spatial': 'grad_w', 'grad_ln_v_gain': 'grad_w', 'grad_ln_v_bias': 'grad_w', 'grad_w_pool': 'grad_w', 'grad_b_pool': 'grad_w', 'grad_pool_scale': 'grad_w', 'grad_w_out': 'grad_w', 'grad_norm2_pre': 'grad_w', 'grad_norm2_post': 'grad_w', 'grad_w_fc1': 'grad_w', 'grad_w_fc2': 'grad_w', 'delta_w_ada': 'delta_w', 'delta_b_ada': 'delta_w', 'delta_norm1_pre': 'delta_w', 'delta_norm1_post': 'delta_w', 'delta_w_in': 'delta_w', 'delta_w_spatial': 'delta_w', 'delta_b_spatial': 'delta_w', 'delta_ln_v_gain': 'delta_w', 'delta_ln_v_bias': 'delta_w', 'delta_w_pool': 'delta_w', 'delta_b_pool': 'delta_w', 'delta_pool_scale': 'delta_w', 'delta_w_out': 'delta_w', 'delta_norm2_pre': 'delta_w', 'delta_norm2_post': 'delta_w', 'delta_w_fc1': 'delta_w', 'delta_w_fc2': 'delta_w', 'new_m_w_ada': 'new_m', 'new_m_b_ada': 'new_m', 'new_m_norm1_pre': 'new_m', 'new_m_norm1_post': 'new_m', 'new_m_w_in': 'new_m', 'new_m_w_spatial': 'new_m', 'new_m_b_spatial': 'new_m', 'new_m_ln_v_gain': 'new_m', 'new_m_ln_v_bias': 'new_m', 'new_m_w_pool': 'new_m', 'new_m_b_pool': 'new_m', 'new_m_pool_scale': 'new_m', 'new_m_w_out': 'new_m', 'new_m_norm2_pre': 'new_m', 'new_m_norm2_post': 'new_m', 'new_m_w_fc1': 'new_m', 'new_m_w_fc2': 'new_m', 'new_v_w_ada': 'new_v', 'new_v_b_ada': 'new_v', 'new_v_norm1_pre': 'new_v', 'new_v_norm1_post': 'new_v', 'new_v_w_in': 'new_v', 'new_v_w_spatial': 'new_v', 'new_v_b_spatial': 'new_v', 'new_v_ln_v_gain': 'new_v', 'new_v_ln_v_bias': 'new_v', 'new_v_w_pool': 'new_v', 'new_v_b_pool': 'new_v', 'new_v_pool_scale': 'new_v', 'new_v_w_out': 'new_v', 'new_v_norm2_pre': 'new_v', 'new_v_norm2_post': 'new_v', 'new_v_w_fc1': 'new_v', 'new_v_w_fc2': 'new_v'}


def _forward(args):
    return _fwd_reference(*[args[k] for k in FWD_PARAMS])


def _output_shape():
    out = _jax.eval_shape(lambda: _forward(_fwd_setup_inputs(0)))
    return out.shape, out.dtype

N_MICROBATCH = 1
ADAM_LR = 0.001
ADAM_B1 = 0.9
ADAM_B2 = 0.999
ADAM_EPS = 1e-08
ADAM_WD = 0.01
ADAM_STEP = 10
PER_EXAMPLE_BATCH_AXIS = {'x': 0, 'c': 0, 'loss_target': 0}
SHARED_INPUTS = []
_WEIGHT_DTYPES = {'w_ada': _jnp.float32, 'b_ada': _jnp.float32, 'norm1_pre': _jnp.float32, 'norm1_post': _jnp.float32, 'w_in': _jnp.float32, 'w_spatial': _jnp.float32, 'b_spatial': _jnp.float32, 'ln_v_gain': _jnp.float32, 'ln_v_bias': _jnp.float32, 'w_pool': _jnp.float32, 'b_pool': _jnp.float32, 'pool_scale': _jnp.float32, 'w_out': _jnp.float32, 'norm2_pre': _jnp.float32, 'norm2_post': _jnp.float32, 'w_fc1': _jnp.float32, 'w_fc2': _jnp.float32}
MOMENT_SCALE = {'w_ada': 1.816806e+00, 'b_ada': 3.472949e+00, 'norm1_pre': 1.266438e-01, 'norm1_post': 3.777276e+00, 'w_in': 1.289118e-01, 'w_spatial': 5.275730e-02, 'b_spatial': 7.327051e-02, 'ln_v_gain': 2.922842e-02, 'ln_v_bias': 2.892298e-02, 'w_pool': 1.567093e-01, 'b_pool': 1.075846e+00, 'pool_scale': 1.919840e-01, 'w_out': 2.771720e-01, 'norm2_pre': 1.412677e-01, 'norm2_post': 3.938913e+00, 'w_fc1': 1.019058e-01, 'w_fc2': 4.745470e-01}


def _to_microbatches(a, axis):
    t = _jnp.moveaxis(a, axis, 0)
    t = t.reshape((N_MICROBATCH, t.shape[0] // N_MICROBATCH) + t.shape[1:])
    return _jnp.moveaxis(t, 1, axis + 1)


def setup_inputs(seed: int = 0) -> dict:
    inp = _fwd_setup_inputs(seed)
    key = _jax.random.fold_in(_jax.random.key(seed), 7919)
    shape, _ = _output_shape()
    out = dict(inp)
    out["loss_target"] = _jax.random.normal(_jax.random.fold_in(key, 0), shape, _jnp.float32)
    for i, name in enumerate(TWIN_WEIGHTS):
        w = inp[name].astype(_jnp.float32)
        if MOMENT_SCALE is None:
            s = _jnp.sqrt(_jnp.mean(_jnp.square(w)) + 1e-30)
        else:
            s = MOMENT_SCALE[name]
        km, kv = _jax.random.split(_jax.random.fold_in(key, i + 1))
        out[name] = w
        out["m_" + name] = s * _jax.random.normal(km, w.shape, _jnp.float32)
        out["v_" + name] = (s * s) * _jax.random.uniform(kv, w.shape, _jnp.float32, 0.5, 1.5)
    if N_MICROBATCH > 1:
        for name, axis in PER_EXAMPLE_BATCH_AXIS.items():
            out[name] = _to_microbatches(out[name], axis)
    return {'x': out['x'], 'c': out['c'], 'w_ada': out['w_ada'], 'b_ada': out['b_ada'], 'norm1_pre': out['norm1_pre'], 'norm1_post': out['norm1_post'], 'w_in': out['w_in'], 'w_spatial': out['w_spatial'], 'b_spatial': out['b_spatial'], 'ln_v_gain': out['ln_v_gain'], 'ln_v_bias': out['ln_v_bias'], 'w_pool': out['w_pool'], 'b_pool': out['b_pool'], 'pool_scale': out['pool_scale'], 'w_out': out['w_out'], 'norm2_pre': out['norm2_pre'], 'norm2_post': out['norm2_post'], 'w_fc1': out['w_fc1'], 'w_fc2': out['w_fc2'], 'loss_target': out['loss_target'], 'm_w_ada': out['m_w_ada'], 'm_b_ada': out['m_b_ada'], 'm_norm1_pre': out['m_norm1_pre'], 'm_norm1_post': out['m_norm1_post'], 'm_w_in': out['m_w_in'], 'm_w_spatial': out['m_w_spatial'], 'm_b_spatial': out['m_b_spatial'], 'm_ln_v_gain': out['m_ln_v_gain'], 'm_ln_v_bias': out['m_ln_v_bias'], 'm_w_pool': out['m_w_pool'], 'm_b_pool': out['m_b_pool'], 'm_pool_scale': out['m_pool_scale'], 'm_w_out': out['m_w_out'], 'm_norm2_pre': out['m_norm2_pre'], 'm_norm2_post': out['m_norm2_post'], 'm_w_fc1': out['m_w_fc1'], 'm_w_fc2': out['m_w_fc2'], 'v_w_ada': out['v_w_ada'], 'v_b_ada': out['v_b_ada'], 'v_norm1_pre': out['v_norm1_pre'], 'v_norm1_post': out['v_norm1_post'], 'v_w_in': out['v_w_in'], 'v_w_spatial': out['v_w_spatial'], 'v_b_spatial': out['v_b_spatial'], 'v_ln_v_gain': out['v_ln_v_gain'], 'v_ln_v_bias': out['v_ln_v_bias'], 'v_w_pool': out['v_w_pool'], 'v_b_pool': out['v_b_pool'], 'v_pool_scale': out['v_pool_scale'], 'v_w_out': out['v_w_out'], 'v_norm2_pre': out['v_norm2_pre'], 'v_norm2_post': out['v_norm2_post'], 'v_w_fc1': out['v_w_fc1'], 'v_w_fc2': out['v_w_fc2']}


def _loss(weights, diff, rest, loss_target):
    with _jax.named_scope("forward"):
        args = {**rest, TWIN_DIFF_INPUT: diff, **{k: w.astype(_WEIGHT_DTYPES[k]) for k, w in weights.items()}}
        y = _forward(args)
    with _jax.named_scope("loss_head"):
        err = _jnp.square(y.astype(_jnp.float32) - loss_target)
        return 0.5 * _jnp.sum(_jnp.mean(err, axis=-1)) if err.ndim else 0.5 * err


def _adamw(w, g, m, v):
    m = ADAM_B1 * m + (1.0 - ADAM_B1) * g
    v = ADAM_B2 * v + (1.0 - ADAM_B2) * _jnp.square(g)
    m_hat = m / (1.0 - ADAM_B1 ** ADAM_STEP)
    v_hat = v / (1.0 - ADAM_B2 ** ADAM_STEP)
    delta = -ADAM_LR * (m_hat / (_jnp.sqrt(v_hat) + ADAM_EPS) + ADAM_WD * w)
    return delta, m, v


def reference(x, c, w_ada, b_ada, norm1_pre, norm1_post, w_in, w_spatial, b_spatial, ln_v_gain, ln_v_bias, w_pool, b_pool, pool_scale, w_out, norm2_pre, norm2_post, w_fc1, w_fc2, loss_target, m_w_ada, m_b_ada, m_norm1_pre, m_norm1_post, m_w_in, m_w_spatial, m_b_spatial, m_ln_v_gain, m_ln_v_bias, m_w_pool, m_b_pool, m_pool_scale, m_w_out, m_norm2_pre, m_norm2_post, m_w_fc1, m_w_fc2, v_w_ada, v_b_ada, v_norm1_pre, v_norm1_post, v_w_in, v_w_spatial, v_b_spatial, v_ln_v_gain, v_ln_v_bias, v_w_pool, v_b_pool, v_pool_scale, v_w_out, v_norm2_pre, v_norm2_post, v_w_fc1, v_w_fc2):
    given = dict(x=x, c=c, w_ada=w_ada, b_ada=b_ada, norm1_pre=norm1_pre, norm1_post=norm1_post, w_in=w_in, w_spatial=w_spatial, b_spatial=b_spatial, ln_v_gain=ln_v_gain, ln_v_bias=ln_v_bias, w_pool=w_pool, b_pool=b_pool, pool_scale=pool_scale, w_out=w_out, norm2_pre=norm2_pre, norm2_post=norm2_post, w_fc1=w_fc1, w_fc2=w_fc2, loss_target=loss_target, m_w_ada=m_w_ada, m_b_ada=m_b_ada, m_norm1_pre=m_norm1_pre, m_norm1_post=m_norm1_post, m_w_in=m_w_in, m_w_spatial=m_w_spatial, m_b_spatial=m_b_spatial, m_ln_v_gain=m_ln_v_gain, m_ln_v_bias=m_ln_v_bias, m_w_pool=m_w_pool, m_b_pool=m_b_pool, m_pool_scale=m_pool_scale, m_w_out=m_w_out, m_norm2_pre=m_norm2_pre, m_norm2_post=m_norm2_post, m_w_fc1=m_w_fc1, m_w_fc2=m_w_fc2, v_w_ada=v_w_ada, v_b_ada=v_b_ada, v_norm1_pre=v_norm1_pre, v_norm1_post=v_norm1_post, v_w_in=v_w_in, v_w_spatial=v_w_spatial, v_b_spatial=v_b_spatial, v_ln_v_gain=v_ln_v_gain, v_ln_v_bias=v_ln_v_bias, v_w_pool=v_w_pool, v_b_pool=v_b_pool, v_pool_scale=v_pool_scale, v_w_out=v_w_out, v_norm2_pre=v_norm2_pre, v_norm2_post=v_norm2_post, v_w_fc1=v_w_fc1, v_w_fc2=v_w_fc2)
    weights = {n: given[n] for n in TWIN_WEIGHTS}
    shared = {n: given[n] for n in SHARED_INPUTS}
    per_example = {n: given[n] for n in ['x', 'c']}
    grad_fn = _jax.value_and_grad(_loss, argnums=(0, 1))

    def one_microbatch(ex, loss_target):
        ex = dict(ex)
        diff = ex.pop(TWIN_DIFF_INPUT)
        return grad_fn(weights, diff, {**shared, **ex}, loss_target)

    if N_MICROBATCH == 1:
        loss, (grad_w, grad_x) = one_microbatch(per_example, given["loss_target"])
    else:
        def body(carry, xs):
            loss_sum, grad_sum = carry
            l_k, (gw_k, gx_k) = one_microbatch(xs[0], xs[1])
            with _jax.named_scope("update"):
                return (loss_sum + l_k, _jax.tree.map(_jnp.add, grad_sum, gw_k)), gx_k

        init = (_jnp.zeros((), _jnp.float32), _jax.tree.map(_jnp.zeros_like, weights))
        (loss, grad_w), grad_x = _jax.lax.scan(body, init, (per_example, given["loss_target"]))
    with _jax.named_scope("update"):
        delta_w, new_m, new_v = {}, {}, {}
        for n in TWIN_WEIGHTS:
            delta_w[n], new_m[n], new_v[n] = _adamw(weights[n], grad_w[n], given["m_" + n], given["v_" + n])
    return (loss, grad_x, *[grad_w[n] for n in TWIN_WEIGHTS], *[delta_w[n] for n in TWIN_WEIGHTS],
            *[new_m[n] for n in TWIN_WEIGHTS], *[new_v[n] for n in TWIN_WEIGHTS])
```

```python
import functools

import jax
import jax.numpy as jnp
from jax import lax
from jax.experimental import pallas as pl
from jax.experimental.pallas import tpu as pltpu

F32 = jnp.float32
BF16 = jnp.bfloat16
MESH = pl.DeviceIdType.MESH

D_MODEL = 1024
D_A = 512
D_B = 512
D_Z = 2 * D_A + D_B
N_HEADS = 4
HEAD_DIM = 128
CHUNK = 128
POOL_WINDOWS = (2, 4, 8, 16)
GROUP_DIM = 128
D_FF = 4096
N_MOD = 6
EPS = 1e-6
HALO = 16
N_CHIPS = 4
N_DEV = 8

ADAM_LR = 0.001
ADAM_B1 = 0.9
ADAM_B2 = 0.999
ADAM_EPS = 1e-08
ADAM_WD = 0.01
ADAM_STEP = 10

VMEM_LIMIT = 56 * 1024 * 1024

_VMEM = pl.BlockSpec(memory_space=pltpu.VMEM)
_ANY = pl.BlockSpec(memory_space=pl.ANY)


def _params(n_grid_axes=1):
    return pltpu.CompilerParams(dimension_semantics=("arbitrary",) * n_grid_axes, vmem_limit_bytes=VMEM_LIMIT)


def _rows(ts, width):
    return pl.BlockSpec((ts, width), lambda i: (i, 0))


def _const(shape):
    return pl.BlockSpec(shape, lambda i: (0,) * len(shape))


def _dot(a, b):
    return jnp.dot(a, b, preferred_element_type=F32)


def _dot_nt(a, b):
    return lax.dot_general(a, b, (((1,), (1,)), ((), ())), preferred_element_type=F32)


def _dot_tn(a, b):
    return lax.dot_general(a, b, (((0,), (0,)), ((), ())), preferred_element_type=F32)


def _rowmean(v):
    return jnp.mean(v, axis=-1, keepdims=True)


def _colsum(v):
    return jnp.sum(v, axis=0, keepdims=True)


def _gelu_parts(z):
    k0 = 0.7978845608028654
    k1 = 0.044715
    z2 = z * z
    t = jnp.tanh(k0 * (z + k1 * z * z2))
    g = 0.5 * z * (1.0 + t)
    dg = 0.5 * (1.0 + t) + 0.5 * z * (1.0 - t * t) * (k0 * (1.0 + 3.0 * k1 * z2))
    return g, dg


def _tril_weights(ws_ref):
    r = lax.broadcasted_iota(jnp.int32, (CHUNK, CHUNK), 0)
    s = lax.broadcasted_iota(jnp.int32, (CHUNK, CHUNK), 1)
    mask = (s <= r).astype(F32)
    return [(ws_ref[h] * mask).astype(BF16) for h in range(N_HEADS)]


def _window_counts(first_row, n_rows):
    pos = (first_row + lax.broadcasted_iota(jnp.int32, (n_rows, 1), 0)).astype(F32)
    return pos, [jnp.minimum(pos + 1.0, float(w)) for w in POOL_WINDOWS]


def _causal_window_sums(ext):
    out = []
    e = ext
    shift = 1
    for g in range(len(POOL_WINDOWS)):
        e = e + pltpu.roll(e, shift, 0)
        shift *= 2
        out.append(e[:, g * GROUP_DIM:(g + 1) * GROUP_DIM])
    return out


def _anticausal_window_sums(ext):
    n = ext.shape[0]
    out = []
    e = ext
    shift = 1
    for g in range(len(POOL_WINDOWS)):
        e = e + pltpu.roll(e, n - shift, 0)
        shift *= 2
        out.append(e[:, g * GROUP_DIM:(g + 1) * GROUP_DIM])
    return out


def _fwd_in(x, mod6, n1pre, win_g, ts):
    s_len = x.shape[0]
    cs = D_Z // N_CHIPS

    def body(x_ref, mod_ref, g_ref, w_ref, z_ref, h_ref):
        xv = x_ref[...]
        r = lax.rsqrt(_rowmean(xv * xv) + EPS)
        h = ((xv * r) * g_ref[...]) * (1.0 + mod_ref[1:2, :]) + mod_ref[0:1, :]
        hb = h.astype(BF16)
        h_ref[...] = hb
        for j in range(N_CHIPS):
            z_ref[:, j * cs:(j + 1) * cs] = _dot(hb, w_ref[j])

    return pl.pallas_call(
        body, name="fwd_in", grid=(s_len // ts,),
        in_specs=[_rows(ts, D_MODEL), _const((N_MOD, D_MODEL)), _const((1, D_MODEL)), _VMEM],
        out_specs=[_rows(ts, D_Z), _rows(ts, D_MODEL)],
        out_shape=[jax.ShapeDtypeStruct((s_len, D_Z), F32), jax.ShapeDtypeStruct((s_len, D_MODEL), BF16)],
        compiler_params=_params(),
    )(x, mod6, n1pre, win_g)


def _mixer_forward_tile(za, wc, bsp_ref, gain, bias, mixed_ref):
    ga, dga = _gelu_parts(za)
    u = ga[:, :D_A]
    v = ga[:, D_A:]
    mu = _rowmean(v)
    vc = v - mu
    rstd = lax.rsqrt(_rowmean(vc * vc) + EPS)
    vhat = vc * rstd
    vn = (vhat * gain + bias).astype(BF16)
    ts = za.shape[0]
    for k in range(ts // CHUNK):
        for h in range(N_HEADS):
            blk = vn[k * CHUNK:(k + 1) * CHUNK, h * HEAD_DIM:(h + 1) * HEAD_DIM]
            mixed_ref[k * CHUNK:(k + 1) * CHUNK, h * HEAD_DIM:(h + 1) * HEAD_DIM] = (
                _dot(wc[h], blk) + bsp_ref[:, h * HEAD_DIM:(h + 1) * HEAD_DIM])
    return u, vhat, rstd, vn, dga


def _mixer_fwd(z, w_spatial, bsp_full, gain, bias, w_pool, b_pool, pool_scale, ts):
    s_len = z.shape[0]
    nb = ts // HALO

    def body(z_ref, zprev_ref, ws_ref, bsp_ref, gain_ref, bias_ref, wp_ref, bp_ref, ps_ref, y_ref, mixed_ref):
        i = pl.program_id(0)
        wc = _tril_weights(ws_ref)
        u, _, _, _, _ = _mixer_forward_tile(z_ref[:, :2 * D_A], wc, bsp_ref, gain_ref[...], bias_ref[...], mixed_ref)
        y_ref[:, :D_A] = (u * mixed_ref[...]).astype(BF16)

        zb = z_ref[:, 2 * D_A:]
        prev = jnp.where(i == 0, 0.0, zprev_ref[...])
        ext = jnp.concatenate([prev, zb], axis=0)
        sums = _causal_window_sums(ext)
        _, counts = _window_counts(i * ts, ts)
        for g in range(len(POOL_WINDOWS)):
            lanes = slice(g * GROUP_DIM, (g + 1) * GROUP_DIM)
            diff = sums[g][HALO:, :] / counts[g] - zb[:, lanes]
            lin = _dot(diff.astype(BF16), wp_ref[g].astype(BF16)) + bp_ref[:, lanes]
            y_ref[:, D_A + g * GROUP_DIM:D_A + (g + 1) * GROUP_DIM] = (lin * ps_ref[:, lanes]).astype(BF16)

    return pl.pallas_call(
        body, name="mixer_fwd", grid=(s_len // ts,),
        in_specs=[_rows(ts, D_Z),
                  pl.BlockSpec((HALO, D_B), lambda i: (jnp.maximum(i * nb - 1, 0), 2)),
                  _const((N_HEADS, CHUNK, CHUNK)), _const((CHUNK, D_A)), _const((1, D_A)), _const((1, D_A)),
                  _const((N_HEADS, GROUP_DIM, GROUP_DIM)), _const((1, D_B)), _const((1, D_B))],
        out_specs=_rows(ts, D_MODEL),
        out_shape=jax.ShapeDtypeStruct((s_len, D_MODEL), BF16),
        scratch_shapes=[pltpu.VMEM((ts, D_A), F32)],
        compiler_params=_params(),
    )(z, z, w_spatial, bsp_full, gain, bias, w_pool, b_pool, pool_scale)


def _fwd_out(ycat, x, mod6, n1post, n2pre, wout_g, ts):
    s_len = x.shape[0]
    rs = D_MODEL // N_CHIPS

    def body(y_ref, x_ref, mod_ref, g1_ref, g2_ref, w_ref, mix_ref, x1_ref, h2_ref):
        yb = y_ref[...]
        mix = _dot(yb[:, 0:rs], w_ref[0])
        for j in range(1, N_CHIPS):
            mix = mix + _dot(yb[:, j * rs:(j + 1) * rs], w_ref[j])
        mix_ref[...] = mix
        r2 = lax.rsqrt(_rowmean(mix * mix) + EPS)
        x1 = x_ref[...] + mod_ref[2:3, :] * ((mix * r2) * g1_ref[...])
        x1_ref[...] = x1
        r3 = lax.rsqrt(_rowmean(x1 * x1) + EPS)
        h2 = ((x1 * r3) * g2_ref[...]) * (1.0 + mod_ref[4:5, :]) + mod_ref[3:4, :]
        h2_ref[...] = h2.astype(BF16)

    return pl.pallas_call(
        body, name="fwd_out", grid=(s_len // ts,),
        in_specs=[_rows(ts, D_MODEL), _rows(ts, D_MODEL), _const((N_MOD, D_MODEL)), _const((1, D_MODEL)),
                  _const((1, D_MODEL)), _VMEM],
        out_specs=[_rows(ts, D_MODEL), _rows(ts, D_MODEL), _rows(ts, D_MODEL)],
        out_shape=[jax.ShapeDtypeStruct((s_len, D_MODEL), F32), jax.ShapeDtypeStruct((s_len, D_MODEL), F32),
                   jax.ShapeDtypeStruct((s_len, D_MODEL), BF16)],
        compiler_params=_params(),
    )(ycat, x, mod6, n1post, n2pre, wout_g)


def _fwd_fc1(h2, fc1_g, ts):
    s_len = h2.shape[0]
    cs = D_FF // N_CHIPS

    def body(h_ref, w_ref, q_ref):
        hb = h_ref[...]
        for j in range(N_CHIPS):
            p = jnp.maximum(_dot(hb, w_ref[j]), 0.0)
            q_ref[:, j * cs:(j + 1) * cs] = (p * p).astype(BF16)

    return pl.pallas_call(
        body, name="fwd_fc1", grid=(s_len // ts,),
        in_specs=[_rows(ts, D_MODEL), _VMEM],
        out_specs=_rows(ts, D_FF),
        out_shape=jax.ShapeDtypeStruct((s_len, D_FF), BF16),
        compiler_params=_params(),
    )(h2, fc1_g)


def _fwd_fc2_loss(q, x1, target, mod6, n2post, fc2_g, ts):
    s_len = q.shape[0]
    rs = D_FF // N_CHIPS

    def body(q_ref, x1_ref, t_ref, mod_ref, g_ref, w_ref, dy_ref, df_ref, loss_ref, dgate_ref, dg_ref):
        i = pl.program_id(0)

        @pl.when(i == 0)
        def _():
            loss_ref[...] = jnp.zeros_like(loss_ref)
            dgate_ref[...] = jnp.zeros_like(dgate_ref)
            dg_ref[...] = jnp.zeros_like(dg_ref)

        f = _dot(q_ref[:, 0:rs], w_ref[0])
        for j in range(1, N_CHIPS):
            f = f + _dot(q_ref[:, j * rs:(j + 1) * rs], w_ref[j])
        r4 = lax.rsqrt(_rowmean(f * f) + EPS)
        fh = f * r4
        gate = mod_ref[5:6, :]
        gn = g_ref[...]
        err = (x1_ref[...] + gate * (fh * gn)) - t_ref[...]
        loss_ref[...] += 0.5 * jnp.sum(_rowmean(err * err), axis=0, keepdims=True)
        dy = err * (1.0 / D_MODEL)
        dy_ref[...] = dy
        dgate_ref[...] += _colsum(dy * (fh * gn))
        dg_ref[...] += _colsum((dy * gate) * fh)
        gh = (dy * gate) * gn
        df_ref[...] = (r4 * (gh - fh * _rowmean(gh * fh))).astype(BF16)

    return pl.pallas_call(
        body, name="fwd_fc2_loss", grid=(s_len // ts,),
        in_specs=[_rows(ts, D_FF), _rows(ts, D_MODEL), _rows(ts, D_MODEL), _const((N_MOD, D_MODEL)),
                  _const((1, D_MODEL)), _VMEM],
        out_specs=[_rows(ts, D_MODEL), _rows(ts, D_MODEL), _const((1, 1)), _const((1, D_MODEL)), _const((1, D_MODEL))],
        out_shape=[jax.ShapeDtypeStruct((s_len, D_MODEL), F32), jax.ShapeDtypeStruct((s_len, D_MODEL), BF16),
                   jax.ShapeDtypeStruct((1, 1), F32), jax.ShapeDtypeStruct((1, D_MODEL), F32),
                   jax.ShapeDtypeStruct((1, D_MODEL), F32)],
        compiler_params=_params(),
    )(q, x1, target, mod6, n2post, fc2_g)


def _bwd_fc2(df, q, fc2_g, ts):
    s_len = df.shape[0]
    cs = D_FF // N_CHIPS

    def body(df_ref, q_ref, w_ref, dp_ref):
        dfb = df_ref[...]
        for j in range(N_CHIPS):
            dq = _dot_nt(dfb, w_ref[j])
            p = jnp.sqrt(q_ref[:, j * cs:(j + 1) * cs].astype(F32))
            dp_ref[:, j * cs:(j + 1) * cs] = (dq * (2.0 * p)).astype(BF16)

    return pl.pallas_call(
        body, name="bwd_fc2", grid=(s_len // ts,),
        in_specs=[_rows(ts, D_MODEL), _rows(ts, D_FF), _VMEM],
        out_specs=_rows(ts, D_FF),
        out_shape=jax.ShapeDtypeStruct((s_len, D_FF), BF16),
        compiler_params=_params(),
    )(df, q, fc2_g)


def _bwd_fc1_out(dp, dy, x1, mix, mod6, n2pre, n1post, fc1_g, wout_g, ts):
    s_len = dp.shape[0]
    cs = D_FF // N_CHIPS
    rs = D_MODEL // N_CHIPS

    def body(dp_ref, dy_ref, x1_ref, mix_ref, mod_ref, g2_ref, g1_ref, w1_ref, wo_ref,
             dx1_ref, dmix_ref, dyc_ref, dshift2_ref, da2_ref, dgate1_ref, dg1_ref):
        i = pl.program_id(0)

        @pl.when(i == 0)
        def _():
            for ref in (dshift2_ref, da2_ref, dgate1_ref, dg1_ref):
                ref[...] = jnp.zeros_like(ref)

        dh2 = _dot_nt(dp_ref[:, 0:cs], w1_ref[0])
        for j in range(1, N_CHIPS):
            dh2 = dh2 + _dot_nt(dp_ref[:, j * cs:(j + 1) * cs], w1_ref[j])
        x1 = x1_ref[...]
        r3 = lax.rsqrt(_rowmean(x1 * x1) + EPS)
        xh = x1 * r3
        a2 = g2_ref[...] * (1.0 + mod_ref[4:5, :])
        dshift2_ref[...] += _colsum(dh2)
        da2_ref[...] += _colsum(dh2 * xh)
        dxh = dh2 * a2
        dx1 = dy_ref[...] + r3 * (dxh - xh * _rowmean(dxh * xh))
        dx1_ref[...] = dx1

        mix = mix_ref[...]
        r2 = lax.rsqrt(_rowmean(mix * mix) + EPS)
        mh = mix * r2
        gate = mod_ref[2:3, :]
        gn = g1_ref[...]
        dgate1_ref[...] += _colsum(dx1 * (mh * gn))
        dg1_ref[...] += _colsum((dx1 * gate) * mh)
        gh = (dx1 * gate) * gn
        dmix = (r2 * (gh - mh * _rowmean(gh * mh))).astype(BF16)
        dmix_ref[...] = dmix
        for j in range(N_CHIPS):
            dyc_ref[:, j * rs:(j + 1) * rs] = _dot_nt(dmix, wo_ref[j])

    vec = jax.ShapeDtypeStruct((1, D_MODEL), F32)
    return pl.pallas_call(
        body, name="bwd_fc1_out", grid=(s_len // ts,),
        in_specs=[_rows(ts, D_FF), _rows(ts, D_MODEL), _rows(ts, D_MODEL), _rows(ts, D_MODEL),
                  _const((N_MOD, D_MODEL)), _const((1, D_MODEL)), _const((1, D_MODEL)), _VMEM, _VMEM],
        out_specs=[_rows(ts, D_MODEL), _rows(ts, D_MODEL), _rows(ts, D_MODEL)] + [_const((1, D_MODEL))] * 4,
        out_shape=[jax.ShapeDtypeStruct((s_len, D_MODEL), F32), jax.ShapeDtypeStruct((s_len, D_MODEL), BF16),
                   jax.ShapeDtypeStruct((s_len, D_MODEL), F32), vec, vec, vec, vec],
        compiler_params=_params(),
    )(dp, dy, x1, mix, mod6, n2pre, n1post, fc1_g, wout_g)


def _mixer_bwd(z, dyc, w_spatial, bsp_full, gain, bias, w_pool, b_pool, pool_scale, ts):
    s_len = z.shape[0]
    nb = ts // HALO
    last = s_len // HALO - 1
    te = ts + HALO

    def body(z_ref, zprev_ref, znext_ref, dyc_ref, dynext_ref, ws_ref, bsp_ref, gain_ref, bias_ref, wp_ref, bp_ref,
             ps_ref, dz_ref, dws_ref, dbsp_ref, dgain_ref, dbias_ref, dwp_ref, dbp_ref, dps_ref, mixed_ref, dvn_ref):
        i = pl.program_id(0)

        @pl.when(i == 0)
        def _():
            for ref in (dws_ref, dbsp_ref, dgain_ref, dbias_ref, dwp_ref, dbp_ref, dps_ref):
                ref[...] = jnp.zeros_like(ref)

        wc = _tril_weights(ws_ref)
        gain = gain_ref[...]
        u, vhat, rstd, vn, dga = _mixer_forward_tile(z_ref[:, :2 * D_A], wc, bsp_ref, gain, bias_ref[...], mixed_ref)
        dya = dyc_ref[:, :D_A]
        du = dya * mixed_ref[...]
        dmixed = dya * u
        dmb = dmixed.astype(BF16)
        dm_sum = dmixed[0:CHUNK, :]
        for k in range(1, ts // CHUNK):
            dm_sum = dm_sum + dmixed[k * CHUNK:(k + 1) * CHUNK, :]
        r_idx = lax.broadcasted_iota(jnp.int32, (CHUNK, CHUNK), 0)
        s_idx = lax.broadcasted_iota(jnp.int32, (CHUNK, CHUNK), 1)
        causal = (s_idx <= r_idx).astype(F32)
        for h in range(N_HEADS):
            lanes = slice(h * HEAD_DIM, (h + 1) * HEAD_DIM)
            dbsp_ref[h] += jnp.sum(dm_sum[:, lanes], axis=1, keepdims=True)
            acc = None
            for k in range(ts // CHUNK):
                rows = slice(k * CHUNK, (k + 1) * CHUNK)
                t = _dot_nt(dmb[rows, lanes], vn[rows, lanes])
                acc = t if acc is None else acc + t
                dvn_ref[rows, lanes] = _dot_tn(wc[h], dmb[rows, lanes])
            dws_ref[h] += acc * causal
        dvn = dvn_ref[...]
        dgain_ref[...] += _colsum(dvn * vhat)
        dbias_ref[...] += _colsum(dvn)
        dvh = dvn * gain
        dv = rstd * (dvh - _rowmean(dvh) - vhat * _rowmean(dvh * vhat))
        dz_ref[:, :D_A] = (du * dga[:, :D_A]).astype(BF16)
        dz_ref[:, D_A:2 * D_A] = (dv * dga[:, D_A:]).astype(BF16)

        zb = z_ref[:, 2 * D_A:]
        prev = jnp.where(i == 0, 0.0, zprev_ref[...])
        zb_ext = jnp.concatenate([zb, znext_ref[...]], axis=0)
        sums = _causal_window_sums(jnp.concatenate([prev, zb_ext], axis=0))
        pos, counts = _window_counts(i * ts, te)
        dyb_ext = jnp.concatenate([dyc_ref[:, D_A:], dynext_ref[...]], axis=0)
        dlin_ext = dyb_ext * ps_ref[...]
        dbp_ref[...] += _colsum(dlin_ext[:ts, :])
        scaled = []
        ddiffs = []
        lins = []
        for g in range(len(POOL_WINDOWS)):
            lanes = slice(g * GROUP_DIM, (g + 1) * GROUP_DIM)
            diff = (sums[g][HALO:, :] / counts[g] - zb_ext[:, lanes]).astype(BF16)
            wpb = wp_ref[g].astype(BF16)
            dlb = dlin_ext[:, lanes].astype(BF16)
            lins.append(_dot(diff[:ts, :], wpb) + bp_ref[:, lanes])
            dwp_ref[g] += _dot_tn(diff[:ts, :], dlb[:ts, :])
            dd = _dot_nt(dlb, wpb)
            ddiffs.append(dd)
            scaled.append(jnp.where(pos < float(s_len), dd / counts[g], 0.0))
        dps_ref[...] += _colsum(dyb_ext[:ts, :] * jnp.concatenate(lins, axis=1))
        back = _anticausal_window_sums(jnp.concatenate(scaled, axis=1))
        for g in range(len(POOL_WINDOWS)):
            dz_ref[:, 2 * D_A + g * GROUP_DIM:2 * D_A + (g + 1) * GROUP_DIM] = (
                back[g][:ts, :] - ddiffs[g][:ts, :]).astype(BF16)

    sq = jax.ShapeDtypeStruct((N_HEADS, CHUNK, CHUNK), F32)
    vec = jax.ShapeDtypeStruct((1, D_A), F32)
    return pl.pallas_call(
        body, name="mixer_bwd", grid=(s_len // ts,),
        in_specs=[_rows(ts, D_Z),
                  pl.BlockSpec((HALO, D_B), lambda i: (jnp.maximum(i * nb - 1, 0), 2)),
                  pl.BlockSpec((HALO, D_B), lambda i: (jnp.minimum((i + 1) * nb, last), 2)),
                  _rows(ts, D_MODEL),
                  pl.BlockSpec((HALO, D_B), lambda i: (jnp.minimum((i + 1) * nb, last), 1)),
                  _const((N_HEADS, CHUNK, CHUNK)), _const((CHUNK, D_A)), _const((1, D_A)), _const((1, D_A)),
                  _const((N_HEADS, GROUP_DIM, GROUP_DIM)), _const((1, D_B)), _const((1, D_B))],
        out_specs=[_rows(ts, D_Z), _const((N_HEADS, CHUNK, CHUNK)), _const((N_HEADS, CHUNK, 1)), _const((1, D_A)),
                   _const((1, D_A)), _const((N_HEADS, GROUP_DIM, GROUP_DIM)), _const((1, D_B)), _const((1, D_B))],
        out_shape=[jax.ShapeDtypeStruct((s_len, D_Z), BF16), sq, jax.ShapeDtypeStruct((N_HEADS, CHUNK, 1), F32), vec,
                   vec, sq, vec, vec],
        scratch_shapes=[pltpu.VMEM((ts, D_A), F32), pltpu.VMEM((ts, D_A), F32)],
        compiler_params=_params(),
    )(z, z, z, dyc, dyc, w_spatial, bsp_full, gain, bias, w_pool, b_pool, pool_scale)


def _bwd_in(dz, dx1, x, mod6, n1pre, win_g, ts):
    s_len = x.shape[0]
    cs = D_Z // N_CHIPS

    def body(dz_ref, dx1_ref, x_ref, mod_ref, g_ref, w_ref, gx_ref, dshift_ref, da_ref):
        i = pl.program_id(0)

        @pl.when(i == 0)
        def _():
            dshift_ref[...] = jnp.zeros_like(dshift_ref)
            da_ref[...] = jnp.zeros_like(da_ref)

        dh = _dot_nt(dz_ref[:, 0:cs], w_ref[0])
        for j in range(1, N_CHIPS):
            dh = dh + _dot_nt(dz_ref[:, j * cs:(j + 1) * cs], w_ref[j])
        xv = x_ref[...]
        r = lax.rsqrt(_rowmean(xv * xv) + EPS)
        xh = xv * r
        a1 = g_ref[...] * (1.0 + mod_ref[1:2, :])
        dshift_ref[...] += _colsum(dh)
        da_ref[...] += _colsum(dh * xh)
        dxh = dh * a1
        gx_ref[...] = dx1_ref[...] + r * (dxh - xh * _rowmean(dxh * xh))

    vec = jax.ShapeDtypeStruct((1, D_MODEL), F32)
    return pl.pallas_call(
        body, name="bwd_in", grid=(s_len // ts,),
        in_specs=[_rows(ts, D_Z), _rows(ts, D_MODEL), _rows(ts, D_MODEL), _const((N_MOD, D_MODEL)),
                  _const((1, D_MODEL)), _VMEM],
        out_specs=[_rows(ts, D_MODEL), _const((1, D_MODEL)), _const((1, D_MODEL))],
        out_shape=[jax.ShapeDtypeStruct((s_len, D_MODEL), F32), vec, vec],
        compiler_params=_params(),
    )(dz, dx1, x, mod6, n1pre, win_g)


def _wgrad(a, b, name, col_sharded, tk):
    s_len, m = a.shape
    n = b.shape[1]
    if col_sharded:
        tm, tn = m, n // N_CHIPS
        a_spec = pl.BlockSpec((tk, tm), lambda j, k: (k, 0))
        b_spec = pl.BlockSpec((tk, tn), lambda j, k: (k, j))
    else:
        tm, tn = m // N_CHIPS, n
        a_spec = pl.BlockSpec((tk, tm), lambda j, k: (k, j))
        b_spec = pl.BlockSpec((tk, tn), lambda j, k: (k, 0))
    half = tm // 2
    nk = s_len // tk

    def body(a_ref, b_ref, o_ref, ob_ref, acc_ref):
        k = pl.program_id(1)

        @pl.when(k == 0)
        def _():
            acc_ref[...] = jnp.zeros_like(acc_ref)

        acc_ref[...] += _dot_tn(a_ref[...], b_ref[...])

        @pl.when(k == nk - 1)
        def _():
            for h in range(2):
                blk = acc_ref[h * half:(h + 1) * half, :]
                o_ref[0, h] = blk
                ob_ref[0, h] = blk.astype(BF16)

    o_spec = pl.BlockSpec((1, 2, half, tn), lambda j, k: (j, 0, 0, 0))
    return pl.pallas_call(
        body, name=name, grid=(N_CHIPS, nk),
        in_specs=[a_spec, b_spec], out_specs=[o_spec, o_spec],
        out_shape=[jax.ShapeDtypeStruct((N_CHIPS, 2, half, tn), F32), jax.ShapeDtypeStruct((N_CHIPS, 2, half, tn), BF16)],
        scratch_shapes=[pltpu.VMEM((tm, tn), F32)],
        compiler_params=_params(2),
    )(a, b)


def _adamw_math(w, g, m, v):
    m = ADAM_B1 * m + (1.0 - ADAM_B1) * g
    v = ADAM_B2 * v + (1.0 - ADAM_B2) * (g * g)
    m_hat = m / (1.0 - ADAM_B1 ** ADAM_STEP)
    v_hat = v / (1.0 - ADAM_B2 ** ADAM_STEP)
    delta = -ADAM_LR * (m_hat / (jnp.sqrt(v_hat) + ADAM_EPS) + ADAM_WD * w)
    return delta, m, v


def _adamw(g, w, m, v, name, tr):
    rows, cols = w.shape

    def body(g_ref, w_ref, m_ref, v_ref, d_ref, nm_ref, nv_ref):
        d, nm, nv = _adamw_math(w_ref[...], g_ref[...], m_ref[...], v_ref[...])
        d_ref[...] = d
        nm_ref[...] = nm
        nv_ref[...] = nv

    spec = _rows(tr, cols)
    shape = jax.ShapeDtypeStruct((rows, cols), F32)
    return pl.pallas_call(
        body, name=name, grid=(rows // tr,), in_specs=[spec] * 4, out_specs=[spec] * 3, out_shape=[shape] * 3,
        compiler_params=_params(),
    )(g, w, m, v)


def _ada_grad_adamw(sc_t, dmod_shard, w, m, v, tr):
    rows, cols = w.shape

    def body(s_ref, dm_ref, w_ref, m_ref, v_ref, g_ref, d_ref, nm_ref, nv_ref):
        g = s_ref[:, 0:1] * dm_ref[0:1, :]
        for b in range(1, N_DEV):
            g = g + s_ref[:, b:b + 1] * dm_ref[b:b + 1, :]
        g_ref[...] = g
        d, nm, nv = _adamw_math(w_ref[...], g, m_ref[...], v_ref[...])
        d_ref[...] = d
        nm_ref[...] = nm
        nv_ref[...] = nv

    spec = _rows(tr, cols)
    shape = jax.ShapeDtypeStruct((rows, cols), F32)
    return pl.pallas_call(
        body, name="ada_grad_adamw", grid=(rows // tr,),
        in_specs=[_rows(tr, N_DEV), _const((N_DEV, cols)), spec, spec, spec],
        out_specs=[spec] * 4, out_shape=[shape] * 4, compiler_params=_params(),
    )(sc_t, dmod_shard, w, m, v)


def _mod_grads(da1, dshift1, dgate1, dg1post, da2, dshift2, dgate2, dg2post, mod6, n1pre, n2pre):
    def body(da1_ref, ds1_ref, dgt1_ref, dg1_ref, da2_ref, ds2_ref, dgt2_ref, dg2_ref, mod_ref, n1_ref, n2_ref,
             dmod_ref, dn_ref):
        dmod_ref[0:1, :] = ds1_ref[...]
        dmod_ref[1:2, :] = da1_ref[...] * n1_ref[...]
        dmod_ref[2:3, :] = dgt1_ref[...]
        dmod_ref[3:4, :] = ds2_ref[...]
        dmod_ref[4:5, :] = da2_ref[...] * n2_ref[...]
        dmod_ref[5:6, :] = dgt2_ref[...]
        dn_ref[0:1, :] = da1_ref[...] * (1.0 + mod_ref[1:2, :])
        dn_ref[1:2, :] = dg1_ref[...]
        dn_ref[2:3, :] = da2_ref[...] * (1.0 + mod_ref[4:5, :])
        dn_ref[3:4, :] = dg2_ref[...]

    return pl.pallas_call(
        body, name="mod_grads",
        out_shape=[jax.ShapeDtypeStruct((N_MOD, D_MODEL), F32), jax.ShapeDtypeStruct((4, D_MODEL), F32)],
    )(da1, dshift1, dgate1, dg1post, da2, dshift2, dgate2, dg2post, mod6, n1pre, n2pre)


def _local_step(x, target, mod6, n1pre, n1post, n2pre, n2post, w_spatial, b_spatial, gain, bias, w_pool, b_pool,
                pool_scale, win_g, wout_g, fc1_g, fc2_g):
    s_len = x.shape[0]
    ts_big = min(512, s_len)
    ts_mid = min(256, s_len)
    bsp_full = jnp.repeat(b_spatial.T, HEAD_DIM, axis=1)

    z, h1 = _fwd_in(x, mod6, n1pre, win_g, ts_big)
    ycat = _mixer_fwd(z, w_spatial, bsp_full, gain, bias, w_pool, b_pool, pool_scale, ts_mid)
    mix, x1, h2 = _fwd_out(ycat, x, mod6, n1post, n2pre, wout_g, ts_big)
    q = _fwd_fc1(h2, fc1_g, ts_big)
    dy, df, loss, dgate2, dg2post = _fwd_fc2_loss(q, x1, target, mod6, n2post, fc2_g, ts_big)

    dp = _bwd_fc2(df, q, fc2_g, ts_big)
    dx1, dmix, dyc, dshift2, da2, dgate1, dg1post = _bwd_fc1_out(dp, dy, x1, mix, mod6, n2pre, n1post, fc1_g, wout_g,
                                                                 ts_mid)
    dz, dws, dbsp, dgain, dbias, dwp, dbp, dps = _mixer_bwd(z, dyc, w_spatial, bsp_full, gain, bias, w_pool,
                                                                b_pool, pool_scale, ts_mid)
    gx, dshift1, da1 = _bwd_in(dz, dx1, x, mod6, n1pre, win_g, ts_mid)
    dmod6, dnorms = _mod_grads(da1, dshift1, dgate1, dg1post, da2, dshift2, dgate2, dg2post, mod6, n1pre, n2pre)

    tk = min(1024, s_len)
    g_fc2 = _wgrad(q, df, "wgrad_fc2", False, tk)
    g_fc1 = _wgrad(h2, dp, "wgrad_fc1", True, tk)
    g_out = _wgrad(ycat, dmix, "wgrad_out", False, tk)
    g_in = _wgrad(h1, dz, "wgrad_in", True, tk)

    small = dict(
        dmod6=dmod6, dnorms=dnorms, w_spatial=dws, b_spatial=dbsp.reshape(N_HEADS, CHUNK),
        ln_v_gain=dgain, ln_v_bias=dbias, w_pool=dwp, b_pool=dbp, pool_scale=dps)
    return loss[0, 0], gx, small, (g_in, g_out, g_fc1, g_fc2)


def _position():
    x, y, c = lax.axis_index("x"), lax.axis_index("y"), lax.axis_index("c")
    return x, y, c


def _flip(v, bit):
    return 1 - v if bit else v


def _peer(x, y, c, k):
    return (_flip(x, k & 4), _flip(y, k & 2), _flip(c, k & 1))


def _remote(src, dst, send_sem, recv_sem, device):
    return pltpu.make_async_remote_copy(src_ref=src, dst_ref=dst, send_sem=send_sem, recv_sem=recv_sem,
                                        device_id=device, device_id_type=MESH)


def _cast_bf16(w, tr):
    rows, cols = w.shape

    def body(w_ref, o_ref):
        o_ref[...] = w_ref[...].astype(BF16)

    return pl.pallas_call(
        body, name="cast_bf16_%dx%d" % (rows, cols), grid=(rows // tr,), in_specs=[_rows(tr, cols)],
        out_specs=_rows(tr, cols), out_shape=jax.ShapeDtypeStruct((rows, cols), BF16), compiler_params=_params(),
    )(w)


def _mod_exchange(c_row, w_ada_shard, b_ada_row):
    cs = w_ada_shard.shape[1]

    def body(c_ref, w_ref, b_ref, mod_ref, sc_ref, rows_ref, send1, recv1, send2, recv2):
        x, y, c = _position()
        me = 4 * x + 2 * y + c
        chip = 2 * x + y
        cv = c_ref[...]
        sc_ref[me] = cv * jax.nn.sigmoid(cv)
        gather = [_remote(sc_ref.at[me], sc_ref.at[me], send1.at[k - 1], recv1.at[k - 1], _peer(x, y, c, k))
                  for k in range(1, N_DEV)]
        for cp in gather:
            cp.start()
        for k in range(1, N_DEV):
            px, py, pc = _peer(x, y, c, k)
            src = 4 * px + 2 * py + pc
            _remote(sc_ref.at[src], sc_ref.at[src], send1.at[k - 1], recv1.at[k - 1], (px, py, pc)).wait_recv()
        for cp in gather:
            cp.wait_send()
        sc_all = jnp.concatenate([sc_ref[b] for b in range(N_DEV)], axis=0)
        part = jnp.dot(sc_all, w_ref[...], precision=lax.Precision.HIGHEST, preferred_element_type=F32)
        part = part + b_ref[:, pl.ds(pl.multiple_of(chip * cs, 128), cs)]
        for b in range(N_DEV):
            rows_ref[b] = part[b:b + 1, :]
        mod_ref[chip] = rows_ref[me]
        hand = []
        for k in (2, 4, 6):
            px, py, _ = _peer(x, y, c, k)
            hand.append(_remote(rows_ref.at[4 * px + 2 * py + c], mod_ref.at[chip], send2.at[k // 2 - 1],
                                recv2.at[k // 2 - 1], (px, py, c)))
        for cp in hand:
            cp.start()
        for k in (2, 4, 6):
            px, py, _ = _peer(x, y, c, k)
            pchip = 2 * px + py
            _remote(rows_ref.at[me], mod_ref.at[pchip], send2.at[k // 2 - 1], recv2.at[k // 2 - 1],
                    (px, py, c)).wait_recv()
        for cp in hand:
            cp.wait_send()

    return pl.pallas_call(
        body, name="mod_exchange",
        in_specs=[_VMEM, _VMEM, _VMEM], out_specs=[_VMEM, _VMEM],
        out_shape=[jax.ShapeDtypeStruct((N_CHIPS, 1, cs), F32), jax.ShapeDtypeStruct((N_DEV, 1, D_MODEL), F32)],
        scratch_shapes=[pltpu.VMEM((N_DEV, 1, cs), F32), pltpu.SemaphoreType.DMA((N_DEV - 1,)),
                        pltpu.SemaphoreType.DMA((N_DEV - 1,)), pltpu.SemaphoreType.DMA((N_CHIPS - 1,)),
                        pltpu.SemaphoreType.DMA((N_CHIPS - 1,))],
        compiler_params=pltpu.CompilerParams(vmem_limit_bytes=VMEM_LIMIT),
    )(c_row, w_ada_shard, b_ada_row)


def _all_gather_weights(shards):
    n = len(shards)

    def body(*refs):
        ins, outs = refs[:n], refs[n:2 * n]
        local_sem, send1, recv1, send2, recv2 = refs[2 * n:]
        x, y, c = _position()
        chip = 2 * x + y
        local = [pltpu.make_async_copy(ins[i], outs[i].at[chip], local_sem.at[i]) for i in range(n)]
        for cp in local:
            cp.start()
        sends = []
        for i in range(n):
            half = ins[i].shape[0] // 2
            mine = pl.ds(c * half, half)
            for k in (2, 4, 6):
                s = i * 3 + k // 2 - 1
                px, py, _ = _peer(x, y, c, k)
                sends.append(_remote(ins[i].at[mine], outs[i].at[chip, mine], send1.at[s], recv1.at[s], (px, py, c)))
        for cp in sends:
            cp.start()
        passed = []
        for i in range(n):
            half = ins[i].shape[0] // 2
            mine = pl.ds(c * half, half)
            for k in (2, 4, 6):
                s = i * 3 + k // 2 - 1
                px, py, _ = _peer(x, y, c, k)
                landed = outs[i].at[2 * px + py, mine]
                _remote(landed, landed, send1.at[s], recv1.at[s], (px, py, c)).wait_recv()
                fwd = _remote(landed, landed, send2.at[s], recv2.at[s], (x, y, 1 - c))
                fwd.start()
                passed.append(fwd)
        for i in range(n):
            half = ins[i].shape[0] // 2
            theirs = pl.ds((1 - c) * half, half)
            for k in (2, 4, 6):
                s = i * 3 + k // 2 - 1
                px, py, _ = _peer(x, y, c, k)
                landed = outs[i].at[2 * px + py, theirs]
                _remote(landed, landed, send2.at[s], recv2.at[s], (x, y, 1 - c)).wait_recv()
        for cp in sends + passed:
            cp.wait_send()
        for cp in local:
            cp.wait()

    return pl.pallas_call(
        body, name="all_gather_weights",
        in_specs=[_ANY] * n, out_specs=[_ANY] * n,
        out_shape=[jax.ShapeDtypeStruct((N_CHIPS,) + s.shape, BF16) for s in shards],
        scratch_shapes=[pltpu.SemaphoreType.DMA((n,))] + [pltpu.SemaphoreType.DMA((3 * n,))] * 4,
    )(*shards)


def _all_reduce_small(v):
    rows = v.shape[0]

    def body(v_ref, out_ref, sib_ref, slots_ref, send_a, recv_a, send_b, recv_b):
        x, y, c = _position()
        chip = 2 * x + y
        pair = _remote(v_ref, sib_ref, send_a, recv_a, (x, y, 1 - c))
        pair.start()
        pair.wait_recv()
        slots_ref[chip] = v_ref[...] + sib_ref[...]
        spread = []
        for k in (2, 4, 6):
            px, py, _ = _peer(x, y, c, k)
            spread.append(_remote(slots_ref.at[chip], slots_ref.at[chip], send_b.at[k // 2 - 1], recv_b.at[k // 2 - 1],
                                  (px, py, c)))
        for cp in spread:
            cp.start()
        for k in (2, 4, 6):
            px, py, _ = _peer(x, y, c, k)
            landed = slots_ref.at[2 * px + py]
            _remote(landed, landed, send_b.at[k // 2 - 1], recv_b.at[k // 2 - 1], (px, py, c)).wait_recv()
        out_ref[...] = ((slots_ref[0] + slots_ref[1]) + slots_ref[2]) + slots_ref[3]
        pair.wait_send()
        for cp in spread:
            cp.wait_send()

    return pl.pallas_call(
        body, name="all_reduce_small",
        in_specs=[_VMEM], out_specs=_VMEM, out_shape=jax.ShapeDtypeStruct((rows, 128), F32),
        scratch_shapes=[pltpu.VMEM((rows, 128), F32), pltpu.VMEM((N_CHIPS, rows, 128), F32),
                        pltpu.SemaphoreType.DMA, pltpu.SemaphoreType.DMA,
                        pltpu.SemaphoreType.DMA((N_CHIPS - 1,)), pltpu.SemaphoreType.DMA((N_CHIPS - 1,))],
        compiler_params=pltpu.CompilerParams(vmem_limit_bytes=VMEM_LIMIT),
    )(v)


def _reduce_scatter(pg, pgb, name):
    _, _, r, cdim = pg.shape

    def body(pg_ref, pgb_ref, out_ref, own, from_sib, outbox, inbox, local_sem, send_a, recv_a, send_b, recv_b,
             send_c, recv_c):
        x, y, c = _position()
        chip = 2 * x + y
        sibling = (x, y, 1 - c)
        loads = [pltpu.make_async_copy(pg_ref.at[j, c], own.at[j], local_sem.at[j]) for j in range(N_CHIPS)]
        pairs = [_remote(pgb_ref.at[j, 1 - c], from_sib.at[j], send_a.at[j], recv_a.at[j], sibling)
                 for j in range(N_CHIPS)]
        for cp in loads + pairs:
            cp.start()
        inbox[chip] = jnp.zeros((r, cdim), BF16)
        for j in range(N_CHIPS):
            loads[j].wait()
            pairs[j].wait_recv()
            p = own[j] + from_sib[j].astype(F32)
            own[j] = p
            outbox[j] = p.astype(BF16)
        spread = []
        for k in (2, 4, 6):
            px, py, _ = _peer(x, y, c, k)
            spread.append(_remote(outbox.at[2 * px + py], inbox.at[chip], send_b.at[k // 2 - 1], recv_b.at[k // 2 - 1],
                                  (px, py, c)))
        for cp in spread:
            cp.start()
        for k in (2, 4, 6):
            px, py, _ = _peer(x, y, c, k)
            landed = inbox.at[2 * px + py]
            _remote(landed, landed, send_b.at[k // 2 - 1], recv_b.at[k // 2 - 1], (px, py, c)).wait_recv()
        total = ((inbox[0].astype(F32) + inbox[1].astype(F32)) + inbox[2].astype(F32)) + inbox[3].astype(F32)
        out_ref[c] = total + own[chip]
        share = _remote(out_ref.at[c], out_ref.at[c], send_c, recv_c, sibling)
        share.start()
        _remote(out_ref.at[1 - c], out_ref.at[1 - c], send_c, recv_c, sibling).wait_recv()
        for cp in pairs + spread + [share]:
            cp.wait_send()

    return pl.pallas_call(
        body, name=name,
        in_specs=[_ANY, _ANY], out_specs=_VMEM, out_shape=jax.ShapeDtypeStruct((2, r, cdim), F32),
        scratch_shapes=[pltpu.VMEM((N_CHIPS, r, cdim), F32), pltpu.VMEM((N_CHIPS, r, cdim), BF16),
                        pltpu.VMEM((N_CHIPS, r, cdim), BF16), pltpu.VMEM((N_CHIPS, r, cdim), BF16),
                        pltpu.SemaphoreType.DMA((N_CHIPS,)), pltpu.SemaphoreType.DMA((N_CHIPS,)),
                        pltpu.SemaphoreType.DMA((N_CHIPS,)), pltpu.SemaphoreType.DMA((N_CHIPS - 1,)),
                        pltpu.SemaphoreType.DMA((N_CHIPS - 1,)), pltpu.SemaphoreType.DMA, pltpu.SemaphoreType.DMA],
        compiler_params=pltpu.CompilerParams(vmem_limit_bytes=VMEM_LIMIT),
    )(pg, pgb)


_SMALL = (("b_ada", N_MOD * D_MODEL), ("norm1_pre", D_MODEL), ("norm1_post", D_MODEL), ("norm2_pre", D_MODEL),
          ("norm2_post", D_MODEL), ("w_spatial", N_HEADS * CHUNK * CHUNK), ("b_spatial", N_HEADS * CHUNK),
          ("ln_v_gain", D_A), ("ln_v_bias", D_A), ("w_pool", N_HEADS * GROUP_DIM * GROUP_DIM),
          ("b_pool", D_B), ("pool_scale", D_B))
_MOD_ROWS = N_MOD * D_MODEL // 128


def _packed_rows(size):
    return -(-(size // 128) // 8) * 8


def _pack(parts):
    out = []
    for name, size in _SMALL:
        a = parts[name].reshape(size // 128, 128)
        pad = _packed_rows(size) - a.shape[0]
        out.append(jnp.pad(a, ((0, pad), (0, 0))) if pad else a)
    return out


def _unpack(packed, shapes):
    res = {}
    row = 0
    for name, size in _SMALL:
        res[name] = packed[row:row + size // 128].reshape(shapes[name])
        row += _packed_rows(size)
    return res


def _small_adamw(total, wp, mp, vp):
    rows = wp.shape[0]
    head = N_DEV * _MOD_ROWS

    def body(t_ref, w_ref, m_ref, v_ref, g_ref, d_ref, nm_ref, nv_ref):
        gb = t_ref[0:_MOD_ROWS, :]
        for b in range(1, N_DEV):
            gb = gb + t_ref[b * _MOD_ROWS:(b + 1) * _MOD_ROWS, :]
        g_ref[0:_MOD_ROWS, :] = gb
        g_ref[_MOD_ROWS:, :] = t_ref[head:, :]
        d, nm, nv = _adamw_math(w_ref[...], g_ref[...], m_ref[...], v_ref[...])
        d_ref[...] = d
        nm_ref[...] = nm
        nv_ref[...] = nv

    return pl.pallas_call(
        body, name="small_adamw", out_shape=[jax.ShapeDtypeStruct((rows, 128), F32)] * 4,
        compiler_params=pltpu.CompilerParams(vmem_limit_bytes=VMEM_LIMIT),
    )(total, wp, mp, vp)


def kernel(x, c, w_ada, b_ada, norm1_pre, norm1_post, w_in, w_spatial, b_spatial, ln_v_gain, ln_v_bias, w_pool, b_pool, pool_scale, w_out, norm2_pre, norm2_post, w_fc1, w_fc2, loss_target, m_w_ada, m_b_ada, m_norm1_pre, m_norm1_post, m_w_in, m_w_spatial, m_b_spatial, m_ln_v_gain, m_ln_v_bias, m_w_pool, m_b_pool, m_pool_scale, m_w_out, m_norm2_pre, m_norm2_post, m_w_fc1, m_w_fc2, v_w_ada, v_b_ada, v_norm1_pre, v_norm1_post, v_w_in, v_w_spatial, v_b_spatial, v_ln_v_gain, v_ln_v_bias, v_w_pool, v_b_pool, v_pool_scale, v_w_out, v_norm2_pre, v_norm2_post, v_w_fc1, v_w_fc2):
    weights = dict(w_ada=w_ada, b_ada=b_ada, norm1_pre=norm1_pre, norm1_post=norm1_post, w_in=w_in,
                   w_spatial=w_spatial, b_spatial=b_spatial, ln_v_gain=ln_v_gain, ln_v_bias=ln_v_bias, w_pool=w_pool,
                   b_pool=b_pool, pool_scale=pool_scale, w_out=w_out, norm2_pre=norm2_pre, norm2_post=norm2_post,
                   w_fc1=w_fc1, w_fc2=w_fc2)
    m_old = dict(w_ada=m_w_ada, b_ada=m_b_ada, norm1_pre=m_norm1_pre, norm1_post=m_norm1_post, w_in=m_w_in,
                 w_spatial=m_w_spatial, b_spatial=m_b_spatial, ln_v_gain=m_ln_v_gain, ln_v_bias=m_ln_v_bias,
                 w_pool=m_w_pool, b_pool=m_b_pool, pool_scale=m_pool_scale, w_out=m_w_out, norm2_pre=m_norm2_pre,
                 norm2_post=m_norm2_post, w_fc1=m_w_fc1, w_fc2=m_w_fc2)
    v_old = dict(w_ada=v_w_ada, b_ada=v_b_ada, norm1_pre=v_norm1_pre, norm1_post=v_norm1_post, w_in=v_w_in,
                 w_spatial=v_w_spatial, b_spatial=v_b_spatial, ln_v_gain=v_ln_v_gain, ln_v_bias=v_ln_v_bias,
                 w_pool=v_w_pool, b_pool=v_b_pool, pool_scale=v_pool_scale, w_out=v_w_out, norm2_pre=v_norm2_pre,
                 norm2_post=v_norm2_post, w_fc1=v_w_fc1, w_fc2=v_w_fc2)
    order = ("w_ada", "b_ada", "norm1_pre", "norm1_post", "w_in", "w_spatial", "b_spatial", "ln_v_gain", "ln_v_bias",
             "w_pool", "b_pool", "pool_scale", "w_out", "norm2_pre", "norm2_post", "w_fc1", "w_fc2")
    big = ("w_in", "w_out", "w_fc1", "w_fc2")
    mx, my, mc = _position()
    me = 4 * mx + 2 * my + mc
    chip = 2 * mx + my
    row = lambda a: a.reshape(1, -1)

    mod4, sc_all = _mod_exchange(c, w_ada, row(b_ada))
    mod6 = mod4.reshape(N_MOD, D_MODEL)
    gathered = _all_gather_weights([_cast_bf16(weights[n], 256) for n in big])

    loss, grad_x, small, partial = _local_step(
        x[0], loss_target[0], mod6, row(norm1_pre), row(norm1_post), row(norm2_pre), row(norm2_post), w_spatial,
        b_spatial, row(ln_v_gain), row(ln_v_bias), w_pool, row(b_pool), row(pool_scale), *gathered)
    loss = lax.psum(loss, ("x", "y", "c"))

    grads, deltas, new_m, new_v = {}, {}, {}, {}
    for n, (pg, pgb) in zip(big, partial):
        g = _reduce_scatter(pg, pgb, "reduce_scatter_" + n).reshape(weights[n].shape)
        grads[n] = g
        deltas[n], new_m[n], new_v[n] = _adamw(g, weights[n], m_old[n], v_old[n], "adamw_" + n, 256)

    dn = small["dnorms"]
    parts = dict(small, norm1_pre=dn[0], norm1_post=dn[1], norm2_pre=dn[2], norm2_post=dn[3], b_ada=small["dmod6"])
    pieces = _pack(parts)
    slots = lax.dynamic_update_slice(jnp.zeros((N_DEV * _MOD_ROWS, 128), F32), pieces[0], (me * _MOD_ROWS, 0))
    total = _all_reduce_small(jnp.concatenate([slots] + pieces[1:], axis=0))
    gp, dp, nmp, nvp = _small_adamw(total, jnp.concatenate(_pack(weights), axis=0),
                                    jnp.concatenate(_pack(m_old), axis=0), jnp.concatenate(_pack(v_old), axis=0))
    shapes = {n: weights[n].shape for n, _ in _SMALL}
    for res, packed in ((grads, gp), (deltas, dp), (new_m, nmp), (new_v, nvp)):
        res.update(_unpack(packed, shapes))

    dmod_all = total[:N_DEV * _MOD_ROWS].reshape(N_DEV, N_MOD * D_MODEL)
    cs = w_ada.shape[1]
    dmod_shard = lax.dynamic_slice(dmod_all, (0, chip * cs), (N_DEV, cs))
    sc_t = sc_all.reshape(N_DEV, D_MODEL).T
    grads["w_ada"], deltas["w_ada"], new_m["w_ada"], new_v["w_ada"] = _ada_grad_adamw(
        sc_t, dmod_shard, w_ada, m_w_ada, v_w_ada, 256)

    return (loss, grad_x[None], *[grads[n] for n in order], *[deltas[n] for n in order],
            *[new_m[n] for n in order], *[new_v[n] for n in order])
```

```python
import functools

import jax
import jax.numpy as jnp
from jax import lax
from jax.experimental import pallas as pl
from jax.experimental.pallas import tpu as pltpu

F32 = jnp.float32
BF16 = jnp.bfloat16
MESH = pl.DeviceIdType.MESH

D_MODEL = 1024
D_A = 512
D_B = 512
D_Z = 2 * D_A + D_B
N_HEADS = 4
HEAD_DIM = 128
CHUNK = 128
POOL_WINDOWS = (2, 4, 8, 16)
GROUP_DIM = 128
D_FF = 4096
N_MOD = 6
EPS = 1e-6
HALO = 16
N_CHIPS = 4
N_DEV = 8

ADAM_LR = 0.001
ADAM_B1 = 0.9
ADAM_B2 = 0.999
ADAM_EPS = 1e-08
ADAM_WD = 0.01
ADAM_STEP = 10

VMEM_LIMIT = 56 * 1024 * 1024

_VMEM = pl.BlockSpec(memory_space=pltpu.VMEM)
_ANY = pl.BlockSpec(memory_space=pl.ANY)


def _params(n_grid_axes=1):
    return pltpu.CompilerParams(dimension_semantics=("arbitrary",) * n_grid_axes, vmem_limit_bytes=VMEM_LIMIT)


def _rows(ts, width):
    return pl.BlockSpec((ts, width), lambda i: (i, 0))


def _const(shape):
    return pl.BlockSpec(shape, lambda i: (0,) * len(shape))


def _dot(a, b):
    return jnp.dot(a, b, preferred_element_type=F32)


def _dot_nt(a, b):
    return lax.dot_general(a, b, (((1,), (1,)), ((), ())), preferred_element_type=F32)


def _dot_tn(a, b):
    return lax.dot_general(a, b, (((0,), (0,)), ((), ())), preferred_element_type=F32)


def _rowmean(v):
    return jnp.mean(v, axis=-1, keepdims=True)


def _colsum(v):
    return jnp.sum(v, axis=0, keepdims=True)


def _gelu_parts(z):
    k0 = 0.7978845608028654
    k1 = 0.044715
    z2 = z * z
    t = jnp.tanh(k0 * (z + k1 * z * z2))
    g = 0.5 * z * (1.0 + t)
    dg = 0.5 * (1.0 + t) + 0.5 * z * (1.0 - t * t) * (k0 * (1.0 + 3.0 * k1 * z2))
    return g, dg


def _tril_weights(ws_ref):
    r = lax.broadcasted_iota(jnp.int32, (CHUNK, CHUNK), 0)
    s = lax.broadcasted_iota(jnp.int32, (CHUNK, CHUNK), 1)
    mask = (s <= r).astype(F32)
    return [(ws_ref[h] * mask).astype(BF16) for h in range(N_HEADS)]


def _window_counts(first_row, n_rows):
    pos = (first_row + lax.broadcasted_iota(jnp.int32, (n_rows, 1), 0)).astype(F32)
    return pos, [jnp.minimum(pos + 1.0, float(w)) for w in POOL_WINDOWS]


def _causal_window_sums(ext):
    out = []
    e = ext
    shift = 1
    for g in range(len(POOL_WINDOWS)):
        e = e + pltpu.roll(e, shift, 0)
        shift *= 2
        out.append(e[:, g * GROUP_DIM:(g + 1) * GROUP_DIM])
    return out


def _anticausal_window_sums(ext):
    n = ext.shape[0]
    out = []
    e = ext
    shift = 1
    for g in range(len(POOL_WINDOWS)):
        e = e + pltpu.roll(e, n - shift, 0)
        shift *= 2
        out.append(e[:, g * GROUP_DIM:(g + 1) * GROUP_DIM])
    return out


def _fwd_in(x, mod6, n1pre, win_g, ts):
    s_len = x.shape[0]
    cs = D_Z // N_CHIPS

    def body(x_ref, mod_ref, g_ref, w_ref, z_ref, h_ref):
        xv = x_ref[...]
        r = lax.rsqrt(_rowmean(xv * xv) + EPS)
        h = ((xv * r) * g_ref[...]) * (1.0 + mod_ref[1:2, :]) + mod_ref[0:1, :]
        hb = h.astype(BF16)
        h_ref[...] = hb
        for j in range(N_CHIPS):
            z_ref[:, j * cs:(j + 1) * cs] = _dot(hb, w_ref[j])

    return pl.pallas_call(
        body, name="fwd_in", grid=(s_len // ts,),
        in_specs=[_rows(ts, D_MODEL), _const((N_MOD, D_MODEL)), _const((1, D_MODEL)), _VMEM],
        out_specs=[_rows(ts, D_Z), _rows(ts, D_MODEL)],
        out_shape=[jax.ShapeDtypeStruct((s_len, D_Z), F32), jax.ShapeDtypeStruct((s_len, D_MODEL), BF16)],
        compiler_params=_params(),
    )(x, mod6, n1pre, win_g)


def _mixer_forward_tile(za, wc, bsp_ref, gain, bias, mixed_ref):
    ga, dga = _gelu_parts(za)
    u = ga[:, :D_A]
    v = ga[:, D_A:]
    mu = _rowmean(v)
    vc = v - mu
    rstd = lax.rsqrt(_rowmean(vc * vc) + EPS)
    vhat = vc * rstd
    vn = (vhat * gain + bias).astype(BF16)
    ts = za.shape[0]
    for k in range(ts // CHUNK):
        for h in range(N_HEADS):
            blk = vn[k * CHUNK:(k + 1) * CHUNK, h * HEAD_DIM:(h + 1) * HEAD_DIM]
            mixed_ref[k * CHUNK:(k + 1) * CHUNK, h * HEAD_DIM:(h + 1) * HEAD_DIM] = (
                _dot(wc[h], blk) + bsp_ref[:, h * HEAD_DIM:(h + 1) * HEAD_DIM])
    return u, vhat, rstd, vn, dga


def _mixer_fwd(z, w_spatial, bsp_full, gain, bias, w_pool, b_pool, pool_scale, ts):
    s_len = z.shape[0]
    nb = ts // HALO

    def body(z_ref, zprev_ref, ws_ref, bsp_ref, gain_ref, bias_ref, wp_ref, bp_ref, ps_ref, y_ref, mixed_ref):
        i = pl.program_id(0)
        wc = _tril_weights(ws_ref)
        u, _, _, _, _ = _mixer_forward_tile(z_ref[:, :2 * D_A], wc, bsp_ref, gain_ref[...], bias_ref[...], mixed_ref)
        y_ref[:, :D_A] = (u * mixed_ref[...]).astype(BF16)

        zb = z_ref[:, 2 * D_A:]
        prev = jnp.where(i == 0, 0.0, zprev_ref[...])
        ext = jnp.concatenate([prev, zb], axis=0)
        sums = _causal_window_sums(ext)
        _, counts = _window_counts(i * ts, ts)
        for g in range(len(POOL_WINDOWS)):
            lanes = slice(g * GROUP_DIM, (g + 1) * GROUP_DIM)
            diff = sums[g][HALO:, :] / counts[g] - zb[:, lanes]
            lin = _dot(diff.astype(BF16), wp_ref[g].astype(BF16)) + bp_ref[:, lanes]
            y_ref[:, D_A + g * GROUP_DIM:D_A + (g + 1) * GROUP_DIM] = (lin * ps_ref[:, lanes]).astype(BF16)

    return pl.pallas_call(
        body, name="mixer_fwd", grid=(s_len // ts,),
        in_specs=[_rows(ts, D_Z),
                  pl.BlockSpec((HALO, D_B), lambda i: (jnp.maximum(i * nb - 1, 0), 2)),
                  _const((N_HEADS, CHUNK, CHUNK)), _const((CHUNK, D_A)), _const((1, D_A)), _const((1, D_A)),
                  _const((N_HEADS, GROUP_DIM, GROUP_DIM)), _const((1, D_B)), _const((1, D_B))],
        out_specs=_rows(ts, D_MODEL),
        out_shape=jax.ShapeDtypeStruct((s_len, D_MODEL), BF16),
        scratch_shapes=[pltpu.VMEM((ts, D_A), F32)],
        compiler_params=_params(),
    )(z, z, w_spatial, bsp_full, gain, bias, w_pool, b_pool, pool_scale)


def _fwd_out(ycat, x, mod6, n1post, n2pre, wout_g, ts):
    s_len = x.shape[0]
    rs = D_MODEL // N_CHIPS

    def body(y_ref, x_ref, mod_ref, g1_ref, g2_ref, w_ref, mix_ref, x1_ref, h2_ref):
        yb = y_ref[...]
        mix = _dot(yb[:, 0:rs], w_ref[0])
        for j in range(1, N_CHIPS):
            mix = mix + _dot(yb[:, j * rs:(j + 1) * rs], w_ref[j])
        mix_ref[...] = mix
        r2 = lax.rsqrt(_rowmean(mix * mix) + EPS)
        x1 = x_ref[...] + mod_ref[2:3, :] * ((mix * r2) * g1_ref[...])
        x1_ref[...] = x1
        r3 = lax.rsqrt(_rowmean(x1 * x1) + EPS)
        h2 = ((x1 * r3) * g2_ref[...]) * (1.0 + mod_ref[4:5, :]) + mod_ref[3:4, :]
        h2_ref[...] = h2.astype(BF16)

    return pl.pallas_call(
        body, name="fwd_out", grid=(s_len // ts,),
        in_specs=[_rows(ts, D_MODEL), _rows(ts, D_MODEL), _const((N_MOD, D_MODEL)), _const((1, D_MODEL)),
                  _const((1, D_MODEL)), _VMEM],
        out_specs=[_rows(ts, D_MODEL), _rows(ts, D_MODEL), _rows(ts, D_MODEL)],
        out_shape=[jax.ShapeDtypeStruct((s_len, D_MODEL), F32), jax.ShapeDtypeStruct((s_len, D_MODEL), F32),
                   jax.ShapeDtypeStruct((s_len, D_MODEL), BF16)],
        compiler_params=_params(),
    )(ycat, x, mod6, n1post, n2pre, wout_g)


def _fwd_fc1(h2, fc1_g, ts):
    s_len = h2.shape[0]
    cs = D_FF // N_CHIPS

    def body(h_ref, w_ref, q_ref):
        hb = h_ref[...]
        for j in range(N_CHIPS):
            p = jnp.maximum(_dot(hb, w_ref[j]), 0.0)
            q_ref[:, j * cs:(j + 1) * cs] = (p * p).astype(BF16)

    return pl.pallas_call(
        body, name="fwd_fc1", grid=(s_len // ts,),
        in_specs=[_rows(ts, D_MODEL), _VMEM],
        out_specs=_rows(ts, D_FF),
        out_shape=jax.ShapeDtypeStruct((s_len, D_FF), BF16),
        compiler_params=_params(),
    )(h2, fc1_g)


def _fwd_fc2_loss(q, x1, target, mod6, n2post, fc2_g, ts):
    s_len = q.shape[0]
    rs = D_FF // N_CHIPS

    def body(q_ref, x1_ref, t_ref, mod_ref, g_ref, w_ref, dy_ref, df_ref, loss_ref, dgate_ref, dg_ref):
        i = pl.program_id(0)

        @pl.when(i == 0)
        def _():
            loss_ref[...] = jnp.zeros_like(loss_ref)
            dgate_ref[...] = jnp.zeros_like(dgate_ref)
            dg_ref[...] = jnp.zeros_like(dg_ref)

        f = _dot(q_ref[:, 0:rs], w_ref[0])
        for j in range(1, N_CHIPS):
            f = f + _dot(q_ref[:, j * rs:(j + 1) * rs], w_ref[j])
        r4 = lax.rsqrt(_rowmean(f * f) + EPS)
        fh = f * r4
        gate = mod_ref[5:6, :]
        gn = g_ref[...]
        err = (x1_ref[...] + gate * (fh * gn)) - t_ref[...]
        loss_ref[...] += 0.5 * jnp.sum(_rowmean(err * err), axis=0, keepdims=True)
        dy = err * (1.0 / D_MODEL)
        dy_ref[...] = dy
        dgate_ref[...] += _colsum(dy * (fh * gn))
        dg_ref[...] += _colsum((dy * gate) * fh)
        gh = (dy * gate) * gn
        df_ref[...] = (r4 * (gh - fh * _rowmean(gh * fh))).astype(BF16)

    return pl.pallas_call(
        body, name="fwd_fc2_loss", grid=(s_len // ts,),
        in_specs=[_rows(ts, D_FF), _rows(ts, D_MODEL), _rows(ts, D_MODEL), _const((N_MOD, D_MODEL)),
                  _const((1, D_MODEL)), _VMEM],
        out_specs=[_rows(ts, D_MODEL), _rows(ts, D_MODEL), _const((1, 1)), _const((1, D_MODEL)), _const((1, D_MODEL))],
        out_shape=[jax.ShapeDtypeStruct((s_len, D_MODEL), F32), jax.ShapeDtypeStruct((s_len, D_MODEL), BF16),
                   jax.ShapeDtypeStruct((1, 1), F32), jax.ShapeDtypeStruct((1, D_MODEL), F32),
                   jax.ShapeDtypeStruct((1, D_MODEL), F32)],
        compiler_params=_params(),
    )(q, x1, target, mod6, n2post, fc2_g)


def _bwd_fc2(df, q, fc2_g, ts):
    s_len = df.shape[0]
    cs = D_FF // N_CHIPS

    def body(df_ref, q_ref, w_ref, dp_ref):
        dfb = df_ref[...]
        for j in range(N_CHIPS):
            dq = _dot_nt(dfb, w_ref[j])
            p = jnp.sqrt(q_ref[:, j * cs:(j + 1) * cs].astype(F32))
            dp_ref[:, j * cs:(j + 1) * cs] = (dq * (2.0 * p)).astype(BF16)

    return pl.pallas_call(
        body, name="bwd_fc2", grid=(s_len // ts,),
        in_specs=[_rows(ts, D_MODEL), _rows(ts, D_FF), _VMEM],
        out_specs=_rows(ts, D_FF),
        out_shape=jax.ShapeDtypeStruct((s_len, D_FF), BF16),
        compiler_params=_params(),
    )(df, q, fc2_g)


def _bwd_fc1_out(dp, dy, x1, mix, mod6, n2pre, n1post, fc1_g, wout_g, ts):
    s_len = dp.shape[0]
    cs = D_FF // N_CHIPS
    rs = D_MODEL // N_CHIPS

    def body(dp_ref, dy_ref, x1_ref, mix_ref, mod_ref, g2_ref, g1_ref, w1_ref, wo_ref,
             dx1_ref, dmix_ref, dyc_ref, dshift2_ref, da2_ref, dgate1_ref, dg1_ref):
        i = pl.program_id(0)

        @pl.when(i == 0)
        def _():
            for ref in (dshift2_ref, da2_ref, dgate1_ref, dg1_ref):
                ref[...] = jnp.zeros_like(ref)

        dh2 = _dot_nt(dp_ref[:, 0:cs], w1_ref[0])
        for j in range(1, N_CHIPS):
            dh2 = dh2 + _dot_nt(dp_ref[:, j * cs:(j + 1) * cs], w1_ref[j])
        x1 = x1_ref[...]
        r3 = lax.rsqrt(_rowmean(x1 * x1) + EPS)
        xh = x1 * r3
        a2 = g2_ref[...] * (1.0 + mod_ref[4:5, :])
        dshift2_ref[...] += _colsum(dh2)
        da2_ref[...] += _colsum(dh2 * xh)
        dxh = dh2 * a2
        dx1 = dy_ref[...] + r3 * (dxh - xh * _rowmean(dxh * xh))
        dx1_ref[...] = dx1

        mix = mix_ref[...]
        r2 = lax.rsqrt(_rowmean(mix * mix) + EPS)
        mh = mix * r2
        gate = mod_ref[2:3, :]
        gn = g1_ref[...]
        dgate1_ref[...] += _colsum(dx1 * (mh * gn))
        dg1_ref[...] += _colsum((dx1 * gate) * mh)
        gh = (dx1 * gate) * gn
        dmix = (r2 * (gh - mh * _rowmean(gh * mh))).astype(BF16)
        dmix_ref[...] = dmix
        for j in range(N_CHIPS):
            dyc_ref[:, j * rs:(j + 1) * rs] = _dot_nt(dmix, wo_ref[j])

    vec = jax.ShapeDtypeStruct((1, D_MODEL), F32)
    return pl.pallas_call(
        body, name="bwd_fc1_out", grid=(s_len // ts,),
        in_specs=[_rows(ts, D_FF), _rows(ts, D_MODEL), _rows(ts, D_MODEL), _rows(ts, D_MODEL),
                  _const((N_MOD, D_MODEL)), _const((1, D_MODEL)), _const((1, D_MODEL)), _VMEM, _VMEM],
        out_specs=[_rows(ts, D_MODEL), _rows(ts, D_MODEL), _rows(ts, D_MODEL)] + [_const((1, D_MODEL))] * 4,
        out_shape=[jax.ShapeDtypeStruct((s_len, D_MODEL), F32), jax.ShapeDtypeStruct((s_len, D_MODEL), BF16),
                   jax.ShapeDtypeStruct((s_len, D_MODEL), F32), vec, vec, vec, vec],
        compiler_params=_params(),
    )(dp, dy, x1, mix, mod6, n2pre, n1post, fc1_g, wout_g)


def _mixer_bwd(z, dyc, w_spatial, bsp_full, gain, bias, w_pool, b_pool, pool_scale, ts):
    s_len = z.shape[0]
    nb = ts // HALO
    last = s_len // HALO - 1
    te = ts + HALO

    def body(z_ref, zprev_ref, znext_ref, dyc_ref, dynext_ref, ws_ref, bsp_ref, gain_ref, bias_ref, wp_ref, bp_ref,
             ps_ref, dz_ref, dws_ref, dbsp_ref, dgain_ref, dbias_ref, dwp_ref, dbp_ref, dps_ref, mixed_ref, dvn_ref):
        i = pl.program_id(0)

        @pl.when(i == 0)
        def _():
            for ref in (dws_ref, dbsp_ref, dgain_ref, dbias_ref, dwp_ref, dbp_ref, dps_ref):
                ref[...] = jnp.zeros_like(ref)

        wc = _tril_weights(ws_ref)
        gain = gain_ref[...]
        u, vhat, rstd, vn, dga = _mixer_forward_tile(z_ref[:, :2 * D_A], wc, bsp_ref, gain, bias_ref[...], mixed_ref)
        dya = dyc_ref[:, :D_A]
        du = dya * mixed_ref[...]
        dmixed = dya * u
        dmb = dmixed.astype(BF16)
        dm_sum = dmixed[0:CHUNK, :]
        for k in range(1, ts // CHUNK):
            dm_sum = dm_sum + dmixed[k * CHUNK:(k + 1) * CHUNK, :]
        r_idx = lax.broadcasted_iota(jnp.int32, (CHUNK, CHUNK), 0)
        s_idx = lax.broadcasted_iota(jnp.int32, (CHUNK, CHUNK), 1)
        causal = (s_idx <= r_idx).astype(F32)
        for h in range(N_HEADS):
            lanes = slice(h * HEAD_DIM, (h + 1) * HEAD_DIM)
            dbsp_ref[h] += jnp.sum(dm_sum[:, lanes], axis=1, keepdims=True)
            acc = None
            for k in range(ts // CHUNK):
                rows = slice(k * CHUNK, (k + 1) * CHUNK)
                t = _dot_nt(dmb[rows, lanes], vn[rows, lanes])
                acc = t if acc is None else acc + t
                dvn_ref[rows, lanes] = _dot_tn(wc[h], dmb[rows, lanes])
            dws_ref[h] += acc * causal
        dvn = dvn_ref[...]
        dgain_ref[...] += _colsum(dvn * vhat)
        dbias_ref[...] += _colsum(dvn)
        dvh = dvn * gain
        dv = rstd * (dvh - _rowmean(dvh) - vhat * _rowmean(dvh * vhat))
        dz_ref[:, :D_A] = (du * dga[:, :D_A]).astype(BF16)
        dz_ref[:, D_A:2 * D_A] = (dv * dga[:, D_A:]).astype(BF16)

        zb = z_ref[:, 2 * D_A:]
        prev = jnp.where(i == 0, 0.0, zprev_ref[...])
        zb_ext = jnp.concatenate([zb, znext_ref[...]], axis=0)
        sums = _causal_window_sums(jnp.concatenate([prev, zb_ext], axis=0))
        pos, counts = _window_counts(i * ts, te)
        dyb_ext = jnp.concatenate([dyc_ref[:, D_A:], dynext_ref[...]], axis=0)
        dlin_ext = dyb_ext * ps_ref[...]
        dbp_ref[...] += _colsum(dlin_ext[:ts, :])
        scaled = []
        ddiffs = []
        lins = []
        for g in range(len(POOL_WINDOWS)):
            lanes = slice(g * GROUP_DIM, (g + 1) * GROUP_DIM)
            diff = (sums[g][HALO:, :] / counts[g] - zb_ext[:, lanes]).astype(BF16)
            wpb = wp_ref[g].astype(BF16)
            dlb = dlin_ext[:, lanes].astype(BF16)
            lins.append(_dot(diff[:ts, :], wpb) + bp_ref[:, lanes])
            dwp_ref[g] += _dot_tn(diff[:ts, :], dlb[:ts, :])
            dd = _dot_nt(dlb, wpb)
            ddiffs.append(dd)
            scaled.append(jnp.where(pos < float(s_len), dd / counts[g], 0.0))
        dps_ref[...] += _colsum(dyb_ext[:ts, :] * jnp.concatenate(lins, axis=1))
        back = _anticausal_window_sums(jnp.concatenate(scaled, axis=1))
        for g in range(len(POOL_WINDOWS)):
            dz_ref[:, 2 * D_A + g * GROUP_DIM:2 * D_A + (g + 1) * GROUP_DIM] = (
                back[g][:ts, :] - ddiffs[g][:ts, :]).astype(BF16)

    sq = jax.ShapeDtypeStruct((N_HEADS, CHUNK, CHUNK), F32)
    vec = jax.ShapeDtypeStruct((1, D_A), F32)
    return pl.pallas_call(
        body, name="mixer_bwd", grid=(s_len // ts,),
        in_specs=[_rows(ts, D_Z),
                  pl.BlockSpec((HALO, D_B), lambda i: (jnp.maximum(i * nb - 1, 0), 2)),
                  pl.BlockSpec((HALO, D_B), lambda i: (jnp.minimum((i + 1) * nb, last), 2)),
                  _rows(ts, D_MODEL),
                  pl.BlockSpec((HALO, D_B), lambda i: (jnp.minimum((i + 1) * nb, last), 1)),
                  _const((N_HEADS, CHUNK, CHUNK)), _const((CHUNK, D_A)), _const((1, D_A)), _const((1, D_A)),
                  _const((N_HEADS, GROUP_DIM, GROUP_DIM)), _const((1, D_B)), _const((1, D_B))],
        out_specs=[_rows(ts, D_Z), _const((N_HEADS, CHUNK, CHUNK)), _const((N_HEADS, CHUNK, 1)), _const((1, D_A)),
                   _const((1, D_A)), _const((N_HEADS, GROUP_DIM, GROUP_DIM)), _const((1, D_B)), _const((1, D_B))],
        out_shape=[jax.ShapeDtypeStruct((s_len, D_Z), BF16), sq, jax.ShapeDtypeStruct((N_HEADS, CHUNK, 1), F32), vec,
                   vec, sq, vec, vec],
        scratch_shapes=[pltpu.VMEM((ts, D_A), F32), pltpu.VMEM((ts, D_A), F32)],
        compiler_params=_params(),
    )(z, z, z, dyc, dyc, w_spatial, bsp_full, gain, bias, w_pool, b_pool, pool_scale)


def _bwd_in(dz, dx1, x, mod6, n1pre, win_g, ts):
    s_len = x.shape[0]
    cs = D_Z // N_CHIPS

    def body(dz_ref, dx1_ref, x_ref, mod_ref, g_ref, w_ref, gx_ref, dshift_ref, da_ref):
        i = pl.program_id(0)

        @pl.when(i == 0)
        def _():
            dshift_ref[...] = jnp.zeros_like(dshift_ref)
            da_ref[...] = jnp.zeros_like(da_ref)

        dh = _dot_nt(dz_ref[:, 0:cs], w_ref[0])
        for j in range(1, N_CHIPS):
            dh = dh + _dot_nt(dz_ref[:, j * cs:(j + 1) * cs], w_ref[j])
        xv = x_ref[...]
        r = lax.rsqrt(_rowmean(xv * xv) + EPS)
        xh = xv * r
        a1 = g_ref[...] * (1.0 + mod_ref[1:2, :])
        dshift_ref[...] += _colsum(dh)
        da_ref[...] += _colsum(dh * xh)
        dxh = dh * a1
        gx_ref[...] = dx1_ref[...] + r * (dxh - xh * _rowmean(dxh * xh))

    vec = jax.ShapeDtypeStruct((1, D_MODEL), F32)
    return pl.pallas_call(
        body, name="bwd_in", grid=(s_len // ts,),
        in_specs=[_rows(ts, D_Z), _rows(ts, D_MODEL), _rows(ts, D_MODEL), _const((N_MOD, D_MODEL)),
                  _const((1, D_MODEL)), _VMEM],
        out_specs=[_rows(ts, D_MODEL), _const((1, D_MODEL)), _const((1, D_MODEL))],
        out_shape=[jax.ShapeDtypeStruct((s_len, D_MODEL), F32), vec, vec],
        compiler_params=_params(),
    )(dz, dx1, x, mod6, n1pre, win_g)


def _wgrad(a, b, name, col_sharded, tk):
    s_len, m = a.shape
    n = b.shape[1]
    if col_sharded:
        tm, tn = m, n // N_CHIPS
        a_spec = pl.BlockSpec((tk, tm), lambda j, k: (k, 0))
        b_spec = pl.BlockSpec((tk, tn), lambda j, k: (k, j))
    else:
        tm, tn = m // N_CHIPS, n
        a_spec = pl.BlockSpec((tk, tm), lambda j, k: (k, j))
        b_spec = pl.BlockSpec((tk, tn), lambda j, k: (k, 0))
    half = tm // 2
    nk = s_len // tk

    def body(a_ref, b_ref, o_ref, ob_ref, acc_ref):
        k = pl.program_id(1)

        @pl.when(k == 0)
        def _():
            acc_ref[...] = jnp.zeros_like(acc_ref)

        acc_ref[...] += _dot_tn(a_ref[...], b_ref[...])

        @pl.when(k == nk - 1)
        def _():
            for h in range(2):
                blk = acc_ref[h * half:(h + 1) * half, :]
                o_ref[0, h] = blk
                ob_ref[0, h] = blk.astype(BF16)

    o_spec = pl.BlockSpec((1, 2, half, tn), lambda j, k: (j, 0, 0, 0))
    return pl.pallas_call(
        body, name=name, grid=(N_CHIPS, nk),
        in_specs=[a_spec, b_spec], out_specs=[o_spec, o_spec],
        out_shape=[jax.ShapeDtypeStruct((N_CHIPS, 2, half, tn), F32), jax.ShapeDtypeStruct((N_CHIPS, 2, half, tn), BF16)],
        scratch_shapes=[pltpu.VMEM((tm, tn), F32)],
        compiler_params=_params(2),
    )(a, b)


def _adamw_math(w, g, m, v):
    m = ADAM_B1 * m + (1.0 - ADAM_B1) * g
    v = ADAM_B2 * v + (1.0 - ADAM_B2) * (g * g)
    m_hat = m / (1.0 - ADAM_B1 ** ADAM_STEP)
    v_hat = v / (1.0 - ADAM_B2 ** ADAM_STEP)
    delta = -ADAM_LR * (m_hat / (jnp.sqrt(v_hat) + ADAM_EPS) + ADAM_WD * w)
    return delta, m, v


def _adamw(g, w, m, v, name, tr):
    rows, cols = w.shape

    def body(g_ref, w_ref, m_ref, v_ref, d_ref, nm_ref, nv_ref):
        d, nm, nv = _adamw_math(w_ref[...], g_ref[...], m_ref[...], v_ref[...])
        d_ref[...] = d
        nm_ref[...] = nm
        nv_ref[...] = nv

    spec = _rows(tr, cols)
    shape = jax.ShapeDtypeStruct((rows, cols), F32)
    return pl.pallas_call(
        body, name=name, grid=(rows // tr,), in_specs=[spec] * 4, out_specs=[spec] * 3, out_shape=[shape] * 3,
        compiler_params=_params(),
    )(g, w, m, v)


def _ada_grad_adamw(sc_t, dmod_shard, w, m, v, tr):
    rows, cols = w.shape

    def body(s_ref, dm_ref, w_ref, m_ref, v_ref, g_ref, d_ref, nm_ref, nv_ref):
        g = s_ref[:, 0:1] * dm_ref[0:1, :]
        for b in range(1, N_DEV):
            g = g + s_ref[:, b:b + 1] * dm_ref[b:b + 1, :]
        g_ref[...] = g
        d, nm, nv = _adamw_math(w_ref[...], g, m_ref[...], v_ref[...])
        d_ref[...] = d
        nm_ref[...] = nm
        nv_ref[...] = nv

    spec = _rows(tr, cols)
    shape = jax.ShapeDtypeStruct((rows, cols), F32)
    return pl.pallas_call(
        body, name="ada_grad_adamw", grid=(rows // tr,),
        in_specs=[_rows(tr, N_DEV), _const((N_DEV, cols)), spec, spec, spec],
        out_specs=[spec] * 4, out_shape=[shape] * 4, compiler_params=_params(),
    )(sc_t, dmod_shard, w, m, v)


def _mod_grads(da1, dshift1, dgate1, dg1post, da2, dshift2, dgate2, dg2post, mod6, n1pre, n2pre):
    def body(da1_ref, ds1_ref, dgt1_ref, dg1_ref, da2_ref, ds2_ref, dgt2_ref, dg2_ref, mod_ref, n1_ref, n2_ref,
             dmod_ref, dn_ref):
        dmod_ref[0:1, :] = ds1_ref[...]
        dmod_ref[1:2, :] = da1_ref[...] * n1_ref[...]
        dmod_ref[2:3, :] = dgt1_ref[...]
        dmod_ref[3:4, :] = ds2_ref[...]
        dmod_ref[4:5, :] = da2_ref[...] * n2_ref[...]
        dmod_ref[5:6, :] = dgt2_ref[...]
        dn_ref[0:1, :] = da1_ref[...] * (1.0 + mod_ref[1:2, :])
        dn_ref[1:2, :] = dg1_ref[...]
        dn_ref[2:3, :] = da2_ref[...] * (1.0 + mod_ref[4:5, :])
        dn_ref[3:4, :] = dg2_ref[...]

    return pl.pallas_call(
        body, name="mod_grads",
        out_shape=[jax.ShapeDtypeStruct((N_MOD, D_MODEL), F32), jax.ShapeDtypeStruct((4, D_MODEL), F32)],
    )(da1, dshift1, dgate1, dg1post, da2, dshift2, dgate2, dg2post, mod6, n1pre, n2pre)


def _position():
    x, y, c = lax.axis_index("x"), lax.axis_index("y"), lax.axis_index("c")
    return x, y, c


def _flip(v, bit):
    return 1 - v if bit else v


def _peer(x, y, c, k):
    return (_flip(x, k & 4), _flip(y, k & 2), _flip(c, k & 1))


def _remote(src, dst, send_sem, recv_sem, device):
    return pltpu.make_async_remote_copy(src_ref=src, dst_ref=dst, send_sem=send_sem, recv_sem=recv_sem,
                                        device_id=device, device_id_type=MESH)


def _cast_bf16(w, tr):
    rows, cols = w.shape

    def body(w_ref, o_ref):
        o_ref[...] = w_ref[...].astype(BF16)

    return pl.pallas_call(
        body, name="cast_bf16_%dx%d" % (rows, cols), grid=(rows // tr,), in_specs=[_rows(tr, cols)],
        out_specs=_rows(tr, cols), out_shape=jax.ShapeDtypeStruct((rows, cols), BF16), compiler_params=_params(),
    )(w)


def _mod_exchange(c_row, w_ada_shard, b_ada_row):
    cs = w_ada_shard.shape[1]

    def body(c_ref, w_ref, b_ref, mod_ref, sc_ref, rows_ref, send1, recv1, send2, recv2):
        x, y, c = _position()
        me = 4 * x + 2 * y + c
        chip = 2 * x + y
        cv = c_ref[...]
        sc_ref[me] = cv * jax.nn.sigmoid(cv)
        gather = [_remote(sc_ref.at[me], sc_ref.at[me], send1.at[k - 1], recv1.at[k - 1], _peer(x, y, c, k))
                  for k in range(1, N_DEV)]
        for cp in gather:
            cp.start()
        for k in range(1, N_DEV):
            px, py, pc = _peer(x, y, c, k)
            src = 4 * px + 2 * py + pc
            _remote(sc_ref.at[src], sc_ref.at[src], send1.at[k - 1], recv1.at[k - 1], (px, py, pc)).wait_recv()
        for cp in gather:
            cp.wait_send()
        sc_all = jnp.concatenate([sc_ref[b] for b in range(N_DEV)], axis=0)
        part = jnp.dot(sc_all, w_ref[...], precision=lax.Precision.HIGHEST, preferred_element_type=F32)
        part = part + b_ref[:, pl.ds(pl.multiple_of(chip * cs, 128), cs)]
        for b in range(N_DEV):
            rows_ref[b] = part[b:b + 1, :]
        mod_ref[chip] = rows_ref[me]
        hand = []
        for k in (2, 4, 6):
            px, py, _ = _peer(x, y, c, k)
            hand.append(_remote(rows_ref.at[4 * px + 2 * py + c], mod_ref.at[chip], send2.at[k // 2 - 1],
                                recv2.at[k // 2 - 1], (px, py, c)))
        for cp in hand:
            cp.start()
        for k in (2, 4, 6):
            px, py, _ = _peer(x, y, c, k)
            pchip = 2 * px + py
            _remote(rows_ref.at[me], mod_ref.at[pchip], send2.at[k // 2 - 1], recv2.at[k // 2 - 1],
                    (px, py, c)).wait_recv()
        for cp in hand:
            cp.wait_send()

    return pl.pallas_call(
        body, name="mod_exchange",
        in_specs=[_VMEM, _VMEM, _VMEM], out_specs=[_VMEM, _VMEM],
        out_shape=[jax.ShapeDtypeStruct((N_CHIPS, 1, cs), F32), jax.ShapeDtypeStruct((N_DEV, 1, D_MODEL), F32)],
        scratch_shapes=[pltpu.VMEM((N_DEV, 1, cs), F32), pltpu.SemaphoreType.DMA((N_DEV - 1,)),
                        pltpu.SemaphoreType.DMA((N_DEV - 1,)), pltpu.SemaphoreType.DMA((N_CHIPS - 1,)),
                        pltpu.SemaphoreType.DMA((N_CHIPS - 1,))],
        compiler_params=pltpu.CompilerParams(vmem_limit_bytes=VMEM_LIMIT),
    )(c_row, w_ada_shard, b_ada_row)


_HBM = pl.BlockSpec(memory_space=pltpu.HBM)
_SEM = pl.BlockSpec(memory_space=pltpu.SEMAPHORE)
_EFFECT = pltpu.SideEffectType.DATAFLOW_SIDE_EFFECTING
_CHIP_HOPS = (2, 4, 6)


def _in_hbm(a):
    return pltpu.with_memory_space_constraint(a, pltpu.HBM)


def _sems3():
    return pltpu.SemaphoreType.DMA((len(_CHIP_HOPS),))


def _ag_start(shards):
    n = len(shards)

    def body(*refs):
        ins, lands = refs[:n], refs[n:2 * n]
        sends, recvs = refs[2 * n:3 * n], refs[3 * n:4 * n]
        token = refs[6 * n]
        local_sem = refs[6 * n + 1]
        x, y, c = _position()
        chip = 2 * x + y
        local = [pltpu.make_async_copy(ins[i], lands[i].at[chip], local_sem.at[i]) for i in range(n)]
        for cp in local:
            cp.start()
        for i in range(n):
            half = ins[i].shape[0] // 2
            mine = pl.ds(c * half, half)
            for s, k in enumerate(_CHIP_HOPS):
                px, py, _ = _peer(x, y, c, k)
                _remote(ins[i].at[mine], lands[i].at[chip, mine], sends[i].at[s], recvs[i].at[s], (px, py, c)).start()
        token[...] = jnp.zeros_like(token)
        for cp in local:
            cp.wait()

    lands = [_in_hbm(lax.empty((N_CHIPS,) + s.shape, BF16)) for s in shards]
    out = pl.pallas_call(
        body, name="ag_start",
        in_specs=[_HBM] * (2 * n),
        out_specs=[_SEM] * (2 * n) + [_HBM] * (2 * n) + [_VMEM],
        out_shape=[_sems3()] * (2 * n) + [pltpu.HBM(s.shape, BF16) for s in shards]
        + [pltpu.HBM((N_CHIPS,) + s.shape, BF16) for s in shards] + [jax.ShapeDtypeStruct((8, 128), F32)],
        input_output_aliases={i: 2 * n + i for i in range(2 * n)},
        scratch_shapes=[pltpu.SemaphoreType.DMA((n,))],
        compiler_params=pltpu.CompilerParams(has_side_effects=_EFFECT),
    )(*[_in_hbm(s) for s in shards], *lands)
    per_weight = [(out[2 * n + i], out[3 * n + i], out[i], out[n + i]) for i in range(n)]
    return per_weight, out[4 * n]


def _ag_pass(group, after, name):
    n = len(group)

    def body(*refs):
        ins, lands = refs[:n], refs[n:2 * n]
        sends, recvs = refs[2 * n:3 * n], refs[3 * n:4 * n]
        fsends, frecvs = refs[5 * n + 1:6 * n + 1], refs[6 * n + 1:7 * n + 1]
        x, y, c = _position()
        chip = 2 * x + y
        for i in range(n):
            half = ins[i].shape[0] // 2
            mine = pl.ds(c * half, half)
            for s, k in enumerate(_CHIP_HOPS):
                px, py, _ = _peer(x, y, c, k)
                landed = lands[i].at[2 * px + py, mine]
                _remote(ins[i].at[mine], landed, sends[i].at[s], recvs[i].at[s], (px, py, c)).wait_recv()
                _remote(landed, landed, fsends[i].at[s], frecvs[i].at[s], (x, y, 1 - c)).start()
        for i in range(n):
            half = ins[i].shape[0] // 2
            mine = pl.ds(c * half, half)
            for s, k in enumerate(_CHIP_HOPS):
                px, py, _ = _peer(x, y, c, k)
                _remote(ins[i].at[mine], lands[i].at[chip, mine], sends[i].at[s], recvs[i].at[s],
                        (px, py, c)).wait_send()

    out = pl.pallas_call(
        body, name=name,
        in_specs=[_HBM] * (2 * n) + [_SEM] * (2 * n) + [_ANY],
        out_specs=[_HBM] * n + [_SEM] * (2 * n),
        out_shape=[pltpu.HBM(g[1].shape, BF16) for g in group] + [_sems3()] * (2 * n),
        input_output_aliases={n + i: i for i in range(n)},
        compiler_params=pltpu.CompilerParams(has_side_effects=_EFFECT),
    )(*[g[0] for g in group], *[g[1] for g in group], *[g[2] for g in group], *[g[3] for g in group], after)
    return [(out[i], out[n + i], out[2 * n + i]) for i in range(n)]


def _ag_done(group, name):
    n = len(group)

    def body(*refs):
        lands = refs[:n]
        fsends, frecvs = refs[n:2 * n], refs[2 * n:3 * n]
        x, y, c = _position()
        for i in range(n):
            half = lands[i].shape[1] // 2
            for s, k in enumerate(_CHIP_HOPS):
                px, py, _ = _peer(x, y, c, k)
                sent = lands[i].at[2 * px + py, pl.ds(c * half, half)]
                got = lands[i].at[2 * px + py, pl.ds((1 - c) * half, half)]
                cp = _remote(sent, got, fsends[i].at[s], frecvs[i].at[s], (x, y, 1 - c))
                cp.wait_recv()
                cp.wait_send()

    out = pl.pallas_call(
        body, name=name,
        in_specs=[_HBM] * n + [_SEM] * (2 * n),
        out_specs=[_HBM] * n,
        out_shape=[pltpu.HBM(g[0].shape, BF16) for g in group],
        input_output_aliases={i: i for i in range(n)},
        compiler_params=pltpu.CompilerParams(has_side_effects=_EFFECT),
    )(*[g[0] for g in group], *[g[1] for g in group], *[g[2] for g in group])
    return list(out)


def _all_reduce_small(v):
    rows = v.shape[0]

    def body(v_ref, out_ref, sib_ref, slots_ref, send_a, recv_a, send_b, recv_b):
        x, y, c = _position()
        chip = 2 * x + y
        pair = _remote(v_ref, sib_ref, send_a, recv_a, (x, y, 1 - c))
        pair.start()
        pair.wait_recv()
        slots_ref[chip] = v_ref[...] + sib_ref[...]
        spread = []
        for k in (2, 4, 6):
            px, py, _ = _peer(x, y, c, k)
            spread.append(_remote(slots_ref.at[chip], slots_ref.at[chip], send_b.at[k // 2 - 1], recv_b.at[k // 2 - 1],
                                  (px, py, c)))
        for cp in spread:
            cp.start()
        for k in (2, 4, 6):
            px, py, _ = _peer(x, y, c, k)
            landed = slots_ref.at[2 * px + py]
            _remote(landed, landed, send_b.at[k // 2 - 1], recv_b.at[k // 2 - 1], (px, py, c)).wait_recv()
        out_ref[...] = ((slots_ref[0] + slots_ref[1]) + slots_ref[2]) + slots_ref[3]
        pair.wait_send()
        for cp in spread:
            cp.wait_send()

    return pl.pallas_call(
        body, name="all_reduce_small",
        in_specs=[_VMEM], out_specs=_VMEM, out_shape=jax.ShapeDtypeStruct((rows, 128), F32),
        scratch_shapes=[pltpu.VMEM((rows, 128), F32), pltpu.VMEM((N_CHIPS, rows, 128), F32),
                        pltpu.SemaphoreType.DMA, pltpu.SemaphoreType.DMA,
                        pltpu.SemaphoreType.DMA((N_CHIPS - 1,)), pltpu.SemaphoreType.DMA((N_CHIPS - 1,))],
        compiler_params=pltpu.CompilerParams(vmem_limit_bytes=VMEM_LIMIT),
    )(v)


def _sibling_exchange(pgbs, name):
    n = len(pgbs)

    def body(*refs):
        ins, outs = refs[:n], refs[n:2 * n]
        send, recv = refs[2 * n], refs[2 * n + 1]
        x, y, c = _position()
        copies = [_remote(ins[i].at[j, 1 - c], outs[i].at[j], send.at[N_CHIPS * i + j], recv.at[N_CHIPS * i + j],
                          (x, y, 1 - c)) for i in range(n) for j in range(N_CHIPS)]
        for cp in copies:
            cp.start()
        for cp in copies:
            cp.wait()

    return pl.pallas_call(
        body, name=name, in_specs=[_ANY] * n, out_specs=[_ANY] * n,
        out_shape=[jax.ShapeDtypeStruct((N_CHIPS,) + p.shape[2:], BF16) for p in pgbs],
        scratch_shapes=[pltpu.SemaphoreType.DMA((N_CHIPS * n,)), pltpu.SemaphoreType.DMA((N_CHIPS * n,))],
    )(*pgbs)


def _sibling_add(pg, from_sib, pos, name):
    _, _, r, cdim = pg.shape

    def body(pos_ref, pg_ref, fs_ref, qb_ref, own_ref):
        j = pl.program_id(0)
        q = pg_ref[0, 0] + fs_ref[0].astype(F32)
        qb_ref[0] = q.astype(BF16)

        @pl.when(j == pos_ref[1])
        def _():
            own_ref[...] = q

    return pl.pallas_call(
        body, name=name,
        grid_spec=pltpu.PrefetchScalarGridSpec(
            num_scalar_prefetch=1, grid=(N_CHIPS,),
            in_specs=[pl.BlockSpec((1, 1, r, cdim), lambda j, pos: (j, pos[0], 0, 0)),
                      pl.BlockSpec((1, r, cdim), lambda j, pos: (j, 0, 0))],
            out_specs=[pl.BlockSpec((1, r, cdim), lambda j, pos: (j, 0, 0)),
                       pl.BlockSpec((r, cdim), lambda j, pos: (0, 0))]),
        out_shape=[jax.ShapeDtypeStruct((N_CHIPS, r, cdim), BF16), jax.ShapeDtypeStruct((r, cdim), F32)],
        compiler_params=_params(),
    )(pos, pg, from_sib)


def _rs_start(qbs, name):
    n = len(qbs)

    def body(*refs):
        outs, inboxes, zeros = refs[:n], refs[n:2 * n], refs[2 * n:3 * n]
        sends, recvs = refs[3 * n:4 * n], refs[4 * n:5 * n]
        token = refs[7 * n]
        local_sem = refs[7 * n + 1]
        x, y, c = _position()
        chip = 2 * x + y
        local = [pltpu.make_async_copy(zeros[i], inboxes[i].at[chip], local_sem.at[i]) for i in range(n)]
        for cp in local:
            cp.start()
        for i in range(n):
            for s, k in enumerate(_CHIP_HOPS):
                px, py, _ = _peer(x, y, c, k)
                _remote(outs[i].at[2 * px + py], inboxes[i].at[chip], sends[i].at[s], recvs[i].at[s], (px, py, c)).start()
        token[...] = jnp.zeros_like(token)
        for cp in local:
            cp.wait()

    inboxes = [_in_hbm(lax.empty(q.shape, BF16)) for q in qbs]
    zeros = [_in_hbm(jnp.zeros(q.shape[1:], BF16)) for q in qbs]
    out = pl.pallas_call(
        body, name=name,
        in_specs=[_HBM] * (3 * n),
        out_specs=[_SEM] * (2 * n) + [_HBM] * (2 * n) + [_VMEM],
        out_shape=[_sems3()] * (2 * n) + [pltpu.HBM(q.shape, BF16) for q in qbs] * 2 + [jax.ShapeDtypeStruct((8, 128), F32)],
        input_output_aliases={i: 2 * n + i for i in range(2 * n)},
        scratch_shapes=[pltpu.SemaphoreType.DMA((n,))],
        compiler_params=pltpu.CompilerParams(has_side_effects=_EFFECT),
    )(*[_in_hbm(q) for q in qbs], *inboxes, *zeros)
    per_weight = [(out[2 * n + i], out[3 * n + i], out[i], out[n + i]) for i in range(n)]
    return per_weight, out[4 * n]


def _rs_wait(group, after, name):
    n = len(group)

    def body(*refs):
        outs, inboxes = refs[:n], refs[n:2 * n]
        sends, recvs = refs[2 * n:3 * n], refs[3 * n:4 * n]
        x, y, c = _position()
        for i in range(n):
            for s, k in enumerate(_CHIP_HOPS):
                px, py, _ = _peer(x, y, c, k)
                slot = 2 * px + py
                cp = _remote(outs[i].at[slot], inboxes[i].at[slot], sends[i].at[s], recvs[i].at[s], (px, py, c))
                cp.wait_recv()
                cp.wait_send()

    out = pl.pallas_call(
        body, name=name,
        in_specs=[_HBM] * (2 * n) + [_SEM] * (2 * n) + [_ANY],
        out_specs=[_HBM] * n,
        out_shape=[pltpu.HBM(g[1].shape, BF16) for g in group],
        input_output_aliases={n + i: i for i in range(n)},
        compiler_params=pltpu.CompilerParams(has_side_effects=_EFFECT),
    )(*[g[0] for g in group], *[g[1] for g in group], *[g[2] for g in group], *[g[3] for g in group], after)
    return list(out)


def _final_add(inbox, own, name, tr):
    _, r, cdim = inbox.shape

    def body(in_ref, own_ref, o_ref):
        total = ((in_ref[0].astype(F32) + in_ref[1].astype(F32)) + in_ref[2].astype(F32)) + in_ref[3].astype(F32)
        o_ref[...] = total + own_ref[...]

    return pl.pallas_call(
        body, name=name, grid=(r // tr,),
        in_specs=[pl.BlockSpec((N_CHIPS, tr, cdim), lambda i: (0, i, 0)), _rows(tr, cdim)],
        out_specs=_rows(tr, cdim), out_shape=jax.ShapeDtypeStruct((r, cdim), F32), compiler_params=_params(),
    )(inbox, own)


def _sibling_share(halves, name):
    n = len(halves)

    def body(*refs):
        ins, outs = refs[:n], refs[n:2 * n]
        local_sem, send, recv = refs[2 * n:]
        x, y, c = _position()
        local = [pltpu.make_async_copy(ins[i], outs[i].at[c], local_sem.at[i]) for i in range(n)]
        copies = [_remote(ins[i], outs[i].at[c], send.at[i], recv.at[i], (x, y, 1 - c)) for i in range(n)]
        for cp in local + copies:
            cp.start()
        for i in range(n):
            _remote(ins[i], outs[i].at[1 - c], send.at[i], recv.at[i], (x, y, 1 - c)).wait_recv()
        for cp in copies:
            cp.wait_send()
        for cp in local:
            cp.wait()

    return pl.pallas_call(
        body, name=name, in_specs=[_ANY] * n, out_specs=[_ANY] * n,
        out_shape=[jax.ShapeDtypeStruct((2,) + h.shape, F32) for h in halves],
        scratch_shapes=[pltpu.SemaphoreType.DMA((n,))] * 3,
    )(*halves)


_SMALL = (("b_ada", N_MOD * D_MODEL), ("norm1_pre", D_MODEL), ("norm1_post", D_MODEL), ("norm2_pre", D_MODEL),
          ("norm2_post", D_MODEL), ("w_spatial", N_HEADS * CHUNK * CHUNK), ("b_spatial", N_HEADS * CHUNK),
          ("ln_v_gain", D_A), ("ln_v_bias", D_A), ("w_pool", N_HEADS * GROUP_DIM * GROUP_DIM),
          ("b_pool", D_B), ("pool_scale", D_B))
_MOD_ROWS = N_MOD * D_MODEL // 128


def _packed_rows(size):
    return -(-(size // 128) // 8) * 8


def _pack(parts):
    out = []
    for name, size in _SMALL:
        a = parts[name].reshape(size // 128, 128)
        pad = _packed_rows(size) - a.shape[0]
        out.append(jnp.pad(a, ((0, pad), (0, 0))) if pad else a)
    return out


def _unpack(packed, shapes):
    res = {}
    row = 0
    for name, size in _SMALL:
        res[name] = packed[row:row + size // 128].reshape(shapes[name])
        row += _packed_rows(size)
    return res


def _small_adamw(total, wp, mp, vp):
    rows = wp.shape[0]
    head = N_DEV * _MOD_ROWS

    def body(t_ref, w_ref, m_ref, v_ref, g_ref, d_ref, nm_ref, nv_ref):
        gb = t_ref[0:_MOD_ROWS, :]
        for b in range(1, N_DEV):
            gb = gb + t_ref[b * _MOD_ROWS:(b + 1) * _MOD_ROWS, :]
        g_ref[0:_MOD_ROWS, :] = gb
        g_ref[_MOD_ROWS:, :] = t_ref[head:, :]
        d, nm, nv = _adamw_math(w_ref[...], g_ref[...], m_ref[...], v_ref[...])
        d_ref[...] = d
        nm_ref[...] = nm
        nv_ref[...] = nv

    return pl.pallas_call(
        body, name="small_adamw", out_shape=[jax.ShapeDtypeStruct((rows, 128), F32)] * 4,
        compiler_params=pltpu.CompilerParams(vmem_limit_bytes=VMEM_LIMIT),
    )(total, wp, mp, vp)


def kernel(x, c, w_ada, b_ada, norm1_pre, norm1_post, w_in, w_spatial, b_spatial, ln_v_gain, ln_v_bias, w_pool, b_pool, pool_scale, w_out, norm2_pre, norm2_post, w_fc1, w_fc2, loss_target, m_w_ada, m_b_ada, m_norm1_pre, m_norm1_post, m_w_in, m_w_spatial, m_b_spatial, m_ln_v_gain, m_ln_v_bias, m_w_pool, m_b_pool, m_pool_scale, m_w_out, m_norm2_pre, m_norm2_post, m_w_fc1, m_w_fc2, v_w_ada, v_b_ada, v_norm1_pre, v_norm1_post, v_w_in, v_w_spatial, v_b_spatial, v_ln_v_gain, v_ln_v_bias, v_w_pool, v_b_pool, v_pool_scale, v_w_out, v_norm2_pre, v_norm2_post, v_w_fc1, v_w_fc2):
    weights = dict(w_ada=w_ada, b_ada=b_ada, norm1_pre=norm1_pre, norm1_post=norm1_post, w_in=w_in,
                   w_spatial=w_spatial, b_spatial=b_spatial, ln_v_gain=ln_v_gain, ln_v_bias=ln_v_bias, w_pool=w_pool,
                   b_pool=b_pool, pool_scale=pool_scale, w_out=w_out, norm2_pre=norm2_pre, norm2_post=norm2_post,
                   w_fc1=w_fc1, w_fc2=w_fc2)
    m_old = dict(w_ada=m_w_ada, b_ada=m_b_ada, norm1_pre=m_norm1_pre, norm1_post=m_norm1_post, w_in=m_w_in,
                 w_spatial=m_w_spatial, b_spatial=m_b_spatial, ln_v_gain=m_ln_v_gain, ln_v_bias=m_ln_v_bias,
                 w_pool=m_w_pool, b_pool=m_b_pool, pool_scale=m_pool_scale, w_out=m_w_out, norm2_pre=m_norm2_pre,
                 norm2_post=m_norm2_post, w_fc1=m_w_fc1, w_fc2=m_w_fc2)
    v_old = dict(w_ada=v_w_ada, b_ada=v_b_ada, norm1_pre=v_norm1_pre, norm1_post=v_norm1_post, w_in=v_w_in,
                 w_spatial=v_w_spatial, b_spatial=v_b_spatial, ln_v_gain=v_ln_v_gain, ln_v_bias=v_ln_v_bias,
                 w_pool=v_w_pool, b_pool=v_b_pool, pool_scale=v_pool_scale, w_out=v_w_out, norm2_pre=v_norm2_pre,
                 norm2_post=v_norm2_post, w_fc1=v_w_fc1, w_fc2=v_w_fc2)
    order = ("w_ada", "b_ada", "norm1_pre", "norm1_post", "w_in", "w_spatial", "b_spatial", "ln_v_gain", "ln_v_bias",
             "w_pool", "b_pool", "pool_scale", "w_out", "norm2_pre", "norm2_post", "w_fc1", "w_fc2")
    big = ("w_in", "w_out", "w_fc1", "w_fc2")
    mx, my, mc = _position()
    me = 4 * mx + 2 * my + mc
    chip = 2 * mx + my
    row = lambda a: a.reshape(1, -1)

    pos = jnp.stack([mc, chip]).astype(jnp.int32)
    after = lambda a, token: lax.optimization_barrier((a, token))[0]
    xs, target = x[0], loss_target[0]
    n1pre, n1post, n2pre, n2post = row(norm1_pre), row(norm1_post), row(norm2_pre), row(norm2_post)
    mixer = (w_spatial, jnp.repeat(b_spatial.T, HEAD_DIM, axis=1), row(ln_v_gain), row(ln_v_bias), w_pool,
             row(b_pool), row(pool_scale))
    ts_big, ts_mid, tk = 512, 256, 1024

    ag, token = _ag_start([_cast_bf16(weights[n], 256) for n in big])
    mod4, sc_all = _mod_exchange(after(c, token), w_ada, row(b_ada))
    mod6 = mod4.reshape(N_MOD, D_MODEL)

    (win_g,) = _ag_done(_ag_pass([ag[0]], mod4, "ag_pass_in"), "ag_done_in")
    z, h1 = _fwd_in(xs, mod6, n1pre, win_g, ts_big)
    ycat = _mixer_fwd(z, *mixer, ts_mid)
    wout_g, fc1_g = _ag_done(_ag_pass([ag[1], ag[2]], ycat, "ag_pass_mid"), "ag_done_mid")
    mix, x1, h2 = _fwd_out(ycat, xs, mod6, n1post, n2pre, wout_g, ts_big)
    q = _fwd_fc1(h2, fc1_g, ts_big)
    (fc2_g,) = _ag_done(_ag_pass([ag[3]], q, "ag_pass_fc2"), "ag_done_fc2")
    dy, df, loss, dgate2, dg2post = _fwd_fc2_loss(q, x1, target, mod6, n2post, fc2_g, ts_big)
    loss = lax.psum(loss[0, 0], ("x", "y", "c"))

    def reduce_start(partials, tag):
        from_sib = _sibling_exchange([p[1] for p in partials], "sibling_exchange_" + tag)
        sums = [_sibling_add(p[0], fs, pos, "sibling_add_%s_%d" % (tag, i))
                for i, (p, fs) in enumerate(zip(partials, from_sib))]
        state, tok = _rs_start([s[0] for s in sums], "rs_start_" + tag)
        return state, [s[1] for s in sums], tok

    def reduce_finish(state, owns, names, tag, dep):
        inboxes = _rs_wait(state, dep, "rs_wait_" + tag)
        halves = [_final_add(ib, own, "final_add_" + n, min(256, own.shape[0])) for ib, own, n in zip(inboxes, owns, names)]
        shards = _sibling_share(halves, "sibling_share_" + tag)
        for n, g in zip(names, shards):
            g = g.reshape(weights[n].shape)
            grads[n] = g
            deltas[n], new_m[n], new_v[n] = _adamw(g, weights[n], m_old[n], v_old[n], "adamw_" + n, 256)

    grads, deltas, new_m, new_v = {}, {}, {}, {}
    g_fc2 = _wgrad(q, df, "wgrad_fc2", False, tk)
    dp = _bwd_fc2(df, q, fc2_g, ts_big)
    g_fc1 = _wgrad(h2, dp, "wgrad_fc1", True, tk)
    state_fc, owns_fc, tok_fc = reduce_start([g_fc2, g_fc1], "fc")
    dx1, dmix, dyc, dshift2, da2, dgate1, dg1post = _bwd_fc1_out(after(dp, tok_fc), dy, x1, mix, mod6, n2pre, n1post,
                                                                 fc1_g, wout_g, ts_mid)
    g_out = _wgrad(ycat, dmix, "wgrad_out", False, tk)
    dz, dws, dbsp, dgain, dbias, dwp, dbp, dps = _mixer_bwd(z, dyc, *mixer, ts_mid)
    g_in = _wgrad(h1, dz, "wgrad_in", True, tk)
    state_io, owns_io, tok_io = reduce_start([g_out, g_in], "io")
    grad_x, dshift1, da1 = _bwd_in(after(dz, tok_io), dx1, xs, mod6, n1pre, win_g, ts_mid)
    dmod6, dnorms = _mod_grads(da1, dshift1, dgate1, dg1post, da2, dshift2, dgate2, dg2post, mod6, n1pre, n2pre)

    parts = dict(b_ada=dmod6, norm1_pre=dnorms[0], norm1_post=dnorms[1], norm2_pre=dnorms[2], norm2_post=dnorms[3],
                 w_spatial=dws, b_spatial=dbsp, ln_v_gain=dgain, ln_v_bias=dbias, w_pool=dwp, b_pool=dbp,
                 pool_scale=dps)
    pieces = _pack(parts)
    slots = lax.dynamic_update_slice(jnp.zeros((N_DEV * _MOD_ROWS, 128), F32), pieces[0], (me * _MOD_ROWS, 0))
    total = _all_reduce_small(jnp.concatenate([slots] + pieces[1:], axis=0))
    reduce_finish(state_fc, owns_fc, ("w_fc2", "w_fc1"), "fc", total)
    reduce_finish(state_io, owns_io, ("w_out", "w_in"), "io", deltas["w_fc1"])
    gp, dp, nmp, nvp = _small_adamw(total, jnp.concatenate(_pack(weights), axis=0),
                                    jnp.concatenate(_pack(m_old), axis=0), jnp.concatenate(_pack(v_old), axis=0))
    shapes = {n: weights[n].shape for n, _ in _SMALL}
    for res, packed in ((grads, gp), (deltas, dp), (new_m, nmp), (new_v, nvp)):
        res.update(_unpack(packed, shapes))

    dmod_all = total[:N_DEV * _MOD_ROWS].reshape(N_DEV, N_MOD * D_MODEL)
    cs = w_ada.shape[1]
    dmod_shard = lax.dynamic_slice(dmod_all, (0, chip * cs), (N_DEV, cs))
    sc_t = sc_all.reshape(N_DEV, D_MODEL).T
    grads["w_ada"], deltas["w_ada"], new_m["w_ada"], new_v["w_ada"] = _ada_grad_adamw(
        sc_t, dmod_shard, w_ada, m_w_ada, v_w_ada, 256)

    return (loss, grad_x[None], *[grads[n] for n in order], *[deltas[n] for n in order],
            *[new_m[n] for n in order], *[new_v[n] for n in order])
```

```python
import functools

import jax
import jax.numpy as jnp
from jax import lax
from jax.experimental import pallas as pl
from jax.experimental.pallas import tpu as pltpu

F32 = jnp.float32
BF16 = jnp.bfloat16
MESH = pl.DeviceIdType.MESH

D_MODEL = 1024
D_A = 512
D_B = 512
D_Z = 2 * D_A + D_B
N_HEADS = 4
HEAD_DIM = 128
CHUNK = 128
POOL_WINDOWS = (2, 4, 8, 16)
GROUP_DIM = 128
D_FF = 4096
N_MOD = 6
EPS = 1e-6
HALO = 16
N_CHIPS = 4
N_DEV = 8

ADAM_LR = 0.001
ADAM_B1 = 0.9
ADAM_B2 = 0.999
ADAM_EPS = 1e-08
ADAM_WD = 0.01
ADAM_STEP = 10

VMEM_LIMIT = 56 * 1024 * 1024

_VMEM = pl.BlockSpec(memory_space=pltpu.VMEM)
_ANY = pl.BlockSpec(memory_space=pl.ANY)


def _params(n_grid_axes=1):
    return pltpu.CompilerParams(dimension_semantics=("arbitrary",) * n_grid_axes, vmem_limit_bytes=VMEM_LIMIT)


def _rows(ts, width):
    return pl.BlockSpec((ts, width), lambda i: (i, 0))


def _const(shape):
    return pl.BlockSpec(shape, lambda i: (0,) * len(shape))


def _dot(a, b):
    return jnp.dot(a, b, preferred_element_type=F32)


def _dot_nt(a, b):
    return lax.dot_general(a, b, (((1,), (1,)), ((), ())), preferred_element_type=F32)


def _dot_tn(a, b):
    return lax.dot_general(a, b, (((0,), (0,)), ((), ())), preferred_element_type=F32)


def _rowmean(v):
    return jnp.mean(v, axis=-1, keepdims=True)


def _colsum(v):
    return jnp.sum(v, axis=0, keepdims=True)


def _gelu_parts(z):
    k0 = 0.7978845608028654
    k1 = 0.044715
    z2 = z * z
    t = jnp.tanh(k0 * (z + k1 * z * z2))
    g = 0.5 * z * (1.0 + t)
    dg = 0.5 * (1.0 + t) + 0.5 * z * (1.0 - t * t) * (k0 * (1.0 + 3.0 * k1 * z2))
    return g, dg


def _tril_weights(ws_ref):
    r = lax.broadcasted_iota(jnp.int32, (CHUNK, CHUNK), 0)
    s = lax.broadcasted_iota(jnp.int32, (CHUNK, CHUNK), 1)
    mask = (s <= r).astype(F32)
    return [(ws_ref[h] * mask).astype(BF16) for h in range(N_HEADS)]


def _window_counts(first_row, n_rows):
    pos = (first_row + lax.broadcasted_iota(jnp.int32, (n_rows, 1), 0)).astype(F32)
    return pos, [jnp.minimum(pos + 1.0, float(w)) for w in POOL_WINDOWS]


def _causal_window_sums(ext):
    out = []
    e = ext
    shift = 1
    for g in range(len(POOL_WINDOWS)):
        e = e + pltpu.roll(e, shift, 0)
        shift *= 2
        out.append(e[:, g * GROUP_DIM:(g + 1) * GROUP_DIM])
    return out


def _anticausal_window_sums(ext):
    n = ext.shape[0]
    out = []
    e = ext
    shift = 1
    for g in range(len(POOL_WINDOWS)):
        e = e + pltpu.roll(e, n - shift, 0)
        shift *= 2
        out.append(e[:, g * GROUP_DIM:(g + 1) * GROUP_DIM])
    return out


def _fwd_in(x, mod6, n1pre, win_g, ts):
    s_len = x.shape[0]
    cs = D_Z // N_CHIPS

    def body(x_ref, mod_ref, g_ref, w_ref, z_ref, h_ref):
        xv = x_ref[...]
        r = lax.rsqrt(_rowmean(xv * xv) + EPS)
        h = ((xv * r) * g_ref[...]) * (1.0 + mod_ref[1:2, :]) + mod_ref[0:1, :]
        hb = h.astype(BF16)
        h_ref[...] = hb
        for j in range(N_CHIPS):
            z_ref[:, j * cs:(j + 1) * cs] = _dot(hb, w_ref[j])

    return pl.pallas_call(
        body, name="fwd_in", grid=(s_len // ts,),
        in_specs=[_rows(ts, D_MODEL), _const((N_MOD, D_MODEL)), _const((1, D_MODEL)), _VMEM],
        out_specs=[_rows(ts, D_Z), _rows(ts, D_MODEL)],
        out_shape=[jax.ShapeDtypeStruct((s_len, D_Z), F32), jax.ShapeDtypeStruct((s_len, D_MODEL), BF16)],
        compiler_params=_params(),
    )(x, mod6, n1pre, win_g)


def _mixer_forward_tile(za, wc, bsp_ref, gain, bias, mixed_ref):
    ga, dga = _gelu_parts(za)
    u = ga[:, :D_A]
    v = ga[:, D_A:]
    mu = _rowmean(v)
    vc = v - mu
    rstd = lax.rsqrt(_rowmean(vc * vc) + EPS)
    vhat = vc * rstd
    vn = (vhat * gain + bias).astype(BF16)
    ts = za.shape[0]
    for k in range(ts // CHUNK):
        for h in range(N_HEADS):
            blk = vn[k * CHUNK:(k + 1) * CHUNK, h * HEAD_DIM:(h + 1) * HEAD_DIM]
            mixed_ref[k * CHUNK:(k + 1) * CHUNK, h * HEAD_DIM:(h + 1) * HEAD_DIM] = (
                _dot(wc[h], blk) + bsp_ref[:, h * HEAD_DIM:(h + 1) * HEAD_DIM])
    return u, vhat, rstd, vn, dga


def _mixer_fwd(z, w_spatial, bsp_full, gain, bias, w_pool, b_pool, pool_scale, ts):
    s_len = z.shape[0]
    nb = ts // HALO

    def body(z_ref, zprev_ref, ws_ref, bsp_ref, gain_ref, bias_ref, wp_ref, bp_ref, ps_ref, y_ref, mixed_ref):
        i = pl.program_id(0)
        wc = _tril_weights(ws_ref)
        u, _, _, _, _ = _mixer_forward_tile(z_ref[:, :2 * D_A], wc, bsp_ref, gain_ref[...], bias_ref[...], mixed_ref)
        y_ref[:, :D_A] = (u * mixed_ref[...]).astype(BF16)

        zb = z_ref[:, 2 * D_A:]
        prev = jnp.where(i == 0, 0.0, zprev_ref[...])
        ext = jnp.concatenate([prev, zb], axis=0)
        sums = _causal_window_sums(ext)
        _, counts = _window_counts(i * ts, ts)
        for g in range(len(POOL_WINDOWS)):
            lanes = slice(g * GROUP_DIM, (g + 1) * GROUP_DIM)
            diff = sums[g][HALO:, :] / counts[g] - zb[:, lanes]
            lin = _dot(diff.astype(BF16), wp_ref[g].astype(BF16)) + bp_ref[:, lanes]
            y_ref[:, D_A + g * GROUP_DIM:D_A + (g + 1) * GROUP_DIM] = (lin * ps_ref[:, lanes]).astype(BF16)

    return pl.pallas_call(
        body, name="mixer_fwd", grid=(s_len // ts,),
        in_specs=[_rows(ts, D_Z),
                  pl.BlockSpec((HALO, D_B), lambda i: (jnp.maximum(i * nb - 1, 0), 2)),
                  _const((N_HEADS, CHUNK, CHUNK)), _const((CHUNK, D_A)), _const((1, D_A)), _const((1, D_A)),
                  _const((N_HEADS, GROUP_DIM, GROUP_DIM)), _const((1, D_B)), _const((1, D_B))],
        out_specs=_rows(ts, D_MODEL),
        out_shape=jax.ShapeDtypeStruct((s_len, D_MODEL), BF16),
        scratch_shapes=[pltpu.VMEM((ts, D_A), F32)],
        compiler_params=_params(),
    )(z, z, w_spatial, bsp_full, gain, bias, w_pool, b_pool, pool_scale)


def _fwd_out(ycat, x, mod6, n1post, n2pre, wout_g, ts):
    s_len = x.shape[0]
    rs = D_MODEL // N_CHIPS

    def body(y_ref, x_ref, mod_ref, g1_ref, g2_ref, w_ref, mix_ref, x1_ref, h2_ref):
        yb = y_ref[...]
        mix = _dot(yb[:, 0:rs], w_ref[0])
        for j in range(1, N_CHIPS):
            mix = mix + _dot(yb[:, j * rs:(j + 1) * rs], w_ref[j])
        mix_ref[...] = mix
        r2 = lax.rsqrt(_rowmean(mix * mix) + EPS)
        x1 = x_ref[...] + mod_ref[2:3, :] * ((mix * r2) * g1_ref[...])
        x1_ref[...] = x1
        r3 = lax.rsqrt(_rowmean(x1 * x1) + EPS)
        h2 = ((x1 * r3) * g2_ref[...]) * (1.0 + mod_ref[4:5, :]) + mod_ref[3:4, :]
        h2_ref[...] = h2.astype(BF16)

    return pl.pallas_call(
        body, name="fwd_out", grid=(s_len // ts,),
        in_specs=[_rows(ts, D_MODEL), _rows(ts, D_MODEL), _const((N_MOD, D_MODEL)), _const((1, D_MODEL)),
                  _const((1, D_MODEL)), _VMEM],
        out_specs=[_rows(ts, D_MODEL), _rows(ts, D_MODEL), _rows(ts, D_MODEL)],
        out_shape=[jax.ShapeDtypeStruct((s_len, D_MODEL), F32), jax.ShapeDtypeStruct((s_len, D_MODEL), F32),
                   jax.ShapeDtypeStruct((s_len, D_MODEL), BF16)],
        compiler_params=_params(),
    )(ycat, x, mod6, n1post, n2pre, wout_g)


def _fwd_fc1(h2, fc1_g, ts):
    s_len = h2.shape[0]
    cs = D_FF // N_CHIPS

    def body(h_ref, w_ref, q_ref):
        hb = h_ref[...]
        for j in range(N_CHIPS):
            p = jnp.maximum(_dot(hb, w_ref[j]), 0.0)
            q_ref[:, j * cs:(j + 1) * cs] = (p * p).astype(BF16)

    return pl.pallas_call(
        body, name="fwd_fc1", grid=(s_len // ts,),
        in_specs=[_rows(ts, D_MODEL), _VMEM],
        out_specs=_rows(ts, D_FF),
        out_shape=jax.ShapeDtypeStruct((s_len, D_FF), BF16),
        compiler_params=_params(),
    )(h2, fc1_g)


def _fwd_fc2_loss(q, x1, target, mod6, n2post, fc2_g, ts):
    s_len = q.shape[0]
    rs = D_FF // N_CHIPS

    def body(q_ref, x1_ref, t_ref, mod_ref, g_ref, w_ref, dy_ref, df_ref, loss_ref, dgate_ref, dg_ref):
        i = pl.program_id(0)

        @pl.when(i == 0)
        def _():
            loss_ref[...] = jnp.zeros_like(loss_ref)
            dgate_ref[...] = jnp.zeros_like(dgate_ref)
            dg_ref[...] = jnp.zeros_like(dg_ref)

        f = _dot(q_ref[:, 0:rs], w_ref[0])
        for j in range(1, N_CHIPS):
            f = f + _dot(q_ref[:, j * rs:(j + 1) * rs], w_ref[j])
        r4 = lax.rsqrt(_rowmean(f * f) + EPS)
        fh = f * r4
        gate = mod_ref[5:6, :]
        gn = g_ref[...]
        err = (x1_ref[...] + gate * (fh * gn)) - t_ref[...]
        loss_ref[...] += 0.5 * jnp.sum(_rowmean(err * err), axis=0, keepdims=True)
        dy = err * (1.0 / D_MODEL)
        dy_ref[...] = dy
        dgate_ref[...] += _colsum(dy * (fh * gn))
        dg_ref[...] += _colsum((dy * gate) * fh)
        gh = (dy * gate) * gn
        df_ref[...] = (r4 * (gh - fh * _rowmean(gh * fh))).astype(BF16)

    return pl.pallas_call(
        body, name="fwd_fc2_loss", grid=(s_len // ts,),
        in_specs=[_rows(ts, D_FF), _rows(ts, D_MODEL), _rows(ts, D_MODEL), _const((N_MOD, D_MODEL)),
                  _const((1, D_MODEL)), _VMEM],
        out_specs=[_rows(ts, D_MODEL), _rows(ts, D_MODEL), _const((1, 1)), _const((1, D_MODEL)), _const((1, D_MODEL))],
        out_shape=[jax.ShapeDtypeStruct((s_len, D_MODEL), F32), jax.ShapeDtypeStruct((s_len, D_MODEL), BF16),
                   jax.ShapeDtypeStruct((1, 1), F32), jax.ShapeDtypeStruct((1, D_MODEL), F32),
                   jax.ShapeDtypeStruct((1, D_MODEL), F32)],
        compiler_params=_params(),
    )(q, x1, target, mod6, n2post, fc2_g)


def _bwd_fc2(df, q, fc2_g, ts):
    s_len = df.shape[0]
    cs = D_FF // N_CHIPS

    def body(df_ref, q_ref, w_ref, dp_ref):
        dfb = df_ref[...]
        for j in range(N_CHIPS):
            dq = _dot_nt(dfb, w_ref[j])
            p = jnp.sqrt(q_ref[:, j * cs:(j + 1) * cs].astype(F32))
            dp_ref[:, j * cs:(j + 1) * cs] = (dq * (2.0 * p)).astype(BF16)

    return pl.pallas_call(
        body, name="bwd_fc2", grid=(s_len // ts,),
        in_specs=[_rows(ts, D_MODEL), _rows(ts, D_FF), _VMEM],
        out_specs=_rows(ts, D_FF),
        out_shape=jax.ShapeDtypeStruct((s_len, D_FF), BF16),
        compiler_params=_params(),
    )(df, q, fc2_g)


def _bwd_fc1_out(dp, dy, x1, mix, mod6, n2pre, n1post, fc1_g, wout_g, dep, ts):
    s_len = dp.shape[0]
    cs = D_FF // N_CHIPS
    rs = D_MODEL // N_CHIPS

    def body(dp_ref, dy_ref, x1_ref, mix_ref, mod_ref, g2_ref, g1_ref, w1_ref, wo_ref, dep_ref,
             dx1_ref, dmix_ref, dyc_ref, dshift2_ref, da2_ref, dgate1_ref, dg1_ref):
        i = pl.program_id(0)

        @pl.when(i == 0)
        def _():
            for ref in (dshift2_ref, da2_ref, dgate1_ref, dg1_ref):
                ref[...] = jnp.zeros_like(ref)

        dh2 = _dot_nt(dp_ref[:, 0:cs], w1_ref[0])
        for j in range(1, N_CHIPS):
            dh2 = dh2 + _dot_nt(dp_ref[:, j * cs:(j + 1) * cs], w1_ref[j])
        x1 = x1_ref[...]
        r3 = lax.rsqrt(_rowmean(x1 * x1) + EPS)
        xh = x1 * r3
        a2 = g2_ref[...] * (1.0 + mod_ref[4:5, :])
        dshift2_ref[...] += _colsum(dh2)
        da2_ref[...] += _colsum(dh2 * xh)
        dxh = dh2 * a2
        dx1 = dy_ref[...] + r3 * (dxh - xh * _rowmean(dxh * xh))
        dx1_ref[...] = dx1

        mix = mix_ref[...]
        r2 = lax.rsqrt(_rowmean(mix * mix) + EPS)
        mh = mix * r2
        gate = mod_ref[2:3, :]
        gn = g1_ref[...]
        dgate1_ref[...] += _colsum(dx1 * (mh * gn))
        dg1_ref[...] += _colsum((dx1 * gate) * mh)
        gh = (dx1 * gate) * gn
        dmix = (r2 * (gh - mh * _rowmean(gh * mh))).astype(BF16)
        dmix_ref[...] = dmix
        for j in range(N_CHIPS):
            dyc_ref[:, j * rs:(j + 1) * rs] = _dot_nt(dmix, wo_ref[j])

    vec = jax.ShapeDtypeStruct((1, D_MODEL), F32)
    return pl.pallas_call(
        body, name="bwd_fc1_out", grid=(s_len // ts,),
        in_specs=[_rows(ts, D_FF), _rows(ts, D_MODEL), _rows(ts, D_MODEL), _rows(ts, D_MODEL),
                  _const((N_MOD, D_MODEL)), _const((1, D_MODEL)), _const((1, D_MODEL)), _VMEM, _VMEM, _ANY],
        out_specs=[_rows(ts, D_MODEL), _rows(ts, D_MODEL), _rows(ts, D_MODEL)] + [_const((1, D_MODEL))] * 4,
        out_shape=[jax.ShapeDtypeStruct((s_len, D_MODEL), F32), jax.ShapeDtypeStruct((s_len, D_MODEL), BF16),
                   jax.ShapeDtypeStruct((s_len, D_MODEL), F32), vec, vec, vec, vec],
        compiler_params=_params(),
    )(dp, dy, x1, mix, mod6, n2pre, n1post, fc1_g, wout_g, dep)


def _mixer_bwd(z, dyc, w_spatial, bsp_full, gain, bias, w_pool, b_pool, pool_scale, ts):
    s_len = z.shape[0]
    nb = ts // HALO
    last = s_len // HALO - 1
    te = ts + HALO

    def body(z_ref, zprev_ref, znext_ref, dyc_ref, dynext_ref, ws_ref, bsp_ref, gain_ref, bias_ref, wp_ref, bp_ref,
             ps_ref, dz_ref, dws_ref, dbsp_ref, dgain_ref, dbias_ref, dwp_ref, dbp_ref, dps_ref, mixed_ref, dvn_ref):
        i = pl.program_id(0)

        @pl.when(i == 0)
        def _():
            for ref in (dws_ref, dbsp_ref, dgain_ref, dbias_ref, dwp_ref, dbp_ref, dps_ref):
                ref[...] = jnp.zeros_like(ref)

        wc = _tril_weights(ws_ref)
        gain = gain_ref[...]
        u, vhat, rstd, vn, dga = _mixer_forward_tile(z_ref[:, :2 * D_A], wc, bsp_ref, gain, bias_ref[...], mixed_ref)
        dya = dyc_ref[:, :D_A]
        du = dya * mixed_ref[...]
        dmixed = dya * u
        dmb = dmixed.astype(BF16)
        dm_sum = dmixed[0:CHUNK, :]
        for k in range(1, ts // CHUNK):
            dm_sum = dm_sum + dmixed[k * CHUNK:(k + 1) * CHUNK, :]
        r_idx = lax.broadcasted_iota(jnp.int32, (CHUNK, CHUNK), 0)
        s_idx = lax.broadcasted_iota(jnp.int32, (CHUNK, CHUNK), 1)
        causal = (s_idx <= r_idx).astype(F32)
        for h in range(N_HEADS):
            lanes = slice(h * HEAD_DIM, (h + 1) * HEAD_DIM)
            dbsp_ref[h] += jnp.sum(dm_sum[:, lanes], axis=1, keepdims=True)
            acc = None
            for k in range(ts // CHUNK):
                rows = slice(k * CHUNK, (k + 1) * CHUNK)
                t = _dot_nt(dmb[rows, lanes], vn[rows, lanes])
                acc = t if acc is None else acc + t
                dvn_ref[rows, lanes] = _dot_tn(wc[h], dmb[rows, lanes])
            dws_ref[h] += acc * causal
        dvn = dvn_ref[...]
        dgain_ref[...] += _colsum(dvn * vhat)
        dbias_ref[...] += _colsum(dvn)
        dvh = dvn * gain
        dv = rstd * (dvh - _rowmean(dvh) - vhat * _rowmean(dvh * vhat))
        dz_ref[:, :D_A] = (du * dga[:, :D_A]).astype(BF16)
        dz_ref[:, D_A:2 * D_A] = (dv * dga[:, D_A:]).astype(BF16)

        zb = z_ref[:, 2 * D_A:]
        prev = jnp.where(i == 0, 0.0, zprev_ref[...])
        zb_ext = jnp.concatenate([zb, znext_ref[...]], axis=0)
        sums = _causal_window_sums(jnp.concatenate([prev, zb_ext], axis=0))
        pos, counts = _window_counts(i * ts, te)
        dyb_ext = jnp.concatenate([dyc_ref[:, D_A:], dynext_ref[...]], axis=0)
        dlin_ext = dyb_ext * ps_ref[...]
        dbp_ref[...] += _colsum(dlin_ext[:ts, :])
        scaled = []
        ddiffs = []
        lins = []
        for g in range(len(POOL_WINDOWS)):
            lanes = slice(g * GROUP_DIM, (g + 1) * GROUP_DIM)
            diff = (sums[g][HALO:, :] / counts[g] - zb_ext[:, lanes]).astype(BF16)
            wpb = wp_ref[g].astype(BF16)
            dlb = dlin_ext[:, lanes].astype(BF16)
            lins.append(_dot(diff[:ts, :], wpb) + bp_ref[:, lanes])
            dwp_ref[g] += _dot_tn(diff[:ts, :], dlb[:ts, :])
            dd = _dot_nt(dlb, wpb)
            ddiffs.append(dd)
            scaled.append(jnp.where(pos < float(s_len), dd / counts[g], 0.0))
        dps_ref[...] += _colsum(dyb_ext[:ts, :] * jnp.concatenate(lins, axis=1))
        back = _anticausal_window_sums(jnp.concatenate(scaled, axis=1))
        for g in range(len(POOL_WINDOWS)):
            dz_ref[:, 2 * D_A + g * GROUP_DIM:2 * D_A + (g + 1) * GROUP_DIM] = (
                back[g][:ts, :] - ddiffs[g][:ts, :]).astype(BF16)

    sq = jax.ShapeDtypeStruct((N_HEADS, CHUNK, CHUNK), F32)
    vec = jax.ShapeDtypeStruct((1, D_A), F32)
    return pl.pallas_call(
        body, name="mixer_bwd", grid=(s_len // ts,),
        in_specs=[_rows(ts, D_Z),
                  pl.BlockSpec((HALO, D_B), lambda i: (jnp.maximum(i * nb - 1, 0), 2)),
                  pl.BlockSpec((HALO, D_B), lambda i: (jnp.minimum((i + 1) * nb, last), 2)),
                  _rows(ts, D_MODEL),
                  pl.BlockSpec((HALO, D_B), lambda i: (jnp.minimum((i + 1) * nb, last), 1)),
                  _const((N_HEADS, CHUNK, CHUNK)), _const((CHUNK, D_A)), _const((1, D_A)), _const((1, D_A)),
                  _const((N_HEADS, GROUP_DIM, GROUP_DIM)), _const((1, D_B)), _const((1, D_B))],
        out_specs=[_rows(ts, D_Z), _const((N_HEADS, CHUNK, CHUNK)), _const((N_HEADS, CHUNK, 1)), _const((1, D_A)),
                   _const((1, D_A)), _const((N_HEADS, GROUP_DIM, GROUP_DIM)), _const((1, D_B)), _const((1, D_B))],
        out_shape=[jax.ShapeDtypeStruct((s_len, D_Z), BF16), sq, jax.ShapeDtypeStruct((N_HEADS, CHUNK, 1), F32), vec,
                   vec, sq, vec, vec],
        scratch_shapes=[pltpu.VMEM((ts, D_A), F32), pltpu.VMEM((ts, D_A), F32)],
        compiler_params=_params(),
    )(z, z, z, dyc, dyc, w_spatial, bsp_full, gain, bias, w_pool, b_pool, pool_scale)


def _bwd_in(dz, dx1, x, mod6, n1pre, win_g, dep, ts):
    s_len = x.shape[0]
    cs = D_Z // N_CHIPS

    def body(dz_ref, dx1_ref, x_ref, mod_ref, g_ref, w_ref, dep_ref, gx_ref, dshift_ref, da_ref):
        i = pl.program_id(0)

        @pl.when(i == 0)
        def _():
            dshift_ref[...] = jnp.zeros_like(dshift_ref)
            da_ref[...] = jnp.zeros_like(da_ref)

        dh = _dot_nt(dz_ref[:, 0:cs], w_ref[0])
        for j in range(1, N_CHIPS):
            dh = dh + _dot_nt(dz_ref[:, j * cs:(j + 1) * cs], w_ref[j])
        xv = x_ref[...]
        r = lax.rsqrt(_rowmean(xv * xv) + EPS)
        xh = xv * r
        a1 = g_ref[...] * (1.0 + mod_ref[1:2, :])
        dshift_ref[...] += _colsum(dh)
        da_ref[...] += _colsum(dh * xh)
        dxh = dh * a1
        gx_ref[...] = dx1_ref[...] + r * (dxh - xh * _rowmean(dxh * xh))

    vec = jax.ShapeDtypeStruct((1, D_MODEL), F32)
    return pl.pallas_call(
        body, name="bwd_in", grid=(s_len // ts,),
        in_specs=[_rows(ts, D_Z), _rows(ts, D_MODEL), _rows(ts, D_MODEL), _const((N_MOD, D_MODEL)),
                  _const((1, D_MODEL)), _VMEM, _ANY],
        out_specs=[_rows(ts, D_MODEL), _const((1, D_MODEL)), _const((1, D_MODEL))],
        out_shape=[jax.ShapeDtypeStruct((s_len, D_MODEL), F32), vec, vec],
        compiler_params=_params(),
    )(dz, dx1, x, mod6, n1pre, win_g, dep)


def _wgrad(a, b, name, col_sharded, tk):
    s_len, m = a.shape
    n = b.shape[1]
    if col_sharded:
        tm, tn = m, n // N_CHIPS
        a_spec = pl.BlockSpec((tk, tm), lambda j, k: (k, 0))
        b_spec = pl.BlockSpec((tk, tn), lambda j, k: (k, j))
    else:
        tm, tn = m // N_CHIPS, n
        a_spec = pl.BlockSpec((tk, tm), lambda j, k: (k, j))
        b_spec = pl.BlockSpec((tk, tn), lambda j, k: (k, 0))
    half = tm // 2
    nk = s_len // tk

    def body(a_ref, b_ref, o_ref, ob_ref, acc_ref):
        k = pl.program_id(1)

        @pl.when(k == 0)
        def _():
            acc_ref[...] = jnp.zeros_like(acc_ref)

        acc_ref[...] += _dot_tn(a_ref[...], b_ref[...])

        @pl.when(k == nk - 1)
        def _():
            for h in range(2):
                blk = acc_ref[h * half:(h + 1) * half, :]
                o_ref[0, h] = blk
                ob_ref[0, h] = blk.astype(BF16)

    o_spec = pl.BlockSpec((1, 2, half, tn), lambda j, k: (j, 0, 0, 0))
    return pl.pallas_call(
        body, name=name, grid=(N_CHIPS, nk),
        in_specs=[a_spec, b_spec], out_specs=[o_spec, o_spec],
        out_shape=[jax.ShapeDtypeStruct((N_CHIPS, 2, half, tn), F32), jax.ShapeDtypeStruct((N_CHIPS, 2, half, tn), BF16)],
        scratch_shapes=[pltpu.VMEM((tm, tn), F32)],
        compiler_params=_params(2),
    )(a, b)


def _adamw_math(w, g, m, v):
    m = ADAM_B1 * m + (1.0 - ADAM_B1) * g
    v = ADAM_B2 * v + (1.0 - ADAM_B2) * (g * g)
    m_hat = m / (1.0 - ADAM_B1 ** ADAM_STEP)
    v_hat = v / (1.0 - ADAM_B2 ** ADAM_STEP)
    delta = -ADAM_LR * (m_hat / (jnp.sqrt(v_hat) + ADAM_EPS) + ADAM_WD * w)
    return delta, m, v


def _adamw(g, w, m, v, name, tr):
    rows, cols = w.shape

    def body(g_ref, w_ref, m_ref, v_ref, d_ref, nm_ref, nv_ref):
        d, nm, nv = _adamw_math(w_ref[...], g_ref[...], m_ref[...], v_ref[...])
        d_ref[...] = d
        nm_ref[...] = nm
        nv_ref[...] = nv

    spec = _rows(tr, cols)
    shape = jax.ShapeDtypeStruct((rows, cols), F32)
    return pl.pallas_call(
        body, name=name, grid=(rows // tr,), in_specs=[spec] * 4, out_specs=[spec] * 3, out_shape=[shape] * 3,
        compiler_params=_params(),
    )(g, w, m, v)


def _ada_grad_adamw(sc_t, dmod_shard, w, m, v, tr):
    rows, cols = w.shape

    def body(s_ref, dm_ref, w_ref, m_ref, v_ref, g_ref, d_ref, nm_ref, nv_ref):
        g = s_ref[:, 0:1] * dm_ref[0:1, :]
        for b in range(1, N_DEV):
            g = g + s_ref[:, b:b + 1] * dm_ref[b:b + 1, :]
        g_ref[...] = g
        d, nm, nv = _adamw_math(w_ref[...], g, m_ref[...], v_ref[...])
        d_ref[...] = d
        nm_ref[...] = nm
        nv_ref[...] = nv

    spec = _rows(tr, cols)
    shape = jax.ShapeDtypeStruct((rows, cols), F32)
    return pl.pallas_call(
        body, name="ada_grad_adamw", grid=(rows // tr,),
        in_specs=[_rows(tr, N_DEV), _const((N_DEV, cols)), spec, spec, spec],
        out_specs=[spec] * 4, out_shape=[shape] * 4, compiler_params=_params(),
    )(sc_t, dmod_shard, w, m, v)


def _mod_grads(da1, dshift1, dgate1, dg1post, da2, dshift2, dgate2, dg2post, mod6, n1pre, n2pre):
    def body(da1_ref, ds1_ref, dgt1_ref, dg1_ref, da2_ref, ds2_ref, dgt2_ref, dg2_ref, mod_ref, n1_ref, n2_ref,
             dmod_ref, dn_ref):
        dmod_ref[0:1, :] = ds1_ref[...]
        dmod_ref[1:2, :] = da1_ref[...] * n1_ref[...]
        dmod_ref[2:3, :] = dgt1_ref[...]
        dmod_ref[3:4, :] = ds2_ref[...]
        dmod_ref[4:5, :] = da2_ref[...] * n2_ref[...]
        dmod_ref[5:6, :] = dgt2_ref[...]
        dn_ref[0:1, :] = da1_ref[...] * (1.0 + mod_ref[1:2, :])
        dn_ref[1:2, :] = dg1_ref[...]
        dn_ref[2:3, :] = da2_ref[...] * (1.0 + mod_ref[4:5, :])
        dn_ref[3:4, :] = dg2_ref[...]

    return pl.pallas_call(
        body, name="mod_grads",
        out_shape=[jax.ShapeDtypeStruct((N_MOD, D_MODEL), F32), jax.ShapeDtypeStruct((4, D_MODEL), F32)],
    )(da1, dshift1, dgate1, dg1post, da2, dshift2, dgate2, dg2post, mod6, n1pre, n2pre)


def _position():
    x, y, c = lax.axis_index("x"), lax.axis_index("y"), lax.axis_index("c")
    return x, y, c


def _flip(v, bit):
    return 1 - v if bit else v


def _peer(x, y, c, k):
    return (_flip(x, k & 4), _flip(y, k & 2), _flip(c, k & 1))


def _remote(src, dst, send_sem, recv_sem, device):
    return pltpu.make_async_remote_copy(src_ref=src, dst_ref=dst, send_sem=send_sem, recv_sem=recv_sem,
                                        device_id=device, device_id_type=MESH)


def _cast_to_slot(w, pos, name, tr):
    rows, cols = w.shape

    def body(pos_ref, w_ref, o_ref):
        o_ref[0] = w_ref[...].astype(BF16)

    return pl.pallas_call(
        body, name=name,
        grid_spec=pltpu.PrefetchScalarGridSpec(
            num_scalar_prefetch=1, grid=(rows // tr,),
            in_specs=[pl.BlockSpec((tr, cols), lambda i, pos: (i, 0))],
            out_specs=pl.BlockSpec((1, tr, cols), lambda i, pos: (pos[1], i, 0))),
        out_shape=jax.ShapeDtypeStruct((N_CHIPS, rows, cols), BF16), compiler_params=_params(),
    )(pos, w)


def _mod_exchange(c_row, w_ada_shard, b_ada_row):
    cs = w_ada_shard.shape[1]

    def body(c_ref, w_ref, b_ref, mod_ref, sc_ref, rows_ref, send1, recv1, send2, recv2):
        x, y, c = _position()
        me = 4 * x + 2 * y + c
        chip = 2 * x + y
        cv = c_ref[...]
        sc_ref[me] = cv * jax.nn.sigmoid(cv)
        gather = [_remote(sc_ref.at[me], sc_ref.at[me], send1.at[k - 1], recv1.at[k - 1], _peer(x, y, c, k))
                  for k in range(1, N_DEV)]
        for cp in gather:
            cp.start()
        for k in range(1, N_DEV):
            px, py, pc = _peer(x, y, c, k)
            src = 4 * px + 2 * py + pc
            _remote(sc_ref.at[src], sc_ref.at[src], send1.at[k - 1], recv1.at[k - 1], (px, py, pc)).wait_recv()
        for cp in gather:
            cp.wait_send()
        sc_all = jnp.concatenate([sc_ref[b] for b in range(N_DEV)], axis=0)
        part = jnp.dot(sc_all, w_ref[...], precision=lax.Precision.HIGHEST, preferred_element_type=F32)
        part = part + b_ref[:, pl.ds(pl.multiple_of(chip * cs, 128), cs)]
        for b in range(N_DEV):
            rows_ref[b] = part[b:b + 1, :]
        mod_ref[chip] = rows_ref[me]
        hand = []
        for k in (2, 4, 6):
            px, py, _ = _peer(x, y, c, k)
            hand.append(_remote(rows_ref.at[4 * px + 2 * py + c], mod_ref.at[chip], send2.at[k // 2 - 1],
                                recv2.at[k // 2 - 1], (px, py, c)))
        for cp in hand:
            cp.start()
        for k in (2, 4, 6):
            px, py, _ = _peer(x, y, c, k)
            pchip = 2 * px + py
            _remote(rows_ref.at[me], mod_ref.at[pchip], send2.at[k // 2 - 1], recv2.at[k // 2 - 1],
                    (px, py, c)).wait_recv()
        for cp in hand:
            cp.wait_send()

    return pl.pallas_call(
        body, name="mod_exchange",
        in_specs=[_VMEM, _VMEM, _VMEM], out_specs=[_VMEM, _VMEM],
        out_shape=[jax.ShapeDtypeStruct((N_CHIPS, 1, cs), F32), jax.ShapeDtypeStruct((N_DEV, 1, D_MODEL), F32)],
        scratch_shapes=[pltpu.VMEM((N_DEV, 1, cs), F32), pltpu.SemaphoreType.DMA((N_DEV - 1,)),
                        pltpu.SemaphoreType.DMA((N_DEV - 1,)), pltpu.SemaphoreType.DMA((N_CHIPS - 1,)),
                        pltpu.SemaphoreType.DMA((N_CHIPS - 1,))],
        compiler_params=pltpu.CompilerParams(vmem_limit_bytes=VMEM_LIMIT),
    )(c_row, w_ada_shard, b_ada_row)


_HBM = pl.BlockSpec(memory_space=pltpu.HBM)
_SEM = pl.BlockSpec(memory_space=pltpu.SEMAPHORE)
_EFFECT = pltpu.SideEffectType.DATAFLOW_SIDE_EFFECTING
_CHIP_HOPS = (2, 4, 6)


def _in_hbm(a):
    return pltpu.with_memory_space_constraint(a, pltpu.HBM)


def _sems3():
    return pltpu.SemaphoreType.DMA((len(_CHIP_HOPS),))


def _ag_start(lands, after):
    n = len(lands)

    def body(*refs):
        zones = refs[:n]
        sends, recvs = refs[n + 1:2 * n + 1], refs[2 * n + 1:3 * n + 1]
        x, y, c = _position()
        chip = 2 * x + y
        for i in range(n):
            half = zones[i].shape[1] // 2
            mine = zones[i].at[chip, pl.ds(c * half, half)]
            for s, k in enumerate(_CHIP_HOPS):
                px, py, _ = _peer(x, y, c, k)
                _remote(mine, mine, sends[i].at[s], recvs[i].at[s], (px, py, c)).start()

    out = pl.pallas_call(
        body, name="ag_start",
        in_specs=[_HBM] * n + [_ANY],
        out_specs=[_SEM] * (2 * n) + [_HBM] * n,
        out_shape=[_sems3()] * (2 * n) + [pltpu.HBM(z.shape, BF16) for z in lands],
        input_output_aliases={i: 2 * n + i for i in range(n)},
        compiler_params=pltpu.CompilerParams(has_side_effects=_EFFECT),
    )(*[_in_hbm(z) for z in lands], after)
    return [(out[2 * n + i], out[i], out[n + i]) for i in range(n)]


def _ag_pass(group, after, name):
    n = len(group)

    def body(*refs):
        zones = refs[:n]
        sends, recvs = refs[n:2 * n], refs[2 * n:3 * n]
        fsends, frecvs = refs[4 * n + 1:5 * n + 1], refs[5 * n + 1:6 * n + 1]
        x, y, c = _position()
        chip = 2 * x + y
        for i in range(n):
            half = zones[i].shape[1] // 2
            rows = pl.ds(c * half, half)
            for s, k in enumerate(_CHIP_HOPS):
                px, py, _ = _peer(x, y, c, k)
                landed = zones[i].at[2 * px + py, rows]
                _remote(landed, landed, sends[i].at[s], recvs[i].at[s], (px, py, c)).wait_recv()
                _remote(landed, landed, fsends[i].at[s], frecvs[i].at[s], (x, y, 1 - c)).start()
        for i in range(n):
            half = zones[i].shape[1] // 2
            mine = zones[i].at[chip, pl.ds(c * half, half)]
            for s, k in enumerate(_CHIP_HOPS):
                px, py, _ = _peer(x, y, c, k)
                _remote(mine, mine, sends[i].at[s], recvs[i].at[s], (px, py, c)).wait_send()

    out = pl.pallas_call(
        body, name=name,
        in_specs=[_HBM] * n + [_SEM] * (2 * n) + [_ANY],
        out_specs=[_HBM] * n + [_SEM] * (2 * n),
        out_shape=[pltpu.HBM(g[0].shape, BF16) for g in group] + [_sems3()] * (2 * n),
        input_output_aliases={i: i for i in range(n)},
        compiler_params=pltpu.CompilerParams(has_side_effects=_EFFECT),
    )(*[g[0] for g in group], *[g[1] for g in group], *[g[2] for g in group], after)
    return [(out[i], out[n + i], out[2 * n + i]) for i in range(n)]


def _ag_done(group, name):
    n = len(group)

    def body(*refs):
        lands = refs[:n]
        fsends, frecvs = refs[n:2 * n], refs[2 * n:3 * n]
        x, y, c = _position()
        for i in range(n):
            half = lands[i].shape[1] // 2
            for s, k in enumerate(_CHIP_HOPS):
                px, py, _ = _peer(x, y, c, k)
                sent = lands[i].at[2 * px + py, pl.ds(c * half, half)]
                got = lands[i].at[2 * px + py, pl.ds((1 - c) * half, half)]
                cp = _remote(sent, got, fsends[i].at[s], frecvs[i].at[s], (x, y, 1 - c))
                cp.wait_recv()
                cp.wait_send()

    out = pl.pallas_call(
        body, name=name,
        in_specs=[_HBM] * n + [_SEM] * (2 * n),
        out_specs=[_HBM] * n,
        out_shape=[pltpu.HBM(g[0].shape, BF16) for g in group],
        input_output_aliases={i: i for i in range(n)},
        compiler_params=pltpu.CompilerParams(has_side_effects=_EFFECT),
    )(*[g[0] for g in group], *[g[1] for g in group], *[g[2] for g in group])
    return list(out)


def _all_reduce_small(v):
    rows = v.shape[0]

    def body(v_ref, out_ref, sib_ref, slots_ref, send_a, recv_a, send_b, recv_b):
        x, y, c = _position()
        chip = 2 * x + y
        pair = _remote(v_ref, sib_ref, send_a, recv_a, (x, y, 1 - c))
        pair.start()
        pair.wait_recv()
        slots_ref[chip] = v_ref[...] + sib_ref[...]
        spread = []
        for k in (2, 4, 6):
            px, py, _ = _peer(x, y, c, k)
            spread.append(_remote(slots_ref.at[chip], slots_ref.at[chip], send_b.at[k // 2 - 1], recv_b.at[k // 2 - 1],
                                  (px, py, c)))
        for cp in spread:
            cp.start()
        for k in (2, 4, 6):
            px, py, _ = _peer(x, y, c, k)
            landed = slots_ref.at[2 * px + py]
            _remote(landed, landed, send_b.at[k // 2 - 1], recv_b.at[k // 2 - 1], (px, py, c)).wait_recv()
        out_ref[...] = ((slots_ref[0] + slots_ref[1]) + slots_ref[2]) + slots_ref[3]
        pair.wait_send()
        for cp in spread:
            cp.wait_send()

    return pl.pallas_call(
        body, name="all_reduce_small",
        in_specs=[_VMEM], out_specs=_VMEM, out_shape=jax.ShapeDtypeStruct((rows, 128), F32),
        scratch_shapes=[pltpu.VMEM((rows, 128), F32), pltpu.VMEM((N_CHIPS, rows, 128), F32),
                        pltpu.SemaphoreType.DMA, pltpu.SemaphoreType.DMA,
                        pltpu.SemaphoreType.DMA((N_CHIPS - 1,)), pltpu.SemaphoreType.DMA((N_CHIPS - 1,))],
        compiler_params=pltpu.CompilerParams(vmem_limit_bytes=VMEM_LIMIT),
    )(v)


def _sibling_exchange(pgbs, name):
    n = len(pgbs)

    def body(*refs):
        ins, outs = refs[:n], refs[n:2 * n]
        send, recv = refs[2 * n], refs[2 * n + 1]
        x, y, c = _position()
        copies = [_remote(ins[i].at[j, 1 - c], outs[i].at[j], send.at[N_CHIPS * i + j], recv.at[N_CHIPS * i + j],
                          (x, y, 1 - c)) for i in range(n) for j in range(N_CHIPS)]
        for cp in copies:
            cp.start()
        for cp in copies:
            cp.wait()

    return pl.pallas_call(
        body, name=name, in_specs=[_ANY] * n, out_specs=[_ANY] * n,
        out_shape=[jax.ShapeDtypeStruct((N_CHIPS,) + p.shape[2:], BF16) for p in pgbs],
        scratch_shapes=[pltpu.SemaphoreType.DMA((N_CHIPS * n,)), pltpu.SemaphoreType.DMA((N_CHIPS * n,))],
    )(*pgbs)


def _sibling_add(pg, from_sib, pos, name):
    _, _, r, cdim = pg.shape

    def body(pos_ref, pg_ref, fs_ref, qb_ref, own_ref):
        j = pl.program_id(0)
        q = pg_ref[0, 0] + fs_ref[0].astype(F32)
        qb_ref[0] = q.astype(BF16)

        @pl.when(j == pos_ref[1])
        def _():
            own_ref[...] = q

    return pl.pallas_call(
        body, name=name,
        grid_spec=pltpu.PrefetchScalarGridSpec(
            num_scalar_prefetch=1, grid=(N_CHIPS,),
            in_specs=[pl.BlockSpec((1, 1, r, cdim), lambda j, pos: (j, pos[0], 0, 0)),
                      pl.BlockSpec((1, r, cdim), lambda j, pos: (j, 0, 0))],
            out_specs=[pl.BlockSpec((1, r, cdim), lambda j, pos: (j, 0, 0)),
                       pl.BlockSpec((r, cdim), lambda j, pos: (0, 0))]),
        out_shape=[jax.ShapeDtypeStruct((N_CHIPS, r, cdim), BF16), jax.ShapeDtypeStruct((r, cdim), F32)],
        compiler_params=_params(),
    )(pos, pg, from_sib)


def _rs_start(qbs, name):
    n = len(qbs)

    def body(*refs):
        outs, inboxes = refs[:n], refs[n:2 * n]
        sends, recvs = refs[2 * n:3 * n], refs[3 * n:4 * n]
        x, y, c = _position()
        chip = 2 * x + y
        for i in range(n):
            for s, k in enumerate(_CHIP_HOPS):
                px, py, _ = _peer(x, y, c, k)
                _remote(outs[i].at[2 * px + py], inboxes[i].at[chip], sends[i].at[s], recvs[i].at[s], (px, py, c)).start()

    inboxes = [_in_hbm(lax.empty(q.shape, BF16)) for q in qbs]
    out = pl.pallas_call(
        body, name=name,
        in_specs=[_HBM] * (2 * n),
        out_specs=[_SEM] * (2 * n) + [_HBM] * (2 * n),
        out_shape=[_sems3()] * (2 * n) + [pltpu.HBM(q.shape, BF16) for q in qbs] * 2,
        input_output_aliases={i: 2 * n + i for i in range(2 * n)},
        compiler_params=pltpu.CompilerParams(has_side_effects=_EFFECT),
    )(*[_in_hbm(q) for q in qbs], *inboxes)
    return [(out[2 * n + i], out[3 * n + i], out[i], out[n + i]) for i in range(n)]


def _rs_wait(group, after, name):
    n = len(group)

    def body(*refs):
        outs, inboxes = refs[:n], refs[n:2 * n]
        sends, recvs = refs[2 * n:3 * n], refs[3 * n:4 * n]
        x, y, c = _position()
        for i in range(n):
            for s, k in enumerate(_CHIP_HOPS):
                px, py, _ = _peer(x, y, c, k)
                slot = 2 * px + py
                cp = _remote(outs[i].at[slot], inboxes[i].at[slot], sends[i].at[s], recvs[i].at[s], (px, py, c))
                cp.wait_recv()
                cp.wait_send()

    out = pl.pallas_call(
        body, name=name,
        in_specs=[_HBM] * (2 * n) + [_SEM] * (2 * n) + [_ANY],
        out_specs=[_HBM] * n,
        out_shape=[pltpu.HBM(g[1].shape, BF16) for g in group],
        input_output_aliases={n + i: i for i in range(n)},
        compiler_params=pltpu.CompilerParams(has_side_effects=_EFFECT),
    )(*[g[0] for g in group], *[g[1] for g in group], *[g[2] for g in group], *[g[3] for g in group], after)
    return list(out)


def _final_add(inbox, own, pos, name, tr):
    _, r, cdim = inbox.shape

    def body(pos_ref, a_ref, b_ref, c_ref, own_ref, o_ref):
        total = (a_ref[0].astype(F32) + b_ref[0].astype(F32)) + c_ref[0].astype(F32)
        o_ref[0] = total + own_ref[...]

    def slot(flip):
        return pl.BlockSpec((1, tr, cdim), lambda i, pos: (jnp.bitwise_xor(pos[1], flip), i, 0))

    return pl.pallas_call(
        body, name=name,
        grid_spec=pltpu.PrefetchScalarGridSpec(
            num_scalar_prefetch=1, grid=(r // tr,),
            in_specs=[slot(1), slot(2), slot(3), pl.BlockSpec((tr, cdim), lambda i, pos: (i, 0))],
            out_specs=pl.BlockSpec((1, tr, cdim), lambda i, pos: (pos[0], i, 0))),
        out_shape=jax.ShapeDtypeStruct((2, r, cdim), F32), compiler_params=_params(),
    )(pos, inbox, inbox, inbox, own)


def _sibling_share(shards, name):
    n = len(shards)

    def body(*refs):
        outs = refs[n:2 * n]
        send, recv = refs[2 * n:]
        x, y, c = _position()
        copies = [_remote(outs[i].at[c], outs[i].at[c], send.at[i], recv.at[i], (x, y, 1 - c)) for i in range(n)]
        for cp in copies:
            cp.start()
        for i in range(n):
            theirs = outs[i].at[1 - c]
            _remote(theirs, theirs, send.at[i], recv.at[i], (x, y, 1 - c)).wait_recv()
        for cp in copies:
            cp.wait_send()

    return pl.pallas_call(
        body, name=name, in_specs=[_ANY] * n, out_specs=[_ANY] * n,
        out_shape=[jax.ShapeDtypeStruct(s.shape, F32) for s in shards],
        input_output_aliases={i: i for i in range(n)},
        scratch_shapes=[pltpu.SemaphoreType.DMA((n,))] * 2,
    )(*shards)


_SMALL = (("b_ada", N_MOD * D_MODEL), ("norm1_pre", D_MODEL), ("norm1_post", D_MODEL), ("norm2_pre", D_MODEL),
          ("norm2_post", D_MODEL), ("w_spatial", N_HEADS * CHUNK * CHUNK), ("b_spatial", N_HEADS * CHUNK),
          ("ln_v_gain", D_A), ("ln_v_bias", D_A), ("w_pool", N_HEADS * GROUP_DIM * GROUP_DIM),
          ("b_pool", D_B), ("pool_scale", D_B))
_MOD_ROWS = N_MOD * D_MODEL // 128


def _packed_rows(size):
    return -(-(size // 128) // 8) * 8


def _pack(parts):
    out = []
    for name, size in _SMALL:
        a = parts[name].reshape(size // 128, 128)
        pad = _packed_rows(size) - a.shape[0]
        out.append(jnp.pad(a, ((0, pad), (0, 0))) if pad else a)
    return out


def _unpack(packed, shapes):
    res = {}
    row = 0
    for name, size in _SMALL:
        res[name] = packed[row:row + size // 128].reshape(shapes[name])
        row += _packed_rows(size)
    return res


def _small_adamw(total, wp, mp, vp):
    rows = wp.shape[0]
    head = N_DEV * _MOD_ROWS

    def body(t_ref, w_ref, m_ref, v_ref, g_ref, d_ref, nm_ref, nv_ref):
        gb = t_ref[0:_MOD_ROWS, :]
        for b in range(1, N_DEV):
            gb = gb + t_ref[b * _MOD_ROWS:(b + 1) * _MOD_ROWS, :]
        g_ref[0:_MOD_ROWS, :] = gb
        g_ref[_MOD_ROWS:, :] = t_ref[head:, :]
        d, nm, nv = _adamw_math(w_ref[...], g_ref[...], m_ref[...], v_ref[...])
        d_ref[...] = d
        nm_ref[...] = nm
        nv_ref[...] = nv

    return pl.pallas_call(
        body, name="small_adamw", out_shape=[jax.ShapeDtypeStruct((rows, 128), F32)] * 4,
        compiler_params=pltpu.CompilerParams(vmem_limit_bytes=VMEM_LIMIT),
    )(total, wp, mp, vp)


def kernel(x, c, w_ada, b_ada, norm1_pre, norm1_post, w_in, w_spatial, b_spatial, ln_v_gain, ln_v_bias, w_pool, b_pool, pool_scale, w_out, norm2_pre, norm2_post, w_fc1, w_fc2, loss_target, m_w_ada, m_b_ada, m_norm1_pre, m_norm1_post, m_w_in, m_w_spatial, m_b_spatial, m_ln_v_gain, m_ln_v_bias, m_w_pool, m_b_pool, m_pool_scale, m_w_out, m_norm2_pre, m_norm2_post, m_w_fc1, m_w_fc2, v_w_ada, v_b_ada, v_norm1_pre, v_norm1_post, v_w_in, v_w_spatial, v_b_spatial, v_ln_v_gain, v_ln_v_bias, v_w_pool, v_b_pool, v_pool_scale, v_w_out, v_norm2_pre, v_norm2_post, v_w_fc1, v_w_fc2):
    weights = dict(w_ada=w_ada, b_ada=b_ada, norm1_pre=norm1_pre, norm1_post=norm1_post, w_in=w_in,
                   w_spatial=w_spatial, b_spatial=b_spatial, ln_v_gain=ln_v_gain, ln_v_bias=ln_v_bias, w_pool=w_pool,
                   b_pool=b_pool, pool_scale=pool_scale, w_out=w_out, norm2_pre=norm2_pre, norm2_post=norm2_post,
                   w_fc1=w_fc1, w_fc2=w_fc2)
    m_old = dict(w_ada=m_w_ada, b_ada=m_b_ada, norm1_pre=m_norm1_pre, norm1_post=m_norm1_post, w_in=m_w_in,
                 w_spatial=m_w_spatial, b_spatial=m_b_spatial, ln_v_gain=m_ln_v_gain, ln_v_bias=m_ln_v_bias,
                 w_pool=m_w_pool, b_pool=m_b_pool, pool_scale=m_pool_scale, w_out=m_w_out, norm2_pre=m_norm2_pre,
                 norm2_post=m_norm2_post, w_fc1=m_w_fc1, w_fc2=m_w_fc2)
    v_old = dict(w_ada=v_w_ada, b_ada=v_b_ada, norm1_pre=v_norm1_pre, norm1_post=v_norm1_post, w_in=v_w_in,
                 w_spatial=v_w_spatial, b_spatial=v_b_spatial, ln_v_gain=v_ln_v_gain, ln_v_bias=v_ln_v_bias,
                 w_pool=v_w_pool, b_pool=v_b_pool, pool_scale=v_pool_scale, w_out=v_w_out, norm2_pre=v_norm2_pre,
                 norm2_post=v_norm2_post, w_fc1=v_w_fc1, w_fc2=v_w_fc2)
    order = ("w_ada", "b_ada", "norm1_pre", "norm1_post", "w_in", "w_spatial", "b_spatial", "ln_v_gain", "ln_v_bias",
             "w_pool", "b_pool", "pool_scale", "w_out", "norm2_pre", "norm2_post", "w_fc1", "w_fc2")
    big = ("w_in", "w_out", "w_fc1", "w_fc2")
    mx, my, mc = _position()
    me = 4 * mx + 2 * my + mc
    chip = 2 * mx + my
    row = lambda a: a.reshape(1, -1)

    pos = jnp.stack([mc, chip]).astype(jnp.int32)
    xs, target = x[0], loss_target[0]
    n1pre, n1post, n2pre, n2post = row(norm1_pre), row(norm1_post), row(norm2_pre), row(norm2_post)
    mixer = (w_spatial, jnp.repeat(b_spatial.T, HEAD_DIM, axis=1), row(ln_v_gain), row(ln_v_bias), w_pool,
             row(b_pool), row(pool_scale))
    ts_big, ts_mid, tk = 512, 256, 1024

    mod4, sc_all = _mod_exchange(c, w_ada, row(b_ada))
    mod6 = mod4.reshape(N_MOD, D_MODEL)
    ag = _ag_start([_cast_to_slot(weights[n], pos, "cast_" + n, 256) for n in big], mod4)

    (win_g,) = _ag_done(_ag_pass([ag[0]], mod4, "ag_pass_in"), "ag_done_in")
    z, h1 = _fwd_in(xs, mod6, n1pre, win_g, ts_big)
    ycat = _mixer_fwd(z, *mixer, ts_mid)
    wout_g, fc1_g = _ag_done(_ag_pass([ag[1], ag[2]], ycat, "ag_pass_mid"), "ag_done_mid")
    mix, x1, h2 = _fwd_out(ycat, xs, mod6, n1post, n2pre, wout_g, ts_big)
    q = _fwd_fc1(h2, fc1_g, ts_big)
    (fc2_g,) = _ag_done(_ag_pass([ag[3]], q, "ag_pass_fc2"), "ag_done_fc2")
    dy, df, loss, dgate2, dg2post = _fwd_fc2_loss(q, x1, target, mod6, n2post, fc2_g, ts_big)
    loss = lax.psum(loss[0, 0], ("x", "y", "c"))

    def reduce_start(partials, tag):
        from_sib = _sibling_exchange([p[1] for p in partials], "sibling_exchange_" + tag)
        sums = [_sibling_add(p[0], fs, pos, "sibling_add_%s_%d" % (tag, i))
                for i, (p, fs) in enumerate(zip(partials, from_sib))]
        return _rs_start([s[0] for s in sums], "rs_start_" + tag), [s[1] for s in sums]

    def reduce_finish(state, owns, names, tag, dep):
        inboxes = _rs_wait(state, dep, "rs_wait_" + tag)
        halves = [_final_add(ib, own, pos, "final_add_" + n, min(256, own.shape[0]))
                  for ib, own, n in zip(inboxes, owns, names)]
        shards = _sibling_share(halves, "sibling_share_" + tag)
        for n, g in zip(names, shards):
            g = g.reshape(weights[n].shape)
            grads[n] = g
            deltas[n], new_m[n], new_v[n] = _adamw(g, weights[n], m_old[n], v_old[n], "adamw_" + n, 256)

    grads, deltas, new_m, new_v = {}, {}, {}, {}
    g_fc2 = _wgrad(q, df, "wgrad_fc2", False, tk)
    dp = _bwd_fc2(df, q, fc2_g, ts_big)
    g_fc1 = _wgrad(h2, dp, "wgrad_fc1", True, tk)
    state_fc, owns_fc = reduce_start([g_fc2, g_fc1], "fc")
    dx1, dmix, dyc, dshift2, da2, dgate1, dg1post = _bwd_fc1_out(dp, dy, x1, mix, mod6, n2pre, n1post, fc1_g, wout_g,
                                                                 state_fc[0][0], ts_mid)
    g_out = _wgrad(ycat, dmix, "wgrad_out", False, tk)
    dz, dws, dbsp, dgain, dbias, dwp, dbp, dps = _mixer_bwd(z, dyc, *mixer, ts_mid)
    g_in = _wgrad(h1, dz, "wgrad_in", True, tk)
    state_io, owns_io = reduce_start([g_out, g_in], "io")
    grad_x, dshift1, da1 = _bwd_in(dz, dx1, xs, mod6, n1pre, win_g, state_io[0][0], ts_mid)
    dmod6, dnorms = _mod_grads(da1, dshift1, dgate1, dg1post, da2, dshift2, dgate2, dg2post, mod6, n1pre, n2pre)

    parts = dict(b_ada=dmod6, norm1_pre=dnorms[0], norm1_post=dnorms[1], norm2_pre=dnorms[2], norm2_post=dnorms[3],
                 w_spatial=dws, b_spatial=dbsp, ln_v_gain=dgain, ln_v_bias=dbias, w_pool=dwp, b_pool=dbp,
                 pool_scale=dps)
    pieces = _pack(parts)
    slots = lax.dynamic_update_slice(jnp.zeros((N_DEV * _MOD_ROWS, 128), F32), pieces[0], (me * _MOD_ROWS, 0))
    total = _all_reduce_small(jnp.concatenate([slots] + pieces[1:], axis=0))
    reduce_finish(state_fc, owns_fc, ("w_fc2", "w_fc1"), "fc", total)
    reduce_finish(state_io, owns_io, ("w_out", "w_in"), "io", deltas["w_fc1"])
    gp, dp, nmp, nvp = _small_adamw(total, jnp.concatenate(_pack(weights), axis=0),
                                    jnp.concatenate(_pack(m_old), axis=0), jnp.concatenate(_pack(v_old), axis=0))
    shapes = {n: weights[n].shape for n, _ in _SMALL}
    for res, packed in ((grads, gp), (deltas, dp), (new_m, nmp), (new_v, nvp)):
        res.update(_unpack(packed, shapes))

    dmod_all = total[:N_DEV * _MOD_ROWS].reshape(N_DEV, N_MOD * D_MODEL)
    cs = w_ada.shape[1]
    dmod_shard = lax.dynamic_slice(dmod_all, (0, chip * cs), (N_DEV, cs))
    sc_t = sc_all.reshape(N_DEV, D_MODEL).T
    grads["w_ada"], deltas["w_ada"], new_m["w_ada"], new_v["w_ada"] = _ada_grad_adamw(
        sc_t, dmod_shard, w_ada, m_w_ada, v_w_ada, 256)

    return (loss, grad_x[None], *[grads[n] for n in order], *[deltas[n] for n in order],
            *[new_m[n] for n in order], *[new_v[n] for n in order])
```

```python
import functools

import jax
import jax.numpy as jnp
from jax import lax
from jax.experimental import pallas as pl
from jax.experimental.pallas import tpu as pltpu

F32 = jnp.float32
BF16 = jnp.bfloat16
MESH = pl.DeviceIdType.MESH

D_MODEL = 1024
D_A = 512
D_B = 512
D_Z = 2 * D_A + D_B
N_HEADS = 4
HEAD_DIM = 128
CHUNK = 128
POOL_WINDOWS = (2, 4, 8, 16)
GROUP_DIM = 128
D_FF = 4096
N_MOD = 6
EPS = 1e-6
HALO = 16
N_CHIPS = 4
N_DEV = 8

ADAM_LR = 0.001
ADAM_B1 = 0.9
ADAM_B2 = 0.999
ADAM_EPS = 1e-08
ADAM_WD = 0.01
ADAM_STEP = 10

VMEM_LIMIT = 56 * 1024 * 1024

_VMEM = pl.BlockSpec(memory_space=pltpu.VMEM)
_ANY = pl.BlockSpec(memory_space=pl.ANY)


def _params(n_grid_axes=1):
    return pltpu.CompilerParams(dimension_semantics=("arbitrary",) * n_grid_axes, vmem_limit_bytes=VMEM_LIMIT)


def _rows(ts, width):
    return pl.BlockSpec((ts, width), lambda i: (i, 0))


def _const(shape):
    return pl.BlockSpec(shape, lambda i: (0,) * len(shape))


def _dot(a, b):
    return jnp.dot(a, b, preferred_element_type=F32)


def _dot_nt(a, b):
    return lax.dot_general(a, b, (((1,), (1,)), ((), ())), preferred_element_type=F32)


def _dot_tn(a, b):
    return lax.dot_general(a, b, (((0,), (0,)), ((), ())), preferred_element_type=F32)


def _rowmean(v):
    return jnp.mean(v, axis=-1, keepdims=True)


def _colsum(v):
    return jnp.sum(v, axis=0, keepdims=True)


def _gelu_parts(z):
    k0 = 0.7978845608028654
    k1 = 0.044715
    z2 = z * z
    t = jnp.tanh(k0 * (z + k1 * z * z2))
    g = 0.5 * z * (1.0 + t)
    dg = 0.5 * (1.0 + t) + 0.5 * z * (1.0 - t * t) * (k0 * (1.0 + 3.0 * k1 * z2))
    return g, dg


def _tril_weights(ws_ref):
    r = lax.broadcasted_iota(jnp.int32, (CHUNK, CHUNK), 0)
    s = lax.broadcasted_iota(jnp.int32, (CHUNK, CHUNK), 1)
    mask = (s <= r).astype(F32)
    return [(ws_ref[h] * mask).astype(BF16) for h in range(N_HEADS)]


def _window_counts(first_row, n_rows):
    pos = (first_row + lax.broadcasted_iota(jnp.int32, (n_rows, 1), 0)).astype(F32)
    return pos, [jnp.minimum(pos + 1.0, float(w)) for w in POOL_WINDOWS]


def _causal_window_sums(ext):
    out = []
    e = ext
    shift = 1
    for g in range(len(POOL_WINDOWS)):
        e = e + pltpu.roll(e, shift, 0)
        shift *= 2
        out.append(e[:, g * GROUP_DIM:(g + 1) * GROUP_DIM])
    return out


def _anticausal_window_sums(ext):
    n = ext.shape[0]
    out = []
    e = ext
    shift = 1
    for g in range(len(POOL_WINDOWS)):
        e = e + pltpu.roll(e, n - shift, 0)
        shift *= 2
        out.append(e[:, g * GROUP_DIM:(g + 1) * GROUP_DIM])
    return out


def _fwd_mix(x, mod6, n1pre, n1post, n2pre, win_g, wout_g, w_spatial, bsp_full, gain, bias, w_pool, b_pool, pool_scale, ts):
    s_len = x.shape[0]
    cz = D_Z // N_CHIPS
    rs = D_MODEL // N_CHIPS

    def body(x_ref, mod_ref, g1pre_ref, g1post_ref, g2pre_ref, win_ref, wout_ref, ws_ref, bsp_ref, gain_ref,
             bias_ref, wp_ref, bp_ref, ps_ref, h1_ref, z_ref, y_ref, mix_ref, x1_ref, h2_ref, mixed_ref, prev_ref):
        i = pl.program_id(0)
        _zero_on_first_step(prev_ref)
        xv = x_ref[...]
        r = lax.rsqrt(_rowmean(xv * xv) + EPS)
        hb = (((xv * r) * g1pre_ref[...]) * (1.0 + mod_ref[1:2, :]) + mod_ref[0:1, :]).astype(BF16)
        h1_ref[...] = hb
        for j in range(N_CHIPS):
            z_ref[:, j * cz:(j + 1) * cz] = _dot(hb, win_ref[j])

        wc = _tril_weights(ws_ref)
        u, _, _, _, _ = _mixer_forward_tile(z_ref[:, :2 * D_A], wc, bsp_ref, gain_ref[...], bias_ref[...], mixed_ref)
        y_ref[:, :D_A] = (u * mixed_ref[...]).astype(BF16)
        zb = z_ref[:, 2 * D_A:]
        sums = _causal_window_sums(jnp.concatenate([prev_ref[...], zb], axis=0))
        prev_ref[...] = zb[ts - HALO:, :]
        _, counts = _window_counts(i * ts, ts)
        for g in range(len(POOL_WINDOWS)):
            lanes = slice(g * GROUP_DIM, (g + 1) * GROUP_DIM)
            diff = sums[g][HALO:, :] / counts[g] - zb[:, lanes]
            lin = _dot(diff.astype(BF16), wp_ref[g].astype(BF16)) + bp_ref[:, lanes]
            y_ref[:, D_A + g * GROUP_DIM:D_A + (g + 1) * GROUP_DIM] = (lin * ps_ref[:, lanes]).astype(BF16)

        mix = None
        for j in range(N_CHIPS):
            part = _dot(y_ref[:, j * rs:(j + 1) * rs], wout_ref[j])
            mix = part if mix is None else mix + part
        mix_ref[...] = mix
        r2 = lax.rsqrt(_rowmean(mix * mix) + EPS)
        x1 = xv + mod_ref[2:3, :] * ((mix * r2) * g1post_ref[...])
        x1_ref[...] = x1
        r3 = lax.rsqrt(_rowmean(x1 * x1) + EPS)
        h2_ref[...] = (((x1 * r3) * g2pre_ref[...]) * (1.0 + mod_ref[4:5, :]) + mod_ref[3:4, :]).astype(BF16)

    vec = _const((1, D_MODEL))
    f32_rows = jax.ShapeDtypeStruct((s_len, D_MODEL), F32)
    bf16_rows = jax.ShapeDtypeStruct((s_len, D_MODEL), BF16)
    return pl.pallas_call(
        body, name="fwd_mix", grid=(s_len // ts,),
        in_specs=[_rows(ts, D_MODEL), _const((N_MOD, D_MODEL)), vec, vec, vec, _VMEM, _VMEM,
                  _const((N_HEADS, CHUNK, CHUNK)), _const((CHUNK, D_A)), _const((1, D_A)), _const((1, D_A)),
                  _const((N_HEADS, GROUP_DIM, GROUP_DIM)), _const((1, D_B)), _const((1, D_B))],
        out_specs=[_rows(ts, D_MODEL), _rows(ts, D_Z), _rows(ts, D_MODEL), _rows(ts, D_MODEL), _rows(ts, D_MODEL),
                   _rows(ts, D_MODEL)],
        out_shape=[bf16_rows, jax.ShapeDtypeStruct((s_len, D_Z), F32), bf16_rows, f32_rows, f32_rows, bf16_rows],
        scratch_shapes=[pltpu.VMEM((ts, D_A), F32), pltpu.VMEM((HALO, D_B), F32)],
        compiler_params=_params(),
    )(x, mod6, n1pre, n1post, n2pre, win_g, wout_g, w_spatial, bsp_full, gain, bias, w_pool, b_pool, pool_scale)


def _mixer_forward_tile(za, wc, bsp_ref, gain, bias, mixed_ref):
    ga, dga = _gelu_parts(za)
    u = ga[:, :D_A]
    v = ga[:, D_A:]
    mu = _rowmean(v)
    vc = v - mu
    rstd = lax.rsqrt(_rowmean(vc * vc) + EPS)
    vhat = vc * rstd
    vn = (vhat * gain + bias).astype(BF16)
    ts = za.shape[0]
    for k in range(ts // CHUNK):
        for h in range(N_HEADS):
            blk = vn[k * CHUNK:(k + 1) * CHUNK, h * HEAD_DIM:(h + 1) * HEAD_DIM]
            mixed_ref[k * CHUNK:(k + 1) * CHUNK, h * HEAD_DIM:(h + 1) * HEAD_DIM] = (
                _dot(wc[h], blk) + bsp_ref[:, h * HEAD_DIM:(h + 1) * HEAD_DIM])
    return u, vhat, rstd, vn, dga


def _fwd_fc1(h2, fc1_g, ts):
    s_len = h2.shape[0]
    cs = D_FF // N_CHIPS

    def body(h_ref, w_ref, q_ref):
        hb = h_ref[...]
        for j in range(N_CHIPS):
            p = jnp.maximum(_dot(hb, w_ref[j]), 0.0)
            q_ref[:, j * cs:(j + 1) * cs] = (p * p).astype(BF16)

    return pl.pallas_call(
        body, name="fwd_fc1", grid=(s_len // ts,),
        in_specs=[_rows(ts, D_MODEL), _VMEM],
        out_specs=_rows(ts, D_FF),
        out_shape=jax.ShapeDtypeStruct((s_len, D_FF), BF16),
        compiler_params=_params(),
    )(h2, fc1_g)


def _fwd_fc2_loss(q, x1, target, mod6, n2post, fc2_g, ts):
    s_len = q.shape[0]
    rs = D_FF // N_CHIPS

    def body(q_ref, x1_ref, t_ref, mod_ref, g_ref, w_ref, dy_ref, df_ref, loss_ref, dgate_ref, dg_ref):
        i = pl.program_id(0)

        @pl.when(i == 0)
        def _():
            loss_ref[...] = jnp.zeros_like(loss_ref)
            dgate_ref[...] = jnp.zeros_like(dgate_ref)
            dg_ref[...] = jnp.zeros_like(dg_ref)

        f = _dot(q_ref[:, 0:rs], w_ref[0])
        for j in range(1, N_CHIPS):
            f = f + _dot(q_ref[:, j * rs:(j + 1) * rs], w_ref[j])
        r4 = lax.rsqrt(_rowmean(f * f) + EPS)
        fh = f * r4
        gate = mod_ref[5:6, :]
        gn = g_ref[...]
        err = (x1_ref[...] + gate * (fh * gn)) - t_ref[...]
        loss_ref[...] += 0.5 * jnp.sum(_rowmean(err * err), axis=0, keepdims=True)
        dy = err * (1.0 / D_MODEL)
        dy_ref[...] = dy
        dgate_ref[...] += _colsum(dy * (fh * gn))
        dg_ref[...] += _colsum((dy * gate) * fh)
        gh = (dy * gate) * gn
        df_ref[...] = (r4 * (gh - fh * _rowmean(gh * fh))).astype(BF16)

    return pl.pallas_call(
        body, name="fwd_fc2_loss", grid=(s_len // ts,),
        in_specs=[_rows(ts, D_FF), _rows(ts, D_MODEL), _rows(ts, D_MODEL), _const((N_MOD, D_MODEL)),
                  _const((1, D_MODEL)), _VMEM],
        out_specs=[_rows(ts, D_MODEL), _rows(ts, D_MODEL), _const((1, 1)), _const((1, D_MODEL)), _const((1, D_MODEL))],
        out_shape=[jax.ShapeDtypeStruct((s_len, D_MODEL), F32), jax.ShapeDtypeStruct((s_len, D_MODEL), BF16),
                   jax.ShapeDtypeStruct((1, 1), F32), jax.ShapeDtypeStruct((1, D_MODEL), F32),
                   jax.ShapeDtypeStruct((1, D_MODEL), F32)],
        compiler_params=_params(),
    )(q, x1, target, mod6, n2post, fc2_g)


def _zero_on_first_step(*refs):
    @pl.when(pl.program_id(0) == 0)
    def _():
        for ref in refs:
            ref[...] = jnp.zeros_like(ref)


def _store_on_last_step(*pairs):
    @pl.when(pl.program_id(0) == pl.num_programs(0) - 1)
    def _():
        for acc_ref, hbm_ref in pairs:
            pltpu.sync_copy(acc_ref, hbm_ref)


def _bwd_fc2(df, q, fc2_g, ts):
    s_len = df.shape[0]
    cs = D_FF // N_CHIPS

    def body(df_ref, q_ref, w_ref, dp_ref, dw_hbm, dw_ref):
        _zero_on_first_step(dw_ref)
        dfb = df_ref[...]
        for j in range(N_CHIPS):
            qb = q_ref[:, j * cs:(j + 1) * cs]
            dw_ref[j] += _dot_tn(qb, dfb).reshape(2, cs // 2, D_MODEL)
            dq = _dot_nt(dfb, w_ref[j])
            dp_ref[:, j * cs:(j + 1) * cs] = (dq * (2.0 * jnp.sqrt(qb.astype(F32)))).astype(BF16)
        _store_on_last_step((dw_ref, dw_hbm))

    dw_shape = (N_CHIPS, 2, cs // 2, D_MODEL)
    return pl.pallas_call(
        body, name="bwd_fc2", grid=(s_len // ts,),
        in_specs=[_rows(ts, D_MODEL), _rows(ts, D_FF), _VMEM],
        out_specs=[_rows(ts, D_FF), _ANY],
        out_shape=[jax.ShapeDtypeStruct((s_len, D_FF), BF16), jax.ShapeDtypeStruct(dw_shape, F32)],
        scratch_shapes=[pltpu.VMEM(dw_shape, F32)],
        compiler_params=_params(),
    )(df, q, fc2_g)


def _bwd_fc1_out(dp, dy, x1, mix, h2, ycat, mod6, n2pre, n1post, fc1_g, wout_g, dep, ts):
    s_len = dp.shape[0]
    cs = D_FF // N_CHIPS
    rs = D_MODEL // N_CHIPS

    def body(dp_ref, dy_ref, x1_ref, mix_ref, h2_ref, yc_ref, mod_ref, g2_ref, g1_ref, w1_ref, wo_ref, dep_ref,
             dx1_ref, dyc_ref, dshift2_ref, da2_ref, dgate1_ref, dg1_ref, dw1_hbm, dwo_hbm, dw1_ref, dwo_ref):
        _zero_on_first_step(dshift2_ref, da2_ref, dgate1_ref, dg1_ref, dw1_ref, dwo_ref)
        h2b = h2_ref[...]
        dh2 = None
        for j in range(N_CHIPS):
            dpb = dp_ref[:, j * cs:(j + 1) * cs]
            dw1_ref[j] += _dot_tn(h2b, dpb).reshape(2, D_MODEL // 2, cs)
            part = _dot_nt(dpb, w1_ref[j])
            dh2 = part if dh2 is None else dh2 + part
        x1 = x1_ref[...]
        r3 = lax.rsqrt(_rowmean(x1 * x1) + EPS)
        xh = x1 * r3
        a2 = g2_ref[...] * (1.0 + mod_ref[4:5, :])
        dshift2_ref[...] += _colsum(dh2)
        da2_ref[...] += _colsum(dh2 * xh)
        dxh = dh2 * a2
        dx1 = dy_ref[...] + r3 * (dxh - xh * _rowmean(dxh * xh))
        dx1_ref[...] = dx1

        mix = mix_ref[...]
        r2 = lax.rsqrt(_rowmean(mix * mix) + EPS)
        mh = mix * r2
        gate = mod_ref[2:3, :]
        gn = g1_ref[...]
        dgate1_ref[...] += _colsum(dx1 * (mh * gn))
        dg1_ref[...] += _colsum((dx1 * gate) * mh)
        gh = (dx1 * gate) * gn
        dmix = (r2 * (gh - mh * _rowmean(gh * mh))).astype(BF16)
        dwo_ref[...] += _dot_tn(yc_ref[...], dmix).reshape(N_CHIPS, 2, rs // 2, D_MODEL)
        for j in range(N_CHIPS):
            dyc_ref[:, j * rs:(j + 1) * rs] = _dot_nt(dmix, wo_ref[j])
        _store_on_last_step((dw1_ref, dw1_hbm), (dwo_ref, dwo_hbm))

    vec = jax.ShapeDtypeStruct((1, D_MODEL), F32)
    dw1_shape = (N_CHIPS, 2, D_MODEL // 2, cs)
    dwo_shape = (N_CHIPS, 2, rs // 2, D_MODEL)
    return pl.pallas_call(
        body, name="bwd_fc1_out", grid=(s_len // ts,),
        in_specs=[_rows(ts, D_FF), _rows(ts, D_MODEL), _rows(ts, D_MODEL), _rows(ts, D_MODEL), _rows(ts, D_MODEL),
                  _rows(ts, D_MODEL), _const((N_MOD, D_MODEL)), _const((1, D_MODEL)), _const((1, D_MODEL)), _VMEM,
                  _VMEM, _ANY],
        out_specs=[_rows(ts, D_MODEL), _rows(ts, D_MODEL)] + [_const((1, D_MODEL))] * 4 + [_ANY, _ANY],
        out_shape=[jax.ShapeDtypeStruct((s_len, D_MODEL), F32), jax.ShapeDtypeStruct((s_len, D_MODEL), F32),
                   vec, vec, vec, vec, jax.ShapeDtypeStruct(dw1_shape, F32), jax.ShapeDtypeStruct(dwo_shape, F32)],
        scratch_shapes=[pltpu.VMEM(dw1_shape, F32), pltpu.VMEM(dwo_shape, F32)],
        compiler_params=_params(),
    )(dp, dy, x1, mix, h2, ycat, mod6, n2pre, n1post, fc1_g, wout_g, dep)


def _mixer_bwd(z, dyc, w_spatial, bsp_full, gain, bias, w_pool, b_pool, pool_scale, dep, ts):
    s_len = z.shape[0]
    nb = ts // HALO
    last = s_len // HALO - 1
    te = ts + HALO

    def body(z_ref, zprev_ref, znext_ref, dyc_ref, dynext_ref, ws_ref, bsp_ref, gain_ref, bias_ref, wp_ref, bp_ref,
             ps_ref, dep_ref, dz_ref, dws_ref, dbsp_ref, dgain_ref, dbias_ref, dwp_ref, dbp_ref, dps_ref, mixed_ref,
             dvn_ref):
        i = pl.program_id(0)

        @pl.when(i == 0)
        def _():
            for ref in (dws_ref, dbsp_ref, dgain_ref, dbias_ref, dwp_ref, dbp_ref, dps_ref):
                ref[...] = jnp.zeros_like(ref)

        wc = _tril_weights(ws_ref)
        gain = gain_ref[...]
        u, vhat, rstd, vn, dga = _mixer_forward_tile(z_ref[:, :2 * D_A], wc, bsp_ref, gain, bias_ref[...], mixed_ref)
        dya = dyc_ref[:, :D_A]
        du = dya * mixed_ref[...]
        dmixed = dya * u
        dmb = dmixed.astype(BF16)
        dm_sum = dmixed[0:CHUNK, :]
        for k in range(1, ts // CHUNK):
            dm_sum = dm_sum + dmixed[k * CHUNK:(k + 1) * CHUNK, :]
        r_idx = lax.broadcasted_iota(jnp.int32, (CHUNK, CHUNK), 0)
        s_idx = lax.broadcasted_iota(jnp.int32, (CHUNK, CHUNK), 1)
        causal = (s_idx <= r_idx).astype(F32)
        for h in range(N_HEADS):
            lanes = slice(h * HEAD_DIM, (h + 1) * HEAD_DIM)
            dbsp_ref[h] += jnp.sum(dm_sum[:, lanes], axis=1, keepdims=True)
            acc = None
            for k in range(ts // CHUNK):
                rows = slice(k * CHUNK, (k + 1) * CHUNK)
                t = _dot_nt(dmb[rows, lanes], vn[rows, lanes])
                acc = t if acc is None else acc + t
                dvn_ref[rows, lanes] = _dot_tn(wc[h], dmb[rows, lanes])
            dws_ref[h] += acc * causal
        dvn = dvn_ref[...]
        dgain_ref[...] += _colsum(dvn * vhat)
        dbias_ref[...] += _colsum(dvn)
        dvh = dvn * gain
        dv = rstd * (dvh - _rowmean(dvh) - vhat * _rowmean(dvh * vhat))
        dz_ref[:, :D_A] = (du * dga[:, :D_A]).astype(BF16)
        dz_ref[:, D_A:2 * D_A] = (dv * dga[:, D_A:]).astype(BF16)

        zb = z_ref[:, 2 * D_A:]
        prev = jnp.where(i == 0, 0.0, zprev_ref[...])
        zb_ext = jnp.concatenate([zb, znext_ref[...]], axis=0)
        sums = _causal_window_sums(jnp.concatenate([prev, zb_ext], axis=0))
        pos, counts = _window_counts(i * ts, te)
        dyb_ext = jnp.concatenate([dyc_ref[:, D_A:], dynext_ref[...]], axis=0)
        dlin_ext = dyb_ext * ps_ref[...]
        dbp_ref[...] += _colsum(dlin_ext[:ts, :])
        scaled = []
        ddiffs = []
        lins = []
        for g in range(len(POOL_WINDOWS)):
            lanes = slice(g * GROUP_DIM, (g + 1) * GROUP_DIM)
            diff = (sums[g][HALO:, :] / counts[g] - zb_ext[:, lanes]).astype(BF16)
            wpb = wp_ref[g].astype(BF16)
            dlb = dlin_ext[:, lanes].astype(BF16)
            lins.append(_dot(diff[:ts, :], wpb) + bp_ref[:, lanes])
            dwp_ref[g] += _dot_tn(diff[:ts, :], dlb[:ts, :])
            dd = _dot_nt(dlb, wpb)
            ddiffs.append(dd)
            scaled.append(jnp.where(pos < float(s_len), dd / counts[g], 0.0))
        dps_ref[...] += _colsum(dyb_ext[:ts, :] * jnp.concatenate(lins, axis=1))
        back = _anticausal_window_sums(jnp.concatenate(scaled, axis=1))
        for g in range(len(POOL_WINDOWS)):
            dz_ref[:, 2 * D_A + g * GROUP_DIM:2 * D_A + (g + 1) * GROUP_DIM] = (
                back[g][:ts, :] - ddiffs[g][:ts, :]).astype(BF16)

    sq = jax.ShapeDtypeStruct((N_HEADS, CHUNK, CHUNK), F32)
    vec = jax.ShapeDtypeStruct((1, D_A), F32)
    return pl.pallas_call(
        body, name="mixer_bwd", grid=(s_len // ts,),
        in_specs=[_rows(ts, D_Z),
                  pl.BlockSpec((HALO, D_B), lambda i: (jnp.maximum(i * nb - 1, 0), 2)),
                  pl.BlockSpec((HALO, D_B), lambda i: (jnp.minimum((i + 1) * nb, last), 2)),
                  _rows(ts, D_MODEL),
                  pl.BlockSpec((HALO, D_B), lambda i: (jnp.minimum((i + 1) * nb, last), 1)),
                  _const((N_HEADS, CHUNK, CHUNK)), _const((CHUNK, D_A)), _const((1, D_A)), _const((1, D_A)),
                  _const((N_HEADS, GROUP_DIM, GROUP_DIM)), _const((1, D_B)), _const((1, D_B)), _ANY],
        out_specs=[_rows(ts, D_Z), _const((N_HEADS, CHUNK, CHUNK)), _const((N_HEADS, CHUNK, 1)), _const((1, D_A)),
                   _const((1, D_A)), _const((N_HEADS, GROUP_DIM, GROUP_DIM)), _const((1, D_B)), _const((1, D_B))],
        out_shape=[jax.ShapeDtypeStruct((s_len, D_Z), BF16), sq, jax.ShapeDtypeStruct((N_HEADS, CHUNK, 1), F32), vec,
                   vec, sq, vec, vec],
        scratch_shapes=[pltpu.VMEM((ts, D_A), F32), pltpu.VMEM((ts, D_A), F32)],
        compiler_params=_params(),
    )(z, z, z, dyc, dyc, w_spatial, bsp_full, gain, bias, w_pool, b_pool, pool_scale, dep)


def _bwd_in(dz, dx1, x, h1, mod6, n1pre, win_g, dep, ts):
    s_len = x.shape[0]
    cs = D_Z // N_CHIPS

    def body(dz_ref, dx1_ref, x_ref, h1_ref, mod_ref, g_ref, w_ref, dep_ref, gx_ref, dshift_ref, da_ref, dw_hbm,
             dw_ref):
        _zero_on_first_step(dshift_ref, da_ref, dw_ref)
        h1b = h1_ref[...]
        dh = None
        for j in range(N_CHIPS):
            dzb = dz_ref[:, j * cs:(j + 1) * cs]
            dw_ref[j] += _dot_tn(h1b, dzb).reshape(2, D_MODEL // 2, cs)
            part = _dot_nt(dzb, w_ref[j])
            dh = part if dh is None else dh + part
        xv = x_ref[...]
        r = lax.rsqrt(_rowmean(xv * xv) + EPS)
        xh = xv * r
        a1 = g_ref[...] * (1.0 + mod_ref[1:2, :])
        dshift_ref[...] += _colsum(dh)
        da_ref[...] += _colsum(dh * xh)
        dxh = dh * a1
        gx_ref[...] = dx1_ref[...] + r * (dxh - xh * _rowmean(dxh * xh))
        _store_on_last_step((dw_ref, dw_hbm))

    vec = jax.ShapeDtypeStruct((1, D_MODEL), F32)
    dw_shape = (N_CHIPS, 2, D_MODEL // 2, cs)
    return pl.pallas_call(
        body, name="bwd_in", grid=(s_len // ts,),
        in_specs=[_rows(ts, D_Z), _rows(ts, D_MODEL), _rows(ts, D_MODEL), _rows(ts, D_MODEL),
                  _const((N_MOD, D_MODEL)), _const((1, D_MODEL)), _VMEM, _ANY],
        out_specs=[_rows(ts, D_MODEL), _const((1, D_MODEL)), _const((1, D_MODEL)), _ANY],
        out_shape=[jax.ShapeDtypeStruct((s_len, D_MODEL), F32), vec, vec, jax.ShapeDtypeStruct(dw_shape, F32)],
        scratch_shapes=[pltpu.VMEM(dw_shape, F32)],
        compiler_params=_params(),
    )(dz, dx1, x, h1, mod6, n1pre, win_g, dep)


def _adamw_math(w, g, m, v):
    m = ADAM_B1 * m + (1.0 - ADAM_B1) * g
    v = ADAM_B2 * v + (1.0 - ADAM_B2) * (g * g)
    m_hat = m / (1.0 - ADAM_B1 ** ADAM_STEP)
    v_hat = v / (1.0 - ADAM_B2 ** ADAM_STEP)
    delta = -ADAM_LR * (m_hat / (jnp.sqrt(v_hat) + ADAM_EPS) + ADAM_WD * w)
    return delta, m, v


def _adamw(g, w, m, v, name, tr):
    rows, cols = w.shape

    def body(g_ref, w_ref, m_ref, v_ref, d_ref, nm_ref, nv_ref):
        d, nm, nv = _adamw_math(w_ref[...], g_ref[...], m_ref[...], v_ref[...])
        d_ref[...] = d
        nm_ref[...] = nm
        nv_ref[...] = nv

    spec = _rows(tr, cols)
    shape = jax.ShapeDtypeStruct((rows, cols), F32)
    return pl.pallas_call(
        body, name=name, grid=(rows // tr,), in_specs=[spec] * 4, out_specs=[spec] * 3, out_shape=[shape] * 3,
        compiler_params=_params(),
    )(g, w, m, v)


def _ada_grad_adamw(sc_t, dmod_shard, w, m, v, tr):
    rows, cols = w.shape

    def body(s_ref, dm_ref, w_ref, m_ref, v_ref, g_ref, d_ref, nm_ref, nv_ref):
        g = s_ref[:, 0:1] * dm_ref[0:1, :]
        for b in range(1, N_DEV):
            g = g + s_ref[:, b:b + 1] * dm_ref[b:b + 1, :]
        g_ref[...] = g
        d, nm, nv = _adamw_math(w_ref[...], g, m_ref[...], v_ref[...])
        d_ref[...] = d
        nm_ref[...] = nm
        nv_ref[...] = nv

    spec = _rows(tr, cols)
    shape = jax.ShapeDtypeStruct((rows, cols), F32)
    return pl.pallas_call(
        body, name="ada_grad_adamw", grid=(rows // tr,),
        in_specs=[_rows(tr, N_DEV), _const((N_DEV, cols)), spec, spec, spec],
        out_specs=[spec] * 4, out_shape=[shape] * 4, compiler_params=_params(),
    )(sc_t, dmod_shard, w, m, v)


def _mod_grads(da1, dshift1, dgate1, dg1post, da2, dshift2, dgate2, dg2post, mod6, n1pre, n2pre):
    def body(da1_ref, ds1_ref, dgt1_ref, dg1_ref, da2_ref, ds2_ref, dgt2_ref, dg2_ref, mod_ref, n1_ref, n2_ref,
             dmod_ref, dn_ref):
        dmod_ref[0:1, :] = ds1_ref[...]
        dmod_ref[1:2, :] = da1_ref[...] * n1_ref[...]
        dmod_ref[2:3, :] = dgt1_ref[...]
        dmod_ref[3:4, :] = ds2_ref[...]
        dmod_ref[4:5, :] = da2_ref[...] * n2_ref[...]
        dmod_ref[5:6, :] = dgt2_ref[...]
        dn_ref[0:1, :] = da1_ref[...] * (1.0 + mod_ref[1:2, :])
        dn_ref[1:2, :] = dg1_ref[...]
        dn_ref[2:3, :] = da2_ref[...] * (1.0 + mod_ref[4:5, :])
        dn_ref[3:4, :] = dg2_ref[...]

    return pl.pallas_call(
        body, name="mod_grads",
        out_shape=[jax.ShapeDtypeStruct((N_MOD, D_MODEL), F32), jax.ShapeDtypeStruct((4, D_MODEL), F32)],
    )(da1, dshift1, dgate1, dg1post, da2, dshift2, dgate2, dg2post, mod6, n1pre, n2pre)


def _position():
    x, y, c = lax.axis_index("x"), lax.axis_index("y"), lax.axis_index("c")
    return x, y, c


def _flip(v, bit):
    return 1 - v if bit else v


def _peer(x, y, c, k):
    return (_flip(x, k & 4), _flip(y, k & 2), _flip(c, k & 1))


def _remote(src, dst, send_sem, recv_sem, device):
    return pltpu.make_async_remote_copy(src_ref=src, dst_ref=dst, send_sem=send_sem, recv_sem=recv_sem,
                                        device_id=device, device_id_type=MESH)


def _cast_to_slot(w, pos, name, tr):
    rows, cols = w.shape

    def body(pos_ref, w_ref, o_ref):
        o_ref[0] = w_ref[...].astype(BF16)

    return pl.pallas_call(
        body, name=name,
        grid_spec=pltpu.PrefetchScalarGridSpec(
            num_scalar_prefetch=1, grid=(rows // tr,),
            in_specs=[pl.BlockSpec((tr, cols), lambda i, pos: (i, 0))],
            out_specs=pl.BlockSpec((1, tr, cols), lambda i, pos: (pos[1], i, 0))),
        out_shape=jax.ShapeDtypeStruct((N_CHIPS, rows, cols), BF16), compiler_params=_params(),
    )(pos, w)


def _mod_exchange(c_row, w_ada_shard, b_ada_row):
    cs = w_ada_shard.shape[1]

    def body(c_ref, w_ref, b_ref, mod_ref, sc_ref, rows_ref, send1, recv1, send2, recv2):
        x, y, c = _position()
        me = 4 * x + 2 * y + c
        chip = 2 * x + y
        cv = c_ref[...]
        sc_ref[me] = cv * jax.nn.sigmoid(cv)
        gather = [_remote(sc_ref.at[me], sc_ref.at[me], send1.at[k - 1], recv1.at[k - 1], _peer(x, y, c, k))
                  for k in range(1, N_DEV)]
        for cp in gather:
            cp.start()
        for k in range(1, N_DEV):
            px, py, pc = _peer(x, y, c, k)
            src = 4 * px + 2 * py + pc
            _remote(sc_ref.at[src], sc_ref.at[src], send1.at[k - 1], recv1.at[k - 1], (px, py, pc)).wait_recv()
        for cp in gather:
            cp.wait_send()
        sc_all = jnp.concatenate([sc_ref[b] for b in range(N_DEV)], axis=0)
        part = jnp.dot(sc_all, w_ref[...], precision=lax.Precision.HIGHEST, preferred_element_type=F32)
        part = part + b_ref[:, pl.ds(pl.multiple_of(chip * cs, 128), cs)]
        for b in range(N_DEV):
            rows_ref[b] = part[b:b + 1, :]
        mod_ref[chip] = rows_ref[me]
        hand = []
        for k in (2, 4, 6):
            px, py, _ = _peer(x, y, c, k)
            hand.append(_remote(rows_ref.at[4 * px + 2 * py + c], mod_ref.at[chip], send2.at[k // 2 - 1],
                                recv2.at[k // 2 - 1], (px, py, c)))
        for cp in hand:
            cp.start()
        for k in (2, 4, 6):
            px, py, _ = _peer(x, y, c, k)
            pchip = 2 * px + py
            _remote(rows_ref.at[me], mod_ref.at[pchip], send2.at[k // 2 - 1], recv2.at[k // 2 - 1],
                    (px, py, c)).wait_recv()
        for cp in hand:
            cp.wait_send()

    return pl.pallas_call(
        body, name="mod_exchange",
        in_specs=[_VMEM, _VMEM, _VMEM], out_specs=[_VMEM, _VMEM],
        out_shape=[jax.ShapeDtypeStruct((N_CHIPS, 1, cs), F32), jax.ShapeDtypeStruct((N_DEV, 1, D_MODEL), F32)],
        scratch_shapes=[pltpu.VMEM((N_DEV, 1, cs), F32), pltpu.SemaphoreType.DMA((N_DEV - 1,)),
                        pltpu.SemaphoreType.DMA((N_DEV - 1,)), pltpu.SemaphoreType.DMA((N_CHIPS - 1,)),
                        pltpu.SemaphoreType.DMA((N_CHIPS - 1,))],
        compiler_params=pltpu.CompilerParams(vmem_limit_bytes=VMEM_LIMIT),
    )(c_row, w_ada_shard, b_ada_row)


_HBM = pl.BlockSpec(memory_space=pltpu.HBM)
_SEM = pl.BlockSpec(memory_space=pltpu.SEMAPHORE)
_EFFECT = pltpu.SideEffectType.DATAFLOW_SIDE_EFFECTING
_CHIP_HOPS = (2, 4, 6)


def _in_hbm(a):
    return pltpu.with_memory_space_constraint(a, pltpu.HBM)


def _sems3():
    return pltpu.SemaphoreType.DMA((len(_CHIP_HOPS),))


def _ag_start(lands, after):
    n = len(lands)

    def body(*refs):
        zones = refs[:n]
        sends, recvs = refs[n + 1:2 * n + 1], refs[2 * n + 1:3 * n + 1]
        x, y, c = _position()
        chip = 2 * x + y
        for i in range(n):
            half = zones[i].shape[1] // 2
            mine = zones[i].at[chip, pl.ds(c * half, half)]
            for s, k in enumerate(_CHIP_HOPS):
                px, py, _ = _peer(x, y, c, k)
                _remote(mine, mine, sends[i].at[s], recvs[i].at[s], (px, py, c)).start()

    out = pl.pallas_call(
        body, name="ag_start",
        in_specs=[_HBM] * n + [_ANY],
        out_specs=[_SEM] * (2 * n) + [_HBM] * n,
        out_shape=[_sems3()] * (2 * n) + [pltpu.HBM(z.shape, BF16) for z in lands],
        input_output_aliases={i: 2 * n + i for i in range(n)},
        compiler_params=pltpu.CompilerParams(has_side_effects=_EFFECT),
    )(*[_in_hbm(z) for z in lands], after)
    return [(out[2 * n + i], out[i], out[n + i]) for i in range(n)]


def _ag_pass(group, after, name):
    n = len(group)

    def body(*refs):
        zones = refs[:n]
        sends, recvs = refs[n:2 * n], refs[2 * n:3 * n]
        fsends, frecvs = refs[4 * n + 1:5 * n + 1], refs[5 * n + 1:6 * n + 1]
        x, y, c = _position()
        chip = 2 * x + y
        for i in range(n):
            half = zones[i].shape[1] // 2
            rows = pl.ds(c * half, half)
            for s, k in enumerate(_CHIP_HOPS):
                px, py, _ = _peer(x, y, c, k)
                landed = zones[i].at[2 * px + py, rows]
                _remote(landed, landed, sends[i].at[s], recvs[i].at[s], (px, py, c)).wait_recv()
                _remote(landed, landed, fsends[i].at[s], frecvs[i].at[s], (x, y, 1 - c)).start()
        for i in range(n):
            half = zones[i].shape[1] // 2
            mine = zones[i].at[chip, pl.ds(c * half, half)]
            for s, k in enumerate(_CHIP_HOPS):
                px, py, _ = _peer(x, y, c, k)
                _remote(mine, mine, sends[i].at[s], recvs[i].at[s], (px, py, c)).wait_send()

    out = pl.pallas_call(
        body, name=name,
        in_specs=[_HBM] * n + [_SEM] * (2 * n) + [_ANY],
        out_specs=[_HBM] * n + [_SEM] * (2 * n),
        out_shape=[pltpu.HBM(g[0].shape, BF16) for g in group] + [_sems3()] * (2 * n),
        input_output_aliases={i: i for i in range(n)},
        compiler_params=pltpu.CompilerParams(has_side_effects=_EFFECT),
    )(*[g[0] for g in group], *[g[1] for g in group], *[g[2] for g in group], after)
    return [(out[i], out[n + i], out[2 * n + i]) for i in range(n)]


def _ag_done(group, name):
    n = len(group)

    def body(*refs):
        lands = refs[:n]
        fsends, frecvs = refs[n:2 * n], refs[2 * n:3 * n]
        x, y, c = _position()
        for i in range(n):
            half = lands[i].shape[1] // 2
            for s, k in enumerate(_CHIP_HOPS):
                px, py, _ = _peer(x, y, c, k)
                sent = lands[i].at[2 * px + py, pl.ds(c * half, half)]
                got = lands[i].at[2 * px + py, pl.ds((1 - c) * half, half)]
                cp = _remote(sent, got, fsends[i].at[s], frecvs[i].at[s], (x, y, 1 - c))
                cp.wait_recv()
                cp.wait_send()

    out = pl.pallas_call(
        body, name=name,
        in_specs=[_HBM] * n + [_SEM] * (2 * n),
        out_specs=[_HBM] * n,
        out_shape=[pltpu.HBM(g[0].shape, BF16) for g in group],
        input_output_aliases={i: i for i in range(n)},
        compiler_params=pltpu.CompilerParams(has_side_effects=_EFFECT),
    )(*[g[0] for g in group], *[g[1] for g in group], *[g[2] for g in group])
    return list(out)


def _all_reduce_small(v, dep):
    rows = v.shape[0]

    def body(v_ref, dep_ref, out_ref, sib_ref, slots_ref, send_a, recv_a, send_b, recv_b):
        x, y, c = _position()
        chip = 2 * x + y
        pair = _remote(v_ref, sib_ref, send_a, recv_a, (x, y, 1 - c))
        pair.start()
        pair.wait_recv()
        slots_ref[chip] = v_ref[...] + sib_ref[...]
        spread = []
        for k in (2, 4, 6):
            px, py, _ = _peer(x, y, c, k)
            spread.append(_remote(slots_ref.at[chip], slots_ref.at[chip], send_b.at[k // 2 - 1], recv_b.at[k // 2 - 1],
                                  (px, py, c)))
        for cp in spread:
            cp.start()
        for k in (2, 4, 6):
            px, py, _ = _peer(x, y, c, k)
            landed = slots_ref.at[2 * px + py]
            _remote(landed, landed, send_b.at[k // 2 - 1], recv_b.at[k // 2 - 1], (px, py, c)).wait_recv()
        out_ref[...] = ((slots_ref[0] + slots_ref[1]) + slots_ref[2]) + slots_ref[3]
        pair.wait_send()
        for cp in spread:
            cp.wait_send()

    return pl.pallas_call(
        body, name="all_reduce_small",
        in_specs=[_VMEM, _ANY], out_specs=_VMEM, out_shape=jax.ShapeDtypeStruct((rows, 128), F32),
        scratch_shapes=[pltpu.VMEM((rows, 128), F32), pltpu.VMEM((N_CHIPS, rows, 128), F32),
                        pltpu.SemaphoreType.DMA, pltpu.SemaphoreType.DMA,
                        pltpu.SemaphoreType.DMA((N_CHIPS - 1,)), pltpu.SemaphoreType.DMA((N_CHIPS - 1,))],
        compiler_params=pltpu.CompilerParams(vmem_limit_bytes=VMEM_LIMIT),
    )(v, dep)


def _sibling_exchange(pgbs, name):
    n = len(pgbs)

    def body(*refs):
        ins, outs = refs[:n], refs[n:2 * n]
        send, recv = refs[2 * n], refs[2 * n + 1]
        x, y, c = _position()
        copies = [_remote(ins[i].at[j, 1 - c], outs[i].at[j], send.at[N_CHIPS * i + j], recv.at[N_CHIPS * i + j],
                          (x, y, 1 - c)) for i in range(n) for j in range(N_CHIPS)]
        for cp in copies:
            cp.start()
        for cp in copies:
            cp.wait()

    return pl.pallas_call(
        body, name=name, in_specs=[_ANY] * n, out_specs=[_ANY] * n,
        out_shape=[jax.ShapeDtypeStruct((N_CHIPS,) + p.shape[2:], p.dtype) for p in pgbs],
        scratch_shapes=[pltpu.SemaphoreType.DMA((N_CHIPS * n,)), pltpu.SemaphoreType.DMA((N_CHIPS * n,))],
    )(*pgbs)


def _sibling_add(pg, from_sib, pos, name):
    _, _, r, cdim = pg.shape

    def body(pos_ref, pg_ref, fs_ref, qb_ref, own_ref):
        j = pl.program_id(0)
        q = pg_ref[0, 0] + fs_ref[0].astype(F32)
        qb_ref[0] = q.astype(BF16)

        @pl.when(j == pos_ref[1])
        def _():
            own_ref[...] = q

    return pl.pallas_call(
        body, name=name,
        grid_spec=pltpu.PrefetchScalarGridSpec(
            num_scalar_prefetch=1, grid=(N_CHIPS,),
            in_specs=[pl.BlockSpec((1, 1, r, cdim), lambda j, pos: (j, pos[0], 0, 0)),
                      pl.BlockSpec((1, r, cdim), lambda j, pos: (j, 0, 0))],
            out_specs=[pl.BlockSpec((1, r, cdim), lambda j, pos: (j, 0, 0)),
                       pl.BlockSpec((r, cdim), lambda j, pos: (0, 0))]),
        out_shape=[jax.ShapeDtypeStruct((N_CHIPS, r, cdim), BF16), jax.ShapeDtypeStruct((r, cdim), F32)],
        compiler_params=_params(),
    )(pos, pg, from_sib)


def _rs_start(qbs, name):
    n = len(qbs)

    def body(*refs):
        outs, inboxes = refs[:n], refs[n:2 * n]
        sends, recvs = refs[2 * n:3 * n], refs[3 * n:4 * n]
        x, y, c = _position()
        chip = 2 * x + y
        for i in range(n):
            for s, k in enumerate(_CHIP_HOPS):
                px, py, _ = _peer(x, y, c, k)
                _remote(outs[i].at[2 * px + py], inboxes[i].at[chip], sends[i].at[s], recvs[i].at[s], (px, py, c)).start()

    inboxes = [_in_hbm(lax.empty(q.shape, BF16)) for q in qbs]
    out = pl.pallas_call(
        body, name=name,
        in_specs=[_HBM] * (2 * n),
        out_specs=[_SEM] * (2 * n) + [_HBM] * (2 * n),
        out_shape=[_sems3()] * (2 * n) + [pltpu.HBM(q.shape, BF16) for q in qbs] * 2,
        input_output_aliases={i: 2 * n + i for i in range(2 * n)},
        compiler_params=pltpu.CompilerParams(has_side_effects=_EFFECT),
    )(*[_in_hbm(q) for q in qbs], *inboxes)
    return [(out[2 * n + i], out[3 * n + i], out[i], out[n + i]) for i in range(n)]


def _rs_wait(group, after, name):
    n = len(group)

    def body(*refs):
        outs, inboxes = refs[:n], refs[n:2 * n]
        sends, recvs = refs[2 * n:3 * n], refs[3 * n:4 * n]
        x, y, c = _position()
        for i in range(n):
            for s, k in enumerate(_CHIP_HOPS):
                px, py, _ = _peer(x, y, c, k)
                slot = 2 * px + py
                cp = _remote(outs[i].at[slot], inboxes[i].at[slot], sends[i].at[s], recvs[i].at[s], (px, py, c))
                cp.wait_recv()
                cp.wait_send()

    out = pl.pallas_call(
        body, name=name,
        in_specs=[_HBM] * (2 * n) + [_SEM] * (2 * n) + [_ANY],
        out_specs=[_HBM] * n,
        out_shape=[pltpu.HBM(g[1].shape, BF16) for g in group],
        input_output_aliases={n + i: i for i in range(n)},
        compiler_params=pltpu.CompilerParams(has_side_effects=_EFFECT),
    )(*[g[0] for g in group], *[g[1] for g in group], *[g[2] for g in group], *[g[3] for g in group], after)
    return list(out)


def _final_add(inbox, own, pos, name, tr):
    _, r, cdim = inbox.shape

    def body(pos_ref, a_ref, b_ref, c_ref, own_ref, o_ref):
        total = (a_ref[0].astype(F32) + b_ref[0].astype(F32)) + c_ref[0].astype(F32)
        o_ref[0] = total + own_ref[...]

    def slot(flip):
        return pl.BlockSpec((1, tr, cdim), lambda i, pos: (jnp.bitwise_xor(pos[1], flip), i, 0))

    return pl.pallas_call(
        body, name=name,
        grid_spec=pltpu.PrefetchScalarGridSpec(
            num_scalar_prefetch=1, grid=(r // tr,),
            in_specs=[slot(1), slot(2), slot(3), pl.BlockSpec((tr, cdim), lambda i, pos: (i, 0))],
            out_specs=pl.BlockSpec((1, tr, cdim), lambda i, pos: (pos[0], i, 0))),
        out_shape=jax.ShapeDtypeStruct((2, r, cdim), F32), compiler_params=_params(),
    )(pos, inbox, inbox, inbox, own)


def _sibling_share(shards, name):
    n = len(shards)

    def body(*refs):
        outs = refs[n:2 * n]
        send, recv = refs[2 * n:]
        x, y, c = _position()
        copies = [_remote(outs[i].at[c], outs[i].at[c], send.at[i], recv.at[i], (x, y, 1 - c)) for i in range(n)]
        for cp in copies:
            cp.start()
        for i in range(n):
            theirs = outs[i].at[1 - c]
            _remote(theirs, theirs, send.at[i], recv.at[i], (x, y, 1 - c)).wait_recv()
        for cp in copies:
            cp.wait_send()

    return pl.pallas_call(
        body, name=name, in_specs=[_ANY] * n, out_specs=[_ANY] * n,
        out_shape=[jax.ShapeDtypeStruct(s.shape, F32) for s in shards],
        input_output_aliases={i: i for i in range(n)},
        scratch_shapes=[pltpu.SemaphoreType.DMA((n,))] * 2,
    )(*shards)


_SMALL = (("b_ada", N_MOD * D_MODEL), ("norm1_pre", D_MODEL), ("norm1_post", D_MODEL), ("norm2_pre", D_MODEL),
          ("norm2_post", D_MODEL), ("w_spatial", N_HEADS * CHUNK * CHUNK), ("b_spatial", N_HEADS * CHUNK),
          ("ln_v_gain", D_A), ("ln_v_bias", D_A), ("w_pool", N_HEADS * GROUP_DIM * GROUP_DIM),
          ("b_pool", D_B), ("pool_scale", D_B))
_MOD_ROWS = N_MOD * D_MODEL // 128


def _packed_rows(size):
    return -(-(size // 128) // 8) * 8


def _pack(parts):
    out = []
    for name, size in _SMALL:
        a = parts[name].reshape(size // 128, 128)
        pad = _packed_rows(size) - a.shape[0]
        out.append(jnp.pad(a, ((0, pad), (0, 0))) if pad else a)
    return out


def _unpack(packed, shapes):
    res = {}
    row = 0
    for name, size in _SMALL:
        res[name] = packed[row:row + size // 128].reshape(shapes[name])
        row += _packed_rows(size)
    return res


def _small_adamw(total, wp, mp, vp):
    rows = wp.shape[0]
    head = N_DEV * _MOD_ROWS

    def body(t_ref, w_ref, m_ref, v_ref, g_ref, d_ref, nm_ref, nv_ref):
        gb = t_ref[0:_MOD_ROWS, :]
        for b in range(1, N_DEV):
            gb = gb + t_ref[b * _MOD_ROWS:(b + 1) * _MOD_ROWS, :]
        g_ref[0:_MOD_ROWS, :] = gb
        g_ref[_MOD_ROWS:, :] = t_ref[head:, :]
        d, nm, nv = _adamw_math(w_ref[...], g_ref[...], m_ref[...], v_ref[...])
        d_ref[...] = d
        nm_ref[...] = nm
        nv_ref[...] = nv

    return pl.pallas_call(
        body, name="small_adamw", out_shape=[jax.ShapeDtypeStruct((rows, 128), F32)] * 4,
        compiler_params=pltpu.CompilerParams(vmem_limit_bytes=VMEM_LIMIT),
    )(total, wp, mp, vp)


def kernel(x, c, w_ada, b_ada, norm1_pre, norm1_post, w_in, w_spatial, b_spatial, ln_v_gain, ln_v_bias, w_pool, b_pool, pool_scale, w_out, norm2_pre, norm2_post, w_fc1, w_fc2, loss_target, m_w_ada, m_b_ada, m_norm1_pre, m_norm1_post, m_w_in, m_w_spatial, m_b_spatial, m_ln_v_gain, m_ln_v_bias, m_w_pool, m_b_pool, m_pool_scale, m_w_out, m_norm2_pre, m_norm2_post, m_w_fc1, m_w_fc2, v_w_ada, v_b_ada, v_norm1_pre, v_norm1_post, v_w_in, v_w_spatial, v_b_spatial, v_ln_v_gain, v_ln_v_bias, v_w_pool, v_b_pool, v_pool_scale, v_w_out, v_norm2_pre, v_norm2_post, v_w_fc1, v_w_fc2):
    weights = dict(w_ada=w_ada, b_ada=b_ada, norm1_pre=norm1_pre, norm1_post=norm1_post, w_in=w_in,
                   w_spatial=w_spatial, b_spatial=b_spatial, ln_v_gain=ln_v_gain, ln_v_bias=ln_v_bias, w_pool=w_pool,
                   b_pool=b_pool, pool_scale=pool_scale, w_out=w_out, norm2_pre=norm2_pre, norm2_post=norm2_post,
                   w_fc1=w_fc1, w_fc2=w_fc2)
    m_old = dict(w_ada=m_w_ada, b_ada=m_b_ada, norm1_pre=m_norm1_pre, norm1_post=m_norm1_post, w_in=m_w_in,
                 w_spatial=m_w_spatial, b_spatial=m_b_spatial, ln_v_gain=m_ln_v_gain, ln_v_bias=m_ln_v_bias,
                 w_pool=m_w_pool, b_pool=m_b_pool, pool_scale=m_pool_scale, w_out=m_w_out, norm2_pre=m_norm2_pre,
                 norm2_post=m_norm2_post, w_fc1=m_w_fc1, w_fc2=m_w_fc2)
    v_old = dict(w_ada=v_w_ada, b_ada=v_b_ada, norm1_pre=v_norm1_pre, norm1_post=v_norm1_post, w_in=v_w_in,
                 w_spatial=v_w_spatial, b_spatial=v_b_spatial, ln_v_gain=v_ln_v_gain, ln_v_bias=v_ln_v_bias,
                 w_pool=v_w_pool, b_pool=v_b_pool, pool_scale=v_pool_scale, w_out=v_w_out, norm2_pre=v_norm2_pre,
                 norm2_post=v_norm2_post, w_fc1=v_w_fc1, w_fc2=v_w_fc2)
    order = ("w_ada", "b_ada", "norm1_pre", "norm1_post", "w_in", "w_spatial", "b_spatial", "ln_v_gain", "ln_v_bias",
             "w_pool", "b_pool", "pool_scale", "w_out", "norm2_pre", "norm2_post", "w_fc1", "w_fc2")
    big = ("w_in", "w_out", "w_fc1", "w_fc2")
    mx, my, mc = _position()
    me = 4 * mx + 2 * my + mc
    chip = 2 * mx + my
    row = lambda a: a.reshape(1, -1)

    pos = jnp.stack([mc, chip]).astype(jnp.int32)
    xs, target = x[0], loss_target[0]
    n1pre, n1post, n2pre, n2post = row(norm1_pre), row(norm1_post), row(norm2_pre), row(norm2_post)
    mixer = (w_spatial, jnp.repeat(b_spatial.T, HEAD_DIM, axis=1), row(ln_v_gain), row(ln_v_bias), w_pool,
             row(b_pool), row(pool_scale))
    ts_big, ts_mid = 512, 256

    mod4, sc_all = _mod_exchange(c, w_ada, row(b_ada))
    mod6 = mod4.reshape(N_MOD, D_MODEL)
    ag = _ag_start([_cast_to_slot(weights[n], pos, "cast_" + n, 256) for n in big], mod4)

    win_g, wout_g = _ag_done(_ag_pass([ag[0], ag[1]], mod4, "ag_pass_mix"), "ag_done_mix")
    h1, z, ycat, mix, x1, h2 = _fwd_mix(xs, mod6, n1pre, n1post, n2pre, win_g, wout_g, *mixer, ts_mid)
    (fc1_g,) = _ag_done(_ag_pass([ag[2]], h2, "ag_pass_fc1"), "ag_done_fc1")
    q = _fwd_fc1(h2, fc1_g, ts_big)
    (fc2_g,) = _ag_done(_ag_pass([ag[3]], q, "ag_pass_fc2"), "ag_done_fc2")
    dy, df, loss, dgate2, dg2post = _fwd_fc2_loss(q, x1, target, mod6, n2post, fc2_g, ts_big)
    loss = lax.psum(loss[0, 0], ("x", "y", "c"))

    def reduce_start(partials, tag):
        from_sib = _sibling_exchange(partials, "sibling_exchange_" + tag)
        sums = [_sibling_add(p, fs, pos, "sibling_add_%s_%d" % (tag, i))
                for i, (p, fs) in enumerate(zip(partials, from_sib))]
        return _rs_start([s[0] for s in sums], "rs_start_" + tag), [s[1] for s in sums]

    def reduce_finish(state, owns, names, tag, dep):
        inboxes = _rs_wait(state, dep, "rs_wait_" + tag)
        halves = [_final_add(ib, own, pos, "final_add_" + n, min(256, own.shape[0]))
                  for ib, own, n in zip(inboxes, owns, names)]
        shards = _sibling_share(halves, "sibling_share_" + tag)
        for n, g in zip(names, shards):
            g = g.reshape(weights[n].shape)
            grads[n] = g
            deltas[n], new_m[n], new_v[n] = _adamw(g, weights[n], m_old[n], v_old[n], "adamw_" + n, 256)

    grads, deltas, new_m, new_v = {}, {}, {}, {}
    dp, g_fc2 = _bwd_fc2(df, q, fc2_g, ts_big)
    state_fc2, owns_fc2 = reduce_start([g_fc2], "fc2")
    dx1, dyc, dshift2, da2, dgate1, dg1post, g_fc1, g_out = _bwd_fc1_out(
        dp, dy, x1, mix, h2, ycat, mod6, n2pre, n1post, fc1_g, wout_g, state_fc2[0][0], ts_mid)
    state_mid, owns_mid = reduce_start([g_fc1, g_out], "mid")
    dz, dws, dbsp, dgain, dbias, dwp, dbp, dps = _mixer_bwd(z, dyc, *mixer, state_mid[0][0], ts_mid)
    grad_x, dshift1, da1, g_in = _bwd_in(dz, dx1, xs, h1, mod6, n1pre, win_g, state_mid[0][0], ts_mid)
    state_in, owns_in = reduce_start([g_in], "in")
    dmod6, dnorms = _mod_grads(da1, dshift1, dgate1, dg1post, da2, dshift2, dgate2, dg2post, mod6, n1pre, n2pre)

    parts = dict(b_ada=dmod6, norm1_pre=dnorms[0], norm1_post=dnorms[1], norm2_pre=dnorms[2], norm2_post=dnorms[3],
                 w_spatial=dws, b_spatial=dbsp, ln_v_gain=dgain, ln_v_bias=dbias, w_pool=dwp, b_pool=dbp,
                 pool_scale=dps)
    pieces = _pack(parts)
    slots = lax.dynamic_update_slice(jnp.zeros((N_DEV * _MOD_ROWS, 128), F32), pieces[0], (me * _MOD_ROWS, 0))
    total = _all_reduce_small(jnp.concatenate([slots] + pieces[1:], axis=0), state_in[0][0])
    reduce_finish(state_fc2, owns_fc2, ("w_fc2",), "fc2", total)
    reduce_finish(state_mid, owns_mid, ("w_fc1", "w_out"), "mid", deltas["w_fc2"])
    gp, dp, nmp, nvp = _small_adamw(total, jnp.concatenate(_pack(weights), axis=0),
                                    jnp.concatenate(_pack(m_old), axis=0), jnp.concatenate(_pack(v_old), axis=0))
    shapes = {n: weights[n].shape for n, _ in _SMALL}
    for res, packed in ((grads, gp), (deltas, dp), (new_m, nmp), (new_v, nvp)):
        res.update(_unpack(packed, shapes))

    dmod_all = total[:N_DEV * _MOD_ROWS].reshape(N_DEV, N_MOD * D_MODEL)
    cs = w_ada.shape[1]
    dmod_shard = lax.dynamic_slice(dmod_all, (0, chip * cs), (N_DEV, cs))
    sc_t = sc_all.reshape(N_DEV, D_MODEL).T
    grads["w_ada"], deltas["w_ada"], new_m["w_ada"], new_v["w_ada"] = _ada_grad_adamw(
        sc_t, dmod_shard, w_ada, m_w_ada, v_w_ada, 256)

    reduce_finish(state_in, owns_in, ("w_in",), "in", deltas["w_ada"])

    return (loss, grad_x[None], *[grads[n] for n in order], *[deltas[n] for n in order],
            *[new_m[n] for n in order], *[new_v[n] for n in order])
```

```python
import functools

import jax
import jax.numpy as jnp
from jax import lax
from jax.experimental import pallas as pl
from jax.experimental.pallas import tpu as pltpu

F32 = jnp.float32
BF16 = jnp.bfloat16
MESH = pl.DeviceIdType.MESH

D_MODEL = 1024
D_A = 512
D_B = 512
D_Z = 2 * D_A + D_B
N_HEADS = 4
HEAD_DIM = 128
CHUNK = 128
POOL_WINDOWS = (2, 4, 8, 16)
GROUP_DIM = 128
D_FF = 4096
N_MOD = 6
EPS = 1e-6
HALO = 16
N_CHIPS = 4
N_DEV = 8

ADAM_LR = 0.001
ADAM_B1 = 0.9
ADAM_B2 = 0.999
ADAM_EPS = 1e-08
ADAM_WD = 0.01
ADAM_STEP = 10

VMEM_LIMIT = 56 * 1024 * 1024

_VMEM = pl.BlockSpec(memory_space=pltpu.VMEM)
_ANY = pl.BlockSpec(memory_space=pl.ANY)


def _params(n_grid_axes=1):
    return pltpu.CompilerParams(dimension_semantics=("arbitrary",) * n_grid_axes, vmem_limit_bytes=VMEM_LIMIT)


def _rows(ts, width):
    return pl.BlockSpec((ts, width), lambda i: (i, 0))


def _const(shape):
    return pl.BlockSpec(shape, lambda i: (0,) * len(shape))


def _dot(a, b):
    return jnp.dot(a, b, preferred_element_type=F32)


def _dot_nt(a, b):
    return lax.dot_general(a, b, (((1,), (1,)), ((), ())), preferred_element_type=F32)


def _dot_tn(a, b):
    return lax.dot_general(a, b, (((0,), (0,)), ((), ())), preferred_element_type=F32)


def _rowmean(v):
    return jnp.mean(v, axis=-1, keepdims=True)


def _colsum(v):
    return jnp.sum(v, axis=0, keepdims=True)


def _gelu_parts(z):
    k0 = 0.7978845608028654
    k1 = 0.044715
    z2 = z * z
    t = jnp.tanh(k0 * (z + k1 * z * z2))
    g = 0.5 * z * (1.0 + t)
    dg = 0.5 * (1.0 + t) + 0.5 * z * (1.0 - t * t) * (k0 * (1.0 + 3.0 * k1 * z2))
    return g, dg


def _tril_weights(ws_ref):
    r = lax.broadcasted_iota(jnp.int32, (CHUNK, CHUNK), 0)
    s = lax.broadcasted_iota(jnp.int32, (CHUNK, CHUNK), 1)
    mask = (s <= r).astype(F32)
    return [(ws_ref[h] * mask).astype(BF16) for h in range(N_HEADS)]


def _window_counts(first_row, n_rows):
    pos = (first_row + lax.broadcasted_iota(jnp.int32, (n_rows, 1), 0)).astype(F32)
    return pos, [jnp.minimum(pos + 1.0, float(w)) for w in POOL_WINDOWS]


def _causal_window_sums(ext):
    out = []
    e = ext
    shift = 1
    for g in range(len(POOL_WINDOWS)):
        e = e + pltpu.roll(e, shift, 0)
        shift *= 2
        out.append(e[:, g * GROUP_DIM:(g + 1) * GROUP_DIM])
    return out


def _anticausal_window_sums(ext):
    n = ext.shape[0]
    out = []
    e = ext
    shift = 1
    for g in range(len(POOL_WINDOWS)):
        e = e + pltpu.roll(e, n - shift, 0)
        shift *= 2
        out.append(e[:, g * GROUP_DIM:(g + 1) * GROUP_DIM])
    return out


def _fwd_mix(x, mod6, n1pre, n1post, n2pre, win_g, wout_g, w_spatial, bsp_full, gain, bias, w_pool, b_pool, pool_scale, ts):
    s_len = x.shape[0]
    cz = D_Z // N_CHIPS
    rs = D_MODEL // N_CHIPS

    def body(x_ref, mod_ref, g1pre_ref, g1post_ref, g2pre_ref, win_ref, wout_ref, ws_ref, bsp_ref, gain_ref,
             bias_ref, wp_ref, bp_ref, ps_ref, h1_ref, z_ref, y_ref, mix_ref, x1_ref, h2_ref, mixed_ref, prev_ref):
        i = pl.program_id(0)
        _zero_on_first_step(prev_ref)
        xv = x_ref[...]
        r = lax.rsqrt(_rowmean(xv * xv) + EPS)
        hb = (((xv * r) * g1pre_ref[...]) * (1.0 + mod_ref[1:2, :]) + mod_ref[0:1, :]).astype(BF16)
        h1_ref[...] = hb
        for j in range(N_CHIPS):
            z_ref[:, j * cz:(j + 1) * cz] = _dot(hb, win_ref[j])

        wc = _tril_weights(ws_ref)
        u, _, _, _, _ = _mixer_forward_tile(z_ref[:, :2 * D_A], wc, bsp_ref, gain_ref[...], bias_ref[...], mixed_ref)
        y_ref[:, :D_A] = (u * mixed_ref[...]).astype(BF16)
        zb = z_ref[:, 2 * D_A:]
        sums = _causal_window_sums(jnp.concatenate([prev_ref[...], zb], axis=0))
        prev_ref[...] = zb[ts - HALO:, :]
        _, counts = _window_counts(i * ts, ts)
        for g in range(len(POOL_WINDOWS)):
            lanes = slice(g * GROUP_DIM, (g + 1) * GROUP_DIM)
            diff = sums[g][HALO:, :] / counts[g] - zb[:, lanes]
            lin = _dot(diff.astype(BF16), wp_ref[g].astype(BF16)) + bp_ref[:, lanes]
            y_ref[:, D_A + g * GROUP_DIM:D_A + (g + 1) * GROUP_DIM] = (lin * ps_ref[:, lanes]).astype(BF16)

        mix = None
        for j in range(N_CHIPS):
            part = _dot(y_ref[:, j * rs:(j + 1) * rs], wout_ref[j])
            mix = part if mix is None else mix + part
        mix_ref[...] = mix
        r2 = lax.rsqrt(_rowmean(mix * mix) + EPS)
        x1 = xv + mod_ref[2:3, :] * ((mix * r2) * g1post_ref[...])
        x1_ref[...] = x1
        r3 = lax.rsqrt(_rowmean(x1 * x1) + EPS)
        h2_ref[...] = (((x1 * r3) * g2pre_ref[...]) * (1.0 + mod_ref[4:5, :]) + mod_ref[3:4, :]).astype(BF16)

    vec = _const((1, D_MODEL))
    f32_rows = jax.ShapeDtypeStruct((s_len, D_MODEL), F32)
    bf16_rows = jax.ShapeDtypeStruct((s_len, D_MODEL), BF16)
    return pl.pallas_call(
        body, name="fwd_mix", grid=(s_len // ts,),
        in_specs=[_rows(ts, D_MODEL), _const((N_MOD, D_MODEL)), vec, vec, vec, _VMEM, _VMEM,
                  _const((N_HEADS, CHUNK, CHUNK)), _const((CHUNK, D_A)), _const((1, D_A)), _const((1, D_A)),
                  _const((N_HEADS, GROUP_DIM, GROUP_DIM)), _const((1, D_B)), _const((1, D_B))],
        out_specs=[_rows(ts, D_MODEL), _rows(ts, D_Z), _rows(ts, D_MODEL), _rows(ts, D_MODEL), _rows(ts, D_MODEL),
                   _rows(ts, D_MODEL)],
        out_shape=[bf16_rows, jax.ShapeDtypeStruct((s_len, D_Z), F32), bf16_rows, f32_rows, f32_rows, bf16_rows],
        scratch_shapes=[pltpu.VMEM((ts, D_A), F32), pltpu.VMEM((HALO, D_B), F32)],
        compiler_params=_params(),
    )(x, mod6, n1pre, n1post, n2pre, win_g, wout_g, w_spatial, bsp_full, gain, bias, w_pool, b_pool, pool_scale)


def _mixer_forward_tile(za, wc, bsp_ref, gain, bias, mixed_ref):
    ga, dga = _gelu_parts(za)
    u = ga[:, :D_A]
    v = ga[:, D_A:]
    mu = _rowmean(v)
    vc = v - mu
    rstd = lax.rsqrt(_rowmean(vc * vc) + EPS)
    vhat = vc * rstd
    vn = (vhat * gain + bias).astype(BF16)
    ts = za.shape[0]
    for k in range(ts // CHUNK):
        for h in range(N_HEADS):
            blk = vn[k * CHUNK:(k + 1) * CHUNK, h * HEAD_DIM:(h + 1) * HEAD_DIM]
            mixed_ref[k * CHUNK:(k + 1) * CHUNK, h * HEAD_DIM:(h + 1) * HEAD_DIM] = (
                _dot(wc[h], blk) + bsp_ref[:, h * HEAD_DIM:(h + 1) * HEAD_DIM])
    return u, vhat, rstd, vn, dga


def _fwd_fc1(h2, fc1_g, ts):
    s_len = h2.shape[0]
    cs = D_FF // N_CHIPS

    def body(h_ref, w_ref, q_ref):
        hb = h_ref[...]
        for j in range(N_CHIPS):
            p = jnp.maximum(_dot(hb, w_ref[j]), 0.0)
            q_ref[:, j * cs:(j + 1) * cs] = (p * p).astype(BF16)

    return pl.pallas_call(
        body, name="fwd_fc1", grid=(s_len // ts,),
        in_specs=[_rows(ts, D_MODEL), _VMEM],
        out_specs=_rows(ts, D_FF),
        out_shape=jax.ShapeDtypeStruct((s_len, D_FF), BF16),
        compiler_params=_params(),
    )(h2, fc1_g)


def _fwd_fc2_loss(q, x1, target, mod6, n2post, fc2_g, ts):
    s_len = q.shape[0]
    rs = D_FF // N_CHIPS

    def body(q_ref, x1_ref, t_ref, mod_ref, g_ref, w_ref, dy_ref, df_ref, loss_ref, dgate_ref, dg_ref):
        i = pl.program_id(0)

        @pl.when(i == 0)
        def _():
            loss_ref[...] = jnp.zeros_like(loss_ref)
            dgate_ref[...] = jnp.zeros_like(dgate_ref)
            dg_ref[...] = jnp.zeros_like(dg_ref)

        f = _dot(q_ref[:, 0:rs], w_ref[0])
        for j in range(1, N_CHIPS):
            f = f + _dot(q_ref[:, j * rs:(j + 1) * rs], w_ref[j])
        r4 = lax.rsqrt(_rowmean(f * f) + EPS)
        fh = f * r4
        gate = mod_ref[5:6, :]
        gn = g_ref[...]
        err = (x1_ref[...] + gate * (fh * gn)) - t_ref[...]
        loss_ref[...] += 0.5 * jnp.sum(_rowmean(err * err), axis=0, keepdims=True)
        dy = err * (1.0 / D_MODEL)
        dy_ref[...] = dy
        dgate_ref[...] += _colsum(dy * (fh * gn))
        dg_ref[...] += _colsum((dy * gate) * fh)
        gh = (dy * gate) * gn
        df_ref[...] = (r4 * (gh - fh * _rowmean(gh * fh))).astype(BF16)

    return pl.pallas_call(
        body, name="fwd_fc2_loss", grid=(s_len // ts,),
        in_specs=[_rows(ts, D_FF), _rows(ts, D_MODEL), _rows(ts, D_MODEL), _const((N_MOD, D_MODEL)),
                  _const((1, D_MODEL)), _VMEM],
        out_specs=[_rows(ts, D_MODEL), _rows(ts, D_MODEL), _const((1, 1)), _const((1, D_MODEL)), _const((1, D_MODEL))],
        out_shape=[jax.ShapeDtypeStruct((s_len, D_MODEL), F32), jax.ShapeDtypeStruct((s_len, D_MODEL), BF16),
                   jax.ShapeDtypeStruct((1, 1), F32), jax.ShapeDtypeStruct((1, D_MODEL), F32),
                   jax.ShapeDtypeStruct((1, D_MODEL), F32)],
        compiler_params=_params(),
    )(q, x1, target, mod6, n2post, fc2_g)


def _zero_on_first_step(*refs):
    @pl.when(pl.program_id(0) == 0)
    def _():
        for ref in refs:
            ref[...] = jnp.zeros_like(ref)


def _store_on_last_step(*pairs):
    @pl.when(pl.program_id(0) == pl.num_programs(0) - 1)
    def _():
        for acc_ref, hbm_ref in pairs:
            pltpu.sync_copy(acc_ref, hbm_ref)


def _bwd_fc2(df, q, fc2_g, ts):
    s_len = df.shape[0]
    cs = D_FF // N_CHIPS

    def body(df_ref, q_ref, w_ref, dp_ref, dw_hbm, dw_ref):
        _zero_on_first_step(dw_ref)
        dfb = df_ref[...]
        for j in range(N_CHIPS):
            qb = q_ref[:, j * cs:(j + 1) * cs]
            dw_ref[j] += _dot_tn(qb, dfb).reshape(2, cs // 2, D_MODEL)
            dq = _dot_nt(dfb, w_ref[j])
            dp_ref[:, j * cs:(j + 1) * cs] = (dq * (2.0 * jnp.sqrt(qb.astype(F32)))).astype(BF16)
        _store_on_last_step((dw_ref, dw_hbm))

    dw_shape = (N_CHIPS, 2, cs // 2, D_MODEL)
    return pl.pallas_call(
        body, name="bwd_fc2", grid=(s_len // ts,),
        in_specs=[_rows(ts, D_MODEL), _rows(ts, D_FF), _VMEM],
        out_specs=[_rows(ts, D_FF), _ANY],
        out_shape=[jax.ShapeDtypeStruct((s_len, D_FF), BF16), jax.ShapeDtypeStruct(dw_shape, F32)],
        scratch_shapes=[pltpu.VMEM(dw_shape, F32)],
        compiler_params=_params(),
    )(df, q, fc2_g)


def _bwd_fc1_out(dp, dy, x1, mix, h2, ycat, mod6, n2pre, n1post, fc1_g, wout_g, dep, ts):
    s_len = dp.shape[0]
    cs = D_FF // N_CHIPS
    rs = D_MODEL // N_CHIPS

    def body(dp_ref, dy_ref, x1_ref, mix_ref, h2_ref, yc_ref, mod_ref, g2_ref, g1_ref, w1_ref, wo_ref, dep_ref,
             dx1_ref, dyc_ref, dshift2_ref, da2_ref, dgate1_ref, dg1_ref, dw1_hbm, dwo_hbm, dw1_ref, dwo_ref):
        _zero_on_first_step(dshift2_ref, da2_ref, dgate1_ref, dg1_ref, dw1_ref, dwo_ref)
        h2b = h2_ref[...]
        dh2 = None
        for j in range(N_CHIPS):
            dpb = dp_ref[:, j * cs:(j + 1) * cs]
            dw1_ref[j] += _dot_tn(h2b, dpb).reshape(2, D_MODEL // 2, cs)
            part = _dot_nt(dpb, w1_ref[j])
            dh2 = part if dh2 is None else dh2 + part
        x1 = x1_ref[...]
        r3 = lax.rsqrt(_rowmean(x1 * x1) + EPS)
        xh = x1 * r3
        a2 = g2_ref[...] * (1.0 + mod_ref[4:5, :])
        dshift2_ref[...] += _colsum(dh2)
        da2_ref[...] += _colsum(dh2 * xh)
        dxh = dh2 * a2
        dx1 = dy_ref[...] + r3 * (dxh - xh * _rowmean(dxh * xh))
        dx1_ref[...] = dx1

        mix = mix_ref[...]
        r2 = lax.rsqrt(_rowmean(mix * mix) + EPS)
        mh = mix * r2
        gate = mod_ref[2:3, :]
        gn = g1_ref[...]
        dgate1_ref[...] += _colsum(dx1 * (mh * gn))
        dg1_ref[...] += _colsum((dx1 * gate) * mh)
        gh = (dx1 * gate) * gn
        dmix = (r2 * (gh - mh * _rowmean(gh * mh))).astype(BF16)
        dwo_ref[...] += _dot_tn(yc_ref[...], dmix).reshape(N_CHIPS, 2, rs // 2, D_MODEL)
        for j in range(N_CHIPS):
            dyc_ref[:, j * rs:(j + 1) * rs] = _dot_nt(dmix, wo_ref[j])
        _store_on_last_step((dw1_ref, dw1_hbm), (dwo_ref, dwo_hbm))

    vec = jax.ShapeDtypeStruct((1, D_MODEL), F32)
    dw1_shape = (N_CHIPS, 2, D_MODEL // 2, cs)
    dwo_shape = (N_CHIPS, 2, rs // 2, D_MODEL)
    return pl.pallas_call(
        body, name="bwd_fc1_out", grid=(s_len // ts,),
        in_specs=[_rows(ts, D_FF), _rows(ts, D_MODEL), _rows(ts, D_MODEL), _rows(ts, D_MODEL), _rows(ts, D_MODEL),
                  _rows(ts, D_MODEL), _const((N_MOD, D_MODEL)), _const((1, D_MODEL)), _const((1, D_MODEL)), _VMEM,
                  _VMEM, _ANY],
        out_specs=[_rows(ts, D_MODEL), _rows(ts, D_MODEL)] + [_const((1, D_MODEL))] * 4 + [_ANY, _ANY],
        out_shape=[jax.ShapeDtypeStruct((s_len, D_MODEL), F32), jax.ShapeDtypeStruct((s_len, D_MODEL), F32),
                   vec, vec, vec, vec, jax.ShapeDtypeStruct(dw1_shape, F32), jax.ShapeDtypeStruct(dwo_shape, F32)],
        scratch_shapes=[pltpu.VMEM(dw1_shape, F32), pltpu.VMEM(dwo_shape, F32)],
        compiler_params=_params(),
    )(dp, dy, x1, mix, h2, ycat, mod6, n2pre, n1post, fc1_g, wout_g, dep)


def _mixer_bwd(z, dyc, w_spatial, bsp_full, gain, bias, w_pool, b_pool, pool_scale, dep, ts):
    s_len = z.shape[0]
    nb = ts // HALO
    last = s_len // HALO - 1
    te = ts + HALO

    def body(z_ref, zprev_ref, znext_ref, dyc_ref, dynext_ref, ws_ref, bsp_ref, gain_ref, bias_ref, wp_ref, bp_ref,
             ps_ref, dep_ref, dz_ref, dws_ref, dbsp_ref, dgain_ref, dbias_ref, dwp_ref, dbp_ref, dps_ref, mixed_ref,
             dvn_ref):
        i = pl.program_id(0)

        @pl.when(i == 0)
        def _():
            for ref in (dws_ref, dbsp_ref, dgain_ref, dbias_ref, dwp_ref, dbp_ref, dps_ref):
                ref[...] = jnp.zeros_like(ref)

        wc = _tril_weights(ws_ref)
        gain = gain_ref[...]
        u, vhat, rstd, vn, dga = _mixer_forward_tile(z_ref[:, :2 * D_A], wc, bsp_ref, gain, bias_ref[...], mixed_ref)
        dya = dyc_ref[:, :D_A]
        du = dya * mixed_ref[...]
        dmixed = dya * u
        dmb = dmixed.astype(BF16)
        dm_sum = dmixed[0:CHUNK, :]
        for k in range(1, ts // CHUNK):
            dm_sum = dm_sum + dmixed[k * CHUNK:(k + 1) * CHUNK, :]
        r_idx = lax.broadcasted_iota(jnp.int32, (CHUNK, CHUNK), 0)
        s_idx = lax.broadcasted_iota(jnp.int32, (CHUNK, CHUNK), 1)
        causal = (s_idx <= r_idx).astype(F32)
        for h in range(N_HEADS):
            lanes = slice(h * HEAD_DIM, (h + 1) * HEAD_DIM)
            dbsp_ref[h] += jnp.sum(dm_sum[:, lanes], axis=1, keepdims=True)
            acc = None
            for k in range(ts // CHUNK):
                rows = slice(k * CHUNK, (k + 1) * CHUNK)
                t = _dot_nt(dmb[rows, lanes], vn[rows, lanes])
                acc = t if acc is None else acc + t
                dvn_ref[rows, lanes] = _dot_tn(wc[h], dmb[rows, lanes])
            dws_ref[h] += acc * causal
        dvn = dvn_ref[...]
        dgain_ref[...] += _colsum(dvn * vhat)
        dbias_ref[...] += _colsum(dvn)
        dvh = dvn * gain
        dv = rstd * (dvh - _rowmean(dvh) - vhat * _rowmean(dvh * vhat))
        dz_ref[:, :D_A] = (du * dga[:, :D_A]).astype(BF16)
        dz_ref[:, D_A:2 * D_A] = (dv * dga[:, D_A:]).astype(BF16)

        zb = z_ref[:, 2 * D_A:]
        prev = jnp.where(i == 0, 0.0, zprev_ref[...])
        zb_ext = jnp.concatenate([zb, znext_ref[...]], axis=0)
        sums = _causal_window_sums(jnp.concatenate([prev, zb_ext], axis=0))
        pos, counts = _window_counts(i * ts, te)
        dyb_ext = jnp.concatenate([dyc_ref[:, D_A:], dynext_ref[...]], axis=0)
        dlin_ext = dyb_ext * ps_ref[...]
        dbp_ref[...] += _colsum(dlin_ext[:ts, :])
        scaled = []
        ddiffs = []
        lins = []
        for g in range(len(POOL_WINDOWS)):
            lanes = slice(g * GROUP_DIM, (g + 1) * GROUP_DIM)
            diff = (sums[g][HALO:, :] / counts[g] - zb_ext[:, lanes]).astype(BF16)
            wpb = wp_ref[g].astype(BF16)
            dlb = dlin_ext[:, lanes].astype(BF16)
            lins.append(_dot(diff[:ts, :], wpb) + bp_ref[:, lanes])
            dwp_ref[g] += _dot_tn(diff[:ts, :], dlb[:ts, :])
            dd = _dot_nt(dlb, wpb)
            ddiffs.append(dd)
            scaled.append(jnp.where(pos < float(s_len), dd / counts[g], 0.0))
        dps_ref[...] += _colsum(dyb_ext[:ts, :] * jnp.concatenate(lins, axis=1))
        back = _anticausal_window_sums(jnp.concatenate(scaled, axis=1))
        for g in range(len(POOL_WINDOWS)):
            dz_ref[:, 2 * D_A + g * GROUP_DIM:2 * D_A + (g + 1) * GROUP_DIM] = (
                back[g][:ts, :] - ddiffs[g][:ts, :]).astype(BF16)

    sq = jax.ShapeDtypeStruct((N_HEADS, CHUNK, CHUNK), F32)
    vec = jax.ShapeDtypeStruct((1, D_A), F32)
    return pl.pallas_call(
        body, name="mixer_bwd", grid=(s_len // ts,),
        in_specs=[_rows(ts, D_Z),
                  pl.BlockSpec((HALO, D_B), lambda i: (jnp.maximum(i * nb - 1, 0), 2)),
                  pl.BlockSpec((HALO, D_B), lambda i: (jnp.minimum((i + 1) * nb, last), 2)),
                  _rows(ts, D_MODEL),
                  pl.BlockSpec((HALO, D_B), lambda i: (jnp.minimum((i + 1) * nb, last), 1)),
                  _const((N_HEADS, CHUNK, CHUNK)), _const((CHUNK, D_A)), _const((1, D_A)), _const((1, D_A)),
                  _const((N_HEADS, GROUP_DIM, GROUP_DIM)), _const((1, D_B)), _const((1, D_B)), _ANY],
        out_specs=[_rows(ts, D_Z), _const((N_HEADS, CHUNK, CHUNK)), _const((N_HEADS, CHUNK, 1)), _const((1, D_A)),
                   _const((1, D_A)), _const((N_HEADS, GROUP_DIM, GROUP_DIM)), _const((1, D_B)), _const((1, D_B))],
        out_shape=[jax.ShapeDtypeStruct((s_len, D_Z), BF16), sq, jax.ShapeDtypeStruct((N_HEADS, CHUNK, 1), F32), vec,
                   vec, sq, vec, vec],
        scratch_shapes=[pltpu.VMEM((ts, D_A), F32), pltpu.VMEM((ts, D_A), F32)],
        compiler_params=_params(),
    )(z, z, z, dyc, dyc, w_spatial, bsp_full, gain, bias, w_pool, b_pool, pool_scale, dep)


def _bwd_in(dz, dx1, x, h1, mod6, n1pre, win_g, dep, ts):
    s_len = x.shape[0]
    cs = D_Z // N_CHIPS

    def body(dz_ref, dx1_ref, x_ref, h1_ref, mod_ref, g_ref, w_ref, dep_ref, gx_ref, dshift_ref, da_ref, dw_hbm,
             dw_ref):
        _zero_on_first_step(dshift_ref, da_ref, dw_ref)
        h1b = h1_ref[...]
        dh = None
        for j in range(N_CHIPS):
            dzb = dz_ref[:, j * cs:(j + 1) * cs]
            dw_ref[j] += _dot_tn(h1b, dzb).reshape(2, D_MODEL // 2, cs)
            part = _dot_nt(dzb, w_ref[j])
            dh = part if dh is None else dh + part
        xv = x_ref[...]
        r = lax.rsqrt(_rowmean(xv * xv) + EPS)
        xh = xv * r
        a1 = g_ref[...] * (1.0 + mod_ref[1:2, :])
        dshift_ref[...] += _colsum(dh)
        da_ref[...] += _colsum(dh * xh)
        dxh = dh * a1
        gx_ref[...] = dx1_ref[...] + r * (dxh - xh * _rowmean(dxh * xh))
        _store_on_last_step((dw_ref, dw_hbm))

    vec = jax.ShapeDtypeStruct((1, D_MODEL), F32)
    dw_shape = (N_CHIPS, 2, D_MODEL // 2, cs)
    return pl.pallas_call(
        body, name="bwd_in", grid=(s_len // ts,),
        in_specs=[_rows(ts, D_Z), _rows(ts, D_MODEL), _rows(ts, D_MODEL), _rows(ts, D_MODEL),
                  _const((N_MOD, D_MODEL)), _const((1, D_MODEL)), _VMEM, _ANY],
        out_specs=[_rows(ts, D_MODEL), _const((1, D_MODEL)), _const((1, D_MODEL)), _ANY],
        out_shape=[jax.ShapeDtypeStruct((s_len, D_MODEL), F32), vec, vec, jax.ShapeDtypeStruct(dw_shape, F32)],
        scratch_shapes=[pltpu.VMEM(dw_shape, F32)],
        compiler_params=_params(),
    )(dz, dx1, x, h1, mod6, n1pre, win_g, dep)


def _adamw_math(w, g, m, v):
    m = ADAM_B1 * m + (1.0 - ADAM_B1) * g
    v = ADAM_B2 * v + (1.0 - ADAM_B2) * (g * g)
    m_hat = m / (1.0 - ADAM_B1 ** ADAM_STEP)
    v_hat = v / (1.0 - ADAM_B2 ** ADAM_STEP)
    delta = -ADAM_LR * (m_hat / (jnp.sqrt(v_hat) + ADAM_EPS) + ADAM_WD * w)
    return delta, m, v


def _adamw(g, w, m, v, name, tr):
    rows, cols = w.shape

    def body(g_ref, w_ref, m_ref, v_ref, d_ref, nm_ref, nv_ref):
        d, nm, nv = _adamw_math(w_ref[...], g_ref[...], m_ref[...], v_ref[...])
        d_ref[...] = d
        nm_ref[...] = nm
        nv_ref[...] = nv

    spec = _rows(tr, cols)
    shape = jax.ShapeDtypeStruct((rows, cols), F32)
    return pl.pallas_call(
        body, name=name, grid=(rows // tr,), in_specs=[spec] * 4, out_specs=[spec] * 3, out_shape=[shape] * 3,
        compiler_params=_params(),
    )(g, w, m, v)


def _ada_grad_adamw(sc_t, dmod_shard, w, m, v, tr):
    rows, cols = w.shape

    def body(s_ref, dm_ref, w_ref, m_ref, v_ref, g_ref, d_ref, nm_ref, nv_ref):
        g = s_ref[:, 0:1] * dm_ref[0:1, :]
        for b in range(1, N_DEV):
            g = g + s_ref[:, b:b + 1] * dm_ref[b:b + 1, :]
        g_ref[...] = g
        d, nm, nv = _adamw_math(w_ref[...], g, m_ref[...], v_ref[...])
        d_ref[...] = d
        nm_ref[...] = nm
        nv_ref[...] = nv

    spec = _rows(tr, cols)
    shape = jax.ShapeDtypeStruct((rows, cols), F32)
    return pl.pallas_call(
        body, name="ada_grad_adamw", grid=(rows // tr,),
        in_specs=[_rows(tr, N_DEV), _const((N_DEV, cols)), spec, spec, spec],
        out_specs=[spec] * 4, out_shape=[shape] * 4, compiler_params=_params(),
    )(sc_t, dmod_shard, w, m, v)


def _mod_grads(da1, dshift1, dgate1, dg1post, da2, dshift2, dgate2, dg2post, mod6, n1pre, n2pre):
    def body(da1_ref, ds1_ref, dgt1_ref, dg1_ref, da2_ref, ds2_ref, dgt2_ref, dg2_ref, mod_ref, n1_ref, n2_ref,
             dmod_ref, dn_ref):
        dmod_ref[0:1, :] = ds1_ref[...]
        dmod_ref[1:2, :] = da1_ref[...] * n1_ref[...]
        dmod_ref[2:3, :] = dgt1_ref[...]
        dmod_ref[3:4, :] = ds2_ref[...]
        dmod_ref[4:5, :] = da2_ref[...] * n2_ref[...]
        dmod_ref[5:6, :] = dgt2_ref[...]
        dn_ref[0:1, :] = da1_ref[...] * (1.0 + mod_ref[1:2, :])
        dn_ref[1:2, :] = dg1_ref[...]
        dn_ref[2:3, :] = da2_ref[...] * (1.0 + mod_ref[4:5, :])
        dn_ref[3:4, :] = dg2_ref[...]

    return pl.pallas_call(
        body, name="mod_grads",
        out_shape=[jax.ShapeDtypeStruct((N_MOD, D_MODEL), F32), jax.ShapeDtypeStruct((4, D_MODEL), F32)],
    )(da1, dshift1, dgate1, dg1post, da2, dshift2, dgate2, dg2post, mod6, n1pre, n2pre)


def _position():
    x, y, c = lax.axis_index("x"), lax.axis_index("y"), lax.axis_index("c")
    return x, y, c


def _flip(v, bit):
    return 1 - v if bit else v


def _peer(x, y, c, k):
    return (_flip(x, k & 4), _flip(y, k & 2), _flip(c, k & 1))


def _remote(src, dst, send_sem, recv_sem, device):
    return pltpu.make_async_remote_copy(src_ref=src, dst_ref=dst, send_sem=send_sem, recv_sem=recv_sem,
                                        device_id=device, device_id_type=MESH)


def _cast_to_slot(w, pos, name, tr):
    rows, cols = w.shape

    def body(pos_ref, w_ref, o_ref):
        o_ref[0] = w_ref[...].astype(BF16)

    return pl.pallas_call(
        body, name=name,
        grid_spec=pltpu.PrefetchScalarGridSpec(
            num_scalar_prefetch=1, grid=(rows // tr,),
            in_specs=[pl.BlockSpec((tr, cols), lambda i, pos: (i, 0))],
            out_specs=pl.BlockSpec((1, tr, cols), lambda i, pos: (pos[1], i, 0))),
        out_shape=jax.ShapeDtypeStruct((N_CHIPS, rows, cols), BF16), compiler_params=_params(),
    )(pos, w)


def _mod_exchange(c_row, w_ada_shard, b_ada_row):
    cs = w_ada_shard.shape[1]

    def body(c_ref, w_ref, b_ref, mod_ref, sc_ref, rows_ref, send1, recv1, send2, recv2):
        x, y, c = _position()
        me = 4 * x + 2 * y + c
        chip = 2 * x + y
        cv = c_ref[...]
        sc_ref[me] = cv * jax.nn.sigmoid(cv)
        gather = [_remote(sc_ref.at[me], sc_ref.at[me], send1.at[k - 1], recv1.at[k - 1], _peer(x, y, c, k))
                  for k in range(1, N_DEV)]
        for cp in gather:
            cp.start()
        for k in range(1, N_DEV):
            px, py, pc = _peer(x, y, c, k)
            src = 4 * px + 2 * py + pc
            _remote(sc_ref.at[src], sc_ref.at[src], send1.at[k - 1], recv1.at[k - 1], (px, py, pc)).wait_recv()
        for cp in gather:
            cp.wait_send()
        sc_all = jnp.concatenate([sc_ref[b] for b in range(N_DEV)], axis=0)
        part = jnp.dot(sc_all, w_ref[...], precision=lax.Precision.HIGHEST, preferred_element_type=F32)
        part = part + b_ref[:, pl.ds(pl.multiple_of(chip * cs, 128), cs)]
        for b in range(N_DEV):
            rows_ref[b] = part[b:b + 1, :]
        mod_ref[chip] = rows_ref[me]
        hand = []
        for k in (2, 4, 6):
            px, py, _ = _peer(x, y, c, k)
            hand.append(_remote(rows_ref.at[4 * px + 2 * py + c], mod_ref.at[chip], send2.at[k // 2 - 1],
                                recv2.at[k // 2 - 1], (px, py, c)))
        for cp in hand:
            cp.start()
        for k in (2, 4, 6):
            px, py, _ = _peer(x, y, c, k)
            pchip = 2 * px + py
            _remote(rows_ref.at[me], mod_ref.at[pchip], send2.at[k // 2 - 1], recv2.at[k // 2 - 1],
                    (px, py, c)).wait_recv()
        for cp in hand:
            cp.wait_send()

    return pl.pallas_call(
        body, name="mod_exchange",
        in_specs=[_VMEM, _VMEM, _VMEM], out_specs=[_VMEM, _VMEM],
        out_shape=[jax.ShapeDtypeStruct((N_CHIPS, 1, cs), F32), jax.ShapeDtypeStruct((N_DEV, 1, D_MODEL), F32)],
        scratch_shapes=[pltpu.VMEM((N_DEV, 1, cs), F32), pltpu.SemaphoreType.DMA((N_DEV - 1,)),
                        pltpu.SemaphoreType.DMA((N_DEV - 1,)), pltpu.SemaphoreType.DMA((N_CHIPS - 1,)),
                        pltpu.SemaphoreType.DMA((N_CHIPS - 1,))],
        compiler_params=pltpu.CompilerParams(vmem_limit_bytes=VMEM_LIMIT),
    )(c_row, w_ada_shard, b_ada_row)


_HBM = pl.BlockSpec(memory_space=pltpu.HBM)
_SEM = pl.BlockSpec(memory_space=pltpu.SEMAPHORE)
_EFFECT = pltpu.SideEffectType.DATAFLOW_SIDE_EFFECTING
_CHIP_HOPS = (2, 4, 6)


def _in_hbm(a):
    return pltpu.with_memory_space_constraint(a, pltpu.HBM)


def _sems3():
    return pltpu.SemaphoreType.DMA((len(_CHIP_HOPS),))


def _ag_start(lands, after):
    n = len(lands)

    def body(*refs):
        zones = refs[:n]
        sends, recvs = refs[n + 1:2 * n + 1], refs[2 * n + 1:3 * n + 1]
        x, y, c = _position()
        chip = 2 * x + y
        for i in range(n):
            half = zones[i].shape[1] // 2
            mine = zones[i].at[chip, pl.ds(c * half, half)]
            for s, k in enumerate(_CHIP_HOPS):
                px, py, _ = _peer(x, y, c, k)
                _remote(mine, mine, sends[i].at[s], recvs[i].at[s], (px, py, c)).start()

    out = pl.pallas_call(
        body, name="ag_start",
        in_specs=[_HBM] * n + [_ANY],
        out_specs=[_SEM] * (2 * n) + [_HBM] * n,
        out_shape=[_sems3()] * (2 * n) + [pltpu.HBM(z.shape, BF16) for z in lands],
        input_output_aliases={i: 2 * n + i for i in range(n)},
        compiler_params=pltpu.CompilerParams(has_side_effects=_EFFECT),
    )(*[_in_hbm(z) for z in lands], after)
    return [(out[2 * n + i], out[i], out[n + i]) for i in range(n)]


def _ag_pass(group, after, name):
    n = len(group)

    def body(*refs):
        zones = refs[:n]
        sends, recvs = refs[n:2 * n], refs[2 * n:3 * n]
        fsends, frecvs = refs[4 * n + 1:5 * n + 1], refs[5 * n + 1:6 * n + 1]
        x, y, c = _position()
        chip = 2 * x + y
        for i in range(n):
            half = zones[i].shape[1] // 2
            rows = pl.ds(c * half, half)
            for s, k in enumerate(_CHIP_HOPS):
                px, py, _ = _peer(x, y, c, k)
                landed = zones[i].at[2 * px + py, rows]
                _remote(landed, landed, sends[i].at[s], recvs[i].at[s], (px, py, c)).wait_recv()
                _remote(landed, landed, fsends[i].at[s], frecvs[i].at[s], (x, y, 1 - c)).start()
        for i in range(n):
            half = zones[i].shape[1] // 2
            mine = zones[i].at[chip, pl.ds(c * half, half)]
            for s, k in enumerate(_CHIP_HOPS):
                px, py, _ = _peer(x, y, c, k)
                _remote(mine, mine, sends[i].at[s], recvs[i].at[s], (px, py, c)).wait_send()

    out = pl.pallas_call(
        body, name=name,
        in_specs=[_HBM] * n + [_SEM] * (2 * n) + [_ANY],
        out_specs=[_HBM] * n + [_SEM] * (2 * n),
        out_shape=[pltpu.HBM(g[0].shape, BF16) for g in group] + [_sems3()] * (2 * n),
        input_output_aliases={i: i for i in range(n)},
        compiler_params=pltpu.CompilerParams(has_side_effects=_EFFECT),
    )(*[g[0] for g in group], *[g[1] for g in group], *[g[2] for g in group], after)
    return [(out[i], out[n + i], out[2 * n + i]) for i in range(n)]


def _ag_done(group, name):
    n = len(group)

    def body(*refs):
        lands = refs[:n]
        fsends, frecvs = refs[n:2 * n], refs[2 * n:3 * n]
        x, y, c = _position()
        for i in range(n):
            half = lands[i].shape[1] // 2
            for s, k in enumerate(_CHIP_HOPS):
                px, py, _ = _peer(x, y, c, k)
                sent = lands[i].at[2 * px + py, pl.ds(c * half, half)]
                got = lands[i].at[2 * px + py, pl.ds((1 - c) * half, half)]
                cp = _remote(sent, got, fsends[i].at[s], frecvs[i].at[s], (x, y, 1 - c))
                cp.wait_recv()
                cp.wait_send()

    out = pl.pallas_call(
        body, name=name,
        in_specs=[_HBM] * n + [_SEM] * (2 * n),
        out_specs=[_HBM] * n,
        out_shape=[pltpu.HBM(g[0].shape, BF16) for g in group],
        input_output_aliases={i: i for i in range(n)},
        compiler_params=pltpu.CompilerParams(has_side_effects=_EFFECT),
    )(*[g[0] for g in group], *[g[1] for g in group], *[g[2] for g in group])
    return list(out)


def _small_pair_sum(v, dep):
    rows = v.shape[0]

    def body(v_ref, dep_ref, slots_ref, sib_ref, send, recv):
        x, y, c = _position()
        pair = _remote(v_ref, sib_ref, send, recv, (x, y, 1 - c))
        pair.start()
        pair.wait()
        slots_ref[2 * x + y] = v_ref[...] + sib_ref[...]

    return pl.pallas_call(
        body, name="small_pair_sum",
        in_specs=[_VMEM, _ANY], out_specs=_VMEM, out_shape=jax.ShapeDtypeStruct((N_CHIPS, rows, 128), F32),
        scratch_shapes=[pltpu.VMEM((rows, 128), F32), pltpu.SemaphoreType.DMA, pltpu.SemaphoreType.DMA],
        compiler_params=pltpu.CompilerParams(vmem_limit_bytes=VMEM_LIMIT),
    )(v, dep)


def _small_spread_start(slots):
    def body(z_ref, sends, recvs, z_out):
        x, y, c = _position()
        mine = z_ref.at[2 * x + y]
        for s, k in enumerate(_CHIP_HOPS):
            px, py, _ = _peer(x, y, c, k)
            _remote(mine, mine, sends.at[s], recvs.at[s], (px, py, c)).start()

    sends, recvs, out = pl.pallas_call(
        body, name="small_spread_start",
        in_specs=[_HBM], out_specs=[_SEM, _SEM, _HBM],
        out_shape=[_sems3(), _sems3(), pltpu.HBM(slots.shape, F32)],
        input_output_aliases={0: 2},
        compiler_params=pltpu.CompilerParams(has_side_effects=_EFFECT),
    )(_in_hbm(slots))
    return out, sends, recvs


def _small_spread_wait(slots, sends, recvs, after):
    def body(z_ref, sends, recvs, after_ref, z_out):
        x, y, c = _position()
        mine = z_ref.at[2 * x + y]
        for s, k in enumerate(_CHIP_HOPS):
            px, py, _ = _peer(x, y, c, k)
            cp = _remote(mine, z_ref.at[2 * px + py], sends.at[s], recvs.at[s], (px, py, c))
            cp.wait_recv()
            cp.wait_send()

    return pl.pallas_call(
        body, name="small_spread_wait",
        in_specs=[_HBM, _SEM, _SEM, _ANY], out_specs=_HBM, out_shape=pltpu.HBM(slots.shape, F32),
        input_output_aliases={0: 0},
        compiler_params=pltpu.CompilerParams(has_side_effects=_EFFECT),
    )(slots, sends, recvs, after)


def _sibling_sum(pgs, name):
    n = len(pgs)
    units = [(i, j) for i in range(n) for j in range(N_CHIPS)]

    def body(*refs):
        ins, qbs, owns = refs[:n], refs[n:2 * n], refs[2 * n:3 * n]
        mine, other, stage, got = refs[3 * n:4 * n], refs[4 * n:5 * n], refs[5 * n:6 * n], refs[6 * n:7 * n]
        load_a, load_b, send, recv = refs[7 * n:]
        x, y, c = _position()
        chip = 2 * x + y
        loads_a = [pltpu.make_async_copy(ins[i].at[j, 1 - c], other[i].at[j], load_a.at[u])
                   for u, (i, j) in enumerate(units)]
        loads_b = [pltpu.make_async_copy(ins[i].at[j, c], mine[i].at[j], load_b.at[u])
                   for u, (i, j) in enumerate(units)]
        for cp in loads_a + loads_b:
            cp.start()
        sent = []
        for u, (i, j) in enumerate(units):
            loads_a[u].wait()
            stage[i][j] = other[i][j].astype(BF16)
            cp = _remote(stage[i].at[j], got[i].at[j], send.at[u], recv.at[u], (x, y, 1 - c))
            cp.start()
            sent.append(cp)
        for u, (i, j) in enumerate(units):
            loads_b[u].wait()
            sent[u].wait_recv()
            q = mine[i][j] + got[i][j].astype(F32)
            mine[i][j] = q
            qbs[i][j] = q.astype(BF16)
        for i in range(n):
            owns[i][...] = mine[i][chip]
        for cp in sent:
            cp.wait_send()

    wire = [(N_CHIPS,) + p.shape[2:] for p in pgs]
    out = pl.pallas_call(
        body, name=name, in_specs=[_ANY] * n, out_specs=[_VMEM] * (2 * n),
        out_shape=[jax.ShapeDtypeStruct(w, BF16) for w in wire] + [jax.ShapeDtypeStruct(w[1:], F32) for w in wire],
        scratch_shapes=[pltpu.VMEM(w, F32) for w in wire] * 2 + [pltpu.VMEM(w, BF16) for w in wire] * 2
        + [pltpu.SemaphoreType.DMA((len(units),))] * 4,
        compiler_params=pltpu.CompilerParams(vmem_limit_bytes=VMEM_LIMIT),
    )(*pgs)
    return list(out[:n]), list(out[n:])


def _rs_start(qbs, name):
    n = len(qbs)

    def body(*refs):
        outs, inboxes = refs[:n], refs[n:2 * n]
        sends, recvs = refs[2 * n:3 * n], refs[3 * n:4 * n]
        x, y, c = _position()
        chip = 2 * x + y
        for i in range(n):
            for s, k in enumerate(_CHIP_HOPS):
                px, py, _ = _peer(x, y, c, k)
                _remote(outs[i].at[2 * px + py], inboxes[i].at[chip], sends[i].at[s], recvs[i].at[s], (px, py, c)).start()

    inboxes = [_in_hbm(lax.empty(q.shape, BF16)) for q in qbs]
    out = pl.pallas_call(
        body, name=name,
        in_specs=[_HBM] * (2 * n),
        out_specs=[_SEM] * (2 * n) + [_HBM] * (2 * n),
        out_shape=[_sems3()] * (2 * n) + [pltpu.HBM(q.shape, BF16) for q in qbs] * 2,
        input_output_aliases={i: 2 * n + i for i in range(2 * n)},
        compiler_params=pltpu.CompilerParams(has_side_effects=_EFFECT),
    )(*[_in_hbm(q) for q in qbs], *inboxes)
    return [(out[2 * n + i], out[3 * n + i], out[i], out[n + i]) for i in range(n)]


def _rs_wait(group, after, name):
    n = len(group)

    def body(*refs):
        outs, inboxes = refs[:n], refs[n:2 * n]
        sends, recvs = refs[2 * n:3 * n], refs[3 * n:4 * n]
        x, y, c = _position()
        for i in range(n):
            for s, k in enumerate(_CHIP_HOPS):
                px, py, _ = _peer(x, y, c, k)
                slot = 2 * px + py
                cp = _remote(outs[i].at[slot], inboxes[i].at[slot], sends[i].at[s], recvs[i].at[s], (px, py, c))
                cp.wait_recv()
                cp.wait_send()

    out = pl.pallas_call(
        body, name=name,
        in_specs=[_HBM] * (2 * n) + [_SEM] * (2 * n) + [_ANY],
        out_specs=[_HBM] * n,
        out_shape=[pltpu.HBM(g[1].shape, BF16) for g in group],
        input_output_aliases={n + i: i for i in range(n)},
        compiler_params=pltpu.CompilerParams(has_side_effects=_EFFECT),
    )(*[g[0] for g in group], *[g[1] for g in group], *[g[2] for g in group], *[g[3] for g in group], after)
    return list(out)


def _final_add(inbox, own, pos, name, tr):
    _, r, cdim = inbox.shape

    def body(pos_ref, a_ref, b_ref, c_ref, own_ref, o_ref):
        total = (a_ref[0].astype(F32) + b_ref[0].astype(F32)) + c_ref[0].astype(F32)
        o_ref[0] = total + own_ref[...]

    def slot(flip):
        return pl.BlockSpec((1, tr, cdim), lambda i, pos: (jnp.bitwise_xor(pos[1], flip), i, 0))

    return pl.pallas_call(
        body, name=name,
        grid_spec=pltpu.PrefetchScalarGridSpec(
            num_scalar_prefetch=1, grid=(r // tr,),
            in_specs=[slot(1), slot(2), slot(3), pl.BlockSpec((tr, cdim), lambda i, pos: (i, 0))],
            out_specs=pl.BlockSpec((1, tr, cdim), lambda i, pos: (pos[0], i, 0))),
        out_shape=jax.ShapeDtypeStruct((2, r, cdim), F32), compiler_params=_params(),
    )(pos, inbox, inbox, inbox, own)


def _sibling_share(shards, name):
    n = len(shards)

    def body(*refs):
        outs = refs[n:2 * n]
        send, recv = refs[2 * n:]
        x, y, c = _position()
        copies = [_remote(outs[i].at[c], outs[i].at[c], send.at[i], recv.at[i], (x, y, 1 - c)) for i in range(n)]
        for cp in copies:
            cp.start()
        for i in range(n):
            theirs = outs[i].at[1 - c]
            _remote(theirs, theirs, send.at[i], recv.at[i], (x, y, 1 - c)).wait_recv()
        for cp in copies:
            cp.wait_send()

    return pl.pallas_call(
        body, name=name, in_specs=[_ANY] * n, out_specs=[_ANY] * n,
        out_shape=[jax.ShapeDtypeStruct(s.shape, F32) for s in shards],
        input_output_aliases={i: i for i in range(n)},
        scratch_shapes=[pltpu.SemaphoreType.DMA((n,))] * 2,
    )(*shards)


_SMALL = (("b_ada", N_MOD * D_MODEL), ("norm1_pre", D_MODEL), ("norm1_post", D_MODEL), ("norm2_pre", D_MODEL),
          ("norm2_post", D_MODEL), ("w_spatial", N_HEADS * CHUNK * CHUNK), ("b_spatial", N_HEADS * CHUNK),
          ("ln_v_gain", D_A), ("ln_v_bias", D_A), ("w_pool", N_HEADS * GROUP_DIM * GROUP_DIM),
          ("b_pool", D_B), ("pool_scale", D_B))
_MOD_ROWS = N_MOD * D_MODEL // 128


def _packed_rows(size):
    return -(-(size // 128) // 8) * 8


def _pack(parts):
    out = []
    for name, size in _SMALL:
        a = parts[name].reshape(size // 128, 128)
        pad = _packed_rows(size) - a.shape[0]
        out.append(jnp.pad(a, ((0, pad), (0, 0))) if pad else a)
    return out


def _unpack(packed, shapes):
    res = {}
    row = 0
    for name, size in _SMALL:
        res[name] = packed[row:row + size // 128].reshape(shapes[name])
        row += _packed_rows(size)
    return res


def _small_adamw(slots, wp, mp, vp):
    rows = wp.shape[0]
    head = N_DEV * _MOD_ROWS

    def body(s_ref, w_ref, m_ref, v_ref, g_ref, d_ref, nm_ref, nv_ref, dmod_ref, loss_ref, t_ref):
        t_ref[...] = ((s_ref[0] + s_ref[1]) + s_ref[2]) + s_ref[3]
        dmod_ref[...] = t_ref[0:head, :]
        loss_ref[...] = t_ref[head + rows - _MOD_ROWS:, :]
        gb = t_ref[0:_MOD_ROWS, :]
        for b in range(1, N_DEV):
            gb = gb + t_ref[b * _MOD_ROWS:(b + 1) * _MOD_ROWS, :]
        g_ref[0:_MOD_ROWS, :] = gb
        g_ref[_MOD_ROWS:, :] = t_ref[head:head + rows - _MOD_ROWS, :]
        d, nm, nv = _adamw_math(w_ref[...], g_ref[...], m_ref[...], v_ref[...])
        d_ref[...] = d
        nm_ref[...] = nm
        nv_ref[...] = nv

    return pl.pallas_call(
        body, name="small_adamw",
        out_shape=[jax.ShapeDtypeStruct((rows, 128), F32)] * 4
        + [jax.ShapeDtypeStruct((head, 128), F32), jax.ShapeDtypeStruct((8, 128), F32)],
        scratch_shapes=[pltpu.VMEM(slots.shape[1:], F32)],
        compiler_params=pltpu.CompilerParams(vmem_limit_bytes=VMEM_LIMIT),
    )(slots, wp, mp, vp)


def kernel(x, c, w_ada, b_ada, norm1_pre, norm1_post, w_in, w_spatial, b_spatial, ln_v_gain, ln_v_bias, w_pool, b_pool, pool_scale, w_out, norm2_pre, norm2_post, w_fc1, w_fc2, loss_target, m_w_ada, m_b_ada, m_norm1_pre, m_norm1_post, m_w_in, m_w_spatial, m_b_spatial, m_ln_v_gain, m_ln_v_bias, m_w_pool, m_b_pool, m_pool_scale, m_w_out, m_norm2_pre, m_norm2_post, m_w_fc1, m_w_fc2, v_w_ada, v_b_ada, v_norm1_pre, v_norm1_post, v_w_in, v_w_spatial, v_b_spatial, v_ln_v_gain, v_ln_v_bias, v_w_pool, v_b_pool, v_pool_scale, v_w_out, v_norm2_pre, v_norm2_post, v_w_fc1, v_w_fc2):
    weights = dict(w_ada=w_ada, b_ada=b_ada, norm1_pre=norm1_pre, norm1_post=norm1_post, w_in=w_in,
                   w_spatial=w_spatial, b_spatial=b_spatial, ln_v_gain=ln_v_gain, ln_v_bias=ln_v_bias, w_pool=w_pool,
                   b_pool=b_pool, pool_scale=pool_scale, w_out=w_out, norm2_pre=norm2_pre, norm2_post=norm2_post,
                   w_fc1=w_fc1, w_fc2=w_fc2)
    m_old = dict(w_ada=m_w_ada, b_ada=m_b_ada, norm1_pre=m_norm1_pre, norm1_post=m_norm1_post, w_in=m_w_in,
                 w_spatial=m_w_spatial, b_spatial=m_b_spatial, ln_v_gain=m_ln_v_gain, ln_v_bias=m_ln_v_bias,
                 w_pool=m_w_pool, b_pool=m_b_pool, pool_scale=m_pool_scale, w_out=m_w_out, norm2_pre=m_norm2_pre,
                 norm2_post=m_norm2_post, w_fc1=m_w_fc1, w_fc2=m_w_fc2)
    v_old = dict(w_ada=v_w_ada, b_ada=v_b_ada, norm1_pre=v_norm1_pre, norm1_post=v_norm1_post, w_in=v_w_in,
                 w_spatial=v_w_spatial, b_spatial=v_b_spatial, ln_v_gain=v_ln_v_gain, ln_v_bias=v_ln_v_bias,
                 w_pool=v_w_pool, b_pool=v_b_pool, pool_scale=v_pool_scale, w_out=v_w_out, norm2_pre=v_norm2_pre,
                 norm2_post=v_norm2_post, w_fc1=v_w_fc1, w_fc2=v_w_fc2)
    order = ("w_ada", "b_ada", "norm1_pre", "norm1_post", "w_in", "w_spatial", "b_spatial", "ln_v_gain", "ln_v_bias",
             "w_pool", "b_pool", "pool_scale", "w_out", "norm2_pre", "norm2_post", "w_fc1", "w_fc2")
    big = ("w_in", "w_out", "w_fc1", "w_fc2")
    mx, my, mc = _position()
    me = 4 * mx + 2 * my + mc
    chip = 2 * mx + my
    row = lambda a: a.reshape(1, -1)

    pos = jnp.stack([mc, chip]).astype(jnp.int32)
    xs, target = x[0], loss_target[0]
    n1pre, n1post, n2pre, n2post = row(norm1_pre), row(norm1_post), row(norm2_pre), row(norm2_post)
    mixer = (w_spatial, jnp.repeat(b_spatial.T, HEAD_DIM, axis=1), row(ln_v_gain), row(ln_v_bias), w_pool,
             row(b_pool), row(pool_scale))
    ts_big, ts_mid = 512, 256

    mod4, sc_all = _mod_exchange(c, w_ada, row(b_ada))
    mod6 = mod4.reshape(N_MOD, D_MODEL)
    ag = _ag_start([_cast_to_slot(weights[n], pos, "cast_" + n, 256) for n in big], mod4)

    win_g, wout_g = _ag_done(_ag_pass([ag[0], ag[1]], mod4, "ag_pass_mix"), "ag_done_mix")
    h1, z, ycat, mix, x1, h2 = _fwd_mix(xs, mod6, n1pre, n1post, n2pre, win_g, wout_g, *mixer, ts_mid)
    (fc1_g,) = _ag_done(_ag_pass([ag[2]], h2, "ag_pass_fc1"), "ag_done_fc1")
    q = _fwd_fc1(h2, fc1_g, ts_big)
    (fc2_g,) = _ag_done(_ag_pass([ag[3]], q, "ag_pass_fc2"), "ag_done_fc2")
    dy, df, loss, dgate2, dg2post = _fwd_fc2_loss(q, x1, target, mod6, n2post, fc2_g, ts_big)

    def reduce_start(partials, tag):
        wire, owns = _sibling_sum(partials, "sibling_sum_" + tag)
        return _rs_start(wire, "rs_start_" + tag), owns

    def reduce_finish(state, owns, names, tag, dep):
        inboxes = _rs_wait(state, dep, "rs_wait_" + tag)
        halves = [_final_add(ib, own, pos, "final_add_" + n, min(256, own.shape[0]))
                  for ib, own, n in zip(inboxes, owns, names)]
        shards = _sibling_share(halves, "sibling_share_" + tag)
        for n, g in zip(names, shards):
            g = g.reshape(weights[n].shape)
            grads[n] = g
            deltas[n], new_m[n], new_v[n] = _adamw(g, weights[n], m_old[n], v_old[n], "adamw_" + n, 256)

    grads, deltas, new_m, new_v = {}, {}, {}, {}
    dp, g_fc2 = _bwd_fc2(df, q, fc2_g, ts_big)
    state_fc2, owns_fc2 = reduce_start([g_fc2], "fc2")
    dx1, dyc, dshift2, da2, dgate1, dg1post, g_fc1, g_out = _bwd_fc1_out(
        dp, dy, x1, mix, h2, ycat, mod6, n2pre, n1post, fc1_g, wout_g, state_fc2[0][0], ts_mid)
    state_mid, owns_mid = reduce_start([g_fc1, g_out], "mid")
    dz, dws, dbsp, dgain, dbias, dwp, dbp, dps = _mixer_bwd(z, dyc, *mixer, state_mid[0][0], ts_mid)
    grad_x, dshift1, da1, g_in = _bwd_in(dz, dx1, xs, h1, mod6, n1pre, win_g, state_mid[0][0], ts_mid)
    state_in, owns_in = reduce_start([g_in], "in")
    dmod6, dnorms = _mod_grads(da1, dshift1, dgate1, dg1post, da2, dshift2, dgate2, dg2post, mod6, n1pre, n2pre)

    parts = dict(b_ada=dmod6, norm1_pre=dnorms[0], norm1_post=dnorms[1], norm2_pre=dnorms[2], norm2_post=dnorms[3],
                 w_spatial=dws, b_spatial=dbsp, ln_v_gain=dgain, ln_v_bias=dbias, w_pool=dwp, b_pool=dbp,
                 pool_scale=dps)
    pieces = _pack(parts)
    slots = lax.dynamic_update_slice(jnp.zeros((N_DEV * _MOD_ROWS, 128), F32), pieces[0], (me * _MOD_ROWS, 0))
    loss_tile = jnp.pad(loss, ((0, 7), (0, 127)))
    spread = _small_spread_start(_small_pair_sum(jnp.concatenate([slots] + pieces[1:] + [loss_tile], axis=0),
                                                 state_in[0][0]))
    reduce_finish(state_fc2, owns_fc2, ("w_fc2",), "fc2", spread[0])
    reduce_finish(state_mid, owns_mid, ("w_fc1", "w_out"), "mid", deltas["w_fc2"])
    gp, dp, nmp, nvp, dmod_all, loss_tile = _small_adamw(
        _small_spread_wait(*spread, deltas["w_fc1"]), jnp.concatenate(_pack(weights), axis=0),
        jnp.concatenate(_pack(m_old), axis=0), jnp.concatenate(_pack(v_old), axis=0))
    loss = loss_tile[0, 0]
    shapes = {n: weights[n].shape for n, _ in _SMALL}
    for res, packed in ((grads, gp), (deltas, dp), (new_m, nmp), (new_v, nvp)):
        res.update(_unpack(packed, shapes))

    dmod_all = dmod_all.reshape(N_DEV, N_MOD * D_MODEL)
    cs = w_ada.shape[1]
    dmod_shard = lax.dynamic_slice(dmod_all, (0, chip * cs), (N_DEV, cs))
    sc_t = sc_all.reshape(N_DEV, D_MODEL).T
    grads["w_ada"], deltas["w_ada"], new_m["w_ada"], new_v["w_ada"] = _ada_grad_adamw(
        sc_t, dmod_shard, w_ada, m_w_ada, v_w_ada, 256)

    reduce_finish(state_in, owns_in, ("w_in",), "in", deltas["w_ada"])

    return (loss, grad_x[None], *[grads[n] for n in order], *[deltas[n] for n in order],
            *[new_m[n] for n in order], *[new_v[n] for n in order])
```

```python
import functools

import jax
import jax.numpy as jnp
from jax import lax
from jax.experimental import pallas as pl
from jax.experimental.pallas import tpu as pltpu

F32 = jnp.float32
BF16 = jnp.bfloat16
MESH = pl.DeviceIdType.MESH

D_MODEL = 1024
D_A = 512
D_B = 512
D_Z = 2 * D_A + D_B
N_HEADS = 4
HEAD_DIM = 128
CHUNK = 128
POOL_WINDOWS = (2, 4, 8, 16)
GROUP_DIM = 128
D_FF = 4096
N_MOD = 6
EPS = 1e-6
HALO = 16
N_CHIPS = 4
N_DEV = 8

ADAM_LR = 0.001
ADAM_B1 = 0.9
ADAM_B2 = 0.999
ADAM_EPS = 1e-08
ADAM_WD = 0.01
ADAM_STEP = 10

VMEM_LIMIT = 56 * 1024 * 1024

_VMEM = pl.BlockSpec(memory_space=pltpu.VMEM)
_ANY = pl.BlockSpec(memory_space=pl.ANY)


def _params(n_grid_axes=1):
    return pltpu.CompilerParams(dimension_semantics=("arbitrary",) * n_grid_axes, vmem_limit_bytes=VMEM_LIMIT)


def _rows(ts, width):
    return pl.BlockSpec((ts, width), lambda i: (i, 0))


def _const(shape):
    return pl.BlockSpec(shape, lambda i: (0,) * len(shape))


def _dot(a, b):
    return jnp.dot(a, b, preferred_element_type=F32)


def _dot_nt(a, b):
    return lax.dot_general(a, b, (((1,), (1,)), ((), ())), preferred_element_type=F32)


def _dot_tn(a, b):
    return lax.dot_general(a, b, (((0,), (0,)), ((), ())), preferred_element_type=F32)


def _rowmean(v):
    return jnp.mean(v, axis=-1, keepdims=True)


def _colsum(v):
    return jnp.sum(v, axis=0, keepdims=True)


def _gelu_parts(z):
    k0 = 0.7978845608028654
    k1 = 0.044715
    z2 = z * z
    t = jnp.tanh(k0 * (z + k1 * z * z2))
    g = 0.5 * z * (1.0 + t)
    dg = 0.5 * (1.0 + t) + 0.5 * z * (1.0 - t * t) * (k0 * (1.0 + 3.0 * k1 * z2))
    return g, dg


def _tril_weights(ws_ref):
    r = lax.broadcasted_iota(jnp.int32, (CHUNK, CHUNK), 0)
    s = lax.broadcasted_iota(jnp.int32, (CHUNK, CHUNK), 1)
    mask = (s <= r).astype(F32)
    return [(ws_ref[h] * mask).astype(BF16) for h in range(N_HEADS)]


def _window_counts(first_row, n_rows):
    pos = (first_row + lax.broadcasted_iota(jnp.int32, (n_rows, 1), 0)).astype(F32)
    return pos, [jnp.minimum(pos + 1.0, float(w)) for w in POOL_WINDOWS]


def _causal_window_sums(ext):
    out = []
    e = ext
    shift = 1
    for g in range(len(POOL_WINDOWS)):
        e = e + pltpu.roll(e, shift, 0)
        shift *= 2
        out.append(e[:, g * GROUP_DIM:(g + 1) * GROUP_DIM])
    return out


def _anticausal_window_sums(ext):
    n = ext.shape[0]
    out = []
    e = ext
    shift = 1
    for g in range(len(POOL_WINDOWS)):
        e = e + pltpu.roll(e, n - shift, 0)
        shift *= 2
        out.append(e[:, g * GROUP_DIM:(g + 1) * GROUP_DIM])
    return out


def _fwd_mix(x, mod6, n1pre, n1post, n2pre, win_g, wout_g, w_spatial, bsp_full, gain, bias, w_pool, b_pool, pool_scale, ts):
    s_len = x.shape[0]
    rs = D_MODEL // N_CHIPS

    def body(x_ref, mod_ref, g1pre_ref, g1post_ref, g2pre_ref, win_ref, wout_ref, ws_ref, bsp_ref, gain_ref,
             bias_ref, wp_ref, bp_ref, ps_ref, h1_ref, z_ref, y_ref, mix_ref, x1_ref, h2_ref, mixed_ref, prev_ref,
             wfull_ref):
        i = pl.program_id(0)
        _zero_on_first_step(prev_ref)
        _join_w_in_on_first_step(win_ref, wfull_ref)
        xv = x_ref[...]
        r = lax.rsqrt(_rowmean(xv * xv) + EPS)
        hb = (((xv * r) * g1pre_ref[...]) * (1.0 + mod_ref[1:2, :]) + mod_ref[0:1, :]).astype(BF16)
        h1_ref[...] = hb
        z_ref[...] = _dot(hb, wfull_ref[...])

        wc = _tril_weights(ws_ref)
        u, _, _, _, _ = _mixer_forward_tile(z_ref[:, :2 * D_A], wc, bsp_ref, gain_ref[...], bias_ref[...], mixed_ref)
        y_ref[:, :D_A] = (u * mixed_ref[...]).astype(BF16)
        zb = z_ref[:, 2 * D_A:]
        sums = _causal_window_sums(jnp.concatenate([prev_ref[...], zb], axis=0))
        prev_ref[...] = zb[ts - HALO:, :]
        _, counts = _window_counts(i * ts, ts)
        for g in range(len(POOL_WINDOWS)):
            lanes = slice(g * GROUP_DIM, (g + 1) * GROUP_DIM)
            diff = sums[g][HALO:, :] / counts[g] - zb[:, lanes]
            lin = _dot(diff.astype(BF16), wp_ref[g].astype(BF16)) + bp_ref[:, lanes]
            y_ref[:, D_A + g * GROUP_DIM:D_A + (g + 1) * GROUP_DIM] = (lin * ps_ref[:, lanes]).astype(BF16)

        mix = None
        for j in range(N_CHIPS):
            part = _dot(y_ref[:, j * rs:(j + 1) * rs], wout_ref[j])
            mix = part if mix is None else mix + part
        mix_ref[...] = mix
        r2 = lax.rsqrt(_rowmean(mix * mix) + EPS)
        x1 = xv + mod_ref[2:3, :] * ((mix * r2) * g1post_ref[...])
        x1_ref[...] = x1
        r3 = lax.rsqrt(_rowmean(x1 * x1) + EPS)
        h2_ref[...] = (((x1 * r3) * g2pre_ref[...]) * (1.0 + mod_ref[4:5, :]) + mod_ref[3:4, :]).astype(BF16)

    vec = _const((1, D_MODEL))
    f32_rows = jax.ShapeDtypeStruct((s_len, D_MODEL), F32)
    bf16_rows = jax.ShapeDtypeStruct((s_len, D_MODEL), BF16)
    return pl.pallas_call(
        body, name="fwd_mix", grid=(s_len // ts,),
        in_specs=[_rows(ts, D_MODEL), _const((N_MOD, D_MODEL)), vec, vec, vec, _VMEM, _VMEM,
                  _const((N_HEADS, CHUNK, CHUNK)), _const((CHUNK, D_A)), _const((1, D_A)), _const((1, D_A)),
                  _const((N_HEADS, GROUP_DIM, GROUP_DIM)), _const((1, D_B)), _const((1, D_B))],
        out_specs=[_rows(ts, D_MODEL), _rows(ts, D_Z), _rows(ts, D_MODEL), _rows(ts, D_MODEL), _rows(ts, D_MODEL),
                   _rows(ts, D_MODEL)],
        out_shape=[bf16_rows, jax.ShapeDtypeStruct((s_len, D_Z), F32), bf16_rows, f32_rows, f32_rows, bf16_rows],
        scratch_shapes=[pltpu.VMEM((ts, D_A), F32), pltpu.VMEM((HALO, D_B), F32), pltpu.VMEM((D_MODEL, D_Z), BF16)],
        compiler_params=_params(),
    )(x, mod6, n1pre, n1post, n2pre, win_g, wout_g, w_spatial, bsp_full, gain, bias, w_pool, b_pool, pool_scale)


def _mixer_forward_tile(za, wc, bsp_ref, gain, bias, mixed_ref):
    ga, dga = _gelu_parts(za)
    u = ga[:, :D_A]
    v = ga[:, D_A:]
    mu = _rowmean(v)
    vc = v - mu
    rstd = lax.rsqrt(_rowmean(vc * vc) + EPS)
    vhat = vc * rstd
    vn = (vhat * gain + bias).astype(BF16)
    ts = za.shape[0]
    for k in range(ts // CHUNK):
        for h in range(N_HEADS):
            blk = vn[k * CHUNK:(k + 1) * CHUNK, h * HEAD_DIM:(h + 1) * HEAD_DIM]
            mixed_ref[k * CHUNK:(k + 1) * CHUNK, h * HEAD_DIM:(h + 1) * HEAD_DIM] = (
                _dot(wc[h], blk) + bsp_ref[:, h * HEAD_DIM:(h + 1) * HEAD_DIM])
    return u, vhat, rstd, vn, dga


def _fwd_fc1(h2, fc1_g, ts):
    s_len = h2.shape[0]
    cs = D_FF // N_CHIPS

    def body(h_ref, w_ref, q_ref):
        hb = h_ref[...]
        for j in range(N_CHIPS):
            p = jnp.maximum(_dot(hb, w_ref[j]), 0.0)
            q_ref[:, j * cs:(j + 1) * cs] = (p * p).astype(BF16)

    return pl.pallas_call(
        body, name="fwd_fc1", grid=(s_len // ts,),
        in_specs=[_rows(ts, D_MODEL), _VMEM],
        out_specs=_rows(ts, D_FF),
        out_shape=jax.ShapeDtypeStruct((s_len, D_FF), BF16),
        compiler_params=_params(),
    )(h2, fc1_g)


def _fwd_fc2_loss(q, x1, target, mod6, n2post, fc2_g, ts):
    s_len = q.shape[0]
    rs = D_FF // N_CHIPS

    def body(q_ref, x1_ref, t_ref, mod_ref, g_ref, w_ref, dy_ref, df_ref, loss_ref, dgate_ref, dg_ref):
        _zero_on_first_step(loss_ref, dgate_ref, dg_ref)
        gate = mod_ref[5:6, :]
        gn = g_ref[...]
        f = _dot(q_ref[:, 0:rs], w_ref[0])
        for j in range(1, N_CHIPS):
            f = f + _dot(q_ref[:, j * rs:(j + 1) * rs], w_ref[j])
        r4 = lax.rsqrt(_rowmean(f * f) + EPS)
        fh = f * r4
        err = (x1_ref[...] + gate * (fh * gn)) - t_ref[...]
        loss_ref[...] += 0.5 * jnp.sum(_rowmean(err * err), axis=0, keepdims=True)
        dy = err * (1.0 / D_MODEL)
        dy_ref[...] = dy
        dgate_ref[...] += _colsum(dy * (fh * gn))
        dg_ref[...] += _colsum((dy * gate) * fh)
        gh = (dy * gate) * gn
        df_ref[...] = (r4 * (gh - fh * _rowmean(gh * fh))).astype(BF16)

    return pl.pallas_call(
        body, name="fwd_fc2_loss", grid=(s_len // ts,),
        in_specs=[_rows(ts, D_FF), _rows(ts, D_MODEL), _rows(ts, D_MODEL), _const((N_MOD, D_MODEL)),
                  _const((1, D_MODEL)), _VMEM],
        out_specs=[_rows(ts, D_MODEL), _rows(ts, D_MODEL), _const((1, 1)), _const((1, D_MODEL)), _const((1, D_MODEL))],
        out_shape=[jax.ShapeDtypeStruct((s_len, D_MODEL), F32), jax.ShapeDtypeStruct((s_len, D_MODEL), BF16),
                   jax.ShapeDtypeStruct((1, 1), F32), jax.ShapeDtypeStruct((1, D_MODEL), F32),
                   jax.ShapeDtypeStruct((1, D_MODEL), F32)],
        compiler_params=_params(),
    )(q, x1, target, mod6, n2post, fc2_g)


def _join_w_in_on_first_step(win_ref, full_ref):
    cs = D_Z // N_CHIPS

    @pl.when(pl.program_id(0) == 0)
    def _():
        for j in range(N_CHIPS):
            full_ref[:, j * cs:(j + 1) * cs] = win_ref[j]


def _zero_on_first_step(*refs):
    @pl.when(pl.program_id(0) == 0)
    def _():
        for ref in refs:
            ref[...] = jnp.zeros_like(ref)


def _store_on_last_step(*pairs):
    @pl.when(pl.program_id(0) == pl.num_programs(0) - 1)
    def _():
        for acc_ref, hbm_ref in pairs:
            pltpu.sync_copy(acc_ref, hbm_ref)


def _bwd_fc2(df, q, fc2_g, ts):
    s_len = df.shape[0]
    cs = D_FF // N_CHIPS

    def body(df_ref, q_ref, w_ref, dp_ref, dw_hbm, dw_ref):
        _zero_on_first_step(dw_ref)
        dfb = df_ref[...]
        for j in range(N_CHIPS):
            qb = q_ref[:, j * cs:(j + 1) * cs]
            dw_ref[j] += _dot_tn(qb, dfb).reshape(2, cs // 2, D_MODEL)
            dq = _dot_nt(dfb, w_ref[j])
            dp_ref[:, j * cs:(j + 1) * cs] = (dq * (2.0 * jnp.sqrt(qb.astype(F32)))).astype(BF16)
        _store_on_last_step((dw_ref, dw_hbm))

    dw_shape = (N_CHIPS, 2, cs // 2, D_MODEL)
    return pl.pallas_call(
        body, name="bwd_fc2", grid=(s_len // ts,),
        in_specs=[_rows(ts, D_MODEL), _rows(ts, D_FF), _VMEM],
        out_specs=[_rows(ts, D_FF), _ANY],
        out_shape=[jax.ShapeDtypeStruct((s_len, D_FF), BF16), jax.ShapeDtypeStruct(dw_shape, F32)],
        scratch_shapes=[pltpu.VMEM(dw_shape, F32)],
        compiler_params=_params(),
    )(df, q, fc2_g)


def _bwd_fc1_out(dp, dy, x1, mix, h2, ycat, mod6, n2pre, n1post, fc1_g, wout_g, dep, ts):
    s_len = dp.shape[0]
    cs = D_FF // N_CHIPS
    rs = D_MODEL // N_CHIPS

    def body(dp_ref, dy_ref, x1_ref, mix_ref, h2_ref, yc_ref, mod_ref, g2_ref, g1_ref, w1_ref, wo_ref, dep_ref,
             dx1_ref, dyc_ref, dshift2_ref, da2_ref, dgate1_ref, dg1_ref, dw1_hbm, dwo_hbm, dw1_ref, dwo_ref):
        _zero_on_first_step(dshift2_ref, da2_ref, dgate1_ref, dg1_ref, dw1_ref, dwo_ref)
        h2b = h2_ref[...]
        dh2 = None
        for j in range(N_CHIPS):
            dpb = dp_ref[:, j * cs:(j + 1) * cs]
            dw1_ref[j] += _dot_tn(h2b, dpb).reshape(2, D_MODEL // 2, cs)
            part = _dot_nt(dpb, w1_ref[j])
            dh2 = part if dh2 is None else dh2 + part
        x1 = x1_ref[...]
        r3 = lax.rsqrt(_rowmean(x1 * x1) + EPS)
        xh = x1 * r3
        a2 = g2_ref[...] * (1.0 + mod_ref[4:5, :])
        dshift2_ref[...] += _colsum(dh2)
        da2_ref[...] += _colsum(dh2 * xh)
        dxh = dh2 * a2
        dx1 = dy_ref[...] + r3 * (dxh - xh * _rowmean(dxh * xh))
        dx1_ref[...] = dx1

        mix = mix_ref[...]
        r2 = lax.rsqrt(_rowmean(mix * mix) + EPS)
        mh = mix * r2
        gate = mod_ref[2:3, :]
        gn = g1_ref[...]
        dgate1_ref[...] += _colsum(dx1 * (mh * gn))
        dg1_ref[...] += _colsum((dx1 * gate) * mh)
        gh = (dx1 * gate) * gn
        dmix = (r2 * (gh - mh * _rowmean(gh * mh))).astype(BF16)
        dwo_ref[...] += _dot_tn(yc_ref[...], dmix).reshape(N_CHIPS, 2, rs // 2, D_MODEL)
        for j in range(N_CHIPS):
            dyc_ref[:, j * rs:(j + 1) * rs] = _dot_nt(dmix, wo_ref[j])
        _store_on_last_step((dw1_ref, dw1_hbm), (dwo_ref, dwo_hbm))

    vec = jax.ShapeDtypeStruct((1, D_MODEL), F32)
    dw1_shape = (N_CHIPS, 2, D_MODEL // 2, cs)
    dwo_shape = (N_CHIPS, 2, rs // 2, D_MODEL)
    return pl.pallas_call(
        body, name="bwd_fc1_out", grid=(s_len // ts,),
        in_specs=[_rows(ts, D_FF), _rows(ts, D_MODEL), _rows(ts, D_MODEL), _rows(ts, D_MODEL), _rows(ts, D_MODEL),
                  _rows(ts, D_MODEL), _const((N_MOD, D_MODEL)), _const((1, D_MODEL)), _const((1, D_MODEL)), _VMEM,
                  _VMEM, _ANY],
        out_specs=[_rows(ts, D_MODEL), _rows(ts, D_MODEL)] + [_const((1, D_MODEL))] * 4 + [_ANY, _ANY],
        out_shape=[jax.ShapeDtypeStruct((s_len, D_MODEL), F32), jax.ShapeDtypeStruct((s_len, D_MODEL), F32),
                   vec, vec, vec, vec, jax.ShapeDtypeStruct(dw1_shape, F32), jax.ShapeDtypeStruct(dwo_shape, F32)],
        scratch_shapes=[pltpu.VMEM(dw1_shape, F32), pltpu.VMEM(dwo_shape, F32)],
        compiler_params=_params(),
    )(dp, dy, x1, mix, h2, ycat, mod6, n2pre, n1post, fc1_g, wout_g, dep)


def _mixer_bwd(z, dyc, w_spatial, bsp_full, gain, bias, w_pool, b_pool, pool_scale, dep, ts):
    s_len = z.shape[0]
    nb = ts // HALO
    last = s_len // HALO - 1
    te = ts + HALO

    def body(z_ref, zprev_ref, znext_ref, dyc_ref, dynext_ref, ws_ref, bsp_ref, gain_ref, bias_ref, wp_ref, bp_ref,
             ps_ref, dep_ref, dz_ref, dws_ref, dbsp_ref, dgain_ref, dbias_ref, dwp_ref, dbp_ref, dps_ref, mixed_ref,
             dvn_ref):
        i = pl.program_id(0)

        @pl.when(i == 0)
        def _():
            for ref in (dws_ref, dbsp_ref, dgain_ref, dbias_ref, dwp_ref, dbp_ref, dps_ref):
                ref[...] = jnp.zeros_like(ref)

        wc = _tril_weights(ws_ref)
        gain = gain_ref[...]
        u, vhat, rstd, vn, dga = _mixer_forward_tile(z_ref[:, :2 * D_A], wc, bsp_ref, gain, bias_ref[...], mixed_ref)
        dya = dyc_ref[:, :D_A]
        du = dya * mixed_ref[...]
        dmixed = dya * u
        dmb = dmixed.astype(BF16)
        dm_sum = dmixed[0:CHUNK, :]
        for k in range(1, ts // CHUNK):
            dm_sum = dm_sum + dmixed[k * CHUNK:(k + 1) * CHUNK, :]
        r_idx = lax.broadcasted_iota(jnp.int32, (CHUNK, CHUNK), 0)
        s_idx = lax.broadcasted_iota(jnp.int32, (CHUNK, CHUNK), 1)
        causal = (s_idx <= r_idx).astype(F32)
        for h in range(N_HEADS):
            lanes = slice(h * HEAD_DIM, (h + 1) * HEAD_DIM)
            dbsp_ref[h] += jnp.sum(dm_sum[:, lanes], axis=1, keepdims=True)
            acc = None
            for k in range(ts // CHUNK):
                rows = slice(k * CHUNK, (k + 1) * CHUNK)
                t = _dot_nt(dmb[rows, lanes], vn[rows, lanes])
                acc = t if acc is None else acc + t
                dvn_ref[rows, lanes] = _dot_tn(wc[h], dmb[rows, lanes])
            dws_ref[h] += acc * causal
        dvn = dvn_ref[...]
        dgain_ref[...] += _colsum(dvn * vhat)
        dbias_ref[...] += _colsum(dvn)
        dvh = dvn * gain
        dv = rstd * (dvh - _rowmean(dvh) - vhat * _rowmean(dvh * vhat))
        dz_ref[:, :D_A] = (du * dga[:, :D_A]).astype(BF16)
        dz_ref[:, D_A:2 * D_A] = (dv * dga[:, D_A:]).astype(BF16)

        zb = z_ref[:, 2 * D_A:]
        prev = jnp.where(i == 0, 0.0, zprev_ref[...])
        zb_ext = jnp.concatenate([zb, znext_ref[...]], axis=0)
        sums = _causal_window_sums(jnp.concatenate([prev, zb_ext], axis=0))
        pos, counts = _window_counts(i * ts, te)
        dyb_ext = jnp.concatenate([dyc_ref[:, D_A:], dynext_ref[...]], axis=0)
        dlin_ext = dyb_ext * ps_ref[...]
        dbp_ref[...] += _colsum(dlin_ext[:ts, :])
        scaled = []
        ddiffs = []
        lins = []
        for g in range(len(POOL_WINDOWS)):
            lanes = slice(g * GROUP_DIM, (g + 1) * GROUP_DIM)
            diff = (sums[g][HALO:, :] / counts[g] - zb_ext[:, lanes]).astype(BF16)
            wpb = wp_ref[g].astype(BF16)
            dlb = dlin_ext[:, lanes].astype(BF16)
            lins.append(_dot(diff[:ts, :], wpb) + bp_ref[:, lanes])
            dwp_ref[g] += _dot_tn(diff[:ts, :], dlb[:ts, :])
            dd = _dot_nt(dlb, wpb)
            ddiffs.append(dd)
            scaled.append(jnp.where(pos < float(s_len), dd / counts[g], 0.0))
        dps_ref[...] += _colsum(dyb_ext[:ts, :] * jnp.concatenate(lins, axis=1))
        back = _anticausal_window_sums(jnp.concatenate(scaled, axis=1))
        for g in range(len(POOL_WINDOWS)):
            dz_ref[:, 2 * D_A + g * GROUP_DIM:2 * D_A + (g + 1) * GROUP_DIM] = (
                back[g][:ts, :] - ddiffs[g][:ts, :]).astype(BF16)

    sq = jax.ShapeDtypeStruct((N_HEADS, CHUNK, CHUNK), F32)
    vec = jax.ShapeDtypeStruct((1, D_A), F32)
    return pl.pallas_call(
        body, name="mixer_bwd", grid=(s_len // ts,),
        in_specs=[_rows(ts, D_Z),
                  pl.BlockSpec((HALO, D_B), lambda i: (jnp.maximum(i * nb - 1, 0), 2)),
                  pl.BlockSpec((HALO, D_B), lambda i: (jnp.minimum((i + 1) * nb, last), 2)),
                  _rows(ts, D_MODEL),
                  pl.BlockSpec((HALO, D_B), lambda i: (jnp.minimum((i + 1) * nb, last), 1)),
                  _const((N_HEADS, CHUNK, CHUNK)), _const((CHUNK, D_A)), _const((1, D_A)), _const((1, D_A)),
                  _const((N_HEADS, GROUP_DIM, GROUP_DIM)), _const((1, D_B)), _const((1, D_B)), _ANY],
        out_specs=[_rows(ts, D_Z), _const((N_HEADS, CHUNK, CHUNK)), _const((N_HEADS, CHUNK, 1)), _const((1, D_A)),
                   _const((1, D_A)), _const((N_HEADS, GROUP_DIM, GROUP_DIM)), _const((1, D_B)), _const((1, D_B))],
        out_shape=[jax.ShapeDtypeStruct((s_len, D_Z), BF16), sq, jax.ShapeDtypeStruct((N_HEADS, CHUNK, 1), F32), vec,
                   vec, sq, vec, vec],
        scratch_shapes=[pltpu.VMEM((ts, D_A), F32), pltpu.VMEM((ts, D_A), F32)],
        compiler_params=_params(),
    )(z, z, z, dyc, dyc, w_spatial, bsp_full, gain, bias, w_pool, b_pool, pool_scale, dep)


def _bwd_in(dz, dx1, x, h1, mod6, n1pre, win_g, dep, ts):
    s_len = x.shape[0]
    cs = D_Z // N_CHIPS

    def body(dz_ref, dx1_ref, x_ref, h1_ref, mod_ref, g_ref, w_ref, dep_ref, gx_ref, dshift_ref, da_ref, dw_hbm,
             dw_ref, wfull_ref):
        _zero_on_first_step(dshift_ref, da_ref, dw_ref)
        _join_w_in_on_first_step(w_ref, wfull_ref)
        dzb = dz_ref[...]
        dw = _dot_tn(h1_ref[...], dzb)
        for j in range(N_CHIPS):
            dw_ref[j] += dw[:, j * cs:(j + 1) * cs].reshape(2, D_MODEL // 2, cs)
        dh = _dot_nt(dzb, wfull_ref[...])
        xv = x_ref[...]
        r = lax.rsqrt(_rowmean(xv * xv) + EPS)
        xh = xv * r
        a1 = g_ref[...] * (1.0 + mod_ref[1:2, :])
        dshift_ref[...] += _colsum(dh)
        da_ref[...] += _colsum(dh * xh)
        dxh = dh * a1
        gx_ref[...] = dx1_ref[...] + r * (dxh - xh * _rowmean(dxh * xh))
        _store_on_last_step((dw_ref, dw_hbm))

    vec = jax.ShapeDtypeStruct((1, D_MODEL), F32)
    dw_shape = (N_CHIPS, 2, D_MODEL // 2, cs)
    return pl.pallas_call(
        body, name="bwd_in", grid=(s_len // ts,),
        in_specs=[_rows(ts, D_Z), _rows(ts, D_MODEL), _rows(ts, D_MODEL), _rows(ts, D_MODEL),
                  _const((N_MOD, D_MODEL)), _const((1, D_MODEL)), _VMEM, _ANY],
        out_specs=[_rows(ts, D_MODEL), _const((1, D_MODEL)), _const((1, D_MODEL)), _ANY],
        out_shape=[jax.ShapeDtypeStruct((s_len, D_MODEL), F32), vec, vec, jax.ShapeDtypeStruct(dw_shape, F32)],
        scratch_shapes=[pltpu.VMEM(dw_shape, F32), pltpu.VMEM((D_MODEL, D_Z), BF16)],
        compiler_params=_params(),
    )(dz, dx1, x, h1, mod6, n1pre, win_g, dep)


def _adamw_math(w, g, m, v):
    m = ADAM_B1 * m + (1.0 - ADAM_B1) * g
    v = ADAM_B2 * v + (1.0 - ADAM_B2) * (g * g)
    m_hat = m / (1.0 - ADAM_B1 ** ADAM_STEP)
    v_hat = v / (1.0 - ADAM_B2 ** ADAM_STEP)
    delta = -ADAM_LR * (m_hat / (jnp.sqrt(v_hat) + ADAM_EPS) + ADAM_WD * w)
    return delta, m, v


def _adamw(g, w, m, v, name, tr):
    rows, cols = w.shape

    def body(g_ref, w_ref, m_ref, v_ref, d_ref, nm_ref, nv_ref):
        d, nm, nv = _adamw_math(w_ref[...], g_ref[...], m_ref[...], v_ref[...])
        d_ref[...] = d
        nm_ref[...] = nm
        nv_ref[...] = nv

    spec = _rows(tr, cols)
    shape = jax.ShapeDtypeStruct((rows, cols), F32)
    return pl.pallas_call(
        body, name=name, grid=(rows // tr,), in_specs=[spec] * 4, out_specs=[spec] * 3, out_shape=[shape] * 3,
        compiler_params=_params(),
    )(g, w, m, v)


def _ada_grad_adamw(sc_t, dmod_shard, w, m, v, tr):
    rows, cols = w.shape

    def body(s_ref, dm_ref, w_ref, m_ref, v_ref, g_ref, d_ref, nm_ref, nv_ref):
        g = s_ref[:, 0:1] * dm_ref[0:1, :]
        for b in range(1, N_DEV):
            g = g + s_ref[:, b:b + 1] * dm_ref[b:b + 1, :]
        g_ref[...] = g
        d, nm, nv = _adamw_math(w_ref[...], g, m_ref[...], v_ref[...])
        d_ref[...] = d
        nm_ref[...] = nm
        nv_ref[...] = nv

    spec = _rows(tr, cols)
    shape = jax.ShapeDtypeStruct((rows, cols), F32)
    return pl.pallas_call(
        body, name="ada_grad_adamw", grid=(rows // tr,),
        in_specs=[_rows(tr, N_DEV), _const((N_DEV, cols)), spec, spec, spec],
        out_specs=[spec] * 4, out_shape=[shape] * 4, compiler_params=_params(),
    )(sc_t, dmod_shard, w, m, v)


def _mod_grads(da1, dshift1, dgate1, dg1post, da2, dshift2, dgate2, dg2post, mod6, n1pre, n2pre):
    def body(da1_ref, ds1_ref, dgt1_ref, dg1_ref, da2_ref, ds2_ref, dgt2_ref, dg2_ref, mod_ref, n1_ref, n2_ref,
             dmod_ref, dn_ref):
        dmod_ref[0:1, :] = ds1_ref[...]
        dmod_ref[1:2, :] = da1_ref[...] * n1_ref[...]
        dmod_ref[2:3, :] = dgt1_ref[...]
        dmod_ref[3:4, :] = ds2_ref[...]
        dmod_ref[4:5, :] = da2_ref[...] * n2_ref[...]
        dmod_ref[5:6, :] = dgt2_ref[...]
        dn_ref[0:1, :] = da1_ref[...] * (1.0 + mod_ref[1:2, :])
        dn_ref[1:2, :] = dg1_ref[...]
        dn_ref[2:3, :] = da2_ref[...] * (1.0 + mod_ref[4:5, :])
        dn_ref[3:4, :] = dg2_ref[...]

    return pl.pallas_call(
        body, name="mod_grads",
        out_shape=[jax.ShapeDtypeStruct((N_MOD, D_MODEL), F32), jax.ShapeDtypeStruct((4, D_MODEL), F32)],
    )(da1, dshift1, dgate1, dg1post, da2, dshift2, dgate2, dg2post, mod6, n1pre, n2pre)


def _position():
    x, y, c = lax.axis_index("x"), lax.axis_index("y"), lax.axis_index("c")
    return x, y, c


def _flip(v, bit):
    return 1 - v if bit else v


def _peer(x, y, c, k):
    return (_flip(x, k & 4), _flip(y, k & 2), _flip(c, k & 1))


def _remote(src, dst, send_sem, recv_sem, device):
    return pltpu.make_async_remote_copy(src_ref=src, dst_ref=dst, send_sem=send_sem, recv_sem=recv_sem,
                                        device_id=device, device_id_type=MESH)


def _cast_to_slot(w, pos, dep, name, tr):
    rows, cols = w.shape

    def body(pos_ref, w_ref, dep_ref, o_ref):
        o_ref[0] = w_ref[...].astype(BF16)

    return pl.pallas_call(
        body, name=name,
        grid_spec=pltpu.PrefetchScalarGridSpec(
            num_scalar_prefetch=1, grid=(rows // tr,),
            in_specs=[pl.BlockSpec((tr, cols), lambda i, pos: (i, 0)), _ANY],
            out_specs=pl.BlockSpec((1, tr, cols), lambda i, pos: (pos[1], i, 0))),
        out_shape=jax.ShapeDtypeStruct((N_CHIPS, rows, cols), BF16), compiler_params=_params(),
    )(pos, w, dep)


def _mod_exchange(c_row, w_ada_shard, b_ada_row):
    cs = w_ada_shard.shape[1]

    def body(c_ref, w_ref, b_ref, mod_ref, sc_ref, rows_ref, send1, recv1, send2, recv2):
        x, y, c = _position()
        me = 4 * x + 2 * y + c
        chip = 2 * x + y
        cv = c_ref[...]
        sc_ref[me] = cv * jax.nn.sigmoid(cv)
        gather = [_remote(sc_ref.at[me], sc_ref.at[me], send1.at[k - 1], recv1.at[k - 1], _peer(x, y, c, k))
                  for k in range(1, N_DEV)]
        for cp in gather:
            cp.start()
        for k in range(1, N_DEV):
            px, py, pc = _peer(x, y, c, k)
            src = 4 * px + 2 * py + pc
            _remote(sc_ref.at[src], sc_ref.at[src], send1.at[k - 1], recv1.at[k - 1], (px, py, pc)).wait_recv()
        for cp in gather:
            cp.wait_send()
        sc_all = jnp.concatenate([sc_ref[b] for b in range(N_DEV)], axis=0)
        part = jnp.dot(sc_all, w_ref[...], precision=lax.Precision.HIGHEST, preferred_element_type=F32)
        part = part + b_ref[:, pl.ds(pl.multiple_of(chip * cs, 128), cs)]
        for b in range(N_DEV):
            rows_ref[b] = part[b:b + 1, :]
        mod_ref[chip] = rows_ref[me]
        hand = []
        for k in (2, 4, 6):
            px, py, _ = _peer(x, y, c, k)
            hand.append(_remote(rows_ref.at[4 * px + 2 * py + c], mod_ref.at[chip], send2.at[k // 2 - 1],
                                recv2.at[k // 2 - 1], (px, py, c)))
        for cp in hand:
            cp.start()
        for k in (2, 4, 6):
            px, py, _ = _peer(x, y, c, k)
            pchip = 2 * px + py
            _remote(rows_ref.at[me], mod_ref.at[pchip], send2.at[k // 2 - 1], recv2.at[k // 2 - 1],
                    (px, py, c)).wait_recv()
        for cp in hand:
            cp.wait_send()

    return pl.pallas_call(
        body, name="mod_exchange",
        in_specs=[_VMEM, _VMEM, _VMEM], out_specs=[_VMEM, _VMEM],
        out_shape=[jax.ShapeDtypeStruct((N_CHIPS, 1, cs), F32), jax.ShapeDtypeStruct((N_DEV, 1, D_MODEL), F32)],
        scratch_shapes=[pltpu.VMEM((N_DEV, 1, cs), F32), pltpu.SemaphoreType.DMA((N_DEV - 1,)),
                        pltpu.SemaphoreType.DMA((N_DEV - 1,)), pltpu.SemaphoreType.DMA((N_CHIPS - 1,)),
                        pltpu.SemaphoreType.DMA((N_CHIPS - 1,))],
        compiler_params=pltpu.CompilerParams(vmem_limit_bytes=VMEM_LIMIT),
    )(c_row, w_ada_shard, b_ada_row)


_HBM = pl.BlockSpec(memory_space=pltpu.HBM)
_SEM = pl.BlockSpec(memory_space=pltpu.SEMAPHORE)
_EFFECT = pltpu.SideEffectType.DATAFLOW_SIDE_EFFECTING
_CHIP_HOPS = (2, 4, 6)


def _in_hbm(a):
    return pltpu.with_memory_space_constraint(a, pltpu.HBM)


def _sems3():
    return pltpu.SemaphoreType.DMA((len(_CHIP_HOPS),))


def _ag_start(lands, after, name):
    n = len(lands)

    def body(*refs):
        zones = refs[:n]
        sends, recvs = refs[n + 1:2 * n + 1], refs[2 * n + 1:3 * n + 1]
        x, y, c = _position()
        chip = 2 * x + y
        for i in range(n):
            half = zones[i].shape[1] // 2
            mine = zones[i].at[chip, pl.ds(c * half, half)]
            for s, k in enumerate(_CHIP_HOPS):
                px, py, _ = _peer(x, y, c, k)
                _remote(mine, mine, sends[i].at[s], recvs[i].at[s], (px, py, c)).start()

    out = pl.pallas_call(
        body, name=name,
        in_specs=[_HBM] * n + [_ANY],
        out_specs=[_SEM] * (2 * n) + [_HBM] * n,
        out_shape=[_sems3()] * (2 * n) + [pltpu.HBM(z.shape, BF16) for z in lands],
        input_output_aliases={i: 2 * n + i for i in range(n)},
        compiler_params=pltpu.CompilerParams(has_side_effects=_EFFECT),
    )(*[_in_hbm(z) for z in lands], after)
    return [(out[2 * n + i], out[i], out[n + i]) for i in range(n)]


def _ag_pass(group, after, name):
    n = len(group)

    def body(*refs):
        zones = refs[:n]
        sends, recvs = refs[n:2 * n], refs[2 * n:3 * n]
        fsends, frecvs = refs[4 * n + 1:5 * n + 1], refs[5 * n + 1:6 * n + 1]
        x, y, c = _position()
        chip = 2 * x + y
        for i in range(n):
            half = zones[i].shape[1] // 2
            rows = pl.ds(c * half, half)
            for s, k in enumerate(_CHIP_HOPS):
                px, py, _ = _peer(x, y, c, k)
                landed = zones[i].at[2 * px + py, rows]
                _remote(landed, landed, sends[i].at[s], recvs[i].at[s], (px, py, c)).wait_recv()
                _remote(landed, landed, fsends[i].at[s], frecvs[i].at[s], (x, y, 1 - c)).start()
        for i in range(n):
            half = zones[i].shape[1] // 2
            mine = zones[i].at[chip, pl.ds(c * half, half)]
            for s, k in enumerate(_CHIP_HOPS):
                px, py, _ = _peer(x, y, c, k)
                _remote(mine, mine, sends[i].at[s], recvs[i].at[s], (px, py, c)).wait_send()

    out = pl.pallas_call(
        body, name=name,
        in_specs=[_HBM] * n + [_SEM] * (2 * n) + [_ANY],
        out_specs=[_HBM] * n + [_SEM] * (2 * n),
        out_shape=[pltpu.HBM(g[0].shape, BF16) for g in group] + [_sems3()] * (2 * n),
        input_output_aliases={i: i for i in range(n)},
        compiler_params=pltpu.CompilerParams(has_side_effects=_EFFECT),
    )(*[g[0] for g in group], *[g[1] for g in group], *[g[2] for g in group], after)
    return [(out[i], out[n + i], out[2 * n + i]) for i in range(n)]


def _ag_done(group, name):
    n = len(group)

    def body(*refs):
        lands = refs[:n]
        fsends, frecvs = refs[n:2 * n], refs[2 * n:3 * n]
        x, y, c = _position()
        for i in range(n):
            half = lands[i].shape[1] // 2
            for s, k in enumerate(_CHIP_HOPS):
                px, py, _ = _peer(x, y, c, k)
                sent = lands[i].at[2 * px + py, pl.ds(c * half, half)]
                got = lands[i].at[2 * px + py, pl.ds((1 - c) * half, half)]
                cp = _remote(sent, got, fsends[i].at[s], frecvs[i].at[s], (x, y, 1 - c))
                cp.wait_recv()
                cp.wait_send()

    out = pl.pallas_call(
        body, name=name,
        in_specs=[_HBM] * n + [_SEM] * (2 * n),
        out_specs=[_HBM] * n,
        out_shape=[pltpu.HBM(g[0].shape, BF16) for g in group],
        input_output_aliases={i: i for i in range(n)},
        compiler_params=pltpu.CompilerParams(has_side_effects=_EFFECT),
    )(*[g[0] for g in group], *[g[1] for g in group], *[g[2] for g in group])
    return list(out)


def _small_pair_sum(v, dep):
    rows = v.shape[0]

    def body(v_ref, dep_ref, slots_ref, sib_ref, send, recv):
        x, y, c = _position()
        pair = _remote(v_ref, sib_ref, send, recv, (x, y, 1 - c))
        pair.start()
        pair.wait()
        slots_ref[2 * x + y] = v_ref[...] + sib_ref[...]

    return pl.pallas_call(
        body, name="small_pair_sum",
        in_specs=[_VMEM, _ANY], out_specs=_VMEM, out_shape=jax.ShapeDtypeStruct((N_CHIPS, rows, 128), F32),
        scratch_shapes=[pltpu.VMEM((rows, 128), F32), pltpu.SemaphoreType.DMA, pltpu.SemaphoreType.DMA],
        compiler_params=pltpu.CompilerParams(vmem_limit_bytes=VMEM_LIMIT),
    )(v, dep)


def _small_spread_start(slots):
    def body(z_ref, sends, recvs, z_out):
        x, y, c = _position()
        mine = z_ref.at[2 * x + y]
        for s, k in enumerate(_CHIP_HOPS):
            px, py, _ = _peer(x, y, c, k)
            _remote(mine, mine, sends.at[s], recvs.at[s], (px, py, c)).start()

    sends, recvs, out = pl.pallas_call(
        body, name="small_spread_start",
        in_specs=[_HBM], out_specs=[_SEM, _SEM, _HBM],
        out_shape=[_sems3(), _sems3(), pltpu.HBM(slots.shape, F32)],
        input_output_aliases={0: 2},
        compiler_params=pltpu.CompilerParams(has_side_effects=_EFFECT),
    )(_in_hbm(slots))
    return out, sends, recvs


def _small_spread_wait(slots, sends, recvs, after):
    def body(z_ref, sends, recvs, after_ref, z_out):
        x, y, c = _position()
        mine = z_ref.at[2 * x + y]
        for s, k in enumerate(_CHIP_HOPS):
            px, py, _ = _peer(x, y, c, k)
            cp = _remote(mine, z_ref.at[2 * px + py], sends.at[s], recvs.at[s], (px, py, c))
            cp.wait_recv()
            cp.wait_send()

    return pl.pallas_call(
        body, name="small_spread_wait",
        in_specs=[_HBM, _SEM, _SEM, _ANY], out_specs=_HBM, out_shape=pltpu.HBM(slots.shape, F32),
        input_output_aliases={0: 0},
        compiler_params=pltpu.CompilerParams(has_side_effects=_EFFECT),
    )(slots, sends, recvs, after)


def _sibling_sum(pgs, name):
    n = len(pgs)
    units = [(i, j) for i in range(n) for j in range(N_CHIPS)]

    def body(*refs):
        ins, qbs, owns = refs[:n], refs[n:2 * n], refs[2 * n:3 * n]
        mine, other, stage, got = refs[3 * n:4 * n], refs[4 * n:5 * n], refs[5 * n:6 * n], refs[6 * n:7 * n]
        load_a, load_b, send, recv = refs[7 * n:]
        x, y, c = _position()
        chip = 2 * x + y
        loads_a = [pltpu.make_async_copy(ins[i].at[j, 1 - c], other[i].at[j], load_a.at[u])
                   for u, (i, j) in enumerate(units)]
        loads_b = [pltpu.make_async_copy(ins[i].at[j, c], mine[i].at[j], load_b.at[u])
                   for u, (i, j) in enumerate(units)]
        for cp in loads_a + loads_b:
            cp.start()
        sent = []
        for u, (i, j) in enumerate(units):
            loads_a[u].wait()
            stage[i][j] = other[i][j].astype(BF16)
            cp = _remote(stage[i].at[j], got[i].at[j], send.at[u], recv.at[u], (x, y, 1 - c))
            cp.start()
            sent.append(cp)
        for u, (i, j) in enumerate(units):
            loads_b[u].wait()
            sent[u].wait_recv()
            q = mine[i][j] + got[i][j].astype(F32)
            mine[i][j] = q
            qbs[i][j] = q.astype(BF16)
        for i in range(n):
            owns[i][...] = mine[i][chip]
        for cp in sent:
            cp.wait_send()

    wire = [(N_CHIPS,) + p.shape[2:] for p in pgs]
    out = pl.pallas_call(
        body, name=name, in_specs=[_ANY] * n, out_specs=[_VMEM] * (2 * n),
        out_shape=[jax.ShapeDtypeStruct(w, BF16) for w in wire] + [jax.ShapeDtypeStruct(w[1:], F32) for w in wire],
        scratch_shapes=[pltpu.VMEM(w, F32) for w in wire] * 2 + [pltpu.VMEM(w, BF16) for w in wire] * 2
        + [pltpu.SemaphoreType.DMA((len(units),))] * 4,
        compiler_params=pltpu.CompilerParams(vmem_limit_bytes=VMEM_LIMIT),
    )(*pgs)
    return list(out[:n]), list(out[n:])


def _rs_start(qbs, name):
    n = len(qbs)

    def body(*refs):
        outs, inboxes = refs[:n], refs[n:2 * n]
        sends, recvs = refs[2 * n:3 * n], refs[3 * n:4 * n]
        x, y, c = _position()
        chip = 2 * x + y
        for i in range(n):
            for s, k in enumerate(_CHIP_HOPS):
                px, py, _ = _peer(x, y, c, k)
                _remote(outs[i].at[2 * px + py], inboxes[i].at[chip], sends[i].at[s], recvs[i].at[s], (px, py, c)).start()

    inboxes = [_in_hbm(lax.empty(q.shape, BF16)) for q in qbs]
    out = pl.pallas_call(
        body, name=name,
        in_specs=[_HBM] * (2 * n),
        out_specs=[_SEM] * (2 * n) + [_HBM] * (2 * n),
        out_shape=[_sems3()] * (2 * n) + [pltpu.HBM(q.shape, BF16) for q in qbs] * 2,
        input_output_aliases={i: 2 * n + i for i in range(2 * n)},
        compiler_params=pltpu.CompilerParams(has_side_effects=_EFFECT),
    )(*[_in_hbm(q) for q in qbs], *inboxes)
    return [(out[2 * n + i], out[3 * n + i], out[i], out[n + i]) for i in range(n)]


def _rs_wait(group, after, name):
    n = len(group)

    def body(*refs):
        outs, inboxes = refs[:n], refs[n:2 * n]
        sends, recvs = refs[2 * n:3 * n], refs[3 * n:4 * n]
        x, y, c = _position()
        for i in range(n):
            for s, k in enumerate(_CHIP_HOPS):
                px, py, _ = _peer(x, y, c, k)
                slot = 2 * px + py
                cp = _remote(outs[i].at[slot], inboxes[i].at[slot], sends[i].at[s], recvs[i].at[s], (px, py, c))
                cp.wait_recv()
                cp.wait_send()

    out = pl.pallas_call(
        body, name=name,
        in_specs=[_HBM] * (2 * n) + [_SEM] * (2 * n) + [_ANY],
        out_specs=[_HBM] * n,
        out_shape=[pltpu.HBM(g[1].shape, BF16) for g in group],
        input_output_aliases={n + i: i for i in range(n)},
        compiler_params=pltpu.CompilerParams(has_side_effects=_EFFECT),
    )(*[g[0] for g in group], *[g[1] for g in group], *[g[2] for g in group], *[g[3] for g in group], after)
    return list(out)


def _final_add(inbox, own, pos, name, tr):
    _, r, cdim = inbox.shape

    def body(pos_ref, a_ref, b_ref, c_ref, own_ref, o_ref):
        total = (a_ref[0].astype(F32) + b_ref[0].astype(F32)) + c_ref[0].astype(F32)
        o_ref[0] = total + own_ref[...]

    def slot(flip):
        return pl.BlockSpec((1, tr, cdim), lambda i, pos: (jnp.bitwise_xor(pos[1], flip), i, 0))

    return pl.pallas_call(
        body, name=name,
        grid_spec=pltpu.PrefetchScalarGridSpec(
            num_scalar_prefetch=1, grid=(r // tr,),
            in_specs=[slot(1), slot(2), slot(3), pl.BlockSpec((tr, cdim), lambda i, pos: (i, 0))],
            out_specs=pl.BlockSpec((1, tr, cdim), lambda i, pos: (pos[0], i, 0))),
        out_shape=jax.ShapeDtypeStruct((2, r, cdim), F32), compiler_params=_params(),
    )(pos, inbox, inbox, inbox, own)


def _sibling_share(shards, name):
    n = len(shards)

    def body(*refs):
        outs = refs[n:2 * n]
        send, recv = refs[2 * n:]
        x, y, c = _position()
        copies = [_remote(outs[i].at[c], outs[i].at[c], send.at[i], recv.at[i], (x, y, 1 - c)) for i in range(n)]
        for cp in copies:
            cp.start()
        for i in range(n):
            theirs = outs[i].at[1 - c]
            _remote(theirs, theirs, send.at[i], recv.at[i], (x, y, 1 - c)).wait_recv()
        for cp in copies:
            cp.wait_send()

    return pl.pallas_call(
        body, name=name, in_specs=[_ANY] * n, out_specs=[_ANY] * n,
        out_shape=[jax.ShapeDtypeStruct(s.shape, F32) for s in shards],
        input_output_aliases={i: i for i in range(n)},
        scratch_shapes=[pltpu.SemaphoreType.DMA((n,))] * 2,
    )(*shards)


_SMALL = (("b_ada", N_MOD * D_MODEL), ("norm1_pre", D_MODEL), ("norm1_post", D_MODEL), ("norm2_pre", D_MODEL),
          ("norm2_post", D_MODEL), ("w_spatial", N_HEADS * CHUNK * CHUNK), ("b_spatial", N_HEADS * CHUNK),
          ("ln_v_gain", D_A), ("ln_v_bias", D_A), ("w_pool", N_HEADS * GROUP_DIM * GROUP_DIM),
          ("b_pool", D_B), ("pool_scale", D_B))
_MOD_ROWS = N_MOD * D_MODEL // 128


def _packed_rows(size):
    return -(-(size // 128) // 8) * 8


def _pack(parts):
    out = []
    for name, size in _SMALL:
        a = parts[name].reshape(size // 128, 128)
        pad = _packed_rows(size) - a.shape[0]
        out.append(jnp.pad(a, ((0, pad), (0, 0))) if pad else a)
    return out


def _unpack(packed, shapes):
    res = {}
    row = 0
    for name, size in _SMALL:
        res[name] = packed[row:row + size // 128].reshape(shapes[name])
        row += _packed_rows(size)
    return res


def _small_adamw(slots, wp, mp, vp):
    rows = wp.shape[0]
    head = N_DEV * _MOD_ROWS

    def body(s_ref, w_ref, m_ref, v_ref, g_ref, d_ref, nm_ref, nv_ref, dmod_ref, loss_ref, t_ref):
        t_ref[...] = ((s_ref[0] + s_ref[1]) + s_ref[2]) + s_ref[3]
        dmod_ref[...] = t_ref[0:head, :]
        loss_ref[...] = t_ref[head + rows - _MOD_ROWS:, :]
        gb = t_ref[0:_MOD_ROWS, :]
        for b in range(1, N_DEV):
            gb = gb + t_ref[b * _MOD_ROWS:(b + 1) * _MOD_ROWS, :]
        g_ref[0:_MOD_ROWS, :] = gb
        g_ref[_MOD_ROWS:, :] = t_ref[head:head + rows - _MOD_ROWS, :]
        d, nm, nv = _adamw_math(w_ref[...], g_ref[...], m_ref[...], v_ref[...])
        d_ref[...] = d
        nm_ref[...] = nm
        nv_ref[...] = nv

    return pl.pallas_call(
        body, name="small_adamw",
        out_shape=[jax.ShapeDtypeStruct((rows, 128), F32)] * 4
        + [jax.ShapeDtypeStruct((head, 128), F32), jax.ShapeDtypeStruct((8, 128), F32)],
        scratch_shapes=[pltpu.VMEM(slots.shape[1:], F32)],
        compiler_params=pltpu.CompilerParams(vmem_limit_bytes=VMEM_LIMIT),
    )(slots, wp, mp, vp)


def kernel(x, c, w_ada, b_ada, norm1_pre, norm1_post, w_in, w_spatial, b_spatial, ln_v_gain, ln_v_bias, w_pool, b_pool, pool_scale, w_out, norm2_pre, norm2_post, w_fc1, w_fc2, loss_target, m_w_ada, m_b_ada, m_norm1_pre, m_norm1_post, m_w_in, m_w_spatial, m_b_spatial, m_ln_v_gain, m_ln_v_bias, m_w_pool, m_b_pool, m_pool_scale, m_w_out, m_norm2_pre, m_norm2_post, m_w_fc1, m_w_fc2, v_w_ada, v_b_ada, v_norm1_pre, v_norm1_post, v_w_in, v_w_spatial, v_b_spatial, v_ln_v_gain, v_ln_v_bias, v_w_pool, v_b_pool, v_pool_scale, v_w_out, v_norm2_pre, v_norm2_post, v_w_fc1, v_w_fc2):
    weights = dict(w_ada=w_ada, b_ada=b_ada, norm1_pre=norm1_pre, norm1_post=norm1_post, w_in=w_in,
                   w_spatial=w_spatial, b_spatial=b_spatial, ln_v_gain=ln_v_gain, ln_v_bias=ln_v_bias, w_pool=w_pool,
                   b_pool=b_pool, pool_scale=pool_scale, w_out=w_out, norm2_pre=norm2_pre, norm2_post=norm2_post,
                   w_fc1=w_fc1, w_fc2=w_fc2)
    m_old = dict(w_ada=m_w_ada, b_ada=m_b_ada, norm1_pre=m_norm1_pre, norm1_post=m_norm1_post, w_in=m_w_in,
                 w_spatial=m_w_spatial, b_spatial=m_b_spatial, ln_v_gain=m_ln_v_gain, ln_v_bias=m_ln_v_bias,
                 w_pool=m_w_pool, b_pool=m_b_pool, pool_scale=m_pool_scale, w_out=m_w_out, norm2_pre=m_norm2_pre,
                 norm2_post=m_norm2_post, w_fc1=m_w_fc1, w_fc2=m_w_fc2)
    v_old = dict(w_ada=v_w_ada, b_ada=v_b_ada, norm1_pre=v_norm1_pre, norm1_post=v_norm1_post, w_in=v_w_in,
                 w_spatial=v_w_spatial, b_spatial=v_b_spatial, ln_v_gain=v_ln_v_gain, ln_v_bias=v_ln_v_bias,
                 w_pool=v_w_pool, b_pool=v_b_pool, pool_scale=v_pool_scale, w_out=v_w_out, norm2_pre=v_norm2_pre,
                 norm2_post=v_norm2_post, w_fc1=v_w_fc1, w_fc2=v_w_fc2)
    order = ("w_ada", "b_ada", "norm1_pre", "norm1_post", "w_in", "w_spatial", "b_spatial", "ln_v_gain", "ln_v_bias",
             "w_pool", "b_pool", "pool_scale", "w_out", "norm2_pre", "norm2_post", "w_fc1", "w_fc2")
    big = ("w_in", "w_out", "w_fc1", "w_fc2")
    mx, my, mc = _position()
    me = 4 * mx + 2 * my + mc
    chip = 2 * mx + my
    row = lambda a: a.reshape(1, -1)

    pos = jnp.stack([mc, chip]).astype(jnp.int32)
    xs, target = x[0], loss_target[0]
    n1pre, n1post, n2pre, n2post = row(norm1_pre), row(norm1_post), row(norm2_pre), row(norm2_post)
    mixer = (w_spatial, jnp.repeat(b_spatial.T, HEAD_DIM, axis=1), row(ln_v_gain), row(ln_v_bias), w_pool,
             row(b_pool), row(pool_scale))
    ts_big, ts_mid = 512, 256

    mod4, sc_all = _mod_exchange(c, w_ada, row(b_ada))
    mod6 = mod4.reshape(N_MOD, D_MODEL)
    ag = _ag_start([_cast_to_slot(weights[n], pos, mod4, "cast_" + n, 256) for n in big[:2]], mod4, "ag_start_mix")
    ag += _ag_start([_cast_to_slot(weights[n], pos, ag[0][0], "cast_" + n, 256) for n in big[2:]], ag[0][0],
                    "ag_start_mlp")

    win_g, wout_g = _ag_done(_ag_pass([ag[0], ag[1]], ag[2][0], "ag_pass_mix"), "ag_done_mix")
    h1, z, ycat, mix, x1, h2 = _fwd_mix(xs, mod6, n1pre, n1post, n2pre, win_g, wout_g, *mixer, ts_mid)
    (fc1_g,) = _ag_done(_ag_pass([ag[2]], h2, "ag_pass_fc1"), "ag_done_fc1")
    q = _fwd_fc1(h2, fc1_g, ts_big)
    (fc2_g,) = _ag_done(_ag_pass([ag[3]], q, "ag_pass_fc2"), "ag_done_fc2")
    dy, df, loss, dgate2, dg2post = _fwd_fc2_loss(q, x1, target, mod6, n2post, fc2_g, ts_big)

    def reduce_start(partials, tag):
        wire, owns = _sibling_sum(partials, "sibling_sum_" + tag)
        return _rs_start(wire, "rs_start_" + tag), owns

    def reduce_finish(state, owns, names, tag, dep):
        inboxes = _rs_wait(state, dep, "rs_wait_" + tag)
        halves = [_final_add(ib, own, pos, "final_add_" + n, min(256, own.shape[0]))
                  for ib, own, n in zip(inboxes, owns, names)]
        shards = _sibling_share(halves, "sibling_share_" + tag)
        for n, g in zip(names, shards):
            g = g.reshape(weights[n].shape)
            grads[n] = g
            deltas[n], new_m[n], new_v[n] = _adamw(g, weights[n], m_old[n], v_old[n], "adamw_" + n, 256)

    grads, deltas, new_m, new_v = {}, {}, {}, {}
    dp, g_fc2 = _bwd_fc2(df, q, fc2_g, ts_big)
    state_fc2, owns_fc2 = reduce_start([g_fc2], "fc2")
    dx1, dyc, dshift2, da2, dgate1, dg1post, g_fc1, g_out = _bwd_fc1_out(
        dp, dy, x1, mix, h2, ycat, mod6, n2pre, n1post, fc1_g, wout_g, state_fc2[0][0], ts_mid)
    state_mid, owns_mid = reduce_start([g_fc1, g_out], "mid")
    dz, dws, dbsp, dgain, dbias, dwp, dbp, dps = _mixer_bwd(z, dyc, *mixer, state_mid[0][0], ts_mid)
    grad_x, dshift1, da1, g_in = _bwd_in(dz, dx1, xs, h1, mod6, n1pre, win_g, state_mid[0][0], ts_mid)
    state_in, owns_in = reduce_start([g_in], "in")
    dmod6, dnorms = _mod_grads(da1, dshift1, dgate1, dg1post, da2, dshift2, dgate2, dg2post, mod6, n1pre, n2pre)

    parts = dict(b_ada=dmod6, norm1_pre=dnorms[0], norm1_post=dnorms[1], norm2_pre=dnorms[2], norm2_post=dnorms[3],
                 w_spatial=dws, b_spatial=dbsp, ln_v_gain=dgain, ln_v_bias=dbias, w_pool=dwp, b_pool=dbp,
                 pool_scale=dps)
    pieces = _pack(parts)
    slots = lax.dynamic_update_slice(jnp.zeros((N_DEV * _MOD_ROWS, 128), F32), pieces[0], (me * _MOD_ROWS, 0))
    loss_tile = jnp.pad(loss, ((0, 7), (0, 127)))
    spread = _small_spread_start(_small_pair_sum(jnp.concatenate([slots] + pieces[1:] + [loss_tile], axis=0),
                                                 state_in[0][0]))
    reduce_finish(state_fc2, owns_fc2, ("w_fc2",), "fc2", spread[0])
    reduce_finish(state_mid, owns_mid, ("w_fc1", "w_out"), "mid", deltas["w_fc2"])
    gp, dp, nmp, nvp, dmod_all, loss_tile = _small_adamw(
        _small_spread_wait(*spread, deltas["w_fc1"]), jnp.concatenate(_pack(weights), axis=0),
        jnp.concatenate(_pack(m_old), axis=0), jnp.concatenate(_pack(v_old), axis=0))
    loss = loss_tile[0, 0]
    shapes = {n: weights[n].shape for n, _ in _SMALL}
    for res, packed in ((grads, gp), (deltas, dp), (new_m, nmp), (new_v, nvp)):
        res.update(_unpack(packed, shapes))

    dmod_all = dmod_all.reshape(N_DEV, N_MOD * D_MODEL)
    cs = w_ada.shape[1]
    dmod_shard = lax.dynamic_slice(dmod_all, (0, chip * cs), (N_DEV, cs))
    sc_t = sc_all.reshape(N_DEV, D_MODEL).T
    grads["w_ada"], deltas["w_ada"], new_m["w_ada"], new_v["w_ada"] = _ada_grad_adamw(
        sc_t, dmod_shard, w_ada, m_w_ada, v_w_ada, 256)

    reduce_finish(state_in, owns_in, ("w_in",), "in", deltas["w_ada"])

    return (loss, grad_x[None], *[grads[n] for n in order], *[deltas[n] for n in order],
            *[new_m[n] for n in order], *[new_v[n] for n in order])
```

```python
import functools

import jax
import jax.numpy as jnp
from jax import lax
from jax.experimental import pallas as pl
from jax.experimental.pallas import tpu as pltpu

F32 = jnp.float32
BF16 = jnp.bfloat16
MESH = pl.DeviceIdType.MESH

D_MODEL = 1024
D_A = 512
D_B = 512
D_Z = 2 * D_A + D_B
N_HEADS = 4
HEAD_DIM = 128
CHUNK = 128
POOL_WINDOWS = (2, 4, 8, 16)
GROUP_DIM = 128
D_FF = 4096
N_MOD = 6
EPS = 1e-6
HALO = 16
N_CHIPS = 4
N_DEV = 8

ADAM_LR = 0.001
ADAM_B1 = 0.9
ADAM_B2 = 0.999
ADAM_EPS = 1e-08
ADAM_WD = 0.01
ADAM_STEP = 10

VMEM_LIMIT = 56 * 1024 * 1024

_VMEM = pl.BlockSpec(memory_space=pltpu.VMEM)
_ANY = pl.BlockSpec(memory_space=pl.ANY)


def _params(n_grid_axes=1):
    return pltpu.CompilerParams(dimension_semantics=("arbitrary",) * n_grid_axes, vmem_limit_bytes=VMEM_LIMIT)


def _rows(ts, width):
    return pl.BlockSpec((ts, width), lambda i: (i, 0))


def _const(shape):
    return pl.BlockSpec(shape, lambda i: (0,) * len(shape))


def _dot(a, b):
    return jnp.dot(a, b, preferred_element_type=F32)


def _dot_nt(a, b):
    return lax.dot_general(a, b, (((1,), (1,)), ((), ())), preferred_element_type=F32)


def _dot_tn(a, b):
    return lax.dot_general(a, b, (((0,), (0,)), ((), ())), preferred_element_type=F32)


def _rowmean(v):
    return jnp.mean(v, axis=-1, keepdims=True)


def _colsum(v):
    return jnp.sum(v, axis=0, keepdims=True)


def _gelu_parts(z):
    k0 = 0.7978845608028654
    k1 = 0.044715
    z2 = z * z
    t = jnp.tanh(k0 * (z + k1 * z * z2))
    g = 0.5 * z * (1.0 + t)
    dg = 0.5 * (1.0 + t) + 0.5 * z * (1.0 - t * t) * (k0 * (1.0 + 3.0 * k1 * z2))
    return g, dg


def _tril_weights(ws_ref):
    r = lax.broadcasted_iota(jnp.int32, (CHUNK, CHUNK), 0)
    s = lax.broadcasted_iota(jnp.int32, (CHUNK, CHUNK), 1)
    mask = (s <= r).astype(F32)
    return [(ws_ref[h] * mask).astype(BF16) for h in range(N_HEADS)]


def _window_counts(first_row, n_rows):
    pos = (first_row + lax.broadcasted_iota(jnp.int32, (n_rows, 1), 0)).astype(F32)
    return pos, [jnp.minimum(pos + 1.0, float(w)) for w in POOL_WINDOWS]


def _causal_window_sums(ext):
    out = []
    e = ext
    shift = 1
    for g in range(len(POOL_WINDOWS)):
        e = e + pltpu.roll(e, shift, 0)
        shift *= 2
        out.append(e[:, g * GROUP_DIM:(g + 1) * GROUP_DIM])
    return out


def _anticausal_window_sums(ext):
    n = ext.shape[0]
    out = []
    e = ext
    shift = 1
    for g in range(len(POOL_WINDOWS)):
        e = e + pltpu.roll(e, n - shift, 0)
        shift *= 2
        out.append(e[:, g * GROUP_DIM:(g + 1) * GROUP_DIM])
    return out


def _fwd_mix(x, mod6, n1pre, n1post, n2pre, win_g, wout_g, w_spatial, bsp_full, gain, bias, w_pool, b_pool, pool_scale, ts):
    s_len = x.shape[0]
    rs = D_MODEL // N_CHIPS

    def body(x_ref, mod_ref, g1pre_ref, g1post_ref, g2pre_ref, win_ref, wout_ref, ws_ref, bsp_ref, gain_ref,
             bias_ref, wp_ref, bp_ref, ps_ref, h1_ref, z_ref, y_ref, mix_ref, x1_ref, h2_ref, mixed_ref, prev_ref,
             wfull_ref):
        i = pl.program_id(0)
        _zero_on_first_step(prev_ref)
        _join_w_in_on_first_step(win_ref, wfull_ref)
        xv = x_ref[...]
        r = lax.rsqrt(_rowmean(xv * xv) + EPS)
        hb = (((xv * r) * g1pre_ref[...]) * (1.0 + mod_ref[1:2, :]) + mod_ref[0:1, :]).astype(BF16)
        h1_ref[...] = hb
        z_ref[...] = _dot(hb, wfull_ref[...])

        wc = _tril_weights(ws_ref)
        u, _, _, _, _ = _mixer_forward_tile(z_ref[:, :2 * D_A], wc, bsp_ref, gain_ref[...], bias_ref[...], mixed_ref)
        y_ref[:, :D_A] = (u * mixed_ref[...]).astype(BF16)
        zb = z_ref[:, 2 * D_A:]
        sums = _causal_window_sums(jnp.concatenate([prev_ref[...], zb], axis=0))
        prev_ref[...] = zb[ts - HALO:, :]
        _, counts = _window_counts(i * ts, ts)
        for g in range(len(POOL_WINDOWS)):
            lanes = slice(g * GROUP_DIM, (g + 1) * GROUP_DIM)
            diff = sums[g][HALO:, :] / counts[g] - zb[:, lanes]
            lin = _dot(diff.astype(BF16), wp_ref[g].astype(BF16)) + bp_ref[:, lanes]
            y_ref[:, D_A + g * GROUP_DIM:D_A + (g + 1) * GROUP_DIM] = (lin * ps_ref[:, lanes]).astype(BF16)

        mix = None
        for j in range(N_CHIPS):
            part = _dot(y_ref[:, j * rs:(j + 1) * rs], wout_ref[j])
            mix = part if mix is None else mix + part
        mix_ref[...] = mix
        r2 = lax.rsqrt(_rowmean(mix * mix) + EPS)
        x1 = xv + mod_ref[2:3, :] * ((mix * r2) * g1post_ref[...])
        x1_ref[...] = x1
        r3 = lax.rsqrt(_rowmean(x1 * x1) + EPS)
        h2_ref[...] = (((x1 * r3) * g2pre_ref[...]) * (1.0 + mod_ref[4:5, :]) + mod_ref[3:4, :]).astype(BF16)

    vec = _const((1, D_MODEL))
    f32_rows = jax.ShapeDtypeStruct((s_len, D_MODEL), F32)
    bf16_rows = jax.ShapeDtypeStruct((s_len, D_MODEL), BF16)
    return pl.pallas_call(
        body, name="fwd_mix", grid=(s_len // ts,),
        in_specs=[_rows(ts, D_MODEL), _const((N_MOD, D_MODEL)), vec, vec, vec, _VMEM, _VMEM,
                  _const((N_HEADS, CHUNK, CHUNK)), _const((CHUNK, D_A)), _const((1, D_A)), _const((1, D_A)),
                  _const((N_HEADS, GROUP_DIM, GROUP_DIM)), _const((1, D_B)), _const((1, D_B))],
        out_specs=[_rows(ts, D_MODEL), _rows(ts, D_Z), _rows(ts, D_MODEL), _rows(ts, D_MODEL), _rows(ts, D_MODEL),
                   _rows(ts, D_MODEL)],
        out_shape=[bf16_rows, jax.ShapeDtypeStruct((s_len, D_Z), F32), bf16_rows, f32_rows, f32_rows, bf16_rows],
        scratch_shapes=[pltpu.VMEM((ts, D_A), F32), pltpu.VMEM((HALO, D_B), F32), pltpu.VMEM((D_MODEL, D_Z), BF16)],
        compiler_params=_params(),
    )(x, mod6, n1pre, n1post, n2pre, win_g, wout_g, w_spatial, bsp_full, gain, bias, w_pool, b_pool, pool_scale)


def _mixer_forward_tile(za, wc, bsp_ref, gain, bias, mixed_ref):
    ga, dga = _gelu_parts(za)
    u = ga[:, :D_A]
    v = ga[:, D_A:]
    mu = _rowmean(v)
    vc = v - mu
    rstd = lax.rsqrt(_rowmean(vc * vc) + EPS)
    vhat = vc * rstd
    vn = (vhat * gain + bias).astype(BF16)
    ts = za.shape[0]
    for k in range(ts // CHUNK):
        for h in range(N_HEADS):
            blk = vn[k * CHUNK:(k + 1) * CHUNK, h * HEAD_DIM:(h + 1) * HEAD_DIM]
            mixed_ref[k * CHUNK:(k + 1) * CHUNK, h * HEAD_DIM:(h + 1) * HEAD_DIM] = (
                _dot(wc[h], blk) + bsp_ref[:, h * HEAD_DIM:(h + 1) * HEAD_DIM])
    return u, vhat, rstd, vn, dga


def _fwd_fc1(h2, fc1_g, ts):
    s_len = h2.shape[0]
    cs = D_FF // N_CHIPS

    def body(h_ref, w_ref, q_ref):
        hb = h_ref[...]
        for j in range(N_CHIPS):
            p = jnp.maximum(_dot(hb, w_ref[j]), 0.0)
            q_ref[:, j * cs:(j + 1) * cs] = (p * p).astype(BF16)

    return pl.pallas_call(
        body, name="fwd_fc1", grid=(s_len // ts,),
        in_specs=[_rows(ts, D_MODEL), _VMEM],
        out_specs=_rows(ts, D_FF),
        out_shape=jax.ShapeDtypeStruct((s_len, D_FF), BF16),
        compiler_params=_params(),
    )(h2, fc1_g)


def _fwd_fc2_loss(q, x1, target, mod6, n2post, fc2_g, ts):
    s_len = q.shape[0]
    rs = D_FF // N_CHIPS

    def body(q_ref, x1_ref, t_ref, mod_ref, g_ref, w_ref, dy_ref, df_ref, loss_ref, dgate_ref, dg_ref):
        _zero_on_first_step(loss_ref, dgate_ref, dg_ref)
        gate = mod_ref[5:6, :]
        gn = g_ref[...]
        f = _dot(q_ref[:, 0:rs], w_ref[0])
        for j in range(1, N_CHIPS):
            f = f + _dot(q_ref[:, j * rs:(j + 1) * rs], w_ref[j])
        r4 = lax.rsqrt(_rowmean(f * f) + EPS)
        fh = f * r4
        err = (x1_ref[...] + gate * (fh * gn)) - t_ref[...]
        loss_ref[...] += 0.5 * jnp.sum(_rowmean(err * err), axis=0, keepdims=True)
        dy = err * (1.0 / D_MODEL)
        dy_ref[...] = dy
        dgate_ref[...] += _colsum(dy * (fh * gn))
        dg_ref[...] += _colsum((dy * gate) * fh)
        gh = (dy * gate) * gn
        df_ref[...] = (r4 * (gh - fh * _rowmean(gh * fh))).astype(BF16)

    return pl.pallas_call(
        body, name="fwd_fc2_loss", grid=(s_len // ts,),
        in_specs=[_rows(ts, D_FF), _rows(ts, D_MODEL), _rows(ts, D_MODEL), _const((N_MOD, D_MODEL)),
                  _const((1, D_MODEL)), _VMEM],
        out_specs=[_rows(ts, D_MODEL), _rows(ts, D_MODEL), _const((1, 1)), _const((1, D_MODEL)), _const((1, D_MODEL))],
        out_shape=[jax.ShapeDtypeStruct((s_len, D_MODEL), F32), jax.ShapeDtypeStruct((s_len, D_MODEL), BF16),
                   jax.ShapeDtypeStruct((1, 1), F32), jax.ShapeDtypeStruct((1, D_MODEL), F32),
                   jax.ShapeDtypeStruct((1, D_MODEL), F32)],
        compiler_params=_params(),
    )(q, x1, target, mod6, n2post, fc2_g)


def _join_w_in_on_first_step(win_ref, full_ref):
    cs = D_Z // N_CHIPS

    @pl.when(pl.program_id(0) == 0)
    def _():
        for j in range(N_CHIPS):
            full_ref[:, j * cs:(j + 1) * cs] = win_ref[j]


def _zero_on_first_step(*refs):
    @pl.when(pl.program_id(0) == 0)
    def _():
        for ref in refs:
            ref[...] = jnp.zeros_like(ref)


def _store_on_last_step(*pairs):
    @pl.when(pl.program_id(0) == pl.num_programs(0) - 1)
    def _():
        for acc_ref, hbm_ref in pairs:
            pltpu.sync_copy(acc_ref, hbm_ref)


def _bwd_fc2(df, q, fc2_g, ts):
    s_len = df.shape[0]
    cs = D_FF // N_CHIPS

    def body(df_ref, q_ref, w_ref, dp_ref, dw_hbm, dw_ref):
        _zero_on_first_step(dw_ref)
        dfb = df_ref[...]
        for j in range(N_CHIPS):
            qb = q_ref[:, j * cs:(j + 1) * cs]
            dw_ref[j] += _dot_tn(qb, dfb).reshape(2, cs // 2, D_MODEL)
            dq = _dot_nt(dfb, w_ref[j])
            dp_ref[:, j * cs:(j + 1) * cs] = (dq * (2.0 * jnp.sqrt(qb.astype(F32)))).astype(BF16)
        _store_on_last_step((dw_ref, dw_hbm))

    dw_shape = (N_CHIPS, 2, cs // 2, D_MODEL)
    return pl.pallas_call(
        body, name="bwd_fc2", grid=(s_len // ts,),
        in_specs=[_rows(ts, D_MODEL), _rows(ts, D_FF), _VMEM],
        out_specs=[_rows(ts, D_FF), _ANY],
        out_shape=[jax.ShapeDtypeStruct((s_len, D_FF), BF16), jax.ShapeDtypeStruct(dw_shape, F32)],
        scratch_shapes=[pltpu.VMEM(dw_shape, F32)],
        compiler_params=_params(),
    )(df, q, fc2_g)


def _bwd_fc1_out(dp, dy, x1, mix, h2, ycat, mod6, n2pre, n1post, fc1_g, wout_g, dep, ts):
    s_len = dp.shape[0]
    cs = D_FF // N_CHIPS
    rs = D_MODEL // N_CHIPS

    def body(dp_ref, dy_ref, x1_ref, mix_ref, h2_ref, yc_ref, mod_ref, g2_ref, g1_ref, w1_ref, wo_ref, dep_ref,
             dx1_ref, dyc_ref, dshift2_ref, da2_ref, dgate1_ref, dg1_ref, dw1_hbm, dwo_hbm, dw1_ref, dwo_ref):
        _zero_on_first_step(dshift2_ref, da2_ref, dgate1_ref, dg1_ref, dw1_ref, dwo_ref)
        h2b = h2_ref[...]
        dh2 = None
        for j in range(N_CHIPS):
            dpb = dp_ref[:, j * cs:(j + 1) * cs]
            dw1_ref[j] += _dot_tn(h2b, dpb).reshape(2, D_MODEL // 2, cs)
            part = _dot_nt(dpb, w1_ref[j])
            dh2 = part if dh2 is None else dh2 + part
        x1 = x1_ref[...]
        r3 = lax.rsqrt(_rowmean(x1 * x1) + EPS)
        xh = x1 * r3
        a2 = g2_ref[...] * (1.0 + mod_ref[4:5, :])
        dshift2_ref[...] += _colsum(dh2)
        da2_ref[...] += _colsum(dh2 * xh)
        dxh = dh2 * a2
        dx1 = dy_ref[...] + r3 * (dxh - xh * _rowmean(dxh * xh))
        dx1_ref[...] = dx1

        mix = mix_ref[...]
        r2 = lax.rsqrt(_rowmean(mix * mix) + EPS)
        mh = mix * r2
        gate = mod_ref[2:3, :]
        gn = g1_ref[...]
        dgate1_ref[...] += _colsum(dx1 * (mh * gn))
        dg1_ref[...] += _colsum((dx1 * gate) * mh)
        gh = (dx1 * gate) * gn
        dmix = (r2 * (gh - mh * _rowmean(gh * mh))).astype(BF16)
        dwo_ref[...] += _dot_tn(yc_ref[...], dmix).reshape(N_CHIPS, 2, rs // 2, D_MODEL)
        for j in range(N_CHIPS):
            dyc_ref[:, j * rs:(j + 1) * rs] = _dot_nt(dmix, wo_ref[j])
        _store_on_last_step((dw1_ref, dw1_hbm), (dwo_ref, dwo_hbm))

    vec = jax.ShapeDtypeStruct((1, D_MODEL), F32)
    dw1_shape = (N_CHIPS, 2, D_MODEL // 2, cs)
    dwo_shape = (N_CHIPS, 2, rs // 2, D_MODEL)
    return pl.pallas_call(
        body, name="bwd_fc1_out", grid=(s_len // ts,),
        in_specs=[_rows(ts, D_FF), _rows(ts, D_MODEL), _rows(ts, D_MODEL), _rows(ts, D_MODEL), _rows(ts, D_MODEL),
                  _rows(ts, D_MODEL), _const((N_MOD, D_MODEL)), _const((1, D_MODEL)), _const((1, D_MODEL)), _VMEM,
                  _VMEM, _ANY],
        out_specs=[_rows(ts, D_MODEL), _rows(ts, D_MODEL)] + [_const((1, D_MODEL))] * 4 + [_ANY, _ANY],
        out_shape=[jax.ShapeDtypeStruct((s_len, D_MODEL), F32), jax.ShapeDtypeStruct((s_len, D_MODEL), F32),
                   vec, vec, vec, vec, jax.ShapeDtypeStruct(dw1_shape, F32), jax.ShapeDtypeStruct(dwo_shape, F32)],
        scratch_shapes=[pltpu.VMEM(dw1_shape, F32), pltpu.VMEM(dwo_shape, F32)],
        compiler_params=_params(),
    )(dp, dy, x1, mix, h2, ycat, mod6, n2pre, n1post, fc1_g, wout_g, dep)


def _mixer_bwd(z, dyc, w_spatial, bsp_full, gain, bias, w_pool, b_pool, pool_scale, dep, ts):
    s_len = z.shape[0]
    nb = ts // HALO
    last = s_len // HALO - 1
    te = ts + HALO

    def body(z_ref, zprev_ref, znext_ref, dyc_ref, dynext_ref, ws_ref, bsp_ref, gain_ref, bias_ref, wp_ref, bp_ref,
             ps_ref, dep_ref, dz_ref, dws_ref, dbsp_ref, dgain_ref, dbias_ref, dwp_ref, dbp_ref, dps_ref, mixed_ref,
             dvn_ref):
        i = pl.program_id(0)

        @pl.when(i == 0)
        def _():
            for ref in (dws_ref, dbsp_ref, dgain_ref, dbias_ref, dwp_ref, dbp_ref, dps_ref):
                ref[...] = jnp.zeros_like(ref)

        wc = _tril_weights(ws_ref)
        gain = gain_ref[...]
        u, vhat, rstd, vn, dga = _mixer_forward_tile(z_ref[:, :2 * D_A], wc, bsp_ref, gain, bias_ref[...], mixed_ref)
        dya = dyc_ref[:, :D_A]
        du = dya * mixed_ref[...]
        dmixed = dya * u
        dmb = dmixed.astype(BF16)
        dm_sum = dmixed[0:CHUNK, :]
        for k in range(1, ts // CHUNK):
            dm_sum = dm_sum + dmixed[k * CHUNK:(k + 1) * CHUNK, :]
        r_idx = lax.broadcasted_iota(jnp.int32, (CHUNK, CHUNK), 0)
        s_idx = lax.broadcasted_iota(jnp.int32, (CHUNK, CHUNK), 1)
        causal = (s_idx <= r_idx).astype(F32)
        for h in range(N_HEADS):
            lanes = slice(h * HEAD_DIM, (h + 1) * HEAD_DIM)
            dbsp_ref[h] += jnp.sum(dm_sum[:, lanes], axis=1, keepdims=True)
            acc = None
            for k in range(ts // CHUNK):
                rows = slice(k * CHUNK, (k + 1) * CHUNK)
                t = _dot_nt(dmb[rows, lanes], vn[rows, lanes])
                acc = t if acc is None else acc + t
                dvn_ref[rows, lanes] = _dot_tn(wc[h], dmb[rows, lanes])
            dws_ref[h] += acc * causal
        dvn = dvn_ref[...]
        dgain_ref[...] += _colsum(dvn * vhat)
        dbias_ref[...] += _colsum(dvn)
        dvh = dvn * gain
        dv = rstd * (dvh - _rowmean(dvh) - vhat * _rowmean(dvh * vhat))
        dz_ref[:, :D_A] = (du * dga[:, :D_A]).astype(BF16)
        dz_ref[:, D_A:2 * D_A] = (dv * dga[:, D_A:]).astype(BF16)

        zb = z_ref[:, 2 * D_A:]
        prev = jnp.where(i == 0, 0.0, zprev_ref[...])
        zb_ext = jnp.concatenate([zb, znext_ref[...]], axis=0)
        sums = _causal_window_sums(jnp.concatenate([prev, zb_ext], axis=0))
        pos, counts = _window_counts(i * ts, te)
        dyb_ext = jnp.concatenate([dyc_ref[:, D_A:], dynext_ref[...]], axis=0)
        dlin_ext = dyb_ext * ps_ref[...]
        dbp_ref[...] += _colsum(dlin_ext[:ts, :])
        scaled = []
        ddiffs = []
        lins = []
        for g in range(len(POOL_WINDOWS)):
            lanes = slice(g * GROUP_DIM, (g + 1) * GROUP_DIM)
            diff = (sums[g][HALO:, :] / counts[g] - zb_ext[:, lanes]).astype(BF16)
            wpb = wp_ref[g].astype(BF16)
            dlb = dlin_ext[:, lanes].astype(BF16)
            lins.append(_dot(diff[:ts, :], wpb) + bp_ref[:, lanes])
            dwp_ref[g] += _dot_tn(diff[:ts, :], dlb[:ts, :])
            dd = _dot_nt(dlb, wpb)
            ddiffs.append(dd)
            scaled.append(jnp.where(pos < float(s_len), dd / counts[g], 0.0))
        dps_ref[...] += _colsum(dyb_ext[:ts, :] * jnp.concatenate(lins, axis=1))
        back = _anticausal_window_sums(jnp.concatenate(scaled, axis=1))
        for g in range(len(POOL_WINDOWS)):
            dz_ref[:, 2 * D_A + g * GROUP_DIM:2 * D_A + (g + 1) * GROUP_DIM] = (
                back[g][:ts, :] - ddiffs[g][:ts, :]).astype(BF16)

    sq = jax.ShapeDtypeStruct((N_HEADS, CHUNK, CHUNK), F32)
    vec = jax.ShapeDtypeStruct((1, D_A), F32)
    return pl.pallas_call(
        body, name="mixer_bwd", grid=(s_len // ts,),
        in_specs=[_rows(ts, D_Z),
                  pl.BlockSpec((HALO, D_B), lambda i: (jnp.maximum(i * nb - 1, 0), 2)),
                  pl.BlockSpec((HALO, D_B), lambda i: (jnp.minimum((i + 1) * nb, last), 2)),
                  _rows(ts, D_MODEL),
                  pl.BlockSpec((HALO, D_B), lambda i: (jnp.minimum((i + 1) * nb, last), 1)),
                  _const((N_HEADS, CHUNK, CHUNK)), _const((CHUNK, D_A)), _const((1, D_A)), _const((1, D_A)),
                  _const((N_HEADS, GROUP_DIM, GROUP_DIM)), _const((1, D_B)), _const((1, D_B)), _ANY],
        out_specs=[_rows(ts, D_Z), _const((N_HEADS, CHUNK, CHUNK)), _const((N_HEADS, CHUNK, 1)), _const((1, D_A)),
                   _const((1, D_A)), _const((N_HEADS, GROUP_DIM, GROUP_DIM)), _const((1, D_B)), _const((1, D_B))],
        out_shape=[jax.ShapeDtypeStruct((s_len, D_Z), BF16), sq, jax.ShapeDtypeStruct((N_HEADS, CHUNK, 1), F32), vec,
                   vec, sq, vec, vec],
        scratch_shapes=[pltpu.VMEM((ts, D_A), F32), pltpu.VMEM((ts, D_A), F32)],
        compiler_params=_params(),
    )(z, z, z, dyc, dyc, w_spatial, bsp_full, gain, bias, w_pool, b_pool, pool_scale, dep)


def _bwd_in(dz, dx1, x, h1, mod6, n1pre, win_g, dep, ts):
    s_len = x.shape[0]
    cs = D_Z // N_CHIPS

    def body(dz_ref, dx1_ref, x_ref, h1_ref, mod_ref, g_ref, w_ref, dep_ref, gx_ref, dshift_ref, da_ref, dw_hbm,
             dw_ref, wfull_ref):
        _zero_on_first_step(dshift_ref, da_ref, dw_ref)
        _join_w_in_on_first_step(w_ref, wfull_ref)
        dzb = dz_ref[...]
        dw = _dot_tn(h1_ref[...], dzb)
        for j in range(N_CHIPS):
            dw_ref[j] += dw[:, j * cs:(j + 1) * cs].reshape(2, D_MODEL // 2, cs)
        dh = _dot_nt(dzb, wfull_ref[...])
        xv = x_ref[...]
        r = lax.rsqrt(_rowmean(xv * xv) + EPS)
        xh = xv * r
        a1 = g_ref[...] * (1.0 + mod_ref[1:2, :])
        dshift_ref[...] += _colsum(dh)
        da_ref[...] += _colsum(dh * xh)
        dxh = dh * a1
        gx_ref[...] = dx1_ref[...] + r * (dxh - xh * _rowmean(dxh * xh))
        _store_on_last_step((dw_ref, dw_hbm))

    vec = jax.ShapeDtypeStruct((1, D_MODEL), F32)
    dw_shape = (N_CHIPS, 2, D_MODEL // 2, cs)
    return pl.pallas_call(
        body, name="bwd_in", grid=(s_len // ts,),
        in_specs=[_rows(ts, D_Z), _rows(ts, D_MODEL), _rows(ts, D_MODEL), _rows(ts, D_MODEL),
                  _const((N_MOD, D_MODEL)), _const((1, D_MODEL)), _VMEM, _ANY],
        out_specs=[_rows(ts, D_MODEL), _const((1, D_MODEL)), _const((1, D_MODEL)), _ANY],
        out_shape=[jax.ShapeDtypeStruct((s_len, D_MODEL), F32), vec, vec, jax.ShapeDtypeStruct(dw_shape, F32)],
        scratch_shapes=[pltpu.VMEM(dw_shape, F32), pltpu.VMEM((D_MODEL, D_Z), BF16)],
        compiler_params=_params(),
    )(dz, dx1, x, h1, mod6, n1pre, win_g, dep)


def _adamw_math(w, g, m, v):
    m = ADAM_B1 * m + (1.0 - ADAM_B1) * g
    v = ADAM_B2 * v + (1.0 - ADAM_B2) * (g * g)
    m_hat = m / (1.0 - ADAM_B1 ** ADAM_STEP)
    v_hat = v / (1.0 - ADAM_B2 ** ADAM_STEP)
    delta = -ADAM_LR * (m_hat / (jnp.sqrt(v_hat) + ADAM_EPS) + ADAM_WD * w)
    return delta, m, v


def _adamw(g, w, m, v, name, tr):
    rows, cols = w.shape

    def body(g_ref, w_ref, m_ref, v_ref, go_ref, d_ref, nm_ref, nv_ref):
        g = g_ref[...]
        go_ref[...] = g
        d, nm, nv = _adamw_math(w_ref[...], g, m_ref[...], v_ref[...])
        d_ref[...] = d
        nm_ref[...] = nm
        nv_ref[...] = nv

    spec = _rows(tr, cols)
    shape = jax.ShapeDtypeStruct((rows, cols), F32)
    return pl.pallas_call(
        body, name=name, grid=(rows // tr,), in_specs=[spec] * 4, out_specs=[spec] * 4, out_shape=[shape] * 4,
        compiler_params=_params(),
    )(g, w, m, v)


def _ada_grad_adamw(sc_t, dmod_shard, w, m, v, tr):
    rows, cols = w.shape

    def body(s_ref, dm_ref, w_ref, m_ref, v_ref, g_ref, d_ref, nm_ref, nv_ref):
        g = s_ref[:, 0:1] * dm_ref[0:1, :]
        for b in range(1, N_DEV):
            g = g + s_ref[:, b:b + 1] * dm_ref[b:b + 1, :]
        g_ref[...] = g
        d, nm, nv = _adamw_math(w_ref[...], g, m_ref[...], v_ref[...])
        d_ref[...] = d
        nm_ref[...] = nm
        nv_ref[...] = nv

    spec = _rows(tr, cols)
    shape = jax.ShapeDtypeStruct((rows, cols), F32)
    return pl.pallas_call(
        body, name="ada_grad_adamw", grid=(rows // tr,),
        in_specs=[_rows(tr, N_DEV), _const((N_DEV, cols)), spec, spec, spec],
        out_specs=[spec] * 4, out_shape=[shape] * 4, compiler_params=_params(),
    )(sc_t, dmod_shard, w, m, v)


def _mod_grads(da1, dshift1, dgate1, dg1post, da2, dshift2, dgate2, dg2post, mod6, n1pre, n2pre):
    def body(da1_ref, ds1_ref, dgt1_ref, dg1_ref, da2_ref, ds2_ref, dgt2_ref, dg2_ref, mod_ref, n1_ref, n2_ref,
             dmod_ref, dn_ref):
        dmod_ref[0:1, :] = ds1_ref[...]
        dmod_ref[1:2, :] = da1_ref[...] * n1_ref[...]
        dmod_ref[2:3, :] = dgt1_ref[...]
        dmod_ref[3:4, :] = ds2_ref[...]
        dmod_ref[4:5, :] = da2_ref[...] * n2_ref[...]
        dmod_ref[5:6, :] = dgt2_ref[...]
        dn_ref[0:1, :] = da1_ref[...] * (1.0 + mod_ref[1:2, :])
        dn_ref[1:2, :] = dg1_ref[...]
        dn_ref[2:3, :] = da2_ref[...] * (1.0 + mod_ref[4:5, :])
        dn_ref[3:4, :] = dg2_ref[...]

    return pl.pallas_call(
        body, name="mod_grads",
        out_shape=[jax.ShapeDtypeStruct((N_MOD, D_MODEL), F32), jax.ShapeDtypeStruct((4, D_MODEL), F32)],
    )(da1, dshift1, dgate1, dg1post, da2, dshift2, dgate2, dg2post, mod6, n1pre, n2pre)


def _position():
    x, y, c = lax.axis_index("x"), lax.axis_index("y"), lax.axis_index("c")
    return x, y, c


def _flip(v, bit):
    return 1 - v if bit else v


def _peer(x, y, c, k):
    return (_flip(x, k & 4), _flip(y, k & 2), _flip(c, k & 1))


def _remote(src, dst, send_sem, recv_sem, device):
    return pltpu.make_async_remote_copy(src_ref=src, dst_ref=dst, send_sem=send_sem, recv_sem=recv_sem,
                                        device_id=device, device_id_type=MESH)


def _cast_to_slot(w, pos, dep, name, tr):
    rows, cols = w.shape

    def body(pos_ref, w_ref, dep_ref, o_ref):
        o_ref[0] = w_ref[...].astype(BF16)

    return pl.pallas_call(
        body, name=name,
        grid_spec=pltpu.PrefetchScalarGridSpec(
            num_scalar_prefetch=1, grid=(rows // tr,),
            in_specs=[pl.BlockSpec((tr, cols), lambda i, pos: (i, 0)), _ANY],
            out_specs=pl.BlockSpec((1, tr, cols), lambda i, pos: (pos[1], i, 0))),
        out_shape=jax.ShapeDtypeStruct((N_CHIPS, rows, cols), BF16), compiler_params=_params(),
    )(pos, w, dep)


def _mod_exchange(c_row, w_ada_shard, b_ada_row):
    cs = w_ada_shard.shape[1]

    def body(c_ref, w_ref, b_ref, mod_ref, sc_ref, rows_ref, send1, recv1, send2, recv2):
        x, y, c = _position()
        me = 4 * x + 2 * y + c
        chip = 2 * x + y
        cv = c_ref[...]
        sc_ref[me] = cv * jax.nn.sigmoid(cv)
        gather = [_remote(sc_ref.at[me], sc_ref.at[me], send1.at[k - 1], recv1.at[k - 1], _peer(x, y, c, k))
                  for k in range(1, N_DEV)]
        for cp in gather:
            cp.start()
        for k in range(1, N_DEV):
            px, py, pc = _peer(x, y, c, k)
            src = 4 * px + 2 * py + pc
            _remote(sc_ref.at[src], sc_ref.at[src], send1.at[k - 1], recv1.at[k - 1], (px, py, pc)).wait_recv()
        for cp in gather:
            cp.wait_send()
        sc_all = jnp.concatenate([sc_ref[b] for b in range(N_DEV)], axis=0)
        part = jnp.dot(sc_all, w_ref[...], precision=lax.Precision.HIGHEST, preferred_element_type=F32)
        part = part + b_ref[:, pl.ds(pl.multiple_of(chip * cs, 128), cs)]
        for b in range(N_DEV):
            rows_ref[b] = part[b:b + 1, :]
        mod_ref[chip] = rows_ref[me]
        hand = []
        for k in (2, 4, 6):
            px, py, _ = _peer(x, y, c, k)
            hand.append(_remote(rows_ref.at[4 * px + 2 * py + c], mod_ref.at[chip], send2.at[k // 2 - 1],
                                recv2.at[k // 2 - 1], (px, py, c)))
        for cp in hand:
            cp.start()
        for k in (2, 4, 6):
            px, py, _ = _peer(x, y, c, k)
            pchip = 2 * px + py
            _remote(rows_ref.at[me], mod_ref.at[pchip], send2.at[k // 2 - 1], recv2.at[k // 2 - 1],
                    (px, py, c)).wait_recv()
        for cp in hand:
            cp.wait_send()

    return pl.pallas_call(
        body, name="mod_exchange",
        in_specs=[_VMEM, _VMEM, _VMEM], out_specs=[_VMEM, _VMEM],
        out_shape=[jax.ShapeDtypeStruct((N_CHIPS, 1, cs), F32), jax.ShapeDtypeStruct((N_DEV, 1, D_MODEL), F32)],
        scratch_shapes=[pltpu.VMEM((N_DEV, 1, cs), F32), pltpu.SemaphoreType.DMA((N_DEV - 1,)),
                        pltpu.SemaphoreType.DMA((N_DEV - 1,)), pltpu.SemaphoreType.DMA((N_CHIPS - 1,)),
                        pltpu.SemaphoreType.DMA((N_CHIPS - 1,))],
        compiler_params=pltpu.CompilerParams(vmem_limit_bytes=VMEM_LIMIT),
    )(c_row, w_ada_shard, b_ada_row)


_HBM = pl.BlockSpec(memory_space=pltpu.HBM)
_SEM = pl.BlockSpec(memory_space=pltpu.SEMAPHORE)
_EFFECT = pltpu.SideEffectType.DATAFLOW_SIDE_EFFECTING
_CHIP_HOPS = (2, 4, 6)


def _in_hbm(a):
    return pltpu.with_memory_space_constraint(a, pltpu.HBM)


def _sems3():
    return pltpu.SemaphoreType.DMA((len(_CHIP_HOPS),))


def _ag_start(lands, after, name):
    n = len(lands)

    def body(*refs):
        zones = refs[:n]
        sends, recvs = refs[n + 1:2 * n + 1], refs[2 * n + 1:3 * n + 1]
        x, y, c = _position()
        chip = 2 * x + y
        for i in range(n):
            half = zones[i].shape[1] // 2
            mine = zones[i].at[chip, pl.ds(c * half, half)]
            for s, k in enumerate(_CHIP_HOPS):
                px, py, _ = _peer(x, y, c, k)
                _remote(mine, mine, sends[i].at[s], recvs[i].at[s], (px, py, c)).start()

    out = pl.pallas_call(
        body, name=name,
        in_specs=[_HBM] * n + [_ANY],
        out_specs=[_SEM] * (2 * n) + [_HBM] * n,
        out_shape=[_sems3()] * (2 * n) + [pltpu.HBM(z.shape, BF16) for z in lands],
        input_output_aliases={i: 2 * n + i for i in range(n)},
        compiler_params=pltpu.CompilerParams(has_side_effects=_EFFECT),
    )(*[_in_hbm(z) for z in lands], after)
    return [(out[2 * n + i], out[i], out[n + i]) for i in range(n)]


def _ag_pass(group, after, name):
    n = len(group)

    def body(*refs):
        zones = refs[:n]
        sends, recvs = refs[n:2 * n], refs[2 * n:3 * n]
        fsends, frecvs = refs[4 * n + 1:5 * n + 1], refs[5 * n + 1:6 * n + 1]
        x, y, c = _position()
        chip = 2 * x + y
        for i in range(n):
            half = zones[i].shape[1] // 2
            rows = pl.ds(c * half, half)
            for s, k in enumerate(_CHIP_HOPS):
                px, py, _ = _peer(x, y, c, k)
                landed = zones[i].at[2 * px + py, rows]
                _remote(landed, landed, sends[i].at[s], recvs[i].at[s], (px, py, c)).wait_recv()
                _remote(landed, landed, fsends[i].at[s], frecvs[i].at[s], (x, y, 1 - c)).start()
        for i in range(n):
            half = zones[i].shape[1] // 2
            mine = zones[i].at[chip, pl.ds(c * half, half)]
            for s, k in enumerate(_CHIP_HOPS):
                px, py, _ = _peer(x, y, c, k)
                _remote(mine, mine, sends[i].at[s], recvs[i].at[s], (px, py, c)).wait_send()

    out = pl.pallas_call(
        body, name=name,
        in_specs=[_HBM] * n + [_SEM] * (2 * n) + [_ANY],
        out_specs=[_HBM] * n + [_SEM] * (2 * n),
        out_shape=[pltpu.HBM(g[0].shape, BF16) for g in group] + [_sems3()] * (2 * n),
        input_output_aliases={i: i for i in range(n)},
        compiler_params=pltpu.CompilerParams(has_side_effects=_EFFECT),
    )(*[g[0] for g in group], *[g[1] for g in group], *[g[2] for g in group], after)
    return [(out[i], out[n + i], out[2 * n + i]) for i in range(n)]


def _ag_done(group, name):
    n = len(group)

    def body(*refs):
        lands = refs[:n]
        fsends, frecvs = refs[n:2 * n], refs[2 * n:3 * n]
        x, y, c = _position()
        for i in range(n):
            half = lands[i].shape[1] // 2
            for s, k in enumerate(_CHIP_HOPS):
                px, py, _ = _peer(x, y, c, k)
                sent = lands[i].at[2 * px + py, pl.ds(c * half, half)]
                got = lands[i].at[2 * px + py, pl.ds((1 - c) * half, half)]
                cp = _remote(sent, got, fsends[i].at[s], frecvs[i].at[s], (x, y, 1 - c))
                cp.wait_recv()
                cp.wait_send()

    out = pl.pallas_call(
        body, name=name,
        in_specs=[_HBM] * n + [_SEM] * (2 * n),
        out_specs=[_HBM] * n,
        out_shape=[pltpu.HBM(g[0].shape, BF16) for g in group],
        input_output_aliases={i: i for i in range(n)},
        compiler_params=pltpu.CompilerParams(has_side_effects=_EFFECT),
    )(*[g[0] for g in group], *[g[1] for g in group], *[g[2] for g in group])
    return list(out)


def _small_spread_start(slots):
    def body(z_ref, sends, recvs, z_out):
        x, y, c = _position()
        mine = z_ref.at[2 * x + y]
        for s, k in enumerate(_CHIP_HOPS):
            px, py, _ = _peer(x, y, c, k)
            _remote(mine, mine, sends.at[s], recvs.at[s], (px, py, c)).start()

    sends, recvs, out = pl.pallas_call(
        body, name="small_spread_start",
        in_specs=[_HBM], out_specs=[_SEM, _SEM, _HBM],
        out_shape=[_sems3(), _sems3(), pltpu.HBM(slots.shape, F32)],
        input_output_aliases={0: 2},
        compiler_params=pltpu.CompilerParams(has_side_effects=_EFFECT),
    )(_in_hbm(slots))
    return out, sends, recvs


def _small_spread_wait(slots, sends, recvs, after):
    def body(z_ref, sends, recvs, after_ref, z_out):
        x, y, c = _position()
        mine = z_ref.at[2 * x + y]
        for s, k in enumerate(_CHIP_HOPS):
            px, py, _ = _peer(x, y, c, k)
            cp = _remote(mine, z_ref.at[2 * px + py], sends.at[s], recvs.at[s], (px, py, c))
            cp.wait_recv()
            cp.wait_send()

    return pl.pallas_call(
        body, name="small_spread_wait",
        in_specs=[_HBM, _SEM, _SEM, _ANY], out_specs=_HBM, out_shape=pltpu.HBM(slots.shape, F32),
        input_output_aliases={0: 0},
        compiler_params=pltpu.CompilerParams(has_side_effects=_EFFECT),
    )(slots, sends, recvs, after)


def _sibling_sum(pgs, name, small=None):
    n = len(pgs)
    k = 0 if small is None else 1
    units = [(i, j) for i in range(n) for j in range(N_CHIPS)]

    def body(*refs):
        refs = list(refs)
        take = lambda count: [refs.pop(0) for _ in range(count)]
        ins, small_in = take(n), take(k)
        qbs, owns, slots_out = take(n), take(n), take(k)
        mine, other, stage, got = take(n), take(n), take(n), take(n)
        load_a, load_b, send, recv = take(4)
        x, y, c = _position()
        chip = 2 * x + y
        if k:
            sib_ref, pair_send, pair_recv = take(3)
            pair = _remote(small_in[0], sib_ref, pair_send, pair_recv, (x, y, 1 - c))
            pair.start()
        loads_a = [pltpu.make_async_copy(ins[i].at[j, 1 - c], other[i].at[j], load_a.at[u])
                   for u, (i, j) in enumerate(units)]
        loads_b = [pltpu.make_async_copy(ins[i].at[j, c], mine[i].at[j], load_b.at[u])
                   for u, (i, j) in enumerate(units)]
        for cp in loads_a + loads_b:
            cp.start()
        sent = []
        for u, (i, j) in enumerate(units):
            loads_a[u].wait()
            stage[i][j] = other[i][j].astype(BF16)
            cp = _remote(stage[i].at[j], got[i].at[j], send.at[u], recv.at[u], (x, y, 1 - c))
            cp.start()
            sent.append(cp)
        for u, (i, j) in enumerate(units):
            loads_b[u].wait()
            sent[u].wait_recv()
            q = mine[i][j] + got[i][j].astype(F32)
            mine[i][j] = q
            qbs[i][j] = q.astype(BF16)
        for i in range(n):
            owns[i][...] = mine[i][chip]
        if k:
            pair.wait()
            slots_out[0][chip] = small_in[0][...] + sib_ref[...]
        for cp in sent:
            cp.wait_send()

    wire = [(N_CHIPS,) + p.shape[2:] for p in pgs]
    extra_out, extra_scratch = [], []
    if k:
        extra_out = [jax.ShapeDtypeStruct((N_CHIPS,) + small.shape, F32)]
        extra_scratch = [pltpu.VMEM(small.shape, F32), pltpu.SemaphoreType.DMA, pltpu.SemaphoreType.DMA]
    out = pl.pallas_call(
        body, name=name, in_specs=[_ANY] * n + [_VMEM] * k, out_specs=[_VMEM] * (2 * n + k),
        out_shape=[jax.ShapeDtypeStruct(w, BF16) for w in wire] + [jax.ShapeDtypeStruct(w[1:], F32) for w in wire]
        + extra_out,
        scratch_shapes=[pltpu.VMEM(w, F32) for w in wire] * 2 + [pltpu.VMEM(w, BF16) for w in wire] * 2
        + [pltpu.SemaphoreType.DMA((len(units),))] * 4 + extra_scratch,
        compiler_params=pltpu.CompilerParams(vmem_limit_bytes=VMEM_LIMIT),
    )(*pgs, *([small] if k else []))
    return list(out[:n]), list(out[n:2 * n]), list(out[2 * n:])


def _rs_start(qbs, name):
    n = len(qbs)

    def body(*refs):
        outs, inboxes = refs[:n], refs[n:2 * n]
        sends, recvs = refs[2 * n:3 * n], refs[3 * n:4 * n]
        x, y, c = _position()
        chip = 2 * x + y
        for i in range(n):
            for s, k in enumerate(_CHIP_HOPS):
                px, py, _ = _peer(x, y, c, k)
                _remote(outs[i].at[2 * px + py], inboxes[i].at[chip], sends[i].at[s], recvs[i].at[s], (px, py, c)).start()

    inboxes = [_in_hbm(lax.empty(q.shape, BF16)) for q in qbs]
    out = pl.pallas_call(
        body, name=name,
        in_specs=[_HBM] * (2 * n),
        out_specs=[_SEM] * (2 * n) + [_HBM] * (2 * n),
        out_shape=[_sems3()] * (2 * n) + [pltpu.HBM(q.shape, BF16) for q in qbs] * 2,
        input_output_aliases={i: 2 * n + i for i in range(2 * n)},
        compiler_params=pltpu.CompilerParams(has_side_effects=_EFFECT),
    )(*[_in_hbm(q) for q in qbs], *inboxes)
    return [(out[2 * n + i], out[3 * n + i], out[i], out[n + i]) for i in range(n)]


def _rs_wait(group, after, name):
    n = len(group)

    def body(*refs):
        outs, inboxes = refs[:n], refs[n:2 * n]
        sends, recvs = refs[2 * n:3 * n], refs[3 * n:4 * n]
        x, y, c = _position()
        for i in range(n):
            for s, k in enumerate(_CHIP_HOPS):
                px, py, _ = _peer(x, y, c, k)
                slot = 2 * px + py
                cp = _remote(outs[i].at[slot], inboxes[i].at[slot], sends[i].at[s], recvs[i].at[s], (px, py, c))
                cp.wait_recv()
                cp.wait_send()

    out = pl.pallas_call(
        body, name=name,
        in_specs=[_HBM] * (2 * n) + [_SEM] * (2 * n) + [_ANY],
        out_specs=[_HBM] * n,
        out_shape=[pltpu.HBM(g[1].shape, BF16) for g in group],
        input_output_aliases={n + i: i for i in range(n)},
        compiler_params=pltpu.CompilerParams(has_side_effects=_EFFECT),
    )(*[g[0] for g in group], *[g[1] for g in group], *[g[2] for g in group], *[g[3] for g in group], after)
    return list(out)


def _final_add(inbox, own, pos, name, tr):
    _, r, cdim = inbox.shape

    def body(pos_ref, a_ref, b_ref, c_ref, own_ref, o_ref):
        total = (a_ref[0].astype(F32) + b_ref[0].astype(F32)) + c_ref[0].astype(F32)
        o_ref[0] = total + own_ref[...]

    def slot(flip):
        return pl.BlockSpec((1, tr, cdim), lambda i, pos: (jnp.bitwise_xor(pos[1], flip), i, 0))

    return pl.pallas_call(
        body, name=name,
        grid_spec=pltpu.PrefetchScalarGridSpec(
            num_scalar_prefetch=1, grid=(r // tr,),
            in_specs=[slot(1), slot(2), slot(3), pl.BlockSpec((tr, cdim), lambda i, pos: (i, 0))],
            out_specs=pl.BlockSpec((1, tr, cdim), lambda i, pos: (pos[0], i, 0))),
        out_shape=jax.ShapeDtypeStruct((2, r, cdim), F32), compiler_params=_params(),
    )(pos, inbox, inbox, inbox, own)


def _sibling_share(shards, name):
    n = len(shards)

    def body(*refs):
        outs = refs[n:2 * n]
        send, recv = refs[2 * n:]
        x, y, c = _position()
        copies = [_remote(outs[i].at[c], outs[i].at[c], send.at[i], recv.at[i], (x, y, 1 - c)) for i in range(n)]
        for cp in copies:
            cp.start()
        for i in range(n):
            theirs = outs[i].at[1 - c]
            _remote(theirs, theirs, send.at[i], recv.at[i], (x, y, 1 - c)).wait_recv()
        for cp in copies:
            cp.wait_send()

    return pl.pallas_call(
        body, name=name, in_specs=[_ANY] * n, out_specs=[_ANY] * n,
        out_shape=[jax.ShapeDtypeStruct(s.shape, F32) for s in shards],
        input_output_aliases={i: i for i in range(n)},
        scratch_shapes=[pltpu.SemaphoreType.DMA((n,))] * 2,
    )(*shards)


_SMALL = (("b_ada", N_MOD * D_MODEL), ("norm1_pre", D_MODEL), ("norm1_post", D_MODEL), ("norm2_pre", D_MODEL),
          ("norm2_post", D_MODEL), ("w_spatial", N_HEADS * CHUNK * CHUNK), ("b_spatial", N_HEADS * CHUNK),
          ("ln_v_gain", D_A), ("ln_v_bias", D_A), ("w_pool", N_HEADS * GROUP_DIM * GROUP_DIM),
          ("b_pool", D_B), ("pool_scale", D_B))
_MOD_ROWS = N_MOD * D_MODEL // 128


def _packed_rows(size):
    return -(-(size // 128) // 8) * 8


def _pack(parts):
    out = []
    for name, size in _SMALL:
        a = parts[name].reshape(size // 128, 128)
        pad = _packed_rows(size) - a.shape[0]
        out.append(jnp.pad(a, ((0, pad), (0, 0))) if pad else a)
    return out


def _small_adamw(slots, ws, ms, vs):
    n = len(_SMALL)
    head = N_DEV * _MOD_ROWS

    def body(*refs):
        s_ref, w, m, v = refs[0], refs[1:1 + n], refs[1 + n:1 + 2 * n], refs[1 + 2 * n:1 + 3 * n]
        outs = refs[1 + 3 * n:1 + 7 * n]
        dmod_ref, loss_ref, t_ref = refs[1 + 7 * n:]
        t_ref[...] = ((s_ref[0] + s_ref[1]) + s_ref[2]) + s_ref[3]
        dmod_ref[...] = t_ref[0:head, :]
        loss_ref[...] = t_ref[t_ref.shape[0] - 8:, :]
        row = head
        for i, (_, size) in enumerate(_SMALL):
            if i == 0:
                g = t_ref[0:_MOD_ROWS, :]
                for b in range(1, N_DEV):
                    g = g + t_ref[b * _MOD_ROWS:(b + 1) * _MOD_ROWS, :]
            else:
                g = t_ref[row:row + size // 128, :]
                row += _packed_rows(size)
            d, nm, nv = _adamw_math(w[i][...], g, m[i][...], v[i][...])
            for ref, val in zip(outs[4 * i:4 * i + 4], (g, d, nm, nv)):
                ref[...] = val

    each = [jax.ShapeDtypeStruct((size // 128, 128), F32) for _, size in _SMALL for _ in range(4)]
    out = pl.pallas_call(
        body, name="small_adamw",
        out_shape=each + [jax.ShapeDtypeStruct((head, 128), F32), jax.ShapeDtypeStruct((8, 128), F32)],
        scratch_shapes=[pltpu.VMEM(slots.shape[1:], F32)],
        compiler_params=pltpu.CompilerParams(vmem_limit_bytes=VMEM_LIMIT),
    )(slots, *ws, *ms, *vs)
    return [out[4 * i:4 * i + 4] for i in range(n)], out[4 * n], out[4 * n + 1]


def kernel(x, c, w_ada, b_ada, norm1_pre, norm1_post, w_in, w_spatial, b_spatial, ln_v_gain, ln_v_bias, w_pool, b_pool, pool_scale, w_out, norm2_pre, norm2_post, w_fc1, w_fc2, loss_target, m_w_ada, m_b_ada, m_norm1_pre, m_norm1_post, m_w_in, m_w_spatial, m_b_spatial, m_ln_v_gain, m_ln_v_bias, m_w_pool, m_b_pool, m_pool_scale, m_w_out, m_norm2_pre, m_norm2_post, m_w_fc1, m_w_fc2, v_w_ada, v_b_ada, v_norm1_pre, v_norm1_post, v_w_in, v_w_spatial, v_b_spatial, v_ln_v_gain, v_ln_v_bias, v_w_pool, v_b_pool, v_pool_scale, v_w_out, v_norm2_pre, v_norm2_post, v_w_fc1, v_w_fc2):
    weights = dict(w_ada=w_ada, b_ada=b_ada, norm1_pre=norm1_pre, norm1_post=norm1_post, w_in=w_in,
                   w_spatial=w_spatial, b_spatial=b_spatial, ln_v_gain=ln_v_gain, ln_v_bias=ln_v_bias, w_pool=w_pool,
                   b_pool=b_pool, pool_scale=pool_scale, w_out=w_out, norm2_pre=norm2_pre, norm2_post=norm2_post,
                   w_fc1=w_fc1, w_fc2=w_fc2)
    m_old = dict(w_ada=m_w_ada, b_ada=m_b_ada, norm1_pre=m_norm1_pre, norm1_post=m_norm1_post, w_in=m_w_in,
                 w_spatial=m_w_spatial, b_spatial=m_b_spatial, ln_v_gain=m_ln_v_gain, ln_v_bias=m_ln_v_bias,
                 w_pool=m_w_pool, b_pool=m_b_pool, pool_scale=m_pool_scale, w_out=m_w_out, norm2_pre=m_norm2_pre,
                 norm2_post=m_norm2_post, w_fc1=m_w_fc1, w_fc2=m_w_fc2)
    v_old = dict(w_ada=v_w_ada, b_ada=v_b_ada, norm1_pre=v_norm1_pre, norm1_post=v_norm1_post, w_in=v_w_in,
                 w_spatial=v_w_spatial, b_spatial=v_b_spatial, ln_v_gain=v_ln_v_gain, ln_v_bias=v_ln_v_bias,
                 w_pool=v_w_pool, b_pool=v_b_pool, pool_scale=v_pool_scale, w_out=v_w_out, norm2_pre=v_norm2_pre,
                 norm2_post=v_norm2_post, w_fc1=v_w_fc1, w_fc2=v_w_fc2)
    order = ("w_ada", "b_ada", "norm1_pre", "norm1_post", "w_in", "w_spatial", "b_spatial", "ln_v_gain", "ln_v_bias",
             "w_pool", "b_pool", "pool_scale", "w_out", "norm2_pre", "norm2_post", "w_fc1", "w_fc2")
    big = ("w_in", "w_out", "w_fc1", "w_fc2")
    mx, my, mc = _position()
    me = 4 * mx + 2 * my + mc
    chip = 2 * mx + my
    row = lambda a: a.reshape(1, -1)

    pos = jnp.stack([mc, chip]).astype(jnp.int32)
    xs, target = x[0], loss_target[0]
    n1pre, n1post, n2pre, n2post = row(norm1_pre), row(norm1_post), row(norm2_pre), row(norm2_post)
    mixer = (w_spatial, jnp.repeat(b_spatial.T, HEAD_DIM, axis=1), row(ln_v_gain), row(ln_v_bias), w_pool,
             row(b_pool), row(pool_scale))
    ts_big, ts_mid = 512, 256

    mod4, sc_all = _mod_exchange(c, w_ada, row(b_ada))
    mod6 = mod4.reshape(N_MOD, D_MODEL)
    ag = _ag_start([_cast_to_slot(weights[n], pos, mod4, "cast_" + n, 256) for n in big[:2]], mod4, "ag_start_mix")
    ag += _ag_start([_cast_to_slot(weights[n], pos, ag[0][0], "cast_" + n, 256) for n in big[2:]], ag[0][0],
                    "ag_start_mlp")

    win_g, wout_g = _ag_done(_ag_pass([ag[0], ag[1]], ag[2][0], "ag_pass_mix"), "ag_done_mix")
    h1, z, ycat, mix, x1, h2 = _fwd_mix(xs, mod6, n1pre, n1post, n2pre, win_g, wout_g, *mixer, ts_mid)
    (fc1_g,) = _ag_done(_ag_pass([ag[2]], h2, "ag_pass_fc1"), "ag_done_fc1")
    q = _fwd_fc1(h2, fc1_g, ts_big)
    (fc2_g,) = _ag_done(_ag_pass([ag[3]], q, "ag_pass_fc2"), "ag_done_fc2")
    dy, df, loss, dgate2, dg2post = _fwd_fc2_loss(q, x1, target, mod6, n2post, fc2_g, ts_big)

    def reduce_start(partials, tag, small=None):
        wire, owns, slots = _sibling_sum(partials, "sibling_sum_" + tag, small)
        return _rs_start(wire, "rs_start_" + tag), owns, slots

    def reduce_finish(state, owns, names, tag, dep):
        inboxes = _rs_wait(state, dep, "rs_wait_" + tag)
        halves = [_final_add(ib, own, pos, "final_add_" + n, min(256, own.shape[0]))
                  for ib, own, n in zip(inboxes, owns, names)]
        shards = _sibling_share(halves, "sibling_share_" + tag)
        for n, g in zip(names, shards):
            grads[n], deltas[n], new_m[n], new_v[n] = _adamw(g.reshape(weights[n].shape), weights[n], m_old[n],
                                                             v_old[n], "adamw_" + n, 256)

    grads, deltas, new_m, new_v = {}, {}, {}, {}
    dp, g_fc2 = _bwd_fc2(df, q, fc2_g, ts_big)
    state_fc2, owns_fc2, _ = reduce_start([g_fc2], "fc2")
    dx1, dyc, dshift2, da2, dgate1, dg1post, g_fc1, g_out = _bwd_fc1_out(
        dp, dy, x1, mix, h2, ycat, mod6, n2pre, n1post, fc1_g, wout_g, state_fc2[0][0], ts_mid)
    state_mid, owns_mid, _ = reduce_start([g_fc1, g_out], "mid")
    dz, dws, dbsp, dgain, dbias, dwp, dbp, dps = _mixer_bwd(z, dyc, *mixer, state_mid[0][0], ts_mid)
    grad_x, dshift1, da1, g_in = _bwd_in(dz, dx1, xs, h1, mod6, n1pre, win_g, state_mid[0][0], ts_mid)
    dmod6, dnorms = _mod_grads(da1, dshift1, dgate1, dg1post, da2, dshift2, dgate2, dg2post, mod6, n1pre, n2pre)

    parts = dict(b_ada=dmod6, norm1_pre=dnorms[0], norm1_post=dnorms[1], norm2_pre=dnorms[2], norm2_post=dnorms[3],
                 w_spatial=dws, b_spatial=dbsp, ln_v_gain=dgain, ln_v_bias=dbias, w_pool=dwp, b_pool=dbp,
                 pool_scale=dps)
    pieces = _pack(parts)
    slots = lax.dynamic_update_slice(jnp.zeros((N_DEV * _MOD_ROWS, 128), F32), pieces[0], (me * _MOD_ROWS, 0))
    loss_tile = jnp.pad(loss, ((0, 7), (0, 127)))
    state_in, owns_in, pair_sum = reduce_start(
        [g_in], "in", jnp.concatenate([slots] + pieces[1:] + [loss_tile], axis=0))
    spread = _small_spread_start(pair_sum[0])
    reduce_finish(state_fc2 + state_mid, owns_fc2 + owns_mid, ("w_fc2", "w_fc1", "w_out"), "mlp", spread[0])
    flat = lambda d: [d[n].reshape(size // 128, 128) for n, size in _SMALL]
    small_out, dmod_all, loss_tile = _small_adamw(
        _small_spread_wait(*spread, deltas["w_out"]), flat(weights), flat(m_old), flat(v_old))
    loss = loss_tile[0, 0]
    for (n, _), (g, d, nm, nv) in zip(_SMALL, small_out):
        shape = weights[n].shape
        grads[n], deltas[n], new_m[n], new_v[n] = g.reshape(shape), d.reshape(shape), nm.reshape(shape), nv.reshape(shape)

    dmod_all = dmod_all.reshape(N_DEV, N_MOD * D_MODEL)
    cs = w_ada.shape[1]
    dmod_shard = lax.dynamic_slice(dmod_all, (0, chip * cs), (N_DEV, cs))
    sc_t = sc_all.reshape(N_DEV, D_MODEL).T
    grads["w_ada"], deltas["w_ada"], new_m["w_ada"], new_v["w_ada"] = _ada_grad_adamw(
        sc_t, dmod_shard, w_ada, m_w_ada, v_w_ada, 256)

    reduce_finish(state_in, owns_in, ("w_in",), "in", deltas["w_ada"])

    return (loss, grad_x[None], *[grads[n] for n in order], *[deltas[n] for n in order],
            *[new_m[n] for n in order], *[new_v[n] for n in order])
```

```python
import functools

import jax
import jax.numpy as jnp
from jax import lax
from jax.experimental import pallas as pl
from jax.experimental.pallas import tpu as pltpu

F32 = jnp.float32
BF16 = jnp.bfloat16
MESH = pl.DeviceIdType.MESH

D_MODEL = 1024
D_A = 512
D_B = 512
D_Z = 2 * D_A + D_B
N_HEADS = 4
HEAD_DIM = 128
CHUNK = 128
POOL_WINDOWS = (2, 4, 8, 16)
GROUP_DIM = 128
D_FF = 4096
N_MOD = 6
EPS = 1e-6
HALO = 16
N_CHIPS = 4
N_DEV = 8

ADAM_LR = 0.001
ADAM_B1 = 0.9
ADAM_B2 = 0.999
ADAM_EPS = 1e-08
ADAM_WD = 0.01
ADAM_STEP = 10

VMEM_LIMIT = 56 * 1024 * 1024

_VMEM = pl.BlockSpec(memory_space=pltpu.VMEM)
_ANY = pl.BlockSpec(memory_space=pl.ANY)


def _params(n_grid_axes=1):
    return pltpu.CompilerParams(dimension_semantics=("arbitrary",) * n_grid_axes, vmem_limit_bytes=VMEM_LIMIT)


def _rows(ts, width):
    return pl.BlockSpec((ts, width), lambda i: (i, 0))


def _const(shape):
    return pl.BlockSpec(shape, lambda i: (0,) * len(shape))


def _dot(a, b):
    return jnp.dot(a, b, preferred_element_type=F32)


def _dot_nt(a, b):
    return lax.dot_general(a, b, (((1,), (1,)), ((), ())), preferred_element_type=F32)


def _dot_tn(a, b):
    return lax.dot_general(a, b, (((0,), (0,)), ((), ())), preferred_element_type=F32)


def _rowmean(v):
    return jnp.mean(v, axis=-1, keepdims=True)


def _colsum(v):
    return jnp.sum(v, axis=0, keepdims=True)


def _gelu_parts(z):
    k0 = 0.7978845608028654
    k1 = 0.044715
    z2 = z * z
    t = jnp.tanh(k0 * (z + k1 * z * z2))
    g = 0.5 * z * (1.0 + t)
    dg = 0.5 * (1.0 + t) + 0.5 * z * (1.0 - t * t) * (k0 * (1.0 + 3.0 * k1 * z2))
    return g, dg


def _tril_weights(ws_ref):
    r = lax.broadcasted_iota(jnp.int32, (CHUNK, CHUNK), 0)
    s = lax.broadcasted_iota(jnp.int32, (CHUNK, CHUNK), 1)
    mask = (s <= r).astype(F32)
    return [(ws_ref[h] * mask).astype(BF16) for h in range(N_HEADS)]


def _window_counts(first_row, n_rows):
    pos = (first_row + lax.broadcasted_iota(jnp.int32, (n_rows, 1), 0)).astype(F32)
    return pos, [1.0 / jnp.minimum(pos + 1.0, float(w)) for w in POOL_WINDOWS]


def _causal_window_sums(ext):
    out = []
    e = ext
    shift = 1
    for g in range(len(POOL_WINDOWS)):
        e = e + pltpu.roll(e, shift, 0)
        shift *= 2
        out.append(e[:, g * GROUP_DIM:(g + 1) * GROUP_DIM])
    return out


def _anticausal_window_sums(ext):
    n = ext.shape[0]
    out = []
    e = ext
    shift = 1
    for g in range(len(POOL_WINDOWS)):
        e = e + pltpu.roll(e, n - shift, 0)
        shift *= 2
        out.append(e[:, g * GROUP_DIM:(g + 1) * GROUP_DIM])
    return out


def _fwd_mix(x, mod6, n1pre, n1post, n2pre, win_g, wout_g, w_spatial, bsp_full, gain, bias, w_pool, b_pool, pool_scale, ts):
    s_len = x.shape[0]
    rs = D_MODEL // N_CHIPS

    def body(x_ref, mod_ref, g1pre_ref, g1post_ref, g2pre_ref, win_ref, wout_ref, ws_ref, bsp_ref, gain_ref,
             bias_ref, wp_ref, bp_ref, ps_ref, h1_ref, z_ref, y_ref, mix_ref, x1_ref, h2_ref, mixed_ref, prev_ref,
             wfull_ref):
        i = pl.program_id(0)
        _zero_on_first_step(prev_ref)
        _join_w_in_on_first_step(win_ref, wfull_ref)
        xv = x_ref[...]
        r = lax.rsqrt(_rowmean(xv * xv) + EPS)
        hb = (((xv * r) * g1pre_ref[...]) * (1.0 + mod_ref[1:2, :]) + mod_ref[0:1, :]).astype(BF16)
        h1_ref[...] = hb
        z_ref[...] = _dot(hb, wfull_ref[...])

        wc = _tril_weights(ws_ref)
        u, _, _, _, _ = _mixer_forward_tile(z_ref[:, :2 * D_A], wc, bsp_ref, gain_ref[...], bias_ref[...], mixed_ref)
        y_ref[:, :D_A] = (u * mixed_ref[...]).astype(BF16)
        zb = z_ref[:, 2 * D_A:]
        sums = _causal_window_sums(jnp.concatenate([prev_ref[...], zb], axis=0))
        prev_ref[...] = zb[ts - HALO:, :]
        _, inv_counts = _window_counts(i * ts, ts)
        for g in range(len(POOL_WINDOWS)):
            lanes = slice(g * GROUP_DIM, (g + 1) * GROUP_DIM)
            diff = sums[g][HALO:, :] * inv_counts[g] - zb[:, lanes]
            lin = _dot(diff.astype(BF16), wp_ref[g].astype(BF16)) + bp_ref[:, lanes]
            y_ref[:, D_A + g * GROUP_DIM:D_A + (g + 1) * GROUP_DIM] = (lin * ps_ref[:, lanes]).astype(BF16)

        mix = None
        for j in range(N_CHIPS):
            part = _dot(y_ref[:, j * rs:(j + 1) * rs], wout_ref[j])
            mix = part if mix is None else mix + part
        mix_ref[...] = mix
        r2 = lax.rsqrt(_rowmean(mix * mix) + EPS)
        x1 = xv + mod_ref[2:3, :] * ((mix * r2) * g1post_ref[...])
        x1_ref[...] = x1
        r3 = lax.rsqrt(_rowmean(x1 * x1) + EPS)
        h2_ref[...] = (((x1 * r3) * g2pre_ref[...]) * (1.0 + mod_ref[4:5, :]) + mod_ref[3:4, :]).astype(BF16)

    vec = _const((1, D_MODEL))
    f32_rows = jax.ShapeDtypeStruct((s_len, D_MODEL), F32)
    bf16_rows = jax.ShapeDtypeStruct((s_len, D_MODEL), BF16)
    return pl.pallas_call(
        body, name="fwd_mix", grid=(s_len // ts,),
        in_specs=[_rows(ts, D_MODEL), _const((N_MOD, D_MODEL)), vec, vec, vec, _VMEM, _VMEM,
                  _const((N_HEADS, CHUNK, CHUNK)), _const((CHUNK, D_A)), _const((1, D_A)), _const((1, D_A)),
                  _const((N_HEADS, GROUP_DIM, GROUP_DIM)), _const((1, D_B)), _const((1, D_B))],
        out_specs=[_rows(ts, D_MODEL), _rows(ts, D_Z), _rows(ts, D_MODEL), _rows(ts, D_MODEL), _rows(ts, D_MODEL),
                   _rows(ts, D_MODEL)],
        out_shape=[bf16_rows, jax.ShapeDtypeStruct((s_len, D_Z), F32), bf16_rows, f32_rows, f32_rows, bf16_rows],
        scratch_shapes=[pltpu.VMEM((ts, D_A), F32), pltpu.VMEM((HALO, D_B), F32), pltpu.VMEM((D_MODEL, D_Z), BF16)],
        compiler_params=_params(),
    )(x, mod6, n1pre, n1post, n2pre, win_g, wout_g, w_spatial, bsp_full, gain, bias, w_pool, b_pool, pool_scale)


def _mixer_forward_tile(za, wc, bsp_ref, gain, bias, mixed_ref):
    ga, dga = _gelu_parts(za)
    u = ga[:, :D_A]
    v = ga[:, D_A:]
    mu = _rowmean(v)
    vc = v - mu
    rstd = lax.rsqrt(_rowmean(vc * vc) + EPS)
    vhat = vc * rstd
    vn = (vhat * gain + bias).astype(BF16)
    ts = za.shape[0]
    for k in range(ts // CHUNK):
        for h in range(N_HEADS):
            blk = vn[k * CHUNK:(k + 1) * CHUNK, h * HEAD_DIM:(h + 1) * HEAD_DIM]
            mixed_ref[k * CHUNK:(k + 1) * CHUNK, h * HEAD_DIM:(h + 1) * HEAD_DIM] = (
                _dot(wc[h], blk) + bsp_ref[:, h * HEAD_DIM:(h + 1) * HEAD_DIM])
    return u, vhat, rstd, vn, dga


def _fwd_fc1(h2, fc1_g, ts):
    s_len = h2.shape[0]
    cs = D_FF // N_CHIPS

    def body(h_ref, w_ref, q_ref):
        hb = h_ref[...]
        for j in range(N_CHIPS):
            p = jnp.maximum(_dot(hb, w_ref[j]), 0.0)
            q_ref[:, j * cs:(j + 1) * cs] = (p * p).astype(BF16)

    return pl.pallas_call(
        body, name="fwd_fc1", grid=(s_len // ts,),
        in_specs=[_rows(ts, D_MODEL), _VMEM],
        out_specs=_rows(ts, D_FF),
        out_shape=jax.ShapeDtypeStruct((s_len, D_FF), BF16),
        compiler_params=_params(),
    )(h2, fc1_g)


def _fwd_fc2_loss(q, x1, target, mod6, n2post, fc2_g, ts):
    s_len = q.shape[0]
    rs = D_FF // N_CHIPS

    def body(q_ref, x1_ref, t_ref, mod_ref, g_ref, w_ref, dy_ref, df_ref, loss_ref, dgate_ref, dg_ref):
        _zero_on_first_step(loss_ref, dgate_ref, dg_ref)
        gate = mod_ref[5:6, :]
        gn = g_ref[...]
        f = _dot(q_ref[:, 0:rs], w_ref[0])
        for j in range(1, N_CHIPS):
            f = f + _dot(q_ref[:, j * rs:(j + 1) * rs], w_ref[j])
        r4 = lax.rsqrt(_rowmean(f * f) + EPS)
        fh = f * r4
        err = (x1_ref[...] + gate * (fh * gn)) - t_ref[...]
        loss_ref[...] += 0.5 * jnp.sum(_rowmean(err * err), axis=0, keepdims=True)
        dy = err * (1.0 / D_MODEL)
        dy_ref[...] = dy
        dgate_ref[...] += _colsum(dy * (fh * gn))
        dg_ref[...] += _colsum((dy * gate) * fh)
        gh = (dy * gate) * gn
        df_ref[...] = (r4 * (gh - fh * _rowmean(gh * fh))).astype(BF16)

    return pl.pallas_call(
        body, name="fwd_fc2_loss", grid=(s_len // ts,),
        in_specs=[_rows(ts, D_FF), _rows(ts, D_MODEL), _rows(ts, D_MODEL), _const((N_MOD, D_MODEL)),
                  _const((1, D_MODEL)), _VMEM],
        out_specs=[_rows(ts, D_MODEL), _rows(ts, D_MODEL), _const((1, 1)), _const((1, D_MODEL)), _const((1, D_MODEL))],
        out_shape=[jax.ShapeDtypeStruct((s_len, D_MODEL), F32), jax.ShapeDtypeStruct((s_len, D_MODEL), BF16),
                   jax.ShapeDtypeStruct((1, 1), F32), jax.ShapeDtypeStruct((1, D_MODEL), F32),
                   jax.ShapeDtypeStruct((1, D_MODEL), F32)],
        compiler_params=_params(),
    )(q, x1, target, mod6, n2post, fc2_g)


def _join_w_in_on_first_step(win_ref, full_ref):
    cs = D_Z // N_CHIPS

    @pl.when(pl.program_id(0) == 0)
    def _():
        for j in range(N_CHIPS):
            full_ref[:, j * cs:(j + 1) * cs] = win_ref[j]


def _zero_on_first_step(*refs):
    @pl.when(pl.program_id(0) == 0)
    def _():
        for ref in refs:
            ref[...] = jnp.zeros_like(ref)


def _store_on_last_step(*pairs):
    @pl.when(pl.program_id(0) == pl.num_programs(0) - 1)
    def _():
        for acc_ref, hbm_ref in pairs:
            pltpu.sync_copy(acc_ref, hbm_ref)


def _bwd_fc2(df, q, fc2_g, ts):
    s_len = df.shape[0]
    cs = D_FF // N_CHIPS

    def body(df_ref, q_ref, w_ref, dp_ref, dw_hbm, dw_ref):
        _zero_on_first_step(dw_ref)
        dfb = df_ref[...]
        for j in range(N_CHIPS):
            qb = q_ref[:, j * cs:(j + 1) * cs]
            dw_ref[j] += _dot_tn(qb, dfb).reshape(2, cs // 2, D_MODEL)
            dq = _dot_nt(dfb, w_ref[j])
            dp_ref[:, j * cs:(j + 1) * cs] = (dq * (2.0 * jnp.sqrt(qb.astype(F32)))).astype(BF16)
        _store_on_last_step((dw_ref, dw_hbm))

    dw_shape = (N_CHIPS, 2, cs // 2, D_MODEL)
    return pl.pallas_call(
        body, name="bwd_fc2", grid=(s_len // ts,),
        in_specs=[_rows(ts, D_MODEL), _rows(ts, D_FF), _VMEM],
        out_specs=[_rows(ts, D_FF), _ANY],
        out_shape=[jax.ShapeDtypeStruct((s_len, D_FF), BF16), jax.ShapeDtypeStruct(dw_shape, F32)],
        scratch_shapes=[pltpu.VMEM(dw_shape, F32)],
        compiler_params=_params(),
    )(df, q, fc2_g)


def _bwd_fc1_out(dp, dy, x1, mix, h2, ycat, mod6, n2pre, n1post, fc1_g, wout_g, dep, ts):
    s_len = dp.shape[0]
    cs = D_FF // N_CHIPS
    rs = D_MODEL // N_CHIPS

    def body(dp_ref, dy_ref, x1_ref, mix_ref, h2_ref, yc_ref, mod_ref, g2_ref, g1_ref, w1_ref, wo_ref, dep_ref,
             dx1_ref, dyc_ref, dshift2_ref, da2_ref, dgate1_ref, dg1_ref, dw1_hbm, dwo_hbm, dw1_ref, dwo_ref):
        _zero_on_first_step(dshift2_ref, da2_ref, dgate1_ref, dg1_ref, dw1_ref, dwo_ref)
        h2b = h2_ref[...]
        dh2 = None
        for j in range(N_CHIPS):
            dpb = dp_ref[:, j * cs:(j + 1) * cs]
            dw1_ref[j] += _dot_tn(h2b, dpb).reshape(2, D_MODEL // 2, cs)
            part = _dot_nt(dpb, w1_ref[j])
            dh2 = part if dh2 is None else dh2 + part
        x1 = x1_ref[...]
        r3 = lax.rsqrt(_rowmean(x1 * x1) + EPS)
        xh = x1 * r3
        a2 = g2_ref[...] * (1.0 + mod_ref[4:5, :])
        dshift2_ref[...] += _colsum(dh2)
        da2_ref[...] += _colsum(dh2 * xh)
        dxh = dh2 * a2
        dx1 = dy_ref[...] + r3 * (dxh - xh * _rowmean(dxh * xh))
        dx1_ref[...] = dx1

        mix = mix_ref[...]
        r2 = lax.rsqrt(_rowmean(mix * mix) + EPS)
        mh = mix * r2
        gate = mod_ref[2:3, :]
        gn = g1_ref[...]
        dgate1_ref[...] += _colsum(dx1 * (mh * gn))
        dg1_ref[...] += _colsum((dx1 * gate) * mh)
        gh = (dx1 * gate) * gn
        dmix = (r2 * (gh - mh * _rowmean(gh * mh))).astype(BF16)
        dwo_ref[...] += _dot_tn(yc_ref[...], dmix).reshape(N_CHIPS, 2, rs // 2, D_MODEL)
        for j in range(N_CHIPS):
            dyc_ref[:, j * rs:(j + 1) * rs] = _dot_nt(dmix, wo_ref[j])
        _store_on_last_step((dw1_ref, dw1_hbm), (dwo_ref, dwo_hbm))

    vec = jax.ShapeDtypeStruct((1, D_MODEL), F32)
    dw1_shape = (N_CHIPS, 2, D_MODEL // 2, cs)
    dwo_shape = (N_CHIPS, 2, rs // 2, D_MODEL)
    return pl.pallas_call(
        body, name="bwd_fc1_out", grid=(s_len // ts,),
        in_specs=[_rows(ts, D_FF), _rows(ts, D_MODEL), _rows(ts, D_MODEL), _rows(ts, D_MODEL), _rows(ts, D_MODEL),
                  _rows(ts, D_MODEL), _const((N_MOD, D_MODEL)), _const((1, D_MODEL)), _const((1, D_MODEL)), _VMEM,
                  _VMEM, _ANY],
        out_specs=[_rows(ts, D_MODEL), _rows(ts, D_MODEL)] + [_const((1, D_MODEL))] * 4 + [_ANY, _ANY],
        out_shape=[jax.ShapeDtypeStruct((s_len, D_MODEL), F32), jax.ShapeDtypeStruct((s_len, D_MODEL), F32),
                   vec, vec, vec, vec, jax.ShapeDtypeStruct(dw1_shape, F32), jax.ShapeDtypeStruct(dwo_shape, F32)],
        scratch_shapes=[pltpu.VMEM(dw1_shape, F32), pltpu.VMEM(dwo_shape, F32)],
        compiler_params=_params(),
    )(dp, dy, x1, mix, h2, ycat, mod6, n2pre, n1post, fc1_g, wout_g, dep)


def _mixer_bwd(z, dyc, w_spatial, bsp_full, gain, bias, w_pool, b_pool, pool_scale, dep, ts):
    s_len = z.shape[0]
    nb = ts // HALO
    last = s_len // HALO - 1
    te = ts + HALO

    def body(z_ref, zprev_ref, znext_ref, dyc_ref, dynext_ref, ws_ref, bsp_ref, gain_ref, bias_ref, wp_ref, bp_ref,
             ps_ref, dep_ref, dz_ref, dws_ref, dbsp_ref, dgain_ref, dbias_ref, dwp_ref, dbp_ref, dps_ref, mixed_ref,
             dvn_ref):
        i = pl.program_id(0)

        @pl.when(i == 0)
        def _():
            for ref in (dws_ref, dbsp_ref, dgain_ref, dbias_ref, dwp_ref, dbp_ref, dps_ref):
                ref[...] = jnp.zeros_like(ref)

        wc = _tril_weights(ws_ref)
        gain = gain_ref[...]
        u, vhat, rstd, vn, dga = _mixer_forward_tile(z_ref[:, :2 * D_A], wc, bsp_ref, gain, bias_ref[...], mixed_ref)
        dya = dyc_ref[:, :D_A]
        du = dya * mixed_ref[...]
        dmixed = dya * u
        dmb = dmixed.astype(BF16)
        dm_sum = dmixed[0:CHUNK, :]
        for k in range(1, ts // CHUNK):
            dm_sum = dm_sum + dmixed[k * CHUNK:(k + 1) * CHUNK, :]
        r_idx = lax.broadcasted_iota(jnp.int32, (CHUNK, CHUNK), 0)
        s_idx = lax.broadcasted_iota(jnp.int32, (CHUNK, CHUNK), 1)
        causal = (s_idx <= r_idx).astype(F32)
        for h in range(N_HEADS):
            lanes = slice(h * HEAD_DIM, (h + 1) * HEAD_DIM)
            dbsp_ref[h] += jnp.sum(dm_sum[:, lanes], axis=1, keepdims=True)
            acc = None
            for k in range(ts // CHUNK):
                rows = slice(k * CHUNK, (k + 1) * CHUNK)
                t = _dot_nt(dmb[rows, lanes], vn[rows, lanes])
                acc = t if acc is None else acc + t
                dvn_ref[rows, lanes] = _dot_tn(wc[h], dmb[rows, lanes])
            dws_ref[h] += acc * causal
        dvn = dvn_ref[...]
        dgain_ref[...] += _colsum(dvn * vhat)
        dbias_ref[...] += _colsum(dvn)
        dvh = dvn * gain
        dv = rstd * (dvh - _rowmean(dvh) - vhat * _rowmean(dvh * vhat))
        dz_ref[:, :D_A] = (du * dga[:, :D_A]).astype(BF16)
        dz_ref[:, D_A:2 * D_A] = (dv * dga[:, D_A:]).astype(BF16)

        zb = z_ref[:, 2 * D_A:]
        prev = jnp.where(i == 0, 0.0, zprev_ref[...])
        zb_ext = jnp.concatenate([zb, znext_ref[...]], axis=0)
        sums = _causal_window_sums(jnp.concatenate([prev, zb_ext], axis=0))
        pos, inv_counts = _window_counts(i * ts, te)
        dyb_ext = jnp.concatenate([dyc_ref[:, D_A:], dynext_ref[...]], axis=0)
        dlin_ext = dyb_ext * ps_ref[...]
        dbp_ref[...] += _colsum(dlin_ext[:ts, :])
        scaled = []
        ddiffs = []
        lins = []
        for g in range(len(POOL_WINDOWS)):
            lanes = slice(g * GROUP_DIM, (g + 1) * GROUP_DIM)
            diff = (sums[g][HALO:, :] * inv_counts[g] - zb_ext[:, lanes]).astype(BF16)
            wpb = wp_ref[g].astype(BF16)
            dlb = dlin_ext[:, lanes].astype(BF16)
            lins.append(_dot(diff[:ts, :], wpb) + bp_ref[:, lanes])
            dwp_ref[g] += _dot_tn(diff[:ts, :], dlb[:ts, :])
            dd = _dot_nt(dlb, wpb)
            ddiffs.append(dd)
            scaled.append(jnp.where(pos < float(s_len), dd * inv_counts[g], 0.0))
        dps_ref[...] += _colsum(dyb_ext[:ts, :] * jnp.concatenate(lins, axis=1))
        back = _anticausal_window_sums(jnp.concatenate(scaled, axis=1))
        for g in range(len(POOL_WINDOWS)):
            dz_ref[:, 2 * D_A + g * GROUP_DIM:2 * D_A + (g + 1) * GROUP_DIM] = (
                back[g][:ts, :] - ddiffs[g][:ts, :]).astype(BF16)

    sq = jax.ShapeDtypeStruct((N_HEADS, CHUNK, CHUNK), F32)
    vec = jax.ShapeDtypeStruct((1, D_A), F32)
    return pl.pallas_call(
        body, name="mixer_bwd", grid=(s_len // ts,),
        in_specs=[_rows(ts, D_Z),
                  pl.BlockSpec((HALO, D_B), lambda i: (jnp.maximum(i * nb - 1, 0), 2)),
                  pl.BlockSpec((HALO, D_B), lambda i: (jnp.minimum((i + 1) * nb, last), 2)),
                  _rows(ts, D_MODEL),
                  pl.BlockSpec((HALO, D_B), lambda i: (jnp.minimum((i + 1) * nb, last), 1)),
                  _const((N_HEADS, CHUNK, CHUNK)), _const((CHUNK, D_A)), _const((1, D_A)), _const((1, D_A)),
                  _const((N_HEADS, GROUP_DIM, GROUP_DIM)), _const((1, D_B)), _const((1, D_B)), _ANY],
        out_specs=[_rows(ts, D_Z), _const((N_HEADS, CHUNK, CHUNK)), _const((N_HEADS, CHUNK, 1)), _const((1, D_A)),
                   _const((1, D_A)), _const((N_HEADS, GROUP_DIM, GROUP_DIM)), _const((1, D_B)), _const((1, D_B))],
        out_shape=[jax.ShapeDtypeStruct((s_len, D_Z), BF16), sq, jax.ShapeDtypeStruct((N_HEADS, CHUNK, 1), F32), vec,
                   vec, sq, vec, vec],
        scratch_shapes=[pltpu.VMEM((ts, D_A), F32), pltpu.VMEM((ts, D_A), F32)],
        compiler_params=_params(),
    )(z, z, z, dyc, dyc, w_spatial, bsp_full, gain, bias, w_pool, b_pool, pool_scale, dep)


def _bwd_in(dz, dx1, x, h1, mod6, n1pre, win_g, dep, ts):
    s_len = x.shape[0]
    cs = D_Z // N_CHIPS

    def body(dz_ref, dx1_ref, x_ref, h1_ref, mod_ref, g_ref, w_ref, dep_ref, gx_ref, dshift_ref, da_ref, dw_hbm,
             dw_ref, wfull_ref):
        _zero_on_first_step(dshift_ref, da_ref, dw_ref)
        _join_w_in_on_first_step(w_ref, wfull_ref)
        dzb = dz_ref[...]
        dw = _dot_tn(h1_ref[...], dzb)
        for j in range(N_CHIPS):
            dw_ref[j] += dw[:, j * cs:(j + 1) * cs].reshape(2, D_MODEL // 2, cs)
        dh = _dot_nt(dzb, wfull_ref[...])
        xv = x_ref[...]
        r = lax.rsqrt(_rowmean(xv * xv) + EPS)
        xh = xv * r
        a1 = g_ref[...] * (1.0 + mod_ref[1:2, :])
        dshift_ref[...] += _colsum(dh)
        da_ref[...] += _colsum(dh * xh)
        dxh = dh * a1
        gx_ref[...] = dx1_ref[...] + r * (dxh - xh * _rowmean(dxh * xh))
        _store_on_last_step((dw_ref, dw_hbm))

    vec = jax.ShapeDtypeStruct((1, D_MODEL), F32)
    dw_shape = (N_CHIPS, 2, D_MODEL // 2, cs)
    return pl.pallas_call(
        body, name="bwd_in", grid=(s_len // ts,),
        in_specs=[_rows(ts, D_Z), _rows(ts, D_MODEL), _rows(ts, D_MODEL), _rows(ts, D_MODEL),
                  _const((N_MOD, D_MODEL)), _const((1, D_MODEL)), _VMEM, _ANY],
        out_specs=[_rows(ts, D_MODEL), _const((1, D_MODEL)), _const((1, D_MODEL)), _ANY],
        out_shape=[jax.ShapeDtypeStruct((s_len, D_MODEL), F32), vec, vec, jax.ShapeDtypeStruct(dw_shape, F32)],
        scratch_shapes=[pltpu.VMEM(dw_shape, F32), pltpu.VMEM((D_MODEL, D_Z), BF16)],
        compiler_params=_params(),
    )(dz, dx1, x, h1, mod6, n1pre, win_g, dep)


def _adamw_math(w, g, m, v):
    m = ADAM_B1 * m + (1.0 - ADAM_B1) * g
    v = ADAM_B2 * v + (1.0 - ADAM_B2) * (g * g)
    m_hat = m / (1.0 - ADAM_B1 ** ADAM_STEP)
    v_hat = v / (1.0 - ADAM_B2 ** ADAM_STEP)
    delta = -ADAM_LR * (m_hat / (jnp.sqrt(v_hat) + ADAM_EPS) + ADAM_WD * w)
    return delta, m, v


def _adamw(g, w, m, v, name, tr):
    rows, cols = w.shape

    def body(g_ref, w_ref, m_ref, v_ref, go_ref, d_ref, nm_ref, nv_ref):
        g = g_ref[...]
        go_ref[...] = g
        d, nm, nv = _adamw_math(w_ref[...], g, m_ref[...], v_ref[...])
        d_ref[...] = d
        nm_ref[...] = nm
        nv_ref[...] = nv

    spec = _rows(tr, cols)
    shape = jax.ShapeDtypeStruct((rows, cols), F32)
    return pl.pallas_call(
        body, name=name, grid=(rows // tr,), in_specs=[spec] * 4, out_specs=[spec] * 4, out_shape=[shape] * 4,
        compiler_params=_params(),
    )(g, w, m, v)


def _ada_grad_adamw(sc_t, dmod_shard, w, m, v, tr):
    rows, cols = w.shape

    def body(s_ref, dm_ref, w_ref, m_ref, v_ref, g_ref, d_ref, nm_ref, nv_ref):
        g = s_ref[:, 0:1] * dm_ref[0:1, :]
        for b in range(1, N_DEV):
            g = g + s_ref[:, b:b + 1] * dm_ref[b:b + 1, :]
        g_ref[...] = g
        d, nm, nv = _adamw_math(w_ref[...], g, m_ref[...], v_ref[...])
        d_ref[...] = d
        nm_ref[...] = nm
        nv_ref[...] = nv

    spec = _rows(tr, cols)
    shape = jax.ShapeDtypeStruct((rows, cols), F32)
    return pl.pallas_call(
        body, name="ada_grad_adamw", grid=(rows // tr,),
        in_specs=[_rows(tr, N_DEV), _const((N_DEV, cols)), spec, spec, spec],
        out_specs=[spec] * 4, out_shape=[shape] * 4, compiler_params=_params(),
    )(sc_t, dmod_shard, w, m, v)


def _mod_grads(da1, dshift1, dgate1, dg1post, da2, dshift2, dgate2, dg2post, mod6, n1pre, n2pre):
    def body(da1_ref, ds1_ref, dgt1_ref, dg1_ref, da2_ref, ds2_ref, dgt2_ref, dg2_ref, mod_ref, n1_ref, n2_ref,
             dmod_ref, dn_ref):
        dmod_ref[0:1, :] = ds1_ref[...]
        dmod_ref[1:2, :] = da1_ref[...] * n1_ref[...]
        dmod_ref[2:3, :] = dgt1_ref[...]
        dmod_ref[3:4, :] = ds2_ref[...]
        dmod_ref[4:5, :] = da2_ref[...] * n2_ref[...]
        dmod_ref[5:6, :] = dgt2_ref[...]
        dn_ref[0:1, :] = da1_ref[...] * (1.0 + mod_ref[1:2, :])
        dn_ref[1:2, :] = dg1_ref[...]
        dn_ref[2:3, :] = da2_ref[...] * (1.0 + mod_ref[4:5, :])
        dn_ref[3:4, :] = dg2_ref[...]

    return pl.pallas_call(
        body, name="mod_grads",
        out_shape=[jax.ShapeDtypeStruct((N_MOD, D_MODEL), F32), jax.ShapeDtypeStruct((4, D_MODEL), F32)],
    )(da1, dshift1, dgate1, dg1post, da2, dshift2, dgate2, dg2post, mod6, n1pre, n2pre)


def _position():
    x, y, c = lax.axis_index("x"), lax.axis_index("y"), lax.axis_index("c")
    return x, y, c


def _flip(v, bit):
    return 1 - v if bit else v


def _peer(x, y, c, k):
    return (_flip(x, k & 4), _flip(y, k & 2), _flip(c, k & 1))


def _remote(src, dst, send_sem, recv_sem, device):
    return pltpu.make_async_remote_copy(src_ref=src, dst_ref=dst, send_sem=send_sem, recv_sem=recv_sem,
                                        device_id=device, device_id_type=MESH)


def _cast_to_slot(w, pos, dep, name, tr):
    rows, cols = w.shape

    def body(pos_ref, w_ref, dep_ref, o_ref):
        o_ref[0] = w_ref[...].astype(BF16)

    return pl.pallas_call(
        body, name=name,
        grid_spec=pltpu.PrefetchScalarGridSpec(
            num_scalar_prefetch=1, grid=(rows // tr,),
            in_specs=[pl.BlockSpec((tr, cols), lambda i, pos: (i, 0)), _ANY],
            out_specs=pl.BlockSpec((1, tr, cols), lambda i, pos: (pos[1], i, 0))),
        out_shape=jax.ShapeDtypeStruct((N_CHIPS, rows, cols), BF16), compiler_params=_params(),
    )(pos, w, dep)


def _mod_exchange(c_row, w_ada_shard, b_ada_row):
    cs = w_ada_shard.shape[1]

    def body(c_ref, w_ref, b_ref, mod_ref, sc_ref, rows_ref, send1, recv1, send2, recv2):
        x, y, c = _position()
        me = 4 * x + 2 * y + c
        chip = 2 * x + y
        cv = c_ref[...]
        sc_ref[me] = cv * jax.nn.sigmoid(cv)
        gather = [_remote(sc_ref.at[me], sc_ref.at[me], send1.at[k - 1], recv1.at[k - 1], _peer(x, y, c, k))
                  for k in range(1, N_DEV)]
        for cp in gather:
            cp.start()
        for k in range(1, N_DEV):
            px, py, pc = _peer(x, y, c, k)
            src = 4 * px + 2 * py + pc
            _remote(sc_ref.at[src], sc_ref.at[src], send1.at[k - 1], recv1.at[k - 1], (px, py, pc)).wait_recv()
        for cp in gather:
            cp.wait_send()
        sc_all = jnp.concatenate([sc_ref[b] for b in range(N_DEV)], axis=0)
        part = jnp.dot(sc_all, w_ref[...], precision=lax.Precision.HIGHEST, preferred_element_type=F32)
        part = part + b_ref[:, pl.ds(pl.multiple_of(chip * cs, 128), cs)]
        for b in range(N_DEV):
            rows_ref[b] = part[b:b + 1, :]
        mod_ref[chip] = rows_ref[me]
        hand = []
        for k in (2, 4, 6):
            px, py, _ = _peer(x, y, c, k)
            hand.append(_remote(rows_ref.at[4 * px + 2 * py + c], mod_ref.at[chip], send2.at[k // 2 - 1],
                                recv2.at[k // 2 - 1], (px, py, c)))
        for cp in hand:
            cp.start()
        for k in (2, 4, 6):
            px, py, _ = _peer(x, y, c, k)
            pchip = 2 * px + py
            _remote(rows_ref.at[me], mod_ref.at[pchip], send2.at[k // 2 - 1], recv2.at[k // 2 - 1],
                    (px, py, c)).wait_recv()
        for cp in hand:
            cp.wait_send()

    return pl.pallas_call(
        body, name="mod_exchange",
        in_specs=[_VMEM, _VMEM, _VMEM], out_specs=[_VMEM, _VMEM],
        out_shape=[jax.ShapeDtypeStruct((N_CHIPS, 1, cs), F32), jax.ShapeDtypeStruct((N_DEV, 1, D_MODEL), F32)],
        scratch_shapes=[pltpu.VMEM((N_DEV, 1, cs), F32), pltpu.SemaphoreType.DMA((N_DEV - 1,)),
                        pltpu.SemaphoreType.DMA((N_DEV - 1,)), pltpu.SemaphoreType.DMA((N_CHIPS - 1,)),
                        pltpu.SemaphoreType.DMA((N_CHIPS - 1,))],
        compiler_params=pltpu.CompilerParams(vmem_limit_bytes=VMEM_LIMIT),
    )(c_row, w_ada_shard, b_ada_row)


_HBM = pl.BlockSpec(memory_space=pltpu.HBM)
_SEM = pl.BlockSpec(memory_space=pltpu.SEMAPHORE)
_EFFECT = pltpu.SideEffectType.DATAFLOW_SIDE_EFFECTING
_CHIP_HOPS = (2, 4, 6)


def _in_hbm(a):
    return pltpu.with_memory_space_constraint(a, pltpu.HBM)


def _sems3():
    return pltpu.SemaphoreType.DMA((len(_CHIP_HOPS),))


def _ag_start(lands, after, name):
    n = len(lands)

    def body(*refs):
        zones = refs[:n]
        sends, recvs = refs[n + 1:2 * n + 1], refs[2 * n + 1:3 * n + 1]
        x, y, c = _position()
        chip = 2 * x + y
        for i in range(n):
            half = zones[i].shape[1] // 2
            mine = zones[i].at[chip, pl.ds(c * half, half)]
            for s, k in enumerate(_CHIP_HOPS):
                px, py, _ = _peer(x, y, c, k)
                _remote(mine, mine, sends[i].at[s], recvs[i].at[s], (px, py, c)).start()

    out = pl.pallas_call(
        body, name=name,
        in_specs=[_HBM] * n + [_ANY],
        out_specs=[_SEM] * (2 * n) + [_HBM] * n,
        out_shape=[_sems3()] * (2 * n) + [pltpu.HBM(z.shape, BF16) for z in lands],
        input_output_aliases={i: 2 * n + i for i in range(n)},
        compiler_params=pltpu.CompilerParams(has_side_effects=_EFFECT),
    )(*[_in_hbm(z) for z in lands], after)
    return [(out[2 * n + i], out[i], out[n + i]) for i in range(n)]


def _ag_pass(group, after, name):
    n = len(group)

    def body(*refs):
        zones = refs[:n]
        sends, recvs = refs[n:2 * n], refs[2 * n:3 * n]
        fsends, frecvs = refs[4 * n + 1:5 * n + 1], refs[5 * n + 1:6 * n + 1]
        x, y, c = _position()
        chip = 2 * x + y
        for i in range(n):
            half = zones[i].shape[1] // 2
            rows = pl.ds(c * half, half)
            for s, k in enumerate(_CHIP_HOPS):
                px, py, _ = _peer(x, y, c, k)
                landed = zones[i].at[2 * px + py, rows]
                _remote(landed, landed, sends[i].at[s], recvs[i].at[s], (px, py, c)).wait_recv()
                _remote(landed, landed, fsends[i].at[s], frecvs[i].at[s], (x, y, 1 - c)).start()
        for i in range(n):
            half = zones[i].shape[1] // 2
            mine = zones[i].at[chip, pl.ds(c * half, half)]
            for s, k in enumerate(_CHIP_HOPS):
                px, py, _ = _peer(x, y, c, k)
                _remote(mine, mine, sends[i].at[s], recvs[i].at[s], (px, py, c)).wait_send()

    out = pl.pallas_call(
        body, name=name,
        in_specs=[_HBM] * n + [_SEM] * (2 * n) + [_ANY],
        out_specs=[_HBM] * n + [_SEM] * (2 * n),
        out_shape=[pltpu.HBM(g[0].shape, BF16) for g in group] + [_sems3()] * (2 * n),
        input_output_aliases={i: i for i in range(n)},
        compiler_params=pltpu.CompilerParams(has_side_effects=_EFFECT),
    )(*[g[0] for g in group], *[g[1] for g in group], *[g[2] for g in group], after)
    return [(out[i], out[n + i], out[2 * n + i]) for i in range(n)]


def _ag_done(group, name):
    n = len(group)

    def body(*refs):
        lands = refs[:n]
        fsends, frecvs = refs[n:2 * n], refs[2 * n:3 * n]
        x, y, c = _position()
        for i in range(n):
            half = lands[i].shape[1] // 2
            for s, k in enumerate(_CHIP_HOPS):
                px, py, _ = _peer(x, y, c, k)
                sent = lands[i].at[2 * px + py, pl.ds(c * half, half)]
                got = lands[i].at[2 * px + py, pl.ds((1 - c) * half, half)]
                cp = _remote(sent, got, fsends[i].at[s], frecvs[i].at[s], (x, y, 1 - c))
                cp.wait_recv()
                cp.wait_send()

    out = pl.pallas_call(
        body, name=name,
        in_specs=[_HBM] * n + [_SEM] * (2 * n),
        out_specs=[_HBM] * n,
        out_shape=[pltpu.HBM(g[0].shape, BF16) for g in group],
        input_output_aliases={i: i for i in range(n)},
        compiler_params=pltpu.CompilerParams(has_side_effects=_EFFECT),
    )(*[g[0] for g in group], *[g[1] for g in group], *[g[2] for g in group])
    return list(out)


def _small_spread_start(slots, after):
    def body(z_ref, after_ref, sends, recvs, z_out):
        x, y, c = _position()
        mine = z_ref.at[2 * x + y]
        for s, k in enumerate(_CHIP_HOPS):
            px, py, _ = _peer(x, y, c, k)
            _remote(mine, mine, sends.at[s], recvs.at[s], (px, py, c)).start()

    sends, recvs, out = pl.pallas_call(
        body, name="small_spread_start",
        in_specs=[_HBM, _ANY], out_specs=[_SEM, _SEM, _HBM],
        out_shape=[_sems3(), _sems3(), pltpu.HBM(slots.shape, F32)],
        input_output_aliases={0: 2},
        compiler_params=pltpu.CompilerParams(has_side_effects=_EFFECT),
    )(_in_hbm(slots), after)
    return out, sends, recvs


def _small_spread_wait(slots, sends, recvs, afters):
    def body(z_ref, sends, recvs, *rest):
        x, y, c = _position()
        mine = z_ref.at[2 * x + y]
        for s, k in enumerate(_CHIP_HOPS):
            px, py, _ = _peer(x, y, c, k)
            cp = _remote(mine, z_ref.at[2 * px + py], sends.at[s], recvs.at[s], (px, py, c))
            cp.wait_recv()
            cp.wait_send()

    return pl.pallas_call(
        body, name="small_spread_wait",
        in_specs=[_HBM, _SEM, _SEM] + [_ANY] * len(afters), out_specs=_HBM, out_shape=pltpu.HBM(slots.shape, F32),
        input_output_aliases={0: 0},
        compiler_params=pltpu.CompilerParams(has_side_effects=_EFFECT),
    )(slots, sends, recvs, *afters)


def _sibling_sum(pgs, name, small=None):
    n = len(pgs)
    k = 0 if small is None else 1
    units = [(i, j) for i in range(n) for j in range(N_CHIPS)]

    def body(*refs):
        refs = list(refs)
        take = lambda count: [refs.pop(0) for _ in range(count)]
        ins, small_in = take(n), take(k)
        qbs, owns, slots_out = take(n), take(n), take(k)
        mine, other, stage, got = take(n), take(n), take(n), take(n)
        load_a, load_b, send, recv = take(4)
        x, y, c = _position()
        chip = 2 * x + y
        if k:
            sib_ref, pair_send, pair_recv = take(3)
            pair = _remote(small_in[0], sib_ref, pair_send, pair_recv, (x, y, 1 - c))
            pair.start()
        loads_a = [pltpu.make_async_copy(ins[i].at[j, 1 - c], other[i].at[j], load_a.at[u])
                   for u, (i, j) in enumerate(units)]
        loads_b = [pltpu.make_async_copy(ins[i].at[j, c], mine[i].at[j], load_b.at[u])
                   for u, (i, j) in enumerate(units)]
        for cp in loads_a + loads_b:
            cp.start()
        sent = []
        for u, (i, j) in enumerate(units):
            loads_a[u].wait()
            stage[i][j] = other[i][j].astype(BF16)
            cp = _remote(stage[i].at[j], got[i].at[j], send.at[u], recv.at[u], (x, y, 1 - c))
            cp.start()
            sent.append(cp)
        for u, (i, j) in enumerate(units):
            loads_b[u].wait()
            sent[u].wait_recv()
            q = mine[i][j] + got[i][j].astype(F32)
            mine[i][j] = q
            qbs[i][j] = q.astype(BF16)
        for i in range(n):
            owns[i][...] = mine[i][chip]
        if k:
            pair.wait()
            slots_out[0][chip] = small_in[0][...] + sib_ref[...]
        for cp in sent:
            cp.wait_send()

    wire = [(N_CHIPS,) + p.shape[2:] for p in pgs]
    extra_out, extra_scratch = [], []
    if k:
        extra_out = [jax.ShapeDtypeStruct((N_CHIPS,) + small.shape, F32)]
        extra_scratch = [pltpu.VMEM(small.shape, F32), pltpu.SemaphoreType.DMA, pltpu.SemaphoreType.DMA]
    out = pl.pallas_call(
        body, name=name, in_specs=[_ANY] * n + [_VMEM] * k, out_specs=[_VMEM] * (2 * n + k),
        out_shape=[jax.ShapeDtypeStruct(w, BF16) for w in wire] + [jax.ShapeDtypeStruct(w[1:], F32) for w in wire]
        + extra_out,
        scratch_shapes=[pltpu.VMEM(w, F32) for w in wire] * 2 + [pltpu.VMEM(w, BF16) for w in wire] * 2
        + [pltpu.SemaphoreType.DMA((len(units),))] * 4 + extra_scratch,
        compiler_params=pltpu.CompilerParams(vmem_limit_bytes=VMEM_LIMIT),
    )(*pgs, *([small] if k else []))
    return list(out[:n]), list(out[n:2 * n]), list(out[2 * n:])


def _rs_start(qbs, name):
    n = len(qbs)

    def body(*refs):
        outs, inboxes = refs[:n], refs[n:2 * n]
        sends, recvs = refs[2 * n:3 * n], refs[3 * n:4 * n]
        x, y, c = _position()
        chip = 2 * x + y
        for i in range(n):
            for s, k in enumerate(_CHIP_HOPS):
                px, py, _ = _peer(x, y, c, k)
                _remote(outs[i].at[2 * px + py], inboxes[i].at[chip], sends[i].at[s], recvs[i].at[s], (px, py, c)).start()

    inboxes = [_in_hbm(lax.empty(q.shape, BF16)) for q in qbs]
    out = pl.pallas_call(
        body, name=name,
        in_specs=[_HBM] * (2 * n),
        out_specs=[_SEM] * (2 * n) + [_HBM] * (2 * n),
        out_shape=[_sems3()] * (2 * n) + [pltpu.HBM(q.shape, BF16) for q in qbs] * 2,
        input_output_aliases={i: 2 * n + i for i in range(2 * n)},
        compiler_params=pltpu.CompilerParams(has_side_effects=_EFFECT),
    )(*[_in_hbm(q) for q in qbs], *inboxes)
    return [(out[2 * n + i], out[3 * n + i], out[i], out[n + i]) for i in range(n)]


def _rs_wait(group, after, name):
    n = len(group)

    def body(*refs):
        outs, inboxes = refs[:n], refs[n:2 * n]
        sends, recvs = refs[2 * n:3 * n], refs[3 * n:4 * n]
        x, y, c = _position()
        for i in range(n):
            for s, k in enumerate(_CHIP_HOPS):
                px, py, _ = _peer(x, y, c, k)
                slot = 2 * px + py
                cp = _remote(outs[i].at[slot], inboxes[i].at[slot], sends[i].at[s], recvs[i].at[s], (px, py, c))
                cp.wait_recv()
                cp.wait_send()

    out = pl.pallas_call(
        body, name=name,
        in_specs=[_HBM] * (2 * n) + [_SEM] * (2 * n) + [_ANY],
        out_specs=[_HBM] * n,
        out_shape=[pltpu.HBM(g[1].shape, BF16) for g in group],
        input_output_aliases={n + i: i for i in range(n)},
        compiler_params=pltpu.CompilerParams(has_side_effects=_EFFECT),
    )(*[g[0] for g in group], *[g[1] for g in group], *[g[2] for g in group], *[g[3] for g in group], after)
    return list(out)


def _final_share(inboxes, owns, name):
    n = len(inboxes)
    units = [(i, s) for i in range(n) for s in range(len(_CHIP_HOPS))]

    def body(*refs):
        ins, mine, outs, landed = refs[:n], refs[n:2 * n], refs[2 * n:3 * n], refs[3 * n:4 * n]
        load, send, recv = refs[4 * n:]
        x, y, c = _position()
        loads = []
        for u, (i, s) in enumerate(units):
            px, py, _ = _peer(x, y, c, _CHIP_HOPS[s])
            loads.append(pltpu.make_async_copy(ins[i].at[2 * px + py], landed[i].at[s], load.at[u]))
        for cp in loads:
            cp.start()
        copies = []
        for i in range(n):
            for s in range(len(_CHIP_HOPS)):
                loads[len(_CHIP_HOPS) * i + s].wait()
            total = (landed[i][0].astype(F32) + landed[i][1].astype(F32)) + landed[i][2].astype(F32)
            outs[i][c] = total + mine[i][...]
            cp = _remote(outs[i].at[c], outs[i].at[c], send.at[i], recv.at[i], (x, y, 1 - c))
            cp.start()
            copies.append(cp)
        for i in range(n):
            theirs = outs[i].at[1 - c]
            _remote(theirs, theirs, send.at[i], recv.at[i], (x, y, 1 - c)).wait_recv()
        for cp in copies:
            cp.wait_send()

    return pl.pallas_call(
        body, name=name, in_specs=[_ANY] * n + [_VMEM] * n, out_specs=[_VMEM] * n,
        out_shape=[jax.ShapeDtypeStruct((2,) + o.shape, F32) for o in owns],
        scratch_shapes=[pltpu.VMEM((len(_CHIP_HOPS),) + o.shape, BF16) for o in owns]
        + [pltpu.SemaphoreType.DMA((len(units),)), pltpu.SemaphoreType.DMA((n,)), pltpu.SemaphoreType.DMA((n,))],
        compiler_params=pltpu.CompilerParams(vmem_limit_bytes=VMEM_LIMIT),
    )(*inboxes, *owns)


_SMALL = (("b_ada", N_MOD * D_MODEL), ("norm1_pre", D_MODEL), ("norm1_post", D_MODEL), ("norm2_pre", D_MODEL),
          ("norm2_post", D_MODEL), ("w_spatial", N_HEADS * CHUNK * CHUNK), ("b_spatial", N_HEADS * CHUNK),
          ("ln_v_gain", D_A), ("ln_v_bias", D_A), ("w_pool", N_HEADS * GROUP_DIM * GROUP_DIM),
          ("b_pool", D_B), ("pool_scale", D_B))
_MOD_ROWS = N_MOD * D_MODEL // 128


def _packed_rows(size):
    return -(-(size // 128) // 8) * 8


def _pack(parts):
    out = []
    for name, size in _SMALL:
        a = parts[name].reshape(size // 128, 128)
        pad = _packed_rows(size) - a.shape[0]
        out.append(jnp.pad(a, ((0, pad), (0, 0))) if pad else a)
    return out


def _small_adamw(slots, ws, ms, vs):
    n = len(_SMALL)
    head = N_DEV * _MOD_ROWS

    def body(*refs):
        s_ref, w, m, v = refs[0], refs[1:1 + n], refs[1 + n:1 + 2 * n], refs[1 + 2 * n:1 + 3 * n]
        outs = refs[1 + 3 * n:1 + 7 * n]
        dmod_ref, loss_ref, t_ref = refs[1 + 7 * n:]
        t_ref[...] = ((s_ref[0] + s_ref[1]) + s_ref[2]) + s_ref[3]
        dmod_ref[...] = t_ref[0:head, :]
        loss_ref[...] = t_ref[t_ref.shape[0] - 8:, :]
        row = head
        for i, (_, size) in enumerate(_SMALL):
            if i == 0:
                g = t_ref[0:_MOD_ROWS, :]
                for b in range(1, N_DEV):
                    g = g + t_ref[b * _MOD_ROWS:(b + 1) * _MOD_ROWS, :]
            else:
                g = t_ref[row:row + size // 128, :]
                row += _packed_rows(size)
            d, nm, nv = _adamw_math(w[i][...], g, m[i][...], v[i][...])
            for ref, val in zip(outs[4 * i:4 * i + 4], (g, d, nm, nv)):
                ref[...] = val

    each = [jax.ShapeDtypeStruct((size // 128, 128), F32) for _, size in _SMALL for _ in range(4)]
    out = pl.pallas_call(
        body, name="small_adamw",
        out_shape=each + [jax.ShapeDtypeStruct((head, 128), F32), jax.ShapeDtypeStruct((8, 128), F32)],
        scratch_shapes=[pltpu.VMEM(slots.shape[1:], F32)],
        compiler_params=pltpu.CompilerParams(vmem_limit_bytes=VMEM_LIMIT),
    )(slots, *ws, *ms, *vs)
    return [out[4 * i:4 * i + 4] for i in range(n)], out[4 * n], out[4 * n + 1]


def kernel(x, c, w_ada, b_ada, norm1_pre, norm1_post, w_in, w_spatial, b_spatial, ln_v_gain, ln_v_bias, w_pool, b_pool, pool_scale, w_out, norm2_pre, norm2_post, w_fc1, w_fc2, loss_target, m_w_ada, m_b_ada, m_norm1_pre, m_norm1_post, m_w_in, m_w_spatial, m_b_spatial, m_ln_v_gain, m_ln_v_bias, m_w_pool, m_b_pool, m_pool_scale, m_w_out, m_norm2_pre, m_norm2_post, m_w_fc1, m_w_fc2, v_w_ada, v_b_ada, v_norm1_pre, v_norm1_post, v_w_in, v_w_spatial, v_b_spatial, v_ln_v_gain, v_ln_v_bias, v_w_pool, v_b_pool, v_pool_scale, v_w_out, v_norm2_pre, v_norm2_post, v_w_fc1, v_w_fc2):
    weights = dict(w_ada=w_ada, b_ada=b_ada, norm1_pre=norm1_pre, norm1_post=norm1_post, w_in=w_in,
                   w_spatial=w_spatial, b_spatial=b_spatial, ln_v_gain=ln_v_gain, ln_v_bias=ln_v_bias, w_pool=w_pool,
                   b_pool=b_pool, pool_scale=pool_scale, w_out=w_out, norm2_pre=norm2_pre, norm2_post=norm2_post,
                   w_fc1=w_fc1, w_fc2=w_fc2)
    m_old = dict(w_ada=m_w_ada, b_ada=m_b_ada, norm1_pre=m_norm1_pre, norm1_post=m_norm1_post, w_in=m_w_in,
                 w_spatial=m_w_spatial, b_spatial=m_b_spatial, ln_v_gain=m_ln_v_gain, ln_v_bias=m_ln_v_bias,
                 w_pool=m_w_pool, b_pool=m_b_pool, pool_scale=m_pool_scale, w_out=m_w_out, norm2_pre=m_norm2_pre,
                 norm2_post=m_norm2_post, w_fc1=m_w_fc1, w_fc2=m_w_fc2)
    v_old = dict(w_ada=v_w_ada, b_ada=v_b_ada, norm1_pre=v_norm1_pre, norm1_post=v_norm1_post, w_in=v_w_in,
                 w_spatial=v_w_spatial, b_spatial=v_b_spatial, ln_v_gain=v_ln_v_gain, ln_v_bias=v_ln_v_bias,
                 w_pool=v_w_pool, b_pool=v_b_pool, pool_scale=v_pool_scale, w_out=v_w_out, norm2_pre=v_norm2_pre,
                 norm2_post=v_norm2_post, w_fc1=v_w_fc1, w_fc2=v_w_fc2)
    order = ("w_ada", "b_ada", "norm1_pre", "norm1_post", "w_in", "w_spatial", "b_spatial", "ln_v_gain", "ln_v_bias",
             "w_pool", "b_pool", "pool_scale", "w_out", "norm2_pre", "norm2_post", "w_fc1", "w_fc2")
    big = ("w_in", "w_out", "w_fc1", "w_fc2")
    mx, my, mc = _position()
    me = 4 * mx + 2 * my + mc
    chip = 2 * mx + my
    row = lambda a: a.reshape(1, -1)

    pos = jnp.stack([mc, chip]).astype(jnp.int32)
    xs, target = x[0], loss_target[0]
    n1pre, n1post, n2pre, n2post = row(norm1_pre), row(norm1_post), row(norm2_pre), row(norm2_post)
    mixer = (w_spatial, jnp.repeat(b_spatial.T, HEAD_DIM, axis=1), row(ln_v_gain), row(ln_v_bias), w_pool,
             row(b_pool), row(pool_scale))
    ts_big, ts_mid = 512, 256

    mod4, sc_all = _mod_exchange(c, w_ada, row(b_ada))
    mod6 = mod4.reshape(N_MOD, D_MODEL)
    ag = _ag_start([_cast_to_slot(weights[n], pos, mod4, "cast_" + n, 256) for n in big[:2]], mod4, "ag_start_mix")
    ag += _ag_start([_cast_to_slot(weights[n], pos, ag[0][0], "cast_" + n, 256) for n in big[2:]], ag[0][0],
                    "ag_start_mlp")

    win_g, wout_g = _ag_done(_ag_pass([ag[0], ag[1]], ag[2][0], "ag_pass_mix"), "ag_done_mix")
    h1, z, ycat, mix, x1, h2 = _fwd_mix(xs, mod6, n1pre, n1post, n2pre, win_g, wout_g, *mixer, ts_mid)
    (fc1_g,) = _ag_done(_ag_pass([ag[2]], h2, "ag_pass_fc1"), "ag_done_fc1")
    q = _fwd_fc1(h2, fc1_g, ts_big)
    (fc2_g,) = _ag_done(_ag_pass([ag[3]], q, "ag_pass_fc2"), "ag_done_fc2")
    dy, df, loss, dgate2, dg2post = _fwd_fc2_loss(q, x1, target, mod6, n2post, fc2_g, ts_big)

    def reduce_start(partials, tag, small=None):
        wire, owns, slots = _sibling_sum(partials, "sibling_sum_" + tag, small)
        return _rs_start(wire, "rs_start_" + tag), owns, slots

    def reduce_finish(state, owns, names, tag, dep):
        inboxes = _rs_wait(state, dep, "rs_wait_" + tag)
        shards = _final_share(inboxes, owns, "final_share_" + tag)
        for n, g in zip(names, shards):
            grads[n], deltas[n], new_m[n], new_v[n] = _adamw(g.reshape(weights[n].shape), weights[n], m_old[n],
                                                             v_old[n], "adamw_" + n, 256)

    grads, deltas, new_m, new_v = {}, {}, {}, {}
    dp, g_fc2 = _bwd_fc2(df, q, fc2_g, ts_big)
    state_fc2, owns_fc2, _ = reduce_start([g_fc2], "fc2")
    dx1, dyc, dshift2, da2, dgate1, dg1post, g_fc1, g_out = _bwd_fc1_out(
        dp, dy, x1, mix, h2, ycat, mod6, n2pre, n1post, fc1_g, wout_g, state_fc2[0][0], ts_mid)
    state_mid, owns_mid, _ = reduce_start([g_fc1, g_out], "mid")
    dz, dws, dbsp, dgain, dbias, dwp, dbp, dps = _mixer_bwd(z, dyc, *mixer, state_mid[0][0], ts_mid)
    grad_x, dshift1, da1, g_in = _bwd_in(dz, dx1, xs, h1, mod6, n1pre, win_g, state_mid[0][0], ts_mid)
    dmod6, dnorms = _mod_grads(da1, dshift1, dgate1, dg1post, da2, dshift2, dgate2, dg2post, mod6, n1pre, n2pre)

    parts = dict(b_ada=dmod6, norm1_pre=dnorms[0], norm1_post=dnorms[1], norm2_pre=dnorms[2], norm2_post=dnorms[3],
                 w_spatial=dws, b_spatial=dbsp, ln_v_gain=dgain, ln_v_bias=dbias, w_pool=dwp, b_pool=dbp,
                 pool_scale=dps)
    pieces = _pack(parts)
    slots = lax.dynamic_update_slice(jnp.zeros((N_DEV * _MOD_ROWS, 128), F32), pieces[0], (me * _MOD_ROWS, 0))
    loss_tile = jnp.pad(loss, ((0, 7), (0, 127)))
    state_in, owns_in, pair_sum = reduce_start(
        [g_in], "in", jnp.concatenate([slots] + pieces[1:] + [loss_tile], axis=0))
    spread = _small_spread_start(pair_sum[0], state_in[0][0])
    reduce_finish(state_fc2 + state_mid, owns_fc2 + owns_mid, ("w_fc2", "w_fc1", "w_out"), "mlp", spread[0])
    flat = lambda d: [d[n].reshape(size // 128, 128) for n, size in _SMALL]
    small_out, dmod_all, loss_tile = _small_adamw(
        _small_spread_wait(*spread, [deltas[n] for n in ("w_fc2", "w_fc1", "w_out")]), flat(weights), flat(m_old),
        flat(v_old))
    loss = loss_tile[0, 0]
    for (n, _), (g, d, nm, nv) in zip(_SMALL, small_out):
        shape = weights[n].shape
        grads[n], deltas[n], new_m[n], new_v[n] = g.reshape(shape), d.reshape(shape), nm.reshape(shape), nv.reshape(shape)

    dmod_all = dmod_all.reshape(N_DEV, N_MOD * D_MODEL)
    cs = w_ada.shape[1]
    dmod_shard = lax.dynamic_slice(dmod_all, (0, chip * cs), (N_DEV, cs))
    sc_t = sc_all.reshape(N_DEV, D_MODEL).T
    grads["w_ada"], deltas["w_ada"], new_m["w_ada"], new_v["w_ada"] = _ada_grad_adamw(
        sc_t, dmod_shard, w_ada, m_w_ada, v_w_ada, 256)

    reduce_finish(state_in, owns_in, ("w_in",), "in", deltas["w_ada"])

    return (loss, grad_x[None], *[grads[n] for n in order], *[deltas[n] for n in order],
            *[new_m[n] for n in order], *[new_v[n] for n in order])
```

```python
import functools

import jax
import jax.numpy as jnp
from jax import lax
from jax.experimental import pallas as pl
from jax.experimental.pallas import tpu as pltpu

F32 = jnp.float32
BF16 = jnp.bfloat16
MESH = pl.DeviceIdType.MESH

D_MODEL = 1024
D_A = 512
D_B = 512
D_Z = 2 * D_A + D_B
N_HEADS = 4
HEAD_DIM = 128
CHUNK = 128
POOL_WINDOWS = (2, 4, 8, 16)
GROUP_DIM = 128
D_FF = 4096
N_MOD = 6
EPS = 1e-6
HALO = 16
N_CHIPS = 4
N_DEV = 8

ADAM_LR = 0.001
ADAM_B1 = 0.9
ADAM_B2 = 0.999
ADAM_EPS = 1e-08
ADAM_WD = 0.01
ADAM_STEP = 10

VMEM_LIMIT = 56 * 1024 * 1024

_VMEM = pl.BlockSpec(memory_space=pltpu.VMEM)
_ANY = pl.BlockSpec(memory_space=pl.ANY)


def _params(n_grid_axes=1):
    return pltpu.CompilerParams(dimension_semantics=("arbitrary",) * n_grid_axes, vmem_limit_bytes=VMEM_LIMIT)


def _rows(ts, width):
    return pl.BlockSpec((ts, width), lambda i: (i, 0))


def _const(shape):
    return pl.BlockSpec(shape, lambda i: (0,) * len(shape))


def _dot(a, b):
    return jnp.dot(a, b, preferred_element_type=F32)


def _dot_nt(a, b):
    return lax.dot_general(a, b, (((1,), (1,)), ((), ())), preferred_element_type=F32)


def _dot_tn(a, b):
    return lax.dot_general(a, b, (((0,), (0,)), ((), ())), preferred_element_type=F32)


def _rowmean(v):
    return jnp.mean(v, axis=-1, keepdims=True)


def _colsum(v):
    return jnp.sum(v, axis=0, keepdims=True)


def _gelu_parts(z):
    k0 = 0.7978845608028654
    k1 = 0.044715
    z2 = z * z
    t = jnp.tanh(k0 * (z + k1 * z * z2))
    g = 0.5 * z * (1.0 + t)
    dg = 0.5 * (1.0 + t) + 0.5 * z * (1.0 - t * t) * (k0 * (1.0 + 3.0 * k1 * z2))
    return g, dg


def _tril_weights(ws_ref):
    r = lax.broadcasted_iota(jnp.int32, (CHUNK, CHUNK), 0)
    s = lax.broadcasted_iota(jnp.int32, (CHUNK, CHUNK), 1)
    mask = (s <= r).astype(F32)
    return [(ws_ref[h] * mask).astype(BF16) for h in range(N_HEADS)]


def _window_counts(first_row, n_rows):
    pos = (first_row + lax.broadcasted_iota(jnp.int32, (n_rows, 1), 0)).astype(F32)
    return pos, [1.0 / jnp.minimum(pos + 1.0, float(w)) for w in POOL_WINDOWS]


def _causal_window_sums(ext):
    out = []
    e = ext
    shift = 1
    for g in range(len(POOL_WINDOWS)):
        e = e + pltpu.roll(e, shift, 0)
        shift *= 2
        out.append(e[:, g * GROUP_DIM:(g + 1) * GROUP_DIM])
    return out


def _anticausal_window_sums(ext):
    n = ext.shape[0]
    out = []
    e = ext
    shift = 1
    for g in range(len(POOL_WINDOWS)):
        e = e + pltpu.roll(e, n - shift, 0)
        shift *= 2
        out.append(e[:, g * GROUP_DIM:(g + 1) * GROUP_DIM])
    return out


def _fwd_mix(x, mod6, n1pre, n1post, n2pre, win_g, wout_g, w_spatial, bsp_full, gain, bias, w_pool, b_pool, pool_scale, ts):
    s_len = x.shape[0]
    rs = D_MODEL // N_CHIPS

    def body(x_ref, mod_ref, g1pre_ref, g1post_ref, g2pre_ref, win_ref, wout_ref, ws_ref, bsp_ref, gain_ref,
             bias_ref, wp_ref, bp_ref, ps_ref, h1_ref, z_ref, y_ref, mix_ref, x1_ref, h2_ref, mixed_ref, prev_ref,
             wfull_ref):
        i = pl.program_id(0)
        _zero_on_first_step(prev_ref)
        _join_w_in_on_first_step(win_ref, wfull_ref)
        xv = x_ref[...]
        r = lax.rsqrt(_rowmean(xv * xv) + EPS)
        hb = (((xv * r) * g1pre_ref[...]) * (1.0 + mod_ref[1:2, :]) + mod_ref[0:1, :]).astype(BF16)
        h1_ref[...] = hb
        z_ref[...] = _dot(hb, wfull_ref[...])

        wc = _tril_weights(ws_ref)
        u, _, _, _, _ = _mixer_forward_tile(z_ref[:, :2 * D_A], wc, bsp_ref, gain_ref[...], bias_ref[...], mixed_ref)
        y_ref[:, :D_A] = (u * mixed_ref[...]).astype(BF16)
        zb = z_ref[:, 2 * D_A:]
        sums = _causal_window_sums(jnp.concatenate([prev_ref[...], zb], axis=0))
        prev_ref[...] = zb[ts - HALO:, :]
        _, inv_counts = _window_counts(i * ts, ts)
        for g in range(len(POOL_WINDOWS)):
            lanes = slice(g * GROUP_DIM, (g + 1) * GROUP_DIM)
            diff = sums[g][HALO:, :] * inv_counts[g] - zb[:, lanes]
            lin = _dot(diff.astype(BF16), wp_ref[g].astype(BF16)) + bp_ref[:, lanes]
            y_ref[:, D_A + g * GROUP_DIM:D_A + (g + 1) * GROUP_DIM] = (lin * ps_ref[:, lanes]).astype(BF16)

        mix = None
        for j in range(N_CHIPS):
            part = _dot(y_ref[:, j * rs:(j + 1) * rs], wout_ref[j])
            mix = part if mix is None else mix + part
        mix_ref[...] = mix
        r2 = lax.rsqrt(_rowmean(mix * mix) + EPS)
        x1 = xv + mod_ref[2:3, :] * ((mix * r2) * g1post_ref[...])
        x1_ref[...] = x1
        r3 = lax.rsqrt(_rowmean(x1 * x1) + EPS)
        h2_ref[...] = (((x1 * r3) * g2pre_ref[...]) * (1.0 + mod_ref[4:5, :]) + mod_ref[3:4, :]).astype(BF16)

    vec = _const((1, D_MODEL))
    f32_rows = jax.ShapeDtypeStruct((s_len, D_MODEL), F32)
    bf16_rows = jax.ShapeDtypeStruct((s_len, D_MODEL), BF16)
    return pl.pallas_call(
        body, name="fwd_mix", grid=(s_len // ts,),
        in_specs=[_rows(ts, D_MODEL), _const((N_MOD, D_MODEL)), vec, vec, vec, _VMEM, _VMEM,
                  _const((N_HEADS, CHUNK, CHUNK)), _const((CHUNK, D_A)), _const((1, D_A)), _const((1, D_A)),
                  _const((N_HEADS, GROUP_DIM, GROUP_DIM)), _const((1, D_B)), _const((1, D_B))],
        out_specs=[_rows(ts, D_MODEL), _rows(ts, D_Z), _rows(ts, D_MODEL), _rows(ts, D_MODEL), _rows(ts, D_MODEL),
                   _rows(ts, D_MODEL)],
        out_shape=[bf16_rows, jax.ShapeDtypeStruct((s_len, D_Z), F32), bf16_rows, f32_rows, f32_rows, bf16_rows],
        scratch_shapes=[pltpu.VMEM((ts, D_A), F32), pltpu.VMEM((HALO, D_B), F32), pltpu.VMEM((D_MODEL, D_Z), BF16)],
        compiler_params=_params(),
    )(x, mod6, n1pre, n1post, n2pre, win_g, wout_g, w_spatial, bsp_full, gain, bias, w_pool, b_pool, pool_scale)


def _mixer_forward_tile(za, wc, bsp_ref, gain, bias, mixed_ref):
    ga, dga = _gelu_parts(za)
    u = ga[:, :D_A]
    v = ga[:, D_A:]
    mu = _rowmean(v)
    vc = v - mu
    rstd = lax.rsqrt(_rowmean(vc * vc) + EPS)
    vhat = vc * rstd
    vn = (vhat * gain + bias).astype(BF16)
    ts = za.shape[0]
    for k in range(ts // CHUNK):
        for h in range(N_HEADS):
            blk = vn[k * CHUNK:(k + 1) * CHUNK, h * HEAD_DIM:(h + 1) * HEAD_DIM]
            mixed_ref[k * CHUNK:(k + 1) * CHUNK, h * HEAD_DIM:(h + 1) * HEAD_DIM] = (
                _dot(wc[h], blk) + bsp_ref[:, h * HEAD_DIM:(h + 1) * HEAD_DIM])
    return u, vhat, rstd, vn, dga


def _fwd_fc1(h2, fc1_g, ts):
    s_len = h2.shape[0]
    cs = D_FF // N_CHIPS

    def body(h_ref, w_ref, q_ref):
        hb = h_ref[...]
        for j in range(N_CHIPS):
            p = jnp.maximum(_dot(hb, w_ref[j]), 0.0)
            q_ref[:, j * cs:(j + 1) * cs] = (p * p).astype(BF16)

    return pl.pallas_call(
        body, name="fwd_fc1", grid=(s_len // ts,),
        in_specs=[_rows(ts, D_MODEL), _VMEM],
        out_specs=_rows(ts, D_FF),
        out_shape=jax.ShapeDtypeStruct((s_len, D_FF), BF16),
        compiler_params=_params(),
    )(h2, fc1_g)


def _fwd_fc2_loss(q, x1, target, mod6, n2post, fc2_g, ts):
    s_len = q.shape[0]
    rs = D_FF // N_CHIPS

    def body(q_ref, x1_ref, t_ref, mod_ref, g_ref, w_ref, dy_ref, df_ref, loss_ref, dgate_ref, dg_ref):
        _zero_on_first_step(loss_ref, dgate_ref, dg_ref)
        gate = mod_ref[5:6, :]
        gn = g_ref[...]
        f = _dot(q_ref[:, 0:rs], w_ref[0])
        for j in range(1, N_CHIPS):
            f = f + _dot(q_ref[:, j * rs:(j + 1) * rs], w_ref[j])
        r4 = lax.rsqrt(_rowmean(f * f) + EPS)
        fh = f * r4
        err = (x1_ref[...] + gate * (fh * gn)) - t_ref[...]
        loss_ref[...] += 0.5 * jnp.sum(_rowmean(err * err), axis=0, keepdims=True)
        dy = err * (1.0 / D_MODEL)
        dy_ref[...] = dy
        dgate_ref[...] += _colsum(dy * (fh * gn))
        dg_ref[...] += _colsum((dy * gate) * fh)
        gh = (dy * gate) * gn
        df_ref[...] = (r4 * (gh - fh * _rowmean(gh * fh))).astype(BF16)

    return pl.pallas_call(
        body, name="fwd_fc2_loss", grid=(s_len // ts,),
        in_specs=[_rows(ts, D_FF), _rows(ts, D_MODEL), _rows(ts, D_MODEL), _const((N_MOD, D_MODEL)),
                  _const((1, D_MODEL)), _VMEM],
        out_specs=[_rows(ts, D_MODEL), _rows(ts, D_MODEL), _const((1, 1)), _const((1, D_MODEL)), _const((1, D_MODEL))],
        out_shape=[jax.ShapeDtypeStruct((s_len, D_MODEL), F32), jax.ShapeDtypeStruct((s_len, D_MODEL), BF16),
                   jax.ShapeDtypeStruct((1, 1), F32), jax.ShapeDtypeStruct((1, D_MODEL), F32),
                   jax.ShapeDtypeStruct((1, D_MODEL), F32)],
        compiler_params=_params(),
    )(q, x1, target, mod6, n2post, fc2_g)


def _join_w_in_on_first_step(win_ref, full_ref):
    cs = D_Z // N_CHIPS

    @pl.when(pl.program_id(0) == 0)
    def _():
        for j in range(N_CHIPS):
            full_ref[:, j * cs:(j + 1) * cs] = win_ref[j]


def _zero_on_first_step(*refs):
    @pl.when(pl.program_id(0) == 0)
    def _():
        for ref in refs:
            ref[...] = jnp.zeros_like(ref)


def _on_last_step(fn):
    pl.when(pl.program_id(0) == pl.num_programs(0) - 1)(fn)


def _store_shard_on_last_step(acc_ref, hbm_ref, sem, j):
    _on_last_step(lambda: pltpu.make_async_copy(acc_ref.at[j], hbm_ref.at[j], sem.at[j]).start())


def _wait_stores_on_last_step(*stores):
    def wait_all():
        for acc_ref, hbm_ref, sem in stores:
            for j in range(N_CHIPS):
                pltpu.make_async_copy(acc_ref.at[j], hbm_ref.at[j], sem.at[j]).wait()

    _on_last_step(wait_all)


def _bwd_fc2(df, q, fc2_g, ts):
    s_len = df.shape[0]
    cs = D_FF // N_CHIPS

    def body(df_ref, q_ref, w_ref, dp_ref, dw_hbm, dw_ref, dw_sem):
        _zero_on_first_step(dw_ref)
        dfb = df_ref[...]
        for j in range(N_CHIPS):
            qb = q_ref[:, j * cs:(j + 1) * cs]
            dw_ref[j] += _dot_tn(qb, dfb).reshape(2, cs // 2, D_MODEL)
            _store_shard_on_last_step(dw_ref, dw_hbm, dw_sem, j)
            dq = _dot_nt(dfb, w_ref[j])
            dp_ref[:, j * cs:(j + 1) * cs] = (dq * (2.0 * jnp.sqrt(qb.astype(F32)))).astype(BF16)
        _wait_stores_on_last_step((dw_ref, dw_hbm, dw_sem))

    dw_shape = (N_CHIPS, 2, cs // 2, D_MODEL)
    return pl.pallas_call(
        body, name="bwd_fc2", grid=(s_len // ts,),
        in_specs=[_rows(ts, D_MODEL), _rows(ts, D_FF), _VMEM],
        out_specs=[_rows(ts, D_FF), _ANY],
        out_shape=[jax.ShapeDtypeStruct((s_len, D_FF), BF16), jax.ShapeDtypeStruct(dw_shape, F32)],
        scratch_shapes=[pltpu.VMEM(dw_shape, F32), pltpu.SemaphoreType.DMA((N_CHIPS,))],
        compiler_params=_params(),
    )(df, q, fc2_g)


def _bwd_fc1_out(dp, dy, x1, mix, h2, ycat, mod6, n2pre, n1post, fc1_g, wout_g, dep, ts):
    s_len = dp.shape[0]
    cs = D_FF // N_CHIPS
    rs = D_MODEL // N_CHIPS

    def body(dp_ref, dy_ref, x1_ref, mix_ref, h2_ref, yc_ref, mod_ref, g2_ref, g1_ref, w1_ref, wo_ref, dep_ref,
             dx1_ref, dyc_ref, dshift2_ref, da2_ref, dgate1_ref, dg1_ref, dw1_hbm, dwo_hbm, dw1_ref, dwo_ref, dw1_sem,
             dwo_sem):
        _zero_on_first_step(dshift2_ref, da2_ref, dgate1_ref, dg1_ref, dw1_ref, dwo_ref)
        h2b = h2_ref[...]
        dh2 = None
        for j in range(N_CHIPS):
            dpb = dp_ref[:, j * cs:(j + 1) * cs]
            dw1_ref[j] += _dot_tn(h2b, dpb).reshape(2, D_MODEL // 2, cs)
            _store_shard_on_last_step(dw1_ref, dw1_hbm, dw1_sem, j)
            part = _dot_nt(dpb, w1_ref[j])
            dh2 = part if dh2 is None else dh2 + part
        x1 = x1_ref[...]
        r3 = lax.rsqrt(_rowmean(x1 * x1) + EPS)
        xh = x1 * r3
        a2 = g2_ref[...] * (1.0 + mod_ref[4:5, :])
        dshift2_ref[...] += _colsum(dh2)
        da2_ref[...] += _colsum(dh2 * xh)
        dxh = dh2 * a2
        dx1 = dy_ref[...] + r3 * (dxh - xh * _rowmean(dxh * xh))
        dx1_ref[...] = dx1

        mix = mix_ref[...]
        r2 = lax.rsqrt(_rowmean(mix * mix) + EPS)
        mh = mix * r2
        gate = mod_ref[2:3, :]
        gn = g1_ref[...]
        dgate1_ref[...] += _colsum(dx1 * (mh * gn))
        dg1_ref[...] += _colsum((dx1 * gate) * mh)
        gh = (dx1 * gate) * gn
        dmix = (r2 * (gh - mh * _rowmean(gh * mh))).astype(BF16)
        dwo_ref[...] += _dot_tn(yc_ref[...], dmix).reshape(N_CHIPS, 2, rs // 2, D_MODEL)
        for j in range(N_CHIPS):
            _store_shard_on_last_step(dwo_ref, dwo_hbm, dwo_sem, j)
            dyc_ref[:, j * rs:(j + 1) * rs] = _dot_nt(dmix, wo_ref[j])
        _wait_stores_on_last_step((dw1_ref, dw1_hbm, dw1_sem), (dwo_ref, dwo_hbm, dwo_sem))

    vec = jax.ShapeDtypeStruct((1, D_MODEL), F32)
    dw1_shape = (N_CHIPS, 2, D_MODEL // 2, cs)
    dwo_shape = (N_CHIPS, 2, rs // 2, D_MODEL)
    return pl.pallas_call(
        body, name="bwd_fc1_out", grid=(s_len // ts,),
        in_specs=[_rows(ts, D_FF), _rows(ts, D_MODEL), _rows(ts, D_MODEL), _rows(ts, D_MODEL), _rows(ts, D_MODEL),
                  _rows(ts, D_MODEL), _const((N_MOD, D_MODEL)), _const((1, D_MODEL)), _const((1, D_MODEL)), _VMEM,
                  _VMEM, _ANY],
        out_specs=[_rows(ts, D_MODEL), _rows(ts, D_MODEL)] + [_const((1, D_MODEL))] * 4 + [_ANY, _ANY],
        out_shape=[jax.ShapeDtypeStruct((s_len, D_MODEL), F32), jax.ShapeDtypeStruct((s_len, D_MODEL), F32),
                   vec, vec, vec, vec, jax.ShapeDtypeStruct(dw1_shape, F32), jax.ShapeDtypeStruct(dwo_shape, F32)],
        scratch_shapes=[pltpu.VMEM(dw1_shape, F32), pltpu.VMEM(dwo_shape, F32), pltpu.SemaphoreType.DMA((N_CHIPS,)),
                        pltpu.SemaphoreType.DMA((N_CHIPS,))],
        compiler_params=_params(),
    )(dp, dy, x1, mix, h2, ycat, mod6, n2pre, n1post, fc1_g, wout_g, dep)


def _mixer_bwd(z, dyc, w_spatial, bsp_full, gain, bias, w_pool, b_pool, pool_scale, dep, ts):
    s_len = z.shape[0]
    nb = ts // HALO
    last = s_len // HALO - 1
    te = ts + HALO

    def body(z_ref, zprev_ref, znext_ref, dyc_ref, dynext_ref, ws_ref, bsp_ref, gain_ref, bias_ref, wp_ref, bp_ref,
             ps_ref, dep_ref, dz_ref, dws_ref, dbsp_ref, dgain_ref, dbias_ref, dwp_ref, dbp_ref, dps_ref, mixed_ref,
             dvn_ref):
        i = pl.program_id(0)

        @pl.when(i == 0)
        def _():
            for ref in (dws_ref, dbsp_ref, dgain_ref, dbias_ref, dwp_ref, dbp_ref, dps_ref):
                ref[...] = jnp.zeros_like(ref)

        wc = _tril_weights(ws_ref)
        gain = gain_ref[...]
        u, vhat, rstd, vn, dga = _mixer_forward_tile(z_ref[:, :2 * D_A], wc, bsp_ref, gain, bias_ref[...], mixed_ref)
        dya = dyc_ref[:, :D_A]
        du = dya * mixed_ref[...]
        dmixed = dya * u
        dmb = dmixed.astype(BF16)
        dm_sum = dmixed[0:CHUNK, :]
        for k in range(1, ts // CHUNK):
            dm_sum = dm_sum + dmixed[k * CHUNK:(k + 1) * CHUNK, :]
        r_idx = lax.broadcasted_iota(jnp.int32, (CHUNK, CHUNK), 0)
        s_idx = lax.broadcasted_iota(jnp.int32, (CHUNK, CHUNK), 1)
        causal = (s_idx <= r_idx).astype(F32)
        for h in range(N_HEADS):
            lanes = slice(h * HEAD_DIM, (h + 1) * HEAD_DIM)
            dbsp_ref[h] += jnp.sum(dm_sum[:, lanes], axis=1, keepdims=True)
            acc = None
            for k in range(ts // CHUNK):
                rows = slice(k * CHUNK, (k + 1) * CHUNK)
                t = _dot_nt(dmb[rows, lanes], vn[rows, lanes])
                acc = t if acc is None else acc + t
                dvn_ref[rows, lanes] = _dot_tn(wc[h], dmb[rows, lanes])
            dws_ref[h] += acc * causal
        dvn = dvn_ref[...]
        dgain_ref[...] += _colsum(dvn * vhat)
        dbias_ref[...] += _colsum(dvn)
        dvh = dvn * gain
        dv = rstd * (dvh - _rowmean(dvh) - vhat * _rowmean(dvh * vhat))
        dz_ref[:, :D_A] = (du * dga[:, :D_A]).astype(BF16)
        dz_ref[:, D_A:2 * D_A] = (dv * dga[:, D_A:]).astype(BF16)

        zb = z_ref[:, 2 * D_A:]
        prev = jnp.where(i == 0, 0.0, zprev_ref[...])
        zb_ext = jnp.concatenate([zb, znext_ref[...]], axis=0)
        sums = _causal_window_sums(jnp.concatenate([prev, zb_ext], axis=0))
        pos, inv_counts = _window_counts(i * ts, te)
        dyb_ext = jnp.concatenate([dyc_ref[:, D_A:], dynext_ref[...]], axis=0)
        dlin_ext = dyb_ext * ps_ref[...]
        dbp_ref[...] += _colsum(dlin_ext[:ts, :])
        scaled = []
        ddiffs = []
        lins = []
        for g in range(len(POOL_WINDOWS)):
            lanes = slice(g * GROUP_DIM, (g + 1) * GROUP_DIM)
            diff = (sums[g][HALO:, :] * inv_counts[g] - zb_ext[:, lanes]).astype(BF16)
            wpb = wp_ref[g].astype(BF16)
            dlb = dlin_ext[:, lanes].astype(BF16)
            lins.append(_dot(diff[:ts, :], wpb) + bp_ref[:, lanes])
            dwp_ref[g] += _dot_tn(diff[:ts, :], dlb[:ts, :])
            dd = _dot_nt(dlb, wpb)
            ddiffs.append(dd)
            scaled.append(jnp.where(pos < float(s_len), dd * inv_counts[g], 0.0))
        dps_ref[...] += _colsum(dyb_ext[:ts, :] * jnp.concatenate(lins, axis=1))
        back = _anticausal_window_sums(jnp.concatenate(scaled, axis=1))
        for g in range(len(POOL_WINDOWS)):
            dz_ref[:, 2 * D_A + g * GROUP_DIM:2 * D_A + (g + 1) * GROUP_DIM] = (
                back[g][:ts, :] - ddiffs[g][:ts, :]).astype(BF16)

    sq = jax.ShapeDtypeStruct((N_HEADS, CHUNK, CHUNK), F32)
    vec = jax.ShapeDtypeStruct((1, D_A), F32)
    return pl.pallas_call(
        body, name="mixer_bwd", grid=(s_len // ts,),
        in_specs=[_rows(ts, D_Z),
                  pl.BlockSpec((HALO, D_B), lambda i: (jnp.maximum(i * nb - 1, 0), 2)),
                  pl.BlockSpec((HALO, D_B), lambda i: (jnp.minimum((i + 1) * nb, last), 2)),
                  _rows(ts, D_MODEL),
                  pl.BlockSpec((HALO, D_B), lambda i: (jnp.minimum((i + 1) * nb, last), 1)),
                  _const((N_HEADS, CHUNK, CHUNK)), _const((CHUNK, D_A)), _const((1, D_A)), _const((1, D_A)),
                  _const((N_HEADS, GROUP_DIM, GROUP_DIM)), _const((1, D_B)), _const((1, D_B)), _ANY],
        out_specs=[_rows(ts, D_Z), _const((N_HEADS, CHUNK, CHUNK)), _const((N_HEADS, CHUNK, 1)), _const((1, D_A)),
                   _const((1, D_A)), _const((N_HEADS, GROUP_DIM, GROUP_DIM)), _const((1, D_B)), _const((1, D_B))],
        out_shape=[jax.ShapeDtypeStruct((s_len, D_Z), BF16), sq, jax.ShapeDtypeStruct((N_HEADS, CHUNK, 1), F32), vec,
                   vec, sq, vec, vec],
        scratch_shapes=[pltpu.VMEM((ts, D_A), F32), pltpu.VMEM((ts, D_A), F32)],
        compiler_params=_params(),
    )(z, z, z, dyc, dyc, w_spatial, bsp_full, gain, bias, w_pool, b_pool, pool_scale, dep)


def _bwd_in(dz, dx1, x, h1, mod6, n1pre, win_g, dep, ts):
    s_len = x.shape[0]
    cs = D_Z // N_CHIPS

    def body(dz_ref, dx1_ref, x_ref, h1_ref, mod_ref, g_ref, w_ref, dep_ref, gx_ref, dshift_ref, da_ref, dw_hbm,
             dw_ref, wfull_ref, dw_sem):
        _zero_on_first_step(dshift_ref, da_ref, dw_ref)
        _join_w_in_on_first_step(w_ref, wfull_ref)
        dzb = dz_ref[...]
        dw = _dot_tn(h1_ref[...], dzb)
        for j in range(N_CHIPS):
            dw_ref[j] += dw[:, j * cs:(j + 1) * cs].reshape(2, D_MODEL // 2, cs)
        dh = _dot_nt(dzb, wfull_ref[...])
        xv = x_ref[...]
        r = lax.rsqrt(_rowmean(xv * xv) + EPS)
        xh = xv * r
        a1 = g_ref[...] * (1.0 + mod_ref[1:2, :])
        dshift_ref[...] += _colsum(dh)
        da_ref[...] += _colsum(dh * xh)
        dxh = dh * a1
        gx_ref[...] = dx1_ref[...] + r * (dxh - xh * _rowmean(dxh * xh))
        for j in range(N_CHIPS):
            _store_shard_on_last_step(dw_ref, dw_hbm, dw_sem, j)
        _wait_stores_on_last_step((dw_ref, dw_hbm, dw_sem))

    vec = jax.ShapeDtypeStruct((1, D_MODEL), F32)
    dw_shape = (N_CHIPS, 2, D_MODEL // 2, cs)
    return pl.pallas_call(
        body, name="bwd_in", grid=(s_len // ts,),
        in_specs=[_rows(ts, D_Z), _rows(ts, D_MODEL), _rows(ts, D_MODEL), _rows(ts, D_MODEL),
                  _const((N_MOD, D_MODEL)), _const((1, D_MODEL)), _VMEM, _ANY],
        out_specs=[_rows(ts, D_MODEL), _const((1, D_MODEL)), _const((1, D_MODEL)), _ANY],
        out_shape=[jax.ShapeDtypeStruct((s_len, D_MODEL), F32), vec, vec, jax.ShapeDtypeStruct(dw_shape, F32)],
        scratch_shapes=[pltpu.VMEM(dw_shape, F32), pltpu.VMEM((D_MODEL, D_Z), BF16),
                        pltpu.SemaphoreType.DMA((N_CHIPS,))],
        compiler_params=_params(),
    )(dz, dx1, x, h1, mod6, n1pre, win_g, dep)


def _adamw_math(w, g, m, v):
    m = ADAM_B1 * m + (1.0 - ADAM_B1) * g
    v = ADAM_B2 * v + (1.0 - ADAM_B2) * (g * g)
    m_hat = m / (1.0 - ADAM_B1 ** ADAM_STEP)
    v_hat = v / (1.0 - ADAM_B2 ** ADAM_STEP)
    delta = -ADAM_LR * (m_hat / (jnp.sqrt(v_hat) + ADAM_EPS) + ADAM_WD * w)
    return delta, m, v


def _adamw(g, w, m, v, name, tr):
    rows, cols = w.shape

    def body(g_ref, w_ref, m_ref, v_ref, d_ref, nm_ref, nv_ref):
        d, nm, nv = _adamw_math(w_ref[...], g_ref[...], m_ref[...], v_ref[...])
        d_ref[...] = d
        nm_ref[...] = nm
        nv_ref[...] = nv

    spec = _rows(tr, cols)
    shape = jax.ShapeDtypeStruct((rows, cols), F32)
    return pl.pallas_call(
        body, name=name, grid=(rows // tr,), in_specs=[spec] * 4, out_specs=[spec] * 3, out_shape=[shape] * 3,
        compiler_params=_params(),
    )(g, w, m, v)


def _ada_grad_adamw(sc_t, dmod_shard, w, m, v, tr):
    rows, cols = w.shape

    def body(s_ref, dm_ref, w_ref, m_ref, v_ref, g_ref, d_ref, nm_ref, nv_ref):
        g = s_ref[:, 0:1] * dm_ref[0:1, :]
        for b in range(1, N_DEV):
            g = g + s_ref[:, b:b + 1] * dm_ref[b:b + 1, :]
        g_ref[...] = g
        d, nm, nv = _adamw_math(w_ref[...], g, m_ref[...], v_ref[...])
        d_ref[...] = d
        nm_ref[...] = nm
        nv_ref[...] = nv

    spec = _rows(tr, cols)
    shape = jax.ShapeDtypeStruct((rows, cols), F32)
    return pl.pallas_call(
        body, name="ada_grad_adamw", grid=(rows // tr,),
        in_specs=[_rows(tr, N_DEV), _const((N_DEV, cols)), spec, spec, spec],
        out_specs=[spec] * 4, out_shape=[shape] * 4, compiler_params=_params(),
    )(sc_t, dmod_shard, w, m, v)


def _mod_grads(da1, dshift1, dgate1, dg1post, da2, dshift2, dgate2, dg2post, mod6, n1pre, n2pre):
    def body(da1_ref, ds1_ref, dgt1_ref, dg1_ref, da2_ref, ds2_ref, dgt2_ref, dg2_ref, mod_ref, n1_ref, n2_ref,
             dmod_ref, dn_ref):
        dmod_ref[0:1, :] = ds1_ref[...]
        dmod_ref[1:2, :] = da1_ref[...] * n1_ref[...]
        dmod_ref[2:3, :] = dgt1_ref[...]
        dmod_ref[3:4, :] = ds2_ref[...]
        dmod_ref[4:5, :] = da2_ref[...] * n2_ref[...]
        dmod_ref[5:6, :] = dgt2_ref[...]
        dn_ref[0:1, :] = da1_ref[...] * (1.0 + mod_ref[1:2, :])
        dn_ref[1:2, :] = dg1_ref[...]
        dn_ref[2:3, :] = da2_ref[...] * (1.0 + mod_ref[4:5, :])
        dn_ref[3:4, :] = dg2_ref[...]

    return pl.pallas_call(
        body, name="mod_grads",
        out_shape=[jax.ShapeDtypeStruct((N_MOD, D_MODEL), F32), jax.ShapeDtypeStruct((4, D_MODEL), F32)],
    )(da1, dshift1, dgate1, dg1post, da2, dshift2, dgate2, dg2post, mod6, n1pre, n2pre)


def _position():
    x, y, c = lax.axis_index("x"), lax.axis_index("y"), lax.axis_index("c")
    return x, y, c


def _flip(v, bit):
    return 1 - v if bit else v


def _peer(x, y, c, k):
    return (_flip(x, k & 4), _flip(y, k & 2), _flip(c, k & 1))


def _remote(src, dst, send_sem, recv_sem, device):
    return pltpu.make_async_remote_copy(src_ref=src, dst_ref=dst, send_sem=send_sem, recv_sem=recv_sem,
                                        device_id=device, device_id_type=MESH)


def _cast_to_slot(w, pos, dep, name, tr):
    rows, cols = w.shape

    def body(pos_ref, w_ref, dep_ref, o_ref):
        o_ref[0] = w_ref[...].astype(BF16)

    return pl.pallas_call(
        body, name=name,
        grid_spec=pltpu.PrefetchScalarGridSpec(
            num_scalar_prefetch=1, grid=(rows // tr,),
            in_specs=[pl.BlockSpec((tr, cols), lambda i, pos: (i, 0)), _ANY],
            out_specs=pl.BlockSpec((1, tr, cols), lambda i, pos: (pos[1], i, 0))),
        out_shape=jax.ShapeDtypeStruct((N_CHIPS, rows, cols), BF16), compiler_params=_params(),
    )(pos, w, dep)


def _mod_exchange(c_row, w_ada_shard, b_ada_row):
    cs = w_ada_shard.shape[1]

    def body(c_ref, w_ref, b_ref, mod_ref, sc_ref, rows_ref, send1, recv1, send2, recv2):
        x, y, c = _position()
        me = 4 * x + 2 * y + c
        chip = 2 * x + y
        cv = c_ref[...]
        sc_ref[me] = cv * jax.nn.sigmoid(cv)
        gather = [_remote(sc_ref.at[me], sc_ref.at[me], send1.at[k - 1], recv1.at[k - 1], _peer(x, y, c, k))
                  for k in range(1, N_DEV)]
        for cp in gather:
            cp.start()
        for k in range(1, N_DEV):
            px, py, pc = _peer(x, y, c, k)
            src = 4 * px + 2 * py + pc
            _remote(sc_ref.at[src], sc_ref.at[src], send1.at[k - 1], recv1.at[k - 1], (px, py, pc)).wait_recv()
        for cp in gather:
            cp.wait_send()
        sc_all = jnp.concatenate([sc_ref[b] for b in range(N_DEV)], axis=0)
        part = _dot(sc_all.astype(BF16), w_ref[...].astype(BF16))
        part = part + b_ref[:, pl.ds(pl.multiple_of(chip * cs, 128), cs)]
        for b in range(N_DEV):
            rows_ref[b] = part[b:b + 1, :]
        mod_ref[chip] = rows_ref[me]
        hand = []
        for k in (2, 4, 6):
            px, py, _ = _peer(x, y, c, k)
            hand.append(_remote(rows_ref.at[4 * px + 2 * py + c], mod_ref.at[chip], send2.at[k // 2 - 1],
                                recv2.at[k // 2 - 1], (px, py, c)))
        for cp in hand:
            cp.start()
        for k in (2, 4, 6):
            px, py, _ = _peer(x, y, c, k)
            pchip = 2 * px + py
            _remote(rows_ref.at[me], mod_ref.at[pchip], send2.at[k // 2 - 1], recv2.at[k // 2 - 1],
                    (px, py, c)).wait_recv()
        for cp in hand:
            cp.wait_send()

    return pl.pallas_call(
        body, name="mod_exchange",
        in_specs=[_VMEM, _VMEM, _VMEM], out_specs=[_VMEM, _VMEM],
        out_shape=[jax.ShapeDtypeStruct((N_CHIPS, 1, cs), F32), jax.ShapeDtypeStruct((N_DEV, 1, D_MODEL), F32)],
        scratch_shapes=[pltpu.VMEM((N_DEV, 1, cs), F32), pltpu.SemaphoreType.DMA((N_DEV - 1,)),
                        pltpu.SemaphoreType.DMA((N_DEV - 1,)), pltpu.SemaphoreType.DMA((N_CHIPS - 1,)),
                        pltpu.SemaphoreType.DMA((N_CHIPS - 1,))],
        compiler_params=pltpu.CompilerParams(vmem_limit_bytes=VMEM_LIMIT),
    )(c_row, w_ada_shard, b_ada_row)


_HBM = pl.BlockSpec(memory_space=pltpu.HBM)
_SEM = pl.BlockSpec(memory_space=pltpu.SEMAPHORE)
_EFFECT = pltpu.SideEffectType.DATAFLOW_SIDE_EFFECTING
_CHIP_HOPS = (2, 4, 6)


def _in_hbm(a):
    return pltpu.with_memory_space_constraint(a, pltpu.HBM)


def _sems3():
    return pltpu.SemaphoreType.DMA((len(_CHIP_HOPS),))


def _ag_start(lands, after, name):
    n = len(lands)

    def body(*refs):
        zones = refs[:n]
        sends, recvs = refs[n + 1:2 * n + 1], refs[2 * n + 1:3 * n + 1]
        x, y, c = _position()
        chip = 2 * x + y
        for i in range(n):
            half = zones[i].shape[1] // 2
            mine = zones[i].at[chip, pl.ds(c * half, half)]
            for s, k in enumerate(_CHIP_HOPS):
                px, py, _ = _peer(x, y, c, k)
                _remote(mine, mine, sends[i].at[s], recvs[i].at[s], (px, py, c)).start()

    out = pl.pallas_call(
        body, name=name,
        in_specs=[_HBM] * n + [_ANY],
        out_specs=[_SEM] * (2 * n) + [_HBM] * n,
        out_shape=[_sems3()] * (2 * n) + [pltpu.HBM(z.shape, BF16) for z in lands],
        input_output_aliases={i: 2 * n + i for i in range(n)},
        compiler_params=pltpu.CompilerParams(has_side_effects=_EFFECT),
    )(*[_in_hbm(z) for z in lands], after)
    return [(out[2 * n + i], out[i], out[n + i]) for i in range(n)]


def _ag_pass(group, after, name):
    n = len(group)

    def body(*refs):
        zones = refs[:n]
        sends, recvs = refs[n:2 * n], refs[2 * n:3 * n]
        fsends, frecvs = refs[4 * n + 1:5 * n + 1], refs[5 * n + 1:6 * n + 1]
        x, y, c = _position()
        chip = 2 * x + y
        for i in range(n):
            half = zones[i].shape[1] // 2
            rows = pl.ds(c * half, half)
            for s, k in enumerate(_CHIP_HOPS):
                px, py, _ = _peer(x, y, c, k)
                landed = zones[i].at[2 * px + py, rows]
                _remote(landed, landed, sends[i].at[s], recvs[i].at[s], (px, py, c)).wait_recv()
                _remote(landed, landed, fsends[i].at[s], frecvs[i].at[s], (x, y, 1 - c)).start()
        for i in range(n):
            half = zones[i].shape[1] // 2
            mine = zones[i].at[chip, pl.ds(c * half, half)]
            for s, k in enumerate(_CHIP_HOPS):
                px, py, _ = _peer(x, y, c, k)
                _remote(mine, mine, sends[i].at[s], recvs[i].at[s], (px, py, c)).wait_send()

    out = pl.pallas_call(
        body, name=name,
        in_specs=[_HBM] * n + [_SEM] * (2 * n) + [_ANY],
        out_specs=[_HBM] * n + [_SEM] * (2 * n),
        out_shape=[pltpu.HBM(g[0].shape, BF16) for g in group] + [_sems3()] * (2 * n),
        input_output_aliases={i: i for i in range(n)},
        compiler_params=pltpu.CompilerParams(has_side_effects=_EFFECT),
    )(*[g[0] for g in group], *[g[1] for g in group], *[g[2] for g in group], after)
    return [(out[i], out[n + i], out[2 * n + i]) for i in range(n)]


def _ag_done(group, name):
    n = len(group)

    def body(*refs):
        lands = refs[:n]
        fsends, frecvs = refs[n:2 * n], refs[2 * n:3 * n]
        x, y, c = _position()
        for i in range(n):
            half = lands[i].shape[1] // 2
            for s, k in enumerate(_CHIP_HOPS):
                px, py, _ = _peer(x, y, c, k)
                sent = lands[i].at[2 * px + py, pl.ds(c * half, half)]
                got = lands[i].at[2 * px + py, pl.ds((1 - c) * half, half)]
                cp = _remote(sent, got, fsends[i].at[s], frecvs[i].at[s], (x, y, 1 - c))
                cp.wait_recv()
                cp.wait_send()

    out = pl.pallas_call(
        body, name=name,
        in_specs=[_HBM] * n + [_SEM] * (2 * n),
        out_specs=[_HBM] * n,
        out_shape=[pltpu.HBM(g[0].shape, BF16) for g in group],
        input_output_aliases={i: i for i in range(n)},
        compiler_params=pltpu.CompilerParams(has_side_effects=_EFFECT),
    )(*[g[0] for g in group], *[g[1] for g in group], *[g[2] for g in group])
    return list(out)


def _small_spread_start(slots, after):
    def body(z_ref, after_ref, sends, recvs, z_out):
        x, y, c = _position()
        mine = z_ref.at[2 * x + y]
        for s, k in enumerate(_CHIP_HOPS):
            px, py, _ = _peer(x, y, c, k)
            _remote(mine, mine, sends.at[s], recvs.at[s], (px, py, c)).start()

    sends, recvs, out = pl.pallas_call(
        body, name="small_spread_start",
        in_specs=[_HBM, _ANY], out_specs=[_SEM, _SEM, _HBM],
        out_shape=[_sems3(), _sems3(), pltpu.HBM(slots.shape, F32)],
        input_output_aliases={0: 2},
        compiler_params=pltpu.CompilerParams(has_side_effects=_EFFECT),
    )(_in_hbm(slots), after)
    return out, sends, recvs


def _small_spread_wait(slots, sends, recvs, afters):
    def body(z_ref, sends, recvs, *rest):
        x, y, c = _position()
        mine = z_ref.at[2 * x + y]
        for s, k in enumerate(_CHIP_HOPS):
            px, py, _ = _peer(x, y, c, k)
            cp = _remote(mine, z_ref.at[2 * px + py], sends.at[s], recvs.at[s], (px, py, c))
            cp.wait_recv()
            cp.wait_send()

    return pl.pallas_call(
        body, name="small_spread_wait",
        in_specs=[_HBM, _SEM, _SEM] + [_ANY] * len(afters), out_specs=_HBM, out_shape=pltpu.HBM(slots.shape, F32),
        input_output_aliases={0: 0},
        compiler_params=pltpu.CompilerParams(has_side_effects=_EFFECT),
    )(slots, sends, recvs, *afters)


def _sibling_sum(pgs, name, small=None):
    n = len(pgs)
    k = 0 if small is None else 1
    units = [(i, j) for i in range(n) for j in range(N_CHIPS)]

    def body(*refs):
        refs = list(refs)
        take = lambda count: [refs.pop(0) for _ in range(count)]
        ins, small_in = take(n), take(k)
        qbs, owns, slots_out = take(n), take(n), take(k)
        mine, other, stage, got = take(n), take(n), take(n), take(n)
        load_a, load_b, send, recv = take(4)
        x, y, c = _position()
        chip = 2 * x + y
        if k:
            sib_ref, pair_send, pair_recv = take(3)
            pair = _remote(small_in[0], sib_ref, pair_send, pair_recv, (x, y, 1 - c))
            pair.start()
        loads_a = [pltpu.make_async_copy(ins[i].at[j, 1 - c], other[i].at[j], load_a.at[u])
                   for u, (i, j) in enumerate(units)]
        loads_b = [pltpu.make_async_copy(ins[i].at[j, c], mine[i].at[j], load_b.at[u])
                   for u, (i, j) in enumerate(units)]
        for cp in loads_a + loads_b:
            cp.start()
        sent = []
        for u, (i, j) in enumerate(units):
            loads_a[u].wait()
            stage[i][j] = other[i][j].astype(BF16)
            cp = _remote(stage[i].at[j], got[i].at[j], send.at[u], recv.at[u], (x, y, 1 - c))
            cp.start()
            sent.append(cp)
        for u, (i, j) in enumerate(units):
            loads_b[u].wait()
            sent[u].wait_recv()
            q = mine[i][j] + got[i][j].astype(F32)
            mine[i][j] = q
            qbs[i][j] = q.astype(BF16)
        for i in range(n):
            owns[i][...] = mine[i][chip]
        if k:
            pair.wait()
            slots_out[0][chip] = small_in[0][...] + sib_ref[...]
        for cp in sent:
            cp.wait_send()

    wire = [(N_CHIPS,) + p.shape[2:] for p in pgs]
    extra_out, extra_scratch = [], []
    if k:
        extra_out = [jax.ShapeDtypeStruct((N_CHIPS,) + small.shape, F32)]
        extra_scratch = [pltpu.VMEM(small.shape, F32), pltpu.SemaphoreType.DMA, pltpu.SemaphoreType.DMA]
    out = pl.pallas_call(
        body, name=name, in_specs=[_ANY] * n + [_VMEM] * k, out_specs=[_VMEM] * (2 * n + k),
        out_shape=[jax.ShapeDtypeStruct(w, BF16) for w in wire] + [jax.ShapeDtypeStruct(w[1:], F32) for w in wire]
        + extra_out,
        scratch_shapes=[pltpu.VMEM(w, F32) for w in wire] * 2 + [pltpu.VMEM(w, BF16) for w in wire] * 2
        + [pltpu.SemaphoreType.DMA((len(units),))] * 4 + extra_scratch,
        compiler_params=pltpu.CompilerParams(vmem_limit_bytes=VMEM_LIMIT),
    )(*pgs, *([small] if k else []))
    return list(out[:n]), list(out[n:2 * n]), list(out[2 * n:])


def _rs_start(qbs, name):
    n = len(qbs)

    def body(*refs):
        outs, inboxes = refs[:n], refs[n:2 * n]
        sends, recvs = refs[2 * n:3 * n], refs[3 * n:4 * n]
        x, y, c = _position()
        chip = 2 * x + y
        for i in range(n):
            for s, k in enumerate(_CHIP_HOPS):
                px, py, _ = _peer(x, y, c, k)
                _remote(outs[i].at[2 * px + py], inboxes[i].at[chip], sends[i].at[s], recvs[i].at[s], (px, py, c)).start()

    inboxes = [_in_hbm(lax.empty(q.shape, BF16)) for q in qbs]
    out = pl.pallas_call(
        body, name=name,
        in_specs=[_HBM] * (2 * n),
        out_specs=[_SEM] * (2 * n) + [_HBM] * (2 * n),
        out_shape=[_sems3()] * (2 * n) + [pltpu.HBM(q.shape, BF16) for q in qbs] * 2,
        input_output_aliases={i: 2 * n + i for i in range(2 * n)},
        compiler_params=pltpu.CompilerParams(has_side_effects=_EFFECT),
    )(*[_in_hbm(q) for q in qbs], *inboxes)
    return [(out[2 * n + i], out[3 * n + i], out[i], out[n + i]) for i in range(n)]


def _rs_wait(group, after, name):
    n = len(group)

    def body(*refs):
        outs, inboxes = refs[:n], refs[n:2 * n]
        sends, recvs = refs[2 * n:3 * n], refs[3 * n:4 * n]
        x, y, c = _position()
        for i in range(n):
            for s, k in enumerate(_CHIP_HOPS):
                px, py, _ = _peer(x, y, c, k)
                slot = 2 * px + py
                cp = _remote(outs[i].at[slot], inboxes[i].at[slot], sends[i].at[s], recvs[i].at[s], (px, py, c))
                cp.wait_recv()
                cp.wait_send()

    out = pl.pallas_call(
        body, name=name,
        in_specs=[_HBM] * (2 * n) + [_SEM] * (2 * n) + [_ANY],
        out_specs=[_HBM] * n,
        out_shape=[pltpu.HBM(g[1].shape, BF16) for g in group],
        input_output_aliases={n + i: i for i in range(n)},
        compiler_params=pltpu.CompilerParams(has_side_effects=_EFFECT),
    )(*[g[0] for g in group], *[g[1] for g in group], *[g[2] for g in group], *[g[3] for g in group], after)
    return list(out)


def _final_share(inboxes, owns, name):
    n = len(inboxes)
    units = [(i, s) for i in range(n) for s in range(len(_CHIP_HOPS))]

    def body(*refs):
        ins, mine, outs, landed = refs[:n], refs[n:2 * n], refs[2 * n:3 * n], refs[3 * n:4 * n]
        load, send, recv = refs[4 * n:]
        x, y, c = _position()
        loads = []
        for u, (i, s) in enumerate(units):
            px, py, _ = _peer(x, y, c, _CHIP_HOPS[s])
            loads.append(pltpu.make_async_copy(ins[i].at[2 * px + py], landed[i].at[s], load.at[u]))
        for cp in loads:
            cp.start()
        copies = []
        for i in range(n):
            for s in range(len(_CHIP_HOPS)):
                loads[len(_CHIP_HOPS) * i + s].wait()
            total = (landed[i][0].astype(F32) + landed[i][1].astype(F32)) + landed[i][2].astype(F32)
            outs[i][c] = total + mine[i][...]
            cp = _remote(outs[i].at[c], outs[i].at[c], send.at[i], recv.at[i], (x, y, 1 - c))
            cp.start()
            copies.append(cp)
        for i in range(n):
            theirs = outs[i].at[1 - c]
            _remote(theirs, theirs, send.at[i], recv.at[i], (x, y, 1 - c)).wait_recv()
        for cp in copies:
            cp.wait_send()

    return pl.pallas_call(
        body, name=name, in_specs=[_ANY] * n + [_VMEM] * n, out_specs=[_VMEM] * n,
        out_shape=[jax.ShapeDtypeStruct((2,) + o.shape, F32) for o in owns],
        scratch_shapes=[pltpu.VMEM((len(_CHIP_HOPS),) + o.shape, BF16) for o in owns]
        + [pltpu.SemaphoreType.DMA((len(units),)), pltpu.SemaphoreType.DMA((n,)), pltpu.SemaphoreType.DMA((n,))],
        compiler_params=pltpu.CompilerParams(vmem_limit_bytes=VMEM_LIMIT),
    )(*inboxes, *owns)


_SMALL = (("b_ada", N_MOD * D_MODEL), ("norm1_pre", D_MODEL), ("norm1_post", D_MODEL), ("norm2_pre", D_MODEL),
          ("norm2_post", D_MODEL), ("w_spatial", N_HEADS * CHUNK * CHUNK), ("b_spatial", N_HEADS * CHUNK),
          ("ln_v_gain", D_A), ("ln_v_bias", D_A), ("w_pool", N_HEADS * GROUP_DIM * GROUP_DIM),
          ("b_pool", D_B), ("pool_scale", D_B))
_MOD_ROWS = N_MOD * D_MODEL // 128


def _packed_rows(size):
    return -(-(size // 128) // 8) * 8


def _pack(parts):
    out = []
    for name, size in _SMALL:
        a = parts[name].reshape(size // 128, 128)
        pad = _packed_rows(size) - a.shape[0]
        out.append(jnp.pad(a, ((0, pad), (0, 0))) if pad else a)
    return out


def _small_adamw(slots, ws, ms, vs):
    n = len(_SMALL)
    head = N_DEV * _MOD_ROWS

    def body(*refs):
        s_ref, w, m, v = refs[0], refs[1:1 + n], refs[1 + n:1 + 2 * n], refs[1 + 2 * n:1 + 3 * n]
        outs = refs[1 + 3 * n:1 + 7 * n]
        dmod_ref, loss_ref, t_ref = refs[1 + 7 * n:]
        t_ref[...] = ((s_ref[0] + s_ref[1]) + s_ref[2]) + s_ref[3]
        dmod_ref[...] = t_ref[0:head, :]
        loss_ref[...] = t_ref[t_ref.shape[0] - 8:, :]
        row = head
        for i, (_, size) in enumerate(_SMALL):
            if i == 0:
                g = t_ref[0:_MOD_ROWS, :]
                for b in range(1, N_DEV):
                    g = g + t_ref[b * _MOD_ROWS:(b + 1) * _MOD_ROWS, :]
            else:
                g = t_ref[row:row + size // 128, :]
                row += _packed_rows(size)
            d, nm, nv = _adamw_math(w[i][...], g, m[i][...], v[i][...])
            for ref, val in zip(outs[4 * i:4 * i + 4], (g, d, nm, nv)):
                ref[...] = val

    each = [jax.ShapeDtypeStruct((size // 128, 128), F32) for _, size in _SMALL for _ in range(4)]
    out = pl.pallas_call(
        body, name="small_adamw",
        out_shape=each + [jax.ShapeDtypeStruct((head, 128), F32), jax.ShapeDtypeStruct((8, 128), F32)],
        scratch_shapes=[pltpu.VMEM(slots.shape[1:], F32)],
        compiler_params=pltpu.CompilerParams(vmem_limit_bytes=VMEM_LIMIT),
    )(slots, *ws, *ms, *vs)
    return [out[4 * i:4 * i + 4] for i in range(n)], out[4 * n], out[4 * n + 1]


def kernel(x, c, w_ada, b_ada, norm1_pre, norm1_post, w_in, w_spatial, b_spatial, ln_v_gain, ln_v_bias, w_pool, b_pool, pool_scale, w_out, norm2_pre, norm2_post, w_fc1, w_fc2, loss_target, m_w_ada, m_b_ada, m_norm1_pre, m_norm1_post, m_w_in, m_w_spatial, m_b_spatial, m_ln_v_gain, m_ln_v_bias, m_w_pool, m_b_pool, m_pool_scale, m_w_out, m_norm2_pre, m_norm2_post, m_w_fc1, m_w_fc2, v_w_ada, v_b_ada, v_norm1_pre, v_norm1_post, v_w_in, v_w_spatial, v_b_spatial, v_ln_v_gain, v_ln_v_bias, v_w_pool, v_b_pool, v_pool_scale, v_w_out, v_norm2_pre, v_norm2_post, v_w_fc1, v_w_fc2):
    weights = dict(w_ada=w_ada, b_ada=b_ada, norm1_pre=norm1_pre, norm1_post=norm1_post, w_in=w_in,
                   w_spatial=w_spatial, b_spatial=b_spatial, ln_v_gain=ln_v_gain, ln_v_bias=ln_v_bias, w_pool=w_pool,
                   b_pool=b_pool, pool_scale=pool_scale, w_out=w_out, norm2_pre=norm2_pre, norm2_post=norm2_post,
                   w_fc1=w_fc1, w_fc2=w_fc2)
    m_old = dict(w_ada=m_w_ada, b_ada=m_b_ada, norm1_pre=m_norm1_pre, norm1_post=m_norm1_post, w_in=m_w_in,
                 w_spatial=m_w_spatial, b_spatial=m_b_spatial, ln_v_gain=m_ln_v_gain, ln_v_bias=m_ln_v_bias,
                 w_pool=m_w_pool, b_pool=m_b_pool, pool_scale=m_pool_scale, w_out=m_w_out, norm2_pre=m_norm2_pre,
                 norm2_post=m_norm2_post, w_fc1=m_w_fc1, w_fc2=m_w_fc2)
    v_old = dict(w_ada=v_w_ada, b_ada=v_b_ada, norm1_pre=v_norm1_pre, norm1_post=v_norm1_post, w_in=v_w_in,
                 w_spatial=v_w_spatial, b_spatial=v_b_spatial, ln_v_gain=v_ln_v_gain, ln_v_bias=v_ln_v_bias,
                 w_pool=v_w_pool, b_pool=v_b_pool, pool_scale=v_pool_scale, w_out=v_w_out, norm2_pre=v_norm2_pre,
                 norm2_post=v_norm2_post, w_fc1=v_w_fc1, w_fc2=v_w_fc2)
    order = ("w_ada", "b_ada", "norm1_pre", "norm1_post", "w_in", "w_spatial", "b_spatial", "ln_v_gain", "ln_v_bias",
             "w_pool", "b_pool", "pool_scale", "w_out", "norm2_pre", "norm2_post", "w_fc1", "w_fc2")
    big = ("w_in", "w_out", "w_fc1", "w_fc2")
    mx, my, mc = _position()
    me = 4 * mx + 2 * my + mc
    chip = 2 * mx + my
    row = lambda a: a.reshape(1, -1)

    pos = jnp.stack([mc, chip]).astype(jnp.int32)
    xs, target = x[0], loss_target[0]
    n1pre, n1post, n2pre, n2post = row(norm1_pre), row(norm1_post), row(norm2_pre), row(norm2_post)
    mixer = (w_spatial, jnp.repeat(b_spatial.T, HEAD_DIM, axis=1), row(ln_v_gain), row(ln_v_bias), w_pool,
             row(b_pool), row(pool_scale))
    ts_big, ts_mid = 512, 256

    mod4, sc_all = _mod_exchange(c, w_ada, row(b_ada))
    mod6 = mod4.reshape(N_MOD, D_MODEL)
    ag = _ag_start([_cast_to_slot(weights[n], pos, mod4, "cast_" + n, 256) for n in big[:2]], mod4, "ag_start_mix")
    ag += _ag_start([_cast_to_slot(weights[n], pos, ag[0][0], "cast_" + n, 256) for n in big[2:]], ag[0][0],
                    "ag_start_mlp")

    win_g, wout_g = _ag_done(_ag_pass([ag[0], ag[1]], ag[2][0], "ag_pass_mix"), "ag_done_mix")
    h1, z, ycat, mix, x1, h2 = _fwd_mix(xs, mod6, n1pre, n1post, n2pre, win_g, wout_g, *mixer, ts_mid)
    (fc1_g,) = _ag_done(_ag_pass([ag[2]], h2, "ag_pass_fc1"), "ag_done_fc1")
    q = _fwd_fc1(h2, fc1_g, ts_big)
    (fc2_g,) = _ag_done(_ag_pass([ag[3]], q, "ag_pass_fc2"), "ag_done_fc2")
    dy, df, loss, dgate2, dg2post = _fwd_fc2_loss(q, x1, target, mod6, n2post, fc2_g, ts_big)

    def reduce_start(partials, tag, small=None):
        wire, owns, slots = _sibling_sum(partials, "sibling_sum_" + tag, small)
        return _rs_start(wire, "rs_start_" + tag), owns, slots

    def reduce_finish(state, owns, names, tag, dep):
        inboxes = _rs_wait(state, dep, "rs_wait_" + tag)
        shards = _final_share(inboxes, owns, "final_share_" + tag)
        for n, g in zip(names, shards):
            grads[n] = g.reshape(weights[n].shape)
            deltas[n], new_m[n], new_v[n] = _adamw(grads[n], weights[n], m_old[n], v_old[n], "adamw_" + n, 256)

    grads, deltas, new_m, new_v = {}, {}, {}, {}
    dp, g_fc2 = _bwd_fc2(df, q, fc2_g, ts_big)
    state_fc2, owns_fc2, _ = reduce_start([g_fc2], "fc2")
    dx1, dyc, dshift2, da2, dgate1, dg1post, g_fc1, g_out = _bwd_fc1_out(
        dp, dy, x1, mix, h2, ycat, mod6, n2pre, n1post, fc1_g, wout_g, state_fc2[0][0], ts_mid)
    state_mid, owns_mid, _ = reduce_start([g_fc1, g_out], "mid")
    dz, dws, dbsp, dgain, dbias, dwp, dbp, dps = _mixer_bwd(z, dyc, *mixer, state_mid[0][0], ts_mid)
    grad_x, dshift1, da1, g_in = _bwd_in(dz, dx1, xs, h1, mod6, n1pre, win_g, state_mid[0][0], ts_mid)
    dmod6, dnorms = _mod_grads(da1, dshift1, dgate1, dg1post, da2, dshift2, dgate2, dg2post, mod6, n1pre, n2pre)

    parts = dict(b_ada=dmod6, norm1_pre=dnorms[0], norm1_post=dnorms[1], norm2_pre=dnorms[2], norm2_post=dnorms[3],
                 w_spatial=dws, b_spatial=dbsp, ln_v_gain=dgain, ln_v_bias=dbias, w_pool=dwp, b_pool=dbp,
                 pool_scale=dps)
    pieces = _pack(parts)
    slots = lax.dynamic_update_slice(jnp.zeros((N_DEV * _MOD_ROWS, 128), F32), pieces[0], (me * _MOD_ROWS, 0))
    loss_tile = jnp.pad(loss, ((0, 7), (0, 127)))
    state_in, owns_in, pair_sum = reduce_start(
        [g_in], "in", jnp.concatenate([slots] + pieces[1:] + [loss_tile], axis=0))
    spread = _small_spread_start(pair_sum[0], state_in[0][0])
    reduce_finish(state_fc2 + state_mid, owns_fc2 + owns_mid, ("w_fc2", "w_fc1", "w_out"), "mlp", spread[0])
    flat = lambda d: [d[n].reshape(size // 128, 128) for n, size in _SMALL]
    small_out, dmod_all, loss_tile = _small_adamw(
        _small_spread_wait(*spread, [deltas[n] for n in ("w_fc2", "w_fc1", "w_out")]), flat(weights), flat(m_old),
        flat(v_old))
    loss = loss_tile[0, 0]
    for (n, _), (g, d, nm, nv) in zip(_SMALL, small_out):
        shape = weights[n].shape
        grads[n], deltas[n], new_m[n], new_v[n] = g.reshape(shape), d.reshape(shape), nm.reshape(shape), nv.reshape(shape)

    dmod_all = dmod_all.reshape(N_DEV, N_MOD * D_MODEL)
    cs = w_ada.shape[1]
    dmod_shard = lax.dynamic_slice(dmod_all, (0, chip * cs), (N_DEV, cs))
    sc_t = sc_all.reshape(N_DEV, D_MODEL).T
    grads["w_ada"], deltas["w_ada"], new_m["w_ada"], new_v["w_ada"] = _ada_grad_adamw(
        sc_t, dmod_shard, w_ada, m_w_ada, v_w_ada, 256)

    reduce_finish(state_in, owns_in, ("w_in",), "in", deltas["w_ada"])

    return (loss, grad_x[None], *[grads[n] for n in order], *[deltas[n] for n in order],
            *[new_m[n] for n in order], *[new_v[n] for n in order])
```

```python
import functools

import jax
import jax.numpy as jnp
from jax import lax
from jax.experimental import pallas as pl
from jax.experimental.pallas import tpu as pltpu

F32 = jnp.float32
BF16 = jnp.bfloat16
MESH = pl.DeviceIdType.MESH

D_MODEL = 1024
D_A = 512
D_B = 512
D_Z = 2 * D_A + D_B
N_HEADS = 4
HEAD_DIM = 128
CHUNK = 128
POOL_WINDOWS = (2, 4, 8, 16)
GROUP_DIM = 128
D_FF = 4096
N_MOD = 6
EPS = 1e-6
HALO = 16
N_CHIPS = 4
N_DEV = 8

ADAM_LR = 0.001
ADAM_B1 = 0.9
ADAM_B2 = 0.999
ADAM_EPS = 1e-08
ADAM_WD = 0.01
ADAM_STEP = 10

VMEM_LIMIT = 56 * 1024 * 1024
LANES = 128
SUBLANES = 8

_VMEM = pl.BlockSpec(memory_space=pltpu.VMEM)
_ANY = pl.BlockSpec(memory_space=pl.ANY)


def _params(n_grid_axes=1):
    return pltpu.CompilerParams(dimension_semantics=("arbitrary",) * n_grid_axes, vmem_limit_bytes=VMEM_LIMIT)


def _rows(ts, width):
    return pl.BlockSpec((ts, width), lambda i: (i, 0))


def _const(shape):
    return pl.BlockSpec(shape, lambda i: (0,) * len(shape))


def _dot(a, b):
    return jnp.dot(a, b, preferred_element_type=F32)


def _dot_nt(a, b):
    return lax.dot_general(a, b, (((1,), (1,)), ((), ())), preferred_element_type=F32)


def _dot_tn(a, b):
    return lax.dot_general(a, b, (((0,), (0,)), ((), ())), preferred_element_type=F32)


def _rowmean(v):
    return jnp.mean(v, axis=-1, keepdims=True)


def _colsum(v):
    return jnp.sum(v, axis=0, keepdims=True)


def _gelu_parts(z):
    k0 = 0.7978845608028654
    k1 = 0.044715
    z2 = z * z
    t = jnp.tanh(k0 * (z + k1 * z * z2))
    g = 0.5 * z * (1.0 + t)
    dg = 0.5 * (1.0 + t) + 0.5 * z * (1.0 - t * t) * (k0 * (1.0 + 3.0 * k1 * z2))
    return g, dg


def _tril_weights(ws_ref):
    r = lax.broadcasted_iota(jnp.int32, (CHUNK, CHUNK), 0)
    s = lax.broadcasted_iota(jnp.int32, (CHUNK, CHUNK), 1)
    mask = (s <= r).astype(F32)
    return [(ws_ref[h] * mask).astype(BF16) for h in range(N_HEADS)]


def _window_counts(first_row, n_rows):
    pos = (first_row + lax.broadcasted_iota(jnp.int32, (n_rows, 1), 0)).astype(F32)
    return pos, [1.0 / jnp.minimum(pos + 1.0, float(w)) for w in POOL_WINDOWS]


def _causal_window_sums(ext):
    out = []
    e = ext
    shift = 1
    for g in range(len(POOL_WINDOWS)):
        e = e + pltpu.roll(e, shift, 0)
        shift *= 2
        out.append(e[:, g * GROUP_DIM:(g + 1) * GROUP_DIM])
    return out


def _anticausal_window_sums(ext):
    n = ext.shape[0]
    out = []
    e = ext
    shift = 1
    for g in range(len(POOL_WINDOWS)):
        e = e + pltpu.roll(e, n - shift, 0)
        shift *= 2
        out.append(e[:, g * GROUP_DIM:(g + 1) * GROUP_DIM])
    return out


def _fwd_mix(x, mod6, n1pre, n1post, n2pre, win_g, wout_g, w_spatial, bsp_full, gain, bias, w_pool, b_pool, pool_scale, ts):
    s_len = x.shape[0]
    rs = D_MODEL // N_CHIPS

    def body(x_ref, mod_ref, g1pre_ref, g1post_ref, g2pre_ref, win_ref, wout_ref, ws_ref, bsp_ref, gain_ref,
             bias_ref, wp_ref, bp_ref, ps_ref, h1_ref, z_ref, y_ref, mix_ref, x1_ref, h2_ref, mixed_ref, prev_ref,
             wfull_ref):
        i = pl.program_id(0)
        _zero_on_first_step(prev_ref)
        _join_w_in_on_first_step(win_ref, wfull_ref)
        xv = x_ref[...]
        r = lax.rsqrt(_rowmean(xv * xv) + EPS)
        hb = (((xv * r) * g1pre_ref[...]) * (1.0 + mod_ref[1:2, :]) + mod_ref[0:1, :]).astype(BF16)
        h1_ref[...] = hb
        z_ref[...] = _dot(hb, wfull_ref[...])

        wc = _tril_weights(ws_ref)
        u, _, _, _, _ = _mixer_forward_tile(z_ref[:, :2 * D_A], wc, bsp_ref, gain_ref[...], bias_ref[...], mixed_ref)
        y_ref[:, :D_A] = (u * mixed_ref[...]).astype(BF16)
        zb = z_ref[:, 2 * D_A:]
        sums = _causal_window_sums(jnp.concatenate([prev_ref[...], zb], axis=0))
        prev_ref[...] = zb[ts - HALO:, :]
        _, inv_counts = _window_counts(i * ts, ts)
        for g in range(len(POOL_WINDOWS)):
            lanes = slice(g * GROUP_DIM, (g + 1) * GROUP_DIM)
            diff = sums[g][HALO:, :] * inv_counts[g] - zb[:, lanes]
            lin = _dot(diff.astype(BF16), wp_ref[g].astype(BF16)) + bp_ref[:, lanes]
            y_ref[:, D_A + g * GROUP_DIM:D_A + (g + 1) * GROUP_DIM] = (lin * ps_ref[:, lanes]).astype(BF16)

        mix = None
        for j in range(N_CHIPS):
            part = _dot(y_ref[:, j * rs:(j + 1) * rs], wout_ref[j])
            mix = part if mix is None else mix + part
        mix_ref[...] = mix
        r2 = lax.rsqrt(_rowmean(mix * mix) + EPS)
        x1 = xv + mod_ref[2:3, :] * ((mix * r2) * g1post_ref[...])
        x1_ref[...] = x1
        r3 = lax.rsqrt(_rowmean(x1 * x1) + EPS)
        h2_ref[...] = (((x1 * r3) * g2pre_ref[...]) * (1.0 + mod_ref[4:5, :]) + mod_ref[3:4, :]).astype(BF16)

    vec = _const((1, D_MODEL))
    f32_rows = jax.ShapeDtypeStruct((s_len, D_MODEL), F32)
    bf16_rows = jax.ShapeDtypeStruct((s_len, D_MODEL), BF16)
    return pl.pallas_call(
        body, name="fwd_mix", grid=(s_len // ts,),
        in_specs=[_rows(ts, D_MODEL), _const((N_MOD, D_MODEL)), vec, vec, vec, _VMEM, _VMEM,
                  _const((N_HEADS, CHUNK, CHUNK)), _const((CHUNK, D_A)), _const((1, D_A)), _const((1, D_A)),
                  _const((N_HEADS, GROUP_DIM, GROUP_DIM)), _const((1, D_B)), _const((1, D_B))],
        out_specs=[_rows(ts, D_MODEL), _rows(ts, D_Z), _rows(ts, D_MODEL), _rows(ts, D_MODEL), _rows(ts, D_MODEL),
                   _rows(ts, D_MODEL)],
        out_shape=[bf16_rows, jax.ShapeDtypeStruct((s_len, D_Z), F32), bf16_rows, f32_rows, f32_rows, bf16_rows],
        scratch_shapes=[pltpu.VMEM((ts, D_A), F32), pltpu.VMEM((HALO, D_B), F32), pltpu.VMEM((D_MODEL, D_Z), BF16)],
        compiler_params=_params(),
    )(x, mod6, n1pre, n1post, n2pre, win_g, wout_g, w_spatial, bsp_full, gain, bias, w_pool, b_pool, pool_scale)


def _mixer_forward_tile(za, wc, bsp_ref, gain, bias, mixed_ref):
    ga, dga = _gelu_parts(za)
    u = ga[:, :D_A]
    v = ga[:, D_A:]
    mu = _rowmean(v)
    vc = v - mu
    rstd = lax.rsqrt(_rowmean(vc * vc) + EPS)
    vhat = vc * rstd
    vn = (vhat * gain + bias).astype(BF16)
    ts = za.shape[0]
    for k in range(ts // CHUNK):
        for h in range(N_HEADS):
            blk = vn[k * CHUNK:(k + 1) * CHUNK, h * HEAD_DIM:(h + 1) * HEAD_DIM]
            mixed_ref[k * CHUNK:(k + 1) * CHUNK, h * HEAD_DIM:(h + 1) * HEAD_DIM] = (
                _dot(wc[h], blk) + bsp_ref[:, h * HEAD_DIM:(h + 1) * HEAD_DIM])
    return u, vhat, rstd, vn, dga


def _fwd_fc1(h2, fc1_g, ts):
    s_len = h2.shape[0]
    cs = D_FF // N_CHIPS

    def body(h_ref, w_ref, q_ref):
        hb = h_ref[...]
        for j in range(N_CHIPS):
            p = jnp.maximum(_dot(hb, w_ref[j]), 0.0)
            q_ref[:, j * cs:(j + 1) * cs] = (p * p).astype(BF16)

    return pl.pallas_call(
        body, name="fwd_fc1", grid=(s_len // ts,),
        in_specs=[_rows(ts, D_MODEL), _VMEM],
        out_specs=_rows(ts, D_FF),
        out_shape=jax.ShapeDtypeStruct((s_len, D_FF), BF16),
        compiler_params=_params(),
    )(h2, fc1_g)


def _fwd_fc2_loss(q, x1, target, mod6, n2post, fc2_g, ts):
    s_len = q.shape[0]
    rs = D_FF // N_CHIPS

    def body(q_ref, x1_ref, t_ref, mod_ref, g_ref, w_ref, dy_ref, df_ref, loss_ref, dgate_ref, dg_ref):
        _zero_on_first_step(loss_ref, dgate_ref, dg_ref)
        gate = mod_ref[5:6, :]
        gn = g_ref[...]
        f = _dot(q_ref[:, 0:rs], w_ref[0])
        for j in range(1, N_CHIPS):
            f = f + _dot(q_ref[:, j * rs:(j + 1) * rs], w_ref[j])
        r4 = lax.rsqrt(_rowmean(f * f) + EPS)
        fh = f * r4
        err = (x1_ref[...] + gate * (fh * gn)) - t_ref[...]
        loss_ref[...] += 0.5 * jnp.sum(_rowmean(err * err), axis=0, keepdims=True)
        dy = err * (1.0 / D_MODEL)
        dy_ref[...] = dy
        dgate_ref[...] += _colsum(dy * (fh * gn))
        dg_ref[...] += _colsum((dy * gate) * fh)
        gh = (dy * gate) * gn
        df_ref[...] = (r4 * (gh - fh * _rowmean(gh * fh))).astype(BF16)

    return pl.pallas_call(
        body, name="fwd_fc2_loss", grid=(s_len // ts,),
        in_specs=[_rows(ts, D_FF), _rows(ts, D_MODEL), _rows(ts, D_MODEL), _const((N_MOD, D_MODEL)),
                  _const((1, D_MODEL)), _VMEM],
        out_specs=[_rows(ts, D_MODEL), _rows(ts, D_MODEL), _const((1, 1)), _const((1, D_MODEL)), _const((1, D_MODEL))],
        out_shape=[jax.ShapeDtypeStruct((s_len, D_MODEL), F32), jax.ShapeDtypeStruct((s_len, D_MODEL), BF16),
                   jax.ShapeDtypeStruct((1, 1), F32), jax.ShapeDtypeStruct((1, D_MODEL), F32),
                   jax.ShapeDtypeStruct((1, D_MODEL), F32)],
        compiler_params=_params(),
    )(q, x1, target, mod6, n2post, fc2_g)


def _join_w_in_on_first_step(win_ref, full_ref):
    cs = D_Z // N_CHIPS

    @pl.when(pl.program_id(0) == 0)
    def _():
        for j in range(N_CHIPS):
            full_ref[:, j * cs:(j + 1) * cs] = win_ref[j]


def _zero_on_first_step(*refs):
    @pl.when(pl.program_id(0) == 0)
    def _():
        for ref in refs:
            ref[...] = jnp.zeros_like(ref)


def _on_last_step(fn):
    pl.when(pl.program_id(0) == pl.num_programs(0) - 1)(fn)


def _store_shard_on_last_step(acc_ref, hbm_ref, sem, j):
    _on_last_step(lambda: pltpu.make_async_copy(acc_ref.at[j], hbm_ref.at[j], sem.at[j]).start())


def _wait_stores_on_last_step(*stores):
    def wait_all():
        for acc_ref, hbm_ref, sem in stores:
            for j in range(N_CHIPS):
                pltpu.make_async_copy(acc_ref.at[j], hbm_ref.at[j], sem.at[j]).wait()

    _on_last_step(wait_all)


def _bwd_fc2(df, q, fc2_g, ts):
    s_len = df.shape[0]
    cs = D_FF // N_CHIPS

    def body(df_ref, q_ref, w_ref, dp_ref, dw_hbm, dw_ref, dw_sem):
        _zero_on_first_step(dw_ref)
        dfb = df_ref[...]
        for j in range(N_CHIPS):
            qb = q_ref[:, j * cs:(j + 1) * cs]
            dw_ref[j] += _dot_tn(qb, dfb).reshape(2, cs // 2, D_MODEL)
            _store_shard_on_last_step(dw_ref, dw_hbm, dw_sem, j)
            dq = _dot_nt(dfb, w_ref[j])
            dp_ref[:, j * cs:(j + 1) * cs] = (dq * (2.0 * jnp.sqrt(qb.astype(F32)))).astype(BF16)
        _wait_stores_on_last_step((dw_ref, dw_hbm, dw_sem))

    dw_shape = (N_CHIPS, 2, cs // 2, D_MODEL)
    return pl.pallas_call(
        body, name="bwd_fc2", grid=(s_len // ts,),
        in_specs=[_rows(ts, D_MODEL), _rows(ts, D_FF), _VMEM],
        out_specs=[_rows(ts, D_FF), _ANY],
        out_shape=[jax.ShapeDtypeStruct((s_len, D_FF), BF16), jax.ShapeDtypeStruct(dw_shape, F32)],
        scratch_shapes=[pltpu.VMEM(dw_shape, F32), pltpu.SemaphoreType.DMA((N_CHIPS,))],
        compiler_params=_params(),
    )(df, q, fc2_g)


def _bwd_fc1_out(dp, dy, x1, mix, h2, ycat, mod6, n2pre, n1post, fc1_g, wout_g, dep, ts):
    s_len = dp.shape[0]
    cs = D_FF // N_CHIPS
    rs = D_MODEL // N_CHIPS

    def body(dp_ref, dy_ref, x1_ref, mix_ref, h2_ref, yc_ref, mod_ref, g2_ref, g1_ref, w1_ref, wo_ref, dep_ref,
             dx1_ref, dyc_ref, dshift2_ref, da2_ref, dgate1_ref, dg1_ref, dw1_hbm, dwo_hbm, dw1_ref, dwo_ref, dw1_sem,
             dwo_sem):
        _zero_on_first_step(dshift2_ref, da2_ref, dgate1_ref, dg1_ref, dw1_ref, dwo_ref)
        h2b = h2_ref[...]
        dh2 = None
        for j in range(N_CHIPS):
            dpb = dp_ref[:, j * cs:(j + 1) * cs]
            dw1_ref[j] += _dot_tn(h2b, dpb).reshape(2, D_MODEL // 2, cs)
            _store_shard_on_last_step(dw1_ref, dw1_hbm, dw1_sem, j)
            part = _dot_nt(dpb, w1_ref[j])
            dh2 = part if dh2 is None else dh2 + part
        x1 = x1_ref[...]
        r3 = lax.rsqrt(_rowmean(x1 * x1) + EPS)
        xh = x1 * r3
        a2 = g2_ref[...] * (1.0 + mod_ref[4:5, :])
        dshift2_ref[...] += _colsum(dh2)
        da2_ref[...] += _colsum(dh2 * xh)
        dxh = dh2 * a2
        dx1 = dy_ref[...] + r3 * (dxh - xh * _rowmean(dxh * xh))
        dx1_ref[...] = dx1

        mix = mix_ref[...]
        r2 = lax.rsqrt(_rowmean(mix * mix) + EPS)
        mh = mix * r2
        gate = mod_ref[2:3, :]
        gn = g1_ref[...]
        dgate1_ref[...] += _colsum(dx1 * (mh * gn))
        dg1_ref[...] += _colsum((dx1 * gate) * mh)
        gh = (dx1 * gate) * gn
        dmix = (r2 * (gh - mh * _rowmean(gh * mh))).astype(BF16)
        dwo_ref[...] += _dot_tn(yc_ref[...], dmix).reshape(N_CHIPS, 2, rs // 2, D_MODEL)
        for j in range(N_CHIPS):
            _store_shard_on_last_step(dwo_ref, dwo_hbm, dwo_sem, j)
            dyc_ref[:, j * rs:(j + 1) * rs] = _dot_nt(dmix, wo_ref[j])
        _wait_stores_on_last_step((dw1_ref, dw1_hbm, dw1_sem), (dwo_ref, dwo_hbm, dwo_sem))

    vec = jax.ShapeDtypeStruct((1, D_MODEL), F32)
    dw1_shape = (N_CHIPS, 2, D_MODEL // 2, cs)
    dwo_shape = (N_CHIPS, 2, rs // 2, D_MODEL)
    return pl.pallas_call(
        body, name="bwd_fc1_out", grid=(s_len // ts,),
        in_specs=[_rows(ts, D_FF), _rows(ts, D_MODEL), _rows(ts, D_MODEL), _rows(ts, D_MODEL), _rows(ts, D_MODEL),
                  _rows(ts, D_MODEL), _const((N_MOD, D_MODEL)), _const((1, D_MODEL)), _const((1, D_MODEL)), _VMEM,
                  _VMEM, _ANY],
        out_specs=[_rows(ts, D_MODEL), _rows(ts, D_MODEL)] + [_const((1, D_MODEL))] * 4 + [_ANY, _ANY],
        out_shape=[jax.ShapeDtypeStruct((s_len, D_MODEL), F32), jax.ShapeDtypeStruct((s_len, D_MODEL), F32),
                   vec, vec, vec, vec, jax.ShapeDtypeStruct(dw1_shape, F32), jax.ShapeDtypeStruct(dwo_shape, F32)],
        scratch_shapes=[pltpu.VMEM(dw1_shape, F32), pltpu.VMEM(dwo_shape, F32), pltpu.SemaphoreType.DMA((N_CHIPS,)),
                        pltpu.SemaphoreType.DMA((N_CHIPS,))],
        compiler_params=_params(),
    )(dp, dy, x1, mix, h2, ycat, mod6, n2pre, n1post, fc1_g, wout_g, dep)


def _mixer_bwd(z, dyc, w_spatial, bsp_full, gain, bias, w_pool, b_pool, pool_scale, dep, ts):
    s_len = z.shape[0]
    nb = ts // HALO
    last = s_len // HALO - 1
    te = ts + HALO

    def body(z_ref, zprev_ref, znext_ref, dyc_ref, dynext_ref, ws_ref, bsp_ref, gain_ref, bias_ref, wp_ref, bp_ref,
             ps_ref, dep_ref, dz_ref, dws_ref, dbsp_ref, dgain_ref, dbias_ref, dwp_ref, dbp_ref, dps_ref, mixed_ref,
             dvn_ref):
        i = pl.program_id(0)

        @pl.when(i == 0)
        def _():
            for ref in (dws_ref, dbsp_ref, dgain_ref, dbias_ref, dwp_ref, dbp_ref, dps_ref):
                ref[...] = jnp.zeros_like(ref)

        wc = _tril_weights(ws_ref)
        gain = gain_ref[...]
        u, vhat, rstd, vn, dga = _mixer_forward_tile(z_ref[:, :2 * D_A], wc, bsp_ref, gain, bias_ref[...], mixed_ref)
        dya = dyc_ref[:, :D_A]
        du = dya * mixed_ref[...]
        dmixed = dya * u
        dmb = dmixed.astype(BF16)
        dm_sum = dmixed[0:CHUNK, :]
        for k in range(1, ts // CHUNK):
            dm_sum = dm_sum + dmixed[k * CHUNK:(k + 1) * CHUNK, :]
        r_idx = lax.broadcasted_iota(jnp.int32, (CHUNK, CHUNK), 0)
        s_idx = lax.broadcasted_iota(jnp.int32, (CHUNK, CHUNK), 1)
        causal = (s_idx <= r_idx).astype(F32)
        for h in range(N_HEADS):
            lanes = slice(h * HEAD_DIM, (h + 1) * HEAD_DIM)
            dbsp_ref[h] += jnp.sum(dm_sum[:, lanes], axis=1, keepdims=True)
            acc = None
            for k in range(ts // CHUNK):
                rows = slice(k * CHUNK, (k + 1) * CHUNK)
                t = _dot_nt(dmb[rows, lanes], vn[rows, lanes])
                acc = t if acc is None else acc + t
                dvn_ref[rows, lanes] = _dot_tn(wc[h], dmb[rows, lanes])
            dws_ref[h] += acc * causal
        dvn = dvn_ref[...]
        dgain_ref[...] += _colsum(dvn * vhat)
        dbias_ref[...] += _colsum(dvn)
        dvh = dvn * gain
        dv = rstd * (dvh - _rowmean(dvh) - vhat * _rowmean(dvh * vhat))
        dz_ref[:, :D_A] = (du * dga[:, :D_A]).astype(BF16)
        dz_ref[:, D_A:2 * D_A] = (dv * dga[:, D_A:]).astype(BF16)

        zb = z_ref[:, 2 * D_A:]
        prev = jnp.where(i == 0, 0.0, zprev_ref[...])
        zb_ext = jnp.concatenate([zb, znext_ref[...]], axis=0)
        sums = _causal_window_sums(jnp.concatenate([prev, zb_ext], axis=0))
        pos, inv_counts = _window_counts(i * ts, te)
        dyb_ext = jnp.concatenate([dyc_ref[:, D_A:], dynext_ref[...]], axis=0)
        dlin_ext = dyb_ext * ps_ref[...]
        dbp_ref[...] += _colsum(dlin_ext[:ts, :])
        scaled = []
        ddiffs = []
        lins = []
        for g in range(len(POOL_WINDOWS)):
            lanes = slice(g * GROUP_DIM, (g + 1) * GROUP_DIM)
            diff = (sums[g][HALO:, :] * inv_counts[g] - zb_ext[:, lanes]).astype(BF16)
            wpb = wp_ref[g].astype(BF16)
            dlb = dlin_ext[:, lanes].astype(BF16)
            lins.append(_dot(diff[:ts, :], wpb) + bp_ref[:, lanes])
            dwp_ref[g] += _dot_tn(diff[:ts, :], dlb[:ts, :])
            dd = _dot_nt(dlb, wpb)
            ddiffs.append(dd)
            scaled.append(jnp.where(pos < float(s_len), dd * inv_counts[g], 0.0))
        dps_ref[...] += _colsum(dyb_ext[:ts, :] * jnp.concatenate(lins, axis=1))
        back = _anticausal_window_sums(jnp.concatenate(scaled, axis=1))
        for g in range(len(POOL_WINDOWS)):
            dz_ref[:, 2 * D_A + g * GROUP_DIM:2 * D_A + (g + 1) * GROUP_DIM] = (
                back[g][:ts, :] - ddiffs[g][:ts, :]).astype(BF16)

    sq = jax.ShapeDtypeStruct((N_HEADS, CHUNK, CHUNK), F32)
    vec = jax.ShapeDtypeStruct((1, D_A), F32)
    return pl.pallas_call(
        body, name="mixer_bwd", grid=(s_len // ts,),
        in_specs=[_rows(ts, D_Z),
                  pl.BlockSpec((HALO, D_B), lambda i: (jnp.maximum(i * nb - 1, 0), 2)),
                  pl.BlockSpec((HALO, D_B), lambda i: (jnp.minimum((i + 1) * nb, last), 2)),
                  _rows(ts, D_MODEL),
                  pl.BlockSpec((HALO, D_B), lambda i: (jnp.minimum((i + 1) * nb, last), 1)),
                  _const((N_HEADS, CHUNK, CHUNK)), _const((CHUNK, D_A)), _const((1, D_A)), _const((1, D_A)),
                  _const((N_HEADS, GROUP_DIM, GROUP_DIM)), _const((1, D_B)), _const((1, D_B)), _ANY],
        out_specs=[_rows(ts, D_Z), _const((N_HEADS, CHUNK, CHUNK)), _const((N_HEADS, CHUNK, 1)), _const((1, D_A)),
                   _const((1, D_A)), _const((N_HEADS, GROUP_DIM, GROUP_DIM)), _const((1, D_B)), _const((1, D_B))],
        out_shape=[jax.ShapeDtypeStruct((s_len, D_Z), BF16), sq, jax.ShapeDtypeStruct((N_HEADS, CHUNK, 1), F32), vec,
                   vec, sq, vec, vec],
        scratch_shapes=[pltpu.VMEM((ts, D_A), F32), pltpu.VMEM((ts, D_A), F32)],
        compiler_params=_params(),
    )(z, z, z, dyc, dyc, w_spatial, bsp_full, gain, bias, w_pool, b_pool, pool_scale, dep)


def _bwd_in(dz, dx1, x, h1, mod6, n1pre, win_g, dep, ts):
    s_len = x.shape[0]
    cs = D_Z // N_CHIPS

    def body(dz_ref, dx1_ref, x_ref, h1_ref, mod_ref, g_ref, w_ref, dep_ref, gx_ref, dshift_ref, da_ref, dw_hbm,
             dw_ref, wfull_ref, dw_sem):
        _zero_on_first_step(dshift_ref, da_ref, dw_ref)
        _join_w_in_on_first_step(w_ref, wfull_ref)
        dzb = dz_ref[...]
        dw = _dot_tn(h1_ref[...], dzb)
        for j in range(N_CHIPS):
            dw_ref[j] += dw[:, j * cs:(j + 1) * cs].reshape(2, D_MODEL // 2, cs)
        dh = _dot_nt(dzb, wfull_ref[...])
        xv = x_ref[...]
        r = lax.rsqrt(_rowmean(xv * xv) + EPS)
        xh = xv * r
        a1 = g_ref[...] * (1.0 + mod_ref[1:2, :])
        dshift_ref[...] += _colsum(dh)
        da_ref[...] += _colsum(dh * xh)
        dxh = dh * a1
        gx_ref[...] = dx1_ref[...] + r * (dxh - xh * _rowmean(dxh * xh))
        for j in range(N_CHIPS):
            _store_shard_on_last_step(dw_ref, dw_hbm, dw_sem, j)
        _wait_stores_on_last_step((dw_ref, dw_hbm, dw_sem))

    vec = jax.ShapeDtypeStruct((1, D_MODEL), F32)
    dw_shape = (N_CHIPS, 2, D_MODEL // 2, cs)
    return pl.pallas_call(
        body, name="bwd_in", grid=(s_len // ts,),
        in_specs=[_rows(ts, D_Z), _rows(ts, D_MODEL), _rows(ts, D_MODEL), _rows(ts, D_MODEL),
                  _const((N_MOD, D_MODEL)), _const((1, D_MODEL)), _VMEM, _ANY],
        out_specs=[_rows(ts, D_MODEL), _const((1, D_MODEL)), _const((1, D_MODEL)), _ANY],
        out_shape=[jax.ShapeDtypeStruct((s_len, D_MODEL), F32), vec, vec, jax.ShapeDtypeStruct(dw_shape, F32)],
        scratch_shapes=[pltpu.VMEM(dw_shape, F32), pltpu.VMEM((D_MODEL, D_Z), BF16),
                        pltpu.SemaphoreType.DMA((N_CHIPS,))],
        compiler_params=_params(),
    )(dz, dx1, x, h1, mod6, n1pre, win_g, dep)


def _adamw_math(w, g, m, v):
    m = ADAM_B1 * m + (1.0 - ADAM_B1) * g
    v = ADAM_B2 * v + (1.0 - ADAM_B2) * (g * g)
    m_hat = m / (1.0 - ADAM_B1 ** ADAM_STEP)
    v_hat = v / (1.0 - ADAM_B2 ** ADAM_STEP)
    delta = -ADAM_LR * (m_hat / (jnp.sqrt(v_hat) + ADAM_EPS) + ADAM_WD * w)
    return delta, m, v


def _adamw(g, w, m, v, name, tr):
    rows, cols = w.shape

    def body(g_ref, w_ref, m_ref, v_ref, d_ref, nm_ref, nv_ref):
        d, nm, nv = _adamw_math(w_ref[...], g_ref[...], m_ref[...], v_ref[...])
        d_ref[...] = d
        nm_ref[...] = nm
        nv_ref[...] = nv

    spec = _rows(tr, cols)
    shape = jax.ShapeDtypeStruct((rows, cols), F32)
    return pl.pallas_call(
        body, name=name, grid=(rows // tr,), in_specs=[spec] * 4, out_specs=[spec] * 3, out_shape=[shape] * 3,
        compiler_params=_params(),
    )(g, w, m, v)


def _ada_grad_adamw(sc_t, dmod_shard, w, m, v, tr):
    rows, cols = w.shape

    def body(s_ref, dm_ref, w_ref, m_ref, v_ref, g_ref, d_ref, nm_ref, nv_ref):
        g = s_ref[:, 0:1] * dm_ref[0:1, :]
        for b in range(1, N_DEV):
            g = g + s_ref[:, b:b + 1] * dm_ref[b:b + 1, :]
        g_ref[...] = g
        d, nm, nv = _adamw_math(w_ref[...], g, m_ref[...], v_ref[...])
        d_ref[...] = d
        nm_ref[...] = nm
        nv_ref[...] = nv

    spec = _rows(tr, cols)
    shape = jax.ShapeDtypeStruct((rows, cols), F32)
    return pl.pallas_call(
        body, name="ada_grad_adamw", grid=(rows // tr,),
        in_specs=[_rows(tr, N_DEV), _const((N_DEV, cols)), spec, spec, spec],
        out_specs=[spec] * 4, out_shape=[shape] * 4, compiler_params=_params(),
    )(sc_t, dmod_shard, w, m, v)


def _mod_grads(da1, dshift1, dgate1, dg1post, da2, dshift2, dgate2, dg2post, mod6, n1pre, n2pre):
    def body(da1_ref, ds1_ref, dgt1_ref, dg1_ref, da2_ref, ds2_ref, dgt2_ref, dg2_ref, mod_ref, n1_ref, n2_ref,
             dmod_ref, dn_ref):
        dmod_ref[0:1, :] = ds1_ref[...]
        dmod_ref[1:2, :] = da1_ref[...] * n1_ref[...]
        dmod_ref[2:3, :] = dgt1_ref[...]
        dmod_ref[3:4, :] = ds2_ref[...]
        dmod_ref[4:5, :] = da2_ref[...] * n2_ref[...]
        dmod_ref[5:6, :] = dgt2_ref[...]
        dn_ref[0:1, :] = da1_ref[...] * (1.0 + mod_ref[1:2, :])
        dn_ref[1:2, :] = dg1_ref[...]
        dn_ref[2:3, :] = da2_ref[...] * (1.0 + mod_ref[4:5, :])
        dn_ref[3:4, :] = dg2_ref[...]

    return pl.pallas_call(
        body, name="mod_grads",
        out_shape=[jax.ShapeDtypeStruct((N_MOD, D_MODEL), F32), jax.ShapeDtypeStruct((4, D_MODEL), F32)],
    )(da1, dshift1, dgate1, dg1post, da2, dshift2, dgate2, dg2post, mod6, n1pre, n2pre)


def _position():
    x, y, c = lax.axis_index("x"), lax.axis_index("y"), lax.axis_index("c")
    return x, y, c


def _flip(v, bit):
    return 1 - v if bit else v


def _peer(x, y, c, k):
    return (_flip(x, k & 4), _flip(y, k & 2), _flip(c, k & 1))


def _remote(src, dst, send_sem, recv_sem, device):
    return pltpu.make_async_remote_copy(src_ref=src, dst_ref=dst, send_sem=send_sem, recv_sem=recv_sem,
                                        device_id=device, device_id_type=MESH)


def _cast_to_slot(w, pos, dep, name, tr):
    rows, cols = w.shape

    def body(pos_ref, w_ref, dep_ref, o_ref):
        o_ref[0] = w_ref[...].astype(BF16)

    return pl.pallas_call(
        body, name=name,
        grid_spec=pltpu.PrefetchScalarGridSpec(
            num_scalar_prefetch=1, grid=(rows // tr,),
            in_specs=[pl.BlockSpec((tr, cols), lambda i, pos: (i, 0)), _ANY],
            out_specs=pl.BlockSpec((1, tr, cols), lambda i, pos: (pos[1], i, 0))),
        out_shape=jax.ShapeDtypeStruct((N_CHIPS, rows, cols), BF16), compiler_params=_params(),
    )(pos, w, dep)


def _mod_exchange(c_row, w_ada_shard, b_ada_row):
    cs = w_ada_shard.shape[1]

    def body(c_ref, w_ref, b_ref, mod_ref, sc_ref, rows_ref, send1, recv1, send2, recv2):
        x, y, c = _position()
        me = 4 * x + 2 * y + c
        chip = 2 * x + y
        cv = c_ref[...]
        sc_ref[me] = cv * jax.nn.sigmoid(cv)
        gather = [_remote(sc_ref.at[me], sc_ref.at[me], send1.at[k - 1], recv1.at[k - 1], _peer(x, y, c, k))
                  for k in range(1, N_DEV)]
        for cp in gather:
            cp.start()
        for k in range(1, N_DEV):
            px, py, pc = _peer(x, y, c, k)
            src = 4 * px + 2 * py + pc
            _remote(sc_ref.at[src], sc_ref.at[src], send1.at[k - 1], recv1.at[k - 1], (px, py, pc)).wait_recv()
        for cp in gather:
            cp.wait_send()
        sc_all = jnp.concatenate([sc_ref[b] for b in range(N_DEV)], axis=0)
        part = _dot(sc_all.astype(BF16), w_ref[...].astype(BF16))
        part = part + b_ref[:, pl.ds(pl.multiple_of(chip * cs, LANES), cs)]
        for b in range(N_DEV):
            rows_ref[b] = part[b:b + 1, :]
        mod_ref[chip] = rows_ref[me]
        hand = []
        for k in (2, 4, 6):
            px, py, _ = _peer(x, y, c, k)
            hand.append(_remote(rows_ref.at[4 * px + 2 * py + c], mod_ref.at[chip], send2.at[k // 2 - 1],
                                recv2.at[k // 2 - 1], (px, py, c)))
        for cp in hand:
            cp.start()
        for k in (2, 4, 6):
            px, py, _ = _peer(x, y, c, k)
            pchip = 2 * px + py
            _remote(rows_ref.at[me], mod_ref.at[pchip], send2.at[k // 2 - 1], recv2.at[k // 2 - 1],
                    (px, py, c)).wait_recv()
        for cp in hand:
            cp.wait_send()

    return pl.pallas_call(
        body, name="mod_exchange",
        in_specs=[_VMEM, _VMEM, _VMEM], out_specs=[_VMEM, _VMEM],
        out_shape=[jax.ShapeDtypeStruct((N_CHIPS, 1, cs), F32), jax.ShapeDtypeStruct((N_DEV, 1, D_MODEL), F32)],
        scratch_shapes=[pltpu.VMEM((N_DEV, 1, cs), F32), pltpu.SemaphoreType.DMA((N_DEV - 1,)),
                        pltpu.SemaphoreType.DMA((N_DEV - 1,)), pltpu.SemaphoreType.DMA((N_CHIPS - 1,)),
                        pltpu.SemaphoreType.DMA((N_CHIPS - 1,))],
        compiler_params=pltpu.CompilerParams(vmem_limit_bytes=VMEM_LIMIT),
    )(c_row, w_ada_shard, b_ada_row)


_HBM = pl.BlockSpec(memory_space=pltpu.HBM)
_SEM = pl.BlockSpec(memory_space=pltpu.SEMAPHORE)
_EFFECT = pltpu.SideEffectType.DATAFLOW_SIDE_EFFECTING
_CHIP_HOPS = (2, 4, 6)


def _in_hbm(a):
    return pltpu.with_memory_space_constraint(a, pltpu.HBM)


def _sems3():
    return pltpu.SemaphoreType.DMA((len(_CHIP_HOPS),))


def _ag_start(lands, after, name):
    n = len(lands)

    def body(*refs):
        zones = refs[:n]
        sends, recvs = refs[n + 1:2 * n + 1], refs[2 * n + 1:3 * n + 1]
        x, y, c = _position()
        chip = 2 * x + y
        for i in range(n):
            half = zones[i].shape[1] // 2
            mine = zones[i].at[chip, pl.ds(c * half, half)]
            for s, k in enumerate(_CHIP_HOPS):
                px, py, _ = _peer(x, y, c, k)
                _remote(mine, mine, sends[i].at[s], recvs[i].at[s], (px, py, c)).start()

    out = pl.pallas_call(
        body, name=name,
        in_specs=[_HBM] * n + [_ANY],
        out_specs=[_SEM] * (2 * n) + [_HBM] * n,
        out_shape=[_sems3()] * (2 * n) + [pltpu.HBM(z.shape, BF16) for z in lands],
        input_output_aliases={i: 2 * n + i for i in range(n)},
        compiler_params=pltpu.CompilerParams(has_side_effects=_EFFECT),
    )(*[_in_hbm(z) for z in lands], after)
    return [(out[2 * n + i], out[i], out[n + i]) for i in range(n)]


def _ag_pass(group, after, name):
    n = len(group)

    def body(*refs):
        zones = refs[:n]
        sends, recvs = refs[n:2 * n], refs[2 * n:3 * n]
        fsends, frecvs = refs[4 * n + 1:5 * n + 1], refs[5 * n + 1:6 * n + 1]
        x, y, c = _position()
        chip = 2 * x + y
        for i in range(n):
            half = zones[i].shape[1] // 2
            rows = pl.ds(c * half, half)
            for s, k in enumerate(_CHIP_HOPS):
                px, py, _ = _peer(x, y, c, k)
                landed = zones[i].at[2 * px + py, rows]
                _remote(landed, landed, sends[i].at[s], recvs[i].at[s], (px, py, c)).wait_recv()
                _remote(landed, landed, fsends[i].at[s], frecvs[i].at[s], (x, y, 1 - c)).start()
        for i in range(n):
            half = zones[i].shape[1] // 2
            mine = zones[i].at[chip, pl.ds(c * half, half)]
            for s, k in enumerate(_CHIP_HOPS):
                px, py, _ = _peer(x, y, c, k)
                _remote(mine, mine, sends[i].at[s], recvs[i].at[s], (px, py, c)).wait_send()

    out = pl.pallas_call(
        body, name=name,
        in_specs=[_HBM] * n + [_SEM] * (2 * n) + [_ANY],
        out_specs=[_HBM] * n + [_SEM] * (2 * n),
        out_shape=[pltpu.HBM(g[0].shape, BF16) for g in group] + [_sems3()] * (2 * n),
        input_output_aliases={i: i for i in range(n)},
        compiler_params=pltpu.CompilerParams(has_side_effects=_EFFECT),
    )(*[g[0] for g in group], *[g[1] for g in group], *[g[2] for g in group], after)
    return [(out[i], out[n + i], out[2 * n + i]) for i in range(n)]


def _ag_done(group, name):
    n = len(group)

    def body(*refs):
        lands = refs[:n]
        fsends, frecvs = refs[n:2 * n], refs[2 * n:3 * n]
        x, y, c = _position()
        for i in range(n):
            half = lands[i].shape[1] // 2
            for s, k in enumerate(_CHIP_HOPS):
                px, py, _ = _peer(x, y, c, k)
                sent = lands[i].at[2 * px + py, pl.ds(c * half, half)]
                got = lands[i].at[2 * px + py, pl.ds((1 - c) * half, half)]
                cp = _remote(sent, got, fsends[i].at[s], frecvs[i].at[s], (x, y, 1 - c))
                cp.wait_recv()
                cp.wait_send()

    out = pl.pallas_call(
        body, name=name,
        in_specs=[_HBM] * n + [_SEM] * (2 * n),
        out_specs=[_HBM] * n,
        out_shape=[pltpu.HBM(g[0].shape, BF16) for g in group],
        input_output_aliases={i: i for i in range(n)},
        compiler_params=pltpu.CompilerParams(has_side_effects=_EFFECT),
    )(*[g[0] for g in group], *[g[1] for g in group], *[g[2] for g in group])
    return list(out)


def _small_spread_start(slots, after):
    def body(z_ref, after_ref, sends, recvs, z_out):
        x, y, c = _position()
        mine = z_ref.at[2 * x + y]
        for s, k in enumerate(_CHIP_HOPS):
            px, py, _ = _peer(x, y, c, k)
            _remote(mine, mine, sends.at[s], recvs.at[s], (px, py, c)).start()

    sends, recvs, out = pl.pallas_call(
        body, name="small_spread_start",
        in_specs=[_HBM, _ANY], out_specs=[_SEM, _SEM, _HBM],
        out_shape=[_sems3(), _sems3(), pltpu.HBM(slots.shape, F32)],
        input_output_aliases={0: 2},
        compiler_params=pltpu.CompilerParams(has_side_effects=_EFFECT),
    )(_in_hbm(slots), after)
    return out, sends, recvs


def _small_spread_wait(slots, sends, recvs, afters):
    def body(z_ref, sends, recvs, *rest):
        x, y, c = _position()
        mine = z_ref.at[2 * x + y]
        for s, k in enumerate(_CHIP_HOPS):
            px, py, _ = _peer(x, y, c, k)
            cp = _remote(mine, z_ref.at[2 * px + py], sends.at[s], recvs.at[s], (px, py, c))
            cp.wait_recv()
            cp.wait_send()

    return pl.pallas_call(
        body, name="small_spread_wait",
        in_specs=[_HBM, _SEM, _SEM] + [_ANY] * len(afters), out_specs=_HBM, out_shape=pltpu.HBM(slots.shape, F32),
        input_output_aliases={0: 0},
        compiler_params=pltpu.CompilerParams(has_side_effects=_EFFECT),
    )(slots, sends, recvs, *afters)


def _sibling_sum(pgs, name, small=None):
    n = len(pgs)
    k = 0 if small is None else 1
    units = [(i, j) for i in range(n) for j in range(N_CHIPS)]

    def body(*refs):
        refs = list(refs)
        take = lambda count: [refs.pop(0) for _ in range(count)]
        ins, small_in = take(n), take(k)
        qbs, owns, slots_out = take(n), take(n), take(k)
        mine, other, stage, got = take(n), take(n), take(n), take(n)
        load_a, load_b, send, recv = take(4)
        x, y, c = _position()
        chip = 2 * x + y
        if k:
            sib_ref, pair_send, pair_recv = take(3)
            pair = _remote(small_in[0], sib_ref, pair_send, pair_recv, (x, y, 1 - c))
            pair.start()
        loads_a = [pltpu.make_async_copy(ins[i].at[j, 1 - c], other[i].at[j], load_a.at[u])
                   for u, (i, j) in enumerate(units)]
        loads_b = [pltpu.make_async_copy(ins[i].at[j, c], mine[i].at[j], load_b.at[u])
                   for u, (i, j) in enumerate(units)]
        for cp in loads_a + loads_b:
            cp.start()
        sent = []
        for u, (i, j) in enumerate(units):
            loads_a[u].wait()
            stage[i][j] = other[i][j].astype(BF16)
            cp = _remote(stage[i].at[j], got[i].at[j], send.at[u], recv.at[u], (x, y, 1 - c))
            cp.start()
            sent.append(cp)
        for u, (i, j) in enumerate(units):
            loads_b[u].wait()
            sent[u].wait_recv()
            q = mine[i][j] + got[i][j].astype(F32)
            mine[i][j] = q
            qbs[i][j] = q.astype(BF16)
        for i in range(n):
            owns[i][...] = mine[i][chip]
        if k:
            pair.wait()
            slots_out[0][chip] = small_in[0][...] + sib_ref[...]
        for cp in sent:
            cp.wait_send()

    wire = [(N_CHIPS,) + p.shape[2:] for p in pgs]
    extra_out, extra_scratch = [], []
    if k:
        extra_out = [jax.ShapeDtypeStruct((N_CHIPS,) + small.shape, F32)]
        extra_scratch = [pltpu.VMEM(small.shape, F32), pltpu.SemaphoreType.DMA, pltpu.SemaphoreType.DMA]
    out = pl.pallas_call(
        body, name=name, in_specs=[_ANY] * n + [_VMEM] * k, out_specs=[_VMEM] * (2 * n + k),
        out_shape=[jax.ShapeDtypeStruct(w, BF16) for w in wire] + [jax.ShapeDtypeStruct(w[1:], F32) for w in wire]
        + extra_out,
        scratch_shapes=[pltpu.VMEM(w, F32) for w in wire] * 2 + [pltpu.VMEM(w, BF16) for w in wire] * 2
        + [pltpu.SemaphoreType.DMA((len(units),))] * 4 + extra_scratch,
        compiler_params=pltpu.CompilerParams(vmem_limit_bytes=VMEM_LIMIT),
    )(*pgs, *([small] if k else []))
    return list(out[:n]), list(out[n:2 * n]), list(out[2 * n:])


def _rs_start(qbs, name):
    n = len(qbs)

    def body(*refs):
        outs, inboxes = refs[:n], refs[n:2 * n]
        sends, recvs = refs[2 * n:3 * n], refs[3 * n:4 * n]
        x, y, c = _position()
        chip = 2 * x + y
        for i in range(n):
            for s, k in enumerate(_CHIP_HOPS):
                px, py, _ = _peer(x, y, c, k)
                _remote(outs[i].at[2 * px + py], inboxes[i].at[chip], sends[i].at[s], recvs[i].at[s], (px, py, c)).start()

    inboxes = [_in_hbm(lax.empty(q.shape, BF16)) for q in qbs]
    out = pl.pallas_call(
        body, name=name,
        in_specs=[_HBM] * (2 * n),
        out_specs=[_SEM] * (2 * n) + [_HBM] * (2 * n),
        out_shape=[_sems3()] * (2 * n) + [pltpu.HBM(q.shape, BF16) for q in qbs] * 2,
        input_output_aliases={i: 2 * n + i for i in range(2 * n)},
        compiler_params=pltpu.CompilerParams(has_side_effects=_EFFECT),
    )(*[_in_hbm(q) for q in qbs], *inboxes)
    return [(out[2 * n + i], out[3 * n + i], out[i], out[n + i]) for i in range(n)]


def _rs_wait(group, after, name):
    n = len(group)

    def body(*refs):
        outs, inboxes = refs[:n], refs[n:2 * n]
        sends, recvs = refs[2 * n:3 * n], refs[3 * n:4 * n]
        x, y, c = _position()
        for i in range(n):
            for s, k in enumerate(_CHIP_HOPS):
                px, py, _ = _peer(x, y, c, k)
                slot = 2 * px + py
                cp = _remote(outs[i].at[slot], inboxes[i].at[slot], sends[i].at[s], recvs[i].at[s], (px, py, c))
                cp.wait_recv()
                cp.wait_send()

    out = pl.pallas_call(
        body, name=name,
        in_specs=[_HBM] * (2 * n) + [_SEM] * (2 * n) + [_ANY],
        out_specs=[_HBM] * n,
        out_shape=[pltpu.HBM(g[1].shape, BF16) for g in group],
        input_output_aliases={n + i: i for i in range(n)},
        compiler_params=pltpu.CompilerParams(has_side_effects=_EFFECT),
    )(*[g[0] for g in group], *[g[1] for g in group], *[g[2] for g in group], *[g[3] for g in group], after)
    return list(out)


def _final_share(inboxes, owns, name):
    n = len(inboxes)
    units = [(i, s) for i in range(n) for s in range(len(_CHIP_HOPS))]

    def body(*refs):
        ins, mine, outs, landed = refs[:n], refs[n:2 * n], refs[2 * n:3 * n], refs[3 * n:4 * n]
        load, send, recv = refs[4 * n:]
        x, y, c = _position()
        loads = []
        for u, (i, s) in enumerate(units):
            px, py, _ = _peer(x, y, c, _CHIP_HOPS[s])
            loads.append(pltpu.make_async_copy(ins[i].at[2 * px + py], landed[i].at[s], load.at[u]))
        for cp in loads:
            cp.start()
        copies = []
        for i in range(n):
            for s in range(len(_CHIP_HOPS)):
                loads[len(_CHIP_HOPS) * i + s].wait()
            total = (landed[i][0].astype(F32) + landed[i][1].astype(F32)) + landed[i][2].astype(F32)
            outs[i][c] = total + mine[i][...]
            cp = _remote(outs[i].at[c], outs[i].at[c], send.at[i], recv.at[i], (x, y, 1 - c))
            cp.start()
            copies.append(cp)
        for i in range(n):
            theirs = outs[i].at[1 - c]
            _remote(theirs, theirs, send.at[i], recv.at[i], (x, y, 1 - c)).wait_recv()
        for cp in copies:
            cp.wait_send()

    return pl.pallas_call(
        body, name=name, in_specs=[_ANY] * n + [_VMEM] * n, out_specs=[_VMEM] * n,
        out_shape=[jax.ShapeDtypeStruct((2,) + o.shape, F32) for o in owns],
        scratch_shapes=[pltpu.VMEM((len(_CHIP_HOPS),) + o.shape, BF16) for o in owns]
        + [pltpu.SemaphoreType.DMA((len(units),)), pltpu.SemaphoreType.DMA((n,)), pltpu.SemaphoreType.DMA((n,))],
        compiler_params=pltpu.CompilerParams(vmem_limit_bytes=VMEM_LIMIT),
    )(*inboxes, *owns)


_SMALL = (("b_ada", N_MOD * D_MODEL), ("norm1_pre", D_MODEL), ("norm1_post", D_MODEL), ("norm2_pre", D_MODEL),
          ("norm2_post", D_MODEL), ("w_spatial", N_HEADS * CHUNK * CHUNK), ("b_spatial", N_HEADS * CHUNK),
          ("ln_v_gain", D_A), ("ln_v_bias", D_A), ("w_pool", N_HEADS * GROUP_DIM * GROUP_DIM),
          ("b_pool", D_B), ("pool_scale", D_B))
_MOD_ROWS = N_MOD * D_MODEL // LANES


def _packed_rows(size):
    return -(-(size // LANES) // SUBLANES) * SUBLANES


def _pack(parts):
    out = []
    for name, size in _SMALL:
        a = parts[name].reshape(size // LANES, LANES)
        pad = _packed_rows(size) - a.shape[0]
        out.append(jnp.pad(a, ((0, pad), (0, 0))) if pad else a)
    return out


def _small_adamw(slots, ws, ms, vs):
    n = len(_SMALL)
    head = N_DEV * _MOD_ROWS

    def body(*refs):
        s_ref, w, m, v = refs[0], refs[1:1 + n], refs[1 + n:1 + 2 * n], refs[1 + 2 * n:1 + 3 * n]
        outs = refs[1 + 3 * n:1 + 7 * n]
        dmod_ref, loss_ref, t_ref = refs[1 + 7 * n:]
        t_ref[...] = ((s_ref[0] + s_ref[1]) + s_ref[2]) + s_ref[3]
        dmod_ref[...] = t_ref[0:head, :]
        loss_ref[...] = t_ref[t_ref.shape[0] - 8:, :]
        row = head
        for i, (_, size) in enumerate(_SMALL):
            if i == 0:
                g = t_ref[0:_MOD_ROWS, :]
                for b in range(1, N_DEV):
                    g = g + t_ref[b * _MOD_ROWS:(b + 1) * _MOD_ROWS, :]
            else:
                g = t_ref[row:row + size // LANES, :]
                row += _packed_rows(size)
            d, nm, nv = _adamw_math(w[i][...], g, m[i][...], v[i][...])
            for ref, val in zip(outs[4 * i:4 * i + 4], (g, d, nm, nv)):
                ref[...] = val

    each = [jax.ShapeDtypeStruct((size // LANES, LANES), F32) for _, size in _SMALL for _ in range(4)]
    out = pl.pallas_call(
        body, name="small_adamw",
        out_shape=each + [jax.ShapeDtypeStruct((head, LANES), F32), jax.ShapeDtypeStruct((SUBLANES, LANES), F32)],
        scratch_shapes=[pltpu.VMEM(slots.shape[1:], F32)],
        compiler_params=pltpu.CompilerParams(vmem_limit_bytes=VMEM_LIMIT),
    )(slots, *ws, *ms, *vs)
    return [out[4 * i:4 * i + 4] for i in range(n)], out[4 * n], out[4 * n + 1]


def kernel(x, c, w_ada, b_ada, norm1_pre, norm1_post, w_in, w_spatial, b_spatial, ln_v_gain, ln_v_bias, w_pool, b_pool, pool_scale, w_out, norm2_pre, norm2_post, w_fc1, w_fc2, loss_target, m_w_ada, m_b_ada, m_norm1_pre, m_norm1_post, m_w_in, m_w_spatial, m_b_spatial, m_ln_v_gain, m_ln_v_bias, m_w_pool, m_b_pool, m_pool_scale, m_w_out, m_norm2_pre, m_norm2_post, m_w_fc1, m_w_fc2, v_w_ada, v_b_ada, v_norm1_pre, v_norm1_post, v_w_in, v_w_spatial, v_b_spatial, v_ln_v_gain, v_ln_v_bias, v_w_pool, v_b_pool, v_pool_scale, v_w_out, v_norm2_pre, v_norm2_post, v_w_fc1, v_w_fc2):
    weights = dict(w_ada=w_ada, b_ada=b_ada, norm1_pre=norm1_pre, norm1_post=norm1_post, w_in=w_in,
                   w_spatial=w_spatial, b_spatial=b_spatial, ln_v_gain=ln_v_gain, ln_v_bias=ln_v_bias, w_pool=w_pool,
                   b_pool=b_pool, pool_scale=pool_scale, w_out=w_out, norm2_pre=norm2_pre, norm2_post=norm2_post,
                   w_fc1=w_fc1, w_fc2=w_fc2)
    m_old = dict(w_ada=m_w_ada, b_ada=m_b_ada, norm1_pre=m_norm1_pre, norm1_post=m_norm1_post, w_in=m_w_in,
                 w_spatial=m_w_spatial, b_spatial=m_b_spatial, ln_v_gain=m_ln_v_gain, ln_v_bias=m_ln_v_bias,
                 w_pool=m_w_pool, b_pool=m_b_pool, pool_scale=m_pool_scale, w_out=m_w_out, norm2_pre=m_norm2_pre,
                 norm2_post=m_norm2_post, w_fc1=m_w_fc1, w_fc2=m_w_fc2)
    v_old = dict(w_ada=v_w_ada, b_ada=v_b_ada, norm1_pre=v_norm1_pre, norm1_post=v_norm1_post, w_in=v_w_in,
                 w_spatial=v_w_spatial, b_spatial=v_b_spatial, ln_v_gain=v_ln_v_gain, ln_v_bias=v_ln_v_bias,
                 w_pool=v_w_pool, b_pool=v_b_pool, pool_scale=v_pool_scale, w_out=v_w_out, norm2_pre=v_norm2_pre,
                 norm2_post=v_norm2_post, w_fc1=v_w_fc1, w_fc2=v_w_fc2)
    order = ("w_ada", "b_ada", "norm1_pre", "norm1_post", "w_in", "w_spatial", "b_spatial", "ln_v_gain", "ln_v_bias",
             "w_pool", "b_pool", "pool_scale", "w_out", "norm2_pre", "norm2_post", "w_fc1", "w_fc2")
    big = ("w_in", "w_out", "w_fc1", "w_fc2")
    mx, my, mc = _position()
    me = 4 * mx + 2 * my + mc
    chip = 2 * mx + my
    row = lambda a: a.reshape(1, -1)

    pos = jnp.stack([mc, chip]).astype(jnp.int32)
    xs, target = x[0], loss_target[0]
    n1pre, n1post, n2pre, n2post = row(norm1_pre), row(norm1_post), row(norm2_pre), row(norm2_post)
    mixer = (w_spatial, jnp.repeat(b_spatial.T, HEAD_DIM, axis=1), row(ln_v_gain), row(ln_v_bias), w_pool,
             row(b_pool), row(pool_scale))
    ts_big, ts_mid = 512, 256

    mod4, sc_all = _mod_exchange(c, w_ada, row(b_ada))
    mod6 = mod4.reshape(N_MOD, D_MODEL)
    ag = _ag_start([_cast_to_slot(weights[n], pos, mod4, "cast_" + n, 256) for n in big[:2]], mod4, "ag_start_mix")
    ag += _ag_start([_cast_to_slot(weights[n], pos, ag[0][0], "cast_" + n, 256) for n in big[2:]], ag[0][0],
                    "ag_start_mlp")

    win_g, wout_g = _ag_done(_ag_pass([ag[0], ag[1]], ag[2][0], "ag_pass_mix"), "ag_done_mix")
    h1, z, ycat, mix, x1, h2 = _fwd_mix(xs, mod6, n1pre, n1post, n2pre, win_g, wout_g, *mixer, ts_big)
    (fc1_g,) = _ag_done(_ag_pass([ag[2]], h2, "ag_pass_fc1"), "ag_done_fc1")
    q = _fwd_fc1(h2, fc1_g, ts_big)
    (fc2_g,) = _ag_done(_ag_pass([ag[3]], q, "ag_pass_fc2"), "ag_done_fc2")
    dy, df, loss, dgate2, dg2post = _fwd_fc2_loss(q, x1, target, mod6, n2post, fc2_g, ts_big)

    def reduce_start(partials, tag, small=None):
        wire, owns, slots = _sibling_sum(partials, "sibling_sum_" + tag, small)
        return _rs_start(wire, "rs_start_" + tag), owns, slots

    def reduce_finish(state, owns, names, tag, dep):
        inboxes = _rs_wait(state, dep, "rs_wait_" + tag)
        shards = _final_share(inboxes, owns, "final_share_" + tag)
        for n, g in zip(names, shards):
            grads[n] = g.reshape(weights[n].shape)
            deltas[n], new_m[n], new_v[n] = _adamw(grads[n], weights[n], m_old[n], v_old[n], "adamw_" + n, 256)

    grads, deltas, new_m, new_v = {}, {}, {}, {}
    dp, g_fc2 = _bwd_fc2(df, q, fc2_g, ts_big)
    state_fc2, owns_fc2, _ = reduce_start([g_fc2], "fc2")
    dx1, dyc, dshift2, da2, dgate1, dg1post, g_fc1, g_out = _bwd_fc1_out(
        dp, dy, x1, mix, h2, ycat, mod6, n2pre, n1post, fc1_g, wout_g, state_fc2[0][0], ts_mid)
    state_mid, owns_mid, _ = reduce_start([g_fc1, g_out], "mid")
    dz, dws, dbsp, dgain, dbias, dwp, dbp, dps = _mixer_bwd(z, dyc, *mixer, state_mid[0][0], ts_big)
    grad_x, dshift1, da1, g_in = _bwd_in(dz, dx1, xs, h1, mod6, n1pre, win_g, state_mid[0][0], ts_big)
    dmod6, dnorms = _mod_grads(da1, dshift1, dgate1, dg1post, da2, dshift2, dgate2, dg2post, mod6, n1pre, n2pre)

    parts = dict(b_ada=dmod6, norm1_pre=dnorms[0], norm1_post=dnorms[1], norm2_pre=dnorms[2], norm2_post=dnorms[3],
                 w_spatial=dws, b_spatial=dbsp, ln_v_gain=dgain, ln_v_bias=dbias, w_pool=dwp, b_pool=dbp,
                 pool_scale=dps)
    pieces = _pack(parts)
    slots = lax.dynamic_update_slice(jnp.zeros((N_DEV * _MOD_ROWS, LANES), F32), pieces[0], (me * _MOD_ROWS, 0))
    loss_tile = jnp.pad(loss, ((0, SUBLANES - 1), (0, LANES - 1)))
    state_in, owns_in, pair_sum = reduce_start(
        [g_in], "in", jnp.concatenate([slots] + pieces[1:] + [loss_tile], axis=0))
    spread = _small_spread_start(pair_sum[0], state_in[0][0])
    reduce_finish(state_fc2 + state_mid, owns_fc2 + owns_mid, ("w_fc2", "w_fc1", "w_out"), "mlp", spread[0])
    flat = lambda d: [d[n].reshape(size // LANES, LANES) for n, size in _SMALL]
    small_out, dmod_all, loss_tile = _small_adamw(
        _small_spread_wait(*spread, [deltas[n] for n in ("w_fc2", "w_fc1", "w_out")]), flat(weights), flat(m_old),
        flat(v_old))
    loss = loss_tile[0, 0]
    for (n, _), (g, d, nm, nv) in zip(_SMALL, small_out):
        shape = weights[n].shape
        grads[n], deltas[n], new_m[n], new_v[n] = g.reshape(shape), d.reshape(shape), nm.reshape(shape), nv.reshape(shape)

    dmod_all = dmod_all.reshape(N_DEV, N_MOD * D_MODEL)
    cs = w_ada.shape[1]
    dmod_shard = lax.dynamic_slice(dmod_all, (0, chip * cs), (N_DEV, cs))
    sc_t = sc_all.reshape(N_DEV, D_MODEL).T
    grads["w_ada"], deltas["w_ada"], new_m["w_ada"], new_v["w_ada"] = _ada_grad_adamw(
        sc_t, dmod_shard, w_ada, m_w_ada, v_w_ada, 256)

    reduce_finish(state_in, owns_in, ("w_in",), "in", deltas["w_ada"])

    return (loss, grad_x[None], *[grads[n] for n in order], *[deltas[n] for n in order],
            *[new_m[n] for n in order], *[new_v[n] for n in order])
```

```python
import functools

import jax
import jax.numpy as jnp
from jax import lax
from jax.experimental import pallas as pl
from jax.experimental.pallas import tpu as pltpu
from jax.experimental.pallas import tpu_sc as plsc

F32 = jnp.float32
BF16 = jnp.bfloat16
MESH = pl.DeviceIdType.MESH

D_MODEL = 1024
D_A = 512
D_B = 512
D_Z = 2 * D_A + D_B
N_HEADS = 4
HEAD_DIM = 128
CHUNK = 128
POOL_WINDOWS = (2, 4, 8, 16)
GROUP_DIM = 128
D_FF = 4096
N_MOD = 6
EPS = 1e-6
HALO = 16
N_CHIPS = 4
N_DEV = 8

ADAM_LR = 0.001
ADAM_B1 = 0.9
ADAM_B2 = 0.999
ADAM_EPS = 1e-08
ADAM_WD = 0.01
ADAM_STEP = 10

VMEM_LIMIT = 56 * 1024 * 1024
LANES = 128
SUBLANES = 8

_VMEM = pl.BlockSpec(memory_space=pltpu.VMEM)
_ANY = pl.BlockSpec(memory_space=pl.ANY)


def _params(n_grid_axes=1):
    return pltpu.CompilerParams(dimension_semantics=("arbitrary",) * n_grid_axes, vmem_limit_bytes=VMEM_LIMIT)


def _rows(ts, width):
    return pl.BlockSpec((ts, width), lambda i: (i, 0))


def _const(shape):
    return pl.BlockSpec(shape, lambda i: (0,) * len(shape))


def _dot(a, b):
    return jnp.dot(a, b, preferred_element_type=F32)


def _dot_nt(a, b):
    return lax.dot_general(a, b, (((1,), (1,)), ((), ())), preferred_element_type=F32)


def _dot_tn(a, b):
    return lax.dot_general(a, b, (((0,), (0,)), ((), ())), preferred_element_type=F32)


def _rowmean(v):
    return jnp.mean(v, axis=-1, keepdims=True)


def _colsum(v):
    return jnp.sum(v, axis=0, keepdims=True)


def _gelu_parts(z):
    k0 = 0.7978845608028654
    k1 = 0.044715
    z2 = z * z
    t = jnp.tanh(k0 * (z + k1 * z * z2))
    g = 0.5 * z * (1.0 + t)
    dg = 0.5 * (1.0 + t) + 0.5 * z * (1.0 - t * t) * (k0 * (1.0 + 3.0 * k1 * z2))
    return g, dg


def _tril_weights(ws_ref):
    r = lax.broadcasted_iota(jnp.int32, (CHUNK, CHUNK), 0)
    s = lax.broadcasted_iota(jnp.int32, (CHUNK, CHUNK), 1)
    mask = (s <= r).astype(F32)
    return [(ws_ref[h] * mask).astype(BF16) for h in range(N_HEADS)]


def _window_counts(first_row, n_rows):
    pos = (first_row + lax.broadcasted_iota(jnp.int32, (n_rows, 1), 0)).astype(F32)
    return pos, [1.0 / jnp.minimum(pos + 1.0, float(w)) for w in POOL_WINDOWS]


def _causal_window_sums(ext):
    out = []
    e = ext
    shift = 1
    for g in range(len(POOL_WINDOWS)):
        e = e + pltpu.roll(e, shift, 0)
        shift *= 2
        out.append(e[:, g * GROUP_DIM:(g + 1) * GROUP_DIM])
    return out


def _anticausal_window_sums(ext):
    n = ext.shape[0]
    out = []
    e = ext
    shift = 1
    for g in range(len(POOL_WINDOWS)):
        e = e + pltpu.roll(e, n - shift, 0)
        shift *= 2
        out.append(e[:, g * GROUP_DIM:(g + 1) * GROUP_DIM])
    return out


def _fwd_mix(x, mod6, n1pre, n1post, n2pre, win_g, wout_g, w_spatial, bsp_full, gain, bias, w_pool, b_pool, pool_scale, ts):
    s_len = x.shape[0]
    rs = D_MODEL // N_CHIPS

    def body(x_ref, mod_ref, g1pre_ref, g1post_ref, g2pre_ref, win_ref, wout_ref, ws_ref, bsp_ref, gain_ref,
             bias_ref, wp_ref, bp_ref, ps_ref, h1_ref, z_ref, y_ref, mix_ref, x1_ref, h2_ref, mixed_ref, prev_ref,
             wfull_ref):
        i = pl.program_id(0)
        _zero_on_first_step(prev_ref)
        _join_w_in_on_first_step(win_ref, wfull_ref)
        xv = x_ref[...]
        r = lax.rsqrt(_rowmean(xv * xv) + EPS)
        hb = (((xv * r) * g1pre_ref[...]) * (1.0 + mod_ref[1:2, :]) + mod_ref[0:1, :]).astype(BF16)
        h1_ref[...] = hb
        z_ref[...] = _dot(hb, wfull_ref[...])

        wc = _tril_weights(ws_ref)
        u, _, _, _, _ = _mixer_forward_tile(z_ref[:, :2 * D_A], wc, bsp_ref, gain_ref[...], bias_ref[...], mixed_ref)
        y_ref[:, :D_A] = (u * mixed_ref[...]).astype(BF16)
        zb = z_ref[:, 2 * D_A:]
        sums = _causal_window_sums(jnp.concatenate([prev_ref[...], zb], axis=0))
        prev_ref[...] = zb[ts - HALO:, :]
        _, inv_counts = _window_counts(i * ts, ts)
        for g in range(len(POOL_WINDOWS)):
            lanes = slice(g * GROUP_DIM, (g + 1) * GROUP_DIM)
            diff = sums[g][HALO:, :] * inv_counts[g] - zb[:, lanes]
            lin = _dot(diff.astype(BF16), wp_ref[g].astype(BF16)) + bp_ref[:, lanes]
            y_ref[:, D_A + g * GROUP_DIM:D_A + (g + 1) * GROUP_DIM] = (lin * ps_ref[:, lanes]).astype(BF16)

        mix = None
        for j in range(N_CHIPS):
            part = _dot(y_ref[:, j * rs:(j + 1) * rs], wout_ref[j])
            mix = part if mix is None else mix + part
        mix_ref[...] = mix
        r2 = lax.rsqrt(_rowmean(mix * mix) + EPS)
        x1 = xv + mod_ref[2:3, :] * ((mix * r2) * g1post_ref[...])
        x1_ref[...] = x1
        r3 = lax.rsqrt(_rowmean(x1 * x1) + EPS)
        h2_ref[...] = (((x1 * r3) * g2pre_ref[...]) * (1.0 + mod_ref[4:5, :]) + mod_ref[3:4, :]).astype(BF16)

    vec = _const((1, D_MODEL))
    f32_rows = jax.ShapeDtypeStruct((s_len, D_MODEL), F32)
    bf16_rows = jax.ShapeDtypeStruct((s_len, D_MODEL), BF16)
    return pl.pallas_call(
        body, name="fwd_mix", grid=(s_len // ts,),
        in_specs=[_rows(ts, D_MODEL), _const((N_MOD, D_MODEL)), vec, vec, vec, _VMEM, _VMEM,
                  _const((N_HEADS, CHUNK, CHUNK)), _const((CHUNK, D_A)), _const((1, D_A)), _const((1, D_A)),
                  _const((N_HEADS, GROUP_DIM, GROUP_DIM)), _const((1, D_B)), _const((1, D_B))],
        out_specs=[_rows(ts, D_MODEL), _rows(ts, D_Z), _rows(ts, D_MODEL), _rows(ts, D_MODEL), _rows(ts, D_MODEL),
                   _rows(ts, D_MODEL)],
        out_shape=[bf16_rows, jax.ShapeDtypeStruct((s_len, D_Z), F32), bf16_rows, f32_rows, f32_rows, bf16_rows],
        scratch_shapes=[pltpu.VMEM((ts, D_A), F32), pltpu.VMEM((HALO, D_B), F32), pltpu.VMEM((D_MODEL, D_Z), BF16)],
        compiler_params=_params(),
    )(x, mod6, n1pre, n1post, n2pre, win_g, wout_g, w_spatial, bsp_full, gain, bias, w_pool, b_pool, pool_scale)


def _mixer_forward_tile(za, wc, bsp_ref, gain, bias, mixed_ref):
    ga, dga = _gelu_parts(za)
    u = ga[:, :D_A]
    v = ga[:, D_A:]
    mu = _rowmean(v)
    vc = v - mu
    rstd = lax.rsqrt(_rowmean(vc * vc) + EPS)
    vhat = vc * rstd
    vn = (vhat * gain + bias).astype(BF16)
    ts = za.shape[0]
    for k in range(ts // CHUNK):
        for h in range(N_HEADS):
            blk = vn[k * CHUNK:(k + 1) * CHUNK, h * HEAD_DIM:(h + 1) * HEAD_DIM]
            mixed_ref[k * CHUNK:(k + 1) * CHUNK, h * HEAD_DIM:(h + 1) * HEAD_DIM] = (
                _dot(wc[h], blk) + bsp_ref[:, h * HEAD_DIM:(h + 1) * HEAD_DIM])
    return u, vhat, rstd, vn, dga


def _fwd_fc1(h2, fc1_g, ts):
    s_len = h2.shape[0]
    cs = D_FF // N_CHIPS

    def body(h_ref, w_ref, q_ref):
        hb = h_ref[...]
        for j in range(N_CHIPS):
            p = jnp.maximum(_dot(hb, w_ref[j]), 0.0)
            q_ref[:, j * cs:(j + 1) * cs] = (p * p).astype(BF16)

    return pl.pallas_call(
        body, name="fwd_fc1", grid=(s_len // ts,),
        in_specs=[_rows(ts, D_MODEL), _VMEM],
        out_specs=_rows(ts, D_FF),
        out_shape=jax.ShapeDtypeStruct((s_len, D_FF), BF16),
        compiler_params=_params(),
    )(h2, fc1_g)


def _fwd_fc2_loss(q, x1, target, mod6, n2post, fc2_g, ts):
    s_len = q.shape[0]
    rs = D_FF // N_CHIPS

    def body(q_ref, x1_ref, t_ref, mod_ref, g_ref, w_ref, dy_ref, df_ref, loss_ref, dgate_ref, dg_ref):
        _zero_on_first_step(loss_ref, dgate_ref, dg_ref)
        gate = mod_ref[5:6, :]
        gn = g_ref[...]
        f = _dot(q_ref[:, 0:rs], w_ref[0])
        for j in range(1, N_CHIPS):
            f = f + _dot(q_ref[:, j * rs:(j + 1) * rs], w_ref[j])
        r4 = lax.rsqrt(_rowmean(f * f) + EPS)
        fh = f * r4
        err = (x1_ref[...] + gate * (fh * gn)) - t_ref[...]
        loss_ref[...] += 0.5 * jnp.sum(_rowmean(err * err), axis=0, keepdims=True)
        dy = err * (1.0 / D_MODEL)
        dy_ref[...] = dy
        dgate_ref[...] += _colsum(dy * (fh * gn))
        dg_ref[...] += _colsum((dy * gate) * fh)
        gh = (dy * gate) * gn
        df_ref[...] = (r4 * (gh - fh * _rowmean(gh * fh))).astype(BF16)

    return pl.pallas_call(
        body, name="fwd_fc2_loss", grid=(s_len // ts,),
        in_specs=[_rows(ts, D_FF), _rows(ts, D_MODEL), _rows(ts, D_MODEL), _const((N_MOD, D_MODEL)),
                  _const((1, D_MODEL)), _VMEM],
        out_specs=[_rows(ts, D_MODEL), _rows(ts, D_MODEL), _const((1, 1)), _const((1, D_MODEL)), _const((1, D_MODEL))],
        out_shape=[jax.ShapeDtypeStruct((s_len, D_MODEL), F32), jax.ShapeDtypeStruct((s_len, D_MODEL), BF16),
                   jax.ShapeDtypeStruct((1, 1), F32), jax.ShapeDtypeStruct((1, D_MODEL), F32),
                   jax.ShapeDtypeStruct((1, D_MODEL), F32)],
        compiler_params=_params(),
    )(q, x1, target, mod6, n2post, fc2_g)


def _join_w_in_on_first_step(win_ref, full_ref):
    cs = D_Z // N_CHIPS

    @pl.when(pl.program_id(0) == 0)
    def _():
        for j in range(N_CHIPS):
            full_ref[:, j * cs:(j + 1) * cs] = win_ref[j]


def _zero_on_first_step(*refs):
    @pl.when(pl.program_id(0) == 0)
    def _():
        for ref in refs:
            ref[...] = jnp.zeros_like(ref)


def _on_last_step(fn):
    pl.when(pl.program_id(0) == pl.num_programs(0) - 1)(fn)


def _store_shard_on_last_step(acc_ref, hbm_ref, sem, j):
    _on_last_step(lambda: pltpu.make_async_copy(acc_ref.at[j], hbm_ref.at[j], sem.at[j]).start())


def _wait_stores_on_last_step(*stores):
    def wait_all():
        for acc_ref, hbm_ref, sem in stores:
            for j in range(N_CHIPS):
                pltpu.make_async_copy(acc_ref.at[j], hbm_ref.at[j], sem.at[j]).wait()

    _on_last_step(wait_all)


def _bwd_fc2(df, q, fc2_g, ts):
    s_len = df.shape[0]
    cs = D_FF // N_CHIPS

    def body(df_ref, q_ref, w_ref, dp_ref, dw_hbm, dw_ref, dw_sem):
        _zero_on_first_step(dw_ref)
        dfb = df_ref[...]
        for j in range(N_CHIPS):
            qb = q_ref[:, j * cs:(j + 1) * cs]
            dw_ref[j] += _dot_tn(qb, dfb).reshape(2, cs // 2, D_MODEL)
            _store_shard_on_last_step(dw_ref, dw_hbm, dw_sem, j)
            dq = _dot_nt(dfb, w_ref[j])
            dp_ref[:, j * cs:(j + 1) * cs] = (dq * (2.0 * jnp.sqrt(qb.astype(F32)))).astype(BF16)
        _wait_stores_on_last_step((dw_ref, dw_hbm, dw_sem))

    dw_shape = (N_CHIPS, 2, cs // 2, D_MODEL)
    return pl.pallas_call(
        body, name="bwd_fc2", grid=(s_len // ts,),
        in_specs=[_rows(ts, D_MODEL), _rows(ts, D_FF), _VMEM],
        out_specs=[_rows(ts, D_FF), _ANY],
        out_shape=[jax.ShapeDtypeStruct((s_len, D_FF), BF16), jax.ShapeDtypeStruct(dw_shape, F32)],
        scratch_shapes=[pltpu.VMEM(dw_shape, F32), pltpu.SemaphoreType.DMA((N_CHIPS,))],
        compiler_params=_params(),
    )(df, q, fc2_g)


def _bwd_fc1_out(dp, dy, x1, mix, h2, ycat, mod6, n2pre, n1post, fc1_g, wout_g, dep, ts):
    s_len = dp.shape[0]
    cs = D_FF // N_CHIPS
    rs = D_MODEL // N_CHIPS

    def body(dp_ref, dy_ref, x1_ref, mix_ref, h2_ref, yc_ref, mod_ref, g2_ref, g1_ref, w1_ref, wo_ref, dep_ref,
             dx1_ref, dyc_ref, dshift2_ref, da2_ref, dgate1_ref, dg1_ref, dw1_hbm, dwo_hbm, dw1_ref, dwo_ref, dw1_sem,
             dwo_sem):
        _zero_on_first_step(dshift2_ref, da2_ref, dgate1_ref, dg1_ref, dw1_ref, dwo_ref)
        h2b = h2_ref[...]
        dh2 = None
        for j in range(N_CHIPS):
            dpb = dp_ref[:, j * cs:(j + 1) * cs]
            dw1_ref[j] += _dot_tn(h2b, dpb).reshape(2, D_MODEL // 2, cs)
            _store_shard_on_last_step(dw1_ref, dw1_hbm, dw1_sem, j)
            part = _dot_nt(dpb, w1_ref[j])
            dh2 = part if dh2 is None else dh2 + part
        x1 = x1_ref[...]
        r3 = lax.rsqrt(_rowmean(x1 * x1) + EPS)
        xh = x1 * r3
        a2 = g2_ref[...] * (1.0 + mod_ref[4:5, :])
        dshift2_ref[...] += _colsum(dh2)
        da2_ref[...] += _colsum(dh2 * xh)
        dxh = dh2 * a2
        dx1 = dy_ref[...] + r3 * (dxh - xh * _rowmean(dxh * xh))
        dx1_ref[...] = dx1

        mix = mix_ref[...]
        r2 = lax.rsqrt(_rowmean(mix * mix) + EPS)
        mh = mix * r2
        gate = mod_ref[2:3, :]
        gn = g1_ref[...]
        dgate1_ref[...] += _colsum(dx1 * (mh * gn))
        dg1_ref[...] += _colsum((dx1 * gate) * mh)
        gh = (dx1 * gate) * gn
        dmix = (r2 * (gh - mh * _rowmean(gh * mh))).astype(BF16)
        dwo_ref[...] += _dot_tn(yc_ref[...], dmix).reshape(N_CHIPS, 2, rs // 2, D_MODEL)
        for j in range(N_CHIPS):
            _store_shard_on_last_step(dwo_ref, dwo_hbm, dwo_sem, j)
            dyc_ref[:, j * rs:(j + 1) * rs] = _dot_nt(dmix, wo_ref[j])
        _wait_stores_on_last_step((dw1_ref, dw1_hbm, dw1_sem), (dwo_ref, dwo_hbm, dwo_sem))

    vec = jax.ShapeDtypeStruct((1, D_MODEL), F32)
    dw1_shape = (N_CHIPS, 2, D_MODEL // 2, cs)
    dwo_shape = (N_CHIPS, 2, rs // 2, D_MODEL)
    return pl.pallas_call(
        body, name="bwd_fc1_out", grid=(s_len // ts,),
        in_specs=[_rows(ts, D_FF), _rows(ts, D_MODEL), _rows(ts, D_MODEL), _rows(ts, D_MODEL), _rows(ts, D_MODEL),
                  _rows(ts, D_MODEL), _const((N_MOD, D_MODEL)), _const((1, D_MODEL)), _const((1, D_MODEL)), _VMEM,
                  _VMEM, _ANY],
        out_specs=[_rows(ts, D_MODEL), _rows(ts, D_MODEL)] + [_const((1, D_MODEL))] * 4 + [_ANY, _ANY],
        out_shape=[jax.ShapeDtypeStruct((s_len, D_MODEL), F32), jax.ShapeDtypeStruct((s_len, D_MODEL), F32),
                   vec, vec, vec, vec, jax.ShapeDtypeStruct(dw1_shape, F32), jax.ShapeDtypeStruct(dwo_shape, F32)],
        scratch_shapes=[pltpu.VMEM(dw1_shape, F32), pltpu.VMEM(dwo_shape, F32), pltpu.SemaphoreType.DMA((N_CHIPS,)),
                        pltpu.SemaphoreType.DMA((N_CHIPS,))],
        compiler_params=_params(),
    )(dp, dy, x1, mix, h2, ycat, mod6, n2pre, n1post, fc1_g, wout_g, dep)


def _mixer_bwd(z, dyc, w_spatial, bsp_full, gain, bias, w_pool, b_pool, pool_scale, dep, ts):
    s_len = z.shape[0]
    nb = ts // HALO
    last = s_len // HALO - 1
    te = ts + HALO

    def body(z_ref, zprev_ref, znext_ref, dyc_ref, dynext_ref, ws_ref, bsp_ref, gain_ref, bias_ref, wp_ref, bp_ref,
             ps_ref, dep_ref, dz_ref, dws_ref, dbsp_ref, dgain_ref, dbias_ref, dwp_ref, dbp_ref, dps_ref, mixed_ref,
             dvn_ref):
        i = pl.program_id(0)

        @pl.when(i == 0)
        def _():
            for ref in (dws_ref, dbsp_ref, dgain_ref, dbias_ref, dwp_ref, dbp_ref, dps_ref):
                ref[...] = jnp.zeros_like(ref)

        wc = _tril_weights(ws_ref)
        gain = gain_ref[...]
        u, vhat, rstd, vn, dga = _mixer_forward_tile(z_ref[:, :2 * D_A], wc, bsp_ref, gain, bias_ref[...], mixed_ref)
        dya = dyc_ref[:, :D_A]
        du = dya * mixed_ref[...]
        dmixed = dya * u
        dmb = dmixed.astype(BF16)
        dm_sum = dmixed[0:CHUNK, :]
        for k in range(1, ts // CHUNK):
            dm_sum = dm_sum + dmixed[k * CHUNK:(k + 1) * CHUNK, :]
        r_idx = lax.broadcasted_iota(jnp.int32, (CHUNK, CHUNK), 0)
        s_idx = lax.broadcasted_iota(jnp.int32, (CHUNK, CHUNK), 1)
        causal = (s_idx <= r_idx).astype(F32)
        for h in range(N_HEADS):
            lanes = slice(h * HEAD_DIM, (h + 1) * HEAD_DIM)
            dbsp_ref[h] += jnp.sum(dm_sum[:, lanes], axis=1, keepdims=True)
            acc = None
            for k in range(ts // CHUNK):
                rows = slice(k * CHUNK, (k + 1) * CHUNK)
                t = _dot_nt(dmb[rows, lanes], vn[rows, lanes])
                acc = t if acc is None else acc + t
                dvn_ref[rows, lanes] = _dot_tn(wc[h], dmb[rows, lanes])
            dws_ref[h] += acc * causal
        dvn = dvn_ref[...]
        dgain_ref[...] += _colsum(dvn * vhat)
        dbias_ref[...] += _colsum(dvn)
        dvh = dvn * gain
        dv = rstd * (dvh - _rowmean(dvh) - vhat * _rowmean(dvh * vhat))
        dz_ref[:, :D_A] = (du * dga[:, :D_A]).astype(BF16)
        dz_ref[:, D_A:2 * D_A] = (dv * dga[:, D_A:]).astype(BF16)

        zb = z_ref[:, 2 * D_A:]
        prev = jnp.where(i == 0, 0.0, zprev_ref[...])
        zb_ext = jnp.concatenate([zb, znext_ref[...]], axis=0)
        sums = _causal_window_sums(jnp.concatenate([prev, zb_ext], axis=0))
        pos, inv_counts = _window_counts(i * ts, te)
        dyb_ext = jnp.concatenate([dyc_ref[:, D_A:], dynext_ref[...]], axis=0)
        dlin_ext = dyb_ext * ps_ref[...]
        dbp_ref[...] += _colsum(dlin_ext[:ts, :])
        scaled = []
        ddiffs = []
        lins = []
        for g in range(len(POOL_WINDOWS)):
            lanes = slice(g * GROUP_DIM, (g + 1) * GROUP_DIM)
            diff = (sums[g][HALO:, :] * inv_counts[g] - zb_ext[:, lanes]).astype(BF16)
            wpb = wp_ref[g].astype(BF16)
            dlb = dlin_ext[:, lanes].astype(BF16)
            lins.append(_dot(diff[:ts, :], wpb) + bp_ref[:, lanes])
            dwp_ref[g] += _dot_tn(diff[:ts, :], dlb[:ts, :])
            dd = _dot_nt(dlb, wpb)
            ddiffs.append(dd)
            scaled.append(jnp.where(pos < float(s_len), dd * inv_counts[g], 0.0))
        dps_ref[...] += _colsum(dyb_ext[:ts, :] * jnp.concatenate(lins, axis=1))
        back = _anticausal_window_sums(jnp.concatenate(scaled, axis=1))
        for g in range(len(POOL_WINDOWS)):
            dz_ref[:, 2 * D_A + g * GROUP_DIM:2 * D_A + (g + 1) * GROUP_DIM] = (
                back[g][:ts, :] - ddiffs[g][:ts, :]).astype(BF16)

    sq = jax.ShapeDtypeStruct((N_HEADS, CHUNK, CHUNK), F32)
    vec = jax.ShapeDtypeStruct((1, D_A), F32)
    return pl.pallas_call(
        body, name="mixer_bwd", grid=(s_len // ts,),
        in_specs=[_rows(ts, D_Z),
                  pl.BlockSpec((HALO, D_B), lambda i: (jnp.maximum(i * nb - 1, 0), 2)),
                  pl.BlockSpec((HALO, D_B), lambda i: (jnp.minimum((i + 1) * nb, last), 2)),
                  _rows(ts, D_MODEL),
                  pl.BlockSpec((HALO, D_B), lambda i: (jnp.minimum((i + 1) * nb, last), 1)),
                  _const((N_HEADS, CHUNK, CHUNK)), _const((CHUNK, D_A)), _const((1, D_A)), _const((1, D_A)),
                  _const((N_HEADS, GROUP_DIM, GROUP_DIM)), _const((1, D_B)), _const((1, D_B)), _ANY],
        out_specs=[_rows(ts, D_Z), _const((N_HEADS, CHUNK, CHUNK)), _const((N_HEADS, CHUNK, 1)), _const((1, D_A)),
                   _const((1, D_A)), _const((N_HEADS, GROUP_DIM, GROUP_DIM)), _const((1, D_B)), _const((1, D_B))],
        out_shape=[jax.ShapeDtypeStruct((s_len, D_Z), BF16), sq, jax.ShapeDtypeStruct((N_HEADS, CHUNK, 1), F32), vec,
                   vec, sq, vec, vec],
        scratch_shapes=[pltpu.VMEM((ts, D_A), F32), pltpu.VMEM((ts, D_A), F32)],
        compiler_params=_params(),
    )(z, z, z, dyc, dyc, w_spatial, bsp_full, gain, bias, w_pool, b_pool, pool_scale, dep)


def _bwd_in(dz, dx1, x, h1, mod6, n1pre, win_g, dep, ts):
    s_len = x.shape[0]
    cs = D_Z // N_CHIPS

    def body(dz_ref, dx1_ref, x_ref, h1_ref, mod_ref, g_ref, w_ref, dep_ref, gx_ref, dshift_ref, da_ref, dw_hbm,
             dw_ref, wfull_ref, dw_sem):
        _zero_on_first_step(dshift_ref, da_ref, dw_ref)
        _join_w_in_on_first_step(w_ref, wfull_ref)
        dzb = dz_ref[...]
        dw = _dot_tn(h1_ref[...], dzb)
        for j in range(N_CHIPS):
            dw_ref[j] += dw[:, j * cs:(j + 1) * cs].reshape(2, D_MODEL // 2, cs)
        dh = _dot_nt(dzb, wfull_ref[...])
        xv = x_ref[...]
        r = lax.rsqrt(_rowmean(xv * xv) + EPS)
        xh = xv * r
        a1 = g_ref[...] * (1.0 + mod_ref[1:2, :])
        dshift_ref[...] += _colsum(dh)
        da_ref[...] += _colsum(dh * xh)
        dxh = dh * a1
        gx_ref[...] = dx1_ref[...] + r * (dxh - xh * _rowmean(dxh * xh))
        for j in range(N_CHIPS):
            _store_shard_on_last_step(dw_ref, dw_hbm, dw_sem, j)
        _wait_stores_on_last_step((dw_ref, dw_hbm, dw_sem))

    vec = jax.ShapeDtypeStruct((1, D_MODEL), F32)
    dw_shape = (N_CHIPS, 2, D_MODEL // 2, cs)
    return pl.pallas_call(
        body, name="bwd_in", grid=(s_len // ts,),
        in_specs=[_rows(ts, D_Z), _rows(ts, D_MODEL), _rows(ts, D_MODEL), _rows(ts, D_MODEL),
                  _const((N_MOD, D_MODEL)), _const((1, D_MODEL)), _VMEM, _ANY],
        out_specs=[_rows(ts, D_MODEL), _const((1, D_MODEL)), _const((1, D_MODEL)), _ANY],
        out_shape=[jax.ShapeDtypeStruct((s_len, D_MODEL), F32), vec, vec, jax.ShapeDtypeStruct(dw_shape, F32)],
        scratch_shapes=[pltpu.VMEM(dw_shape, F32), pltpu.VMEM((D_MODEL, D_Z), BF16),
                        pltpu.SemaphoreType.DMA((N_CHIPS,))],
        compiler_params=_params(),
    )(dz, dx1, x, h1, mod6, n1pre, win_g, dep)


def _adamw_math(w, g, m, v):
    m = ADAM_B1 * m + (1.0 - ADAM_B1) * g
    v = ADAM_B2 * v + (1.0 - ADAM_B2) * (g * g)
    m_hat = m / (1.0 - ADAM_B1 ** ADAM_STEP)
    v_hat = v / (1.0 - ADAM_B2 ** ADAM_STEP)
    delta = -ADAM_LR * (m_hat / (jnp.sqrt(v_hat) + ADAM_EPS) + ADAM_WD * w)
    return delta, m, v


def _adamw(g, w, m, v, name, tr):
    rows, cols = w.shape

    def body(g_ref, w_ref, m_ref, v_ref, d_ref, nm_ref, nv_ref):
        d, nm, nv = _adamw_math(w_ref[...], g_ref[...], m_ref[...], v_ref[...])
        d_ref[...] = d
        nm_ref[...] = nm
        nv_ref[...] = nv

    spec = _rows(tr, cols)
    shape = jax.ShapeDtypeStruct((rows, cols), F32)
    return pl.pallas_call(
        body, name=name, grid=(rows // tr,), in_specs=[spec] * 4, out_specs=[spec] * 3, out_shape=[shape] * 3,
        compiler_params=_params(),
    )(g, w, m, v)


SC_TILES = 32
SC_LANES = 16
SC_BLOCK_ROWS = 8


def _adamw_on_sparsecore(g, w, m, v, name):
    rows, cols = w.shape
    per_tile = rows // SC_TILES
    block = min(SC_BLOCK_ROWS, per_tile)

    def body(g_hbm, w_hbm, m_hbm, v_hbm, d_hbm, nm_hbm, nv_hbm, g_buf, w_buf, m_buf, v_buf, d_buf):
        tile = lax.axis_index("subcore") * 2 + lax.axis_index("core")

        @pl.loop(0, per_tile, step=block)
        def _(r0):
            mine = pl.ds(tile * per_tile + r0, block)
            for hbm, buf in ((g_hbm, g_buf), (w_hbm, w_buf), (m_hbm, m_buf), (v_hbm, v_buf)):
                pltpu.sync_copy(hbm.at[mine], buf)

            @pl.loop(0, block)
            def _(r):
                @pl.loop(0, cols, step=SC_LANES)
                def _(i):
                    lanes = pl.ds(i, SC_LANES)
                    d, nm, nv = _adamw_math(w_buf[r, lanes], g_buf[r, lanes], m_buf[r, lanes], v_buf[r, lanes])
                    d_buf[r, lanes] = d
                    m_buf[r, lanes] = nm
                    v_buf[r, lanes] = nv

            for buf, hbm in ((d_buf, d_hbm), (m_buf, nm_hbm), (v_buf, nv_hbm)):
                pltpu.sync_copy(buf, hbm.at[mine])

    shape = jax.ShapeDtypeStruct((rows, cols), F32)
    return pl.kernel(
        body, name=name, out_type=[shape] * 3,
        mesh=plsc.VectorSubcoreMesh(core_axis_name="core", subcore_axis_name="subcore"),
        scratch_types=[pltpu.VMEM((block, cols), F32)] * 5,
    )(g, w, m, v)


def _ada_grad_adamw(sc_t, dmod_shard, w, m, v, tr):
    rows, cols = w.shape

    def body(s_ref, dm_ref, w_ref, m_ref, v_ref, g_ref, d_ref, nm_ref, nv_ref):
        g = s_ref[:, 0:1] * dm_ref[0:1, :]
        for b in range(1, N_DEV):
            g = g + s_ref[:, b:b + 1] * dm_ref[b:b + 1, :]
        g_ref[...] = g
        d, nm, nv = _adamw_math(w_ref[...], g, m_ref[...], v_ref[...])
        d_ref[...] = d
        nm_ref[...] = nm
        nv_ref[...] = nv

    spec = _rows(tr, cols)
    shape = jax.ShapeDtypeStruct((rows, cols), F32)
    return pl.pallas_call(
        body, name="ada_grad_adamw", grid=(rows // tr,),
        in_specs=[_rows(tr, N_DEV), _const((N_DEV, cols)), spec, spec, spec],
        out_specs=[spec] * 4, out_shape=[shape] * 4, compiler_params=_params(),
    )(sc_t, dmod_shard, w, m, v)


def _mod_grads(da1, dshift1, dgate1, dg1post, da2, dshift2, dgate2, dg2post, mod6, n1pre, n2pre):
    def body(da1_ref, ds1_ref, dgt1_ref, dg1_ref, da2_ref, ds2_ref, dgt2_ref, dg2_ref, mod_ref, n1_ref, n2_ref,
             dmod_ref, dn_ref):
        dmod_ref[0:1, :] = ds1_ref[...]
        dmod_ref[1:2, :] = da1_ref[...] * n1_ref[...]
        dmod_ref[2:3, :] = dgt1_ref[...]
        dmod_ref[3:4, :] = ds2_ref[...]
        dmod_ref[4:5, :] = da2_ref[...] * n2_ref[...]
        dmod_ref[5:6, :] = dgt2_ref[...]
        dn_ref[0:1, :] = da1_ref[...] * (1.0 + mod_ref[1:2, :])
        dn_ref[1:2, :] = dg1_ref[...]
        dn_ref[2:3, :] = da2_ref[...] * (1.0 + mod_ref[4:5, :])
        dn_ref[3:4, :] = dg2_ref[...]

    return pl.pallas_call(
        body, name="mod_grads",
        out_shape=[jax.ShapeDtypeStruct((N_MOD, D_MODEL), F32), jax.ShapeDtypeStruct((4, D_MODEL), F32)],
    )(da1, dshift1, dgate1, dg1post, da2, dshift2, dgate2, dg2post, mod6, n1pre, n2pre)


def _position():
    x, y, c = lax.axis_index("x"), lax.axis_index("y"), lax.axis_index("c")
    return x, y, c


def _flip(v, bit):
    return 1 - v if bit else v


def _peer(x, y, c, k):
    return (_flip(x, k & 4), _flip(y, k & 2), _flip(c, k & 1))


def _remote(src, dst, send_sem, recv_sem, device):
    return pltpu.make_async_remote_copy(src_ref=src, dst_ref=dst, send_sem=send_sem, recv_sem=recv_sem,
                                        device_id=device, device_id_type=MESH)


def _cast_to_slot(w, pos, dep, name, tr):
    rows, cols = w.shape

    def body(pos_ref, w_ref, dep_ref, o_ref):
        o_ref[0] = w_ref[...].astype(BF16)

    return pl.pallas_call(
        body, name=name,
        grid_spec=pltpu.PrefetchScalarGridSpec(
            num_scalar_prefetch=1, grid=(rows // tr,),
            in_specs=[pl.BlockSpec((tr, cols), lambda i, pos: (i, 0)), _ANY],
            out_specs=pl.BlockSpec((1, tr, cols), lambda i, pos: (pos[1], i, 0))),
        out_shape=jax.ShapeDtypeStruct((N_CHIPS, rows, cols), BF16), compiler_params=_params(),
    )(pos, w, dep)


def _mod_exchange(c_row, w_ada_shard, b_ada_row):
    cs = w_ada_shard.shape[1]

    def body(c_ref, w_ref, b_ref, mod_ref, sc_ref, rows_ref, send1, recv1, send2, recv2):
        x, y, c = _position()
        me = 4 * x + 2 * y + c
        chip = 2 * x + y
        cv = c_ref[...]
        sc_ref[me] = cv * jax.nn.sigmoid(cv)
        gather = [_remote(sc_ref.at[me], sc_ref.at[me], send1.at[k - 1], recv1.at[k - 1], _peer(x, y, c, k))
                  for k in range(1, N_DEV)]
        for cp in gather:
            cp.start()
        for k in range(1, N_DEV):
            px, py, pc = _peer(x, y, c, k)
            src = 4 * px + 2 * py + pc
            _remote(sc_ref.at[src], sc_ref.at[src], send1.at[k - 1], recv1.at[k - 1], (px, py, pc)).wait_recv()
        for cp in gather:
            cp.wait_send()
        sc_all = jnp.concatenate([sc_ref[b] for b in range(N_DEV)], axis=0)
        part = _dot(sc_all.astype(BF16), w_ref[...].astype(BF16))
        part = part + b_ref[:, pl.ds(pl.multiple_of(chip * cs, LANES), cs)]
        for b in range(N_DEV):
            rows_ref[b] = part[b:b + 1, :]
        mod_ref[chip] = rows_ref[me]
        hand = []
        for k in (2, 4, 6):
            px, py, _ = _peer(x, y, c, k)
            hand.append(_remote(rows_ref.at[4 * px + 2 * py + c], mod_ref.at[chip], send2.at[k // 2 - 1],
                                recv2.at[k // 2 - 1], (px, py, c)))
        for cp in hand:
            cp.start()
        for k in (2, 4, 6):
            px, py, _ = _peer(x, y, c, k)
            pchip = 2 * px + py
            _remote(rows_ref.at[me], mod_ref.at[pchip], send2.at[k // 2 - 1], recv2.at[k // 2 - 1],
                    (px, py, c)).wait_recv()
        for cp in hand:
            cp.wait_send()

    return pl.pallas_call(
        body, name="mod_exchange",
        in_specs=[_VMEM, _VMEM, _VMEM], out_specs=[_VMEM, _VMEM],
        out_shape=[jax.ShapeDtypeStruct((N_CHIPS, 1, cs), F32), jax.ShapeDtypeStruct((N_DEV, 1, D_MODEL), F32)],
        scratch_shapes=[pltpu.VMEM((N_DEV, 1, cs), F32), pltpu.SemaphoreType.DMA((N_DEV - 1,)),
                        pltpu.SemaphoreType.DMA((N_DEV - 1,)), pltpu.SemaphoreType.DMA((N_CHIPS - 1,)),
                        pltpu.SemaphoreType.DMA((N_CHIPS - 1,))],
        compiler_params=pltpu.CompilerParams(vmem_limit_bytes=VMEM_LIMIT),
    )(c_row, w_ada_shard, b_ada_row)


_HBM = pl.BlockSpec(memory_space=pltpu.HBM)
_SEM = pl.BlockSpec(memory_space=pltpu.SEMAPHORE)
_EFFECT = pltpu.SideEffectType.DATAFLOW_SIDE_EFFECTING
_CHIP_HOPS = (2, 4, 6)


def _in_hbm(a):
    return pltpu.with_memory_space_constraint(a, pltpu.HBM)


def _sems3():
    return pltpu.SemaphoreType.DMA((len(_CHIP_HOPS),))


def _ag_start(lands, after, name):
    n = len(lands)

    def body(*refs):
        zones = refs[:n]
        sends, recvs = refs[n + 1:2 * n + 1], refs[2 * n + 1:3 * n + 1]
        x, y, c = _position()
        chip = 2 * x + y
        for i in range(n):
            half = zones[i].shape[1] // 2
            mine = zones[i].at[chip, pl.ds(c * half, half)]
            for s, k in enumerate(_CHIP_HOPS):
                px, py, _ = _peer(x, y, c, k)
                _remote(mine, mine, sends[i].at[s], recvs[i].at[s], (px, py, c)).start()

    out = pl.pallas_call(
        body, name=name,
        in_specs=[_HBM] * n + [_ANY],
        out_specs=[_SEM] * (2 * n) + [_HBM] * n,
        out_shape=[_sems3()] * (2 * n) + [pltpu.HBM(z.shape, BF16) for z in lands],
        input_output_aliases={i: 2 * n + i for i in range(n)},
        compiler_params=pltpu.CompilerParams(has_side_effects=_EFFECT),
    )(*[_in_hbm(z) for z in lands], after)
    return [(out[2 * n + i], out[i], out[n + i]) for i in range(n)]


def _ag_pass(group, after, name):
    n = len(group)

    def body(*refs):
        zones = refs[:n]
        sends, recvs = refs[n:2 * n], refs[2 * n:3 * n]
        fsends, frecvs = refs[4 * n + 1:5 * n + 1], refs[5 * n + 1:6 * n + 1]
        x, y, c = _position()
        chip = 2 * x + y
        for i in range(n):
            half = zones[i].shape[1] // 2
            rows = pl.ds(c * half, half)
            for s, k in enumerate(_CHIP_HOPS):
                px, py, _ = _peer(x, y, c, k)
                landed = zones[i].at[2 * px + py, rows]
                _remote(landed, landed, sends[i].at[s], recvs[i].at[s], (px, py, c)).wait_recv()
                _remote(landed, landed, fsends[i].at[s], frecvs[i].at[s], (x, y, 1 - c)).start()
        for i in range(n):
            half = zones[i].shape[1] // 2
            mine = zones[i].at[chip, pl.ds(c * half, half)]
            for s, k in enumerate(_CHIP_HOPS):
                px, py, _ = _peer(x, y, c, k)
                _remote(mine, mine, sends[i].at[s], recvs[i].at[s], (px, py, c)).wait_send()

    out = pl.pallas_call(
        body, name=name,
        in_specs=[_HBM] * n + [_SEM] * (2 * n) + [_ANY],
        out_specs=[_HBM] * n + [_SEM] * (2 * n),
        out_shape=[pltpu.HBM(g[0].shape, BF16) for g in group] + [_sems3()] * (2 * n),
        input_output_aliases={i: i for i in range(n)},
        compiler_params=pltpu.CompilerParams(has_side_effects=_EFFECT),
    )(*[g[0] for g in group], *[g[1] for g in group], *[g[2] for g in group], after)
    return [(out[i], out[n + i], out[2 * n + i]) for i in range(n)]


def _ag_done(group, name):
    n = len(group)

    def body(*refs):
        lands = refs[:n]
        fsends, frecvs = refs[n:2 * n], refs[2 * n:3 * n]
        x, y, c = _position()
        for i in range(n):
            half = lands[i].shape[1] // 2
            for s, k in enumerate(_CHIP_HOPS):
                px, py, _ = _peer(x, y, c, k)
                sent = lands[i].at[2 * px + py, pl.ds(c * half, half)]
                got = lands[i].at[2 * px + py, pl.ds((1 - c) * half, half)]
                cp = _remote(sent, got, fsends[i].at[s], frecvs[i].at[s], (x, y, 1 - c))
                cp.wait_recv()
                cp.wait_send()

    out = pl.pallas_call(
        body, name=name,
        in_specs=[_HBM] * n + [_SEM] * (2 * n),
        out_specs=[_HBM] * n,
        out_shape=[pltpu.HBM(g[0].shape, BF16) for g in group],
        input_output_aliases={i: i for i in range(n)},
        compiler_params=pltpu.CompilerParams(has_side_effects=_EFFECT),
    )(*[g[0] for g in group], *[g[1] for g in group], *[g[2] for g in group])
    return list(out)


def _small_spread_start(slots, after):
    def body(z_ref, after_ref, sends, recvs, z_out):
        x, y, c = _position()
        mine = z_ref.at[2 * x + y]
        for s, k in enumerate(_CHIP_HOPS):
            px, py, _ = _peer(x, y, c, k)
            _remote(mine, mine, sends.at[s], recvs.at[s], (px, py, c)).start()

    sends, recvs, out = pl.pallas_call(
        body, name="small_spread_start",
        in_specs=[_HBM, _ANY], out_specs=[_SEM, _SEM, _HBM],
        out_shape=[_sems3(), _sems3(), pltpu.HBM(slots.shape, F32)],
        input_output_aliases={0: 2},
        compiler_params=pltpu.CompilerParams(has_side_effects=_EFFECT),
    )(_in_hbm(slots), after)
    return out, sends, recvs


def _small_spread_wait(slots, sends, recvs, afters):
    def body(z_ref, sends, recvs, *rest):
        x, y, c = _position()
        mine = z_ref.at[2 * x + y]
        for s, k in enumerate(_CHIP_HOPS):
            px, py, _ = _peer(x, y, c, k)
            cp = _remote(mine, z_ref.at[2 * px + py], sends.at[s], recvs.at[s], (px, py, c))
            cp.wait_recv()
            cp.wait_send()

    return pl.pallas_call(
        body, name="small_spread_wait",
        in_specs=[_HBM, _SEM, _SEM] + [_ANY] * len(afters), out_specs=_HBM, out_shape=pltpu.HBM(slots.shape, F32),
        input_output_aliases={0: 0},
        compiler_params=pltpu.CompilerParams(has_side_effects=_EFFECT),
    )(slots, sends, recvs, *afters)


def _sibling_sum(pgs, name, small=None):
    n = len(pgs)
    k = 0 if small is None else 1
    units = [(i, j) for i in range(n) for j in range(N_CHIPS)]

    def body(*refs):
        refs = list(refs)
        take = lambda count: [refs.pop(0) for _ in range(count)]
        ins, small_in = take(n), take(k)
        qbs, owns, slots_out = take(n), take(n), take(k)
        mine, other, stage, got = take(n), take(n), take(n), take(n)
        load_a, load_b, send, recv = take(4)
        x, y, c = _position()
        chip = 2 * x + y
        if k:
            sib_ref, pair_send, pair_recv = take(3)
            pair = _remote(small_in[0], sib_ref, pair_send, pair_recv, (x, y, 1 - c))
            pair.start()
        loads_a = [pltpu.make_async_copy(ins[i].at[j, 1 - c], other[i].at[j], load_a.at[u])
                   for u, (i, j) in enumerate(units)]
        loads_b = [pltpu.make_async_copy(ins[i].at[j, c], mine[i].at[j], load_b.at[u])
                   for u, (i, j) in enumerate(units)]
        for cp in loads_a + loads_b:
            cp.start()
        sent = []
        for u, (i, j) in enumerate(units):
            loads_a[u].wait()
            stage[i][j] = other[i][j].astype(BF16)
            cp = _remote(stage[i].at[j], got[i].at[j], send.at[u], recv.at[u], (x, y, 1 - c))
            cp.start()
            sent.append(cp)
        for u, (i, j) in enumerate(units):
            loads_b[u].wait()
            sent[u].wait_recv()
            q = mine[i][j] + got[i][j].astype(F32)
            mine[i][j] = q
            qbs[i][j] = q.astype(BF16)
        for i in range(n):
            owns[i][...] = mine[i][chip]
        if k:
            pair.wait()
            slots_out[0][chip] = small_in[0][...] + sib_ref[...]
        for cp in sent:
            cp.wait_send()

    wire = [(N_CHIPS,) + p.shape[2:] for p in pgs]
    extra_out, extra_scratch = [], []
    if k:
        extra_out = [jax.ShapeDtypeStruct((N_CHIPS,) + small.shape, F32)]
        extra_scratch = [pltpu.VMEM(small.shape, F32), pltpu.SemaphoreType.DMA, pltpu.SemaphoreType.DMA]
    out = pl.pallas_call(
        body, name=name, in_specs=[_ANY] * n + [_VMEM] * k, out_specs=[_VMEM] * (2 * n + k),
        out_shape=[jax.ShapeDtypeStruct(w, BF16) for w in wire] + [jax.ShapeDtypeStruct(w[1:], F32) for w in wire]
        + extra_out,
        scratch_shapes=[pltpu.VMEM(w, F32) for w in wire] * 2 + [pltpu.VMEM(w, BF16) for w in wire] * 2
        + [pltpu.SemaphoreType.DMA((len(units),))] * 4 + extra_scratch,
        compiler_params=pltpu.CompilerParams(vmem_limit_bytes=VMEM_LIMIT),
    )(*pgs, *([small] if k else []))
    return list(out[:n]), list(out[n:2 * n]), list(out[2 * n:])


def _rs_start(qbs, name):
    n = len(qbs)

    def body(*refs):
        outs, inboxes = refs[:n], refs[n:2 * n]
        sends, recvs = refs[2 * n:3 * n], refs[3 * n:4 * n]
        x, y, c = _position()
        chip = 2 * x + y
        for i in range(n):
            for s, k in enumerate(_CHIP_HOPS):
                px, py, _ = _peer(x, y, c, k)
                _remote(outs[i].at[2 * px + py], inboxes[i].at[chip], sends[i].at[s], recvs[i].at[s], (px, py, c)).start()

    inboxes = [_in_hbm(lax.empty(q.shape, BF16)) for q in qbs]
    out = pl.pallas_call(
        body, name=name,
        in_specs=[_HBM] * (2 * n),
        out_specs=[_SEM] * (2 * n) + [_HBM] * (2 * n),
        out_shape=[_sems3()] * (2 * n) + [pltpu.HBM(q.shape, BF16) for q in qbs] * 2,
        input_output_aliases={i: 2 * n + i for i in range(2 * n)},
        compiler_params=pltpu.CompilerParams(has_side_effects=_EFFECT),
    )(*[_in_hbm(q) for q in qbs], *inboxes)
    return [(out[2 * n + i], out[3 * n + i], out[i], out[n + i]) for i in range(n)]


def _rs_wait(group, after, name):
    n = len(group)

    def body(*refs):
        outs, inboxes = refs[:n], refs[n:2 * n]
        sends, recvs = refs[2 * n:3 * n], refs[3 * n:4 * n]
        x, y, c = _position()
        for i in range(n):
            for s, k in enumerate(_CHIP_HOPS):
                px, py, _ = _peer(x, y, c, k)
                slot = 2 * px + py
                cp = _remote(outs[i].at[slot], inboxes[i].at[slot], sends[i].at[s], recvs[i].at[s], (px, py, c))
                cp.wait_recv()
                cp.wait_send()

    out = pl.pallas_call(
        body, name=name,
        in_specs=[_HBM] * (2 * n) + [_SEM] * (2 * n) + [_ANY],
        out_specs=[_HBM] * n,
        out_shape=[pltpu.HBM(g[1].shape, BF16) for g in group],
        input_output_aliases={n + i: i for i in range(n)},
        compiler_params=pltpu.CompilerParams(has_side_effects=_EFFECT),
    )(*[g[0] for g in group], *[g[1] for g in group], *[g[2] for g in group], *[g[3] for g in group], after)
    return list(out)


def _final_share(inboxes, owns, name):
    n = len(inboxes)
    units = [(i, s) for i in range(n) for s in range(len(_CHIP_HOPS))]

    def body(*refs):
        ins, mine, outs, landed = refs[:n], refs[n:2 * n], refs[2 * n:3 * n], refs[3 * n:4 * n]
        load, send, recv = refs[4 * n:]
        x, y, c = _position()
        loads = []
        for u, (i, s) in enumerate(units):
            px, py, _ = _peer(x, y, c, _CHIP_HOPS[s])
            loads.append(pltpu.make_async_copy(ins[i].at[2 * px + py], landed[i].at[s], load.at[u]))
        for cp in loads:
            cp.start()
        copies = []
        for i in range(n):
            for s in range(len(_CHIP_HOPS)):
                loads[len(_CHIP_HOPS) * i + s].wait()
            total = (landed[i][0].astype(F32) + landed[i][1].astype(F32)) + landed[i][2].astype(F32)
            outs[i][c] = total + mine[i][...]
            cp = _remote(outs[i].at[c], outs[i].at[c], send.at[i], recv.at[i], (x, y, 1 - c))
            cp.start()
            copies.append(cp)
        for i in range(n):
            theirs = outs[i].at[1 - c]
            _remote(theirs, theirs, send.at[i], recv.at[i], (x, y, 1 - c)).wait_recv()
        for cp in copies:
            cp.wait_send()

    return pl.pallas_call(
        body, name=name, in_specs=[_ANY] * n + [_VMEM] * n, out_specs=[_VMEM] * n,
        out_shape=[jax.ShapeDtypeStruct((2,) + o.shape, F32) for o in owns],
        scratch_shapes=[pltpu.VMEM((len(_CHIP_HOPS),) + o.shape, BF16) for o in owns]
        + [pltpu.SemaphoreType.DMA((len(units),)), pltpu.SemaphoreType.DMA((n,)), pltpu.SemaphoreType.DMA((n,))],
        compiler_params=pltpu.CompilerParams(vmem_limit_bytes=VMEM_LIMIT),
    )(*inboxes, *owns)


_SMALL = (("b_ada", N_MOD * D_MODEL), ("norm1_pre", D_MODEL), ("norm1_post", D_MODEL), ("norm2_pre", D_MODEL),
          ("norm2_post", D_MODEL), ("w_spatial", N_HEADS * CHUNK * CHUNK), ("b_spatial", N_HEADS * CHUNK),
          ("ln_v_gain", D_A), ("ln_v_bias", D_A), ("w_pool", N_HEADS * GROUP_DIM * GROUP_DIM),
          ("b_pool", D_B), ("pool_scale", D_B))
_MOD_ROWS = N_MOD * D_MODEL // LANES


def _packed_rows(size):
    return -(-(size // LANES) // SUBLANES) * SUBLANES


def _pack(parts):
    out = []
    for name, size in _SMALL:
        a = parts[name].reshape(size // LANES, LANES)
        pad = _packed_rows(size) - a.shape[0]
        out.append(jnp.pad(a, ((0, pad), (0, 0))) if pad else a)
    return out


def _small_adamw(slots, ws, ms, vs):
    n = len(_SMALL)
    head = N_DEV * _MOD_ROWS

    def body(*refs):
        s_ref, w, m, v = refs[0], refs[1:1 + n], refs[1 + n:1 + 2 * n], refs[1 + 2 * n:1 + 3 * n]
        outs = refs[1 + 3 * n:1 + 7 * n]
        dmod_ref, loss_ref, t_ref = refs[1 + 7 * n:]
        t_ref[...] = ((s_ref[0] + s_ref[1]) + s_ref[2]) + s_ref[3]
        dmod_ref[...] = t_ref[0:head, :]
        loss_ref[...] = t_ref[t_ref.shape[0] - 8:, :]
        row = head
        for i, (_, size) in enumerate(_SMALL):
            if i == 0:
                g = t_ref[0:_MOD_ROWS, :]
                for b in range(1, N_DEV):
                    g = g + t_ref[b * _MOD_ROWS:(b + 1) * _MOD_ROWS, :]
            else:
                g = t_ref[row:row + size // LANES, :]
                row += _packed_rows(size)
            d, nm, nv = _adamw_math(w[i][...], g, m[i][...], v[i][...])
            for ref, val in zip(outs[4 * i:4 * i + 4], (g, d, nm, nv)):
                ref[...] = val

    each = [jax.ShapeDtypeStruct((size // LANES, LANES), F32) for _, size in _SMALL for _ in range(4)]
    out = pl.pallas_call(
        body, name="small_adamw",
        out_shape=each + [jax.ShapeDtypeStruct((head, LANES), F32), jax.ShapeDtypeStruct((SUBLANES, LANES), F32)],
        scratch_shapes=[pltpu.VMEM(slots.shape[1:], F32)],
        compiler_params=pltpu.CompilerParams(vmem_limit_bytes=VMEM_LIMIT),
    )(slots, *ws, *ms, *vs)
    return [out[4 * i:4 * i + 4] for i in range(n)], out[4 * n], out[4 * n + 1]


def kernel(x, c, w_ada, b_ada, norm1_pre, norm1_post, w_in, w_spatial, b_spatial, ln_v_gain, ln_v_bias, w_pool, b_pool, pool_scale, w_out, norm2_pre, norm2_post, w_fc1, w_fc2, loss_target, m_w_ada, m_b_ada, m_norm1_pre, m_norm1_post, m_w_in, m_w_spatial, m_b_spatial, m_ln_v_gain, m_ln_v_bias, m_w_pool, m_b_pool, m_pool_scale, m_w_out, m_norm2_pre, m_norm2_post, m_w_fc1, m_w_fc2, v_w_ada, v_b_ada, v_norm1_pre, v_norm1_post, v_w_in, v_w_spatial, v_b_spatial, v_ln_v_gain, v_ln_v_bias, v_w_pool, v_b_pool, v_pool_scale, v_w_out, v_norm2_pre, v_norm2_post, v_w_fc1, v_w_fc2):
    weights = dict(w_ada=w_ada, b_ada=b_ada, norm1_pre=norm1_pre, norm1_post=norm1_post, w_in=w_in,
                   w_spatial=w_spatial, b_spatial=b_spatial, ln_v_gain=ln_v_gain, ln_v_bias=ln_v_bias, w_pool=w_pool,
                   b_pool=b_pool, pool_scale=pool_scale, w_out=w_out, norm2_pre=norm2_pre, norm2_post=norm2_post,
                   w_fc1=w_fc1, w_fc2=w_fc2)
    m_old = dict(w_ada=m_w_ada, b_ada=m_b_ada, norm1_pre=m_norm1_pre, norm1_post=m_norm1_post, w_in=m_w_in,
                 w_spatial=m_w_spatial, b_spatial=m_b_spatial, ln_v_gain=m_ln_v_gain, ln_v_bias=m_ln_v_bias,
                 w_pool=m_w_pool, b_pool=m_b_pool, pool_scale=m_pool_scale, w_out=m_w_out, norm2_pre=m_norm2_pre,
                 norm2_post=m_norm2_post, w_fc1=m_w_fc1, w_fc2=m_w_fc2)
    v_old = dict(w_ada=v_w_ada, b_ada=v_b_ada, norm1_pre=v_norm1_pre, norm1_post=v_norm1_post, w_in=v_w_in,
                 w_spatial=v_w_spatial, b_spatial=v_b_spatial, ln_v_gain=v_ln_v_gain, ln_v_bias=v_ln_v_bias,
                 w_pool=v_w_pool, b_pool=v_b_pool, pool_scale=v_pool_scale, w_out=v_w_out, norm2_pre=v_norm2_pre,
                 norm2_post=v_norm2_post, w_fc1=v_w_fc1, w_fc2=v_w_fc2)
    order = ("w_ada", "b_ada", "norm1_pre", "norm1_post", "w_in", "w_spatial", "b_spatial", "ln_v_gain", "ln_v_bias",
             "w_pool", "b_pool", "pool_scale", "w_out", "norm2_pre", "norm2_post", "w_fc1", "w_fc2")
    big = ("w_in", "w_out", "w_fc1", "w_fc2")
    mx, my, mc = _position()
    me = 4 * mx + 2 * my + mc
    chip = 2 * mx + my
    row = lambda a: a.reshape(1, -1)

    pos = jnp.stack([mc, chip]).astype(jnp.int32)
    xs, target = x[0], loss_target[0]
    n1pre, n1post, n2pre, n2post = row(norm1_pre), row(norm1_post), row(norm2_pre), row(norm2_post)
    mixer = (w_spatial, jnp.repeat(b_spatial.T, HEAD_DIM, axis=1), row(ln_v_gain), row(ln_v_bias), w_pool,
             row(b_pool), row(pool_scale))
    ts_big, ts_mid = 512, 256

    mod4, sc_all = _mod_exchange(c, w_ada, row(b_ada))
    mod6 = mod4.reshape(N_MOD, D_MODEL)
    ag = _ag_start([_cast_to_slot(weights[n], pos, mod4, "cast_" + n, 256) for n in big[:2]], mod4, "ag_start_mix")
    ag += _ag_start([_cast_to_slot(weights[n], pos, ag[0][0], "cast_" + n, 256) for n in big[2:]], ag[0][0],
                    "ag_start_mlp")

    win_g, wout_g = _ag_done(_ag_pass([ag[0], ag[1]], ag[2][0], "ag_pass_mix"), "ag_done_mix")
    h1, z, ycat, mix, x1, h2 = _fwd_mix(xs, mod6, n1pre, n1post, n2pre, win_g, wout_g, *mixer, ts_big)
    (fc1_g,) = _ag_done(_ag_pass([ag[2]], h2, "ag_pass_fc1"), "ag_done_fc1")
    q = _fwd_fc1(h2, fc1_g, ts_big)
    (fc2_g,) = _ag_done(_ag_pass([ag[3]], q, "ag_pass_fc2"), "ag_done_fc2")
    dy, df, loss, dgate2, dg2post = _fwd_fc2_loss(q, x1, target, mod6, n2post, fc2_g, ts_big)

    def reduce_start(partials, tag, small=None):
        wire, owns, slots = _sibling_sum(partials, "sibling_sum_" + tag, small)
        return _rs_start(wire, "rs_start_" + tag), owns, slots

    def reduce_finish(state, owns, names, tag, dep):
        inboxes = _rs_wait(state, dep, "rs_wait_" + tag)
        shards = _final_share(inboxes, owns, "final_share_" + tag)
        for n, g in zip(names, shards):
            grads[n] = g.reshape(weights[n].shape)
            if n in ("w_fc2", "w_out"):
                deltas[n], new_m[n], new_v[n] = _adamw_on_sparsecore(grads[n], weights[n], m_old[n], v_old[n],
                                                                     "adamw_sc_" + n)
            else:
                deltas[n], new_m[n], new_v[n] = _adamw(grads[n], weights[n], m_old[n], v_old[n], "adamw_" + n, 256)

    grads, deltas, new_m, new_v = {}, {}, {}, {}
    dp, g_fc2 = _bwd_fc2(df, q, fc2_g, ts_big)
    state_fc2, owns_fc2, _ = reduce_start([g_fc2], "fc2")
    dx1, dyc, dshift2, da2, dgate1, dg1post, g_fc1, g_out = _bwd_fc1_out(
        dp, dy, x1, mix, h2, ycat, mod6, n2pre, n1post, fc1_g, wout_g, state_fc2[0][0], ts_mid)
    state_mid, owns_mid, _ = reduce_start([g_fc1, g_out], "mid")
    dz, dws, dbsp, dgain, dbias, dwp, dbp, dps = _mixer_bwd(z, dyc, *mixer, state_mid[0][0], ts_big)
    grad_x, dshift1, da1, g_in = _bwd_in(dz, dx1, xs, h1, mod6, n1pre, win_g, state_mid[0][0], ts_big)
    dmod6, dnorms = _mod_grads(da1, dshift1, dgate1, dg1post, da2, dshift2, dgate2, dg2post, mod6, n1pre, n2pre)

    parts = dict(b_ada=dmod6, norm1_pre=dnorms[0], norm1_post=dnorms[1], norm2_pre=dnorms[2], norm2_post=dnorms[3],
                 w_spatial=dws, b_spatial=dbsp, ln_v_gain=dgain, ln_v_bias=dbias, w_pool=dwp, b_pool=dbp,
                 pool_scale=dps)
    pieces = _pack(parts)
    slots = lax.dynamic_update_slice(jnp.zeros((N_DEV * _MOD_ROWS, LANES), F32), pieces[0], (me * _MOD_ROWS, 0))
    loss_tile = jnp.pad(loss, ((0, SUBLANES - 1), (0, LANES - 1)))
    state_in, owns_in, pair_sum = reduce_start(
        [g_in], "in", jnp.concatenate([slots] + pieces[1:] + [loss_tile], axis=0))
    spread = _small_spread_start(pair_sum[0], state_in[0][0])
    reduce_finish(state_fc2 + state_mid, owns_fc2 + owns_mid, ("w_fc2", "w_fc1", "w_out"), "mlp", spread[0])
    flat = lambda d: [d[n].reshape(size // LANES, LANES) for n, size in _SMALL]
    small_out, dmod_all, loss_tile = _small_adamw(
        _small_spread_wait(*spread, [deltas["w_fc1"]]), flat(weights), flat(m_old), flat(v_old))
    loss = loss_tile[0, 0]
    for (n, _), (g, d, nm, nv) in zip(_SMALL, small_out):
        shape = weights[n].shape
        grads[n], deltas[n], new_m[n], new_v[n] = g.reshape(shape), d.reshape(shape), nm.reshape(shape), nv.reshape(shape)

    dmod_all = dmod_all.reshape(N_DEV, N_MOD * D_MODEL)
    cs = w_ada.shape[1]
    dmod_shard = lax.dynamic_slice(dmod_all, (0, chip * cs), (N_DEV, cs))
    sc_t = sc_all.reshape(N_DEV, D_MODEL).T
    grads["w_ada"], deltas["w_ada"], new_m["w_ada"], new_v["w_ada"] = _ada_grad_adamw(
        sc_t, dmod_shard, w_ada, m_w_ada, v_w_ada, 256)

    reduce_finish(state_in, owns_in, ("w_in",), "in", deltas["w_ada"])

    return (loss, grad_x[None], *[grads[n] for n in order], *[deltas[n] for n in order],
            *[new_m[n] for n in order], *[new_v[n] for n in order])
```

```python
import functools

import jax
import jax.numpy as jnp
from jax import lax
from jax.experimental import pallas as pl
from jax.experimental.pallas import tpu as pltpu

F32 = jnp.float32
BF16 = jnp.bfloat16
MESH = pl.DeviceIdType.MESH

D_MODEL = 1024
D_A = 512
D_B = 512
D_Z = 2 * D_A + D_B
N_HEADS = 4
HEAD_DIM = 128
CHUNK = 128
POOL_WINDOWS = (2, 4, 8, 16)
GROUP_DIM = 128
D_FF = 4096
N_MOD = 6
EPS = 1e-6
HALO = 16
N_CHIPS = 4
N_DEV = 8

ADAM_LR = 0.001
ADAM_B1 = 0.9
ADAM_B2 = 0.999
ADAM_EPS = 1e-08
ADAM_WD = 0.01
ADAM_STEP = 10

VMEM_LIMIT = 56 * 1024 * 1024
LANES = 128
SUBLANES = 8

_VMEM = pl.BlockSpec(memory_space=pltpu.VMEM)
_ANY = pl.BlockSpec(memory_space=pl.ANY)


def _params(n_grid_axes=1):
    return pltpu.CompilerParams(dimension_semantics=("arbitrary",) * n_grid_axes, vmem_limit_bytes=VMEM_LIMIT)


def _rows(ts, width):
    return pl.BlockSpec((ts, width), lambda i: (i, 0))


def _const(shape):
    return pl.BlockSpec(shape, lambda i: (0,) * len(shape))


def _dot(a, b):
    return jnp.dot(a, b, preferred_element_type=F32)


def _dot_nt(a, b):
    return lax.dot_general(a, b, (((1,), (1,)), ((), ())), preferred_element_type=F32)


def _dot_tn(a, b):
    return lax.dot_general(a, b, (((0,), (0,)), ((), ())), preferred_element_type=F32)


def _rowmean(v):
    return jnp.mean(v, axis=-1, keepdims=True)


def _colsum(v):
    return jnp.sum(v, axis=0, keepdims=True)


def _gelu_parts(z):
    k0 = 0.7978845608028654
    k1 = 0.044715
    z2 = z * z
    t = jnp.tanh(k0 * (z + k1 * z * z2))
    g = 0.5 * z * (1.0 + t)
    dg = 0.5 * (1.0 + t) + 0.5 * z * (1.0 - t * t) * (k0 * (1.0 + 3.0 * k1 * z2))
    return g, dg


def _tril_weights(ws_ref):
    r = lax.broadcasted_iota(jnp.int32, (CHUNK, CHUNK), 0)
    s = lax.broadcasted_iota(jnp.int32, (CHUNK, CHUNK), 1)
    mask = (s <= r).astype(F32)
    return [(ws_ref[h] * mask).astype(BF16) for h in range(N_HEADS)]


def _window_counts(first_row, n_rows):
    pos = (first_row + lax.broadcasted_iota(jnp.int32, (n_rows, 1), 0)).astype(F32)
    return pos, [1.0 / jnp.minimum(pos + 1.0, float(w)) for w in POOL_WINDOWS]


def _causal_window_sums(ext):
    out = []
    e = ext
    shift = 1
    for g in range(len(POOL_WINDOWS)):
        e = e + pltpu.roll(e, shift, 0)
        shift *= 2
        out.append(e[:, g * GROUP_DIM:(g + 1) * GROUP_DIM])
    return out


def _anticausal_window_sums(ext):
    n = ext.shape[0]
    out = []
    e = ext
    shift = 1
    for g in range(len(POOL_WINDOWS)):
        e = e + pltpu.roll(e, n - shift, 0)
        shift *= 2
        out.append(e[:, g * GROUP_DIM:(g + 1) * GROUP_DIM])
    return out


def _fwd_mix(x, mod6, n1pre, n1post, n2pre, win_g, wout_g, w_spatial, bsp_full, gain, bias, w_pool, b_pool, pool_scale, ts):
    s_len = x.shape[0]
    rs = D_MODEL // N_CHIPS

    def body(x_ref, mod_ref, g1pre_ref, g1post_ref, g2pre_ref, win_ref, wout_ref, ws_ref, bsp_ref, gain_ref,
             bias_ref, wp_ref, bp_ref, ps_ref, z_ref, y_ref, mix_ref, x1_ref, h2_ref, mixed_ref, prev_ref, wfull_ref):
        i = pl.program_id(0)
        _zero_on_first_step(prev_ref)
        _join_w_in_on_first_step(win_ref, wfull_ref)
        xv = x_ref[...]
        r = lax.rsqrt(_rowmean(xv * xv) + EPS)
        hb = (((xv * r) * g1pre_ref[...]) * (1.0 + mod_ref[1:2, :]) + mod_ref[0:1, :]).astype(BF16)
        z_ref[...] = _dot(hb, wfull_ref[...])

        wc = _tril_weights(ws_ref)
        u, _, _, _, _ = _mixer_forward_tile(z_ref[:, :2 * D_A], wc, bsp_ref, gain_ref[...], bias_ref[...], mixed_ref)
        y_ref[:, :D_A] = (u * mixed_ref[...]).astype(BF16)
        zb = z_ref[:, 2 * D_A:]
        sums = _causal_window_sums(jnp.concatenate([prev_ref[...], zb], axis=0))
        prev_ref[...] = zb[ts - HALO:, :]
        _, inv_counts = _window_counts(i * ts, ts)
        for g in range(len(POOL_WINDOWS)):
            lanes = slice(g * GROUP_DIM, (g + 1) * GROUP_DIM)
            diff = sums[g][HALO:, :] * inv_counts[g] - zb[:, lanes]
            lin = _dot(diff.astype(BF16), wp_ref[g].astype(BF16)) + bp_ref[:, lanes]
            y_ref[:, D_A + g * GROUP_DIM:D_A + (g + 1) * GROUP_DIM] = (lin * ps_ref[:, lanes]).astype(BF16)

        mix = None
        for j in range(N_CHIPS):
            part = _dot(y_ref[:, j * rs:(j + 1) * rs], wout_ref[j])
            mix = part if mix is None else mix + part
        mix_ref[...] = mix
        r2 = lax.rsqrt(_rowmean(mix * mix) + EPS)
        x1 = xv + mod_ref[2:3, :] * ((mix * r2) * g1post_ref[...])
        x1_ref[...] = x1
        r3 = lax.rsqrt(_rowmean(x1 * x1) + EPS)
        h2_ref[...] = (((x1 * r3) * g2pre_ref[...]) * (1.0 + mod_ref[4:5, :]) + mod_ref[3:4, :]).astype(BF16)

    vec = _const((1, D_MODEL))
    f32_rows = jax.ShapeDtypeStruct((s_len, D_MODEL), F32)
    bf16_rows = jax.ShapeDtypeStruct((s_len, D_MODEL), BF16)
    return pl.pallas_call(
        body, name="fwd_mix", grid=(s_len // ts,),
        in_specs=[_rows(ts, D_MODEL), _const((N_MOD, D_MODEL)), vec, vec, vec, _VMEM, _VMEM,
                  _const((N_HEADS, CHUNK, CHUNK)), _const((CHUNK, D_A)), _const((1, D_A)), _const((1, D_A)),
                  _const((N_HEADS, GROUP_DIM, GROUP_DIM)), _const((1, D_B)), _const((1, D_B))],
        out_specs=[_rows(ts, D_Z), _rows(ts, D_MODEL), _rows(ts, D_MODEL), _rows(ts, D_MODEL), _rows(ts, D_MODEL)],
        out_shape=[jax.ShapeDtypeStruct((s_len, D_Z), F32), bf16_rows, f32_rows, f32_rows, bf16_rows],
        scratch_shapes=[pltpu.VMEM((ts, D_A), F32), pltpu.VMEM((HALO, D_B), F32), pltpu.VMEM((D_MODEL, D_Z), BF16)],
        compiler_params=_params(),
    )(x, mod6, n1pre, n1post, n2pre, win_g, wout_g, w_spatial, bsp_full, gain, bias, w_pool, b_pool, pool_scale)


def _mixer_forward_tile(za, wc, bsp_ref, gain, bias, mixed_ref):
    ga, dga = _gelu_parts(za)
    u = ga[:, :D_A]
    v = ga[:, D_A:]
    mu = _rowmean(v)
    vc = v - mu
    rstd = lax.rsqrt(_rowmean(vc * vc) + EPS)
    vhat = vc * rstd
    vn = (vhat * gain + bias).astype(BF16)
    ts = za.shape[0]
    for k in range(ts // CHUNK):
        for h in range(N_HEADS):
            blk = vn[k * CHUNK:(k + 1) * CHUNK, h * HEAD_DIM:(h + 1) * HEAD_DIM]
            mixed_ref[k * CHUNK:(k + 1) * CHUNK, h * HEAD_DIM:(h + 1) * HEAD_DIM] = (
                _dot(wc[h], blk) + bsp_ref[:, h * HEAD_DIM:(h + 1) * HEAD_DIM])
    return u, vhat, rstd, vn, dga


def _fwd_fc1(h2, fc1_g, ts):
    s_len = h2.shape[0]
    cs = D_FF // N_CHIPS

    def body(h_ref, w_ref, q_ref):
        hb = h_ref[...]
        for j in range(N_CHIPS):
            p = jnp.maximum(_dot(hb, w_ref[j]), 0.0)
            q_ref[:, j * cs:(j + 1) * cs] = (p * p).astype(BF16)

    return pl.pallas_call(
        body, name="fwd_fc1", grid=(s_len // ts,),
        in_specs=[_rows(ts, D_MODEL), _VMEM],
        out_specs=_rows(ts, D_FF),
        out_shape=jax.ShapeDtypeStruct((s_len, D_FF), BF16),
        compiler_params=_params(),
    )(h2, fc1_g)


def _fwd_fc2_loss(q, x1, target, mod6, n2post, fc2_g, ts):
    s_len = q.shape[0]
    rs = D_FF // N_CHIPS

    def body(q_ref, x1_ref, t_ref, mod_ref, g_ref, w_ref, dy_ref, df_ref, loss_ref, dgate_ref, dg_ref):
        _zero_on_first_step(loss_ref, dgate_ref, dg_ref)
        gate = mod_ref[5:6, :]
        gn = g_ref[...]
        f = _dot(q_ref[:, 0:rs], w_ref[0])
        for j in range(1, N_CHIPS):
            f = f + _dot(q_ref[:, j * rs:(j + 1) * rs], w_ref[j])
        r4 = lax.rsqrt(_rowmean(f * f) + EPS)
        fh = f * r4
        err = (x1_ref[...] + gate * (fh * gn)) - t_ref[...]
        loss_ref[...] += 0.5 * jnp.sum(_rowmean(err * err), axis=0, keepdims=True)
        dy = err * (1.0 / D_MODEL)
        dy_ref[...] = dy
        dgate_ref[...] += _colsum(dy * (fh * gn))
        dg_ref[...] += _colsum((dy * gate) * fh)
        gh = (dy * gate) * gn
        df_ref[...] = (r4 * (gh - fh * _rowmean(gh * fh))).astype(BF16)

    return pl.pallas_call(
        body, name="fwd_fc2_loss", grid=(s_len // ts,),
        in_specs=[_rows(ts, D_FF), _rows(ts, D_MODEL), _rows(ts, D_MODEL), _const((N_MOD, D_MODEL)),
                  _const((1, D_MODEL)), _VMEM],
        out_specs=[_rows(ts, D_MODEL), _rows(ts, D_MODEL), _const((1, 1)), _const((1, D_MODEL)), _const((1, D_MODEL))],
        out_shape=[jax.ShapeDtypeStruct((s_len, D_MODEL), F32), jax.ShapeDtypeStruct((s_len, D_MODEL), BF16),
                   jax.ShapeDtypeStruct((1, 1), F32), jax.ShapeDtypeStruct((1, D_MODEL), F32),
                   jax.ShapeDtypeStruct((1, D_MODEL), F32)],
        compiler_params=_params(),
    )(q, x1, target, mod6, n2post, fc2_g)


def _join_w_in_on_first_step(win_ref, full_ref):
    cs = D_Z // N_CHIPS

    @pl.when(pl.program_id(0) == 0)
    def _():
        for j in range(N_CHIPS):
            full_ref[:, j * cs:(j + 1) * cs] = win_ref[j]


def _zero_on_first_step(*refs):
    @pl.when(pl.program_id(0) == 0)
    def _():
        for ref in refs:
            ref[...] = jnp.zeros_like(ref)


def _on_last_step(fn):
    pl.when(pl.program_id(0) == pl.num_programs(0) - 1)(fn)


def _store_shard_on_last_step(acc_ref, hbm_ref, sem, j):
    _on_last_step(lambda: pltpu.make_async_copy(acc_ref.at[j], hbm_ref.at[j], sem.at[j]).start())


def _wait_stores_on_last_step(*stores):
    def wait_all():
        for acc_ref, hbm_ref, sem in stores:
            for j in range(N_CHIPS):
                pltpu.make_async_copy(acc_ref.at[j], hbm_ref.at[j], sem.at[j]).wait()

    _on_last_step(wait_all)


def _bwd_fc2(df, q, fc2_g, ts):
    s_len = df.shape[0]
    cs = D_FF // N_CHIPS

    def body(df_ref, q_ref, w_ref, dp_ref, dw_hbm, dw_ref, dw_sem):
        _zero_on_first_step(dw_ref)
        dfb = df_ref[...]
        for j in range(N_CHIPS):
            qb = q_ref[:, j * cs:(j + 1) * cs]
            dw_ref[j] += _dot_tn(qb, dfb).reshape(2, cs // 2, D_MODEL)
            _store_shard_on_last_step(dw_ref, dw_hbm, dw_sem, j)
            dq = _dot_nt(dfb, w_ref[j])
            dp_ref[:, j * cs:(j + 1) * cs] = (dq * (2.0 * jnp.sqrt(qb.astype(F32)))).astype(BF16)
        _wait_stores_on_last_step((dw_ref, dw_hbm, dw_sem))

    dw_shape = (N_CHIPS, 2, cs // 2, D_MODEL)
    return pl.pallas_call(
        body, name="bwd_fc2", grid=(s_len // ts,),
        in_specs=[_rows(ts, D_MODEL), _rows(ts, D_FF), _VMEM],
        out_specs=[_rows(ts, D_FF), _ANY],
        out_shape=[jax.ShapeDtypeStruct((s_len, D_FF), BF16), jax.ShapeDtypeStruct(dw_shape, F32)],
        scratch_shapes=[pltpu.VMEM(dw_shape, F32), pltpu.SemaphoreType.DMA((N_CHIPS,))],
        compiler_params=_params(),
    )(df, q, fc2_g)


def _bwd_fc1_out(dp, dy, x1, mix, h2, ycat, mod6, n2pre, n1post, fc1_g, wout_g, dep, ts):
    s_len = dp.shape[0]
    cs = D_FF // N_CHIPS
    rs = D_MODEL // N_CHIPS

    def body(dp_ref, dy_ref, x1_ref, mix_ref, h2_ref, yc_ref, mod_ref, g2_ref, g1_ref, w1_ref, wo_ref, dep_ref,
             dx1_ref, dyc_ref, dshift2_ref, da2_ref, dgate1_ref, dg1_ref, dw1_hbm, dwo_hbm, dw1_ref, dwo_ref, dw1_sem,
             dwo_sem):
        _zero_on_first_step(dshift2_ref, da2_ref, dgate1_ref, dg1_ref, dw1_ref, dwo_ref)
        h2b = h2_ref[...]
        dh2 = None
        for j in range(N_CHIPS):
            dpb = dp_ref[:, j * cs:(j + 1) * cs]
            dw1_ref[j] += _dot_tn(h2b, dpb).reshape(2, D_MODEL // 2, cs)
            _store_shard_on_last_step(dw1_ref, dw1_hbm, dw1_sem, j)
            part = _dot_nt(dpb, w1_ref[j])
            dh2 = part if dh2 is None else dh2 + part
        x1 = x1_ref[...]
        r3 = lax.rsqrt(_rowmean(x1 * x1) + EPS)
        xh = x1 * r3
        a2 = g2_ref[...] * (1.0 + mod_ref[4:5, :])
        dshift2_ref[...] += _colsum(dh2)
        da2_ref[...] += _colsum(dh2 * xh)
        dxh = dh2 * a2
        dx1 = dy_ref[...] + r3 * (dxh - xh * _rowmean(dxh * xh))
        dx1_ref[...] = dx1

        mix = mix_ref[...]
        r2 = lax.rsqrt(_rowmean(mix * mix) + EPS)
        mh = mix * r2
        gate = mod_ref[2:3, :]
        gn = g1_ref[...]
        dgate1_ref[...] += _colsum(dx1 * (mh * gn))
        dg1_ref[...] += _colsum((dx1 * gate) * mh)
        gh = (dx1 * gate) * gn
        dmix = (r2 * (gh - mh * _rowmean(gh * mh))).astype(BF16)
        dwo_ref[...] += _dot_tn(yc_ref[...], dmix).reshape(N_CHIPS, 2, rs // 2, D_MODEL)
        for j in range(N_CHIPS):
            _store_shard_on_last_step(dwo_ref, dwo_hbm, dwo_sem, j)
            dyc_ref[:, j * rs:(j + 1) * rs] = _dot_nt(dmix, wo_ref[j])
        _wait_stores_on_last_step((dw1_ref, dw1_hbm, dw1_sem), (dwo_ref, dwo_hbm, dwo_sem))

    vec = jax.ShapeDtypeStruct((1, D_MODEL), F32)
    dw1_shape = (N_CHIPS, 2, D_MODEL // 2, cs)
    dwo_shape = (N_CHIPS, 2, rs // 2, D_MODEL)
    return pl.pallas_call(
        body, name="bwd_fc1_out", grid=(s_len // ts,),
        in_specs=[_rows(ts, D_FF), _rows(ts, D_MODEL), _rows(ts, D_MODEL), _rows(ts, D_MODEL), _rows(ts, D_MODEL),
                  _rows(ts, D_MODEL), _const((N_MOD, D_MODEL)), _const((1, D_MODEL)), _const((1, D_MODEL)), _VMEM,
                  _VMEM, _ANY],
        out_specs=[_rows(ts, D_MODEL), _rows(ts, D_MODEL)] + [_const((1, D_MODEL))] * 4 + [_ANY, _ANY],
        out_shape=[jax.ShapeDtypeStruct((s_len, D_MODEL), F32), jax.ShapeDtypeStruct((s_len, D_MODEL), F32),
                   vec, vec, vec, vec, jax.ShapeDtypeStruct(dw1_shape, F32), jax.ShapeDtypeStruct(dwo_shape, F32)],
        scratch_shapes=[pltpu.VMEM(dw1_shape, F32), pltpu.VMEM(dwo_shape, F32), pltpu.SemaphoreType.DMA((N_CHIPS,)),
                        pltpu.SemaphoreType.DMA((N_CHIPS,))],
        compiler_params=_params(),
    )(dp, dy, x1, mix, h2, ycat, mod6, n2pre, n1post, fc1_g, wout_g, dep)


def _mixer_bwd(z, dyc, w_spatial, bsp_full, gain, bias, w_pool, b_pool, pool_scale, dep, ts):
    s_len = z.shape[0]
    nb = ts // HALO
    last = s_len // HALO - 1
    te = ts + HALO

    def body(z_ref, zprev_ref, znext_ref, dyc_ref, dynext_ref, ws_ref, bsp_ref, gain_ref, bias_ref, wp_ref, bp_ref,
             ps_ref, dep_ref, dz_ref, dws_ref, dbsp_ref, dgain_ref, dbias_ref, dwp_ref, dbp_ref, dps_ref, mixed_ref,
             dvn_ref):
        i = pl.program_id(0)

        @pl.when(i == 0)
        def _():
            for ref in (dws_ref, dbsp_ref, dgain_ref, dbias_ref, dwp_ref, dbp_ref, dps_ref):
                ref[...] = jnp.zeros_like(ref)

        wc = _tril_weights(ws_ref)
        gain = gain_ref[...]
        u, vhat, rstd, vn, dga = _mixer_forward_tile(z_ref[:, :2 * D_A], wc, bsp_ref, gain, bias_ref[...], mixed_ref)
        dya = dyc_ref[:, :D_A]
        du = dya * mixed_ref[...]
        dmixed = dya * u
        dmb = dmixed.astype(BF16)
        dm_sum = dmixed[0:CHUNK, :]
        for k in range(1, ts // CHUNK):
            dm_sum = dm_sum + dmixed[k * CHUNK:(k + 1) * CHUNK, :]
        r_idx = lax.broadcasted_iota(jnp.int32, (CHUNK, CHUNK), 0)
        s_idx = lax.broadcasted_iota(jnp.int32, (CHUNK, CHUNK), 1)
        causal = (s_idx <= r_idx).astype(F32)
        for h in range(N_HEADS):
            lanes = slice(h * HEAD_DIM, (h + 1) * HEAD_DIM)
            dbsp_ref[h] += jnp.sum(dm_sum[:, lanes], axis=1, keepdims=True)
            acc = None
            for k in range(ts // CHUNK):
                rows = slice(k * CHUNK, (k + 1) * CHUNK)
                t = _dot_nt(dmb[rows, lanes], vn[rows, lanes])
                acc = t if acc is None else acc + t
                dvn_ref[rows, lanes] = _dot_tn(wc[h], dmb[rows, lanes])
            dws_ref[h] += acc * causal
        dvn = dvn_ref[...]
        dgain_ref[...] += _colsum(dvn * vhat)
        dbias_ref[...] += _colsum(dvn)
        dvh = dvn * gain
        dv = rstd * (dvh - _rowmean(dvh) - vhat * _rowmean(dvh * vhat))
        dz_ref[:, :D_A] = (du * dga[:, :D_A]).astype(BF16)
        dz_ref[:, D_A:2 * D_A] = (dv * dga[:, D_A:]).astype(BF16)

        zb = z_ref[:, 2 * D_A:]
        prev = jnp.where(i == 0, 0.0, zprev_ref[...])
        zb_ext = jnp.concatenate([zb, znext_ref[...]], axis=0)
        sums = _causal_window_sums(jnp.concatenate([prev, zb_ext], axis=0))
        pos, inv_counts = _window_counts(i * ts, te)
        dyb_ext = jnp.concatenate([dyc_ref[:, D_A:], dynext_ref[...]], axis=0)
        dlin_ext = dyb_ext * ps_ref[...]
        dbp_ref[...] += _colsum(dlin_ext[:ts, :])
        scaled = []
        ddiffs = []
        lins = []
        for g in range(len(POOL_WINDOWS)):
            lanes = slice(g * GROUP_DIM, (g + 1) * GROUP_DIM)
            diff = (sums[g][HALO:, :] * inv_counts[g] - zb_ext[:, lanes]).astype(BF16)
            wpb = wp_ref[g].astype(BF16)
            dlb = dlin_ext[:, lanes].astype(BF16)
            lins.append(_dot(diff[:ts, :], wpb) + bp_ref[:, lanes])
            dwp_ref[g] += _dot_tn(diff[:ts, :], dlb[:ts, :])
            dd = _dot_nt(dlb, wpb)
            ddiffs.append(dd)
            scaled.append(jnp.where(pos < float(s_len), dd * inv_counts[g], 0.0))
        dps_ref[...] += _colsum(dyb_ext[:ts, :] * jnp.concatenate(lins, axis=1))
        back = _anticausal_window_sums(jnp.concatenate(scaled, axis=1))
        for g in range(len(POOL_WINDOWS)):
            dz_ref[:, 2 * D_A + g * GROUP_DIM:2 * D_A + (g + 1) * GROUP_DIM] = (
                back[g][:ts, :] - ddiffs[g][:ts, :]).astype(BF16)

    sq = jax.ShapeDtypeStruct((N_HEADS, CHUNK, CHUNK), F32)
    vec = jax.ShapeDtypeStruct((1, D_A), F32)
    return pl.pallas_call(
        body, name="mixer_bwd", grid=(s_len // ts,),
        in_specs=[_rows(ts, D_Z),
                  pl.BlockSpec((HALO, D_B), lambda i: (jnp.maximum(i * nb - 1, 0), 2)),
                  pl.BlockSpec((HALO, D_B), lambda i: (jnp.minimum((i + 1) * nb, last), 2)),
                  _rows(ts, D_MODEL),
                  pl.BlockSpec((HALO, D_B), lambda i: (jnp.minimum((i + 1) * nb, last), 1)),
                  _const((N_HEADS, CHUNK, CHUNK)), _const((CHUNK, D_A)), _const((1, D_A)), _const((1, D_A)),
                  _const((N_HEADS, GROUP_DIM, GROUP_DIM)), _const((1, D_B)), _const((1, D_B)), _ANY],
        out_specs=[_rows(ts, D_Z), _const((N_HEADS, CHUNK, CHUNK)), _const((N_HEADS, CHUNK, 1)), _const((1, D_A)),
                   _const((1, D_A)), _const((N_HEADS, GROUP_DIM, GROUP_DIM)), _const((1, D_B)), _const((1, D_B))],
        out_shape=[jax.ShapeDtypeStruct((s_len, D_Z), BF16), sq, jax.ShapeDtypeStruct((N_HEADS, CHUNK, 1), F32), vec,
                   vec, sq, vec, vec],
        scratch_shapes=[pltpu.VMEM((ts, D_A), F32), pltpu.VMEM((ts, D_A), F32)],
        compiler_params=_params(),
    )(z, z, z, dyc, dyc, w_spatial, bsp_full, gain, bias, w_pool, b_pool, pool_scale, dep)


def _bwd_in(dz, dx1, x, mod6, n1pre, win_g, dep, ts):
    s_len = x.shape[0]
    cs = D_Z // N_CHIPS

    def body(dz_ref, dx1_ref, x_ref, mod_ref, g_ref, w_ref, dep_ref, gx_ref, dshift_ref, da_ref, dw_hbm, dw_ref,
             wfull_ref, dw_sem):
        _zero_on_first_step(dshift_ref, da_ref, dw_ref)
        _join_w_in_on_first_step(w_ref, wfull_ref)
        xv = x_ref[...]
        r = lax.rsqrt(_rowmean(xv * xv) + EPS)
        xh = xv * r
        h1b = ((xh * g_ref[...]) * (1.0 + mod_ref[1:2, :]) + mod_ref[0:1, :]).astype(BF16)
        dzb = dz_ref[...]
        dw = _dot_tn(h1b, dzb)
        for j in range(N_CHIPS):
            dw_ref[j] += dw[:, j * cs:(j + 1) * cs].reshape(2, D_MODEL // 2, cs)
        dh = _dot_nt(dzb, wfull_ref[...])
        a1 = g_ref[...] * (1.0 + mod_ref[1:2, :])
        dshift_ref[...] += _colsum(dh)
        da_ref[...] += _colsum(dh * xh)
        dxh = dh * a1
        gx_ref[...] = dx1_ref[...] + r * (dxh - xh * _rowmean(dxh * xh))
        for j in range(N_CHIPS):
            _store_shard_on_last_step(dw_ref, dw_hbm, dw_sem, j)
        _wait_stores_on_last_step((dw_ref, dw_hbm, dw_sem))

    vec = jax.ShapeDtypeStruct((1, D_MODEL), F32)
    dw_shape = (N_CHIPS, 2, D_MODEL // 2, cs)
    return pl.pallas_call(
        body, name="bwd_in", grid=(s_len // ts,),
        in_specs=[_rows(ts, D_Z), _rows(ts, D_MODEL), _rows(ts, D_MODEL), _const((N_MOD, D_MODEL)),
                  _const((1, D_MODEL)), _VMEM, _ANY],
        out_specs=[_rows(ts, D_MODEL), _const((1, D_MODEL)), _const((1, D_MODEL)), _ANY],
        out_shape=[jax.ShapeDtypeStruct((s_len, D_MODEL), F32), vec, vec, jax.ShapeDtypeStruct(dw_shape, F32)],
        scratch_shapes=[pltpu.VMEM(dw_shape, F32), pltpu.VMEM((D_MODEL, D_Z), BF16),
                        pltpu.SemaphoreType.DMA((N_CHIPS,))],
        compiler_params=_params(),
    )(dz, dx1, x, mod6, n1pre, win_g, dep)


def _adamw_math(w, g, m, v):
    m = ADAM_B1 * m + (1.0 - ADAM_B1) * g
    v = ADAM_B2 * v + (1.0 - ADAM_B2) * (g * g)
    m_hat = m / (1.0 - ADAM_B1 ** ADAM_STEP)
    v_hat = v / (1.0 - ADAM_B2 ** ADAM_STEP)
    delta = -ADAM_LR * (m_hat / (jnp.sqrt(v_hat) + ADAM_EPS) + ADAM_WD * w)
    return delta, m, v


def _adamw(g, w, m, v, name, tr):
    rows, cols = w.shape

    def body(g_ref, w_ref, m_ref, v_ref, d_ref, nm_ref, nv_ref):
        d, nm, nv = _adamw_math(w_ref[...], g_ref[...], m_ref[...], v_ref[...])
        d_ref[...] = d
        nm_ref[...] = nm
        nv_ref[...] = nv

    spec = _rows(tr, cols)
    shape = jax.ShapeDtypeStruct((rows, cols), F32)
    return pl.pallas_call(
        body, name=name, grid=(rows // tr,), in_specs=[spec] * 4, out_specs=[spec] * 3, out_shape=[shape] * 3,
        compiler_params=_params(),
    )(g, w, m, v)


def _ada_grad_adamw(sc_t, dmod_shard, w, m, v, tr):
    rows, cols = w.shape

    def body(s_ref, dm_ref, w_ref, m_ref, v_ref, g_ref, d_ref, nm_ref, nv_ref):
        g = s_ref[:, 0:1] * dm_ref[0:1, :]
        for b in range(1, N_DEV):
            g = g + s_ref[:, b:b + 1] * dm_ref[b:b + 1, :]
        g_ref[...] = g
        d, nm, nv = _adamw_math(w_ref[...], g, m_ref[...], v_ref[...])
        d_ref[...] = d
        nm_ref[...] = nm
        nv_ref[...] = nv

    spec = _rows(tr, cols)
    shape = jax.ShapeDtypeStruct((rows, cols), F32)
    return pl.pallas_call(
        body, name="ada_grad_adamw", grid=(rows // tr,),
        in_specs=[_rows(tr, N_DEV), _const((N_DEV, cols)), spec, spec, spec],
        out_specs=[spec] * 4, out_shape=[shape] * 4, compiler_params=_params(),
    )(sc_t, dmod_shard, w, m, v)


def _mod_grads(da1, dshift1, dgate1, dg1post, da2, dshift2, dgate2, dg2post, mod6, n1pre, n2pre):
    def body(da1_ref, ds1_ref, dgt1_ref, dg1_ref, da2_ref, ds2_ref, dgt2_ref, dg2_ref, mod_ref, n1_ref, n2_ref,
             dmod_ref, dn_ref):
        dmod_ref[0:1, :] = ds1_ref[...]
        dmod_ref[1:2, :] = da1_ref[...] * n1_ref[...]
        dmod_ref[2:3, :] = dgt1_ref[...]
        dmod_ref[3:4, :] = ds2_ref[...]
        dmod_ref[4:5, :] = da2_ref[...] * n2_ref[...]
        dmod_ref[5:6, :] = dgt2_ref[...]
        dn_ref[0:1, :] = da1_ref[...] * (1.0 + mod_ref[1:2, :])
        dn_ref[1:2, :] = dg1_ref[...]
        dn_ref[2:3, :] = da2_ref[...] * (1.0 + mod_ref[4:5, :])
        dn_ref[3:4, :] = dg2_ref[...]

    return pl.pallas_call(
        body, name="mod_grads",
        out_shape=[jax.ShapeDtypeStruct((N_MOD, D_MODEL), F32), jax.ShapeDtypeStruct((4, D_MODEL), F32)],
    )(da1, dshift1, dgate1, dg1post, da2, dshift2, dgate2, dg2post, mod6, n1pre, n2pre)


def _position():
    x, y, c = lax.axis_index("x"), lax.axis_index("y"), lax.axis_index("c")
    return x, y, c


def _flip(v, bit):
    return 1 - v if bit else v


def _peer(x, y, c, k):
    return (_flip(x, k & 4), _flip(y, k & 2), _flip(c, k & 1))


def _remote(src, dst, send_sem, recv_sem, device):
    return pltpu.make_async_remote_copy(src_ref=src, dst_ref=dst, send_sem=send_sem, recv_sem=recv_sem,
                                        device_id=device, device_id_type=MESH)


def _cast_to_slot(w, pos, dep, name, tr):
    rows, cols = w.shape

    def body(pos_ref, w_ref, dep_ref, o_ref):
        o_ref[0] = w_ref[...].astype(BF16)

    return pl.pallas_call(
        body, name=name,
        grid_spec=pltpu.PrefetchScalarGridSpec(
            num_scalar_prefetch=1, grid=(rows // tr,),
            in_specs=[pl.BlockSpec((tr, cols), lambda i, pos: (i, 0)), _ANY],
            out_specs=pl.BlockSpec((1, tr, cols), lambda i, pos: (pos[1], i, 0))),
        out_shape=jax.ShapeDtypeStruct((N_CHIPS, rows, cols), BF16), compiler_params=_params(),
    )(pos, w, dep)


def _mod_exchange(c_row, w_ada_shard, b_ada_row):
    cs = w_ada_shard.shape[1]

    def body(c_ref, w_hbm, b_ref, mod_ref, sc_ref, rows_ref, w_ref, w_sem, send1, recv1, send2, recv2):
        x, y, c = _position()
        me = 4 * x + 2 * y + c
        chip = 2 * x + y
        w_load = pltpu.make_async_copy(w_hbm, w_ref, w_sem)
        w_load.start()
        cv = c_ref[...]
        sc_ref[me] = cv * jax.nn.sigmoid(cv)
        gather = [_remote(sc_ref.at[me], sc_ref.at[me], send1.at[k - 1], recv1.at[k - 1], _peer(x, y, c, k))
                  for k in range(1, N_DEV)]
        for cp in gather:
            cp.start()
        for k in range(1, N_DEV):
            px, py, pc = _peer(x, y, c, k)
            src = 4 * px + 2 * py + pc
            _remote(sc_ref.at[src], sc_ref.at[src], send1.at[k - 1], recv1.at[k - 1], (px, py, pc)).wait_recv()
        for cp in gather:
            cp.wait_send()
        sc_all = jnp.concatenate([sc_ref[b] for b in range(N_DEV)], axis=0)
        w_load.wait()
        part = _dot(sc_all.astype(BF16), w_ref[...].astype(BF16))
        part = part + b_ref[:, pl.ds(pl.multiple_of(chip * cs, LANES), cs)]
        for b in range(N_DEV):
            rows_ref[b] = part[b:b + 1, :]
        mod_ref[chip] = rows_ref[me]
        hand = []
        for k in (2, 4, 6):
            px, py, _ = _peer(x, y, c, k)
            hand.append(_remote(rows_ref.at[4 * px + 2 * py + c], mod_ref.at[chip], send2.at[k // 2 - 1],
                                recv2.at[k // 2 - 1], (px, py, c)))
        for cp in hand:
            cp.start()
        for k in (2, 4, 6):
            px, py, _ = _peer(x, y, c, k)
            pchip = 2 * px + py
            _remote(rows_ref.at[me], mod_ref.at[pchip], send2.at[k // 2 - 1], recv2.at[k // 2 - 1],
                    (px, py, c)).wait_recv()
        for cp in hand:
            cp.wait_send()

    return pl.pallas_call(
        body, name="mod_exchange",
        in_specs=[_VMEM, _ANY, _VMEM], out_specs=[_VMEM, _VMEM],
        out_shape=[jax.ShapeDtypeStruct((N_CHIPS, 1, cs), F32), jax.ShapeDtypeStruct((N_DEV, 1, D_MODEL), F32)],
        scratch_shapes=[pltpu.VMEM((N_DEV, 1, cs), F32), pltpu.VMEM(w_ada_shard.shape, F32), pltpu.SemaphoreType.DMA,
                        pltpu.SemaphoreType.DMA((N_DEV - 1,)),
                        pltpu.SemaphoreType.DMA((N_DEV - 1,)), pltpu.SemaphoreType.DMA((N_CHIPS - 1,)),
                        pltpu.SemaphoreType.DMA((N_CHIPS - 1,))],
        compiler_params=pltpu.CompilerParams(vmem_limit_bytes=VMEM_LIMIT),
    )(c_row, w_ada_shard, b_ada_row)


_HBM = pl.BlockSpec(memory_space=pltpu.HBM)
_SEM = pl.BlockSpec(memory_space=pltpu.SEMAPHORE)
_EFFECT = pltpu.SideEffectType.DATAFLOW_SIDE_EFFECTING
_CHIP_HOPS = (2, 4, 6)


def _in_hbm(a):
    return pltpu.with_memory_space_constraint(a, pltpu.HBM)


def _sems3():
    return pltpu.SemaphoreType.DMA((len(_CHIP_HOPS),))


def _ag_start(lands, after, name):
    n = len(lands)

    def body(*refs):
        zones = refs[:n]
        sends, recvs = refs[n + 1:2 * n + 1], refs[2 * n + 1:3 * n + 1]
        x, y, c = _position()
        chip = 2 * x + y
        for i in range(n):
            half = zones[i].shape[1] // 2
            mine = zones[i].at[chip, pl.ds(c * half, half)]
            for s, k in enumerate(_CHIP_HOPS):
                px, py, _ = _peer(x, y, c, k)
                _remote(mine, mine, sends[i].at[s], recvs[i].at[s], (px, py, c)).start()

    out = pl.pallas_call(
        body, name=name,
        in_specs=[_HBM] * n + [_ANY],
        out_specs=[_SEM] * (2 * n) + [_HBM] * n,
        out_shape=[_sems3()] * (2 * n) + [pltpu.HBM(z.shape, BF16) for z in lands],
        input_output_aliases={i: 2 * n + i for i in range(n)},
        compiler_params=pltpu.CompilerParams(has_side_effects=_EFFECT),
    )(*[_in_hbm(z) for z in lands], after)
    return [(out[2 * n + i], out[i], out[n + i]) for i in range(n)]


def _ag_pass(group, after, name):
    n = len(group)

    def body(*refs):
        zones = refs[:n]
        sends, recvs = refs[n:2 * n], refs[2 * n:3 * n]
        fsends, frecvs = refs[4 * n + 1:5 * n + 1], refs[5 * n + 1:6 * n + 1]
        x, y, c = _position()
        chip = 2 * x + y
        for i in range(n):
            half = zones[i].shape[1] // 2
            rows = pl.ds(c * half, half)
            for s, k in enumerate(_CHIP_HOPS):
                px, py, _ = _peer(x, y, c, k)
                landed = zones[i].at[2 * px + py, rows]
                _remote(landed, landed, sends[i].at[s], recvs[i].at[s], (px, py, c)).wait_recv()
                _remote(landed, landed, fsends[i].at[s], frecvs[i].at[s], (x, y, 1 - c)).start()
        for i in range(n):
            half = zones[i].shape[1] // 2
            mine = zones[i].at[chip, pl.ds(c * half, half)]
            for s, k in enumerate(_CHIP_HOPS):
                px, py, _ = _peer(x, y, c, k)
                _remote(mine, mine, sends[i].at[s], recvs[i].at[s], (px, py, c)).wait_send()

    out = pl.pallas_call(
        body, name=name,
        in_specs=[_HBM] * n + [_SEM] * (2 * n) + [_ANY],
        out_specs=[_HBM] * n + [_SEM] * (2 * n),
        out_shape=[pltpu.HBM(g[0].shape, BF16) for g in group] + [_sems3()] * (2 * n),
        input_output_aliases={i: i for i in range(n)},
        compiler_params=pltpu.CompilerParams(has_side_effects=_EFFECT),
    )(*[g[0] for g in group], *[g[1] for g in group], *[g[2] for g in group], after)
    return [(out[i], out[n + i], out[2 * n + i]) for i in range(n)]


def _ag_done(group, name):
    n = len(group)

    def body(*refs):
        lands = refs[:n]
        fsends, frecvs = refs[n:2 * n], refs[2 * n:3 * n]
        x, y, c = _position()
        for i in range(n):
            half = lands[i].shape[1] // 2
            for s, k in enumerate(_CHIP_HOPS):
                px, py, _ = _peer(x, y, c, k)
                sent = lands[i].at[2 * px + py, pl.ds(c * half, half)]
                got = lands[i].at[2 * px + py, pl.ds((1 - c) * half, half)]
                cp = _remote(sent, got, fsends[i].at[s], frecvs[i].at[s], (x, y, 1 - c))
                cp.wait_recv()
                cp.wait_send()

    out = pl.pallas_call(
        body, name=name,
        in_specs=[_HBM] * n + [_SEM] * (2 * n),
        out_specs=[_HBM] * n,
        out_shape=[pltpu.HBM(g[0].shape, BF16) for g in group],
        input_output_aliases={i: i for i in range(n)},
        compiler_params=pltpu.CompilerParams(has_side_effects=_EFFECT),
    )(*[g[0] for g in group], *[g[1] for g in group], *[g[2] for g in group])
    return list(out)


def _small_spread_start(slots, after):
    def body(z_ref, after_ref, sends, recvs, z_out):
        x, y, c = _position()
        mine = z_ref.at[2 * x + y]
        for s, k in enumerate(_CHIP_HOPS):
            px, py, _ = _peer(x, y, c, k)
            _remote(mine, mine, sends.at[s], recvs.at[s], (px, py, c)).start()

    sends, recvs, out = pl.pallas_call(
        body, name="small_spread_start",
        in_specs=[_HBM, _ANY], out_specs=[_SEM, _SEM, _HBM],
        out_shape=[_sems3(), _sems3(), pltpu.HBM(slots.shape, F32)],
        input_output_aliases={0: 2},
        compiler_params=pltpu.CompilerParams(has_side_effects=_EFFECT),
    )(_in_hbm(slots), after)
    return out, sends, recvs


def _small_spread_wait(slots, sends, recvs, afters):
    def body(z_ref, sends, recvs, *rest):
        x, y, c = _position()
        mine = z_ref.at[2 * x + y]
        for s, k in enumerate(_CHIP_HOPS):
            px, py, _ = _peer(x, y, c, k)
            cp = _remote(mine, z_ref.at[2 * px + py], sends.at[s], recvs.at[s], (px, py, c))
            cp.wait_recv()
            cp.wait_send()

    return pl.pallas_call(
        body, name="small_spread_wait",
        in_specs=[_HBM, _SEM, _SEM] + [_ANY] * len(afters), out_specs=_HBM, out_shape=pltpu.HBM(slots.shape, F32),
        input_output_aliases={0: 0},
        compiler_params=pltpu.CompilerParams(has_side_effects=_EFFECT),
    )(slots, sends, recvs, *afters)


def _sibling_sum(pgs, name, small=None):
    n = len(pgs)
    k = 0 if small is None else 1
    units = [(i, j) for i in range(n) for j in range(N_CHIPS)]

    def body(*refs):
        refs = list(refs)
        take = lambda count: [refs.pop(0) for _ in range(count)]
        ins, small_in = take(n), take(k)
        qbs, owns, slots_out = take(n), take(n), take(k)
        mine, other, stage, got = take(n), take(n), take(n), take(n)
        load_a, load_b, send, recv = take(4)
        x, y, c = _position()
        chip = 2 * x + y
        if k:
            sib_ref, pair_send, pair_recv = take(3)
            pair = _remote(small_in[0], sib_ref, pair_send, pair_recv, (x, y, 1 - c))
            pair.start()
        loads_a = [pltpu.make_async_copy(ins[i].at[j, 1 - c], other[i].at[j], load_a.at[u])
                   for u, (i, j) in enumerate(units)]
        loads_b = [pltpu.make_async_copy(ins[i].at[j, c], mine[i].at[j], load_b.at[u])
                   for u, (i, j) in enumerate(units)]
        for cp in loads_a + loads_b:
            cp.start()
        sent = []
        for u, (i, j) in enumerate(units):
            loads_a[u].wait()
            stage[i][j] = other[i][j].astype(BF16)
            cp = _remote(stage[i].at[j], got[i].at[j], send.at[u], recv.at[u], (x, y, 1 - c))
            cp.start()
            sent.append(cp)
        for u, (i, j) in enumerate(units):
            loads_b[u].wait()
            sent[u].wait_recv()
            q = mine[i][j] + got[i][j].astype(F32)
            mine[i][j] = q
            qbs[i][j] = q.astype(BF16)
        for i in range(n):
            owns[i][...] = mine[i][chip]
        if k:
            pair.wait()
            slots_out[0][chip] = small_in[0][...] + sib_ref[...]
        for cp in sent:
            cp.wait_send()

    wire = [(N_CHIPS,) + p.shape[2:] for p in pgs]
    extra_out, extra_scratch = [], []
    if k:
        extra_out = [jax.ShapeDtypeStruct((N_CHIPS,) + small.shape, F32)]
        extra_scratch = [pltpu.VMEM(small.shape, F32), pltpu.SemaphoreType.DMA, pltpu.SemaphoreType.DMA]
    out = pl.pallas_call(
        body, name=name, in_specs=[_ANY] * n + [_VMEM] * k, out_specs=[_VMEM] * (2 * n + k),
        out_shape=[jax.ShapeDtypeStruct(w, BF16) for w in wire] + [jax.ShapeDtypeStruct(w[1:], F32) for w in wire]
        + extra_out,
        scratch_shapes=[pltpu.VMEM(w, F32) for w in wire] * 2 + [pltpu.VMEM(w, BF16) for w in wire] * 2
        + [pltpu.SemaphoreType.DMA((len(units),))] * 4 + extra_scratch,
        compiler_params=pltpu.CompilerParams(vmem_limit_bytes=VMEM_LIMIT),
    )(*pgs, *([small] if k else []))
    return list(out[:n]), list(out[n:2 * n]), list(out[2 * n:])


def _rs_start(qbs, name):
    n = len(qbs)

    def body(*refs):
        outs, inboxes = refs[:n], refs[n:2 * n]
        sends, recvs = refs[2 * n:3 * n], refs[3 * n:4 * n]
        x, y, c = _position()
        chip = 2 * x + y
        for i in range(n):
            for s, k in enumerate(_CHIP_HOPS):
                px, py, _ = _peer(x, y, c, k)
                _remote(outs[i].at[2 * px + py], inboxes[i].at[chip], sends[i].at[s], recvs[i].at[s], (px, py, c)).start()

    inboxes = [_in_hbm(lax.empty(q.shape, BF16)) for q in qbs]
    out = pl.pallas_call(
        body, name=name,
        in_specs=[_HBM] * (2 * n),
        out_specs=[_SEM] * (2 * n) + [_HBM] * (2 * n),
        out_shape=[_sems3()] * (2 * n) + [pltpu.HBM(q.shape, BF16) for q in qbs] * 2,
        input_output_aliases={i: 2 * n + i for i in range(2 * n)},
        compiler_params=pltpu.CompilerParams(has_side_effects=_EFFECT),
    )(*[_in_hbm(q) for q in qbs], *inboxes)
    return [(out[2 * n + i], out[3 * n + i], out[i], out[n + i]) for i in range(n)]


def _rs_wait(group, after, name):
    n = len(group)

    def body(*refs):
        outs, inboxes = refs[:n], refs[n:2 * n]
        sends, recvs = refs[2 * n:3 * n], refs[3 * n:4 * n]
        x, y, c = _position()
        for i in range(n):
            for s, k in enumerate(_CHIP_HOPS):
                px, py, _ = _peer(x, y, c, k)
                slot = 2 * px + py
                cp = _remote(outs[i].at[slot], inboxes[i].at[slot], sends[i].at[s], recvs[i].at[s], (px, py, c))
                cp.wait_recv()
                cp.wait_send()

    out = pl.pallas_call(
        body, name=name,
        in_specs=[_HBM] * (2 * n) + [_SEM] * (2 * n) + [_ANY],
        out_specs=[_HBM] * n,
        out_shape=[pltpu.HBM(g[1].shape, BF16) for g in group],
        input_output_aliases={n + i: i for i in range(n)},
        compiler_params=pltpu.CompilerParams(has_side_effects=_EFFECT),
    )(*[g[0] for g in group], *[g[1] for g in group], *[g[2] for g in group], *[g[3] for g in group], after)
    return list(out)


def _final_share(inboxes, owns, name):
    n = len(inboxes)
    units = [(i, s) for i in range(n) for s in range(len(_CHIP_HOPS))]

    def body(*refs):
        ins, mine, outs, landed = refs[:n], refs[n:2 * n], refs[2 * n:3 * n], refs[3 * n:4 * n]
        load, send, recv = refs[4 * n:]
        x, y, c = _position()
        loads = []
        for u, (i, s) in enumerate(units):
            px, py, _ = _peer(x, y, c, _CHIP_HOPS[s])
            loads.append(pltpu.make_async_copy(ins[i].at[2 * px + py], landed[i].at[s], load.at[u]))
        for cp in loads:
            cp.start()
        copies = []
        for i in range(n):
            for s in range(len(_CHIP_HOPS)):
                loads[len(_CHIP_HOPS) * i + s].wait()
            total = (landed[i][0].astype(F32) + landed[i][1].astype(F32)) + landed[i][2].astype(F32)
            outs[i][c] = total + mine[i][...]
            cp = _remote(outs[i].at[c], outs[i].at[c], send.at[i], recv.at[i], (x, y, 1 - c))
            cp.start()
            copies.append(cp)
        for i in range(n):
            theirs = outs[i].at[1 - c]
            _remote(theirs, theirs, send.at[i], recv.at[i], (x, y, 1 - c)).wait_recv()
        for cp in copies:
            cp.wait_send()

    return pl.pallas_call(
        body, name=name, in_specs=[_ANY] * n + [_VMEM] * n, out_specs=[_VMEM] * n,
        out_shape=[jax.ShapeDtypeStruct((2,) + o.shape, F32) for o in owns],
        scratch_shapes=[pltpu.VMEM((len(_CHIP_HOPS),) + o.shape, BF16) for o in owns]
        + [pltpu.SemaphoreType.DMA((len(units),)), pltpu.SemaphoreType.DMA((n,)), pltpu.SemaphoreType.DMA((n,))],
        compiler_params=pltpu.CompilerParams(vmem_limit_bytes=VMEM_LIMIT),
    )(*inboxes, *owns)


_SMALL = (("b_ada", N_MOD * D_MODEL), ("norm1_pre", D_MODEL), ("norm1_post", D_MODEL), ("norm2_pre", D_MODEL),
          ("norm2_post", D_MODEL), ("w_spatial", N_HEADS * CHUNK * CHUNK), ("b_spatial", N_HEADS * CHUNK),
          ("ln_v_gain", D_A), ("ln_v_bias", D_A), ("w_pool", N_HEADS * GROUP_DIM * GROUP_DIM),
          ("b_pool", D_B), ("pool_scale", D_B))
_MOD_ROWS = N_MOD * D_MODEL // LANES


def _packed_rows(size):
    return -(-(size // LANES) // SUBLANES) * SUBLANES


def _pack(parts):
    out = []
    for name, size in _SMALL:
        a = parts[name].reshape(size // LANES, LANES)
        pad = _packed_rows(size) - a.shape[0]
        out.append(jnp.pad(a, ((0, pad), (0, 0))) if pad else a)
    return out


def _small_adamw(slots, ws, ms, vs):
    n = len(_SMALL)
    head = N_DEV * _MOD_ROWS

    def body(*refs):
        s_ref, w, m, v = refs[0], refs[1:1 + n], refs[1 + n:1 + 2 * n], refs[1 + 2 * n:1 + 3 * n]
        outs = refs[1 + 3 * n:1 + 7 * n]
        dmod_ref, loss_ref, t_ref = refs[1 + 7 * n:]
        t_ref[...] = ((s_ref[0] + s_ref[1]) + s_ref[2]) + s_ref[3]
        dmod_ref[...] = t_ref[0:head, :]
        loss_ref[...] = t_ref[t_ref.shape[0] - 8:, :]
        row = head
        for i, (_, size) in enumerate(_SMALL):
            if i == 0:
                g = t_ref[0:_MOD_ROWS, :]
                for b in range(1, N_DEV):
                    g = g + t_ref[b * _MOD_ROWS:(b + 1) * _MOD_ROWS, :]
            else:
                g = t_ref[row:row + size // LANES, :]
                row += _packed_rows(size)
            d, nm, nv = _adamw_math(w[i][...], g, m[i][...], v[i][...])
            for ref, val in zip(outs[4 * i:4 * i + 4], (g, d, nm, nv)):
                ref[...] = val

    each = [jax.ShapeDtypeStruct((size // LANES, LANES), F32) for _, size in _SMALL for _ in range(4)]
    out = pl.pallas_call(
        body, name="small_adamw",
        out_shape=each + [jax.ShapeDtypeStruct((head, LANES), F32), jax.ShapeDtypeStruct((SUBLANES, LANES), F32)],
        scratch_shapes=[pltpu.VMEM(slots.shape[1:], F32)],
        compiler_params=pltpu.CompilerParams(vmem_limit_bytes=VMEM_LIMIT),
    )(slots, *ws, *ms, *vs)
    return [out[4 * i:4 * i + 4] for i in range(n)], out[4 * n], out[4 * n + 1]


def kernel(x, c, w_ada, b_ada, norm1_pre, norm1_post, w_in, w_spatial, b_spatial, ln_v_gain, ln_v_bias, w_pool, b_pool, pool_scale, w_out, norm2_pre, norm2_post, w_fc1, w_fc2, loss_target, m_w_ada, m_b_ada, m_norm1_pre, m_norm1_post, m_w_in, m_w_spatial, m_b_spatial, m_ln_v_gain, m_ln_v_bias, m_w_pool, m_b_pool, m_pool_scale, m_w_out, m_norm2_pre, m_norm2_post, m_w_fc1, m_w_fc2, v_w_ada, v_b_ada, v_norm1_pre, v_norm1_post, v_w_in, v_w_spatial, v_b_spatial, v_ln_v_gain, v_ln_v_bias, v_w_pool, v_b_pool, v_pool_scale, v_w_out, v_norm2_pre, v_norm2_post, v_w_fc1, v_w_fc2):
    weights = dict(w_ada=w_ada, b_ada=b_ada, norm1_pre=norm1_pre, norm1_post=norm1_post, w_in=w_in,
                   w_spatial=w_spatial, b_spatial=b_spatial, ln_v_gain=ln_v_gain, ln_v_bias=ln_v_bias, w_pool=w_pool,
                   b_pool=b_pool, pool_scale=pool_scale, w_out=w_out, norm2_pre=norm2_pre, norm2_post=norm2_post,
                   w_fc1=w_fc1, w_fc2=w_fc2)
    m_old = dict(w_ada=m_w_ada, b_ada=m_b_ada, norm1_pre=m_norm1_pre, norm1_post=m_norm1_post, w_in=m_w_in,
                 w_spatial=m_w_spatial, b_spatial=m_b_spatial, ln_v_gain=m_ln_v_gain, ln_v_bias=m_ln_v_bias,
                 w_pool=m_w_pool, b_pool=m_b_pool, pool_scale=m_pool_scale, w_out=m_w_out, norm2_pre=m_norm2_pre,
                 norm2_post=m_norm2_post, w_fc1=m_w_fc1, w_fc2=m_w_fc2)
    v_old = dict(w_ada=v_w_ada, b_ada=v_b_ada, norm1_pre=v_norm1_pre, norm1_post=v_norm1_post, w_in=v_w_in,
                 w_spatial=v_w_spatial, b_spatial=v_b_spatial, ln_v_gain=v_ln_v_gain, ln_v_bias=v_ln_v_bias,
                 w_pool=v_w_pool, b_pool=v_b_pool, pool_scale=v_pool_scale, w_out=v_w_out, norm2_pre=v_norm2_pre,
                 norm2_post=v_norm2_post, w_fc1=v_w_fc1, w_fc2=v_w_fc2)
    order = ("w_ada", "b_ada", "norm1_pre", "norm1_post", "w_in", "w_spatial", "b_spatial", "ln_v_gain", "ln_v_bias",
             "w_pool", "b_pool", "pool_scale", "w_out", "norm2_pre", "norm2_post", "w_fc1", "w_fc2")
    big = ("w_in", "w_out", "w_fc1", "w_fc2")
    mx, my, mc = _position()
    me = 4 * mx + 2 * my + mc
    chip = 2 * mx + my
    row = lambda a: a.reshape(1, -1)

    pos = jnp.stack([mc, chip]).astype(jnp.int32)
    xs, target = x[0], loss_target[0]
    n1pre, n1post, n2pre, n2post = row(norm1_pre), row(norm1_post), row(norm2_pre), row(norm2_post)
    mixer = (w_spatial, jnp.repeat(b_spatial.T, HEAD_DIM, axis=1), row(ln_v_gain), row(ln_v_bias), w_pool,
             row(b_pool), row(pool_scale))
    ts_big, ts_mid = 512, 256

    mod4, sc_all = _mod_exchange(c, w_ada, row(b_ada))
    mod6 = mod4.reshape(N_MOD, D_MODEL)
    ag = _ag_start([_cast_to_slot(weights[n], pos, mod4, "cast_" + n, 256) for n in big[:2]], mod4, "ag_start_mix")
    ag += _ag_start([_cast_to_slot(weights[n], pos, ag[0][0], "cast_" + n, 256) for n in big[2:]], ag[0][0],
                    "ag_start_mlp")

    win_g, wout_g = _ag_done(_ag_pass([ag[0], ag[1]], ag[2][0], "ag_pass_mix"), "ag_done_mix")
    z, ycat, mix, x1, h2 = _fwd_mix(xs, mod6, n1pre, n1post, n2pre, win_g, wout_g, *mixer, ts_big)
    (fc1_g,) = _ag_done(_ag_pass([ag[2]], h2, "ag_pass_fc1"), "ag_done_fc1")
    q = _fwd_fc1(h2, fc1_g, ts_big)
    (fc2_g,) = _ag_done(_ag_pass([ag[3]], q, "ag_pass_fc2"), "ag_done_fc2")
    dy, df, loss, dgate2, dg2post = _fwd_fc2_loss(q, x1, target, mod6, n2post, fc2_g, ts_big)

    def reduce_start(partials, tag, small=None):
        wire, owns, slots = _sibling_sum(partials, "sibling_sum_" + tag, small)
        return _rs_start(wire, "rs_start_" + tag), owns, slots

    def reduce_finish(state, owns, names, tag, dep):
        inboxes = _rs_wait(state, dep, "rs_wait_" + tag)
        shards = _final_share(inboxes, owns, "final_share_" + tag)
        for n, g in zip(names, shards):
            grads[n] = g.reshape(weights[n].shape)
            deltas[n], new_m[n], new_v[n] = _adamw(grads[n], weights[n], m_old[n], v_old[n], "adamw_" + n, 256)

    grads, deltas, new_m, new_v = {}, {}, {}, {}
    dp, g_fc2 = _bwd_fc2(df, q, fc2_g, ts_big)
    state_fc2, owns_fc2, _ = reduce_start([g_fc2], "fc2")
    dx1, dyc, dshift2, da2, dgate1, dg1post, g_fc1, g_out = _bwd_fc1_out(
        dp, dy, x1, mix, h2, ycat, mod6, n2pre, n1post, fc1_g, wout_g, state_fc2[0][0], ts_mid)
    state_mid, owns_mid, _ = reduce_start([g_fc1, g_out], "mid")
    dz, dws, dbsp, dgain, dbias, dwp, dbp, dps = _mixer_bwd(z, dyc, *mixer, state_mid[0][0], ts_big)
    grad_x, dshift1, da1, g_in = _bwd_in(dz, dx1, xs, mod6, n1pre, win_g, state_mid[0][0], ts_big)
    dmod6, dnorms = _mod_grads(da1, dshift1, dgate1, dg1post, da2, dshift2, dgate2, dg2post, mod6, n1pre, n2pre)

    parts = dict(b_ada=dmod6, norm1_pre=dnorms[0], norm1_post=dnorms[1], norm2_pre=dnorms[2], norm2_post=dnorms[3],
                 w_spatial=dws, b_spatial=dbsp, ln_v_gain=dgain, ln_v_bias=dbias, w_pool=dwp, b_pool=dbp,
                 pool_scale=dps)
    pieces = _pack(parts)
    slots = lax.dynamic_update_slice(jnp.zeros((N_DEV * _MOD_ROWS, LANES), F32), pieces[0], (me * _MOD_ROWS, 0))
    loss_tile = jnp.pad(loss, ((0, SUBLANES - 1), (0, LANES - 1)))
    state_in, owns_in, pair_sum = reduce_start(
        [g_in], "in", jnp.concatenate([slots] + pieces[1:] + [loss_tile], axis=0))
    spread = _small_spread_start(pair_sum[0], state_in[0][0])
    reduce_finish(state_fc2 + state_mid, owns_fc2 + owns_mid, ("w_fc2", "w_fc1", "w_out"), "mlp", spread[0])
    flat = lambda d: [d[n].reshape(size // LANES, LANES) for n, size in _SMALL]
    small_out, dmod_all, loss_tile = _small_adamw(
        _small_spread_wait(*spread, [deltas[n] for n in ("w_fc2", "w_fc1", "w_out")]), flat(weights), flat(m_old),
        flat(v_old))
    loss = loss_tile[0, 0]
    for (n, _), (g, d, nm, nv) in zip(_SMALL, small_out):
        shape = weights[n].shape
        grads[n], deltas[n], new_m[n], new_v[n] = g.reshape(shape), d.reshape(shape), nm.reshape(shape), nv.reshape(shape)

    dmod_all = dmod_all.reshape(N_DEV, N_MOD * D_MODEL)
    cs = w_ada.shape[1]
    dmod_shard = lax.dynamic_slice(dmod_all, (0, chip * cs), (N_DEV, cs))
    sc_t = sc_all.reshape(N_DEV, D_MODEL).T
    grads["w_ada"], deltas["w_ada"], new_m["w_ada"], new_v["w_ada"] = _ada_grad_adamw(
        sc_t, dmod_shard, w_ada, m_w_ada, v_w_ada, 256)

    reduce_finish(state_in, owns_in, ("w_in",), "in", deltas["w_ada"])

    return (loss, grad_x[None], *[grads[n] for n in order], *[deltas[n] for n in order],
            *[new_m[n] for n in order], *[new_v[n] for n in order])
```

```python
import jax
import jax.numpy as jnp
from jax import lax
from jax.experimental import pallas as pl
from jax.experimental.pallas import tpu as pltpu

F32 = jnp.float32
BF16 = jnp.bfloat16
MESH = pl.DeviceIdType.MESH

D_MODEL = 1024
D_A = 512
D_B = 512
D_Z = 2 * D_A + D_B
N_HEADS = 4
HEAD_DIM = 128
CHUNK = 128
POOL_WINDOWS = (2, 4, 8, 16)
GROUP_DIM = 128
D_FF = 4096
N_MOD = 6
EPS = 1e-6
HALO = 16
N_CHIPS = 4
N_DEV = 8

ADAM_LR = 0.001
ADAM_B1 = 0.9
ADAM_B2 = 0.999
ADAM_EPS = 1e-08
ADAM_WD = 0.01
ADAM_STEP = 10

VMEM_LIMIT = 56 * 1024 * 1024
LANES = 128
SUBLANES = 8

_VMEM = pl.BlockSpec(memory_space=pltpu.VMEM)
_ANY = pl.BlockSpec(memory_space=pl.ANY)


def _params(n_grid_axes=1):
    return pltpu.CompilerParams(dimension_semantics=("arbitrary",) * n_grid_axes, vmem_limit_bytes=VMEM_LIMIT)


def _rows(ts, width):
    return pl.BlockSpec((ts, width), lambda i: (i, 0))


def _const(shape):
    return pl.BlockSpec(shape, lambda i: (0,) * len(shape))


def _dot(a, b):
    return jnp.dot(a, b, preferred_element_type=F32)


def _dot_nt(a, b):
    return lax.dot_general(a, b, (((1,), (1,)), ((), ())), preferred_element_type=F32)


def _dot_tn(a, b):
    return lax.dot_general(a, b, (((0,), (0,)), ((), ())), preferred_element_type=F32)


def _rowmean(v):
    return jnp.mean(v, axis=-1, keepdims=True)


def _colsum(v):
    return jnp.sum(v, axis=0, keepdims=True)


def _gelu_parts(z):
    k0 = 0.7978845608028654
    k1 = 0.044715
    z2 = z * z
    t = jnp.tanh(k0 * (z + k1 * z * z2))
    g = 0.5 * z * (1.0 + t)
    dg = 0.5 * (1.0 + t) + 0.5 * z * (1.0 - t * t) * (k0 * (1.0 + 3.0 * k1 * z2))
    return g, dg


def _tril_weights(ws_ref):
    r = lax.broadcasted_iota(jnp.int32, (CHUNK, CHUNK), 0)
    s = lax.broadcasted_iota(jnp.int32, (CHUNK, CHUNK), 1)
    mask = (s <= r).astype(F32)
    return [(ws_ref[h] * mask).astype(BF16) for h in range(N_HEADS)]


def _window_counts(first_row, n_rows):
    pos = (first_row + lax.broadcasted_iota(jnp.int32, (n_rows, 1), 0)).astype(F32)
    return pos, [1.0 / jnp.minimum(pos + 1.0, float(w)) for w in POOL_WINDOWS]


def _causal_window_sums(ext):
    out = []
    e = ext
    shift = 1
    for g in range(len(POOL_WINDOWS)):
        e = e + pltpu.roll(e, shift, 0)
        shift *= 2
        out.append(e[:, g * GROUP_DIM:(g + 1) * GROUP_DIM])
    return out


def _anticausal_window_sums(ext):
    n = ext.shape[0]
    out = []
    e = ext
    shift = 1
    for g in range(len(POOL_WINDOWS)):
        e = e + pltpu.roll(e, n - shift, 0)
        shift *= 2
        out.append(e[:, g * GROUP_DIM:(g + 1) * GROUP_DIM])
    return out


def _fwd_mix(x, mod6, n1pre, n1post, n2pre, win_g, wout_g, w_spatial, bsp_full, gain, bias, w_pool, b_pool, pool_scale, ts):
    s_len = x.shape[0]
    rs = D_MODEL // N_CHIPS

    def body(x_ref, mod_ref, g1pre_ref, g1post_ref, g2pre_ref, win_ref, wout_ref, ws_ref, bsp_ref, gain_ref,
             bias_ref, wp_ref, bp_ref, ps_ref, z_ref, y_ref, mix_ref, x1_ref, h2_ref, mixed_ref, prev_ref, wfull_ref):
        i = pl.program_id(0)
        _zero_on_first_step(prev_ref)
        _join_w_in_on_first_step(win_ref, wfull_ref)
        xv = x_ref[...]
        r = lax.rsqrt(_rowmean(xv * xv) + EPS)
        hb = (((xv * r) * g1pre_ref[...]) * (1.0 + mod_ref[1:2, :]) + mod_ref[0:1, :]).astype(BF16)
        z_ref[...] = _dot(hb, wfull_ref[...])

        wc = _tril_weights(ws_ref)
        u, _, _, _, _ = _mixer_forward_tile(z_ref[:, :2 * D_A], wc, bsp_ref, gain_ref[...], bias_ref[...], mixed_ref)
        y_ref[:, :D_A] = (u * mixed_ref[...]).astype(BF16)
        zb = z_ref[:, 2 * D_A:]
        sums = _causal_window_sums(jnp.concatenate([prev_ref[...], zb], axis=0))
        prev_ref[...] = zb[ts - HALO:, :]
        _, inv_counts = _window_counts(i * ts, ts)
        for g in range(len(POOL_WINDOWS)):
            lanes = slice(g * GROUP_DIM, (g + 1) * GROUP_DIM)
            diff = sums[g][HALO:, :] * inv_counts[g] - zb[:, lanes]
            lin = _dot(diff.astype(BF16), wp_ref[g].astype(BF16)) + bp_ref[:, lanes]
            y_ref[:, D_A + g * GROUP_DIM:D_A + (g + 1) * GROUP_DIM] = (lin * ps_ref[:, lanes]).astype(BF16)

        mix = None
        for j in range(N_CHIPS):
            part = _dot(y_ref[:, j * rs:(j + 1) * rs], wout_ref[j])
            mix = part if mix is None else mix + part
        mix_ref[...] = mix
        r2 = lax.rsqrt(_rowmean(mix * mix) + EPS)
        x1 = xv + mod_ref[2:3, :] * ((mix * r2) * g1post_ref[...])
        x1_ref[...] = x1
        r3 = lax.rsqrt(_rowmean(x1 * x1) + EPS)
        h2_ref[...] = (((x1 * r3) * g2pre_ref[...]) * (1.0 + mod_ref[4:5, :]) + mod_ref[3:4, :]).astype(BF16)

    vec = _const((1, D_MODEL))
    f32_rows = jax.ShapeDtypeStruct((s_len, D_MODEL), F32)
    bf16_rows = jax.ShapeDtypeStruct((s_len, D_MODEL), BF16)
    return pl.pallas_call(
        body, name="fwd_mix", grid=(s_len // ts,),
        in_specs=[_rows(ts, D_MODEL), _const((N_MOD, D_MODEL)), vec, vec, vec, _VMEM, _VMEM,
                  _const((N_HEADS, CHUNK, CHUNK)), _const((CHUNK, D_A)), _const((1, D_A)), _const((1, D_A)),
                  _const((N_HEADS, GROUP_DIM, GROUP_DIM)), _const((1, D_B)), _const((1, D_B))],
        out_specs=[_rows(ts, D_Z), _rows(ts, D_MODEL), _rows(ts, D_MODEL), _rows(ts, D_MODEL), _rows(ts, D_MODEL)],
        out_shape=[jax.ShapeDtypeStruct((s_len, D_Z), F32), bf16_rows, f32_rows, f32_rows, bf16_rows],
        scratch_shapes=[pltpu.VMEM((ts, D_A), F32), pltpu.VMEM((HALO, D_B), F32), pltpu.VMEM((D_MODEL, D_Z), BF16)],
        compiler_params=_params(),
    )(x, mod6, n1pre, n1post, n2pre, win_g, wout_g, w_spatial, bsp_full, gain, bias, w_pool, b_pool, pool_scale)


def _mixer_forward_tile(za, wc, bsp_ref, gain, bias, mixed_ref):
    ga, dga = _gelu_parts(za)
    u = ga[:, :D_A]
    v = ga[:, D_A:]
    mu = _rowmean(v)
    vc = v - mu
    rstd = lax.rsqrt(_rowmean(vc * vc) + EPS)
    vhat = vc * rstd
    vn = (vhat * gain + bias).astype(BF16)
    ts = za.shape[0]
    for k in range(ts // CHUNK):
        for h in range(N_HEADS):
            blk = vn[k * CHUNK:(k + 1) * CHUNK, h * HEAD_DIM:(h + 1) * HEAD_DIM]
            mixed_ref[k * CHUNK:(k + 1) * CHUNK, h * HEAD_DIM:(h + 1) * HEAD_DIM] = (
                _dot(wc[h], blk) + bsp_ref[:, h * HEAD_DIM:(h + 1) * HEAD_DIM])
    return u, vhat, rstd, vn, dga


def _fwd_fc1(h2, fc1_g, ts):
    s_len = h2.shape[0]
    cs = D_FF // N_CHIPS

    def body(h_ref, w_ref, q_ref):
        hb = h_ref[...]
        for j in range(N_CHIPS):
            p = jnp.maximum(_dot(hb, w_ref[j]), 0.0)
            q_ref[:, j * cs:(j + 1) * cs] = (p * p).astype(BF16)

    return pl.pallas_call(
        body, name="fwd_fc1", grid=(s_len // ts,),
        in_specs=[_rows(ts, D_MODEL), _VMEM],
        out_specs=_rows(ts, D_FF),
        out_shape=jax.ShapeDtypeStruct((s_len, D_FF), BF16),
        compiler_params=_params(),
    )(h2, fc1_g)


def _fwd_fc2_loss(q, x1, target, mod6, n2post, fc2_g, ts):
    s_len = q.shape[0]
    rs = D_FF // N_CHIPS

    def body(q_ref, x1_ref, t_ref, mod_ref, g_ref, w_ref, dy_ref, df_ref, loss_ref, dgate_ref, dg_ref):
        _zero_on_first_step(loss_ref, dgate_ref, dg_ref)
        gate = mod_ref[5:6, :]
        gn = g_ref[...]
        f = _dot(q_ref[:, 0:rs], w_ref[0])
        for j in range(1, N_CHIPS):
            f = f + _dot(q_ref[:, j * rs:(j + 1) * rs], w_ref[j])
        r4 = lax.rsqrt(_rowmean(f * f) + EPS)
        fh = f * r4
        err = (x1_ref[...] + gate * (fh * gn)) - t_ref[...]
        loss_ref[...] += 0.5 * jnp.sum(_rowmean(err * err), axis=0, keepdims=True)
        dy = err * (1.0 / D_MODEL)
        dy_ref[...] = dy
        dgate_ref[...] += _colsum(dy * (fh * gn))
        dg_ref[...] += _colsum((dy * gate) * fh)
        gh = (dy * gate) * gn
        df_ref[...] = (r4 * (gh - fh * _rowmean(gh * fh))).astype(BF16)

    return pl.pallas_call(
        body, name="fwd_fc2_loss", grid=(s_len // ts,),
        in_specs=[_rows(ts, D_FF), _rows(ts, D_MODEL), _rows(ts, D_MODEL), _const((N_MOD, D_MODEL)),
                  _const((1, D_MODEL)), _VMEM],
        out_specs=[_rows(ts, D_MODEL), _rows(ts, D_MODEL), _const((1, 1)), _const((1, D_MODEL)), _const((1, D_MODEL))],
        out_shape=[jax.ShapeDtypeStruct((s_len, D_MODEL), F32), jax.ShapeDtypeStruct((s_len, D_MODEL), BF16),
                   jax.ShapeDtypeStruct((1, 1), F32), jax.ShapeDtypeStruct((1, D_MODEL), F32),
                   jax.ShapeDtypeStruct((1, D_MODEL), F32)],
        compiler_params=_params(),
    )(q, x1, target, mod6, n2post, fc2_g)


def _join_w_in_on_first_step(win_ref, full_ref):
    cs = D_Z // N_CHIPS

    @pl.when(pl.program_id(0) == 0)
    def _():
        for j in range(N_CHIPS):
            full_ref[:, j * cs:(j + 1) * cs] = win_ref[j]


def _zero_on_first_step(*refs):
    @pl.when(pl.program_id(0) == 0)
    def _():
        for ref in refs:
            ref[...] = jnp.zeros_like(ref)


def _on_last_step(fn):
    pl.when(pl.program_id(0) == pl.num_programs(0) - 1)(fn)


def _store_shard_on_last_step(acc_ref, hbm_ref, sem, j):
    _on_last_step(lambda: pltpu.make_async_copy(acc_ref.at[j], hbm_ref.at[j], sem.at[j]).start())


def _wait_stores_on_last_step(*stores):
    def wait_all():
        for acc_ref, hbm_ref, sem in stores:
            for j in range(N_CHIPS):
                pltpu.make_async_copy(acc_ref.at[j], hbm_ref.at[j], sem.at[j]).wait()

    _on_last_step(wait_all)


def _bwd_fc2(df, q, fc2_g, ts):
    s_len = df.shape[0]
    cs = D_FF // N_CHIPS

    def body(df_ref, q_ref, w_ref, dp_ref, dw_hbm, dw_ref, dw_sem):
        _zero_on_first_step(dw_ref)
        dfb = df_ref[...]
        for j in range(N_CHIPS):
            qb = q_ref[:, j * cs:(j + 1) * cs]
            dw_ref[j] += _dot_tn(qb, dfb).reshape(2, cs // 2, D_MODEL)
            _store_shard_on_last_step(dw_ref, dw_hbm, dw_sem, j)
            dq = _dot_nt(dfb, w_ref[j])
            dp_ref[:, j * cs:(j + 1) * cs] = (dq * (2.0 * jnp.sqrt(qb.astype(F32)))).astype(BF16)
        _wait_stores_on_last_step((dw_ref, dw_hbm, dw_sem))

    dw_shape = (N_CHIPS, 2, cs // 2, D_MODEL)
    return pl.pallas_call(
        body, name="bwd_fc2", grid=(s_len // ts,),
        in_specs=[_rows(ts, D_MODEL), _rows(ts, D_FF), _VMEM],
        out_specs=[_rows(ts, D_FF), _ANY],
        out_shape=[jax.ShapeDtypeStruct((s_len, D_FF), BF16), jax.ShapeDtypeStruct(dw_shape, F32)],
        scratch_shapes=[pltpu.VMEM(dw_shape, F32), pltpu.SemaphoreType.DMA((N_CHIPS,))],
        compiler_params=_params(),
    )(df, q, fc2_g)


def _bwd_fc1_out(dp, dy, x1, mix, h2, ycat, mod6, n2pre, n1post, fc1_g, wout_g, dep, ts):
    s_len = dp.shape[0]
    cs = D_FF // N_CHIPS
    rs = D_MODEL // N_CHIPS

    def body(dp_ref, dy_ref, x1_ref, mix_ref, h2_ref, yc_ref, mod_ref, g2_ref, g1_ref, w1_ref, wo_ref, dep_ref,
             dx1_ref, dyc_ref, dshift2_ref, da2_ref, dgate1_ref, dg1_ref, dw1_hbm, dwo_hbm, dw1_ref, dwo_ref, dw1_sem,
             dwo_sem):
        _zero_on_first_step(dshift2_ref, da2_ref, dgate1_ref, dg1_ref, dw1_ref, dwo_ref)
        h2b = h2_ref[...]
        dh2 = None
        for j in range(N_CHIPS):
            dpb = dp_ref[:, j * cs:(j + 1) * cs]
            dw1_ref[j] += _dot_tn(h2b, dpb).reshape(2, D_MODEL // 2, cs)
            _store_shard_on_last_step(dw1_ref, dw1_hbm, dw1_sem, j)
            part = _dot_nt(dpb, w1_ref[j])
            dh2 = part if dh2 is None else dh2 + part
        x1 = x1_ref[...]
        r3 = lax.rsqrt(_rowmean(x1 * x1) + EPS)
        xh = x1 * r3
        a2 = g2_ref[...] * (1.0 + mod_ref[4:5, :])
        dshift2_ref[...] += _colsum(dh2)
        da2_ref[...] += _colsum(dh2 * xh)
        dxh = dh2 * a2
        dx1 = dy_ref[...] + r3 * (dxh - xh * _rowmean(dxh * xh))
        dx1_ref[...] = dx1

        mix = mix_ref[...]
        r2 = lax.rsqrt(_rowmean(mix * mix) + EPS)
        mh = mix * r2
        gate = mod_ref[2:3, :]
        gn = g1_ref[...]
        dgate1_ref[...] += _colsum(dx1 * (mh * gn))
        dg1_ref[...] += _colsum((dx1 * gate) * mh)
        gh = (dx1 * gate) * gn
        dmix = (r2 * (gh - mh * _rowmean(gh * mh))).astype(BF16)
        dwo_ref[...] += _dot_tn(yc_ref[...], dmix).reshape(N_CHIPS, 2, rs // 2, D_MODEL)
        for j in range(N_CHIPS):
            _store_shard_on_last_step(dwo_ref, dwo_hbm, dwo_sem, j)
            dyc_ref[:, j * rs:(j + 1) * rs] = _dot_nt(dmix, wo_ref[j])
        _wait_stores_on_last_step((dw1_ref, dw1_hbm, dw1_sem), (dwo_ref, dwo_hbm, dwo_sem))

    vec = jax.ShapeDtypeStruct((1, D_MODEL), F32)
    dw1_shape = (N_CHIPS, 2, D_MODEL // 2, cs)
    dwo_shape = (N_CHIPS, 2, rs // 2, D_MODEL)
    return pl.pallas_call(
        body, name="bwd_fc1_out", grid=(s_len // ts,),
        in_specs=[_rows(ts, D_FF), _rows(ts, D_MODEL), _rows(ts, D_MODEL), _rows(ts, D_MODEL), _rows(ts, D_MODEL),
                  _rows(ts, D_MODEL), _const((N_MOD, D_MODEL)), _const((1, D_MODEL)), _const((1, D_MODEL)), _VMEM,
                  _VMEM, _ANY],
        out_specs=[_rows(ts, D_MODEL), _rows(ts, D_MODEL)] + [_const((1, D_MODEL))] * 4 + [_ANY, _ANY],
        out_shape=[jax.ShapeDtypeStruct((s_len, D_MODEL), F32), jax.ShapeDtypeStruct((s_len, D_MODEL), F32),
                   vec, vec, vec, vec, jax.ShapeDtypeStruct(dw1_shape, F32), jax.ShapeDtypeStruct(dwo_shape, F32)],
        scratch_shapes=[pltpu.VMEM(dw1_shape, F32), pltpu.VMEM(dwo_shape, F32), pltpu.SemaphoreType.DMA((N_CHIPS,)),
                        pltpu.SemaphoreType.DMA((N_CHIPS,))],
        compiler_params=_params(),
    )(dp, dy, x1, mix, h2, ycat, mod6, n2pre, n1post, fc1_g, wout_g, dep)


def _mixer_bwd(z, dyc, w_spatial, bsp_full, gain, bias, w_pool, b_pool, pool_scale, dep, ts):
    s_len = z.shape[0]
    nb = ts // HALO
    last = s_len // HALO - 1
    te = ts + HALO

    def body(z_ref, zprev_ref, znext_ref, dyc_ref, dynext_ref, ws_ref, bsp_ref, gain_ref, bias_ref, wp_ref, bp_ref,
             ps_ref, dep_ref, dz_ref, dws_ref, dbsp_ref, dgain_ref, dbias_ref, dwp_ref, dbp_ref, dps_ref, mixed_ref,
             dvn_ref):
        i = pl.program_id(0)

        @pl.when(i == 0)
        def _():
            for ref in (dws_ref, dbsp_ref, dgain_ref, dbias_ref, dwp_ref, dbp_ref, dps_ref):
                ref[...] = jnp.zeros_like(ref)

        wc = _tril_weights(ws_ref)
        gain = gain_ref[...]
        u, vhat, rstd, vn, dga = _mixer_forward_tile(z_ref[:, :2 * D_A], wc, bsp_ref, gain, bias_ref[...], mixed_ref)
        dya = dyc_ref[:, :D_A]
        du = dya * mixed_ref[...]
        dmixed = dya * u
        dmb = dmixed.astype(BF16)
        dm_sum = dmixed[0:CHUNK, :]
        for k in range(1, ts // CHUNK):
            dm_sum = dm_sum + dmixed[k * CHUNK:(k + 1) * CHUNK, :]
        r_idx = lax.broadcasted_iota(jnp.int32, (CHUNK, CHUNK), 0)
        s_idx = lax.broadcasted_iota(jnp.int32, (CHUNK, CHUNK), 1)
        causal = (s_idx <= r_idx).astype(F32)
        for h in range(N_HEADS):
            lanes = slice(h * HEAD_DIM, (h + 1) * HEAD_DIM)
            dbsp_ref[h] += jnp.sum(dm_sum[:, lanes], axis=1, keepdims=True)
            acc = None
            for k in range(ts // CHUNK):
                rows = slice(k * CHUNK, (k + 1) * CHUNK)
                t = _dot_nt(dmb[rows, lanes], vn[rows, lanes])
                acc = t if acc is None else acc + t
                dvn_ref[rows, lanes] = _dot_tn(wc[h], dmb[rows, lanes])
            dws_ref[h] += acc * causal
        dvn = dvn_ref[...]
        dgain_ref[...] += _colsum(dvn * vhat)
        dbias_ref[...] += _colsum(dvn)
        dvh = dvn * gain
        dv = rstd * (dvh - _rowmean(dvh) - vhat * _rowmean(dvh * vhat))
        dz_ref[:, :D_A] = (du * dga[:, :D_A]).astype(BF16)
        dz_ref[:, D_A:2 * D_A] = (dv * dga[:, D_A:]).astype(BF16)

        zb = z_ref[:, 2 * D_A:]
        prev = jnp.where(i == 0, 0.0, zprev_ref[...])
        zb_ext = jnp.concatenate([zb, znext_ref[...]], axis=0)
        sums = _causal_window_sums(jnp.concatenate([prev, zb_ext], axis=0))
        pos, inv_counts = _window_counts(i * ts, te)
        dyb_ext = jnp.concatenate([dyc_ref[:, D_A:], dynext_ref[...]], axis=0)
        dlin_ext = dyb_ext * ps_ref[...]
        dbp_ref[...] += _colsum(dlin_ext[:ts, :])
        scaled = []
        ddiffs = []
        lins = []
        for g in range(len(POOL_WINDOWS)):
            lanes = slice(g * GROUP_DIM, (g + 1) * GROUP_DIM)
            diff = (sums[g][HALO:, :] * inv_counts[g] - zb_ext[:, lanes]).astype(BF16)
            wpb = wp_ref[g].astype(BF16)
            dlb = dlin_ext[:, lanes].astype(BF16)
            lins.append(_dot(diff[:ts, :], wpb) + bp_ref[:, lanes])
            dwp_ref[g] += _dot_tn(diff[:ts, :], dlb[:ts, :])
            dd = _dot_nt(dlb, wpb)
            ddiffs.append(dd)
            scaled.append(jnp.where(pos < float(s_len), dd * inv_counts[g], 0.0))
        dps_ref[...] += _colsum(dyb_ext[:ts, :] * jnp.concatenate(lins, axis=1))
        back = _anticausal_window_sums(jnp.concatenate(scaled, axis=1))
        for g in range(len(POOL_WINDOWS)):
            dz_ref[:, 2 * D_A + g * GROUP_DIM:2 * D_A + (g + 1) * GROUP_DIM] = (
                back[g][:ts, :] - ddiffs[g][:ts, :]).astype(BF16)

    sq = jax.ShapeDtypeStruct((N_HEADS, CHUNK, CHUNK), F32)
    vec = jax.ShapeDtypeStruct((1, D_A), F32)
    return pl.pallas_call(
        body, name="mixer_bwd", grid=(s_len // ts,),
        in_specs=[_rows(ts, D_Z),
                  pl.BlockSpec((HALO, D_B), lambda i: (jnp.maximum(i * nb - 1, 0), 2)),
                  pl.BlockSpec((HALO, D_B), lambda i: (jnp.minimum((i + 1) * nb, last), 2)),
                  _rows(ts, D_MODEL),
                  pl.BlockSpec((HALO, D_B), lambda i: (jnp.minimum((i + 1) * nb, last), 1)),
                  _const((N_HEADS, CHUNK, CHUNK)), _const((CHUNK, D_A)), _const((1, D_A)), _const((1, D_A)),
                  _const((N_HEADS, GROUP_DIM, GROUP_DIM)), _const((1, D_B)), _const((1, D_B)), _ANY],
        out_specs=[_rows(ts, D_Z), _const((N_HEADS, CHUNK, CHUNK)), _const((N_HEADS, CHUNK, 1)), _const((1, D_A)),
                   _const((1, D_A)), _const((N_HEADS, GROUP_DIM, GROUP_DIM)), _const((1, D_B)), _const((1, D_B))],
        out_shape=[jax.ShapeDtypeStruct((s_len, D_Z), BF16), sq, jax.ShapeDtypeStruct((N_HEADS, CHUNK, 1), F32), vec,
                   vec, sq, vec, vec],
        scratch_shapes=[pltpu.VMEM((ts, D_A), F32), pltpu.VMEM((ts, D_A), F32)],
        compiler_params=_params(),
    )(z, z, z, dyc, dyc, w_spatial, bsp_full, gain, bias, w_pool, b_pool, pool_scale, dep)


def _bwd_in(dz, dx1, x, mod6, n1pre, win_g, dep, ts):
    s_len = x.shape[0]
    cs = D_Z // N_CHIPS

    def body(dz_ref, dx1_ref, x_ref, mod_ref, g_ref, w_ref, dep_ref, gx_ref, dshift_ref, da_ref, dw_hbm, dw_ref,
             wfull_ref, dw_sem):
        _zero_on_first_step(dshift_ref, da_ref, dw_ref)
        _join_w_in_on_first_step(w_ref, wfull_ref)
        xv = x_ref[...]
        r = lax.rsqrt(_rowmean(xv * xv) + EPS)
        xh = xv * r
        h1b = ((xh * g_ref[...]) * (1.0 + mod_ref[1:2, :]) + mod_ref[0:1, :]).astype(BF16)
        dzb = dz_ref[...]
        dw = _dot_tn(h1b, dzb)
        for j in range(N_CHIPS):
            dw_ref[j] += dw[:, j * cs:(j + 1) * cs].reshape(2, D_MODEL // 2, cs)
        dh = _dot_nt(dzb, wfull_ref[...])
        a1 = g_ref[...] * (1.0 + mod_ref[1:2, :])
        dshift_ref[...] += _colsum(dh)
        da_ref[...] += _colsum(dh * xh)
        dxh = dh * a1
        gx_ref[...] = dx1_ref[...] + r * (dxh - xh * _rowmean(dxh * xh))
        for j in range(N_CHIPS):
            _store_shard_on_last_step(dw_ref, dw_hbm, dw_sem, j)
        _wait_stores_on_last_step((dw_ref, dw_hbm, dw_sem))

    vec = jax.ShapeDtypeStruct((1, D_MODEL), F32)
    dw_shape = (N_CHIPS, 2, D_MODEL // 2, cs)
    return pl.pallas_call(
        body, name="bwd_in", grid=(s_len // ts,),
        in_specs=[_rows(ts, D_Z), _rows(ts, D_MODEL), _rows(ts, D_MODEL), _const((N_MOD, D_MODEL)),
                  _const((1, D_MODEL)), _VMEM, _ANY],
        out_specs=[_rows(ts, D_MODEL), _const((1, D_MODEL)), _const((1, D_MODEL)), _ANY],
        out_shape=[jax.ShapeDtypeStruct((s_len, D_MODEL), F32), vec, vec, jax.ShapeDtypeStruct(dw_shape, F32)],
        scratch_shapes=[pltpu.VMEM(dw_shape, F32), pltpu.VMEM((D_MODEL, D_Z), BF16),
                        pltpu.SemaphoreType.DMA((N_CHIPS,))],
        compiler_params=_params(),
    )(dz, dx1, x, mod6, n1pre, win_g, dep)


def _adamw_math(w, g, m, v):
    m = ADAM_B1 * m + (1.0 - ADAM_B1) * g
    v = ADAM_B2 * v + (1.0 - ADAM_B2) * (g * g)
    m_hat = m / (1.0 - ADAM_B1 ** ADAM_STEP)
    v_hat = v / (1.0 - ADAM_B2 ** ADAM_STEP)
    delta = -ADAM_LR * (m_hat / (jnp.sqrt(v_hat) + ADAM_EPS) + ADAM_WD * w)
    return delta, m, v


def _adamw(g, w, m, v, name, tr):
    rows, cols = w.shape

    def body(g_ref, w_ref, m_ref, v_ref, d_ref, nm_ref, nv_ref):
        d, nm, nv = _adamw_math(w_ref[...], g_ref[...], m_ref[...], v_ref[...])
        d_ref[...] = d
        nm_ref[...] = nm
        nv_ref[...] = nv

    spec = _rows(tr, cols)
    shape = jax.ShapeDtypeStruct((rows, cols), F32)
    return pl.pallas_call(
        body, name=name, grid=(rows // tr,), in_specs=[spec] * 4, out_specs=[spec] * 3, out_shape=[shape] * 3,
        compiler_params=_params(),
    )(g, w, m, v)


def _ada_grad_adamw(sc_t, dmod_shard, w, m, v, tr):
    rows, cols = w.shape

    def body(s_ref, dm_ref, w_ref, m_ref, v_ref, g_ref, d_ref, nm_ref, nv_ref):
        g = s_ref[:, 0:1] * dm_ref[0:1, :]
        for b in range(1, N_DEV):
            g = g + s_ref[:, b:b + 1] * dm_ref[b:b + 1, :]
        g_ref[...] = g
        d, nm, nv = _adamw_math(w_ref[...], g, m_ref[...], v_ref[...])
        d_ref[...] = d
        nm_ref[...] = nm
        nv_ref[...] = nv

    spec = _rows(tr, cols)
    shape = jax.ShapeDtypeStruct((rows, cols), F32)
    return pl.pallas_call(
        body, name="ada_grad_adamw", grid=(rows // tr,),
        in_specs=[_rows(tr, N_DEV), _const((N_DEV, cols)), spec, spec, spec],
        out_specs=[spec] * 4, out_shape=[shape] * 4, compiler_params=_params(),
    )(sc_t, dmod_shard, w, m, v)


def _mod_grads(da1, dshift1, dgate1, dg1post, da2, dshift2, dgate2, dg2post, mod6, n1pre, n2pre):
    def body(da1_ref, ds1_ref, dgt1_ref, dg1_ref, da2_ref, ds2_ref, dgt2_ref, dg2_ref, mod_ref, n1_ref, n2_ref,
             dmod_ref, dn_ref):
        dmod_ref[0:1, :] = ds1_ref[...]
        dmod_ref[1:2, :] = da1_ref[...] * n1_ref[...]
        dmod_ref[2:3, :] = dgt1_ref[...]
        dmod_ref[3:4, :] = ds2_ref[...]
        dmod_ref[4:5, :] = da2_ref[...] * n2_ref[...]
        dmod_ref[5:6, :] = dgt2_ref[...]
        dn_ref[0:1, :] = da1_ref[...] * (1.0 + mod_ref[1:2, :])
        dn_ref[1:2, :] = dg1_ref[...]
        dn_ref[2:3, :] = da2_ref[...] * (1.0 + mod_ref[4:5, :])
        dn_ref[3:4, :] = dg2_ref[...]

    return pl.pallas_call(
        body, name="mod_grads",
        out_shape=[jax.ShapeDtypeStruct((N_MOD, D_MODEL), F32), jax.ShapeDtypeStruct((4, D_MODEL), F32)],
    )(da1, dshift1, dgate1, dg1post, da2, dshift2, dgate2, dg2post, mod6, n1pre, n2pre)


def _position():
    x, y, c = lax.axis_index("x"), lax.axis_index("y"), lax.axis_index("c")
    return x, y, c


def _flip(v, bit):
    return 1 - v if bit else v


def _peer(x, y, c, k):
    return (_flip(x, k & 4), _flip(y, k & 2), _flip(c, k & 1))


def _remote(src, dst, send_sem, recv_sem, device):
    return pltpu.make_async_remote_copy(src_ref=src, dst_ref=dst, send_sem=send_sem, recv_sem=recv_sem,
                                        device_id=device, device_id_type=MESH)


def _cast_to_slot(w, pos, dep, name, tr):
    rows, cols = w.shape

    def body(pos_ref, w_ref, dep_ref, o_ref):
        o_ref[0] = w_ref[...].astype(BF16)

    return pl.pallas_call(
        body, name=name,
        grid_spec=pltpu.PrefetchScalarGridSpec(
            num_scalar_prefetch=1, grid=(rows // tr,),
            in_specs=[pl.BlockSpec((tr, cols), lambda i, pos: (i, 0)), _ANY],
            out_specs=pl.BlockSpec((1, tr, cols), lambda i, pos: (pos[1], i, 0))),
        out_shape=jax.ShapeDtypeStruct((N_CHIPS, rows, cols), BF16), compiler_params=_params(),
    )(pos, w, dep)


def _mod_exchange(c_row, w_ada_shard, b_ada_row):
    cs = w_ada_shard.shape[1]

    def body(c_ref, w_hbm, b_ref, mod_ref, sc_ref, rows_ref, w_ref, w_sem, send1, recv1, send2, recv2):
        x, y, c = _position()
        me = 4 * x + 2 * y + c
        chip = 2 * x + y
        w_load = pltpu.make_async_copy(w_hbm, w_ref, w_sem)
        w_load.start()
        cv = c_ref[...]
        sc_ref[me] = cv * jax.nn.sigmoid(cv)
        gather = [_remote(sc_ref.at[me], sc_ref.at[me], send1.at[k - 1], recv1.at[k - 1], _peer(x, y, c, k))
                  for k in range(1, N_DEV)]
        for cp in gather:
            cp.start()
        for k in range(1, N_DEV):
            px, py, pc = _peer(x, y, c, k)
            src = 4 * px + 2 * py + pc
            _remote(sc_ref.at[src], sc_ref.at[src], send1.at[k - 1], recv1.at[k - 1], (px, py, pc)).wait_recv()
        for cp in gather:
            cp.wait_send()
        sc_all = jnp.concatenate([sc_ref[b] for b in range(N_DEV)], axis=0)
        w_load.wait()
        part = _dot(sc_all.astype(BF16), w_ref[...].astype(BF16))
        part = part + b_ref[:, pl.ds(pl.multiple_of(chip * cs, LANES), cs)]
        for b in range(N_DEV):
            rows_ref[b] = part[b:b + 1, :]
        mod_ref[chip] = rows_ref[me]
        hand = []
        for k in (2, 4, 6):
            px, py, _ = _peer(x, y, c, k)
            hand.append(_remote(rows_ref.at[4 * px + 2 * py + c], mod_ref.at[chip], send2.at[k // 2 - 1],
                                recv2.at[k // 2 - 1], (px, py, c)))
        for cp in hand:
            cp.start()
        for k in (2, 4, 6):
            px, py, _ = _peer(x, y, c, k)
            pchip = 2 * px + py
            _remote(rows_ref.at[me], mod_ref.at[pchip], send2.at[k // 2 - 1], recv2.at[k // 2 - 1],
                    (px, py, c)).wait_recv()
        for cp in hand:
            cp.wait_send()

    return pl.pallas_call(
        body, name="mod_exchange",
        in_specs=[_VMEM, _ANY, _VMEM], out_specs=[_VMEM, _VMEM],
        out_shape=[jax.ShapeDtypeStruct((N_CHIPS, 1, cs), F32), jax.ShapeDtypeStruct((N_DEV, 1, D_MODEL), F32)],
        scratch_shapes=[pltpu.VMEM((N_DEV, 1, cs), F32), pltpu.VMEM(w_ada_shard.shape, F32), pltpu.SemaphoreType.DMA,
                        pltpu.SemaphoreType.DMA((N_DEV - 1,)),
                        pltpu.SemaphoreType.DMA((N_DEV - 1,)), pltpu.SemaphoreType.DMA((N_CHIPS - 1,)),
                        pltpu.SemaphoreType.DMA((N_CHIPS - 1,))],
        compiler_params=pltpu.CompilerParams(vmem_limit_bytes=VMEM_LIMIT),
    )(c_row, w_ada_shard, b_ada_row)


_HBM = pl.BlockSpec(memory_space=pltpu.HBM)
_SEM = pl.BlockSpec(memory_space=pltpu.SEMAPHORE)
_EFFECT = pltpu.SideEffectType.DATAFLOW_SIDE_EFFECTING
_CHIP_HOPS = (2, 4, 6)


def _in_hbm(a):
    return pltpu.with_memory_space_constraint(a, pltpu.HBM)


def _sems3():
    return pltpu.SemaphoreType.DMA((len(_CHIP_HOPS),))


def _ag_start(lands, after, name):
    n = len(lands)

    def body(*refs):
        zones = refs[:n]
        sends, recvs = refs[n + 1:2 * n + 1], refs[2 * n + 1:3 * n + 1]
        x, y, c = _position()
        chip = 2 * x + y
        for i in range(n):
            half = zones[i].shape[1] // 2
            mine = zones[i].at[chip, pl.ds(c * half, half)]
            for s, k in enumerate(_CHIP_HOPS):
                px, py, _ = _peer(x, y, c, k)
                _remote(mine, mine, sends[i].at[s], recvs[i].at[s], (px, py, c)).start()

    out = pl.pallas_call(
        body, name=name,
        in_specs=[_HBM] * n + [_ANY],
        out_specs=[_SEM] * (2 * n) + [_HBM] * n,
        out_shape=[_sems3()] * (2 * n) + [pltpu.HBM(z.shape, BF16) for z in lands],
        input_output_aliases={i: 2 * n + i for i in range(n)},
        compiler_params=pltpu.CompilerParams(has_side_effects=_EFFECT),
    )(*[_in_hbm(z) for z in lands], after)
    return [(out[2 * n + i], out[i], out[n + i]) for i in range(n)]


def _ag_pass(group, after, name):
    n = len(group)

    def body(*refs):
        zones = refs[:n]
        sends, recvs = refs[n:2 * n], refs[2 * n:3 * n]
        fsends, frecvs = refs[4 * n + 1:5 * n + 1], refs[5 * n + 1:6 * n + 1]
        x, y, c = _position()
        chip = 2 * x + y
        for i in range(n):
            half = zones[i].shape[1] // 2
            rows = pl.ds(c * half, half)
            for s, k in enumerate(_CHIP_HOPS):
                px, py, _ = _peer(x, y, c, k)
                landed = zones[i].at[2 * px + py, rows]
                _remote(landed, landed, sends[i].at[s], recvs[i].at[s], (px, py, c)).wait_recv()
                _remote(landed, landed, fsends[i].at[s], frecvs[i].at[s], (x, y, 1 - c)).start()
        for i in range(n):
            half = zones[i].shape[1] // 2
            mine = zones[i].at[chip, pl.ds(c * half, half)]
            for s, k in enumerate(_CHIP_HOPS):
                px, py, _ = _peer(x, y, c, k)
                _remote(mine, mine, sends[i].at[s], recvs[i].at[s], (px, py, c)).wait_send()

    out = pl.pallas_call(
        body, name=name,
        in_specs=[_HBM] * n + [_SEM] * (2 * n) + [_ANY],
        out_specs=[_HBM] * n + [_SEM] * (2 * n),
        out_shape=[pltpu.HBM(g[0].shape, BF16) for g in group] + [_sems3()] * (2 * n),
        input_output_aliases={i: i for i in range(n)},
        compiler_params=pltpu.CompilerParams(has_side_effects=_EFFECT),
    )(*[g[0] for g in group], *[g[1] for g in group], *[g[2] for g in group], after)
    return [(out[i], out[n + i], out[2 * n + i]) for i in range(n)]


def _ag_done(group, name):
    n = len(group)

    def body(*refs):
        lands = refs[:n]
        fsends, frecvs = refs[n:2 * n], refs[2 * n:3 * n]
        x, y, c = _position()
        for i in range(n):
            half = lands[i].shape[1] // 2
            for s, k in enumerate(_CHIP_HOPS):
                px, py, _ = _peer(x, y, c, k)
                sent = lands[i].at[2 * px + py, pl.ds(c * half, half)]
                got = lands[i].at[2 * px + py, pl.ds((1 - c) * half, half)]
                cp = _remote(sent, got, fsends[i].at[s], frecvs[i].at[s], (x, y, 1 - c))
                cp.wait_recv()
                cp.wait_send()

    out = pl.pallas_call(
        body, name=name,
        in_specs=[_HBM] * n + [_SEM] * (2 * n),
        out_specs=[_HBM] * n,
        out_shape=[pltpu.HBM(g[0].shape, BF16) for g in group],
        input_output_aliases={i: i for i in range(n)},
        compiler_params=pltpu.CompilerParams(has_side_effects=_EFFECT),
    )(*[g[0] for g in group], *[g[1] for g in group], *[g[2] for g in group])
    return list(out)


def _small_spread_start(slots, after):
    def body(z_ref, after_ref, sends, recvs, z_out):
        x, y, c = _position()
        mine = z_ref.at[2 * x + y]
        for s, k in enumerate(_CHIP_HOPS):
            px, py, _ = _peer(x, y, c, k)
            _remote(mine, mine, sends.at[s], recvs.at[s], (px, py, c)).start()

    sends, recvs, out = pl.pallas_call(
        body, name="small_spread_start",
        in_specs=[_HBM, _ANY], out_specs=[_SEM, _SEM, _HBM],
        out_shape=[_sems3(), _sems3(), pltpu.HBM(slots.shape, F32)],
        input_output_aliases={0: 2},
        compiler_params=pltpu.CompilerParams(has_side_effects=_EFFECT),
    )(_in_hbm(slots), after)
    return out, sends, recvs


def _small_spread_wait(slots, sends, recvs, afters):
    def body(z_ref, sends, recvs, *rest):
        x, y, c = _position()
        mine = z_ref.at[2 * x + y]
        for s, k in enumerate(_CHIP_HOPS):
            px, py, _ = _peer(x, y, c, k)
            cp = _remote(mine, z_ref.at[2 * px + py], sends.at[s], recvs.at[s], (px, py, c))
            cp.wait_recv()
            cp.wait_send()

    return pl.pallas_call(
        body, name="small_spread_wait",
        in_specs=[_HBM, _SEM, _SEM] + [_ANY] * len(afters), out_specs=_HBM, out_shape=pltpu.HBM(slots.shape, F32),
        input_output_aliases={0: 0},
        compiler_params=pltpu.CompilerParams(has_side_effects=_EFFECT),
    )(slots, sends, recvs, *afters)


def _sibling_sum(pgs, name, small=None):
    n = len(pgs)
    k = 0 if small is None else 1
    units = [(i, j) for i in range(n) for j in range(N_CHIPS)]

    def body(*refs):
        refs = list(refs)
        take = lambda count: [refs.pop(0) for _ in range(count)]
        ins, small_in = take(n), take(k)
        qbs, owns, slots_out = take(n), take(n), take(k)
        mine, other, stage, got = take(n), take(n), take(n), take(n)
        load_a, load_b, send, recv = take(4)
        x, y, c = _position()
        chip = 2 * x + y
        if k:
            sib_ref, pair_send, pair_recv = take(3)
            pair = _remote(small_in[0], sib_ref, pair_send, pair_recv, (x, y, 1 - c))
            pair.start()
        loads_a = [pltpu.make_async_copy(ins[i].at[j, 1 - c], other[i].at[j], load_a.at[u])
                   for u, (i, j) in enumerate(units)]
        loads_b = [pltpu.make_async_copy(ins[i].at[j, c], mine[i].at[j], load_b.at[u])
                   for u, (i, j) in enumerate(units)]
        for cp in loads_a + loads_b:
            cp.start()
        sent = []
        for u, (i, j) in enumerate(units):
            loads_a[u].wait()
            stage[i][j] = other[i][j].astype(BF16)
            cp = _remote(stage[i].at[j], got[i].at[j], send.at[u], recv.at[u], (x, y, 1 - c))
            cp.start()
            sent.append(cp)
        for u, (i, j) in enumerate(units):
            loads_b[u].wait()
            sent[u].wait_recv()
            q = mine[i][j] + got[i][j].astype(F32)
            mine[i][j] = q
            qbs[i][j] = q.astype(BF16)
        for i in range(n):
            owns[i][...] = mine[i][chip]
        if k:
            pair.wait()
            slots_out[0][chip] = small_in[0][...] + sib_ref[...]
        for cp in sent:
            cp.wait_send()

    wire = [(N_CHIPS,) + p.shape[2:] for p in pgs]
    extra_out, extra_scratch = [], []
    if k:
        extra_out = [jax.ShapeDtypeStruct((N_CHIPS,) + small.shape, F32)]
        extra_scratch = [pltpu.VMEM(small.shape, F32), pltpu.SemaphoreType.DMA, pltpu.SemaphoreType.DMA]
    out = pl.pallas_call(
        body, name=name, in_specs=[_ANY] * n + [_VMEM] * k, out_specs=[_VMEM] * (2 * n + k),
        out_shape=[jax.ShapeDtypeStruct(w, BF16) for w in wire] + [jax.ShapeDtypeStruct(w[1:], F32) for w in wire]
        + extra_out,
        scratch_shapes=[pltpu.VMEM(w, F32) for w in wire] * 2 + [pltpu.VMEM(w, BF16) for w in wire] * 2
        + [pltpu.SemaphoreType.DMA((len(units),))] * 4 + extra_scratch,
        compiler_params=pltpu.CompilerParams(vmem_limit_bytes=VMEM_LIMIT),
    )(*pgs, *([small] if k else []))
    return list(out[:n]), list(out[n:2 * n]), list(out[2 * n:])


def _sibling_send_start(pgs, after, name):
    n = len(pgs)

    def body(*refs):
        ins, lands = refs[:n], refs[n:2 * n]
        sends, recvs = refs[2 * n + 1:3 * n + 1], refs[3 * n + 1:4 * n + 1]
        x, y, c = _position()
        for i in range(n):
            for j in range(N_CHIPS):
                _remote(ins[i].at[j, 1 - c], lands[i].at[j], sends[i].at[j], recvs[i].at[j], (x, y, 1 - c)).start()

    zones = [(N_CHIPS,) + p.shape[2:] for p in pgs]
    sems = pltpu.SemaphoreType.DMA((N_CHIPS,))
    out = pl.pallas_call(
        body, name=name,
        in_specs=[_HBM] * (2 * n) + [_ANY],
        out_specs=[_SEM] * (2 * n) + [_HBM] * (2 * n),
        out_shape=[sems] * (2 * n) + [pltpu.HBM(p.shape, F32) for p in pgs] + [pltpu.HBM(z, F32) for z in zones],
        input_output_aliases={i: 2 * n + i for i in range(2 * n)},
        compiler_params=pltpu.CompilerParams(has_side_effects=_EFFECT),
    )(*[_in_hbm(p) for p in pgs], *[_in_hbm(lax.empty(z, F32)) for z in zones], after)
    return [(out[2 * n + i], out[3 * n + i], out[i], out[n + i]) for i in range(n)]


def _sibling_add(group, after, name):
    n = len(group)
    units = [(i, j) for i in range(n) for j in range(N_CHIPS)]

    def body(*refs):
        refs = list(refs)
        take = lambda count: [refs.pop(0) for _ in range(count)]
        ins, lands, sends, recvs, _ = take(n), take(n), take(n), take(n), take(1)
        qbs, owns = take(n), take(n)
        mine, got = take(n), take(n)
        load_a, load_b = take(2)
        x, y, c = _position()
        chip = 2 * x + y
        loads_b = [pltpu.make_async_copy(ins[i].at[j, c], mine[i].at[j], load_b.at[u])
                   for u, (i, j) in enumerate(units)]
        for cp in loads_b:
            cp.start()
        loads_a = []
        for u, (i, j) in enumerate(units):
            _remote(ins[i].at[j, 1 - c], lands[i].at[j], sends[i].at[j], recvs[i].at[j], (x, y, 1 - c)).wait_recv()
            cp = pltpu.make_async_copy(lands[i].at[j], got[i].at[j], load_a.at[u])
            cp.start()
            loads_a.append(cp)
        for u, (i, j) in enumerate(units):
            loads_b[u].wait()
            loads_a[u].wait()
            q = mine[i][j] + got[i][j]
            mine[i][j] = q
            qbs[i][j] = q.astype(BF16)
        for i in range(n):
            owns[i][...] = mine[i][chip]
        for i, j in units:
            _remote(ins[i].at[j, 1 - c], lands[i].at[j], sends[i].at[j], recvs[i].at[j], (x, y, 1 - c)).wait_send()

    wire = [g[1].shape for g in group]
    out = pl.pallas_call(
        body, name=name,
        in_specs=[_HBM] * (2 * n) + [_SEM] * (2 * n) + [_ANY], out_specs=[_VMEM] * (2 * n),
        out_shape=[jax.ShapeDtypeStruct(w, BF16) for w in wire] + [jax.ShapeDtypeStruct(w[1:], F32) for w in wire],
        scratch_shapes=[pltpu.VMEM(w, F32) for w in wire] * 2 + [pltpu.SemaphoreType.DMA((len(units),))] * 2,
        compiler_params=pltpu.CompilerParams(vmem_limit_bytes=VMEM_LIMIT, has_side_effects=_EFFECT),
    )(*[g[0] for g in group], *[g[1] for g in group], *[g[2] for g in group], *[g[3] for g in group], after)
    return list(out[:n]), list(out[n:])


def _rs_start(qbs, name):
    n = len(qbs)

    def body(*refs):
        outs, inboxes = refs[:n], refs[n:2 * n]
        sends, recvs = refs[2 * n:3 * n], refs[3 * n:4 * n]
        x, y, c = _position()
        chip = 2 * x + y
        for i in range(n):
            for s, k in enumerate(_CHIP_HOPS):
                px, py, _ = _peer(x, y, c, k)
                _remote(outs[i].at[2 * px + py], inboxes[i].at[chip], sends[i].at[s], recvs[i].at[s], (px, py, c)).start()

    inboxes = [_in_hbm(lax.empty(q.shape, BF16)) for q in qbs]
    out = pl.pallas_call(
        body, name=name,
        in_specs=[_HBM] * (2 * n),
        out_specs=[_SEM] * (2 * n) + [_HBM] * (2 * n),
        out_shape=[_sems3()] * (2 * n) + [pltpu.HBM(q.shape, BF16) for q in qbs] * 2,
        input_output_aliases={i: 2 * n + i for i in range(2 * n)},
        compiler_params=pltpu.CompilerParams(has_side_effects=_EFFECT),
    )(*[_in_hbm(q) for q in qbs], *inboxes)
    return [(out[2 * n + i], out[3 * n + i], out[i], out[n + i]) for i in range(n)]


def _rs_wait(group, after, name):
    n = len(group)

    def body(*refs):
        outs, inboxes = refs[:n], refs[n:2 * n]
        sends, recvs = refs[2 * n:3 * n], refs[3 * n:4 * n]
        x, y, c = _position()
        for i in range(n):
            for s, k in enumerate(_CHIP_HOPS):
                px, py, _ = _peer(x, y, c, k)
                slot = 2 * px + py
                cp = _remote(outs[i].at[slot], inboxes[i].at[slot], sends[i].at[s], recvs[i].at[s], (px, py, c))
                cp.wait_recv()
                cp.wait_send()

    out = pl.pallas_call(
        body, name=name,
        in_specs=[_HBM] * (2 * n) + [_SEM] * (2 * n) + [_ANY],
        out_specs=[_HBM] * n,
        out_shape=[pltpu.HBM(g[1].shape, BF16) for g in group],
        input_output_aliases={n + i: i for i in range(n)},
        compiler_params=pltpu.CompilerParams(has_side_effects=_EFFECT),
    )(*[g[0] for g in group], *[g[1] for g in group], *[g[2] for g in group], *[g[3] for g in group], after)
    return list(out)


def _final_share(inboxes, owns, name):
    n = len(inboxes)
    units = [(i, s) for i in range(n) for s in range(len(_CHIP_HOPS))]

    def body(*refs):
        ins, mine, outs, landed = refs[:n], refs[n:2 * n], refs[2 * n:3 * n], refs[3 * n:4 * n]
        load, send, recv = refs[4 * n:]
        x, y, c = _position()
        loads = []
        for u, (i, s) in enumerate(units):
            px, py, _ = _peer(x, y, c, _CHIP_HOPS[s])
            loads.append(pltpu.make_async_copy(ins[i].at[2 * px + py], landed[i].at[s], load.at[u]))
        for cp in loads:
            cp.start()
        copies = []
        for i in range(n):
            for s in range(len(_CHIP_HOPS)):
                loads[len(_CHIP_HOPS) * i + s].wait()
            total = (landed[i][0].astype(F32) + landed[i][1].astype(F32)) + landed[i][2].astype(F32)
            outs[i][c] = total + mine[i][...]
            cp = _remote(outs[i].at[c], outs[i].at[c], send.at[i], recv.at[i], (x, y, 1 - c))
            cp.start()
            copies.append(cp)
        for i in range(n):
            theirs = outs[i].at[1 - c]
            _remote(theirs, theirs, send.at[i], recv.at[i], (x, y, 1 - c)).wait_recv()
        for cp in copies:
            cp.wait_send()

    return pl.pallas_call(
        body, name=name, in_specs=[_ANY] * n + [_VMEM] * n, out_specs=[_VMEM] * n,
        out_shape=[jax.ShapeDtypeStruct((2,) + o.shape, F32) for o in owns],
        scratch_shapes=[pltpu.VMEM((len(_CHIP_HOPS),) + o.shape, BF16) for o in owns]
        + [pltpu.SemaphoreType.DMA((len(units),)), pltpu.SemaphoreType.DMA((n,)), pltpu.SemaphoreType.DMA((n,))],
        compiler_params=pltpu.CompilerParams(vmem_limit_bytes=VMEM_LIMIT),
    )(*inboxes, *owns)


_SMALL = (("b_ada", N_MOD * D_MODEL), ("norm1_pre", D_MODEL), ("norm1_post", D_MODEL), ("norm2_pre", D_MODEL),
          ("norm2_post", D_MODEL), ("w_spatial", N_HEADS * CHUNK * CHUNK), ("b_spatial", N_HEADS * CHUNK),
          ("ln_v_gain", D_A), ("ln_v_bias", D_A), ("w_pool", N_HEADS * GROUP_DIM * GROUP_DIM),
          ("b_pool", D_B), ("pool_scale", D_B))
_MOD_ROWS = N_MOD * D_MODEL // LANES


def _packed_rows(size):
    return -(-(size // LANES) // SUBLANES) * SUBLANES


def _pack(parts):
    out = []
    for name, size in _SMALL:
        a = parts[name].reshape(size // LANES, LANES)
        pad = _packed_rows(size) - a.shape[0]
        out.append(jnp.pad(a, ((0, pad), (0, 0))) if pad else a)
    return out


def _small_adamw(slots, ws, ms, vs):
    n = len(_SMALL)
    head = N_DEV * _MOD_ROWS

    def body(*refs):
        s_ref, w, m, v = refs[0], refs[1:1 + n], refs[1 + n:1 + 2 * n], refs[1 + 2 * n:1 + 3 * n]
        outs = refs[1 + 3 * n:1 + 7 * n]
        dmod_ref, loss_ref, t_ref = refs[1 + 7 * n:]
        t_ref[...] = ((s_ref[0] + s_ref[1]) + s_ref[2]) + s_ref[3]
        dmod_ref[...] = t_ref[0:head, :]
        loss_ref[...] = t_ref[t_ref.shape[0] - 8:, :]
        row = head
        for i, (_, size) in enumerate(_SMALL):
            if i == 0:
                g = t_ref[0:_MOD_ROWS, :]
                for b in range(1, N_DEV):
                    g = g + t_ref[b * _MOD_ROWS:(b + 1) * _MOD_ROWS, :]
            else:
                g = t_ref[row:row + size // LANES, :]
                row += _packed_rows(size)
            d, nm, nv = _adamw_math(w[i][...], g, m[i][...], v[i][...])
            for ref, val in zip(outs[4 * i:4 * i + 4], (g, d, nm, nv)):
                ref[...] = val

    each = [jax.ShapeDtypeStruct((size // LANES, LANES), F32) for _, size in _SMALL for _ in range(4)]
    out = pl.pallas_call(
        body, name="small_adamw",
        out_shape=each + [jax.ShapeDtypeStruct((head, LANES), F32), jax.ShapeDtypeStruct((SUBLANES, LANES), F32)],
        scratch_shapes=[pltpu.VMEM(slots.shape[1:], F32)],
        compiler_params=pltpu.CompilerParams(vmem_limit_bytes=VMEM_LIMIT),
    )(slots, *ws, *ms, *vs)
    return [out[4 * i:4 * i + 4] for i in range(n)], out[4 * n], out[4 * n + 1]


def kernel(x, c, w_ada, b_ada, norm1_pre, norm1_post, w_in, w_spatial, b_spatial, ln_v_gain, ln_v_bias, w_pool, b_pool, pool_scale, w_out, norm2_pre, norm2_post, w_fc1, w_fc2, loss_target, m_w_ada, m_b_ada, m_norm1_pre, m_norm1_post, m_w_in, m_w_spatial, m_b_spatial, m_ln_v_gain, m_ln_v_bias, m_w_pool, m_b_pool, m_pool_scale, m_w_out, m_norm2_pre, m_norm2_post, m_w_fc1, m_w_fc2, v_w_ada, v_b_ada, v_norm1_pre, v_norm1_post, v_w_in, v_w_spatial, v_b_spatial, v_ln_v_gain, v_ln_v_bias, v_w_pool, v_b_pool, v_pool_scale, v_w_out, v_norm2_pre, v_norm2_post, v_w_fc1, v_w_fc2):
    weights = dict(w_ada=w_ada, b_ada=b_ada, norm1_pre=norm1_pre, norm1_post=norm1_post, w_in=w_in,
                   w_spatial=w_spatial, b_spatial=b_spatial, ln_v_gain=ln_v_gain, ln_v_bias=ln_v_bias, w_pool=w_pool,
                   b_pool=b_pool, pool_scale=pool_scale, w_out=w_out, norm2_pre=norm2_pre, norm2_post=norm2_post,
                   w_fc1=w_fc1, w_fc2=w_fc2)
    m_old = dict(w_ada=m_w_ada, b_ada=m_b_ada, norm1_pre=m_norm1_pre, norm1_post=m_norm1_post, w_in=m_w_in,
                 w_spatial=m_w_spatial, b_spatial=m_b_spatial, ln_v_gain=m_ln_v_gain, ln_v_bias=m_ln_v_bias,
                 w_pool=m_w_pool, b_pool=m_b_pool, pool_scale=m_pool_scale, w_out=m_w_out, norm2_pre=m_norm2_pre,
                 norm2_post=m_norm2_post, w_fc1=m_w_fc1, w_fc2=m_w_fc2)
    v_old = dict(w_ada=v_w_ada, b_ada=v_b_ada, norm1_pre=v_norm1_pre, norm1_post=v_norm1_post, w_in=v_w_in,
                 w_spatial=v_w_spatial, b_spatial=v_b_spatial, ln_v_gain=v_ln_v_gain, ln_v_bias=v_ln_v_bias,
                 w_pool=v_w_pool, b_pool=v_b_pool, pool_scale=v_pool_scale, w_out=v_w_out, norm2_pre=v_norm2_pre,
                 norm2_post=v_norm2_post, w_fc1=v_w_fc1, w_fc2=v_w_fc2)
    order = ("w_ada", "b_ada", "norm1_pre", "norm1_post", "w_in", "w_spatial", "b_spatial", "ln_v_gain", "ln_v_bias",
             "w_pool", "b_pool", "pool_scale", "w_out", "norm2_pre", "norm2_post", "w_fc1", "w_fc2")
    big = ("w_in", "w_out", "w_fc1", "w_fc2")
    mx, my, mc = _position()
    me = 4 * mx + 2 * my + mc
    chip = 2 * mx + my
    row = lambda a: a.reshape(1, -1)

    pos = jnp.stack([mc, chip]).astype(jnp.int32)
    xs, target = x[0], loss_target[0]
    n1pre, n1post, n2pre, n2post = row(norm1_pre), row(norm1_post), row(norm2_pre), row(norm2_post)
    mixer = (w_spatial, jnp.repeat(b_spatial.T, HEAD_DIM, axis=1), row(ln_v_gain), row(ln_v_bias), w_pool,
             row(b_pool), row(pool_scale))
    ts_big, ts_mid = 512, 256

    mod4, sc_all = _mod_exchange(c, w_ada, row(b_ada))
    mod6 = mod4.reshape(N_MOD, D_MODEL)
    ag = _ag_start([_cast_to_slot(weights[n], pos, mod4, "cast_" + n, 256) for n in big[:2]], mod4, "ag_start_mix")
    ag += _ag_start([_cast_to_slot(weights[n], pos, ag[0][0], "cast_" + n, 256) for n in big[2:]], ag[0][0],
                    "ag_start_mlp")

    win_g, wout_g = _ag_done(_ag_pass([ag[0], ag[1]], ag[2][0], "ag_pass_mix"), "ag_done_mix")
    z, ycat, mix, x1, h2 = _fwd_mix(xs, mod6, n1pre, n1post, n2pre, win_g, wout_g, *mixer, ts_big)
    (fc1_g,) = _ag_done(_ag_pass([ag[2]], h2, "ag_pass_fc1"), "ag_done_fc1")
    q = _fwd_fc1(h2, fc1_g, ts_big)
    (fc2_g,) = _ag_done(_ag_pass([ag[3]], q, "ag_pass_fc2"), "ag_done_fc2")
    dy, df, loss, dgate2, dg2post = _fwd_fc2_loss(q, x1, target, mod6, n2post, fc2_g, ts_big)

    def reduce_start(partials, tag, small=None):
        wire, owns, slots = _sibling_sum(partials, "sibling_sum_" + tag, small)
        return _rs_start(wire, "rs_start_" + tag), owns, slots

    def reduce_finish(state, owns, names, tag, dep):
        inboxes = _rs_wait(state, dep, "rs_wait_" + tag)
        shards = _final_share(inboxes, owns, "final_share_" + tag)
        for n, g in zip(names, shards):
            grads[n] = g.reshape(weights[n].shape)
            deltas[n], new_m[n], new_v[n] = _adamw(grads[n], weights[n], m_old[n], v_old[n], "adamw_" + n, 256)

    grads, deltas, new_m, new_v = {}, {}, {}, {}
    dp, g_fc2 = _bwd_fc2(df, q, fc2_g, ts_big)
    send_fc2 = _sibling_send_start([g_fc2], dp, "sibling_send_fc2")
    dx1, dyc, dshift2, da2, dgate1, dg1post, g_fc1, g_out = _bwd_fc1_out(
        dp, dy, x1, mix, h2, ycat, mod6, n2pre, n1post, fc1_g, wout_g, send_fc2[0][1], ts_mid)
    wire_fc2, owns_fc2 = _sibling_add(send_fc2, dx1, "sibling_add_fc2")
    state_fc2 = _rs_start(wire_fc2, "rs_start_fc2")
    send_mid = _sibling_send_start([g_fc1, g_out], state_fc2[0][0], "sibling_send_mid")
    dz, dws, dbsp, dgain, dbias, dwp, dbp, dps = _mixer_bwd(z, dyc, *mixer, send_mid[0][1], ts_big)
    wire_mid, owns_mid = _sibling_add(send_mid, dz, "sibling_add_mid")
    state_mid = _rs_start(wire_mid, "rs_start_mid")
    grad_x, dshift1, da1, g_in = _bwd_in(dz, dx1, xs, mod6, n1pre, win_g, state_mid[0][0], ts_big)
    dmod6, dnorms = _mod_grads(da1, dshift1, dgate1, dg1post, da2, dshift2, dgate2, dg2post, mod6, n1pre, n2pre)

    parts = dict(b_ada=dmod6, norm1_pre=dnorms[0], norm1_post=dnorms[1], norm2_pre=dnorms[2], norm2_post=dnorms[3],
                 w_spatial=dws, b_spatial=dbsp, ln_v_gain=dgain, ln_v_bias=dbias, w_pool=dwp, b_pool=dbp,
                 pool_scale=dps)
    pieces = _pack(parts)
    slots = lax.dynamic_update_slice(jnp.zeros((N_DEV * _MOD_ROWS, LANES), F32), pieces[0], (me * _MOD_ROWS, 0))
    loss_tile = jnp.pad(loss, ((0, SUBLANES - 1), (0, LANES - 1)))
    state_in, owns_in, pair_sum = reduce_start(
        [g_in], "in", jnp.concatenate([slots] + pieces[1:] + [loss_tile], axis=0))
    spread = _small_spread_start(pair_sum[0], state_in[0][0])
    reduce_finish(state_fc2 + state_mid, owns_fc2 + owns_mid, ("w_fc2", "w_fc1", "w_out"), "mlp", spread[0])
    flat = lambda d: [d[n].reshape(size // LANES, LANES) for n, size in _SMALL]
    small_out, dmod_all, loss_tile = _small_adamw(
        _small_spread_wait(*spread, [deltas[n] for n in ("w_fc2", "w_fc1", "w_out")]), flat(weights), flat(m_old),
        flat(v_old))
    loss = loss_tile[0, 0]
    for (n, _), (g, d, nm, nv) in zip(_SMALL, small_out):
        shape = weights[n].shape
        grads[n], deltas[n], new_m[n], new_v[n] = g.reshape(shape), d.reshape(shape), nm.reshape(shape), nv.reshape(shape)

    dmod_all = dmod_all.reshape(N_DEV, N_MOD * D_MODEL)
    cs = w_ada.shape[1]
    dmod_shard = lax.dynamic_slice(dmod_all, (0, chip * cs), (N_DEV, cs))
    sc_t = sc_all.reshape(N_DEV, D_MODEL).T
    grads["w_ada"], deltas["w_ada"], new_m["w_ada"], new_v["w_ada"] = _ada_grad_adamw(
        sc_t, dmod_shard, w_ada, m_w_ada, v_w_ada, 256)

    reduce_finish(state_in, owns_in, ("w_in",), "in", deltas["w_ada"])

    return (loss, grad_x[None], *[grads[n] for n in order], *[deltas[n] for n in order],
            *[new_m[n] for n in order], *[new_v[n] for n in order])
```

```python
import jax
import jax.numpy as jnp
from jax import lax
from jax.experimental import pallas as pl
from jax.experimental.pallas import tpu as pltpu

F32 = jnp.float32
BF16 = jnp.bfloat16
MESH = pl.DeviceIdType.MESH

D_MODEL = 1024
D_A = 512
D_B = 512
D_Z = 2 * D_A + D_B
N_HEADS = 4
HEAD_DIM = 128
CHUNK = 128
POOL_WINDOWS = (2, 4, 8, 16)
GROUP_DIM = 128
D_FF = 4096
N_MOD = 6
EPS = 1e-6
HALO = 16
N_CHIPS = 4
N_DEV = 8

ADAM_LR = 0.001
ADAM_B1 = 0.9
ADAM_B2 = 0.999
ADAM_EPS = 1e-08
ADAM_WD = 0.01
ADAM_STEP = 10

VMEM_LIMIT = 60 * 1024 * 1024
LANES = 128
SUBLANES = 8

_VMEM = pl.BlockSpec(memory_space=pltpu.VMEM)
_ANY = pl.BlockSpec(memory_space=pl.ANY)


def _params(n_grid_axes=1):
    return pltpu.CompilerParams(dimension_semantics=("arbitrary",) * n_grid_axes, vmem_limit_bytes=VMEM_LIMIT)


def _rows(ts, width):
    return pl.BlockSpec((ts, width), lambda i: (i, 0))


def _const(shape):
    return pl.BlockSpec(shape, lambda i: (0,) * len(shape))


def _dot(a, b):
    return jnp.dot(a, b, preferred_element_type=F32)


def _dot_nt(a, b):
    return lax.dot_general(a, b, (((1,), (1,)), ((), ())), preferred_element_type=F32)


def _dot_tn(a, b):
    return lax.dot_general(a, b, (((0,), (0,)), ((), ())), preferred_element_type=F32)


def _rowmean(v):
    return jnp.mean(v, axis=-1, keepdims=True)


def _colsum(v):
    return jnp.sum(v, axis=0, keepdims=True)


def _gelu_parts(z):
    k0 = 0.7978845608028654
    k1 = 0.044715
    z2 = z * z
    t = jnp.tanh(k0 * (z + k1 * z * z2))
    g = 0.5 * z * (1.0 + t)
    dg = 0.5 * (1.0 + t) + 0.5 * z * (1.0 - t * t) * (k0 * (1.0 + 3.0 * k1 * z2))
    return g, dg


def _tril_weights(ws_ref):
    r = lax.broadcasted_iota(jnp.int32, (CHUNK, CHUNK), 0)
    s = lax.broadcasted_iota(jnp.int32, (CHUNK, CHUNK), 1)
    mask = (s <= r).astype(F32)
    return [(ws_ref[h] * mask).astype(BF16) for h in range(N_HEADS)]


def _window_counts(first_row, n_rows):
    pos = (first_row + lax.broadcasted_iota(jnp.int32, (n_rows, 1), 0)).astype(F32)
    return pos, [1.0 / jnp.minimum(pos + 1.0, float(w)) for w in POOL_WINDOWS]


def _causal_window_sums(ext):
    out = []
    e = ext
    shift = 1
    for g in range(len(POOL_WINDOWS)):
        e = e + pltpu.roll(e, shift, 0)
        shift *= 2
        out.append(e[:, g * GROUP_DIM:(g + 1) * GROUP_DIM])
    return out


def _anticausal_window_sums(ext):
    n = ext.shape[0]
    out = []
    e = ext
    shift = 1
    for g in range(len(POOL_WINDOWS)):
        e = e + pltpu.roll(e, n - shift, 0)
        shift *= 2
        out.append(e[:, g * GROUP_DIM:(g + 1) * GROUP_DIM])
    return out


def _fwd_mix(x, mod6, n1pre, n1post, n2pre, win_g, wout_g, w_spatial, bsp_full, gain, bias, w_pool, b_pool, pool_scale, ts):
    s_len = x.shape[0]
    rs = D_MODEL // N_CHIPS

    def body(x_ref, mod_ref, g1pre_ref, g1post_ref, g2pre_ref, win_ref, wout_ref, ws_ref, bsp_ref, gain_ref,
             bias_ref, wp_ref, bp_ref, ps_ref, z_ref, y_ref, mix_ref, x1_ref, h2_ref, mixed_ref, prev_ref, wfull_ref):
        i = pl.program_id(0)
        _zero_on_first_step(prev_ref)
        _join_w_in_on_first_step(win_ref, wfull_ref)
        xv = x_ref[...]
        r = lax.rsqrt(_rowmean(xv * xv) + EPS)
        hb = (((xv * r) * g1pre_ref[...]) * (1.0 + mod_ref[1:2, :]) + mod_ref[0:1, :]).astype(BF16)
        z_ref[...] = _dot(hb, wfull_ref[...])

        wc = _tril_weights(ws_ref)
        u, _, _, _, _ = _mixer_forward_tile(z_ref[:, :2 * D_A], wc, bsp_ref, gain_ref[...], bias_ref[...], mixed_ref)
        y_ref[:, :D_A] = (u * mixed_ref[...]).astype(BF16)
        zb = z_ref[:, 2 * D_A:]
        sums = _causal_window_sums(jnp.concatenate([prev_ref[...], zb], axis=0))
        prev_ref[...] = zb[ts - HALO:, :]
        _, inv_counts = _window_counts(i * ts, ts)
        for g in range(len(POOL_WINDOWS)):
            lanes = slice(g * GROUP_DIM, (g + 1) * GROUP_DIM)
            diff = sums[g][HALO:, :] * inv_counts[g] - zb[:, lanes]
            lin = _dot(diff.astype(BF16), wp_ref[g].astype(BF16)) + bp_ref[:, lanes]
            y_ref[:, D_A + g * GROUP_DIM:D_A + (g + 1) * GROUP_DIM] = (lin * ps_ref[:, lanes]).astype(BF16)

        mix = None
        for j in range(N_CHIPS):
            part = _dot(y_ref[:, j * rs:(j + 1) * rs], wout_ref[j])
            mix = part if mix is None else mix + part
        mix_ref[...] = mix
        r2 = lax.rsqrt(_rowmean(mix * mix) + EPS)
        x1 = xv + mod_ref[2:3, :] * ((mix * r2) * g1post_ref[...])
        x1_ref[...] = x1
        r3 = lax.rsqrt(_rowmean(x1 * x1) + EPS)
        h2_ref[...] = (((x1 * r3) * g2pre_ref[...]) * (1.0 + mod_ref[4:5, :]) + mod_ref[3:4, :]).astype(BF16)

    vec = _const((1, D_MODEL))
    f32_rows = jax.ShapeDtypeStruct((s_len, D_MODEL), F32)
    bf16_rows = jax.ShapeDtypeStruct((s_len, D_MODEL), BF16)
    return pl.pallas_call(
        body, name="fwd_mix", grid=(s_len // ts,),
        in_specs=[_rows(ts, D_MODEL), _const((N_MOD, D_MODEL)), vec, vec, vec, _VMEM, _VMEM,
                  _const((N_HEADS, CHUNK, CHUNK)), _const((CHUNK, D_A)), _const((1, D_A)), _const((1, D_A)),
                  _const((N_HEADS, GROUP_DIM, GROUP_DIM)), _const((1, D_B)), _const((1, D_B))],
        out_specs=[_rows(ts, D_Z), _rows(ts, D_MODEL), _rows(ts, D_MODEL), _rows(ts, D_MODEL), _rows(ts, D_MODEL)],
        out_shape=[jax.ShapeDtypeStruct((s_len, D_Z), F32), bf16_rows, f32_rows, f32_rows, bf16_rows],
        scratch_shapes=[pltpu.VMEM((ts, D_A), F32), pltpu.VMEM((HALO, D_B), F32), pltpu.VMEM((D_MODEL, D_Z), BF16)],
        compiler_params=_params(),
    )(x, mod6, n1pre, n1post, n2pre, win_g, wout_g, w_spatial, bsp_full, gain, bias, w_pool, b_pool, pool_scale)


def _mixer_forward_tile(za, wc, bsp_ref, gain, bias, mixed_ref):
    ga, dga = _gelu_parts(za)
    u = ga[:, :D_A]
    v = ga[:, D_A:]
    mu = _rowmean(v)
    vc = v - mu
    rstd = lax.rsqrt(_rowmean(vc * vc) + EPS)
    vhat = vc * rstd
    vn = (vhat * gain + bias).astype(BF16)
    ts = za.shape[0]
    for k in range(ts // CHUNK):
        for h in range(N_HEADS):
            blk = vn[k * CHUNK:(k + 1) * CHUNK, h * HEAD_DIM:(h + 1) * HEAD_DIM]
            mixed_ref[k * CHUNK:(k + 1) * CHUNK, h * HEAD_DIM:(h + 1) * HEAD_DIM] = (
                _dot(wc[h], blk) + bsp_ref[:, h * HEAD_DIM:(h + 1) * HEAD_DIM])
    return u, vhat, rstd, vn, dga


def _fwd_fc1(h2, fc1_g, ts):
    s_len = h2.shape[0]
    cs = D_FF // N_CHIPS

    def body(h_ref, w_ref, q_ref):
        hb = h_ref[...]
        for j in range(N_CHIPS):
            p = jnp.maximum(_dot(hb, w_ref[j]), 0.0)
            q_ref[:, j * cs:(j + 1) * cs] = (p * p).astype(BF16)

    return pl.pallas_call(
        body, name="fwd_fc1", grid=(s_len // ts,),
        in_specs=[_rows(ts, D_MODEL), _VMEM],
        out_specs=_rows(ts, D_FF),
        out_shape=jax.ShapeDtypeStruct((s_len, D_FF), BF16),
        compiler_params=_params(),
    )(h2, fc1_g)


def _fwd_fc2_loss(q, x1, target, mod6, n2post, fc2_g, ts):
    s_len = q.shape[0]
    rs = D_FF // N_CHIPS

    def body(q_ref, x1_ref, t_ref, mod_ref, g_ref, w_ref, dy_ref, df_ref, loss_ref, dgate_ref, dg_ref):
        _zero_on_first_step(loss_ref, dgate_ref, dg_ref)
        gate = mod_ref[5:6, :]
        gn = g_ref[...]
        f = _dot(q_ref[:, 0:rs], w_ref[0])
        for j in range(1, N_CHIPS):
            f = f + _dot(q_ref[:, j * rs:(j + 1) * rs], w_ref[j])
        r4 = lax.rsqrt(_rowmean(f * f) + EPS)
        fh = f * r4
        err = (x1_ref[...] + gate * (fh * gn)) - t_ref[...]
        loss_ref[...] += 0.5 * jnp.sum(_rowmean(err * err), axis=0, keepdims=True)
        dy = err * (1.0 / D_MODEL)
        dy_ref[...] = dy
        dgate_ref[...] += _colsum(dy * (fh * gn))
        dg_ref[...] += _colsum((dy * gate) * fh)
        gh = (dy * gate) * gn
        df_ref[...] = (r4 * (gh - fh * _rowmean(gh * fh))).astype(BF16)

    return pl.pallas_call(
        body, name="fwd_fc2_loss", grid=(s_len // ts,),
        in_specs=[_rows(ts, D_FF), _rows(ts, D_MODEL), _rows(ts, D_MODEL), _const((N_MOD, D_MODEL)),
                  _const((1, D_MODEL)), _VMEM],
        out_specs=[_rows(ts, D_MODEL), _rows(ts, D_MODEL), _const((1, 1)), _const((1, D_MODEL)), _const((1, D_MODEL))],
        out_shape=[jax.ShapeDtypeStruct((s_len, D_MODEL), F32), jax.ShapeDtypeStruct((s_len, D_MODEL), BF16),
                   jax.ShapeDtypeStruct((1, 1), F32), jax.ShapeDtypeStruct((1, D_MODEL), F32),
                   jax.ShapeDtypeStruct((1, D_MODEL), F32)],
        compiler_params=_params(),
    )(q, x1, target, mod6, n2post, fc2_g)


def _join_w_in_on_first_step(win_ref, full_ref):
    cs = D_Z // N_CHIPS

    @pl.when(pl.program_id(0) == 0)
    def _():
        for j in range(N_CHIPS):
            full_ref[:, j * cs:(j + 1) * cs] = win_ref[j]


def _zero_on_first_step(*refs):
    @pl.when(pl.program_id(0) == 0)
    def _():
        for ref in refs:
            ref[...] = jnp.zeros_like(ref)


def _on_last_step(fn):
    pl.when(pl.program_id(0) == pl.num_programs(0) - 1)(fn)


def _store_shard_on_last_step(acc_ref, hbm_ref, sem, j):
    _on_last_step(lambda: pltpu.make_async_copy(acc_ref.at[j], hbm_ref.at[j], sem.at[j]).start())


def _wait_stores_on_last_step(*stores):
    def wait_all():
        for acc_ref, hbm_ref, sem in stores:
            for j in range(N_CHIPS):
                pltpu.make_async_copy(acc_ref.at[j], hbm_ref.at[j], sem.at[j]).wait()

    _on_last_step(wait_all)


def _bwd_fc2(df, q, fc2_g, ts):
    s_len = df.shape[0]
    cs = D_FF // N_CHIPS

    def body(df_ref, q_ref, w_ref, dp_ref, dw_hbm, dw_ref, dw_sem):
        _zero_on_first_step(dw_ref)
        dfb = df_ref[...]
        for j in range(N_CHIPS):
            qb = q_ref[:, j * cs:(j + 1) * cs]
            dw_ref[j] += _dot_tn(qb, dfb).reshape(2, cs // 2, D_MODEL)
            _store_shard_on_last_step(dw_ref, dw_hbm, dw_sem, j)
            dq = _dot_nt(dfb, w_ref[j])
            dp_ref[:, j * cs:(j + 1) * cs] = (dq * (2.0 * jnp.sqrt(qb.astype(F32)))).astype(BF16)
        _wait_stores_on_last_step((dw_ref, dw_hbm, dw_sem))

    dw_shape = (N_CHIPS, 2, cs // 2, D_MODEL)
    return pl.pallas_call(
        body, name="bwd_fc2", grid=(s_len // ts,),
        in_specs=[_rows(ts, D_MODEL), _rows(ts, D_FF), _VMEM],
        out_specs=[_rows(ts, D_FF), _ANY],
        out_shape=[jax.ShapeDtypeStruct((s_len, D_FF), BF16), jax.ShapeDtypeStruct(dw_shape, F32)],
        scratch_shapes=[pltpu.VMEM(dw_shape, F32), pltpu.SemaphoreType.DMA((N_CHIPS,))],
        compiler_params=_params(),
    )(df, q, fc2_g)


def _bwd_fc1(dp, dy, x1, mix, h2, mod6, n2pre, n1post, fc1_g, dep, ts):
    s_len = dp.shape[0]
    cs = D_FF // N_CHIPS

    def body(dp_ref, dy_ref, x1_ref, mix_ref, h2_ref, mod_ref, g2_ref, g1_ref, w1_ref, dep_ref,
             dx1_ref, dmix_ref, dshift2_ref, da2_ref, dgate1_ref, dg1_ref, dw1_hbm, dw1_ref, dw1_sem):
        _zero_on_first_step(dshift2_ref, da2_ref, dgate1_ref, dg1_ref, dw1_ref)
        h2b = h2_ref[...]
        dh2 = None
        for j in range(N_CHIPS):
            dpb = dp_ref[:, j * cs:(j + 1) * cs]
            dw1_ref[j] += _dot_tn(h2b, dpb).reshape(2, D_MODEL // 2, cs)
            _store_shard_on_last_step(dw1_ref, dw1_hbm, dw1_sem, j)
            part = _dot_nt(dpb, w1_ref[j])
            dh2 = part if dh2 is None else dh2 + part
        x1 = x1_ref[...]
        r3 = lax.rsqrt(_rowmean(x1 * x1) + EPS)
        xh = x1 * r3
        a2 = g2_ref[...] * (1.0 + mod_ref[4:5, :])
        dshift2_ref[...] += _colsum(dh2)
        da2_ref[...] += _colsum(dh2 * xh)
        dxh = dh2 * a2
        dx1 = dy_ref[...] + r3 * (dxh - xh * _rowmean(dxh * xh))
        dx1_ref[...] = dx1

        mix = mix_ref[...]
        r2 = lax.rsqrt(_rowmean(mix * mix) + EPS)
        mh = mix * r2
        gate = mod_ref[2:3, :]
        gn = g1_ref[...]
        dgate1_ref[...] += _colsum(dx1 * (mh * gn))
        dg1_ref[...] += _colsum((dx1 * gate) * mh)
        gh = (dx1 * gate) * gn
        dmix_ref[...] = (r2 * (gh - mh * _rowmean(gh * mh))).astype(BF16)
        _wait_stores_on_last_step((dw1_ref, dw1_hbm, dw1_sem))

    vec = jax.ShapeDtypeStruct((1, D_MODEL), F32)
    dw1_shape = (N_CHIPS, 2, D_MODEL // 2, cs)
    return pl.pallas_call(
        body, name="bwd_fc1", grid=(s_len // ts,),
        in_specs=[_rows(ts, D_FF), _rows(ts, D_MODEL), _rows(ts, D_MODEL), _rows(ts, D_MODEL), _rows(ts, D_MODEL),
                  _const((N_MOD, D_MODEL)), _const((1, D_MODEL)), _const((1, D_MODEL)), _VMEM, _ANY],
        out_specs=[_rows(ts, D_MODEL), _rows(ts, D_MODEL)] + [_const((1, D_MODEL))] * 4 + [_ANY],
        out_shape=[jax.ShapeDtypeStruct((s_len, D_MODEL), F32), jax.ShapeDtypeStruct((s_len, D_MODEL), BF16),
                   vec, vec, vec, vec, jax.ShapeDtypeStruct(dw1_shape, F32)],
        scratch_shapes=[pltpu.VMEM(dw1_shape, F32), pltpu.SemaphoreType.DMA((N_CHIPS,))],
        compiler_params=_params(),
    )(dp, dy, x1, mix, h2, mod6, n2pre, n1post, fc1_g, dep)


def _mixer_bwd(z, dmix, ycat, wout_g, w_spatial, bsp_full, gain, bias, w_pool, b_pool, pool_scale, dep, ts):
    s_len = z.shape[0]
    nb = ts // HALO
    last = s_len // HALO - 1
    te = ts + HALO
    rs = D_MODEL // N_CHIPS

    def body(z_ref, zprev_ref, znext_ref, dmix_ref, dmnext_ref, yc_ref, wo_ref, ws_ref, bsp_ref, gain_ref, bias_ref,
             wp_ref, bp_ref, ps_ref, dep_ref, dz_ref, dws_ref, dbsp_ref, dgain_ref, dbias_ref, dwp_ref, dbp_ref, dps_ref,
             dwo_hbm, mixed_ref, dvn_ref, dyc_ref, dwo_ref, dwo_sem):
        i = pl.program_id(0)
        _zero_on_first_step(dws_ref, dbsp_ref, dgain_ref, dbias_ref, dwp_ref, dbp_ref, dps_ref, dwo_ref)

        dmix = dmix_ref[...]
        dwo_ref[...] += _dot_tn(yc_ref[...], dmix).reshape(N_CHIPS, 2, rs // 2, D_MODEL)
        for j in range(N_CHIPS):
            dyc_ref[:, j * rs:(j + 1) * rs] = _dot_nt(dmix, wo_ref[j])
        dmnext = dmnext_ref[...]
        dynext = jnp.concatenate([_dot_nt(dmnext, wo_ref[j]) for j in range(D_A // rs, N_CHIPS)], axis=1)

        wc = _tril_weights(ws_ref)
        gain = gain_ref[...]
        u, vhat, rstd, vn, dga = _mixer_forward_tile(z_ref[:, :2 * D_A], wc, bsp_ref, gain, bias_ref[...], mixed_ref)
        dya = dyc_ref[:, :D_A]
        du = dya * mixed_ref[...]
        dmixed = dya * u
        dmb = dmixed.astype(BF16)
        dm_sum = dmixed[0:CHUNK, :]
        for k in range(1, ts // CHUNK):
            dm_sum = dm_sum + dmixed[k * CHUNK:(k + 1) * CHUNK, :]
        r_idx = lax.broadcasted_iota(jnp.int32, (CHUNK, CHUNK), 0)
        s_idx = lax.broadcasted_iota(jnp.int32, (CHUNK, CHUNK), 1)
        causal = (s_idx <= r_idx).astype(F32)
        for h in range(N_HEADS):
            lanes = slice(h * HEAD_DIM, (h + 1) * HEAD_DIM)
            dbsp_ref[h] += jnp.sum(dm_sum[:, lanes], axis=1, keepdims=True)
            acc = None
            for k in range(ts // CHUNK):
                rows = slice(k * CHUNK, (k + 1) * CHUNK)
                t = _dot_nt(dmb[rows, lanes], vn[rows, lanes])
                acc = t if acc is None else acc + t
                dvn_ref[rows, lanes] = _dot_tn(wc[h], dmb[rows, lanes])
            dws_ref[h] += acc * causal
        dvn = dvn_ref[...]
        dgain_ref[...] += _colsum(dvn * vhat)
        dbias_ref[...] += _colsum(dvn)
        dvh = dvn * gain
        dv = rstd * (dvh - _rowmean(dvh) - vhat * _rowmean(dvh * vhat))
        dz_ref[:, :D_A] = (du * dga[:, :D_A]).astype(BF16)
        dz_ref[:, D_A:2 * D_A] = (dv * dga[:, D_A:]).astype(BF16)

        zb = z_ref[:, 2 * D_A:]
        prev = jnp.where(i == 0, 0.0, zprev_ref[...])
        zb_ext = jnp.concatenate([zb, znext_ref[...]], axis=0)
        sums = _causal_window_sums(jnp.concatenate([prev, zb_ext], axis=0))
        pos, inv_counts = _window_counts(i * ts, te)
        dyb_ext = jnp.concatenate([dyc_ref[:, D_A:], dynext], axis=0)
        dlin_ext = dyb_ext * ps_ref[...]
        dbp_ref[...] += _colsum(dlin_ext[:ts, :])
        scaled = []
        ddiffs = []
        lins = []
        for g in range(len(POOL_WINDOWS)):
            lanes = slice(g * GROUP_DIM, (g + 1) * GROUP_DIM)
            diff = (sums[g][HALO:, :] * inv_counts[g] - zb_ext[:, lanes]).astype(BF16)
            wpb = wp_ref[g].astype(BF16)
            dlb = dlin_ext[:, lanes].astype(BF16)
            lins.append(_dot(diff[:ts, :], wpb) + bp_ref[:, lanes])
            dwp_ref[g] += _dot_tn(diff[:ts, :], dlb[:ts, :])
            dd = _dot_nt(dlb, wpb)
            ddiffs.append(dd)
            scaled.append(jnp.where(pos < float(s_len), dd * inv_counts[g], 0.0))
        dps_ref[...] += _colsum(dyb_ext[:ts, :] * jnp.concatenate(lins, axis=1))
        back = _anticausal_window_sums(jnp.concatenate(scaled, axis=1))
        for g in range(len(POOL_WINDOWS)):
            dz_ref[:, 2 * D_A + g * GROUP_DIM:2 * D_A + (g + 1) * GROUP_DIM] = (
                back[g][:ts, :] - ddiffs[g][:ts, :]).astype(BF16)
        for j in range(N_CHIPS):
            _store_shard_on_last_step(dwo_ref, dwo_hbm, dwo_sem, j)
        _wait_stores_on_last_step((dwo_ref, dwo_hbm, dwo_sem))

    sq = jax.ShapeDtypeStruct((N_HEADS, CHUNK, CHUNK), F32)
    vec = jax.ShapeDtypeStruct((1, D_A), F32)
    dwo_shape = (N_CHIPS, 2, rs // 2, D_MODEL)
    return pl.pallas_call(
        body, name="mixer_bwd", grid=(s_len // ts,),
        in_specs=[_rows(ts, D_Z),
                  pl.BlockSpec((HALO, D_B), lambda i: (jnp.maximum(i * nb - 1, 0), 2)),
                  pl.BlockSpec((HALO, D_B), lambda i: (jnp.minimum((i + 1) * nb, last), 2)),
                  _rows(ts, D_MODEL),
                  pl.BlockSpec((HALO, D_MODEL), lambda i: (jnp.minimum((i + 1) * nb, last), 0)),
                  _rows(ts, D_MODEL), _VMEM,
                  _const((N_HEADS, CHUNK, CHUNK)), _const((CHUNK, D_A)), _const((1, D_A)), _const((1, D_A)),
                  _const((N_HEADS, GROUP_DIM, GROUP_DIM)), _const((1, D_B)), _const((1, D_B)), _ANY],
        out_specs=[_rows(ts, D_Z), _const((N_HEADS, CHUNK, CHUNK)), _const((N_HEADS, CHUNK, 1)), _const((1, D_A)),
                   _const((1, D_A)), _const((N_HEADS, GROUP_DIM, GROUP_DIM)), _const((1, D_B)), _const((1, D_B)), _ANY],
        out_shape=[jax.ShapeDtypeStruct((s_len, D_Z), BF16), sq, jax.ShapeDtypeStruct((N_HEADS, CHUNK, 1), F32), vec,
                   vec, sq, vec, vec, jax.ShapeDtypeStruct(dwo_shape, F32)],
        scratch_shapes=[pltpu.VMEM((ts, D_A), F32), pltpu.VMEM((ts, D_A), F32), pltpu.VMEM((ts, D_MODEL), F32),
                        pltpu.VMEM(dwo_shape, F32), pltpu.SemaphoreType.DMA((N_CHIPS,))],
        compiler_params=_params(),
    )(z, z, z, dmix, dmix, ycat, wout_g, w_spatial, bsp_full, gain, bias, w_pool, b_pool, pool_scale, dep)


def _bwd_in(dz, dx1, x, mod6, n1pre, win_g, dep, ts):
    s_len = x.shape[0]
    cs = D_Z // N_CHIPS

    def body(dz_ref, dx1_ref, x_ref, mod_ref, g_ref, w_ref, dep_ref, gx_ref, dshift_ref, da_ref, dw_hbm, dw_ref,
             wfull_ref, dw_sem):
        _zero_on_first_step(dshift_ref, da_ref, dw_ref)
        _join_w_in_on_first_step(w_ref, wfull_ref)
        xv = x_ref[...]
        r = lax.rsqrt(_rowmean(xv * xv) + EPS)
        xh = xv * r
        h1b = ((xh * g_ref[...]) * (1.0 + mod_ref[1:2, :]) + mod_ref[0:1, :]).astype(BF16)
        dzb = dz_ref[...]
        dw = _dot_tn(h1b, dzb)
        for j in range(N_CHIPS):
            dw_ref[j] += dw[:, j * cs:(j + 1) * cs].reshape(2, D_MODEL // 2, cs)
        dh = _dot_nt(dzb, wfull_ref[...])
        a1 = g_ref[...] * (1.0 + mod_ref[1:2, :])
        dshift_ref[...] += _colsum(dh)
        da_ref[...] += _colsum(dh * xh)
        dxh = dh * a1
        gx_ref[...] = dx1_ref[...] + r * (dxh - xh * _rowmean(dxh * xh))
        for j in range(N_CHIPS):
            _store_shard_on_last_step(dw_ref, dw_hbm, dw_sem, j)
        _wait_stores_on_last_step((dw_ref, dw_hbm, dw_sem))

    vec = jax.ShapeDtypeStruct((1, D_MODEL), F32)
    dw_shape = (N_CHIPS, 2, D_MODEL // 2, cs)
    return pl.pallas_call(
        body, name="bwd_in", grid=(s_len // ts,),
        in_specs=[_rows(ts, D_Z), _rows(ts, D_MODEL), _rows(ts, D_MODEL), _const((N_MOD, D_MODEL)),
                  _const((1, D_MODEL)), _VMEM, _ANY],
        out_specs=[_rows(ts, D_MODEL), _const((1, D_MODEL)), _const((1, D_MODEL)), _ANY],
        out_shape=[jax.ShapeDtypeStruct((s_len, D_MODEL), F32), vec, vec, jax.ShapeDtypeStruct(dw_shape, F32)],
        scratch_shapes=[pltpu.VMEM(dw_shape, F32), pltpu.VMEM((D_MODEL, D_Z), BF16),
                        pltpu.SemaphoreType.DMA((N_CHIPS,))],
        compiler_params=_params(),
    )(dz, dx1, x, mod6, n1pre, win_g, dep)


def _adamw_math(w, g, m, v):
    m = ADAM_B1 * m + (1.0 - ADAM_B1) * g
    v = ADAM_B2 * v + (1.0 - ADAM_B2) * (g * g)
    m_hat = m / (1.0 - ADAM_B1 ** ADAM_STEP)
    v_hat = v / (1.0 - ADAM_B2 ** ADAM_STEP)
    delta = -ADAM_LR * (m_hat / (jnp.sqrt(v_hat) + ADAM_EPS) + ADAM_WD * w)
    return delta, m, v


def _adamw(g, w, m, v, name, tr):
    rows, cols = w.shape

    def body(g_ref, w_ref, m_ref, v_ref, d_ref, nm_ref, nv_ref):
        d, nm, nv = _adamw_math(w_ref[...], g_ref[...], m_ref[...], v_ref[...])
        d_ref[...] = d
        nm_ref[...] = nm
        nv_ref[...] = nv

    spec = _rows(tr, cols)
    shape = jax.ShapeDtypeStruct((rows, cols), F32)
    return pl.pallas_call(
        body, name=name, grid=(rows // tr,), in_specs=[spec] * 4, out_specs=[spec] * 3, out_shape=[shape] * 3,
        compiler_params=_params(),
    )(g, w, m, v)


def _ada_grad_adamw(sc_t, dmod_shard, w, m, v, tr):
    rows, cols = w.shape

    def body(s_ref, dm_ref, w_ref, m_ref, v_ref, g_ref, d_ref, nm_ref, nv_ref):
        g = s_ref[:, 0:1] * dm_ref[0:1, :]
        for b in range(1, N_DEV):
            g = g + s_ref[:, b:b + 1] * dm_ref[b:b + 1, :]
        g_ref[...] = g
        d, nm, nv = _adamw_math(w_ref[...], g, m_ref[...], v_ref[...])
        d_ref[...] = d
        nm_ref[...] = nm
        nv_ref[...] = nv

    spec = _rows(tr, cols)
    shape = jax.ShapeDtypeStruct((rows, cols), F32)
    return pl.pallas_call(
        body, name="ada_grad_adamw", grid=(rows // tr,),
        in_specs=[_rows(tr, N_DEV), _const((N_DEV, cols)), spec, spec, spec],
        out_specs=[spec] * 4, out_shape=[shape] * 4, compiler_params=_params(),
    )(sc_t, dmod_shard, w, m, v)


def _mod_grads(da1, dshift1, dgate1, dg1post, da2, dshift2, dgate2, dg2post, mod6, n1pre, n2pre):
    def body(da1_ref, ds1_ref, dgt1_ref, dg1_ref, da2_ref, ds2_ref, dgt2_ref, dg2_ref, mod_ref, n1_ref, n2_ref,
             dmod_ref, dn_ref):
        dmod_ref[0:1, :] = ds1_ref[...]
        dmod_ref[1:2, :] = da1_ref[...] * n1_ref[...]
        dmod_ref[2:3, :] = dgt1_ref[...]
        dmod_ref[3:4, :] = ds2_ref[...]
        dmod_ref[4:5, :] = da2_ref[...] * n2_ref[...]
        dmod_ref[5:6, :] = dgt2_ref[...]
        dn_ref[0:1, :] = da1_ref[...] * (1.0 + mod_ref[1:2, :])
        dn_ref[1:2, :] = dg1_ref[...]
        dn_ref[2:3, :] = da2_ref[...] * (1.0 + mod_ref[4:5, :])
        dn_ref[3:4, :] = dg2_ref[...]

    return pl.pallas_call(
        body, name="mod_grads",
        out_shape=[jax.ShapeDtypeStruct((N_MOD, D_MODEL), F32), jax.ShapeDtypeStruct((4, D_MODEL), F32)],
    )(da1, dshift1, dgate1, dg1post, da2, dshift2, dgate2, dg2post, mod6, n1pre, n2pre)


def _position():
    x, y, c = lax.axis_index("x"), lax.axis_index("y"), lax.axis_index("c")
    return x, y, c


def _flip(v, bit):
    return 1 - v if bit else v


def _peer(x, y, c, k):
    return (_flip(x, k & 4), _flip(y, k & 2), _flip(c, k & 1))


def _remote(src, dst, send_sem, recv_sem, device):
    return pltpu.make_async_remote_copy(src_ref=src, dst_ref=dst, send_sem=send_sem, recv_sem=recv_sem,
                                        device_id=device, device_id_type=MESH)


def _cast_to_slot(w, pos, dep, name, tr):
    rows, cols = w.shape

    def body(pos_ref, w_ref, dep_ref, o_ref):
        o_ref[0] = w_ref[...].astype(BF16)

    return pl.pallas_call(
        body, name=name,
        grid_spec=pltpu.PrefetchScalarGridSpec(
            num_scalar_prefetch=1, grid=(rows // tr,),
            in_specs=[pl.BlockSpec((tr, cols), lambda i, pos: (i, 0)), _ANY],
            out_specs=pl.BlockSpec((1, tr, cols), lambda i, pos: (pos[1], i, 0))),
        out_shape=jax.ShapeDtypeStruct((N_CHIPS, rows, cols), BF16), compiler_params=_params(),
    )(pos, w, dep)


def _mod_exchange(c_row, w_ada_shard, b_ada_row):
    cs = w_ada_shard.shape[1]

    def body(c_ref, w_hbm, b_ref, mod_ref, sc_ref, rows_ref, w_ref, w_sem, send1, recv1, send2, recv2):
        x, y, c = _position()
        me = 4 * x + 2 * y + c
        chip = 2 * x + y
        w_load = pltpu.make_async_copy(w_hbm, w_ref, w_sem)
        w_load.start()
        cv = c_ref[...]
        sc_ref[me] = cv * jax.nn.sigmoid(cv)
        gather = [_remote(sc_ref.at[me], sc_ref.at[me], send1.at[k - 1], recv1.at[k - 1], _peer(x, y, c, k))
                  for k in range(1, N_DEV)]
        for cp in gather:
            cp.start()
        for k in range(1, N_DEV):
            px, py, pc = _peer(x, y, c, k)
            src = 4 * px + 2 * py + pc
            _remote(sc_ref.at[src], sc_ref.at[src], send1.at[k - 1], recv1.at[k - 1], (px, py, pc)).wait_recv()
        for cp in gather:
            cp.wait_send()
        sc_all = jnp.concatenate([sc_ref[b] for b in range(N_DEV)], axis=0)
        w_load.wait()
        part = _dot(sc_all.astype(BF16), w_ref[...].astype(BF16))
        part = part + b_ref[:, pl.ds(pl.multiple_of(chip * cs, LANES), cs)]
        for b in range(N_DEV):
            rows_ref[b] = part[b:b + 1, :]
        mod_ref[chip] = rows_ref[me]
        hand = []
        for k in (2, 4, 6):
            px, py, _ = _peer(x, y, c, k)
            hand.append(_remote(rows_ref.at[4 * px + 2 * py + c], mod_ref.at[chip], send2.at[k // 2 - 1],
                                recv2.at[k // 2 - 1], (px, py, c)))
        for cp in hand:
            cp.start()
        for k in (2, 4, 6):
            px, py, _ = _peer(x, y, c, k)
            pchip = 2 * px + py
            _remote(rows_ref.at[me], mod_ref.at[pchip], send2.at[k // 2 - 1], recv2.at[k // 2 - 1],
                    (px, py, c)).wait_recv()
        for cp in hand:
            cp.wait_send()

    return pl.pallas_call(
        body, name="mod_exchange",
        in_specs=[_VMEM, _ANY, _VMEM], out_specs=[_VMEM, _VMEM],
        out_shape=[jax.ShapeDtypeStruct((N_CHIPS, 1, cs), F32), jax.ShapeDtypeStruct((N_DEV, 1, D_MODEL), F32)],
        scratch_shapes=[pltpu.VMEM((N_DEV, 1, cs), F32), pltpu.VMEM(w_ada_shard.shape, F32), pltpu.SemaphoreType.DMA,
                        pltpu.SemaphoreType.DMA((N_DEV - 1,)),
                        pltpu.SemaphoreType.DMA((N_DEV - 1,)), pltpu.SemaphoreType.DMA((N_CHIPS - 1,)),
                        pltpu.SemaphoreType.DMA((N_CHIPS - 1,))],
        compiler_params=pltpu.CompilerParams(vmem_limit_bytes=VMEM_LIMIT),
    )(c_row, w_ada_shard, b_ada_row)


_HBM = pl.BlockSpec(memory_space=pltpu.HBM)
_SEM = pl.BlockSpec(memory_space=pltpu.SEMAPHORE)
_EFFECT = pltpu.SideEffectType.DATAFLOW_SIDE_EFFECTING
_CHIP_HOPS = (2, 4, 6)


def _in_hbm(a):
    return pltpu.with_memory_space_constraint(a, pltpu.HBM)


def _sems3():
    return pltpu.SemaphoreType.DMA((len(_CHIP_HOPS),))


def _ag_start(lands, after, name):
    n = len(lands)

    def body(*refs):
        zones = refs[:n]
        sends, recvs = refs[n + 1:2 * n + 1], refs[2 * n + 1:3 * n + 1]
        x, y, c = _position()
        chip = 2 * x + y
        for i in range(n):
            half = zones[i].shape[1] // 2
            mine = zones[i].at[chip, pl.ds(c * half, half)]
            for s, k in enumerate(_CHIP_HOPS):
                px, py, _ = _peer(x, y, c, k)
                _remote(mine, mine, sends[i].at[s], recvs[i].at[s], (px, py, c)).start()

    out = pl.pallas_call(
        body, name=name,
        in_specs=[_HBM] * n + [_ANY],
        out_specs=[_SEM] * (2 * n) + [_HBM] * n,
        out_shape=[_sems3()] * (2 * n) + [pltpu.HBM(z.shape, BF16) for z in lands],
        input_output_aliases={i: 2 * n + i for i in range(n)},
        compiler_params=pltpu.CompilerParams(has_side_effects=_EFFECT),
    )(*[_in_hbm(z) for z in lands], after)
    return [(out[2 * n + i], out[i], out[n + i]) for i in range(n)]


def _ag_pass(group, after, name):
    n = len(group)

    def body(*refs):
        zones = refs[:n]
        sends, recvs = refs[n:2 * n], refs[2 * n:3 * n]
        fsends, frecvs = refs[4 * n + 1:5 * n + 1], refs[5 * n + 1:6 * n + 1]
        x, y, c = _position()
        chip = 2 * x + y
        for i in range(n):
            half = zones[i].shape[1] // 2
            rows = pl.ds(c * half, half)
            for s, k in enumerate(_CHIP_HOPS):
                px, py, _ = _peer(x, y, c, k)
                landed = zones[i].at[2 * px + py, rows]
                _remote(landed, landed, sends[i].at[s], recvs[i].at[s], (px, py, c)).wait_recv()
                _remote(landed, landed, fsends[i].at[s], frecvs[i].at[s], (x, y, 1 - c)).start()
        for i in range(n):
            half = zones[i].shape[1] // 2
            mine = zones[i].at[chip, pl.ds(c * half, half)]
            for s, k in enumerate(_CHIP_HOPS):
                px, py, _ = _peer(x, y, c, k)
                _remote(mine, mine, sends[i].at[s], recvs[i].at[s], (px, py, c)).wait_send()

    out = pl.pallas_call(
        body, name=name,
        in_specs=[_HBM] * n + [_SEM] * (2 * n) + [_ANY],
        out_specs=[_HBM] * n + [_SEM] * (2 * n),
        out_shape=[pltpu.HBM(g[0].shape, BF16) for g in group] + [_sems3()] * (2 * n),
        input_output_aliases={i: i for i in range(n)},
        compiler_params=pltpu.CompilerParams(has_side_effects=_EFFECT),
    )(*[g[0] for g in group], *[g[1] for g in group], *[g[2] for g in group], after)
    return [(out[i], out[n + i], out[2 * n + i]) for i in range(n)]


def _ag_done(group, name):
    n = len(group)

    def body(*refs):
        lands = refs[:n]
        fsends, frecvs = refs[n:2 * n], refs[2 * n:3 * n]
        x, y, c = _position()
        for i in range(n):
            half = lands[i].shape[1] // 2
            for s, k in enumerate(_CHIP_HOPS):
                px, py, _ = _peer(x, y, c, k)
                sent = lands[i].at[2 * px + py, pl.ds(c * half, half)]
                got = lands[i].at[2 * px + py, pl.ds((1 - c) * half, half)]
                cp = _remote(sent, got, fsends[i].at[s], frecvs[i].at[s], (x, y, 1 - c))
                cp.wait_recv()
                cp.wait_send()

    out = pl.pallas_call(
        body, name=name,
        in_specs=[_HBM] * n + [_SEM] * (2 * n),
        out_specs=[_HBM] * n,
        out_shape=[pltpu.HBM(g[0].shape, BF16) for g in group],
        input_output_aliases={i: i for i in range(n)},
        compiler_params=pltpu.CompilerParams(has_side_effects=_EFFECT),
    )(*[g[0] for g in group], *[g[1] for g in group], *[g[2] for g in group])
    return list(out)


def _small_spread_start(slots, after):
    def body(z_ref, after_ref, sends, recvs, z_out):
        x, y, c = _position()
        mine = z_ref.at[2 * x + y]
        for s, k in enumerate(_CHIP_HOPS):
            px, py, _ = _peer(x, y, c, k)
            _remote(mine, mine, sends.at[s], recvs.at[s], (px, py, c)).start()

    sends, recvs, out = pl.pallas_call(
        body, name="small_spread_start",
        in_specs=[_HBM, _ANY], out_specs=[_SEM, _SEM, _HBM],
        out_shape=[_sems3(), _sems3(), pltpu.HBM(slots.shape, F32)],
        input_output_aliases={0: 2},
        compiler_params=pltpu.CompilerParams(has_side_effects=_EFFECT),
    )(_in_hbm(slots), after)
    return out, sends, recvs


def _small_spread_wait(slots, sends, recvs, afters):
    def body(z_ref, sends, recvs, *rest):
        x, y, c = _position()
        mine = z_ref.at[2 * x + y]
        for s, k in enumerate(_CHIP_HOPS):
            px, py, _ = _peer(x, y, c, k)
            cp = _remote(mine, z_ref.at[2 * px + py], sends.at[s], recvs.at[s], (px, py, c))
            cp.wait_recv()
            cp.wait_send()

    return pl.pallas_call(
        body, name="small_spread_wait",
        in_specs=[_HBM, _SEM, _SEM] + [_ANY] * len(afters), out_specs=_HBM, out_shape=pltpu.HBM(slots.shape, F32),
        input_output_aliases={0: 0},
        compiler_params=pltpu.CompilerParams(has_side_effects=_EFFECT),
    )(slots, sends, recvs, *afters)


def _sibling_sum(pgs, name, small=None):
    n = len(pgs)
    k = 0 if small is None else 1
    units = [(i, j) for i in range(n) for j in range(N_CHIPS)]

    def body(*refs):
        refs = list(refs)
        take = lambda count: [refs.pop(0) for _ in range(count)]
        ins, small_in = take(n), take(k)
        qbs, owns, slots_out = take(n), take(n), take(k)
        mine, other, stage, got = take(n), take(n), take(n), take(n)
        load_a, load_b, send, recv = take(4)
        x, y, c = _position()
        chip = 2 * x + y
        if k:
            sib_ref, pair_send, pair_recv = take(3)
            pair = _remote(small_in[0], sib_ref, pair_send, pair_recv, (x, y, 1 - c))
            pair.start()
        loads_a = [pltpu.make_async_copy(ins[i].at[j, 1 - c], other[i].at[j], load_a.at[u])
                   for u, (i, j) in enumerate(units)]
        loads_b = [pltpu.make_async_copy(ins[i].at[j, c], mine[i].at[j], load_b.at[u])
                   for u, (i, j) in enumerate(units)]
        for cp in loads_a + loads_b:
            cp.start()
        sent = []
        for u, (i, j) in enumerate(units):
            loads_a[u].wait()
            stage[i][j] = other[i][j].astype(BF16)
            cp = _remote(stage[i].at[j], got[i].at[j], send.at[u], recv.at[u], (x, y, 1 - c))
            cp.start()
            sent.append(cp)
        for u, (i, j) in enumerate(units):
            loads_b[u].wait()
            sent[u].wait_recv()
            q = mine[i][j] + got[i][j].astype(F32)
            mine[i][j] = q
            qbs[i][j] = q.astype(BF16)
        for i in range(n):
            owns[i][...] = mine[i][chip]
        if k:
            pair.wait()
            slots_out[0][chip] = small_in[0][...] + sib_ref[...]
        for cp in sent:
            cp.wait_send()

    wire = [(N_CHIPS,) + p.shape[2:] for p in pgs]
    extra_out, extra_scratch = [], []
    if k:
        extra_out = [jax.ShapeDtypeStruct((N_CHIPS,) + small.shape, F32)]
        extra_scratch = [pltpu.VMEM(small.shape, F32), pltpu.SemaphoreType.DMA, pltpu.SemaphoreType.DMA]
    out = pl.pallas_call(
        body, name=name, in_specs=[_ANY] * n + [_VMEM] * k, out_specs=[_VMEM] * (2 * n + k),
        out_shape=[jax.ShapeDtypeStruct(w, BF16) for w in wire] + [jax.ShapeDtypeStruct(w[1:], F32) for w in wire]
        + extra_out,
        scratch_shapes=[pltpu.VMEM(w, F32) for w in wire] * 2 + [pltpu.VMEM(w, BF16) for w in wire] * 2
        + [pltpu.SemaphoreType.DMA((len(units),))] * 4 + extra_scratch,
        compiler_params=pltpu.CompilerParams(vmem_limit_bytes=VMEM_LIMIT),
    )(*pgs, *([small] if k else []))
    return list(out[:n]), list(out[n:2 * n]), list(out[2 * n:])


def _rs_start(qbs, name):
    n = len(qbs)

    def body(*refs):
        outs, inboxes = refs[:n], refs[n:2 * n]
        sends, recvs = refs[2 * n:3 * n], refs[3 * n:4 * n]
        x, y, c = _position()
        chip = 2 * x + y
        for i in range(n):
            for s, k in enumerate(_CHIP_HOPS):
                px, py, _ = _peer(x, y, c, k)
                _remote(outs[i].at[2 * px + py], inboxes[i].at[chip], sends[i].at[s], recvs[i].at[s], (px, py, c)).start()

    inboxes = [_in_hbm(lax.empty(q.shape, BF16)) for q in qbs]
    out = pl.pallas_call(
        body, name=name,
        in_specs=[_HBM] * (2 * n),
        out_specs=[_SEM] * (2 * n) + [_HBM] * (2 * n),
        out_shape=[_sems3()] * (2 * n) + [pltpu.HBM(q.shape, BF16) for q in qbs] * 2,
        input_output_aliases={i: 2 * n + i for i in range(2 * n)},
        compiler_params=pltpu.CompilerParams(has_side_effects=_EFFECT),
    )(*[_in_hbm(q) for q in qbs], *inboxes)
    return [(out[2 * n + i], out[3 * n + i], out[i], out[n + i]) for i in range(n)]


def _rs_wait(group, after, name):
    n = len(group)

    def body(*refs):
        outs, inboxes = refs[:n], refs[n:2 * n]
        sends, recvs = refs[2 * n:3 * n], refs[3 * n:4 * n]
        x, y, c = _position()
        for i in range(n):
            for s, k in enumerate(_CHIP_HOPS):
                px, py, _ = _peer(x, y, c, k)
                slot = 2 * px + py
                cp = _remote(outs[i].at[slot], inboxes[i].at[slot], sends[i].at[s], recvs[i].at[s], (px, py, c))
                cp.wait_recv()
                cp.wait_send()

    out = pl.pallas_call(
        body, name=name,
        in_specs=[_HBM] * (2 * n) + [_SEM] * (2 * n) + [_ANY],
        out_specs=[_HBM] * n,
        out_shape=[pltpu.HBM(g[1].shape, BF16) for g in group],
        input_output_aliases={n + i: i for i in range(n)},
        compiler_params=pltpu.CompilerParams(has_side_effects=_EFFECT),
    )(*[g[0] for g in group], *[g[1] for g in group], *[g[2] for g in group], *[g[3] for g in group], after)
    return list(out)


def _final_share(inboxes, owns, name):
    n = len(inboxes)
    units = [(i, s) for i in range(n) for s in range(len(_CHIP_HOPS))]

    def body(*refs):
        ins, mine, outs, landed = refs[:n], refs[n:2 * n], refs[2 * n:3 * n], refs[3 * n:4 * n]
        load, send, recv = refs[4 * n:]
        x, y, c = _position()
        loads = []
        for u, (i, s) in enumerate(units):
            px, py, _ = _peer(x, y, c, _CHIP_HOPS[s])
            loads.append(pltpu.make_async_copy(ins[i].at[2 * px + py], landed[i].at[s], load.at[u]))
        for cp in loads:
            cp.start()
        copies = []
        for i in range(n):
            for s in range(len(_CHIP_HOPS)):
                loads[len(_CHIP_HOPS) * i + s].wait()
            total = (landed[i][0].astype(F32) + landed[i][1].astype(F32)) + landed[i][2].astype(F32)
            outs[i][c] = total + mine[i][...]
            cp = _remote(outs[i].at[c], outs[i].at[c], send.at[i], recv.at[i], (x, y, 1 - c))
            cp.start()
            copies.append(cp)
        for i in range(n):
            theirs = outs[i].at[1 - c]
            _remote(theirs, theirs, send.at[i], recv.at[i], (x, y, 1 - c)).wait_recv()
        for cp in copies:
            cp.wait_send()

    return pl.pallas_call(
        body, name=name, in_specs=[_ANY] * n + [_VMEM] * n, out_specs=[_VMEM] * n,
        out_shape=[jax.ShapeDtypeStruct((2,) + o.shape, F32) for o in owns],
        scratch_shapes=[pltpu.VMEM((len(_CHIP_HOPS),) + o.shape, BF16) for o in owns]
        + [pltpu.SemaphoreType.DMA((len(units),)), pltpu.SemaphoreType.DMA((n,)), pltpu.SemaphoreType.DMA((n,))],
        compiler_params=pltpu.CompilerParams(vmem_limit_bytes=VMEM_LIMIT),
    )(*inboxes, *owns)


_SMALL = (("b_ada", N_MOD * D_MODEL), ("norm1_pre", D_MODEL), ("norm1_post", D_MODEL), ("norm2_pre", D_MODEL),
          ("norm2_post", D_MODEL), ("w_spatial", N_HEADS * CHUNK * CHUNK), ("b_spatial", N_HEADS * CHUNK),
          ("ln_v_gain", D_A), ("ln_v_bias", D_A), ("w_pool", N_HEADS * GROUP_DIM * GROUP_DIM),
          ("b_pool", D_B), ("pool_scale", D_B))
_MOD_ROWS = N_MOD * D_MODEL // LANES


def _packed_rows(size):
    return -(-(size // LANES) // SUBLANES) * SUBLANES


def _pack(parts):
    out = []
    for name, size in _SMALL:
        a = parts[name].reshape(size // LANES, LANES)
        pad = _packed_rows(size) - a.shape[0]
        out.append(jnp.pad(a, ((0, pad), (0, 0))) if pad else a)
    return out


def _small_adamw(slots, ws, ms, vs):
    n = len(_SMALL)
    head = N_DEV * _MOD_ROWS

    def body(*refs):
        s_ref, w, m, v = refs[0], refs[1:1 + n], refs[1 + n:1 + 2 * n], refs[1 + 2 * n:1 + 3 * n]
        outs = refs[1 + 3 * n:1 + 7 * n]
        dmod_ref, loss_ref, t_ref = refs[1 + 7 * n:]
        t_ref[...] = ((s_ref[0] + s_ref[1]) + s_ref[2]) + s_ref[3]
        dmod_ref[...] = t_ref[0:head, :]
        loss_ref[...] = t_ref[t_ref.shape[0] - 8:, :]
        row = head
        for i, (_, size) in enumerate(_SMALL):
            if i == 0:
                g = t_ref[0:_MOD_ROWS, :]
                for b in range(1, N_DEV):
                    g = g + t_ref[b * _MOD_ROWS:(b + 1) * _MOD_ROWS, :]
            else:
                g = t_ref[row:row + size // LANES, :]
                row += _packed_rows(size)
            d, nm, nv = _adamw_math(w[i][...], g, m[i][...], v[i][...])
            for ref, val in zip(outs[4 * i:4 * i + 4], (g, d, nm, nv)):
                ref[...] = val

    each = [jax.ShapeDtypeStruct((size // LANES, LANES), F32) for _, size in _SMALL for _ in range(4)]
    out = pl.pallas_call(
        body, name="small_adamw",
        out_shape=each + [jax.ShapeDtypeStruct((head, LANES), F32), jax.ShapeDtypeStruct((SUBLANES, LANES), F32)],
        scratch_shapes=[pltpu.VMEM(slots.shape[1:], F32)],
        compiler_params=pltpu.CompilerParams(vmem_limit_bytes=VMEM_LIMIT),
    )(slots, *ws, *ms, *vs)
    return [out[4 * i:4 * i + 4] for i in range(n)], out[4 * n], out[4 * n + 1]


def kernel(x, c, w_ada, b_ada, norm1_pre, norm1_post, w_in, w_spatial, b_spatial, ln_v_gain, ln_v_bias, w_pool, b_pool, pool_scale, w_out, norm2_pre, norm2_post, w_fc1, w_fc2, loss_target, m_w_ada, m_b_ada, m_norm1_pre, m_norm1_post, m_w_in, m_w_spatial, m_b_spatial, m_ln_v_gain, m_ln_v_bias, m_w_pool, m_b_pool, m_pool_scale, m_w_out, m_norm2_pre, m_norm2_post, m_w_fc1, m_w_fc2, v_w_ada, v_b_ada, v_norm1_pre, v_norm1_post, v_w_in, v_w_spatial, v_b_spatial, v_ln_v_gain, v_ln_v_bias, v_w_pool, v_b_pool, v_pool_scale, v_w_out, v_norm2_pre, v_norm2_post, v_w_fc1, v_w_fc2):
    weights = dict(w_ada=w_ada, b_ada=b_ada, norm1_pre=norm1_pre, norm1_post=norm1_post, w_in=w_in,
                   w_spatial=w_spatial, b_spatial=b_spatial, ln_v_gain=ln_v_gain, ln_v_bias=ln_v_bias, w_pool=w_pool,
                   b_pool=b_pool, pool_scale=pool_scale, w_out=w_out, norm2_pre=norm2_pre, norm2_post=norm2_post,
                   w_fc1=w_fc1, w_fc2=w_fc2)
    m_old = dict(w_ada=m_w_ada, b_ada=m_b_ada, norm1_pre=m_norm1_pre, norm1_post=m_norm1_post, w_in=m_w_in,
                 w_spatial=m_w_spatial, b_spatial=m_b_spatial, ln_v_gain=m_ln_v_gain, ln_v_bias=m_ln_v_bias,
                 w_pool=m_w_pool, b_pool=m_b_pool, pool_scale=m_pool_scale, w_out=m_w_out, norm2_pre=m_norm2_pre,
                 norm2_post=m_norm2_post, w_fc1=m_w_fc1, w_fc2=m_w_fc2)
    v_old = dict(w_ada=v_w_ada, b_ada=v_b_ada, norm1_pre=v_norm1_pre, norm1_post=v_norm1_post, w_in=v_w_in,
                 w_spatial=v_w_spatial, b_spatial=v_b_spatial, ln_v_gain=v_ln_v_gain, ln_v_bias=v_ln_v_bias,
                 w_pool=v_w_pool, b_pool=v_b_pool, pool_scale=v_pool_scale, w_out=v_w_out, norm2_pre=v_norm2_pre,
                 norm2_post=v_norm2_post, w_fc1=v_w_fc1, w_fc2=v_w_fc2)
    order = ("w_ada", "b_ada", "norm1_pre", "norm1_post", "w_in", "w_spatial", "b_spatial", "ln_v_gain", "ln_v_bias",
             "w_pool", "b_pool", "pool_scale", "w_out", "norm2_pre", "norm2_post", "w_fc1", "w_fc2")
    big = ("w_in", "w_out", "w_fc1", "w_fc2")
    mx, my, mc = _position()
    me = 4 * mx + 2 * my + mc
    chip = 2 * mx + my
    row = lambda a: a.reshape(1, -1)

    pos = jnp.stack([mc, chip]).astype(jnp.int32)
    xs, target = x[0], loss_target[0]
    n1pre, n1post, n2pre, n2post = row(norm1_pre), row(norm1_post), row(norm2_pre), row(norm2_post)
    mixer = (w_spatial, jnp.repeat(b_spatial.T, HEAD_DIM, axis=1), row(ln_v_gain), row(ln_v_bias), w_pool,
             row(b_pool), row(pool_scale))
    ts_big = 512

    mod4, sc_all = _mod_exchange(c, w_ada, row(b_ada))
    mod6 = mod4.reshape(N_MOD, D_MODEL)
    ag = _ag_start([_cast_to_slot(weights[n], pos, mod4, "cast_" + n, 256) for n in big[:2]], mod4, "ag_start_mix")
    ag += _ag_start([_cast_to_slot(weights[n], pos, ag[0][0], "cast_" + n, 256) for n in big[2:]], ag[0][0],
                    "ag_start_mlp")

    win_g, wout_g = _ag_done(_ag_pass([ag[0], ag[1]], ag[2][0], "ag_pass_mix"), "ag_done_mix")
    z, ycat, mix, x1, h2 = _fwd_mix(xs, mod6, n1pre, n1post, n2pre, win_g, wout_g, *mixer, ts_big)
    (fc1_g,) = _ag_done(_ag_pass([ag[2]], h2, "ag_pass_fc1"), "ag_done_fc1")
    q = _fwd_fc1(h2, fc1_g, ts_big)
    (fc2_g,) = _ag_done(_ag_pass([ag[3]], q, "ag_pass_fc2"), "ag_done_fc2")
    dy, df, loss, dgate2, dg2post = _fwd_fc2_loss(q, x1, target, mod6, n2post, fc2_g, ts_big)

    def reduce_start(partials, tag, small=None):
        wire, owns, slots = _sibling_sum(partials, "sibling_sum_" + tag, small)
        return _rs_start(wire, "rs_start_" + tag), owns, slots

    def reduce_finish(state, owns, names, tag, dep):
        inboxes = _rs_wait(state, dep, "rs_wait_" + tag)
        shards = _final_share(inboxes, owns, "final_share_" + tag)
        for n, g in zip(names, shards):
            grads[n] = g.reshape(weights[n].shape)
            deltas[n], new_m[n], new_v[n] = _adamw(grads[n], weights[n], m_old[n], v_old[n], "adamw_" + n, 256)

    grads, deltas, new_m, new_v = {}, {}, {}, {}
    dp, g_fc2 = _bwd_fc2(df, q, fc2_g, ts_big)
    state_fc2, owns_fc2, _ = reduce_start([g_fc2], "fc2")
    dx1, dmix, dshift2, da2, dgate1, dg1post, g_fc1 = _bwd_fc1(
        dp, dy, x1, mix, h2, mod6, n2pre, n1post, fc1_g, state_fc2[0][0], ts_big)
    state_fc1, owns_fc1, _ = reduce_start([g_fc1], "fc1")
    dz, dws, dbsp, dgain, dbias, dwp, dbp, dps, g_out = _mixer_bwd(z, dmix, ycat, wout_g, *mixer, state_fc1[0][0],
                                                                   ts_big)
    grad_x, dshift1, da1, g_in = _bwd_in(dz, dx1, xs, mod6, n1pre, win_g, state_fc1[0][0], ts_big)
    dmod6, dnorms = _mod_grads(da1, dshift1, dgate1, dg1post, da2, dshift2, dgate2, dg2post, mod6, n1pre, n2pre)

    parts = dict(b_ada=dmod6, norm1_pre=dnorms[0], norm1_post=dnorms[1], norm2_pre=dnorms[2], norm2_post=dnorms[3],
                 w_spatial=dws, b_spatial=dbsp, ln_v_gain=dgain, ln_v_bias=dbias, w_pool=dwp, b_pool=dbp,
                 pool_scale=dps)
    pieces = _pack(parts)
    slots = lax.dynamic_update_slice(jnp.zeros((N_DEV * _MOD_ROWS, LANES), F32), pieces[0], (me * _MOD_ROWS, 0))
    loss_tile = jnp.pad(loss, ((0, SUBLANES - 1), (0, LANES - 1)))
    state_in, owns_in, pair_sum = reduce_start(
        [g_out, g_in], "mix", jnp.concatenate([slots] + pieces[1:] + [loss_tile], axis=0))
    spread = _small_spread_start(pair_sum[0], state_in[0][0])
    reduce_finish(state_fc2 + state_fc1, owns_fc2 + owns_fc1, ("w_fc2", "w_fc1"), "mlp", spread[0])
    flat = lambda d: [d[n].reshape(size // LANES, LANES) for n, size in _SMALL]
    small_out, dmod_all, loss_tile = _small_adamw(
        _small_spread_wait(*spread, [deltas[n] for n in ("w_fc2", "w_fc1")]), flat(weights), flat(m_old), flat(v_old))
    loss = loss_tile[0, 0]
    for (n, _), (g, d, nm, nv) in zip(_SMALL, small_out):
        shape = weights[n].shape
        grads[n], deltas[n], new_m[n], new_v[n] = g.reshape(shape), d.reshape(shape), nm.reshape(shape), nv.reshape(shape)

    dmod_all = dmod_all.reshape(N_DEV, N_MOD * D_MODEL)
    cs = w_ada.shape[1]
    dmod_shard = lax.dynamic_slice(dmod_all, (0, chip * cs), (N_DEV, cs))
    sc_t = sc_all.reshape(N_DEV, D_MODEL).T
    grads["w_ada"], deltas["w_ada"], new_m["w_ada"], new_v["w_ada"] = _ada_grad_adamw(
        sc_t, dmod_shard, w_ada, m_w_ada, v_w_ada, 256)

    reduce_finish(state_in, owns_in, ("w_out", "w_in"), "mix", deltas["w_ada"])

    return (loss, grad_x[None], *[grads[n] for n in order], *[deltas[n] for n in order],
            *[new_m[n] for n in order], *[new_v[n] for n in order])
```

```python
import jax
import jax.numpy as jnp
from jax import lax
from jax.experimental import pallas as pl
from jax.experimental.pallas import tpu as pltpu

F32 = jnp.float32
BF16 = jnp.bfloat16
MESH = pl.DeviceIdType.MESH

D_MODEL = 1024
D_A = 512
D_B = 512
D_Z = 2 * D_A + D_B
N_HEADS = 4
HEAD_DIM = 128
CHUNK = 128
POOL_WINDOWS = (2, 4, 8, 16)
GROUP_DIM = 128
D_FF = 4096
N_MOD = 6
EPS = 1e-6
HALO = 16
N_CHIPS = 4
N_DEV = 8

ADAM_LR = 0.001
ADAM_B1 = 0.9
ADAM_B2 = 0.999
ADAM_EPS = 1e-08
ADAM_WD = 0.01
ADAM_STEP = 10

VMEM_LIMIT = 56 * 1024 * 1024
LANES = 128
SUBLANES = 8

_VMEM = pl.BlockSpec(memory_space=pltpu.VMEM)
_ANY = pl.BlockSpec(memory_space=pl.ANY)


def _params(n_grid_axes=1):
    return pltpu.CompilerParams(dimension_semantics=("arbitrary",) * n_grid_axes, vmem_limit_bytes=VMEM_LIMIT)


def _rows(ts, width):
    return pl.BlockSpec((ts, width), lambda i: (i, 0))


def _const(shape):
    return pl.BlockSpec(shape, lambda i: (0,) * len(shape))


def _dot(a, b):
    return jnp.dot(a, b, preferred_element_type=F32)


def _dot_nt(a, b):
    return lax.dot_general(a, b, (((1,), (1,)), ((), ())), preferred_element_type=F32)


def _dot_tn(a, b):
    return lax.dot_general(a, b, (((0,), (0,)), ((), ())), preferred_element_type=F32)


def _rowmean(v):
    return jnp.mean(v, axis=-1, keepdims=True)


def _colsum(v):
    return jnp.sum(v, axis=0, keepdims=True)


def _gelu_parts(z):
    k0 = 0.7978845608028654
    k1 = 0.044715
    z2 = z * z
    t = jnp.tanh(k0 * (z + k1 * z * z2))
    g = 0.5 * z * (1.0 + t)
    dg = 0.5 * (1.0 + t) + 0.5 * z * (1.0 - t * t) * (k0 * (1.0 + 3.0 * k1 * z2))
    return g, dg


def _tril_weights(ws_ref):
    r = lax.broadcasted_iota(jnp.int32, (CHUNK, CHUNK), 0)
    s = lax.broadcasted_iota(jnp.int32, (CHUNK, CHUNK), 1)
    mask = (s <= r).astype(F32)
    return [(ws_ref[h] * mask).astype(BF16) for h in range(N_HEADS)]


def _window_counts(first_row, n_rows):
    pos = (first_row + lax.broadcasted_iota(jnp.int32, (n_rows, 1), 0)).astype(F32)
    return pos, [1.0 / jnp.minimum(pos + 1.0, float(w)) for w in POOL_WINDOWS]


def _causal_window_sums(ext):
    out = []
    e = ext
    shift = 1
    for g in range(len(POOL_WINDOWS)):
        e = e + pltpu.roll(e, shift, 0)
        shift *= 2
        out.append(e[:, g * GROUP_DIM:(g + 1) * GROUP_DIM])
    return out


def _anticausal_window_sums(ext):
    n = ext.shape[0]
    out = []
    e = ext
    shift = 1
    for g in range(len(POOL_WINDOWS)):
        e = e + pltpu.roll(e, n - shift, 0)
        shift *= 2
        out.append(e[:, g * GROUP_DIM:(g + 1) * GROUP_DIM])
    return out


def _fwd_mix(x, mod6, n1pre, n1post, n2pre, win_g, wout_g, w_spatial, bsp_full, gain, bias, w_pool, b_pool, pool_scale, ts):
    s_len = x.shape[0]
    rs = D_MODEL // N_CHIPS

    def body(x_ref, mod_ref, g1pre_ref, g1post_ref, g2pre_ref, win_ref, wout_ref, ws_ref, bsp_ref, gain_ref,
             bias_ref, wp_ref, bp_ref, ps_ref, z_ref, y_ref, mix_ref, x1_ref, h2_ref, mixed_ref, prev_ref, wfull_ref):
        i = pl.program_id(0)
        _zero_on_first_step(prev_ref)
        _join_w_in_on_first_step(win_ref, wfull_ref)
        xv = x_ref[...]
        r = lax.rsqrt(_rowmean(xv * xv) + EPS)
        hb = (((xv * r) * g1pre_ref[...]) * (1.0 + mod_ref[1:2, :]) + mod_ref[0:1, :]).astype(BF16)
        z_ref[...] = _dot(hb, wfull_ref[...])

        wc = _tril_weights(ws_ref)
        u, _, _, _, _ = _mixer_forward_tile(z_ref[:, :2 * D_A], wc, bsp_ref, gain_ref[...], bias_ref[...], mixed_ref)
        y_ref[:, :D_A] = (u * mixed_ref[...]).astype(BF16)
        zb = z_ref[:, 2 * D_A:]
        sums = _causal_window_sums(jnp.concatenate([prev_ref[...], zb], axis=0))
        prev_ref[...] = zb[ts - HALO:, :]
        _, inv_counts = _window_counts(i * ts, ts)
        for g in range(len(POOL_WINDOWS)):
            lanes = slice(g * GROUP_DIM, (g + 1) * GROUP_DIM)
            diff = sums[g][HALO:, :] * inv_counts[g] - zb[:, lanes]
            lin = _dot(diff.astype(BF16), wp_ref[g].astype(BF16)) + bp_ref[:, lanes]
            y_ref[:, D_A + g * GROUP_DIM:D_A + (g + 1) * GROUP_DIM] = (lin * ps_ref[:, lanes]).astype(BF16)

        mix = None
        for j in range(N_CHIPS):
            part = _dot(y_ref[:, j * rs:(j + 1) * rs], wout_ref[j])
            mix = part if mix is None else mix + part
        mix_ref[...] = mix
        r2 = lax.rsqrt(_rowmean(mix * mix) + EPS)
        x1 = xv + mod_ref[2:3, :] * ((mix * r2) * g1post_ref[...])
        x1_ref[...] = x1
        r3 = lax.rsqrt(_rowmean(x1 * x1) + EPS)
        h2_ref[...] = (((x1 * r3) * g2pre_ref[...]) * (1.0 + mod_ref[4:5, :]) + mod_ref[3:4, :]).astype(BF16)

    vec = _const((1, D_MODEL))
    f32_rows = jax.ShapeDtypeStruct((s_len, D_MODEL), F32)
    bf16_rows = jax.ShapeDtypeStruct((s_len, D_MODEL), BF16)
    return pl.pallas_call(
        body, name="fwd_mix", grid=(s_len // ts,),
        in_specs=[_rows(ts, D_MODEL), _const((N_MOD, D_MODEL)), vec, vec, vec, _VMEM, _VMEM,
                  _const((N_HEADS, CHUNK, CHUNK)), _const((CHUNK, D_A)), _const((1, D_A)), _const((1, D_A)),
                  _const((N_HEADS, GROUP_DIM, GROUP_DIM)), _const((1, D_B)), _const((1, D_B))],
        out_specs=[_rows(ts, D_Z), _rows(ts, D_MODEL), _rows(ts, D_MODEL), _rows(ts, D_MODEL), _rows(ts, D_MODEL)],
        out_shape=[jax.ShapeDtypeStruct((s_len, D_Z), F32), bf16_rows, f32_rows, f32_rows, bf16_rows],
        scratch_shapes=[pltpu.VMEM((ts, D_A), F32), pltpu.VMEM((HALO, D_B), F32), pltpu.VMEM((D_MODEL, D_Z), BF16)],
        compiler_params=_params(),
    )(x, mod6, n1pre, n1post, n2pre, win_g, wout_g, w_spatial, bsp_full, gain, bias, w_pool, b_pool, pool_scale)


def _mixer_forward_tile(za, wc, bsp_ref, gain, bias, mixed_ref):
    ga, dga = _gelu_parts(za)
    u = ga[:, :D_A]
    v = ga[:, D_A:]
    mu = _rowmean(v)
    vc = v - mu
    rstd = lax.rsqrt(_rowmean(vc * vc) + EPS)
    vhat = vc * rstd
    vn = (vhat * gain + bias).astype(BF16)
    ts = za.shape[0]
    for k in range(ts // CHUNK):
        for h in range(N_HEADS):
            blk = vn[k * CHUNK:(k + 1) * CHUNK, h * HEAD_DIM:(h + 1) * HEAD_DIM]
            mixed_ref[k * CHUNK:(k + 1) * CHUNK, h * HEAD_DIM:(h + 1) * HEAD_DIM] = (
                _dot(wc[h], blk) + bsp_ref[:, h * HEAD_DIM:(h + 1) * HEAD_DIM])
    return u, vhat, rstd, vn, dga


def _fwd_fc1(h2, fc1_g, ts):
    s_len = h2.shape[0]
    cs = D_FF // N_CHIPS

    def body(h_ref, w_ref, q_ref):
        hb = h_ref[...]
        for j in range(N_CHIPS):
            p = jnp.maximum(_dot(hb, w_ref[j]), 0.0)
            q_ref[:, j * cs:(j + 1) * cs] = (p * p).astype(BF16)

    return pl.pallas_call(
        body, name="fwd_fc1", grid=(s_len // ts,),
        in_specs=[_rows(ts, D_MODEL), _VMEM],
        out_specs=_rows(ts, D_FF),
        out_shape=jax.ShapeDtypeStruct((s_len, D_FF), BF16),
        compiler_params=_params(),
    )(h2, fc1_g)


def _fwd_fc2_loss(q, x1, target, mod6, n2post, fc2_g, ts):
    s_len = q.shape[0]
    rs = D_FF // N_CHIPS

    def body(q_ref, x1_ref, t_ref, mod_ref, g_ref, w_ref, dy_ref, df_ref, loss_ref, s_ref):
        _zero_on_first_step(loss_ref, s_ref)
        gate_gain = mod_ref[5:6, :] * g_ref[...]
        f = _dot(q_ref[:, 0:rs], w_ref[0])
        for j in range(1, N_CHIPS):
            f = f + _dot(q_ref[:, j * rs:(j + 1) * rs], w_ref[j])
        r4 = lax.rsqrt(_rowmean(f * f) + EPS)
        fh = f * r4
        err = (x1_ref[...] + fh * gate_gain) - t_ref[...]
        loss_ref[...] += 0.5 * jnp.sum(_rowmean(err * err), axis=0, keepdims=True)
        dy = err * (1.0 / D_MODEL)
        dy_ref[...] = dy
        s_ref[...] += _colsum(dy * fh)
        gh = dy * gate_gain
        df_ref[...] = (r4 * (gh - fh * _rowmean(gh * fh))).astype(BF16)

    return pl.pallas_call(
        body, name="fwd_fc2_loss", grid=(s_len // ts,),
        in_specs=[_rows(ts, D_FF), _rows(ts, D_MODEL), _rows(ts, D_MODEL), _const((N_MOD, D_MODEL)),
                  _const((1, D_MODEL)), _VMEM],
        out_specs=[_rows(ts, D_MODEL), _rows(ts, D_MODEL), _const((1, 1)), _const((1, D_MODEL))],
        out_shape=[jax.ShapeDtypeStruct((s_len, D_MODEL), F32), jax.ShapeDtypeStruct((s_len, D_MODEL), BF16),
                   jax.ShapeDtypeStruct((1, 1), F32), jax.ShapeDtypeStruct((1, D_MODEL), F32)],
        compiler_params=_params(),
    )(q, x1, target, mod6, n2post, fc2_g)


def _join_w_in_on_first_step(win_ref, full_ref):
    cs = D_Z // N_CHIPS

    @pl.when(pl.program_id(0) == 0)
    def _():
        for j in range(N_CHIPS):
            full_ref[:, j * cs:(j + 1) * cs] = win_ref[j]


def _zero_on_first_step(*refs):
    @pl.when(pl.program_id(0) == 0)
    def _():
        for ref in refs:
            ref[...] = jnp.zeros_like(ref)


def _on_last_step(fn):
    pl.when(pl.program_id(0) == pl.num_programs(0) - 1)(fn)


def _store_shard_on_last_step(acc_ref, hbm_ref, sem, j):
    _on_last_step(lambda: pltpu.make_async_copy(acc_ref.at[j], hbm_ref.at[j], sem.at[j]).start())


def _wait_stores_on_last_step(*stores):
    def wait_all():
        for acc_ref, hbm_ref, sem in stores:
            for j in range(N_CHIPS):
                pltpu.make_async_copy(acc_ref.at[j], hbm_ref.at[j], sem.at[j]).wait()

    _on_last_step(wait_all)


def _bwd_fc2(df, q, fc2_g, ts):
    s_len = df.shape[0]
    cs = D_FF // N_CHIPS

    def body(df_ref, q_ref, w_ref, dp_ref, dw_hbm, dw_ref, dw_sem):
        _zero_on_first_step(dw_ref)
        dfb = df_ref[...]
        for j in range(N_CHIPS):
            qb = q_ref[:, j * cs:(j + 1) * cs]
            dw_ref[j] += _dot_tn(qb, dfb).reshape(2, cs // 2, D_MODEL)
            _store_shard_on_last_step(dw_ref, dw_hbm, dw_sem, j)
            dq = _dot_nt(dfb, w_ref[j])
            dp_ref[:, j * cs:(j + 1) * cs] = (dq * (2.0 * jnp.sqrt(qb.astype(F32)))).astype(BF16)
        _wait_stores_on_last_step((dw_ref, dw_hbm, dw_sem))

    dw_shape = (N_CHIPS, 2, cs // 2, D_MODEL)
    return pl.pallas_call(
        body, name="bwd_fc2", grid=(s_len // ts,),
        in_specs=[_rows(ts, D_MODEL), _rows(ts, D_FF), _VMEM],
        out_specs=[_rows(ts, D_FF), _ANY],
        out_shape=[jax.ShapeDtypeStruct((s_len, D_FF), BF16), jax.ShapeDtypeStruct(dw_shape, F32)],
        scratch_shapes=[pltpu.VMEM(dw_shape, F32), pltpu.SemaphoreType.DMA((N_CHIPS,))],
        compiler_params=_params(),
    )(df, q, fc2_g)


def _bwd_fc1_out(dp, dy, x1, mix, h2, ycat, mod6, n2pre, n1post, fc1_g, wout_g, dep, ts):
    s_len = dp.shape[0]
    cs = D_FF // N_CHIPS
    rs = D_MODEL // N_CHIPS

    def body(dp_ref, dy_ref, x1_ref, mix_ref, h2_ref, yc_ref, mod_ref, g2_ref, g1_ref, w1_ref, wo_ref, dep_ref,
             dx1_ref, dyc_ref, dshift2_ref, da2_ref, s1_ref, dw1_hbm, dwo_hbm, dw1_ref, dwo_ref, dw1_sem, dwo_sem):
        _zero_on_first_step(dshift2_ref, da2_ref, s1_ref, dw1_ref, dwo_ref)
        h2b = h2_ref[...]
        dh2 = None
        for j in range(N_CHIPS):
            dpb = dp_ref[:, j * cs:(j + 1) * cs]
            dw1_ref[j] += _dot_tn(h2b, dpb).reshape(2, D_MODEL // 2, cs)
            _store_shard_on_last_step(dw1_ref, dw1_hbm, dw1_sem, j)
            part = _dot_nt(dpb, w1_ref[j])
            dh2 = part if dh2 is None else dh2 + part
        x1 = x1_ref[...]
        r3 = lax.rsqrt(_rowmean(x1 * x1) + EPS)
        xh = x1 * r3
        a2 = g2_ref[...] * (1.0 + mod_ref[4:5, :])
        dshift2_ref[...] += _colsum(dh2)
        da2_ref[...] += _colsum(dh2 * xh)
        dxh = dh2 * a2
        dx1 = dy_ref[...] + r3 * (dxh - xh * _rowmean(dxh * xh))
        dx1_ref[...] = dx1

        mix = mix_ref[...]
        r2 = lax.rsqrt(_rowmean(mix * mix) + EPS)
        mh = mix * r2
        s1_ref[...] += _colsum(dx1 * mh)
        gh = dx1 * (mod_ref[2:3, :] * g1_ref[...])
        dmix = (r2 * (gh - mh * _rowmean(gh * mh))).astype(BF16)
        dwo_ref[...] += _dot_tn(yc_ref[...], dmix).reshape(N_CHIPS, 2, rs // 2, D_MODEL)
        for j in range(N_CHIPS):
            _store_shard_on_last_step(dwo_ref, dwo_hbm, dwo_sem, j)
            dyc_ref[:, j * rs:(j + 1) * rs] = _dot_nt(dmix, wo_ref[j])
        _wait_stores_on_last_step((dw1_ref, dw1_hbm, dw1_sem), (dwo_ref, dwo_hbm, dwo_sem))

    vec = jax.ShapeDtypeStruct((1, D_MODEL), F32)
    dw1_shape = (N_CHIPS, 2, D_MODEL // 2, cs)
    dwo_shape = (N_CHIPS, 2, rs // 2, D_MODEL)
    return pl.pallas_call(
        body, name="bwd_fc1_out", grid=(s_len // ts,),
        in_specs=[_rows(ts, D_FF), _rows(ts, D_MODEL), _rows(ts, D_MODEL), _rows(ts, D_MODEL), _rows(ts, D_MODEL),
                  _rows(ts, D_MODEL), _const((N_MOD, D_MODEL)), _const((1, D_MODEL)), _const((1, D_MODEL)), _VMEM,
                  _VMEM, _ANY],
        out_specs=[_rows(ts, D_MODEL), _rows(ts, D_MODEL)] + [_const((1, D_MODEL))] * 3 + [_ANY, _ANY],
        out_shape=[jax.ShapeDtypeStruct((s_len, D_MODEL), F32), jax.ShapeDtypeStruct((s_len, D_MODEL), F32),
                   vec, vec, vec, jax.ShapeDtypeStruct(dw1_shape, F32), jax.ShapeDtypeStruct(dwo_shape, F32)],
        scratch_shapes=[pltpu.VMEM(dw1_shape, F32), pltpu.VMEM(dwo_shape, F32), pltpu.SemaphoreType.DMA((N_CHIPS,)),
                        pltpu.SemaphoreType.DMA((N_CHIPS,))],
        compiler_params=_params(),
    )(dp, dy, x1, mix, h2, ycat, mod6, n2pre, n1post, fc1_g, wout_g, dep)


def _mixer_bwd(z, dyc, w_spatial, bsp_full, gain, bias, w_pool, b_pool, pool_scale, dep, ts):
    s_len = z.shape[0]
    nb = ts // HALO
    last = s_len // HALO - 1
    te = ts + HALO

    def body(z_ref, zprev_ref, znext_ref, dyc_ref, dynext_ref, ws_ref, bsp_ref, gain_ref, bias_ref, wp_ref, bp_ref,
             ps_ref, dep_ref, dz_ref, dws_ref, dbsp_ref, dgain_ref, dbias_ref, dwp_ref, dbp_ref, dps_ref, mixed_ref,
             dvn_ref):
        i = pl.program_id(0)

        @pl.when(i == 0)
        def _():
            for ref in (dws_ref, dbsp_ref, dgain_ref, dbias_ref, dwp_ref, dbp_ref, dps_ref):
                ref[...] = jnp.zeros_like(ref)

        wc = _tril_weights(ws_ref)
        gain = gain_ref[...]
        u, vhat, rstd, vn, dga = _mixer_forward_tile(z_ref[:, :2 * D_A], wc, bsp_ref, gain, bias_ref[...], mixed_ref)
        dya = dyc_ref[:, :D_A]
        du = dya * mixed_ref[...]
        dmixed = dya * u
        dmb = dmixed.astype(BF16)
        dm_sum = dmixed[0:CHUNK, :]
        for k in range(1, ts // CHUNK):
            dm_sum = dm_sum + dmixed[k * CHUNK:(k + 1) * CHUNK, :]
        r_idx = lax.broadcasted_iota(jnp.int32, (CHUNK, CHUNK), 0)
        s_idx = lax.broadcasted_iota(jnp.int32, (CHUNK, CHUNK), 1)
        causal = (s_idx <= r_idx).astype(F32)
        for h in range(N_HEADS):
            lanes = slice(h * HEAD_DIM, (h + 1) * HEAD_DIM)
            dbsp_ref[h] += jnp.sum(dm_sum[:, lanes], axis=1, keepdims=True)
            acc = None
            for k in range(ts // CHUNK):
                rows = slice(k * CHUNK, (k + 1) * CHUNK)
                t = _dot_nt(dmb[rows, lanes], vn[rows, lanes])
                acc = t if acc is None else acc + t
                dvn_ref[rows, lanes] = _dot_tn(wc[h], dmb[rows, lanes])
            dws_ref[h] += acc * causal
        dvn = dvn_ref[...]
        dgain_ref[...] += _colsum(dvn * vhat)
        dbias_ref[...] += _colsum(dvn)
        dvh = dvn * gain
        dv = rstd * (dvh - _rowmean(dvh) - vhat * _rowmean(dvh * vhat))
        dz_ref[:, :D_A] = (du * dga[:, :D_A]).astype(BF16)
        dz_ref[:, D_A:2 * D_A] = (dv * dga[:, D_A:]).astype(BF16)

        zb = z_ref[:, 2 * D_A:]
        prev = jnp.where(i == 0, 0.0, zprev_ref[...])
        zb_ext = jnp.concatenate([zb, znext_ref[...]], axis=0)
        sums = _causal_window_sums(jnp.concatenate([prev, zb_ext], axis=0))
        pos, inv_counts = _window_counts(i * ts, te)
        dyb_ext = jnp.concatenate([dyc_ref[:, D_A:], dynext_ref[...]], axis=0)
        dlin_ext = dyb_ext * ps_ref[...]
        dbp_ref[...] += _colsum(dlin_ext[:ts, :])
        scaled = []
        ddiffs = []
        lins = []
        for g in range(len(POOL_WINDOWS)):
            lanes = slice(g * GROUP_DIM, (g + 1) * GROUP_DIM)
            diff = (sums[g][HALO:, :] * inv_counts[g] - zb_ext[:, lanes]).astype(BF16)
            wpb = wp_ref[g].astype(BF16)
            dlb = dlin_ext[:, lanes].astype(BF16)
            lins.append(_dot(diff[:ts, :], wpb) + bp_ref[:, lanes])
            dwp_ref[g] += _dot_tn(diff[:ts, :], dlb[:ts, :])
            dd = _dot_nt(dlb, wpb)
            ddiffs.append(dd)
            scaled.append(jnp.where(pos < float(s_len), dd * inv_counts[g], 0.0))
        dps_ref[...] += _colsum(dyb_ext[:ts, :] * jnp.concatenate(lins, axis=1))
        back = _anticausal_window_sums(jnp.concatenate(scaled, axis=1))
        for g in range(len(POOL_WINDOWS)):
            dz_ref[:, 2 * D_A + g * GROUP_DIM:2 * D_A + (g + 1) * GROUP_DIM] = (
                back[g][:ts, :] - ddiffs[g][:ts, :]).astype(BF16)

    sq = jax.ShapeDtypeStruct((N_HEADS, CHUNK, CHUNK), F32)
    vec = jax.ShapeDtypeStruct((1, D_A), F32)
    return pl.pallas_call(
        body, name="mixer_bwd", grid=(s_len // ts,),
        in_specs=[_rows(ts, D_Z),
                  pl.BlockSpec((HALO, D_B), lambda i: (jnp.maximum(i * nb - 1, 0), 2)),
                  pl.BlockSpec((HALO, D_B), lambda i: (jnp.minimum((i + 1) * nb, last), 2)),
                  _rows(ts, D_MODEL),
                  pl.BlockSpec((HALO, D_B), lambda i: (jnp.minimum((i + 1) * nb, last), 1)),
                  _const((N_HEADS, CHUNK, CHUNK)), _const((CHUNK, D_A)), _const((1, D_A)), _const((1, D_A)),
                  _const((N_HEADS, GROUP_DIM, GROUP_DIM)), _const((1, D_B)), _const((1, D_B)), _ANY],
        out_specs=[_rows(ts, D_Z), _const((N_HEADS, CHUNK, CHUNK)), _const((N_HEADS, CHUNK, 1)), _const((1, D_A)),
                   _const((1, D_A)), _const((N_HEADS, GROUP_DIM, GROUP_DIM)), _const((1, D_B)), _const((1, D_B))],
        out_shape=[jax.ShapeDtypeStruct((s_len, D_Z), BF16), sq, jax.ShapeDtypeStruct((N_HEADS, CHUNK, 1), F32), vec,
                   vec, sq, vec, vec],
        scratch_shapes=[pltpu.VMEM((ts, D_A), F32), pltpu.VMEM((ts, D_A), F32)],
        compiler_params=_params(),
    )(z, z, z, dyc, dyc, w_spatial, bsp_full, gain, bias, w_pool, b_pool, pool_scale, dep)


def _bwd_in(dz, dx1, x, mod6, n1pre, win_g, dep, ts):
    s_len = x.shape[0]
    cs = D_Z // N_CHIPS

    def body(dz_ref, dx1_ref, x_ref, mod_ref, g_ref, w_ref, dep_ref, gx_ref, dshift_ref, da_ref, dw_hbm, dw_ref,
             wfull_ref, dw_sem):
        _zero_on_first_step(dshift_ref, da_ref, dw_ref)
        _join_w_in_on_first_step(w_ref, wfull_ref)
        xv = x_ref[...]
        r = lax.rsqrt(_rowmean(xv * xv) + EPS)
        xh = xv * r
        h1b = ((xh * g_ref[...]) * (1.0 + mod_ref[1:2, :]) + mod_ref[0:1, :]).astype(BF16)
        dzb = dz_ref[...]
        dw = _dot_tn(h1b, dzb)
        for j in range(N_CHIPS):
            dw_ref[j] += dw[:, j * cs:(j + 1) * cs].reshape(2, D_MODEL // 2, cs)
        dh = _dot_nt(dzb, wfull_ref[...])
        a1 = g_ref[...] * (1.0 + mod_ref[1:2, :])
        dshift_ref[...] += _colsum(dh)
        da_ref[...] += _colsum(dh * xh)
        dxh = dh * a1
        gx_ref[...] = dx1_ref[...] + r * (dxh - xh * _rowmean(dxh * xh))
        for j in range(N_CHIPS):
            _store_shard_on_last_step(dw_ref, dw_hbm, dw_sem, j)
        _wait_stores_on_last_step((dw_ref, dw_hbm, dw_sem))

    vec = jax.ShapeDtypeStruct((1, D_MODEL), F32)
    dw_shape = (N_CHIPS, 2, D_MODEL // 2, cs)
    return pl.pallas_call(
        body, name="bwd_in", grid=(s_len // ts,),
        in_specs=[_rows(ts, D_Z), _rows(ts, D_MODEL), _rows(ts, D_MODEL), _const((N_MOD, D_MODEL)),
                  _const((1, D_MODEL)), _VMEM, _ANY],
        out_specs=[_rows(ts, D_MODEL), _const((1, D_MODEL)), _const((1, D_MODEL)), _ANY],
        out_shape=[jax.ShapeDtypeStruct((s_len, D_MODEL), F32), vec, vec, jax.ShapeDtypeStruct(dw_shape, F32)],
        scratch_shapes=[pltpu.VMEM(dw_shape, F32), pltpu.VMEM((D_MODEL, D_Z), BF16),
                        pltpu.SemaphoreType.DMA((N_CHIPS,))],
        compiler_params=_params(),
    )(dz, dx1, x, mod6, n1pre, win_g, dep)


def _adamw_math(w, g, m, v):
    m = ADAM_B1 * m + (1.0 - ADAM_B1) * g
    v = ADAM_B2 * v + (1.0 - ADAM_B2) * (g * g)
    m_hat = m / (1.0 - ADAM_B1 ** ADAM_STEP)
    v_hat = v / (1.0 - ADAM_B2 ** ADAM_STEP)
    delta = -ADAM_LR * (m_hat / (jnp.sqrt(v_hat) + ADAM_EPS) + ADAM_WD * w)
    return delta, m, v


def _adamw(g, w, m, v, name, tr):
    rows, cols = w.shape

    def body(g_ref, w_ref, m_ref, v_ref, d_ref, nm_ref, nv_ref):
        d, nm, nv = _adamw_math(w_ref[...], g_ref[...], m_ref[...], v_ref[...])
        d_ref[...] = d
        nm_ref[...] = nm
        nv_ref[...] = nv

    spec = _rows(tr, cols)
    shape = jax.ShapeDtypeStruct((rows, cols), F32)
    return pl.pallas_call(
        body, name=name, grid=(rows // tr,), in_specs=[spec] * 4, out_specs=[spec] * 3, out_shape=[shape] * 3,
        compiler_params=_params(),
    )(g, w, m, v)


def _ada_grad_adamw(sc_t, dmod_shard, w, m, v, tr):
    rows, cols = w.shape

    def body(s_ref, dm_ref, w_ref, m_ref, v_ref, g_ref, d_ref, nm_ref, nv_ref):
        g = s_ref[:, 0:1] * dm_ref[0:1, :]
        for b in range(1, N_DEV):
            g = g + s_ref[:, b:b + 1] * dm_ref[b:b + 1, :]
        g_ref[...] = g
        d, nm, nv = _adamw_math(w_ref[...], g, m_ref[...], v_ref[...])
        d_ref[...] = d
        nm_ref[...] = nm
        nv_ref[...] = nv

    spec = _rows(tr, cols)
    shape = jax.ShapeDtypeStruct((rows, cols), F32)
    return pl.pallas_call(
        body, name="ada_grad_adamw", grid=(rows // tr,),
        in_specs=[_rows(tr, N_DEV), _const((N_DEV, cols)), spec, spec, spec],
        out_specs=[spec] * 4, out_shape=[shape] * 4, compiler_params=_params(),
    )(sc_t, dmod_shard, w, m, v)


def _mod_grads(da1, dshift1, s1, da2, dshift2, s2, mod6, n1pre, n1post, n2pre, n2post):
    def body(da1_ref, ds1_ref, s1_ref, da2_ref, ds2_ref, s2_ref, mod_ref, n1_ref, p1_ref, n2_ref, p2_ref, dmod_ref,
             dn_ref):
        dmod_ref[0:1, :] = ds1_ref[...]
        dmod_ref[1:2, :] = da1_ref[...] * n1_ref[...]
        dmod_ref[2:3, :] = s1_ref[...] * p1_ref[...]
        dmod_ref[3:4, :] = ds2_ref[...]
        dmod_ref[4:5, :] = da2_ref[...] * n2_ref[...]
        dmod_ref[5:6, :] = s2_ref[...] * p2_ref[...]
        dn_ref[0:1, :] = da1_ref[...] * (1.0 + mod_ref[1:2, :])
        dn_ref[1:2, :] = s1_ref[...] * mod_ref[2:3, :]
        dn_ref[2:3, :] = da2_ref[...] * (1.0 + mod_ref[4:5, :])
        dn_ref[3:4, :] = s2_ref[...] * mod_ref[5:6, :]

    return pl.pallas_call(
        body, name="mod_grads",
        out_shape=[jax.ShapeDtypeStruct((N_MOD, D_MODEL), F32), jax.ShapeDtypeStruct((4, D_MODEL), F32)],
    )(da1, dshift1, s1, da2, dshift2, s2, mod6, n1pre, n1post, n2pre, n2post)


def _position():
    x, y, c = lax.axis_index("x"), lax.axis_index("y"), lax.axis_index("c")
    return x, y, c


def _flip(v, bit):
    return 1 - v if bit else v


def _peer(x, y, c, k):
    return (_flip(x, k & 4), _flip(y, k & 2), _flip(c, k & 1))


def _remote(src, dst, send_sem, recv_sem, device):
    return pltpu.make_async_remote_copy(src_ref=src, dst_ref=dst, send_sem=send_sem, recv_sem=recv_sem,
                                        device_id=device, device_id_type=MESH)


def _cast_to_slot(w, pos, dep, name, tr):
    rows, cols = w.shape

    def body(pos_ref, w_ref, dep_ref, o_ref):
        o_ref[0] = w_ref[...].astype(BF16)

    return pl.pallas_call(
        body, name=name,
        grid_spec=pltpu.PrefetchScalarGridSpec(
            num_scalar_prefetch=1, grid=(rows // tr,),
            in_specs=[pl.BlockSpec((tr, cols), lambda i, pos: (i, 0)), _ANY],
            out_specs=pl.BlockSpec((1, tr, cols), lambda i, pos: (pos[1], i, 0))),
        out_shape=jax.ShapeDtypeStruct((N_CHIPS, rows, cols), BF16), compiler_params=_params(),
    )(pos, w, dep)


def _mod_exchange(c_row, w_ada_shard, b_ada_row):
    cs = w_ada_shard.shape[1]

    def body(c_ref, w_hbm, b_ref, mod_ref, sc_ref, rows_ref, w_ref, w_sem, send1, recv1, send2, recv2):
        x, y, c = _position()
        me = 4 * x + 2 * y + c
        chip = 2 * x + y
        w_load = pltpu.make_async_copy(w_hbm, w_ref, w_sem)
        w_load.start()
        cv = c_ref[...]
        sc_ref[me] = cv * jax.nn.sigmoid(cv)
        gather = [_remote(sc_ref.at[me], sc_ref.at[me], send1.at[k - 1], recv1.at[k - 1], _peer(x, y, c, k))
                  for k in range(1, N_DEV)]
        for cp in gather:
            cp.start()
        for k in range(1, N_DEV):
            px, py, pc = _peer(x, y, c, k)
            src = 4 * px + 2 * py + pc
            _remote(sc_ref.at[src], sc_ref.at[src], send1.at[k - 1], recv1.at[k - 1], (px, py, pc)).wait_recv()
        for cp in gather:
            cp.wait_send()
        sc_all = jnp.concatenate([sc_ref[b] for b in range(N_DEV)], axis=0)
        w_load.wait()
        part = _dot(sc_all.astype(BF16), w_ref[...].astype(BF16))
        part = part + b_ref[:, pl.ds(pl.multiple_of(chip * cs, LANES), cs)]
        for b in range(N_DEV):
            rows_ref[b] = part[b:b + 1, :]
        mod_ref[chip] = rows_ref[me]
        hand = []
        for k in (2, 4, 6):
            px, py, _ = _peer(x, y, c, k)
            hand.append(_remote(rows_ref.at[4 * px + 2 * py + c], mod_ref.at[chip], send2.at[k // 2 - 1],
                                recv2.at[k // 2 - 1], (px, py, c)))
        for cp in hand:
            cp.start()
        for k in (2, 4, 6):
            px, py, _ = _peer(x, y, c, k)
            pchip = 2 * px + py
            _remote(rows_ref.at[me], mod_ref.at[pchip], send2.at[k // 2 - 1], recv2.at[k // 2 - 1],
                    (px, py, c)).wait_recv()
        for cp in hand:
            cp.wait_send()

    return pl.pallas_call(
        body, name="mod_exchange",
        in_specs=[_VMEM, _ANY, _VMEM], out_specs=[_VMEM, _VMEM],
        out_shape=[jax.ShapeDtypeStruct((N_CHIPS, 1, cs), F32), jax.ShapeDtypeStruct((N_DEV, 1, D_MODEL), F32)],
        scratch_shapes=[pltpu.VMEM((N_DEV, 1, cs), F32), pltpu.VMEM(w_ada_shard.shape, F32), pltpu.SemaphoreType.DMA,
                        pltpu.SemaphoreType.DMA((N_DEV - 1,)),
                        pltpu.SemaphoreType.DMA((N_DEV - 1,)), pltpu.SemaphoreType.DMA((N_CHIPS - 1,)),
                        pltpu.SemaphoreType.DMA((N_CHIPS - 1,))],
        compiler_params=pltpu.CompilerParams(vmem_limit_bytes=VMEM_LIMIT),
    )(c_row, w_ada_shard, b_ada_row)


_HBM = pl.BlockSpec(memory_space=pltpu.HBM)
_SEM = pl.BlockSpec(memory_space=pltpu.SEMAPHORE)
_EFFECT = pltpu.SideEffectType.DATAFLOW_SIDE_EFFECTING
_CHIP_HOPS = (2, 4, 6)


def _in_hbm(a):
    return pltpu.with_memory_space_constraint(a, pltpu.HBM)


def _sems3():
    return pltpu.SemaphoreType.DMA((len(_CHIP_HOPS),))


def _ag_start(lands, after, name):
    n = len(lands)

    def body(*refs):
        zones = refs[:n]
        sends, recvs = refs[n + 1:2 * n + 1], refs[2 * n + 1:3 * n + 1]
        x, y, c = _position()
        chip = 2 * x + y
        for i in range(n):
            half = zones[i].shape[1] // 2
            mine = zones[i].at[chip, pl.ds(c * half, half)]
            for s, k in enumerate(_CHIP_HOPS):
                px, py, _ = _peer(x, y, c, k)
                _remote(mine, mine, sends[i].at[s], recvs[i].at[s], (px, py, c)).start()

    out = pl.pallas_call(
        body, name=name,
        in_specs=[_HBM] * n + [_ANY],
        out_specs=[_SEM] * (2 * n) + [_HBM] * n,
        out_shape=[_sems3()] * (2 * n) + [pltpu.HBM(z.shape, BF16) for z in lands],
        input_output_aliases={i: 2 * n + i for i in range(n)},
        compiler_params=pltpu.CompilerParams(has_side_effects=_EFFECT),
    )(*[_in_hbm(z) for z in lands], after)
    return [(out[2 * n + i], out[i], out[n + i]) for i in range(n)]


def _ag_pass(group, after, name):
    n = len(group)

    def body(*refs):
        zones = refs[:n]
        sends, recvs = refs[n:2 * n], refs[2 * n:3 * n]
        fsends, frecvs = refs[4 * n + 1:5 * n + 1], refs[5 * n + 1:6 * n + 1]
        x, y, c = _position()
        chip = 2 * x + y
        for i in range(n):
            half = zones[i].shape[1] // 2
            rows = pl.ds(c * half, half)
            for s, k in enumerate(_CHIP_HOPS):
                px, py, _ = _peer(x, y, c, k)
                landed = zones[i].at[2 * px + py, rows]
                _remote(landed, landed, sends[i].at[s], recvs[i].at[s], (px, py, c)).wait_recv()
                _remote(landed, landed, fsends[i].at[s], frecvs[i].at[s], (x, y, 1 - c)).start()
        for i in range(n):
            half = zones[i].shape[1] // 2
            mine = zones[i].at[chip, pl.ds(c * half, half)]
            for s, k in enumerate(_CHIP_HOPS):
                px, py, _ = _peer(x, y, c, k)
                _remote(mine, mine, sends[i].at[s], recvs[i].at[s], (px, py, c)).wait_send()

    out = pl.pallas_call(
        body, name=name,
        in_specs=[_HBM] * n + [_SEM] * (2 * n) + [_ANY],
        out_specs=[_HBM] * n + [_SEM] * (2 * n),
        out_shape=[pltpu.HBM(g[0].shape, BF16) for g in group] + [_sems3()] * (2 * n),
        input_output_aliases={i: i for i in range(n)},
        compiler_params=pltpu.CompilerParams(has_side_effects=_EFFECT),
    )(*[g[0] for g in group], *[g[1] for g in group], *[g[2] for g in group], after)
    return [(out[i], out[n + i], out[2 * n + i]) for i in range(n)]


def _ag_done(group, name):
    n = len(group)

    def body(*refs):
        lands = refs[:n]
        fsends, frecvs = refs[n:2 * n], refs[2 * n:3 * n]
        x, y, c = _position()
        for i in range(n):
            half = lands[i].shape[1] // 2
            for s, k in enumerate(_CHIP_HOPS):
                px, py, _ = _peer(x, y, c, k)
                sent = lands[i].at[2 * px + py, pl.ds(c * half, half)]
                got = lands[i].at[2 * px + py, pl.ds((1 - c) * half, half)]
                cp = _remote(sent, got, fsends[i].at[s], frecvs[i].at[s], (x, y, 1 - c))
                cp.wait_recv()
                cp.wait_send()

    out = pl.pallas_call(
        body, name=name,
        in_specs=[_HBM] * n + [_SEM] * (2 * n),
        out_specs=[_HBM] * n,
        out_shape=[pltpu.HBM(g[0].shape, BF16) for g in group],
        input_output_aliases={i: i for i in range(n)},
        compiler_params=pltpu.CompilerParams(has_side_effects=_EFFECT),
    )(*[g[0] for g in group], *[g[1] for g in group], *[g[2] for g in group])
    return list(out)


def _small_spread_start(slots, after):
    def body(z_ref, after_ref, sends, recvs, z_out):
        x, y, c = _position()
        mine = z_ref.at[2 * x + y]
        for s, k in enumerate(_CHIP_HOPS):
            px, py, _ = _peer(x, y, c, k)
            _remote(mine, mine, sends.at[s], recvs.at[s], (px, py, c)).start()

    sends, recvs, out = pl.pallas_call(
        body, name="small_spread_start",
        in_specs=[_HBM, _ANY], out_specs=[_SEM, _SEM, _HBM],
        out_shape=[_sems3(), _sems3(), pltpu.HBM(slots.shape, F32)],
        input_output_aliases={0: 2},
        compiler_params=pltpu.CompilerParams(has_side_effects=_EFFECT),
    )(_in_hbm(slots), after)
    return out, sends, recvs


def _small_spread_wait(slots, sends, recvs, afters):
    def body(z_ref, sends, recvs, *rest):
        x, y, c = _position()
        mine = z_ref.at[2 * x + y]
        for s, k in enumerate(_CHIP_HOPS):
            px, py, _ = _peer(x, y, c, k)
            cp = _remote(mine, z_ref.at[2 * px + py], sends.at[s], recvs.at[s], (px, py, c))
            cp.wait_recv()
            cp.wait_send()

    return pl.pallas_call(
        body, name="small_spread_wait",
        in_specs=[_HBM, _SEM, _SEM] + [_ANY] * len(afters), out_specs=_HBM, out_shape=pltpu.HBM(slots.shape, F32),
        input_output_aliases={0: 0},
        compiler_params=pltpu.CompilerParams(has_side_effects=_EFFECT),
    )(slots, sends, recvs, *afters)


def _sibling_sum(pgs, name, small=None):
    n = len(pgs)
    k = 0 if small is None else 1
    units = [(i, j) for i in range(n) for j in range(N_CHIPS)]

    def body(*refs):
        refs = list(refs)
        take = lambda count: [refs.pop(0) for _ in range(count)]
        ins, small_in = take(n), take(k)
        qbs, owns, slots_out = take(n), take(n), take(k)
        mine, other, stage, got = take(n), take(n), take(n), take(n)
        load_a, load_b, send, recv = take(4)
        x, y, c = _position()
        chip = 2 * x + y
        if k:
            sib_ref, pair_send, pair_recv = take(3)
            pair = _remote(small_in[0], sib_ref, pair_send, pair_recv, (x, y, 1 - c))
            pair.start()
        loads_a = [pltpu.make_async_copy(ins[i].at[j, 1 - c], other[i].at[j], load_a.at[u])
                   for u, (i, j) in enumerate(units)]
        loads_b = [pltpu.make_async_copy(ins[i].at[j, c], mine[i].at[j], load_b.at[u])
                   for u, (i, j) in enumerate(units)]
        for cp in loads_a + loads_b:
            cp.start()
        sent = []
        for u, (i, j) in enumerate(units):
            loads_a[u].wait()
            stage[i][j] = other[i][j].astype(BF16)
            cp = _remote(stage[i].at[j], got[i].at[j], send.at[u], recv.at[u], (x, y, 1 - c))
            cp.start()
            sent.append(cp)
        for u, (i, j) in enumerate(units):
            loads_b[u].wait()
            sent[u].wait_recv()
            q = mine[i][j] + got[i][j].astype(F32)
            mine[i][j] = q
            qbs[i][j] = q.astype(BF16)
        for i in range(n):
            owns[i][...] = mine[i][chip]
        if k:
            pair.wait()
            slots_out[0][chip] = small_in[0][...] + sib_ref[...]
        for cp in sent:
            cp.wait_send()

    wire = [(N_CHIPS,) + p.shape[2:] for p in pgs]
    extra_out, extra_scratch = [], []
    if k:
        extra_out = [jax.ShapeDtypeStruct((N_CHIPS,) + small.shape, F32)]
        extra_scratch = [pltpu.VMEM(small.shape, F32), pltpu.SemaphoreType.DMA, pltpu.SemaphoreType.DMA]
    out = pl.pallas_call(
        body, name=name, in_specs=[_ANY] * n + [_VMEM] * k, out_specs=[_VMEM] * (2 * n + k),
        out_shape=[jax.ShapeDtypeStruct(w, BF16) for w in wire] + [jax.ShapeDtypeStruct(w[1:], F32) for w in wire]
        + extra_out,
        scratch_shapes=[pltpu.VMEM(w, F32) for w in wire] * 2 + [pltpu.VMEM(w, BF16) for w in wire] * 2
        + [pltpu.SemaphoreType.DMA((len(units),))] * 4 + extra_scratch,
        compiler_params=pltpu.CompilerParams(vmem_limit_bytes=VMEM_LIMIT),
    )(*pgs, *([small] if k else []))
    return list(out[:n]), list(out[n:2 * n]), list(out[2 * n:])


def _rs_start(qbs, name):
    n = len(qbs)

    def body(*refs):
        outs, inboxes = refs[:n], refs[n:2 * n]
        sends, recvs = refs[2 * n:3 * n], refs[3 * n:4 * n]
        x, y, c = _position()
        chip = 2 * x + y
        for i in range(n):
            for s, k in enumerate(_CHIP_HOPS):
                px, py, _ = _peer(x, y, c, k)
                _remote(outs[i].at[2 * px + py], inboxes[i].at[chip], sends[i].at[s], recvs[i].at[s], (px, py, c)).start()

    inboxes = [_in_hbm(lax.empty(q.shape, BF16)) for q in qbs]
    out = pl.pallas_call(
        body, name=name,
        in_specs=[_HBM] * (2 * n),
        out_specs=[_SEM] * (2 * n) + [_HBM] * (2 * n),
        out_shape=[_sems3()] * (2 * n) + [pltpu.HBM(q.shape, BF16) for q in qbs] * 2,
        input_output_aliases={i: 2 * n + i for i in range(2 * n)},
        compiler_params=pltpu.CompilerParams(has_side_effects=_EFFECT),
    )(*[_in_hbm(q) for q in qbs], *inboxes)
    return [(out[2 * n + i], out[3 * n + i], out[i], out[n + i]) for i in range(n)]


def _rs_wait(group, after, name):
    n = len(group)

    def body(*refs):
        outs, inboxes = refs[:n], refs[n:2 * n]
        sends, recvs = refs[2 * n:3 * n], refs[3 * n:4 * n]
        x, y, c = _position()
        for i in range(n):
            for s, k in enumerate(_CHIP_HOPS):
                px, py, _ = _peer(x, y, c, k)
                slot = 2 * px + py
                cp = _remote(outs[i].at[slot], inboxes[i].at[slot], sends[i].at[s], recvs[i].at[s], (px, py, c))
                cp.wait_recv()
                cp.wait_send()

    out = pl.pallas_call(
        body, name=name,
        in_specs=[_HBM] * (2 * n) + [_SEM] * (2 * n) + [_ANY],
        out_specs=[_HBM] * n,
        out_shape=[pltpu.HBM(g[1].shape, BF16) for g in group],
        input_output_aliases={n + i: i for i in range(n)},
        compiler_params=pltpu.CompilerParams(has_side_effects=_EFFECT),
    )(*[g[0] for g in group], *[g[1] for g in group], *[g[2] for g in group], *[g[3] for g in group], after)
    return list(out)


def _final_share(inboxes, owns, name):
    n = len(inboxes)
    units = [(i, s) for i in range(n) for s in range(len(_CHIP_HOPS))]

    def body(*refs):
        ins, mine, outs, landed = refs[:n], refs[n:2 * n], refs[2 * n:3 * n], refs[3 * n:4 * n]
        load, send, recv = refs[4 * n:]
        x, y, c = _position()
        loads = []
        for u, (i, s) in enumerate(units):
            px, py, _ = _peer(x, y, c, _CHIP_HOPS[s])
            loads.append(pltpu.make_async_copy(ins[i].at[2 * px + py], landed[i].at[s], load.at[u]))
        for cp in loads:
            cp.start()
        copies = []
        for i in range(n):
            for s in range(len(_CHIP_HOPS)):
                loads[len(_CHIP_HOPS) * i + s].wait()
            total = (landed[i][0].astype(F32) + landed[i][1].astype(F32)) + landed[i][2].astype(F32)
            outs[i][c] = total + mine[i][...]
            cp = _remote(outs[i].at[c], outs[i].at[c], send.at[i], recv.at[i], (x, y, 1 - c))
            cp.start()
            copies.append(cp)
        for i in range(n):
            theirs = outs[i].at[1 - c]
            _remote(theirs, theirs, send.at[i], recv.at[i], (x, y, 1 - c)).wait_recv()
        for cp in copies:
            cp.wait_send()

    return pl.pallas_call(
        body, name=name, in_specs=[_ANY] * n + [_VMEM] * n, out_specs=[_VMEM] * n,
        out_shape=[jax.ShapeDtypeStruct((2,) + o.shape, F32) for o in owns],
        scratch_shapes=[pltpu.VMEM((len(_CHIP_HOPS),) + o.shape, BF16) for o in owns]
        + [pltpu.SemaphoreType.DMA((len(units),)), pltpu.SemaphoreType.DMA((n,)), pltpu.SemaphoreType.DMA((n,))],
        compiler_params=pltpu.CompilerParams(vmem_limit_bytes=VMEM_LIMIT),
    )(*inboxes, *owns)


_SMALL = (("b_ada", N_MOD * D_MODEL), ("norm1_pre", D_MODEL), ("norm1_post", D_MODEL), ("norm2_pre", D_MODEL),
          ("norm2_post", D_MODEL), ("w_spatial", N_HEADS * CHUNK * CHUNK), ("b_spatial", N_HEADS * CHUNK),
          ("ln_v_gain", D_A), ("ln_v_bias", D_A), ("w_pool", N_HEADS * GROUP_DIM * GROUP_DIM),
          ("b_pool", D_B), ("pool_scale", D_B))
_MOD_ROWS = N_MOD * D_MODEL // LANES


def _packed_rows(size):
    return -(-(size // LANES) // SUBLANES) * SUBLANES


def _pack(parts):
    out = []
    for name, size in _SMALL:
        a = parts[name].reshape(size // LANES, LANES)
        pad = _packed_rows(size) - a.shape[0]
        out.append(jnp.pad(a, ((0, pad), (0, 0))) if pad else a)
    return out


def _small_adamw(slots, ws, ms, vs):
    n = len(_SMALL)
    head = N_DEV * _MOD_ROWS

    def body(*refs):
        s_ref, w, m, v = refs[0], refs[1:1 + n], refs[1 + n:1 + 2 * n], refs[1 + 2 * n:1 + 3 * n]
        outs = refs[1 + 3 * n:1 + 7 * n]
        dmod_ref, loss_ref, t_ref = refs[1 + 7 * n:]
        t_ref[...] = ((s_ref[0] + s_ref[1]) + s_ref[2]) + s_ref[3]
        dmod_ref[...] = t_ref[0:head, :]
        loss_ref[...] = t_ref[t_ref.shape[0] - 8:, :]
        row = head
        for i, (_, size) in enumerate(_SMALL):
            if i == 0:
                g = t_ref[0:_MOD_ROWS, :]
                for b in range(1, N_DEV):
                    g = g + t_ref[b * _MOD_ROWS:(b + 1) * _MOD_ROWS, :]
            else:
                g = t_ref[row:row + size // LANES, :]
                row += _packed_rows(size)
            d, nm, nv = _adamw_math(w[i][...], g, m[i][...], v[i][...])
            for ref, val in zip(outs[4 * i:4 * i + 4], (g, d, nm, nv)):
                ref[...] = val

    each = [jax.ShapeDtypeStruct((size // LANES, LANES), F32) for _, size in _SMALL for _ in range(4)]
    out = pl.pallas_call(
        body, name="small_adamw",
        out_shape=each + [jax.ShapeDtypeStruct((head, LANES), F32), jax.ShapeDtypeStruct((SUBLANES, LANES), F32)],
        scratch_shapes=[pltpu.VMEM(slots.shape[1:], F32)],
        compiler_params=pltpu.CompilerParams(vmem_limit_bytes=VMEM_LIMIT),
    )(slots, *ws, *ms, *vs)
    return [out[4 * i:4 * i + 4] for i in range(n)], out[4 * n], out[4 * n + 1]


def kernel(x, c, w_ada, b_ada, norm1_pre, norm1_post, w_in, w_spatial, b_spatial, ln_v_gain, ln_v_bias, w_pool, b_pool, pool_scale, w_out, norm2_pre, norm2_post, w_fc1, w_fc2, loss_target, m_w_ada, m_b_ada, m_norm1_pre, m_norm1_post, m_w_in, m_w_spatial, m_b_spatial, m_ln_v_gain, m_ln_v_bias, m_w_pool, m_b_pool, m_pool_scale, m_w_out, m_norm2_pre, m_norm2_post, m_w_fc1, m_w_fc2, v_w_ada, v_b_ada, v_norm1_pre, v_norm1_post, v_w_in, v_w_spatial, v_b_spatial, v_ln_v_gain, v_ln_v_bias, v_w_pool, v_b_pool, v_pool_scale, v_w_out, v_norm2_pre, v_norm2_post, v_w_fc1, v_w_fc2):
    weights = dict(w_ada=w_ada, b_ada=b_ada, norm1_pre=norm1_pre, norm1_post=norm1_post, w_in=w_in,
                   w_spatial=w_spatial, b_spatial=b_spatial, ln_v_gain=ln_v_gain, ln_v_bias=ln_v_bias, w_pool=w_pool,
                   b_pool=b_pool, pool_scale=pool_scale, w_out=w_out, norm2_pre=norm2_pre, norm2_post=norm2_post,
                   w_fc1=w_fc1, w_fc2=w_fc2)
    m_old = dict(w_ada=m_w_ada, b_ada=m_b_ada, norm1_pre=m_norm1_pre, norm1_post=m_norm1_post, w_in=m_w_in,
                 w_spatial=m_w_spatial, b_spatial=m_b_spatial, ln_v_gain=m_ln_v_gain, ln_v_bias=m_ln_v_bias,
                 w_pool=m_w_pool, b_pool=m_b_pool, pool_scale=m_pool_scale, w_out=m_w_out, norm2_pre=m_norm2_pre,
                 norm2_post=m_norm2_post, w_fc1=m_w_fc1, w_fc2=m_w_fc2)
    v_old = dict(w_ada=v_w_ada, b_ada=v_b_ada, norm1_pre=v_norm1_pre, norm1_post=v_norm1_post, w_in=v_w_in,
                 w_spatial=v_w_spatial, b_spatial=v_b_spatial, ln_v_gain=v_ln_v_gain, ln_v_bias=v_ln_v_bias,
                 w_pool=v_w_pool, b_pool=v_b_pool, pool_scale=v_pool_scale, w_out=v_w_out, norm2_pre=v_norm2_pre,
                 norm2_post=v_norm2_post, w_fc1=v_w_fc1, w_fc2=v_w_fc2)
    order = ("w_ada", "b_ada", "norm1_pre", "norm1_post", "w_in", "w_spatial", "b_spatial", "ln_v_gain", "ln_v_bias",
             "w_pool", "b_pool", "pool_scale", "w_out", "norm2_pre", "norm2_post", "w_fc1", "w_fc2")
    big = ("w_in", "w_out", "w_fc1", "w_fc2")
    mx, my, mc = _position()
    me = 4 * mx + 2 * my + mc
    chip = 2 * mx + my
    row = lambda a: a.reshape(1, -1)

    pos = jnp.stack([mc, chip]).astype(jnp.int32)
    xs, target = x[0], loss_target[0]
    n1pre, n1post, n2pre, n2post = row(norm1_pre), row(norm1_post), row(norm2_pre), row(norm2_post)
    mixer = (w_spatial, jnp.repeat(b_spatial.T, HEAD_DIM, axis=1), row(ln_v_gain), row(ln_v_bias), w_pool,
             row(b_pool), row(pool_scale))
    ts_big, ts_mid = 512, 256

    mod4, sc_all = _mod_exchange(c, w_ada, row(b_ada))
    mod6 = mod4.reshape(N_MOD, D_MODEL)
    ag = _ag_start([_cast_to_slot(weights[n], pos, mod4, "cast_" + n, 256) for n in big[:2]], mod4, "ag_start_mix")
    ag += _ag_start([_cast_to_slot(weights[n], pos, ag[0][0], "cast_" + n, 256) for n in big[2:]], ag[0][0],
                    "ag_start_mlp")

    win_g, wout_g = _ag_done(_ag_pass([ag[0], ag[1]], ag[2][0], "ag_pass_mix"), "ag_done_mix")
    z, ycat, mix, x1, h2 = _fwd_mix(xs, mod6, n1pre, n1post, n2pre, win_g, wout_g, *mixer, ts_big)
    (fc1_g,) = _ag_done(_ag_pass([ag[2]], h2, "ag_pass_fc1"), "ag_done_fc1")
    q = _fwd_fc1(h2, fc1_g, ts_big)
    (fc2_g,) = _ag_done(_ag_pass([ag[3]], q, "ag_pass_fc2"), "ag_done_fc2")
    dy, df, loss, s2 = _fwd_fc2_loss(q, x1, target, mod6, n2post, fc2_g, ts_big)

    def reduce_start(partials, tag, small=None):
        wire, owns, slots = _sibling_sum(partials, "sibling_sum_" + tag, small)
        return _rs_start(wire, "rs_start_" + tag), owns, slots

    def reduce_finish(state, owns, names, tag, dep):
        inboxes = _rs_wait(state, dep, "rs_wait_" + tag)
        shards = _final_share(inboxes, owns, "final_share_" + tag)
        for n, g in zip(names, shards):
            grads[n] = g.reshape(weights[n].shape)
            deltas[n], new_m[n], new_v[n] = _adamw(grads[n], weights[n], m_old[n], v_old[n], "adamw_" + n, 256)

    grads, deltas, new_m, new_v = {}, {}, {}, {}
    dp, g_fc2 = _bwd_fc2(df, q, fc2_g, ts_big)
    state_fc2, owns_fc2, _ = reduce_start([g_fc2], "fc2")
    dx1, dyc, dshift2, da2, s1, g_fc1, g_out = _bwd_fc1_out(
        dp, dy, x1, mix, h2, ycat, mod6, n2pre, n1post, fc1_g, wout_g, state_fc2[0][0], ts_mid)
    state_mid, owns_mid, _ = reduce_start([g_fc1, g_out], "mid")
    dz, dws, dbsp, dgain, dbias, dwp, dbp, dps = _mixer_bwd(z, dyc, *mixer, state_mid[0][0], ts_big)
    grad_x, dshift1, da1, g_in = _bwd_in(dz, dx1, xs, mod6, n1pre, win_g, state_mid[0][0], ts_big)
    dmod6, dnorms = _mod_grads(da1, dshift1, s1, da2, dshift2, s2, mod6, n1pre, n1post, n2pre, n2post)

    parts = dict(b_ada=dmod6, norm1_pre=dnorms[0], norm1_post=dnorms[1], norm2_pre=dnorms[2], norm2_post=dnorms[3],
                 w_spatial=dws, b_spatial=dbsp, ln_v_gain=dgain, ln_v_bias=dbias, w_pool=dwp, b_pool=dbp,
                 pool_scale=dps)
    pieces = _pack(parts)
    slots = lax.dynamic_update_slice(jnp.zeros((N_DEV * _MOD_ROWS, LANES), F32), pieces[0], (me * _MOD_ROWS, 0))
    loss_tile = jnp.pad(loss, ((0, SUBLANES - 1), (0, LANES - 1)))
    state_in, owns_in, pair_sum = reduce_start(
        [g_in], "in", jnp.concatenate([slots] + pieces[1:] + [loss_tile], axis=0))
    spread = _small_spread_start(pair_sum[0], state_in[0][0])
    reduce_finish(state_fc2 + state_mid, owns_fc2 + owns_mid, ("w_fc2", "w_fc1", "w_out"), "mlp", spread[0])
    flat = lambda d: [d[n].reshape(size // LANES, LANES) for n, size in _SMALL]
    small_out, dmod_all, loss_tile = _small_adamw(
        _small_spread_wait(*spread, [deltas[n] for n in ("w_fc2", "w_fc1", "w_out")]), flat(weights), flat(m_old),
        flat(v_old))
    loss = loss_tile[0, 0]
    for (n, _), (g, d, nm, nv) in zip(_SMALL, small_out):
        shape = weights[n].shape
        grads[n], deltas[n], new_m[n], new_v[n] = g.reshape(shape), d.reshape(shape), nm.reshape(shape), nv.reshape(shape)

    dmod_all = dmod_all.reshape(N_DEV, N_MOD * D_MODEL)
    cs = w_ada.shape[1]
    dmod_shard = lax.dynamic_slice(dmod_all, (0, chip * cs), (N_DEV, cs))
    sc_t = sc_all.reshape(N_DEV, D_MODEL).T
    grads["w_ada"], deltas["w_ada"], new_m["w_ada"], new_v["w_ada"] = _ada_grad_adamw(
        sc_t, dmod_shard, w_ada, m_w_ada, v_w_ada, 256)

    reduce_finish(state_in, owns_in, ("w_in",), "in", deltas["w_ada"])

    return (loss, grad_x[None], *[grads[n] for n in order], *[deltas[n] for n in order],
            *[new_m[n] for n in order], *[new_v[n] for n in order])
```

```python
import jax
import jax.numpy as jnp
from jax import lax
from jax.experimental import pallas as pl
from jax.experimental.pallas import tpu as pltpu

F32 = jnp.float32
BF16 = jnp.bfloat16
MESH = pl.DeviceIdType.MESH

D_MODEL = 1024
D_A = 512
D_B = 512
D_Z = 2 * D_A + D_B
N_HEADS = 4
HEAD_DIM = 128
CHUNK = 128
POOL_WINDOWS = (2, 4, 8, 16)
GROUP_DIM = 128
D_FF = 4096
N_MOD = 6
EPS = 1e-6
HALO = 16
N_CHIPS = 4
N_DEV = 8

ADAM_LR = 0.001
ADAM_B1 = 0.9
ADAM_B2 = 0.999
ADAM_EPS = 1e-08
ADAM_WD = 0.01
ADAM_STEP = 10

VMEM_LIMIT = 56 * 1024 * 1024
LANES = 128
SUBLANES = 8

_VMEM = pl.BlockSpec(memory_space=pltpu.VMEM)
_ANY = pl.BlockSpec(memory_space=pl.ANY)


def _params(n_grid_axes=1):
    return pltpu.CompilerParams(dimension_semantics=("arbitrary",) * n_grid_axes, vmem_limit_bytes=VMEM_LIMIT)


def _rows(ts, width):
    return pl.BlockSpec((ts, width), lambda i: (i, 0))


def _const(shape):
    return pl.BlockSpec(shape, lambda i: (0,) * len(shape))


def _dot(a, b):
    return jnp.dot(a, b, preferred_element_type=F32)


def _dot_nt(a, b):
    return lax.dot_general(a, b, (((1,), (1,)), ((), ())), preferred_element_type=F32)


def _dot_tn(a, b):
    return lax.dot_general(a, b, (((0,), (0,)), ((), ())), preferred_element_type=F32)


def _rowmean(v):
    return jnp.mean(v, axis=-1, keepdims=True)


def _colsum(v):
    return jnp.sum(v, axis=0, keepdims=True)


def _gelu_parts(z):
    k0 = 0.7978845608028654
    k1 = 0.044715
    z2 = z * z
    t = jnp.tanh(z * (k0 + (k0 * k1) * z2))
    u = 0.5 * t + 0.5
    g = z * u
    dg = u + (0.5 * z) * (1.0 - t * t) * (k0 + (3.0 * k0 * k1) * z2)
    return g, dg


def _tril_weights(ws_ref):
    r = lax.broadcasted_iota(jnp.int32, (CHUNK, CHUNK), 0)
    s = lax.broadcasted_iota(jnp.int32, (CHUNK, CHUNK), 1)
    mask = (s <= r).astype(F32)
    return [(ws_ref[h] * mask).astype(BF16) for h in range(N_HEADS)]


def _window_counts(first_row, n_rows):
    pos = (first_row + lax.broadcasted_iota(jnp.int32, (n_rows, 1), 0)).astype(F32)
    return pos, [1.0 / jnp.minimum(pos + 1.0, float(w)) for w in POOL_WINDOWS]


def _causal_window_sums(ext):
    out = []
    e = ext
    shift = 1
    for g in range(len(POOL_WINDOWS)):
        e = e + pltpu.roll(e, shift, 0)
        shift *= 2
        out.append(e[:, g * GROUP_DIM:(g + 1) * GROUP_DIM])
    return out


def _anticausal_window_sums(ext):
    n = ext.shape[0]
    out = []
    e = ext
    shift = 1
    for g in range(len(POOL_WINDOWS)):
        e = e + pltpu.roll(e, n - shift, 0)
        shift *= 2
        out.append(e[:, g * GROUP_DIM:(g + 1) * GROUP_DIM])
    return out


def _fwd_mix(x, mod6, n1pre, n1post, n2pre, win_g, wout_g, w_spatial, bsp_full, gain, bias, w_pool, b_pool, pool_scale, ts):
    s_len = x.shape[0]
    rs = D_MODEL // N_CHIPS

    def body(x_ref, mod_ref, g1pre_ref, g1post_ref, g2pre_ref, win_ref, wout_ref, ws_ref, bsp_ref, gain_ref,
             bias_ref, wp_ref, bp_ref, ps_ref, z_ref, y_ref, mix_ref, x1_ref, h2_ref, mixed_ref, prev_ref, wfull_ref):
        i = pl.program_id(0)
        _zero_on_first_step(prev_ref)
        _join_w_in_on_first_step(win_ref, wfull_ref)
        xv = x_ref[...]
        r = lax.rsqrt(_rowmean(xv * xv) + EPS)
        hb = ((xv * r) * (g1pre_ref[...] * (1.0 + mod_ref[1:2, :])) + mod_ref[0:1, :]).astype(BF16)
        z_ref[...] = _dot(hb, wfull_ref[...])

        wc = _tril_weights(ws_ref)
        u, _, _, _, _ = _mixer_forward_tile(z_ref[:, :2 * D_A], wc, bsp_ref, gain_ref[...], bias_ref[...], mixed_ref)
        y_ref[:, :D_A] = (u * mixed_ref[...]).astype(BF16)
        zb = z_ref[:, 2 * D_A:]
        sums = _causal_window_sums(jnp.concatenate([prev_ref[...], zb], axis=0))
        prev_ref[...] = zb[ts - HALO:, :]
        _, inv_counts = _window_counts(i * ts, ts)
        for g in range(len(POOL_WINDOWS)):
            lanes = slice(g * GROUP_DIM, (g + 1) * GROUP_DIM)
            diff = sums[g][HALO:, :] * inv_counts[g] - zb[:, lanes]
            lin = _dot(diff.astype(BF16), wp_ref[g].astype(BF16)) + bp_ref[:, lanes]
            y_ref[:, D_A + g * GROUP_DIM:D_A + (g + 1) * GROUP_DIM] = (lin * ps_ref[:, lanes]).astype(BF16)

        mix = None
        for j in range(N_CHIPS):
            part = _dot(y_ref[:, j * rs:(j + 1) * rs], wout_ref[j])
            mix = part if mix is None else mix + part
        mix_ref[...] = mix
        r2 = lax.rsqrt(_rowmean(mix * mix) + EPS)
        x1 = xv + (mix * r2) * (mod_ref[2:3, :] * g1post_ref[...])
        x1_ref[...] = x1
        r3 = lax.rsqrt(_rowmean(x1 * x1) + EPS)
        h2_ref[...] = ((x1 * r3) * (g2pre_ref[...] * (1.0 + mod_ref[4:5, :])) + mod_ref[3:4, :]).astype(BF16)

    vec = _const((1, D_MODEL))
    f32_rows = jax.ShapeDtypeStruct((s_len, D_MODEL), F32)
    bf16_rows = jax.ShapeDtypeStruct((s_len, D_MODEL), BF16)
    return pl.pallas_call(
        body, name="fwd_mix", grid=(s_len // ts,),
        in_specs=[_rows(ts, D_MODEL), _const((N_MOD, D_MODEL)), vec, vec, vec, _VMEM, _VMEM,
                  _const((N_HEADS, CHUNK, CHUNK)), _const((CHUNK, D_A)), _const((1, D_A)), _const((1, D_A)),
                  _const((N_HEADS, GROUP_DIM, GROUP_DIM)), _const((1, D_B)), _const((1, D_B))],
        out_specs=[_rows(ts, D_Z), _rows(ts, D_MODEL), _rows(ts, D_MODEL), _rows(ts, D_MODEL), _rows(ts, D_MODEL)],
        out_shape=[jax.ShapeDtypeStruct((s_len, D_Z), F32), bf16_rows, f32_rows, f32_rows, bf16_rows],
        scratch_shapes=[pltpu.VMEM((ts, D_A), F32), pltpu.VMEM((HALO, D_B), F32), pltpu.VMEM((D_MODEL, D_Z), BF16)],
        compiler_params=_params(),
    )(x, mod6, n1pre, n1post, n2pre, win_g, wout_g, w_spatial, bsp_full, gain, bias, w_pool, b_pool, pool_scale)


def _mixer_forward_tile(za, wc, bsp_ref, gain, bias, mixed_ref):
    ga, dga = _gelu_parts(za)
    u = ga[:, :D_A]
    v = ga[:, D_A:]
    mu = _rowmean(v)
    vc = v - mu
    rstd = lax.rsqrt(_rowmean(vc * vc) + EPS)
    vhat = vc * rstd
    vn = (vhat * gain + bias).astype(BF16)
    ts = za.shape[0]
    for k in range(ts // CHUNK):
        for h in range(N_HEADS):
            blk = vn[k * CHUNK:(k + 1) * CHUNK, h * HEAD_DIM:(h + 1) * HEAD_DIM]
            mixed_ref[k * CHUNK:(k + 1) * CHUNK, h * HEAD_DIM:(h + 1) * HEAD_DIM] = (
                _dot(wc[h], blk) + bsp_ref[:, h * HEAD_DIM:(h + 1) * HEAD_DIM])
    return u, vhat, rstd, vn, dga


def _fwd_fc1(h2, fc1_g, ts):
    s_len = h2.shape[0]
    cs = D_FF // N_CHIPS

    def body(h_ref, w_ref, q_ref):
        hb = h_ref[...]
        for j in range(N_CHIPS):
            p = jnp.maximum(_dot(hb, w_ref[j]), 0.0)
            q_ref[:, j * cs:(j + 1) * cs] = (p * p).astype(BF16)

    return pl.pallas_call(
        body, name="fwd_fc1", grid=(s_len // ts,),
        in_specs=[_rows(ts, D_MODEL), _VMEM],
        out_specs=_rows(ts, D_FF),
        out_shape=jax.ShapeDtypeStruct((s_len, D_FF), BF16),
        compiler_params=_params(),
    )(h2, fc1_g)


def _fwd_fc2_loss(q, x1, target, mod6, n2post, fc2_g, ts):
    s_len = q.shape[0]
    rs = D_FF // N_CHIPS

    def body(q_ref, x1_ref, t_ref, mod_ref, g_ref, w_ref, dy_ref, df_ref, loss_ref, s_ref):
        _zero_on_first_step(loss_ref, s_ref)
        gate_gain = mod_ref[5:6, :] * g_ref[...]
        f = _dot(q_ref[:, 0:rs], w_ref[0])
        for j in range(1, N_CHIPS):
            f = f + _dot(q_ref[:, j * rs:(j + 1) * rs], w_ref[j])
        r4 = lax.rsqrt(_rowmean(f * f) + EPS)
        fh = f * r4
        err = (x1_ref[...] + fh * gate_gain) - t_ref[...]
        loss_ref[...] += 0.5 * jnp.sum(_rowmean(err * err), axis=0, keepdims=True)
        dy = err * (1.0 / D_MODEL)
        dy_ref[...] = dy
        s_ref[...] += _colsum(dy * fh)
        gh = dy * gate_gain
        df_ref[...] = (r4 * (gh - fh * _rowmean(gh * fh))).astype(BF16)

    return pl.pallas_call(
        body, name="fwd_fc2_loss", grid=(s_len // ts,),
        in_specs=[_rows(ts, D_FF), _rows(ts, D_MODEL), _rows(ts, D_MODEL), _const((N_MOD, D_MODEL)),
                  _const((1, D_MODEL)), _VMEM],
        out_specs=[_rows(ts, D_MODEL), _rows(ts, D_MODEL), _const((1, 1)), _const((1, D_MODEL))],
        out_shape=[jax.ShapeDtypeStruct((s_len, D_MODEL), F32), jax.ShapeDtypeStruct((s_len, D_MODEL), BF16),
                   jax.ShapeDtypeStruct((1, 1), F32), jax.ShapeDtypeStruct((1, D_MODEL), F32)],
        compiler_params=_params(),
    )(q, x1, target, mod6, n2post, fc2_g)


def _join_w_in_on_first_step(win_ref, full_ref):
    cs = D_Z // N_CHIPS

    @pl.when(pl.program_id(0) == 0)
    def _():
        for j in range(N_CHIPS):
            full_ref[:, j * cs:(j + 1) * cs] = win_ref[j]


def _zero_on_first_step(*refs):
    @pl.when(pl.program_id(0) == 0)
    def _():
        for ref in refs:
            ref[...] = jnp.zeros_like(ref)


def _on_last_step(fn):
    pl.when(pl.program_id(0) == pl.num_programs(0) - 1)(fn)


def _store_shard_on_last_step(acc_ref, hbm_ref, sem, j):
    _on_last_step(lambda: pltpu.make_async_copy(acc_ref.at[j], hbm_ref.at[j], sem.at[j]).start())


def _wait_stores_on_last_step(*stores):
    def wait_all():
        for acc_ref, hbm_ref, sem in stores:
            for j in range(N_CHIPS):
                pltpu.make_async_copy(acc_ref.at[j], hbm_ref.at[j], sem.at[j]).wait()

    _on_last_step(wait_all)


def _bwd_fc2(df, q, fc2_g, ts):
    s_len = df.shape[0]
    cs = D_FF // N_CHIPS

    def body(df_ref, q_ref, w_ref, dp_ref, dw_hbm, dw_ref, dw_sem):
        _zero_on_first_step(dw_ref)
        dfb = df_ref[...]
        df2 = dfb * 2.0
        for j in range(N_CHIPS):
            qb = q_ref[:, j * cs:(j + 1) * cs]
            dw_ref[j] += _dot_tn(qb, dfb).reshape(2, cs // 2, D_MODEL)
            _store_shard_on_last_step(dw_ref, dw_hbm, dw_sem, j)
            dq2 = _dot_nt(df2, w_ref[j])
            dp_ref[:, j * cs:(j + 1) * cs] = (dq2 * jnp.sqrt(qb.astype(F32))).astype(BF16)
        _wait_stores_on_last_step((dw_ref, dw_hbm, dw_sem))

    dw_shape = (N_CHIPS, 2, cs // 2, D_MODEL)
    return pl.pallas_call(
        body, name="bwd_fc2", grid=(s_len // ts,),
        in_specs=[_rows(ts, D_MODEL), _rows(ts, D_FF), _VMEM],
        out_specs=[_rows(ts, D_FF), _ANY],
        out_shape=[jax.ShapeDtypeStruct((s_len, D_FF), BF16), jax.ShapeDtypeStruct(dw_shape, F32)],
        scratch_shapes=[pltpu.VMEM(dw_shape, F32), pltpu.SemaphoreType.DMA((N_CHIPS,))],
        compiler_params=_params(),
    )(df, q, fc2_g)


def _bwd_fc1_out(dp, dy, x1, mix, h2, ycat, mod6, n2pre, n1post, fc1_g, wout_g, dep, ts):
    s_len = dp.shape[0]
    cs = D_FF // N_CHIPS
    rs = D_MODEL // N_CHIPS

    def body(dp_ref, dy_ref, x1_ref, mix_ref, h2_ref, yc_ref, mod_ref, g2_ref, g1_ref, w1_ref, wo_ref, dep_ref,
             dx1_ref, dyc_ref, dshift2_ref, da2_ref, s1_ref, dw1_hbm, dwo_hbm, dw1_ref, dwo_ref, dw1_sem, dwo_sem):
        _zero_on_first_step(dshift2_ref, da2_ref, s1_ref, dw1_ref, dwo_ref)
        h2b = h2_ref[...]
        dh2 = None
        for j in range(N_CHIPS):
            dpb = dp_ref[:, j * cs:(j + 1) * cs]
            dw1_ref[j] += _dot_tn(h2b, dpb).reshape(2, D_MODEL // 2, cs)
            _store_shard_on_last_step(dw1_ref, dw1_hbm, dw1_sem, j)
            part = _dot_nt(dpb, w1_ref[j])
            dh2 = part if dh2 is None else dh2 + part
        x1 = x1_ref[...]
        r3 = lax.rsqrt(_rowmean(x1 * x1) + EPS)
        xh = x1 * r3
        a2 = g2_ref[...] * (1.0 + mod_ref[4:5, :])
        dshift2_ref[...] += _colsum(dh2)
        da2_ref[...] += _colsum(dh2 * xh)
        dxh = dh2 * a2
        dx1 = dy_ref[...] + r3 * (dxh - xh * _rowmean(dxh * xh))
        dx1_ref[...] = dx1

        mix = mix_ref[...]
        r2 = lax.rsqrt(_rowmean(mix * mix) + EPS)
        mh = mix * r2
        s1_ref[...] += _colsum(dx1 * mh)
        gh = dx1 * (mod_ref[2:3, :] * g1_ref[...])
        dmix = (r2 * (gh - mh * _rowmean(gh * mh))).astype(BF16)
        dwo_ref[...] += _dot_tn(yc_ref[...], dmix).reshape(N_CHIPS, 2, rs // 2, D_MODEL)
        for j in range(N_CHIPS):
            _store_shard_on_last_step(dwo_ref, dwo_hbm, dwo_sem, j)
            dyc_ref[:, j * rs:(j + 1) * rs] = _dot_nt(dmix, wo_ref[j])
        _wait_stores_on_last_step((dw1_ref, dw1_hbm, dw1_sem), (dwo_ref, dwo_hbm, dwo_sem))

    vec = jax.ShapeDtypeStruct((1, D_MODEL), F32)
    dw1_shape = (N_CHIPS, 2, D_MODEL // 2, cs)
    dwo_shape = (N_CHIPS, 2, rs // 2, D_MODEL)
    return pl.pallas_call(
        body, name="bwd_fc1_out", grid=(s_len // ts,),
        in_specs=[_rows(ts, D_FF), _rows(ts, D_MODEL), _rows(ts, D_MODEL), _rows(ts, D_MODEL), _rows(ts, D_MODEL),
                  _rows(ts, D_MODEL), _const((N_MOD, D_MODEL)), _const((1, D_MODEL)), _const((1, D_MODEL)), _VMEM,
                  _VMEM, _ANY],
        out_specs=[_rows(ts, D_MODEL), _rows(ts, D_MODEL)] + [_const((1, D_MODEL))] * 3 + [_ANY, _ANY],
        out_shape=[jax.ShapeDtypeStruct((s_len, D_MODEL), F32), jax.ShapeDtypeStruct((s_len, D_MODEL), F32),
                   vec, vec, vec, jax.ShapeDtypeStruct(dw1_shape, F32), jax.ShapeDtypeStruct(dwo_shape, F32)],
        scratch_shapes=[pltpu.VMEM(dw1_shape, F32), pltpu.VMEM(dwo_shape, F32), pltpu.SemaphoreType.DMA((N_CHIPS,)),
                        pltpu.SemaphoreType.DMA((N_CHIPS,))],
        compiler_params=_params(),
    )(dp, dy, x1, mix, h2, ycat, mod6, n2pre, n1post, fc1_g, wout_g, dep)


def _mixer_bwd(z, dyc, w_spatial, bsp_full, gain, bias, w_pool, b_pool, pool_scale, dep, ts):
    s_len = z.shape[0]
    nb = ts // HALO
    last = s_len // HALO - 1
    te = ts + HALO

    def body(z_ref, zprev_ref, znext_ref, dyc_ref, dynext_ref, ws_ref, bsp_ref, gain_ref, bias_ref, wp_ref, bp_ref,
             ps_ref, dep_ref, dz_ref, dws_ref, dbsp_ref, dgain_ref, dbias_ref, dwp_ref, dbp_ref, dps_ref, mixed_ref,
             dvn_ref):
        i = pl.program_id(0)

        @pl.when(i == 0)
        def _():
            for ref in (dws_ref, dbsp_ref, dgain_ref, dbias_ref, dwp_ref, dbp_ref, dps_ref):
                ref[...] = jnp.zeros_like(ref)

        wc = _tril_weights(ws_ref)
        gain = gain_ref[...]
        u, vhat, rstd, vn, dga = _mixer_forward_tile(z_ref[:, :2 * D_A], wc, bsp_ref, gain, bias_ref[...], mixed_ref)
        dya = dyc_ref[:, :D_A]
        du = dya * mixed_ref[...]
        dmixed = dya * u
        dmb = dmixed.astype(BF16)
        dm_sum = dmixed[0:CHUNK, :]
        for k in range(1, ts // CHUNK):
            dm_sum = dm_sum + dmixed[k * CHUNK:(k + 1) * CHUNK, :]
        r_idx = lax.broadcasted_iota(jnp.int32, (CHUNK, CHUNK), 0)
        s_idx = lax.broadcasted_iota(jnp.int32, (CHUNK, CHUNK), 1)
        causal = (s_idx <= r_idx).astype(F32)
        for h in range(N_HEADS):
            lanes = slice(h * HEAD_DIM, (h + 1) * HEAD_DIM)
            dbsp_ref[h] += jnp.sum(dm_sum[:, lanes], axis=1, keepdims=True)
            acc = None
            for k in range(ts // CHUNK):
                rows = slice(k * CHUNK, (k + 1) * CHUNK)
                t = _dot_nt(dmb[rows, lanes], vn[rows, lanes])
                acc = t if acc is None else acc + t
                dvn_ref[rows, lanes] = _dot_tn(wc[h], dmb[rows, lanes])
            dws_ref[h] += acc * causal
        dvn = dvn_ref[...]
        dgain_ref[...] += _colsum(dvn * vhat)
        dbias_ref[...] += _colsum(dvn)
        dvh = dvn * gain
        dv = rstd * (dvh - _rowmean(dvh) - vhat * _rowmean(dvh * vhat))
        dz_ref[:, :D_A] = (du * dga[:, :D_A]).astype(BF16)
        dz_ref[:, D_A:2 * D_A] = (dv * dga[:, D_A:]).astype(BF16)

        zb = z_ref[:, 2 * D_A:]
        prev = jnp.where(i == 0, 0.0, zprev_ref[...])
        zb_ext = jnp.concatenate([zb, znext_ref[...]], axis=0)
        sums = _causal_window_sums(jnp.concatenate([prev, zb_ext], axis=0))
        pos, inv_counts = _window_counts(i * ts, te)
        dyb_ext = jnp.concatenate([dyc_ref[:, D_A:], dynext_ref[...]], axis=0)
        dlin_ext = dyb_ext * ps_ref[...]
        dbp_ref[...] += _colsum(dlin_ext[:ts, :])
        scaled = []
        ddiffs = []
        lins = []
        for g in range(len(POOL_WINDOWS)):
            lanes = slice(g * GROUP_DIM, (g + 1) * GROUP_DIM)
            diff = (sums[g][HALO:, :] * inv_counts[g] - zb_ext[:, lanes]).astype(BF16)
            wpb = wp_ref[g].astype(BF16)
            dlb = dlin_ext[:, lanes].astype(BF16)
            lins.append(_dot(diff[:ts, :], wpb) + bp_ref[:, lanes])
            dwp_ref[g] += _dot_tn(diff[:ts, :], dlb[:ts, :])
            dd = _dot_nt(dlb, wpb)
            ddiffs.append(dd)
            scaled.append(jnp.where(pos < float(s_len), dd * inv_counts[g], 0.0))
        dps_ref[...] += _colsum(dyb_ext[:ts, :] * jnp.concatenate(lins, axis=1))
        back = _anticausal_window_sums(jnp.concatenate(scaled, axis=1))
        for g in range(len(POOL_WINDOWS)):
            dz_ref[:, 2 * D_A + g * GROUP_DIM:2 * D_A + (g + 1) * GROUP_DIM] = (
                back[g][:ts, :] - ddiffs[g][:ts, :]).astype(BF16)

    sq = jax.ShapeDtypeStruct((N_HEADS, CHUNK, CHUNK), F32)
    vec = jax.ShapeDtypeStruct((1, D_A), F32)
    return pl.pallas_call(
        body, name="mixer_bwd", grid=(s_len // ts,),
        in_specs=[_rows(ts, D_Z),
                  pl.BlockSpec((HALO, D_B), lambda i: (jnp.maximum(i * nb - 1, 0), 2)),
                  pl.BlockSpec((HALO, D_B), lambda i: (jnp.minimum((i + 1) * nb, last), 2)),
                  _rows(ts, D_MODEL),
                  pl.BlockSpec((HALO, D_B), lambda i: (jnp.minimum((i + 1) * nb, last), 1)),
                  _const((N_HEADS, CHUNK, CHUNK)), _const((CHUNK, D_A)), _const((1, D_A)), _const((1, D_A)),
                  _const((N_HEADS, GROUP_DIM, GROUP_DIM)), _const((1, D_B)), _const((1, D_B)), _ANY],
        out_specs=[_rows(ts, D_Z), _const((N_HEADS, CHUNK, CHUNK)), _const((N_HEADS, CHUNK, 1)), _const((1, D_A)),
                   _const((1, D_A)), _const((N_HEADS, GROUP_DIM, GROUP_DIM)), _const((1, D_B)), _const((1, D_B))],
        out_shape=[jax.ShapeDtypeStruct((s_len, D_Z), BF16), sq, jax.ShapeDtypeStruct((N_HEADS, CHUNK, 1), F32), vec,
                   vec, sq, vec, vec],
        scratch_shapes=[pltpu.VMEM((ts, D_A), F32), pltpu.VMEM((ts, D_A), F32)],
        compiler_params=_params(),
    )(z, z, z, dyc, dyc, w_spatial, bsp_full, gain, bias, w_pool, b_pool, pool_scale, dep)


def _bwd_in(dz, dx1, x, mod6, n1pre, win_g, dep, ts):
    s_len = x.shape[0]
    cs = D_Z // N_CHIPS

    def body(dz_ref, dx1_ref, x_ref, mod_ref, g_ref, w_ref, dep_ref, gx_ref, dshift_ref, da_ref, dw_hbm, dw_ref,
             wfull_ref, dw_sem):
        _zero_on_first_step(dshift_ref, da_ref, dw_ref)
        _join_w_in_on_first_step(w_ref, wfull_ref)
        xv = x_ref[...]
        r = lax.rsqrt(_rowmean(xv * xv) + EPS)
        xh = xv * r
        h1b = (xh * (g_ref[...] * (1.0 + mod_ref[1:2, :])) + mod_ref[0:1, :]).astype(BF16)
        dzb = dz_ref[...]
        dw = _dot_tn(h1b, dzb)
        for j in range(N_CHIPS):
            dw_ref[j] += dw[:, j * cs:(j + 1) * cs].reshape(2, D_MODEL // 2, cs)
        dh = _dot_nt(dzb, wfull_ref[...])
        a1 = g_ref[...] * (1.0 + mod_ref[1:2, :])
        dshift_ref[...] += _colsum(dh)
        da_ref[...] += _colsum(dh * xh)
        dxh = dh * a1
        gx_ref[...] = dx1_ref[...] + r * (dxh - xh * _rowmean(dxh * xh))
        for j in range(N_CHIPS):
            _store_shard_on_last_step(dw_ref, dw_hbm, dw_sem, j)
        _wait_stores_on_last_step((dw_ref, dw_hbm, dw_sem))

    vec = jax.ShapeDtypeStruct((1, D_MODEL), F32)
    dw_shape = (N_CHIPS, 2, D_MODEL // 2, cs)
    return pl.pallas_call(
        body, name="bwd_in", grid=(s_len // ts,),
        in_specs=[_rows(ts, D_Z), _rows(ts, D_MODEL), _rows(ts, D_MODEL), _const((N_MOD, D_MODEL)),
                  _const((1, D_MODEL)), _VMEM, _ANY],
        out_specs=[_rows(ts, D_MODEL), _const((1, D_MODEL)), _const((1, D_MODEL)), _ANY],
        out_shape=[jax.ShapeDtypeStruct((s_len, D_MODEL), F32), vec, vec, jax.ShapeDtypeStruct(dw_shape, F32)],
        scratch_shapes=[pltpu.VMEM(dw_shape, F32), pltpu.VMEM((D_MODEL, D_Z), BF16),
                        pltpu.SemaphoreType.DMA((N_CHIPS,))],
        compiler_params=_params(),
    )(dz, dx1, x, mod6, n1pre, win_g, dep)


def _adamw_math(w, g, m, v):
    m = ADAM_B1 * m + (1.0 - ADAM_B1) * g
    v = ADAM_B2 * v + (1.0 - ADAM_B2) * (g * g)
    m_hat = m / (1.0 - ADAM_B1 ** ADAM_STEP)
    v_hat = v / (1.0 - ADAM_B2 ** ADAM_STEP)
    delta = -ADAM_LR * (m_hat / (jnp.sqrt(v_hat) + ADAM_EPS) + ADAM_WD * w)
    return delta, m, v


def _adamw(g, w, m, v, name, tr):
    rows, cols = w.shape

    def body(g_ref, w_ref, m_ref, v_ref, d_ref, nm_ref, nv_ref):
        d, nm, nv = _adamw_math(w_ref[...], g_ref[...], m_ref[...], v_ref[...])
        d_ref[...] = d
        nm_ref[...] = nm
        nv_ref[...] = nv

    spec = _rows(tr, cols)
    shape = jax.ShapeDtypeStruct((rows, cols), F32)
    return pl.pallas_call(
        body, name=name, grid=(rows // tr,), in_specs=[spec] * 4, out_specs=[spec] * 3, out_shape=[shape] * 3,
        compiler_params=_params(),
    )(g, w, m, v)


def _ada_grad_adamw(sc_t, dmod_shard, w, m, v, tr):
    rows, cols = w.shape

    def body(s_ref, dm_ref, w_ref, m_ref, v_ref, g_ref, d_ref, nm_ref, nv_ref):
        g = s_ref[:, 0:1] * dm_ref[0:1, :]
        for b in range(1, N_DEV):
            g = g + s_ref[:, b:b + 1] * dm_ref[b:b + 1, :]
        g_ref[...] = g
        d, nm, nv = _adamw_math(w_ref[...], g, m_ref[...], v_ref[...])
        d_ref[...] = d
        nm_ref[...] = nm
        nv_ref[...] = nv

    spec = _rows(tr, cols)
    shape = jax.ShapeDtypeStruct((rows, cols), F32)
    return pl.pallas_call(
        body, name="ada_grad_adamw", grid=(rows // tr,),
        in_specs=[_rows(tr, N_DEV), _const((N_DEV, cols)), spec, spec, spec],
        out_specs=[spec] * 4, out_shape=[shape] * 4, compiler_params=_params(),
    )(sc_t, dmod_shard, w, m, v)


def _mod_grads(da1, dshift1, s1, da2, dshift2, s2, mod6, n1pre, n1post, n2pre, n2post):
    def body(da1_ref, ds1_ref, s1_ref, da2_ref, ds2_ref, s2_ref, mod_ref, n1_ref, p1_ref, n2_ref, p2_ref, dmod_ref,
             dn_ref):
        dmod_ref[0:1, :] = ds1_ref[...]
        dmod_ref[1:2, :] = da1_ref[...] * n1_ref[...]
        dmod_ref[2:3, :] = s1_ref[...] * p1_ref[...]
        dmod_ref[3:4, :] = ds2_ref[...]
        dmod_ref[4:5, :] = da2_ref[...] * n2_ref[...]
        dmod_ref[5:6, :] = s2_ref[...] * p2_ref[...]
        dn_ref[0:1, :] = da1_ref[...] * (1.0 + mod_ref[1:2, :])
        dn_ref[1:2, :] = s1_ref[...] * mod_ref[2:3, :]
        dn_ref[2:3, :] = da2_ref[...] * (1.0 + mod_ref[4:5, :])
        dn_ref[3:4, :] = s2_ref[...] * mod_ref[5:6, :]

    return pl.pallas_call(
        body, name="mod_grads",
        out_shape=[jax.ShapeDtypeStruct((N_MOD, D_MODEL), F32), jax.ShapeDtypeStruct((4, D_MODEL), F32)],
    )(da1, dshift1, s1, da2, dshift2, s2, mod6, n1pre, n1post, n2pre, n2post)


def _position():
    x, y, c = lax.axis_index("x"), lax.axis_index("y"), lax.axis_index("c")
    return x, y, c


def _flip(v, bit):
    return 1 - v if bit else v


def _peer(x, y, c, k):
    return (_flip(x, k & 4), _flip(y, k & 2), _flip(c, k & 1))


def _remote(src, dst, send_sem, recv_sem, device):
    return pltpu.make_async_remote_copy(src_ref=src, dst_ref=dst, send_sem=send_sem, recv_sem=recv_sem,
                                        device_id=device, device_id_type=MESH)


def _cast_to_slot(w, pos, dep, name, tr):
    rows, cols = w.shape

    def body(pos_ref, w_ref, dep_ref, o_ref):
        o_ref[0] = w_ref[...].astype(BF16)

    return pl.pallas_call(
        body, name=name,
        grid_spec=pltpu.PrefetchScalarGridSpec(
            num_scalar_prefetch=1, grid=(rows // tr,),
            in_specs=[pl.BlockSpec((tr, cols), lambda i, pos: (i, 0)), _ANY],
            out_specs=pl.BlockSpec((1, tr, cols), lambda i, pos: (pos[1], i, 0))),
        out_shape=jax.ShapeDtypeStruct((N_CHIPS, rows, cols), BF16), compiler_params=_params(),
    )(pos, w, dep)


def _mod_exchange(c_row, w_ada_shard, b_ada_row):
    cs = w_ada_shard.shape[1]

    def body(c_ref, w_hbm, b_ref, mod_ref, sc_ref, rows_ref, w_ref, w_sem, send1, recv1, send2, recv2):
        x, y, c = _position()
        me = 4 * x + 2 * y + c
        chip = 2 * x + y
        w_load = pltpu.make_async_copy(w_hbm, w_ref, w_sem)
        w_load.start()
        cv = c_ref[...]
        sc_ref[me] = cv * jax.nn.sigmoid(cv)
        gather = [_remote(sc_ref.at[me], sc_ref.at[me], send1.at[k - 1], recv1.at[k - 1], _peer(x, y, c, k))
                  for k in range(1, N_DEV)]
        for cp in gather:
            cp.start()
        for k in range(1, N_DEV):
            px, py, pc = _peer(x, y, c, k)
            src = 4 * px + 2 * py + pc
            _remote(sc_ref.at[src], sc_ref.at[src], send1.at[k - 1], recv1.at[k - 1], (px, py, pc)).wait_recv()
        for cp in gather:
            cp.wait_send()
        sc_all = jnp.concatenate([sc_ref[b] for b in range(N_DEV)], axis=0)
        w_load.wait()
        part = _dot(sc_all.astype(BF16), w_ref[...].astype(BF16))
        part = part + b_ref[:, pl.ds(pl.multiple_of(chip * cs, LANES), cs)]
        for b in range(N_DEV):
            rows_ref[b] = part[b:b + 1, :]
        mod_ref[chip] = rows_ref[me]
        hand = []
        for k in (2, 4, 6):
            px, py, _ = _peer(x, y, c, k)
            hand.append(_remote(rows_ref.at[4 * px + 2 * py + c], mod_ref.at[chip], send2.at[k // 2 - 1],
                                recv2.at[k // 2 - 1], (px, py, c)))
        for cp in hand:
            cp.start()
        for k in (2, 4, 6):
            px, py, _ = _peer(x, y, c, k)
            pchip = 2 * px + py
            _remote(rows_ref.at[me], mod_ref.at[pchip], send2.at[k // 2 - 1], recv2.at[k // 2 - 1],
                    (px, py, c)).wait_recv()
        for cp in hand:
            cp.wait_send()

    return pl.pallas_call(
        body, name="mod_exchange",
        in_specs=[_VMEM, _ANY, _VMEM], out_specs=[_VMEM, _VMEM],
        out_shape=[jax.ShapeDtypeStruct((N_CHIPS, 1, cs), F32), jax.ShapeDtypeStruct((N_DEV, 1, D_MODEL), F32)],
        scratch_shapes=[pltpu.VMEM((N_DEV, 1, cs), F32), pltpu.VMEM(w_ada_shard.shape, F32), pltpu.SemaphoreType.DMA,
                        pltpu.SemaphoreType.DMA((N_DEV - 1,)),
                        pltpu.SemaphoreType.DMA((N_DEV - 1,)), pltpu.SemaphoreType.DMA((N_CHIPS - 1,)),
                        pltpu.SemaphoreType.DMA((N_CHIPS - 1,))],
        compiler_params=pltpu.CompilerParams(vmem_limit_bytes=VMEM_LIMIT),
    )(c_row, w_ada_shard, b_ada_row)


_HBM = pl.BlockSpec(memory_space=pltpu.HBM)
_SEM = pl.BlockSpec(memory_space=pltpu.SEMAPHORE)
_EFFECT = pltpu.SideEffectType.DATAFLOW_SIDE_EFFECTING
_CHIP_HOPS = (2, 4, 6)


def _in_hbm(a):
    return pltpu.with_memory_space_constraint(a, pltpu.HBM)


def _sems3():
    return pltpu.SemaphoreType.DMA((len(_CHIP_HOPS),))


def _ag_start(lands, after, name):
    n = len(lands)

    def body(*refs):
        zones = refs[:n]
        sends, recvs = refs[n + 1:2 * n + 1], refs[2 * n + 1:3 * n + 1]
        x, y, c = _position()
        chip = 2 * x + y
        for i in range(n):
            half = zones[i].shape[1] // 2
            mine = zones[i].at[chip, pl.ds(c * half, half)]
            for s, k in enumerate(_CHIP_HOPS):
                px, py, _ = _peer(x, y, c, k)
                _remote(mine, mine, sends[i].at[s], recvs[i].at[s], (px, py, c)).start()

    out = pl.pallas_call(
        body, name=name,
        in_specs=[_HBM] * n + [_ANY],
        out_specs=[_SEM] * (2 * n) + [_HBM] * n,
        out_shape=[_sems3()] * (2 * n) + [pltpu.HBM(z.shape, BF16) for z in lands],
        input_output_aliases={i: 2 * n + i for i in range(n)},
        compiler_params=pltpu.CompilerParams(has_side_effects=_EFFECT),
    )(*[_in_hbm(z) for z in lands], after)
    return [(out[2 * n + i], out[i], out[n + i]) for i in range(n)]


def _ag_pass(group, after, name):
    n = len(group)

    def body(*refs):
        zones = refs[:n]
        sends, recvs = refs[n:2 * n], refs[2 * n:3 * n]
        fsends, frecvs = refs[4 * n + 1:5 * n + 1], refs[5 * n + 1:6 * n + 1]
        x, y, c = _position()
        chip = 2 * x + y
        for i in range(n):
            half = zones[i].shape[1] // 2
            rows = pl.ds(c * half, half)
            for s, k in enumerate(_CHIP_HOPS):
                px, py, _ = _peer(x, y, c, k)
                landed = zones[i].at[2 * px + py, rows]
                _remote(landed, landed, sends[i].at[s], recvs[i].at[s], (px, py, c)).wait_recv()
                _remote(landed, landed, fsends[i].at[s], frecvs[i].at[s], (x, y, 1 - c)).start()
        for i in range(n):
            half = zones[i].shape[1] // 2
            mine = zones[i].at[chip, pl.ds(c * half, half)]
            for s, k in enumerate(_CHIP_HOPS):
                px, py, _ = _peer(x, y, c, k)
                _remote(mine, mine, sends[i].at[s], recvs[i].at[s], (px, py, c)).wait_send()

    out = pl.pallas_call(
        body, name=name,
        in_specs=[_HBM] * n + [_SEM] * (2 * n) + [_ANY],
        out_specs=[_HBM] * n + [_SEM] * (2 * n),
        out_shape=[pltpu.HBM(g[0].shape, BF16) for g in group] + [_sems3()] * (2 * n),
        input_output_aliases={i: i for i in range(n)},
        compiler_params=pltpu.CompilerParams(has_side_effects=_EFFECT),
    )(*[g[0] for g in group], *[g[1] for g in group], *[g[2] for g in group], after)
    return [(out[i], out[n + i], out[2 * n + i]) for i in range(n)]


def _ag_done(group, name):
    n = len(group)

    def body(*refs):
        lands = refs[:n]
        fsends, frecvs = refs[n:2 * n], refs[2 * n:3 * n]
        x, y, c = _position()
        for i in range(n):
            half = lands[i].shape[1] // 2
            for s, k in enumerate(_CHIP_HOPS):
                px, py, _ = _peer(x, y, c, k)
                sent = lands[i].at[2 * px + py, pl.ds(c * half, half)]
                got = lands[i].at[2 * px + py, pl.ds((1 - c) * half, half)]
                cp = _remote(sent, got, fsends[i].at[s], frecvs[i].at[s], (x, y, 1 - c))
                cp.wait_recv()
                cp.wait_send()

    out = pl.pallas_call(
        body, name=name,
        in_specs=[_HBM] * n + [_SEM] * (2 * n),
        out_specs=[_HBM] * n,
        out_shape=[pltpu.HBM(g[0].shape, BF16) for g in group],
        input_output_aliases={i: i for i in range(n)},
        compiler_params=pltpu.CompilerParams(has_side_effects=_EFFECT),
    )(*[g[0] for g in group], *[g[1] for g in group], *[g[2] for g in group])
    return list(out)


def _small_spread_start(slots, after):
    def body(z_ref, after_ref, sends, recvs, z_out):
        x, y, c = _position()
        mine = z_ref.at[2 * x + y]
        for s, k in enumerate(_CHIP_HOPS):
            px, py, _ = _peer(x, y, c, k)
            _remote(mine, mine, sends.at[s], recvs.at[s], (px, py, c)).start()

    sends, recvs, out = pl.pallas_call(
        body, name="small_spread_start",
        in_specs=[_HBM, _ANY], out_specs=[_SEM, _SEM, _HBM],
        out_shape=[_sems3(), _sems3(), pltpu.HBM(slots.shape, F32)],
        input_output_aliases={0: 2},
        compiler_params=pltpu.CompilerParams(has_side_effects=_EFFECT),
    )(_in_hbm(slots), after)
    return out, sends, recvs


def _small_spread_wait(slots, sends, recvs, afters):
    def body(z_ref, sends, recvs, *rest):
        x, y, c = _position()
        mine = z_ref.at[2 * x + y]
        for s, k in enumerate(_CHIP_HOPS):
            px, py, _ = _peer(x, y, c, k)
            cp = _remote(mine, z_ref.at[2 * px + py], sends.at[s], recvs.at[s], (px, py, c))
            cp.wait_recv()
            cp.wait_send()

    return pl.pallas_call(
        body, name="small_spread_wait",
        in_specs=[_HBM, _SEM, _SEM] + [_ANY] * len(afters), out_specs=_HBM, out_shape=pltpu.HBM(slots.shape, F32),
        input_output_aliases={0: 0},
        compiler_params=pltpu.CompilerParams(has_side_effects=_EFFECT),
    )(slots, sends, recvs, *afters)


def _sibling_sum(pgs, name, small=None):
    n = len(pgs)
    k = 0 if small is None else 1
    units = [(i, j) for i in range(n) for j in range(N_CHIPS)]

    def body(*refs):
        refs = list(refs)
        take = lambda count: [refs.pop(0) for _ in range(count)]
        ins, small_in = take(n), take(k)
        qbs, owns, slots_out = take(n), take(n), take(k)
        mine, other, stage, got = take(n), take(n), take(n), take(n)
        load_a, load_b, send, recv = take(4)
        x, y, c = _position()
        chip = 2 * x + y
        if k:
            sib_ref, pair_send, pair_recv = take(3)
            pair = _remote(small_in[0], sib_ref, pair_send, pair_recv, (x, y, 1 - c))
            pair.start()
        loads_a = [pltpu.make_async_copy(ins[i].at[j, 1 - c], other[i].at[j], load_a.at[u])
                   for u, (i, j) in enumerate(units)]
        loads_b = [pltpu.make_async_copy(ins[i].at[j, c], mine[i].at[j], load_b.at[u])
                   for u, (i, j) in enumerate(units)]
        for cp in loads_a + loads_b:
            cp.start()
        sent = []
        for u, (i, j) in enumerate(units):
            loads_a[u].wait()
            stage[i][j] = other[i][j].astype(BF16)
            cp = _remote(stage[i].at[j], got[i].at[j], send.at[u], recv.at[u], (x, y, 1 - c))
            cp.start()
            sent.append(cp)
        for u, (i, j) in enumerate(units):
            loads_b[u].wait()
            sent[u].wait_recv()
            q = mine[i][j] + got[i][j].astype(F32)
            mine[i][j] = q
            qbs[i][j] = q.astype(BF16)
        for i in range(n):
            owns[i][...] = mine[i][chip]
        if k:
            pair.wait()
            slots_out[0][chip] = small_in[0][...] + sib_ref[...]
        for cp in sent:
            cp.wait_send()

    wire = [(N_CHIPS,) + p.shape[2:] for p in pgs]
    extra_out, extra_scratch = [], []
    if k:
        extra_out = [jax.ShapeDtypeStruct((N_CHIPS,) + small.shape, F32)]
        extra_scratch = [pltpu.VMEM(small.shape, F32), pltpu.SemaphoreType.DMA, pltpu.SemaphoreType.DMA]
    out = pl.pallas_call(
        body, name=name, in_specs=[_ANY] * n + [_VMEM] * k, out_specs=[_VMEM] * (2 * n + k),
        out_shape=[jax.ShapeDtypeStruct(w, BF16) for w in wire] + [jax.ShapeDtypeStruct(w[1:], F32) for w in wire]
        + extra_out,
        scratch_shapes=[pltpu.VMEM(w, F32) for w in wire] * 2 + [pltpu.VMEM(w, BF16) for w in wire] * 2
        + [pltpu.SemaphoreType.DMA((len(units),))] * 4 + extra_scratch,
        compiler_params=pltpu.CompilerParams(vmem_limit_bytes=VMEM_LIMIT),
    )(*pgs, *([small] if k else []))
    return list(out[:n]), list(out[n:2 * n]), list(out[2 * n:])


def _rs_start(qbs, name):
    n = len(qbs)

    def body(*refs):
        outs, inboxes = refs[:n], refs[n:2 * n]
        sends, recvs = refs[2 * n:3 * n], refs[3 * n:4 * n]
        x, y, c = _position()
        chip = 2 * x + y
        for i in range(n):
            for s, k in enumerate(_CHIP_HOPS):
                px, py, _ = _peer(x, y, c, k)
                _remote(outs[i].at[2 * px + py], inboxes[i].at[chip], sends[i].at[s], recvs[i].at[s], (px, py, c)).start()

    inboxes = [_in_hbm(lax.empty(q.shape, BF16)) for q in qbs]
    out = pl.pallas_call(
        body, name=name,
        in_specs=[_HBM] * (2 * n),
        out_specs=[_SEM] * (2 * n) + [_HBM] * (2 * n),
        out_shape=[_sems3()] * (2 * n) + [pltpu.HBM(q.shape, BF16) for q in qbs] * 2,
        input_output_aliases={i: 2 * n + i for i in range(2 * n)},
        compiler_params=pltpu.CompilerParams(has_side_effects=_EFFECT),
    )(*[_in_hbm(q) for q in qbs], *inboxes)
    return [(out[2 * n + i], out[3 * n + i], out[i], out[n + i]) for i in range(n)]


def _rs_wait(group, after, name):
    n = len(group)

    def body(*refs):
        outs, inboxes = refs[:n], refs[n:2 * n]
        sends, recvs = refs[2 * n:3 * n], refs[3 * n:4 * n]
        x, y, c = _position()
        for i in range(n):
            for s, k in enumerate(_CHIP_HOPS):
                px, py, _ = _peer(x, y, c, k)
                slot = 2 * px + py
                cp = _remote(outs[i].at[slot], inboxes[i].at[slot], sends[i].at[s], recvs[i].at[s], (px, py, c))
                cp.wait_recv()
                cp.wait_send()

    out = pl.pallas_call(
        body, name=name,
        in_specs=[_HBM] * (2 * n) + [_SEM] * (2 * n) + [_ANY],
        out_specs=[_HBM] * n,
        out_shape=[pltpu.HBM(g[1].shape, BF16) for g in group],
        input_output_aliases={n + i: i for i in range(n)},
        compiler_params=pltpu.CompilerParams(has_side_effects=_EFFECT),
    )(*[g[0] for g in group], *[g[1] for g in group], *[g[2] for g in group], *[g[3] for g in group], after)
    return list(out)


def _final_share(inboxes, owns, name):
    n = len(inboxes)
    units = [(i, s) for i in range(n) for s in range(len(_CHIP_HOPS))]

    def body(*refs):
        ins, mine, outs, landed = refs[:n], refs[n:2 * n], refs[2 * n:3 * n], refs[3 * n:4 * n]
        load, send, recv = refs[4 * n:]
        x, y, c = _position()
        loads = []
        for u, (i, s) in enumerate(units):
            px, py, _ = _peer(x, y, c, _CHIP_HOPS[s])
            loads.append(pltpu.make_async_copy(ins[i].at[2 * px + py], landed[i].at[s], load.at[u]))
        for cp in loads:
            cp.start()
        copies = []
        for i in range(n):
            for s in range(len(_CHIP_HOPS)):
                loads[len(_CHIP_HOPS) * i + s].wait()
            total = (landed[i][0].astype(F32) + landed[i][1].astype(F32)) + landed[i][2].astype(F32)
            outs[i][c] = total + mine[i][...]
            cp = _remote(outs[i].at[c], outs[i].at[c], send.at[i], recv.at[i], (x, y, 1 - c))
            cp.start()
            copies.append(cp)
        for i in range(n):
            theirs = outs[i].at[1 - c]
            _remote(theirs, theirs, send.at[i], recv.at[i], (x, y, 1 - c)).wait_recv()
        for cp in copies:
            cp.wait_send()

    return pl.pallas_call(
        body, name=name, in_specs=[_ANY] * n + [_VMEM] * n, out_specs=[_VMEM] * n,
        out_shape=[jax.ShapeDtypeStruct((2,) + o.shape, F32) for o in owns],
        scratch_shapes=[pltpu.VMEM((len(_CHIP_HOPS),) + o.shape, BF16) for o in owns]
        + [pltpu.SemaphoreType.DMA((len(units),)), pltpu.SemaphoreType.DMA((n,)), pltpu.SemaphoreType.DMA((n,))],
        compiler_params=pltpu.CompilerParams(vmem_limit_bytes=VMEM_LIMIT),
    )(*inboxes, *owns)


_SMALL = (("b_ada", N_MOD * D_MODEL), ("norm1_pre", D_MODEL), ("norm1_post", D_MODEL), ("norm2_pre", D_MODEL),
          ("norm2_post", D_MODEL), ("w_spatial", N_HEADS * CHUNK * CHUNK), ("b_spatial", N_HEADS * CHUNK),
          ("ln_v_gain", D_A), ("ln_v_bias", D_A), ("w_pool", N_HEADS * GROUP_DIM * GROUP_DIM),
          ("b_pool", D_B), ("pool_scale", D_B))
_MOD_ROWS = N_MOD * D_MODEL // LANES


def _packed_rows(size):
    return -(-(size // LANES) // SUBLANES) * SUBLANES


def _pack(parts):
    out = []
    for name, size in _SMALL:
        a = parts[name].reshape(size // LANES, LANES)
        pad = _packed_rows(size) - a.shape[0]
        out.append(jnp.pad(a, ((0, pad), (0, 0))) if pad else a)
    return out


def _small_adamw(slots, ws, ms, vs):
    n = len(_SMALL)
    head = N_DEV * _MOD_ROWS

    def body(*refs):
        s_ref, w, m, v = refs[0], refs[1:1 + n], refs[1 + n:1 + 2 * n], refs[1 + 2 * n:1 + 3 * n]
        outs = refs[1 + 3 * n:1 + 7 * n]
        dmod_ref, loss_ref, t_ref = refs[1 + 7 * n:]
        t_ref[...] = ((s_ref[0] + s_ref[1]) + s_ref[2]) + s_ref[3]
        dmod_ref[...] = t_ref[0:head, :]
        loss_ref[...] = t_ref[t_ref.shape[0] - 8:, :]
        row = head
        for i, (_, size) in enumerate(_SMALL):
            if i == 0:
                g = t_ref[0:_MOD_ROWS, :]
                for b in range(1, N_DEV):
                    g = g + t_ref[b * _MOD_ROWS:(b + 1) * _MOD_ROWS, :]
            else:
                g = t_ref[row:row + size // LANES, :]
                row += _packed_rows(size)
            d, nm, nv = _adamw_math(w[i][...], g, m[i][...], v[i][...])
            for ref, val in zip(outs[4 * i:4 * i + 4], (g, d, nm, nv)):
                ref[...] = val

    each = [jax.ShapeDtypeStruct((size // LANES, LANES), F32) for _, size in _SMALL for _ in range(4)]
    out = pl.pallas_call(
        body, name="small_adamw",
        out_shape=each + [jax.ShapeDtypeStruct((head, LANES), F32), jax.ShapeDtypeStruct((SUBLANES, LANES), F32)],
        scratch_shapes=[pltpu.VMEM(slots.shape[1:], F32)],
        compiler_params=pltpu.CompilerParams(vmem_limit_bytes=VMEM_LIMIT),
    )(slots, *ws, *ms, *vs)
    return [out[4 * i:4 * i + 4] for i in range(n)], out[4 * n], out[4 * n + 1]


def kernel(x, c, w_ada, b_ada, norm1_pre, norm1_post, w_in, w_spatial, b_spatial, ln_v_gain, ln_v_bias, w_pool, b_pool, pool_scale, w_out, norm2_pre, norm2_post, w_fc1, w_fc2, loss_target, m_w_ada, m_b_ada, m_norm1_pre, m_norm1_post, m_w_in, m_w_spatial, m_b_spatial, m_ln_v_gain, m_ln_v_bias, m_w_pool, m_b_pool, m_pool_scale, m_w_out, m_norm2_pre, m_norm2_post, m_w_fc1, m_w_fc2, v_w_ada, v_b_ada, v_norm1_pre, v_norm1_post, v_w_in, v_w_spatial, v_b_spatial, v_ln_v_gain, v_ln_v_bias, v_w_pool, v_b_pool, v_pool_scale, v_w_out, v_norm2_pre, v_norm2_post, v_w_fc1, v_w_fc2):
    weights = dict(w_ada=w_ada, b_ada=b_ada, norm1_pre=norm1_pre, norm1_post=norm1_post, w_in=w_in,
                   w_spatial=w_spatial, b_spatial=b_spatial, ln_v_gain=ln_v_gain, ln_v_bias=ln_v_bias, w_pool=w_pool,
                   b_pool=b_pool, pool_scale=pool_scale, w_out=w_out, norm2_pre=norm2_pre, norm2_post=norm2_post,
                   w_fc1=w_fc1, w_fc2=w_fc2)
    m_old = dict(w_ada=m_w_ada, b_ada=m_b_ada, norm1_pre=m_norm1_pre, norm1_post=m_norm1_post, w_in=m_w_in,
                 w_spatial=m_w_spatial, b_spatial=m_b_spatial, ln_v_gain=m_ln_v_gain, ln_v_bias=m_ln_v_bias,
                 w_pool=m_w_pool, b_pool=m_b_pool, pool_scale=m_pool_scale, w_out=m_w_out, norm2_pre=m_norm2_pre,
                 norm2_post=m_norm2_post, w_fc1=m_w_fc1, w_fc2=m_w_fc2)
    v_old = dict(w_ada=v_w_ada, b_ada=v_b_ada, norm1_pre=v_norm1_pre, norm1_post=v_norm1_post, w_in=v_w_in,
                 w_spatial=v_w_spatial, b_spatial=v_b_spatial, ln_v_gain=v_ln_v_gain, ln_v_bias=v_ln_v_bias,
                 w_pool=v_w_pool, b_pool=v_b_pool, pool_scale=v_pool_scale, w_out=v_w_out, norm2_pre=v_norm2_pre,
                 norm2_post=v_norm2_post, w_fc1=v_w_fc1, w_fc2=v_w_fc2)
    order = ("w_ada", "b_ada", "norm1_pre", "norm1_post", "w_in", "w_spatial", "b_spatial", "ln_v_gain", "ln_v_bias",
             "w_pool", "b_pool", "pool_scale", "w_out", "norm2_pre", "norm2_post", "w_fc1", "w_fc2")
    big = ("w_in", "w_out", "w_fc1", "w_fc2")
    mx, my, mc = _position()
    me = 4 * mx + 2 * my + mc
    chip = 2 * mx + my
    row = lambda a: a.reshape(1, -1)

    pos = jnp.stack([mc, chip]).astype(jnp.int32)
    xs, target = x[0], loss_target[0]
    n1pre, n1post, n2pre, n2post = row(norm1_pre), row(norm1_post), row(norm2_pre), row(norm2_post)
    mixer = (w_spatial, jnp.repeat(b_spatial.T, HEAD_DIM, axis=1), row(ln_v_gain), row(ln_v_bias), w_pool,
             row(b_pool), row(pool_scale))
    ts_big, ts_mid = 512, 256

    mod4, sc_all = _mod_exchange(c, w_ada, row(b_ada))
    mod6 = mod4.reshape(N_MOD, D_MODEL)
    ag = _ag_start([_cast_to_slot(weights[n], pos, mod4, "cast_" + n, 256) for n in big[:2]], mod4, "ag_start_mix")
    ag += _ag_start([_cast_to_slot(weights[n], pos, ag[0][0], "cast_" + n, 256) for n in big[2:]], ag[0][0],
                    "ag_start_mlp")

    win_g, wout_g = _ag_done(_ag_pass([ag[0], ag[1]], ag[2][0], "ag_pass_mix"), "ag_done_mix")
    z, ycat, mix, x1, h2 = _fwd_mix(xs, mod6, n1pre, n1post, n2pre, win_g, wout_g, *mixer, ts_big)
    (fc1_g,) = _ag_done(_ag_pass([ag[2]], h2, "ag_pass_fc1"), "ag_done_fc1")
    q = _fwd_fc1(h2, fc1_g, ts_big)
    (fc2_g,) = _ag_done(_ag_pass([ag[3]], q, "ag_pass_fc2"), "ag_done_fc2")
    dy, df, loss, s2 = _fwd_fc2_loss(q, x1, target, mod6, n2post, fc2_g, ts_big)

    def reduce_start(partials, tag, small=None):
        wire, owns, slots = _sibling_sum(partials, "sibling_sum_" + tag, small)
        return _rs_start(wire, "rs_start_" + tag), owns, slots

    def reduce_finish(state, owns, names, tag, dep):
        inboxes = _rs_wait(state, dep, "rs_wait_" + tag)
        shards = _final_share(inboxes, owns, "final_share_" + tag)
        for n, g in zip(names, shards):
            grads[n] = g.reshape(weights[n].shape)
            deltas[n], new_m[n], new_v[n] = _adamw(grads[n], weights[n], m_old[n], v_old[n], "adamw_" + n, 256)

    grads, deltas, new_m, new_v = {}, {}, {}, {}
    dp, g_fc2 = _bwd_fc2(df, q, fc2_g, ts_big)
    state_fc2, owns_fc2, _ = reduce_start([g_fc2], "fc2")
    dx1, dyc, dshift2, da2, s1, g_fc1, g_out = _bwd_fc1_out(
        dp, dy, x1, mix, h2, ycat, mod6, n2pre, n1post, fc1_g, wout_g, state_fc2[0][0], ts_mid)
    state_mid, owns_mid, _ = reduce_start([g_fc1, g_out], "mid")
    dz, dws, dbsp, dgain, dbias, dwp, dbp, dps = _mixer_bwd(z, dyc, *mixer, state_mid[0][0], ts_big)
    grad_x, dshift1, da1, g_in = _bwd_in(dz, dx1, xs, mod6, n1pre, win_g, state_mid[0][0], ts_big)
    dmod6, dnorms = _mod_grads(da1, dshift1, s1, da2, dshift2, s2, mod6, n1pre, n1post, n2pre, n2post)

    parts = dict(b_ada=dmod6, norm1_pre=dnorms[0], norm1_post=dnorms[1], norm2_pre=dnorms[2], norm2_post=dnorms[3],
                 w_spatial=dws, b_spatial=dbsp, ln_v_gain=dgain, ln_v_bias=dbias, w_pool=dwp, b_pool=dbp,
                 pool_scale=dps)
    pieces = _pack(parts)
    slots = lax.dynamic_update_slice(jnp.zeros((N_DEV * _MOD_ROWS, LANES), F32), pieces[0], (me * _MOD_ROWS, 0))
    loss_tile = jnp.pad(loss, ((0, SUBLANES - 1), (0, LANES - 1)))
    state_in, owns_in, pair_sum = reduce_start(
        [g_in], "in", jnp.concatenate([slots] + pieces[1:] + [loss_tile], axis=0))
    spread = _small_spread_start(pair_sum[0], state_in[0][0])
    reduce_finish(state_fc2 + state_mid, owns_fc2 + owns_mid, ("w_fc2", "w_fc1", "w_out"), "mlp", spread[0])
    flat = lambda d: [d[n].reshape(size // LANES, LANES) for n, size in _SMALL]
    small_out, dmod_all, loss_tile = _small_adamw(
        _small_spread_wait(*spread, [deltas[n] for n in ("w_fc2", "w_fc1", "w_out")]), flat(weights), flat(m_old),
        flat(v_old))
    loss = loss_tile[0, 0]
    for (n, _), (g, d, nm, nv) in zip(_SMALL, small_out):
        shape = weights[n].shape
        grads[n], deltas[n], new_m[n], new_v[n] = g.reshape(shape), d.reshape(shape), nm.reshape(shape), nv.reshape(shape)

    dmod_all = dmod_all.reshape(N_DEV, N_MOD * D_MODEL)
    cs = w_ada.shape[1]
    dmod_shard = lax.dynamic_slice(dmod_all, (0, chip * cs), (N_DEV, cs))
    sc_t = sc_all.reshape(N_DEV, D_MODEL).T
    grads["w_ada"], deltas["w_ada"], new_m["w_ada"], new_v["w_ada"] = _ada_grad_adamw(
        sc_t, dmod_shard, w_ada, m_w_ada, v_w_ada, 256)

    reduce_finish(state_in, owns_in, ("w_in",), "in", deltas["w_ada"])

    return (loss, grad_x[None], *[grads[n] for n in order], *[deltas[n] for n in order],
            *[new_m[n] for n in order], *[new_v[n] for n in order])
```

```python
import jax
import jax.numpy as jnp
from jax import lax
from jax.experimental import pallas as pl
from jax.experimental.pallas import tpu as pltpu

F32 = jnp.float32
BF16 = jnp.bfloat16
MESH = pl.DeviceIdType.MESH

D_MODEL = 1024
D_A = 512
D_B = 512
D_Z = 2 * D_A + D_B
N_HEADS = 4
HEAD_DIM = 128
CHUNK = 128
POOL_WINDOWS = (2, 4, 8, 16)
GROUP_DIM = 128
D_FF = 4096
N_MOD = 6
EPS = 1e-6
HALO = 16
N_CHIPS = 4
N_DEV = 8

ADAM_LR = 0.001
ADAM_B1 = 0.9
ADAM_B2 = 0.999
ADAM_EPS = 1e-08
ADAM_WD = 0.01
ADAM_STEP = 10

VMEM_LIMIT = 56 * 1024 * 1024
LANES = 128
SUBLANES = 8

_VMEM = pl.BlockSpec(memory_space=pltpu.VMEM)
_ANY = pl.BlockSpec(memory_space=pl.ANY)


def _params(n_grid_axes=1):
    return pltpu.CompilerParams(dimension_semantics=("arbitrary",) * n_grid_axes, vmem_limit_bytes=VMEM_LIMIT)


def _rows(ts, width):
    return pl.BlockSpec((ts, width), lambda i: (i, 0))


def _const(shape):
    return pl.BlockSpec(shape, lambda i: (0,) * len(shape))


def _dot(a, b):
    return jnp.dot(a, b, preferred_element_type=F32)


def _dot_nt(a, b):
    return lax.dot_general(a, b, (((1,), (1,)), ((), ())), preferred_element_type=F32)


def _dot_tn(a, b):
    return lax.dot_general(a, b, (((0,), (0,)), ((), ())), preferred_element_type=F32)


def _rowmean(v):
    return jnp.mean(v, axis=-1, keepdims=True)


def _colsum(v):
    return jnp.sum(v, axis=0, keepdims=True)


def _gelu_parts(z):
    k0 = 0.7978845608028654
    k1 = 0.044715
    z2 = z * z
    t = jnp.tanh(z * (k0 + (k0 * k1) * z2))
    u = 0.5 * t + 0.5
    g = z * u
    dg = u + (0.5 * z) * (1.0 - t * t) * (k0 + (3.0 * k0 * k1) * z2)
    return g, dg


def _tril_weights(ws_ref):
    r = lax.broadcasted_iota(jnp.int32, (CHUNK, CHUNK), 0)
    s = lax.broadcasted_iota(jnp.int32, (CHUNK, CHUNK), 1)
    mask = (s <= r).astype(F32)
    return [(ws_ref[h] * mask).astype(BF16) for h in range(N_HEADS)]


def _window_counts(first_row, n_rows):
    pos = (first_row + lax.broadcasted_iota(jnp.int32, (n_rows, 1), 0)).astype(F32)
    return pos, [1.0 / jnp.minimum(pos + 1.0, float(w)) for w in POOL_WINDOWS]


def _causal_window_sums(ext):
    out = []
    e = ext
    shift = 1
    for g in range(len(POOL_WINDOWS)):
        e = e + pltpu.roll(e, shift, 0)
        shift *= 2
        out.append(e[:, g * GROUP_DIM:(g + 1) * GROUP_DIM])
    return out


def _anticausal_window_sums(ext):
    n = ext.shape[0]
    out = []
    e = ext
    shift = 1
    for g in range(len(POOL_WINDOWS)):
        e = e + pltpu.roll(e, n - shift, 0)
        shift *= 2
        out.append(e[:, g * GROUP_DIM:(g + 1) * GROUP_DIM])
    return out


def _fwd_mix(x, mod6, n1pre, n1post, n2pre, win_g, wout_g, w_spatial, bsp_full, gain, bias, w_pool, b_pool, pool_scale, ts):
    s_len = x.shape[0]
    rs = D_MODEL // N_CHIPS

    def body(x_ref, mod_ref, g1pre_ref, g1post_ref, g2pre_ref, win_ref, wout_ref, ws_ref, bsp_ref, gain_ref,
             bias_ref, wp_ref, bp_ref, ps_ref, z_ref, y_ref, mix_ref, x1_ref, h2_ref, mixed_ref, prev_ref, wfull_ref):
        i = pl.program_id(0)
        _zero_on_first_step(prev_ref)
        _join_w_in_on_first_step(win_ref, wfull_ref)
        xv = x_ref[...]
        r = lax.rsqrt(_rowmean(xv * xv) + EPS)
        hb = ((xv * r) * (g1pre_ref[...] * (1.0 + mod_ref[1:2, :])) + mod_ref[0:1, :]).astype(BF16)
        z_ref[...] = _dot(hb, wfull_ref[...])

        wc = _tril_weights(ws_ref)
        u, _, _, _, _ = _mixer_forward_tile(z_ref[:, :2 * D_A], wc, bsp_ref, gain_ref[...], bias_ref[...], mixed_ref)
        y_ref[:, :D_A] = (u * mixed_ref[...]).astype(BF16)
        zb = z_ref[:, 2 * D_A:]
        sums = _causal_window_sums(jnp.concatenate([prev_ref[...], zb], axis=0))
        prev_ref[...] = zb[ts - HALO:, :]
        _, inv_counts = _window_counts(i * ts, ts)
        for g in range(len(POOL_WINDOWS)):
            lanes = slice(g * GROUP_DIM, (g + 1) * GROUP_DIM)
            diff = sums[g][HALO:, :] * inv_counts[g] - zb[:, lanes]
            lin = _dot(diff.astype(BF16), wp_ref[g].astype(BF16)) + bp_ref[:, lanes]
            y_ref[:, D_A + g * GROUP_DIM:D_A + (g + 1) * GROUP_DIM] = (lin * ps_ref[:, lanes]).astype(BF16)

        mix = None
        for j in range(N_CHIPS):
            part = _dot(y_ref[:, j * rs:(j + 1) * rs], wout_ref[j])
            mix = part if mix is None else mix + part
        mix_ref[...] = mix
        r2 = lax.rsqrt(_rowmean(mix * mix) + EPS)
        x1 = xv + (mix * r2) * (mod_ref[2:3, :] * g1post_ref[...])
        x1_ref[...] = x1
        r3 = lax.rsqrt(_rowmean(x1 * x1) + EPS)
        h2_ref[...] = ((x1 * r3) * (g2pre_ref[...] * (1.0 + mod_ref[4:5, :])) + mod_ref[3:4, :]).astype(BF16)

    vec = _const((1, D_MODEL))
    f32_rows = jax.ShapeDtypeStruct((s_len, D_MODEL), F32)
    bf16_rows = jax.ShapeDtypeStruct((s_len, D_MODEL), BF16)
    return pl.pallas_call(
        body, name="fwd_mix", grid=(s_len // ts,),
        in_specs=[_rows(ts, D_MODEL), _const((N_MOD, D_MODEL)), vec, vec, vec, _VMEM, _VMEM,
                  _const((N_HEADS, CHUNK, CHUNK)), _const((CHUNK, D_A)), _const((1, D_A)), _const((1, D_A)),
                  _const((N_HEADS, GROUP_DIM, GROUP_DIM)), _const((1, D_B)), _const((1, D_B))],
        out_specs=[_rows(ts, D_Z), _rows(ts, D_MODEL), _rows(ts, D_MODEL), _rows(ts, D_MODEL), _rows(ts, D_MODEL)],
        out_shape=[jax.ShapeDtypeStruct((s_len, D_Z), F32), bf16_rows, f32_rows, f32_rows, bf16_rows],
        scratch_shapes=[pltpu.VMEM((ts, D_A), F32), pltpu.VMEM((HALO, D_B), F32), pltpu.VMEM((D_MODEL, D_Z), BF16)],
        compiler_params=_params(),
    )(x, mod6, n1pre, n1post, n2pre, win_g, wout_g, w_spatial, bsp_full, gain, bias, w_pool, b_pool, pool_scale)


def _mixer_forward_tile(za, wc, bsp_ref, gain, bias, mixed_ref):
    ga, dga = _gelu_parts(za)
    u = ga[:, :D_A]
    v = ga[:, D_A:]
    mu = _rowmean(v)
    vc = v - mu
    rstd = lax.rsqrt(_rowmean(vc * vc) + EPS)
    vhat = vc * rstd
    vn = (vhat * gain + bias).astype(BF16)
    ts = za.shape[0]
    for k in range(ts // CHUNK):
        for h in range(N_HEADS):
            blk = vn[k * CHUNK:(k + 1) * CHUNK, h * HEAD_DIM:(h + 1) * HEAD_DIM]
            mixed_ref[k * CHUNK:(k + 1) * CHUNK, h * HEAD_DIM:(h + 1) * HEAD_DIM] = (
                _dot(wc[h], blk) + bsp_ref[:, h * HEAD_DIM:(h + 1) * HEAD_DIM])
    return u, vhat, rstd, vn, dga


def _fwd_fc1(h2, fc1_g, ts):
    s_len = h2.shape[0]
    cs = D_FF // N_CHIPS

    def body(h_ref, w_ref, q_ref):
        hb = h_ref[...]
        for j in range(N_CHIPS):
            p = jnp.maximum(_dot(hb, w_ref[j]), 0.0)
            q_ref[:, j * cs:(j + 1) * cs] = (p * p).astype(BF16)

    return pl.pallas_call(
        body, name="fwd_fc1", grid=(s_len // ts,),
        in_specs=[_rows(ts, D_MODEL), _VMEM],
        out_specs=_rows(ts, D_FF),
        out_shape=jax.ShapeDtypeStruct((s_len, D_FF), BF16),
        compiler_params=_params(),
    )(h2, fc1_g)


def _fwd_fc2_loss(q, x1, target, mod6, n2post, fc2_g, ts):
    s_len = q.shape[0]
    rs = D_FF // N_CHIPS

    def body(q_ref, x1_ref, t_ref, mod_ref, g_ref, w_ref, dy_ref, df_ref, loss_ref, s_ref):
        _zero_on_first_step(loss_ref, s_ref)
        gate_gain = mod_ref[5:6, :] * g_ref[...]
        f = _dot(q_ref[:, 0:rs], w_ref[0])
        for j in range(1, N_CHIPS):
            f = f + _dot(q_ref[:, j * rs:(j + 1) * rs], w_ref[j])
        r4 = lax.rsqrt(_rowmean(f * f) + EPS)
        fh = f * r4
        err = (x1_ref[...] + fh * gate_gain) - t_ref[...]
        loss_ref[...] += 0.5 * jnp.sum(_rowmean(err * err), axis=0, keepdims=True)
        dy = err * (1.0 / D_MODEL)
        dy_ref[...] = dy
        s_ref[...] += _colsum(dy * fh)
        gh = dy * gate_gain
        df_ref[...] = (r4 * (gh - fh * _rowmean(gh * fh))).astype(BF16)

    return pl.pallas_call(
        body, name="fwd_fc2_loss", grid=(s_len // ts,),
        in_specs=[_rows(ts, D_FF), _rows(ts, D_MODEL), _rows(ts, D_MODEL), _const((N_MOD, D_MODEL)),
                  _const((1, D_MODEL)), _VMEM],
        out_specs=[_rows(ts, D_MODEL), _rows(ts, D_MODEL), _const((1, 1)), _const((1, D_MODEL))],
        out_shape=[jax.ShapeDtypeStruct((s_len, D_MODEL), F32), jax.ShapeDtypeStruct((s_len, D_MODEL), BF16),
                   jax.ShapeDtypeStruct((1, 1), F32), jax.ShapeDtypeStruct((1, D_MODEL), F32)],
        compiler_params=_params(),
    )(q, x1, target, mod6, n2post, fc2_g)


def _join_w_in_on_first_step(win_ref, full_ref):
    cs = D_Z // N_CHIPS

    @pl.when(pl.program_id(0) == 0)
    def _():
        for j in range(N_CHIPS):
            full_ref[:, j * cs:(j + 1) * cs] = win_ref[j]


def _zero_on_first_step(*refs):
    @pl.when(pl.program_id(0) == 0)
    def _():
        for ref in refs:
            ref[...] = jnp.zeros_like(ref)


def _on_last_step(fn):
    pl.when(pl.program_id(0) == pl.num_programs(0) - 1)(fn)


def _store_shard_on_last_step(acc_ref, hbm_ref, sem, j):
    _on_last_step(lambda: pltpu.make_async_copy(acc_ref.at[j], hbm_ref.at[j], sem.at[j]).start())


def _wait_stores_on_last_step(*stores):
    def wait_all():
        for acc_ref, hbm_ref, sem in stores:
            for j in range(N_CHIPS):
                pltpu.make_async_copy(acc_ref.at[j], hbm_ref.at[j], sem.at[j]).wait()

    _on_last_step(wait_all)


def _bwd_fc2(df, q, fc2_g, ts):
    s_len = df.shape[0]
    cs = D_FF // N_CHIPS

    def body(df_ref, q_ref, w_ref, dp_ref, dw_hbm, dw_ref, dw_sem):
        _zero_on_first_step(dw_ref)
        dfb = df_ref[...]
        df2 = dfb * 2.0
        for j in range(N_CHIPS):
            qb = q_ref[:, j * cs:(j + 1) * cs]
            dw_ref[j] += _dot_tn(qb, dfb).reshape(2, cs // 2, D_MODEL)
            _store_shard_on_last_step(dw_ref, dw_hbm, dw_sem, j)
            dq2 = _dot_nt(df2, w_ref[j])
            dp_ref[:, j * cs:(j + 1) * cs] = (dq2 * jnp.sqrt(qb.astype(F32))).astype(BF16)
        _wait_stores_on_last_step((dw_ref, dw_hbm, dw_sem))

    dw_shape = (N_CHIPS, 2, cs // 2, D_MODEL)
    return pl.pallas_call(
        body, name="bwd_fc2", grid=(s_len // ts,),
        in_specs=[_rows(ts, D_MODEL), _rows(ts, D_FF), _VMEM],
        out_specs=[_rows(ts, D_FF), _ANY],
        out_shape=[jax.ShapeDtypeStruct((s_len, D_FF), BF16), jax.ShapeDtypeStruct(dw_shape, F32)],
        scratch_shapes=[pltpu.VMEM(dw_shape, F32), pltpu.SemaphoreType.DMA((N_CHIPS,))],
        compiler_params=_params(),
    )(df, q, fc2_g)


def _bwd_fc1_out(dp, dy, x1, mix, h2, ycat, mod6, n2pre, n1post, fc1_g, wout_g, dep, ts):
    s_len = dp.shape[0]
    cs = D_FF // N_CHIPS
    rs = D_MODEL // N_CHIPS

    def body(dp_ref, dy_ref, x1_ref, mix_ref, h2_ref, yc_ref, mod_ref, g2_ref, g1_ref, w1_ref, wo_ref, dep_ref,
             dx1_ref, dyc_ref, dshift2_ref, da2_ref, s1_ref, dw1_hbm, dwo_hbm, dw1_ref, dwo_ref, dw1_sem, dwo_sem):
        _zero_on_first_step(dshift2_ref, da2_ref, s1_ref, dw1_ref, dwo_ref)
        h2b = h2_ref[...]
        dh2 = None
        for j in range(N_CHIPS):
            dpb = dp_ref[:, j * cs:(j + 1) * cs]
            dw1_ref[j] += _dot_tn(h2b, dpb).reshape(2, D_MODEL // 2, cs)
            _store_shard_on_last_step(dw1_ref, dw1_hbm, dw1_sem, j)
            part = _dot_nt(dpb, w1_ref[j])
            dh2 = part if dh2 is None else dh2 + part
        x1 = x1_ref[...]
        r3 = lax.rsqrt(_rowmean(x1 * x1) + EPS)
        xh = x1 * r3
        a2 = g2_ref[...] * (1.0 + mod_ref[4:5, :])
        dshift2_ref[...] += _colsum(dh2)
        da2_ref[...] += _colsum(dh2 * xh)
        dxh = dh2 * a2
        dx1 = dy_ref[...] + r3 * (dxh - xh * _rowmean(dxh * xh))
        dx1_ref[...] = dx1

        mix = mix_ref[...]
        r2 = lax.rsqrt(_rowmean(mix * mix) + EPS)
        mh = mix * r2
        s1_ref[...] += _colsum(dx1 * mh)
        gh = dx1 * (mod_ref[2:3, :] * g1_ref[...])
        dmix = (r2 * (gh - mh * _rowmean(gh * mh))).astype(BF16)
        dwo_ref[...] += _dot_tn(yc_ref[...], dmix).reshape(N_CHIPS, 2, rs // 2, D_MODEL)
        for j in range(N_CHIPS):
            _store_shard_on_last_step(dwo_ref, dwo_hbm, dwo_sem, j)
            dyc_ref[:, j * rs:(j + 1) * rs] = _dot_nt(dmix, wo_ref[j])
        _wait_stores_on_last_step((dw1_ref, dw1_hbm, dw1_sem), (dwo_ref, dwo_hbm, dwo_sem))

    vec = jax.ShapeDtypeStruct((1, D_MODEL), F32)
    dw1_shape = (N_CHIPS, 2, D_MODEL // 2, cs)
    dwo_shape = (N_CHIPS, 2, rs // 2, D_MODEL)
    return pl.pallas_call(
        body, name="bwd_fc1_out", grid=(s_len // ts,),
        in_specs=[_rows(ts, D_FF), _rows(ts, D_MODEL), _rows(ts, D_MODEL), _rows(ts, D_MODEL), _rows(ts, D_MODEL),
                  _rows(ts, D_MODEL), _const((N_MOD, D_MODEL)), _const((1, D_MODEL)), _const((1, D_MODEL)), _VMEM,
                  _VMEM, _ANY],
        out_specs=[_rows(ts, D_MODEL), _rows(ts, D_MODEL)] + [_const((1, D_MODEL))] * 3 + [_ANY, _ANY],
        out_shape=[jax.ShapeDtypeStruct((s_len, D_MODEL), F32), jax.ShapeDtypeStruct((s_len, D_MODEL), F32),
                   vec, vec, vec, jax.ShapeDtypeStruct(dw1_shape, F32), jax.ShapeDtypeStruct(dwo_shape, F32)],
        scratch_shapes=[pltpu.VMEM(dw1_shape, F32), pltpu.VMEM(dwo_shape, F32), pltpu.SemaphoreType.DMA((N_CHIPS,)),
                        pltpu.SemaphoreType.DMA((N_CHIPS,))],
        compiler_params=_params(),
    )(dp, dy, x1, mix, h2, ycat, mod6, n2pre, n1post, fc1_g, wout_g, dep)


def _mixer_bwd(z, dyc, w_spatial, bsp_full, gain, bias, w_pool, b_pool, pool_scale, dep, ts):
    s_len = z.shape[0]
    nb = ts // HALO
    last = s_len // HALO - 1
    te = ts + HALO

    def body(z_ref, zprev_ref, znext_ref, dyc_ref, dynext_ref, ws_ref, bsp_ref, gain_ref, bias_ref, wp_ref, bp_ref,
             ps_ref, dep_ref, dz_ref, dws_ref, dbsp_ref, dgain_ref, dbias_ref, dwp_ref, dbp_ref, dps_ref, mixed_ref,
             dvn_ref):
        i = pl.program_id(0)

        @pl.when(i == 0)
        def _():
            for ref in (dws_ref, dbsp_ref, dgain_ref, dbias_ref, dwp_ref, dbp_ref, dps_ref):
                ref[...] = jnp.zeros_like(ref)

        wc = _tril_weights(ws_ref)
        gain = gain_ref[...]
        u, vhat, rstd, vn, dga = _mixer_forward_tile(z_ref[:, :2 * D_A], wc, bsp_ref, gain, bias_ref[...], mixed_ref)
        dya = dyc_ref[:, :D_A]
        du = dya * mixed_ref[...]
        dmixed = dya * u
        dmb = dmixed.astype(BF16)
        dm_sum = dmixed[0:CHUNK, :]
        for k in range(1, ts // CHUNK):
            dm_sum = dm_sum + dmixed[k * CHUNK:(k + 1) * CHUNK, :]
        r_idx = lax.broadcasted_iota(jnp.int32, (CHUNK, CHUNK), 0)
        s_idx = lax.broadcasted_iota(jnp.int32, (CHUNK, CHUNK), 1)
        causal = (s_idx <= r_idx).astype(F32)
        for h in range(N_HEADS):
            lanes = slice(h * HEAD_DIM, (h + 1) * HEAD_DIM)
            dbsp_ref[h] += jnp.sum(dm_sum[:, lanes], axis=1, keepdims=True)
            acc = None
            for k in range(ts // CHUNK):
                rows = slice(k * CHUNK, (k + 1) * CHUNK)
                t = _dot_nt(dmb[rows, lanes], vn[rows, lanes])
                acc = t if acc is None else acc + t
                dvn_ref[rows, lanes] = _dot_tn(wc[h], dmb[rows, lanes])
            dws_ref[h] += acc * causal
        dvn = dvn_ref[...]
        dgain_ref[...] += _colsum(dvn * vhat)
        dbias_ref[...] += _colsum(dvn)
        dvh = dvn * gain
        dv = rstd * (dvh - _rowmean(dvh) - vhat * _rowmean(dvh * vhat))
        dz_ref[:, :D_A] = (du * dga[:, :D_A]).astype(BF16)
        dz_ref[:, D_A:2 * D_A] = (dv * dga[:, D_A:]).astype(BF16)

        zb = z_ref[:, 2 * D_A:]
        prev = jnp.where(i == 0, 0.0, zprev_ref[...])
        zb_ext = jnp.concatenate([zb, znext_ref[...]], axis=0)
        sums = _causal_window_sums(jnp.concatenate([prev, zb_ext], axis=0))
        pos, inv_counts = _window_counts(i * ts, te)
        dyb_ext = jnp.concatenate([dyc_ref[:, D_A:], dynext_ref[...]], axis=0)
        dlin_ext = dyb_ext * ps_ref[...]
        dbp_ref[...] += _colsum(dlin_ext[:ts, :])
        scaled = []
        ddiffs = []
        lins = []
        for g in range(len(POOL_WINDOWS)):
            lanes = slice(g * GROUP_DIM, (g + 1) * GROUP_DIM)
            diff = (sums[g][HALO:, :] * inv_counts[g] - zb_ext[:, lanes]).astype(BF16)
            wpb = wp_ref[g].astype(BF16)
            dlb = dlin_ext[:, lanes].astype(BF16)
            lins.append(_dot(diff[:ts, :], wpb) + bp_ref[:, lanes])
            dwp_ref[g] += _dot_tn(diff[:ts, :], dlb[:ts, :])
            dd = _dot_nt(dlb, wpb)
            ddiffs.append(dd)
            scaled.append(jnp.where(pos < float(s_len), dd * inv_counts[g], 0.0))
        dps_ref[...] += _colsum(dyb_ext[:ts, :] * jnp.concatenate(lins, axis=1))
        back = _anticausal_window_sums(jnp.concatenate(scaled, axis=1))
        for g in range(len(POOL_WINDOWS)):
            dz_ref[:, 2 * D_A + g * GROUP_DIM:2 * D_A + (g + 1) * GROUP_DIM] = (
                back[g][:ts, :] - ddiffs[g][:ts, :]).astype(BF16)

    sq = jax.ShapeDtypeStruct((N_HEADS, CHUNK, CHUNK), F32)
    vec = jax.ShapeDtypeStruct((1, D_A), F32)
    return pl.pallas_call(
        body, name="mixer_bwd", grid=(s_len // ts,),
        in_specs=[_rows(ts, D_Z),
                  pl.BlockSpec((HALO, D_B), lambda i: (jnp.maximum(i * nb - 1, 0), 2)),
                  pl.BlockSpec((HALO, D_B), lambda i: (jnp.minimum((i + 1) * nb, last), 2)),
                  _rows(ts, D_MODEL),
                  pl.BlockSpec((HALO, D_B), lambda i: (jnp.minimum((i + 1) * nb, last), 1)),
                  _const((N_HEADS, CHUNK, CHUNK)), _const((CHUNK, D_A)), _const((1, D_A)), _const((1, D_A)),
                  _const((N_HEADS, GROUP_DIM, GROUP_DIM)), _const((1, D_B)), _const((1, D_B)), _ANY],
        out_specs=[_rows(ts, D_Z), _const((N_HEADS, CHUNK, CHUNK)), _const((N_HEADS, CHUNK, 1)), _const((1, D_A)),
                   _const((1, D_A)), _const((N_HEADS, GROUP_DIM, GROUP_DIM)), _const((1, D_B)), _const((1, D_B))],
        out_shape=[jax.ShapeDtypeStruct((s_len, D_Z), BF16), sq, jax.ShapeDtypeStruct((N_HEADS, CHUNK, 1), F32), vec,
                   vec, sq, vec, vec],
        scratch_shapes=[pltpu.VMEM((ts, D_A), F32), pltpu.VMEM((ts, D_A), F32)],
        compiler_params=_params(),
    )(z, z, z, dyc, dyc, w_spatial, bsp_full, gain, bias, w_pool, b_pool, pool_scale, dep)


def _bwd_in(dz, dx1, x, mod6, n1pre, win_g, dep, ts):
    s_len = x.shape[0]
    cs = D_Z // N_CHIPS

    def body(dz_ref, dx1_ref, x_ref, mod_ref, g_ref, w_ref, dep_ref, gx_ref, dshift_ref, da_ref, dw_hbm, dw_ref,
             wfull_ref, dw_sem):
        _zero_on_first_step(dshift_ref, da_ref, dw_ref)
        _join_w_in_on_first_step(w_ref, wfull_ref)
        xv = x_ref[...]
        r = lax.rsqrt(_rowmean(xv * xv) + EPS)
        xh = xv * r
        h1b = (xh * (g_ref[...] * (1.0 + mod_ref[1:2, :])) + mod_ref[0:1, :]).astype(BF16)
        dzb = dz_ref[...]
        dw = _dot_tn(h1b, dzb)
        for j in range(N_CHIPS):
            dw_ref[j] += dw[:, j * cs:(j + 1) * cs].reshape(2, D_MODEL // 2, cs)
        dh = _dot_nt(dzb, wfull_ref[...])
        a1 = g_ref[...] * (1.0 + mod_ref[1:2, :])
        dshift_ref[...] += _colsum(dh)
        da_ref[...] += _colsum(dh * xh)
        dxh = dh * a1
        gx_ref[...] = dx1_ref[...] + r * (dxh - xh * _rowmean(dxh * xh))
        for j in range(N_CHIPS):
            _store_shard_on_last_step(dw_ref, dw_hbm, dw_sem, j)
        _wait_stores_on_last_step((dw_ref, dw_hbm, dw_sem))

    vec = jax.ShapeDtypeStruct((1, D_MODEL), F32)
    dw_shape = (N_CHIPS, 2, D_MODEL // 2, cs)
    return pl.pallas_call(
        body, name="bwd_in", grid=(s_len // ts,),
        in_specs=[_rows(ts, D_Z), _rows(ts, D_MODEL), _rows(ts, D_MODEL), _const((N_MOD, D_MODEL)),
                  _const((1, D_MODEL)), _VMEM, _ANY],
        out_specs=[_rows(ts, D_MODEL), _const((1, D_MODEL)), _const((1, D_MODEL)), _ANY],
        out_shape=[jax.ShapeDtypeStruct((s_len, D_MODEL), F32), vec, vec, jax.ShapeDtypeStruct(dw_shape, F32)],
        scratch_shapes=[pltpu.VMEM(dw_shape, F32), pltpu.VMEM((D_MODEL, D_Z), BF16),
                        pltpu.SemaphoreType.DMA((N_CHIPS,))],
        compiler_params=_params(),
    )(dz, dx1, x, mod6, n1pre, win_g, dep)


def _adamw_math(w, g, m, v):
    m = ADAM_B1 * m + (1.0 - ADAM_B1) * g
    v = ADAM_B2 * v + (1.0 - ADAM_B2) * (g * g)
    m_hat = m / (1.0 - ADAM_B1 ** ADAM_STEP)
    v_hat = v / (1.0 - ADAM_B2 ** ADAM_STEP)
    delta = -ADAM_LR * (m_hat / (jnp.sqrt(v_hat) + ADAM_EPS) + ADAM_WD * w)
    return delta, m, v


def _adamw(gs, ws, ms, vs, name, steps):
    n = len(ws)

    def body(*refs):
        for i in range(n):
            g_ref, w_ref, m_ref, v_ref = refs[4 * i:4 * i + 4]
            d, nm, nv = _adamw_math(w_ref[...], g_ref[...], m_ref[...], v_ref[...])
            for ref, val in zip(refs[4 * n + 3 * i:4 * n + 3 * i + 3], (d, nm, nv)):
                ref[...] = val

    specs = [_rows(w.shape[0] // steps, w.shape[1]) for w in ws]
    out = pl.pallas_call(
        body, name=name, grid=(steps,),
        in_specs=[s for s in specs for _ in range(4)], out_specs=[s for s in specs for _ in range(3)],
        out_shape=[jax.ShapeDtypeStruct(w.shape, F32) for w in ws for _ in range(3)],
        compiler_params=_params(),
    )(*[a for quad in zip(gs, ws, ms, vs) for a in quad])
    return [out[3 * i:3 * i + 3] for i in range(n)]


def _ada_grad_adamw(sc_t, dmod_shard, w, m, v, tr):
    rows, cols = w.shape

    def body(s_ref, dm_ref, w_ref, m_ref, v_ref, g_ref, d_ref, nm_ref, nv_ref):
        g = s_ref[:, 0:1] * dm_ref[0:1, :]
        for b in range(1, N_DEV):
            g = g + s_ref[:, b:b + 1] * dm_ref[b:b + 1, :]
        g_ref[...] = g
        d, nm, nv = _adamw_math(w_ref[...], g, m_ref[...], v_ref[...])
        d_ref[...] = d
        nm_ref[...] = nm
        nv_ref[...] = nv

    spec = _rows(tr, cols)
    shape = jax.ShapeDtypeStruct((rows, cols), F32)
    return pl.pallas_call(
        body, name="ada_grad_adamw", grid=(rows // tr,),
        in_specs=[_rows(tr, N_DEV), _const((N_DEV, cols)), spec, spec, spec],
        out_specs=[spec] * 4, out_shape=[shape] * 4, compiler_params=_params(),
    )(sc_t, dmod_shard, w, m, v)


def _mod_grads(da1, dshift1, s1, da2, dshift2, s2, mod6, n1pre, n1post, n2pre, n2post):
    def body(da1_ref, ds1_ref, s1_ref, da2_ref, ds2_ref, s2_ref, mod_ref, n1_ref, p1_ref, n2_ref, p2_ref, dmod_ref,
             dn_ref):
        dmod_ref[0:1, :] = ds1_ref[...]
        dmod_ref[1:2, :] = da1_ref[...] * n1_ref[...]
        dmod_ref[2:3, :] = s1_ref[...] * p1_ref[...]
        dmod_ref[3:4, :] = ds2_ref[...]
        dmod_ref[4:5, :] = da2_ref[...] * n2_ref[...]
        dmod_ref[5:6, :] = s2_ref[...] * p2_ref[...]
        dn_ref[0:1, :] = da1_ref[...] * (1.0 + mod_ref[1:2, :])
        dn_ref[1:2, :] = s1_ref[...] * mod_ref[2:3, :]
        dn_ref[2:3, :] = da2_ref[...] * (1.0 + mod_ref[4:5, :])
        dn_ref[3:4, :] = s2_ref[...] * mod_ref[5:6, :]

    return pl.pallas_call(
        body, name="mod_grads",
        out_shape=[jax.ShapeDtypeStruct((N_MOD, D_MODEL), F32), jax.ShapeDtypeStruct((4, D_MODEL), F32)],
    )(da1, dshift1, s1, da2, dshift2, s2, mod6, n1pre, n1post, n2pre, n2post)


def _position():
    x, y, c = lax.axis_index("x"), lax.axis_index("y"), lax.axis_index("c")
    return x, y, c


def _flip(v, bit):
    return 1 - v if bit else v


def _peer(x, y, c, k):
    return (_flip(x, k & 4), _flip(y, k & 2), _flip(c, k & 1))


def _remote(src, dst, send_sem, recv_sem, device):
    return pltpu.make_async_remote_copy(src_ref=src, dst_ref=dst, send_sem=send_sem, recv_sem=recv_sem,
                                        device_id=device, device_id_type=MESH)


def _cast_to_slots(ws, pos, dep, name, steps):
    n = len(ws)

    def body(pos_ref, *refs):
        for w_ref, o_ref in zip(refs[:n], refs[n + 1:]):
            o_ref[0] = w_ref[...].astype(BF16)

    return pl.pallas_call(
        body, name=name,
        grid_spec=pltpu.PrefetchScalarGridSpec(
            num_scalar_prefetch=1, grid=(steps,),
            in_specs=[pl.BlockSpec((w.shape[0] // steps, w.shape[1]), lambda i, pos: (i, 0)) for w in ws] + [_ANY],
            out_specs=[pl.BlockSpec((1, w.shape[0] // steps, w.shape[1]), lambda i, pos: (pos[1], i, 0)) for w in ws]),
        out_shape=[jax.ShapeDtypeStruct((N_CHIPS,) + w.shape, BF16) for w in ws], compiler_params=_params(),
    )(pos, *ws, dep)


def _mod_exchange(c_row, w_ada_shard, b_ada_row):
    cs = w_ada_shard.shape[1]

    def body(c_ref, w_hbm, b_ref, mod_ref, sc_ref, rows_ref, w_ref, w_sem, send1, recv1, send2, recv2):
        x, y, c = _position()
        me = 4 * x + 2 * y + c
        chip = 2 * x + y
        w_load = pltpu.make_async_copy(w_hbm, w_ref, w_sem)
        w_load.start()
        cv = c_ref[...]
        sc_ref[me] = cv * jax.nn.sigmoid(cv)
        gather = [_remote(sc_ref.at[me], sc_ref.at[me], send1.at[k - 1], recv1.at[k - 1], _peer(x, y, c, k))
                  for k in range(1, N_DEV)]
        for cp in gather:
            cp.start()
        for k in range(1, N_DEV):
            px, py, pc = _peer(x, y, c, k)
            src = 4 * px + 2 * py + pc
            _remote(sc_ref.at[src], sc_ref.at[src], send1.at[k - 1], recv1.at[k - 1], (px, py, pc)).wait_recv()
        for cp in gather:
            cp.wait_send()
        sc_all = jnp.concatenate([sc_ref[b] for b in range(N_DEV)], axis=0)
        w_load.wait()
        part = _dot(sc_all.astype(BF16), w_ref[...].astype(BF16))
        part = part + b_ref[:, pl.ds(pl.multiple_of(chip * cs, LANES), cs)]
        for b in range(N_DEV):
            rows_ref[b] = part[b:b + 1, :]
        mod_ref[chip] = rows_ref[me]
        hand = []
        for k in (2, 4, 6):
            px, py, _ = _peer(x, y, c, k)
            hand.append(_remote(rows_ref.at[4 * px + 2 * py + c], mod_ref.at[chip], send2.at[k // 2 - 1],
                                recv2.at[k // 2 - 1], (px, py, c)))
        for cp in hand:
            cp.start()
        for k in (2, 4, 6):
            px, py, _ = _peer(x, y, c, k)
            pchip = 2 * px + py
            _remote(rows_ref.at[me], mod_ref.at[pchip], send2.at[k // 2 - 1], recv2.at[k // 2 - 1],
                    (px, py, c)).wait_recv()
        for cp in hand:
            cp.wait_send()

    return pl.pallas_call(
        body, name="mod_exchange",
        in_specs=[_VMEM, _ANY, _VMEM], out_specs=[_VMEM, _VMEM],
        out_shape=[jax.ShapeDtypeStruct((N_CHIPS, 1, cs), F32), jax.ShapeDtypeStruct((N_DEV, 1, D_MODEL), F32)],
        scratch_shapes=[pltpu.VMEM((N_DEV, 1, cs), F32), pltpu.VMEM(w_ada_shard.shape, F32), pltpu.SemaphoreType.DMA,
                        pltpu.SemaphoreType.DMA((N_DEV - 1,)),
                        pltpu.SemaphoreType.DMA((N_DEV - 1,)), pltpu.SemaphoreType.DMA((N_CHIPS - 1,)),
                        pltpu.SemaphoreType.DMA((N_CHIPS - 1,))],
        compiler_params=pltpu.CompilerParams(vmem_limit_bytes=VMEM_LIMIT),
    )(c_row, w_ada_shard, b_ada_row)


_HBM = pl.BlockSpec(memory_space=pltpu.HBM)
_SEM = pl.BlockSpec(memory_space=pltpu.SEMAPHORE)
_EFFECT = pltpu.SideEffectType.DATAFLOW_SIDE_EFFECTING
_CHIP_HOPS = (2, 4, 6)


def _in_hbm(a):
    return pltpu.with_memory_space_constraint(a, pltpu.HBM)


def _sems3():
    return pltpu.SemaphoreType.DMA((len(_CHIP_HOPS),))


def _ag_start(lands, after, name):
    n = len(lands)

    def body(*refs):
        zones = refs[:n]
        sends, recvs = refs[n + 1:2 * n + 1], refs[2 * n + 1:3 * n + 1]
        x, y, c = _position()
        chip = 2 * x + y
        for i in range(n):
            half = zones[i].shape[1] // 2
            mine = zones[i].at[chip, pl.ds(c * half, half)]
            for s, k in enumerate(_CHIP_HOPS):
                px, py, _ = _peer(x, y, c, k)
                _remote(mine, mine, sends[i].at[s], recvs[i].at[s], (px, py, c)).start()

    out = pl.pallas_call(
        body, name=name,
        in_specs=[_HBM] * n + [_ANY],
        out_specs=[_SEM] * (2 * n) + [_HBM] * n,
        out_shape=[_sems3()] * (2 * n) + [pltpu.HBM(z.shape, BF16) for z in lands],
        input_output_aliases={i: 2 * n + i for i in range(n)},
        compiler_params=pltpu.CompilerParams(has_side_effects=_EFFECT),
    )(*[_in_hbm(z) for z in lands], after)
    return [(out[2 * n + i], out[i], out[n + i]) for i in range(n)]


def _ag_pass(group, after, name):
    n = len(group)

    def body(*refs):
        zones = refs[:n]
        sends, recvs = refs[n:2 * n], refs[2 * n:3 * n]
        fsends, frecvs = refs[4 * n + 1:5 * n + 1], refs[5 * n + 1:6 * n + 1]
        x, y, c = _position()
        chip = 2 * x + y
        for i in range(n):
            half = zones[i].shape[1] // 2
            rows = pl.ds(c * half, half)
            for s, k in enumerate(_CHIP_HOPS):
                px, py, _ = _peer(x, y, c, k)
                landed = zones[i].at[2 * px + py, rows]
                _remote(landed, landed, sends[i].at[s], recvs[i].at[s], (px, py, c)).wait_recv()
                _remote(landed, landed, fsends[i].at[s], frecvs[i].at[s], (x, y, 1 - c)).start()
        for i in range(n):
            half = zones[i].shape[1] // 2
            mine = zones[i].at[chip, pl.ds(c * half, half)]
            for s, k in enumerate(_CHIP_HOPS):
                px, py, _ = _peer(x, y, c, k)
                _remote(mine, mine, sends[i].at[s], recvs[i].at[s], (px, py, c)).wait_send()

    out = pl.pallas_call(
        body, name=name,
        in_specs=[_HBM] * n + [_SEM] * (2 * n) + [_ANY],
        out_specs=[_HBM] * n + [_SEM] * (2 * n),
        out_shape=[pltpu.HBM(g[0].shape, BF16) for g in group] + [_sems3()] * (2 * n),
        input_output_aliases={i: i for i in range(n)},
        compiler_params=pltpu.CompilerParams(has_side_effects=_EFFECT),
    )(*[g[0] for g in group], *[g[1] for g in group], *[g[2] for g in group], after)
    return [(out[i], out[n + i], out[2 * n + i]) for i in range(n)]


def _ag_done(group, name):
    n = len(group)

    def body(*refs):
        lands = refs[:n]
        fsends, frecvs = refs[n:2 * n], refs[2 * n:3 * n]
        x, y, c = _position()
        for i in range(n):
            half = lands[i].shape[1] // 2
            for s, k in enumerate(_CHIP_HOPS):
                px, py, _ = _peer(x, y, c, k)
                sent = lands[i].at[2 * px + py, pl.ds(c * half, half)]
                got = lands[i].at[2 * px + py, pl.ds((1 - c) * half, half)]
                cp = _remote(sent, got, fsends[i].at[s], frecvs[i].at[s], (x, y, 1 - c))
                cp.wait_recv()
                cp.wait_send()

    out = pl.pallas_call(
        body, name=name,
        in_specs=[_HBM] * n + [_SEM] * (2 * n),
        out_specs=[_HBM] * n,
        out_shape=[pltpu.HBM(g[0].shape, BF16) for g in group],
        input_output_aliases={i: i for i in range(n)},
        compiler_params=pltpu.CompilerParams(has_side_effects=_EFFECT),
    )(*[g[0] for g in group], *[g[1] for g in group], *[g[2] for g in group])
    return list(out)


def _small_spread_start(slots, after):
    def body(z_ref, after_ref, sends, recvs, z_out):
        x, y, c = _position()
        mine = z_ref.at[2 * x + y]
        for s, k in enumerate(_CHIP_HOPS):
            px, py, _ = _peer(x, y, c, k)
            _remote(mine, mine, sends.at[s], recvs.at[s], (px, py, c)).start()

    sends, recvs, out = pl.pallas_call(
        body, name="small_spread_start",
        in_specs=[_HBM, _ANY], out_specs=[_SEM, _SEM, _HBM],
        out_shape=[_sems3(), _sems3(), pltpu.HBM(slots.shape, F32)],
        input_output_aliases={0: 2},
        compiler_params=pltpu.CompilerParams(has_side_effects=_EFFECT),
    )(_in_hbm(slots), after)
    return out, sends, recvs


def _small_spread_wait(slots, sends, recvs, afters):
    def body(z_ref, sends, recvs, *rest):
        x, y, c = _position()
        mine = z_ref.at[2 * x + y]
        for s, k in enumerate(_CHIP_HOPS):
            px, py, _ = _peer(x, y, c, k)
            cp = _remote(mine, z_ref.at[2 * px + py], sends.at[s], recvs.at[s], (px, py, c))
            cp.wait_recv()
            cp.wait_send()

    return pl.pallas_call(
        body, name="small_spread_wait",
        in_specs=[_HBM, _SEM, _SEM] + [_ANY] * len(afters), out_specs=_HBM, out_shape=pltpu.HBM(slots.shape, F32),
        input_output_aliases={0: 0},
        compiler_params=pltpu.CompilerParams(has_side_effects=_EFFECT),
    )(slots, sends, recvs, *afters)


def _sibling_sum(pgs, name, small=None):
    n = len(pgs)
    k = 0 if small is None else 1
    units = [(i, j) for i in range(n) for j in range(N_CHIPS)]

    def body(*refs):
        refs = list(refs)
        take = lambda count: [refs.pop(0) for _ in range(count)]
        ins, small_in = take(n), take(k)
        qbs, owns, slots_out = take(n), take(n), take(k)
        mine, other, stage, got = take(n), take(n), take(n), take(n)
        load_a, load_b, send, recv = take(4)
        x, y, c = _position()
        chip = 2 * x + y
        if k:
            sib_ref, pair_send, pair_recv = take(3)
            pair = _remote(small_in[0], sib_ref, pair_send, pair_recv, (x, y, 1 - c))
            pair.start()
        loads_a = [pltpu.make_async_copy(ins[i].at[j, 1 - c], other[i].at[j], load_a.at[u])
                   for u, (i, j) in enumerate(units)]
        loads_b = [pltpu.make_async_copy(ins[i].at[j, c], mine[i].at[j], load_b.at[u])
                   for u, (i, j) in enumerate(units)]
        for cp in loads_a + loads_b:
            cp.start()
        sent = []
        for u, (i, j) in enumerate(units):
            loads_a[u].wait()
            stage[i][j] = other[i][j].astype(BF16)
            cp = _remote(stage[i].at[j], got[i].at[j], send.at[u], recv.at[u], (x, y, 1 - c))
            cp.start()
            sent.append(cp)
        for u, (i, j) in enumerate(units):
            loads_b[u].wait()
            sent[u].wait_recv()
            q = mine[i][j] + got[i][j].astype(F32)
            mine[i][j] = q
            qbs[i][j] = q.astype(BF16)
        for i in range(n):
            owns[i][...] = mine[i][chip]
        if k:
            pair.wait()
            slots_out[0][chip] = small_in[0][...] + sib_ref[...]
        for cp in sent:
            cp.wait_send()

    wire = [(N_CHIPS,) + p.shape[2:] for p in pgs]
    extra_out, extra_scratch = [], []
    if k:
        extra_out = [jax.ShapeDtypeStruct((N_CHIPS,) + small.shape, F32)]
        extra_scratch = [pltpu.VMEM(small.shape, F32), pltpu.SemaphoreType.DMA, pltpu.SemaphoreType.DMA]
    out = pl.pallas_call(
        body, name=name, in_specs=[_ANY] * n + [_VMEM] * k, out_specs=[_VMEM] * (2 * n + k),
        out_shape=[jax.ShapeDtypeStruct(w, BF16) for w in wire] + [jax.ShapeDtypeStruct(w[1:], F32) for w in wire]
        + extra_out,
        scratch_shapes=[pltpu.VMEM(w, F32) for w in wire] * 2 + [pltpu.VMEM(w, BF16) for w in wire] * 2
        + [pltpu.SemaphoreType.DMA((len(units),))] * 4 + extra_scratch,
        compiler_params=pltpu.CompilerParams(vmem_limit_bytes=VMEM_LIMIT),
    )(*pgs, *([small] if k else []))
    return list(out[:n]), list(out[n:2 * n]), list(out[2 * n:])


def _rs_start(qbs, name):
    n = len(qbs)

    def body(*refs):
        outs, inboxes = refs[:n], refs[n:2 * n]
        sends, recvs = refs[2 * n:3 * n], refs[3 * n:4 * n]
        x, y, c = _position()
        chip = 2 * x + y
        for i in range(n):
            for s, k in enumerate(_CHIP_HOPS):
                px, py, _ = _peer(x, y, c, k)
                _remote(outs[i].at[2 * px + py], inboxes[i].at[chip], sends[i].at[s], recvs[i].at[s], (px, py, c)).start()

    inboxes = [_in_hbm(lax.empty(q.shape, BF16)) for q in qbs]
    out = pl.pallas_call(
        body, name=name,
        in_specs=[_HBM] * (2 * n),
        out_specs=[_SEM] * (2 * n) + [_HBM] * (2 * n),
        out_shape=[_sems3()] * (2 * n) + [pltpu.HBM(q.shape, BF16) for q in qbs] * 2,
        input_output_aliases={i: 2 * n + i for i in range(2 * n)},
        compiler_params=pltpu.CompilerParams(has_side_effects=_EFFECT),
    )(*[_in_hbm(q) for q in qbs], *inboxes)
    return [(out[2 * n + i], out[3 * n + i], out[i], out[n + i]) for i in range(n)]


def _rs_wait(group, after, name):
    n = len(group)

    def body(*refs):
        outs, inboxes = refs[:n], refs[n:2 * n]
        sends, recvs = refs[2 * n:3 * n], refs[3 * n:4 * n]
        x, y, c = _position()
        for i in range(n):
            for s, k in enumerate(_CHIP_HOPS):
                px, py, _ = _peer(x, y, c, k)
                slot = 2 * px + py
                cp = _remote(outs[i].at[slot], inboxes[i].at[slot], sends[i].at[s], recvs[i].at[s], (px, py, c))
                cp.wait_recv()
                cp.wait_send()

    out = pl.pallas_call(
        body, name=name,
        in_specs=[_HBM] * (2 * n) + [_SEM] * (2 * n) + [_ANY],
        out_specs=[_HBM] * n,
        out_shape=[pltpu.HBM(g[1].shape, BF16) for g in group],
        input_output_aliases={n + i: i for i in range(n)},
        compiler_params=pltpu.CompilerParams(has_side_effects=_EFFECT),
    )(*[g[0] for g in group], *[g[1] for g in group], *[g[2] for g in group], *[g[3] for g in group], after)
    return list(out)


def _final_share(inboxes, owns, name):
    n = len(inboxes)
    units = [(i, s) for i in range(n) for s in range(len(_CHIP_HOPS))]

    def body(*refs):
        ins, mine, outs, landed = refs[:n], refs[n:2 * n], refs[2 * n:3 * n], refs[3 * n:4 * n]
        load, send, recv = refs[4 * n:]
        x, y, c = _position()
        loads = []
        for u, (i, s) in enumerate(units):
            px, py, _ = _peer(x, y, c, _CHIP_HOPS[s])
            loads.append(pltpu.make_async_copy(ins[i].at[2 * px + py], landed[i].at[s], load.at[u]))
        for cp in loads:
            cp.start()
        copies = []
        for i in range(n):
            for s in range(len(_CHIP_HOPS)):
                loads[len(_CHIP_HOPS) * i + s].wait()
            total = (landed[i][0].astype(F32) + landed[i][1].astype(F32)) + landed[i][2].astype(F32)
            outs[i][c] = total + mine[i][...]
            cp = _remote(outs[i].at[c], outs[i].at[c], send.at[i], recv.at[i], (x, y, 1 - c))
            cp.start()
            copies.append(cp)
        for i in range(n):
            theirs = outs[i].at[1 - c]
            _remote(theirs, theirs, send.at[i], recv.at[i], (x, y, 1 - c)).wait_recv()
        for cp in copies:
            cp.wait_send()

    return pl.pallas_call(
        body, name=name, in_specs=[_ANY] * n + [_VMEM] * n, out_specs=[_VMEM] * n,
        out_shape=[jax.ShapeDtypeStruct((2,) + o.shape, F32) for o in owns],
        scratch_shapes=[pltpu.VMEM((len(_CHIP_HOPS),) + o.shape, BF16) for o in owns]
        + [pltpu.SemaphoreType.DMA((len(units),)), pltpu.SemaphoreType.DMA((n,)), pltpu.SemaphoreType.DMA((n,))],
        compiler_params=pltpu.CompilerParams(vmem_limit_bytes=VMEM_LIMIT),
    )(*inboxes, *owns)


_SMALL = (("b_ada", N_MOD * D_MODEL), ("norm1_pre", D_MODEL), ("norm1_post", D_MODEL), ("norm2_pre", D_MODEL),
          ("norm2_post", D_MODEL), ("w_spatial", N_HEADS * CHUNK * CHUNK), ("b_spatial", N_HEADS * CHUNK),
          ("ln_v_gain", D_A), ("ln_v_bias", D_A), ("w_pool", N_HEADS * GROUP_DIM * GROUP_DIM),
          ("b_pool", D_B), ("pool_scale", D_B))
_MOD_ROWS = N_MOD * D_MODEL // LANES


def _packed_rows(size):
    return -(-(size // LANES) // SUBLANES) * SUBLANES


def _pack(parts):
    out = []
    for name, size in _SMALL:
        a = parts[name].reshape(size // LANES, LANES)
        pad = _packed_rows(size) - a.shape[0]
        out.append(jnp.pad(a, ((0, pad), (0, 0))) if pad else a)
    return out


def _small_adamw(slots, ws, ms, vs):
    n = len(_SMALL)
    head = N_DEV * _MOD_ROWS

    def body(*refs):
        s_ref, w, m, v = refs[0], refs[1:1 + n], refs[1 + n:1 + 2 * n], refs[1 + 2 * n:1 + 3 * n]
        outs = refs[1 + 3 * n:1 + 7 * n]
        dmod_ref, loss_ref, t_ref = refs[1 + 7 * n:]
        t_ref[...] = ((s_ref[0] + s_ref[1]) + s_ref[2]) + s_ref[3]
        dmod_ref[...] = t_ref[0:head, :]
        loss_ref[...] = t_ref[t_ref.shape[0] - 8:, :]
        row = head
        for i, (_, size) in enumerate(_SMALL):
            if i == 0:
                g = t_ref[0:_MOD_ROWS, :]
                for b in range(1, N_DEV):
                    g = g + t_ref[b * _MOD_ROWS:(b + 1) * _MOD_ROWS, :]
            else:
                g = t_ref[row:row + size // LANES, :]
                row += _packed_rows(size)
            d, nm, nv = _adamw_math(w[i][...], g, m[i][...], v[i][...])
            for ref, val in zip(outs[4 * i:4 * i + 4], (g, d, nm, nv)):
                ref[...] = val

    each = [jax.ShapeDtypeStruct((size // LANES, LANES), F32) for _, size in _SMALL for _ in range(4)]
    out = pl.pallas_call(
        body, name="small_adamw",
        out_shape=each + [jax.ShapeDtypeStruct((head, LANES), F32), jax.ShapeDtypeStruct((SUBLANES, LANES), F32)],
        scratch_shapes=[pltpu.VMEM(slots.shape[1:], F32)],
        compiler_params=pltpu.CompilerParams(vmem_limit_bytes=VMEM_LIMIT),
    )(slots, *ws, *ms, *vs)
    return [out[4 * i:4 * i + 4] for i in range(n)], out[4 * n], out[4 * n + 1]


def kernel(x, c, w_ada, b_ada, norm1_pre, norm1_post, w_in, w_spatial, b_spatial, ln_v_gain, ln_v_bias, w_pool, b_pool, pool_scale, w_out, norm2_pre, norm2_post, w_fc1, w_fc2, loss_target, m_w_ada, m_b_ada, m_norm1_pre, m_norm1_post, m_w_in, m_w_spatial, m_b_spatial, m_ln_v_gain, m_ln_v_bias, m_w_pool, m_b_pool, m_pool_scale, m_w_out, m_norm2_pre, m_norm2_post, m_w_fc1, m_w_fc2, v_w_ada, v_b_ada, v_norm1_pre, v_norm1_post, v_w_in, v_w_spatial, v_b_spatial, v_ln_v_gain, v_ln_v_bias, v_w_pool, v_b_pool, v_pool_scale, v_w_out, v_norm2_pre, v_norm2_post, v_w_fc1, v_w_fc2):
    weights = dict(w_ada=w_ada, b_ada=b_ada, norm1_pre=norm1_pre, norm1_post=norm1_post, w_in=w_in,
                   w_spatial=w_spatial, b_spatial=b_spatial, ln_v_gain=ln_v_gain, ln_v_bias=ln_v_bias, w_pool=w_pool,
                   b_pool=b_pool, pool_scale=pool_scale, w_out=w_out, norm2_pre=norm2_pre, norm2_post=norm2_post,
                   w_fc1=w_fc1, w_fc2=w_fc2)
    m_old = dict(w_ada=m_w_ada, b_ada=m_b_ada, norm1_pre=m_norm1_pre, norm1_post=m_norm1_post, w_in=m_w_in,
                 w_spatial=m_w_spatial, b_spatial=m_b_spatial, ln_v_gain=m_ln_v_gain, ln_v_bias=m_ln_v_bias,
                 w_pool=m_w_pool, b_pool=m_b_pool, pool_scale=m_pool_scale, w_out=m_w_out, norm2_pre=m_norm2_pre,
                 norm2_post=m_norm2_post, w_fc1=m_w_fc1, w_fc2=m_w_fc2)
    v_old = dict(w_ada=v_w_ada, b_ada=v_b_ada, norm1_pre=v_norm1_pre, norm1_post=v_norm1_post, w_in=v_w_in,
                 w_spatial=v_w_spatial, b_spatial=v_b_spatial, ln_v_gain=v_ln_v_gain, ln_v_bias=v_ln_v_bias,
                 w_pool=v_w_pool, b_pool=v_b_pool, pool_scale=v_pool_scale, w_out=v_w_out, norm2_pre=v_norm2_pre,
                 norm2_post=v_norm2_post, w_fc1=v_w_fc1, w_fc2=v_w_fc2)
    order = ("w_ada", "b_ada", "norm1_pre", "norm1_post", "w_in", "w_spatial", "b_spatial", "ln_v_gain", "ln_v_bias",
             "w_pool", "b_pool", "pool_scale", "w_out", "norm2_pre", "norm2_post", "w_fc1", "w_fc2")
    mx, my, mc = _position()
    me = 4 * mx + 2 * my + mc
    chip = 2 * mx + my
    row = lambda a: a.reshape(1, -1)

    pos = jnp.stack([mc, chip]).astype(jnp.int32)
    xs, target = x[0], loss_target[0]
    n1pre, n1post, n2pre, n2post = row(norm1_pre), row(norm1_post), row(norm2_pre), row(norm2_post)
    mixer = (w_spatial, jnp.repeat(b_spatial.T, HEAD_DIM, axis=1), row(ln_v_gain), row(ln_v_bias), w_pool,
             row(b_pool), row(pool_scale))
    ts_big, ts_mid = 512, 256

    mod4, sc_all = _mod_exchange(c, w_ada, row(b_ada))
    mod6 = mod4.reshape(N_MOD, D_MODEL)
    ag = _ag_start(_cast_to_slots([w_in, w_out], pos, mod4, "cast_mix", 1), mod4, "ag_start_mix")
    ag += _ag_start(_cast_to_slots([w_fc1, w_fc2], pos, ag[0][0], "cast_mlp", 4), ag[0][0], "ag_start_mlp")

    win_g, wout_g = _ag_done(_ag_pass([ag[0], ag[1]], ag[2][0], "ag_pass_mix"), "ag_done_mix")
    z, ycat, mix, x1, h2 = _fwd_mix(xs, mod6, n1pre, n1post, n2pre, win_g, wout_g, *mixer, ts_big)
    (fc1_g,) = _ag_done(_ag_pass([ag[2]], h2, "ag_pass_fc1"), "ag_done_fc1")
    q = _fwd_fc1(h2, fc1_g, ts_big)
    (fc2_g,) = _ag_done(_ag_pass([ag[3]], q, "ag_pass_fc2"), "ag_done_fc2")
    dy, df, loss, s2 = _fwd_fc2_loss(q, x1, target, mod6, n2post, fc2_g, ts_big)

    def reduce_start(partials, tag, small=None):
        wire, owns, slots = _sibling_sum(partials, "sibling_sum_" + tag, small)
        return _rs_start(wire, "rs_start_" + tag), owns, slots

    def reduce_finish(state, owns, names, tag, dep):
        inboxes = _rs_wait(state, dep, "rs_wait_" + tag)
        shards = _final_share(inboxes, owns, "final_share_" + tag)
        for n, g in zip(names, shards):
            grads[n] = g.reshape(weights[n].shape)
        updates = _adamw([grads[n] for n in names], [weights[n] for n in names], [m_old[n] for n in names],
                         [v_old[n] for n in names], "adamw_" + tag, 4)
        for n, (d, nm, nv) in zip(names, updates):
            deltas[n], new_m[n], new_v[n] = d, nm, nv

    grads, deltas, new_m, new_v = {}, {}, {}, {}
    dp, g_fc2 = _bwd_fc2(df, q, fc2_g, ts_big)
    state_fc2, owns_fc2, _ = reduce_start([g_fc2], "fc2")
    dx1, dyc, dshift2, da2, s1, g_fc1, g_out = _bwd_fc1_out(
        dp, dy, x1, mix, h2, ycat, mod6, n2pre, n1post, fc1_g, wout_g, state_fc2[0][0], ts_mid)
    state_mid, owns_mid, _ = reduce_start([g_fc1, g_out], "mid")
    dz, dws, dbsp, dgain, dbias, dwp, dbp, dps = _mixer_bwd(z, dyc, *mixer, state_mid[0][0], ts_big)
    grad_x, dshift1, da1, g_in = _bwd_in(dz, dx1, xs, mod6, n1pre, win_g, state_mid[0][0], ts_big)
    dmod6, dnorms = _mod_grads(da1, dshift1, s1, da2, dshift2, s2, mod6, n1pre, n1post, n2pre, n2post)

    parts = dict(b_ada=dmod6, norm1_pre=dnorms[0], norm1_post=dnorms[1], norm2_pre=dnorms[2], norm2_post=dnorms[3],
                 w_spatial=dws, b_spatial=dbsp, ln_v_gain=dgain, ln_v_bias=dbias, w_pool=dwp, b_pool=dbp,
                 pool_scale=dps)
    pieces = _pack(parts)
    slots = lax.dynamic_update_slice(jnp.zeros((N_DEV * _MOD_ROWS, LANES), F32), pieces[0], (me * _MOD_ROWS, 0))
    loss_tile = jnp.pad(loss, ((0, SUBLANES - 1), (0, LANES - 1)))
    state_in, owns_in, pair_sum = reduce_start(
        [g_in], "in", jnp.concatenate([slots] + pieces[1:] + [loss_tile], axis=0))
    spread = _small_spread_start(pair_sum[0], state_in[0][0])
    reduce_finish(state_fc2 + state_mid, owns_fc2 + owns_mid, ("w_fc2", "w_fc1", "w_out"), "mlp", spread[0])
    flat = lambda d: [d[n].reshape(size // LANES, LANES) for n, size in _SMALL]
    small_out, dmod_all, loss_tile = _small_adamw(
        _small_spread_wait(*spread, [deltas[n] for n in ("w_fc2", "w_fc1", "w_out")]), flat(weights), flat(m_old),
        flat(v_old))
    loss = loss_tile[0, 0]
    for (n, _), (g, d, nm, nv) in zip(_SMALL, small_out):
        shape = weights[n].shape
        grads[n], deltas[n], new_m[n], new_v[n] = g.reshape(shape), d.reshape(shape), nm.reshape(shape), nv.reshape(shape)

    dmod_all = dmod_all.reshape(N_DEV, N_MOD * D_MODEL)
    cs = w_ada.shape[1]
    dmod_shard = lax.dynamic_slice(dmod_all, (0, chip * cs), (N_DEV, cs))
    sc_t = sc_all.reshape(N_DEV, D_MODEL).T
    grads["w_ada"], deltas["w_ada"], new_m["w_ada"], new_v["w_ada"] = _ada_grad_adamw(
        sc_t, dmod_shard, w_ada, m_w_ada, v_w_ada, 256)

    reduce_finish(state_in, owns_in, ("w_in",), "in", deltas["w_ada"])

    return (loss, grad_x[None], *[grads[n] for n in order], *[deltas[n] for n in order],
            *[new_m[n] for n in order], *[new_v[n] for n in order])
```

```python
import jax
import jax.numpy as jnp
from jax import lax
from jax.experimental import pallas as pl
from jax.experimental.pallas import tpu as pltpu

F32 = jnp.float32
BF16 = jnp.bfloat16
MESH = pl.DeviceIdType.MESH

D_MODEL = 1024
D_A = 512
D_B = 512
D_Z = 2 * D_A + D_B
N_HEADS = 4
HEAD_DIM = 128
CHUNK = 128
POOL_WINDOWS = (2, 4, 8, 16)
GROUP_DIM = 128
D_FF = 4096
N_MOD = 6
EPS = 1e-6
HALO = 16
N_CHIPS = 4
N_DEV = 8

ADAM_LR = 0.001
ADAM_B1 = 0.9
ADAM_B2 = 0.999
ADAM_EPS = 1e-08
ADAM_WD = 0.01
ADAM_STEP = 10

VMEM_LIMIT = 56 * 1024 * 1024
LANES = 128
SUBLANES = 8

_VMEM = pl.BlockSpec(memory_space=pltpu.VMEM)
_ANY = pl.BlockSpec(memory_space=pl.ANY)


def _params(n_grid_axes=1):
    return pltpu.CompilerParams(dimension_semantics=("arbitrary",) * n_grid_axes, vmem_limit_bytes=VMEM_LIMIT)


def _rows(ts, width):
    return pl.BlockSpec((ts, width), lambda i: (i, 0))


def _const(shape):
    return pl.BlockSpec(shape, lambda i: (0,) * len(shape))


def _dot(a, b):
    return jnp.dot(a, b, preferred_element_type=F32)


def _dot_nt(a, b):
    return lax.dot_general(a, b, (((1,), (1,)), ((), ())), preferred_element_type=F32)


def _dot_tn(a, b):
    return lax.dot_general(a, b, (((0,), (0,)), ((), ())), preferred_element_type=F32)


def _rowmean(v):
    return jnp.mean(v, axis=-1, keepdims=True)


def _colsum(v):
    return jnp.sum(v, axis=0, keepdims=True)


def _gelu_parts(z):
    k0 = 0.7978845608028654
    k1 = 0.044715
    z2 = z * z
    t = jnp.tanh(z * (k0 + (k0 * k1) * z2))
    u = 0.5 * t + 0.5
    g = z * u
    dg = u + (0.5 * z) * (1.0 - t * t) * (k0 + (3.0 * k0 * k1) * z2)
    return g, dg


def _tril_weights(ws_ref):
    r = lax.broadcasted_iota(jnp.int32, (CHUNK, CHUNK), 0)
    s = lax.broadcasted_iota(jnp.int32, (CHUNK, CHUNK), 1)
    mask = (s <= r).astype(F32)
    return [(ws_ref[h] * mask).astype(BF16) for h in range(N_HEADS)]


def _window_counts(first_row, n_rows):
    pos = (first_row + lax.broadcasted_iota(jnp.int32, (n_rows, 1), 0)).astype(F32)
    return pos, [1.0 / jnp.minimum(pos + 1.0, float(w)) for w in POOL_WINDOWS]


def _causal_window_sums(ext):
    out = []
    e = ext
    shift = 1
    for g in range(len(POOL_WINDOWS)):
        e = e + pltpu.roll(e, shift, 0)
        shift *= 2
        out.append(e[:, g * GROUP_DIM:(g + 1) * GROUP_DIM])
    return out


def _anticausal_window_sums(ext):
    n = ext.shape[0]
    out = []
    e = ext
    shift = 1
    for g in range(len(POOL_WINDOWS)):
        e = e + pltpu.roll(e, n - shift, 0)
        shift *= 2
        out.append(e[:, g * GROUP_DIM:(g + 1) * GROUP_DIM])
    return out


def _fwd_mix(x, mod6, n1pre, n1post, n2pre, win_g, wout_g, w_spatial, bsp_full, gain, bias, w_pool, b_pool, pool_scale, ts):
    s_len = x.shape[0]
    rs = D_MODEL // N_CHIPS

    def body(x_ref, mod_ref, g1pre_ref, g1post_ref, g2pre_ref, win_ref, wout_ref, ws_ref, bsp_ref, gain_ref,
             bias_ref, wp_ref, bp_ref, ps_ref, z_ref, y_ref, mix_ref, x1_ref, h2_ref, mixed_ref, prev_ref, wfull_ref):
        i = pl.program_id(0)
        _zero_on_first_step(prev_ref)
        _join_w_in_on_first_step(win_ref, wfull_ref)
        xv = x_ref[...]
        r = lax.rsqrt(_rowmean(xv * xv) + EPS)
        hb = ((xv * r) * (g1pre_ref[...] * (1.0 + mod_ref[1:2, :])) + mod_ref[0:1, :]).astype(BF16)
        z_ref[...] = _dot(hb, wfull_ref[...])

        wc = _tril_weights(ws_ref)
        u, _, _, _, _ = _mixer_forward_tile(z_ref[:, :2 * D_A], wc, bsp_ref, gain_ref[...], bias_ref[...], mixed_ref)
        y_ref[:, :D_A] = (u * mixed_ref[...]).astype(BF16)
        zb = z_ref[:, 2 * D_A:]
        sums = _causal_window_sums(jnp.concatenate([prev_ref[...], zb], axis=0))
        prev_ref[...] = zb[ts - HALO:, :]
        _, inv_counts = _window_counts(i * ts, ts)
        for g in range(len(POOL_WINDOWS)):
            lanes = slice(g * GROUP_DIM, (g + 1) * GROUP_DIM)
            diff = sums[g][HALO:, :] * inv_counts[g] - zb[:, lanes]
            lin = _dot(diff.astype(BF16), wp_ref[g].astype(BF16)) + bp_ref[:, lanes]
            y_ref[:, D_A + g * GROUP_DIM:D_A + (g + 1) * GROUP_DIM] = (lin * ps_ref[:, lanes]).astype(BF16)

        mix = None
        for j in range(N_CHIPS):
            part = _dot(y_ref[:, j * rs:(j + 1) * rs], wout_ref[j])
            mix = part if mix is None else mix + part
        mix_ref[...] = mix
        r2 = lax.rsqrt(_rowmean(mix * mix) + EPS)
        x1 = xv + (mix * r2) * (mod_ref[2:3, :] * g1post_ref[...])
        x1_ref[...] = x1
        r3 = lax.rsqrt(_rowmean(x1 * x1) + EPS)
        h2_ref[...] = ((x1 * r3) * (g2pre_ref[...] * (1.0 + mod_ref[4:5, :])) + mod_ref[3:4, :]).astype(BF16)

    vec = _const((1, D_MODEL))
    f32_rows = jax.ShapeDtypeStruct((s_len, D_MODEL), F32)
    bf16_rows = jax.ShapeDtypeStruct((s_len, D_MODEL), BF16)
    return pl.pallas_call(
        body, name="fwd_mix", grid=(s_len // ts,),
        in_specs=[_rows(ts, D_MODEL), _const((N_MOD, D_MODEL)), vec, vec, vec, _VMEM, _VMEM,
                  _const((N_HEADS, CHUNK, CHUNK)), _const((CHUNK, D_A)), _const((1, D_A)), _const((1, D_A)),
                  _const((N_HEADS, GROUP_DIM, GROUP_DIM)), _const((1, D_B)), _const((1, D_B))],
        out_specs=[_rows(ts, D_Z), _rows(ts, D_MODEL), _rows(ts, D_MODEL), _rows(ts, D_MODEL), _rows(ts, D_MODEL)],
        out_shape=[jax.ShapeDtypeStruct((s_len, D_Z), F32), bf16_rows, f32_rows, f32_rows, bf16_rows],
        scratch_shapes=[pltpu.VMEM((ts, D_A), F32), pltpu.VMEM((HALO, D_B), F32), pltpu.VMEM((D_MODEL, D_Z), BF16)],
        compiler_params=_params(),
    )(x, mod6, n1pre, n1post, n2pre, win_g, wout_g, w_spatial, bsp_full, gain, bias, w_pool, b_pool, pool_scale)


def _mixer_forward_tile(za, wc, bsp_ref, gain, bias, mixed_ref):
    ga, dga = _gelu_parts(za)
    u = ga[:, :D_A]
    v = ga[:, D_A:]
    mu = _rowmean(v)
    vc = v - mu
    rstd = lax.rsqrt(_rowmean(vc * vc) + EPS)
    vhat = vc * rstd
    vn = (vhat * gain + bias).astype(BF16)
    ts = za.shape[0]
    for k in range(ts // CHUNK):
        for h in range(N_HEADS):
            blk = vn[k * CHUNK:(k + 1) * CHUNK, h * HEAD_DIM:(h + 1) * HEAD_DIM]
            mixed_ref[k * CHUNK:(k + 1) * CHUNK, h * HEAD_DIM:(h + 1) * HEAD_DIM] = (
                _dot(wc[h], blk) + bsp_ref[:, h * HEAD_DIM:(h + 1) * HEAD_DIM])
    return u, vhat, rstd, vn, dga


def _fwd_fc1(h2, fc1_g, ts):
    s_len = h2.shape[0]
    cs = D_FF // N_CHIPS

    def body(h_ref, w_ref, q_ref):
        hb = h_ref[...]
        for j in range(N_CHIPS):
            p = jnp.maximum(_dot(hb, w_ref[j]), 0.0)
            q_ref[:, j * cs:(j + 1) * cs] = (p * p).astype(BF16)

    return pl.pallas_call(
        body, name="fwd_fc1", grid=(s_len // ts,),
        in_specs=[_rows(ts, D_MODEL), _VMEM],
        out_specs=_rows(ts, D_FF),
        out_shape=jax.ShapeDtypeStruct((s_len, D_FF), BF16),
        compiler_params=_params(),
    )(h2, fc1_g)


def _fwd_fc2_loss(q, x1, target, mod6, n2post, fc2_g, ts):
    s_len = q.shape[0]
    rs = D_FF // N_CHIPS

    def body(q_ref, x1_ref, t_ref, mod_ref, g_ref, w_ref, dy_ref, df_ref, loss_ref, s_ref):
        _zero_on_first_step(loss_ref, s_ref)
        gate_gain = mod_ref[5:6, :] * g_ref[...]
        f = _dot(q_ref[:, 0:rs], w_ref[0])
        for j in range(1, N_CHIPS):
            f = f + _dot(q_ref[:, j * rs:(j + 1) * rs], w_ref[j])
        r4 = lax.rsqrt(_rowmean(f * f) + EPS)
        fh = f * r4
        err = (x1_ref[...] + fh * gate_gain) - t_ref[...]
        loss_ref[...] += 0.5 * jnp.sum(_rowmean(err * err), axis=0, keepdims=True)
        dy = err * (1.0 / D_MODEL)
        dy_ref[...] = dy
        s_ref[...] += _colsum(dy * fh)
        gh = dy * gate_gain
        df_ref[...] = (r4 * (gh - fh * _rowmean(gh * fh))).astype(BF16)

    return pl.pallas_call(
        body, name="fwd_fc2_loss", grid=(s_len // ts,),
        in_specs=[_rows(ts, D_FF), _rows(ts, D_MODEL), _rows(ts, D_MODEL), _const((N_MOD, D_MODEL)),
                  _const((1, D_MODEL)), _VMEM],
        out_specs=[_rows(ts, D_MODEL), _rows(ts, D_MODEL), _const((1, 1)), _const((1, D_MODEL))],
        out_shape=[jax.ShapeDtypeStruct((s_len, D_MODEL), F32), jax.ShapeDtypeStruct((s_len, D_MODEL), BF16),
                   jax.ShapeDtypeStruct((1, 1), F32), jax.ShapeDtypeStruct((1, D_MODEL), F32)],
        compiler_params=_params(),
    )(q, x1, target, mod6, n2post, fc2_g)


def _join_w_in_on_first_step(win_ref, full_ref):
    cs = D_Z // N_CHIPS

    @pl.when(pl.program_id(0) == 0)
    def _():
        for j in range(N_CHIPS):
            full_ref[:, j * cs:(j + 1) * cs] = win_ref[j]


def _zero_on_first_step(*refs):
    @pl.when(pl.program_id(0) == 0)
    def _():
        for ref in refs:
            ref[...] = jnp.zeros_like(ref)


def _on_last_step(fn):
    pl.when(pl.program_id(0) == pl.num_programs(0) - 1)(fn)


def _store_shard_on_last_step(acc_ref, hbm_ref, sem, j):
    _on_last_step(lambda: pltpu.make_async_copy(acc_ref.at[j], hbm_ref.at[j], sem.at[j]).start())


def _wait_stores_on_last_step(*stores):
    def wait_all():
        for acc_ref, hbm_ref, sem in stores:
            for j in range(N_CHIPS):
                pltpu.make_async_copy(acc_ref.at[j], hbm_ref.at[j], sem.at[j]).wait()

    _on_last_step(wait_all)


def _bwd_fc2(df, q, fc2_g, ts):
    s_len = df.shape[0]
    cs = D_FF // N_CHIPS

    def body(df_ref, q_ref, w_ref, dp_ref, dw_hbm, dw_ref, dw_sem):
        _zero_on_first_step(dw_ref)
        dfb = df_ref[...]
        df2 = dfb * 2.0
        for j in range(N_CHIPS):
            qb = q_ref[:, j * cs:(j + 1) * cs]
            dw_ref[j] += _dot_tn(qb, dfb).reshape(2, cs // 2, D_MODEL)
            _store_shard_on_last_step(dw_ref, dw_hbm, dw_sem, j)
            dq2 = _dot_nt(df2, w_ref[j])
            dp_ref[:, j * cs:(j + 1) * cs] = (dq2 * jnp.sqrt(qb.astype(F32))).astype(BF16)
        _wait_stores_on_last_step((dw_ref, dw_hbm, dw_sem))

    dw_shape = (N_CHIPS, 2, cs // 2, D_MODEL)
    return pl.pallas_call(
        body, name="bwd_fc2", grid=(s_len // ts,),
        in_specs=[_rows(ts, D_MODEL), _rows(ts, D_FF), _VMEM],
        out_specs=[_rows(ts, D_FF), _ANY],
        out_shape=[jax.ShapeDtypeStruct((s_len, D_FF), BF16), jax.ShapeDtypeStruct(dw_shape, F32)],
        scratch_shapes=[pltpu.VMEM(dw_shape, F32), pltpu.SemaphoreType.DMA((N_CHIPS,))],
        compiler_params=_params(),
    )(df, q, fc2_g)


def _bwd_fc1_out(dp, dy, x1, mix, h2, ycat, mod6, n2pre, n1post, fc1_g, wout_g, dep, ts):
    s_len = dp.shape[0]
    cs = D_FF // N_CHIPS
    rs = D_MODEL // N_CHIPS

    def body(dp_ref, dy_ref, x1_ref, mix_ref, h2_ref, yc_ref, mod_ref, g2_ref, g1_ref, w1_ref, wo_ref, dep_ref,
             dx1_ref, dyc_ref, dshift2_ref, da2_ref, s1_ref, dw1_hbm, dwo_hbm, dw1_ref, dwo_ref, dw1_sem, dwo_sem):
        _zero_on_first_step(dshift2_ref, da2_ref, s1_ref, dw1_ref, dwo_ref)
        h2b = h2_ref[...]
        dh2 = None
        for j in range(N_CHIPS):
            dpb = dp_ref[:, j * cs:(j + 1) * cs]
            dw1_ref[j] += _dot_tn(h2b, dpb).reshape(2, D_MODEL // 2, cs)
            _store_shard_on_last_step(dw1_ref, dw1_hbm, dw1_sem, j)
            part = _dot_nt(dpb, w1_ref[j])
            dh2 = part if dh2 is None else dh2 + part
        x1 = x1_ref[...]
        r3 = lax.rsqrt(_rowmean(x1 * x1) + EPS)
        xh = x1 * r3
        a2 = g2_ref[...] * (1.0 + mod_ref[4:5, :])
        dshift2_ref[...] += _colsum(dh2)
        da2_ref[...] += _colsum(dh2 * xh)
        dxh = dh2 * a2
        dx1 = dy_ref[...] + r3 * (dxh - xh * _rowmean(dxh * xh))
        dx1_ref[...] = dx1

        mix = mix_ref[...]
        r2 = lax.rsqrt(_rowmean(mix * mix) + EPS)
        mh = mix * r2
        s1_ref[...] += _colsum(dx1 * mh)
        gh = dx1 * (mod_ref[2:3, :] * g1_ref[...])
        dmix = (r2 * (gh - mh * _rowmean(gh * mh))).astype(BF16)
        dwo_ref[...] += _dot_tn(yc_ref[...], dmix).reshape(N_CHIPS, 2, rs // 2, D_MODEL)
        for j in range(N_CHIPS):
            _store_shard_on_last_step(dwo_ref, dwo_hbm, dwo_sem, j)
            dyc_ref[:, j * rs:(j + 1) * rs] = _dot_nt(dmix, wo_ref[j])
        _wait_stores_on_last_step((dw1_ref, dw1_hbm, dw1_sem), (dwo_ref, dwo_hbm, dwo_sem))

    vec = jax.ShapeDtypeStruct((1, D_MODEL), F32)
    dw1_shape = (N_CHIPS, 2, D_MODEL // 2, cs)
    dwo_shape = (N_CHIPS, 2, rs // 2, D_MODEL)
    return pl.pallas_call(
        body, name="bwd_fc1_out", grid=(s_len // ts,),
        in_specs=[_rows(ts, D_FF), _rows(ts, D_MODEL), _rows(ts, D_MODEL), _rows(ts, D_MODEL), _rows(ts, D_MODEL),
                  _rows(ts, D_MODEL), _const((N_MOD, D_MODEL)), _const((1, D_MODEL)), _const((1, D_MODEL)), _VMEM,
                  _VMEM, _ANY],
        out_specs=[_rows(ts, D_MODEL), _rows(ts, D_MODEL)] + [_const((1, D_MODEL))] * 3 + [_ANY, _ANY],
        out_shape=[jax.ShapeDtypeStruct((s_len, D_MODEL), F32), jax.ShapeDtypeStruct((s_len, D_MODEL), F32),
                   vec, vec, vec, jax.ShapeDtypeStruct(dw1_shape, F32), jax.ShapeDtypeStruct(dwo_shape, F32)],
        scratch_shapes=[pltpu.VMEM(dw1_shape, F32), pltpu.VMEM(dwo_shape, F32), pltpu.SemaphoreType.DMA((N_CHIPS,)),
                        pltpu.SemaphoreType.DMA((N_CHIPS,))],
        compiler_params=_params(),
    )(dp, dy, x1, mix, h2, ycat, mod6, n2pre, n1post, fc1_g, wout_g, dep)


def _mixer_bwd(z, dyc, w_spatial, bsp_full, gain, bias, w_pool, b_pool, pool_scale, dep, ts):
    s_len = z.shape[0]
    nb = ts // HALO
    last = s_len // HALO - 1
    te = ts + HALO

    def body(z_ref, zprev_ref, znext_ref, dyc_ref, dynext_ref, ws_ref, bsp_ref, gain_ref, bias_ref, wp_ref, bp_ref,
             ps_ref, dep_ref, dz_ref, dws_ref, dbsp_ref, dgain_ref, dbias_ref, dwp_ref, dbp_ref, dps_ref, mixed_ref,
             dvn_ref):
        i = pl.program_id(0)

        @pl.when(i == 0)
        def _():
            for ref in (dws_ref, dbsp_ref, dgain_ref, dbias_ref, dwp_ref, dbp_ref, dps_ref):
                ref[...] = jnp.zeros_like(ref)

        wc = _tril_weights(ws_ref)
        gain = gain_ref[...]
        u, vhat, rstd, vn, dga = _mixer_forward_tile(z_ref[:, :2 * D_A], wc, bsp_ref, gain, bias_ref[...], mixed_ref)
        dya = dyc_ref[:, :D_A]
        du = dya * mixed_ref[...]
        dmixed = dya * u
        dmb = dmixed.astype(BF16)
        dm_sum = dmixed[0:CHUNK, :]
        for k in range(1, ts // CHUNK):
            dm_sum = dm_sum + dmixed[k * CHUNK:(k + 1) * CHUNK, :]
        r_idx = lax.broadcasted_iota(jnp.int32, (CHUNK, CHUNK), 0)
        s_idx = lax.broadcasted_iota(jnp.int32, (CHUNK, CHUNK), 1)
        causal = (s_idx <= r_idx).astype(F32)
        for h in range(N_HEADS):
            lanes = slice(h * HEAD_DIM, (h + 1) * HEAD_DIM)
            dbsp_ref[h] += jnp.sum(dm_sum[:, lanes], axis=1, keepdims=True)
            acc = None
            for k in range(ts // CHUNK):
                rows = slice(k * CHUNK, (k + 1) * CHUNK)
                t = _dot_nt(dmb[rows, lanes], vn[rows, lanes])
                acc = t if acc is None else acc + t
                dvn_ref[rows, lanes] = _dot_tn(wc[h], dmb[rows, lanes])
            dws_ref[h] += acc * causal
        dvn = dvn_ref[...]
        dgain_ref[...] += _colsum(dvn * vhat)
        dbias_ref[...] += _colsum(dvn)
        dvh = dvn * gain
        dv = rstd * (dvh - _rowmean(dvh) - vhat * _rowmean(dvh * vhat))
        dz_ref[:, :D_A] = (du * dga[:, :D_A]).astype(BF16)
        dz_ref[:, D_A:2 * D_A] = (dv * dga[:, D_A:]).astype(BF16)

        zb = z_ref[:, 2 * D_A:]
        prev = jnp.where(i == 0, 0.0, zprev_ref[...])
        zb_ext = jnp.concatenate([zb, znext_ref[...]], axis=0)
        sums = _causal_window_sums(jnp.concatenate([prev, zb_ext], axis=0))
        pos, inv_counts = _window_counts(i * ts, te)
        dyb_ext = jnp.concatenate([dyc_ref[:, D_A:], dynext_ref[...]], axis=0)
        dlin_ext = dyb_ext * ps_ref[...]
        dbp_ref[...] += _colsum(dlin_ext[:ts, :])
        scaled = []
        ddiffs = []
        lins = []
        for g in range(len(POOL_WINDOWS)):
            lanes = slice(g * GROUP_DIM, (g + 1) * GROUP_DIM)
            diff = (sums[g][HALO:, :] * inv_counts[g] - zb_ext[:, lanes]).astype(BF16)
            wpb = wp_ref[g].astype(BF16)
            dlb = dlin_ext[:, lanes].astype(BF16)
            lins.append(_dot(diff[:ts, :], wpb) + bp_ref[:, lanes])
            dwp_ref[g] += _dot_tn(diff[:ts, :], dlb[:ts, :])
            dd = _dot_nt(dlb, wpb)
            ddiffs.append(dd)
            scaled.append(jnp.where(pos < float(s_len), dd * inv_counts[g], 0.0))
        dps_ref[...] += _colsum(dyb_ext[:ts, :] * jnp.concatenate(lins, axis=1))
        back = _anticausal_window_sums(jnp.concatenate(scaled, axis=1))
        for g in range(len(POOL_WINDOWS)):
            dz_ref[:, 2 * D_A + g * GROUP_DIM:2 * D_A + (g + 1) * GROUP_DIM] = (
                back[g][:ts, :] - ddiffs[g][:ts, :]).astype(BF16)

    sq = jax.ShapeDtypeStruct((N_HEADS, CHUNK, CHUNK), F32)
    vec = jax.ShapeDtypeStruct((1, D_A), F32)
    return pl.pallas_call(
        body, name="mixer_bwd", grid=(s_len // ts,),
        in_specs=[_rows(ts, D_Z),
                  pl.BlockSpec((HALO, D_B), lambda i: (jnp.maximum(i * nb - 1, 0), 2)),
                  pl.BlockSpec((HALO, D_B), lambda i: (jnp.minimum((i + 1) * nb, last), 2)),
                  _rows(ts, D_MODEL),
                  pl.BlockSpec((HALO, D_B), lambda i: (jnp.minimum((i + 1) * nb, last), 1)),
                  _const((N_HEADS, CHUNK, CHUNK)), _const((CHUNK, D_A)), _const((1, D_A)), _const((1, D_A)),
                  _const((N_HEADS, GROUP_DIM, GROUP_DIM)), _const((1, D_B)), _const((1, D_B)), _ANY],
        out_specs=[_rows(ts, D_Z), _const((N_HEADS, CHUNK, CHUNK)), _const((N_HEADS, CHUNK, 1)), _const((1, D_A)),
                   _const((1, D_A)), _const((N_HEADS, GROUP_DIM, GROUP_DIM)), _const((1, D_B)), _const((1, D_B))],
        out_shape=[jax.ShapeDtypeStruct((s_len, D_Z), BF16), sq, jax.ShapeDtypeStruct((N_HEADS, CHUNK, 1), F32), vec,
                   vec, sq, vec, vec],
        scratch_shapes=[pltpu.VMEM((ts, D_A), F32), pltpu.VMEM((ts, D_A), F32)],
        compiler_params=_params(),
    )(z, z, z, dyc, dyc, w_spatial, bsp_full, gain, bias, w_pool, b_pool, pool_scale, dep)


def _bwd_in(dz, dx1, x, mod6, n1pre, win_g, dep, ts):
    s_len = x.shape[0]
    cs = D_Z // N_CHIPS

    def body(dz_ref, dx1_ref, x_ref, mod_ref, g_ref, w_ref, dep_ref, gx_ref, dshift_ref, da_ref, dw_hbm, dw_ref,
             wfull_ref, dw_sem):
        _zero_on_first_step(dshift_ref, da_ref, dw_ref)
        _join_w_in_on_first_step(w_ref, wfull_ref)
        xv = x_ref[...]
        r = lax.rsqrt(_rowmean(xv * xv) + EPS)
        xh = xv * r
        h1b = (xh * (g_ref[...] * (1.0 + mod_ref[1:2, :])) + mod_ref[0:1, :]).astype(BF16)
        dzb = dz_ref[...]
        dw = _dot_tn(h1b, dzb)
        for j in range(N_CHIPS):
            dw_ref[j] += dw[:, j * cs:(j + 1) * cs].reshape(2, D_MODEL // 2, cs)
        dh = _dot_nt(dzb, wfull_ref[...])
        a1 = g_ref[...] * (1.0 + mod_ref[1:2, :])
        dshift_ref[...] += _colsum(dh)
        da_ref[...] += _colsum(dh * xh)
        dxh = dh * a1
        gx_ref[...] = dx1_ref[...] + r * (dxh - xh * _rowmean(dxh * xh))
        for j in range(N_CHIPS):
            _store_shard_on_last_step(dw_ref, dw_hbm, dw_sem, j)
        _wait_stores_on_last_step((dw_ref, dw_hbm, dw_sem))

    vec = jax.ShapeDtypeStruct((1, D_MODEL), F32)
    dw_shape = (N_CHIPS, 2, D_MODEL // 2, cs)
    return pl.pallas_call(
        body, name="bwd_in", grid=(s_len // ts,),
        in_specs=[_rows(ts, D_Z), _rows(ts, D_MODEL), _rows(ts, D_MODEL), _const((N_MOD, D_MODEL)),
                  _const((1, D_MODEL)), _VMEM, _ANY],
        out_specs=[_rows(ts, D_MODEL), _const((1, D_MODEL)), _const((1, D_MODEL)), _ANY],
        out_shape=[jax.ShapeDtypeStruct((s_len, D_MODEL), F32), vec, vec, jax.ShapeDtypeStruct(dw_shape, F32)],
        scratch_shapes=[pltpu.VMEM(dw_shape, F32), pltpu.VMEM((D_MODEL, D_Z), BF16),
                        pltpu.SemaphoreType.DMA((N_CHIPS,))],
        compiler_params=_params(),
    )(dz, dx1, x, mod6, n1pre, win_g, dep)


def _adamw_math(w, g, m, v):
    m = ADAM_B1 * m + (1.0 - ADAM_B1) * g
    v = ADAM_B2 * v + (1.0 - ADAM_B2) * (g * g)
    m_hat = m / (1.0 - ADAM_B1 ** ADAM_STEP)
    v_hat = v / (1.0 - ADAM_B2 ** ADAM_STEP)
    delta = -ADAM_LR * (m_hat / (jnp.sqrt(v_hat) + ADAM_EPS) + ADAM_WD * w)
    return delta, m, v


def _adamw(gs, ws, ms, vs, name, steps):
    n = len(ws)

    def body(*refs):
        for i in range(n):
            g_ref, w_ref, m_ref, v_ref = refs[4 * i:4 * i + 4]
            d, nm, nv = _adamw_math(w_ref[...], g_ref[...], m_ref[...], v_ref[...])
            for ref, val in zip(refs[4 * n + 3 * i:4 * n + 3 * i + 3], (d, nm, nv)):
                ref[...] = val

    specs = [_rows(w.shape[0] // steps, w.shape[1]) for w in ws]
    out = pl.pallas_call(
        body, name=name, grid=(steps,),
        in_specs=[s for s in specs for _ in range(4)], out_specs=[s for s in specs for _ in range(3)],
        out_shape=[jax.ShapeDtypeStruct(w.shape, F32) for w in ws for _ in range(3)],
        compiler_params=_params(),
    )(*[a for quad in zip(gs, ws, ms, vs) for a in quad])
    return [out[3 * i:3 * i + 3] for i in range(n)]


def _ada_grad_adamw(sc_t, dmod_shard, w, m, v, tr):
    rows, cols = w.shape

    def body(s_ref, dm_ref, w_ref, m_ref, v_ref, g_ref, d_ref, nm_ref, nv_ref):
        g = s_ref[:, 0:1] * dm_ref[0:1, :]
        for b in range(1, N_DEV):
            g = g + s_ref[:, b:b + 1] * dm_ref[b:b + 1, :]
        g_ref[...] = g
        d, nm, nv = _adamw_math(w_ref[...], g, m_ref[...], v_ref[...])
        d_ref[...] = d
        nm_ref[...] = nm
        nv_ref[...] = nv

    spec = _rows(tr, cols)
    shape = jax.ShapeDtypeStruct((rows, cols), F32)
    return pl.pallas_call(
        body, name="ada_grad_adamw", grid=(rows // tr,),
        in_specs=[_rows(tr, N_DEV), _const((N_DEV, cols)), spec, spec, spec],
        out_specs=[spec] * 4, out_shape=[shape] * 4, compiler_params=_params(),
    )(sc_t, dmod_shard, w, m, v)


def _mod_grads(da1, dshift1, s1, da2, dshift2, s2, mod6, n1pre, n1post, n2pre, n2post):
    def body(da1_ref, ds1_ref, s1_ref, da2_ref, ds2_ref, s2_ref, mod_ref, n1_ref, p1_ref, n2_ref, p2_ref, dmod_ref,
             dn_ref):
        dmod_ref[0:1, :] = ds1_ref[...]
        dmod_ref[1:2, :] = da1_ref[...] * n1_ref[...]
        dmod_ref[2:3, :] = s1_ref[...] * p1_ref[...]
        dmod_ref[3:4, :] = ds2_ref[...]
        dmod_ref[4:5, :] = da2_ref[...] * n2_ref[...]
        dmod_ref[5:6, :] = s2_ref[...] * p2_ref[...]
        dn_ref[0:1, :] = da1_ref[...] * (1.0 + mod_ref[1:2, :])
        dn_ref[1:2, :] = s1_ref[...] * mod_ref[2:3, :]
        dn_ref[2:3, :] = da2_ref[...] * (1.0 + mod_ref[4:5, :])
        dn_ref[3:4, :] = s2_ref[...] * mod_ref[5:6, :]

    return pl.pallas_call(
        body, name="mod_grads",
        out_shape=[jax.ShapeDtypeStruct((N_MOD, D_MODEL), F32), jax.ShapeDtypeStruct((4, D_MODEL), F32)],
    )(da1, dshift1, s1, da2, dshift2, s2, mod6, n1pre, n1post, n2pre, n2post)


def _position():
    x, y, c = lax.axis_index("x"), lax.axis_index("y"), lax.axis_index("c")
    return x, y, c


def _flip(v, bit):
    return 1 - v if bit else v


def _peer(x, y, c, k):
    return (_flip(x, k & 4), _flip(y, k & 2), _flip(c, k & 1))


def _remote(src, dst, send_sem, recv_sem, device):
    return pltpu.make_async_remote_copy(src_ref=src, dst_ref=dst, send_sem=send_sem, recv_sem=recv_sem,
                                        device_id=device, device_id_type=MESH)


def _cast_to_slots(ws, pos, dep, name, steps):
    n = len(ws)

    def body(pos_ref, *refs):
        for w_ref, o_ref in zip(refs[:n], refs[n + 1:]):
            o_ref[0] = w_ref[...].astype(BF16)

    return pl.pallas_call(
        body, name=name,
        grid_spec=pltpu.PrefetchScalarGridSpec(
            num_scalar_prefetch=1, grid=(steps,),
            in_specs=[pl.BlockSpec((w.shape[0] // steps, w.shape[1]), lambda i, pos: (i, 0)) for w in ws] + [_ANY],
            out_specs=[pl.BlockSpec((1, w.shape[0] // steps, w.shape[1]), lambda i, pos: (pos[1], i, 0)) for w in ws]),
        out_shape=[jax.ShapeDtypeStruct((N_CHIPS,) + w.shape, BF16) for w in ws], compiler_params=_params(),
    )(pos, *ws, dep)


def _mod_exchange(c_row, w_ada_shard, b_ada_row):
    cs = w_ada_shard.shape[1]

    def body(c_ref, w_hbm, b_ref, mod_ref, sc_ref, rows_ref, w_ref, w_sem, send1, recv1, send2, recv2):
        x, y, c = _position()
        me = 4 * x + 2 * y + c
        chip = 2 * x + y
        w_load = pltpu.make_async_copy(w_hbm, w_ref, w_sem)
        w_load.start()
        cv = c_ref[...]
        sc_ref[me] = cv * jax.nn.sigmoid(cv)
        gather = [_remote(sc_ref.at[me], sc_ref.at[me], send1.at[k - 1], recv1.at[k - 1], _peer(x, y, c, k))
                  for k in range(1, N_DEV)]
        for cp in gather:
            cp.start()
        for k in range(1, N_DEV):
            px, py, pc = _peer(x, y, c, k)
            src = 4 * px + 2 * py + pc
            _remote(sc_ref.at[src], sc_ref.at[src], send1.at[k - 1], recv1.at[k - 1], (px, py, pc)).wait_recv()
        for cp in gather:
            cp.wait_send()
        sc_all = jnp.concatenate([sc_ref[b] for b in range(N_DEV)], axis=0)
        w_load.wait()
        part = _dot(sc_all.astype(BF16), w_ref[...].astype(BF16))
        part = part + b_ref[:, pl.ds(pl.multiple_of(chip * cs, LANES), cs)]
        for b in range(N_DEV):
            rows_ref[b] = part[b:b + 1, :]
        mod_ref[chip] = rows_ref[me]
        hand = []
        for k in (2, 4, 6):
            px, py, _ = _peer(x, y, c, k)
            hand.append(_remote(rows_ref.at[4 * px + 2 * py + c], mod_ref.at[chip], send2.at[k // 2 - 1],
                                recv2.at[k // 2 - 1], (px, py, c)))
        for cp in hand:
            cp.start()
        for k in (2, 4, 6):
            px, py, _ = _peer(x, y, c, k)
            pchip = 2 * px + py
            _remote(rows_ref.at[me], mod_ref.at[pchip], send2.at[k // 2 - 1], recv2.at[k // 2 - 1],
                    (px, py, c)).wait_recv()
        for cp in hand:
            cp.wait_send()

    return pl.pallas_call(
        body, name="mod_exchange",
        in_specs=[_VMEM, _ANY, _VMEM], out_specs=[_VMEM, _VMEM],
        out_shape=[jax.ShapeDtypeStruct((N_CHIPS, 1, cs), F32), jax.ShapeDtypeStruct((N_DEV, 1, D_MODEL), F32)],
        scratch_shapes=[pltpu.VMEM((N_DEV, 1, cs), F32), pltpu.VMEM(w_ada_shard.shape, F32), pltpu.SemaphoreType.DMA,
                        pltpu.SemaphoreType.DMA((N_DEV - 1,)),
                        pltpu.SemaphoreType.DMA((N_DEV - 1,)), pltpu.SemaphoreType.DMA((N_CHIPS - 1,)),
                        pltpu.SemaphoreType.DMA((N_CHIPS - 1,))],
        compiler_params=pltpu.CompilerParams(vmem_limit_bytes=VMEM_LIMIT),
    )(c_row, w_ada_shard, b_ada_row)


_HBM = pl.BlockSpec(memory_space=pltpu.HBM)
_SEM = pl.BlockSpec(memory_space=pltpu.SEMAPHORE)
_EFFECT = pltpu.SideEffectType.DATAFLOW_SIDE_EFFECTING
_CHIP_HOPS = (2, 4, 6)


def _in_hbm(a):
    return pltpu.with_memory_space_constraint(a, pltpu.HBM)


def _sems3():
    return pltpu.SemaphoreType.DMA((len(_CHIP_HOPS),))


def _ag_start(lands, after, name):
    n = len(lands)

    def body(*refs):
        zones = refs[:n]
        sends, recvs = refs[n + 1:2 * n + 1], refs[2 * n + 1:3 * n + 1]
        x, y, c = _position()
        chip = 2 * x + y
        for i in range(n):
            half = zones[i].shape[1] // 2
            mine = zones[i].at[chip, pl.ds(c * half, half)]
            for s, k in enumerate(_CHIP_HOPS):
                px, py, _ = _peer(x, y, c, k)
                _remote(mine, mine, sends[i].at[s], recvs[i].at[s], (px, py, c)).start()

    out = pl.pallas_call(
        body, name=name,
        in_specs=[_HBM] * n + [_ANY],
        out_specs=[_SEM] * (2 * n) + [_HBM] * n,
        out_shape=[_sems3()] * (2 * n) + [pltpu.HBM(z.shape, BF16) for z in lands],
        input_output_aliases={i: 2 * n + i for i in range(n)},
        compiler_params=pltpu.CompilerParams(has_side_effects=_EFFECT),
    )(*[_in_hbm(z) for z in lands], after)
    return [(out[2 * n + i], out[i], out[n + i]) for i in range(n)]


def _ag_pass(group, after, name):
    n = len(group)

    def body(*refs):
        zones = refs[:n]
        sends, recvs = refs[n:2 * n], refs[2 * n:3 * n]
        fsends, frecvs = refs[4 * n + 1:5 * n + 1], refs[5 * n + 1:6 * n + 1]
        x, y, c = _position()
        chip = 2 * x + y
        for i in range(n):
            half = zones[i].shape[1] // 2
            rows = pl.ds(c * half, half)
            for s, k in enumerate(_CHIP_HOPS):
                px, py, _ = _peer(x, y, c, k)
                landed = zones[i].at[2 * px + py, rows]
                _remote(landed, landed, sends[i].at[s], recvs[i].at[s], (px, py, c)).wait_recv()
                _remote(landed, landed, fsends[i].at[s], frecvs[i].at[s], (x, y, 1 - c)).start()
        for i in range(n):
            half = zones[i].shape[1] // 2
            mine = zones[i].at[chip, pl.ds(c * half, half)]
            for s, k in enumerate(_CHIP_HOPS):
                px, py, _ = _peer(x, y, c, k)
                _remote(mine, mine, sends[i].at[s], recvs[i].at[s], (px, py, c)).wait_send()

    out = pl.pallas_call(
        body, name=name,
        in_specs=[_HBM] * n + [_SEM] * (2 * n) + [_ANY],
        out_specs=[_HBM] * n + [_SEM] * (2 * n),
        out_shape=[pltpu.HBM(g[0].shape, BF16) for g in group] + [_sems3()] * (2 * n),
        input_output_aliases={i: i for i in range(n)},
        compiler_params=pltpu.CompilerParams(has_side_effects=_EFFECT),
    )(*[g[0] for g in group], *[g[1] for g in group], *[g[2] for g in group], after)
    return [(out[i], out[n + i], out[2 * n + i]) for i in range(n)]


def _ag_done(group, name):
    n = len(group)

    def body(*refs):
        lands = refs[:n]
        fsends, frecvs = refs[n:2 * n], refs[2 * n:3 * n]
        x, y, c = _position()
        for i in range(n):
            half = lands[i].shape[1] // 2
            for s, k in enumerate(_CHIP_HOPS):
                px, py, _ = _peer(x, y, c, k)
                sent = lands[i].at[2 * px + py, pl.ds(c * half, half)]
                got = lands[i].at[2 * px + py, pl.ds((1 - c) * half, half)]
                cp = _remote(sent, got, fsends[i].at[s], frecvs[i].at[s], (x, y, 1 - c))
                cp.wait_recv()
                cp.wait_send()

    out = pl.pallas_call(
        body, name=name,
        in_specs=[_HBM] * n + [_SEM] * (2 * n),
        out_specs=[_HBM] * n,
        out_shape=[pltpu.HBM(g[0].shape, BF16) for g in group],
        input_output_aliases={i: i for i in range(n)},
        compiler_params=pltpu.CompilerParams(has_side_effects=_EFFECT),
    )(*[g[0] for g in group], *[g[1] for g in group], *[g[2] for g in group])
    return list(out)


def _small_spread_start(slots, after):
    def body(z_ref, after_ref, sends, recvs, z_out):
        x, y, c = _position()
        mine = z_ref.at[2 * x + y]
        for s, k in enumerate(_CHIP_HOPS):
            px, py, _ = _peer(x, y, c, k)
            _remote(mine, mine, sends.at[s], recvs.at[s], (px, py, c)).start()

    sends, recvs, out = pl.pallas_call(
        body, name="small_spread_start",
        in_specs=[_HBM, _ANY], out_specs=[_SEM, _SEM, _HBM],
        out_shape=[_sems3(), _sems3(), pltpu.HBM(slots.shape, F32)],
        input_output_aliases={0: 2},
        compiler_params=pltpu.CompilerParams(has_side_effects=_EFFECT),
    )(_in_hbm(slots), after)
    return out, sends, recvs


def _small_spread_wait(slots, sends, recvs, afters):
    def body(z_ref, sends, recvs, *rest):
        x, y, c = _position()
        mine = z_ref.at[2 * x + y]
        for s, k in enumerate(_CHIP_HOPS):
            px, py, _ = _peer(x, y, c, k)
            cp = _remote(mine, z_ref.at[2 * px + py], sends.at[s], recvs.at[s], (px, py, c))
            cp.wait_recv()
            cp.wait_send()

    return pl.pallas_call(
        body, name="small_spread_wait",
        in_specs=[_HBM, _SEM, _SEM] + [_ANY] * len(afters), out_specs=_HBM, out_shape=pltpu.HBM(slots.shape, F32),
        input_output_aliases={0: 0},
        compiler_params=pltpu.CompilerParams(has_side_effects=_EFFECT),
    )(slots, sends, recvs, *afters)


def _sibling_sum(pgs, name, small=None):
    n = len(pgs)
    k = 0 if small is None else 1
    units = [(i, j) for i in range(n) for j in range(N_CHIPS)]

    def body(*refs):
        refs = list(refs)
        take = lambda count: [refs.pop(0) for _ in range(count)]
        ins, small_in = take(n), take(k)
        qbs, owns, slots_out = take(n), take(n), take(k)
        mine, other, stage, got = take(n), take(n), take(n), take(n)
        load_a, load_b, send, recv = take(4)
        x, y, c = _position()
        chip = 2 * x + y
        if k:
            sib_ref, pair_send, pair_recv = take(3)
            pair = _remote(small_in[0], sib_ref, pair_send, pair_recv, (x, y, 1 - c))
            pair.start()
        loads_a = [pltpu.make_async_copy(ins[i].at[j, 1 - c], other[i].at[j], load_a.at[u])
                   for u, (i, j) in enumerate(units)]
        loads_b = [pltpu.make_async_copy(ins[i].at[j, c], mine[i].at[j], load_b.at[u])
                   for u, (i, j) in enumerate(units)]
        for cp in loads_a + loads_b:
            cp.start()
        sent = []
        for u, (i, j) in enumerate(units):
            loads_a[u].wait()
            stage[i][j] = other[i][j].astype(BF16)
            cp = _remote(stage[i].at[j], got[i].at[j], send.at[u], recv.at[u], (x, y, 1 - c))
            cp.start()
            sent.append(cp)
        for u, (i, j) in enumerate(units):
            loads_b[u].wait()
            sent[u].wait_recv()
            q = mine[i][j] + got[i][j].astype(F32)
            mine[i][j] = q
            qbs[i][j] = q.astype(BF16)
        for i in range(n):
            owns[i][...] = mine[i][chip]
        if k:
            pair.wait()
            slots_out[0][chip] = small_in[0][...] + sib_ref[...]
        for cp in sent:
            cp.wait_send()

    wire = [(N_CHIPS,) + p.shape[2:] for p in pgs]
    extra_out, extra_scratch = [], []
    if k:
        extra_out = [jax.ShapeDtypeStruct((N_CHIPS,) + small.shape, F32)]
        extra_scratch = [pltpu.VMEM(small.shape, F32), pltpu.SemaphoreType.DMA, pltpu.SemaphoreType.DMA]
    out = pl.pallas_call(
        body, name=name, in_specs=[_ANY] * n + [_VMEM] * k, out_specs=[_VMEM] * (2 * n + k),
        out_shape=[jax.ShapeDtypeStruct(w, BF16) for w in wire] + [jax.ShapeDtypeStruct(w[1:], F32) for w in wire]
        + extra_out,
        scratch_shapes=[pltpu.VMEM(w, F32) for w in wire] * 2 + [pltpu.VMEM(w, BF16) for w in wire] * 2
        + [pltpu.SemaphoreType.DMA((len(units),))] * 4 + extra_scratch,
        compiler_params=pltpu.CompilerParams(vmem_limit_bytes=VMEM_LIMIT),
    )(*pgs, *([small] if k else []))
    return list(out[:n]), list(out[n:2 * n]), list(out[2 * n:])


def _rs_start(qbs, name, after=None):
    n = len(qbs)
    k = 0 if after is None else 1

    def body(*refs):
        outs, inboxes = refs[:n], refs[n:2 * n]
        sends, recvs = refs[2 * n + k:3 * n + k], refs[3 * n + k:4 * n + k]
        x, y, c = _position()
        chip = 2 * x + y
        for i in range(n):
            for s, hop in enumerate(_CHIP_HOPS):
                px, py, _ = _peer(x, y, c, hop)
                _remote(outs[i].at[2 * px + py], inboxes[i].at[chip], sends[i].at[s], recvs[i].at[s], (px, py, c)).start()

    inboxes = [_in_hbm(lax.empty(q.shape, BF16)) for q in qbs]
    out = pl.pallas_call(
        body, name=name,
        in_specs=[_HBM] * (2 * n) + [_ANY] * k,
        out_specs=[_SEM] * (2 * n) + [_HBM] * (2 * n),
        out_shape=[_sems3()] * (2 * n) + [pltpu.HBM(q.shape, BF16) for q in qbs] * 2,
        input_output_aliases={i: 2 * n + i for i in range(2 * n)},
        compiler_params=pltpu.CompilerParams(has_side_effects=_EFFECT),
    )(*[_in_hbm(q) for q in qbs], *inboxes, *([after] if k else []))
    return [(out[2 * n + i], out[3 * n + i], out[i], out[n + i]) for i in range(n)]


def _rs_wait(group, after, name):
    n = len(group)

    def body(*refs):
        outs, inboxes = refs[:n], refs[n:2 * n]
        sends, recvs = refs[2 * n:3 * n], refs[3 * n:4 * n]
        x, y, c = _position()
        for i in range(n):
            for s, k in enumerate(_CHIP_HOPS):
                px, py, _ = _peer(x, y, c, k)
                slot = 2 * px + py
                cp = _remote(outs[i].at[slot], inboxes[i].at[slot], sends[i].at[s], recvs[i].at[s], (px, py, c))
                cp.wait_recv()
                cp.wait_send()

    out = pl.pallas_call(
        body, name=name,
        in_specs=[_HBM] * (2 * n) + [_SEM] * (2 * n) + [_ANY],
        out_specs=[_HBM] * n,
        out_shape=[pltpu.HBM(g[1].shape, BF16) for g in group],
        input_output_aliases={n + i: i for i in range(n)},
        compiler_params=pltpu.CompilerParams(has_side_effects=_EFFECT),
    )(*[g[0] for g in group], *[g[1] for g in group], *[g[2] for g in group], *[g[3] for g in group], after)
    return list(out)


def _final_share(inboxes, owns, name):
    n = len(inboxes)
    units = [(i, s) for i in range(n) for s in range(len(_CHIP_HOPS))]

    def body(*refs):
        ins, mine, outs, landed = refs[:n], refs[n:2 * n], refs[2 * n:3 * n], refs[3 * n:4 * n]
        load, send, recv = refs[4 * n:]
        x, y, c = _position()
        loads = []
        for u, (i, s) in enumerate(units):
            px, py, _ = _peer(x, y, c, _CHIP_HOPS[s])
            loads.append(pltpu.make_async_copy(ins[i].at[2 * px + py], landed[i].at[s], load.at[u]))
        for cp in loads:
            cp.start()
        copies = []
        for i in range(n):
            for s in range(len(_CHIP_HOPS)):
                loads[len(_CHIP_HOPS) * i + s].wait()
            total = (landed[i][0].astype(F32) + landed[i][1].astype(F32)) + landed[i][2].astype(F32)
            outs[i][c] = total + mine[i][...]
            cp = _remote(outs[i].at[c], outs[i].at[c], send.at[i], recv.at[i], (x, y, 1 - c))
            cp.start()
            copies.append(cp)
        for i in range(n):
            theirs = outs[i].at[1 - c]
            _remote(theirs, theirs, send.at[i], recv.at[i], (x, y, 1 - c)).wait_recv()
        for cp in copies:
            cp.wait_send()

    return pl.pallas_call(
        body, name=name, in_specs=[_ANY] * n + [_VMEM] * n, out_specs=[_VMEM] * n,
        out_shape=[jax.ShapeDtypeStruct((2,) + o.shape, F32) for o in owns],
        scratch_shapes=[pltpu.VMEM((len(_CHIP_HOPS),) + o.shape, BF16) for o in owns]
        + [pltpu.SemaphoreType.DMA((len(units),)), pltpu.SemaphoreType.DMA((n,)), pltpu.SemaphoreType.DMA((n,))],
        compiler_params=pltpu.CompilerParams(vmem_limit_bytes=VMEM_LIMIT),
    )(*inboxes, *owns)


_SMALL = (("b_ada", N_MOD * D_MODEL), ("norm1_pre", D_MODEL), ("norm1_post", D_MODEL), ("norm2_pre", D_MODEL),
          ("norm2_post", D_MODEL), ("w_spatial", N_HEADS * CHUNK * CHUNK), ("b_spatial", N_HEADS * CHUNK),
          ("ln_v_gain", D_A), ("ln_v_bias", D_A), ("w_pool", N_HEADS * GROUP_DIM * GROUP_DIM),
          ("b_pool", D_B), ("pool_scale", D_B))
_MOD_ROWS = N_MOD * D_MODEL // LANES


def _packed_rows(size):
    return -(-(size // LANES) // SUBLANES) * SUBLANES


def _pack(parts):
    out = []
    for name, size in _SMALL:
        a = parts[name].reshape(size // LANES, LANES)
        pad = _packed_rows(size) - a.shape[0]
        out.append(jnp.pad(a, ((0, pad), (0, 0))) if pad else a)
    return out


def _small_adamw(slots, ws, ms, vs):
    n = len(_SMALL)
    head = N_DEV * _MOD_ROWS

    def body(*refs):
        s_ref, w, m, v = refs[0], refs[1:1 + n], refs[1 + n:1 + 2 * n], refs[1 + 2 * n:1 + 3 * n]
        outs = refs[1 + 3 * n:1 + 7 * n]
        dmod_ref, loss_ref, t_ref = refs[1 + 7 * n:]
        t_ref[...] = ((s_ref[0] + s_ref[1]) + s_ref[2]) + s_ref[3]
        dmod_ref[...] = t_ref[0:head, :]
        loss_ref[...] = t_ref[t_ref.shape[0] - 8:, :]
        row = head
        for i, (_, size) in enumerate(_SMALL):
            if i == 0:
                g = t_ref[0:_MOD_ROWS, :]
                for b in range(1, N_DEV):
                    g = g + t_ref[b * _MOD_ROWS:(b + 1) * _MOD_ROWS, :]
            else:
                g = t_ref[row:row + size // LANES, :]
                row += _packed_rows(size)
            d, nm, nv = _adamw_math(w[i][...], g, m[i][...], v[i][...])
            for ref, val in zip(outs[4 * i:4 * i + 4], (g, d, nm, nv)):
                ref[...] = val

    each = [jax.ShapeDtypeStruct((size // LANES, LANES), F32) for _, size in _SMALL for _ in range(4)]
    out = pl.pallas_call(
        body, name="small_adamw",
        out_shape=each + [jax.ShapeDtypeStruct((head, LANES), F32), jax.ShapeDtypeStruct((SUBLANES, LANES), F32)],
        scratch_shapes=[pltpu.VMEM(slots.shape[1:], F32)],
        compiler_params=pltpu.CompilerParams(vmem_limit_bytes=VMEM_LIMIT),
    )(slots, *ws, *ms, *vs)
    return [out[4 * i:4 * i + 4] for i in range(n)], out[4 * n], out[4 * n + 1]


def kernel(x, c, w_ada, b_ada, norm1_pre, norm1_post, w_in, w_spatial, b_spatial, ln_v_gain, ln_v_bias, w_pool, b_pool, pool_scale, w_out, norm2_pre, norm2_post, w_fc1, w_fc2, loss_target, m_w_ada, m_b_ada, m_norm1_pre, m_norm1_post, m_w_in, m_w_spatial, m_b_spatial, m_ln_v_gain, m_ln_v_bias, m_w_pool, m_b_pool, m_pool_scale, m_w_out, m_norm2_pre, m_norm2_post, m_w_fc1, m_w_fc2, v_w_ada, v_b_ada, v_norm1_pre, v_norm1_post, v_w_in, v_w_spatial, v_b_spatial, v_ln_v_gain, v_ln_v_bias, v_w_pool, v_b_pool, v_pool_scale, v_w_out, v_norm2_pre, v_norm2_post, v_w_fc1, v_w_fc2):
    weights = dict(w_ada=w_ada, b_ada=b_ada, norm1_pre=norm1_pre, norm1_post=norm1_post, w_in=w_in,
                   w_spatial=w_spatial, b_spatial=b_spatial, ln_v_gain=ln_v_gain, ln_v_bias=ln_v_bias, w_pool=w_pool,
                   b_pool=b_pool, pool_scale=pool_scale, w_out=w_out, norm2_pre=norm2_pre, norm2_post=norm2_post,
                   w_fc1=w_fc1, w_fc2=w_fc2)
    m_old = dict(w_ada=m_w_ada, b_ada=m_b_ada, norm1_pre=m_norm1_pre, norm1_post=m_norm1_post, w_in=m_w_in,
                 w_spatial=m_w_spatial, b_spatial=m_b_spatial, ln_v_gain=m_ln_v_gain, ln_v_bias=m_ln_v_bias,
                 w_pool=m_w_pool, b_pool=m_b_pool, pool_scale=m_pool_scale, w_out=m_w_out, norm2_pre=m_norm2_pre,
                 norm2_post=m_norm2_post, w_fc1=m_w_fc1, w_fc2=m_w_fc2)
    v_old = dict(w_ada=v_w_ada, b_ada=v_b_ada, norm1_pre=v_norm1_pre, norm1_post=v_norm1_post, w_in=v_w_in,
                 w_spatial=v_w_spatial, b_spatial=v_b_spatial, ln_v_gain=v_ln_v_gain, ln_v_bias=v_ln_v_bias,
                 w_pool=v_w_pool, b_pool=v_b_pool, pool_scale=v_pool_scale, w_out=v_w_out, norm2_pre=v_norm2_pre,
                 norm2_post=v_norm2_post, w_fc1=v_w_fc1, w_fc2=v_w_fc2)
    order = ("w_ada", "b_ada", "norm1_pre", "norm1_post", "w_in", "w_spatial", "b_spatial", "ln_v_gain", "ln_v_bias",
             "w_pool", "b_pool", "pool_scale", "w_out", "norm2_pre", "norm2_post", "w_fc1", "w_fc2")
    mx, my, mc = _position()
    me = 4 * mx + 2 * my + mc
    chip = 2 * mx + my
    row = lambda a: a.reshape(1, -1)

    pos = jnp.stack([mc, chip]).astype(jnp.int32)
    xs, target = x[0], loss_target[0]
    n1pre, n1post, n2pre, n2post = row(norm1_pre), row(norm1_post), row(norm2_pre), row(norm2_post)
    mixer = (w_spatial, jnp.repeat(b_spatial.T, HEAD_DIM, axis=1), row(ln_v_gain), row(ln_v_bias), w_pool,
             row(b_pool), row(pool_scale))
    ts_big, ts_mid = 512, 256

    mod4, sc_all = _mod_exchange(c, w_ada, row(b_ada))
    mod6 = mod4.reshape(N_MOD, D_MODEL)
    ag = _ag_start(_cast_to_slots([w_in, w_out], pos, mod4, "cast_mix", 1), mod4, "ag_start_mix")
    ag += _ag_start(_cast_to_slots([w_fc1, w_fc2], pos, ag[0][0], "cast_mlp", 4), ag[0][0], "ag_start_mlp")

    win_g, wout_g = _ag_done(_ag_pass([ag[0], ag[1]], ag[2][0], "ag_pass_mix"), "ag_done_mix")
    z, ycat, mix, x1, h2 = _fwd_mix(xs, mod6, n1pre, n1post, n2pre, win_g, wout_g, *mixer, ts_big)
    (fc1_g,) = _ag_done(_ag_pass([ag[2]], h2, "ag_pass_fc1"), "ag_done_fc1")
    q = _fwd_fc1(h2, fc1_g, ts_big)
    (fc2_g,) = _ag_done(_ag_pass([ag[3]], q, "ag_pass_fc2"), "ag_done_fc2")
    dy, df, loss, s2 = _fwd_fc2_loss(q, x1, target, mod6, n2post, fc2_g, ts_big)

    def reduce_start(partials, tag, small=None):
        wire, owns, slots = _sibling_sum(partials, "sibling_sum_" + tag, small)
        return _rs_start(wire, "rs_start_" + tag), owns, slots

    def reduce_finish(state, owns, names, tag, dep):
        inboxes = _rs_wait(state, dep, "rs_wait_" + tag)
        shards = _final_share(inboxes, owns, "final_share_" + tag)
        for n, g in zip(names, shards):
            grads[n] = g.reshape(weights[n].shape)
        updates = _adamw([grads[n] for n in names], [weights[n] for n in names], [m_old[n] for n in names],
                         [v_old[n] for n in names], "adamw_" + tag, 4)
        for n, (d, nm, nv) in zip(names, updates):
            deltas[n], new_m[n], new_v[n] = d, nm, nv

    grads, deltas, new_m, new_v = {}, {}, {}, {}
    dp, g_fc2 = _bwd_fc2(df, q, fc2_g, ts_big)
    state_fc2, owns_fc2, _ = reduce_start([g_fc2], "fc2")
    dx1, dyc, dshift2, da2, s1, g_fc1, g_out = _bwd_fc1_out(
        dp, dy, x1, mix, h2, ycat, mod6, n2pre, n1post, fc1_g, wout_g, state_fc2[0][0], ts_mid)
    state_mid, owns_mid, _ = reduce_start([g_fc1, g_out], "mid")
    dz, dws, dbsp, dgain, dbias, dwp, dbp, dps = _mixer_bwd(z, dyc, *mixer, state_mid[0][0], ts_big)
    grad_x, dshift1, da1, g_in = _bwd_in(dz, dx1, xs, mod6, n1pre, win_g, state_mid[0][0], ts_big)
    dmod6, dnorms = _mod_grads(da1, dshift1, s1, da2, dshift2, s2, mod6, n1pre, n1post, n2pre, n2post)

    parts = dict(b_ada=dmod6, norm1_pre=dnorms[0], norm1_post=dnorms[1], norm2_pre=dnorms[2], norm2_post=dnorms[3],
                 w_spatial=dws, b_spatial=dbsp, ln_v_gain=dgain, ln_v_bias=dbias, w_pool=dwp, b_pool=dbp,
                 pool_scale=dps)
    pieces = _pack(parts)
    slots = lax.dynamic_update_slice(jnp.zeros((N_DEV * _MOD_ROWS, LANES), F32), pieces[0], (me * _MOD_ROWS, 0))
    loss_tile = jnp.pad(loss, ((0, SUBLANES - 1), (0, LANES - 1)))
    wire_in, owns_in, pair_sum = _sibling_sum([g_in], "sibling_sum_in",
                                              jnp.concatenate([slots] + pieces[1:] + [loss_tile], axis=0))
    spread = _small_spread_start(pair_sum[0], wire_in[0])
    state_in = _rs_start(wire_in, "rs_start_in", spread[0])
    reduce_finish(state_fc2 + state_mid, owns_fc2 + owns_mid, ("w_fc2", "w_fc1", "w_out"), "mlp", state_in[0][0])
    flat = lambda d: [d[n].reshape(size // LANES, LANES) for n, size in _SMALL]
    small_out, dmod_all, loss_tile = _small_adamw(
        _small_spread_wait(*spread, [deltas[n] for n in ("w_fc2", "w_fc1", "w_out")]), flat(weights), flat(m_old),
        flat(v_old))
    loss = loss_tile[0, 0]
    for (n, _), (g, d, nm, nv) in zip(_SMALL, small_out):
        shape = weights[n].shape
        grads[n], deltas[n], new_m[n], new_v[n] = g.reshape(shape), d.reshape(shape), nm.reshape(shape), nv.reshape(shape)

    dmod_all = dmod_all.reshape(N_DEV, N_MOD * D_MODEL)
    cs = w_ada.shape[1]
    dmod_shard = lax.dynamic_slice(dmod_all, (0, chip * cs), (N_DEV, cs))
    sc_t = sc_all.reshape(N_DEV, D_MODEL).T
    grads["w_ada"], deltas["w_ada"], new_m["w_ada"], new_v["w_ada"] = _ada_grad_adamw(
        sc_t, dmod_shard, w_ada, m_w_ada, v_w_ada, 256)

    reduce_finish(state_in, owns_in, ("w_in",), "in", deltas["w_ada"])

    return (loss, grad_x[None], *[grads[n] for n in order], *[deltas[n] for n in order],
            *[new_m[n] for n in order], *[new_v[n] for n in order])
```

```python
import jax
import jax.numpy as jnp
from jax import lax
from jax.experimental import pallas as pl
from jax.experimental.pallas import tpu as pltpu

F32 = jnp.float32
BF16 = jnp.bfloat16
MESH = pl.DeviceIdType.MESH

D_MODEL = 1024
D_A = 512
D_B = 512
D_Z = 2 * D_A + D_B
N_HEADS = 4
HEAD_DIM = 128
CHUNK = 128
POOL_WINDOWS = (2, 4, 8, 16)
GROUP_DIM = 128
D_FF = 4096
N_MOD = 6
EPS = 1e-6
HALO = 16
N_CHIPS = 4
N_DEV = 8

ADAM_LR = 0.001
ADAM_B1 = 0.9
ADAM_B2 = 0.999
ADAM_EPS = 1e-08
ADAM_WD = 0.01
ADAM_STEP = 10

VMEM_LIMIT = 56 * 1024 * 1024
LANES = 128
SUBLANES = 8

_VMEM = pl.BlockSpec(memory_space=pltpu.VMEM)
_ANY = pl.BlockSpec(memory_space=pl.ANY)


def _params(n_grid_axes=1):
    return pltpu.CompilerParams(dimension_semantics=("arbitrary",) * n_grid_axes, vmem_limit_bytes=VMEM_LIMIT)


def _rows(ts, width):
    return pl.BlockSpec((ts, width), lambda i: (i, 0))


def _const(shape):
    return pl.BlockSpec(shape, lambda i: (0,) * len(shape))


def _dot(a, b):
    return jnp.dot(a, b, preferred_element_type=F32)


def _dot_nt(a, b):
    return lax.dot_general(a, b, (((1,), (1,)), ((), ())), preferred_element_type=F32)


def _dot_tn(a, b):
    return lax.dot_general(a, b, (((0,), (0,)), ((), ())), preferred_element_type=F32)


def _rowmean(v):
    return jnp.mean(v, axis=-1, keepdims=True)


def _colsum(v):
    return jnp.sum(v, axis=0, keepdims=True)


def _gelu_parts(z):
    k0 = 0.7978845608028654
    k1 = 0.044715
    z2 = z * z
    t = jnp.tanh(z * (k0 + (k0 * k1) * z2))
    u = 0.5 * t + 0.5
    g = z * u
    dg = u + (0.5 * z) * (1.0 - t * t) * (k0 + (3.0 * k0 * k1) * z2)
    return g, dg


def _tril_weights(ws_ref):
    r = lax.broadcasted_iota(jnp.int32, (CHUNK, CHUNK), 0)
    s = lax.broadcasted_iota(jnp.int32, (CHUNK, CHUNK), 1)
    mask = (s <= r).astype(F32)
    return [(ws_ref[h] * mask).astype(BF16) for h in range(N_HEADS)]


def _window_counts(first_row, n_rows):
    pos = (first_row + lax.broadcasted_iota(jnp.int32, (n_rows, 1), 0)).astype(F32)
    return pos, [1.0 / jnp.minimum(pos + 1.0, float(w)) for w in POOL_WINDOWS]


def _causal_window_sums(ext):
    out = []
    e = ext
    shift = 1
    for g in range(len(POOL_WINDOWS)):
        e = e + pltpu.roll(e, shift, 0)
        shift *= 2
        out.append(e[:, g * GROUP_DIM:(g + 1) * GROUP_DIM])
    return out


def _anticausal_window_sums(ext):
    n = ext.shape[0]
    out = []
    e = ext
    shift = 1
    for g in range(len(POOL_WINDOWS)):
        e = e + pltpu.roll(e, n - shift, 0)
        shift *= 2
        out.append(e[:, g * GROUP_DIM:(g + 1) * GROUP_DIM])
    return out


def _prenorm(x, mod6, n1pre, dep, ts):
    s_len = x.shape[0]

    def body(x_ref, mod_ref, g_ref, dep_ref, h_ref, r_ref):
        xv = x_ref[...]
        r = lax.rsqrt(_rowmean(xv * xv) + EPS)
        r_ref[...] = r
        h_ref[...] = ((xv * r) * (g_ref[...] * (1.0 + mod_ref[1:2, :])) + mod_ref[0:1, :]).astype(BF16)

    return pl.pallas_call(
        body, name="prenorm", grid=(s_len // ts,),
        in_specs=[_rows(ts, D_MODEL), _const((N_MOD, D_MODEL)), _const((1, D_MODEL)), _ANY],
        out_specs=[_rows(ts, D_MODEL), _rows(ts, 1)],
        out_shape=[jax.ShapeDtypeStruct((s_len, D_MODEL), BF16), jax.ShapeDtypeStruct((s_len, 1), F32)],
        compiler_params=_params(),
    )(x, mod6, n1pre, dep)


def _fwd_mix(x, h1, mod6, n1post, n2pre, win_g, wout_g, w_spatial, bsp_full, gain, bias, w_pool, b_pool, pool_scale, ts):
    s_len = x.shape[0]
    rs = D_MODEL // N_CHIPS

    def body(x_ref, h1_ref, mod_ref, g1post_ref, g2pre_ref, win_ref, wout_ref, ws_ref, bsp_ref, gain_ref,
             bias_ref, wp_ref, bp_ref, ps_ref, z_ref, y_ref, mix_ref, x1_ref, h2_ref, mixed_ref, prev_ref, wfull_ref):
        i = pl.program_id(0)
        _zero_on_first_step(prev_ref)
        _join_w_in_on_first_step(win_ref, wfull_ref)
        z_ref[...] = _dot(h1_ref[...], wfull_ref[...])

        wc = _tril_weights(ws_ref)
        u, _, _, _, _ = _mixer_forward_tile(z_ref[:, :2 * D_A], wc, bsp_ref, gain_ref[...], bias_ref[...], mixed_ref)
        y_ref[:, :D_A] = (u * mixed_ref[...]).astype(BF16)
        zb = z_ref[:, 2 * D_A:]
        sums = _causal_window_sums(jnp.concatenate([prev_ref[...], zb], axis=0))
        prev_ref[...] = zb[ts - HALO:, :]
        _, inv_counts = _window_counts(i * ts, ts)
        for g in range(len(POOL_WINDOWS)):
            lanes = slice(g * GROUP_DIM, (g + 1) * GROUP_DIM)
            diff = sums[g][HALO:, :] * inv_counts[g] - zb[:, lanes]
            lin = _dot(diff.astype(BF16), wp_ref[g].astype(BF16)) + bp_ref[:, lanes]
            y_ref[:, D_A + g * GROUP_DIM:D_A + (g + 1) * GROUP_DIM] = (lin * ps_ref[:, lanes]).astype(BF16)

        mix = None
        for j in range(N_CHIPS):
            part = _dot(y_ref[:, j * rs:(j + 1) * rs], wout_ref[j])
            mix = part if mix is None else mix + part
        mix_ref[...] = mix
        r2 = lax.rsqrt(_rowmean(mix * mix) + EPS)
        x1 = x_ref[...] + (mix * r2) * (mod_ref[2:3, :] * g1post_ref[...])
        x1_ref[...] = x1
        r3 = lax.rsqrt(_rowmean(x1 * x1) + EPS)
        h2_ref[...] = ((x1 * r3) * (g2pre_ref[...] * (1.0 + mod_ref[4:5, :])) + mod_ref[3:4, :]).astype(BF16)

    vec = _const((1, D_MODEL))
    f32_rows = jax.ShapeDtypeStruct((s_len, D_MODEL), F32)
    bf16_rows = jax.ShapeDtypeStruct((s_len, D_MODEL), BF16)
    return pl.pallas_call(
        body, name="fwd_mix", grid=(s_len // ts,),
        in_specs=[_rows(ts, D_MODEL), _rows(ts, D_MODEL), _const((N_MOD, D_MODEL)), vec, vec, _VMEM, _VMEM,
                  _const((N_HEADS, CHUNK, CHUNK)), _const((CHUNK, D_A)), _const((1, D_A)), _const((1, D_A)),
                  _const((N_HEADS, GROUP_DIM, GROUP_DIM)), _const((1, D_B)), _const((1, D_B))],
        out_specs=[_rows(ts, D_Z), _rows(ts, D_MODEL), _rows(ts, D_MODEL), _rows(ts, D_MODEL), _rows(ts, D_MODEL)],
        out_shape=[jax.ShapeDtypeStruct((s_len, D_Z), F32), bf16_rows, f32_rows, f32_rows, bf16_rows],
        scratch_shapes=[pltpu.VMEM((ts, D_A), F32), pltpu.VMEM((HALO, D_B), F32), pltpu.VMEM((D_MODEL, D_Z), BF16)],
        compiler_params=_params(),
    )(x, h1, mod6, n1post, n2pre, win_g, wout_g, w_spatial, bsp_full, gain, bias, w_pool, b_pool, pool_scale)


def _mixer_forward_tile(za, wc, bsp_ref, gain, bias, mixed_ref):
    ga, dga = _gelu_parts(za)
    u = ga[:, :D_A]
    v = ga[:, D_A:]
    mu = _rowmean(v)
    vc = v - mu
    rstd = lax.rsqrt(_rowmean(vc * vc) + EPS)
    vhat = vc * rstd
    vn = (vhat * gain + bias).astype(BF16)
    ts = za.shape[0]
    for k in range(ts // CHUNK):
        for h in range(N_HEADS):
            blk = vn[k * CHUNK:(k + 1) * CHUNK, h * HEAD_DIM:(h + 1) * HEAD_DIM]
            mixed_ref[k * CHUNK:(k + 1) * CHUNK, h * HEAD_DIM:(h + 1) * HEAD_DIM] = (
                _dot(wc[h], blk) + bsp_ref[:, h * HEAD_DIM:(h + 1) * HEAD_DIM])
    return u, vhat, rstd, vn, dga


def _fwd_fc1(h2, fc1_g, ts):
    s_len = h2.shape[0]
    cs = D_FF // N_CHIPS

    def body(h_ref, w_ref, q_ref):
        hb = h_ref[...]
        for j in range(N_CHIPS):
            p = jnp.maximum(_dot(hb, w_ref[j]), 0.0)
            q_ref[:, j * cs:(j + 1) * cs] = (p * p).astype(BF16)

    return pl.pallas_call(
        body, name="fwd_fc1", grid=(s_len // ts,),
        in_specs=[_rows(ts, D_MODEL), _VMEM],
        out_specs=_rows(ts, D_FF),
        out_shape=jax.ShapeDtypeStruct((s_len, D_FF), BF16),
        compiler_params=_params(),
    )(h2, fc1_g)


def _fwd_fc2_loss(q, x1, target, mod6, n2post, fc2_g, ts):
    s_len = q.shape[0]
    rs = D_FF // N_CHIPS

    def body(q_ref, x1_ref, t_ref, mod_ref, g_ref, w_ref, dy_ref, df_ref, loss_ref, s_ref):
        _zero_on_first_step(loss_ref, s_ref)
        gate_gain = mod_ref[5:6, :] * g_ref[...]
        f = _dot(q_ref[:, 0:rs], w_ref[0])
        for j in range(1, N_CHIPS):
            f = f + _dot(q_ref[:, j * rs:(j + 1) * rs], w_ref[j])
        r4 = lax.rsqrt(_rowmean(f * f) + EPS)
        fh = f * r4
        err = (x1_ref[...] + fh * gate_gain) - t_ref[...]
        loss_ref[...] += 0.5 * jnp.sum(_rowmean(err * err), axis=0, keepdims=True)
        dy = err * (1.0 / D_MODEL)
        dy_ref[...] = dy
        s_ref[...] += _colsum(dy * fh)
        gh = dy * gate_gain
        df_ref[...] = (r4 * (gh - fh * _rowmean(gh * fh))).astype(BF16)

    return pl.pallas_call(
        body, name="fwd_fc2_loss", grid=(s_len // ts,),
        in_specs=[_rows(ts, D_FF), _rows(ts, D_MODEL), _rows(ts, D_MODEL), _const((N_MOD, D_MODEL)),
                  _const((1, D_MODEL)), _VMEM],
        out_specs=[_rows(ts, D_MODEL), _rows(ts, D_MODEL), _const((1, 1)), _const((1, D_MODEL))],
        out_shape=[jax.ShapeDtypeStruct((s_len, D_MODEL), F32), jax.ShapeDtypeStruct((s_len, D_MODEL), BF16),
                   jax.ShapeDtypeStruct((1, 1), F32), jax.ShapeDtypeStruct((1, D_MODEL), F32)],
        compiler_params=_params(),
    )(q, x1, target, mod6, n2post, fc2_g)


def _join_w_in_on_first_step(win_ref, full_ref):
    cs = D_Z // N_CHIPS

    @pl.when(pl.program_id(0) == 0)
    def _():
        for j in range(N_CHIPS):
            full_ref[:, j * cs:(j + 1) * cs] = win_ref[j]


def _zero_on_first_step(*refs):
    @pl.when(pl.program_id(0) == 0)
    def _():
        for ref in refs:
            ref[...] = jnp.zeros_like(ref)


def _on_last_step(fn):
    pl.when(pl.program_id(0) == pl.num_programs(0) - 1)(fn)


def _store_shard_on_last_step(acc_ref, hbm_ref, sem, j):
    _on_last_step(lambda: pltpu.make_async_copy(acc_ref.at[j], hbm_ref.at[j], sem.at[j]).start())


def _wait_stores_on_last_step(*stores):
    def wait_all():
        for acc_ref, hbm_ref, sem in stores:
            for j in range(N_CHIPS):
                pltpu.make_async_copy(acc_ref.at[j], hbm_ref.at[j], sem.at[j]).wait()

    _on_last_step(wait_all)


def _bwd_fc2(df, q, fc2_g, ts):
    s_len = df.shape[0]
    cs = D_FF // N_CHIPS

    def body(df_ref, q_ref, w_ref, dp_ref, dw_hbm, dw_ref, dw_sem):
        _zero_on_first_step(dw_ref)
        dfb = df_ref[...]
        df2 = dfb * 2.0
        for j in range(N_CHIPS):
            qb = q_ref[:, j * cs:(j + 1) * cs]
            dw_ref[j] += _dot_tn(qb, dfb).reshape(2, cs // 2, D_MODEL)
            _store_shard_on_last_step(dw_ref, dw_hbm, dw_sem, j)
            dq2 = _dot_nt(df2, w_ref[j])
            dp_ref[:, j * cs:(j + 1) * cs] = (dq2 * jnp.sqrt(qb.astype(F32))).astype(BF16)
        _wait_stores_on_last_step((dw_ref, dw_hbm, dw_sem))

    dw_shape = (N_CHIPS, 2, cs // 2, D_MODEL)
    return pl.pallas_call(
        body, name="bwd_fc2", grid=(s_len // ts,),
        in_specs=[_rows(ts, D_MODEL), _rows(ts, D_FF), _VMEM],
        out_specs=[_rows(ts, D_FF), _ANY],
        out_shape=[jax.ShapeDtypeStruct((s_len, D_FF), BF16), jax.ShapeDtypeStruct(dw_shape, F32)],
        scratch_shapes=[pltpu.VMEM(dw_shape, F32), pltpu.SemaphoreType.DMA((N_CHIPS,))],
        compiler_params=_params(),
    )(df, q, fc2_g)


def _bwd_fc1_out(dp, dy, x1, mix, h2, ycat, mod6, n2pre, n1post, fc1_g, wout_g, dep, ts):
    s_len = dp.shape[0]
    cs = D_FF // N_CHIPS
    rs = D_MODEL // N_CHIPS

    def body(dp_ref, dy_ref, x1_ref, mix_ref, h2_ref, yc_ref, mod_ref, g2_ref, g1_ref, w1_ref, wo_ref, dep_ref,
             dx1_ref, dyc_ref, dshift2_ref, da2_ref, s1_ref, dw1_hbm, dwo_hbm, dw1_ref, dwo_ref, dw1_sem, dwo_sem):
        _zero_on_first_step(dshift2_ref, da2_ref, s1_ref, dw1_ref, dwo_ref)
        h2b = h2_ref[...]
        dh2 = None
        for j in range(N_CHIPS):
            dpb = dp_ref[:, j * cs:(j + 1) * cs]
            dw1_ref[j] += _dot_tn(h2b, dpb).reshape(2, D_MODEL // 2, cs)
            _store_shard_on_last_step(dw1_ref, dw1_hbm, dw1_sem, j)
            part = _dot_nt(dpb, w1_ref[j])
            dh2 = part if dh2 is None else dh2 + part
        x1 = x1_ref[...]
        r3 = lax.rsqrt(_rowmean(x1 * x1) + EPS)
        xh = x1 * r3
        a2 = g2_ref[...] * (1.0 + mod_ref[4:5, :])
        dshift2_ref[...] += _colsum(dh2)
        da2_ref[...] += _colsum(dh2 * xh)
        dxh = dh2 * a2
        dx1 = dy_ref[...] + r3 * (dxh - xh * _rowmean(dxh * xh))
        dx1_ref[...] = dx1

        mix = mix_ref[...]
        r2 = lax.rsqrt(_rowmean(mix * mix) + EPS)
        mh = mix * r2
        s1_ref[...] += _colsum(dx1 * mh)
        gh = dx1 * (mod_ref[2:3, :] * g1_ref[...])
        dmix = (r2 * (gh - mh * _rowmean(gh * mh))).astype(BF16)
        dwo_ref[...] += _dot_tn(yc_ref[...], dmix).reshape(N_CHIPS, 2, rs // 2, D_MODEL)
        for j in range(N_CHIPS):
            _store_shard_on_last_step(dwo_ref, dwo_hbm, dwo_sem, j)
            dyc_ref[:, j * rs:(j + 1) * rs] = _dot_nt(dmix, wo_ref[j])
        _wait_stores_on_last_step((dw1_ref, dw1_hbm, dw1_sem), (dwo_ref, dwo_hbm, dwo_sem))

    vec = jax.ShapeDtypeStruct((1, D_MODEL), F32)
    dw1_shape = (N_CHIPS, 2, D_MODEL // 2, cs)
    dwo_shape = (N_CHIPS, 2, rs // 2, D_MODEL)
    return pl.pallas_call(
        body, name="bwd_fc1_out", grid=(s_len // ts,),
        in_specs=[_rows(ts, D_FF), _rows(ts, D_MODEL), _rows(ts, D_MODEL), _rows(ts, D_MODEL), _rows(ts, D_MODEL),
                  _rows(ts, D_MODEL), _const((N_MOD, D_MODEL)), _const((1, D_MODEL)), _const((1, D_MODEL)), _VMEM,
                  _VMEM, _ANY],
        out_specs=[_rows(ts, D_MODEL), _rows(ts, D_MODEL)] + [_const((1, D_MODEL))] * 3 + [_ANY, _ANY],
        out_shape=[jax.ShapeDtypeStruct((s_len, D_MODEL), F32), jax.ShapeDtypeStruct((s_len, D_MODEL), F32),
                   vec, vec, vec, jax.ShapeDtypeStruct(dw1_shape, F32), jax.ShapeDtypeStruct(dwo_shape, F32)],
        scratch_shapes=[pltpu.VMEM(dw1_shape, F32), pltpu.VMEM(dwo_shape, F32), pltpu.SemaphoreType.DMA((N_CHIPS,)),
                        pltpu.SemaphoreType.DMA((N_CHIPS,))],
        compiler_params=_params(),
    )(dp, dy, x1, mix, h2, ycat, mod6, n2pre, n1post, fc1_g, wout_g, dep)


def _mixer_bwd(z, dyc, w_spatial, bsp_full, gain, bias, w_pool, b_pool, pool_scale, dep, ts):
    s_len = z.shape[0]
    nb = ts // HALO
    last = s_len // HALO - 1
    te = ts + HALO

    def body(z_ref, zprev_ref, znext_ref, dyc_ref, dynext_ref, ws_ref, bsp_ref, gain_ref, bias_ref, wp_ref, bp_ref,
             ps_ref, dep_ref, dz_ref, dws_ref, dbsp_ref, dgain_ref, dbias_ref, dwp_ref, dbp_ref, dps_ref, mixed_ref,
             dvn_ref):
        i = pl.program_id(0)

        @pl.when(i == 0)
        def _():
            for ref in (dws_ref, dbsp_ref, dgain_ref, dbias_ref, dwp_ref, dbp_ref, dps_ref):
                ref[...] = jnp.zeros_like(ref)

        wc = _tril_weights(ws_ref)
        gain = gain_ref[...]
        u, vhat, rstd, vn, dga = _mixer_forward_tile(z_ref[:, :2 * D_A], wc, bsp_ref, gain, bias_ref[...], mixed_ref)
        dya = dyc_ref[:, :D_A]
        du = dya * mixed_ref[...]
        dmixed = dya * u
        dmb = dmixed.astype(BF16)
        dm_sum = dmixed[0:CHUNK, :]
        for k in range(1, ts // CHUNK):
            dm_sum = dm_sum + dmixed[k * CHUNK:(k + 1) * CHUNK, :]
        r_idx = lax.broadcasted_iota(jnp.int32, (CHUNK, CHUNK), 0)
        s_idx = lax.broadcasted_iota(jnp.int32, (CHUNK, CHUNK), 1)
        causal = (s_idx <= r_idx).astype(F32)
        for h in range(N_HEADS):
            lanes = slice(h * HEAD_DIM, (h + 1) * HEAD_DIM)
            dbsp_ref[h] += jnp.sum(dm_sum[:, lanes], axis=1, keepdims=True)
            acc = None
            for k in range(ts // CHUNK):
                rows = slice(k * CHUNK, (k + 1) * CHUNK)
                t = _dot_nt(dmb[rows, lanes], vn[rows, lanes])
                acc = t if acc is None else acc + t
                dvn_ref[rows, lanes] = _dot_tn(wc[h], dmb[rows, lanes])
            dws_ref[h] += acc * causal
        dvn = dvn_ref[...]
        dgain_ref[...] += _colsum(dvn * vhat)
        dbias_ref[...] += _colsum(dvn)
        dvh = dvn * gain
        dv = rstd * (dvh - _rowmean(dvh) - vhat * _rowmean(dvh * vhat))
        dz_ref[:, :D_A] = (du * dga[:, :D_A]).astype(BF16)
        dz_ref[:, D_A:2 * D_A] = (dv * dga[:, D_A:]).astype(BF16)

        zb = z_ref[:, 2 * D_A:]
        prev = jnp.where(i == 0, 0.0, zprev_ref[...])
        zb_ext = jnp.concatenate([zb, znext_ref[...]], axis=0)
        sums = _causal_window_sums(jnp.concatenate([prev, zb_ext], axis=0))
        pos, inv_counts = _window_counts(i * ts, te)
        dyb_ext = jnp.concatenate([dyc_ref[:, D_A:], dynext_ref[...]], axis=0)
        dlin_ext = dyb_ext * ps_ref[...]
        dbp_ref[...] += _colsum(dlin_ext[:ts, :])
        scaled = []
        ddiffs = []
        lins = []
        for g in range(len(POOL_WINDOWS)):
            lanes = slice(g * GROUP_DIM, (g + 1) * GROUP_DIM)
            diff = (sums[g][HALO:, :] * inv_counts[g] - zb_ext[:, lanes]).astype(BF16)
            wpb = wp_ref[g].astype(BF16)
            dlb = dlin_ext[:, lanes].astype(BF16)
            lins.append(_dot(diff[:ts, :], wpb) + bp_ref[:, lanes])
            dwp_ref[g] += _dot_tn(diff[:ts, :], dlb[:ts, :])
            dd = _dot_nt(dlb, wpb)
            ddiffs.append(dd)
            scaled.append(jnp.where(pos < float(s_len), dd * inv_counts[g], 0.0))
        dps_ref[...] += _colsum(dyb_ext[:ts, :] * jnp.concatenate(lins, axis=1))
        back = _anticausal_window_sums(jnp.concatenate(scaled, axis=1))
        for g in range(len(POOL_WINDOWS)):
            dz_ref[:, 2 * D_A + g * GROUP_DIM:2 * D_A + (g + 1) * GROUP_DIM] = (
                back[g][:ts, :] - ddiffs[g][:ts, :]).astype(BF16)

    sq = jax.ShapeDtypeStruct((N_HEADS, CHUNK, CHUNK), F32)
    vec = jax.ShapeDtypeStruct((1, D_A), F32)
    return pl.pallas_call(
        body, name="mixer_bwd", grid=(s_len // ts,),
        in_specs=[_rows(ts, D_Z),
                  pl.BlockSpec((HALO, D_B), lambda i: (jnp.maximum(i * nb - 1, 0), 2)),
                  pl.BlockSpec((HALO, D_B), lambda i: (jnp.minimum((i + 1) * nb, last), 2)),
                  _rows(ts, D_MODEL),
                  pl.BlockSpec((HALO, D_B), lambda i: (jnp.minimum((i + 1) * nb, last), 1)),
                  _const((N_HEADS, CHUNK, CHUNK)), _const((CHUNK, D_A)), _const((1, D_A)), _const((1, D_A)),
                  _const((N_HEADS, GROUP_DIM, GROUP_DIM)), _const((1, D_B)), _const((1, D_B)), _ANY],
        out_specs=[_rows(ts, D_Z), _const((N_HEADS, CHUNK, CHUNK)), _const((N_HEADS, CHUNK, 1)), _const((1, D_A)),
                   _const((1, D_A)), _const((N_HEADS, GROUP_DIM, GROUP_DIM)), _const((1, D_B)), _const((1, D_B))],
        out_shape=[jax.ShapeDtypeStruct((s_len, D_Z), BF16), sq, jax.ShapeDtypeStruct((N_HEADS, CHUNK, 1), F32), vec,
                   vec, sq, vec, vec],
        scratch_shapes=[pltpu.VMEM((ts, D_A), F32), pltpu.VMEM((ts, D_A), F32)],
        compiler_params=_params(),
    )(z, z, z, dyc, dyc, w_spatial, bsp_full, gain, bias, w_pool, b_pool, pool_scale, dep)


def _bwd_in(dz, dx1, x, h1, rstd, mod6, n1pre, win_g, dep, ts):
    s_len = x.shape[0]
    cs = D_Z // N_CHIPS

    def body(dz_ref, dx1_ref, x_ref, h1_ref, r_ref, mod_ref, g_ref, w_ref, dep_ref, gx_ref, dshift_ref, da_ref, dw_hbm,
             dw_ref, wfull_ref, dw_sem):
        _zero_on_first_step(dshift_ref, da_ref, dw_ref)
        _join_w_in_on_first_step(w_ref, wfull_ref)
        dzb = dz_ref[...]
        dw = _dot_tn(h1_ref[...], dzb)
        for j in range(N_CHIPS):
            dw_ref[j] += dw[:, j * cs:(j + 1) * cs].reshape(2, D_MODEL // 2, cs)
        dh = _dot_nt(dzb, wfull_ref[...])
        r = r_ref[...]
        xh = x_ref[...] * r
        a1 = g_ref[...] * (1.0 + mod_ref[1:2, :])
        dshift_ref[...] += _colsum(dh)
        da_ref[...] += _colsum(dh * xh)
        dxh = dh * a1
        gx_ref[...] = dx1_ref[...] + r * (dxh - xh * _rowmean(dxh * xh))
        for j in range(N_CHIPS):
            _store_shard_on_last_step(dw_ref, dw_hbm, dw_sem, j)
        _wait_stores_on_last_step((dw_ref, dw_hbm, dw_sem))

    vec = jax.ShapeDtypeStruct((1, D_MODEL), F32)
    dw_shape = (N_CHIPS, 2, D_MODEL // 2, cs)
    return pl.pallas_call(
        body, name="bwd_in", grid=(s_len // ts,),
        in_specs=[_rows(ts, D_Z), _rows(ts, D_MODEL), _rows(ts, D_MODEL), _rows(ts, D_MODEL), _rows(ts, 1),
                  _const((N_MOD, D_MODEL)), _const((1, D_MODEL)), _VMEM, _ANY],
        out_specs=[_rows(ts, D_MODEL), _const((1, D_MODEL)), _const((1, D_MODEL)), _ANY],
        out_shape=[jax.ShapeDtypeStruct((s_len, D_MODEL), F32), vec, vec, jax.ShapeDtypeStruct(dw_shape, F32)],
        scratch_shapes=[pltpu.VMEM(dw_shape, F32), pltpu.VMEM((D_MODEL, D_Z), BF16),
                        pltpu.SemaphoreType.DMA((N_CHIPS,))],
        compiler_params=_params(),
    )(dz, dx1, x, h1, rstd, mod6, n1pre, win_g, dep)


def _adamw_math(w, g, m, v):
    m = ADAM_B1 * m + (1.0 - ADAM_B1) * g
    v = ADAM_B2 * v + (1.0 - ADAM_B2) * (g * g)
    m_hat = m / (1.0 - ADAM_B1 ** ADAM_STEP)
    v_hat = v / (1.0 - ADAM_B2 ** ADAM_STEP)
    delta = -ADAM_LR * (m_hat / (jnp.sqrt(v_hat) + ADAM_EPS) + ADAM_WD * w)
    return delta, m, v


def _adamw(gs, ws, ms, vs, name, steps):
    n = len(ws)

    def body(*refs):
        for i in range(n):
            g_ref, w_ref, m_ref, v_ref = refs[4 * i:4 * i + 4]
            d, nm, nv = _adamw_math(w_ref[...], g_ref[...], m_ref[...], v_ref[...])
            for ref, val in zip(refs[4 * n + 3 * i:4 * n + 3 * i + 3], (d, nm, nv)):
                ref[...] = val

    specs = [_rows(w.shape[0] // steps, w.shape[1]) for w in ws]
    out = pl.pallas_call(
        body, name=name, grid=(steps,),
        in_specs=[s for s in specs for _ in range(4)], out_specs=[s for s in specs for _ in range(3)],
        out_shape=[jax.ShapeDtypeStruct(w.shape, F32) for w in ws for _ in range(3)],
        compiler_params=_params(),
    )(*[a for quad in zip(gs, ws, ms, vs) for a in quad])
    return [out[3 * i:3 * i + 3] for i in range(n)]


def _ada_grad_adamw(sc_t, dmod_shard, w, m, v, tr):
    rows, cols = w.shape

    def body(s_ref, dm_ref, w_ref, m_ref, v_ref, g_ref, d_ref, nm_ref, nv_ref):
        g = s_ref[:, 0:1] * dm_ref[0:1, :]
        for b in range(1, N_DEV):
            g = g + s_ref[:, b:b + 1] * dm_ref[b:b + 1, :]
        g_ref[...] = g
        d, nm, nv = _adamw_math(w_ref[...], g, m_ref[...], v_ref[...])
        d_ref[...] = d
        nm_ref[...] = nm
        nv_ref[...] = nv

    spec = _rows(tr, cols)
    shape = jax.ShapeDtypeStruct((rows, cols), F32)
    return pl.pallas_call(
        body, name="ada_grad_adamw", grid=(rows // tr,),
        in_specs=[_rows(tr, N_DEV), _const((N_DEV, cols)), spec, spec, spec],
        out_specs=[spec] * 4, out_shape=[shape] * 4, compiler_params=_params(),
    )(sc_t, dmod_shard, w, m, v)


def _mod_grads(da1, dshift1, s1, da2, dshift2, s2, mod6, n1pre, n1post, n2pre, n2post):
    def body(da1_ref, ds1_ref, s1_ref, da2_ref, ds2_ref, s2_ref, mod_ref, n1_ref, p1_ref, n2_ref, p2_ref, dmod_ref,
             dn_ref):
        dmod_ref[0:1, :] = ds1_ref[...]
        dmod_ref[1:2, :] = da1_ref[...] * n1_ref[...]
        dmod_ref[2:3, :] = s1_ref[...] * p1_ref[...]
        dmod_ref[3:4, :] = ds2_ref[...]
        dmod_ref[4:5, :] = da2_ref[...] * n2_ref[...]
        dmod_ref[5:6, :] = s2_ref[...] * p2_ref[...]
        dn_ref[0:1, :] = da1_ref[...] * (1.0 + mod_ref[1:2, :])
        dn_ref[1:2, :] = s1_ref[...] * mod_ref[2:3, :]
        dn_ref[2:3, :] = da2_ref[...] * (1.0 + mod_ref[4:5, :])
        dn_ref[3:4, :] = s2_ref[...] * mod_ref[5:6, :]

    return pl.pallas_call(
        body, name="mod_grads",
        out_shape=[jax.ShapeDtypeStruct((N_MOD, D_MODEL), F32), jax.ShapeDtypeStruct((4, D_MODEL), F32)],
    )(da1, dshift1, s1, da2, dshift2, s2, mod6, n1pre, n1post, n2pre, n2post)


def _position():
    x, y, c = lax.axis_index("x"), lax.axis_index("y"), lax.axis_index("c")
    return x, y, c


def _flip(v, bit):
    return 1 - v if bit else v


def _peer(x, y, c, k):
    return (_flip(x, k & 4), _flip(y, k & 2), _flip(c, k & 1))


def _remote(src, dst, send_sem, recv_sem, device):
    return pltpu.make_async_remote_copy(src_ref=src, dst_ref=dst, send_sem=send_sem, recv_sem=recv_sem,
                                        device_id=device, device_id_type=MESH)


def _cast_to_slots(ws, pos, dep, name, steps):
    n = len(ws)

    def body(pos_ref, *refs):
        for w_ref, o_ref in zip(refs[:n], refs[n + 1:]):
            o_ref[0] = w_ref[...].astype(BF16)

    return pl.pallas_call(
        body, name=name,
        grid_spec=pltpu.PrefetchScalarGridSpec(
            num_scalar_prefetch=1, grid=(steps,),
            in_specs=[pl.BlockSpec((w.shape[0] // steps, w.shape[1]), lambda i, pos: (i, 0)) for w in ws] + [_ANY],
            out_specs=[pl.BlockSpec((1, w.shape[0] // steps, w.shape[1]), lambda i, pos: (pos[1], i, 0)) for w in ws]),
        out_shape=[jax.ShapeDtypeStruct((N_CHIPS,) + w.shape, BF16) for w in ws], compiler_params=_params(),
    )(pos, *ws, dep)


def _mod_exchange(c_row, w_ada_shard, b_ada_row):
    cs = w_ada_shard.shape[1]

    def body(c_ref, w_hbm, b_ref, mod_ref, sc_ref, rows_ref, w_ref, w_sem, send1, recv1, send2, recv2):
        x, y, c = _position()
        me = 4 * x + 2 * y + c
        chip = 2 * x + y
        w_load = pltpu.make_async_copy(w_hbm, w_ref, w_sem)
        w_load.start()
        cv = c_ref[...]
        sc_ref[me] = cv * jax.nn.sigmoid(cv)
        gather = [_remote(sc_ref.at[me], sc_ref.at[me], send1.at[k - 1], recv1.at[k - 1], _peer(x, y, c, k))
                  for k in range(1, N_DEV)]
        for cp in gather:
            cp.start()
        for k in range(1, N_DEV):
            px, py, pc = _peer(x, y, c, k)
            src = 4 * px + 2 * py + pc
            _remote(sc_ref.at[src], sc_ref.at[src], send1.at[k - 1], recv1.at[k - 1], (px, py, pc)).wait_recv()
        for cp in gather:
            cp.wait_send()
        sc_all = jnp.concatenate([sc_ref[b] for b in range(N_DEV)], axis=0)
        w_load.wait()
        part = _dot(sc_all.astype(BF16), w_ref[...].astype(BF16))
        part = part + b_ref[:, pl.ds(pl.multiple_of(chip * cs, LANES), cs)]
        for b in range(N_DEV):
            rows_ref[b] = part[b:b + 1, :]
        mod_ref[chip] = rows_ref[me]
        hand = []
        for k in (2, 4, 6):
            px, py, _ = _peer(x, y, c, k)
            hand.append(_remote(rows_ref.at[4 * px + 2 * py + c], mod_ref.at[chip], send2.at[k // 2 - 1],
                                recv2.at[k // 2 - 1], (px, py, c)))
        for cp in hand:
            cp.start()
        for k in (2, 4, 6):
            px, py, _ = _peer(x, y, c, k)
            pchip = 2 * px + py
            _remote(rows_ref.at[me], mod_ref.at[pchip], send2.at[k // 2 - 1], recv2.at[k // 2 - 1],
                    (px, py, c)).wait_recv()
        for cp in hand:
            cp.wait_send()

    return pl.pallas_call(
        body, name="mod_exchange",
        in_specs=[_VMEM, _ANY, _VMEM], out_specs=[_VMEM, _VMEM],
        out_shape=[jax.ShapeDtypeStruct((N_CHIPS, 1, cs), F32), jax.ShapeDtypeStruct((N_DEV, 1, D_MODEL), F32)],
        scratch_shapes=[pltpu.VMEM((N_DEV, 1, cs), F32), pltpu.VMEM(w_ada_shard.shape, F32), pltpu.SemaphoreType.DMA,
                        pltpu.SemaphoreType.DMA((N_DEV - 1,)),
                        pltpu.SemaphoreType.DMA((N_DEV - 1,)), pltpu.SemaphoreType.DMA((N_CHIPS - 1,)),
                        pltpu.SemaphoreType.DMA((N_CHIPS - 1,))],
        compiler_params=pltpu.CompilerParams(vmem_limit_bytes=VMEM_LIMIT),
    )(c_row, w_ada_shard, b_ada_row)


_HBM = pl.BlockSpec(memory_space=pltpu.HBM)
_SEM = pl.BlockSpec(memory_space=pltpu.SEMAPHORE)
_EFFECT = pltpu.SideEffectType.DATAFLOW_SIDE_EFFECTING
_CHIP_HOPS = (2, 4, 6)


def _in_hbm(a):
    return pltpu.with_memory_space_constraint(a, pltpu.HBM)


def _sems3():
    return pltpu.SemaphoreType.DMA((len(_CHIP_HOPS),))


def _ag_start(lands, after, name):
    n = len(lands)

    def body(*refs):
        zones = refs[:n]
        sends, recvs = refs[n + 1:2 * n + 1], refs[2 * n + 1:3 * n + 1]
        x, y, c = _position()
        chip = 2 * x + y
        for i in range(n):
            half = zones[i].shape[1] // 2
            mine = zones[i].at[chip, pl.ds(c * half, half)]
            for s, k in enumerate(_CHIP_HOPS):
                px, py, _ = _peer(x, y, c, k)
                _remote(mine, mine, sends[i].at[s], recvs[i].at[s], (px, py, c)).start()

    out = pl.pallas_call(
        body, name=name,
        in_specs=[_HBM] * n + [_ANY],
        out_specs=[_SEM] * (2 * n) + [_HBM] * n,
        out_shape=[_sems3()] * (2 * n) + [pltpu.HBM(z.shape, BF16) for z in lands],
        input_output_aliases={i: 2 * n + i for i in range(n)},
        compiler_params=pltpu.CompilerParams(has_side_effects=_EFFECT),
    )(*[_in_hbm(z) for z in lands], after)
    return [(out[2 * n + i], out[i], out[n + i]) for i in range(n)]


def _ag_pass(group, after, name):
    n = len(group)

    def body(*refs):
        zones = refs[:n]
        sends, recvs = refs[n:2 * n], refs[2 * n:3 * n]
        fsends, frecvs = refs[4 * n + 1:5 * n + 1], refs[5 * n + 1:6 * n + 1]
        x, y, c = _position()
        chip = 2 * x + y
        for i in range(n):
            half = zones[i].shape[1] // 2
            rows = pl.ds(c * half, half)
            for s, k in enumerate(_CHIP_HOPS):
                px, py, _ = _peer(x, y, c, k)
                landed = zones[i].at[2 * px + py, rows]
                _remote(landed, landed, sends[i].at[s], recvs[i].at[s], (px, py, c)).wait_recv()
                _remote(landed, landed, fsends[i].at[s], frecvs[i].at[s], (x, y, 1 - c)).start()
        for i in range(n):
            half = zones[i].shape[1] // 2
            mine = zones[i].at[chip, pl.ds(c * half, half)]
            for s, k in enumerate(_CHIP_HOPS):
                px, py, _ = _peer(x, y, c, k)
                _remote(mine, mine, sends[i].at[s], recvs[i].at[s], (px, py, c)).wait_send()

    out = pl.pallas_call(
        body, name=name,
        in_specs=[_HBM] * n + [_SEM] * (2 * n) + [_ANY],
        out_specs=[_HBM] * n + [_SEM] * (2 * n),
        out_shape=[pltpu.HBM(g[0].shape, BF16) for g in group] + [_sems3()] * (2 * n),
        input_output_aliases={i: i for i in range(n)},
        compiler_params=pltpu.CompilerParams(has_side_effects=_EFFECT),
    )(*[g[0] for g in group], *[g[1] for g in group], *[g[2] for g in group], after)
    return [(out[i], out[n + i], out[2 * n + i]) for i in range(n)]


def _ag_done(group, name):
    n = len(group)

    def body(*refs):
        lands = refs[:n]
        fsends, frecvs = refs[n:2 * n], refs[2 * n:3 * n]
        x, y, c = _position()
        for i in range(n):
            half = lands[i].shape[1] // 2
            for s, k in enumerate(_CHIP_HOPS):
                px, py, _ = _peer(x, y, c, k)
                sent = lands[i].at[2 * px + py, pl.ds(c * half, half)]
                got = lands[i].at[2 * px + py, pl.ds((1 - c) * half, half)]
                cp = _remote(sent, got, fsends[i].at[s], frecvs[i].at[s], (x, y, 1 - c))
                cp.wait_recv()
                cp.wait_send()

    out = pl.pallas_call(
        body, name=name,
        in_specs=[_HBM] * n + [_SEM] * (2 * n),
        out_specs=[_HBM] * n,
        out_shape=[pltpu.HBM(g[0].shape, BF16) for g in group],
        input_output_aliases={i: i for i in range(n)},
        compiler_params=pltpu.CompilerParams(has_side_effects=_EFFECT),
    )(*[g[0] for g in group], *[g[1] for g in group], *[g[2] for g in group])
    return list(out)


def _small_spread_start(slots, after):
    def body(z_ref, after_ref, sends, recvs, z_out):
        x, y, c = _position()
        mine = z_ref.at[2 * x + y]
        for s, k in enumerate(_CHIP_HOPS):
            px, py, _ = _peer(x, y, c, k)
            _remote(mine, mine, sends.at[s], recvs.at[s], (px, py, c)).start()

    sends, recvs, out = pl.pallas_call(
        body, name="small_spread_start",
        in_specs=[_HBM, _ANY], out_specs=[_SEM, _SEM, _HBM],
        out_shape=[_sems3(), _sems3(), pltpu.HBM(slots.shape, F32)],
        input_output_aliases={0: 2},
        compiler_params=pltpu.CompilerParams(has_side_effects=_EFFECT),
    )(_in_hbm(slots), after)
    return out, sends, recvs


def _small_spread_wait(slots, sends, recvs, afters):
    def body(z_ref, sends, recvs, *rest):
        x, y, c = _position()
        mine = z_ref.at[2 * x + y]
        for s, k in enumerate(_CHIP_HOPS):
            px, py, _ = _peer(x, y, c, k)
            cp = _remote(mine, z_ref.at[2 * px + py], sends.at[s], recvs.at[s], (px, py, c))
            cp.wait_recv()
            cp.wait_send()

    return pl.pallas_call(
        body, name="small_spread_wait",
        in_specs=[_HBM, _SEM, _SEM] + [_ANY] * len(afters), out_specs=_HBM, out_shape=pltpu.HBM(slots.shape, F32),
        input_output_aliases={0: 0},
        compiler_params=pltpu.CompilerParams(has_side_effects=_EFFECT),
    )(slots, sends, recvs, *afters)


def _sibling_sum(pgs, name, small=None):
    n = len(pgs)
    k = 0 if small is None else 1
    units = [(i, j) for i in range(n) for j in range(N_CHIPS)]

    def body(*refs):
        refs = list(refs)
        take = lambda count: [refs.pop(0) for _ in range(count)]
        ins, small_in = take(n), take(k)
        qbs, owns, slots_out = take(n), take(n), take(k)
        mine, other, stage, got = take(n), take(n), take(n), take(n)
        load_a, load_b, send, recv = take(4)
        x, y, c = _position()
        chip = 2 * x + y
        if k:
            sib_ref, pair_send, pair_recv = take(3)
            pair = _remote(small_in[0], sib_ref, pair_send, pair_recv, (x, y, 1 - c))
            pair.start()
        loads_a = [pltpu.make_async_copy(ins[i].at[j, 1 - c], other[i].at[j], load_a.at[u])
                   for u, (i, j) in enumerate(units)]
        loads_b = [pltpu.make_async_copy(ins[i].at[j, c], mine[i].at[j], load_b.at[u])
                   for u, (i, j) in enumerate(units)]
        for cp in loads_a + loads_b:
            cp.start()
        sent = []
        for u, (i, j) in enumerate(units):
            loads_a[u].wait()
            stage[i][j] = other[i][j].astype(BF16)
            cp = _remote(stage[i].at[j], got[i].at[j], send.at[u], recv.at[u], (x, y, 1 - c))
            cp.start()
            sent.append(cp)
        for u, (i, j) in enumerate(units):
            loads_b[u].wait()
            sent[u].wait_recv()
            q = mine[i][j] + got[i][j].astype(F32)
            mine[i][j] = q
            qbs[i][j] = q.astype(BF16)
        for i in range(n):
            owns[i][...] = mine[i][chip]
        if k:
            pair.wait()
            slots_out[0][chip] = small_in[0][...] + sib_ref[...]
        for cp in sent:
            cp.wait_send()

    wire = [(N_CHIPS,) + p.shape[2:] for p in pgs]
    extra_out, extra_scratch = [], []
    if k:
        extra_out = [jax.ShapeDtypeStruct((N_CHIPS,) + small.shape, F32)]
        extra_scratch = [pltpu.VMEM(small.shape, F32), pltpu.SemaphoreType.DMA, pltpu.SemaphoreType.DMA]
    out = pl.pallas_call(
        body, name=name, in_specs=[_ANY] * n + [_VMEM] * k, out_specs=[_VMEM] * (2 * n + k),
        out_shape=[jax.ShapeDtypeStruct(w, BF16) for w in wire] + [jax.ShapeDtypeStruct(w[1:], F32) for w in wire]
        + extra_out,
        scratch_shapes=[pltpu.VMEM(w, F32) for w in wire] * 2 + [pltpu.VMEM(w, BF16) for w in wire] * 2
        + [pltpu.SemaphoreType.DMA((len(units),))] * 4 + extra_scratch,
        compiler_params=pltpu.CompilerParams(vmem_limit_bytes=VMEM_LIMIT),
    )(*pgs, *([small] if k else []))
    return list(out[:n]), list(out[n:2 * n]), list(out[2 * n:])


def _rs_start(qbs, name, after=None):
    n = len(qbs)
    k = 0 if after is None else 1

    def body(*refs):
        outs, inboxes = refs[:n], refs[n:2 * n]
        sends, recvs = refs[2 * n + k:3 * n + k], refs[3 * n + k:4 * n + k]
        x, y, c = _position()
        chip = 2 * x + y
        for i in range(n):
            for s, hop in enumerate(_CHIP_HOPS):
                px, py, _ = _peer(x, y, c, hop)
                _remote(outs[i].at[2 * px + py], inboxes[i].at[chip], sends[i].at[s], recvs[i].at[s], (px, py, c)).start()

    inboxes = [_in_hbm(lax.empty(q.shape, BF16)) for q in qbs]
    out = pl.pallas_call(
        body, name=name,
        in_specs=[_HBM] * (2 * n) + [_ANY] * k,
        out_specs=[_SEM] * (2 * n) + [_HBM] * (2 * n),
        out_shape=[_sems3()] * (2 * n) + [pltpu.HBM(q.shape, BF16) for q in qbs] * 2,
        input_output_aliases={i: 2 * n + i for i in range(2 * n)},
        compiler_params=pltpu.CompilerParams(has_side_effects=_EFFECT),
    )(*[_in_hbm(q) for q in qbs], *inboxes, *([after] if k else []))
    return [(out[2 * n + i], out[3 * n + i], out[i], out[n + i]) for i in range(n)]


def _rs_wait(group, after, name):
    n = len(group)

    def body(*refs):
        outs, inboxes = refs[:n], refs[n:2 * n]
        sends, recvs = refs[2 * n:3 * n], refs[3 * n:4 * n]
        x, y, c = _position()
        for i in range(n):
            for s, k in enumerate(_CHIP_HOPS):
                px, py, _ = _peer(x, y, c, k)
                slot = 2 * px + py
                cp = _remote(outs[i].at[slot], inboxes[i].at[slot], sends[i].at[s], recvs[i].at[s], (px, py, c))
                cp.wait_recv()
                cp.wait_send()

    out = pl.pallas_call(
        body, name=name,
        in_specs=[_HBM] * (2 * n) + [_SEM] * (2 * n) + [_ANY],
        out_specs=[_HBM] * n,
        out_shape=[pltpu.HBM(g[1].shape, BF16) for g in group],
        input_output_aliases={n + i: i for i in range(n)},
        compiler_params=pltpu.CompilerParams(has_side_effects=_EFFECT),
    )(*[g[0] for g in group], *[g[1] for g in group], *[g[2] for g in group], *[g[3] for g in group], after)
    return list(out)


def _final_share(inboxes, owns, name):
    n = len(inboxes)
    units = [(i, s) for i in range(n) for s in range(len(_CHIP_HOPS))]

    def body(*refs):
        ins, mine, outs, landed = refs[:n], refs[n:2 * n], refs[2 * n:3 * n], refs[3 * n:4 * n]
        load, send, recv = refs[4 * n:]
        x, y, c = _position()
        loads = []
        for u, (i, s) in enumerate(units):
            px, py, _ = _peer(x, y, c, _CHIP_HOPS[s])
            loads.append(pltpu.make_async_copy(ins[i].at[2 * px + py], landed[i].at[s], load.at[u]))
        for cp in loads:
            cp.start()
        copies = []
        for i in range(n):
            for s in range(len(_CHIP_HOPS)):
                loads[len(_CHIP_HOPS) * i + s].wait()
            total = (landed[i][0].astype(F32) + landed[i][1].astype(F32)) + landed[i][2].astype(F32)
            outs[i][c] = total + mine[i][...]
            cp = _remote(outs[i].at[c], outs[i].at[c], send.at[i], recv.at[i], (x, y, 1 - c))
            cp.start()
            copies.append(cp)
        for i in range(n):
            theirs = outs[i].at[1 - c]
            _remote(theirs, theirs, send.at[i], recv.at[i], (x, y, 1 - c)).wait_recv()
        for cp in copies:
            cp.wait_send()

    return pl.pallas_call(
        body, name=name, in_specs=[_ANY] * n + [_VMEM] * n, out_specs=[_VMEM] * n,
        out_shape=[jax.ShapeDtypeStruct((2,) + o.shape, F32) for o in owns],
        scratch_shapes=[pltpu.VMEM((len(_CHIP_HOPS),) + o.shape, BF16) for o in owns]
        + [pltpu.SemaphoreType.DMA((len(units),)), pltpu.SemaphoreType.DMA((n,)), pltpu.SemaphoreType.DMA((n,))],
        compiler_params=pltpu.CompilerParams(vmem_limit_bytes=VMEM_LIMIT),
    )(*inboxes, *owns)


_SMALL = (("b_ada", N_MOD * D_MODEL), ("norm1_pre", D_MODEL), ("norm1_post", D_MODEL), ("norm2_pre", D_MODEL),
          ("norm2_post", D_MODEL), ("w_spatial", N_HEADS * CHUNK * CHUNK), ("b_spatial", N_HEADS * CHUNK),
          ("ln_v_gain", D_A), ("ln_v_bias", D_A), ("w_pool", N_HEADS * GROUP_DIM * GROUP_DIM),
          ("b_pool", D_B), ("pool_scale", D_B))
_MOD_ROWS = N_MOD * D_MODEL // LANES


def _packed_rows(size):
    return -(-(size // LANES) // SUBLANES) * SUBLANES


def _pack(parts):
    out = []
    for name, size in _SMALL:
        a = parts[name].reshape(size // LANES, LANES)
        pad = _packed_rows(size) - a.shape[0]
        out.append(jnp.pad(a, ((0, pad), (0, 0))) if pad else a)
    return out


def _small_adamw(slots, ws, ms, vs):
    n = len(_SMALL)
    head = N_DEV * _MOD_ROWS

    def body(*refs):
        s_ref, w, m, v = refs[0], refs[1:1 + n], refs[1 + n:1 + 2 * n], refs[1 + 2 * n:1 + 3 * n]
        outs = refs[1 + 3 * n:1 + 7 * n]
        dmod_ref, loss_ref, t_ref = refs[1 + 7 * n:]
        t_ref[...] = ((s_ref[0] + s_ref[1]) + s_ref[2]) + s_ref[3]
        dmod_ref[...] = t_ref[0:head, :]
        loss_ref[...] = t_ref[t_ref.shape[0] - 8:, :]
        row = head
        for i, (_, size) in enumerate(_SMALL):
            if i == 0:
                g = t_ref[0:_MOD_ROWS, :]
                for b in range(1, N_DEV):
                    g = g + t_ref[b * _MOD_ROWS:(b + 1) * _MOD_ROWS, :]
            else:
                g = t_ref[row:row + size // LANES, :]
                row += _packed_rows(size)
            d, nm, nv = _adamw_math(w[i][...], g, m[i][...], v[i][...])
            for ref, val in zip(outs[4 * i:4 * i + 4], (g, d, nm, nv)):
                ref[...] = val

    each = [jax.ShapeDtypeStruct((size // LANES, LANES), F32) for _, size in _SMALL for _ in range(4)]
    out = pl.pallas_call(
        body, name="small_adamw",
        out_shape=each + [jax.ShapeDtypeStruct((head, LANES), F32), jax.ShapeDtypeStruct((SUBLANES, LANES), F32)],
        scratch_shapes=[pltpu.VMEM(slots.shape[1:], F32)],
        compiler_params=pltpu.CompilerParams(vmem_limit_bytes=VMEM_LIMIT),
    )(slots, *ws, *ms, *vs)
    return [out[4 * i:4 * i + 4] for i in range(n)], out[4 * n], out[4 * n + 1]


def kernel(x, c, w_ada, b_ada, norm1_pre, norm1_post, w_in, w_spatial, b_spatial, ln_v_gain, ln_v_bias, w_pool, b_pool, pool_scale, w_out, norm2_pre, norm2_post, w_fc1, w_fc2, loss_target, m_w_ada, m_b_ada, m_norm1_pre, m_norm1_post, m_w_in, m_w_spatial, m_b_spatial, m_ln_v_gain, m_ln_v_bias, m_w_pool, m_b_pool, m_pool_scale, m_w_out, m_norm2_pre, m_norm2_post, m_w_fc1, m_w_fc2, v_w_ada, v_b_ada, v_norm1_pre, v_norm1_post, v_w_in, v_w_spatial, v_b_spatial, v_ln_v_gain, v_ln_v_bias, v_w_pool, v_b_pool, v_pool_scale, v_w_out, v_norm2_pre, v_norm2_post, v_w_fc1, v_w_fc2):
    weights = dict(w_ada=w_ada, b_ada=b_ada, norm1_pre=norm1_pre, norm1_post=norm1_post, w_in=w_in,
                   w_spatial=w_spatial, b_spatial=b_spatial, ln_v_gain=ln_v_gain, ln_v_bias=ln_v_bias, w_pool=w_pool,
                   b_pool=b_pool, pool_scale=pool_scale, w_out=w_out, norm2_pre=norm2_pre, norm2_post=norm2_post,
                   w_fc1=w_fc1, w_fc2=w_fc2)
    m_old = dict(w_ada=m_w_ada, b_ada=m_b_ada, norm1_pre=m_norm1_pre, norm1_post=m_norm1_post, w_in=m_w_in,
                 w_spatial=m_w_spatial, b_spatial=m_b_spatial, ln_v_gain=m_ln_v_gain, ln_v_bias=m_ln_v_bias,
                 w_pool=m_w_pool, b_pool=m_b_pool, pool_scale=m_pool_scale, w_out=m_w_out, norm2_pre=m_norm2_pre,
                 norm2_post=m_norm2_post, w_fc1=m_w_fc1, w_fc2=m_w_fc2)
    v_old = dict(w_ada=v_w_ada, b_ada=v_b_ada, norm1_pre=v_norm1_pre, norm1_post=v_norm1_post, w_in=v_w_in,
                 w_spatial=v_w_spatial, b_spatial=v_b_spatial, ln_v_gain=v_ln_v_gain, ln_v_bias=v_ln_v_bias,
                 w_pool=v_w_pool, b_pool=v_b_pool, pool_scale=v_pool_scale, w_out=v_w_out, norm2_pre=v_norm2_pre,
                 norm2_post=v_norm2_post, w_fc1=v_w_fc1, w_fc2=v_w_fc2)
    order = ("w_ada", "b_ada", "norm1_pre", "norm1_post", "w_in", "w_spatial", "b_spatial", "ln_v_gain", "ln_v_bias",
             "w_pool", "b_pool", "pool_scale", "w_out", "norm2_pre", "norm2_post", "w_fc1", "w_fc2")
    mx, my, mc = _position()
    me = 4 * mx + 2 * my + mc
    chip = 2 * mx + my
    row = lambda a: a.reshape(1, -1)

    pos = jnp.stack([mc, chip]).astype(jnp.int32)
    xs, target = x[0], loss_target[0]
    n1pre, n1post, n2pre, n2post = row(norm1_pre), row(norm1_post), row(norm2_pre), row(norm2_post)
    mixer = (w_spatial, jnp.repeat(b_spatial.T, HEAD_DIM, axis=1), row(ln_v_gain), row(ln_v_bias), w_pool,
             row(b_pool), row(pool_scale))
    ts_big, ts_mid = 512, 256

    mod4, sc_all = _mod_exchange(c, w_ada, row(b_ada))
    mod6 = mod4.reshape(N_MOD, D_MODEL)
    ag = _ag_start(_cast_to_slots([w_in, w_out], pos, mod4, "cast_mix", 1), mod4, "ag_start_mix")
    ag += _ag_start(_cast_to_slots([w_fc1, w_fc2], pos, ag[0][0], "cast_mlp", 4), ag[0][0], "ag_start_mlp")

    h1, rstd1 = _prenorm(xs, mod6, n1pre, ag[2][0], ts_big)
    win_g, wout_g = _ag_done(_ag_pass([ag[0], ag[1]], h1, "ag_pass_mix"), "ag_done_mix")
    z, ycat, mix, x1, h2 = _fwd_mix(xs, h1, mod6, n1post, n2pre, win_g, wout_g, *mixer, ts_big)
    (fc1_g,) = _ag_done(_ag_pass([ag[2]], h2, "ag_pass_fc1"), "ag_done_fc1")
    q = _fwd_fc1(h2, fc1_g, ts_big)
    (fc2_g,) = _ag_done(_ag_pass([ag[3]], q, "ag_pass_fc2"), "ag_done_fc2")
    dy, df, loss, s2 = _fwd_fc2_loss(q, x1, target, mod6, n2post, fc2_g, ts_big)

    def reduce_start(partials, tag, small=None):
        wire, owns, slots = _sibling_sum(partials, "sibling_sum_" + tag, small)
        return _rs_start(wire, "rs_start_" + tag), owns, slots

    def reduce_finish(state, owns, names, tag, dep):
        inboxes = _rs_wait(state, dep, "rs_wait_" + tag)
        shards = _final_share(inboxes, owns, "final_share_" + tag)
        for n, g in zip(names, shards):
            grads[n] = g.reshape(weights[n].shape)
        updates = _adamw([grads[n] for n in names], [weights[n] for n in names], [m_old[n] for n in names],
                         [v_old[n] for n in names], "adamw_" + tag, 4)
        for n, (d, nm, nv) in zip(names, updates):
            deltas[n], new_m[n], new_v[n] = d, nm, nv

    grads, deltas, new_m, new_v = {}, {}, {}, {}
    dp, g_fc2 = _bwd_fc2(df, q, fc2_g, ts_big)
    state_fc2, owns_fc2, _ = reduce_start([g_fc2], "fc2")
    dx1, dyc, dshift2, da2, s1, g_fc1, g_out = _bwd_fc1_out(
        dp, dy, x1, mix, h2, ycat, mod6, n2pre, n1post, fc1_g, wout_g, state_fc2[0][0], ts_mid)
    state_mid, owns_mid, _ = reduce_start([g_fc1, g_out], "mid")
    dz, dws, dbsp, dgain, dbias, dwp, dbp, dps = _mixer_bwd(z, dyc, *mixer, state_mid[0][0], ts_big)
    grad_x, dshift1, da1, g_in = _bwd_in(dz, dx1, xs, h1, rstd1, mod6, n1pre, win_g, state_mid[0][0], ts_big)
    dmod6, dnorms = _mod_grads(da1, dshift1, s1, da2, dshift2, s2, mod6, n1pre, n1post, n2pre, n2post)

    parts = dict(b_ada=dmod6, norm1_pre=dnorms[0], norm1_post=dnorms[1], norm2_pre=dnorms[2], norm2_post=dnorms[3],
                 w_spatial=dws, b_spatial=dbsp, ln_v_gain=dgain, ln_v_bias=dbias, w_pool=dwp, b_pool=dbp,
                 pool_scale=dps)
    pieces = _pack(parts)
    slots = lax.dynamic_update_slice(jnp.zeros((N_DEV * _MOD_ROWS, LANES), F32), pieces[0], (me * _MOD_ROWS, 0))
    loss_tile = jnp.pad(loss, ((0, SUBLANES - 1), (0, LANES - 1)))
    wire_in, owns_in, pair_sum = _sibling_sum([g_in], "sibling_sum_in",
                                              jnp.concatenate([slots] + pieces[1:] + [loss_tile], axis=0))
    spread = _small_spread_start(pair_sum[0], wire_in[0])
    state_in = _rs_start(wire_in, "rs_start_in", spread[0])
    reduce_finish(state_fc2 + state_mid, owns_fc2 + owns_mid, ("w_fc2", "w_fc1", "w_out"), "mlp", state_in[0][0])
    flat = lambda d: [d[n].reshape(size // LANES, LANES) for n, size in _SMALL]
    small_out, dmod_all, loss_tile = _small_adamw(
        _small_spread_wait(*spread, [deltas[n] for n in ("w_fc2", "w_fc1", "w_out")]), flat(weights), flat(m_old),
        flat(v_old))
    loss = loss_tile[0, 0]
    for (n, _), (g, d, nm, nv) in zip(_SMALL, small_out):
        shape = weights[n].shape
        grads[n], deltas[n], new_m[n], new_v[n] = g.reshape(shape), d.reshape(shape), nm.reshape(shape), nv.reshape(shape)

    dmod_all = dmod_all.reshape(N_DEV, N_MOD * D_MODEL)
    cs = w_ada.shape[1]
    dmod_shard = lax.dynamic_slice(dmod_all, (0, chip * cs), (N_DEV, cs))
    sc_t = sc_all.reshape(N_DEV, D_MODEL).T
    grads["w_ada"], deltas["w_ada"], new_m["w_ada"], new_v["w_ada"] = _ada_grad_adamw(
        sc_t, dmod_shard, w_ada, m_w_ada, v_w_ada, 256)

    reduce_finish(state_in, owns_in, ("w_in",), "in", deltas["w_ada"])

    return (loss, grad_x[None], *[grads[n] for n in order], *[deltas[n] for n in order],
            *[new_m[n] for n in order], *[new_v[n] for n in order])
```

```python
import jax
import jax.numpy as jnp
from jax import lax
from jax.experimental import pallas as pl
from jax.experimental.pallas import tpu as pltpu

F32 = jnp.float32
BF16 = jnp.bfloat16
MESH = pl.DeviceIdType.MESH

D_MODEL = 1024
D_A = 512
D_B = 512
D_Z = 2 * D_A + D_B
N_HEADS = 4
HEAD_DIM = 128
CHUNK = 128
POOL_WINDOWS = (2, 4, 8, 16)
GROUP_DIM = 128
D_FF = 4096
N_MOD = 6
EPS = 1e-6
HALO = 16
N_CHIPS = 4
N_DEV = 8

ADAM_LR = 0.001
ADAM_B1 = 0.9
ADAM_B2 = 0.999
ADAM_EPS = 1e-08
ADAM_WD = 0.01
ADAM_STEP = 10

VMEM_LIMIT = 56 * 1024 * 1024
LANES = 128
SUBLANES = 8

_VMEM = pl.BlockSpec(memory_space=pltpu.VMEM)
_ANY = pl.BlockSpec(memory_space=pl.ANY)


def _params(n_grid_axes=1):
    return pltpu.CompilerParams(dimension_semantics=("arbitrary",) * n_grid_axes, vmem_limit_bytes=VMEM_LIMIT)


def _rows(ts, width):
    return pl.BlockSpec((ts, width), lambda i: (i, 0))


def _const(shape):
    return pl.BlockSpec(shape, lambda i: (0,) * len(shape))


def _dot(a, b):
    return jnp.dot(a, b, preferred_element_type=F32)


def _dot_nt(a, b):
    return lax.dot_general(a, b, (((1,), (1,)), ((), ())), preferred_element_type=F32)


def _dot_tn(a, b):
    return lax.dot_general(a, b, (((0,), (0,)), ((), ())), preferred_element_type=F32)


def _rowmean(v):
    return jnp.mean(v, axis=-1, keepdims=True)


def _colsum(v):
    return jnp.sum(v, axis=0, keepdims=True)


def _gelu_parts(z):
    k0 = 0.7978845608028654
    k1 = 0.044715
    z2 = z * z
    t = jnp.tanh(z * (k0 + (k0 * k1) * z2))
    u = 0.5 * t + 0.5
    g = z * u
    dg = u + (0.5 * z) * (1.0 - t * t) * (k0 + (3.0 * k0 * k1) * z2)
    return g, dg


def _tril_weights(ws_ref):
    r = lax.broadcasted_iota(jnp.int32, (CHUNK, CHUNK), 0)
    s = lax.broadcasted_iota(jnp.int32, (CHUNK, CHUNK), 1)
    mask = (s <= r).astype(F32)
    return [(ws_ref[h] * mask).astype(BF16) for h in range(N_HEADS)]


def _window_counts(first_row, n_rows):
    pos = (first_row + lax.broadcasted_iota(jnp.int32, (n_rows, 1), 0)).astype(F32)
    return pos, [1.0 / jnp.minimum(pos + 1.0, float(w)) for w in POOL_WINDOWS]


def _causal_window_sums(ext):
    out = []
    e = ext
    shift = 1
    for g in range(len(POOL_WINDOWS)):
        e = e + pltpu.roll(e, shift, 0)
        shift *= 2
        out.append(e[:, g * GROUP_DIM:(g + 1) * GROUP_DIM])
    return out


def _anticausal_window_sums(ext):
    n = ext.shape[0]
    out = []
    e = ext
    shift = 1
    for g in range(len(POOL_WINDOWS)):
        e = e + pltpu.roll(e, n - shift, 0)
        shift *= 2
        out.append(e[:, g * GROUP_DIM:(g + 1) * GROUP_DIM])
    return out


def _fwd_mix(x, mod6, n1pre, n1post, n2pre, win_g, wout_g, w_spatial, bsp_full, gain, bias, w_pool, b_pool, pool_scale, ts):
    s_len = x.shape[0]
    rs = D_MODEL // N_CHIPS

    def body(x_ref, mod_ref, g1pre_ref, g1post_ref, g2pre_ref, win_ref, wout_ref, ws_ref, bsp_ref, gain_ref,
             bias_ref, wp_ref, bp_ref, ps_ref, z_ref, y_ref, mix_ref, x1_ref, h2_ref, mixed_ref, prev_ref, wfull_ref):
        i = pl.program_id(0)
        _zero_on_first_step(prev_ref)
        _join_w_in_on_first_step(win_ref, wfull_ref)
        xv = x_ref[...]
        r = lax.rsqrt(_rowmean(xv * xv) + EPS)
        hb = ((xv * r) * (g1pre_ref[...] * (1.0 + mod_ref[1:2, :])) + mod_ref[0:1, :]).astype(BF16)
        z_ref[...] = _dot(hb, wfull_ref[...])

        wc = _tril_weights(ws_ref)
        u, _, _, _, _ = _mixer_forward_tile(z_ref[:, :2 * D_A], wc, bsp_ref, gain_ref[...], bias_ref[...], mixed_ref)
        y_ref[:, :D_A] = (u * mixed_ref[...]).astype(BF16)
        zb = z_ref[:, 2 * D_A:]
        sums = _causal_window_sums(jnp.concatenate([prev_ref[...], zb], axis=0))
        prev_ref[...] = zb[ts - HALO:, :]
        _, inv_counts = _window_counts(i * ts, ts)
        for g in range(len(POOL_WINDOWS)):
            lanes = slice(g * GROUP_DIM, (g + 1) * GROUP_DIM)
            diff = sums[g][HALO:, :] * inv_counts[g] - zb[:, lanes]
            lin = _dot(diff.astype(BF16), wp_ref[g].astype(BF16)) + bp_ref[:, lanes]
            y_ref[:, D_A + g * GROUP_DIM:D_A + (g + 1) * GROUP_DIM] = (lin * ps_ref[:, lanes]).astype(BF16)

        mix = None
        for j in range(N_CHIPS):
            part = _dot(y_ref[:, j * rs:(j + 1) * rs], wout_ref[j])
            mix = part if mix is None else mix + part
        mix_ref[...] = mix
        r2 = lax.rsqrt(_rowmean(mix * mix) + EPS)
        x1 = xv + (mix * r2) * (mod_ref[2:3, :] * g1post_ref[...])
        x1_ref[...] = x1
        r3 = lax.rsqrt(_rowmean(x1 * x1) + EPS)
        h2_ref[...] = ((x1 * r3) * (g2pre_ref[...] * (1.0 + mod_ref[4:5, :])) + mod_ref[3:4, :]).astype(BF16)

    vec = _const((1, D_MODEL))
    f32_rows = jax.ShapeDtypeStruct((s_len, D_MODEL), F32)
    bf16_rows = jax.ShapeDtypeStruct((s_len, D_MODEL), BF16)
    return pl.pallas_call(
        body, name="fwd_mix", grid=(s_len // ts,),
        in_specs=[_rows(ts, D_MODEL), _const((N_MOD, D_MODEL)), vec, vec, vec, _VMEM, _VMEM,
                  _const((N_HEADS, CHUNK, CHUNK)), _const((CHUNK, D_A)), _const((1, D_A)), _const((1, D_A)),
                  _const((N_HEADS, GROUP_DIM, GROUP_DIM)), _const((1, D_B)), _const((1, D_B))],
        out_specs=[_rows(ts, D_Z), _rows(ts, D_MODEL), _rows(ts, D_MODEL), _rows(ts, D_MODEL), _rows(ts, D_MODEL)],
        out_shape=[jax.ShapeDtypeStruct((s_len, D_Z), F32), bf16_rows, f32_rows, f32_rows, bf16_rows],
        scratch_shapes=[pltpu.VMEM((ts, D_A), F32), pltpu.VMEM((HALO, D_B), F32), pltpu.VMEM((D_MODEL, D_Z), BF16)],
        compiler_params=_params(),
    )(x, mod6, n1pre, n1post, n2pre, win_g, wout_g, w_spatial, bsp_full, gain, bias, w_pool, b_pool, pool_scale)


def _mixer_forward_tile(za, wc, bsp_ref, gain, bias, mixed_ref):
    ga, dga = _gelu_parts(za)
    u = ga[:, :D_A]
    v = ga[:, D_A:]
    mu = _rowmean(v)
    vc = v - mu
    rstd = lax.rsqrt(_rowmean(vc * vc) + EPS)
    vhat = vc * rstd
    vn = (vhat * gain + bias).astype(BF16)
    ts = za.shape[0]
    for k in range(ts // CHUNK):
        for h in range(N_HEADS):
            blk = vn[k * CHUNK:(k + 1) * CHUNK, h * HEAD_DIM:(h + 1) * HEAD_DIM]
            mixed_ref[k * CHUNK:(k + 1) * CHUNK, h * HEAD_DIM:(h + 1) * HEAD_DIM] = (
                _dot(wc[h], blk) + bsp_ref[:, h * HEAD_DIM:(h + 1) * HEAD_DIM])
    return u, vhat, rstd, vn, dga


def _fwd_fc1(h2, fc1_g, ts):
    s_len = h2.shape[0]
    cs = D_FF // N_CHIPS

    def body(h_ref, w_ref, q_ref):
        hb = h_ref[...]
        for j in range(N_CHIPS):
            p = jnp.maximum(_dot(hb, w_ref[j]), 0.0)
            q_ref[:, j * cs:(j + 1) * cs] = (p * p).astype(BF16)

    return pl.pallas_call(
        body, name="fwd_fc1", grid=(s_len // ts,),
        in_specs=[_rows(ts, D_MODEL), _VMEM],
        out_specs=_rows(ts, D_FF),
        out_shape=jax.ShapeDtypeStruct((s_len, D_FF), BF16),
        compiler_params=_params(),
    )(h2, fc1_g)


def _fwd_fc2_loss(q, x1, target, mod6, n2post, fc2_g, ts):
    s_len = q.shape[0]
    rs = D_FF // N_CHIPS

    def body(q_ref, x1_ref, t_ref, mod_ref, g_ref, w_ref, dy_ref, df_ref, loss_ref, s_ref):
        _zero_on_first_step(loss_ref, s_ref)
        gate_gain = mod_ref[5:6, :] * g_ref[...]
        f = _dot(q_ref[:, 0:rs], w_ref[0])
        for j in range(1, N_CHIPS):
            f = f + _dot(q_ref[:, j * rs:(j + 1) * rs], w_ref[j])
        r4 = lax.rsqrt(_rowmean(f * f) + EPS)
        fh = f * r4
        err = (x1_ref[...] + fh * gate_gain) - t_ref[...]
        loss_ref[...] += 0.5 * jnp.sum(_rowmean(err * err), axis=0, keepdims=True)
        dy = err * (1.0 / D_MODEL)
        dy_ref[...] = dy
        s_ref[...] += _colsum(dy * fh)
        gh = dy * gate_gain
        df_ref[...] = (r4 * (gh - fh * _rowmean(gh * fh))).astype(BF16)

    return pl.pallas_call(
        body, name="fwd_fc2_loss", grid=(s_len // ts,),
        in_specs=[_rows(ts, D_FF), _rows(ts, D_MODEL), _rows(ts, D_MODEL), _const((N_MOD, D_MODEL)),
                  _const((1, D_MODEL)), _VMEM],
        out_specs=[_rows(ts, D_MODEL), _rows(ts, D_MODEL), _const((1, 1)), _const((1, D_MODEL))],
        out_shape=[jax.ShapeDtypeStruct((s_len, D_MODEL), F32), jax.ShapeDtypeStruct((s_len, D_MODEL), BF16),
                   jax.ShapeDtypeStruct((1, 1), F32), jax.ShapeDtypeStruct((1, D_MODEL), F32)],
        compiler_params=_params(),
    )(q, x1, target, mod6, n2post, fc2_g)


def _join_w_in_on_first_step(win_ref, full_ref):
    cs = D_Z // N_CHIPS

    @pl.when(pl.program_id(0) == 0)
    def _():
        for j in range(N_CHIPS):
            full_ref[:, j * cs:(j + 1) * cs] = win_ref[j]


def _zero_on_first_step(*refs):
    @pl.when(pl.program_id(0) == 0)
    def _():
        for ref in refs:
            ref[...] = jnp.zeros_like(ref)


def _on_last_step(fn):
    pl.when(pl.program_id(0) == pl.num_programs(0) - 1)(fn)


def _store_shard_on_last_step(acc_ref, hbm_ref, sem, j):
    _on_last_step(lambda: pltpu.make_async_copy(acc_ref.at[j], hbm_ref.at[j], sem.at[j]).start())


def _wait_stores_on_last_step(*stores):
    def wait_all():
        for acc_ref, hbm_ref, sem in stores:
            for j in range(N_CHIPS):
                pltpu.make_async_copy(acc_ref.at[j], hbm_ref.at[j], sem.at[j]).wait()

    _on_last_step(wait_all)


def _bwd_fc2(df, q, fc2_g, ts):
    s_len = df.shape[0]
    cs = D_FF // N_CHIPS

    def body(df_ref, q_ref, w_ref, dp_ref, dw_hbm, dw_ref, dw_sem):
        _zero_on_first_step(dw_ref)
        dfb = df_ref[...]
        df2 = dfb * 2.0
        for j in range(N_CHIPS):
            qb = q_ref[:, j * cs:(j + 1) * cs]
            dw_ref[j] += _dot_tn(qb, dfb).reshape(2, cs // 2, D_MODEL)
            _store_shard_on_last_step(dw_ref, dw_hbm, dw_sem, j)
            dq2 = _dot_nt(df2, w_ref[j])
            dp_ref[:, j * cs:(j + 1) * cs] = (dq2 * jnp.sqrt(qb.astype(F32))).astype(BF16)
        _wait_stores_on_last_step((dw_ref, dw_hbm, dw_sem))

    dw_shape = (N_CHIPS, 2, cs // 2, D_MODEL)
    return pl.pallas_call(
        body, name="bwd_fc2", grid=(s_len // ts,),
        in_specs=[_rows(ts, D_MODEL), _rows(ts, D_FF), _VMEM],
        out_specs=[_rows(ts, D_FF), _ANY],
        out_shape=[jax.ShapeDtypeStruct((s_len, D_FF), BF16), jax.ShapeDtypeStruct(dw_shape, F32)],
        scratch_shapes=[pltpu.VMEM(dw_shape, F32), pltpu.SemaphoreType.DMA((N_CHIPS,))],
        compiler_params=_params(),
    )(df, q, fc2_g)


def _bwd_fc1_out(dp, dy, x1, mix, h2, ycat, mod6, n2pre, n1post, fc1_g, wout_g, dep, ts):
    s_len = dp.shape[0]
    cs = D_FF // N_CHIPS
    rs = D_MODEL // N_CHIPS

    def body(dp_ref, dy_ref, x1_ref, mix_ref, h2_ref, yc_ref, mod_ref, g2_ref, g1_ref, w1_ref, wo_ref, dep_ref,
             dx1_ref, dyc_ref, dshift2_ref, da2_ref, s1_ref, dw1_hbm, dwo_hbm, dw1_ref, dwo_ref, dw1_sem, dwo_sem):
        _zero_on_first_step(dshift2_ref, da2_ref, s1_ref, dw1_ref, dwo_ref)
        h2b = h2_ref[...]
        dh2 = None
        for j in range(N_CHIPS):
            dpb = dp_ref[:, j * cs:(j + 1) * cs]
            dw1_ref[j] += _dot_tn(h2b, dpb).reshape(2, D_MODEL // 2, cs)
            _store_shard_on_last_step(dw1_ref, dw1_hbm, dw1_sem, j)
            part = _dot_nt(dpb, w1_ref[j])
            dh2 = part if dh2 is None else dh2 + part
        x1 = x1_ref[...]
        r3 = lax.rsqrt(_rowmean(x1 * x1) + EPS)
        xh = x1 * r3
        a2 = g2_ref[...] * (1.0 + mod_ref[4:5, :])
        dshift2_ref[...] += _colsum(dh2)
        da2_ref[...] += _colsum(dh2 * xh)
        dxh = dh2 * a2
        dx1 = dy_ref[...] + r3 * (dxh - xh * _rowmean(dxh * xh))
        dx1_ref[...] = dx1

        mix = mix_ref[...]
        r2 = lax.rsqrt(_rowmean(mix * mix) + EPS)
        mh = mix * r2
        s1_ref[...] += _colsum(dx1 * mh)
        gh = dx1 * (mod_ref[2:3, :] * g1_ref[...])
        dmix = (r2 * (gh - mh * _rowmean(gh * mh))).astype(BF16)
        dwo_ref[...] += _dot_tn(yc_ref[...], dmix).reshape(N_CHIPS, 2, rs // 2, D_MODEL)
        for j in range(N_CHIPS):
            _store_shard_on_last_step(dwo_ref, dwo_hbm, dwo_sem, j)
            dyc_ref[:, j * rs:(j + 1) * rs] = _dot_nt(dmix, wo_ref[j])
        _wait_stores_on_last_step((dw1_ref, dw1_hbm, dw1_sem), (dwo_ref, dwo_hbm, dwo_sem))

    vec = jax.ShapeDtypeStruct((1, D_MODEL), F32)
    dw1_shape = (N_CHIPS, 2, D_MODEL // 2, cs)
    dwo_shape = (N_CHIPS, 2, rs // 2, D_MODEL)
    return pl.pallas_call(
        body, name="bwd_fc1_out", grid=(s_len // ts,),
        in_specs=[_rows(ts, D_FF), _rows(ts, D_MODEL), _rows(ts, D_MODEL), _rows(ts, D_MODEL), _rows(ts, D_MODEL),
                  _rows(ts, D_MODEL), _const((N_MOD, D_MODEL)), _const((1, D_MODEL)), _const((1, D_MODEL)), _VMEM,
                  _VMEM, _ANY],
        out_specs=[_rows(ts, D_MODEL), _rows(ts, D_MODEL)] + [_const((1, D_MODEL))] * 3 + [_ANY, _ANY],
        out_shape=[jax.ShapeDtypeStruct((s_len, D_MODEL), F32), jax.ShapeDtypeStruct((s_len, D_MODEL), F32),
                   vec, vec, vec, jax.ShapeDtypeStruct(dw1_shape, F32), jax.ShapeDtypeStruct(dwo_shape, F32)],
        scratch_shapes=[pltpu.VMEM(dw1_shape, F32), pltpu.VMEM(dwo_shape, F32), pltpu.SemaphoreType.DMA((N_CHIPS,)),
                        pltpu.SemaphoreType.DMA((N_CHIPS,))],
        compiler_params=_params(),
    )(dp, dy, x1, mix, h2, ycat, mod6, n2pre, n1post, fc1_g, wout_g, dep)


def _mixer_bwd(z, dyc, w_spatial, bsp_full, gain, bias, w_pool, b_pool, pool_scale, dep, ts):
    s_len = z.shape[0]
    nb = ts // HALO
    last = s_len // HALO - 1
    te = ts + HALO

    def body(z_ref, zprev_ref, znext_ref, dyc_ref, dynext_ref, ws_ref, bsp_ref, gain_ref, bias_ref, wp_ref, bp_ref,
             ps_ref, dep_ref, dz_ref, dws_ref, dbsp_ref, dgain_ref, dbias_ref, dwp_ref, dbp_ref, dps_ref, mixed_ref,
             dvn_ref):
        i = pl.program_id(0)

        @pl.when(i == 0)
        def _():
            for ref in (dws_ref, dbsp_ref, dgain_ref, dbias_ref, dwp_ref, dbp_ref, dps_ref):
                ref[...] = jnp.zeros_like(ref)

        wc = _tril_weights(ws_ref)
        gain = gain_ref[...]
        u, vhat, rstd, vn, dga = _mixer_forward_tile(z_ref[:, :2 * D_A], wc, bsp_ref, gain, bias_ref[...], mixed_ref)
        dya = dyc_ref[:, :D_A]
        du = dya * mixed_ref[...]
        dmixed = dya * u
        dmb = dmixed.astype(BF16)
        dm_sum = dmixed[0:CHUNK, :]
        for k in range(1, ts // CHUNK):
            dm_sum = dm_sum + dmixed[k * CHUNK:(k + 1) * CHUNK, :]
        r_idx = lax.broadcasted_iota(jnp.int32, (CHUNK, CHUNK), 0)
        s_idx = lax.broadcasted_iota(jnp.int32, (CHUNK, CHUNK), 1)
        causal = (s_idx <= r_idx).astype(F32)
        for h in range(N_HEADS):
            lanes = slice(h * HEAD_DIM, (h + 1) * HEAD_DIM)
            dbsp_ref[h] += jnp.sum(dm_sum[:, lanes], axis=1, keepdims=True)
            acc = None
            for k in range(ts // CHUNK):
                rows = slice(k * CHUNK, (k + 1) * CHUNK)
                t = _dot_nt(dmb[rows, lanes], vn[rows, lanes])
                acc = t if acc is None else acc + t
                dvn_ref[rows, lanes] = _dot_tn(wc[h], dmb[rows, lanes])
            dws_ref[h] += acc * causal
        dvn = dvn_ref[...]
        dgain_ref[...] += _colsum(dvn * vhat)
        dbias_ref[...] += _colsum(dvn)
        dvh = dvn * gain
        dv = rstd * (dvh - _rowmean(dvh) - vhat * _rowmean(dvh * vhat))
        dz_ref[:, :D_A] = (du * dga[:, :D_A]).astype(BF16)
        dz_ref[:, D_A:2 * D_A] = (dv * dga[:, D_A:]).astype(BF16)

        zb = z_ref[:, 2 * D_A:]
        prev = jnp.where(i == 0, 0.0, zprev_ref[...])
        zb_ext = jnp.concatenate([zb, znext_ref[...]], axis=0)
        sums = _causal_window_sums(jnp.concatenate([prev, zb_ext], axis=0))
        pos, inv_counts = _window_counts(i * ts, te)
        dyb_ext = jnp.concatenate([dyc_ref[:, D_A:], dynext_ref[...]], axis=0)
        dlin_ext = dyb_ext * ps_ref[...]
        dbp_ref[...] += _colsum(dlin_ext[:ts, :])
        scaled = []
        ddiffs = []
        lins = []
        for g in range(len(POOL_WINDOWS)):
            lanes = slice(g * GROUP_DIM, (g + 1) * GROUP_DIM)
            diff = (sums[g][HALO:, :] * inv_counts[g] - zb_ext[:, lanes]).astype(BF16)
            wpb = wp_ref[g].astype(BF16)
            dlb = dlin_ext[:, lanes].astype(BF16)
            lins.append(_dot(diff[:ts, :], wpb) + bp_ref[:, lanes])
            dwp_ref[g] += _dot_tn(diff[:ts, :], dlb[:ts, :])
            dd = _dot_nt(dlb, wpb)
            ddiffs.append(dd)
            scaled.append(jnp.where(pos < float(s_len), dd * inv_counts[g], 0.0))
        dps_ref[...] += _colsum(dyb_ext[:ts, :] * jnp.concatenate(lins, axis=1))
        back = _anticausal_window_sums(jnp.concatenate(scaled, axis=1))
        for g in range(len(POOL_WINDOWS)):
            dz_ref[:, 2 * D_A + g * GROUP_DIM:2 * D_A + (g + 1) * GROUP_DIM] = (
                back[g][:ts, :] - ddiffs[g][:ts, :]).astype(BF16)

    sq = jax.ShapeDtypeStruct((N_HEADS, CHUNK, CHUNK), F32)
    vec = jax.ShapeDtypeStruct((1, D_A), F32)
    return pl.pallas_call(
        body, name="mixer_bwd", grid=(s_len // ts,),
        in_specs=[_rows(ts, D_Z),
                  pl.BlockSpec((HALO, D_B), lambda i: (jnp.maximum(i * nb - 1, 0), 2)),
                  pl.BlockSpec((HALO, D_B), lambda i: (jnp.minimum((i + 1) * nb, last), 2)),
                  _rows(ts, D_MODEL),
                  pl.BlockSpec((HALO, D_B), lambda i: (jnp.minimum((i + 1) * nb, last), 1)),
                  _const((N_HEADS, CHUNK, CHUNK)), _const((CHUNK, D_A)), _const((1, D_A)), _const((1, D_A)),
                  _const((N_HEADS, GROUP_DIM, GROUP_DIM)), _const((1, D_B)), _const((1, D_B)), _ANY],
        out_specs=[_rows(ts, D_Z), _const((N_HEADS, CHUNK, CHUNK)), _const((N_HEADS, CHUNK, 1)), _const((1, D_A)),
                   _const((1, D_A)), _const((N_HEADS, GROUP_DIM, GROUP_DIM)), _const((1, D_B)), _const((1, D_B))],
        out_shape=[jax.ShapeDtypeStruct((s_len, D_Z), BF16), sq, jax.ShapeDtypeStruct((N_HEADS, CHUNK, 1), F32), vec,
                   vec, sq, vec, vec],
        scratch_shapes=[pltpu.VMEM((ts, D_A), F32), pltpu.VMEM((ts, D_A), F32)],
        compiler_params=_params(),
    )(z, z, z, dyc, dyc, w_spatial, bsp_full, gain, bias, w_pool, b_pool, pool_scale, dep)


def _bwd_in(dz, dx1, x, mod6, n1pre, win_g, dep, ts):
    s_len = x.shape[0]
    cs = D_Z // N_CHIPS

    def body(dz_ref, dx1_ref, x_ref, mod_ref, g_ref, w_ref, dep_ref, gx_ref, dshift_ref, da_ref, dw_hbm, dw_ref,
             wfull_ref, dw_sem):
        _zero_on_first_step(dshift_ref, da_ref, dw_ref)
        _join_w_in_on_first_step(w_ref, wfull_ref)
        xv = x_ref[...]
        r = lax.rsqrt(_rowmean(xv * xv) + EPS)
        xh = xv * r
        h1b = (xh * (g_ref[...] * (1.0 + mod_ref[1:2, :])) + mod_ref[0:1, :]).astype(BF16)
        dzb = dz_ref[...]
        dw = _dot_tn(h1b, dzb)
        for j in range(N_CHIPS):
            dw_ref[j] += dw[:, j * cs:(j + 1) * cs].reshape(2, D_MODEL // 2, cs)
        dh = _dot_nt(dzb, wfull_ref[...])
        a1 = g_ref[...] * (1.0 + mod_ref[1:2, :])
        dshift_ref[...] += _colsum(dh)
        da_ref[...] += _colsum(dh * xh)
        dxh = dh * a1
        gx_ref[...] = dx1_ref[...] + r * (dxh - xh * _rowmean(dxh * xh))
        for j in range(N_CHIPS):
            _store_shard_on_last_step(dw_ref, dw_hbm, dw_sem, j)
        _wait_stores_on_last_step((dw_ref, dw_hbm, dw_sem))

    vec = jax.ShapeDtypeStruct((1, D_MODEL), F32)
    dw_shape = (N_CHIPS, 2, D_MODEL // 2, cs)
    return pl.pallas_call(
        body, name="bwd_in", grid=(s_len // ts,),
        in_specs=[_rows(ts, D_Z), _rows(ts, D_MODEL), _rows(ts, D_MODEL), _const((N_MOD, D_MODEL)),
                  _const((1, D_MODEL)), _VMEM, _ANY],
        out_specs=[_rows(ts, D_MODEL), _const((1, D_MODEL)), _const((1, D_MODEL)), _ANY],
        out_shape=[jax.ShapeDtypeStruct((s_len, D_MODEL), F32), vec, vec, jax.ShapeDtypeStruct(dw_shape, F32)],
        scratch_shapes=[pltpu.VMEM(dw_shape, F32), pltpu.VMEM((D_MODEL, D_Z), BF16),
                        pltpu.SemaphoreType.DMA((N_CHIPS,))],
        compiler_params=_params(),
    )(dz, dx1, x, mod6, n1pre, win_g, dep)


def _adamw_math(w, g, m, v):
    m = ADAM_B1 * m + (1.0 - ADAM_B1) * g
    v = ADAM_B2 * v + (1.0 - ADAM_B2) * (g * g)
    m_hat = m / (1.0 - ADAM_B1 ** ADAM_STEP)
    v_hat = v / (1.0 - ADAM_B2 ** ADAM_STEP)
    delta = -ADAM_LR * (m_hat / (jnp.sqrt(v_hat) + ADAM_EPS) + ADAM_WD * w)
    return delta, m, v


def _adamw(gs, ws, ms, vs, name, steps):
    n = len(ws)

    def body(*refs):
        for i in range(n):
            g_ref, w_ref, m_ref, v_ref = refs[4 * i:4 * i + 4]
            d, nm, nv = _adamw_math(w_ref[...], g_ref[...], m_ref[...], v_ref[...])
            for ref, val in zip(refs[4 * n + 3 * i:4 * n + 3 * i + 3], (d, nm, nv)):
                ref[...] = val

    specs = [_rows(w.shape[0] // steps, w.shape[1]) for w in ws]
    out = pl.pallas_call(
        body, name=name, grid=(steps,),
        in_specs=[s for s in specs for _ in range(4)], out_specs=[s for s in specs for _ in range(3)],
        out_shape=[jax.ShapeDtypeStruct(w.shape, F32) for w in ws for _ in range(3)],
        compiler_params=_params(),
    )(*[a for quad in zip(gs, ws, ms, vs) for a in quad])
    return [out[3 * i:3 * i + 3] for i in range(n)]


def _ada_grad_adamw(sc_t, dmod_shard, w, m, v, tr):
    rows, cols = w.shape

    def body(s_ref, dm_ref, w_ref, m_ref, v_ref, g_ref, d_ref, nm_ref, nv_ref):
        g = s_ref[:, 0:1] * dm_ref[0:1, :]
        for b in range(1, N_DEV):
            g = g + s_ref[:, b:b + 1] * dm_ref[b:b + 1, :]
        g_ref[...] = g
        d, nm, nv = _adamw_math(w_ref[...], g, m_ref[...], v_ref[...])
        d_ref[...] = d
        nm_ref[...] = nm
        nv_ref[...] = nv

    spec = _rows(tr, cols)
    shape = jax.ShapeDtypeStruct((rows, cols), F32)
    return pl.pallas_call(
        body, name="ada_grad_adamw", grid=(rows // tr,),
        in_specs=[_rows(tr, N_DEV), _const((N_DEV, cols)), spec, spec, spec],
        out_specs=[spec] * 4, out_shape=[shape] * 4, compiler_params=_params(),
    )(sc_t, dmod_shard, w, m, v)


def _mod_grads(da1, dshift1, s1, da2, dshift2, s2, mod6, n1pre, n1post, n2pre, n2post):
    def body(da1_ref, ds1_ref, s1_ref, da2_ref, ds2_ref, s2_ref, mod_ref, n1_ref, p1_ref, n2_ref, p2_ref, dmod_ref,
             dn_ref):
        dmod_ref[0:1, :] = ds1_ref[...]
        dmod_ref[1:2, :] = da1_ref[...] * n1_ref[...]
        dmod_ref[2:3, :] = s1_ref[...] * p1_ref[...]
        dmod_ref[3:4, :] = ds2_ref[...]
        dmod_ref[4:5, :] = da2_ref[...] * n2_ref[...]
        dmod_ref[5:6, :] = s2_ref[...] * p2_ref[...]
        dn_ref[0:1, :] = da1_ref[...] * (1.0 + mod_ref[1:2, :])
        dn_ref[1:2, :] = s1_ref[...] * mod_ref[2:3, :]
        dn_ref[2:3, :] = da2_ref[...] * (1.0 + mod_ref[4:5, :])
        dn_ref[3:4, :] = s2_ref[...] * mod_ref[5:6, :]

    return pl.pallas_call(
        body, name="mod_grads",
        out_shape=[jax.ShapeDtypeStruct((N_MOD, D_MODEL), F32), jax.ShapeDtypeStruct((4, D_MODEL), F32)],
    )(da1, dshift1, s1, da2, dshift2, s2, mod6, n1pre, n1post, n2pre, n2post)


def _position():
    x, y, c = lax.axis_index("x"), lax.axis_index("y"), lax.axis_index("c")
    return x, y, c


def _flip(v, bit):
    return 1 - v if bit else v


def _peer(x, y, c, k):
    return (_flip(x, k & 4), _flip(y, k & 2), _flip(c, k & 1))


def _remote(src, dst, send_sem, recv_sem, device):
    return pltpu.make_async_remote_copy(src_ref=src, dst_ref=dst, send_sem=send_sem, recv_sem=recv_sem,
                                        device_id=device, device_id_type=MESH)


def _cast_to_slots(ws, pos, dep, name, steps):
    n = len(ws)

    def body(pos_ref, *refs):
        for w_ref, o_ref in zip(refs[:n], refs[n + 1:]):
            o_ref[0] = w_ref[...].astype(BF16)

    return pl.pallas_call(
        body, name=name,
        grid_spec=pltpu.PrefetchScalarGridSpec(
            num_scalar_prefetch=1, grid=(steps,),
            in_specs=[pl.BlockSpec((w.shape[0] // steps, w.shape[1]), lambda i, pos: (i, 0)) for w in ws] + [_ANY],
            out_specs=[pl.BlockSpec((1, w.shape[0] // steps, w.shape[1]), lambda i, pos: (pos[1], i, 0)) for w in ws]),
        out_shape=[jax.ShapeDtypeStruct((N_CHIPS,) + w.shape, BF16) for w in ws], compiler_params=_params(),
    )(pos, *ws, dep)


def _mod_exchange(c_row, w_ada_shard, b_ada_row):
    cs = w_ada_shard.shape[1]

    def body(c_ref, w_hbm, b_ref, mod_ref, sc_ref, rows_ref, w_ref, w_sem, send1, recv1, send2, recv2):
        x, y, c = _position()
        me = 4 * x + 2 * y + c
        chip = 2 * x + y
        w_load = pltpu.make_async_copy(w_hbm, w_ref, w_sem)
        w_load.start()
        cv = c_ref[...]
        sc_ref[me] = cv * jax.nn.sigmoid(cv)
        gather = [_remote(sc_ref.at[me], sc_ref.at[me], send1.at[k - 1], recv1.at[k - 1], _peer(x, y, c, k))
                  for k in range(1, N_DEV)]
        for cp in gather:
            cp.start()
        for k in range(1, N_DEV):
            px, py, pc = _peer(x, y, c, k)
            src = 4 * px + 2 * py + pc
            _remote(sc_ref.at[src], sc_ref.at[src], send1.at[k - 1], recv1.at[k - 1], (px, py, pc)).wait_recv()
        for cp in gather:
            cp.wait_send()
        sc_all = jnp.concatenate([sc_ref[b] for b in range(N_DEV)], axis=0)
        w_load.wait()
        part = _dot(sc_all.astype(BF16), w_ref[...].astype(BF16))
        part = part + b_ref[:, pl.ds(pl.multiple_of(chip * cs, LANES), cs)]
        for b in range(N_DEV):
            rows_ref[b] = part[b:b + 1, :]
        mod_ref[chip] = rows_ref[me]
        hand = []
        for k in (2, 4, 6):
            px, py, _ = _peer(x, y, c, k)
            hand.append(_remote(rows_ref.at[4 * px + 2 * py + c], mod_ref.at[chip], send2.at[k // 2 - 1],
                                recv2.at[k // 2 - 1], (px, py, c)))
        for cp in hand:
            cp.start()
        for k in (2, 4, 6):
            px, py, _ = _peer(x, y, c, k)
            pchip = 2 * px + py
            _remote(rows_ref.at[me], mod_ref.at[pchip], send2.at[k // 2 - 1], recv2.at[k // 2 - 1],
                    (px, py, c)).wait_recv()
        for cp in hand:
            cp.wait_send()

    return pl.pallas_call(
        body, name="mod_exchange",
        in_specs=[_VMEM, _ANY, _VMEM], out_specs=[_VMEM, _VMEM],
        out_shape=[jax.ShapeDtypeStruct((N_CHIPS, 1, cs), F32), jax.ShapeDtypeStruct((N_DEV, 1, D_MODEL), F32)],
        scratch_shapes=[pltpu.VMEM((N_DEV, 1, cs), F32), pltpu.VMEM(w_ada_shard.shape, F32), pltpu.SemaphoreType.DMA,
                        pltpu.SemaphoreType.DMA((N_DEV - 1,)),
                        pltpu.SemaphoreType.DMA((N_DEV - 1,)), pltpu.SemaphoreType.DMA((N_CHIPS - 1,)),
                        pltpu.SemaphoreType.DMA((N_CHIPS - 1,))],
        compiler_params=pltpu.CompilerParams(vmem_limit_bytes=VMEM_LIMIT),
    )(c_row, w_ada_shard, b_ada_row)


_HBM = pl.BlockSpec(memory_space=pltpu.HBM)
_SEM = pl.BlockSpec(memory_space=pltpu.SEMAPHORE)
_EFFECT = pltpu.SideEffectType.DATAFLOW_SIDE_EFFECTING
_CHIP_HOPS = (2, 4, 6)


def _in_hbm(a):
    return pltpu.with_memory_space_constraint(a, pltpu.HBM)


def _sems3():
    return pltpu.SemaphoreType.DMA((len(_CHIP_HOPS),))


def _ag_start(lands, after, name):
    n = len(lands)

    def body(*refs):
        zones = refs[:n]
        sends, recvs = refs[n + 1:2 * n + 1], refs[2 * n + 1:3 * n + 1]
        x, y, c = _position()
        chip = 2 * x + y
        for i in range(n):
            half = zones[i].shape[1] // 2
            mine = zones[i].at[chip, pl.ds(c * half, half)]
            for s, k in enumerate(_CHIP_HOPS):
                px, py, _ = _peer(x, y, c, k)
                _remote(mine, mine, sends[i].at[s], recvs[i].at[s], (px, py, c)).start()

    out = pl.pallas_call(
        body, name=name,
        in_specs=[_HBM] * n + [_ANY],
        out_specs=[_SEM] * (2 * n) + [_HBM] * n,
        out_shape=[_sems3()] * (2 * n) + [pltpu.HBM(z.shape, BF16) for z in lands],
        input_output_aliases={i: 2 * n + i for i in range(n)},
        compiler_params=pltpu.CompilerParams(has_side_effects=_EFFECT),
    )(*[_in_hbm(z) for z in lands], after)
    return [(out[2 * n + i], out[i], out[n + i]) for i in range(n)]


def _ag_pass(group, after, name):
    n = len(group)

    def body(*refs):
        zones = refs[:n]
        sends, recvs = refs[n:2 * n], refs[2 * n:3 * n]
        fsends, frecvs = refs[4 * n + 1:5 * n + 1], refs[5 * n + 1:6 * n + 1]
        x, y, c = _position()
        chip = 2 * x + y
        for i in range(n):
            half = zones[i].shape[1] // 2
            rows = pl.ds(c * half, half)
            for s, k in enumerate(_CHIP_HOPS):
                px, py, _ = _peer(x, y, c, k)
                landed = zones[i].at[2 * px + py, rows]
                _remote(landed, landed, sends[i].at[s], recvs[i].at[s], (px, py, c)).wait_recv()
                _remote(landed, landed, fsends[i].at[s], frecvs[i].at[s], (x, y, 1 - c)).start()
        for i in range(n):
            half = zones[i].shape[1] // 2
            mine = zones[i].at[chip, pl.ds(c * half, half)]
            for s, k in enumerate(_CHIP_HOPS):
                px, py, _ = _peer(x, y, c, k)
                _remote(mine, mine, sends[i].at[s], recvs[i].at[s], (px, py, c)).wait_send()

    out = pl.pallas_call(
        body, name=name,
        in_specs=[_HBM] * n + [_SEM] * (2 * n) + [_ANY],
        out_specs=[_HBM] * n + [_SEM] * (2 * n),
        out_shape=[pltpu.HBM(g[0].shape, BF16) for g in group] + [_sems3()] * (2 * n),
        input_output_aliases={i: i for i in range(n)},
        compiler_params=pltpu.CompilerParams(has_side_effects=_EFFECT),
    )(*[g[0] for g in group], *[g[1] for g in group], *[g[2] for g in group], after)
    return [(out[i], out[n + i], out[2 * n + i]) for i in range(n)]


def _ag_done(group, name):
    n = len(group)

    def body(*refs):
        lands = refs[:n]
        fsends, frecvs = refs[n:2 * n], refs[2 * n:3 * n]
        x, y, c = _position()
        for i in range(n):
            half = lands[i].shape[1] // 2
            for s, k in enumerate(_CHIP_HOPS):
                px, py, _ = _peer(x, y, c, k)
                sent = lands[i].at[2 * px + py, pl.ds(c * half, half)]
                got = lands[i].at[2 * px + py, pl.ds((1 - c) * half, half)]
                cp = _remote(sent, got, fsends[i].at[s], frecvs[i].at[s], (x, y, 1 - c))
                cp.wait_recv()
                cp.wait_send()

    out = pl.pallas_call(
        body, name=name,
        in_specs=[_HBM] * n + [_SEM] * (2 * n),
        out_specs=[_HBM] * n,
        out_shape=[pltpu.HBM(g[0].shape, BF16) for g in group],
        input_output_aliases={i: i for i in range(n)},
        compiler_params=pltpu.CompilerParams(has_side_effects=_EFFECT),
    )(*[g[0] for g in group], *[g[1] for g in group], *[g[2] for g in group])
    return list(out)


def _small_spread_start(slots, after):
    def body(z_ref, after_ref, sends, recvs, z_out):
        x, y, c = _position()
        mine = z_ref.at[2 * x + y]
        for s, k in enumerate(_CHIP_HOPS):
            px, py, _ = _peer(x, y, c, k)
            _remote(mine, mine, sends.at[s], recvs.at[s], (px, py, c)).start()

    sends, recvs, out = pl.pallas_call(
        body, name="small_spread_start",
        in_specs=[_HBM, _ANY], out_specs=[_SEM, _SEM, _HBM],
        out_shape=[_sems3(), _sems3(), pltpu.HBM(slots.shape, F32)],
        input_output_aliases={0: 2},
        compiler_params=pltpu.CompilerParams(has_side_effects=_EFFECT),
    )(_in_hbm(slots), after)
    return out, sends, recvs


def _small_spread_wait(slots, sends, recvs, afters):
    def body(z_ref, sends, recvs, *rest):
        x, y, c = _position()
        mine = z_ref.at[2 * x + y]
        for s, k in enumerate(_CHIP_HOPS):
            px, py, _ = _peer(x, y, c, k)
            cp = _remote(mine, z_ref.at[2 * px + py], sends.at[s], recvs.at[s], (px, py, c))
            cp.wait_recv()
            cp.wait_send()

    return pl.pallas_call(
        body, name="small_spread_wait",
        in_specs=[_HBM, _SEM, _SEM] + [_ANY] * len(afters), out_specs=_HBM, out_shape=pltpu.HBM(slots.shape, F32),
        input_output_aliases={0: 0},
        compiler_params=pltpu.CompilerParams(has_side_effects=_EFFECT),
    )(slots, sends, recvs, *afters)


def _sibling_sum(pgs, name, small=None):
    n = len(pgs)
    k = 0 if small is None else 1
    units = [(i, j) for i in range(n) for j in range(N_CHIPS)]

    def body(*refs):
        refs = list(refs)
        take = lambda count: [refs.pop(0) for _ in range(count)]
        ins, small_in = take(n), take(k)
        qbs, owns, slots_out = take(n), take(n), take(k)
        mine, other, stage, got = take(n), take(n), take(n), take(n)
        load_a, load_b, send, recv = take(4)
        x, y, c = _position()
        chip = 2 * x + y
        if k:
            sib_ref, pair_send, pair_recv = take(3)
            pair = _remote(small_in[0], sib_ref, pair_send, pair_recv, (x, y, 1 - c))
            pair.start()
        loads_a = [pltpu.make_async_copy(ins[i].at[j, 1 - c], other[i].at[j], load_a.at[u])
                   for u, (i, j) in enumerate(units)]
        loads_b = [pltpu.make_async_copy(ins[i].at[j, c], mine[i].at[j], load_b.at[u])
                   for u, (i, j) in enumerate(units)]
        for cp in loads_a + loads_b:
            cp.start()
        sent = []
        for u, (i, j) in enumerate(units):
            loads_a[u].wait()
            stage[i][j] = other[i][j].astype(BF16)
            cp = _remote(stage[i].at[j], got[i].at[j], send.at[u], recv.at[u], (x, y, 1 - c))
            cp.start()
            sent.append(cp)
        for u, (i, j) in enumerate(units):
            loads_b[u].wait()
            sent[u].wait_recv()
            q = mine[i][j] + got[i][j].astype(F32)
            mine[i][j] = q
            qbs[i][j] = q.astype(BF16)
        for i in range(n):
            owns[i][...] = mine[i][chip]
        if k:
            pair.wait()
            slots_out[0][chip] = small_in[0][...] + sib_ref[...]
        for cp in sent:
            cp.wait_send()

    wire = [(N_CHIPS,) + p.shape[2:] for p in pgs]
    extra_out, extra_scratch = [], []
    if k:
        extra_out = [jax.ShapeDtypeStruct((N_CHIPS,) + small.shape, F32)]
        extra_scratch = [pltpu.VMEM(small.shape, F32), pltpu.SemaphoreType.DMA, pltpu.SemaphoreType.DMA]
    out = pl.pallas_call(
        body, name=name, in_specs=[_ANY] * n + [_VMEM] * k, out_specs=[_VMEM] * (2 * n + k),
        out_shape=[jax.ShapeDtypeStruct(w, BF16) for w in wire] + [jax.ShapeDtypeStruct(w[1:], F32) for w in wire]
        + extra_out,
        scratch_shapes=[pltpu.VMEM(w, F32) for w in wire] * 2 + [pltpu.VMEM(w, BF16) for w in wire] * 2
        + [pltpu.SemaphoreType.DMA((len(units),))] * 4 + extra_scratch,
        compiler_params=pltpu.CompilerParams(vmem_limit_bytes=VMEM_LIMIT),
    )(*pgs, *([small] if k else []))
    return list(out[:n]), list(out[n:2 * n]), list(out[2 * n:])


def _rs_start(qbs, name, after=None):
    n = len(qbs)
    k = 0 if after is None else 1

    def body(*refs):
        outs, inboxes = refs[:n], refs[n:2 * n]
        sends, recvs = refs[2 * n + k:3 * n + k], refs[3 * n + k:4 * n + k]
        x, y, c = _position()
        chip = 2 * x + y
        for i in range(n):
            for s, hop in enumerate(_CHIP_HOPS):
                px, py, _ = _peer(x, y, c, hop)
                _remote(outs[i].at[2 * px + py], inboxes[i].at[chip], sends[i].at[s], recvs[i].at[s], (px, py, c)).start()

    inboxes = [_in_hbm(lax.empty(q.shape, BF16)) for q in qbs]
    out = pl.pallas_call(
        body, name=name,
        in_specs=[_HBM] * (2 * n) + [_ANY] * k,
        out_specs=[_SEM] * (2 * n) + [_HBM] * (2 * n),
        out_shape=[_sems3()] * (2 * n) + [pltpu.HBM(q.shape, BF16) for q in qbs] * 2,
        input_output_aliases={i: 2 * n + i for i in range(2 * n)},
        compiler_params=pltpu.CompilerParams(has_side_effects=_EFFECT),
    )(*[_in_hbm(q) for q in qbs], *inboxes, *([after] if k else []))
    return [(out[2 * n + i], out[3 * n + i], out[i], out[n + i]) for i in range(n)]


def _rs_wait(group, after, name):
    n = len(group)

    def body(*refs):
        outs, inboxes = refs[:n], refs[n:2 * n]
        sends, recvs = refs[2 * n:3 * n], refs[3 * n:4 * n]
        x, y, c = _position()
        for i in range(n):
            for s, k in enumerate(_CHIP_HOPS):
                px, py, _ = _peer(x, y, c, k)
                slot = 2 * px + py
                cp = _remote(outs[i].at[slot], inboxes[i].at[slot], sends[i].at[s], recvs[i].at[s], (px, py, c))
                cp.wait_recv()
                cp.wait_send()

    out = pl.pallas_call(
        body, name=name,
        in_specs=[_HBM] * (2 * n) + [_SEM] * (2 * n) + [_ANY],
        out_specs=[_HBM] * n,
        out_shape=[pltpu.HBM(g[1].shape, BF16) for g in group],
        input_output_aliases={n + i: i for i in range(n)},
        compiler_params=pltpu.CompilerParams(has_side_effects=_EFFECT),
    )(*[g[0] for g in group], *[g[1] for g in group], *[g[2] for g in group], *[g[3] for g in group], after)
    return list(out)


def _final_share(inboxes, owns, name):
    n = len(inboxes)
    units = [(i, s) for i in range(n) for s in range(len(_CHIP_HOPS))]

    def body(*refs):
        ins, mine, outs, landed, half = (refs[k * n:(k + 1) * n] for k in range(5))
        load, load_own, keep, send, recv = refs[5 * n:]
        x, y, c = _position()
        loads = []
        for u, (i, s) in enumerate(units):
            px, py, _ = _peer(x, y, c, _CHIP_HOPS[s])
            loads.append(pltpu.make_async_copy(ins[i].at[2 * px + py], landed[i].at[s], load.at[u]))
        loads_own = [pltpu.make_async_copy(mine[i], half[i], load_own.at[i]) for i in range(n)]
        for cp in loads + loads_own:
            cp.start()
        copies = []
        for i in range(n):
            for s in range(len(_CHIP_HOPS)):
                loads[len(_CHIP_HOPS) * i + s].wait()
            loads_own[i].wait()
            total = (landed[i][0].astype(F32) + landed[i][1].astype(F32)) + landed[i][2].astype(F32)
            half[i][...] = total + half[i][...]
            copies.append(pltpu.make_async_copy(half[i], outs[i].at[c], keep.at[i]))
            copies.append(_remote(half[i], outs[i].at[c], send.at[i], recv.at[i], (x, y, 1 - c)))
            for cp in copies[-2:]:
                cp.start()
        for i in range(n):
            theirs = outs[i].at[1 - c]
            _remote(theirs, theirs, send.at[i], recv.at[i], (x, y, 1 - c)).wait_recv()
        for i in range(n):
            copies[2 * i].wait()
            copies[2 * i + 1].wait_send()

    return pl.pallas_call(
        body, name=name, in_specs=[_ANY] * (2 * n), out_specs=[_ANY] * n,
        out_shape=[jax.ShapeDtypeStruct((2,) + o.shape, F32) for o in owns],
        scratch_shapes=[pltpu.VMEM((len(_CHIP_HOPS),) + o.shape, BF16) for o in owns]
        + [pltpu.VMEM(o.shape, F32) for o in owns]
        + [pltpu.SemaphoreType.DMA((len(units),))] + [pltpu.SemaphoreType.DMA((n,))] * 4,
        compiler_params=pltpu.CompilerParams(vmem_limit_bytes=VMEM_LIMIT),
    )(*inboxes, *owns)


_SMALL = (("b_ada", N_MOD * D_MODEL), ("norm1_pre", D_MODEL), ("norm1_post", D_MODEL), ("norm2_pre", D_MODEL),
          ("norm2_post", D_MODEL), ("w_spatial", N_HEADS * CHUNK * CHUNK), ("b_spatial", N_HEADS * CHUNK),
          ("ln_v_gain", D_A), ("ln_v_bias", D_A), ("w_pool", N_HEADS * GROUP_DIM * GROUP_DIM),
          ("b_pool", D_B), ("pool_scale", D_B))
_MOD_ROWS = N_MOD * D_MODEL // LANES


def _packed_rows(size):
    return -(-(size // LANES) // SUBLANES) * SUBLANES


def _pack(parts):
    out = []
    for name, size in _SMALL:
        a = parts[name].reshape(size // LANES, LANES)
        pad = _packed_rows(size) - a.shape[0]
        out.append(jnp.pad(a, ((0, pad), (0, 0))) if pad else a)
    return out


def _small_adamw(slots, ws, ms, vs):
    n = len(_SMALL)
    head = N_DEV * _MOD_ROWS

    def body(*refs):
        s_ref, w, m, v = refs[0], refs[1:1 + n], refs[1 + n:1 + 2 * n], refs[1 + 2 * n:1 + 3 * n]
        outs = refs[1 + 3 * n:1 + 7 * n]
        dmod_ref, loss_ref, t_ref = refs[1 + 7 * n:]
        t_ref[...] = ((s_ref[0] + s_ref[1]) + s_ref[2]) + s_ref[3]
        dmod_ref[...] = t_ref[0:head, :]
        loss_ref[...] = t_ref[t_ref.shape[0] - 8:, :]
        row = head
        for i, (_, size) in enumerate(_SMALL):
            if i == 0:
                g = t_ref[0:_MOD_ROWS, :]
                for b in range(1, N_DEV):
                    g = g + t_ref[b * _MOD_ROWS:(b + 1) * _MOD_ROWS, :]
            else:
                g = t_ref[row:row + size // LANES, :]
                row += _packed_rows(size)
            d, nm, nv = _adamw_math(w[i][...], g, m[i][...], v[i][...])
            for ref, val in zip(outs[4 * i:4 * i + 4], (g, d, nm, nv)):
                ref[...] = val

    each = [jax.ShapeDtypeStruct((size // LANES, LANES), F32) for _, size in _SMALL for _ in range(4)]
    out = pl.pallas_call(
        body, name="small_adamw",
        out_shape=each + [jax.ShapeDtypeStruct((head, LANES), F32), jax.ShapeDtypeStruct((SUBLANES, LANES), F32)],
        scratch_shapes=[pltpu.VMEM(slots.shape[1:], F32)],
        compiler_params=pltpu.CompilerParams(vmem_limit_bytes=VMEM_LIMIT),
    )(slots, *ws, *ms, *vs)
    return [out[4 * i:4 * i + 4] for i in range(n)], out[4 * n], out[4 * n + 1]


def kernel(x, c, w_ada, b_ada, norm1_pre, norm1_post, w_in, w_spatial, b_spatial, ln_v_gain, ln_v_bias, w_pool, b_pool, pool_scale, w_out, norm2_pre, norm2_post, w_fc1, w_fc2, loss_target, m_w_ada, m_b_ada, m_norm1_pre, m_norm1_post, m_w_in, m_w_spatial, m_b_spatial, m_ln_v_gain, m_ln_v_bias, m_w_pool, m_b_pool, m_pool_scale, m_w_out, m_norm2_pre, m_norm2_post, m_w_fc1, m_w_fc2, v_w_ada, v_b_ada, v_norm1_pre, v_norm1_post, v_w_in, v_w_spatial, v_b_spatial, v_ln_v_gain, v_ln_v_bias, v_w_pool, v_b_pool, v_pool_scale, v_w_out, v_norm2_pre, v_norm2_post, v_w_fc1, v_w_fc2):
    weights = dict(w_ada=w_ada, b_ada=b_ada, norm1_pre=norm1_pre, norm1_post=norm1_post, w_in=w_in,
                   w_spatial=w_spatial, b_spatial=b_spatial, ln_v_gain=ln_v_gain, ln_v_bias=ln_v_bias, w_pool=w_pool,
                   b_pool=b_pool, pool_scale=pool_scale, w_out=w_out, norm2_pre=norm2_pre, norm2_post=norm2_post,
                   w_fc1=w_fc1, w_fc2=w_fc2)
    m_old = dict(w_ada=m_w_ada, b_ada=m_b_ada, norm1_pre=m_norm1_pre, norm1_post=m_norm1_post, w_in=m_w_in,
                 w_spatial=m_w_spatial, b_spatial=m_b_spatial, ln_v_gain=m_ln_v_gain, ln_v_bias=m_ln_v_bias,
                 w_pool=m_w_pool, b_pool=m_b_pool, pool_scale=m_pool_scale, w_out=m_w_out, norm2_pre=m_norm2_pre,
                 norm2_post=m_norm2_post, w_fc1=m_w_fc1, w_fc2=m_w_fc2)
    v_old = dict(w_ada=v_w_ada, b_ada=v_b_ada, norm1_pre=v_norm1_pre, norm1_post=v_norm1_post, w_in=v_w_in,
                 w_spatial=v_w_spatial, b_spatial=v_b_spatial, ln_v_gain=v_ln_v_gain, ln_v_bias=v_ln_v_bias,
                 w_pool=v_w_pool, b_pool=v_b_pool, pool_scale=v_pool_scale, w_out=v_w_out, norm2_pre=v_norm2_pre,
                 norm2_post=v_norm2_post, w_fc1=v_w_fc1, w_fc2=v_w_fc2)
    order = ("w_ada", "b_ada", "norm1_pre", "norm1_post", "w_in", "w_spatial", "b_spatial", "ln_v_gain", "ln_v_bias",
             "w_pool", "b_pool", "pool_scale", "w_out", "norm2_pre", "norm2_post", "w_fc1", "w_fc2")
    mx, my, mc = _position()
    me = 4 * mx + 2 * my + mc
    chip = 2 * mx + my
    row = lambda a: a.reshape(1, -1)

    pos = jnp.stack([mc, chip]).astype(jnp.int32)
    xs, target = x[0], loss_target[0]
    n1pre, n1post, n2pre, n2post = row(norm1_pre), row(norm1_post), row(norm2_pre), row(norm2_post)
    mixer = (w_spatial, jnp.repeat(b_spatial.T, HEAD_DIM, axis=1), row(ln_v_gain), row(ln_v_bias), w_pool,
             row(b_pool), row(pool_scale))
    ts_big, ts_mid = 512, 256

    mod4, sc_all = _mod_exchange(c, w_ada, row(b_ada))
    mod6 = mod4.reshape(N_MOD, D_MODEL)
    ag = _ag_start(_cast_to_slots([w_in, w_out], pos, mod4, "cast_mix", 1), mod4, "ag_start_mix")
    ag += _ag_start(_cast_to_slots([w_fc1, w_fc2], pos, ag[0][0], "cast_mlp", 4), ag[0][0], "ag_start_mlp")

    win_g, wout_g = _ag_done(_ag_pass([ag[0], ag[1]], ag[2][0], "ag_pass_mix"), "ag_done_mix")
    z, ycat, mix, x1, h2 = _fwd_mix(xs, mod6, n1pre, n1post, n2pre, win_g, wout_g, *mixer, ts_big)
    (fc1_g,) = _ag_done(_ag_pass([ag[2]], h2, "ag_pass_fc1"), "ag_done_fc1")
    q = _fwd_fc1(h2, fc1_g, ts_big)
    (fc2_g,) = _ag_done(_ag_pass([ag[3]], q, "ag_pass_fc2"), "ag_done_fc2")
    dy, df, loss, s2 = _fwd_fc2_loss(q, x1, target, mod6, n2post, fc2_g, ts_big)

    def reduce_start(partials, tag, small=None):
        wire, owns, slots = _sibling_sum(partials, "sibling_sum_" + tag, small)
        return _rs_start(wire, "rs_start_" + tag), owns, slots

    def reduce_finish(state, owns, names, tag, dep):
        inboxes = _rs_wait(state, dep, "rs_wait_" + tag)
        shards = _final_share(inboxes, owns, "final_share_" + tag)
        for n, g in zip(names, shards):
            grads[n] = g.reshape(weights[n].shape)
        updates = _adamw([grads[n] for n in names], [weights[n] for n in names], [m_old[n] for n in names],
                         [v_old[n] for n in names], "adamw_" + tag, 4)
        for n, (d, nm, nv) in zip(names, updates):
            deltas[n], new_m[n], new_v[n] = d, nm, nv

    grads, deltas, new_m, new_v = {}, {}, {}, {}
    dp, g_fc2 = _bwd_fc2(df, q, fc2_g, ts_big)
    state_fc2, owns_fc2, _ = reduce_start([g_fc2], "fc2")
    dx1, dyc, dshift2, da2, s1, g_fc1, g_out = _bwd_fc1_out(
        dp, dy, x1, mix, h2, ycat, mod6, n2pre, n1post, fc1_g, wout_g, state_fc2[0][0], ts_mid)
    state_mid, owns_mid, _ = reduce_start([g_fc1, g_out], "mid")
    dz, dws, dbsp, dgain, dbias, dwp, dbp, dps = _mixer_bwd(z, dyc, *mixer, state_mid[0][0], ts_big)
    grad_x, dshift1, da1, g_in = _bwd_in(dz, dx1, xs, mod6, n1pre, win_g, state_mid[0][0], ts_big)
    dmod6, dnorms = _mod_grads(da1, dshift1, s1, da2, dshift2, s2, mod6, n1pre, n1post, n2pre, n2post)

    parts = dict(b_ada=dmod6, norm1_pre=dnorms[0], norm1_post=dnorms[1], norm2_pre=dnorms[2], norm2_post=dnorms[3],
                 w_spatial=dws, b_spatial=dbsp, ln_v_gain=dgain, ln_v_bias=dbias, w_pool=dwp, b_pool=dbp,
                 pool_scale=dps)
    pieces = _pack(parts)
    slots = lax.dynamic_update_slice(jnp.zeros((N_DEV * _MOD_ROWS, LANES), F32), pieces[0], (me * _MOD_ROWS, 0))
    loss_tile = jnp.pad(loss, ((0, SUBLANES - 1), (0, LANES - 1)))
    wire_in, owns_in, pair_sum = _sibling_sum([g_in], "sibling_sum_in",
                                              jnp.concatenate([slots] + pieces[1:] + [loss_tile], axis=0))
    spread = _small_spread_start(pair_sum[0], wire_in[0])
    state_in = _rs_start(wire_in, "rs_start_in", spread[0])
    reduce_finish(state_fc2 + state_mid, owns_fc2 + owns_mid, ("w_fc2", "w_fc1", "w_out"), "mlp", state_in[0][0])
    flat = lambda d: [d[n].reshape(size // LANES, LANES) for n, size in _SMALL]
    small_out, dmod_all, loss_tile = _small_adamw(
        _small_spread_wait(*spread, [deltas[n] for n in ("w_fc2", "w_fc1", "w_out")]), flat(weights), flat(m_old),
        flat(v_old))
    loss = loss_tile[0, 0]
    for (n, _), (g, d, nm, nv) in zip(_SMALL, small_out):
        shape = weights[n].shape
        grads[n], deltas[n], new_m[n], new_v[n] = g.reshape(shape), d.reshape(shape), nm.reshape(shape), nv.reshape(shape)

    dmod_all = dmod_all.reshape(N_DEV, N_MOD * D_MODEL)
    cs = w_ada.shape[1]
    dmod_shard = lax.dynamic_slice(dmod_all, (0, chip * cs), (N_DEV, cs))
    sc_t = sc_all.reshape(N_DEV, D_MODEL).T
    grads["w_ada"], deltas["w_ada"], new_m["w_ada"], new_v["w_ada"] = _ada_grad_adamw(
        sc_t, dmod_shard, w_ada, m_w_ada, v_w_ada, 256)

    reduce_finish(state_in, owns_in, ("w_in",), "in", deltas["w_ada"])

    return (loss, grad_x[None], *[grads[n] for n in order], *[deltas[n] for n in order],
            *[new_m[n] for n in order], *[new_v[n] for n in order])
```

```python
import jax
import jax.numpy as jnp
from jax import lax
from jax.experimental import pallas as pl
from jax.experimental.pallas import tpu as pltpu

F32 = jnp.float32
BF16 = jnp.bfloat16
MESH = pl.DeviceIdType.MESH

D_MODEL = 1024
D_A = 512
D_B = 512
D_Z = 2 * D_A + D_B
N_HEADS = 4
HEAD_DIM = 128
CHUNK = 128
POOL_WINDOWS = (2, 4, 8, 16)
GROUP_DIM = 128
D_FF = 4096
N_MOD = 6
EPS = 1e-6
HALO = 16
N_CHIPS = 4
N_DEV = 8

ADAM_LR = 0.001
ADAM_B1 = 0.9
ADAM_B2 = 0.999
ADAM_EPS = 1e-08
ADAM_WD = 0.01
ADAM_STEP = 10

VMEM_LIMIT = 56 * 1024 * 1024
LANES = 128
SUBLANES = 8

_VMEM = pl.BlockSpec(memory_space=pltpu.VMEM)
_ANY = pl.BlockSpec(memory_space=pl.ANY)


def _params(n_grid_axes=1):
    return pltpu.CompilerParams(dimension_semantics=("arbitrary",) * n_grid_axes, vmem_limit_bytes=VMEM_LIMIT)


def _rows(ts, width):
    return pl.BlockSpec((ts, width), lambda i: (i, 0))


def _const(shape):
    return pl.BlockSpec(shape, lambda i: (0,) * len(shape))


def _dot(a, b):
    return jnp.dot(a, b, preferred_element_type=F32)


def _dot_nt(a, b):
    return lax.dot_general(a, b, (((1,), (1,)), ((), ())), preferred_element_type=F32)


def _dot_tn(a, b):
    return lax.dot_general(a, b, (((0,), (0,)), ((), ())), preferred_element_type=F32)


def _rowmean(v):
    return jnp.mean(v, axis=-1, keepdims=True)


def _colsum(v):
    return jnp.sum(v, axis=0, keepdims=True)


def _gelu_parts(z):
    k0 = 0.7978845608028654
    k1 = 0.044715
    z2 = z * z
    t = jnp.tanh(z * (k0 + (k0 * k1) * z2))
    u = 0.5 * t + 0.5
    g = z * u
    dg = u + (0.5 * z) * (1.0 - t * t) * (k0 + (3.0 * k0 * k1) * z2)
    return g, dg


def _tril_weights(ws_ref):
    r = lax.broadcasted_iota(jnp.int32, (CHUNK, CHUNK), 0)
    s = lax.broadcasted_iota(jnp.int32, (CHUNK, CHUNK), 1)
    mask = (s <= r).astype(F32)
    return [(ws_ref[h] * mask).astype(BF16) for h in range(N_HEADS)]


def _window_counts(first_row, n_rows):
    pos = (first_row + lax.broadcasted_iota(jnp.int32, (n_rows, 1), 0)).astype(F32)
    return pos, [1.0 / jnp.minimum(pos + 1.0, float(w)) for w in POOL_WINDOWS]


def _causal_window_sums(ext):
    out = []
    e = ext
    shift = 1
    for g in range(len(POOL_WINDOWS)):
        e = e + pltpu.roll(e, shift, 0)
        shift *= 2
        out.append(e[:, g * GROUP_DIM:(g + 1) * GROUP_DIM])
    return out


def _anticausal_window_sums(ext):
    n = ext.shape[0]
    out = []
    e = ext
    shift = 1
    for g in range(len(POOL_WINDOWS)):
        e = e + pltpu.roll(e, n - shift, 0)
        shift *= 2
        out.append(e[:, g * GROUP_DIM:(g + 1) * GROUP_DIM])
    return out


def _fwd_mix(x, mod6, n1pre, n1post, n2pre, win_g, wout_g, w_spatial, bsp_full, gain, bias, w_pool, b_pool, pool_scale, ts):
    s_len = x.shape[0]
    rs = D_MODEL // N_CHIPS

    def body(x_ref, mod_ref, g1pre_ref, g1post_ref, g2pre_ref, win_ref, wout_ref, ws_ref, bsp_ref, gain_ref,
             bias_ref, wp_ref, bp_ref, ps_ref, z_ref, y_ref, mix_ref, x1_ref, h2_ref, mixed_ref, prev_ref, wfull_ref):
        i = pl.program_id(0)
        _zero_on_first_step(prev_ref)
        _join_w_in_on_first_step(win_ref, wfull_ref)
        xv = x_ref[...]
        r = lax.rsqrt(_rowmean(xv * xv) + EPS)
        hb = ((xv * r) * (g1pre_ref[...] * (1.0 + mod_ref[1:2, :])) + mod_ref[0:1, :]).astype(BF16)
        z_ref[...] = _dot(hb, wfull_ref[...])

        wc = _tril_weights(ws_ref)
        u, _, _, _, _ = _mixer_forward_tile(z_ref[:, :2 * D_A], wc, bsp_ref, gain_ref[...], bias_ref[...], mixed_ref)
        y_ref[:, :D_A] = (u * mixed_ref[...]).astype(BF16)
        zb = z_ref[:, 2 * D_A:]
        sums = _causal_window_sums(jnp.concatenate([prev_ref[...], zb], axis=0))
        prev_ref[...] = zb[ts - HALO:, :]
        _, inv_counts = _window_counts(i * ts, ts)
        for g in range(len(POOL_WINDOWS)):
            lanes = slice(g * GROUP_DIM, (g + 1) * GROUP_DIM)
            diff = sums[g][HALO:, :] * inv_counts[g] - zb[:, lanes]
            lin = _dot(diff.astype(BF16), wp_ref[g].astype(BF16)) + bp_ref[:, lanes]
            y_ref[:, D_A + g * GROUP_DIM:D_A + (g + 1) * GROUP_DIM] = (lin * ps_ref[:, lanes]).astype(BF16)

        mix = None
        for j in range(N_CHIPS):
            part = _dot(y_ref[:, j * rs:(j + 1) * rs], wout_ref[j])
            mix = part if mix is None else mix + part
        mix_ref[...] = mix
        r2 = lax.rsqrt(_rowmean(mix * mix) + EPS)
        x1 = xv + (mix * r2) * (mod_ref[2:3, :] * g1post_ref[...])
        x1_ref[...] = x1
        r3 = lax.rsqrt(_rowmean(x1 * x1) + EPS)
        h2_ref[...] = ((x1 * r3) * (g2pre_ref[...] * (1.0 + mod_ref[4:5, :])) + mod_ref[3:4, :]).astype(BF16)

    vec = _const((1, D_MODEL))
    f32_rows = jax.ShapeDtypeStruct((s_len, D_MODEL), F32)
    bf16_rows = jax.ShapeDtypeStruct((s_len, D_MODEL), BF16)
    return pl.pallas_call(
        body, name="fwd_mix", grid=(s_len // ts,),
        in_specs=[_rows(ts, D_MODEL), _const((N_MOD, D_MODEL)), vec, vec, vec, _VMEM, _VMEM,
                  _const((N_HEADS, CHUNK, CHUNK)), _const((CHUNK, D_A)), _const((1, D_A)), _const((1, D_A)),
                  _const((N_HEADS, GROUP_DIM, GROUP_DIM)), _const((1, D_B)), _const((1, D_B))],
        out_specs=[_rows(ts, D_Z), _rows(ts, D_MODEL), _rows(ts, D_MODEL), _rows(ts, D_MODEL), _rows(ts, D_MODEL)],
        out_shape=[jax.ShapeDtypeStruct((s_len, D_Z), F32), bf16_rows, f32_rows, f32_rows, bf16_rows],
        scratch_shapes=[pltpu.VMEM((ts, D_A), F32), pltpu.VMEM((HALO, D_B), F32), pltpu.VMEM((D_MODEL, D_Z), BF16)],
        compiler_params=_params(),
    )(x, mod6, n1pre, n1post, n2pre, win_g, wout_g, w_spatial, bsp_full, gain, bias, w_pool, b_pool, pool_scale)


def _mixer_forward_tile(za, wc, bsp_ref, gain, bias, mixed_ref):
    ga, dga = _gelu_parts(za)
    u = ga[:, :D_A]
    v = ga[:, D_A:]
    mu = _rowmean(v)
    vc = v - mu
    rstd = lax.rsqrt(_rowmean(vc * vc) + EPS)
    vhat = vc * rstd
    vn = (vhat * gain + bias).astype(BF16)
    ts = za.shape[0]
    for k in range(ts // CHUNK):
        for h in range(N_HEADS):
            blk = vn[k * CHUNK:(k + 1) * CHUNK, h * HEAD_DIM:(h + 1) * HEAD_DIM]
            mixed_ref[k * CHUNK:(k + 1) * CHUNK, h * HEAD_DIM:(h + 1) * HEAD_DIM] = (
                _dot(wc[h], blk) + bsp_ref[:, h * HEAD_DIM:(h + 1) * HEAD_DIM])
    return u, vhat, rstd, vn, dga


def _fwd_fc1(h2, fc1_g, ts):
    s_len = h2.shape[0]
    cs = D_FF // N_CHIPS

    def body(h_ref, w_ref, q_ref):
        hb = h_ref[...]
        for j in range(N_CHIPS):
            p = jnp.maximum(_dot(hb, w_ref[j]), 0.0)
            q_ref[:, j * cs:(j + 1) * cs] = (p * p).astype(BF16)

    return pl.pallas_call(
        body, name="fwd_fc1", grid=(s_len // ts,),
        in_specs=[_rows(ts, D_MODEL), _VMEM],
        out_specs=_rows(ts, D_FF),
        out_shape=jax.ShapeDtypeStruct((s_len, D_FF), BF16),
        compiler_params=_params(),
    )(h2, fc1_g)


def _fwd_fc2_loss(q, x1, target, mod6, n2post, fc2_g, ts):
    s_len = q.shape[0]
    rs = D_FF // N_CHIPS

    def body(q_ref, x1_ref, t_ref, mod_ref, g_ref, w_ref, dy_ref, df_ref, loss_ref, s_ref):
        _zero_on_first_step(loss_ref, s_ref)
        gate_gain = mod_ref[5:6, :] * g_ref[...]
        f = _dot(q_ref[:, 0:rs], w_ref[0])
        for j in range(1, N_CHIPS):
            f = f + _dot(q_ref[:, j * rs:(j + 1) * rs], w_ref[j])
        r4 = lax.rsqrt(_rowmean(f * f) + EPS)
        fh = f * r4
        err = (x1_ref[...] + fh * gate_gain) - t_ref[...]
        loss_ref[...] += 0.5 * jnp.sum(_rowmean(err * err), axis=0, keepdims=True)
        dy = err * (1.0 / D_MODEL)
        dy_ref[...] = dy
        s_ref[...] += _colsum(dy * fh)
        gh = dy * gate_gain
        df_ref[...] = (r4 * (gh - fh * _rowmean(gh * fh))).astype(BF16)

    return pl.pallas_call(
        body, name="fwd_fc2_loss", grid=(s_len // ts,),
        in_specs=[_rows(ts, D_FF), _rows(ts, D_MODEL), _rows(ts, D_MODEL), _const((N_MOD, D_MODEL)),
                  _const((1, D_MODEL)), _VMEM],
        out_specs=[_rows(ts, D_MODEL), _rows(ts, D_MODEL), _const((1, 1)), _const((1, D_MODEL))],
        out_shape=[jax.ShapeDtypeStruct((s_len, D_MODEL), F32), jax.ShapeDtypeStruct((s_len, D_MODEL), BF16),
                   jax.ShapeDtypeStruct((1, 1), F32), jax.ShapeDtypeStruct((1, D_MODEL), F32)],
        compiler_params=_params(),
    )(q, x1, target, mod6, n2post, fc2_g)


def _join_w_in_on_first_step(win_ref, full_ref):
    cs = D_Z // N_CHIPS

    @pl.when(pl.program_id(0) == 0)
    def _():
        for j in range(N_CHIPS):
            full_ref[:, j * cs:(j + 1) * cs] = win_ref[j]


def _zero_on_first_step(*refs):
    @pl.when(pl.program_id(0) == 0)
    def _():
        for ref in refs:
            ref[...] = jnp.zeros_like(ref)


def _on_last_step(fn):
    pl.when(pl.program_id(0) == pl.num_programs(0) - 1)(fn)


def _store_shard_on_last_step(acc_ref, hbm_ref, sem, j):
    _on_last_step(lambda: pltpu.make_async_copy(acc_ref.at[j], hbm_ref.at[j], sem.at[j]).start())


def _wait_stores_on_last_step(*stores):
    def wait_all():
        for acc_ref, hbm_ref, sem in stores:
            for j in range(N_CHIPS):
                pltpu.make_async_copy(acc_ref.at[j], hbm_ref.at[j], sem.at[j]).wait()

    _on_last_step(wait_all)


def _bwd_fc2(df, q, fc2_g, ts):
    s_len = df.shape[0]
    cs = D_FF // N_CHIPS

    def body(df_ref, q_ref, w_ref, dp_ref, dw_hbm, dw_ref, dw_sem):
        _zero_on_first_step(dw_ref)
        dfb = df_ref[...]
        df2 = dfb * 2.0
        for j in range(N_CHIPS):
            qb = q_ref[:, j * cs:(j + 1) * cs]
            dw_ref[j] += _dot_tn(qb, dfb).reshape(2, cs // 2, D_MODEL)
            _store_shard_on_last_step(dw_ref, dw_hbm, dw_sem, j)
            dq2 = _dot_nt(df2, w_ref[j])
            dp_ref[:, j * cs:(j + 1) * cs] = (dq2 * jnp.sqrt(qb.astype(F32))).astype(BF16)
        _wait_stores_on_last_step((dw_ref, dw_hbm, dw_sem))

    dw_shape = (N_CHIPS, 2, cs // 2, D_MODEL)
    return pl.pallas_call(
        body, name="bwd_fc2", grid=(s_len // ts,),
        in_specs=[_rows(ts, D_MODEL), _rows(ts, D_FF), _VMEM],
        out_specs=[_rows(ts, D_FF), _ANY],
        out_shape=[jax.ShapeDtypeStruct((s_len, D_FF), BF16), jax.ShapeDtypeStruct(dw_shape, F32)],
        scratch_shapes=[pltpu.VMEM(dw_shape, F32), pltpu.SemaphoreType.DMA((N_CHIPS,))],
        compiler_params=_params(),
    )(df, q, fc2_g)


def _bwd_fc1_out(dp, dy, x1, mix, h2, ycat, mod6, n2pre, n1post, fc1_g, wout_g, dep, ts):
    s_len = dp.shape[0]
    cs = D_FF // N_CHIPS
    rs = D_MODEL // N_CHIPS

    def body(dp_ref, dy_ref, x1_ref, mix_ref, h2_ref, yc_ref, mod_ref, g2_ref, g1_ref, w1_ref, wo_ref, dep_ref,
             dx1_ref, dyc_ref, dshift2_ref, da2_ref, s1_ref, dw1_hbm, dwo_hbm, dw1_ref, dwo_ref, dw1_sem, dwo_sem):
        _zero_on_first_step(dshift2_ref, da2_ref, s1_ref, dw1_ref, dwo_ref)
        h2b = h2_ref[...]
        dh2 = None
        for j in range(N_CHIPS):
            dpb = dp_ref[:, j * cs:(j + 1) * cs]
            dw1_ref[j] += _dot_tn(h2b, dpb).reshape(2, D_MODEL // 2, cs)
            _store_shard_on_last_step(dw1_ref, dw1_hbm, dw1_sem, j)
            part = _dot_nt(dpb, w1_ref[j])
            dh2 = part if dh2 is None else dh2 + part
        x1 = x1_ref[...]
        r3 = lax.rsqrt(_rowmean(x1 * x1) + EPS)
        xh = x1 * r3
        a2 = g2_ref[...] * (1.0 + mod_ref[4:5, :])
        dshift2_ref[...] += _colsum(dh2)
        da2_ref[...] += _colsum(dh2 * xh)
        dxh = dh2 * a2
        dx1 = dy_ref[...] + r3 * (dxh - xh * _rowmean(dxh * xh))
        dx1_ref[...] = dx1

        mix = mix_ref[...]
        r2 = lax.rsqrt(_rowmean(mix * mix) + EPS)
        mh = mix * r2
        s1_ref[...] += _colsum(dx1 * mh)
        gh = dx1 * (mod_ref[2:3, :] * g1_ref[...])
        dmix = (r2 * (gh - mh * _rowmean(gh * mh))).astype(BF16)
        dwo_ref[...] += _dot_tn(yc_ref[...], dmix).reshape(N_CHIPS, 2, rs // 2, D_MODEL)
        for j in range(N_CHIPS):
            _store_shard_on_last_step(dwo_ref, dwo_hbm, dwo_sem, j)
            dyc_ref[:, j * rs:(j + 1) * rs] = _dot_nt(dmix, wo_ref[j])
        _wait_stores_on_last_step((dw1_ref, dw1_hbm, dw1_sem), (dwo_ref, dwo_hbm, dwo_sem))

    vec = jax.ShapeDtypeStruct((1, D_MODEL), F32)
    dw1_shape = (N_CHIPS, 2, D_MODEL // 2, cs)
    dwo_shape = (N_CHIPS, 2, rs // 2, D_MODEL)
    return pl.pallas_call(
        body, name="bwd_fc1_out", grid=(s_len // ts,),
        in_specs=[_rows(ts, D_FF), _rows(ts, D_MODEL), _rows(ts, D_MODEL), _rows(ts, D_MODEL), _rows(ts, D_MODEL),
                  _rows(ts, D_MODEL), _const((N_MOD, D_MODEL)), _const((1, D_MODEL)), _const((1, D_MODEL)), _VMEM,
                  _VMEM, _ANY],
        out_specs=[_rows(ts, D_MODEL), _rows(ts, D_MODEL)] + [_const((1, D_MODEL))] * 3 + [_ANY, _ANY],
        out_shape=[jax.ShapeDtypeStruct((s_len, D_MODEL), F32), jax.ShapeDtypeStruct((s_len, D_MODEL), F32),
                   vec, vec, vec, jax.ShapeDtypeStruct(dw1_shape, F32), jax.ShapeDtypeStruct(dwo_shape, F32)],
        scratch_shapes=[pltpu.VMEM(dw1_shape, F32), pltpu.VMEM(dwo_shape, F32), pltpu.SemaphoreType.DMA((N_CHIPS,)),
                        pltpu.SemaphoreType.DMA((N_CHIPS,))],
        compiler_params=_params(),
    )(dp, dy, x1, mix, h2, ycat, mod6, n2pre, n1post, fc1_g, wout_g, dep)


def _mixer_bwd(z, dyc, w_spatial, bsp_full, gain, bias, w_pool, b_pool, pool_scale, dep, ts):
    s_len = z.shape[0]
    nb = ts // HALO
    last = s_len // HALO - 1
    te = ts + HALO

    def body(z_ref, zprev_ref, znext_ref, dyc_ref, dynext_ref, ws_ref, bsp_ref, gain_ref, bias_ref, wp_ref, bp_ref,
             ps_ref, dep_ref, dz_ref, dws_ref, dbsp_ref, dgain_ref, dbias_ref, dwp_ref, dbp_ref, dps_ref, mixed_ref,
             dvn_ref):
        i = pl.program_id(0)

        @pl.when(i == 0)
        def _():
            for ref in (dws_ref, dbsp_ref, dgain_ref, dbias_ref, dwp_ref, dbp_ref, dps_ref):
                ref[...] = jnp.zeros_like(ref)

        wc = _tril_weights(ws_ref)
        gain = gain_ref[...]
        u, vhat, rstd, vn, dga = _mixer_forward_tile(z_ref[:, :2 * D_A], wc, bsp_ref, gain, bias_ref[...], mixed_ref)
        dya = dyc_ref[:, :D_A]
        du = dya * mixed_ref[...]
        dmixed = dya * u
        dmb = dmixed.astype(BF16)
        dm_sum = dmixed[0:CHUNK, :]
        for k in range(1, ts // CHUNK):
            dm_sum = dm_sum + dmixed[k * CHUNK:(k + 1) * CHUNK, :]
        r_idx = lax.broadcasted_iota(jnp.int32, (CHUNK, CHUNK), 0)
        s_idx = lax.broadcasted_iota(jnp.int32, (CHUNK, CHUNK), 1)
        causal = (s_idx <= r_idx).astype(F32)
        for h in range(N_HEADS):
            lanes = slice(h * HEAD_DIM, (h + 1) * HEAD_DIM)
            dbsp_ref[h] += jnp.sum(dm_sum[:, lanes], axis=1, keepdims=True)
            acc = None
            for k in range(ts // CHUNK):
                rows = slice(k * CHUNK, (k + 1) * CHUNK)
                t = _dot_nt(dmb[rows, lanes], vn[rows, lanes])
                acc = t if acc is None else acc + t
                dvn_ref[rows, lanes] = _dot_tn(wc[h], dmb[rows, lanes])
            dws_ref[h] += acc * causal
        dvn = dvn_ref[...]
        dgain_ref[...] += _colsum(dvn * vhat)
        dbias_ref[...] += _colsum(dvn)
        dvh = dvn * gain
        dv = rstd * (dvh - _rowmean(dvh) - vhat * _rowmean(dvh * vhat))
        dz_ref[:, :D_A] = (du * dga[:, :D_A]).astype(BF16)
        dz_ref[:, D_A:2 * D_A] = (dv * dga[:, D_A:]).astype(BF16)

        zb = z_ref[:, 2 * D_A:]
        prev = jnp.where(i == 0, 0.0, zprev_ref[...])
        zb_ext = jnp.concatenate([zb, znext_ref[...]], axis=0)
        sums = _causal_window_sums(jnp.concatenate([prev, zb_ext], axis=0))
        pos, inv_counts = _window_counts(i * ts, te)
        dyb_ext = jnp.concatenate([dyc_ref[:, D_A:], dynext_ref[...]], axis=0)
        dlin_ext = dyb_ext * ps_ref[...]
        dbp_ref[...] += _colsum(dlin_ext[:ts, :])
        scaled = []
        ddiffs = []
        lins = []
        for g in range(len(POOL_WINDOWS)):
            lanes = slice(g * GROUP_DIM, (g + 1) * GROUP_DIM)
            diff = (sums[g][HALO:, :] * inv_counts[g] - zb_ext[:, lanes]).astype(BF16)
            wpb = wp_ref[g].astype(BF16)
            dlb = dlin_ext[:, lanes].astype(BF16)
            lins.append(_dot(diff[:ts, :], wpb) + bp_ref[:, lanes])
            dwp_ref[g] += _dot_tn(diff[:ts, :], dlb[:ts, :])
            dd = _dot_nt(dlb, wpb)
            ddiffs.append(dd)
            scaled.append(jnp.where(pos < float(s_len), dd * inv_counts[g], 0.0))
        dps_ref[...] += _colsum(dyb_ext[:ts, :] * jnp.concatenate(lins, axis=1))
        back = _anticausal_window_sums(jnp.concatenate(scaled, axis=1))
        for g in range(len(POOL_WINDOWS)):
            dz_ref[:, 2 * D_A + g * GROUP_DIM:2 * D_A + (g + 1) * GROUP_DIM] = (
                back[g][:ts, :] - ddiffs[g][:ts, :]).astype(BF16)

    sq = jax.ShapeDtypeStruct((N_HEADS, CHUNK, CHUNK), F32)
    vec = jax.ShapeDtypeStruct((1, D_A), F32)
    return pl.pallas_call(
        body, name="mixer_bwd", grid=(s_len // ts,),
        in_specs=[_rows(ts, D_Z),
                  pl.BlockSpec((HALO, D_B), lambda i: (jnp.maximum(i * nb - 1, 0), 2)),
                  pl.BlockSpec((HALO, D_B), lambda i: (jnp.minimum((i + 1) * nb, last), 2)),
                  _rows(ts, D_MODEL),
                  pl.BlockSpec((HALO, D_B), lambda i: (jnp.minimum((i + 1) * nb, last), 1)),
                  _const((N_HEADS, CHUNK, CHUNK)), _const((CHUNK, D_A)), _const((1, D_A)), _const((1, D_A)),
                  _const((N_HEADS, GROUP_DIM, GROUP_DIM)), _const((1, D_B)), _const((1, D_B)), _ANY],
        out_specs=[_rows(ts, D_Z), _const((N_HEADS, CHUNK, CHUNK)), _const((N_HEADS, CHUNK, 1)), _const((1, D_A)),
                   _const((1, D_A)), _const((N_HEADS, GROUP_DIM, GROUP_DIM)), _const((1, D_B)), _const((1, D_B))],
        out_shape=[jax.ShapeDtypeStruct((s_len, D_Z), BF16), sq, jax.ShapeDtypeStruct((N_HEADS, CHUNK, 1), F32), vec,
                   vec, sq, vec, vec],
        scratch_shapes=[pltpu.VMEM((ts, D_A), F32), pltpu.VMEM((ts, D_A), F32)],
        compiler_params=_params(),
    )(z, z, z, dyc, dyc, w_spatial, bsp_full, gain, bias, w_pool, b_pool, pool_scale, dep)


def _bwd_in(dz, dx1, x, mod6, n1pre, win_g, dep, ts):
    s_len = x.shape[0]
    cs = D_Z // N_CHIPS

    def body(dz_ref, dx1_ref, x_ref, mod_ref, g_ref, w_ref, dep_ref, gx_ref, dshift_ref, da_ref, dw_hbm, dw_ref,
             wfull_ref, dw_sem):
        _zero_on_first_step(dshift_ref, da_ref, dw_ref)
        _join_w_in_on_first_step(w_ref, wfull_ref)
        xv = x_ref[...]
        r = lax.rsqrt(_rowmean(xv * xv) + EPS)
        xh = xv * r
        h1b = (xh * (g_ref[...] * (1.0 + mod_ref[1:2, :])) + mod_ref[0:1, :]).astype(BF16)
        dzb = dz_ref[...]
        dw = _dot_tn(h1b, dzb)
        for j in range(N_CHIPS):
            dw_ref[j] += dw[:, j * cs:(j + 1) * cs].reshape(2, D_MODEL // 2, cs)
        dh = _dot_nt(dzb, wfull_ref[...])
        a1 = g_ref[...] * (1.0 + mod_ref[1:2, :])
        dshift_ref[...] += _colsum(dh)
        da_ref[...] += _colsum(dh * xh)
        dxh = dh * a1
        gx_ref[...] = dx1_ref[...] + r * (dxh - xh * _rowmean(dxh * xh))
        for j in range(N_CHIPS):
            _store_shard_on_last_step(dw_ref, dw_hbm, dw_sem, j)
        _wait_stores_on_last_step((dw_ref, dw_hbm, dw_sem))

    vec = jax.ShapeDtypeStruct((1, D_MODEL), F32)
    dw_shape = (N_CHIPS, 2, D_MODEL // 2, cs)
    return pl.pallas_call(
        body, name="bwd_in", grid=(s_len // ts,),
        in_specs=[_rows(ts, D_Z), _rows(ts, D_MODEL), _rows(ts, D_MODEL), _const((N_MOD, D_MODEL)),
                  _const((1, D_MODEL)), _VMEM, _ANY],
        out_specs=[_rows(ts, D_MODEL), _const((1, D_MODEL)), _const((1, D_MODEL)), _ANY],
        out_shape=[jax.ShapeDtypeStruct((s_len, D_MODEL), F32), vec, vec, jax.ShapeDtypeStruct(dw_shape, F32)],
        scratch_shapes=[pltpu.VMEM(dw_shape, F32), pltpu.VMEM((D_MODEL, D_Z), BF16),
                        pltpu.SemaphoreType.DMA((N_CHIPS,))],
        compiler_params=_params(),
    )(dz, dx1, x, mod6, n1pre, win_g, dep)


def _adamw_math(w, g, m, v):
    m = ADAM_B1 * m + (1.0 - ADAM_B1) * g
    v = ADAM_B2 * v + (1.0 - ADAM_B2) * (g * g)
    m_hat = m / (1.0 - ADAM_B1 ** ADAM_STEP)
    v_hat = v / (1.0 - ADAM_B2 ** ADAM_STEP)
    delta = -ADAM_LR * (m_hat / (jnp.sqrt(v_hat) + ADAM_EPS) + ADAM_WD * w)
    return delta, m, v


def _adamw(gs, ws, ms, vs, name, steps):
    n = len(ws)

    def body(*refs):
        for i in range(n):
            g_ref, w_ref, m_ref, v_ref = refs[4 * i:4 * i + 4]
            d, nm, nv = _adamw_math(w_ref[...], g_ref[...], m_ref[...], v_ref[...])
            for ref, val in zip(refs[4 * n + 3 * i:4 * n + 3 * i + 3], (d, nm, nv)):
                ref[...] = val

    specs = [_rows(w.shape[0] // steps, w.shape[1]) for w in ws]
    out = pl.pallas_call(
        body, name=name, grid=(steps,),
        in_specs=[s for s in specs for _ in range(4)], out_specs=[s for s in specs for _ in range(3)],
        out_shape=[jax.ShapeDtypeStruct(w.shape, F32) for w in ws for _ in range(3)],
        compiler_params=_params(),
    )(*[a for quad in zip(gs, ws, ms, vs) for a in quad])
    return [out[3 * i:3 * i + 3] for i in range(n)]


def _ada_grad_adamw(sc_t, dmod_shard, w, m, v, tr):
    rows, cols = w.shape

    def body(s_ref, dm_ref, w_ref, m_ref, v_ref, g_ref, d_ref, nm_ref, nv_ref):
        g = s_ref[:, 0:1] * dm_ref[0:1, :]
        for b in range(1, N_DEV):
            g = g + s_ref[:, b:b + 1] * dm_ref[b:b + 1, :]
        g_ref[...] = g
        d, nm, nv = _adamw_math(w_ref[...], g, m_ref[...], v_ref[...])
        d_ref[...] = d
        nm_ref[...] = nm
        nv_ref[...] = nv

    spec = _rows(tr, cols)
    shape = jax.ShapeDtypeStruct((rows, cols), F32)
    return pl.pallas_call(
        body, name="ada_grad_adamw", grid=(rows // tr,),
        in_specs=[_rows(tr, N_DEV), _const((N_DEV, cols)), spec, spec, spec],
        out_specs=[spec] * 4, out_shape=[shape] * 4, compiler_params=_params(),
    )(sc_t, dmod_shard, w, m, v)


def _mod_grads(da1, dshift1, s1, da2, dshift2, s2, mod6, n1pre, n1post, n2pre, n2post):
    def body(da1_ref, ds1_ref, s1_ref, da2_ref, ds2_ref, s2_ref, mod_ref, n1_ref, p1_ref, n2_ref, p2_ref, dmod_ref,
             dn_ref):
        dmod_ref[0:1, :] = ds1_ref[...]
        dmod_ref[1:2, :] = da1_ref[...] * n1_ref[...]
        dmod_ref[2:3, :] = s1_ref[...] * p1_ref[...]
        dmod_ref[3:4, :] = ds2_ref[...]
        dmod_ref[4:5, :] = da2_ref[...] * n2_ref[...]
        dmod_ref[5:6, :] = s2_ref[...] * p2_ref[...]
        dn_ref[0:1, :] = da1_ref[...] * (1.0 + mod_ref[1:2, :])
        dn_ref[1:2, :] = s1_ref[...] * mod_ref[2:3, :]
        dn_ref[2:3, :] = da2_ref[...] * (1.0 + mod_ref[4:5, :])
        dn_ref[3:4, :] = s2_ref[...] * mod_ref[5:6, :]

    return pl.pallas_call(
        body, name="mod_grads",
        out_shape=[jax.ShapeDtypeStruct((N_MOD, D_MODEL), F32), jax.ShapeDtypeStruct((4, D_MODEL), F32)],
    )(da1, dshift1, s1, da2, dshift2, s2, mod6, n1pre, n1post, n2pre, n2post)


def _position():
    x, y, c = lax.axis_index("x"), lax.axis_index("y"), lax.axis_index("c")
    return x, y, c


def _flip(v, bit):
    return 1 - v if bit else v


def _peer(x, y, c, k):
    return (_flip(x, k & 4), _flip(y, k & 2), _flip(c, k & 1))


def _remote(src, dst, send_sem, recv_sem, device):
    return pltpu.make_async_remote_copy(src_ref=src, dst_ref=dst, send_sem=send_sem, recv_sem=recv_sem,
                                        device_id=device, device_id_type=MESH)


def _cast_to_slots(ws, pos, dep, name, steps):
    n = len(ws)

    def body(pos_ref, *refs):
        for w_ref, o_ref in zip(refs[:n], refs[n + 1:]):
            o_ref[0] = w_ref[...].astype(BF16)

    return pl.pallas_call(
        body, name=name,
        grid_spec=pltpu.PrefetchScalarGridSpec(
            num_scalar_prefetch=1, grid=(steps,),
            in_specs=[pl.BlockSpec((w.shape[0] // steps, w.shape[1]), lambda i, pos: (i, 0)) for w in ws] + [_ANY],
            out_specs=[pl.BlockSpec((1, w.shape[0] // steps, w.shape[1]), lambda i, pos: (pos[1], i, 0)) for w in ws]),
        out_shape=[jax.ShapeDtypeStruct((N_CHIPS,) + w.shape, BF16) for w in ws], compiler_params=_params(),
    )(pos, *ws, dep)


def _mod_exchange(c_row, w_ada_shard, b_ada_row):
    cs = w_ada_shard.shape[1]

    def body(c_ref, w_hbm, b_ref, mod_ref, sc_ref, rows_ref, w_ref, w_sem, send1, recv1, send2, recv2):
        x, y, c = _position()
        me = 4 * x + 2 * y + c
        chip = 2 * x + y
        w_load = pltpu.make_async_copy(w_hbm, w_ref, w_sem)
        w_load.start()
        cv = c_ref[...]
        sc_ref[me] = cv * jax.nn.sigmoid(cv)
        gather = [_remote(sc_ref.at[me], sc_ref.at[me], send1.at[k - 1], recv1.at[k - 1], _peer(x, y, c, k))
                  for k in range(1, N_DEV)]
        for cp in gather:
            cp.start()
        for k in range(1, N_DEV):
            px, py, pc = _peer(x, y, c, k)
            src = 4 * px + 2 * py + pc
            _remote(sc_ref.at[src], sc_ref.at[src], send1.at[k - 1], recv1.at[k - 1], (px, py, pc)).wait_recv()
        for cp in gather:
            cp.wait_send()
        sc_all = jnp.concatenate([sc_ref[b] for b in range(N_DEV)], axis=0)
        w_load.wait()
        part = _dot(sc_all.astype(BF16), w_ref[...].astype(BF16))
        part = part + b_ref[:, pl.ds(pl.multiple_of(chip * cs, LANES), cs)]
        for b in range(N_DEV):
            rows_ref[b] = part[b:b + 1, :]
        mod_ref[chip] = rows_ref[me]
        hand = []
        for k in (2, 4, 6):
            px, py, _ = _peer(x, y, c, k)
            hand.append(_remote(rows_ref.at[4 * px + 2 * py + c], mod_ref.at[chip], send2.at[k // 2 - 1],
                                recv2.at[k // 2 - 1], (px, py, c)))
        for cp in hand:
            cp.start()
        for k in (2, 4, 6):
            px, py, _ = _peer(x, y, c, k)
            pchip = 2 * px + py
            _remote(rows_ref.at[me], mod_ref.at[pchip], send2.at[k // 2 - 1], recv2.at[k // 2 - 1],
                    (px, py, c)).wait_recv()
        for cp in hand:
            cp.wait_send()

    return pl.pallas_call(
        body, name="mod_exchange",
        in_specs=[_VMEM, _ANY, _VMEM], out_specs=[_VMEM, _VMEM],
        out_shape=[jax.ShapeDtypeStruct((N_CHIPS, 1, cs), F32), jax.ShapeDtypeStruct((N_DEV, 1, D_MODEL), F32)],
        scratch_shapes=[pltpu.VMEM((N_DEV, 1, cs), F32), pltpu.VMEM(w_ada_shard.shape, F32), pltpu.SemaphoreType.DMA,
                        pltpu.SemaphoreType.DMA((N_DEV - 1,)),
                        pltpu.SemaphoreType.DMA((N_DEV - 1,)), pltpu.SemaphoreType.DMA((N_CHIPS - 1,)),
                        pltpu.SemaphoreType.DMA((N_CHIPS - 1,))],
        compiler_params=pltpu.CompilerParams(vmem_limit_bytes=VMEM_LIMIT),
    )(c_row, w_ada_shard, b_ada_row)


_HBM = pl.BlockSpec(memory_space=pltpu.HBM)
_SEM = pl.BlockSpec(memory_space=pltpu.SEMAPHORE)
_EFFECT = pltpu.SideEffectType.DATAFLOW_SIDE_EFFECTING
_CHIP_HOPS = (2, 4, 6)


def _in_hbm(a):
    return pltpu.with_memory_space_constraint(a, pltpu.HBM)


def _sems3():
    return pltpu.SemaphoreType.DMA((len(_CHIP_HOPS),))


def _ag_start(lands, after, name):
    n = len(lands)

    def body(*refs):
        zones = refs[:n]
        sends, recvs = refs[n + 1:2 * n + 1], refs[2 * n + 1:3 * n + 1]
        x, y, c = _position()
        chip = 2 * x + y
        for i in range(n):
            half = zones[i].shape[1] // 2
            mine = zones[i].at[chip, pl.ds(c * half, half)]
            for s, k in enumerate(_CHIP_HOPS):
                px, py, _ = _peer(x, y, c, k)
                _remote(mine, mine, sends[i].at[s], recvs[i].at[s], (px, py, c)).start()

    out = pl.pallas_call(
        body, name=name,
        in_specs=[_HBM] * n + [_ANY],
        out_specs=[_SEM] * (2 * n) + [_HBM] * n,
        out_shape=[_sems3()] * (2 * n) + [pltpu.HBM(z.shape, BF16) for z in lands],
        input_output_aliases={i: 2 * n + i for i in range(n)},
        compiler_params=pltpu.CompilerParams(has_side_effects=_EFFECT),
    )(*[_in_hbm(z) for z in lands], after)
    return [(out[2 * n + i], out[i], out[n + i]) for i in range(n)]


def _ag_pass(group, after, name):
    n = len(group)

    def body(*refs):
        zones = refs[:n]
        sends, recvs = refs[n:2 * n], refs[2 * n:3 * n]
        fsends, frecvs = refs[4 * n + 1:5 * n + 1], refs[5 * n + 1:6 * n + 1]
        x, y, c = _position()
        chip = 2 * x + y
        for i in range(n):
            half = zones[i].shape[1] // 2
            rows = pl.ds(c * half, half)
            for s, k in enumerate(_CHIP_HOPS):
                px, py, _ = _peer(x, y, c, k)
                landed = zones[i].at[2 * px + py, rows]
                _remote(landed, landed, sends[i].at[s], recvs[i].at[s], (px, py, c)).wait_recv()
                _remote(landed, landed, fsends[i].at[s], frecvs[i].at[s], (x, y, 1 - c)).start()
        for i in range(n):
            half = zones[i].shape[1] // 2
            mine = zones[i].at[chip, pl.ds(c * half, half)]
            for s, k in enumerate(_CHIP_HOPS):
                px, py, _ = _peer(x, y, c, k)
                _remote(mine, mine, sends[i].at[s], recvs[i].at[s], (px, py, c)).wait_send()

    out = pl.pallas_call(
        body, name=name,
        in_specs=[_HBM] * n + [_SEM] * (2 * n) + [_ANY],
        out_specs=[_HBM] * n + [_SEM] * (2 * n),
        out_shape=[pltpu.HBM(g[0].shape, BF16) for g in group] + [_sems3()] * (2 * n),
        input_output_aliases={i: i for i in range(n)},
        compiler_params=pltpu.CompilerParams(has_side_effects=_EFFECT),
    )(*[g[0] for g in group], *[g[1] for g in group], *[g[2] for g in group], after)
    return [(out[i], out[n + i], out[2 * n + i]) for i in range(n)]


def _ag_done(group, name):
    n = len(group)

    def body(*refs):
        lands = refs[:n]
        fsends, frecvs = refs[n:2 * n], refs[2 * n:3 * n]
        x, y, c = _position()
        for i in range(n):
            half = lands[i].shape[1] // 2
            for s, k in enumerate(_CHIP_HOPS):
                px, py, _ = _peer(x, y, c, k)
                sent = lands[i].at[2 * px + py, pl.ds(c * half, half)]
                got = lands[i].at[2 * px + py, pl.ds((1 - c) * half, half)]
                cp = _remote(sent, got, fsends[i].at[s], frecvs[i].at[s], (x, y, 1 - c))
                cp.wait_recv()
                cp.wait_send()

    out = pl.pallas_call(
        body, name=name,
        in_specs=[_HBM] * n + [_SEM] * (2 * n),
        out_specs=[_HBM] * n,
        out_shape=[pltpu.HBM(g[0].shape, BF16) for g in group],
        input_output_aliases={i: i for i in range(n)},
        compiler_params=pltpu.CompilerParams(has_side_effects=_EFFECT),
    )(*[g[0] for g in group], *[g[1] for g in group], *[g[2] for g in group])
    return list(out)


def _small_spread_start(slots, after):
    def body(z_ref, after_ref, sends, recvs, z_out):
        x, y, c = _position()
        mine = z_ref.at[2 * x + y]
        for s, k in enumerate(_CHIP_HOPS):
            px, py, _ = _peer(x, y, c, k)
            _remote(mine, mine, sends.at[s], recvs.at[s], (px, py, c)).start()

    sends, recvs, out = pl.pallas_call(
        body, name="small_spread_start",
        in_specs=[_HBM, _ANY], out_specs=[_SEM, _SEM, _HBM],
        out_shape=[_sems3(), _sems3(), pltpu.HBM(slots.shape, F32)],
        input_output_aliases={0: 2},
        compiler_params=pltpu.CompilerParams(has_side_effects=_EFFECT),
    )(_in_hbm(slots), after)
    return out, sends, recvs


def _small_spread_wait(slots, sends, recvs, afters):
    def body(z_ref, sends, recvs, *rest):
        x, y, c = _position()
        mine = z_ref.at[2 * x + y]
        for s, k in enumerate(_CHIP_HOPS):
            px, py, _ = _peer(x, y, c, k)
            cp = _remote(mine, z_ref.at[2 * px + py], sends.at[s], recvs.at[s], (px, py, c))
            cp.wait_recv()
            cp.wait_send()

    return pl.pallas_call(
        body, name="small_spread_wait",
        in_specs=[_HBM, _SEM, _SEM] + [_ANY] * len(afters), out_specs=_HBM, out_shape=pltpu.HBM(slots.shape, F32),
        input_output_aliases={0: 0},
        compiler_params=pltpu.CompilerParams(has_side_effects=_EFFECT),
    )(slots, sends, recvs, *afters)


def _sibling_sum(pgs, name, small=None):
    n = len(pgs)
    k = 0 if small is None else 1
    units = [(i, j) for i in range(n) for j in range(N_CHIPS)]

    def body(*refs):
        refs = list(refs)
        take = lambda count: [refs.pop(0) for _ in range(count)]
        ins, small_in = take(n), take(k)
        qbs, owns, slots_out = take(n), take(n), take(k)
        mine, other, stage, got = take(n), take(n), take(n), take(n)
        load_a, load_b, send, recv, keep_wire, keep_own = take(6)
        x, y, c = _position()
        chip = 2 * x + y
        if k:
            sib_ref, pair_send, pair_recv, keep_small = take(4)
            pair = _remote(small_in[0], sib_ref, pair_send, pair_recv, (x, y, 1 - c))
            pair.start()
        loads_a = [pltpu.make_async_copy(ins[i].at[j, 1 - c], other[i].at[j], load_a.at[u])
                   for u, (i, j) in enumerate(units)]
        loads_b = [pltpu.make_async_copy(ins[i].at[j, c], mine[i].at[j], load_b.at[u])
                   for u, (i, j) in enumerate(units)]
        for cp in loads_a + loads_b:
            cp.start()
        sent = []
        for u, (i, j) in enumerate(units):
            loads_a[u].wait()
            stage[i][j] = other[i][j].astype(BF16)
            cp = _remote(stage[i].at[j], got[i].at[j], send.at[u], recv.at[u], (x, y, 1 - c))
            cp.start()
            sent.append(cp)
        stores = []
        for u, (i, j) in enumerate(units):
            loads_b[u].wait()
            sent[u].wait_recv()
            q = mine[i][j] + got[i][j].astype(F32)
            mine[i][j] = q
            got[i][j] = q.astype(BF16)
            stores.append(pltpu.make_async_copy(got[i].at[j], qbs[i].at[j], keep_wire.at[u]))
            stores[-1].start()
        for i in range(n):
            stores.append(pltpu.make_async_copy(mine[i].at[chip], owns[i], keep_own.at[i]))
            stores[-1].start()
        if k:
            pair.wait()
            sib_ref[...] = small_in[0][...] + sib_ref[...]
            stores.append(pltpu.make_async_copy(sib_ref, slots_out[0].at[chip], keep_small))
            stores[-1].start()
        for cp in sent:
            cp.wait_send()
        for cp in stores:
            cp.wait()

    wire = [(N_CHIPS,) + p.shape[2:] for p in pgs]
    extra_out, extra_scratch = [], []
    if k:
        extra_out = [jax.ShapeDtypeStruct((N_CHIPS,) + small.shape, F32)]
        extra_scratch = [pltpu.VMEM(small.shape, F32)] + [pltpu.SemaphoreType.DMA] * 3
    out = pl.pallas_call(
        body, name=name, in_specs=[_ANY] * n + [_VMEM] * k, out_specs=[_ANY] * (2 * n + k),
        out_shape=[jax.ShapeDtypeStruct(w, BF16) for w in wire] + [jax.ShapeDtypeStruct(w[1:], F32) for w in wire]
        + extra_out,
        scratch_shapes=[pltpu.VMEM(w, F32) for w in wire] * 2 + [pltpu.VMEM(w, BF16) for w in wire] * 2
        + [pltpu.SemaphoreType.DMA((len(units),))] * 5 + [pltpu.SemaphoreType.DMA((n,))] + extra_scratch,
        compiler_params=pltpu.CompilerParams(vmem_limit_bytes=VMEM_LIMIT),
    )(*pgs, *([small] if k else []))
    return list(out[:n]), list(out[n:2 * n]), list(out[2 * n:])


def _rs_start(qbs, name, after=None):
    n = len(qbs)
    k = 0 if after is None else 1

    def body(*refs):
        outs, inboxes = refs[:n], refs[n:2 * n]
        sends, recvs = refs[2 * n + k:3 * n + k], refs[3 * n + k:4 * n + k]
        x, y, c = _position()
        chip = 2 * x + y
        for i in range(n):
            for s, hop in enumerate(_CHIP_HOPS):
                px, py, _ = _peer(x, y, c, hop)
                _remote(outs[i].at[2 * px + py], inboxes[i].at[chip], sends[i].at[s], recvs[i].at[s], (px, py, c)).start()

    inboxes = [_in_hbm(lax.empty(q.shape, BF16)) for q in qbs]
    out = pl.pallas_call(
        body, name=name,
        in_specs=[_HBM] * (2 * n) + [_ANY] * k,
        out_specs=[_SEM] * (2 * n) + [_HBM] * (2 * n),
        out_shape=[_sems3()] * (2 * n) + [pltpu.HBM(q.shape, BF16) for q in qbs] * 2,
        input_output_aliases={i: 2 * n + i for i in range(2 * n)},
        compiler_params=pltpu.CompilerParams(has_side_effects=_EFFECT),
    )(*[_in_hbm(q) for q in qbs], *inboxes, *([after] if k else []))
    return [(out[2 * n + i], out[3 * n + i], out[i], out[n + i]) for i in range(n)]


def _rs_wait(group, after, name):
    n = len(group)

    def body(*refs):
        outs, inboxes = refs[:n], refs[n:2 * n]
        sends, recvs = refs[2 * n:3 * n], refs[3 * n:4 * n]
        x, y, c = _position()
        for i in range(n):
            for s, k in enumerate(_CHIP_HOPS):
                px, py, _ = _peer(x, y, c, k)
                slot = 2 * px + py
                cp = _remote(outs[i].at[slot], inboxes[i].at[slot], sends[i].at[s], recvs[i].at[s], (px, py, c))
                cp.wait_recv()
                cp.wait_send()

    out = pl.pallas_call(
        body, name=name,
        in_specs=[_HBM] * (2 * n) + [_SEM] * (2 * n) + [_ANY],
        out_specs=[_HBM] * n,
        out_shape=[pltpu.HBM(g[1].shape, BF16) for g in group],
        input_output_aliases={n + i: i for i in range(n)},
        compiler_params=pltpu.CompilerParams(has_side_effects=_EFFECT),
    )(*[g[0] for g in group], *[g[1] for g in group], *[g[2] for g in group], *[g[3] for g in group], after)
    return list(out)


def _final_share(inboxes, owns, name):
    n = len(inboxes)
    units = [(i, s) for i in range(n) for s in range(len(_CHIP_HOPS))]

    def body(*refs):
        ins, mine, outs, landed, half = (refs[k * n:(k + 1) * n] for k in range(5))
        load, load_own, keep, send, recv = refs[5 * n:]
        x, y, c = _position()
        loads = []
        for u, (i, s) in enumerate(units):
            px, py, _ = _peer(x, y, c, _CHIP_HOPS[s])
            loads.append(pltpu.make_async_copy(ins[i].at[2 * px + py], landed[i].at[s], load.at[u]))
        loads_own = [pltpu.make_async_copy(mine[i], half[i], load_own.at[i]) for i in range(n)]
        for cp in loads + loads_own:
            cp.start()
        copies = []
        for i in range(n):
            for s in range(len(_CHIP_HOPS)):
                loads[len(_CHIP_HOPS) * i + s].wait()
            loads_own[i].wait()
            total = (landed[i][0].astype(F32) + landed[i][1].astype(F32)) + landed[i][2].astype(F32)
            half[i][...] = total + half[i][...]
            copies.append(pltpu.make_async_copy(half[i], outs[i].at[c], keep.at[i]))
            copies.append(_remote(half[i], outs[i].at[c], send.at[i], recv.at[i], (x, y, 1 - c)))
            for cp in copies[-2:]:
                cp.start()
        for i in range(n):
            theirs = outs[i].at[1 - c]
            _remote(theirs, theirs, send.at[i], recv.at[i], (x, y, 1 - c)).wait_recv()
        for i in range(n):
            copies[2 * i].wait()
            copies[2 * i + 1].wait_send()

    return pl.pallas_call(
        body, name=name, in_specs=[_ANY] * (2 * n), out_specs=[_ANY] * n,
        out_shape=[jax.ShapeDtypeStruct((2,) + o.shape, F32) for o in owns],
        scratch_shapes=[pltpu.VMEM((len(_CHIP_HOPS),) + o.shape, BF16) for o in owns]
        + [pltpu.VMEM(o.shape, F32) for o in owns]
        + [pltpu.SemaphoreType.DMA((len(units),))] + [pltpu.SemaphoreType.DMA((n,))] * 4,
        compiler_params=pltpu.CompilerParams(vmem_limit_bytes=VMEM_LIMIT),
    )(*inboxes, *owns)


_SMALL = (("b_ada", N_MOD * D_MODEL), ("norm1_pre", D_MODEL), ("norm1_post", D_MODEL), ("norm2_pre", D_MODEL),
          ("norm2_post", D_MODEL), ("w_spatial", N_HEADS * CHUNK * CHUNK), ("b_spatial", N_HEADS * CHUNK),
          ("ln_v_gain", D_A), ("ln_v_bias", D_A), ("w_pool", N_HEADS * GROUP_DIM * GROUP_DIM),
          ("b_pool", D_B), ("pool_scale", D_B))
_MOD_ROWS = N_MOD * D_MODEL // LANES


def _packed_rows(size):
    return -(-(size // LANES) // SUBLANES) * SUBLANES


def _pack(parts):
    out = []
    for name, size in _SMALL:
        a = parts[name].reshape(size // LANES, LANES)
        pad = _packed_rows(size) - a.shape[0]
        out.append(jnp.pad(a, ((0, pad), (0, 0))) if pad else a)
    return out


def _small_adamw(slots, ws, ms, vs):
    n = len(_SMALL)
    head = N_DEV * _MOD_ROWS

    def body(*refs):
        s_ref, w, m, v = refs[0], refs[1:1 + n], refs[1 + n:1 + 2 * n], refs[1 + 2 * n:1 + 3 * n]
        outs = refs[1 + 3 * n:1 + 7 * n]
        dmod_ref, loss_ref, t_ref = refs[1 + 7 * n:]
        t_ref[...] = ((s_ref[0] + s_ref[1]) + s_ref[2]) + s_ref[3]
        dmod_ref[...] = t_ref[0:head, :]
        loss_ref[...] = t_ref[t_ref.shape[0] - 8:, :]
        row = head
        for i, (_, size) in enumerate(_SMALL):
            if i == 0:
                g = t_ref[0:_MOD_ROWS, :]
                for b in range(1, N_DEV):
                    g = g + t_ref[b * _MOD_ROWS:(b + 1) * _MOD_ROWS, :]
            else:
                g = t_ref[row:row + size // LANES, :]
                row += _packed_rows(size)
            d, nm, nv = _adamw_math(w[i][...], g, m[i][...], v[i][...])
            for ref, val in zip(outs[4 * i:4 * i + 4], (g, d, nm, nv)):
                ref[...] = val

    each = [jax.ShapeDtypeStruct((size // LANES, LANES), F32) for _, size in _SMALL for _ in range(4)]
    out = pl.pallas_call(
        body, name="small_adamw",
        out_shape=each + [jax.ShapeDtypeStruct((head, LANES), F32), jax.ShapeDtypeStruct((SUBLANES, LANES), F32)],
        scratch_shapes=[pltpu.VMEM(slots.shape[1:], F32)],
        compiler_params=pltpu.CompilerParams(vmem_limit_bytes=VMEM_LIMIT),
    )(slots, *ws, *ms, *vs)
    return [out[4 * i:4 * i + 4] for i in range(n)], out[4 * n], out[4 * n + 1]


def kernel(x, c, w_ada, b_ada, norm1_pre, norm1_post, w_in, w_spatial, b_spatial, ln_v_gain, ln_v_bias, w_pool, b_pool, pool_scale, w_out, norm2_pre, norm2_post, w_fc1, w_fc2, loss_target, m_w_ada, m_b_ada, m_norm1_pre, m_norm1_post, m_w_in, m_w_spatial, m_b_spatial, m_ln_v_gain, m_ln_v_bias, m_w_pool, m_b_pool, m_pool_scale, m_w_out, m_norm2_pre, m_norm2_post, m_w_fc1, m_w_fc2, v_w_ada, v_b_ada, v_norm1_pre, v_norm1_post, v_w_in, v_w_spatial, v_b_spatial, v_ln_v_gain, v_ln_v_bias, v_w_pool, v_b_pool, v_pool_scale, v_w_out, v_norm2_pre, v_norm2_post, v_w_fc1, v_w_fc2):
    weights = dict(w_ada=w_ada, b_ada=b_ada, norm1_pre=norm1_pre, norm1_post=norm1_post, w_in=w_in,
                   w_spatial=w_spatial, b_spatial=b_spatial, ln_v_gain=ln_v_gain, ln_v_bias=ln_v_bias, w_pool=w_pool,
                   b_pool=b_pool, pool_scale=pool_scale, w_out=w_out, norm2_pre=norm2_pre, norm2_post=norm2_post,
                   w_fc1=w_fc1, w_fc2=w_fc2)
    m_old = dict(w_ada=m_w_ada, b_ada=m_b_ada, norm1_pre=m_norm1_pre, norm1_post=m_norm1_post, w_in=m_w_in,
                 w_spatial=m_w_spatial, b_spatial=m_b_spatial, ln_v_gain=m_ln_v_gain, ln_v_bias=m_ln_v_bias,
                 w_pool=m_w_pool, b_pool=m_b_pool, pool_scale=m_pool_scale, w_out=m_w_out, norm2_pre=m_norm2_pre,
                 norm2_post=m_norm2_post, w_fc1=m_w_fc1, w_fc2=m_w_fc2)
    v_old = dict(w_ada=v_w_ada, b_ada=v_b_ada, norm1_pre=v_norm1_pre, norm1_post=v_norm1_post, w_in=v_w_in,
                 w_spatial=v_w_spatial, b_spatial=v_b_spatial, ln_v_gain=v_ln_v_gain, ln_v_bias=v_ln_v_bias,
                 w_pool=v_w_pool, b_pool=v_b_pool, pool_scale=v_pool_scale, w_out=v_w_out, norm2_pre=v_norm2_pre,
                 norm2_post=v_norm2_post, w_fc1=v_w_fc1, w_fc2=v_w_fc2)
    order = ("w_ada", "b_ada", "norm1_pre", "norm1_post", "w_in", "w_spatial", "b_spatial", "ln_v_gain", "ln_v_bias",
             "w_pool", "b_pool", "pool_scale", "w_out", "norm2_pre", "norm2_post", "w_fc1", "w_fc2")
    mx, my, mc = _position()
    me = 4 * mx + 2 * my + mc
    chip = 2 * mx + my
    row = lambda a: a.reshape(1, -1)

    pos = jnp.stack([mc, chip]).astype(jnp.int32)
    xs, target = x[0], loss_target[0]
    n1pre, n1post, n2pre, n2post = row(norm1_pre), row(norm1_post), row(norm2_pre), row(norm2_post)
    mixer = (w_spatial, jnp.repeat(b_spatial.T, HEAD_DIM, axis=1), row(ln_v_gain), row(ln_v_bias), w_pool,
             row(b_pool), row(pool_scale))
    ts_big, ts_mid = 512, 256

    mod4, sc_all = _mod_exchange(c, w_ada, row(b_ada))
    mod6 = mod4.reshape(N_MOD, D_MODEL)
    ag = _ag_start(_cast_to_slots([w_in, w_out], pos, mod4, "cast_mix", 1), mod4, "ag_start_mix")
    ag += _ag_start(_cast_to_slots([w_fc1, w_fc2], pos, ag[0][0], "cast_mlp", 4), ag[0][0], "ag_start_mlp")

    win_g, wout_g = _ag_done(_ag_pass([ag[0], ag[1]], ag[2][0], "ag_pass_mix"), "ag_done_mix")
    z, ycat, mix, x1, h2 = _fwd_mix(xs, mod6, n1pre, n1post, n2pre, win_g, wout_g, *mixer, ts_big)
    (fc1_g,) = _ag_done(_ag_pass([ag[2]], h2, "ag_pass_fc1"), "ag_done_fc1")
    q = _fwd_fc1(h2, fc1_g, ts_big)
    (fc2_g,) = _ag_done(_ag_pass([ag[3]], q, "ag_pass_fc2"), "ag_done_fc2")
    dy, df, loss, s2 = _fwd_fc2_loss(q, x1, target, mod6, n2post, fc2_g, ts_big)

    def reduce_start(partials, tag, small=None):
        wire, owns, slots = _sibling_sum(partials, "sibling_sum_" + tag, small)
        return _rs_start(wire, "rs_start_" + tag), owns, slots

    def reduce_finish(state, owns, names, tag, dep):
        inboxes = _rs_wait(state, dep, "rs_wait_" + tag)
        shards = _final_share(inboxes, owns, "final_share_" + tag)
        for n, g in zip(names, shards):
            grads[n] = g.reshape(weights[n].shape)
        updates = _adamw([grads[n] for n in names], [weights[n] for n in names], [m_old[n] for n in names],
                         [v_old[n] for n in names], "adamw_" + tag, 4)
        for n, (d, nm, nv) in zip(names, updates):
            deltas[n], new_m[n], new_v[n] = d, nm, nv

    grads, deltas, new_m, new_v = {}, {}, {}, {}
    dp, g_fc2 = _bwd_fc2(df, q, fc2_g, ts_big)
    state_fc2, owns_fc2, _ = reduce_start([g_fc2], "fc2")
    dx1, dyc, dshift2, da2, s1, g_fc1, g_out = _bwd_fc1_out(
        dp, dy, x1, mix, h2, ycat, mod6, n2pre, n1post, fc1_g, wout_g, state_fc2[0][0], ts_mid)
    state_mid, owns_mid, _ = reduce_start([g_fc1, g_out], "mid")
    dz, dws, dbsp, dgain, dbias, dwp, dbp, dps = _mixer_bwd(z, dyc, *mixer, state_mid[0][0], ts_big)
    grad_x, dshift1, da1, g_in = _bwd_in(dz, dx1, xs, mod6, n1pre, win_g, state_mid[0][0], ts_big)
    dmod6, dnorms = _mod_grads(da1, dshift1, s1, da2, dshift2, s2, mod6, n1pre, n1post, n2pre, n2post)

    parts = dict(b_ada=dmod6, norm1_pre=dnorms[0], norm1_post=dnorms[1], norm2_pre=dnorms[2], norm2_post=dnorms[3],
                 w_spatial=dws, b_spatial=dbsp, ln_v_gain=dgain, ln_v_bias=dbias, w_pool=dwp, b_pool=dbp,
                 pool_scale=dps)
    pieces = _pack(parts)
    slots = lax.dynamic_update_slice(jnp.zeros((N_DEV * _MOD_ROWS, LANES), F32), pieces[0], (me * _MOD_ROWS, 0))
    loss_tile = jnp.pad(loss, ((0, SUBLANES - 1), (0, LANES - 1)))
    wire_in, owns_in, pair_sum = _sibling_sum([g_in], "sibling_sum_in",
                                              jnp.concatenate([slots] + pieces[1:] + [loss_tile], axis=0))
    spread = _small_spread_start(pair_sum[0], wire_in[0])
    state_in = _rs_start(wire_in, "rs_start_in", spread[0])
    reduce_finish(state_fc2 + state_mid, owns_fc2 + owns_mid, ("w_fc2", "w_fc1", "w_out"), "mlp", state_in[0][0])
    flat = lambda d: [d[n].reshape(size // LANES, LANES) for n, size in _SMALL]
    small_out, dmod_all, loss_tile = _small_adamw(
        _small_spread_wait(*spread, [deltas[n] for n in ("w_fc2", "w_fc1", "w_out")]), flat(weights), flat(m_old),
        flat(v_old))
    loss = loss_tile[0, 0]
    for (n, _), (g, d, nm, nv) in zip(_SMALL, small_out):
        shape = weights[n].shape
        grads[n], deltas[n], new_m[n], new_v[n] = g.reshape(shape), d.reshape(shape), nm.reshape(shape), nv.reshape(shape)

    dmod_all = dmod_all.reshape(N_DEV, N_MOD * D_MODEL)
    cs = w_ada.shape[1]
    dmod_shard = lax.dynamic_slice(dmod_all, (0, chip * cs), (N_DEV, cs))
    sc_t = sc_all.reshape(N_DEV, D_MODEL).T
    grads["w_ada"], deltas["w_ada"], new_m["w_ada"], new_v["w_ada"] = _ada_grad_adamw(
        sc_t, dmod_shard, w_ada, m_w_ada, v_w_ada, 256)

    reduce_finish(state_in, owns_in, ("w_in",), "in", deltas["w_ada"])

    return (loss, grad_x[None], *[grads[n] for n in order], *[deltas[n] for n in order],
            *[new_m[n] for n in order], *[new_v[n] for n in order])
```

```python
import jax
import jax.numpy as jnp
from jax import lax
from jax.experimental import pallas as pl
from jax.experimental.pallas import tpu as pltpu

F32 = jnp.float32
BF16 = jnp.bfloat16
MESH = pl.DeviceIdType.MESH

D_MODEL = 1024
D_A = 512
D_B = 512
D_Z = 2 * D_A + D_B
N_HEADS = 4
HEAD_DIM = 128
CHUNK = 128
POOL_WINDOWS = (2, 4, 8, 16)
GROUP_DIM = 128
D_FF = 4096
N_MOD = 6
EPS = 1e-6
HALO = 16
N_CHIPS = 4
N_DEV = 8

ADAM_LR = 0.001
ADAM_B1 = 0.9
ADAM_B2 = 0.999
ADAM_EPS = 1e-08
ADAM_WD = 0.01
ADAM_STEP = 10

VMEM_LIMIT = 56 * 1024 * 1024
LANES = 128
SUBLANES = 8

_VMEM = pl.BlockSpec(memory_space=pltpu.VMEM)
_ANY = pl.BlockSpec(memory_space=pl.ANY)


def _params(n_grid_axes=1):
    return pltpu.CompilerParams(dimension_semantics=("arbitrary",) * n_grid_axes, vmem_limit_bytes=VMEM_LIMIT)


def _rows(ts, width):
    return pl.BlockSpec((ts, width), lambda i: (i, 0))


def _const(shape):
    return pl.BlockSpec(shape, lambda i: (0,) * len(shape))


def _dot(a, b):
    return jnp.dot(a, b, preferred_element_type=F32)


def _dot_nt(a, b):
    return lax.dot_general(a, b, (((1,), (1,)), ((), ())), preferred_element_type=F32)


def _dot_tn(a, b):
    return lax.dot_general(a, b, (((0,), (0,)), ((), ())), preferred_element_type=F32)


def _rowmean(v):
    return jnp.mean(v, axis=-1, keepdims=True)


def _colsum(v):
    return jnp.sum(v, axis=0, keepdims=True)


def _gelu_parts(z):
    k0 = 0.7978845608028654
    k1 = 0.044715
    z2 = z * z
    t = jnp.tanh(z * (k0 + (k0 * k1) * z2))
    u = 0.5 * t + 0.5
    g = z * u
    dg = u + (0.5 * z) * (1.0 - t * t) * (k0 + (3.0 * k0 * k1) * z2)
    return g, dg


def _tril_weights(ws_ref):
    r = lax.broadcasted_iota(jnp.int32, (CHUNK, CHUNK), 0)
    s = lax.broadcasted_iota(jnp.int32, (CHUNK, CHUNK), 1)
    mask = (s <= r).astype(F32)
    return [(ws_ref[h] * mask).astype(BF16) for h in range(N_HEADS)]


def _window_counts(first_row, n_rows):
    pos = (first_row + lax.broadcasted_iota(jnp.int32, (n_rows, 1), 0)).astype(F32)
    return pos, [1.0 / jnp.minimum(pos + 1.0, float(w)) for w in POOL_WINDOWS]


def _causal_window_sums(ext):
    out = []
    e = ext
    shift = 1
    for g in range(len(POOL_WINDOWS)):
        e = e + pltpu.roll(e, shift, 0)
        shift *= 2
        out.append(e[:, g * GROUP_DIM:(g + 1) * GROUP_DIM])
    return out


def _anticausal_window_sums(ext):
    n = ext.shape[0]
    out = []
    e = ext
    shift = 1
    for g in range(len(POOL_WINDOWS)):
        e = e + pltpu.roll(e, n - shift, 0)
        shift *= 2
        out.append(e[:, g * GROUP_DIM:(g + 1) * GROUP_DIM])
    return out


def _fwd_mix(x, mod6, n1pre, n1post, n2pre, win_g, wout_g, w_spatial, bsp_full, gain, bias, w_pool, b_pool, pool_scale, ts):
    s_len = x.shape[0]
    rs = D_MODEL // N_CHIPS

    def body(x_ref, mod_ref, g1pre_ref, g1post_ref, g2pre_ref, win_ref, wout_ref, ws_ref, bsp_ref, gain_ref,
             bias_ref, wp_ref, bp_ref, ps_ref, z_ref, y_ref, mix_ref, x1_ref, h2_ref, mixed_ref, prev_ref, wfull_ref):
        i = pl.program_id(0)
        _zero_on_first_step(prev_ref)
        _join_w_in_on_first_step(win_ref, wfull_ref)
        xv = x_ref[...]
        r = lax.rsqrt(_rowmean(xv * xv) + EPS)
        hb = ((xv * r) * (g1pre_ref[...] * (1.0 + mod_ref[1:2, :])) + mod_ref[0:1, :]).astype(BF16)
        z_ref[...] = _dot(hb, wfull_ref[...])

        wc = _tril_weights(ws_ref)
        u, _, _, _, _ = _mixer_forward_tile(z_ref[:, :2 * D_A], wc, bsp_ref, gain_ref[...], bias_ref[...], mixed_ref)
        y_ref[:, :D_A] = (u * mixed_ref[...]).astype(BF16)
        zb = z_ref[:, 2 * D_A:]
        sums = _causal_window_sums(jnp.concatenate([prev_ref[...], zb], axis=0))
        prev_ref[...] = zb[ts - HALO:, :]
        _, inv_counts = _window_counts(i * ts, ts)
        for g in range(len(POOL_WINDOWS)):
            lanes = slice(g * GROUP_DIM, (g + 1) * GROUP_DIM)
            diff = sums[g][HALO:, :] * inv_counts[g] - zb[:, lanes]
            lin = _dot(diff.astype(BF16), wp_ref[g].astype(BF16)) + bp_ref[:, lanes]
            y_ref[:, D_A + g * GROUP_DIM:D_A + (g + 1) * GROUP_DIM] = (lin * ps_ref[:, lanes]).astype(BF16)

        mix = None
        for j in range(N_CHIPS):
            part = _dot(y_ref[:, j * rs:(j + 1) * rs], wout_ref[j])
            mix = part if mix is None else mix + part
        mix_ref[...] = mix
        r2 = lax.rsqrt(_rowmean(mix * mix) + EPS)
        x1 = xv + (mix * r2) * (mod_ref[2:3, :] * g1post_ref[...])
        x1_ref[...] = x1
        r3 = lax.rsqrt(_rowmean(x1 * x1) + EPS)
        h2_ref[...] = ((x1 * r3) * (g2pre_ref[...] * (1.0 + mod_ref[4:5, :])) + mod_ref[3:4, :]).astype(BF16)

    vec = _const((1, D_MODEL))
    f32_rows = jax.ShapeDtypeStruct((s_len, D_MODEL), F32)
    bf16_rows = jax.ShapeDtypeStruct((s_len, D_MODEL), BF16)
    return pl.pallas_call(
        body, name="fwd_mix", grid=(s_len // ts,),
        in_specs=[_rows(ts, D_MODEL), _const((N_MOD, D_MODEL)), vec, vec, vec, _VMEM, _VMEM,
                  _const((N_HEADS, CHUNK, CHUNK)), _const((CHUNK, D_A)), _const((1, D_A)), _const((1, D_A)),
                  _const((N_HEADS, GROUP_DIM, GROUP_DIM)), _const((1, D_B)), _const((1, D_B))],
        out_specs=[_rows(ts, D_Z), _rows(ts, D_MODEL), _rows(ts, D_MODEL), _rows(ts, D_MODEL), _rows(ts, D_MODEL)],
        out_shape=[jax.ShapeDtypeStruct((s_len, D_Z), F32), bf16_rows, f32_rows, f32_rows, bf16_rows],
        scratch_shapes=[pltpu.VMEM((ts, D_A), F32), pltpu.VMEM((HALO, D_B), F32), pltpu.VMEM((D_MODEL, D_Z), BF16)],
        compiler_params=_params(),
    )(x, mod6, n1pre, n1post, n2pre, win_g, wout_g, w_spatial, bsp_full, gain, bias, w_pool, b_pool, pool_scale)


def _mixer_forward_tile(za, wc, bsp_ref, gain, bias, mixed_ref):
    ga, dga = _gelu_parts(za)
    u = ga[:, :D_A]
    v = ga[:, D_A:]
    mu = _rowmean(v)
    vc = v - mu
    rstd = lax.rsqrt(_rowmean(vc * vc) + EPS)
    vhat = vc * rstd
    vn = (vhat * gain + bias).astype(BF16)
    ts = za.shape[0]
    for k in range(ts // CHUNK):
        for h in range(N_HEADS):
            blk = vn[k * CHUNK:(k + 1) * CHUNK, h * HEAD_DIM:(h + 1) * HEAD_DIM]
            mixed_ref[k * CHUNK:(k + 1) * CHUNK, h * HEAD_DIM:(h + 1) * HEAD_DIM] = (
                _dot(wc[h], blk) + bsp_ref[:, h * HEAD_DIM:(h + 1) * HEAD_DIM])
    return u, vhat, rstd, vn, dga


def _fwd_fc1(h2, fc1_g, ts):
    s_len = h2.shape[0]
    cs = D_FF // N_CHIPS

    def body(h_ref, w_ref, q_ref):
        hb = h_ref[...]
        for j in range(N_CHIPS):
            p = jnp.maximum(_dot(hb, w_ref[j]), 0.0)
            q_ref[:, j * cs:(j + 1) * cs] = (p * p).astype(BF16)

    return pl.pallas_call(
        body, name="fwd_fc1", grid=(s_len // ts,),
        in_specs=[_rows(ts, D_MODEL), _VMEM],
        out_specs=_rows(ts, D_FF),
        out_shape=jax.ShapeDtypeStruct((s_len, D_FF), BF16),
        compiler_params=_params(),
    )(h2, fc1_g)


def _fwd_fc2_loss(q, x1, target, mod6, n2post, fc2_g, ts):
    s_len = q.shape[0]
    rs = D_FF // N_CHIPS

    def body(q_ref, x1_ref, t_ref, mod_ref, g_ref, w_ref, dy_ref, df_ref, loss_ref, s_ref):
        _zero_on_first_step(loss_ref, s_ref)
        gate_gain = mod_ref[5:6, :] * g_ref[...]
        f = _dot(q_ref[:, 0:rs], w_ref[0])
        for j in range(1, N_CHIPS):
            f = f + _dot(q_ref[:, j * rs:(j + 1) * rs], w_ref[j])
        r4 = lax.rsqrt(_rowmean(f * f) + EPS)
        fh = f * r4
        err = (x1_ref[...] + fh * gate_gain) - t_ref[...]
        loss_ref[...] += 0.5 * jnp.sum(_rowmean(err * err), axis=0, keepdims=True)
        dy = err * (1.0 / D_MODEL)
        dy_ref[...] = dy
        s_ref[...] += _colsum(dy * fh)
        gh = dy * gate_gain
        df_ref[...] = (r4 * (gh - fh * _rowmean(gh * fh))).astype(BF16)

    return pl.pallas_call(
        body, name="fwd_fc2_loss", grid=(s_len // ts,),
        in_specs=[_rows(ts, D_FF), _rows(ts, D_MODEL), _rows(ts, D_MODEL), _const((N_MOD, D_MODEL)),
                  _const((1, D_MODEL)), _VMEM],
        out_specs=[_rows(ts, D_MODEL), _rows(ts, D_MODEL), _const((1, 1)), _const((1, D_MODEL))],
        out_shape=[jax.ShapeDtypeStruct((s_len, D_MODEL), F32), jax.ShapeDtypeStruct((s_len, D_MODEL), BF16),
                   jax.ShapeDtypeStruct((1, 1), F32), jax.ShapeDtypeStruct((1, D_MODEL), F32)],
        compiler_params=_params(),
    )(q, x1, target, mod6, n2post, fc2_g)


def _join_w_in_on_first_step(win_ref, full_ref):
    cs = D_Z // N_CHIPS

    @pl.when(pl.program_id(0) == 0)
    def _():
        for j in range(N_CHIPS):
            full_ref[:, j * cs:(j + 1) * cs] = win_ref[j]


def _zero_on_first_step(*refs):
    @pl.when(pl.program_id(0) == 0)
    def _():
        for ref in refs:
            ref[...] = jnp.zeros_like(ref)


def _on_last_step(fn):
    pl.when(pl.program_id(0) == pl.num_programs(0) - 1)(fn)


def _store_shard_on_last_step(acc_ref, hbm_ref, sem, j):
    _on_last_step(lambda: pltpu.make_async_copy(acc_ref.at[j], hbm_ref.at[j], sem.at[j]).start())


def _wait_stores_on_last_step(*stores):
    def wait_all():
        for acc_ref, hbm_ref, sem in stores:
            for j in range(N_CHIPS):
                pltpu.make_async_copy(acc_ref.at[j], hbm_ref.at[j], sem.at[j]).wait()

    _on_last_step(wait_all)


def _bwd_fc2(df, q, fc2_g, ts):
    s_len = df.shape[0]
    cs = D_FF // N_CHIPS

    def body(df_ref, q_ref, w_ref, dp_ref, dw_hbm, dw_ref, dw_sem):
        _zero_on_first_step(dw_ref)
        dfb = df_ref[...]
        df2 = dfb * 2.0
        for j in range(N_CHIPS):
            qb = q_ref[:, j * cs:(j + 1) * cs]
            dw_ref[j] += _dot_tn(qb, dfb).reshape(2, cs // 2, D_MODEL)
            _store_shard_on_last_step(dw_ref, dw_hbm, dw_sem, j)
            dq2 = _dot_nt(df2, w_ref[j])
            dp_ref[:, j * cs:(j + 1) * cs] = (dq2 * jnp.sqrt(qb.astype(F32))).astype(BF16)
        _wait_stores_on_last_step((dw_ref, dw_hbm, dw_sem))

    dw_shape = (N_CHIPS, 2, cs // 2, D_MODEL)
    return pl.pallas_call(
        body, name="bwd_fc2", grid=(s_len // ts,),
        in_specs=[_rows(ts, D_MODEL), _rows(ts, D_FF), _VMEM],
        out_specs=[_rows(ts, D_FF), _ANY],
        out_shape=[jax.ShapeDtypeStruct((s_len, D_FF), BF16), jax.ShapeDtypeStruct(dw_shape, F32)],
        scratch_shapes=[pltpu.VMEM(dw_shape, F32), pltpu.SemaphoreType.DMA((N_CHIPS,))],
        compiler_params=_params(),
    )(df, q, fc2_g)


def _bwd_fc1_out(dp, dy, x1, mix, h2, ycat, mod6, n2pre, n1post, fc1_g, wout_g, dep, ts):
    s_len = dp.shape[0]
    cs = D_FF // N_CHIPS
    rs = D_MODEL // N_CHIPS

    def body(dp_ref, dy_ref, x1_ref, mix_ref, h2_ref, yc_ref, mod_ref, g2_ref, g1_ref, w1_ref, wo_ref, dep_ref,
             dx1_ref, dyc_ref, dshift2_ref, da2_ref, s1_ref, dw1_hbm, dwo_hbm, dw1_ref, dwo_ref, dw1_sem, dwo_sem):
        _zero_on_first_step(dshift2_ref, da2_ref, s1_ref, dw1_ref, dwo_ref)
        h2b = h2_ref[...]
        dh2 = None
        for j in range(N_CHIPS):
            dpb = dp_ref[:, j * cs:(j + 1) * cs]
            dw1_ref[j] += _dot_tn(h2b, dpb).reshape(2, D_MODEL // 2, cs)
            _store_shard_on_last_step(dw1_ref, dw1_hbm, dw1_sem, j)
            part = _dot_nt(dpb, w1_ref[j])
            dh2 = part if dh2 is None else dh2 + part
        x1 = x1_ref[...]
        r3 = lax.rsqrt(_rowmean(x1 * x1) + EPS)
        xh = x1 * r3
        a2 = g2_ref[...] * (1.0 + mod_ref[4:5, :])
        dshift2_ref[...] += _colsum(dh2)
        da2_ref[...] += _colsum(dh2 * xh)
        dxh = dh2 * a2
        dx1 = dy_ref[...] + r3 * (dxh - xh * _rowmean(dxh * xh))
        dx1_ref[...] = dx1

        mix = mix_ref[...]
        r2 = lax.rsqrt(_rowmean(mix * mix) + EPS)
        mh = mix * r2
        s1_ref[...] += _colsum(dx1 * mh)
        gh = dx1 * (mod_ref[2:3, :] * g1_ref[...])
        dmix = (r2 * (gh - mh * _rowmean(gh * mh))).astype(BF16)
        dwo_ref[...] += _dot_tn(yc_ref[...], dmix).reshape(N_CHIPS, 2, rs // 2, D_MODEL)
        for j in range(N_CHIPS):
            _store_shard_on_last_step(dwo_ref, dwo_hbm, dwo_sem, j)
            dyc_ref[:, j * rs:(j + 1) * rs] = _dot_nt(dmix, wo_ref[j])
        _wait_stores_on_last_step((dw1_ref, dw1_hbm, dw1_sem), (dwo_ref, dwo_hbm, dwo_sem))

    vec = jax.ShapeDtypeStruct((1, D_MODEL), F32)
    dw1_shape = (N_CHIPS, 2, D_MODEL // 2, cs)
    dwo_shape = (N_CHIPS, 2, rs // 2, D_MODEL)
    return pl.pallas_call(
        body, name="bwd_fc1_out", grid=(s_len // ts,),
        in_specs=[_rows(ts, D_FF), _rows(ts, D_MODEL), _rows(ts, D_MODEL), _rows(ts, D_MODEL), _rows(ts, D_MODEL),
                  _rows(ts, D_MODEL), _const((N_MOD, D_MODEL)), _const((1, D_MODEL)), _const((1, D_MODEL)), _VMEM,
                  _VMEM, _ANY],
        out_specs=[_rows(ts, D_MODEL), _rows(ts, D_MODEL)] + [_const((1, D_MODEL))] * 3 + [_ANY, _ANY],
        out_shape=[jax.ShapeDtypeStruct((s_len, D_MODEL), F32), jax.ShapeDtypeStruct((s_len, D_MODEL), F32),
                   vec, vec, vec, jax.ShapeDtypeStruct(dw1_shape, F32), jax.ShapeDtypeStruct(dwo_shape, F32)],
        scratch_shapes=[pltpu.VMEM(dw1_shape, F32), pltpu.VMEM(dwo_shape, F32), pltpu.SemaphoreType.DMA((N_CHIPS,)),
                        pltpu.SemaphoreType.DMA((N_CHIPS,))],
        compiler_params=_params(),
    )(dp, dy, x1, mix, h2, ycat, mod6, n2pre, n1post, fc1_g, wout_g, dep)


def _mixer_bwd(z, dyc, w_spatial, bsp_full, gain, bias, w_pool, b_pool, pool_scale, dep, ts):
    s_len = z.shape[0]
    nb = ts // HALO
    last = s_len // HALO - 1
    te = ts + HALO

    def body(z_ref, zprev_ref, znext_ref, dyc_ref, dynext_ref, ws_ref, bsp_ref, gain_ref, bias_ref, wp_ref, bp_ref,
             ps_ref, dep_ref, dz_ref, dws_ref, dbsp_ref, dgain_ref, dbias_ref, dwp_ref, dbp_ref, dps_ref, mixed_ref,
             dvn_ref):
        i = pl.program_id(0)

        @pl.when(i == 0)
        def _():
            for ref in (dws_ref, dbsp_ref, dgain_ref, dbias_ref, dwp_ref, dbp_ref, dps_ref):
                ref[...] = jnp.zeros_like(ref)

        wc = _tril_weights(ws_ref)
        gain = gain_ref[...]
        u, vhat, rstd, vn, dga = _mixer_forward_tile(z_ref[:, :2 * D_A], wc, bsp_ref, gain, bias_ref[...], mixed_ref)
        dya = dyc_ref[:, :D_A]
        du = dya * mixed_ref[...]
        dmixed = dya * u
        dmb = dmixed.astype(BF16)
        dm_sum = dmixed[0:CHUNK, :]
        for k in range(1, ts // CHUNK):
            dm_sum = dm_sum + dmixed[k * CHUNK:(k + 1) * CHUNK, :]
        r_idx = lax.broadcasted_iota(jnp.int32, (CHUNK, CHUNK), 0)
        s_idx = lax.broadcasted_iota(jnp.int32, (CHUNK, CHUNK), 1)
        causal = (s_idx <= r_idx).astype(F32)
        for h in range(N_HEADS):
            lanes = slice(h * HEAD_DIM, (h + 1) * HEAD_DIM)
            dbsp_ref[h] += jnp.sum(dm_sum[:, lanes], axis=1, keepdims=True)
            acc = None
            for k in range(ts // CHUNK):
                rows = slice(k * CHUNK, (k + 1) * CHUNK)
                t = _dot_nt(dmb[rows, lanes], vn[rows, lanes])
                acc = t if acc is None else acc + t
                dvn_ref[rows, lanes] = _dot_tn(wc[h], dmb[rows, lanes])
            dws_ref[h] += acc * causal
        dvn = dvn_ref[...]
        dgain_ref[...] += _colsum(dvn * vhat)
        dbias_ref[...] += _colsum(dvn)
        dvh = dvn * gain
        dv = rstd * (dvh - _rowmean(dvh) - vhat * _rowmean(dvh * vhat))
        dz_ref[:, :D_A] = (du * dga[:, :D_A]).astype(BF16)
        dz_ref[:, D_A:2 * D_A] = (dv * dga[:, D_A:]).astype(BF16)

        zb = z_ref[:, 2 * D_A:]
        prev = jnp.where(i == 0, 0.0, zprev_ref[...])
        zb_ext = jnp.concatenate([zb, znext_ref[...]], axis=0)
        sums = _causal_window_sums(jnp.concatenate([prev, zb_ext], axis=0))
        pos, inv_counts = _window_counts(i * ts, te)
        dyb_ext = jnp.concatenate([dyc_ref[:, D_A:], dynext_ref[...]], axis=0)
        dlin_ext = dyb_ext * ps_ref[...]
        dbp_ref[...] += _colsum(dlin_ext[:ts, :])
        scaled = []
        ddiffs = []
        lins = []
        for g in range(len(POOL_WINDOWS)):
            lanes = slice(g * GROUP_DIM, (g + 1) * GROUP_DIM)
            diff = (sums[g][HALO:, :] * inv_counts[g] - zb_ext[:, lanes]).astype(BF16)
            wpb = wp_ref[g].astype(BF16)
            dlb = dlin_ext[:, lanes].astype(BF16)
            lins.append(_dot(diff[:ts, :], wpb) + bp_ref[:, lanes])
            dwp_ref[g] += _dot_tn(diff[:ts, :], dlb[:ts, :])
            dd = _dot_nt(dlb, wpb)
            ddiffs.append(dd)
            scaled.append(jnp.where(pos < float(s_len), dd * inv_counts[g], 0.0))
        dps_ref[...] += _colsum(dyb_ext[:ts, :] * jnp.concatenate(lins, axis=1))
        back = _anticausal_window_sums(jnp.concatenate(scaled, axis=1))
        for g in range(len(POOL_WINDOWS)):
            dz_ref[:, 2 * D_A + g * GROUP_DIM:2 * D_A + (g + 1) * GROUP_DIM] = (
                back[g][:ts, :] - ddiffs[g][:ts, :]).astype(BF16)

    sq = jax.ShapeDtypeStruct((N_HEADS, CHUNK, CHUNK), F32)
    vec = jax.ShapeDtypeStruct((1, D_A), F32)
    return pl.pallas_call(
        body, name="mixer_bwd", grid=(s_len // ts,),
        in_specs=[_rows(ts, D_Z),
                  pl.BlockSpec((HALO, D_B), lambda i: (jnp.maximum(i * nb - 1, 0), 2)),
                  pl.BlockSpec((HALO, D_B), lambda i: (jnp.minimum((i + 1) * nb, last), 2)),
                  _rows(ts, D_MODEL),
                  pl.BlockSpec((HALO, D_B), lambda i: (jnp.minimum((i + 1) * nb, last), 1)),
                  _const((N_HEADS, CHUNK, CHUNK)), _const((CHUNK, D_A)), _const((1, D_A)), _const((1, D_A)),
                  _const((N_HEADS, GROUP_DIM, GROUP_DIM)), _const((1, D_B)), _const((1, D_B)), _ANY],
        out_specs=[_rows(ts, D_Z), _const((N_HEADS, CHUNK, CHUNK)), _const((N_HEADS, CHUNK, 1)), _const((1, D_A)),
                   _const((1, D_A)), _const((N_HEADS, GROUP_DIM, GROUP_DIM)), _const((1, D_B)), _const((1, D_B))],
        out_shape=[jax.ShapeDtypeStruct((s_len, D_Z), BF16), sq, jax.ShapeDtypeStruct((N_HEADS, CHUNK, 1), F32), vec,
                   vec, sq, vec, vec],
        scratch_shapes=[pltpu.VMEM((ts, D_A), F32), pltpu.VMEM((ts, D_A), F32)],
        compiler_params=_params(),
    )(z, z, z, dyc, dyc, w_spatial, bsp_full, gain, bias, w_pool, b_pool, pool_scale, dep)


def _bwd_in(dz, dx1, x, mod6, n1pre, win_g, dep, ts):
    s_len = x.shape[0]
    cs = D_Z // N_CHIPS

    def body(dz_ref, dx1_ref, x_ref, mod_ref, g_ref, w_ref, dep_ref, gx_ref, dshift_ref, da_ref, dw_hbm, dw_ref,
             wfull_ref, dw_sem):
        _zero_on_first_step(dshift_ref, da_ref, dw_ref)
        _join_w_in_on_first_step(w_ref, wfull_ref)
        xv = x_ref[...]
        r = lax.rsqrt(_rowmean(xv * xv) + EPS)
        xh = xv * r
        h1b = (xh * (g_ref[...] * (1.0 + mod_ref[1:2, :])) + mod_ref[0:1, :]).astype(BF16)
        dzb = dz_ref[...]
        dw = _dot_tn(h1b, dzb)
        for j in range(N_CHIPS):
            dw_ref[j] += dw[:, j * cs:(j + 1) * cs].reshape(2, D_MODEL // 2, cs)
        dh = _dot_nt(dzb, wfull_ref[...])
        a1 = g_ref[...] * (1.0 + mod_ref[1:2, :])
        dshift_ref[...] += _colsum(dh)
        da_ref[...] += _colsum(dh * xh)
        dxh = dh * a1
        gx_ref[...] = dx1_ref[...] + r * (dxh - xh * _rowmean(dxh * xh))
        for j in range(N_CHIPS):
            _store_shard_on_last_step(dw_ref, dw_hbm, dw_sem, j)
        _wait_stores_on_last_step((dw_ref, dw_hbm, dw_sem))

    vec = jax.ShapeDtypeStruct((1, D_MODEL), F32)
    dw_shape = (N_CHIPS, 2, D_MODEL // 2, cs)
    return pl.pallas_call(
        body, name="bwd_in", grid=(s_len // ts,),
        in_specs=[_rows(ts, D_Z), _rows(ts, D_MODEL), _rows(ts, D_MODEL), _const((N_MOD, D_MODEL)),
                  _const((1, D_MODEL)), _VMEM, _ANY],
        out_specs=[_rows(ts, D_MODEL), _const((1, D_MODEL)), _const((1, D_MODEL)), _ANY],
        out_shape=[jax.ShapeDtypeStruct((s_len, D_MODEL), F32), vec, vec, jax.ShapeDtypeStruct(dw_shape, F32)],
        scratch_shapes=[pltpu.VMEM(dw_shape, F32), pltpu.VMEM((D_MODEL, D_Z), BF16),
                        pltpu.SemaphoreType.DMA((N_CHIPS,))],
        compiler_params=_params(),
    )(dz, dx1, x, mod6, n1pre, win_g, dep)


def _adamw_math(w, g, m, v):
    m = ADAM_B1 * m + (1.0 - ADAM_B1) * g
    v = ADAM_B2 * v + (1.0 - ADAM_B2) * (g * g)
    m_hat = m / (1.0 - ADAM_B1 ** ADAM_STEP)
    v_hat = v / (1.0 - ADAM_B2 ** ADAM_STEP)
    delta = -ADAM_LR * (m_hat / (jnp.sqrt(v_hat) + ADAM_EPS) + ADAM_WD * w)
    return delta, m, v


def _adamw(gs, ws, ms, vs, name, steps):
    n = len(ws)

    def body(*refs):
        for i in range(n):
            g_ref, w_ref, m_ref, v_ref = refs[4 * i:4 * i + 4]
            d, nm, nv = _adamw_math(w_ref[...], g_ref[...], m_ref[...], v_ref[...])
            for ref, val in zip(refs[4 * n + 3 * i:4 * n + 3 * i + 3], (d, nm, nv)):
                ref[...] = val

    specs = [_rows(w.shape[0] // steps, w.shape[1]) for w in ws]
    out = pl.pallas_call(
        body, name=name, grid=(steps,),
        in_specs=[s for s in specs for _ in range(4)], out_specs=[s for s in specs for _ in range(3)],
        out_shape=[jax.ShapeDtypeStruct(w.shape, F32) for w in ws for _ in range(3)],
        compiler_params=_params(),
    )(*[a for quad in zip(gs, ws, ms, vs) for a in quad])
    return [out[3 * i:3 * i + 3] for i in range(n)]


def _ada_grad_adamw(sc_t, dmod_shard, w, m, v, tr):
    rows, cols = w.shape

    def body(s_ref, dm_ref, w_ref, m_ref, v_ref, g_ref, d_ref, nm_ref, nv_ref):
        g = s_ref[:, 0:1] * dm_ref[0:1, :]
        for b in range(1, N_DEV):
            g = g + s_ref[:, b:b + 1] * dm_ref[b:b + 1, :]
        g_ref[...] = g
        d, nm, nv = _adamw_math(w_ref[...], g, m_ref[...], v_ref[...])
        d_ref[...] = d
        nm_ref[...] = nm
        nv_ref[...] = nv

    spec = _rows(tr, cols)
    shape = jax.ShapeDtypeStruct((rows, cols), F32)
    return pl.pallas_call(
        body, name="ada_grad_adamw", grid=(rows // tr,),
        in_specs=[_rows(tr, N_DEV), _const((N_DEV, cols)), spec, spec, spec],
        out_specs=[spec] * 4, out_shape=[shape] * 4, compiler_params=_params(),
    )(sc_t, dmod_shard, w, m, v)


def _mod_grads(da1, dshift1, s1, da2, dshift2, s2, mod6, n1pre, n1post, n2pre, n2post):
    def body(da1_ref, ds1_ref, s1_ref, da2_ref, ds2_ref, s2_ref, mod_ref, n1_ref, p1_ref, n2_ref, p2_ref, dmod_ref,
             dn_ref):
        dmod_ref[0:1, :] = ds1_ref[...]
        dmod_ref[1:2, :] = da1_ref[...] * n1_ref[...]
        dmod_ref[2:3, :] = s1_ref[...] * p1_ref[...]
        dmod_ref[3:4, :] = ds2_ref[...]
        dmod_ref[4:5, :] = da2_ref[...] * n2_ref[...]
        dmod_ref[5:6, :] = s2_ref[...] * p2_ref[...]
        dn_ref[0:1, :] = da1_ref[...] * (1.0 + mod_ref[1:2, :])
        dn_ref[1:2, :] = s1_ref[...] * mod_ref[2:3, :]
        dn_ref[2:3, :] = da2_ref[...] * (1.0 + mod_ref[4:5, :])
        dn_ref[3:4, :] = s2_ref[...] * mod_ref[5:6, :]

    return pl.pallas_call(
        body, name="mod_grads",
        out_shape=[jax.ShapeDtypeStruct((N_MOD, D_MODEL), F32), jax.ShapeDtypeStruct((4, D_MODEL), F32)],
    )(da1, dshift1, s1, da2, dshift2, s2, mod6, n1pre, n1post, n2pre, n2post)


def _position():
    x, y, c = lax.axis_index("x"), lax.axis_index("y"), lax.axis_index("c")
    return x, y, c


def _flip(v, bit):
    return 1 - v if bit else v


def _peer(x, y, c, k):
    return (_flip(x, k & 4), _flip(y, k & 2), _flip(c, k & 1))


def _remote(src, dst, send_sem, recv_sem, device):
    return pltpu.make_async_remote_copy(src_ref=src, dst_ref=dst, send_sem=send_sem, recv_sem=recv_sem,
                                        device_id=device, device_id_type=MESH)


def _mod_exchange(c_row, w_ada_shard, b_ada_row, ws, early):
    cs = w_ada_shard.shape[1]
    n = len(ws)

    def body(c_ref, w_hbm, b_ref, *refs):
        shards, (mod_ref, sc_ref), zones = refs[:n], refs[n:n + 2], refs[n + 2:2 * n + 2]
        rows_ref, w_ref, w_sem, send1, recv1, send2, recv2 = refs[2 * n + 2:2 * n + 9]
        wide, narrow = refs[2 * n + 9:3 * n + 9], refs[3 * n + 9:4 * n + 9]
        cast_load, cast_store = refs[4 * n + 9:]
        x, y, c = _position()
        me = 4 * x + 2 * y + c
        chip = 2 * x + y
        w_load = pltpu.make_async_copy(w_hbm, w_ref, w_sem)
        w_load.start()
        loads = [pltpu.make_async_copy(shards[i], wide[i], cast_load.at[i]) for i in range(n)]
        stores = [pltpu.make_async_copy(narrow[i], zones[i].at[chip], cast_store.at[i]) for i in range(n)]

        def cast(i):
            loads[i].wait()
            narrow[i][...] = wide[i][...].astype(BF16)
            stores[i].start()

        for i in range(early):
            loads[i].start()
        cv = c_ref[...]
        sc_ref[me] = cv * jax.nn.sigmoid(cv)
        gather = [_remote(sc_ref.at[me], sc_ref.at[me], send1.at[k - 1], recv1.at[k - 1], _peer(x, y, c, k))
                  for k in range(1, N_DEV)]
        for cp in gather:
            cp.start()
        for i in range(early):
            cast(i)
        for i in range(early, n):
            loads[i].start()
        for k in range(1, N_DEV):
            px, py, pc = _peer(x, y, c, k)
            src = 4 * px + 2 * py + pc
            _remote(sc_ref.at[src], sc_ref.at[src], send1.at[k - 1], recv1.at[k - 1], (px, py, pc)).wait_recv()
        for cp in gather:
            cp.wait_send()
        sc_all = jnp.concatenate([sc_ref[b] for b in range(N_DEV)], axis=0)
        w_load.wait()
        part = _dot(sc_all.astype(BF16), w_ref[...].astype(BF16))
        part = part + b_ref[:, pl.ds(pl.multiple_of(chip * cs, LANES), cs)]
        for b in range(N_DEV):
            rows_ref[b] = part[b:b + 1, :]
        mod_ref[chip] = rows_ref[me]
        hand = []
        for k in (2, 4, 6):
            px, py, _ = _peer(x, y, c, k)
            hand.append(_remote(rows_ref.at[4 * px + 2 * py + c], mod_ref.at[chip], send2.at[k // 2 - 1],
                                recv2.at[k // 2 - 1], (px, py, c)))
        for cp in hand:
            cp.start()
        for i in range(early, n):
            cast(i)
        for k in (2, 4, 6):
            px, py, _ = _peer(x, y, c, k)
            pchip = 2 * px + py
            _remote(rows_ref.at[me], mod_ref.at[pchip], send2.at[k // 2 - 1], recv2.at[k // 2 - 1],
                    (px, py, c)).wait_recv()
        for cp in hand:
            cp.wait_send()
        for cp in stores:
            cp.wait()

    out = pl.pallas_call(
        body, name="mod_exchange",
        in_specs=[_VMEM, _ANY, _VMEM] + [_ANY] * n, out_specs=[_VMEM, _VMEM] + [_ANY] * n,
        out_shape=[jax.ShapeDtypeStruct((N_CHIPS, 1, cs), F32), jax.ShapeDtypeStruct((N_DEV, 1, D_MODEL), F32)]
        + [jax.ShapeDtypeStruct((N_CHIPS,) + w.shape, BF16) for w in ws],
        scratch_shapes=[pltpu.VMEM((N_DEV, 1, cs), F32), pltpu.VMEM(w_ada_shard.shape, F32), pltpu.SemaphoreType.DMA,
                        pltpu.SemaphoreType.DMA((N_DEV - 1,)),
                        pltpu.SemaphoreType.DMA((N_DEV - 1,)), pltpu.SemaphoreType.DMA((N_CHIPS - 1,)),
                        pltpu.SemaphoreType.DMA((N_CHIPS - 1,))]
        + [pltpu.VMEM(w.shape, F32) for w in ws] + [pltpu.VMEM(w.shape, BF16) for w in ws]
        + [pltpu.SemaphoreType.DMA((n,))] * 2,
        compiler_params=pltpu.CompilerParams(vmem_limit_bytes=VMEM_LIMIT),
    )(c_row, w_ada_shard, b_ada_row, *ws)
    return out[0], out[1], list(out[2:])


_HBM = pl.BlockSpec(memory_space=pltpu.HBM)
_SEM = pl.BlockSpec(memory_space=pltpu.SEMAPHORE)
_EFFECT = pltpu.SideEffectType.DATAFLOW_SIDE_EFFECTING
_CHIP_HOPS = (2, 4, 6)


def _in_hbm(a):
    return pltpu.with_memory_space_constraint(a, pltpu.HBM)


def _sems3():
    return pltpu.SemaphoreType.DMA((len(_CHIP_HOPS),))


def _ag_start(lands, after, name):
    n = len(lands)

    def body(*refs):
        zones = refs[:n]
        sends, recvs = refs[n + 1:2 * n + 1], refs[2 * n + 1:3 * n + 1]
        x, y, c = _position()
        chip = 2 * x + y
        for i in range(n):
            half = zones[i].shape[1] // 2
            mine = zones[i].at[chip, pl.ds(c * half, half)]
            for s, k in enumerate(_CHIP_HOPS):
                px, py, _ = _peer(x, y, c, k)
                _remote(mine, mine, sends[i].at[s], recvs[i].at[s], (px, py, c)).start()

    out = pl.pallas_call(
        body, name=name,
        in_specs=[_HBM] * n + [_ANY],
        out_specs=[_SEM] * (2 * n) + [_HBM] * n,
        out_shape=[_sems3()] * (2 * n) + [pltpu.HBM(z.shape, BF16) for z in lands],
        input_output_aliases={i: 2 * n + i for i in range(n)},
        compiler_params=pltpu.CompilerParams(has_side_effects=_EFFECT),
    )(*[_in_hbm(z) for z in lands], after)
    return [(out[2 * n + i], out[i], out[n + i]) for i in range(n)]


def _ag_pass(group, after, name):
    n = len(group)

    def body(*refs):
        zones = refs[:n]
        sends, recvs = refs[n:2 * n], refs[2 * n:3 * n]
        fsends, frecvs = refs[4 * n + 1:5 * n + 1], refs[5 * n + 1:6 * n + 1]
        x, y, c = _position()
        chip = 2 * x + y
        for i in range(n):
            half = zones[i].shape[1] // 2
            rows = pl.ds(c * half, half)
            for s, k in enumerate(_CHIP_HOPS):
                px, py, _ = _peer(x, y, c, k)
                landed = zones[i].at[2 * px + py, rows]
                _remote(landed, landed, sends[i].at[s], recvs[i].at[s], (px, py, c)).wait_recv()
                _remote(landed, landed, fsends[i].at[s], frecvs[i].at[s], (x, y, 1 - c)).start()
        for i in range(n):
            half = zones[i].shape[1] // 2
            mine = zones[i].at[chip, pl.ds(c * half, half)]
            for s, k in enumerate(_CHIP_HOPS):
                px, py, _ = _peer(x, y, c, k)
                _remote(mine, mine, sends[i].at[s], recvs[i].at[s], (px, py, c)).wait_send()

    out = pl.pallas_call(
        body, name=name,
        in_specs=[_HBM] * n + [_SEM] * (2 * n) + [_ANY],
        out_specs=[_HBM] * n + [_SEM] * (2 * n),
        out_shape=[pltpu.HBM(g[0].shape, BF16) for g in group] + [_sems3()] * (2 * n),
        input_output_aliases={i: i for i in range(n)},
        compiler_params=pltpu.CompilerParams(has_side_effects=_EFFECT),
    )(*[g[0] for g in group], *[g[1] for g in group], *[g[2] for g in group], after)
    return [(out[i], out[n + i], out[2 * n + i]) for i in range(n)]


def _ag_done(group, name):
    n = len(group)

    def body(*refs):
        lands = refs[:n]
        fsends, frecvs = refs[n:2 * n], refs[2 * n:3 * n]
        x, y, c = _position()
        for i in range(n):
            half = lands[i].shape[1] // 2
            for s, k in enumerate(_CHIP_HOPS):
                px, py, _ = _peer(x, y, c, k)
                sent = lands[i].at[2 * px + py, pl.ds(c * half, half)]
                got = lands[i].at[2 * px + py, pl.ds((1 - c) * half, half)]
                cp = _remote(sent, got, fsends[i].at[s], frecvs[i].at[s], (x, y, 1 - c))
                cp.wait_recv()
                cp.wait_send()

    out = pl.pallas_call(
        body, name=name,
        in_specs=[_HBM] * n + [_SEM] * (2 * n),
        out_specs=[_HBM] * n,
        out_shape=[pltpu.HBM(g[0].shape, BF16) for g in group],
        input_output_aliases={i: i for i in range(n)},
        compiler_params=pltpu.CompilerParams(has_side_effects=_EFFECT),
    )(*[g[0] for g in group], *[g[1] for g in group], *[g[2] for g in group])
    return list(out)


def _small_spread_start(slots, after):
    def body(z_ref, after_ref, sends, recvs, z_out):
        x, y, c = _position()
        mine = z_ref.at[2 * x + y]
        for s, k in enumerate(_CHIP_HOPS):
            px, py, _ = _peer(x, y, c, k)
            _remote(mine, mine, sends.at[s], recvs.at[s], (px, py, c)).start()

    sends, recvs, out = pl.pallas_call(
        body, name="small_spread_start",
        in_specs=[_HBM, _ANY], out_specs=[_SEM, _SEM, _HBM],
        out_shape=[_sems3(), _sems3(), pltpu.HBM(slots.shape, F32)],
        input_output_aliases={0: 2},
        compiler_params=pltpu.CompilerParams(has_side_effects=_EFFECT),
    )(_in_hbm(slots), after)
    return out, sends, recvs


def _small_spread_wait(slots, sends, recvs, afters):
    def body(z_ref, sends, recvs, *rest):
        x, y, c = _position()
        mine = z_ref.at[2 * x + y]
        for s, k in enumerate(_CHIP_HOPS):
            px, py, _ = _peer(x, y, c, k)
            cp = _remote(mine, z_ref.at[2 * px + py], sends.at[s], recvs.at[s], (px, py, c))
            cp.wait_recv()
            cp.wait_send()

    return pl.pallas_call(
        body, name="small_spread_wait",
        in_specs=[_HBM, _SEM, _SEM] + [_ANY] * len(afters), out_specs=_HBM, out_shape=pltpu.HBM(slots.shape, F32),
        input_output_aliases={0: 0},
        compiler_params=pltpu.CompilerParams(has_side_effects=_EFFECT),
    )(slots, sends, recvs, *afters)


def _sibling_sum(pgs, name, small=None):
    n = len(pgs)
    k = 0 if small is None else 1
    units = [(i, j) for i in range(n) for j in range(N_CHIPS)]

    def body(*refs):
        refs = list(refs)
        take = lambda count: [refs.pop(0) for _ in range(count)]
        ins, small_in = take(n), take(k)
        qbs, owns, slots_out = take(n), take(n), take(k)
        mine, other, stage, got = take(n), take(n), take(n), take(n)
        load_a, load_b, send, recv, keep_wire, keep_own = take(6)
        x, y, c = _position()
        chip = 2 * x + y
        if k:
            sib_ref, pair_send, pair_recv, keep_small = take(4)
            pair = _remote(small_in[0], sib_ref, pair_send, pair_recv, (x, y, 1 - c))
            pair.start()
        loads_a = [pltpu.make_async_copy(ins[i].at[j, 1 - c], other[i].at[j], load_a.at[u])
                   for u, (i, j) in enumerate(units)]
        loads_b = [pltpu.make_async_copy(ins[i].at[j, c], mine[i].at[j], load_b.at[u])
                   for u, (i, j) in enumerate(units)]
        for cp in loads_a + loads_b:
            cp.start()
        sent = []
        for u, (i, j) in enumerate(units):
            loads_a[u].wait()
            stage[i][j] = other[i][j].astype(BF16)
            cp = _remote(stage[i].at[j], got[i].at[j], send.at[u], recv.at[u], (x, y, 1 - c))
            cp.start()
            sent.append(cp)
        stores = []
        for u, (i, j) in enumerate(units):
            loads_b[u].wait()
            sent[u].wait_recv()
            q = mine[i][j] + got[i][j].astype(F32)
            mine[i][j] = q
            got[i][j] = q.astype(BF16)
            stores.append(pltpu.make_async_copy(got[i].at[j], qbs[i].at[j], keep_wire.at[u]))
            stores[-1].start()
        for i in range(n):
            stores.append(pltpu.make_async_copy(mine[i].at[chip], owns[i], keep_own.at[i]))
            stores[-1].start()
        if k:
            pair.wait()
            sib_ref[...] = small_in[0][...] + sib_ref[...]
            stores.append(pltpu.make_async_copy(sib_ref, slots_out[0].at[chip], keep_small))
            stores[-1].start()
        for cp in sent:
            cp.wait_send()
        for cp in stores:
            cp.wait()

    wire = [(N_CHIPS,) + p.shape[2:] for p in pgs]
    extra_out, extra_scratch = [], []
    if k:
        extra_out = [jax.ShapeDtypeStruct((N_CHIPS,) + small.shape, F32)]
        extra_scratch = [pltpu.VMEM(small.shape, F32)] + [pltpu.SemaphoreType.DMA] * 3
    out = pl.pallas_call(
        body, name=name, in_specs=[_ANY] * n + [_VMEM] * k, out_specs=[_ANY] * (2 * n + k),
        out_shape=[jax.ShapeDtypeStruct(w, BF16) for w in wire] + [jax.ShapeDtypeStruct(w[1:], F32) for w in wire]
        + extra_out,
        scratch_shapes=[pltpu.VMEM(w, F32) for w in wire] * 2 + [pltpu.VMEM(w, BF16) for w in wire] * 2
        + [pltpu.SemaphoreType.DMA((len(units),))] * 5 + [pltpu.SemaphoreType.DMA((n,))] + extra_scratch,
        compiler_params=pltpu.CompilerParams(vmem_limit_bytes=VMEM_LIMIT),
    )(*pgs, *([small] if k else []))
    return list(out[:n]), list(out[n:2 * n]), list(out[2 * n:])


def _rs_start(qbs, name, after=None):
    n = len(qbs)
    k = 0 if after is None else 1

    def body(*refs):
        outs, inboxes = refs[:n], refs[n:2 * n]
        sends, recvs = refs[2 * n + k:3 * n + k], refs[3 * n + k:4 * n + k]
        x, y, c = _position()
        chip = 2 * x + y
        for i in range(n):
            for s, hop in enumerate(_CHIP_HOPS):
                px, py, _ = _peer(x, y, c, hop)
                _remote(outs[i].at[2 * px + py], inboxes[i].at[chip], sends[i].at[s], recvs[i].at[s], (px, py, c)).start()

    inboxes = [_in_hbm(lax.empty(q.shape, BF16)) for q in qbs]
    out = pl.pallas_call(
        body, name=name,
        in_specs=[_HBM] * (2 * n) + [_ANY] * k,
        out_specs=[_SEM] * (2 * n) + [_HBM] * (2 * n),
        out_shape=[_sems3()] * (2 * n) + [pltpu.HBM(q.shape, BF16) for q in qbs] * 2,
        input_output_aliases={i: 2 * n + i for i in range(2 * n)},
        compiler_params=pltpu.CompilerParams(has_side_effects=_EFFECT),
    )(*[_in_hbm(q) for q in qbs], *inboxes, *([after] if k else []))
    return [(out[2 * n + i], out[3 * n + i], out[i], out[n + i]) for i in range(n)]


def _rs_wait(group, after, name):
    n = len(group)

    def body(*refs):
        outs, inboxes = refs[:n], refs[n:2 * n]
        sends, recvs = refs[2 * n:3 * n], refs[3 * n:4 * n]
        x, y, c = _position()
        for i in range(n):
            for s, k in enumerate(_CHIP_HOPS):
                px, py, _ = _peer(x, y, c, k)
                slot = 2 * px + py
                cp = _remote(outs[i].at[slot], inboxes[i].at[slot], sends[i].at[s], recvs[i].at[s], (px, py, c))
                cp.wait_recv()
                cp.wait_send()

    out = pl.pallas_call(
        body, name=name,
        in_specs=[_HBM] * (2 * n) + [_SEM] * (2 * n) + [_ANY],
        out_specs=[_HBM] * n,
        out_shape=[pltpu.HBM(g[1].shape, BF16) for g in group],
        input_output_aliases={n + i: i for i in range(n)},
        compiler_params=pltpu.CompilerParams(has_side_effects=_EFFECT),
    )(*[g[0] for g in group], *[g[1] for g in group], *[g[2] for g in group], *[g[3] for g in group], after)
    return list(out)


def _final_share(inboxes, owns, name):
    n = len(inboxes)
    units = [(i, s) for i in range(n) for s in range(len(_CHIP_HOPS))]

    def body(*refs):
        ins, mine, outs, landed, half = (refs[k * n:(k + 1) * n] for k in range(5))
        load, load_own, keep, send, recv = refs[5 * n:]
        x, y, c = _position()
        loads = []
        for u, (i, s) in enumerate(units):
            px, py, _ = _peer(x, y, c, _CHIP_HOPS[s])
            loads.append(pltpu.make_async_copy(ins[i].at[2 * px + py], landed[i].at[s], load.at[u]))
        loads_own = [pltpu.make_async_copy(mine[i], half[i], load_own.at[i]) for i in range(n)]
        for cp in loads + loads_own:
            cp.start()
        copies = []
        for i in range(n):
            for s in range(len(_CHIP_HOPS)):
                loads[len(_CHIP_HOPS) * i + s].wait()
            loads_own[i].wait()
            total = (landed[i][0].astype(F32) + landed[i][1].astype(F32)) + landed[i][2].astype(F32)
            half[i][...] = total + half[i][...]
            copies.append(pltpu.make_async_copy(half[i], outs[i].at[c], keep.at[i]))
            copies.append(_remote(half[i], outs[i].at[c], send.at[i], recv.at[i], (x, y, 1 - c)))
            for cp in copies[-2:]:
                cp.start()
        for i in range(n):
            theirs = outs[i].at[1 - c]
            _remote(theirs, theirs, send.at[i], recv.at[i], (x, y, 1 - c)).wait_recv()
        for i in range(n):
            copies[2 * i].wait()
            copies[2 * i + 1].wait_send()

    return pl.pallas_call(
        body, name=name, in_specs=[_ANY] * (2 * n), out_specs=[_ANY] * n,
        out_shape=[jax.ShapeDtypeStruct((2,) + o.shape, F32) for o in owns],
        scratch_shapes=[pltpu.VMEM((len(_CHIP_HOPS),) + o.shape, BF16) for o in owns]
        + [pltpu.VMEM(o.shape, F32) for o in owns]
        + [pltpu.SemaphoreType.DMA((len(units),))] + [pltpu.SemaphoreType.DMA((n,))] * 4,
        compiler_params=pltpu.CompilerParams(vmem_limit_bytes=VMEM_LIMIT),
    )(*inboxes, *owns)


_SMALL = (("b_ada", N_MOD * D_MODEL), ("norm1_pre", D_MODEL), ("norm1_post", D_MODEL), ("norm2_pre", D_MODEL),
          ("norm2_post", D_MODEL), ("w_spatial", N_HEADS * CHUNK * CHUNK), ("b_spatial", N_HEADS * CHUNK),
          ("ln_v_gain", D_A), ("ln_v_bias", D_A), ("w_pool", N_HEADS * GROUP_DIM * GROUP_DIM),
          ("b_pool", D_B), ("pool_scale", D_B))
_MOD_ROWS = N_MOD * D_MODEL // LANES


def _packed_rows(size):
    return -(-(size // LANES) // SUBLANES) * SUBLANES


def _pack(parts):
    out = []
    for name, size in _SMALL:
        a = parts[name].reshape(size // LANES, LANES)
        pad = _packed_rows(size) - a.shape[0]
        out.append(jnp.pad(a, ((0, pad), (0, 0))) if pad else a)
    return out


def _small_adamw(slots, ws, ms, vs):
    n = len(_SMALL)
    head = N_DEV * _MOD_ROWS

    def body(*refs):
        s_ref, w, m, v = refs[0], refs[1:1 + n], refs[1 + n:1 + 2 * n], refs[1 + 2 * n:1 + 3 * n]
        outs = refs[1 + 3 * n:1 + 7 * n]
        dmod_ref, loss_ref, t_ref = refs[1 + 7 * n:]
        t_ref[...] = ((s_ref[0] + s_ref[1]) + s_ref[2]) + s_ref[3]
        dmod_ref[...] = t_ref[0:head, :]
        loss_ref[...] = t_ref[t_ref.shape[0] - 8:, :]
        row = head
        for i, (_, size) in enumerate(_SMALL):
            if i == 0:
                g = t_ref[0:_MOD_ROWS, :]
                for b in range(1, N_DEV):
                    g = g + t_ref[b * _MOD_ROWS:(b + 1) * _MOD_ROWS, :]
            else:
                g = t_ref[row:row + size // LANES, :]
                row += _packed_rows(size)
            d, nm, nv = _adamw_math(w[i][...], g, m[i][...], v[i][...])
            for ref, val in zip(outs[4 * i:4 * i + 4], (g, d, nm, nv)):
                ref[...] = val

    each = [jax.ShapeDtypeStruct((size // LANES, LANES), F32) for _, size in _SMALL for _ in range(4)]
    out = pl.pallas_call(
        body, name="small_adamw",
        out_shape=each + [jax.ShapeDtypeStruct((head, LANES), F32), jax.ShapeDtypeStruct((SUBLANES, LANES), F32)],
        scratch_shapes=[pltpu.VMEM(slots.shape[1:], F32)],
        compiler_params=pltpu.CompilerParams(vmem_limit_bytes=VMEM_LIMIT),
    )(slots, *ws, *ms, *vs)
    return [out[4 * i:4 * i + 4] for i in range(n)], out[4 * n], out[4 * n + 1]


def kernel(x, c, w_ada, b_ada, norm1_pre, norm1_post, w_in, w_spatial, b_spatial, ln_v_gain, ln_v_bias, w_pool, b_pool, pool_scale, w_out, norm2_pre, norm2_post, w_fc1, w_fc2, loss_target, m_w_ada, m_b_ada, m_norm1_pre, m_norm1_post, m_w_in, m_w_spatial, m_b_spatial, m_ln_v_gain, m_ln_v_bias, m_w_pool, m_b_pool, m_pool_scale, m_w_out, m_norm2_pre, m_norm2_post, m_w_fc1, m_w_fc2, v_w_ada, v_b_ada, v_norm1_pre, v_norm1_post, v_w_in, v_w_spatial, v_b_spatial, v_ln_v_gain, v_ln_v_bias, v_w_pool, v_b_pool, v_pool_scale, v_w_out, v_norm2_pre, v_norm2_post, v_w_fc1, v_w_fc2):
    weights = dict(w_ada=w_ada, b_ada=b_ada, norm1_pre=norm1_pre, norm1_post=norm1_post, w_in=w_in,
                   w_spatial=w_spatial, b_spatial=b_spatial, ln_v_gain=ln_v_gain, ln_v_bias=ln_v_bias, w_pool=w_pool,
                   b_pool=b_pool, pool_scale=pool_scale, w_out=w_out, norm2_pre=norm2_pre, norm2_post=norm2_post,
                   w_fc1=w_fc1, w_fc2=w_fc2)
    m_old = dict(w_ada=m_w_ada, b_ada=m_b_ada, norm1_pre=m_norm1_pre, norm1_post=m_norm1_post, w_in=m_w_in,
                 w_spatial=m_w_spatial, b_spatial=m_b_spatial, ln_v_gain=m_ln_v_gain, ln_v_bias=m_ln_v_bias,
                 w_pool=m_w_pool, b_pool=m_b_pool, pool_scale=m_pool_scale, w_out=m_w_out, norm2_pre=m_norm2_pre,
                 norm2_post=m_norm2_post, w_fc1=m_w_fc1, w_fc2=m_w_fc2)
    v_old = dict(w_ada=v_w_ada, b_ada=v_b_ada, norm1_pre=v_norm1_pre, norm1_post=v_norm1_post, w_in=v_w_in,
                 w_spatial=v_w_spatial, b_spatial=v_b_spatial, ln_v_gain=v_ln_v_gain, ln_v_bias=v_ln_v_bias,
                 w_pool=v_w_pool, b_pool=v_b_pool, pool_scale=v_pool_scale, w_out=v_w_out, norm2_pre=v_norm2_pre,
                 norm2_post=v_norm2_post, w_fc1=v_w_fc1, w_fc2=v_w_fc2)
    order = ("w_ada", "b_ada", "norm1_pre", "norm1_post", "w_in", "w_spatial", "b_spatial", "ln_v_gain", "ln_v_bias",
             "w_pool", "b_pool", "pool_scale", "w_out", "norm2_pre", "norm2_post", "w_fc1", "w_fc2")
    mx, my, mc = _position()
    me = 4 * mx + 2 * my + mc
    chip = 2 * mx + my
    row = lambda a: a.reshape(1, -1)

    xs, target = x[0], loss_target[0]
    n1pre, n1post, n2pre, n2post = row(norm1_pre), row(norm1_post), row(norm2_pre), row(norm2_post)
    mixer = (w_spatial, jnp.repeat(b_spatial.T, HEAD_DIM, axis=1), row(ln_v_gain), row(ln_v_bias), w_pool,
             row(b_pool), row(pool_scale))
    ts_big, ts_mid = 512, 256

    mod4, sc_all, lands = _mod_exchange(c, w_ada, row(b_ada), [w_in, w_out, w_fc1, w_fc2], 2)
    mod6 = mod4.reshape(N_MOD, D_MODEL)
    ag = _ag_start(lands, mod4, "ag_start")

    win_g, wout_g = _ag_done(_ag_pass([ag[0], ag[1]], ag[2][0], "ag_pass_mix"), "ag_done_mix")
    z, ycat, mix, x1, h2 = _fwd_mix(xs, mod6, n1pre, n1post, n2pre, win_g, wout_g, *mixer, ts_big)
    (fc1_g,) = _ag_done(_ag_pass([ag[2]], h2, "ag_pass_fc1"), "ag_done_fc1")
    q = _fwd_fc1(h2, fc1_g, ts_big)
    (fc2_g,) = _ag_done(_ag_pass([ag[3]], q, "ag_pass_fc2"), "ag_done_fc2")
    dy, df, loss, s2 = _fwd_fc2_loss(q, x1, target, mod6, n2post, fc2_g, ts_big)

    def reduce_start(partials, tag, small=None):
        wire, owns, slots = _sibling_sum(partials, "sibling_sum_" + tag, small)
        return _rs_start(wire, "rs_start_" + tag), owns, slots

    def reduce_finish(state, owns, names, tag, dep):
        inboxes = _rs_wait(state, dep, "rs_wait_" + tag)
        shards = _final_share(inboxes, owns, "final_share_" + tag)
        for n, g in zip(names, shards):
            grads[n] = g.reshape(weights[n].shape)
        updates = _adamw([grads[n] for n in names], [weights[n] for n in names], [m_old[n] for n in names],
                         [v_old[n] for n in names], "adamw_" + tag, 4)
        for n, (d, nm, nv) in zip(names, updates):
            deltas[n], new_m[n], new_v[n] = d, nm, nv

    grads, deltas, new_m, new_v = {}, {}, {}, {}
    dp, g_fc2 = _bwd_fc2(df, q, fc2_g, ts_big)
    state_fc2, owns_fc2, _ = reduce_start([g_fc2], "fc2")
    dx1, dyc, dshift2, da2, s1, g_fc1, g_out = _bwd_fc1_out(
        dp, dy, x1, mix, h2, ycat, mod6, n2pre, n1post, fc1_g, wout_g, state_fc2[0][0], ts_mid)
    state_mid, owns_mid, _ = reduce_start([g_fc1, g_out], "mid")
    dz, dws, dbsp, dgain, dbias, dwp, dbp, dps = _mixer_bwd(z, dyc, *mixer, state_mid[0][0], ts_big)
    grad_x, dshift1, da1, g_in = _bwd_in(dz, dx1, xs, mod6, n1pre, win_g, state_mid[0][0], ts_big)
    dmod6, dnorms = _mod_grads(da1, dshift1, s1, da2, dshift2, s2, mod6, n1pre, n1post, n2pre, n2post)

    parts = dict(b_ada=dmod6, norm1_pre=dnorms[0], norm1_post=dnorms[1], norm2_pre=dnorms[2], norm2_post=dnorms[3],
                 w_spatial=dws, b_spatial=dbsp, ln_v_gain=dgain, ln_v_bias=dbias, w_pool=dwp, b_pool=dbp,
                 pool_scale=dps)
    pieces = _pack(parts)
    slots = lax.dynamic_update_slice(jnp.zeros((N_DEV * _MOD_ROWS, LANES), F32), pieces[0], (me * _MOD_ROWS, 0))
    loss_tile = jnp.pad(loss, ((0, SUBLANES - 1), (0, LANES - 1)))
    wire_in, owns_in, pair_sum = _sibling_sum([g_in], "sibling_sum_in",
                                              jnp.concatenate([slots] + pieces[1:] + [loss_tile], axis=0))
    spread = _small_spread_start(pair_sum[0], wire_in[0])
    state_in = _rs_start(wire_in, "rs_start_in", spread[0])
    reduce_finish(state_fc2 + state_mid, owns_fc2 + owns_mid, ("w_fc2", "w_fc1", "w_out"), "mlp", state_in[0][0])
    flat = lambda d: [d[n].reshape(size // LANES, LANES) for n, size in _SMALL]
    small_out, dmod_all, loss_tile = _small_adamw(
        _small_spread_wait(*spread, [deltas[n] for n in ("w_fc2", "w_fc1", "w_out")]), flat(weights), flat(m_old),
        flat(v_old))
    loss = loss_tile[0, 0]
    for (n, _), (g, d, nm, nv) in zip(_SMALL, small_out):
        shape = weights[n].shape
        grads[n], deltas[n], new_m[n], new_v[n] = g.reshape(shape), d.reshape(shape), nm.reshape(shape), nv.reshape(shape)

    dmod_all = dmod_all.reshape(N_DEV, N_MOD * D_MODEL)
    cs = w_ada.shape[1]
    dmod_shard = lax.dynamic_slice(dmod_all, (0, chip * cs), (N_DEV, cs))
    sc_t = sc_all.reshape(N_DEV, D_MODEL).T
    grads["w_ada"], deltas["w_ada"], new_m["w_ada"], new_v["w_ada"] = _ada_grad_adamw(
        sc_t, dmod_shard, w_ada, m_w_ada, v_w_ada, 256)

    reduce_finish(state_in, owns_in, ("w_in",), "in", deltas["w_ada"])

    return (loss, grad_x[None], *[grads[n] for n in order], *[deltas[n] for n in order],
            *[new_m[n] for n in order], *[new_v[n] for n in order])
```

```python
import jax
import jax.numpy as jnp
from jax import lax
from jax.experimental import pallas as pl
from jax.experimental.pallas import tpu as pltpu

F32 = jnp.float32
BF16 = jnp.bfloat16
MESH = pl.DeviceIdType.MESH

D_MODEL = 1024
D_A = 512
D_B = 512
D_Z = 2 * D_A + D_B
N_HEADS = 4
HEAD_DIM = 128
CHUNK = 128
POOL_WINDOWS = (2, 4, 8, 16)
GROUP_DIM = 128
D_FF = 4096
N_MOD = 6
EPS = 1e-6
HALO = 16
N_CHIPS = 4
N_DEV = 8

ADAM_LR = 0.001
ADAM_B1 = 0.9
ADAM_B2 = 0.999
ADAM_EPS = 1e-08
ADAM_WD = 0.01
ADAM_STEP = 10

VMEM_LIMIT = 56 * 1024 * 1024
LANES = 128
SUBLANES = 8

_VMEM = pl.BlockSpec(memory_space=pltpu.VMEM)
_ANY = pl.BlockSpec(memory_space=pl.ANY)


def _params(n_grid_axes=1):
    return pltpu.CompilerParams(dimension_semantics=("arbitrary",) * n_grid_axes, vmem_limit_bytes=VMEM_LIMIT)


def _rows(ts, width):
    return pl.BlockSpec((ts, width), lambda i: (i, 0))


def _const(shape):
    return pl.BlockSpec(shape, lambda i: (0,) * len(shape))


def _dot(a, b):
    return jnp.dot(a, b, preferred_element_type=F32)


def _dot_nt(a, b):
    return lax.dot_general(a, b, (((1,), (1,)), ((), ())), preferred_element_type=F32)


def _dot_tn(a, b):
    return lax.dot_general(a, b, (((0,), (0,)), ((), ())), preferred_element_type=F32)


def _rowmean(v):
    return jnp.mean(v, axis=-1, keepdims=True)


def _colsum(v):
    return jnp.sum(v, axis=0, keepdims=True)


def _gelu_parts(z):
    k0 = 0.7978845608028654
    k1 = 0.044715
    z2 = z * z
    t = jnp.tanh(z * (k0 + (k0 * k1) * z2))
    u = 0.5 * t + 0.5
    g = z * u
    dg = u + (0.5 * z) * (1.0 - t * t) * (k0 + (3.0 * k0 * k1) * z2)
    return g, dg


def _tril_weights(ws_ref):
    r = lax.broadcasted_iota(jnp.int32, (CHUNK, CHUNK), 0)
    s = lax.broadcasted_iota(jnp.int32, (CHUNK, CHUNK), 1)
    mask = (s <= r).astype(F32)
    return [(ws_ref[h] * mask).astype(BF16) for h in range(N_HEADS)]


def _window_counts(first_row, n_rows):
    pos = (first_row + lax.broadcasted_iota(jnp.int32, (n_rows, 1), 0)).astype(F32)
    return pos, [1.0 / jnp.minimum(pos + 1.0, float(w)) for w in POOL_WINDOWS]


def _causal_window_sums(ext):
    out = []
    e = ext
    shift = 1
    for g in range(len(POOL_WINDOWS)):
        e = e + pltpu.roll(e, shift, 0)
        shift *= 2
        out.append(e[:, g * GROUP_DIM:(g + 1) * GROUP_DIM])
    return out


def _anticausal_window_sums(ext):
    n = ext.shape[0]
    out = []
    e = ext
    shift = 1
    for g in range(len(POOL_WINDOWS)):
        e = e + pltpu.roll(e, n - shift, 0)
        shift *= 2
        out.append(e[:, g * GROUP_DIM:(g + 1) * GROUP_DIM])
    return out


def _fwd_mix(x, mod6, n1pre, n1post, n2pre, win_g, wout_g, w_spatial, bsp_full, gain, bias, w_pool, b_pool, pool_scale, ts):
    s_len = x.shape[0]
    rs = D_MODEL // N_CHIPS

    def body(x_ref, mod_ref, g1pre_ref, g1post_ref, g2pre_ref, win_ref, wout_ref, ws_ref, bsp_ref, gain_ref,
             bias_ref, wp_ref, bp_ref, ps_ref, z_ref, y_ref, mix_ref, x1_ref, h2_ref, mixed_ref, prev_ref, wfull_ref):
        i = pl.program_id(0)
        _zero_on_first_step(prev_ref)
        _join_w_in_on_first_step(win_ref, wfull_ref)
        xv = x_ref[...]
        r = lax.rsqrt(_rowmean(xv * xv) + EPS)
        hb = ((xv * r) * (g1pre_ref[...] * (1.0 + mod_ref[1:2, :])) + mod_ref[0:1, :]).astype(BF16)
        z_ref[...] = _dot(hb, wfull_ref[...])

        wc = _tril_weights(ws_ref)
        u, _, _, _, _ = _mixer_forward_tile(z_ref[:, :2 * D_A], wc, bsp_ref, gain_ref[...], bias_ref[...], mixed_ref)
        y_ref[:, :D_A] = (u * mixed_ref[...]).astype(BF16)
        zb = z_ref[:, 2 * D_A:]
        sums = _causal_window_sums(jnp.concatenate([prev_ref[...], zb], axis=0))
        prev_ref[...] = zb[ts - HALO:, :]
        _, inv_counts = _window_counts(i * ts, ts)
        for g in range(len(POOL_WINDOWS)):
            lanes = slice(g * GROUP_DIM, (g + 1) * GROUP_DIM)
            diff = sums[g][HALO:, :] * inv_counts[g] - zb[:, lanes]
            lin = _dot(diff.astype(BF16), wp_ref[g].astype(BF16)) + bp_ref[:, lanes]
            y_ref[:, D_A + g * GROUP_DIM:D_A + (g + 1) * GROUP_DIM] = (lin * ps_ref[:, lanes]).astype(BF16)

        mix = None
        for j in range(N_CHIPS):
            part = _dot(y_ref[:, j * rs:(j + 1) * rs], wout_ref[j])
            mix = part if mix is None else mix + part
        mix_ref[...] = mix
        r2 = lax.rsqrt(_rowmean(mix * mix) + EPS)
        x1 = xv + (mix * r2) * (mod_ref[2:3, :] * g1post_ref[...])
        x1_ref[...] = x1
        r3 = lax.rsqrt(_rowmean(x1 * x1) + EPS)
        h2_ref[...] = ((x1 * r3) * (g2pre_ref[...] * (1.0 + mod_ref[4:5, :])) + mod_ref[3:4, :]).astype(BF16)

    vec = _const((1, D_MODEL))
    f32_rows = jax.ShapeDtypeStruct((s_len, D_MODEL), F32)
    bf16_rows = jax.ShapeDtypeStruct((s_len, D_MODEL), BF16)
    return pl.pallas_call(
        body, name="fwd_mix", grid=(s_len // ts,),
        in_specs=[_rows(ts, D_MODEL), _const((N_MOD, D_MODEL)), vec, vec, vec, _VMEM, _VMEM,
                  _const((N_HEADS, CHUNK, CHUNK)), _const((CHUNK, D_A)), _const((1, D_A)), _const((1, D_A)),
                  _const((N_HEADS, GROUP_DIM, GROUP_DIM)), _const((1, D_B)), _const((1, D_B))],
        out_specs=[_rows(ts, D_Z), _rows(ts, D_MODEL), _rows(ts, D_MODEL), _rows(ts, D_MODEL), _rows(ts, D_MODEL)],
        out_shape=[jax.ShapeDtypeStruct((s_len, D_Z), F32), bf16_rows, f32_rows, f32_rows, bf16_rows],
        scratch_shapes=[pltpu.VMEM((ts, D_A), F32), pltpu.VMEM((HALO, D_B), F32), pltpu.VMEM((D_MODEL, D_Z), BF16)],
        compiler_params=_params(),
    )(x, mod6, n1pre, n1post, n2pre, win_g, wout_g, w_spatial, bsp_full, gain, bias, w_pool, b_pool, pool_scale)


def _mixer_forward_tile(za, wc, bsp_ref, gain, bias, mixed_ref):
    ga, dga = _gelu_parts(za)
    u = ga[:, :D_A]
    v = ga[:, D_A:]
    mu = _rowmean(v)
    vc = v - mu
    rstd = lax.rsqrt(_rowmean(vc * vc) + EPS)
    vhat = vc * rstd
    vn = (vhat * gain + bias).astype(BF16)
    ts = za.shape[0]
    for k in range(ts // CHUNK):
        for h in range(N_HEADS):
            blk = vn[k * CHUNK:(k + 1) * CHUNK, h * HEAD_DIM:(h + 1) * HEAD_DIM]
            mixed_ref[k * CHUNK:(k + 1) * CHUNK, h * HEAD_DIM:(h + 1) * HEAD_DIM] = (
                _dot(wc[h], blk) + bsp_ref[:, h * HEAD_DIM:(h + 1) * HEAD_DIM])
    return u, vhat, rstd, vn, dga


def _fwd_fc1(h2, fc1_g, ts):
    s_len = h2.shape[0]
    cs = D_FF // N_CHIPS

    def body(h_ref, w_ref, q_ref):
        hb = h_ref[...]
        for j in range(N_CHIPS):
            p = jnp.maximum(_dot(hb, w_ref[j]), 0.0)
            q_ref[:, j * cs:(j + 1) * cs] = (p * p).astype(BF16)

    return pl.pallas_call(
        body, name="fwd_fc1", grid=(s_len // ts,),
        in_specs=[_rows(ts, D_MODEL), _VMEM],
        out_specs=_rows(ts, D_FF),
        out_shape=jax.ShapeDtypeStruct((s_len, D_FF), BF16),
        compiler_params=_params(),
    )(h2, fc1_g)


def _fwd_fc2_loss(q, x1, target, mod6, n2post, fc2_g, ts):
    s_len = q.shape[0]
    rs = D_FF // N_CHIPS

    def body(q_ref, x1_ref, t_ref, mod_ref, g_ref, w_ref, dy_ref, df_ref, loss_ref, s_ref):
        _zero_on_first_step(loss_ref, s_ref)
        gate_gain = mod_ref[5:6, :] * g_ref[...]
        f = _dot(q_ref[:, 0:rs], w_ref[0])
        for j in range(1, N_CHIPS):
            f = f + _dot(q_ref[:, j * rs:(j + 1) * rs], w_ref[j])
        r4 = lax.rsqrt(_rowmean(f * f) + EPS)
        fh = f * r4
        err = (x1_ref[...] + fh * gate_gain) - t_ref[...]
        loss_ref[...] += 0.5 * jnp.sum(_rowmean(err * err), axis=0, keepdims=True)
        dy = err * (1.0 / D_MODEL)
        dy_ref[...] = dy
        s_ref[...] += _colsum(dy * fh)
        gh = dy * gate_gain
        df_ref[...] = (r4 * (gh - fh * _rowmean(gh * fh))).astype(BF16)

    return pl.pallas_call(
        body, name="fwd_fc2_loss", grid=(s_len // ts,),
        in_specs=[_rows(ts, D_FF), _rows(ts, D_MODEL), _rows(ts, D_MODEL), _const((N_MOD, D_MODEL)),
                  _const((1, D_MODEL)), _VMEM],
        out_specs=[_rows(ts, D_MODEL), _rows(ts, D_MODEL), _const((1, 1)), _const((1, D_MODEL))],
        out_shape=[jax.ShapeDtypeStruct((s_len, D_MODEL), F32), jax.ShapeDtypeStruct((s_len, D_MODEL), BF16),
                   jax.ShapeDtypeStruct((1, 1), F32), jax.ShapeDtypeStruct((1, D_MODEL), F32)],
        compiler_params=_params(),
    )(q, x1, target, mod6, n2post, fc2_g)


def _join_w_in_on_first_step(win_ref, full_ref):
    cs = D_Z // N_CHIPS

    @pl.when(pl.program_id(0) == 0)
    def _():
        for j in range(N_CHIPS):
            full_ref[:, j * cs:(j + 1) * cs] = win_ref[j]


def _zero_on_first_step(*refs):
    @pl.when(pl.program_id(0) == 0)
    def _():
        for ref in refs:
            ref[...] = jnp.zeros_like(ref)


def _on_last_step(fn):
    pl.when(pl.program_id(0) == pl.num_programs(0) - 1)(fn)


def _store_shard_on_last_step(acc_ref, hbm_ref, sem, j):
    _on_last_step(lambda: pltpu.make_async_copy(acc_ref.at[j], hbm_ref.at[j], sem.at[j]).start())


def _wait_stores_on_last_step(*stores):
    def wait_all():
        for acc_ref, hbm_ref, sem in stores:
            for j in range(N_CHIPS):
                pltpu.make_async_copy(acc_ref.at[j], hbm_ref.at[j], sem.at[j]).wait()

    _on_last_step(wait_all)


def _bwd_fc2(df, q, fc2_g, ts):
    s_len = df.shape[0]
    cs = D_FF // N_CHIPS

    def body(df_ref, q_ref, w_ref, dp_ref, dw_hbm, dw_ref, dw_sem):
        _zero_on_first_step(dw_ref)
        dfb = df_ref[...]
        df2 = dfb * 2.0
        for j in range(N_CHIPS):
            qb = q_ref[:, j * cs:(j + 1) * cs]
            dw_ref[j] += _dot_tn(qb, dfb).reshape(2, cs // 2, D_MODEL)
            _store_shard_on_last_step(dw_ref, dw_hbm, dw_sem, j)
            dq2 = _dot_nt(df2, w_ref[j])
            dp_ref[:, j * cs:(j + 1) * cs] = (dq2 * jnp.sqrt(qb.astype(F32))).astype(BF16)
        _wait_stores_on_last_step((dw_ref, dw_hbm, dw_sem))

    dw_shape = (N_CHIPS, 2, cs // 2, D_MODEL)
    return pl.pallas_call(
        body, name="bwd_fc2", grid=(s_len // ts,),
        in_specs=[_rows(ts, D_MODEL), _rows(ts, D_FF), _VMEM],
        out_specs=[_rows(ts, D_FF), _ANY],
        out_shape=[jax.ShapeDtypeStruct((s_len, D_FF), BF16), jax.ShapeDtypeStruct(dw_shape, F32)],
        scratch_shapes=[pltpu.VMEM(dw_shape, F32), pltpu.SemaphoreType.DMA((N_CHIPS,))],
        compiler_params=_params(),
    )(df, q, fc2_g)


def _bwd_fc1_out(dp, dy, x1, mix, h2, ycat, mod6, n2pre, n1post, fc1_g, wout_g, dep, ts):
    s_len = dp.shape[0]
    cs = D_FF // N_CHIPS
    rs = D_MODEL // N_CHIPS

    def body(dp_ref, dy_ref, x1_ref, mix_ref, h2_ref, yc_ref, mod_ref, g2_ref, g1_ref, w1_ref, wo_ref, dep_ref,
             dx1_ref, dyc_ref, dshift2_ref, da2_ref, s1_ref, dw1_hbm, dwo_hbm, dw1_ref, dwo_ref, dw1_sem, dwo_sem):
        _zero_on_first_step(dshift2_ref, da2_ref, s1_ref, dw1_ref, dwo_ref)
        h2b = h2_ref[...]
        dh2 = None
        for j in range(N_CHIPS):
            dpb = dp_ref[:, j * cs:(j + 1) * cs]
            dw1_ref[j] += _dot_tn(h2b, dpb).reshape(2, D_MODEL // 2, cs)
            _store_shard_on_last_step(dw1_ref, dw1_hbm, dw1_sem, j)
            part = _dot_nt(dpb, w1_ref[j])
            dh2 = part if dh2 is None else dh2 + part
        x1 = x1_ref[...]
        r3 = lax.rsqrt(_rowmean(x1 * x1) + EPS)
        xh = x1 * r3
        a2 = g2_ref[...] * (1.0 + mod_ref[4:5, :])
        dshift2_ref[...] += _colsum(dh2)
        da2_ref[...] += _colsum(dh2 * xh)
        dxh = dh2 * a2
        dx1 = dy_ref[...] + r3 * (dxh - xh * _rowmean(dxh * xh))
        dx1_ref[...] = dx1

        mix = mix_ref[...]
        r2 = lax.rsqrt(_rowmean(mix * mix) + EPS)
        mh = mix * r2
        s1_ref[...] += _colsum(dx1 * mh)
        gh = dx1 * (mod_ref[2:3, :] * g1_ref[...])
        dmix = (r2 * (gh - mh * _rowmean(gh * mh))).astype(BF16)
        dwo_ref[...] += _dot_tn(yc_ref[...], dmix).reshape(N_CHIPS, 2, rs // 2, D_MODEL)
        for j in range(N_CHIPS):
            _store_shard_on_last_step(dwo_ref, dwo_hbm, dwo_sem, j)
            dyc_ref[:, j * rs:(j + 1) * rs] = _dot_nt(dmix, wo_ref[j])
        _wait_stores_on_last_step((dw1_ref, dw1_hbm, dw1_sem), (dwo_ref, dwo_hbm, dwo_sem))

    vec = jax.ShapeDtypeStruct((1, D_MODEL), F32)
    dw1_shape = (N_CHIPS, 2, D_MODEL // 2, cs)
    dwo_shape = (N_CHIPS, 2, rs // 2, D_MODEL)
    return pl.pallas_call(
        body, name="bwd_fc1_out", grid=(s_len // ts,),
        in_specs=[_rows(ts, D_FF), _rows(ts, D_MODEL), _rows(ts, D_MODEL), _rows(ts, D_MODEL), _rows(ts, D_MODEL),
                  _rows(ts, D_MODEL), _const((N_MOD, D_MODEL)), _const((1, D_MODEL)), _const((1, D_MODEL)), _VMEM,
                  _VMEM, _ANY],
        out_specs=[_rows(ts, D_MODEL), _rows(ts, D_MODEL)] + [_const((1, D_MODEL))] * 3 + [_ANY, _ANY],
        out_shape=[jax.ShapeDtypeStruct((s_len, D_MODEL), F32), jax.ShapeDtypeStruct((s_len, D_MODEL), F32),
                   vec, vec, vec, jax.ShapeDtypeStruct(dw1_shape, F32), jax.ShapeDtypeStruct(dwo_shape, F32)],
        scratch_shapes=[pltpu.VMEM(dw1_shape, F32), pltpu.VMEM(dwo_shape, F32), pltpu.SemaphoreType.DMA((N_CHIPS,)),
                        pltpu.SemaphoreType.DMA((N_CHIPS,))],
        compiler_params=_params(),
    )(dp, dy, x1, mix, h2, ycat, mod6, n2pre, n1post, fc1_g, wout_g, dep)


def _mixer_bwd(z, dyc, w_spatial, bsp_full, gain, bias, w_pool, b_pool, pool_scale, dep, ts):
    s_len = z.shape[0]
    nb = ts // HALO
    last = s_len // HALO - 1
    te = ts + HALO

    def body(z_ref, zprev_ref, znext_ref, dyc_ref, dynext_ref, ws_ref, bsp_ref, gain_ref, bias_ref, wp_ref, bp_ref,
             ps_ref, dep_ref, dz_ref, dws_ref, dbsp_ref, dgain_ref, dbias_ref, dwp_ref, dbp_ref, dps_ref, mixed_ref,
             dvn_ref):
        i = pl.program_id(0)

        @pl.when(i == 0)
        def _():
            for ref in (dws_ref, dbsp_ref, dgain_ref, dbias_ref, dwp_ref, dbp_ref, dps_ref):
                ref[...] = jnp.zeros_like(ref)

        wc = _tril_weights(ws_ref)
        gain = gain_ref[...]
        u, vhat, rstd, vn, dga = _mixer_forward_tile(z_ref[:, :2 * D_A], wc, bsp_ref, gain, bias_ref[...], mixed_ref)
        dya = dyc_ref[:, :D_A]
        du = dya * mixed_ref[...]
        dmixed = dya * u
        dmb = dmixed.astype(BF16)
        dm_sum = dmixed[0:CHUNK, :]
        for k in range(1, ts // CHUNK):
            dm_sum = dm_sum + dmixed[k * CHUNK:(k + 1) * CHUNK, :]
        r_idx = lax.broadcasted_iota(jnp.int32, (CHUNK, CHUNK), 0)
        s_idx = lax.broadcasted_iota(jnp.int32, (CHUNK, CHUNK), 1)
        causal = (s_idx <= r_idx).astype(F32)
        for h in range(N_HEADS):
            lanes = slice(h * HEAD_DIM, (h + 1) * HEAD_DIM)
            dbsp_ref[h] += jnp.sum(dm_sum[:, lanes], axis=1, keepdims=True)
            acc = None
            for k in range(ts // CHUNK):
                rows = slice(k * CHUNK, (k + 1) * CHUNK)
                t = _dot_nt(dmb[rows, lanes], vn[rows, lanes])
                acc = t if acc is None else acc + t
                dvn_ref[rows, lanes] = _dot_tn(wc[h], dmb[rows, lanes])
            dws_ref[h] += acc * causal
        dvn = dvn_ref[...]
        dgain_ref[...] += _colsum(dvn * vhat)
        dbias_ref[...] += _colsum(dvn)
        dvh = dvn * gain
        dv = rstd * (dvh - _rowmean(dvh) - vhat * _rowmean(dvh * vhat))
        dz_ref[:, :D_A] = (du * dga[:, :D_A]).astype(BF16)
        dz_ref[:, D_A:2 * D_A] = (dv * dga[:, D_A:]).astype(BF16)

        zb = z_ref[:, 2 * D_A:]
        prev = jnp.where(i == 0, 0.0, zprev_ref[...])
        zb_ext = jnp.concatenate([zb, znext_ref[...]], axis=0)
        sums = _causal_window_sums(jnp.concatenate([prev, zb_ext], axis=0))
        pos, inv_counts = _window_counts(i * ts, te)
        dyb_ext = jnp.concatenate([dyc_ref[:, D_A:], dynext_ref[...]], axis=0)
        dlin_ext = dyb_ext * ps_ref[...]
        dbp_ref[...] += _colsum(dlin_ext[:ts, :])
        scaled = []
        ddiffs = []
        lins = []
        for g in range(len(POOL_WINDOWS)):
            lanes = slice(g * GROUP_DIM, (g + 1) * GROUP_DIM)
            diff = (sums[g][HALO:, :] * inv_counts[g] - zb_ext[:, lanes]).astype(BF16)
            wpb = wp_ref[g].astype(BF16)
            dlb = dlin_ext[:, lanes].astype(BF16)
            lins.append(_dot(diff[:ts, :], wpb) + bp_ref[:, lanes])
            dwp_ref[g] += _dot_tn(diff[:ts, :], dlb[:ts, :])
            dd = _dot_nt(dlb, wpb)
            ddiffs.append(dd)
            scaled.append(jnp.where(pos < float(s_len), dd * inv_counts[g], 0.0))
        dps_ref[...] += _colsum(dyb_ext[:ts, :] * jnp.concatenate(lins, axis=1))
        back = _anticausal_window_sums(jnp.concatenate(scaled, axis=1))
        for g in range(len(POOL_WINDOWS)):
            dz_ref[:, 2 * D_A + g * GROUP_DIM:2 * D_A + (g + 1) * GROUP_DIM] = (
                back[g][:ts, :] - ddiffs[g][:ts, :]).astype(BF16)

    sq = jax.ShapeDtypeStruct((N_HEADS, CHUNK, CHUNK), F32)
    vec = jax.ShapeDtypeStruct((1, D_A), F32)
    return pl.pallas_call(
        body, name="mixer_bwd", grid=(s_len // ts,),
        in_specs=[_rows(ts, D_Z),
                  pl.BlockSpec((HALO, D_B), lambda i: (jnp.maximum(i * nb - 1, 0), 2)),
                  pl.BlockSpec((HALO, D_B), lambda i: (jnp.minimum((i + 1) * nb, last), 2)),
                  _rows(ts, D_MODEL),
                  pl.BlockSpec((HALO, D_B), lambda i: (jnp.minimum((i + 1) * nb, last), 1)),
                  _const((N_HEADS, CHUNK, CHUNK)), _const((CHUNK, D_A)), _const((1, D_A)), _const((1, D_A)),
                  _const((N_HEADS, GROUP_DIM, GROUP_DIM)), _const((1, D_B)), _const((1, D_B)), _ANY],
        out_specs=[_rows(ts, D_Z), _const((N_HEADS, CHUNK, CHUNK)), _const((N_HEADS, CHUNK, 1)), _const((1, D_A)),
                   _const((1, D_A)), _const((N_HEADS, GROUP_DIM, GROUP_DIM)), _const((1, D_B)), _const((1, D_B))],
        out_shape=[jax.ShapeDtypeStruct((s_len, D_Z), BF16), sq, jax.ShapeDtypeStruct((N_HEADS, CHUNK, 1), F32), vec,
                   vec, sq, vec, vec],
        scratch_shapes=[pltpu.VMEM((ts, D_A), F32), pltpu.VMEM((ts, D_A), F32)],
        compiler_params=_params(),
    )(z, z, z, dyc, dyc, w_spatial, bsp_full, gain, bias, w_pool, b_pool, pool_scale, dep)


def _bwd_in(dz, dx1, x, mod6, n1pre, win_g, dep, ts):
    s_len = x.shape[0]
    cs = D_Z // N_CHIPS

    def body(dz_ref, dx1_ref, x_ref, mod_ref, g_ref, w_ref, dep_ref, gx_ref, dshift_ref, da_ref, dw_hbm, dw_ref,
             wfull_ref, dw_sem):
        _zero_on_first_step(dshift_ref, da_ref, dw_ref)
        _join_w_in_on_first_step(w_ref, wfull_ref)
        xv = x_ref[...]
        r = lax.rsqrt(_rowmean(xv * xv) + EPS)
        xh = xv * r
        h1b = (xh * (g_ref[...] * (1.0 + mod_ref[1:2, :])) + mod_ref[0:1, :]).astype(BF16)
        dzb = dz_ref[...]
        dw = _dot_tn(h1b, dzb)
        for j in range(N_CHIPS):
            dw_ref[j] += dw[:, j * cs:(j + 1) * cs].reshape(2, D_MODEL // 2, cs)
        dh = _dot_nt(dzb, wfull_ref[...])
        a1 = g_ref[...] * (1.0 + mod_ref[1:2, :])
        dshift_ref[...] += _colsum(dh)
        da_ref[...] += _colsum(dh * xh)
        dxh = dh * a1
        gx_ref[...] = dx1_ref[...] + r * (dxh - xh * _rowmean(dxh * xh))
        for j in range(N_CHIPS):
            _store_shard_on_last_step(dw_ref, dw_hbm, dw_sem, j)
        _wait_stores_on_last_step((dw_ref, dw_hbm, dw_sem))

    vec = jax.ShapeDtypeStruct((1, D_MODEL), F32)
    dw_shape = (N_CHIPS, 2, D_MODEL // 2, cs)
    return pl.pallas_call(
        body, name="bwd_in", grid=(s_len // ts,),
        in_specs=[_rows(ts, D_Z), _rows(ts, D_MODEL), _rows(ts, D_MODEL), _const((N_MOD, D_MODEL)),
                  _const((1, D_MODEL)), _VMEM, _ANY],
        out_specs=[_rows(ts, D_MODEL), _const((1, D_MODEL)), _const((1, D_MODEL)), _ANY],
        out_shape=[jax.ShapeDtypeStruct((s_len, D_MODEL), F32), vec, vec, jax.ShapeDtypeStruct(dw_shape, F32)],
        scratch_shapes=[pltpu.VMEM(dw_shape, F32), pltpu.VMEM((D_MODEL, D_Z), BF16),
                        pltpu.SemaphoreType.DMA((N_CHIPS,))],
        compiler_params=_params(),
    )(dz, dx1, x, mod6, n1pre, win_g, dep)


def _adamw_math(w, g, m, v):
    m = ADAM_B1 * m + (1.0 - ADAM_B1) * g
    v = ADAM_B2 * v + (1.0 - ADAM_B2) * (g * g)
    m_hat = m / (1.0 - ADAM_B1 ** ADAM_STEP)
    v_hat = v / (1.0 - ADAM_B2 ** ADAM_STEP)
    delta = -ADAM_LR * (m_hat / (jnp.sqrt(v_hat) + ADAM_EPS) + ADAM_WD * w)
    return delta, m, v


def _adamw(gs, ws, ms, vs, name, steps):
    n = len(ws)

    def body(*refs):
        for i in range(n):
            g_ref, w_ref, m_ref, v_ref = refs[4 * i:4 * i + 4]
            d, nm, nv = _adamw_math(w_ref[...], g_ref[...], m_ref[...], v_ref[...])
            for ref, val in zip(refs[4 * n + 3 * i:4 * n + 3 * i + 3], (d, nm, nv)):
                ref[...] = val

    specs = [_rows(w.shape[0] // steps, w.shape[1]) for w in ws]
    out = pl.pallas_call(
        body, name=name, grid=(steps,),
        in_specs=[s for s in specs for _ in range(4)], out_specs=[s for s in specs for _ in range(3)],
        out_shape=[jax.ShapeDtypeStruct(w.shape, F32) for w in ws for _ in range(3)],
        compiler_params=_params(),
    )(*[a for quad in zip(gs, ws, ms, vs) for a in quad])
    return [out[3 * i:3 * i + 3] for i in range(n)]


def _ada_grad_adamw(sc_t, dmod_shard, w, m, v, tr):
    rows, cols = w.shape

    def body(s_ref, dm_ref, w_ref, m_ref, v_ref, g_ref, d_ref, nm_ref, nv_ref):
        g = s_ref[:, 0:1] * dm_ref[0:1, :]
        for b in range(1, N_DEV):
            g = g + s_ref[:, b:b + 1] * dm_ref[b:b + 1, :]
        g_ref[...] = g
        d, nm, nv = _adamw_math(w_ref[...], g, m_ref[...], v_ref[...])
        d_ref[...] = d
        nm_ref[...] = nm
        nv_ref[...] = nv

    spec = _rows(tr, cols)
    shape = jax.ShapeDtypeStruct((rows, cols), F32)
    return pl.pallas_call(
        body, name="ada_grad_adamw", grid=(rows // tr,),
        in_specs=[_rows(tr, N_DEV), _const((N_DEV, cols)), spec, spec, spec],
        out_specs=[spec] * 4, out_shape=[shape] * 4, compiler_params=_params(),
    )(sc_t, dmod_shard, w, m, v)


def _mod_grads(da1, dshift1, s1, da2, dshift2, s2, mod6, n1pre, n1post, n2pre, n2post):
    def body(da1_ref, ds1_ref, s1_ref, da2_ref, ds2_ref, s2_ref, mod_ref, n1_ref, p1_ref, n2_ref, p2_ref, dmod_ref,
             dn_ref):
        dmod_ref[0:1, :] = ds1_ref[...]
        dmod_ref[1:2, :] = da1_ref[...] * n1_ref[...]
        dmod_ref[2:3, :] = s1_ref[...] * p1_ref[...]
        dmod_ref[3:4, :] = ds2_ref[...]
        dmod_ref[4:5, :] = da2_ref[...] * n2_ref[...]
        dmod_ref[5:6, :] = s2_ref[...] * p2_ref[...]
        dn_ref[0:1, :] = da1_ref[...] * (1.0 + mod_ref[1:2, :])
        dn_ref[1:2, :] = s1_ref[...] * mod_ref[2:3, :]
        dn_ref[2:3, :] = da2_ref[...] * (1.0 + mod_ref[4:5, :])
        dn_ref[3:4, :] = s2_ref[...] * mod_ref[5:6, :]

    return pl.pallas_call(
        body, name="mod_grads",
        out_shape=[jax.ShapeDtypeStruct((N_MOD, D_MODEL), F32), jax.ShapeDtypeStruct((4, D_MODEL), F32)],
    )(da1, dshift1, s1, da2, dshift2, s2, mod6, n1pre, n1post, n2pre, n2post)


def _position():
    x, y, c = lax.axis_index("x"), lax.axis_index("y"), lax.axis_index("c")
    return x, y, c


def _flip(v, bit):
    return 1 - v if bit else v


def _peer(x, y, c, k):
    return (_flip(x, k & 4), _flip(y, k & 2), _flip(c, k & 1))


def _remote(src, dst, send_sem, recv_sem, device):
    return pltpu.make_async_remote_copy(src_ref=src, dst_ref=dst, send_sem=send_sem, recv_sem=recv_sem,
                                        device_id=device, device_id_type=MESH)


def _mod_exchange(c_row, w_ada_shard, b_ada_row, ws, early):
    cs = w_ada_shard.shape[1]
    n = len(ws)

    def body(c_ref, w_hbm, b_ref, *refs):
        shards, (mod_ref, sc_ref), zones = refs[:n], refs[n:n + 2], refs[n + 2:2 * n + 2]
        rows_ref, w_ref, w_sem, send1, recv1, send2, recv2 = refs[2 * n + 2:2 * n + 9]
        wide, narrow = refs[2 * n + 9:3 * n + 9], refs[3 * n + 9:4 * n + 9]
        cast_load, cast_store = refs[4 * n + 9:]
        x, y, c = _position()
        me = 4 * x + 2 * y + c
        chip = 2 * x + y
        w_load = pltpu.make_async_copy(w_hbm, w_ref, w_sem)
        w_load.start()
        loads = [pltpu.make_async_copy(shards[i], wide[i], cast_load.at[i]) for i in range(n)]
        stores = [pltpu.make_async_copy(narrow[i], zones[i].at[chip], cast_store.at[i]) for i in range(n)]

        def cast(i):
            loads[i].wait()
            narrow[i][...] = wide[i][...].astype(BF16)
            stores[i].start()

        for i in range(early):
            loads[i].start()
        cv = c_ref[...]
        sc_ref[me] = cv * jax.nn.sigmoid(cv)
        gather = [_remote(sc_ref.at[me], sc_ref.at[me], send1.at[k - 1], recv1.at[k - 1], _peer(x, y, c, k))
                  for k in range(1, N_DEV)]
        for cp in gather:
            cp.start()
        for i in range(early):
            cast(i)
        for i in range(early, n):
            loads[i].start()
        for k in range(1, N_DEV):
            px, py, pc = _peer(x, y, c, k)
            src = 4 * px + 2 * py + pc
            _remote(sc_ref.at[src], sc_ref.at[src], send1.at[k - 1], recv1.at[k - 1], (px, py, pc)).wait_recv()
        for cp in gather:
            cp.wait_send()
        sc_all = jnp.concatenate([sc_ref[b] for b in range(N_DEV)], axis=0)
        w_load.wait()
        part = _dot(sc_all.astype(BF16), w_ref[...].astype(BF16))
        part = part + b_ref[:, pl.ds(pl.multiple_of(chip * cs, LANES), cs)]
        for b in range(N_DEV):
            rows_ref[b] = part[b:b + 1, :]
        mod_ref[chip] = rows_ref[me]
        hand = []
        for k in (2, 4, 6):
            px, py, _ = _peer(x, y, c, k)
            hand.append(_remote(rows_ref.at[4 * px + 2 * py + c], mod_ref.at[chip], send2.at[k // 2 - 1],
                                recv2.at[k // 2 - 1], (px, py, c)))
        for cp in hand:
            cp.start()
        for i in range(early, n):
            cast(i)
        for k in (2, 4, 6):
            px, py, _ = _peer(x, y, c, k)
            pchip = 2 * px + py
            _remote(rows_ref.at[me], mod_ref.at[pchip], send2.at[k // 2 - 1], recv2.at[k // 2 - 1],
                    (px, py, c)).wait_recv()
        for cp in hand:
            cp.wait_send()
        for cp in stores:
            cp.wait()

    out = pl.pallas_call(
        body, name="mod_exchange",
        in_specs=[_VMEM, _ANY, _VMEM] + [_ANY] * n, out_specs=[_VMEM, _VMEM] + [_ANY] * n,
        out_shape=[jax.ShapeDtypeStruct((N_CHIPS, 1, cs), F32), jax.ShapeDtypeStruct((N_DEV, 1, D_MODEL), F32)]
        + [jax.ShapeDtypeStruct((N_CHIPS,) + w.shape, BF16) for w in ws],
        scratch_shapes=[pltpu.VMEM((N_DEV, 1, cs), F32), pltpu.VMEM(w_ada_shard.shape, F32), pltpu.SemaphoreType.DMA,
                        pltpu.SemaphoreType.DMA((N_DEV - 1,)),
                        pltpu.SemaphoreType.DMA((N_DEV - 1,)), pltpu.SemaphoreType.DMA((N_CHIPS - 1,)),
                        pltpu.SemaphoreType.DMA((N_CHIPS - 1,))]
        + [pltpu.VMEM(w.shape, F32) for w in ws] + [pltpu.VMEM(w.shape, BF16) for w in ws]
        + [pltpu.SemaphoreType.DMA((n,))] * 2,
        compiler_params=pltpu.CompilerParams(vmem_limit_bytes=VMEM_LIMIT),
    )(c_row, w_ada_shard, b_ada_row, *ws)
    return out[0], out[1], list(out[2:])


_HBM = pl.BlockSpec(memory_space=pltpu.HBM)
_SEM = pl.BlockSpec(memory_space=pltpu.SEMAPHORE)
_EFFECT = pltpu.SideEffectType.DATAFLOW_SIDE_EFFECTING
_CHIP_HOPS = (2, 4, 6)


def _in_hbm(a):
    return pltpu.with_memory_space_constraint(a, pltpu.HBM)


def _sems3():
    return pltpu.SemaphoreType.DMA((len(_CHIP_HOPS),))


def _ag_start(lands, after, name):
    n = len(lands)

    def body(*refs):
        zones = refs[:n]
        sends, recvs = refs[n + 1:2 * n + 1], refs[2 * n + 1:3 * n + 1]
        x, y, c = _position()
        chip = 2 * x + y
        for i in range(n):
            half = zones[i].shape[1] // 2
            mine = zones[i].at[chip, pl.ds(c * half, half)]
            for s, k in enumerate(_CHIP_HOPS):
                px, py, _ = _peer(x, y, c, k)
                _remote(mine, mine, sends[i].at[s], recvs[i].at[s], (px, py, c)).start()

    out = pl.pallas_call(
        body, name=name,
        in_specs=[_HBM] * n + [_ANY],
        out_specs=[_SEM] * (2 * n) + [_HBM] * n,
        out_shape=[_sems3()] * (2 * n) + [pltpu.HBM(z.shape, BF16) for z in lands],
        input_output_aliases={i: 2 * n + i for i in range(n)},
        compiler_params=pltpu.CompilerParams(has_side_effects=_EFFECT),
    )(*[_in_hbm(z) for z in lands], after)
    return [(out[2 * n + i], out[i], out[n + i]) for i in range(n)]


def _ag_pass(group, after, name):
    n = len(group)

    def body(*refs):
        zones = refs[:n]
        sends, recvs = refs[n:2 * n], refs[2 * n:3 * n]
        fsends, frecvs = refs[4 * n + 1:5 * n + 1], refs[5 * n + 1:6 * n + 1]
        x, y, c = _position()
        chip = 2 * x + y
        for i in range(n):
            half = zones[i].shape[1] // 2
            rows = pl.ds(c * half, half)
            for s, k in enumerate(_CHIP_HOPS):
                px, py, _ = _peer(x, y, c, k)
                landed = zones[i].at[2 * px + py, rows]
                _remote(landed, landed, sends[i].at[s], recvs[i].at[s], (px, py, c)).wait_recv()
                _remote(landed, landed, fsends[i].at[s], frecvs[i].at[s], (x, y, 1 - c)).start()
        for i in range(n):
            half = zones[i].shape[1] // 2
            mine = zones[i].at[chip, pl.ds(c * half, half)]
            for s, k in enumerate(_CHIP_HOPS):
                px, py, _ = _peer(x, y, c, k)
                _remote(mine, mine, sends[i].at[s], recvs[i].at[s], (px, py, c)).wait_send()

    out = pl.pallas_call(
        body, name=name,
        in_specs=[_HBM] * n + [_SEM] * (2 * n) + [_ANY],
        out_specs=[_HBM] * n + [_SEM] * (2 * n),
        out_shape=[pltpu.HBM(g[0].shape, BF16) for g in group] + [_sems3()] * (2 * n),
        input_output_aliases={i: i for i in range(n)},
        compiler_params=pltpu.CompilerParams(has_side_effects=_EFFECT),
    )(*[g[0] for g in group], *[g[1] for g in group], *[g[2] for g in group], after)
    return [(out[i], out[n + i], out[2 * n + i]) for i in range(n)]


def _ag_done(group, name):
    n = len(group)

    def body(*refs):
        lands = refs[:n]
        fsends, frecvs = refs[n:2 * n], refs[2 * n:3 * n]
        x, y, c = _position()
        for i in range(n):
            half = lands[i].shape[1] // 2
            for s, k in enumerate(_CHIP_HOPS):
                px, py, _ = _peer(x, y, c, k)
                sent = lands[i].at[2 * px + py, pl.ds(c * half, half)]
                got = lands[i].at[2 * px + py, pl.ds((1 - c) * half, half)]
                cp = _remote(sent, got, fsends[i].at[s], frecvs[i].at[s], (x, y, 1 - c))
                cp.wait_recv()
                cp.wait_send()

    out = pl.pallas_call(
        body, name=name,
        in_specs=[_HBM] * n + [_SEM] * (2 * n),
        out_specs=[_HBM] * n,
        out_shape=[pltpu.HBM(g[0].shape, BF16) for g in group],
        input_output_aliases={i: i for i in range(n)},
        compiler_params=pltpu.CompilerParams(has_side_effects=_EFFECT),
    )(*[g[0] for g in group], *[g[1] for g in group], *[g[2] for g in group])
    return list(out)


def _small_spread_start(slots, after):
    def body(z_ref, after_ref, sends, recvs, z_out):
        x, y, c = _position()
        mine = z_ref.at[2 * x + y]
        for s, k in enumerate(_CHIP_HOPS):
            px, py, _ = _peer(x, y, c, k)
            _remote(mine, mine, sends.at[s], recvs.at[s], (px, py, c)).start()

    sends, recvs, out = pl.pallas_call(
        body, name="small_spread_start",
        in_specs=[_HBM, _ANY], out_specs=[_SEM, _SEM, _HBM],
        out_shape=[_sems3(), _sems3(), pltpu.HBM(slots.shape, F32)],
        input_output_aliases={0: 2},
        compiler_params=pltpu.CompilerParams(has_side_effects=_EFFECT),
    )(_in_hbm(slots), after)
    return out, sends, recvs


def _small_spread_wait(slots, sends, recvs, afters):
    def body(z_ref, sends, recvs, *rest):
        x, y, c = _position()
        mine = z_ref.at[2 * x + y]
        for s, k in enumerate(_CHIP_HOPS):
            px, py, _ = _peer(x, y, c, k)
            cp = _remote(mine, z_ref.at[2 * px + py], sends.at[s], recvs.at[s], (px, py, c))
            cp.wait_recv()
            cp.wait_send()

    return pl.pallas_call(
        body, name="small_spread_wait",
        in_specs=[_HBM, _SEM, _SEM] + [_ANY] * len(afters), out_specs=_HBM, out_shape=pltpu.HBM(slots.shape, F32),
        input_output_aliases={0: 0},
        compiler_params=pltpu.CompilerParams(has_side_effects=_EFFECT),
    )(slots, sends, recvs, *afters)


def _sibling_sum(pgs, name, small=None):
    n = len(pgs)
    k = 0 if small is None else 1
    units = [(i, j) for i in range(n) for j in range(N_CHIPS)]

    def body(*refs):
        refs = list(refs)
        take = lambda count: [refs.pop(0) for _ in range(count)]
        ins, small_in = take(n), take(k)
        qbs, owns, slots_out = take(n), take(n), take(k)
        mine, other, stage, got = take(n), take(n), take(n), take(n)
        load_a, load_b, send, recv, keep_wire, keep_own = take(6)
        x, y, c = _position()
        chip = 2 * x + y
        if k:
            sib_ref, pair_send, pair_recv, keep_small = take(4)
            pair = _remote(small_in[0], sib_ref, pair_send, pair_recv, (x, y, 1 - c))
            pair.start()
        loads_a = [pltpu.make_async_copy(ins[i].at[j, 1 - c], other[i].at[j], load_a.at[u])
                   for u, (i, j) in enumerate(units)]
        loads_b = [pltpu.make_async_copy(ins[i].at[j, c], mine[i].at[j], load_b.at[u])
                   for u, (i, j) in enumerate(units)]
        for cp in loads_a + loads_b:
            cp.start()
        sent = []
        for u, (i, j) in enumerate(units):
            loads_a[u].wait()
            stage[i][j] = other[i][j].astype(BF16)
            cp = _remote(stage[i].at[j], got[i].at[j], send.at[u], recv.at[u], (x, y, 1 - c))
            cp.start()
            sent.append(cp)
        stores = []
        for u, (i, j) in enumerate(units):
            loads_b[u].wait()
            sent[u].wait_recv()
            q = mine[i][j] + got[i][j].astype(F32)
            mine[i][j] = q
            got[i][j] = q.astype(BF16)
            stores.append(pltpu.make_async_copy(got[i].at[j], qbs[i].at[j], keep_wire.at[u]))
            stores[-1].start()
        for i in range(n):
            stores.append(pltpu.make_async_copy(mine[i].at[chip], owns[i], keep_own.at[i]))
            stores[-1].start()
        if k:
            pair.wait()
            sib_ref[...] = small_in[0][...] + sib_ref[...]
            stores.append(pltpu.make_async_copy(sib_ref, slots_out[0].at[chip], keep_small))
            stores[-1].start()
        for cp in sent:
            cp.wait_send()
        for cp in stores:
            cp.wait()

    wire = [(N_CHIPS,) + p.shape[2:] for p in pgs]
    extra_out, extra_scratch = [], []
    if k:
        extra_out = [jax.ShapeDtypeStruct((N_CHIPS,) + small.shape, F32)]
        extra_scratch = [pltpu.VMEM(small.shape, F32)] + [pltpu.SemaphoreType.DMA] * 3
    out = pl.pallas_call(
        body, name=name, in_specs=[_ANY] * n + [_VMEM] * k, out_specs=[_ANY] * (2 * n + k),
        out_shape=[jax.ShapeDtypeStruct(w, BF16) for w in wire] + [jax.ShapeDtypeStruct(w[1:], F32) for w in wire]
        + extra_out,
        scratch_shapes=[pltpu.VMEM(w, F32) for w in wire] * 2 + [pltpu.VMEM(w, BF16) for w in wire] * 2
        + [pltpu.SemaphoreType.DMA((len(units),))] * 5 + [pltpu.SemaphoreType.DMA((n,))] + extra_scratch,
        compiler_params=pltpu.CompilerParams(vmem_limit_bytes=VMEM_LIMIT),
    )(*pgs, *([small] if k else []))
    return list(out[:n]), list(out[n:2 * n]), list(out[2 * n:])


def _rs_start(qbs, name, after=None):
    n = len(qbs)
    k = 0 if after is None else 1

    def body(*refs):
        outs, inboxes = refs[:n], refs[n:2 * n]
        sends, recvs = refs[2 * n + k:3 * n + k], refs[3 * n + k:4 * n + k]
        x, y, c = _position()
        chip = 2 * x + y
        for i in range(n):
            for s, hop in enumerate(_CHIP_HOPS):
                px, py, _ = _peer(x, y, c, hop)
                _remote(outs[i].at[2 * px + py], inboxes[i].at[chip], sends[i].at[s], recvs[i].at[s], (px, py, c)).start()

    inboxes = [_in_hbm(lax.empty(q.shape, BF16)) for q in qbs]
    out = pl.pallas_call(
        body, name=name,
        in_specs=[_HBM] * (2 * n) + [_ANY] * k,
        out_specs=[_SEM] * (2 * n) + [_HBM] * (2 * n),
        out_shape=[_sems3()] * (2 * n) + [pltpu.HBM(q.shape, BF16) for q in qbs] * 2,
        input_output_aliases={i: 2 * n + i for i in range(2 * n)},
        compiler_params=pltpu.CompilerParams(has_side_effects=_EFFECT),
    )(*[_in_hbm(q) for q in qbs], *inboxes, *([after] if k else []))
    return [(out[2 * n + i], out[3 * n + i], out[i], out[n + i]) for i in range(n)]


def _rs_wait(group, after, name):
    n = len(group)

    def body(*refs):
        outs, inboxes = refs[:n], refs[n:2 * n]
        sends, recvs = refs[2 * n:3 * n], refs[3 * n:4 * n]
        x, y, c = _position()
        for i in range(n):
            for s, k in enumerate(_CHIP_HOPS):
                px, py, _ = _peer(x, y, c, k)
                slot = 2 * px + py
                cp = _remote(outs[i].at[slot], inboxes[i].at[slot], sends[i].at[s], recvs[i].at[s], (px, py, c))
                cp.wait_recv()
                cp.wait_send()

    out = pl.pallas_call(
        body, name=name,
        in_specs=[_HBM] * (2 * n) + [_SEM] * (2 * n) + [_ANY],
        out_specs=[_HBM] * n,
        out_shape=[pltpu.HBM(g[1].shape, BF16) for g in group],
        input_output_aliases={n + i: i for i in range(n)},
        compiler_params=pltpu.CompilerParams(has_side_effects=_EFFECT),
    )(*[g[0] for g in group], *[g[1] for g in group], *[g[2] for g in group], *[g[3] for g in group], after)
    return list(out)


def _final_share(inboxes, owns, name):
    n = len(inboxes)
    units = [(i, s) for i in range(n) for s in range(len(_CHIP_HOPS))]

    def body(*refs):
        ins, mine, outs, landed, half = (refs[k * n:(k + 1) * n] for k in range(5))
        load, load_own, keep, send, recv = refs[5 * n:]
        x, y, c = _position()
        loads = []
        for u, (i, s) in enumerate(units):
            px, py, _ = _peer(x, y, c, _CHIP_HOPS[s])
            loads.append(pltpu.make_async_copy(ins[i].at[2 * px + py], landed[i].at[s], load.at[u]))
        loads_own = [pltpu.make_async_copy(mine[i], half[i], load_own.at[i]) for i in range(n)]
        for cp in loads + loads_own:
            cp.start()
        copies = []
        for i in range(n):
            for s in range(len(_CHIP_HOPS)):
                loads[len(_CHIP_HOPS) * i + s].wait()
            loads_own[i].wait()
            total = (landed[i][0].astype(F32) + landed[i][1].astype(F32)) + landed[i][2].astype(F32)
            half[i][...] = total + half[i][...]
            copies.append(pltpu.make_async_copy(half[i], outs[i].at[c], keep.at[i]))
            copies.append(_remote(half[i], outs[i].at[c], send.at[i], recv.at[i], (x, y, 1 - c)))
            for cp in copies[-2:]:
                cp.start()
        for i in range(n):
            theirs = outs[i].at[1 - c]
            _remote(theirs, theirs, send.at[i], recv.at[i], (x, y, 1 - c)).wait_recv()
        for i in range(n):
            copies[2 * i].wait()
            copies[2 * i + 1].wait_send()

    return pl.pallas_call(
        body, name=name, in_specs=[_ANY] * (2 * n), out_specs=[_ANY] * n,
        out_shape=[jax.ShapeDtypeStruct((2,) + o.shape, F32) for o in owns],
        scratch_shapes=[pltpu.VMEM((len(_CHIP_HOPS),) + o.shape, BF16) for o in owns]
        + [pltpu.VMEM(o.shape, F32) for o in owns]
        + [pltpu.SemaphoreType.DMA((len(units),))] + [pltpu.SemaphoreType.DMA((n,))] * 4,
        compiler_params=pltpu.CompilerParams(vmem_limit_bytes=VMEM_LIMIT),
    )(*inboxes, *owns)


_SMALL = (("b_ada", N_MOD * D_MODEL), ("norm1_pre", D_MODEL), ("norm1_post", D_MODEL), ("norm2_pre", D_MODEL),
          ("norm2_post", D_MODEL), ("w_spatial", N_HEADS * CHUNK * CHUNK), ("b_spatial", N_HEADS * CHUNK),
          ("ln_v_gain", D_A), ("ln_v_bias", D_A), ("w_pool", N_HEADS * GROUP_DIM * GROUP_DIM),
          ("b_pool", D_B), ("pool_scale", D_B))
_MOD_ROWS = N_MOD * D_MODEL // LANES


def _packed_rows(size):
    return -(-(size // LANES) // SUBLANES) * SUBLANES


def _pack(parts):
    out = []
    for name, size in _SMALL:
        a = parts[name].reshape(size // LANES, LANES)
        pad = _packed_rows(size) - a.shape[0]
        out.append(jnp.pad(a, ((0, pad), (0, 0))) if pad else a)
    return out


def _small_adamw(slots, ws, ms, vs):
    n = len(_SMALL)
    head = N_DEV * _MOD_ROWS

    def body(*refs):
        s_ref, w, m, v = refs[0], refs[1:1 + n], refs[1 + n:1 + 2 * n], refs[1 + 2 * n:1 + 3 * n]
        outs = refs[1 + 3 * n:1 + 7 * n]
        dmod_ref, loss_ref, t_ref = refs[1 + 7 * n:]
        t_ref[...] = ((s_ref[0] + s_ref[1]) + s_ref[2]) + s_ref[3]
        dmod_ref[...] = t_ref[0:head, :]
        loss_ref[...] = t_ref[t_ref.shape[0] - 8:, :]
        row = head
        for i, (_, size) in enumerate(_SMALL):
            if i == 0:
                g = t_ref[0:_MOD_ROWS, :]
                for b in range(1, N_DEV):
                    g = g + t_ref[b * _MOD_ROWS:(b + 1) * _MOD_ROWS, :]
            else:
                g = t_ref[row:row + size // LANES, :]
                row += _packed_rows(size)
            d, nm, nv = _adamw_math(w[i][...], g, m[i][...], v[i][...])
            for ref, val in zip(outs[4 * i:4 * i + 4], (g, d, nm, nv)):
                ref[...] = val

    each = [jax.ShapeDtypeStruct((size // LANES, LANES), F32) for _, size in _SMALL for _ in range(4)]
    out = pl.pallas_call(
        body, name="small_adamw",
        out_shape=each + [jax.ShapeDtypeStruct((head, LANES), F32), jax.ShapeDtypeStruct((SUBLANES, LANES), F32)],
        scratch_shapes=[pltpu.VMEM(slots.shape[1:], F32)],
        compiler_params=pltpu.CompilerParams(vmem_limit_bytes=VMEM_LIMIT),
    )(slots, *ws, *ms, *vs)
    return [out[4 * i:4 * i + 4] for i in range(n)], out[4 * n], out[4 * n + 1]


def kernel(x, c, w_ada, b_ada, norm1_pre, norm1_post, w_in, w_spatial, b_spatial, ln_v_gain, ln_v_bias, w_pool, b_pool, pool_scale, w_out, norm2_pre, norm2_post, w_fc1, w_fc2, loss_target, m_w_ada, m_b_ada, m_norm1_pre, m_norm1_post, m_w_in, m_w_spatial, m_b_spatial, m_ln_v_gain, m_ln_v_bias, m_w_pool, m_b_pool, m_pool_scale, m_w_out, m_norm2_pre, m_norm2_post, m_w_fc1, m_w_fc2, v_w_ada, v_b_ada, v_norm1_pre, v_norm1_post, v_w_in, v_w_spatial, v_b_spatial, v_ln_v_gain, v_ln_v_bias, v_w_pool, v_b_pool, v_pool_scale, v_w_out, v_norm2_pre, v_norm2_post, v_w_fc1, v_w_fc2):
    weights = dict(w_ada=w_ada, b_ada=b_ada, norm1_pre=norm1_pre, norm1_post=norm1_post, w_in=w_in,
                   w_spatial=w_spatial, b_spatial=b_spatial, ln_v_gain=ln_v_gain, ln_v_bias=ln_v_bias, w_pool=w_pool,
                   b_pool=b_pool, pool_scale=pool_scale, w_out=w_out, norm2_pre=norm2_pre, norm2_post=norm2_post,
                   w_fc1=w_fc1, w_fc2=w_fc2)
    m_old = dict(w_ada=m_w_ada, b_ada=m_b_ada, norm1_pre=m_norm1_pre, norm1_post=m_norm1_post, w_in=m_w_in,
                 w_spatial=m_w_spatial, b_spatial=m_b_spatial, ln_v_gain=m_ln_v_gain, ln_v_bias=m_ln_v_bias,
                 w_pool=m_w_pool, b_pool=m_b_pool, pool_scale=m_pool_scale, w_out=m_w_out, norm2_pre=m_norm2_pre,
                 norm2_post=m_norm2_post, w_fc1=m_w_fc1, w_fc2=m_w_fc2)
    v_old = dict(w_ada=v_w_ada, b_ada=v_b_ada, norm1_pre=v_norm1_pre, norm1_post=v_norm1_post, w_in=v_w_in,
                 w_spatial=v_w_spatial, b_spatial=v_b_spatial, ln_v_gain=v_ln_v_gain, ln_v_bias=v_ln_v_bias,
                 w_pool=v_w_pool, b_pool=v_b_pool, pool_scale=v_pool_scale, w_out=v_w_out, norm2_pre=v_norm2_pre,
                 norm2_post=v_norm2_post, w_fc1=v_w_fc1, w_fc2=v_w_fc2)
    order = ("w_ada", "b_ada", "norm1_pre", "norm1_post", "w_in", "w_spatial", "b_spatial", "ln_v_gain", "ln_v_bias",
             "w_pool", "b_pool", "pool_scale", "w_out", "norm2_pre", "norm2_post", "w_fc1", "w_fc2")
    mx, my, mc = _position()
    me = 4 * mx + 2 * my + mc
    chip = 2 * mx + my
    row = lambda a: a.reshape(1, -1)

    xs, target = x[0], loss_target[0]
    n1pre, n1post, n2pre, n2post = row(norm1_pre), row(norm1_post), row(norm2_pre), row(norm2_post)
    mixer = (w_spatial, jnp.repeat(b_spatial.T, HEAD_DIM, axis=1), row(ln_v_gain), row(ln_v_bias), w_pool,
             row(b_pool), row(pool_scale))
    ts_big, ts_mid = 512, 256

    mod4, sc_all, lands = _mod_exchange(c, w_ada, row(b_ada), [w_in, w_out, w_fc1, w_fc2], 2)
    mod6 = mod4.reshape(N_MOD, D_MODEL)
    ag = _ag_start(lands, mod4, "ag_start")

    win_g, wout_g = _ag_done(_ag_pass([ag[0], ag[1]], ag[2][0], "ag_pass_mix"), "ag_done_mix")
    z, ycat, mix, x1, h2 = _fwd_mix(xs, mod6, n1pre, n1post, n2pre, win_g, wout_g, *mixer, ts_big)
    (fc1_g,) = _ag_done(_ag_pass([ag[2]], h2, "ag_pass_fc1"), "ag_done_fc1")
    q = _fwd_fc1(h2, fc1_g, ts_big)
    (fc2_g,) = _ag_done(_ag_pass([ag[3]], q, "ag_pass_fc2"), "ag_done_fc2")
    dy, df, loss, s2 = _fwd_fc2_loss(q, x1, target, mod6, n2post, fc2_g, ts_big)

    def reduce_start(partials, tag, small=None):
        wire, owns, slots = _sibling_sum(partials, "sibling_sum_" + tag, small)
        return _rs_start(wire, "rs_start_" + tag), owns, slots

    def reduce_finish(state, owns, names, tag, dep):
        inboxes = _rs_wait(state, dep, "rs_wait_" + tag)
        shards = _final_share(inboxes, owns, "final_share_" + tag)
        for n, g in zip(names, shards):
            grads[n] = g.reshape(weights[n].shape)
        updates = _adamw([grads[n] for n in names], [weights[n] for n in names], [m_old[n] for n in names],
                         [v_old[n] for n in names], "adamw_" + tag, 8)
        for n, (d, nm, nv) in zip(names, updates):
            deltas[n], new_m[n], new_v[n] = d, nm, nv

    grads, deltas, new_m, new_v = {}, {}, {}, {}
    dp, g_fc2 = _bwd_fc2(df, q, fc2_g, ts_big)
    state_fc2, owns_fc2, _ = reduce_start([g_fc2], "fc2")
    dx1, dyc, dshift2, da2, s1, g_fc1, g_out = _bwd_fc1_out(
        dp, dy, x1, mix, h2, ycat, mod6, n2pre, n1post, fc1_g, wout_g, state_fc2[0][0], ts_mid)
    state_mid, owns_mid, _ = reduce_start([g_fc1, g_out], "mid")
    dz, dws, dbsp, dgain, dbias, dwp, dbp, dps = _mixer_bwd(z, dyc, *mixer, state_mid[0][0], ts_big)
    grad_x, dshift1, da1, g_in = _bwd_in(dz, dx1, xs, mod6, n1pre, win_g, state_mid[0][0], ts_big)
    dmod6, dnorms = _mod_grads(da1, dshift1, s1, da2, dshift2, s2, mod6, n1pre, n1post, n2pre, n2post)

    parts = dict(b_ada=dmod6, norm1_pre=dnorms[0], norm1_post=dnorms[1], norm2_pre=dnorms[2], norm2_post=dnorms[3],
                 w_spatial=dws, b_spatial=dbsp, ln_v_gain=dgain, ln_v_bias=dbias, w_pool=dwp, b_pool=dbp,
                 pool_scale=dps)
    pieces = _pack(parts)
    slots = lax.dynamic_update_slice(jnp.zeros((N_DEV * _MOD_ROWS, LANES), F32), pieces[0], (me * _MOD_ROWS, 0))
    loss_tile = jnp.pad(loss, ((0, SUBLANES - 1), (0, LANES - 1)))
    wire_in, owns_in, pair_sum = _sibling_sum([g_in], "sibling_sum_in",
                                              jnp.concatenate([slots] + pieces[1:] + [loss_tile], axis=0))
    spread = _small_spread_start(pair_sum[0], wire_in[0])
    state_in = _rs_start(wire_in, "rs_start_in", spread[0])
    reduce_finish(state_fc2 + state_mid, owns_fc2 + owns_mid, ("w_fc2", "w_fc1", "w_out"), "mlp", state_in[0][0])
    flat = lambda d: [d[n].reshape(size // LANES, LANES) for n, size in _SMALL]
    small_out, dmod_all, loss_tile = _small_adamw(
        _small_spread_wait(*spread, [deltas[n] for n in ("w_fc2", "w_fc1", "w_out")]), flat(weights), flat(m_old),
        flat(v_old))
    loss = loss_tile[0, 0]
    for (n, _), (g, d, nm, nv) in zip(_SMALL, small_out):
        shape = weights[n].shape
        grads[n], deltas[n], new_m[n], new_v[n] = g.reshape(shape), d.reshape(shape), nm.reshape(shape), nv.reshape(shape)

    dmod_all = dmod_all.reshape(N_DEV, N_MOD * D_MODEL)
    cs = w_ada.shape[1]
    dmod_shard = lax.dynamic_slice(dmod_all, (0, chip * cs), (N_DEV, cs))
    sc_t = sc_all.reshape(N_DEV, D_MODEL).T
    grads["w_ada"], deltas["w_ada"], new_m["w_ada"], new_v["w_ada"] = _ada_grad_adamw(
        sc_t, dmod_shard, w_ada, m_w_ada, v_w_ada, 128)

    reduce_finish(state_in, owns_in, ("w_in",), "in", deltas["w_ada"])

    return (loss, grad_x[None], *[grads[n] for n in order], *[deltas[n] for n in order],
            *[new_m[n] for n in order], *[new_v[n] for n in order])
```

```python
import jax
import jax.numpy as jnp
from jax import lax
from jax.experimental import pallas as pl
from jax.experimental.pallas import tpu as pltpu

F32 = jnp.float32
BF16 = jnp.bfloat16
MESH = pl.DeviceIdType.MESH

D_MODEL = 1024
D_A = 512
D_B = 512
D_Z = 2 * D_A + D_B
N_HEADS = 4
HEAD_DIM = 128
CHUNK = 128
POOL_WINDOWS = (2, 4, 8, 16)
GROUP_DIM = 128
D_FF = 4096
N_MOD = 6
EPS = 1e-6
HALO = 16
N_CHIPS = 4
N_DEV = 8

ADAM_LR = 0.001
ADAM_B1 = 0.9
ADAM_B2 = 0.999
ADAM_EPS = 1e-08
ADAM_WD = 0.01
ADAM_STEP = 10

VMEM_LIMIT = 56 * 1024 * 1024
LANES = 128
SUBLANES = 8

_VMEM = pl.BlockSpec(memory_space=pltpu.VMEM)
_ANY = pl.BlockSpec(memory_space=pl.ANY)


def _params(n_grid_axes=1):
    return pltpu.CompilerParams(dimension_semantics=("arbitrary",) * n_grid_axes, vmem_limit_bytes=VMEM_LIMIT)


def _rows(ts, width):
    return pl.BlockSpec((ts, width), lambda i: (i, 0))


def _const(shape):
    return pl.BlockSpec(shape, lambda i: (0,) * len(shape))


def _dot(a, b):
    return jnp.dot(a, b, preferred_element_type=F32)


def _dot_nt(a, b):
    return lax.dot_general(a, b, (((1,), (1,)), ((), ())), preferred_element_type=F32)


def _dot_tn(a, b):
    return lax.dot_general(a, b, (((0,), (0,)), ((), ())), preferred_element_type=F32)


def _rowmean(v):
    return jnp.mean(v, axis=-1, keepdims=True)


def _colsum(v):
    return jnp.sum(v, axis=0, keepdims=True)


def _gelu_parts(z):
    k0 = 0.7978845608028654
    k1 = 0.044715
    z2 = z * z
    t = jnp.tanh(z * (k0 + (k0 * k1) * z2))
    u = 0.5 * t + 0.5
    g = z * u
    dg = u + (0.5 * z) * (1.0 - t * t) * (k0 + (3.0 * k0 * k1) * z2)
    return g, dg


def _tril_weights(ws_ref):
    r = lax.broadcasted_iota(jnp.int32, (CHUNK, CHUNK), 0)
    s = lax.broadcasted_iota(jnp.int32, (CHUNK, CHUNK), 1)
    mask = (s <= r).astype(F32)
    return [(ws_ref[h] * mask).astype(BF16) for h in range(N_HEADS)]


def _window_counts(first_row, n_rows):
    pos = (first_row + lax.broadcasted_iota(jnp.int32, (n_rows, 1), 0)).astype(F32)
    return pos, [1.0 / jnp.minimum(pos + 1.0, float(w)) for w in POOL_WINDOWS]


def _causal_window_sums(ext):
    out = []
    e = ext
    shift = 1
    for g in range(len(POOL_WINDOWS)):
        e = e + pltpu.roll(e, shift, 0)
        shift *= 2
        out.append(e[:, g * GROUP_DIM:(g + 1) * GROUP_DIM])
    return out


def _anticausal_window_sums(ext):
    n = ext.shape[0]
    out = []
    e = ext
    shift = 1
    for g in range(len(POOL_WINDOWS)):
        e = e + pltpu.roll(e, n - shift, 0)
        shift *= 2
        out.append(e[:, g * GROUP_DIM:(g + 1) * GROUP_DIM])
    return out


def _fwd_mix(x, mod6, n1pre, n1post, n2pre, win_g, wout_g, w_spatial, bsp_full, gain, bias, w_pool, b_pool, pool_scale, ts):
    s_len = x.shape[0]
    rs = D_MODEL // N_CHIPS

    def body(x_ref, mod_ref, g1pre_ref, g1post_ref, g2pre_ref, win_ref, wout_ref, ws_ref, bsp_ref, gain_ref,
             bias_ref, wp_ref, bp_ref, ps_ref, z_ref, y_ref, mix_ref, x1_ref, h2_ref, mixed_ref, prev_ref, wfull_ref):
        i = pl.program_id(0)
        _zero_on_first_step(prev_ref)
        _join_w_in_on_first_step(win_ref, wfull_ref)
        xv = x_ref[...]
        r = lax.rsqrt(_rowmean(xv * xv) + EPS)
        hb = ((xv * r) * (g1pre_ref[...] * (1.0 + mod_ref[1:2, :])) + mod_ref[0:1, :]).astype(BF16)
        z_ref[...] = _dot(hb, wfull_ref[...])

        wc = _tril_weights(ws_ref)
        u, _, _, _, _ = _mixer_forward_tile(z_ref[:, :2 * D_A], wc, bsp_ref, gain_ref[...], bias_ref[...], mixed_ref)
        y_ref[:, :D_A] = (u * mixed_ref[...]).astype(BF16)
        zb = z_ref[:, 2 * D_A:]
        sums = _causal_window_sums(jnp.concatenate([prev_ref[...], zb], axis=0))
        prev_ref[...] = zb[ts - HALO:, :]
        _, inv_counts = _window_counts(i * ts, ts)
        for g in range(len(POOL_WINDOWS)):
            lanes = slice(g * GROUP_DIM, (g + 1) * GROUP_DIM)
            diff = sums[g][HALO:, :] * inv_counts[g] - zb[:, lanes]
            lin = _dot(diff.astype(BF16), wp_ref[g].astype(BF16)) + bp_ref[:, lanes]
            y_ref[:, D_A + g * GROUP_DIM:D_A + (g + 1) * GROUP_DIM] = (lin * ps_ref[:, lanes]).astype(BF16)

        mix = None
        for j in range(N_CHIPS):
            part = _dot(y_ref[:, j * rs:(j + 1) * rs], wout_ref[j])
            mix = part if mix is None else mix + part
        mix_ref[...] = mix
        r2 = lax.rsqrt(_rowmean(mix * mix) + EPS)
        x1 = xv + (mix * r2) * (mod_ref[2:3, :] * g1post_ref[...])
        x1_ref[...] = x1
        r3 = lax.rsqrt(_rowmean(x1 * x1) + EPS)
        h2_ref[...] = ((x1 * r3) * (g2pre_ref[...] * (1.0 + mod_ref[4:5, :])) + mod_ref[3:4, :]).astype(BF16)

    vec = _const((1, D_MODEL))
    f32_rows = jax.ShapeDtypeStruct((s_len, D_MODEL), F32)
    bf16_rows = jax.ShapeDtypeStruct((s_len, D_MODEL), BF16)
    return pl.pallas_call(
        body, name="fwd_mix", grid=(s_len // ts,),
        in_specs=[_rows(ts, D_MODEL), _const((N_MOD, D_MODEL)), vec, vec, vec, _VMEM, _VMEM,
                  _const((N_HEADS, CHUNK, CHUNK)), _const((CHUNK, D_A)), _const((1, D_A)), _const((1, D_A)),
                  _const((N_HEADS, GROUP_DIM, GROUP_DIM)), _const((1, D_B)), _const((1, D_B))],
        out_specs=[_rows(ts, D_Z), _rows(ts, D_MODEL), _rows(ts, D_MODEL), _rows(ts, D_MODEL), _rows(ts, D_MODEL)],
        out_shape=[jax.ShapeDtypeStruct((s_len, D_Z), F32), bf16_rows, f32_rows, f32_rows, bf16_rows],
        scratch_shapes=[pltpu.VMEM((ts, D_A), F32), pltpu.VMEM((HALO, D_B), F32), pltpu.VMEM((D_MODEL, D_Z), BF16)],
        compiler_params=_params(),
    )(x, mod6, n1pre, n1post, n2pre, win_g, wout_g, w_spatial, bsp_full, gain, bias, w_pool, b_pool, pool_scale)


def _mixer_forward_tile(za, wc, bsp_ref, gain, bias, mixed_ref):
    ga, dga = _gelu_parts(za)
    u = ga[:, :D_A]
    v = ga[:, D_A:]
    mu = _rowmean(v)
    vc = v - mu
    rstd = lax.rsqrt(_rowmean(vc * vc) + EPS)
    vhat = vc * rstd
    vn = (vhat * gain + bias).astype(BF16)
    ts = za.shape[0]
    for k in range(ts // CHUNK):
        for h in range(N_HEADS):
            blk = vn[k * CHUNK:(k + 1) * CHUNK, h * HEAD_DIM:(h + 1) * HEAD_DIM]
            mixed_ref[k * CHUNK:(k + 1) * CHUNK, h * HEAD_DIM:(h + 1) * HEAD_DIM] = (
                _dot(wc[h], blk) + bsp_ref[:, h * HEAD_DIM:(h + 1) * HEAD_DIM])
    return u, vhat, rstd, vn, dga


def _fwd_fc1(h2, fc1_g, ts):
    s_len = h2.shape[0]
    cs = D_FF // N_CHIPS

    def body(h_ref, w_ref, q_ref):
        hb = h_ref[...]
        for j in range(N_CHIPS):
            p = jnp.maximum(_dot(hb, w_ref[j]), 0.0)
            q_ref[:, j * cs:(j + 1) * cs] = (p * p).astype(BF16)

    return pl.pallas_call(
        body, name="fwd_fc1", grid=(s_len // ts,),
        in_specs=[_rows(ts, D_MODEL), _VMEM],
        out_specs=_rows(ts, D_FF),
        out_shape=jax.ShapeDtypeStruct((s_len, D_FF), BF16),
        compiler_params=_params(),
    )(h2, fc1_g)


def _fwd_fc2_loss(q, x1, target, mod6, n2post, fc2_g, ts):
    s_len = q.shape[0]
    rs = D_FF // N_CHIPS

    def body(q_ref, x1_ref, t_ref, mod_ref, g_ref, w_ref, dy_ref, df_ref, loss_ref, s_ref):
        _zero_on_first_step(loss_ref, s_ref)
        gate_gain = mod_ref[5:6, :] * g_ref[...]
        f = _dot(q_ref[:, 0:rs], w_ref[0])
        for j in range(1, N_CHIPS):
            f = f + _dot(q_ref[:, j * rs:(j + 1) * rs], w_ref[j])
        r4 = lax.rsqrt(_rowmean(f * f) + EPS)
        fh = f * r4
        err = (x1_ref[...] + fh * gate_gain) - t_ref[...]
        loss_ref[...] += 0.5 * jnp.sum(_rowmean(err * err), axis=0, keepdims=True)
        dy = err * (1.0 / D_MODEL)
        dy_ref[...] = dy
        s_ref[...] += _colsum(dy * fh)
        gh = dy * gate_gain
        df_ref[...] = (r4 * (gh - fh * _rowmean(gh * fh))).astype(BF16)

    return pl.pallas_call(
        body, name="fwd_fc2_loss", grid=(s_len // ts,),
        in_specs=[_rows(ts, D_FF), _rows(ts, D_MODEL), _rows(ts, D_MODEL), _const((N_MOD, D_MODEL)),
                  _const((1, D_MODEL)), _VMEM],
        out_specs=[_rows(ts, D_MODEL), _rows(ts, D_MODEL), _const((1, 1)), _const((1, D_MODEL))],
        out_shape=[jax.ShapeDtypeStruct((s_len, D_MODEL), F32), jax.ShapeDtypeStruct((s_len, D_MODEL), BF16),
                   jax.ShapeDtypeStruct((1, 1), F32), jax.ShapeDtypeStruct((1, D_MODEL), F32)],
        compiler_params=_params(),
    )(q, x1, target, mod6, n2post, fc2_g)


def _join_w_in_on_first_step(win_ref, full_ref):
    cs = D_Z // N_CHIPS

    @pl.when(pl.program_id(0) == 0)
    def _():
        for j in range(N_CHIPS):
            full_ref[:, j * cs:(j + 1) * cs] = win_ref[j]


def _zero_on_first_step(*refs):
    @pl.when(pl.program_id(0) == 0)
    def _():
        for ref in refs:
            ref[...] = jnp.zeros_like(ref)


def _on_last_step(fn):
    pl.when(pl.program_id(0) == pl.num_programs(0) - 1)(fn)


def _store_shard_on_last_step(acc_ref, hbm_ref, sem, j):
    _on_last_step(lambda: pltpu.make_async_copy(acc_ref.at[j], hbm_ref.at[j], sem.at[j]).start())


def _wait_stores_on_last_step(*stores):
    def wait_all():
        for acc_ref, hbm_ref, sem in stores:
            for j in range(N_CHIPS):
                pltpu.make_async_copy(acc_ref.at[j], hbm_ref.at[j], sem.at[j]).wait()

    _on_last_step(wait_all)


def _bwd_fc2(df, q, fc2_g, ts):
    s_len = df.shape[0]
    cs = D_FF // N_CHIPS

    def body(df_ref, q_ref, w_ref, dp_ref, dw_hbm, dw_ref, dw_sem):
        _zero_on_first_step(dw_ref)
        dfb = df_ref[...]
        df2 = dfb * 2.0
        for j in range(N_CHIPS):
            qb = q_ref[:, j * cs:(j + 1) * cs]
            dw_ref[j] += _dot_tn(qb, dfb).reshape(2, cs // 2, D_MODEL)
            _store_shard_on_last_step(dw_ref, dw_hbm, dw_sem, j)
            dq2 = _dot_nt(df2, w_ref[j])
            dp_ref[:, j * cs:(j + 1) * cs] = (dq2 * jnp.sqrt(qb.astype(F32))).astype(BF16)
        _wait_stores_on_last_step((dw_ref, dw_hbm, dw_sem))

    dw_shape = (N_CHIPS, 2, cs // 2, D_MODEL)
    return pl.pallas_call(
        body, name="bwd_fc2", grid=(s_len // ts,),
        in_specs=[_rows(ts, D_MODEL), _rows(ts, D_FF), _VMEM],
        out_specs=[_rows(ts, D_FF), _ANY],
        out_shape=[jax.ShapeDtypeStruct((s_len, D_FF), BF16), jax.ShapeDtypeStruct(dw_shape, F32)],
        scratch_shapes=[pltpu.VMEM(dw_shape, F32), pltpu.SemaphoreType.DMA((N_CHIPS,))],
        compiler_params=_params(),
    )(df, q, fc2_g)


def _bwd_fc1_out(dp, dy, x1, mix, h2, ycat, mod6, n2pre, n1post, fc1_g, wout_g, dep, ts):
    s_len = dp.shape[0]
    cs = D_FF // N_CHIPS
    rs = D_MODEL // N_CHIPS

    def body(dp_ref, dy_ref, x1_ref, mix_ref, h2_ref, yc_ref, mod_ref, g2_ref, g1_ref, w1_ref, wo_ref, dep_ref,
             dx1_ref, dyc_ref, dshift2_ref, da2_ref, s1_ref, dw1_hbm, dwo_hbm, dw1_ref, dwo_ref, dw1_sem, dwo_sem):
        _zero_on_first_step(dshift2_ref, da2_ref, s1_ref, dw1_ref, dwo_ref)
        h2b = h2_ref[...]
        dh2 = None
        for j in range(N_CHIPS):
            dpb = dp_ref[:, j * cs:(j + 1) * cs]
            dw1_ref[j] += _dot_tn(h2b, dpb).reshape(2, D_MODEL // 2, cs)
            _store_shard_on_last_step(dw1_ref, dw1_hbm, dw1_sem, j)
            part = _dot_nt(dpb, w1_ref[j])
            dh2 = part if dh2 is None else dh2 + part
        x1 = x1_ref[...]
        r3 = lax.rsqrt(_rowmean(x1 * x1) + EPS)
        xh = x1 * r3
        a2 = g2_ref[...] * (1.0 + mod_ref[4:5, :])
        dshift2_ref[...] += _colsum(dh2)
        da2_ref[...] += _colsum(dh2 * xh)
        dxh = dh2 * a2
        dx1 = dy_ref[...] + r3 * (dxh - xh * _rowmean(dxh * xh))
        dx1_ref[...] = dx1

        mix = mix_ref[...]
        r2 = lax.rsqrt(_rowmean(mix * mix) + EPS)
        mh = mix * r2
        s1_ref[...] += _colsum(dx1 * mh)
        gh = dx1 * (mod_ref[2:3, :] * g1_ref[...])
        dmix = (r2 * (gh - mh * _rowmean(gh * mh))).astype(BF16)
        dwo_ref[...] += _dot_tn(yc_ref[...], dmix).reshape(N_CHIPS, 2, rs // 2, D_MODEL)
        for j in range(N_CHIPS):
            _store_shard_on_last_step(dwo_ref, dwo_hbm, dwo_sem, j)
            dyc_ref[:, j * rs:(j + 1) * rs] = _dot_nt(dmix, wo_ref[j])
        _wait_stores_on_last_step((dw1_ref, dw1_hbm, dw1_sem), (dwo_ref, dwo_hbm, dwo_sem))

    vec = jax.ShapeDtypeStruct((1, D_MODEL), F32)
    dw1_shape = (N_CHIPS, 2, D_MODEL // 2, cs)
    dwo_shape = (N_CHIPS, 2, rs // 2, D_MODEL)
    return pl.pallas_call(
        body, name="bwd_fc1_out", grid=(s_len // ts,),
        in_specs=[_rows(ts, D_FF), _rows(ts, D_MODEL), _rows(ts, D_MODEL), _rows(ts, D_MODEL), _rows(ts, D_MODEL),
                  _rows(ts, D_MODEL), _const((N_MOD, D_MODEL)), _const((1, D_MODEL)), _const((1, D_MODEL)), _VMEM,
                  _VMEM, _ANY],
        out_specs=[_rows(ts, D_MODEL), _rows(ts, D_MODEL)] + [_const((1, D_MODEL))] * 3 + [_ANY, _ANY],
        out_shape=[jax.ShapeDtypeStruct((s_len, D_MODEL), F32), jax.ShapeDtypeStruct((s_len, D_MODEL), F32),
                   vec, vec, vec, jax.ShapeDtypeStruct(dw1_shape, F32), jax.ShapeDtypeStruct(dwo_shape, F32)],
        scratch_shapes=[pltpu.VMEM(dw1_shape, F32), pltpu.VMEM(dwo_shape, F32), pltpu.SemaphoreType.DMA((N_CHIPS,)),
                        pltpu.SemaphoreType.DMA((N_CHIPS,))],
        compiler_params=_params(),
    )(dp, dy, x1, mix, h2, ycat, mod6, n2pre, n1post, fc1_g, wout_g, dep)


def _mixer_bwd(z, dyc, w_spatial, bsp_full, gain, bias, w_pool, b_pool, pool_scale, dep, ts):
    s_len = z.shape[0]
    nb = ts // HALO
    last = s_len // HALO - 1
    te = ts + HALO

    def body(z_ref, zprev_ref, znext_ref, dyc_ref, dynext_ref, ws_ref, bsp_ref, gain_ref, bias_ref, wp_ref, bp_ref,
             ps_ref, dep_ref, dz_ref, dws_ref, dbsp_ref, dgain_ref, dbias_ref, dwp_ref, dbp_ref, dps_ref, mixed_ref,
             dvn_ref):
        i = pl.program_id(0)

        @pl.when(i == 0)
        def _():
            for ref in (dws_ref, dbsp_ref, dgain_ref, dbias_ref, dwp_ref, dbp_ref, dps_ref):
                ref[...] = jnp.zeros_like(ref)

        wc = _tril_weights(ws_ref)
        gain = gain_ref[...]
        u, vhat, rstd, vn, dga = _mixer_forward_tile(z_ref[:, :2 * D_A], wc, bsp_ref, gain, bias_ref[...], mixed_ref)
        dya = dyc_ref[:, :D_A]
        du = dya * mixed_ref[...]
        dmixed = dya * u
        dmb = dmixed.astype(BF16)
        dm_sum = dmixed[0:CHUNK, :]
        for k in range(1, ts // CHUNK):
            dm_sum = dm_sum + dmixed[k * CHUNK:(k + 1) * CHUNK, :]
        r_idx = lax.broadcasted_iota(jnp.int32, (CHUNK, CHUNK), 0)
        s_idx = lax.broadcasted_iota(jnp.int32, (CHUNK, CHUNK), 1)
        causal = (s_idx <= r_idx).astype(F32)
        for h in range(N_HEADS):
            lanes = slice(h * HEAD_DIM, (h + 1) * HEAD_DIM)
            dbsp_ref[h] += jnp.sum(dm_sum[:, lanes], axis=1, keepdims=True)
            acc = None
            for k in range(ts // CHUNK):
                rows = slice(k * CHUNK, (k + 1) * CHUNK)
                t = _dot_nt(dmb[rows, lanes], vn[rows, lanes])
                acc = t if acc is None else acc + t
                dvn_ref[rows, lanes] = _dot_tn(wc[h], dmb[rows, lanes])
            dws_ref[h] += acc * causal
        dvn = dvn_ref[...]
        dgain_ref[...] += _colsum(dvn * vhat)
        dbias_ref[...] += _colsum(dvn)
        dvh = dvn * gain
        dv = rstd * (dvh - _rowmean(dvh) - vhat * _rowmean(dvh * vhat))
        dz_ref[:, :D_A] = (du * dga[:, :D_A]).astype(BF16)
        dz_ref[:, D_A:2 * D_A] = (dv * dga[:, D_A:]).astype(BF16)

        zb = z_ref[:, 2 * D_A:]
        prev = jnp.where(i == 0, 0.0, zprev_ref[...])
        zb_ext = jnp.concatenate([zb, znext_ref[...]], axis=0)
        sums = _causal_window_sums(jnp.concatenate([prev, zb_ext], axis=0))
        pos, inv_counts = _window_counts(i * ts, te)
        dyb_ext = jnp.concatenate([dyc_ref[:, D_A:], dynext_ref[...]], axis=0)
        dlin_ext = dyb_ext * ps_ref[...]
        dbp_ref[...] += _colsum(dlin_ext[:ts, :])
        scaled = []
        ddiffs = []
        lins = []
        for g in range(len(POOL_WINDOWS)):
            lanes = slice(g * GROUP_DIM, (g + 1) * GROUP_DIM)
            diff = (sums[g][HALO:, :] * inv_counts[g] - zb_ext[:, lanes]).astype(BF16)
            wpb = wp_ref[g].astype(BF16)
            dlb = dlin_ext[:, lanes].astype(BF16)
            lins.append(_dot(diff[:ts, :], wpb) + bp_ref[:, lanes])
            dwp_ref[g] += _dot_tn(diff[:ts, :], dlb[:ts, :])
            dd = _dot_nt(dlb, wpb)
            ddiffs.append(dd)
            scaled.append(jnp.where(pos < float(s_len), dd * inv_counts[g], 0.0))
        dps_ref[...] += _colsum(dyb_ext[:ts, :] * jnp.concatenate(lins, axis=1))
        back = _anticausal_window_sums(jnp.concatenate(scaled, axis=1))
        for g in range(len(POOL_WINDOWS)):
            dz_ref[:, 2 * D_A + g * GROUP_DIM:2 * D_A + (g + 1) * GROUP_DIM] = (
                back[g][:ts, :] - ddiffs[g][:ts, :]).astype(BF16)

    sq = jax.ShapeDtypeStruct((N_HEADS, CHUNK, CHUNK), F32)
    vec = jax.ShapeDtypeStruct((1, D_A), F32)
    return pl.pallas_call(
        body, name="mixer_bwd", grid=(s_len // ts,),
        in_specs=[_rows(ts, D_Z),
                  pl.BlockSpec((HALO, D_B), lambda i: (jnp.maximum(i * nb - 1, 0), 2)),
                  pl.BlockSpec((HALO, D_B), lambda i: (jnp.minimum((i + 1) * nb, last), 2)),
                  _rows(ts, D_MODEL),
                  pl.BlockSpec((HALO, D_B), lambda i: (jnp.minimum((i + 1) * nb, last), 1)),
                  _const((N_HEADS, CHUNK, CHUNK)), _const((CHUNK, D_A)), _const((1, D_A)), _const((1, D_A)),
                  _const((N_HEADS, GROUP_DIM, GROUP_DIM)), _const((1, D_B)), _const((1, D_B)), _ANY],
        out_specs=[_rows(ts, D_Z), _const((N_HEADS, CHUNK, CHUNK)), _const((N_HEADS, CHUNK, 1)), _const((1, D_A)),
                   _const((1, D_A)), _const((N_HEADS, GROUP_DIM, GROUP_DIM)), _const((1, D_B)), _const((1, D_B))],
        out_shape=[jax.ShapeDtypeStruct((s_len, D_Z), BF16), sq, jax.ShapeDtypeStruct((N_HEADS, CHUNK, 1), F32), vec,
                   vec, sq, vec, vec],
        scratch_shapes=[pltpu.VMEM((ts, D_A), F32), pltpu.VMEM((ts, D_A), F32)],
        compiler_params=_params(),
    )(z, z, z, dyc, dyc, w_spatial, bsp_full, gain, bias, w_pool, b_pool, pool_scale, dep)


def _bwd_in(dz, dx1, x, mod6, n1pre, win_g, dep, ts):
    s_len = x.shape[0]
    cs = D_Z // N_CHIPS

    def body(dz_ref, dx1_ref, x_ref, mod_ref, g_ref, w_ref, dep_ref, gx_ref, dshift_ref, da_ref, dw_hbm, dw_ref,
             wfull_ref, dw_sem):
        _zero_on_first_step(dshift_ref, da_ref, dw_ref)
        _join_w_in_on_first_step(w_ref, wfull_ref)
        xv = x_ref[...]
        r = lax.rsqrt(_rowmean(xv * xv) + EPS)
        xh = xv * r
        h1b = (xh * (g_ref[...] * (1.0 + mod_ref[1:2, :])) + mod_ref[0:1, :]).astype(BF16)
        dzb = dz_ref[...]
        dw = _dot_tn(h1b, dzb)
        for j in range(N_CHIPS):
            dw_ref[j] += dw[:, j * cs:(j + 1) * cs].reshape(2, D_MODEL // 2, cs)
        dh = _dot_nt(dzb, wfull_ref[...])
        a1 = g_ref[...] * (1.0 + mod_ref[1:2, :])
        dshift_ref[...] += _colsum(dh)
        da_ref[...] += _colsum(dh * xh)
        dxh = dh * a1
        gx_ref[...] = dx1_ref[...] + r * (dxh - xh * _rowmean(dxh * xh))
        for j in range(N_CHIPS):
            _store_shard_on_last_step(dw_ref, dw_hbm, dw_sem, j)
        _wait_stores_on_last_step((dw_ref, dw_hbm, dw_sem))

    vec = jax.ShapeDtypeStruct((1, D_MODEL), F32)
    dw_shape = (N_CHIPS, 2, D_MODEL // 2, cs)
    return pl.pallas_call(
        body, name="bwd_in", grid=(s_len // ts,),
        in_specs=[_rows(ts, D_Z), _rows(ts, D_MODEL), _rows(ts, D_MODEL), _const((N_MOD, D_MODEL)),
                  _const((1, D_MODEL)), _VMEM, _ANY],
        out_specs=[_rows(ts, D_MODEL), _const((1, D_MODEL)), _const((1, D_MODEL)), _ANY],
        out_shape=[jax.ShapeDtypeStruct((s_len, D_MODEL), F32), vec, vec, jax.ShapeDtypeStruct(dw_shape, F32)],
        scratch_shapes=[pltpu.VMEM(dw_shape, F32), pltpu.VMEM((D_MODEL, D_Z), BF16),
                        pltpu.SemaphoreType.DMA((N_CHIPS,))],
        compiler_params=_params(),
    )(dz, dx1, x, mod6, n1pre, win_g, dep)


def _adamw_math(w, g, m, v):
    m = ADAM_B1 * m + (1.0 - ADAM_B1) * g
    v = ADAM_B2 * v + (1.0 - ADAM_B2) * (g * g)
    m_hat = m / (1.0 - ADAM_B1 ** ADAM_STEP)
    v_hat = v / (1.0 - ADAM_B2 ** ADAM_STEP)
    delta = -ADAM_LR * (m_hat / (jnp.sqrt(v_hat) + ADAM_EPS) + ADAM_WD * w)
    return delta, m, v


def _adamw(gs, ws, ms, vs, name, steps):
    n = len(ws)

    def body(*refs):
        for i in range(n):
            g_ref, w_ref, m_ref, v_ref = refs[4 * i:4 * i + 4]
            d, nm, nv = _adamw_math(w_ref[...], g_ref[...], m_ref[...], v_ref[...])
            for ref, val in zip(refs[4 * n + 3 * i:4 * n + 3 * i + 3], (d, nm, nv)):
                ref[...] = val

    specs = [_rows(w.shape[0] // steps, w.shape[1]) for w in ws]
    out = pl.pallas_call(
        body, name=name, grid=(steps,),
        in_specs=[s for s in specs for _ in range(4)], out_specs=[s for s in specs for _ in range(3)],
        out_shape=[jax.ShapeDtypeStruct(w.shape, F32) for w in ws for _ in range(3)],
        compiler_params=_params(),
    )(*[a for quad in zip(gs, ws, ms, vs) for a in quad])
    return [out[3 * i:3 * i + 3] for i in range(n)]


def _ada_grad_adamw(sc_t, dmod_shard, w, m, v, tr):
    rows, cols = w.shape

    def body(s_ref, dm_ref, w_ref, m_ref, v_ref, g_ref, d_ref, nm_ref, nv_ref):
        g = s_ref[:, 0:1] * dm_ref[0:1, :]
        for b in range(1, N_DEV):
            g = g + s_ref[:, b:b + 1] * dm_ref[b:b + 1, :]
        g_ref[...] = g
        d, nm, nv = _adamw_math(w_ref[...], g, m_ref[...], v_ref[...])
        d_ref[...] = d
        nm_ref[...] = nm
        nv_ref[...] = nv

    spec = _rows(tr, cols)
    shape = jax.ShapeDtypeStruct((rows, cols), F32)
    return pl.pallas_call(
        body, name="ada_grad_adamw", grid=(rows // tr,),
        in_specs=[_rows(tr, N_DEV), _const((N_DEV, cols)), spec, spec, spec],
        out_specs=[spec] * 4, out_shape=[shape] * 4, compiler_params=_params(),
    )(sc_t, dmod_shard, w, m, v)


def _mod_grads(da1, dshift1, s1, da2, dshift2, s2, mod6, n1pre, n1post, n2pre, n2post):
    def body(da1_ref, ds1_ref, s1_ref, da2_ref, ds2_ref, s2_ref, mod_ref, n1_ref, p1_ref, n2_ref, p2_ref, dmod_ref,
             dn_ref):
        dmod_ref[0:1, :] = ds1_ref[...]
        dmod_ref[1:2, :] = da1_ref[...] * n1_ref[...]
        dmod_ref[2:3, :] = s1_ref[...] * p1_ref[...]
        dmod_ref[3:4, :] = ds2_ref[...]
        dmod_ref[4:5, :] = da2_ref[...] * n2_ref[...]
        dmod_ref[5:6, :] = s2_ref[...] * p2_ref[...]
        dn_ref[0:1, :] = da1_ref[...] * (1.0 + mod_ref[1:2, :])
        dn_ref[1:2, :] = s1_ref[...] * mod_ref[2:3, :]
        dn_ref[2:3, :] = da2_ref[...] * (1.0 + mod_ref[4:5, :])
        dn_ref[3:4, :] = s2_ref[...] * mod_ref[5:6, :]

    return pl.pallas_call(
        body, name="mod_grads",
        out_shape=[jax.ShapeDtypeStruct((N_MOD, D_MODEL), F32), jax.ShapeDtypeStruct((4, D_MODEL), F32)],
    )(da1, dshift1, s1, da2, dshift2, s2, mod6, n1pre, n1post, n2pre, n2post)


def _position():
    x, y, c = lax.axis_index("x"), lax.axis_index("y"), lax.axis_index("c")
    return x, y, c


def _flip(v, bit):
    return 1 - v if bit else v


def _peer(x, y, c, k):
    return (_flip(x, k & 4), _flip(y, k & 2), _flip(c, k & 1))


def _remote(src, dst, send_sem, recv_sem, device):
    return pltpu.make_async_remote_copy(src_ref=src, dst_ref=dst, send_sem=send_sem, recv_sem=recv_sem,
                                        device_id=device, device_id_type=MESH)


def _mod_exchange(c_row, w_ada_shard, b_ada_row, ws, early):
    cs = w_ada_shard.shape[1]
    n = len(ws)

    def body(c_ref, w_hbm, b_ref, *refs):
        shards, (mod_ref, sc_ref), zones = refs[:n], refs[n:n + 2], refs[n + 2:2 * n + 2]
        rows_ref, w_ref, w_sem, send1, recv1, send2, recv2 = refs[2 * n + 2:2 * n + 9]
        wide, narrow = refs[2 * n + 9:3 * n + 9], refs[3 * n + 9:4 * n + 9]
        cast_load, cast_store = refs[4 * n + 9:]
        x, y, c = _position()
        me = 4 * x + 2 * y + c
        chip = 2 * x + y
        w_load = pltpu.make_async_copy(w_hbm, w_ref, w_sem)
        w_load.start()
        loads = [pltpu.make_async_copy(shards[i], wide[i], cast_load.at[i]) for i in range(n)]
        stores = [pltpu.make_async_copy(narrow[i], zones[i].at[chip], cast_store.at[i]) for i in range(n)]

        def cast(i):
            loads[i].wait()
            narrow[i][...] = wide[i][...].astype(BF16)
            stores[i].start()

        for i in range(early):
            loads[i].start()
        cv = c_ref[...]
        sc_ref[me] = cv * jax.nn.sigmoid(cv)
        gather = [_remote(sc_ref.at[me], sc_ref.at[me], send1.at[k - 1], recv1.at[k - 1], _peer(x, y, c, k))
                  for k in range(1, N_DEV)]
        for cp in gather:
            cp.start()
        for i in range(early):
            cast(i)
        for i in range(early, n):
            loads[i].start()
        for k in range(1, N_DEV):
            px, py, pc = _peer(x, y, c, k)
            src = 4 * px + 2 * py + pc
            _remote(sc_ref.at[src], sc_ref.at[src], send1.at[k - 1], recv1.at[k - 1], (px, py, pc)).wait_recv()
        for cp in gather:
            cp.wait_send()
        sc_all = jnp.concatenate([sc_ref[b] for b in range(N_DEV)], axis=0)
        w_load.wait()
        part = _dot(sc_all.astype(BF16), w_ref[...].astype(BF16))
        part = part + b_ref[:, pl.ds(pl.multiple_of(chip * cs, LANES), cs)]
        for b in range(N_DEV):
            rows_ref[b] = part[b:b + 1, :]
        mod_ref[chip] = rows_ref[me]
        hand = []
        for k in (2, 4, 6):
            px, py, _ = _peer(x, y, c, k)
            hand.append(_remote(rows_ref.at[4 * px + 2 * py + c], mod_ref.at[chip], send2.at[k // 2 - 1],
                                recv2.at[k // 2 - 1], (px, py, c)))
        for cp in hand:
            cp.start()
        for i in range(early, n):
            cast(i)
        for k in (2, 4, 6):
            px, py, _ = _peer(x, y, c, k)
            pchip = 2 * px + py
            _remote(rows_ref.at[me], mod_ref.at[pchip], send2.at[k // 2 - 1], recv2.at[k // 2 - 1],
                    (px, py, c)).wait_recv()
        for cp in hand:
            cp.wait_send()
        for cp in stores:
            cp.wait()

    out = pl.pallas_call(
        body, name="mod_exchange",
        in_specs=[_VMEM, _ANY, _VMEM] + [_ANY] * n, out_specs=[_VMEM, _VMEM] + [_ANY] * n,
        out_shape=[jax.ShapeDtypeStruct((N_CHIPS, 1, cs), F32), jax.ShapeDtypeStruct((N_DEV, 1, D_MODEL), F32)]
        + [jax.ShapeDtypeStruct((N_CHIPS,) + w.shape, BF16) for w in ws],
        scratch_shapes=[pltpu.VMEM((N_DEV, 1, cs), F32), pltpu.VMEM(w_ada_shard.shape, F32), pltpu.SemaphoreType.DMA,
                        pltpu.SemaphoreType.DMA((N_DEV - 1,)),
                        pltpu.SemaphoreType.DMA((N_DEV - 1,)), pltpu.SemaphoreType.DMA((N_CHIPS - 1,)),
                        pltpu.SemaphoreType.DMA((N_CHIPS - 1,))]
        + [pltpu.VMEM(w.shape, F32) for w in ws] + [pltpu.VMEM(w.shape, BF16) for w in ws]
        + [pltpu.SemaphoreType.DMA((n,))] * 2,
        compiler_params=pltpu.CompilerParams(vmem_limit_bytes=VMEM_LIMIT),
    )(c_row, w_ada_shard, b_ada_row, *ws)
    return out[0], out[1], list(out[2:])


_HBM = pl.BlockSpec(memory_space=pltpu.HBM)
_SEM = pl.BlockSpec(memory_space=pltpu.SEMAPHORE)
_EFFECT = pltpu.SideEffectType.DATAFLOW_SIDE_EFFECTING
_CHIP_HOPS = (2, 4, 6)


def _in_hbm(a):
    return pltpu.with_memory_space_constraint(a, pltpu.HBM)


def _sems3():
    return pltpu.SemaphoreType.DMA((len(_CHIP_HOPS),))


def _ag_start(lands, after, name):
    n = len(lands)

    def body(*refs):
        zones = refs[:n]
        sends, recvs = refs[n + 1:2 * n + 1], refs[2 * n + 1:3 * n + 1]
        x, y, c = _position()
        chip = 2 * x + y
        for i in range(n):
            half = zones[i].shape[1] // 2
            mine = zones[i].at[chip, pl.ds(c * half, half)]
            for s, k in enumerate(_CHIP_HOPS):
                px, py, _ = _peer(x, y, c, k)
                _remote(mine, mine, sends[i].at[s], recvs[i].at[s], (px, py, c)).start()

    out = pl.pallas_call(
        body, name=name,
        in_specs=[_HBM] * n + [_ANY],
        out_specs=[_SEM] * (2 * n) + [_HBM] * n,
        out_shape=[_sems3()] * (2 * n) + [pltpu.HBM(z.shape, BF16) for z in lands],
        input_output_aliases={i: 2 * n + i for i in range(n)},
        compiler_params=pltpu.CompilerParams(has_side_effects=_EFFECT),
    )(*[_in_hbm(z) for z in lands], after)
    return [(out[2 * n + i], out[i], out[n + i]) for i in range(n)]


def _ag_pass(group, after, name):
    n = len(group)

    def body(*refs):
        zones = refs[:n]
        sends, recvs = refs[n:2 * n], refs[2 * n:3 * n]
        fsends, frecvs = refs[4 * n + 1:5 * n + 1], refs[5 * n + 1:6 * n + 1]
        x, y, c = _position()
        chip = 2 * x + y
        for i in range(n):
            half = zones[i].shape[1] // 2
            rows = pl.ds(c * half, half)
            for s, k in enumerate(_CHIP_HOPS):
                px, py, _ = _peer(x, y, c, k)
                landed = zones[i].at[2 * px + py, rows]
                _remote(landed, landed, sends[i].at[s], recvs[i].at[s], (px, py, c)).wait_recv()
                _remote(landed, landed, fsends[i].at[s], frecvs[i].at[s], (x, y, 1 - c)).start()
        for i in range(n):
            half = zones[i].shape[1] // 2
            mine = zones[i].at[chip, pl.ds(c * half, half)]
            for s, k in enumerate(_CHIP_HOPS):
                px, py, _ = _peer(x, y, c, k)
                _remote(mine, mine, sends[i].at[s], recvs[i].at[s], (px, py, c)).wait_send()

    out = pl.pallas_call(
        body, name=name,
        in_specs=[_HBM] * n + [_SEM] * (2 * n) + [_ANY],
        out_specs=[_HBM] * n + [_SEM] * (2 * n),
        out_shape=[pltpu.HBM(g[0].shape, BF16) for g in group] + [_sems3()] * (2 * n),
        input_output_aliases={i: i for i in range(n)},
        compiler_params=pltpu.CompilerParams(has_side_effects=_EFFECT),
    )(*[g[0] for g in group], *[g[1] for g in group], *[g[2] for g in group], after)
    return [(out[i], out[n + i], out[2 * n + i]) for i in range(n)]


def _ag_done(group, name):
    n = len(group)

    def body(*refs):
        lands = refs[:n]
        fsends, frecvs = refs[n:2 * n], refs[2 * n:3 * n]
        x, y, c = _position()
        for i in range(n):
            half = lands[i].shape[1] // 2
            for s, k in enumerate(_CHIP_HOPS):
                px, py, _ = _peer(x, y, c, k)
                sent = lands[i].at[2 * px + py, pl.ds(c * half, half)]
                got = lands[i].at[2 * px + py, pl.ds((1 - c) * half, half)]
                cp = _remote(sent, got, fsends[i].at[s], frecvs[i].at[s], (x, y, 1 - c))
                cp.wait_recv()
                cp.wait_send()

    out = pl.pallas_call(
        body, name=name,
        in_specs=[_HBM] * n + [_SEM] * (2 * n),
        out_specs=[_HBM] * n,
        out_shape=[pltpu.HBM(g[0].shape, BF16) for g in group],
        input_output_aliases={i: i for i in range(n)},
        compiler_params=pltpu.CompilerParams(has_side_effects=_EFFECT),
    )(*[g[0] for g in group], *[g[1] for g in group], *[g[2] for g in group])
    return list(out)


def _small_spread_wait(slots, sends, recvs, afters):
    def body(z_ref, sends, recvs, *rest):
        x, y, c = _position()
        mine = z_ref.at[2 * x + y]
        for s, k in enumerate(_CHIP_HOPS):
            px, py, _ = _peer(x, y, c, k)
            cp = _remote(mine, z_ref.at[2 * px + py], sends.at[s], recvs.at[s], (px, py, c))
            cp.wait_recv()
            cp.wait_send()

    return pl.pallas_call(
        body, name="small_spread_wait",
        in_specs=[_HBM, _SEM, _SEM] + [_ANY] * len(afters), out_specs=_HBM, out_shape=pltpu.HBM(slots.shape, F32),
        input_output_aliases={0: 0},
        compiler_params=pltpu.CompilerParams(has_side_effects=_EFFECT),
    )(slots, sends, recvs, *afters)


def _sibling_sum(pgs, name, small=None):
    n = len(pgs)
    k = 0 if small is None else 1
    units = [(i, j) for i in range(n) for j in range(N_CHIPS)]

    def body(*refs):
        refs = list(refs)
        take = lambda count: [refs.pop(0) for _ in range(count)]
        ins, small_in = take(n), take(k)
        qbs, owns, slots_out = take(n), take(n), take(k)
        mine, other, stage, got = take(n), take(n), take(n), take(n)
        load_a, load_b, send, recv, keep_wire, keep_own = take(6)
        x, y, c = _position()
        chip = 2 * x + y
        if k:
            sib_ref, pair_send, pair_recv, keep_small = take(4)
            pair = _remote(small_in[0], sib_ref, pair_send, pair_recv, (x, y, 1 - c))
            pair.start()
        loads_a = [pltpu.make_async_copy(ins[i].at[j, 1 - c], other[i].at[j], load_a.at[u])
                   for u, (i, j) in enumerate(units)]
        loads_b = [pltpu.make_async_copy(ins[i].at[j, c], mine[i].at[j], load_b.at[u])
                   for u, (i, j) in enumerate(units)]
        for cp in loads_a + loads_b:
            cp.start()
        sent = []
        for u, (i, j) in enumerate(units):
            loads_a[u].wait()
            stage[i][j] = other[i][j].astype(BF16)
            cp = _remote(stage[i].at[j], got[i].at[j], send.at[u], recv.at[u], (x, y, 1 - c))
            cp.start()
            sent.append(cp)
        stores = []
        for u, (i, j) in enumerate(units):
            loads_b[u].wait()
            sent[u].wait_recv()
            q = mine[i][j] + got[i][j].astype(F32)
            mine[i][j] = q
            got[i][j] = q.astype(BF16)
            stores.append(pltpu.make_async_copy(got[i].at[j], qbs[i].at[j], keep_wire.at[u]))
            stores[-1].start()
        for i in range(n):
            stores.append(pltpu.make_async_copy(mine[i].at[chip], owns[i], keep_own.at[i]))
            stores[-1].start()
        if k:
            pair.wait()
            sib_ref[...] = small_in[0][...] + sib_ref[...]
            stores.append(pltpu.make_async_copy(sib_ref, slots_out[0].at[chip], keep_small))
            stores[-1].start()
        for cp in sent:
            cp.wait_send()
        for cp in stores:
            cp.wait()

    wire = [(N_CHIPS,) + p.shape[2:] for p in pgs]
    extra_out, extra_scratch = [], []
    if k:
        extra_out = [jax.ShapeDtypeStruct((N_CHIPS,) + small.shape, F32)]
        extra_scratch = [pltpu.VMEM(small.shape, F32)] + [pltpu.SemaphoreType.DMA] * 3
    out = pl.pallas_call(
        body, name=name, in_specs=[_ANY] * n + [_VMEM] * k, out_specs=[_ANY] * (2 * n + k),
        out_shape=[jax.ShapeDtypeStruct(w, BF16) for w in wire] + [jax.ShapeDtypeStruct(w[1:], F32) for w in wire]
        + extra_out,
        scratch_shapes=[pltpu.VMEM(w, F32) for w in wire] * 2 + [pltpu.VMEM(w, BF16) for w in wire] * 2
        + [pltpu.SemaphoreType.DMA((len(units),))] * 5 + [pltpu.SemaphoreType.DMA((n,))] + extra_scratch,
        compiler_params=pltpu.CompilerParams(vmem_limit_bytes=VMEM_LIMIT),
    )(*pgs, *([small] if k else []))
    return list(out[:n]), list(out[n:2 * n]), list(out[2 * n:])


def _rs_start(qbs, name, small=None):
    n = len(qbs)
    k = 0 if small is None else 1

    def body(*refs):
        outs, inboxes, slots = refs[:n], refs[n:2 * n], refs[2 * n:2 * n + k]
        sems = refs[2 * n + k:4 * n + 3 * k]
        sends, recvs = sems[:n], sems[n:2 * n]
        x, y, c = _position()
        chip = 2 * x + y
        if k:
            mine = slots[0].at[chip]
            for s, hop in enumerate(_CHIP_HOPS):
                px, py, _ = _peer(x, y, c, hop)
                _remote(mine, mine, sems[2 * n].at[s], sems[2 * n + 1].at[s], (px, py, c)).start()
        for i in range(n):
            for s, hop in enumerate(_CHIP_HOPS):
                px, py, _ = _peer(x, y, c, hop)
                _remote(outs[i].at[2 * px + py], inboxes[i].at[chip], sends[i].at[s], recvs[i].at[s], (px, py, c)).start()

    inboxes = [_in_hbm(lax.empty(q.shape, BF16)) for q in qbs]
    n_sems = 2 * n + 2 * k
    out = pl.pallas_call(
        body, name=name,
        in_specs=[_HBM] * (2 * n + k),
        out_specs=[_SEM] * n_sems + [_HBM] * (2 * n + k),
        out_shape=[_sems3()] * n_sems + [pltpu.HBM(q.shape, BF16) for q in qbs] * 2
        + ([pltpu.HBM(small.shape, F32)] if k else []),
        input_output_aliases={i: n_sems + i for i in range(2 * n + k)},
        compiler_params=pltpu.CompilerParams(has_side_effects=_EFFECT),
    )(*[_in_hbm(q) for q in qbs], *inboxes, *([_in_hbm(small)] if k else []))
    states = [(out[n_sems + i], out[n_sems + n + i], out[i], out[n + i]) for i in range(n)]
    return (states, (out[n_sems + 2 * n], out[2 * n], out[2 * n + 1])) if k else states


def _rs_wait(group, after, name):
    n = len(group)

    def body(*refs):
        outs, inboxes = refs[:n], refs[n:2 * n]
        sends, recvs = refs[2 * n:3 * n], refs[3 * n:4 * n]
        x, y, c = _position()
        for i in range(n):
            for s, k in enumerate(_CHIP_HOPS):
                px, py, _ = _peer(x, y, c, k)
                slot = 2 * px + py
                cp = _remote(outs[i].at[slot], inboxes[i].at[slot], sends[i].at[s], recvs[i].at[s], (px, py, c))
                cp.wait_recv()
                cp.wait_send()

    out = pl.pallas_call(
        body, name=name,
        in_specs=[_HBM] * (2 * n) + [_SEM] * (2 * n) + [_ANY],
        out_specs=[_HBM] * n,
        out_shape=[pltpu.HBM(g[1].shape, BF16) for g in group],
        input_output_aliases={n + i: i for i in range(n)},
        compiler_params=pltpu.CompilerParams(has_side_effects=_EFFECT),
    )(*[g[0] for g in group], *[g[1] for g in group], *[g[2] for g in group], *[g[3] for g in group], after)
    return list(out)


def _final_share(inboxes, owns, name):
    n = len(inboxes)
    units = [(i, s) for i in range(n) for s in range(len(_CHIP_HOPS))]

    def body(*refs):
        ins, mine, outs, landed, half = (refs[k * n:(k + 1) * n] for k in range(5))
        load, load_own, keep, send, recv = refs[5 * n:]
        x, y, c = _position()
        loads = []
        for u, (i, s) in enumerate(units):
            px, py, _ = _peer(x, y, c, _CHIP_HOPS[s])
            loads.append(pltpu.make_async_copy(ins[i].at[2 * px + py], landed[i].at[s], load.at[u]))
        loads_own = [pltpu.make_async_copy(mine[i], half[i], load_own.at[i]) for i in range(n)]
        for cp in loads + loads_own:
            cp.start()
        copies = []
        for i in range(n):
            for s in range(len(_CHIP_HOPS)):
                loads[len(_CHIP_HOPS) * i + s].wait()
            loads_own[i].wait()
            total = (landed[i][0].astype(F32) + landed[i][1].astype(F32)) + landed[i][2].astype(F32)
            half[i][...] = total + half[i][...]
            copies.append(pltpu.make_async_copy(half[i], outs[i].at[c], keep.at[i]))
            copies.append(_remote(half[i], outs[i].at[c], send.at[i], recv.at[i], (x, y, 1 - c)))
            for cp in copies[-2:]:
                cp.start()
        for i in range(n):
            theirs = outs[i].at[1 - c]
            _remote(theirs, theirs, send.at[i], recv.at[i], (x, y, 1 - c)).wait_recv()
        for i in range(n):
            copies[2 * i].wait()
            copies[2 * i + 1].wait_send()

    return pl.pallas_call(
        body, name=name, in_specs=[_ANY] * (2 * n), out_specs=[_ANY] * n,
        out_shape=[jax.ShapeDtypeStruct((2,) + o.shape, F32) for o in owns],
        scratch_shapes=[pltpu.VMEM((len(_CHIP_HOPS),) + o.shape, BF16) for o in owns]
        + [pltpu.VMEM(o.shape, F32) for o in owns]
        + [pltpu.SemaphoreType.DMA((len(units),))] + [pltpu.SemaphoreType.DMA((n,))] * 4,
        compiler_params=pltpu.CompilerParams(vmem_limit_bytes=VMEM_LIMIT),
    )(*inboxes, *owns)


_SMALL = (("b_ada", N_MOD * D_MODEL), ("norm1_pre", D_MODEL), ("norm1_post", D_MODEL), ("norm2_pre", D_MODEL),
          ("norm2_post", D_MODEL), ("w_spatial", N_HEADS * CHUNK * CHUNK), ("b_spatial", N_HEADS * CHUNK),
          ("ln_v_gain", D_A), ("ln_v_bias", D_A), ("w_pool", N_HEADS * GROUP_DIM * GROUP_DIM),
          ("b_pool", D_B), ("pool_scale", D_B))
_MOD_ROWS = N_MOD * D_MODEL // LANES


def _packed_rows(size):
    return -(-(size // LANES) // SUBLANES) * SUBLANES


def _pack(parts):
    out = []
    for name, size in _SMALL:
        a = parts[name].reshape(size // LANES, LANES)
        pad = _packed_rows(size) - a.shape[0]
        out.append(jnp.pad(a, ((0, pad), (0, 0))) if pad else a)
    return out


def _small_adamw(slots, ws, ms, vs):
    n = len(_SMALL)
    head = N_DEV * _MOD_ROWS

    def body(*refs):
        s_ref, w, m, v = refs[0], refs[1:1 + n], refs[1 + n:1 + 2 * n], refs[1 + 2 * n:1 + 3 * n]
        outs = refs[1 + 3 * n:1 + 7 * n]
        dmod_ref, loss_ref, t_ref = refs[1 + 7 * n:]
        t_ref[...] = ((s_ref[0] + s_ref[1]) + s_ref[2]) + s_ref[3]
        dmod_ref[...] = t_ref[0:head, :]
        loss_ref[...] = t_ref[t_ref.shape[0] - 8:, :]
        row = head
        for i, (_, size) in enumerate(_SMALL):
            if i == 0:
                g = t_ref[0:_MOD_ROWS, :]
                for b in range(1, N_DEV):
                    g = g + t_ref[b * _MOD_ROWS:(b + 1) * _MOD_ROWS, :]
            else:
                g = t_ref[row:row + size // LANES, :]
                row += _packed_rows(size)
            d, nm, nv = _adamw_math(w[i][...], g, m[i][...], v[i][...])
            for ref, val in zip(outs[4 * i:4 * i + 4], (g, d, nm, nv)):
                ref[...] = val

    each = [jax.ShapeDtypeStruct((size // LANES, LANES), F32) for _, size in _SMALL for _ in range(4)]
    out = pl.pallas_call(
        body, name="small_adamw",
        out_shape=each + [jax.ShapeDtypeStruct((head, LANES), F32), jax.ShapeDtypeStruct((SUBLANES, LANES), F32)],
        scratch_shapes=[pltpu.VMEM(slots.shape[1:], F32)],
        compiler_params=pltpu.CompilerParams(vmem_limit_bytes=VMEM_LIMIT),
    )(slots, *ws, *ms, *vs)
    return [out[4 * i:4 * i + 4] for i in range(n)], out[4 * n], out[4 * n + 1]


def kernel(x, c, w_ada, b_ada, norm1_pre, norm1_post, w_in, w_spatial, b_spatial, ln_v_gain, ln_v_bias, w_pool, b_pool, pool_scale, w_out, norm2_pre, norm2_post, w_fc1, w_fc2, loss_target, m_w_ada, m_b_ada, m_norm1_pre, m_norm1_post, m_w_in, m_w_spatial, m_b_spatial, m_ln_v_gain, m_ln_v_bias, m_w_pool, m_b_pool, m_pool_scale, m_w_out, m_norm2_pre, m_norm2_post, m_w_fc1, m_w_fc2, v_w_ada, v_b_ada, v_norm1_pre, v_norm1_post, v_w_in, v_w_spatial, v_b_spatial, v_ln_v_gain, v_ln_v_bias, v_w_pool, v_b_pool, v_pool_scale, v_w_out, v_norm2_pre, v_norm2_post, v_w_fc1, v_w_fc2):
    weights = dict(w_ada=w_ada, b_ada=b_ada, norm1_pre=norm1_pre, norm1_post=norm1_post, w_in=w_in,
                   w_spatial=w_spatial, b_spatial=b_spatial, ln_v_gain=ln_v_gain, ln_v_bias=ln_v_bias, w_pool=w_pool,
                   b_pool=b_pool, pool_scale=pool_scale, w_out=w_out, norm2_pre=norm2_pre, norm2_post=norm2_post,
                   w_fc1=w_fc1, w_fc2=w_fc2)
    m_old = dict(w_ada=m_w_ada, b_ada=m_b_ada, norm1_pre=m_norm1_pre, norm1_post=m_norm1_post, w_in=m_w_in,
                 w_spatial=m_w_spatial, b_spatial=m_b_spatial, ln_v_gain=m_ln_v_gain, ln_v_bias=m_ln_v_bias,
                 w_pool=m_w_pool, b_pool=m_b_pool, pool_scale=m_pool_scale, w_out=m_w_out, norm2_pre=m_norm2_pre,
                 norm2_post=m_norm2_post, w_fc1=m_w_fc1, w_fc2=m_w_fc2)
    v_old = dict(w_ada=v_w_ada, b_ada=v_b_ada, norm1_pre=v_norm1_pre, norm1_post=v_norm1_post, w_in=v_w_in,
                 w_spatial=v_w_spatial, b_spatial=v_b_spatial, ln_v_gain=v_ln_v_gain, ln_v_bias=v_ln_v_bias,
                 w_pool=v_w_pool, b_pool=v_b_pool, pool_scale=v_pool_scale, w_out=v_w_out, norm2_pre=v_norm2_pre,
                 norm2_post=v_norm2_post, w_fc1=v_w_fc1, w_fc2=v_w_fc2)
    order = ("w_ada", "b_ada", "norm1_pre", "norm1_post", "w_in", "w_spatial", "b_spatial", "ln_v_gain", "ln_v_bias",
             "w_pool", "b_pool", "pool_scale", "w_out", "norm2_pre", "norm2_post", "w_fc1", "w_fc2")
    mx, my, mc = _position()
    me = 4 * mx + 2 * my + mc
    chip = 2 * mx + my
    row = lambda a: a.reshape(1, -1)

    xs, target = x[0], loss_target[0]
    n1pre, n1post, n2pre, n2post = row(norm1_pre), row(norm1_post), row(norm2_pre), row(norm2_post)
    mixer = (w_spatial, jnp.repeat(b_spatial.T, HEAD_DIM, axis=1), row(ln_v_gain), row(ln_v_bias), w_pool,
             row(b_pool), row(pool_scale))
    ts_big, ts_mid = 512, 256

    mod4, sc_all, lands = _mod_exchange(c, w_ada, row(b_ada), [w_in, w_out, w_fc1, w_fc2], 2)
    mod6 = mod4.reshape(N_MOD, D_MODEL)
    ag = _ag_start(lands, mod4, "ag_start")

    win_g, wout_g = _ag_done(_ag_pass([ag[0], ag[1]], ag[2][0], "ag_pass_mix"), "ag_done_mix")
    z, ycat, mix, x1, h2 = _fwd_mix(xs, mod6, n1pre, n1post, n2pre, win_g, wout_g, *mixer, ts_big)
    (fc1_g,) = _ag_done(_ag_pass([ag[2]], h2, "ag_pass_fc1"), "ag_done_fc1")
    q = _fwd_fc1(h2, fc1_g, ts_big)
    (fc2_g,) = _ag_done(_ag_pass([ag[3]], q, "ag_pass_fc2"), "ag_done_fc2")
    dy, df, loss, s2 = _fwd_fc2_loss(q, x1, target, mod6, n2post, fc2_g, ts_big)

    def reduce_start(partials, tag, small=None):
        wire, owns, slots = _sibling_sum(partials, "sibling_sum_" + tag, small)
        return _rs_start(wire, "rs_start_" + tag), owns, slots

    def reduce_finish(state, owns, names, tag, dep):
        inboxes = _rs_wait(state, dep, "rs_wait_" + tag)
        shards = _final_share(inboxes, owns, "final_share_" + tag)
        for n, g in zip(names, shards):
            grads[n] = g.reshape(weights[n].shape)
        updates = _adamw([grads[n] for n in names], [weights[n] for n in names], [m_old[n] for n in names],
                         [v_old[n] for n in names], "adamw_" + tag, 4)
        for n, (d, nm, nv) in zip(names, updates):
            deltas[n], new_m[n], new_v[n] = d, nm, nv

    grads, deltas, new_m, new_v = {}, {}, {}, {}
    dp, g_fc2 = _bwd_fc2(df, q, fc2_g, ts_big)
    state_fc2, owns_fc2, _ = reduce_start([g_fc2], "fc2")
    dx1, dyc, dshift2, da2, s1, g_fc1, g_out = _bwd_fc1_out(
        dp, dy, x1, mix, h2, ycat, mod6, n2pre, n1post, fc1_g, wout_g, state_fc2[0][0], ts_mid)
    state_mid, owns_mid, _ = reduce_start([g_fc1, g_out], "mid")
    dz, dws, dbsp, dgain, dbias, dwp, dbp, dps = _mixer_bwd(z, dyc, *mixer, state_mid[0][0], ts_big)
    grad_x, dshift1, da1, g_in = _bwd_in(dz, dx1, xs, mod6, n1pre, win_g, state_mid[0][0], ts_big)
    dmod6, dnorms = _mod_grads(da1, dshift1, s1, da2, dshift2, s2, mod6, n1pre, n1post, n2pre, n2post)

    parts = dict(b_ada=dmod6, norm1_pre=dnorms[0], norm1_post=dnorms[1], norm2_pre=dnorms[2], norm2_post=dnorms[3],
                 w_spatial=dws, b_spatial=dbsp, ln_v_gain=dgain, ln_v_bias=dbias, w_pool=dwp, b_pool=dbp,
                 pool_scale=dps)
    pieces = _pack(parts)
    slots = lax.dynamic_update_slice(jnp.zeros((N_DEV * _MOD_ROWS, LANES), F32), pieces[0], (me * _MOD_ROWS, 0))
    loss_tile = jnp.pad(loss, ((0, SUBLANES - 1), (0, LANES - 1)))
    wire_in, owns_in, pair_sum = _sibling_sum([g_in], "sibling_sum_in",
                                              jnp.concatenate([slots] + pieces[1:] + [loss_tile], axis=0))
    state_in, spread = _rs_start(wire_in, "rs_start_in", pair_sum[0])
    reduce_finish(state_fc2 + state_mid, owns_fc2 + owns_mid, ("w_fc2", "w_fc1", "w_out"), "mlp", state_in[0][0])
    flat = lambda d: [d[n].reshape(size // LANES, LANES) for n, size in _SMALL]
    small_out, dmod_all, loss_tile = _small_adamw(
        _small_spread_wait(*spread, [deltas[n] for n in ("w_fc2", "w_fc1", "w_out")]), flat(weights), flat(m_old),
        flat(v_old))
    loss = loss_tile[0, 0]
    for (n, _), (g, d, nm, nv) in zip(_SMALL, small_out):
        shape = weights[n].shape
        grads[n], deltas[n], new_m[n], new_v[n] = g.reshape(shape), d.reshape(shape), nm.reshape(shape), nv.reshape(shape)

    dmod_all = dmod_all.reshape(N_DEV, N_MOD * D_MODEL)
    cs = w_ada.shape[1]
    dmod_shard = lax.dynamic_slice(dmod_all, (0, chip * cs), (N_DEV, cs))
    sc_t = sc_all.reshape(N_DEV, D_MODEL).T
    grads["w_ada"], deltas["w_ada"], new_m["w_ada"], new_v["w_ada"] = _ada_grad_adamw(
        sc_t, dmod_shard, w_ada, m_w_ada, v_w_ada, 256)

    reduce_finish(state_in, owns_in, ("w_in",), "in", deltas["w_ada"])

    return (loss, grad_x[None], *[grads[n] for n in order], *[deltas[n] for n in order],
            *[new_m[n] for n in order], *[new_v[n] for n in order])
```

```python
import jax
import jax.numpy as jnp
from jax import lax
from jax.experimental import pallas as pl
from jax.experimental.pallas import tpu as pltpu

F32 = jnp.float32
BF16 = jnp.bfloat16
MESH = pl.DeviceIdType.MESH

D_MODEL = 1024
D_A = 512
D_B = 512
D_Z = 2 * D_A + D_B
N_HEADS = 4
HEAD_DIM = 128
CHUNK = 128
POOL_WINDOWS = (2, 4, 8, 16)
GROUP_DIM = 128
D_FF = 4096
N_MOD = 6
EPS = 1e-6
HALO = 16
N_CHIPS = 4
N_DEV = 8

ADAM_LR = 0.001
ADAM_B1 = 0.9
ADAM_B2 = 0.999
ADAM_EPS = 1e-08
ADAM_WD = 0.01
ADAM_STEP = 10

VMEM_LIMIT = 56 * 1024 * 1024
LANES = 128
SUBLANES = 8

_VMEM = pl.BlockSpec(memory_space=pltpu.VMEM)
_ANY = pl.BlockSpec(memory_space=pl.ANY)


def _params(n_grid_axes=1):
    return pltpu.CompilerParams(dimension_semantics=("arbitrary",) * n_grid_axes, vmem_limit_bytes=VMEM_LIMIT)


def _rows(ts, width):
    return pl.BlockSpec((ts, width), lambda i: (i, 0))


def _const(shape):
    return pl.BlockSpec(shape, lambda i: (0,) * len(shape))


def _dot(a, b):
    return jnp.dot(a, b, preferred_element_type=F32)


def _dot_nt(a, b):
    return lax.dot_general(a, b, (((1,), (1,)), ((), ())), preferred_element_type=F32)


def _dot_tn(a, b):
    return lax.dot_general(a, b, (((0,), (0,)), ((), ())), preferred_element_type=F32)


def _rowmean(v):
    return jnp.mean(v, axis=-1, keepdims=True)


def _colsum(v):
    return jnp.sum(v, axis=0, keepdims=True)


def _gelu_parts(z):
    k0 = 0.7978845608028654
    k1 = 0.044715
    z2 = z * z
    t = jnp.tanh(z * (k0 + (k0 * k1) * z2))
    u = 0.5 * t + 0.5
    g = z * u
    dg = u + (0.5 * z) * (1.0 - t * t) * (k0 + (3.0 * k0 * k1) * z2)
    return g, dg


def _tril_weights(ws_ref):
    r = lax.broadcasted_iota(jnp.int32, (CHUNK, CHUNK), 0)
    s = lax.broadcasted_iota(jnp.int32, (CHUNK, CHUNK), 1)
    mask = (s <= r).astype(F32)
    return [(ws_ref[h] * mask).astype(BF16) for h in range(N_HEADS)]


def _window_counts(first_row, n_rows):
    pos = (first_row + lax.broadcasted_iota(jnp.int32, (n_rows, 1), 0)).astype(F32)
    return pos, [1.0 / jnp.minimum(pos + 1.0, float(w)) for w in POOL_WINDOWS]


def _causal_window_sums(ext):
    out = []
    e = ext
    shift = 1
    for g in range(len(POOL_WINDOWS)):
        e = e + pltpu.roll(e, shift, 0)
        shift *= 2
        out.append(e[:, g * GROUP_DIM:(g + 1) * GROUP_DIM])
    return out


def _anticausal_window_sums(ext):
    n = ext.shape[0]
    out = []
    e = ext
    shift = 1
    for g in range(len(POOL_WINDOWS)):
        e = e + pltpu.roll(e, n - shift, 0)
        shift *= 2
        out.append(e[:, g * GROUP_DIM:(g + 1) * GROUP_DIM])
    return out


def _fwd_mix(x, mod6, n1pre, n1post, n2pre, win_g, wout_g, w_spatial, bsp_full, gain, bias, w_pool, b_pool, pool_scale, ts):
    s_len = x.shape[0]
    rs = D_MODEL // N_CHIPS

    def body(x_ref, mod_ref, g1pre_ref, g1post_ref, g2pre_ref, win_ref, wout_ref, ws_ref, bsp_ref, gain_ref,
             bias_ref, wp_ref, bp_ref, ps_ref, z_ref, y_ref, mix_ref, x1_ref, h2_ref, mixed_ref, prev_ref, wfull_ref):
        i = pl.program_id(0)
        _zero_on_first_step(prev_ref)
        _join_w_in_on_first_step(win_ref, wfull_ref)
        xv = x_ref[...]
        r = lax.rsqrt(_rowmean(xv * xv) + EPS)
        hb = ((xv * r) * (g1pre_ref[...] * (1.0 + mod_ref[1:2, :])) + mod_ref[0:1, :]).astype(BF16)
        z_ref[...] = _dot(hb, wfull_ref[...])

        wc = _tril_weights(ws_ref)
        u, _, _, _, _ = _mixer_forward_tile(z_ref[:, :2 * D_A], wc, bsp_ref, gain_ref[...], bias_ref[...], mixed_ref)
        y_ref[:, :D_A] = (u * mixed_ref[...]).astype(BF16)
        zb = z_ref[:, 2 * D_A:]
        sums = _causal_window_sums(jnp.concatenate([prev_ref[...], zb], axis=0))
        prev_ref[...] = zb[ts - HALO:, :]
        _, inv_counts = _window_counts(i * ts, ts)
        for g in range(len(POOL_WINDOWS)):
            lanes = slice(g * GROUP_DIM, (g + 1) * GROUP_DIM)
            diff = sums[g][HALO:, :] * inv_counts[g] - zb[:, lanes]
            lin = _dot(diff.astype(BF16), wp_ref[g].astype(BF16)) + bp_ref[:, lanes]
            y_ref[:, D_A + g * GROUP_DIM:D_A + (g + 1) * GROUP_DIM] = (lin * ps_ref[:, lanes]).astype(BF16)

        mix = None
        for j in range(N_CHIPS):
            part = _dot(y_ref[:, j * rs:(j + 1) * rs], wout_ref[j])
            mix = part if mix is None else mix + part
        mix_ref[...] = mix
        r2 = lax.rsqrt(_rowmean(mix * mix) + EPS)
        x1 = xv + (mix * r2) * (mod_ref[2:3, :] * g1post_ref[...])
        x1_ref[...] = x1
        r3 = lax.rsqrt(_rowmean(x1 * x1) + EPS)
        h2_ref[...] = ((x1 * r3) * (g2pre_ref[...] * (1.0 + mod_ref[4:5, :])) + mod_ref[3:4, :]).astype(BF16)

    vec = _const((1, D_MODEL))
    f32_rows = jax.ShapeDtypeStruct((s_len, D_MODEL), F32)
    bf16_rows = jax.ShapeDtypeStruct((s_len, D_MODEL), BF16)
    return pl.pallas_call(
        body, name="fwd_mix", grid=(s_len // ts,),
        in_specs=[_rows(ts, D_MODEL), _const((N_MOD, D_MODEL)), vec, vec, vec, _VMEM, _VMEM,
                  _const((N_HEADS, CHUNK, CHUNK)), _const((CHUNK, D_A)), _const((1, D_A)), _const((1, D_A)),
                  _const((N_HEADS, GROUP_DIM, GROUP_DIM)), _const((1, D_B)), _const((1, D_B))],
        out_specs=[_rows(ts, D_Z), _rows(ts, D_MODEL), _rows(ts, D_MODEL), _rows(ts, D_MODEL), _rows(ts, D_MODEL)],
        out_shape=[jax.ShapeDtypeStruct((s_len, D_Z), F32), bf16_rows, f32_rows, f32_rows, bf16_rows],
        scratch_shapes=[pltpu.VMEM((ts, D_A), F32), pltpu.VMEM((HALO, D_B), F32), pltpu.VMEM((D_MODEL, D_Z), BF16)],
        compiler_params=_params(),
    )(x, mod6, n1pre, n1post, n2pre, win_g, wout_g, w_spatial, bsp_full, gain, bias, w_pool, b_pool, pool_scale)


def _mixer_forward_tile(za, wc, bsp_ref, gain, bias, mixed_ref):
    ga, dga = _gelu_parts(za)
    u = ga[:, :D_A]
    v = ga[:, D_A:]
    mu = _rowmean(v)
    vc = v - mu
    rstd = lax.rsqrt(_rowmean(vc * vc) + EPS)
    vhat = vc * rstd
    vn = (vhat * gain + bias).astype(BF16)
    ts = za.shape[0]
    for k in range(ts // CHUNK):
        for h in range(N_HEADS):
            blk = vn[k * CHUNK:(k + 1) * CHUNK, h * HEAD_DIM:(h + 1) * HEAD_DIM]
            mixed_ref[k * CHUNK:(k + 1) * CHUNK, h * HEAD_DIM:(h + 1) * HEAD_DIM] = (
                _dot(wc[h], blk) + bsp_ref[:, h * HEAD_DIM:(h + 1) * HEAD_DIM])
    return u, vhat, rstd, vn, dga


def _fwd_fc1(h2, fc1_g, ts):
    s_len = h2.shape[0]
    cs = D_FF // N_CHIPS

    def body(h_ref, w_ref, q_ref):
        hb = h_ref[...]
        for j in range(N_CHIPS):
            p = jnp.maximum(_dot(hb, w_ref[j]), 0.0)
            q_ref[:, j * cs:(j + 1) * cs] = (p * p).astype(BF16)

    return pl.pallas_call(
        body, name="fwd_fc1", grid=(s_len // ts,),
        in_specs=[_rows(ts, D_MODEL), _VMEM],
        out_specs=_rows(ts, D_FF),
        out_shape=jax.ShapeDtypeStruct((s_len, D_FF), BF16),
        compiler_params=_params(),
    )(h2, fc1_g)


def _fwd_fc2_loss(q, x1, target, mod6, n2post, fc2_g, ts):
    s_len = q.shape[0]
    rs = D_FF // N_CHIPS

    def body(q_ref, x1_ref, t_ref, mod_ref, g_ref, w_ref, dy_ref, df_ref, loss_ref, s_ref):
        _zero_on_first_step(loss_ref, s_ref)
        gate_gain = mod_ref[5:6, :] * g_ref[...]
        f = _dot(q_ref[:, 0:rs], w_ref[0])
        for j in range(1, N_CHIPS):
            f = f + _dot(q_ref[:, j * rs:(j + 1) * rs], w_ref[j])
        r4 = lax.rsqrt(_rowmean(f * f) + EPS)
        fh = f * r4
        err = (x1_ref[...] + fh * gate_gain) - t_ref[...]
        loss_ref[...] += 0.5 * jnp.sum(_rowmean(err * err), axis=0, keepdims=True)
        dy = err * (1.0 / D_MODEL)
        dy_ref[...] = dy
        s_ref[...] += _colsum(dy * fh)
        gh = dy * gate_gain
        df_ref[...] = (r4 * (gh - fh * _rowmean(gh * fh))).astype(BF16)

    return pl.pallas_call(
        body, name="fwd_fc2_loss", grid=(s_len // ts,),
        in_specs=[_rows(ts, D_FF), _rows(ts, D_MODEL), _rows(ts, D_MODEL), _const((N_MOD, D_MODEL)),
                  _const((1, D_MODEL)), _VMEM],
        out_specs=[_rows(ts, D_MODEL), _rows(ts, D_MODEL), _const((1, 1)), _const((1, D_MODEL))],
        out_shape=[jax.ShapeDtypeStruct((s_len, D_MODEL), F32), jax.ShapeDtypeStruct((s_len, D_MODEL), BF16),
                   jax.ShapeDtypeStruct((1, 1), F32), jax.ShapeDtypeStruct((1, D_MODEL), F32)],
        compiler_params=_params(),
    )(q, x1, target, mod6, n2post, fc2_g)


def _join_w_in_on_first_step(win_ref, full_ref):
    cs = D_Z // N_CHIPS

    @pl.when(pl.program_id(0) == 0)
    def _():
        for j in range(N_CHIPS):
            full_ref[:, j * cs:(j + 1) * cs] = win_ref[j]


def _zero_on_first_step(*refs):
    @pl.when(pl.program_id(0) == 0)
    def _():
        for ref in refs:
            ref[...] = jnp.zeros_like(ref)


def _on_last_step(fn):
    pl.when(pl.program_id(0) == pl.num_programs(0) - 1)(fn)


def _store_shard_on_last_step(acc_ref, hbm_ref, sem, j):
    _on_last_step(lambda: pltpu.make_async_copy(acc_ref.at[j], hbm_ref.at[j], sem.at[j]).start())


def _wait_stores_on_last_step(*stores):
    def wait_all():
        for acc_ref, hbm_ref, sem in stores:
            for j in range(N_CHIPS):
                pltpu.make_async_copy(acc_ref.at[j], hbm_ref.at[j], sem.at[j]).wait()

    _on_last_step(wait_all)


def _bwd_fc2(df, q, fc2_g, ts):
    s_len = df.shape[0]
    cs = D_FF // N_CHIPS

    def body(df_ref, q_ref, w_ref, dp_ref, dw_hbm, dw_ref, dw_sem):
        _zero_on_first_step(dw_ref)
        dfb = df_ref[...]
        df2 = dfb * 2.0
        for j in range(N_CHIPS):
            qb = q_ref[:, j * cs:(j + 1) * cs]
            dw_ref[j] += _dot_tn(qb, dfb).reshape(2, cs // 2, D_MODEL)
            _store_shard_on_last_step(dw_ref, dw_hbm, dw_sem, j)
            dq2 = _dot_nt(df2, w_ref[j])
            dp_ref[:, j * cs:(j + 1) * cs] = (dq2 * jnp.sqrt(qb.astype(F32))).astype(BF16)
        _wait_stores_on_last_step((dw_ref, dw_hbm, dw_sem))

    dw_shape = (N_CHIPS, 2, cs // 2, D_MODEL)
    return pl.pallas_call(
        body, name="bwd_fc2", grid=(s_len // ts,),
        in_specs=[_rows(ts, D_MODEL), _rows(ts, D_FF), _VMEM],
        out_specs=[_rows(ts, D_FF), _ANY],
        out_shape=[jax.ShapeDtypeStruct((s_len, D_FF), BF16), jax.ShapeDtypeStruct(dw_shape, F32)],
        scratch_shapes=[pltpu.VMEM(dw_shape, F32), pltpu.SemaphoreType.DMA((N_CHIPS,))],
        compiler_params=_params(),
    )(df, q, fc2_g)


def _bwd_fc1_out(dp, dy, x1, mix, h2, ycat, mod6, n2pre, n1post, fc1_g, wout_g, dep, ts):
    s_len = dp.shape[0]
    cs = D_FF // N_CHIPS
    rs = D_MODEL // N_CHIPS

    def body(dp_ref, dy_ref, x1_ref, mix_ref, h2_ref, yc_ref, mod_ref, g2_ref, g1_ref, w1_ref, wo_ref, dep_ref,
             dx1_ref, dyc_ref, dshift2_ref, da2_ref, s1_ref, dw1_hbm, dwo_hbm, dw1_ref, dwo_ref, dw1_sem, dwo_sem):
        _zero_on_first_step(dshift2_ref, da2_ref, s1_ref, dw1_ref, dwo_ref)
        h2b = h2_ref[...]
        dh2 = None
        for j in range(N_CHIPS):
            dpb = dp_ref[:, j * cs:(j + 1) * cs]
            dw1_ref[j] += _dot_tn(h2b, dpb).reshape(2, D_MODEL // 2, cs)
            _store_shard_on_last_step(dw1_ref, dw1_hbm, dw1_sem, j)
            part = _dot_nt(dpb, w1_ref[j])
            dh2 = part if dh2 is None else dh2 + part
        x1 = x1_ref[...]
        r3 = lax.rsqrt(_rowmean(x1 * x1) + EPS)
        xh = x1 * r3
        a2 = g2_ref[...] * (1.0 + mod_ref[4:5, :])
        dshift2_ref[...] += _colsum(dh2)
        da2_ref[...] += _colsum(dh2 * xh)
        dxh = dh2 * a2
        dx1 = dy_ref[...] + r3 * (dxh - xh * _rowmean(dxh * xh))
        dx1_ref[...] = dx1

        mix = mix_ref[...]
        r2 = lax.rsqrt(_rowmean(mix * mix) + EPS)
        mh = mix * r2
        s1_ref[...] += _colsum(dx1 * mh)
        gh = dx1 * (mod_ref[2:3, :] * g1_ref[...])
        dmix = (r2 * (gh - mh * _rowmean(gh * mh))).astype(BF16)
        dwo_ref[...] += _dot_tn(yc_ref[...], dmix).reshape(N_CHIPS, 2, rs // 2, D_MODEL)
        for j in range(N_CHIPS):
            _store_shard_on_last_step(dwo_ref, dwo_hbm, dwo_sem, j)
            dyc_ref[:, j * rs:(j + 1) * rs] = _dot_nt(dmix, wo_ref[j])
        _wait_stores_on_last_step((dw1_ref, dw1_hbm, dw1_sem), (dwo_ref, dwo_hbm, dwo_sem))

    vec = jax.ShapeDtypeStruct((1, D_MODEL), F32)
    dw1_shape = (N_CHIPS, 2, D_MODEL // 2, cs)
    dwo_shape = (N_CHIPS, 2, rs // 2, D_MODEL)
    return pl.pallas_call(
        body, name="bwd_fc1_out", grid=(s_len // ts,),
        in_specs=[_rows(ts, D_FF), _rows(ts, D_MODEL), _rows(ts, D_MODEL), _rows(ts, D_MODEL), _rows(ts, D_MODEL),
                  _rows(ts, D_MODEL), _const((N_MOD, D_MODEL)), _const((1, D_MODEL)), _const((1, D_MODEL)), _VMEM,
                  _VMEM, _ANY],
        out_specs=[_rows(ts, D_MODEL), _rows(ts, D_MODEL)] + [_const((1, D_MODEL))] * 3 + [_ANY, _ANY],
        out_shape=[jax.ShapeDtypeStruct((s_len, D_MODEL), F32), jax.ShapeDtypeStruct((s_len, D_MODEL), F32),
                   vec, vec, vec, jax.ShapeDtypeStruct(dw1_shape, F32), jax.ShapeDtypeStruct(dwo_shape, F32)],
        scratch_shapes=[pltpu.VMEM(dw1_shape, F32), pltpu.VMEM(dwo_shape, F32), pltpu.SemaphoreType.DMA((N_CHIPS,)),
                        pltpu.SemaphoreType.DMA((N_CHIPS,))],
        compiler_params=_params(),
    )(dp, dy, x1, mix, h2, ycat, mod6, n2pre, n1post, fc1_g, wout_g, dep)


def _mixer_bwd(z, dyc, w_spatial, bsp_full, gain, bias, w_pool, b_pool, pool_scale, dep, ts):
    s_len = z.shape[0]
    nb = ts // HALO
    last = s_len // HALO - 1
    te = ts + HALO

    def body(z_ref, zprev_ref, znext_ref, dyc_ref, dynext_ref, ws_ref, bsp_ref, gain_ref, bias_ref, wp_ref, bp_ref,
             ps_ref, dep_ref, dz_ref, dws_ref, dbsp_ref, dgain_ref, dbias_ref, dwp_ref, dbp_ref, dps_ref, mixed_ref,
             dvn_ref):
        i = pl.program_id(0)

        @pl.when(i == 0)
        def _():
            for ref in (dws_ref, dbsp_ref, dgain_ref, dbias_ref, dwp_ref, dbp_ref, dps_ref):
                ref[...] = jnp.zeros_like(ref)

        wc = _tril_weights(ws_ref)
        gain = gain_ref[...]
        u, vhat, rstd, vn, dga = _mixer_forward_tile(z_ref[:, :2 * D_A], wc, bsp_ref, gain, bias_ref[...], mixed_ref)
        dya = dyc_ref[:, :D_A]
        du = dya * mixed_ref[...]
        dmixed = dya * u
        dmb = dmixed.astype(BF16)
        dm_sum = dmixed[0:CHUNK, :]
        for k in range(1, ts // CHUNK):
            dm_sum = dm_sum + dmixed[k * CHUNK:(k + 1) * CHUNK, :]
        r_idx = lax.broadcasted_iota(jnp.int32, (CHUNK, CHUNK), 0)
        s_idx = lax.broadcasted_iota(jnp.int32, (CHUNK, CHUNK), 1)
        causal = (s_idx <= r_idx).astype(F32)
        for h in range(N_HEADS):
            lanes = slice(h * HEAD_DIM, (h + 1) * HEAD_DIM)
            dbsp_ref[h] += jnp.sum(dm_sum[:, lanes], axis=1, keepdims=True)
            acc = None
            for k in range(ts // CHUNK):
                rows = slice(k * CHUNK, (k + 1) * CHUNK)
                t = _dot_nt(dmb[rows, lanes], vn[rows, lanes])
                acc = t if acc is None else acc + t
                dvn_ref[rows, lanes] = _dot_tn(wc[h], dmb[rows, lanes])
            dws_ref[h] += acc * causal
        dvn = dvn_ref[...]
        dgain_ref[...] += _colsum(dvn * vhat)
        dbias_ref[...] += _colsum(dvn)
        dvh = dvn * gain
        dv = rstd * (dvh - _rowmean(dvh) - vhat * _rowmean(dvh * vhat))
        dz_ref[:, :D_A] = (du * dga[:, :D_A]).astype(BF16)
        dz_ref[:, D_A:2 * D_A] = (dv * dga[:, D_A:]).astype(BF16)

        zb = z_ref[:, 2 * D_A:]
        prev = jnp.where(i == 0, 0.0, zprev_ref[...])
        zb_ext = jnp.concatenate([zb, znext_ref[...]], axis=0)
        sums = _causal_window_sums(jnp.concatenate([prev, zb_ext], axis=0))
        pos, inv_counts = _window_counts(i * ts, te)
        dyb_ext = jnp.concatenate([dyc_ref[:, D_A:], dynext_ref[...]], axis=0)
        dlin_ext = dyb_ext * ps_ref[...]
        dbp_ref[...] += _colsum(dlin_ext[:ts, :])
        scaled = []
        ddiffs = []
        lins = []
        for g in range(len(POOL_WINDOWS)):
            lanes = slice(g * GROUP_DIM, (g + 1) * GROUP_DIM)
            diff = (sums[g][HALO:, :] * inv_counts[g] - zb_ext[:, lanes]).astype(BF16)
            wpb = wp_ref[g].astype(BF16)
            dlb = dlin_ext[:, lanes].astype(BF16)
            lins.append(_dot(diff[:ts, :], wpb) + bp_ref[:, lanes])
            dwp_ref[g] += _dot_tn(diff[:ts, :], dlb[:ts, :])
            dd = _dot_nt(dlb, wpb)
            ddiffs.append(dd)
            scaled.append(jnp.where(pos < float(s_len), dd * inv_counts[g], 0.0))
        dps_ref[...] += _colsum(dyb_ext[:ts, :] * jnp.concatenate(lins, axis=1))
        back = _anticausal_window_sums(jnp.concatenate(scaled, axis=1))
        for g in range(len(POOL_WINDOWS)):
            dz_ref[:, 2 * D_A + g * GROUP_DIM:2 * D_A + (g + 1) * GROUP_DIM] = (
                back[g][:ts, :] - ddiffs[g][:ts, :]).astype(BF16)

    sq = jax.ShapeDtypeStruct((N_HEADS, CHUNK, CHUNK), F32)
    vec = jax.ShapeDtypeStruct((1, D_A), F32)
    return pl.pallas_call(
        body, name="mixer_bwd", grid=(s_len // ts,),
        in_specs=[_rows(ts, D_Z),
                  pl.BlockSpec((HALO, D_B), lambda i: (jnp.maximum(i * nb - 1, 0), 2)),
                  pl.BlockSpec((HALO, D_B), lambda i: (jnp.minimum((i + 1) * nb, last), 2)),
                  _rows(ts, D_MODEL),
                  pl.BlockSpec((HALO, D_B), lambda i: (jnp.minimum((i + 1) * nb, last), 1)),
                  _const((N_HEADS, CHUNK, CHUNK)), _const((CHUNK, D_A)), _const((1, D_A)), _const((1, D_A)),
                  _const((N_HEADS, GROUP_DIM, GROUP_DIM)), _const((1, D_B)), _const((1, D_B)), _ANY],
        out_specs=[_rows(ts, D_Z), _const((N_HEADS, CHUNK, CHUNK)), _const((N_HEADS, CHUNK, 1)), _const((1, D_A)),
                   _const((1, D_A)), _const((N_HEADS, GROUP_DIM, GROUP_DIM)), _const((1, D_B)), _const((1, D_B))],
        out_shape=[jax.ShapeDtypeStruct((s_len, D_Z), BF16), sq, jax.ShapeDtypeStruct((N_HEADS, CHUNK, 1), F32), vec,
                   vec, sq, vec, vec],
        scratch_shapes=[pltpu.VMEM((ts, D_A), F32), pltpu.VMEM((ts, D_A), F32)],
        compiler_params=_params(),
    )(z, z, z, dyc, dyc, w_spatial, bsp_full, gain, bias, w_pool, b_pool, pool_scale, dep)


def _bwd_in(dz, dx1, x, mod6, n1pre, win_g, dep, ts):
    s_len = x.shape[0]
    cs = D_Z // N_CHIPS

    def body(dz_ref, dx1_ref, x_ref, mod_ref, g_ref, w_ref, dep_ref, gx_ref, dshift_ref, da_ref, dw_hbm, dw_ref,
             wfull_ref, dw_sem):
        _zero_on_first_step(dshift_ref, da_ref, dw_ref)
        _join_w_in_on_first_step(w_ref, wfull_ref)
        xv = x_ref[...]
        r = lax.rsqrt(_rowmean(xv * xv) + EPS)
        xh = xv * r
        h1b = (xh * (g_ref[...] * (1.0 + mod_ref[1:2, :])) + mod_ref[0:1, :]).astype(BF16)
        dzb = dz_ref[...]
        dw = _dot_tn(h1b, dzb)
        for j in range(N_CHIPS):
            dw_ref[j] += dw[:, j * cs:(j + 1) * cs].reshape(2, D_MODEL // 2, cs)
        dh = _dot_nt(dzb, wfull_ref[...])
        a1 = g_ref[...] * (1.0 + mod_ref[1:2, :])
        dshift_ref[...] += _colsum(dh)
        da_ref[...] += _colsum(dh * xh)
        dxh = dh * a1
        gx_ref[...] = dx1_ref[...] + r * (dxh - xh * _rowmean(dxh * xh))
        for j in range(N_CHIPS):
            _store_shard_on_last_step(dw_ref, dw_hbm, dw_sem, j)
        _wait_stores_on_last_step((dw_ref, dw_hbm, dw_sem))

    vec = jax.ShapeDtypeStruct((1, D_MODEL), F32)
    dw_shape = (N_CHIPS, 2, D_MODEL // 2, cs)
    return pl.pallas_call(
        body, name="bwd_in", grid=(s_len // ts,),
        in_specs=[_rows(ts, D_Z), _rows(ts, D_MODEL), _rows(ts, D_MODEL), _const((N_MOD, D_MODEL)),
                  _const((1, D_MODEL)), _VMEM, _ANY],
        out_specs=[_rows(ts, D_MODEL), _const((1, D_MODEL)), _const((1, D_MODEL)), _ANY],
        out_shape=[jax.ShapeDtypeStruct((s_len, D_MODEL), F32), vec, vec, jax.ShapeDtypeStruct(dw_shape, F32)],
        scratch_shapes=[pltpu.VMEM(dw_shape, F32), pltpu.VMEM((D_MODEL, D_Z), BF16),
                        pltpu.SemaphoreType.DMA((N_CHIPS,))],
        compiler_params=_params(),
    )(dz, dx1, x, mod6, n1pre, win_g, dep)


def _adamw_math(w, g, m, v):
    m = ADAM_B1 * m + (1.0 - ADAM_B1) * g
    v = ADAM_B2 * v + (1.0 - ADAM_B2) * (g * g)
    m_hat = m / (1.0 - ADAM_B1 ** ADAM_STEP)
    v_hat = v / (1.0 - ADAM_B2 ** ADAM_STEP)
    delta = -ADAM_LR * (m_hat / (jnp.sqrt(v_hat) + ADAM_EPS) + ADAM_WD * w)
    return delta, m, v


def _adamw(gs, ws, ms, vs, name, steps):
    n = len(ws)

    def body(*refs):
        for i in range(n):
            g_ref, w_ref, m_ref, v_ref = refs[4 * i:4 * i + 4]
            d, nm, nv = _adamw_math(w_ref[...], g_ref[...], m_ref[...], v_ref[...])
            for ref, val in zip(refs[4 * n + 3 * i:4 * n + 3 * i + 3], (d, nm, nv)):
                ref[...] = val

    specs = [_rows(w.shape[0] // steps, w.shape[1]) for w in ws]
    out = pl.pallas_call(
        body, name=name, grid=(steps,),
        in_specs=[s for s in specs for _ in range(4)], out_specs=[s for s in specs for _ in range(3)],
        out_shape=[jax.ShapeDtypeStruct(w.shape, F32) for w in ws for _ in range(3)],
        compiler_params=_params(),
    )(*[a for quad in zip(gs, ws, ms, vs) for a in quad])
    return [out[3 * i:3 * i + 3] for i in range(n)]


def _ada_grad_adamw(sc_t, dmod_all, w, m, v, tr):
    rows, cols = w.shape

    def body(s_ref, dm_ref, w_ref, m_ref, v_ref, g_ref, d_ref, nm_ref, nv_ref):
        x, y, _ = _position()
        mine = pl.ds(pl.multiple_of((2 * x + y) * cols, LANES), cols)
        g = s_ref[:, 0:1] * dm_ref[0:1, mine]
        for b in range(1, N_DEV):
            g = g + s_ref[:, b:b + 1] * dm_ref[b:b + 1, mine]
        g_ref[...] = g
        d, nm, nv = _adamw_math(w_ref[...], g, m_ref[...], v_ref[...])
        d_ref[...] = d
        nm_ref[...] = nm
        nv_ref[...] = nv

    spec = _rows(tr, cols)
    shape = jax.ShapeDtypeStruct((rows, cols), F32)
    return pl.pallas_call(
        body, name="ada_grad_adamw", grid=(rows // tr,),
        in_specs=[_rows(tr, N_DEV), _const(dmod_all.shape), spec, spec, spec],
        out_specs=[spec] * 4, out_shape=[shape] * 4, compiler_params=_params(),
    )(sc_t, dmod_all, w, m, v)


def _mod_grads(da1, dshift1, s1, da2, dshift2, s2, mod6, n1pre, n1post, n2pre, n2post):
    def body(da1_ref, ds1_ref, s1_ref, da2_ref, ds2_ref, s2_ref, mod_ref, n1_ref, p1_ref, n2_ref, p2_ref, dmod_ref,
             dn_ref):
        dmod_ref[0:1, :] = ds1_ref[...]
        dmod_ref[1:2, :] = da1_ref[...] * n1_ref[...]
        dmod_ref[2:3, :] = s1_ref[...] * p1_ref[...]
        dmod_ref[3:4, :] = ds2_ref[...]
        dmod_ref[4:5, :] = da2_ref[...] * n2_ref[...]
        dmod_ref[5:6, :] = s2_ref[...] * p2_ref[...]
        dn_ref[0:1, :] = da1_ref[...] * (1.0 + mod_ref[1:2, :])
        dn_ref[1:2, :] = s1_ref[...] * mod_ref[2:3, :]
        dn_ref[2:3, :] = da2_ref[...] * (1.0 + mod_ref[4:5, :])
        dn_ref[3:4, :] = s2_ref[...] * mod_ref[5:6, :]

    return pl.pallas_call(
        body, name="mod_grads",
        out_shape=[jax.ShapeDtypeStruct((N_MOD, D_MODEL), F32), jax.ShapeDtypeStruct((4, D_MODEL), F32)],
    )(da1, dshift1, s1, da2, dshift2, s2, mod6, n1pre, n1post, n2pre, n2post)


def _position():
    x, y, c = lax.axis_index("x"), lax.axis_index("y"), lax.axis_index("c")
    return x, y, c


def _flip(v, bit):
    return 1 - v if bit else v


def _peer(x, y, c, k):
    return (_flip(x, k & 4), _flip(y, k & 2), _flip(c, k & 1))


def _remote(src, dst, send_sem, recv_sem, device):
    return pltpu.make_async_remote_copy(src_ref=src, dst_ref=dst, send_sem=send_sem, recv_sem=recv_sem,
                                        device_id=device, device_id_type=MESH)


def _mod_exchange(c_row, w_ada_shard, b_ada_row, ws, early):
    cs = w_ada_shard.shape[1]
    n = len(ws)

    def body(c_ref, w_hbm, b_ref, *refs):
        shards, (mod_ref, sc_ref), zones = refs[:n], refs[n:n + 2], refs[n + 2:2 * n + 2]
        rows_ref, w_ref, w_sem, send1, recv1, send2, recv2 = refs[2 * n + 2:2 * n + 9]
        wide, narrow = refs[2 * n + 9:3 * n + 9], refs[3 * n + 9:4 * n + 9]
        cast_load, cast_store = refs[4 * n + 9:]
        x, y, c = _position()
        me = 4 * x + 2 * y + c
        chip = 2 * x + y
        w_load = pltpu.make_async_copy(w_hbm, w_ref, w_sem)
        w_load.start()
        loads = [pltpu.make_async_copy(shards[i], wide[i], cast_load.at[i]) for i in range(n)]
        stores = [pltpu.make_async_copy(narrow[i], zones[i].at[chip], cast_store.at[i]) for i in range(n)]

        def cast(i):
            loads[i].wait()
            narrow[i][...] = wide[i][...].astype(BF16)
            stores[i].start()

        for i in range(early):
            loads[i].start()
        cv = c_ref[...]
        sc_ref[me] = cv * jax.nn.sigmoid(cv)
        gather = [_remote(sc_ref.at[me], sc_ref.at[me], send1.at[k - 1], recv1.at[k - 1], _peer(x, y, c, k))
                  for k in range(1, N_DEV)]
        for cp in gather:
            cp.start()
        for i in range(early):
            cast(i)
        for i in range(early, n):
            loads[i].start()
        for k in range(1, N_DEV):
            px, py, pc = _peer(x, y, c, k)
            src = 4 * px + 2 * py + pc
            _remote(sc_ref.at[src], sc_ref.at[src], send1.at[k - 1], recv1.at[k - 1], (px, py, pc)).wait_recv()
        for cp in gather:
            cp.wait_send()
        sc_all = jnp.concatenate([sc_ref[b] for b in range(N_DEV)], axis=0)
        w_load.wait()
        part = _dot(sc_all.astype(BF16), w_ref[...].astype(BF16))
        part = part + b_ref[:, pl.ds(pl.multiple_of(chip * cs, LANES), cs)]
        for b in range(N_DEV):
            rows_ref[b] = part[b:b + 1, :]
        mod_ref[chip] = rows_ref[me]
        hand = []
        for k in (2, 4, 6):
            px, py, _ = _peer(x, y, c, k)
            hand.append(_remote(rows_ref.at[4 * px + 2 * py + c], mod_ref.at[chip], send2.at[k // 2 - 1],
                                recv2.at[k // 2 - 1], (px, py, c)))
        for cp in hand:
            cp.start()
        for i in range(early, n):
            cast(i)
        for k in (2, 4, 6):
            px, py, _ = _peer(x, y, c, k)
            pchip = 2 * px + py
            _remote(rows_ref.at[me], mod_ref.at[pchip], send2.at[k // 2 - 1], recv2.at[k // 2 - 1],
                    (px, py, c)).wait_recv()
        for cp in hand:
            cp.wait_send()
        for cp in stores:
            cp.wait()

    out = pl.pallas_call(
        body, name="mod_exchange",
        in_specs=[_VMEM, _ANY, _VMEM] + [_ANY] * n, out_specs=[_VMEM, _VMEM] + [_ANY] * n,
        out_shape=[jax.ShapeDtypeStruct((N_CHIPS, 1, cs), F32), jax.ShapeDtypeStruct((N_DEV, 1, D_MODEL), F32)]
        + [jax.ShapeDtypeStruct((N_CHIPS,) + w.shape, BF16) for w in ws],
        scratch_shapes=[pltpu.VMEM((N_DEV, 1, cs), F32), pltpu.VMEM(w_ada_shard.shape, F32), pltpu.SemaphoreType.DMA,
                        pltpu.SemaphoreType.DMA((N_DEV - 1,)),
                        pltpu.SemaphoreType.DMA((N_DEV - 1,)), pltpu.SemaphoreType.DMA((N_CHIPS - 1,)),
                        pltpu.SemaphoreType.DMA((N_CHIPS - 1,))]
        + [pltpu.VMEM(w.shape, F32) for w in ws] + [pltpu.VMEM(w.shape, BF16) for w in ws]
        + [pltpu.SemaphoreType.DMA((n,))] * 2,
        compiler_params=pltpu.CompilerParams(vmem_limit_bytes=VMEM_LIMIT),
    )(c_row, w_ada_shard, b_ada_row, *ws)
    return out[0], out[1], list(out[2:])


_HBM = pl.BlockSpec(memory_space=pltpu.HBM)
_SEM = pl.BlockSpec(memory_space=pltpu.SEMAPHORE)
_EFFECT = pltpu.SideEffectType.DATAFLOW_SIDE_EFFECTING
_CHIP_HOPS = (2, 4, 6)


def _in_hbm(a):
    return pltpu.with_memory_space_constraint(a, pltpu.HBM)


def _sems3():
    return pltpu.SemaphoreType.DMA((len(_CHIP_HOPS),))


def _ag_start(lands, after, name):
    n = len(lands)

    def body(*refs):
        zones = refs[:n]
        sends, recvs = refs[n + 1:2 * n + 1], refs[2 * n + 1:3 * n + 1]
        x, y, c = _position()
        chip = 2 * x + y
        for i in range(n):
            half = zones[i].shape[1] // 2
            mine = zones[i].at[chip, pl.ds(c * half, half)]
            for s, k in enumerate(_CHIP_HOPS):
                px, py, _ = _peer(x, y, c, k)
                _remote(mine, mine, sends[i].at[s], recvs[i].at[s], (px, py, c)).start()

    out = pl.pallas_call(
        body, name=name,
        in_specs=[_HBM] * n + [_ANY],
        out_specs=[_SEM] * (2 * n) + [_HBM] * n,
        out_shape=[_sems3()] * (2 * n) + [pltpu.HBM(z.shape, BF16) for z in lands],
        input_output_aliases={i: 2 * n + i for i in range(n)},
        compiler_params=pltpu.CompilerParams(has_side_effects=_EFFECT),
    )(*[_in_hbm(z) for z in lands], after)
    return [(out[2 * n + i], out[i], out[n + i]) for i in range(n)]


def _ag_pass(group, after, name):
    n = len(group)

    def body(*refs):
        zones = refs[:n]
        sends, recvs = refs[n:2 * n], refs[2 * n:3 * n]
        fsends, frecvs = refs[4 * n + 1:5 * n + 1], refs[5 * n + 1:6 * n + 1]
        x, y, c = _position()
        chip = 2 * x + y
        for i in range(n):
            half = zones[i].shape[1] // 2
            rows = pl.ds(c * half, half)
            for s, k in enumerate(_CHIP_HOPS):
                px, py, _ = _peer(x, y, c, k)
                landed = zones[i].at[2 * px + py, rows]
                _remote(landed, landed, sends[i].at[s], recvs[i].at[s], (px, py, c)).wait_recv()
                _remote(landed, landed, fsends[i].at[s], frecvs[i].at[s], (x, y, 1 - c)).start()
        for i in range(n):
            half = zones[i].shape[1] // 2
            mine = zones[i].at[chip, pl.ds(c * half, half)]
            for s, k in enumerate(_CHIP_HOPS):
                px, py, _ = _peer(x, y, c, k)
                _remote(mine, mine, sends[i].at[s], recvs[i].at[s], (px, py, c)).wait_send()

    out = pl.pallas_call(
        body, name=name,
        in_specs=[_HBM] * n + [_SEM] * (2 * n) + [_ANY],
        out_specs=[_HBM] * n + [_SEM] * (2 * n),
        out_shape=[pltpu.HBM(g[0].shape, BF16) for g in group] + [_sems3()] * (2 * n),
        input_output_aliases={i: i for i in range(n)},
        compiler_params=pltpu.CompilerParams(has_side_effects=_EFFECT),
    )(*[g[0] for g in group], *[g[1] for g in group], *[g[2] for g in group], after)
    return [(out[i], out[n + i], out[2 * n + i]) for i in range(n)]


def _ag_done(group, name):
    n = len(group)

    def body(*refs):
        lands = refs[:n]
        fsends, frecvs = refs[n:2 * n], refs[2 * n:3 * n]
        x, y, c = _position()
        for i in range(n):
            half = lands[i].shape[1] // 2
            for s, k in enumerate(_CHIP_HOPS):
                px, py, _ = _peer(x, y, c, k)
                sent = lands[i].at[2 * px + py, pl.ds(c * half, half)]
                got = lands[i].at[2 * px + py, pl.ds((1 - c) * half, half)]
                cp = _remote(sent, got, fsends[i].at[s], frecvs[i].at[s], (x, y, 1 - c))
                cp.wait_recv()
                cp.wait_send()

    out = pl.pallas_call(
        body, name=name,
        in_specs=[_HBM] * n + [_SEM] * (2 * n),
        out_specs=[_HBM] * n,
        out_shape=[pltpu.HBM(g[0].shape, BF16) for g in group],
        input_output_aliases={i: i for i in range(n)},
        compiler_params=pltpu.CompilerParams(has_side_effects=_EFFECT),
    )(*[g[0] for g in group], *[g[1] for g in group], *[g[2] for g in group])
    return list(out)


def _small_spread_wait(slots, sends, recvs, afters):
    def body(z_ref, sends, recvs, *rest):
        x, y, c = _position()
        mine = z_ref.at[2 * x + y]
        for s, k in enumerate(_CHIP_HOPS):
            px, py, _ = _peer(x, y, c, k)
            cp = _remote(mine, z_ref.at[2 * px + py], sends.at[s], recvs.at[s], (px, py, c))
            cp.wait_recv()
            cp.wait_send()

    return pl.pallas_call(
        body, name="small_spread_wait",
        in_specs=[_HBM, _SEM, _SEM] + [_ANY] * len(afters), out_specs=_HBM, out_shape=pltpu.HBM(slots.shape, F32),
        input_output_aliases={0: 0},
        compiler_params=pltpu.CompilerParams(has_side_effects=_EFFECT),
    )(slots, sends, recvs, *afters)


def _sibling_sum(pgs, name, small=None):
    n = len(pgs)
    k = 0 if small is None else 1
    units = [(i, j) for i in range(n) for j in range(N_CHIPS)]

    def body(*refs):
        refs = list(refs)
        take = lambda count: [refs.pop(0) for _ in range(count)]
        ins, small_in = take(n), take(k)
        qbs, owns, slots_out = take(n), take(n), take(k)
        mine, other, stage, got = take(n), take(n), take(n), take(n)
        load_a, load_b, send, recv, keep_wire, keep_own = take(6)
        x, y, c = _position()
        chip = 2 * x + y
        if k:
            sib_ref, pair_send, pair_recv, keep_small = take(4)
            pair = _remote(small_in[0], sib_ref, pair_send, pair_recv, (x, y, 1 - c))
            pair.start()
        loads_a = [pltpu.make_async_copy(ins[i].at[j, 1 - c], other[i].at[j], load_a.at[u])
                   for u, (i, j) in enumerate(units)]
        loads_b = [pltpu.make_async_copy(ins[i].at[j, c], mine[i].at[j], load_b.at[u])
                   for u, (i, j) in enumerate(units)]
        for cp in loads_a + loads_b:
            cp.start()
        sent = []
        for u, (i, j) in enumerate(units):
            loads_a[u].wait()
            stage[i][j] = other[i][j].astype(BF16)
            cp = _remote(stage[i].at[j], got[i].at[j], send.at[u], recv.at[u], (x, y, 1 - c))
            cp.start()
            sent.append(cp)
        stores = []
        for u, (i, j) in enumerate(units):
            loads_b[u].wait()
            sent[u].wait_recv()
            q = mine[i][j] + got[i][j].astype(F32)
            mine[i][j] = q
            got[i][j] = q.astype(BF16)
            stores.append(pltpu.make_async_copy(got[i].at[j], qbs[i].at[j], keep_wire.at[u]))
            stores[-1].start()
        for i in range(n):
            stores.append(pltpu.make_async_copy(mine[i].at[chip], owns[i], keep_own.at[i]))
            stores[-1].start()
        if k:
            pair.wait()
            sib_ref[...] = small_in[0][...] + sib_ref[...]
            stores.append(pltpu.make_async_copy(sib_ref, slots_out[0].at[chip], keep_small))
            stores[-1].start()
        for cp in sent:
            cp.wait_send()
        for cp in stores:
            cp.wait()

    wire = [(N_CHIPS,) + p.shape[2:] for p in pgs]
    extra_out, extra_scratch = [], []
    if k:
        extra_out = [jax.ShapeDtypeStruct((N_CHIPS,) + small.shape, F32)]
        extra_scratch = [pltpu.VMEM(small.shape, F32)] + [pltpu.SemaphoreType.DMA] * 3
    out = pl.pallas_call(
        body, name=name, in_specs=[_ANY] * n + [_VMEM] * k, out_specs=[_ANY] * (2 * n + k),
        out_shape=[jax.ShapeDtypeStruct(w, BF16) for w in wire] + [jax.ShapeDtypeStruct(w[1:], F32) for w in wire]
        + extra_out,
        scratch_shapes=[pltpu.VMEM(w, F32) for w in wire] * 2 + [pltpu.VMEM(w, BF16) for w in wire] * 2
        + [pltpu.SemaphoreType.DMA((len(units),))] * 5 + [pltpu.SemaphoreType.DMA((n,))] + extra_scratch,
        compiler_params=pltpu.CompilerParams(vmem_limit_bytes=VMEM_LIMIT),
    )(*pgs, *([small] if k else []))
    return list(out[:n]), list(out[n:2 * n]), list(out[2 * n:])


def _rs_start(qbs, name, small=None):
    n = len(qbs)
    k = 0 if small is None else 1

    def body(*refs):
        outs, inboxes, slots = refs[:n], refs[n:2 * n], refs[2 * n:2 * n + k]
        sems = refs[2 * n + k:4 * n + 3 * k]
        sends, recvs = sems[:n], sems[n:2 * n]
        x, y, c = _position()
        chip = 2 * x + y
        if k:
            mine = slots[0].at[chip]
            for s, hop in enumerate(_CHIP_HOPS):
                px, py, _ = _peer(x, y, c, hop)
                _remote(mine, mine, sems[2 * n].at[s], sems[2 * n + 1].at[s], (px, py, c)).start()
        for i in range(n):
            for s, hop in enumerate(_CHIP_HOPS):
                px, py, _ = _peer(x, y, c, hop)
                _remote(outs[i].at[2 * px + py], inboxes[i].at[chip], sends[i].at[s], recvs[i].at[s], (px, py, c)).start()

    inboxes = [_in_hbm(lax.empty(q.shape, BF16)) for q in qbs]
    n_sems = 2 * n + 2 * k
    out = pl.pallas_call(
        body, name=name,
        in_specs=[_HBM] * (2 * n + k),
        out_specs=[_SEM] * n_sems + [_HBM] * (2 * n + k),
        out_shape=[_sems3()] * n_sems + [pltpu.HBM(q.shape, BF16) for q in qbs] * 2
        + ([pltpu.HBM(small.shape, F32)] if k else []),
        input_output_aliases={i: n_sems + i for i in range(2 * n + k)},
        compiler_params=pltpu.CompilerParams(has_side_effects=_EFFECT),
    )(*[_in_hbm(q) for q in qbs], *inboxes, *([_in_hbm(small)] if k else []))
    states = [(out[n_sems + i], out[n_sems + n + i], out[i], out[n + i]) for i in range(n)]
    return (states, (out[n_sems + 2 * n], out[2 * n], out[2 * n + 1])) if k else states


def _rs_wait(group, after, name):
    n = len(group)

    def body(*refs):
        outs, inboxes = refs[:n], refs[n:2 * n]
        sends, recvs = refs[2 * n:3 * n], refs[3 * n:4 * n]
        x, y, c = _position()
        for i in range(n):
            for s, k in enumerate(_CHIP_HOPS):
                px, py, _ = _peer(x, y, c, k)
                slot = 2 * px + py
                cp = _remote(outs[i].at[slot], inboxes[i].at[slot], sends[i].at[s], recvs[i].at[s], (px, py, c))
                cp.wait_recv()
                cp.wait_send()

    out = pl.pallas_call(
        body, name=name,
        in_specs=[_HBM] * (2 * n) + [_SEM] * (2 * n) + [_ANY],
        out_specs=[_HBM] * n,
        out_shape=[pltpu.HBM(g[1].shape, BF16) for g in group],
        input_output_aliases={n + i: i for i in range(n)},
        compiler_params=pltpu.CompilerParams(has_side_effects=_EFFECT),
    )(*[g[0] for g in group], *[g[1] for g in group], *[g[2] for g in group], *[g[3] for g in group], after)
    return list(out)


def _final_share(inboxes, owns, name):
    n = len(inboxes)
    units = [(i, s) for i in range(n) for s in range(len(_CHIP_HOPS))]

    def body(*refs):
        ins, mine, outs, landed, half = (refs[k * n:(k + 1) * n] for k in range(5))
        load, load_own, keep, send, recv = refs[5 * n:]
        x, y, c = _position()
        loads = []
        for u, (i, s) in enumerate(units):
            px, py, _ = _peer(x, y, c, _CHIP_HOPS[s])
            loads.append(pltpu.make_async_copy(ins[i].at[2 * px + py], landed[i].at[s], load.at[u]))
        loads_own = [pltpu.make_async_copy(mine[i], half[i], load_own.at[i]) for i in range(n)]
        for cp in loads + loads_own:
            cp.start()
        copies = []
        for i in range(n):
            for s in range(len(_CHIP_HOPS)):
                loads[len(_CHIP_HOPS) * i + s].wait()
            loads_own[i].wait()
            total = (landed[i][0].astype(F32) + landed[i][1].astype(F32)) + landed[i][2].astype(F32)
            half[i][...] = total + half[i][...]
            copies.append(pltpu.make_async_copy(half[i], outs[i].at[c], keep.at[i]))
            copies.append(_remote(half[i], outs[i].at[c], send.at[i], recv.at[i], (x, y, 1 - c)))
            for cp in copies[-2:]:
                cp.start()
        for i in range(n):
            theirs = outs[i].at[1 - c]
            _remote(theirs, theirs, send.at[i], recv.at[i], (x, y, 1 - c)).wait_recv()
        for i in range(n):
            copies[2 * i].wait()
            copies[2 * i + 1].wait_send()

    return pl.pallas_call(
        body, name=name, in_specs=[_ANY] * (2 * n), out_specs=[_ANY] * n,
        out_shape=[jax.ShapeDtypeStruct((2,) + o.shape, F32) for o in owns],
        scratch_shapes=[pltpu.VMEM((len(_CHIP_HOPS),) + o.shape, BF16) for o in owns]
        + [pltpu.VMEM(o.shape, F32) for o in owns]
        + [pltpu.SemaphoreType.DMA((len(units),))] + [pltpu.SemaphoreType.DMA((n,))] * 4,
        compiler_params=pltpu.CompilerParams(vmem_limit_bytes=VMEM_LIMIT),
    )(*inboxes, *owns)


_SMALL = (("b_ada", N_MOD * D_MODEL), ("norm1_pre", D_MODEL), ("norm1_post", D_MODEL), ("norm2_pre", D_MODEL),
          ("norm2_post", D_MODEL), ("w_spatial", N_HEADS * CHUNK * CHUNK), ("b_spatial", N_HEADS * CHUNK),
          ("ln_v_gain", D_A), ("ln_v_bias", D_A), ("w_pool", N_HEADS * GROUP_DIM * GROUP_DIM),
          ("b_pool", D_B), ("pool_scale", D_B))
_MOD_ROWS = N_MOD * D_MODEL // LANES


def _packed_rows(size):
    return -(-(size // LANES) // SUBLANES) * SUBLANES


def _pack(parts):
    out = []
    for name, size in _SMALL:
        a = parts[name].reshape(size // LANES, LANES)
        pad = _packed_rows(size) - a.shape[0]
        out.append(jnp.pad(a, ((0, pad), (0, 0))) if pad else a)
    return out


def _small_adamw(slots, ws, ms, vs):
    n = len(_SMALL)
    head = N_DEV * _MOD_ROWS

    def body(*refs):
        s_ref, w, m, v = refs[0], refs[1:1 + n], refs[1 + n:1 + 2 * n], refs[1 + 2 * n:1 + 3 * n]
        outs = refs[1 + 3 * n:1 + 7 * n]
        dmod_ref, loss_ref, t_ref = refs[1 + 7 * n:]
        t_ref[...] = ((s_ref[0] + s_ref[1]) + s_ref[2]) + s_ref[3]
        dmod_ref[...] = t_ref[0:head, :]
        loss_ref[...] = t_ref[t_ref.shape[0] - SUBLANES:t_ref.shape[0] - SUBLANES + 1, 0:1]
        row = head
        for i, (_, size) in enumerate(_SMALL):
            if i == 0:
                g = t_ref[0:_MOD_ROWS, :]
                for b in range(1, N_DEV):
                    g = g + t_ref[b * _MOD_ROWS:(b + 1) * _MOD_ROWS, :]
            else:
                g = t_ref[row:row + size // LANES, :]
                row += _packed_rows(size)
            d, nm, nv = _adamw_math(w[i][...], g, m[i][...], v[i][...])
            for ref, val in zip(outs[4 * i:4 * i + 4], (g, d, nm, nv)):
                ref[...] = val

    each = [jax.ShapeDtypeStruct((size // LANES, LANES), F32) for _, size in _SMALL for _ in range(4)]
    out = pl.pallas_call(
        body, name="small_adamw",
        out_shape=each + [jax.ShapeDtypeStruct((head, LANES), F32), jax.ShapeDtypeStruct((1, 1), F32)],
        scratch_shapes=[pltpu.VMEM(slots.shape[1:], F32)],
        compiler_params=pltpu.CompilerParams(vmem_limit_bytes=VMEM_LIMIT),
    )(slots, *ws, *ms, *vs)
    return [out[4 * i:4 * i + 4] for i in range(n)], out[4 * n], out[4 * n + 1]


def kernel(x, c, w_ada, b_ada, norm1_pre, norm1_post, w_in, w_spatial, b_spatial, ln_v_gain, ln_v_bias, w_pool, b_pool, pool_scale, w_out, norm2_pre, norm2_post, w_fc1, w_fc2, loss_target, m_w_ada, m_b_ada, m_norm1_pre, m_norm1_post, m_w_in, m_w_spatial, m_b_spatial, m_ln_v_gain, m_ln_v_bias, m_w_pool, m_b_pool, m_pool_scale, m_w_out, m_norm2_pre, m_norm2_post, m_w_fc1, m_w_fc2, v_w_ada, v_b_ada, v_norm1_pre, v_norm1_post, v_w_in, v_w_spatial, v_b_spatial, v_ln_v_gain, v_ln_v_bias, v_w_pool, v_b_pool, v_pool_scale, v_w_out, v_norm2_pre, v_norm2_post, v_w_fc1, v_w_fc2):
    weights = dict(w_ada=w_ada, b_ada=b_ada, norm1_pre=norm1_pre, norm1_post=norm1_post, w_in=w_in,
                   w_spatial=w_spatial, b_spatial=b_spatial, ln_v_gain=ln_v_gain, ln_v_bias=ln_v_bias, w_pool=w_pool,
                   b_pool=b_pool, pool_scale=pool_scale, w_out=w_out, norm2_pre=norm2_pre, norm2_post=norm2_post,
                   w_fc1=w_fc1, w_fc2=w_fc2)
    m_old = dict(w_ada=m_w_ada, b_ada=m_b_ada, norm1_pre=m_norm1_pre, norm1_post=m_norm1_post, w_in=m_w_in,
                 w_spatial=m_w_spatial, b_spatial=m_b_spatial, ln_v_gain=m_ln_v_gain, ln_v_bias=m_ln_v_bias,
                 w_pool=m_w_pool, b_pool=m_b_pool, pool_scale=m_pool_scale, w_out=m_w_out, norm2_pre=m_norm2_pre,
                 norm2_post=m_norm2_post, w_fc1=m_w_fc1, w_fc2=m_w_fc2)
    v_old = dict(w_ada=v_w_ada, b_ada=v_b_ada, norm1_pre=v_norm1_pre, norm1_post=v_norm1_post, w_in=v_w_in,
                 w_spatial=v_w_spatial, b_spatial=v_b_spatial, ln_v_gain=v_ln_v_gain, ln_v_bias=v_ln_v_bias,
                 w_pool=v_w_pool, b_pool=v_b_pool, pool_scale=v_pool_scale, w_out=v_w_out, norm2_pre=v_norm2_pre,
                 norm2_post=v_norm2_post, w_fc1=v_w_fc1, w_fc2=v_w_fc2)
    order = ("w_ada", "b_ada", "norm1_pre", "norm1_post", "w_in", "w_spatial", "b_spatial", "ln_v_gain", "ln_v_bias",
             "w_pool", "b_pool", "pool_scale", "w_out", "norm2_pre", "norm2_post", "w_fc1", "w_fc2")
    mx, my, mc = _position()
    me = 4 * mx + 2 * my + mc
    chip = 2 * mx + my
    row = lambda a: a.reshape(1, -1)

    xs, target = x[0], loss_target[0]
    n1pre, n1post, n2pre, n2post = row(norm1_pre), row(norm1_post), row(norm2_pre), row(norm2_post)
    mixer = (w_spatial, jnp.repeat(b_spatial.T, HEAD_DIM, axis=1), row(ln_v_gain), row(ln_v_bias), w_pool,
             row(b_pool), row(pool_scale))
    ts_big, ts_mid = 512, 256

    mod4, sc_all, lands = _mod_exchange(c, w_ada, row(b_ada), [w_in, w_out, w_fc1, w_fc2], 2)
    mod6 = mod4.reshape(N_MOD, D_MODEL)
    ag = _ag_start(lands, mod4, "ag_start")

    win_g, wout_g = _ag_done(_ag_pass([ag[0], ag[1]], ag[2][0], "ag_pass_mix"), "ag_done_mix")
    z, ycat, mix, x1, h2 = _fwd_mix(xs, mod6, n1pre, n1post, n2pre, win_g, wout_g, *mixer, ts_big)
    (fc1_g,) = _ag_done(_ag_pass([ag[2]], h2, "ag_pass_fc1"), "ag_done_fc1")
    q = _fwd_fc1(h2, fc1_g, ts_big)
    (fc2_g,) = _ag_done(_ag_pass([ag[3]], q, "ag_pass_fc2"), "ag_done_fc2")
    dy, df, loss, s2 = _fwd_fc2_loss(q, x1, target, mod6, n2post, fc2_g, ts_big)

    def reduce_start(partials, tag, small=None):
        wire, owns, slots = _sibling_sum(partials, "sibling_sum_" + tag, small)
        return _rs_start(wire, "rs_start_" + tag), owns, slots

    def reduce_finish(state, owns, names, tag, dep):
        inboxes = _rs_wait(state, dep, "rs_wait_" + tag)
        shards = _final_share(inboxes, owns, "final_share_" + tag)
        for n, g in zip(names, shards):
            grads[n] = g.reshape(weights[n].shape)
        updates = _adamw([grads[n] for n in names], [weights[n] for n in names], [m_old[n] for n in names],
                         [v_old[n] for n in names], "adamw_" + tag, 4)
        for n, (d, nm, nv) in zip(names, updates):
            deltas[n], new_m[n], new_v[n] = d, nm, nv

    grads, deltas, new_m, new_v = {}, {}, {}, {}
    dp, g_fc2 = _bwd_fc2(df, q, fc2_g, ts_big)
    state_fc2, owns_fc2, _ = reduce_start([g_fc2], "fc2")
    dx1, dyc, dshift2, da2, s1, g_fc1, g_out = _bwd_fc1_out(
        dp, dy, x1, mix, h2, ycat, mod6, n2pre, n1post, fc1_g, wout_g, state_fc2[0][0], ts_mid)
    state_mid, owns_mid, _ = reduce_start([g_fc1, g_out], "mid")
    dz, dws, dbsp, dgain, dbias, dwp, dbp, dps = _mixer_bwd(z, dyc, *mixer, state_mid[0][0], ts_big)
    grad_x, dshift1, da1, g_in = _bwd_in(dz, dx1, xs, mod6, n1pre, win_g, state_mid[0][0], ts_big)
    dmod6, dnorms = _mod_grads(da1, dshift1, s1, da2, dshift2, s2, mod6, n1pre, n1post, n2pre, n2post)

    parts = dict(b_ada=dmod6, norm1_pre=dnorms[0], norm1_post=dnorms[1], norm2_pre=dnorms[2], norm2_post=dnorms[3],
                 w_spatial=dws, b_spatial=dbsp, ln_v_gain=dgain, ln_v_bias=dbias, w_pool=dwp, b_pool=dbp,
                 pool_scale=dps)
    pieces = _pack(parts)
    slots = lax.dynamic_update_slice(jnp.zeros((N_DEV * _MOD_ROWS, LANES), F32), pieces[0], (me * _MOD_ROWS, 0))
    loss_tile = jnp.pad(loss, ((0, SUBLANES - 1), (0, LANES - 1)))
    wire_in, owns_in, pair_sum = _sibling_sum([g_in], "sibling_sum_in",
                                              jnp.concatenate([slots] + pieces[1:] + [loss_tile], axis=0))
    state_in, spread = _rs_start(wire_in, "rs_start_in", pair_sum[0])
    reduce_finish(state_fc2 + state_mid, owns_fc2 + owns_mid, ("w_fc2", "w_fc1", "w_out"), "mlp", state_in[0][0])
    flat = lambda d: [d[n].reshape(size // LANES, LANES) for n, size in _SMALL]
    small_out, dmod_all, loss = _small_adamw(
        _small_spread_wait(*spread, [deltas[n] for n in ("w_fc2", "w_fc1", "w_out")]), flat(weights), flat(m_old),
        flat(v_old))
    loss = loss.reshape(())
    for (n, _), (g, d, nm, nv) in zip(_SMALL, small_out):
        shape = weights[n].shape
        grads[n], deltas[n], new_m[n], new_v[n] = g.reshape(shape), d.reshape(shape), nm.reshape(shape), nv.reshape(shape)

    sc_t = sc_all.reshape(N_DEV, D_MODEL).T
    grads["w_ada"], deltas["w_ada"], new_m["w_ada"], new_v["w_ada"] = _ada_grad_adamw(
        sc_t, dmod_all.reshape(N_DEV, N_MOD * D_MODEL), w_ada, m_w_ada, v_w_ada, 256)

    reduce_finish(state_in, owns_in, ("w_in",), "in", deltas["w_ada"])

    return (loss, grad_x[None], *[grads[n] for n in order], *[deltas[n] for n in order],
            *[new_m[n] for n in order], *[new_v[n] for n in order])
```

```python
import jax
import jax.numpy as jnp
from jax import lax
from jax.experimental import pallas as pl
from jax.experimental.pallas import tpu as pltpu

F32 = jnp.float32
BF16 = jnp.bfloat16
MESH = pl.DeviceIdType.MESH

D_MODEL = 1024
D_A = 512
D_B = 512
D_Z = 2 * D_A + D_B
N_HEADS = 4
HEAD_DIM = 128
CHUNK = 128
POOL_WINDOWS = (2, 4, 8, 16)
GROUP_DIM = 128
D_FF = 4096
N_MOD = 6
EPS = 1e-6
HALO = 16
N_CHIPS = 4
N_DEV = 8

ADAM_LR = 0.001
ADAM_B1 = 0.9
ADAM_B2 = 0.999
ADAM_EPS = 1e-08
ADAM_WD = 0.01
ADAM_STEP = 10

VMEM_LIMIT = 56 * 1024 * 1024
LANES = 128
SUBLANES = 8

_VMEM = pl.BlockSpec(memory_space=pltpu.VMEM)
_ANY = pl.BlockSpec(memory_space=pl.ANY)


def _params(n_grid_axes=1):
    return pltpu.CompilerParams(dimension_semantics=("arbitrary",) * n_grid_axes, vmem_limit_bytes=VMEM_LIMIT)


def _rows(ts, width):
    return pl.BlockSpec((ts, width), lambda i: (i, 0))


def _const(shape):
    return pl.BlockSpec(shape, lambda i: (0,) * len(shape))


def _dot(a, b):
    return jnp.dot(a, b, preferred_element_type=F32)


def _dot_nt(a, b):
    return lax.dot_general(a, b, (((1,), (1,)), ((), ())), preferred_element_type=F32)


def _dot_tn(a, b):
    return lax.dot_general(a, b, (((0,), (0,)), ((), ())), preferred_element_type=F32)


def _rowmean(v):
    return jnp.mean(v, axis=-1, keepdims=True)


def _colsum(v):
    return jnp.sum(v, axis=0, keepdims=True)


def _gelu_parts(z):
    k0 = 0.7978845608028654
    k1 = 0.044715
    z2 = z * z
    t = jnp.tanh(z * (k0 + (k0 * k1) * z2))
    u = 0.5 * t + 0.5
    g = z * u
    dg = u + (0.5 * z) * (1.0 - t * t) * (k0 + (3.0 * k0 * k1) * z2)
    return g, dg


def _tril_weights(ws_ref):
    r = lax.broadcasted_iota(jnp.int32, (CHUNK, CHUNK), 0)
    s = lax.broadcasted_iota(jnp.int32, (CHUNK, CHUNK), 1)
    mask = (s <= r).astype(F32)
    return [(ws_ref[h] * mask).astype(BF16) for h in range(N_HEADS)]


def _window_counts(first_row, n_rows):
    pos = (first_row + lax.broadcasted_iota(jnp.int32, (n_rows, 1), 0)).astype(F32)
    return pos, [1.0 / jnp.minimum(pos + 1.0, float(w)) for w in POOL_WINDOWS]


def _causal_window_sums(ext):
    out = []
    e = ext
    shift = 1
    for g in range(len(POOL_WINDOWS)):
        e = e + pltpu.roll(e, shift, 0)
        shift *= 2
        out.append(e[:, g * GROUP_DIM:(g + 1) * GROUP_DIM])
    return out


def _anticausal_window_sums(ext):
    n = ext.shape[0]
    out = []
    e = ext
    shift = 1
    for g in range(len(POOL_WINDOWS)):
        e = e + pltpu.roll(e, n - shift, 0)
        shift *= 2
        out.append(e[:, g * GROUP_DIM:(g + 1) * GROUP_DIM])
    return out


def _fwd_mix(x, mod6, n1pre, n1post, n2pre, win_g, wout_g, w_spatial, bsp_full, gain, bias, w_pool, b_pool, pool_scale, ts):
    s_len = x.shape[0]
    rs = D_MODEL // N_CHIPS

    def body(x_ref, mod_ref, g1pre_ref, g1post_ref, g2pre_ref, win_ref, wout_ref, ws_ref, bsp_ref, gain_ref,
             bias_ref, wp_ref, bp_ref, ps_ref, z_ref, y_ref, mix_ref, x1_ref, h2_ref, mixed_ref, prev_ref, wfull_ref):
        i = pl.program_id(0)
        _zero_on_first_step(prev_ref)
        _join_w_in_on_first_step(win_ref, wfull_ref)
        xv = x_ref[...]
        r = lax.rsqrt(_rowmean(xv * xv) + EPS)
        hb = ((xv * r) * (g1pre_ref[...] * (1.0 + mod_ref[1:2, :])) + mod_ref[0:1, :]).astype(BF16)
        z_ref[...] = _dot(hb, wfull_ref[...])

        wc = _tril_weights(ws_ref)
        u, _, _, _, _ = _mixer_forward_tile(z_ref[:, :2 * D_A], wc, bsp_ref, gain_ref[...], bias_ref[...], mixed_ref)
        y_ref[:, :D_A] = (u * mixed_ref[...]).astype(BF16)
        zb = z_ref[:, 2 * D_A:]
        sums = _causal_window_sums(jnp.concatenate([prev_ref[...], zb], axis=0))
        prev_ref[...] = zb[ts - HALO:, :]
        _, inv_counts = _window_counts(i * ts, ts)
        for g in range(len(POOL_WINDOWS)):
            lanes = slice(g * GROUP_DIM, (g + 1) * GROUP_DIM)
            diff = sums[g][HALO:, :] * inv_counts[g] - zb[:, lanes]
            lin = _dot(diff.astype(BF16), wp_ref[g].astype(BF16)) + bp_ref[:, lanes]
            y_ref[:, D_A + g * GROUP_DIM:D_A + (g + 1) * GROUP_DIM] = (lin * ps_ref[:, lanes]).astype(BF16)

        mix = None
        for j in range(N_CHIPS):
            part = _dot(y_ref[:, j * rs:(j + 1) * rs], wout_ref[j])
            mix = part if mix is None else mix + part
        mix_ref[...] = mix
        r2 = lax.rsqrt(_rowmean(mix * mix) + EPS)
        x1 = xv + (mix * r2) * (mod_ref[2:3, :] * g1post_ref[...])
        x1_ref[...] = x1
        r3 = lax.rsqrt(_rowmean(x1 * x1) + EPS)
        h2_ref[...] = ((x1 * r3) * (g2pre_ref[...] * (1.0 + mod_ref[4:5, :])) + mod_ref[3:4, :]).astype(BF16)

    vec = _const((1, D_MODEL))
    f32_rows = jax.ShapeDtypeStruct((s_len, D_MODEL), F32)
    bf16_rows = jax.ShapeDtypeStruct((s_len, D_MODEL), BF16)
    return pl.pallas_call(
        body, name="fwd_mix", grid=(s_len // ts,),
        in_specs=[_rows(ts, D_MODEL), _const((N_MOD, D_MODEL)), vec, vec, vec, _VMEM, _VMEM,
                  _const((N_HEADS, CHUNK, CHUNK)), _const((CHUNK, D_A)), _const((1, D_A)), _const((1, D_A)),
                  _const((N_HEADS, GROUP_DIM, GROUP_DIM)), _const((1, D_B)), _const((1, D_B))],
        out_specs=[_rows(ts, D_Z), _rows(ts, D_MODEL), _rows(ts, D_MODEL), _rows(ts, D_MODEL), _rows(ts, D_MODEL)],
        out_shape=[jax.ShapeDtypeStruct((s_len, D_Z), F32), bf16_rows, f32_rows, f32_rows, bf16_rows],
        scratch_shapes=[pltpu.VMEM((ts, D_A), F32), pltpu.VMEM((HALO, D_B), F32), pltpu.VMEM((D_MODEL, D_Z), BF16)],
        compiler_params=_params(),
    )(x, mod6, n1pre, n1post, n2pre, win_g, wout_g, w_spatial, bsp_full, gain, bias, w_pool, b_pool, pool_scale)


def _mixer_forward_tile(za, wc, bsp_ref, gain, bias, mixed_ref):
    ga, dga = _gelu_parts(za)
    u = ga[:, :D_A]
    v = ga[:, D_A:]
    mu = _rowmean(v)
    vc = v - mu
    rstd = lax.rsqrt(_rowmean(vc * vc) + EPS)
    vhat = vc * rstd
    vn = (vhat * gain + bias).astype(BF16)
    ts = za.shape[0]
    for k in range(ts // CHUNK):
        for h in range(N_HEADS):
            blk = vn[k * CHUNK:(k + 1) * CHUNK, h * HEAD_DIM:(h + 1) * HEAD_DIM]
            mixed_ref[k * CHUNK:(k + 1) * CHUNK, h * HEAD_DIM:(h + 1) * HEAD_DIM] = (
                _dot(wc[h], blk) + bsp_ref[:, h * HEAD_DIM:(h + 1) * HEAD_DIM])
    return u, vhat, rstd, vn, dga


def _fwd_fc1(h2, fc1_g, ts):
    s_len = h2.shape[0]
    cs = D_FF // N_CHIPS

    def body(h_ref, w_ref, q_ref):
        hb = h_ref[...]
        for j in range(N_CHIPS):
            p = jnp.maximum(_dot(hb, w_ref[j]), 0.0)
            q_ref[:, j * cs:(j + 1) * cs] = (p * p).astype(BF16)

    return pl.pallas_call(
        body, name="fwd_fc1", grid=(s_len // ts,),
        in_specs=[_rows(ts, D_MODEL), _VMEM],
        out_specs=_rows(ts, D_FF),
        out_shape=jax.ShapeDtypeStruct((s_len, D_FF), BF16),
        compiler_params=_params(),
    )(h2, fc1_g)


def _fwd_fc2_loss(q, x1, target, mod6, n2post, fc2_g, ts):
    s_len = q.shape[0]
    rs = D_FF // N_CHIPS

    def body(q_ref, x1_ref, t_ref, mod_ref, g_ref, w_ref, dy_ref, df_ref, loss_ref, s_ref):
        _zero_on_first_step(loss_ref, s_ref)
        gate_gain = mod_ref[5:6, :] * g_ref[...]
        f = _dot(q_ref[:, 0:rs], w_ref[0])
        for j in range(1, N_CHIPS):
            f = f + _dot(q_ref[:, j * rs:(j + 1) * rs], w_ref[j])
        r4 = lax.rsqrt(_rowmean(f * f) + EPS)
        fh = f * r4
        err = (x1_ref[...] + fh * gate_gain) - t_ref[...]
        loss_ref[...] += 0.5 * jnp.sum(_rowmean(err * err), axis=0, keepdims=True)
        dy = err * (1.0 / D_MODEL)
        dy_ref[...] = dy
        s_ref[...] += _colsum(dy * fh)
        gh = dy * gate_gain
        df_ref[...] = (r4 * (gh - fh * _rowmean(gh * fh))).astype(BF16)

    return pl.pallas_call(
        body, name="fwd_fc2_loss", grid=(s_len // ts,),
        in_specs=[_rows(ts, D_FF), _rows(ts, D_MODEL), _rows(ts, D_MODEL), _const((N_MOD, D_MODEL)),
                  _const((1, D_MODEL)), _VMEM],
        out_specs=[_rows(ts, D_MODEL), _rows(ts, D_MODEL), _const((1, 1)), _const((1, D_MODEL))],
        out_shape=[jax.ShapeDtypeStruct((s_len, D_MODEL), F32), jax.ShapeDtypeStruct((s_len, D_MODEL), BF16),
                   jax.ShapeDtypeStruct((1, 1), F32), jax.ShapeDtypeStruct((1, D_MODEL), F32)],
        compiler_params=_params(),
    )(q, x1, target, mod6, n2post, fc2_g)


def _join_w_in_on_first_step(win_ref, full_ref):
    cs = D_Z // N_CHIPS

    @pl.when(pl.program_id(0) == 0)
    def _():
        for j in range(N_CHIPS):
            full_ref[:, j * cs:(j + 1) * cs] = win_ref[j]


def _zero_on_first_step(*refs):
    @pl.when(pl.program_id(0) == 0)
    def _():
        for ref in refs:
            ref[...] = jnp.zeros_like(ref)


def _on_last_step(fn):
    pl.when(pl.program_id(0) == pl.num_programs(0) - 1)(fn)


def _store_shard_on_last_step(acc_ref, hbm_ref, sem, j):
    _on_last_step(lambda: pltpu.make_async_copy(acc_ref.at[j], hbm_ref.at[j], sem.at[j]).start())


def _wait_stores_on_last_step(*stores):
    def wait_all():
        for acc_ref, hbm_ref, sem in stores:
            for j in range(N_CHIPS):
                pltpu.make_async_copy(acc_ref.at[j], hbm_ref.at[j], sem.at[j]).wait()

    _on_last_step(wait_all)


def _bwd_fc2(df, q, fc2_g, ts):
    s_len = df.shape[0]
    cs = D_FF // N_CHIPS

    def body(df_ref, q_ref, w_ref, dp_ref, dw_hbm, dw_ref, dw_sem):
        _zero_on_first_step(dw_ref)
        dfb = df_ref[...]
        df2 = dfb * 2.0
        for j in range(N_CHIPS):
            qb = q_ref[:, j * cs:(j + 1) * cs]
            dw_ref[j] += _dot_tn(qb, dfb).reshape(2, cs // 2, D_MODEL)
            _store_shard_on_last_step(dw_ref, dw_hbm, dw_sem, j)
            dq2 = _dot_nt(df2, w_ref[j])
            dp_ref[:, j * cs:(j + 1) * cs] = (dq2 * jnp.sqrt(qb.astype(F32))).astype(BF16)
        _wait_stores_on_last_step((dw_ref, dw_hbm, dw_sem))

    dw_shape = (N_CHIPS, 2, cs // 2, D_MODEL)
    return pl.pallas_call(
        body, name="bwd_fc2", grid=(s_len // ts,),
        in_specs=[_rows(ts, D_MODEL), _rows(ts, D_FF), _VMEM],
        out_specs=[_rows(ts, D_FF), _ANY],
        out_shape=[jax.ShapeDtypeStruct((s_len, D_FF), BF16), jax.ShapeDtypeStruct(dw_shape, F32)],
        scratch_shapes=[pltpu.VMEM(dw_shape, F32), pltpu.SemaphoreType.DMA((N_CHIPS,))],
        compiler_params=_params(),
    )(df, q, fc2_g)


def _bwd_fc1_out(dp, dy, x1, mix, h2, ycat, mod6, n2pre, n1post, fc1_g, wout_g, dep, ts):
    s_len = dp.shape[0]
    cs = D_FF // N_CHIPS
    rs = D_MODEL // N_CHIPS

    def body(dp_ref, dy_ref, x1_ref, mix_ref, h2_ref, yc_ref, mod_ref, g2_ref, g1_ref, w1_ref, wo_ref, dep_ref,
             dx1_ref, dyc_ref, dshift2_ref, da2_ref, s1_ref, dw1_hbm, dwo_hbm, dw1_ref, dwo_ref, dw1_sem, dwo_sem):
        _zero_on_first_step(dshift2_ref, da2_ref, s1_ref, dw1_ref, dwo_ref)
        h2b = h2_ref[...]
        dh2 = None
        for j in range(N_CHIPS):
            dpb = dp_ref[:, j * cs:(j + 1) * cs]
            dw1_ref[j] += _dot_tn(h2b, dpb).reshape(2, D_MODEL // 2, cs)
            _store_shard_on_last_step(dw1_ref, dw1_hbm, dw1_sem, j)
            part = _dot_nt(dpb, w1_ref[j])
            dh2 = part if dh2 is None else dh2 + part
        x1 = x1_ref[...]
        r3 = lax.rsqrt(_rowmean(x1 * x1) + EPS)
        xh = x1 * r3
        a2 = g2_ref[...] * (1.0 + mod_ref[4:5, :])
        dshift2_ref[...] += _colsum(dh2)
        da2_ref[...] += _colsum(dh2 * xh)
        dxh = dh2 * a2
        dx1 = dy_ref[...] + r3 * (dxh - xh * _rowmean(dxh * xh))
        dx1_ref[...] = dx1

        mix = mix_ref[...]
        r2 = lax.rsqrt(_rowmean(mix * mix) + EPS)
        mh = mix * r2
        s1_ref[...] += _colsum(dx1 * mh)
        gh = dx1 * (mod_ref[2:3, :] * g1_ref[...])
        dmix = (r2 * (gh - mh * _rowmean(gh * mh))).astype(BF16)
        dwo_ref[...] += _dot_tn(yc_ref[...], dmix).reshape(N_CHIPS, 2, rs // 2, D_MODEL)
        for j in range(N_CHIPS):
            _store_shard_on_last_step(dwo_ref, dwo_hbm, dwo_sem, j)
            dyc_ref[:, j * rs:(j + 1) * rs] = _dot_nt(dmix, wo_ref[j])
        _wait_stores_on_last_step((dw1_ref, dw1_hbm, dw1_sem), (dwo_ref, dwo_hbm, dwo_sem))

    vec = jax.ShapeDtypeStruct((1, D_MODEL), F32)
    dw1_shape = (N_CHIPS, 2, D_MODEL // 2, cs)
    dwo_shape = (N_CHIPS, 2, rs // 2, D_MODEL)
    return pl.pallas_call(
        body, name="bwd_fc1_out", grid=(s_len // ts,),
        in_specs=[_rows(ts, D_FF), _rows(ts, D_MODEL), _rows(ts, D_MODEL), _rows(ts, D_MODEL), _rows(ts, D_MODEL),
                  _rows(ts, D_MODEL), _const((N_MOD, D_MODEL)), _const((1, D_MODEL)), _const((1, D_MODEL)), _VMEM,
                  _VMEM, _ANY],
        out_specs=[_rows(ts, D_MODEL), _rows(ts, D_MODEL)] + [_const((1, D_MODEL))] * 3 + [_ANY, _ANY],
        out_shape=[jax.ShapeDtypeStruct((s_len, D_MODEL), F32), jax.ShapeDtypeStruct((s_len, D_MODEL), F32),
                   vec, vec, vec, jax.ShapeDtypeStruct(dw1_shape, F32), jax.ShapeDtypeStruct(dwo_shape, F32)],
        scratch_shapes=[pltpu.VMEM(dw1_shape, F32), pltpu.VMEM(dwo_shape, F32), pltpu.SemaphoreType.DMA((N_CHIPS,)),
                        pltpu.SemaphoreType.DMA((N_CHIPS,))],
        compiler_params=_params(),
    )(dp, dy, x1, mix, h2, ycat, mod6, n2pre, n1post, fc1_g, wout_g, dep)


def _mixer_bwd(z, dyc, w_spatial, bsp_full, gain, bias, w_pool, b_pool, pool_scale, dep, ts):
    s_len = z.shape[0]
    nb = ts // HALO
    last = s_len // HALO - 1
    te = ts + HALO

    def body(z_ref, zprev_ref, znext_ref, dyc_ref, dynext_ref, ws_ref, bsp_ref, gain_ref, bias_ref, wp_ref, bp_ref,
             ps_ref, dep_ref, dz_ref, dws_ref, dbsp_ref, dgain_ref, dbias_ref, dwp_ref, dbp_ref, dps_ref, mixed_ref,
             dvn_ref):
        i = pl.program_id(0)

        @pl.when(i == 0)
        def _():
            for ref in (dws_ref, dbsp_ref, dgain_ref, dbias_ref, dwp_ref, dbp_ref, dps_ref):
                ref[...] = jnp.zeros_like(ref)

        wc = _tril_weights(ws_ref)
        gain = gain_ref[...]
        u, vhat, rstd, vn, dga = _mixer_forward_tile(z_ref[:, :2 * D_A], wc, bsp_ref, gain, bias_ref[...], mixed_ref)
        dya = dyc_ref[:, :D_A]
        du = dya * mixed_ref[...]
        dmixed = dya * u
        dmb = dmixed.astype(BF16)
        dm_sum = dmixed[0:CHUNK, :]
        for k in range(1, ts // CHUNK):
            dm_sum = dm_sum + dmixed[k * CHUNK:(k + 1) * CHUNK, :]
        r_idx = lax.broadcasted_iota(jnp.int32, (CHUNK, CHUNK), 0)
        s_idx = lax.broadcasted_iota(jnp.int32, (CHUNK, CHUNK), 1)
        causal = (s_idx <= r_idx).astype(F32)
        for h in range(N_HEADS):
            lanes = slice(h * HEAD_DIM, (h + 1) * HEAD_DIM)
            dbsp_ref[h] += jnp.sum(dm_sum[:, lanes], axis=1, keepdims=True)
            acc = None
            for k in range(ts // CHUNK):
                rows = slice(k * CHUNK, (k + 1) * CHUNK)
                t = _dot_nt(dmb[rows, lanes], vn[rows, lanes])
                acc = t if acc is None else acc + t
                dvn_ref[rows, lanes] = _dot_tn(wc[h], dmb[rows, lanes])
            dws_ref[h] += acc * causal
        dvn = dvn_ref[...]
        dgain_ref[...] += _colsum(dvn * vhat)
        dbias_ref[...] += _colsum(dvn)
        dvh = dvn * gain
        dv = rstd * (dvh - _rowmean(dvh) - vhat * _rowmean(dvh * vhat))
        dz_ref[:, :D_A] = (du * dga[:, :D_A]).astype(BF16)
        dz_ref[:, D_A:2 * D_A] = (dv * dga[:, D_A:]).astype(BF16)

        zb = z_ref[:, 2 * D_A:]
        prev = jnp.where(i == 0, 0.0, zprev_ref[...])
        zb_ext = jnp.concatenate([zb, znext_ref[...]], axis=0)
        sums = _causal_window_sums(jnp.concatenate([prev, zb_ext], axis=0))
        pos, inv_counts = _window_counts(i * ts, te)
        dyb_ext = jnp.concatenate([dyc_ref[:, D_A:], dynext_ref[...]], axis=0)
        dlin_ext = dyb_ext * ps_ref[...]
        dbp_ref[...] += _colsum(dlin_ext[:ts, :])
        scaled = []
        ddiffs = []
        lins = []
        for g in range(len(POOL_WINDOWS)):
            lanes = slice(g * GROUP_DIM, (g + 1) * GROUP_DIM)
            diff = (sums[g][HALO:, :] * inv_counts[g] - zb_ext[:, lanes]).astype(BF16)
            wpb = wp_ref[g].astype(BF16)
            dlb = dlin_ext[:, lanes].astype(BF16)
            lins.append(_dot(diff[:ts, :], wpb) + bp_ref[:, lanes])
            dwp_ref[g] += _dot_tn(diff[:ts, :], dlb[:ts, :])
            dd = _dot_nt(dlb, wpb)
            ddiffs.append(dd)
            scaled.append(jnp.where(pos < float(s_len), dd * inv_counts[g], 0.0))
        dps_ref[...] += _colsum(dyb_ext[:ts, :] * jnp.concatenate(lins, axis=1))
        back = _anticausal_window_sums(jnp.concatenate(scaled, axis=1))
        for g in range(len(POOL_WINDOWS)):
            dz_ref[:, 2 * D_A + g * GROUP_DIM:2 * D_A + (g + 1) * GROUP_DIM] = (
                back[g][:ts, :] - ddiffs[g][:ts, :]).astype(BF16)

    sq = jax.ShapeDtypeStruct((N_HEADS, CHUNK, CHUNK), F32)
    vec = jax.ShapeDtypeStruct((1, D_A), F32)
    return pl.pallas_call(
        body, name="mixer_bwd", grid=(s_len // ts,),
        in_specs=[_rows(ts, D_Z),
                  pl.BlockSpec((HALO, D_B), lambda i: (jnp.maximum(i * nb - 1, 0), 2)),
                  pl.BlockSpec((HALO, D_B), lambda i: (jnp.minimum((i + 1) * nb, last), 2)),
                  _rows(ts, D_MODEL),
                  pl.BlockSpec((HALO, D_B), lambda i: (jnp.minimum((i + 1) * nb, last), 1)),
                  _const((N_HEADS, CHUNK, CHUNK)), _const((CHUNK, D_A)), _const((1, D_A)), _const((1, D_A)),
                  _const((N_HEADS, GROUP_DIM, GROUP_DIM)), _const((1, D_B)), _const((1, D_B)), _ANY],
        out_specs=[_rows(ts, D_Z), _const((N_HEADS, CHUNK, CHUNK)), _const((N_HEADS, CHUNK, 1)), _const((1, D_A)),
                   _const((1, D_A)), _const((N_HEADS, GROUP_DIM, GROUP_DIM)), _const((1, D_B)), _const((1, D_B))],
        out_shape=[jax.ShapeDtypeStruct((s_len, D_Z), BF16), sq, jax.ShapeDtypeStruct((N_HEADS, CHUNK, 1), F32), vec,
                   vec, sq, vec, vec],
        scratch_shapes=[pltpu.VMEM((ts, D_A), F32), pltpu.VMEM((ts, D_A), F32)],
        compiler_params=_params(),
    )(z, z, z, dyc, dyc, w_spatial, bsp_full, gain, bias, w_pool, b_pool, pool_scale, dep)


def _bwd_in(dz, dx1, x, mod6, n1pre, win_g, dep, ts):
    s_len = x.shape[0]
    cs = D_Z // N_CHIPS

    def body(dz_ref, dx1_ref, x_ref, mod_ref, g_ref, w_ref, dep_ref, gx_ref, dshift_ref, da_ref, dw_hbm, dw_ref,
             wfull_ref, dw_sem):
        _zero_on_first_step(dshift_ref, da_ref, dw_ref)
        _join_w_in_on_first_step(w_ref, wfull_ref)
        xv = x_ref[...]
        r = lax.rsqrt(_rowmean(xv * xv) + EPS)
        xh = xv * r
        h1b = (xh * (g_ref[...] * (1.0 + mod_ref[1:2, :])) + mod_ref[0:1, :]).astype(BF16)
        dzb = dz_ref[...]
        dw = _dot_tn(h1b, dzb)
        for j in range(N_CHIPS):
            dw_ref[j] += dw[:, j * cs:(j + 1) * cs].reshape(2, D_MODEL // 2, cs)
        dh = _dot_nt(dzb, wfull_ref[...])
        a1 = g_ref[...] * (1.0 + mod_ref[1:2, :])
        dshift_ref[...] += _colsum(dh)
        da_ref[...] += _colsum(dh * xh)
        dxh = dh * a1
        gx_ref[...] = dx1_ref[...] + r * (dxh - xh * _rowmean(dxh * xh))
        for j in range(N_CHIPS):
            _store_shard_on_last_step(dw_ref, dw_hbm, dw_sem, j)
        _wait_stores_on_last_step((dw_ref, dw_hbm, dw_sem))

    vec = jax.ShapeDtypeStruct((1, D_MODEL), F32)
    dw_shape = (N_CHIPS, 2, D_MODEL // 2, cs)
    return pl.pallas_call(
        body, name="bwd_in", grid=(s_len // ts,),
        in_specs=[_rows(ts, D_Z), _rows(ts, D_MODEL), _rows(ts, D_MODEL), _const((N_MOD, D_MODEL)),
                  _const((1, D_MODEL)), _VMEM, _ANY],
        out_specs=[_rows(ts, D_MODEL), _const((1, D_MODEL)), _const((1, D_MODEL)), _ANY],
        out_shape=[jax.ShapeDtypeStruct((s_len, D_MODEL), F32), vec, vec, jax.ShapeDtypeStruct(dw_shape, F32)],
        scratch_shapes=[pltpu.VMEM(dw_shape, F32), pltpu.VMEM((D_MODEL, D_Z), BF16),
                        pltpu.SemaphoreType.DMA((N_CHIPS,))],
        compiler_params=_params(),
    )(dz, dx1, x, mod6, n1pre, win_g, dep)


def _adamw_math(w, g, m, v):
    m = ADAM_B1 * m + (1.0 - ADAM_B1) * g
    v = ADAM_B2 * v + (1.0 - ADAM_B2) * (g * g)
    m_hat = m / (1.0 - ADAM_B1 ** ADAM_STEP)
    v_hat = v / (1.0 - ADAM_B2 ** ADAM_STEP)
    delta = -ADAM_LR * (m_hat / (jnp.sqrt(v_hat) + ADAM_EPS) + ADAM_WD * w)
    return delta, m, v


def _adamw(gs, ws, ms, vs, name, steps):
    n = len(ws)

    def body(*refs):
        for i in range(n):
            g_ref, w_ref, m_ref, v_ref = refs[4 * i:4 * i + 4]
            d, nm, nv = _adamw_math(w_ref[...], g_ref[...], m_ref[...], v_ref[...])
            for ref, val in zip(refs[4 * n + 3 * i:4 * n + 3 * i + 3], (d, nm, nv)):
                ref[...] = val

    specs = [_rows(w.shape[0] // steps, w.shape[1]) for w in ws]
    out = pl.pallas_call(
        body, name=name, grid=(steps,),
        in_specs=[s for s in specs for _ in range(4)], out_specs=[s for s in specs for _ in range(3)],
        out_shape=[jax.ShapeDtypeStruct(w.shape, F32) for w in ws for _ in range(3)],
        compiler_params=_params(),
    )(*[a for quad in zip(gs, ws, ms, vs) for a in quad])
    return [out[3 * i:3 * i + 3] for i in range(n)]


def _ada_grad_adamw(sc_t, dmod_all, w, m, v, tr):
    rows, cols = w.shape
    per_dev = dmod_all.shape[0] // N_DEV
    blocks = cols // LANES

    def body(s_ref, dm_ref, w_ref, m_ref, v_ref, g_ref, d_ref, nm_ref, nv_ref):
        x, y, _ = _position()
        first = (2 * x + y) * blocks
        for k in range(blocks):
            lanes = slice(k * LANES, (k + 1) * LANES)
            g = s_ref[:, 0:1] * dm_ref[pl.ds(first + k, 1), :]
            for b in range(1, N_DEV):
                g = g + s_ref[:, b:b + 1] * dm_ref[pl.ds(b * per_dev + first + k, 1), :]
            g_ref[:, lanes] = g
            d, nm, nv = _adamw_math(w_ref[:, lanes], g, m_ref[:, lanes], v_ref[:, lanes])
            d_ref[:, lanes] = d
            nm_ref[:, lanes] = nm
            nv_ref[:, lanes] = nv

    spec = _rows(tr, cols)
    shape = jax.ShapeDtypeStruct((rows, cols), F32)
    return pl.pallas_call(
        body, name="ada_grad_adamw", grid=(rows // tr,),
        in_specs=[_rows(tr, N_DEV), _const(dmod_all.shape), spec, spec, spec],
        out_specs=[spec] * 4, out_shape=[shape] * 4, compiler_params=_params(),
    )(sc_t, dmod_all, w, m, v)


def _mod_grads(da1, dshift1, s1, da2, dshift2, s2, mod6, n1pre, n1post, n2pre, n2post):
    def body(da1_ref, ds1_ref, s1_ref, da2_ref, ds2_ref, s2_ref, mod_ref, n1_ref, p1_ref, n2_ref, p2_ref, dmod_ref,
             dn_ref):
        dmod_ref[0:1, :] = ds1_ref[...]
        dmod_ref[1:2, :] = da1_ref[...] * n1_ref[...]
        dmod_ref[2:3, :] = s1_ref[...] * p1_ref[...]
        dmod_ref[3:4, :] = ds2_ref[...]
        dmod_ref[4:5, :] = da2_ref[...] * n2_ref[...]
        dmod_ref[5:6, :] = s2_ref[...] * p2_ref[...]
        dn_ref[0:1, :] = da1_ref[...] * (1.0 + mod_ref[1:2, :])
        dn_ref[1:2, :] = s1_ref[...] * mod_ref[2:3, :]
        dn_ref[2:3, :] = da2_ref[...] * (1.0 + mod_ref[4:5, :])
        dn_ref[3:4, :] = s2_ref[...] * mod_ref[5:6, :]

    return pl.pallas_call(
        body, name="mod_grads",
        out_shape=[jax.ShapeDtypeStruct((N_MOD, D_MODEL), F32), jax.ShapeDtypeStruct((4, D_MODEL), F32)],
    )(da1, dshift1, s1, da2, dshift2, s2, mod6, n1pre, n1post, n2pre, n2post)


def _position():
    x, y, c = lax.axis_index("x"), lax.axis_index("y"), lax.axis_index("c")
    return x, y, c


def _flip(v, bit):
    return 1 - v if bit else v


def _peer(x, y, c, k):
    return (_flip(x, k & 4), _flip(y, k & 2), _flip(c, k & 1))


def _remote(src, dst, send_sem, recv_sem, device):
    return pltpu.make_async_remote_copy(src_ref=src, dst_ref=dst, send_sem=send_sem, recv_sem=recv_sem,
                                        device_id=device, device_id_type=MESH)


def _mod_exchange(c_row, w_ada_shard, b_ada_row, ws, early):
    cs = w_ada_shard.shape[1]
    n = len(ws)

    def body(c_ref, w_hbm, b_ref, *refs):
        shards, (mod_ref, sc_ref), zones = refs[:n], refs[n:n + 2], refs[n + 2:2 * n + 2]
        rows_ref, w_ref, w_sem, send1, recv1, send2, recv2 = refs[2 * n + 2:2 * n + 9]
        wide, narrow = refs[2 * n + 9:3 * n + 9], refs[3 * n + 9:4 * n + 9]
        cast_load, cast_store = refs[4 * n + 9:]
        x, y, c = _position()
        me = 4 * x + 2 * y + c
        chip = 2 * x + y
        w_load = pltpu.make_async_copy(w_hbm, w_ref, w_sem)
        w_load.start()
        loads = [pltpu.make_async_copy(shards[i], wide[i], cast_load.at[i]) for i in range(n)]
        stores = [pltpu.make_async_copy(narrow[i], zones[i].at[chip], cast_store.at[i]) for i in range(n)]

        def cast(i):
            loads[i].wait()
            narrow[i][...] = wide[i][...].astype(BF16)
            stores[i].start()

        for i in range(early):
            loads[i].start()
        cv = c_ref[...]
        sc_ref[me] = cv * jax.nn.sigmoid(cv)
        gather = [_remote(sc_ref.at[me], sc_ref.at[me], send1.at[k - 1], recv1.at[k - 1], _peer(x, y, c, k))
                  for k in range(1, N_DEV)]
        for cp in gather:
            cp.start()
        for i in range(early):
            cast(i)
        for i in range(early, n):
            loads[i].start()
        for k in range(1, N_DEV):
            px, py, pc = _peer(x, y, c, k)
            src = 4 * px + 2 * py + pc
            _remote(sc_ref.at[src], sc_ref.at[src], send1.at[k - 1], recv1.at[k - 1], (px, py, pc)).wait_recv()
        for cp in gather:
            cp.wait_send()
        sc_all = jnp.concatenate([sc_ref[b] for b in range(N_DEV)], axis=0)
        w_load.wait()
        part = _dot(sc_all.astype(BF16), w_ref[...].astype(BF16))
        part = part + b_ref[:, pl.ds(pl.multiple_of(chip * cs, LANES), cs)]
        for b in range(N_DEV):
            rows_ref[b] = part[b:b + 1, :]
        mod_ref[chip] = rows_ref[me]
        hand = []
        for k in (2, 4, 6):
            px, py, _ = _peer(x, y, c, k)
            hand.append(_remote(rows_ref.at[4 * px + 2 * py + c], mod_ref.at[chip], send2.at[k // 2 - 1],
                                recv2.at[k // 2 - 1], (px, py, c)))
        for cp in hand:
            cp.start()
        for i in range(early, n):
            cast(i)
        for k in (2, 4, 6):
            px, py, _ = _peer(x, y, c, k)
            pchip = 2 * px + py
            _remote(rows_ref.at[me], mod_ref.at[pchip], send2.at[k // 2 - 1], recv2.at[k // 2 - 1],
                    (px, py, c)).wait_recv()
        for cp in hand:
            cp.wait_send()
        for cp in stores:
            cp.wait()

    out = pl.pallas_call(
        body, name="mod_exchange",
        in_specs=[_VMEM, _ANY, _VMEM] + [_ANY] * n, out_specs=[_VMEM, _VMEM] + [_ANY] * n,
        out_shape=[jax.ShapeDtypeStruct((N_CHIPS, 1, cs), F32), jax.ShapeDtypeStruct((N_DEV, 1, D_MODEL), F32)]
        + [jax.ShapeDtypeStruct((N_CHIPS,) + w.shape, BF16) for w in ws],
        scratch_shapes=[pltpu.VMEM((N_DEV, 1, cs), F32), pltpu.VMEM(w_ada_shard.shape, F32), pltpu.SemaphoreType.DMA,
                        pltpu.SemaphoreType.DMA((N_DEV - 1,)),
                        pltpu.SemaphoreType.DMA((N_DEV - 1,)), pltpu.SemaphoreType.DMA((N_CHIPS - 1,)),
                        pltpu.SemaphoreType.DMA((N_CHIPS - 1,))]
        + [pltpu.VMEM(w.shape, F32) for w in ws] + [pltpu.VMEM(w.shape, BF16) for w in ws]
        + [pltpu.SemaphoreType.DMA((n,))] * 2,
        compiler_params=pltpu.CompilerParams(vmem_limit_bytes=VMEM_LIMIT),
    )(c_row, w_ada_shard, b_ada_row, *ws)
    return out[0], out[1], list(out[2:])


_HBM = pl.BlockSpec(memory_space=pltpu.HBM)
_SEM = pl.BlockSpec(memory_space=pltpu.SEMAPHORE)
_EFFECT = pltpu.SideEffectType.DATAFLOW_SIDE_EFFECTING
_CHIP_HOPS = (2, 4, 6)


def _in_hbm(a):
    return pltpu.with_memory_space_constraint(a, pltpu.HBM)


def _sems3():
    return pltpu.SemaphoreType.DMA((len(_CHIP_HOPS),))


def _ag_start(lands, after, name):
    n = len(lands)

    def body(*refs):
        zones = refs[:n]
        sends, recvs = refs[n + 1:2 * n + 1], refs[2 * n + 1:3 * n + 1]
        x, y, c = _position()
        chip = 2 * x + y
        for i in range(n):
            half = zones[i].shape[1] // 2
            mine = zones[i].at[chip, pl.ds(c * half, half)]
            for s, k in enumerate(_CHIP_HOPS):
                px, py, _ = _peer(x, y, c, k)
                _remote(mine, mine, sends[i].at[s], recvs[i].at[s], (px, py, c)).start()

    out = pl.pallas_call(
        body, name=name,
        in_specs=[_HBM] * n + [_ANY],
        out_specs=[_SEM] * (2 * n) + [_HBM] * n,
        out_shape=[_sems3()] * (2 * n) + [pltpu.HBM(z.shape, BF16) for z in lands],
        input_output_aliases={i: 2 * n + i for i in range(n)},
        compiler_params=pltpu.CompilerParams(has_side_effects=_EFFECT),
    )(*[_in_hbm(z) for z in lands], after)
    return [(out[2 * n + i], out[i], out[n + i]) for i in range(n)]


def _ag_pass(group, after, name):
    n = len(group)

    def body(*refs):
        zones = refs[:n]
        sends, recvs = refs[n:2 * n], refs[2 * n:3 * n]
        fsends, frecvs = refs[4 * n + 1:5 * n + 1], refs[5 * n + 1:6 * n + 1]
        x, y, c = _position()
        chip = 2 * x + y
        for i in range(n):
            half = zones[i].shape[1] // 2
            rows = pl.ds(c * half, half)
            for s, k in enumerate(_CHIP_HOPS):
                px, py, _ = _peer(x, y, c, k)
                landed = zones[i].at[2 * px + py, rows]
                _remote(landed, landed, sends[i].at[s], recvs[i].at[s], (px, py, c)).wait_recv()
                _remote(landed, landed, fsends[i].at[s], frecvs[i].at[s], (x, y, 1 - c)).start()
        for i in range(n):
            half = zones[i].shape[1] // 2
            mine = zones[i].at[chip, pl.ds(c * half, half)]
            for s, k in enumerate(_CHIP_HOPS):
                px, py, _ = _peer(x, y, c, k)
                _remote(mine, mine, sends[i].at[s], recvs[i].at[s], (px, py, c)).wait_send()

    out = pl.pallas_call(
        body, name=name,
        in_specs=[_HBM] * n + [_SEM] * (2 * n) + [_ANY],
        out_specs=[_HBM] * n + [_SEM] * (2 * n),
        out_shape=[pltpu.HBM(g[0].shape, BF16) for g in group] + [_sems3()] * (2 * n),
        input_output_aliases={i: i for i in range(n)},
        compiler_params=pltpu.CompilerParams(has_side_effects=_EFFECT),
    )(*[g[0] for g in group], *[g[1] for g in group], *[g[2] for g in group], after)
    return [(out[i], out[n + i], out[2 * n + i]) for i in range(n)]


def _ag_done(group, name):
    n = len(group)

    def body(*refs):
        lands = refs[:n]
        fsends, frecvs = refs[n:2 * n], refs[2 * n:3 * n]
        x, y, c = _position()
        for i in range(n):
            half = lands[i].shape[1] // 2
            for s, k in enumerate(_CHIP_HOPS):
                px, py, _ = _peer(x, y, c, k)
                sent = lands[i].at[2 * px + py, pl.ds(c * half, half)]
                got = lands[i].at[2 * px + py, pl.ds((1 - c) * half, half)]
                cp = _remote(sent, got, fsends[i].at[s], frecvs[i].at[s], (x, y, 1 - c))
                cp.wait_recv()
                cp.wait_send()

    out = pl.pallas_call(
        body, name=name,
        in_specs=[_HBM] * n + [_SEM] * (2 * n),
        out_specs=[_HBM] * n,
        out_shape=[pltpu.HBM(g[0].shape, BF16) for g in group],
        input_output_aliases={i: i for i in range(n)},
        compiler_params=pltpu.CompilerParams(has_side_effects=_EFFECT),
    )(*[g[0] for g in group], *[g[1] for g in group], *[g[2] for g in group])
    return list(out)


def _small_spread_wait(slots, sends, recvs, afters):
    def body(z_ref, sends, recvs, *rest):
        x, y, c = _position()
        mine = z_ref.at[2 * x + y]
        for s, k in enumerate(_CHIP_HOPS):
            px, py, _ = _peer(x, y, c, k)
            cp = _remote(mine, z_ref.at[2 * px + py], sends.at[s], recvs.at[s], (px, py, c))
            cp.wait_recv()
            cp.wait_send()

    return pl.pallas_call(
        body, name="small_spread_wait",
        in_specs=[_HBM, _SEM, _SEM] + [_ANY] * len(afters), out_specs=_HBM, out_shape=pltpu.HBM(slots.shape, F32),
        input_output_aliases={0: 0},
        compiler_params=pltpu.CompilerParams(has_side_effects=_EFFECT),
    )(slots, sends, recvs, *afters)


def _sibling_sum(pgs, name, small=None):
    n = len(pgs)
    k = 0 if small is None else 1
    units = [(i, j) for i in range(n) for j in range(N_CHIPS)]

    def body(*refs):
        refs = list(refs)
        take = lambda count: [refs.pop(0) for _ in range(count)]
        ins, small_in = take(n), take(k)
        qbs, owns, slots_out = take(n), take(n), take(k)
        mine, other, stage, got = take(n), take(n), take(n), take(n)
        load_a, load_b, send, recv, keep_wire, keep_own = take(6)
        x, y, c = _position()
        chip = 2 * x + y
        if k:
            sib_ref, pair_send, pair_recv, keep_small = take(4)
            pair = _remote(small_in[0], sib_ref, pair_send, pair_recv, (x, y, 1 - c))
            pair.start()
        loads_a = [pltpu.make_async_copy(ins[i].at[j, 1 - c], other[i].at[j], load_a.at[u])
                   for u, (i, j) in enumerate(units)]
        loads_b = [pltpu.make_async_copy(ins[i].at[j, c], mine[i].at[j], load_b.at[u])
                   for u, (i, j) in enumerate(units)]
        for cp in loads_a + loads_b:
            cp.start()
        sent = []
        for u, (i, j) in enumerate(units):
            loads_a[u].wait()
            stage[i][j] = other[i][j].astype(BF16)
            cp = _remote(stage[i].at[j], got[i].at[j], send.at[u], recv.at[u], (x, y, 1 - c))
            cp.start()
            sent.append(cp)
        stores = []
        for u, (i, j) in enumerate(units):
            loads_b[u].wait()
            sent[u].wait_recv()
            q = mine[i][j] + got[i][j].astype(F32)
            mine[i][j] = q
            got[i][j] = q.astype(BF16)
            stores.append(pltpu.make_async_copy(got[i].at[j], qbs[i].at[j], keep_wire.at[u]))
            stores[-1].start()
        for i in range(n):
            stores.append(pltpu.make_async_copy(mine[i].at[chip], owns[i], keep_own.at[i]))
            stores[-1].start()
        if k:
            pair.wait()
            sib_ref[...] = small_in[0][...] + sib_ref[...]
            stores.append(pltpu.make_async_copy(sib_ref, slots_out[0].at[chip], keep_small))
            stores[-1].start()
        for cp in sent:
            cp.wait_send()
        for cp in stores:
            cp.wait()

    wire = [(N_CHIPS,) + p.shape[2:] for p in pgs]
    extra_out, extra_scratch = [], []
    if k:
        extra_out = [jax.ShapeDtypeStruct((N_CHIPS,) + small.shape, F32)]
        extra_scratch = [pltpu.VMEM(small.shape, F32)] + [pltpu.SemaphoreType.DMA] * 3
    out = pl.pallas_call(
        body, name=name, in_specs=[_ANY] * n + [_VMEM] * k, out_specs=[_ANY] * (2 * n + k),
        out_shape=[jax.ShapeDtypeStruct(w, BF16) for w in wire] + [jax.ShapeDtypeStruct(w[1:], F32) for w in wire]
        + extra_out,
        scratch_shapes=[pltpu.VMEM(w, F32) for w in wire] * 2 + [pltpu.VMEM(w, BF16) for w in wire] * 2
        + [pltpu.SemaphoreType.DMA((len(units),))] * 5 + [pltpu.SemaphoreType.DMA((n,))] + extra_scratch,
        compiler_params=pltpu.CompilerParams(vmem_limit_bytes=VMEM_LIMIT),
    )(*pgs, *([small] if k else []))
    return list(out[:n]), list(out[n:2 * n]), list(out[2 * n:])


def _rs_start(qbs, name, small=None):
    n = len(qbs)
    k = 0 if small is None else 1

    def body(*refs):
        outs, inboxes, slots = refs[:n], refs[n:2 * n], refs[2 * n:2 * n + k]
        sems = refs[2 * n + k:4 * n + 3 * k]
        sends, recvs = sems[:n], sems[n:2 * n]
        x, y, c = _position()
        chip = 2 * x + y
        if k:
            mine = slots[0].at[chip]
            for s, hop in enumerate(_CHIP_HOPS):
                px, py, _ = _peer(x, y, c, hop)
                _remote(mine, mine, sems[2 * n].at[s], sems[2 * n + 1].at[s], (px, py, c)).start()
        for i in range(n):
            for s, hop in enumerate(_CHIP_HOPS):
                px, py, _ = _peer(x, y, c, hop)
                _remote(outs[i].at[2 * px + py], inboxes[i].at[chip], sends[i].at[s], recvs[i].at[s], (px, py, c)).start()

    inboxes = [_in_hbm(lax.empty(q.shape, BF16)) for q in qbs]
    n_sems = 2 * n + 2 * k
    out = pl.pallas_call(
        body, name=name,
        in_specs=[_HBM] * (2 * n + k),
        out_specs=[_SEM] * n_sems + [_HBM] * (2 * n + k),
        out_shape=[_sems3()] * n_sems + [pltpu.HBM(q.shape, BF16) for q in qbs] * 2
        + ([pltpu.HBM(small.shape, F32)] if k else []),
        input_output_aliases={i: n_sems + i for i in range(2 * n + k)},
        compiler_params=pltpu.CompilerParams(has_side_effects=_EFFECT),
    )(*[_in_hbm(q) for q in qbs], *inboxes, *([_in_hbm(small)] if k else []))
    states = [(out[n_sems + i], out[n_sems + n + i], out[i], out[n + i]) for i in range(n)]
    return (states, (out[n_sems + 2 * n], out[2 * n], out[2 * n + 1])) if k else states


def _rs_wait(group, after, name):
    n = len(group)

    def body(*refs):
        outs, inboxes = refs[:n], refs[n:2 * n]
        sends, recvs = refs[2 * n:3 * n], refs[3 * n:4 * n]
        x, y, c = _position()
        for i in range(n):
            for s, k in enumerate(_CHIP_HOPS):
                px, py, _ = _peer(x, y, c, k)
                slot = 2 * px + py
                cp = _remote(outs[i].at[slot], inboxes[i].at[slot], sends[i].at[s], recvs[i].at[s], (px, py, c))
                cp.wait_recv()
                cp.wait_send()

    out = pl.pallas_call(
        body, name=name,
        in_specs=[_HBM] * (2 * n) + [_SEM] * (2 * n) + [_ANY],
        out_specs=[_HBM] * n,
        out_shape=[pltpu.HBM(g[1].shape, BF16) for g in group],
        input_output_aliases={n + i: i for i in range(n)},
        compiler_params=pltpu.CompilerParams(has_side_effects=_EFFECT),
    )(*[g[0] for g in group], *[g[1] for g in group], *[g[2] for g in group], *[g[3] for g in group], after)
    return list(out)


def _final_share(inboxes, owns, name):
    n = len(inboxes)
    units = [(i, s) for i in range(n) for s in range(len(_CHIP_HOPS))]

    def body(*refs):
        ins, mine, outs, landed, half = (refs[k * n:(k + 1) * n] for k in range(5))
        load, load_own, keep, send, recv = refs[5 * n:]
        x, y, c = _position()
        loads = []
        for u, (i, s) in enumerate(units):
            px, py, _ = _peer(x, y, c, _CHIP_HOPS[s])
            loads.append(pltpu.make_async_copy(ins[i].at[2 * px + py], landed[i].at[s], load.at[u]))
        loads_own = [pltpu.make_async_copy(mine[i], half[i], load_own.at[i]) for i in range(n)]
        for cp in loads + loads_own:
            cp.start()
        copies = []
        for i in range(n):
            for s in range(len(_CHIP_HOPS)):
                loads[len(_CHIP_HOPS) * i + s].wait()
            loads_own[i].wait()
            total = (landed[i][0].astype(F32) + landed[i][1].astype(F32)) + landed[i][2].astype(F32)
            half[i][...] = total + half[i][...]
            copies.append(pltpu.make_async_copy(half[i], outs[i].at[c], keep.at[i]))
            copies.append(_remote(half[i], outs[i].at[c], send.at[i], recv.at[i], (x, y, 1 - c)))
            for cp in copies[-2:]:
                cp.start()
        for i in range(n):
            theirs = outs[i].at[1 - c]
            _remote(theirs, theirs, send.at[i], recv.at[i], (x, y, 1 - c)).wait_recv()
        for i in range(n):
            copies[2 * i].wait()
            copies[2 * i + 1].wait_send()

    return pl.pallas_call(
        body, name=name, in_specs=[_ANY] * (2 * n), out_specs=[_ANY] * n,
        out_shape=[jax.ShapeDtypeStruct((2,) + o.shape, F32) for o in owns],
        scratch_shapes=[pltpu.VMEM((len(_CHIP_HOPS),) + o.shape, BF16) for o in owns]
        + [pltpu.VMEM(o.shape, F32) for o in owns]
        + [pltpu.SemaphoreType.DMA((len(units),))] + [pltpu.SemaphoreType.DMA((n,))] * 4,
        compiler_params=pltpu.CompilerParams(vmem_limit_bytes=VMEM_LIMIT),
    )(*inboxes, *owns)


_SMALL = (("b_ada", N_MOD * D_MODEL), ("norm1_pre", D_MODEL), ("norm1_post", D_MODEL), ("norm2_pre", D_MODEL),
          ("norm2_post", D_MODEL), ("w_spatial", N_HEADS * CHUNK * CHUNK), ("b_spatial", N_HEADS * CHUNK),
          ("ln_v_gain", D_A), ("ln_v_bias", D_A), ("w_pool", N_HEADS * GROUP_DIM * GROUP_DIM),
          ("b_pool", D_B), ("pool_scale", D_B))
_MOD_ROWS = N_MOD * D_MODEL // LANES


def _packed_rows(size):
    return -(-(size // LANES) // SUBLANES) * SUBLANES


def _pack(parts):
    out = []
    for name, size in _SMALL:
        a = parts[name].reshape(size // LANES, LANES)
        pad = _packed_rows(size) - a.shape[0]
        out.append(jnp.pad(a, ((0, pad), (0, 0))) if pad else a)
    return out


def _small_adamw(slots, ws, ms, vs):
    n = len(_SMALL)
    head = N_DEV * _MOD_ROWS

    def body(*refs):
        s_ref, w, m, v = refs[0], refs[1:1 + n], refs[1 + n:1 + 2 * n], refs[1 + 2 * n:1 + 3 * n]
        outs = refs[1 + 3 * n:1 + 7 * n]
        dmod_ref, loss_ref, t_ref = refs[1 + 7 * n:]
        t_ref[...] = ((s_ref[0] + s_ref[1]) + s_ref[2]) + s_ref[3]
        dmod_ref[...] = t_ref[0:head, :]
        loss_ref[...] = t_ref[t_ref.shape[0] - SUBLANES:t_ref.shape[0] - SUBLANES + 1, 0:1]
        row = head
        for i, (_, size) in enumerate(_SMALL):
            if i == 0:
                g = t_ref[0:_MOD_ROWS, :]
                for b in range(1, N_DEV):
                    g = g + t_ref[b * _MOD_ROWS:(b + 1) * _MOD_ROWS, :]
            else:
                g = t_ref[row:row + size // LANES, :]
                row += _packed_rows(size)
            d, nm, nv = _adamw_math(w[i][...], g, m[i][...], v[i][...])
            for ref, val in zip(outs[4 * i:4 * i + 4], (g, d, nm, nv)):
                ref[...] = val

    each = [jax.ShapeDtypeStruct((size // LANES, LANES), F32) for _, size in _SMALL for _ in range(4)]
    out = pl.pallas_call(
        body, name="small_adamw",
        out_shape=each + [jax.ShapeDtypeStruct((head, LANES), F32), jax.ShapeDtypeStruct((1, 1), F32)],
        scratch_shapes=[pltpu.VMEM(slots.shape[1:], F32)],
        compiler_params=pltpu.CompilerParams(vmem_limit_bytes=VMEM_LIMIT),
    )(slots, *ws, *ms, *vs)
    return [out[4 * i:4 * i + 4] for i in range(n)], out[4 * n], out[4 * n + 1]


def kernel(x, c, w_ada, b_ada, norm1_pre, norm1_post, w_in, w_spatial, b_spatial, ln_v_gain, ln_v_bias, w_pool, b_pool, pool_scale, w_out, norm2_pre, norm2_post, w_fc1, w_fc2, loss_target, m_w_ada, m_b_ada, m_norm1_pre, m_norm1_post, m_w_in, m_w_spatial, m_b_spatial, m_ln_v_gain, m_ln_v_bias, m_w_pool, m_b_pool, m_pool_scale, m_w_out, m_norm2_pre, m_norm2_post, m_w_fc1, m_w_fc2, v_w_ada, v_b_ada, v_norm1_pre, v_norm1_post, v_w_in, v_w_spatial, v_b_spatial, v_ln_v_gain, v_ln_v_bias, v_w_pool, v_b_pool, v_pool_scale, v_w_out, v_norm2_pre, v_norm2_post, v_w_fc1, v_w_fc2):
    weights = dict(w_ada=w_ada, b_ada=b_ada, norm1_pre=norm1_pre, norm1_post=norm1_post, w_in=w_in,
                   w_spatial=w_spatial, b_spatial=b_spatial, ln_v_gain=ln_v_gain, ln_v_bias=ln_v_bias, w_pool=w_pool,
                   b_pool=b_pool, pool_scale=pool_scale, w_out=w_out, norm2_pre=norm2_pre, norm2_post=norm2_post,
                   w_fc1=w_fc1, w_fc2=w_fc2)
    m_old = dict(w_ada=m_w_ada, b_ada=m_b_ada, norm1_pre=m_norm1_pre, norm1_post=m_norm1_post, w_in=m_w_in,
                 w_spatial=m_w_spatial, b_spatial=m_b_spatial, ln_v_gain=m_ln_v_gain, ln_v_bias=m_ln_v_bias,
                 w_pool=m_w_pool, b_pool=m_b_pool, pool_scale=m_pool_scale, w_out=m_w_out, norm2_pre=m_norm2_pre,
                 norm2_post=m_norm2_post, w_fc1=m_w_fc1, w_fc2=m_w_fc2)
    v_old = dict(w_ada=v_w_ada, b_ada=v_b_ada, norm1_pre=v_norm1_pre, norm1_post=v_norm1_post, w_in=v_w_in,
                 w_spatial=v_w_spatial, b_spatial=v_b_spatial, ln_v_gain=v_ln_v_gain, ln_v_bias=v_ln_v_bias,
                 w_pool=v_w_pool, b_pool=v_b_pool, pool_scale=v_pool_scale, w_out=v_w_out, norm2_pre=v_norm2_pre,
                 norm2_post=v_norm2_post, w_fc1=v_w_fc1, w_fc2=v_w_fc2)
    order = ("w_ada", "b_ada", "norm1_pre", "norm1_post", "w_in", "w_spatial", "b_spatial", "ln_v_gain", "ln_v_bias",
             "w_pool", "b_pool", "pool_scale", "w_out", "norm2_pre", "norm2_post", "w_fc1", "w_fc2")
    mx, my, mc = _position()
    me = 4 * mx + 2 * my + mc
    chip = 2 * mx + my
    row = lambda a: a.reshape(1, -1)

    xs, target = x[0], loss_target[0]
    n1pre, n1post, n2pre, n2post = row(norm1_pre), row(norm1_post), row(norm2_pre), row(norm2_post)
    mixer = (w_spatial, jnp.repeat(b_spatial.T, HEAD_DIM, axis=1), row(ln_v_gain), row(ln_v_bias), w_pool,
             row(b_pool), row(pool_scale))
    ts_big, ts_mid = 512, 256

    mod4, sc_all, lands = _mod_exchange(c, w_ada, row(b_ada), [w_in, w_out, w_fc1, w_fc2], 2)
    mod6 = mod4.reshape(N_MOD, D_MODEL)
    ag = _ag_start(lands, mod4, "ag_start")

    win_g, wout_g = _ag_done(_ag_pass([ag[0], ag[1]], ag[2][0], "ag_pass_mix"), "ag_done_mix")
    z, ycat, mix, x1, h2 = _fwd_mix(xs, mod6, n1pre, n1post, n2pre, win_g, wout_g, *mixer, ts_big)
    (fc1_g,) = _ag_done(_ag_pass([ag[2]], h2, "ag_pass_fc1"), "ag_done_fc1")
    q = _fwd_fc1(h2, fc1_g, ts_big)
    (fc2_g,) = _ag_done(_ag_pass([ag[3]], q, "ag_pass_fc2"), "ag_done_fc2")
    dy, df, loss, s2 = _fwd_fc2_loss(q, x1, target, mod6, n2post, fc2_g, ts_big)

    def reduce_start(partials, tag, small=None):
        wire, owns, slots = _sibling_sum(partials, "sibling_sum_" + tag, small)
        return _rs_start(wire, "rs_start_" + tag), owns, slots

    def reduce_finish(state, owns, names, tag, dep):
        inboxes = _rs_wait(state, dep, "rs_wait_" + tag)
        shards = _final_share(inboxes, owns, "final_share_" + tag)
        for n, g in zip(names, shards):
            grads[n] = g.reshape(weights[n].shape)
        updates = _adamw([grads[n] for n in names], [weights[n] for n in names], [m_old[n] for n in names],
                         [v_old[n] for n in names], "adamw_" + tag, 4)
        for n, (d, nm, nv) in zip(names, updates):
            deltas[n], new_m[n], new_v[n] = d, nm, nv

    grads, deltas, new_m, new_v = {}, {}, {}, {}
    dp, g_fc2 = _bwd_fc2(df, q, fc2_g, ts_big)
    state_fc2, owns_fc2, _ = reduce_start([g_fc2], "fc2")
    dx1, dyc, dshift2, da2, s1, g_fc1, g_out = _bwd_fc1_out(
        dp, dy, x1, mix, h2, ycat, mod6, n2pre, n1post, fc1_g, wout_g, state_fc2[0][0], ts_mid)
    state_mid, owns_mid, _ = reduce_start([g_fc1, g_out], "mid")
    dz, dws, dbsp, dgain, dbias, dwp, dbp, dps = _mixer_bwd(z, dyc, *mixer, state_mid[0][0], ts_big)
    grad_x, dshift1, da1, g_in = _bwd_in(dz, dx1, xs, mod6, n1pre, win_g, state_mid[0][0], ts_big)
    dmod6, dnorms = _mod_grads(da1, dshift1, s1, da2, dshift2, s2, mod6, n1pre, n1post, n2pre, n2post)

    parts = dict(b_ada=dmod6, norm1_pre=dnorms[0], norm1_post=dnorms[1], norm2_pre=dnorms[2], norm2_post=dnorms[3],
                 w_spatial=dws, b_spatial=dbsp, ln_v_gain=dgain, ln_v_bias=dbias, w_pool=dwp, b_pool=dbp,
                 pool_scale=dps)
    pieces = _pack(parts)
    slots = lax.dynamic_update_slice(jnp.zeros((N_DEV * _MOD_ROWS, LANES), F32), pieces[0], (me * _MOD_ROWS, 0))
    loss_tile = jnp.pad(loss, ((0, SUBLANES - 1), (0, LANES - 1)))
    wire_in, owns_in, pair_sum = _sibling_sum([g_in], "sibling_sum_in",
                                              jnp.concatenate([slots] + pieces[1:] + [loss_tile], axis=0))
    state_in, spread = _rs_start(wire_in, "rs_start_in", pair_sum[0])
    reduce_finish(state_fc2 + state_mid, owns_fc2 + owns_mid, ("w_fc2", "w_fc1", "w_out"), "mlp", state_in[0][0])
    flat = lambda d: [d[n].reshape(size // LANES, LANES) for n, size in _SMALL]
    small_out, dmod_all, loss = _small_adamw(
        _small_spread_wait(*spread, [deltas[n] for n in ("w_fc2", "w_fc1", "w_out")]), flat(weights), flat(m_old),
        flat(v_old))
    loss = loss.reshape(())
    for (n, _), (g, d, nm, nv) in zip(_SMALL, small_out):
        shape = weights[n].shape
        grads[n], deltas[n], new_m[n], new_v[n] = g.reshape(shape), d.reshape(shape), nm.reshape(shape), nv.reshape(shape)

    sc_t = sc_all.reshape(N_DEV, D_MODEL).T
    grads["w_ada"], deltas["w_ada"], new_m["w_ada"], new_v["w_ada"] = _ada_grad_adamw(
        sc_t, dmod_all, w_ada, m_w_ada, v_w_ada, 256)

    reduce_finish(state_in, owns_in, ("w_in",), "in", deltas["w_ada"])

    return (loss, grad_x[None], *[grads[n] for n in order], *[deltas[n] for n in order],
            *[new_m[n] for n in order], *[new_v[n] for n in order])
```

```python
import jax
import jax.numpy as jnp
from jax import lax
from jax.experimental import pallas as pl
from jax.experimental.pallas import tpu as pltpu

F32 = jnp.float32
BF16 = jnp.bfloat16
MESH = pl.DeviceIdType.MESH

D_MODEL = 1024
D_A = 512
D_B = 512
D_Z = 2 * D_A + D_B
N_HEADS = 4
HEAD_DIM = 128
CHUNK = 128
POOL_WINDOWS = (2, 4, 8, 16)
GROUP_DIM = 128
D_FF = 4096
N_MOD = 6
EPS = 1e-6
HALO = 16
N_CHIPS = 4
N_DEV = 8

ADAM_LR = 0.001
ADAM_B1 = 0.9
ADAM_B2 = 0.999
ADAM_EPS = 1e-08
ADAM_WD = 0.01
ADAM_STEP = 10

VMEM_LIMIT = 56 * 1024 * 1024
LANES = 128
SUBLANES = 8

_VMEM = pl.BlockSpec(memory_space=pltpu.VMEM)
_ANY = pl.BlockSpec(memory_space=pl.ANY)


def _params(n_grid_axes=1):
    return pltpu.CompilerParams(dimension_semantics=("arbitrary",) * n_grid_axes, vmem_limit_bytes=VMEM_LIMIT)


def _rows(ts, width):
    return pl.BlockSpec((ts, width), lambda i: (i, 0))


def _const(shape):
    return pl.BlockSpec(shape, lambda i: (0,) * len(shape))


def _dot(a, b):
    return jnp.dot(a, b, preferred_element_type=F32)


def _dot_nt(a, b):
    return lax.dot_general(a, b, (((1,), (1,)), ((), ())), preferred_element_type=F32)


def _dot_tn(a, b):
    return lax.dot_general(a, b, (((0,), (0,)), ((), ())), preferred_element_type=F32)


def _rowmean(v):
    return jnp.mean(v, axis=-1, keepdims=True)


def _colsum(v):
    return jnp.sum(v, axis=0, keepdims=True)


def _gelu_parts(z):
    k0 = 0.7978845608028654
    k1 = 0.044715
    z2 = z * z
    t = jnp.tanh(z * (k0 + (k0 * k1) * z2))
    u = 0.5 * t + 0.5
    g = z * u
    dg = u + (0.5 * z) * (1.0 - t * t) * (k0 + (3.0 * k0 * k1) * z2)
    return g, dg


def _tril_weights(ws_ref):
    r = lax.broadcasted_iota(jnp.int32, (CHUNK, CHUNK), 0)
    s = lax.broadcasted_iota(jnp.int32, (CHUNK, CHUNK), 1)
    mask = (s <= r).astype(F32)
    return [(ws_ref[h] * mask).astype(BF16) for h in range(N_HEADS)]


def _window_counts(first_row, n_rows):
    pos = (first_row + lax.broadcasted_iota(jnp.int32, (n_rows, 1), 0)).astype(F32)
    return pos, [1.0 / jnp.minimum(pos + 1.0, float(w)) for w in POOL_WINDOWS]


def _causal_window_sums(ext):
    out = []
    e = ext
    shift = 1
    for g in range(len(POOL_WINDOWS)):
        e = e + pltpu.roll(e, shift, 0)
        shift *= 2
        out.append(e[:, g * GROUP_DIM:(g + 1) * GROUP_DIM])
    return out


def _anticausal_window_sums(ext):
    n = ext.shape[0]
    out = []
    e = ext
    shift = 1
    for g in range(len(POOL_WINDOWS)):
        e = e + pltpu.roll(e, n - shift, 0)
        shift *= 2
        out.append(e[:, g * GROUP_DIM:(g + 1) * GROUP_DIM])
    return out


def _fwd_mix(x, mod6, n1pre, n1post, n2pre, win_g, wout_g, w_spatial, bsp_full, gain, bias, w_pool, b_pool, pool_scale, ts):
    s_len = x.shape[0]
    rs = D_MODEL // N_CHIPS

    def body(x_ref, mod_ref, g1pre_ref, g1post_ref, g2pre_ref, win_ref, wout_ref, ws_ref, bsp_ref, gain_ref,
             bias_ref, wp_ref, bp_ref, ps_ref, z_ref, y_ref, mix_ref, x1_ref, h2_ref, mixed_ref, prev_ref, wfull_ref):
        i = pl.program_id(0)
        _zero_on_first_step(prev_ref)
        _join_w_in_on_first_step(win_ref, wfull_ref)
        xv = x_ref[...]
        r = lax.rsqrt(_rowmean(xv * xv) + EPS)
        hb = ((xv * r) * (g1pre_ref[...] * (1.0 + mod_ref[1:2, :])) + mod_ref[0:1, :]).astype(BF16)
        z_ref[...] = _dot(hb, wfull_ref[...])

        wc = _tril_weights(ws_ref)
        u, _, _, _, _ = _mixer_forward_tile(z_ref[:, :2 * D_A], wc, bsp_ref, gain_ref[...], bias_ref[...], mixed_ref)
        y_ref[:, :D_A] = (u * mixed_ref[...]).astype(BF16)
        zb = z_ref[:, 2 * D_A:]
        sums = _causal_window_sums(jnp.concatenate([prev_ref[...], zb], axis=0))
        prev_ref[...] = zb[ts - HALO:, :]
        _, inv_counts = _window_counts(i * ts, ts)
        for g in range(len(POOL_WINDOWS)):
            lanes = slice(g * GROUP_DIM, (g + 1) * GROUP_DIM)
            diff = sums[g][HALO:, :] * inv_counts[g] - zb[:, lanes]
            lin = _dot(diff.astype(BF16), wp_ref[g].astype(BF16)) + bp_ref[:, lanes]
            y_ref[:, D_A + g * GROUP_DIM:D_A + (g + 1) * GROUP_DIM] = (lin * ps_ref[:, lanes]).astype(BF16)

        mix = None
        for j in range(N_CHIPS):
            part = _dot(y_ref[:, j * rs:(j + 1) * rs], wout_ref[j])
            mix = part if mix is None else mix + part
        mix_ref[...] = mix
        r2 = lax.rsqrt(_rowmean(mix * mix) + EPS)
        x1 = xv + (mix * r2) * (mod_ref[2:3, :] * g1post_ref[...])
        x1_ref[...] = x1
        r3 = lax.rsqrt(_rowmean(x1 * x1) + EPS)
        h2_ref[...] = ((x1 * r3) * (g2pre_ref[...] * (1.0 + mod_ref[4:5, :])) + mod_ref[3:4, :]).astype(BF16)

    vec = _const((1, D_MODEL))
    f32_rows = jax.ShapeDtypeStruct((s_len, D_MODEL), F32)
    bf16_rows = jax.ShapeDtypeStruct((s_len, D_MODEL), BF16)
    return pl.pallas_call(
        body, name="fwd_mix", grid=(s_len // ts,),
        in_specs=[_rows(ts, D_MODEL), _const((N_MOD, D_MODEL)), vec, vec, vec, _VMEM, _VMEM,
                  _const((N_HEADS, CHUNK, CHUNK)), _const((CHUNK, D_A)), _const((1, D_A)), _const((1, D_A)),
                  _const((N_HEADS, GROUP_DIM, GROUP_DIM)), _const((1, D_B)), _const((1, D_B))],
        out_specs=[_rows(ts, D_Z), _rows(ts, D_MODEL), _rows(ts, D_MODEL), _rows(ts, D_MODEL), _rows(ts, D_MODEL)],
        out_shape=[jax.ShapeDtypeStruct((s_len, D_Z), F32), bf16_rows, f32_rows, f32_rows, bf16_rows],
        scratch_shapes=[pltpu.VMEM((ts, D_A), F32), pltpu.VMEM((HALO, D_B), F32), pltpu.VMEM((D_MODEL, D_Z), BF16)],
        compiler_params=_params(),
    )(x, mod6, n1pre, n1post, n2pre, win_g, wout_g, w_spatial, bsp_full, gain, bias, w_pool, b_pool, pool_scale)


def _mixer_forward_tile(za, wc, bsp_ref, gain, bias, mixed_ref):
    ga, dga = _gelu_parts(za)
    u = ga[:, :D_A]
    v = ga[:, D_A:]
    mu = _rowmean(v)
    vc = v - mu
    rstd = lax.rsqrt(_rowmean(vc * vc) + EPS)
    vhat = vc * rstd
    vn = (vhat * gain + bias).astype(BF16)
    ts = za.shape[0]
    for k in range(ts // CHUNK):
        for h in range(N_HEADS):
            blk = vn[k * CHUNK:(k + 1) * CHUNK, h * HEAD_DIM:(h + 1) * HEAD_DIM]
            mixed_ref[k * CHUNK:(k + 1) * CHUNK, h * HEAD_DIM:(h + 1) * HEAD_DIM] = (
                _dot(wc[h], blk) + bsp_ref[:, h * HEAD_DIM:(h + 1) * HEAD_DIM])
    return u, vhat, rstd, vn, dga


def _fwd_fc1(h2, fc1_g, ts):
    s_len = h2.shape[0]
    cs = D_FF // N_CHIPS

    def body(h_ref, w_ref, q_ref):
        hb = h_ref[...]
        for j in range(N_CHIPS):
            p = jnp.maximum(_dot(hb, w_ref[j]), 0.0)
            q_ref[:, j * cs:(j + 1) * cs] = (p * p).astype(BF16)

    return pl.pallas_call(
        body, name="fwd_fc1", grid=(s_len // ts,),
        in_specs=[_rows(ts, D_MODEL), _VMEM],
        out_specs=_rows(ts, D_FF),
        out_shape=jax.ShapeDtypeStruct((s_len, D_FF), BF16),
        compiler_params=_params(),
    )(h2, fc1_g)


def _fwd_fc2_loss(q, x1, target, mod6, n2post, fc2_g, ts):
    s_len = q.shape[0]
    rs = D_FF // N_CHIPS

    def body(q_ref, x1_ref, t_ref, mod_ref, g_ref, w_ref, dy_ref, df_ref, loss_ref, s_ref):
        _zero_on_first_step(loss_ref, s_ref)
        gate_gain = mod_ref[5:6, :] * g_ref[...]
        f = _dot(q_ref[:, 0:rs], w_ref[0])
        for j in range(1, N_CHIPS):
            f = f + _dot(q_ref[:, j * rs:(j + 1) * rs], w_ref[j])
        r4 = lax.rsqrt(_rowmean(f * f) + EPS)
        fh = f * r4
        err = (x1_ref[...] + fh * gate_gain) - t_ref[...]
        loss_ref[...] += 0.5 * jnp.sum(_rowmean(err * err), axis=0, keepdims=True)
        dy = err * (1.0 / D_MODEL)
        dy_ref[...] = dy
        s_ref[...] += _colsum(dy * fh)
        gh = dy * gate_gain
        df_ref[...] = (r4 * (gh - fh * _rowmean(gh * fh))).astype(BF16)

    return pl.pallas_call(
        body, name="fwd_fc2_loss", grid=(s_len // ts,),
        in_specs=[_rows(ts, D_FF), _rows(ts, D_MODEL), _rows(ts, D_MODEL), _const((N_MOD, D_MODEL)),
                  _const((1, D_MODEL)), _VMEM],
        out_specs=[_rows(ts, D_MODEL), _rows(ts, D_MODEL), _const((1, 1)), _const((1, D_MODEL))],
        out_shape=[jax.ShapeDtypeStruct((s_len, D_MODEL), F32), jax.ShapeDtypeStruct((s_len, D_MODEL), BF16),
                   jax.ShapeDtypeStruct((1, 1), F32), jax.ShapeDtypeStruct((1, D_MODEL), F32)],
        compiler_params=_params(),
    )(q, x1, target, mod6, n2post, fc2_g)


def _join_w_in_on_first_step(win_ref, full_ref):
    cs = D_Z // N_CHIPS

    @pl.when(pl.program_id(0) == 0)
    def _():
        for j in range(N_CHIPS):
            full_ref[:, j * cs:(j + 1) * cs] = win_ref[j]


def _zero_on_first_step(*refs):
    @pl.when(pl.program_id(0) == 0)
    def _():
        for ref in refs:
            ref[...] = jnp.zeros_like(ref)


def _on_last_step(fn):
    pl.when(pl.program_id(0) == pl.num_programs(0) - 1)(fn)


def _store_shard_on_last_step(acc_ref, hbm_ref, sem, j):
    _on_last_step(lambda: pltpu.make_async_copy(acc_ref.at[j], hbm_ref.at[j], sem.at[j]).start())


def _wait_stores_on_last_step(*stores):
    def wait_all():
        for acc_ref, hbm_ref, sem in stores:
            for j in range(N_CHIPS):
                pltpu.make_async_copy(acc_ref.at[j], hbm_ref.at[j], sem.at[j]).wait()

    _on_last_step(wait_all)


def _bwd_fc2(df, q, fc2_g, ts):
    s_len = df.shape[0]
    cs = D_FF // N_CHIPS

    def body(df_ref, q_ref, w_ref, dp_ref, dw_hbm, dw_ref, dw_sem):
        _zero_on_first_step(dw_ref)
        dfb = df_ref[...]
        df2 = dfb * 2.0
        for j in range(N_CHIPS):
            qb = q_ref[:, j * cs:(j + 1) * cs]
            dw_ref[j] += _dot_tn(qb, dfb).reshape(2, cs // 2, D_MODEL)
            _store_shard_on_last_step(dw_ref, dw_hbm, dw_sem, j)
            dq2 = _dot_nt(df2, w_ref[j])
            dp_ref[:, j * cs:(j + 1) * cs] = (dq2 * jnp.sqrt(qb.astype(F32))).astype(BF16)
        _wait_stores_on_last_step((dw_ref, dw_hbm, dw_sem))

    dw_shape = (N_CHIPS, 2, cs // 2, D_MODEL)
    return pl.pallas_call(
        body, name="bwd_fc2", grid=(s_len // ts,),
        in_specs=[_rows(ts, D_MODEL), _rows(ts, D_FF), _VMEM],
        out_specs=[_rows(ts, D_FF), _ANY],
        out_shape=[jax.ShapeDtypeStruct((s_len, D_FF), BF16), jax.ShapeDtypeStruct(dw_shape, F32)],
        scratch_shapes=[pltpu.VMEM(dw_shape, F32), pltpu.SemaphoreType.DMA((N_CHIPS,))],
        compiler_params=_params(),
    )(df, q, fc2_g)


def _bwd_fc1_out(dp, dy, x1, mix, h2, ycat, mod6, n2pre, n1post, fc1_g, wout_g, dep, ts):
    s_len = dp.shape[0]
    cs = D_FF // N_CHIPS
    rs = D_MODEL // N_CHIPS

    def body(dp_ref, dy_ref, x1_ref, mix_ref, h2_ref, yc_ref, mod_ref, g2_ref, g1_ref, w1_ref, wo_ref, dep_ref,
             dx1_ref, dyc_ref, dshift2_ref, da2_ref, s1_ref, dw1_hbm, dwo_hbm, dw1_ref, dwo_ref, dw1_sem, dwo_sem):
        _zero_on_first_step(dshift2_ref, da2_ref, s1_ref, dw1_ref, dwo_ref)
        h2b = h2_ref[...]
        dh2 = None
        for j in range(N_CHIPS):
            dpb = dp_ref[:, j * cs:(j + 1) * cs]
            dw1_ref[j] += _dot_tn(h2b, dpb).reshape(2, D_MODEL // 2, cs)
            _store_shard_on_last_step(dw1_ref, dw1_hbm, dw1_sem, j)
            part = _dot_nt(dpb, w1_ref[j])
            dh2 = part if dh2 is None else dh2 + part
        x1 = x1_ref[...]
        r3 = lax.rsqrt(_rowmean(x1 * x1) + EPS)
        xh = x1 * r3
        a2 = g2_ref[...] * (1.0 + mod_ref[4:5, :])
        dshift2_ref[...] += _colsum(dh2)
        da2_ref[...] += _colsum(dh2 * xh)
        dxh = dh2 * a2
        dx1 = dy_ref[...] + r3 * (dxh - xh * _rowmean(dxh * xh))
        dx1_ref[...] = dx1

        mix = mix_ref[...]
        r2 = lax.rsqrt(_rowmean(mix * mix) + EPS)
        mh = mix * r2
        s1_ref[...] += _colsum(dx1 * mh)
        gh = dx1 * (mod_ref[2:3, :] * g1_ref[...])
        dmix = (r2 * (gh - mh * _rowmean(gh * mh))).astype(BF16)
        dwo_ref[...] += _dot_tn(yc_ref[...], dmix).reshape(N_CHIPS, 2, rs // 2, D_MODEL)
        for j in range(N_CHIPS):
            _store_shard_on_last_step(dwo_ref, dwo_hbm, dwo_sem, j)
            dyc_ref[:, j * rs:(j + 1) * rs] = _dot_nt(dmix, wo_ref[j])
        _wait_stores_on_last_step((dw1_ref, dw1_hbm, dw1_sem), (dwo_ref, dwo_hbm, dwo_sem))

    vec = jax.ShapeDtypeStruct((1, D_MODEL), F32)
    dw1_shape = (N_CHIPS, 2, D_MODEL // 2, cs)
    dwo_shape = (N_CHIPS, 2, rs // 2, D_MODEL)
    return pl.pallas_call(
        body, name="bwd_fc1_out", grid=(s_len // ts,),
        in_specs=[_rows(ts, D_FF), _rows(ts, D_MODEL), _rows(ts, D_MODEL), _rows(ts, D_MODEL), _rows(ts, D_MODEL),
                  _rows(ts, D_MODEL), _const((N_MOD, D_MODEL)), _const((1, D_MODEL)), _const((1, D_MODEL)), _VMEM,
                  _VMEM, _ANY],
        out_specs=[_rows(ts, D_MODEL), _rows(ts, D_MODEL)] + [_const((1, D_MODEL))] * 3 + [_ANY, _ANY],
        out_shape=[jax.ShapeDtypeStruct((s_len, D_MODEL), F32), jax.ShapeDtypeStruct((s_len, D_MODEL), F32),
                   vec, vec, vec, jax.ShapeDtypeStruct(dw1_shape, F32), jax.ShapeDtypeStruct(dwo_shape, F32)],
        scratch_shapes=[pltpu.VMEM(dw1_shape, F32), pltpu.VMEM(dwo_shape, F32), pltpu.SemaphoreType.DMA((N_CHIPS,)),
                        pltpu.SemaphoreType.DMA((N_CHIPS,))],
        compiler_params=_params(),
    )(dp, dy, x1, mix, h2, ycat, mod6, n2pre, n1post, fc1_g, wout_g, dep)


def _mixer_bwd(z, dyc, w_spatial, bsp_full, gain, bias, w_pool, b_pool, pool_scale, dep, ts):
    s_len = z.shape[0]
    nb = ts // HALO
    last = s_len // HALO - 1
    te = ts + HALO

    def body(z_ref, zprev_ref, znext_ref, dyc_ref, dynext_ref, ws_ref, bsp_ref, gain_ref, bias_ref, wp_ref, bp_ref,
             ps_ref, dep_ref, dz_ref, dws_ref, dbsp_ref, dgain_ref, dbias_ref, dwp_ref, dbp_ref, dps_ref, mixed_ref,
             dvn_ref):
        i = pl.program_id(0)

        @pl.when(i == 0)
        def _():
            for ref in (dws_ref, dbsp_ref, dgain_ref, dbias_ref, dwp_ref, dbp_ref, dps_ref):
                ref[...] = jnp.zeros_like(ref)

        wc = _tril_weights(ws_ref)
        gain = gain_ref[...]
        u, vhat, rstd, vn, dga = _mixer_forward_tile(z_ref[:, :2 * D_A], wc, bsp_ref, gain, bias_ref[...], mixed_ref)
        dya = dyc_ref[:, :D_A]
        du = dya * mixed_ref[...]
        dmixed = dya * u
        dmb = dmixed.astype(BF16)
        dm_sum = dmixed[0:CHUNK, :]
        for k in range(1, ts // CHUNK):
            dm_sum = dm_sum + dmixed[k * CHUNK:(k + 1) * CHUNK, :]
        r_idx = lax.broadcasted_iota(jnp.int32, (CHUNK, CHUNK), 0)
        s_idx = lax.broadcasted_iota(jnp.int32, (CHUNK, CHUNK), 1)
        causal = (s_idx <= r_idx).astype(F32)
        for h in range(N_HEADS):
            lanes = slice(h * HEAD_DIM, (h + 1) * HEAD_DIM)
            dbsp_ref[h] += jnp.sum(dm_sum[:, lanes], axis=1, keepdims=True)
            acc = None
            for k in range(ts // CHUNK):
                rows = slice(k * CHUNK, (k + 1) * CHUNK)
                t = _dot_nt(dmb[rows, lanes], vn[rows, lanes])
                acc = t if acc is None else acc + t
                dvn_ref[rows, lanes] = _dot_tn(wc[h], dmb[rows, lanes])
            dws_ref[h] += acc * causal
        dvn = dvn_ref[...]
        dgain_ref[...] += _colsum(dvn * vhat)
        dbias_ref[...] += _colsum(dvn)
        dvh = dvn * gain
        dv = rstd * (dvh - _rowmean(dvh) - vhat * _rowmean(dvh * vhat))
        dz_ref[:, :D_A] = (du * dga[:, :D_A]).astype(BF16)
        dz_ref[:, D_A:2 * D_A] = (dv * dga[:, D_A:]).astype(BF16)

        zb = z_ref[:, 2 * D_A:]
        prev = jnp.where(i == 0, 0.0, zprev_ref[...])
        zb_ext = jnp.concatenate([zb, znext_ref[...]], axis=0)
        sums = _causal_window_sums(jnp.concatenate([prev, zb_ext], axis=0))
        pos, inv_counts = _window_counts(i * ts, te)
        dyb_ext = jnp.concatenate([dyc_ref[:, D_A:], dynext_ref[...]], axis=0)
        dlin_ext = dyb_ext * ps_ref[...]
        dbp_ref[...] += _colsum(dlin_ext[:ts, :])
        scaled = []
        ddiffs = []
        lins = []
        for g in range(len(POOL_WINDOWS)):
            lanes = slice(g * GROUP_DIM, (g + 1) * GROUP_DIM)
            diff = (sums[g][HALO:, :] * inv_counts[g] - zb_ext[:, lanes]).astype(BF16)
            wpb = wp_ref[g].astype(BF16)
            dlb = dlin_ext[:, lanes].astype(BF16)
            lins.append(_dot(diff[:ts, :], wpb) + bp_ref[:, lanes])
            dwp_ref[g] += _dot_tn(diff[:ts, :], dlb[:ts, :])
            dd = _dot_nt(dlb, wpb)
            ddiffs.append(dd)
            scaled.append(jnp.where(pos < float(s_len), dd * inv_counts[g], 0.0))
        dps_ref[...] += _colsum(dyb_ext[:ts, :] * jnp.concatenate(lins, axis=1))
        back = _anticausal_window_sums(jnp.concatenate(scaled, axis=1))
        for g in range(len(POOL_WINDOWS)):
            dz_ref[:, 2 * D_A + g * GROUP_DIM:2 * D_A + (g + 1) * GROUP_DIM] = (
                back[g][:ts, :] - ddiffs[g][:ts, :]).astype(BF16)

    sq = jax.ShapeDtypeStruct((N_HEADS, CHUNK, CHUNK), F32)
    vec = jax.ShapeDtypeStruct((1, D_A), F32)
    return pl.pallas_call(
        body, name="mixer_bwd", grid=(s_len // ts,),
        in_specs=[_rows(ts, D_Z),
                  pl.BlockSpec((HALO, D_B), lambda i: (jnp.maximum(i * nb - 1, 0), 2)),
                  pl.BlockSpec((HALO, D_B), lambda i: (jnp.minimum((i + 1) * nb, last), 2)),
                  _rows(ts, D_MODEL),
                  pl.BlockSpec((HALO, D_B), lambda i: (jnp.minimum((i + 1) * nb, last), 1)),
                  _const((N_HEADS, CHUNK, CHUNK)), _const((CHUNK, D_A)), _const((1, D_A)), _const((1, D_A)),
                  _const((N_HEADS, GROUP_DIM, GROUP_DIM)), _const((1, D_B)), _const((1, D_B)), _ANY],
        out_specs=[_rows(ts, D_Z), _const((N_HEADS, CHUNK, CHUNK)), _const((N_HEADS, CHUNK, 1)), _const((1, D_A)),
                   _const((1, D_A)), _const((N_HEADS, GROUP_DIM, GROUP_DIM)), _const((1, D_B)), _const((1, D_B))],
        out_shape=[jax.ShapeDtypeStruct((s_len, D_Z), BF16), sq, jax.ShapeDtypeStruct((N_HEADS, CHUNK, 1), F32), vec,
                   vec, sq, vec, vec],
        scratch_shapes=[pltpu.VMEM((ts, D_A), F32), pltpu.VMEM((ts, D_A), F32)],
        compiler_params=_params(),
    )(z, z, z, dyc, dyc, w_spatial, bsp_full, gain, bias, w_pool, b_pool, pool_scale, dep)


def _bwd_in(dz, dx1, x, mod6, n1pre, win_g, dep, ts):
    s_len = x.shape[0]
    cs = D_Z // N_CHIPS

    def body(dz_ref, dx1_ref, x_ref, mod_ref, g_ref, w_ref, dep_ref, gx_ref, dshift_ref, da_ref, dw_hbm, dw_ref,
             wfull_ref, dw_sem):
        _zero_on_first_step(dshift_ref, da_ref, dw_ref)
        _join_w_in_on_first_step(w_ref, wfull_ref)
        xv = x_ref[...]
        r = lax.rsqrt(_rowmean(xv * xv) + EPS)
        xh = xv * r
        h1b = (xh * (g_ref[...] * (1.0 + mod_ref[1:2, :])) + mod_ref[0:1, :]).astype(BF16)
        dzb = dz_ref[...]
        dw = _dot_tn(h1b, dzb)
        for j in range(N_CHIPS):
            dw_ref[j] += dw[:, j * cs:(j + 1) * cs].reshape(2, D_MODEL // 2, cs)
        dh = _dot_nt(dzb, wfull_ref[...])
        a1 = g_ref[...] * (1.0 + mod_ref[1:2, :])
        dshift_ref[...] += _colsum(dh)
        da_ref[...] += _colsum(dh * xh)
        dxh = dh * a1
        gx_ref[...] = dx1_ref[...] + r * (dxh - xh * _rowmean(dxh * xh))
        for j in range(N_CHIPS):
            _store_shard_on_last_step(dw_ref, dw_hbm, dw_sem, j)
        _wait_stores_on_last_step((dw_ref, dw_hbm, dw_sem))

    vec = jax.ShapeDtypeStruct((1, D_MODEL), F32)
    dw_shape = (N_CHIPS, 2, D_MODEL // 2, cs)
    return pl.pallas_call(
        body, name="bwd_in", grid=(s_len // ts,),
        in_specs=[_rows(ts, D_Z), _rows(ts, D_MODEL), _rows(ts, D_MODEL), _const((N_MOD, D_MODEL)),
                  _const((1, D_MODEL)), _VMEM, _ANY],
        out_specs=[_rows(ts, D_MODEL), _const((1, D_MODEL)), _const((1, D_MODEL)), _ANY],
        out_shape=[jax.ShapeDtypeStruct((s_len, D_MODEL), F32), vec, vec, jax.ShapeDtypeStruct(dw_shape, F32)],
        scratch_shapes=[pltpu.VMEM(dw_shape, F32), pltpu.VMEM((D_MODEL, D_Z), BF16),
                        pltpu.SemaphoreType.DMA((N_CHIPS,))],
        compiler_params=_params(),
    )(dz, dx1, x, mod6, n1pre, win_g, dep)


def _adamw_math(w, g, m, v):
    m = ADAM_B1 * m + (1.0 - ADAM_B1) * g
    v = ADAM_B2 * v + (1.0 - ADAM_B2) * (g * g)
    m_hat = m / (1.0 - ADAM_B1 ** ADAM_STEP)
    v_hat = v / (1.0 - ADAM_B2 ** ADAM_STEP)
    delta = -ADAM_LR * (m_hat / (jnp.sqrt(v_hat) + ADAM_EPS) + ADAM_WD * w)
    return delta, m, v


def _adamw(gs, ws, ms, vs, name, steps):
    n = len(ws)

    def body(*refs):
        for i in range(n):
            g_ref, w_ref, m_ref, v_ref = refs[4 * i:4 * i + 4]
            g = g_ref[...]
            d, nm, nv = _adamw_math(w_ref[...], g, m_ref[...], v_ref[...])
            for ref, val in zip(refs[4 * n + 4 * i:4 * n + 4 * i + 4], (g, d, nm, nv)):
                ref[...] = val

    specs = [_rows(w.shape[0] // steps, w.shape[1]) for w in ws]
    out = pl.pallas_call(
        body, name=name, grid=(steps,),
        in_specs=[s for s in specs for _ in range(4)], out_specs=[s for s in specs for _ in range(4)],
        out_shape=[jax.ShapeDtypeStruct(w.shape, F32) for w in ws for _ in range(4)],
        compiler_params=_params(),
    )(*[a for quad in zip(gs, ws, ms, vs) for a in quad])
    return [out[4 * i:4 * i + 4] for i in range(n)]


def _ada_grad_adamw(sc_t, dmod_all, w, m, v, tr):
    rows, cols = w.shape

    def body(s_ref, dm_ref, w_ref, m_ref, v_ref, g_ref, d_ref, nm_ref, nv_ref):
        x, y, _ = _position()
        mine = pl.ds(pl.multiple_of((2 * x + y) * cols, LANES), cols)
        g = s_ref[:, 0:1] * dm_ref[0:1, mine]
        for b in range(1, N_DEV):
            g = g + s_ref[:, b:b + 1] * dm_ref[b:b + 1, mine]
        g_ref[...] = g
        d, nm, nv = _adamw_math(w_ref[...], g, m_ref[...], v_ref[...])
        d_ref[...] = d
        nm_ref[...] = nm
        nv_ref[...] = nv

    spec = _rows(tr, cols)
    shape = jax.ShapeDtypeStruct((rows, cols), F32)
    return pl.pallas_call(
        body, name="ada_grad_adamw", grid=(rows // tr,),
        in_specs=[_rows(tr, N_DEV), _const(dmod_all.shape), spec, spec, spec],
        out_specs=[spec] * 4, out_shape=[shape] * 4, compiler_params=_params(),
    )(sc_t, dmod_all, w, m, v)


def _mod_grads(da1, dshift1, s1, da2, dshift2, s2, mod6, n1pre, n1post, n2pre, n2post):
    def body(da1_ref, ds1_ref, s1_ref, da2_ref, ds2_ref, s2_ref, mod_ref, n1_ref, p1_ref, n2_ref, p2_ref, dmod_ref,
             dn_ref):
        dmod_ref[0:1, :] = ds1_ref[...]
        dmod_ref[1:2, :] = da1_ref[...] * n1_ref[...]
        dmod_ref[2:3, :] = s1_ref[...] * p1_ref[...]
        dmod_ref[3:4, :] = ds2_ref[...]
        dmod_ref[4:5, :] = da2_ref[...] * n2_ref[...]
        dmod_ref[5:6, :] = s2_ref[...] * p2_ref[...]
        dn_ref[0:1, :] = da1_ref[...] * (1.0 + mod_ref[1:2, :])
        dn_ref[1:2, :] = s1_ref[...] * mod_ref[2:3, :]
        dn_ref[2:3, :] = da2_ref[...] * (1.0 + mod_ref[4:5, :])
        dn_ref[3:4, :] = s2_ref[...] * mod_ref[5:6, :]

    return pl.pallas_call(
        body, name="mod_grads",
        out_shape=[jax.ShapeDtypeStruct((N_MOD, D_MODEL), F32), jax.ShapeDtypeStruct((4, D_MODEL), F32)],
    )(da1, dshift1, s1, da2, dshift2, s2, mod6, n1pre, n1post, n2pre, n2post)


def _position():
    x, y, c = lax.axis_index("x"), lax.axis_index("y"), lax.axis_index("c")
    return x, y, c


def _flip(v, bit):
    return 1 - v if bit else v


def _peer(x, y, c, k):
    return (_flip(x, k & 4), _flip(y, k & 2), _flip(c, k & 1))


def _remote(src, dst, send_sem, recv_sem, device):
    return pltpu.make_async_remote_copy(src_ref=src, dst_ref=dst, send_sem=send_sem, recv_sem=recv_sem,
                                        device_id=device, device_id_type=MESH)


def _mod_exchange(c_row, w_ada_shard, b_ada_row, ws, early):
    cs = w_ada_shard.shape[1]
    n = len(ws)

    def body(c_ref, w_hbm, b_ref, *refs):
        shards, (mod_ref, sc_ref), zones = refs[:n], refs[n:n + 2], refs[n + 2:2 * n + 2]
        rows_ref, w_ref, w_sem, send1, recv1, send2, recv2 = refs[2 * n + 2:2 * n + 9]
        wide, narrow = refs[2 * n + 9:3 * n + 9], refs[3 * n + 9:4 * n + 9]
        cast_load, cast_store = refs[4 * n + 9:]
        x, y, c = _position()
        me = 4 * x + 2 * y + c
        chip = 2 * x + y
        w_load = pltpu.make_async_copy(w_hbm, w_ref, w_sem)
        w_load.start()
        loads = [pltpu.make_async_copy(shards[i], wide[i], cast_load.at[i]) for i in range(n)]
        stores = [pltpu.make_async_copy(narrow[i], zones[i].at[chip], cast_store.at[i]) for i in range(n)]

        def cast(i):
            loads[i].wait()
            narrow[i][...] = wide[i][...].astype(BF16)
            stores[i].start()

        for i in range(early):
            loads[i].start()
        cv = c_ref[...]
        sc_ref[me] = cv * jax.nn.sigmoid(cv)
        gather = [_remote(sc_ref.at[me], sc_ref.at[me], send1.at[k - 1], recv1.at[k - 1], _peer(x, y, c, k))
                  for k in range(1, N_DEV)]
        for cp in gather:
            cp.start()
        for i in range(early):
            cast(i)
        for i in range(early, n):
            loads[i].start()
        for k in range(1, N_DEV):
            px, py, pc = _peer(x, y, c, k)
            src = 4 * px + 2 * py + pc
            _remote(sc_ref.at[src], sc_ref.at[src], send1.at[k - 1], recv1.at[k - 1], (px, py, pc)).wait_recv()
        for cp in gather:
            cp.wait_send()
        sc_all = jnp.concatenate([sc_ref[b] for b in range(N_DEV)], axis=0)
        w_load.wait()
        part = _dot(sc_all.astype(BF16), w_ref[...].astype(BF16))
        part = part + b_ref[:, pl.ds(pl.multiple_of(chip * cs, LANES), cs)]
        for b in range(N_DEV):
            rows_ref[b] = part[b:b + 1, :]
        mod_ref[chip] = rows_ref[me]
        hand = []
        for k in (2, 4, 6):
            px, py, _ = _peer(x, y, c, k)
            hand.append(_remote(rows_ref.at[4 * px + 2 * py + c], mod_ref.at[chip], send2.at[k // 2 - 1],
                                recv2.at[k // 2 - 1], (px, py, c)))
        for cp in hand:
            cp.start()
        for i in range(early, n):
            cast(i)
        for k in (2, 4, 6):
            px, py, _ = _peer(x, y, c, k)
            pchip = 2 * px + py
            _remote(rows_ref.at[me], mod_ref.at[pchip], send2.at[k // 2 - 1], recv2.at[k // 2 - 1],
                    (px, py, c)).wait_recv()
        for cp in hand:
            cp.wait_send()
        for cp in stores:
            cp.wait()

    out = pl.pallas_call(
        body, name="mod_exchange",
        in_specs=[_VMEM, _ANY, _VMEM] + [_ANY] * n, out_specs=[_VMEM, _VMEM] + [_ANY] * n,
        out_shape=[jax.ShapeDtypeStruct((N_CHIPS, 1, cs), F32), jax.ShapeDtypeStruct((N_DEV, 1, D_MODEL), F32)]
        + [jax.ShapeDtypeStruct((N_CHIPS,) + w.shape, BF16) for w in ws],
        scratch_shapes=[pltpu.VMEM((N_DEV, 1, cs), F32), pltpu.VMEM(w_ada_shard.shape, F32), pltpu.SemaphoreType.DMA,
                        pltpu.SemaphoreType.DMA((N_DEV - 1,)),
                        pltpu.SemaphoreType.DMA((N_DEV - 1,)), pltpu.SemaphoreType.DMA((N_CHIPS - 1,)),
                        pltpu.SemaphoreType.DMA((N_CHIPS - 1,))]
        + [pltpu.VMEM(w.shape, F32) for w in ws] + [pltpu.VMEM(w.shape, BF16) for w in ws]
        + [pltpu.SemaphoreType.DMA((n,))] * 2,
        compiler_params=pltpu.CompilerParams(vmem_limit_bytes=VMEM_LIMIT),
    )(c_row, w_ada_shard, b_ada_row, *ws)
    return out[0], out[1], list(out[2:])


_HBM = pl.BlockSpec(memory_space=pltpu.HBM)
_SEM = pl.BlockSpec(memory_space=pltpu.SEMAPHORE)
_EFFECT = pltpu.SideEffectType.DATAFLOW_SIDE_EFFECTING
_CHIP_HOPS = (2, 4, 6)


def _in_hbm(a):
    return pltpu.with_memory_space_constraint(a, pltpu.HBM)


def _sems3():
    return pltpu.SemaphoreType.DMA((len(_CHIP_HOPS),))


def _ag_start(lands, after, name):
    n = len(lands)

    def body(*refs):
        zones = refs[:n]
        sends, recvs = refs[n + 1:2 * n + 1], refs[2 * n + 1:3 * n + 1]
        x, y, c = _position()
        chip = 2 * x + y
        for i in range(n):
            half = zones[i].shape[1] // 2
            mine = zones[i].at[chip, pl.ds(c * half, half)]
            for s, k in enumerate(_CHIP_HOPS):
                px, py, _ = _peer(x, y, c, k)
                _remote(mine, mine, sends[i].at[s], recvs[i].at[s], (px, py, c)).start()

    out = pl.pallas_call(
        body, name=name,
        in_specs=[_HBM] * n + [_ANY],
        out_specs=[_SEM] * (2 * n) + [_HBM] * n,
        out_shape=[_sems3()] * (2 * n) + [pltpu.HBM(z.shape, BF16) for z in lands],
        input_output_aliases={i: 2 * n + i for i in range(n)},
        compiler_params=pltpu.CompilerParams(has_side_effects=_EFFECT),
    )(*[_in_hbm(z) for z in lands], after)
    return [(out[2 * n + i], out[i], out[n + i]) for i in range(n)]


def _ag_pass(group, after, name):
    n = len(group)

    def body(*refs):
        zones = refs[:n]
        sends, recvs = refs[n:2 * n], refs[2 * n:3 * n]
        fsends, frecvs = refs[4 * n + 1:5 * n + 1], refs[5 * n + 1:6 * n + 1]
        x, y, c = _position()
        chip = 2 * x + y
        for i in range(n):
            half = zones[i].shape[1] // 2
            rows = pl.ds(c * half, half)
            for s, k in enumerate(_CHIP_HOPS):
                px, py, _ = _peer(x, y, c, k)
                landed = zones[i].at[2 * px + py, rows]
                _remote(landed, landed, sends[i].at[s], recvs[i].at[s], (px, py, c)).wait_recv()
                _remote(landed, landed, fsends[i].at[s], frecvs[i].at[s], (x, y, 1 - c)).start()
        for i in range(n):
            half = zones[i].shape[1] // 2
            mine = zones[i].at[chip, pl.ds(c * half, half)]
            for s, k in enumerate(_CHIP_HOPS):
                px, py, _ = _peer(x, y, c, k)
                _remote(mine, mine, sends[i].at[s], recvs[i].at[s], (px, py, c)).wait_send()

    out = pl.pallas_call(
        body, name=name,
        in_specs=[_HBM] * n + [_SEM] * (2 * n) + [_ANY],
        out_specs=[_HBM] * n + [_SEM] * (2 * n),
        out_shape=[pltpu.HBM(g[0].shape, BF16) for g in group] + [_sems3()] * (2 * n),
        input_output_aliases={i: i for i in range(n)},
        compiler_params=pltpu.CompilerParams(has_side_effects=_EFFECT),
    )(*[g[0] for g in group], *[g[1] for g in group], *[g[2] for g in group], after)
    return [(out[i], out[n + i], out[2 * n + i]) for i in range(n)]


def _ag_done(group, name):
    n = len(group)

    def body(*refs):
        lands = refs[:n]
        fsends, frecvs = refs[n:2 * n], refs[2 * n:3 * n]
        x, y, c = _position()
        for i in range(n):
            half = lands[i].shape[1] // 2
            for s, k in enumerate(_CHIP_HOPS):
                px, py, _ = _peer(x, y, c, k)
                sent = lands[i].at[2 * px + py, pl.ds(c * half, half)]
                got = lands[i].at[2 * px + py, pl.ds((1 - c) * half, half)]
                cp = _remote(sent, got, fsends[i].at[s], frecvs[i].at[s], (x, y, 1 - c))
                cp.wait_recv()
                cp.wait_send()

    out = pl.pallas_call(
        body, name=name,
        in_specs=[_HBM] * n + [_SEM] * (2 * n),
        out_specs=[_HBM] * n,
        out_shape=[pltpu.HBM(g[0].shape, BF16) for g in group],
        input_output_aliases={i: i for i in range(n)},
        compiler_params=pltpu.CompilerParams(has_side_effects=_EFFECT),
    )(*[g[0] for g in group], *[g[1] for g in group], *[g[2] for g in group])
    return list(out)


def _small_spread_wait(slots, sends, recvs, afters):
    def body(z_ref, sends, recvs, *rest):
        x, y, c = _position()
        mine = z_ref.at[2 * x + y]
        for s, k in enumerate(_CHIP_HOPS):
            px, py, _ = _peer(x, y, c, k)
            cp = _remote(mine, z_ref.at[2 * px + py], sends.at[s], recvs.at[s], (px, py, c))
            cp.wait_recv()
            cp.wait_send()

    return pl.pallas_call(
        body, name="small_spread_wait",
        in_specs=[_HBM, _SEM, _SEM] + [_ANY] * len(afters), out_specs=_HBM, out_shape=pltpu.HBM(slots.shape, F32),
        input_output_aliases={0: 0},
        compiler_params=pltpu.CompilerParams(has_side_effects=_EFFECT),
    )(slots, sends, recvs, *afters)


def _sibling_sum(pgs, name, small=None):
    n = len(pgs)
    k = 0 if small is None else 1
    units = [(i, j) for i in range(n) for j in range(N_CHIPS)]

    def body(*refs):
        refs = list(refs)
        take = lambda count: [refs.pop(0) for _ in range(count)]
        ins, small_in = take(n), take(k)
        qbs, owns, slots_out = take(n), take(n), take(k)
        mine, other, stage, got = take(n), take(n), take(n), take(n)
        load_a, load_b, send, recv, keep_wire, keep_own = take(6)
        x, y, c = _position()
        chip = 2 * x + y
        if k:
            sib_ref, pair_send, pair_recv, keep_small = take(4)
            pair = _remote(small_in[0], sib_ref, pair_send, pair_recv, (x, y, 1 - c))
            pair.start()
        loads_a = [pltpu.make_async_copy(ins[i].at[j, 1 - c], other[i].at[j], load_a.at[u])
                   for u, (i, j) in enumerate(units)]
        loads_b = [pltpu.make_async_copy(ins[i].at[j, c], mine[i].at[j], load_b.at[u])
                   for u, (i, j) in enumerate(units)]
        for cp in loads_a + loads_b:
            cp.start()
        sent = []
        for u, (i, j) in enumerate(units):
            loads_a[u].wait()
            stage[i][j] = other[i][j].astype(BF16)
            cp = _remote(stage[i].at[j], got[i].at[j], send.at[u], recv.at[u], (x, y, 1 - c))
            cp.start()
            sent.append(cp)
        stores = []
        for u, (i, j) in enumerate(units):
            loads_b[u].wait()
            sent[u].wait_recv()
            q = mine[i][j] + got[i][j].astype(F32)
            mine[i][j] = q
            got[i][j] = q.astype(BF16)
            stores.append(pltpu.make_async_copy(got[i].at[j], qbs[i].at[j], keep_wire.at[u]))
            stores[-1].start()
        for i in range(n):
            stores.append(pltpu.make_async_copy(mine[i].at[chip], owns[i], keep_own.at[i]))
            stores[-1].start()
        if k:
            pair.wait()
            sib_ref[...] = small_in[0][...] + sib_ref[...]
            stores.append(pltpu.make_async_copy(sib_ref, slots_out[0].at[chip], keep_small))
            stores[-1].start()
        for cp in sent:
            cp.wait_send()
        for cp in stores:
            cp.wait()

    wire = [(N_CHIPS,) + p.shape[2:] for p in pgs]
    extra_out, extra_scratch = [], []
    if k:
        extra_out = [jax.ShapeDtypeStruct((N_CHIPS,) + small.shape, F32)]
        extra_scratch = [pltpu.VMEM(small.shape, F32)] + [pltpu.SemaphoreType.DMA] * 3
    out = pl.pallas_call(
        body, name=name, in_specs=[_ANY] * n + [_VMEM] * k, out_specs=[_ANY] * (2 * n + k),
        out_shape=[jax.ShapeDtypeStruct(w, BF16) for w in wire] + [jax.ShapeDtypeStruct(w[1:], F32) for w in wire]
        + extra_out,
        scratch_shapes=[pltpu.VMEM(w, F32) for w in wire] * 2 + [pltpu.VMEM(w, BF16) for w in wire] * 2
        + [pltpu.SemaphoreType.DMA((len(units),))] * 5 + [pltpu.SemaphoreType.DMA((n,))] + extra_scratch,
        compiler_params=pltpu.CompilerParams(vmem_limit_bytes=VMEM_LIMIT),
    )(*pgs, *([small] if k else []))
    return list(out[:n]), list(out[n:2 * n]), list(out[2 * n:])


def _rs_start(qbs, name, small=None):
    n = len(qbs)
    k = 0 if small is None else 1

    def body(*refs):
        outs, inboxes, slots = refs[:n], refs[n:2 * n], refs[2 * n:2 * n + k]
        sems = refs[2 * n + k:4 * n + 3 * k]
        sends, recvs = sems[:n], sems[n:2 * n]
        x, y, c = _position()
        chip = 2 * x + y
        if k:
            mine = slots[0].at[chip]
            for s, hop in enumerate(_CHIP_HOPS):
                px, py, _ = _peer(x, y, c, hop)
                _remote(mine, mine, sems[2 * n].at[s], sems[2 * n + 1].at[s], (px, py, c)).start()
        for i in range(n):
            for s, hop in enumerate(_CHIP_HOPS):
                px, py, _ = _peer(x, y, c, hop)
                _remote(outs[i].at[2 * px + py], inboxes[i].at[chip], sends[i].at[s], recvs[i].at[s], (px, py, c)).start()

    inboxes = [_in_hbm(lax.empty(q.shape, BF16)) for q in qbs]
    n_sems = 2 * n + 2 * k
    out = pl.pallas_call(
        body, name=name,
        in_specs=[_HBM] * (2 * n + k),
        out_specs=[_SEM] * n_sems + [_HBM] * (2 * n + k),
        out_shape=[_sems3()] * n_sems + [pltpu.HBM(q.shape, BF16) for q in qbs] * 2
        + ([pltpu.HBM(small.shape, F32)] if k else []),
        input_output_aliases={i: n_sems + i for i in range(2 * n + k)},
        compiler_params=pltpu.CompilerParams(has_side_effects=_EFFECT),
    )(*[_in_hbm(q) for q in qbs], *inboxes, *([_in_hbm(small)] if k else []))
    states = [(out[n_sems + i], out[n_sems + n + i], out[i], out[n + i]) for i in range(n)]
    return (states, (out[n_sems + 2 * n], out[2 * n], out[2 * n + 1])) if k else states


def _rs_wait(group, after, name):
    n = len(group)

    def body(*refs):
        outs, inboxes = refs[:n], refs[n:2 * n]
        sends, recvs = refs[2 * n:3 * n], refs[3 * n:4 * n]
        x, y, c = _position()
        for i in range(n):
            for s, k in enumerate(_CHIP_HOPS):
                px, py, _ = _peer(x, y, c, k)
                slot = 2 * px + py
                cp = _remote(outs[i].at[slot], inboxes[i].at[slot], sends[i].at[s], recvs[i].at[s], (px, py, c))
                cp.wait_recv()
                cp.wait_send()

    out = pl.pallas_call(
        body, name=name,
        in_specs=[_HBM] * (2 * n) + [_SEM] * (2 * n) + [_ANY],
        out_specs=[_HBM] * n,
        out_shape=[pltpu.HBM(g[1].shape, BF16) for g in group],
        input_output_aliases={n + i: i for i in range(n)},
        compiler_params=pltpu.CompilerParams(has_side_effects=_EFFECT),
    )(*[g[0] for g in group], *[g[1] for g in group], *[g[2] for g in group], *[g[3] for g in group], after)
    return list(out)


def _final_share(inboxes, owns, name):
    n = len(inboxes)
    units = [(i, s) for i in range(n) for s in range(len(_CHIP_HOPS))]

    def body(*refs):
        ins, mine, outs, landed, half = (refs[k * n:(k + 1) * n] for k in range(5))
        load, load_own, keep, send, recv = refs[5 * n:]
        x, y, c = _position()
        loads = []
        for u, (i, s) in enumerate(units):
            px, py, _ = _peer(x, y, c, _CHIP_HOPS[s])
            loads.append(pltpu.make_async_copy(ins[i].at[2 * px + py], landed[i].at[s], load.at[u]))
        loads_own = [pltpu.make_async_copy(mine[i], half[i], load_own.at[i]) for i in range(n)]
        for cp in loads + loads_own:
            cp.start()
        copies = []
        for i in range(n):
            for s in range(len(_CHIP_HOPS)):
                loads[len(_CHIP_HOPS) * i + s].wait()
            loads_own[i].wait()
            total = (landed[i][0].astype(F32) + landed[i][1].astype(F32)) + landed[i][2].astype(F32)
            half[i][...] = total + half[i][...]
            copies.append(pltpu.make_async_copy(half[i], outs[i].at[c], keep.at[i]))
            copies.append(_remote(half[i], outs[i].at[c], send.at[i], recv.at[i], (x, y, 1 - c)))
            for cp in copies[-2:]:
                cp.start()
        for i in range(n):
            theirs = outs[i].at[1 - c]
            _remote(theirs, theirs, send.at[i], recv.at[i], (x, y, 1 - c)).wait_recv()
        for i in range(n):
            copies[2 * i].wait()
            copies[2 * i + 1].wait_send()

    return pl.pallas_call(
        body, name=name, in_specs=[_ANY] * (2 * n), out_specs=[_ANY] * n,
        out_shape=[jax.ShapeDtypeStruct((2,) + o.shape, F32) for o in owns],
        scratch_shapes=[pltpu.VMEM((len(_CHIP_HOPS),) + o.shape, BF16) for o in owns]
        + [pltpu.VMEM(o.shape, F32) for o in owns]
        + [pltpu.SemaphoreType.DMA((len(units),))] + [pltpu.SemaphoreType.DMA((n,))] * 4,
        compiler_params=pltpu.CompilerParams(vmem_limit_bytes=VMEM_LIMIT),
    )(*inboxes, *owns)


_SMALL = (("b_ada", N_MOD * D_MODEL), ("norm1_pre", D_MODEL), ("norm1_post", D_MODEL), ("norm2_pre", D_MODEL),
          ("norm2_post", D_MODEL), ("w_spatial", N_HEADS * CHUNK * CHUNK), ("b_spatial", N_HEADS * CHUNK),
          ("ln_v_gain", D_A), ("ln_v_bias", D_A), ("w_pool", N_HEADS * GROUP_DIM * GROUP_DIM),
          ("b_pool", D_B), ("pool_scale", D_B))
_MOD_ROWS = N_MOD * D_MODEL // LANES


def _packed_rows(size):
    return -(-(size // LANES) // SUBLANES) * SUBLANES


def _pack(parts):
    out = []
    for name, size in _SMALL:
        a = parts[name].reshape(size // LANES, LANES)
        pad = _packed_rows(size) - a.shape[0]
        out.append(jnp.pad(a, ((0, pad), (0, 0))) if pad else a)
    return out


def _small_adamw(slots, ws, ms, vs):
    n = len(_SMALL)
    head = N_DEV * _MOD_ROWS

    def body(*refs):
        s_ref, w, m, v = refs[0], refs[1:1 + n], refs[1 + n:1 + 2 * n], refs[1 + 2 * n:1 + 3 * n]
        outs = refs[1 + 3 * n:1 + 7 * n]
        dmod_ref, loss_ref, t_ref = refs[1 + 7 * n:]
        t_ref[...] = ((s_ref[0] + s_ref[1]) + s_ref[2]) + s_ref[3]
        dmod_ref[...] = t_ref[0:head, :]
        loss_ref[...] = t_ref[t_ref.shape[0] - SUBLANES:t_ref.shape[0] - SUBLANES + 1, 0:1]
        row = head
        for i, (_, size) in enumerate(_SMALL):
            if i == 0:
                g = t_ref[0:_MOD_ROWS, :]
                for b in range(1, N_DEV):
                    g = g + t_ref[b * _MOD_ROWS:(b + 1) * _MOD_ROWS, :]
            else:
                g = t_ref[row:row + size // LANES, :]
                row += _packed_rows(size)
            d, nm, nv = _adamw_math(w[i][...], g, m[i][...], v[i][...])
            for ref, val in zip(outs[4 * i:4 * i + 4], (g, d, nm, nv)):
                ref[...] = val

    each = [jax.ShapeDtypeStruct((size // LANES, LANES), F32) for _, size in _SMALL for _ in range(4)]
    out = pl.pallas_call(
        body, name="small_adamw",
        out_shape=each + [jax.ShapeDtypeStruct((head, LANES), F32), jax.ShapeDtypeStruct((1, 1), F32)],
        scratch_shapes=[pltpu.VMEM(slots.shape[1:], F32)],
        compiler_params=pltpu.CompilerParams(vmem_limit_bytes=VMEM_LIMIT),
    )(slots, *ws, *ms, *vs)
    return [out[4 * i:4 * i + 4] for i in range(n)], out[4 * n], out[4 * n + 1]


def kernel(x, c, w_ada, b_ada, norm1_pre, norm1_post, w_in, w_spatial, b_spatial, ln_v_gain, ln_v_bias, w_pool, b_pool, pool_scale, w_out, norm2_pre, norm2_post, w_fc1, w_fc2, loss_target, m_w_ada, m_b_ada, m_norm1_pre, m_norm1_post, m_w_in, m_w_spatial, m_b_spatial, m_ln_v_gain, m_ln_v_bias, m_w_pool, m_b_pool, m_pool_scale, m_w_out, m_norm2_pre, m_norm2_post, m_w_fc1, m_w_fc2, v_w_ada, v_b_ada, v_norm1_pre, v_norm1_post, v_w_in, v_w_spatial, v_b_spatial, v_ln_v_gain, v_ln_v_bias, v_w_pool, v_b_pool, v_pool_scale, v_w_out, v_norm2_pre, v_norm2_post, v_w_fc1, v_w_fc2):
    weights = dict(w_ada=w_ada, b_ada=b_ada, norm1_pre=norm1_pre, norm1_post=norm1_post, w_in=w_in,
                   w_spatial=w_spatial, b_spatial=b_spatial, ln_v_gain=ln_v_gain, ln_v_bias=ln_v_bias, w_pool=w_pool,
                   b_pool=b_pool, pool_scale=pool_scale, w_out=w_out, norm2_pre=norm2_pre, norm2_post=norm2_post,
                   w_fc1=w_fc1, w_fc2=w_fc2)
    m_old = dict(w_ada=m_w_ada, b_ada=m_b_ada, norm1_pre=m_norm1_pre, norm1_post=m_norm1_post, w_in=m_w_in,
                 w_spatial=m_w_spatial, b_spatial=m_b_spatial, ln_v_gain=m_ln_v_gain, ln_v_bias=m_ln_v_bias,
                 w_pool=m_w_pool, b_pool=m_b_pool, pool_scale=m_pool_scale, w_out=m_w_out, norm2_pre=m_norm2_pre,
                 norm2_post=m_norm2_post, w_fc1=m_w_fc1, w_fc2=m_w_fc2)
    v_old = dict(w_ada=v_w_ada, b_ada=v_b_ada, norm1_pre=v_norm1_pre, norm1_post=v_norm1_post, w_in=v_w_in,
                 w_spatial=v_w_spatial, b_spatial=v_b_spatial, ln_v_gain=v_ln_v_gain, ln_v_bias=v_ln_v_bias,
                 w_pool=v_w_pool, b_pool=v_b_pool, pool_scale=v_pool_scale, w_out=v_w_out, norm2_pre=v_norm2_pre,
                 norm2_post=v_norm2_post, w_fc1=v_w_fc1, w_fc2=v_w_fc2)
    order = ("w_ada", "b_ada", "norm1_pre", "norm1_post", "w_in", "w_spatial", "b_spatial", "ln_v_gain", "ln_v_bias",
             "w_pool", "b_pool", "pool_scale", "w_out", "norm2_pre", "norm2_post", "w_fc1", "w_fc2")
    mx, my, mc = _position()
    me = 4 * mx + 2 * my + mc
    chip = 2 * mx + my
    row = lambda a: a.reshape(1, -1)

    xs, target = x[0], loss_target[0]
    n1pre, n1post, n2pre, n2post = row(norm1_pre), row(norm1_post), row(norm2_pre), row(norm2_post)
    mixer = (w_spatial, jnp.repeat(b_spatial.T, HEAD_DIM, axis=1), row(ln_v_gain), row(ln_v_bias), w_pool,
             row(b_pool), row(pool_scale))
    ts_big, ts_mid = 512, 256

    mod4, sc_all, lands = _mod_exchange(c, w_ada, row(b_ada), [w_in, w_out, w_fc1, w_fc2], 2)
    mod6 = mod4.reshape(N_MOD, D_MODEL)
    ag = _ag_start(lands, mod4, "ag_start")

    win_g, wout_g = _ag_done(_ag_pass([ag[0], ag[1]], ag[2][0], "ag_pass_mix"), "ag_done_mix")
    z, ycat, mix, x1, h2 = _fwd_mix(xs, mod6, n1pre, n1post, n2pre, win_g, wout_g, *mixer, ts_big)
    (fc1_g,) = _ag_done(_ag_pass([ag[2]], h2, "ag_pass_fc1"), "ag_done_fc1")
    q = _fwd_fc1(h2, fc1_g, ts_big)
    (fc2_g,) = _ag_done(_ag_pass([ag[3]], q, "ag_pass_fc2"), "ag_done_fc2")
    dy, df, loss, s2 = _fwd_fc2_loss(q, x1, target, mod6, n2post, fc2_g, ts_big)

    def reduce_start(partials, tag, small=None):
        wire, owns, slots = _sibling_sum(partials, "sibling_sum_" + tag, small)
        return _rs_start(wire, "rs_start_" + tag), owns, slots

    def reduce_finish(state, owns, names, tag, dep):
        inboxes = _rs_wait(state, dep, "rs_wait_" + tag)
        shards = _final_share(inboxes, owns, "final_share_" + tag)
        updates = _adamw([g.reshape(weights[n].shape) for n, g in zip(names, shards)], [weights[n] for n in names],
                         [m_old[n] for n in names], [v_old[n] for n in names], "adamw_" + tag, 4)
        for n, (g, d, nm, nv) in zip(names, updates):
            grads[n], deltas[n], new_m[n], new_v[n] = g, d, nm, nv

    grads, deltas, new_m, new_v = {}, {}, {}, {}
    dp, g_fc2 = _bwd_fc2(df, q, fc2_g, ts_big)
    state_fc2, owns_fc2, _ = reduce_start([g_fc2], "fc2")
    dx1, dyc, dshift2, da2, s1, g_fc1, g_out = _bwd_fc1_out(
        dp, dy, x1, mix, h2, ycat, mod6, n2pre, n1post, fc1_g, wout_g, state_fc2[0][0], ts_mid)
    state_mid, owns_mid, _ = reduce_start([g_fc1, g_out], "mid")
    dz, dws, dbsp, dgain, dbias, dwp, dbp, dps = _mixer_bwd(z, dyc, *mixer, state_mid[0][0], ts_big)
    grad_x, dshift1, da1, g_in = _bwd_in(dz, dx1, xs, mod6, n1pre, win_g, state_mid[0][0], ts_big)
    dmod6, dnorms = _mod_grads(da1, dshift1, s1, da2, dshift2, s2, mod6, n1pre, n1post, n2pre, n2post)

    parts = dict(b_ada=dmod6, norm1_pre=dnorms[0], norm1_post=dnorms[1], norm2_pre=dnorms[2], norm2_post=dnorms[3],
                 w_spatial=dws, b_spatial=dbsp, ln_v_gain=dgain, ln_v_bias=dbias, w_pool=dwp, b_pool=dbp,
                 pool_scale=dps)
    pieces = _pack(parts)
    slots = lax.dynamic_update_slice(jnp.zeros((N_DEV * _MOD_ROWS, LANES), F32), pieces[0], (me * _MOD_ROWS, 0))
    loss_tile = jnp.pad(loss, ((0, SUBLANES - 1), (0, LANES - 1)))
    wire_in, owns_in, pair_sum = _sibling_sum([g_in], "sibling_sum_in",
                                              jnp.concatenate([slots] + pieces[1:] + [loss_tile], axis=0))
    state_in, spread = _rs_start(wire_in, "rs_start_in", pair_sum[0])
    reduce_finish(state_fc2 + state_mid, owns_fc2 + owns_mid, ("w_fc2", "w_fc1", "w_out"), "mlp", state_in[0][0])
    flat = lambda d: [d[n].reshape(size // LANES, LANES) for n, size in _SMALL]
    small_out, dmod_all, loss = _small_adamw(
        _small_spread_wait(*spread, [deltas[n] for n in ("w_fc2", "w_fc1", "w_out")]), flat(weights), flat(m_old),
        flat(v_old))
    loss = loss.reshape(())
    for (n, _), (g, d, nm, nv) in zip(_SMALL, small_out):
        shape = weights[n].shape
        grads[n], deltas[n], new_m[n], new_v[n] = g.reshape(shape), d.reshape(shape), nm.reshape(shape), nv.reshape(shape)

    sc_t = sc_all.reshape(N_DEV, D_MODEL).T
    grads["w_ada"], deltas["w_ada"], new_m["w_ada"], new_v["w_ada"] = _ada_grad_adamw(
        sc_t, dmod_all.reshape(N_DEV, N_MOD * D_MODEL), w_ada, m_w_ada, v_w_ada, 256)

    reduce_finish(state_in, owns_in, ("w_in",), "in", deltas["w_ada"])

    return (loss, grad_x[None], *[grads[n] for n in order], *[deltas[n] for n in order],
            *[new_m[n] for n in order], *[new_v[n] for n in order])
```

```python
import jax
import jax.numpy as jnp
from jax import lax
from jax.experimental import pallas as pl
from jax.experimental.pallas import tpu as pltpu

F32 = jnp.float32
BF16 = jnp.bfloat16
MESH = pl.DeviceIdType.MESH

D_MODEL = 1024
D_A = 512
D_B = 512
D_Z = 2 * D_A + D_B
N_HEADS = 4
HEAD_DIM = 128
CHUNK = 128
POOL_WINDOWS = (2, 4, 8, 16)
GROUP_DIM = 128
D_FF = 4096
N_MOD = 6
EPS = 1e-6
HALO = 16
N_CHIPS = 4
N_DEV = 8

ADAM_LR = 0.001
ADAM_B1 = 0.9
ADAM_B2 = 0.999
ADAM_EPS = 1e-08
ADAM_WD = 0.01
ADAM_STEP = 10

VMEM_LIMIT = 56 * 1024 * 1024
LANES = 128
SUBLANES = 8

_VMEM = pl.BlockSpec(memory_space=pltpu.VMEM)
_ANY = pl.BlockSpec(memory_space=pl.ANY)


def _params(n_grid_axes=1):
    return pltpu.CompilerParams(dimension_semantics=("arbitrary",) * n_grid_axes, vmem_limit_bytes=VMEM_LIMIT)


def _rows(ts, width):
    return pl.BlockSpec((ts, width), lambda i: (i, 0))


def _const(shape):
    return pl.BlockSpec(shape, lambda i: (0,) * len(shape))


def _dot(a, b):
    return jnp.dot(a, b, preferred_element_type=F32)


def _dot_nt(a, b):
    return lax.dot_general(a, b, (((1,), (1,)), ((), ())), preferred_element_type=F32)


def _dot_tn(a, b):
    return lax.dot_general(a, b, (((0,), (0,)), ((), ())), preferred_element_type=F32)


def _rowmean(v):
    return jnp.mean(v, axis=-1, keepdims=True)


def _colsum(v):
    return jnp.sum(v, axis=0, keepdims=True)


def _gelu_parts(z):
    k0 = 0.7978845608028654
    k1 = 0.044715
    z2 = z * z
    t = jnp.tanh(z * (k0 + (k0 * k1) * z2))
    u = 0.5 * t + 0.5
    g = z * u
    dg = u + (0.5 * z) * (1.0 - t * t) * (k0 + (3.0 * k0 * k1) * z2)
    return g, dg


def _tril_weights(ws_ref):
    r = lax.broadcasted_iota(jnp.int32, (CHUNK, CHUNK), 0)
    s = lax.broadcasted_iota(jnp.int32, (CHUNK, CHUNK), 1)
    mask = (s <= r).astype(F32)
    return [(ws_ref[h] * mask).astype(BF16) for h in range(N_HEADS)]


def _window_counts(first_row, n_rows):
    pos = (first_row + lax.broadcasted_iota(jnp.int32, (n_rows, 1), 0)).astype(F32)
    return pos, [1.0 / jnp.minimum(pos + 1.0, float(w)) for w in POOL_WINDOWS]


def _causal_window_sums(ext):
    out = []
    e = ext
    shift = 1
    for g in range(len(POOL_WINDOWS)):
        e = e + pltpu.roll(e, shift, 0)
        shift *= 2
        out.append(e[:, g * GROUP_DIM:(g + 1) * GROUP_DIM])
    return out


def _anticausal_window_sums(ext):
    n = ext.shape[0]
    out = []
    e = ext
    shift = 1
    for g in range(len(POOL_WINDOWS)):
        e = e + pltpu.roll(e, n - shift, 0)
        shift *= 2
        out.append(e[:, g * GROUP_DIM:(g + 1) * GROUP_DIM])
    return out


def _fwd_mix(x, mod6, n1pre, n1post, n2pre, win_g, wout_g, w_spatial, bsp_full, gain, bias, w_pool, b_pool, pool_scale, ts):
    s_len = x.shape[0]
    rs = D_MODEL // N_CHIPS

    def body(x_ref, mod_ref, g1pre_ref, g1post_ref, g2pre_ref, win_ref, wout_ref, ws_ref, bsp_ref, gain_ref,
             bias_ref, wp_ref, bp_ref, ps_ref, z_ref, y_ref, mix_ref, x1_ref, h2_ref, mixed_ref, prev_ref, wfull_ref):
        i = pl.program_id(0)
        _zero_on_first_step(prev_ref)
        _join_w_in_on_first_step(win_ref, wfull_ref)
        xv = x_ref[...]
        r = lax.rsqrt(_rowmean(xv * xv) + EPS)
        hb = ((xv * r) * (g1pre_ref[...] * (1.0 + mod_ref[1:2, :])) + mod_ref[0:1, :]).astype(BF16)
        z_ref[...] = _dot(hb, wfull_ref[...])

        wc = _tril_weights(ws_ref)
        u, _, _, _, _ = _mixer_forward_tile(z_ref[:, :2 * D_A], wc, bsp_ref, gain_ref[...], bias_ref[...], mixed_ref)
        y_ref[:, :D_A] = (u * mixed_ref[...]).astype(BF16)
        zb = z_ref[:, 2 * D_A:]
        sums = _causal_window_sums(jnp.concatenate([prev_ref[...], zb], axis=0))
        prev_ref[...] = zb[ts - HALO:, :]
        _, inv_counts = _window_counts(i * ts, ts)
        for g in range(len(POOL_WINDOWS)):
            lanes = slice(g * GROUP_DIM, (g + 1) * GROUP_DIM)
            diff = sums[g][HALO:, :] * inv_counts[g] - zb[:, lanes]
            lin = _dot(diff.astype(BF16), wp_ref[g].astype(BF16)) + bp_ref[:, lanes]
            y_ref[:, D_A + g * GROUP_DIM:D_A + (g + 1) * GROUP_DIM] = (lin * ps_ref[:, lanes]).astype(BF16)

        mix = None
        for j in range(N_CHIPS):
            part = _dot(y_ref[:, j * rs:(j + 1) * rs], wout_ref[j])
            mix = part if mix is None else mix + part
        mix_ref[...] = mix
        r2 = lax.rsqrt(_rowmean(mix * mix) + EPS)
        x1 = xv + (mix * r2) * (mod_ref[2:3, :] * g1post_ref[...])
        x1_ref[...] = x1
        r3 = lax.rsqrt(_rowmean(x1 * x1) + EPS)
        h2_ref[...] = ((x1 * r3) * (g2pre_ref[...] * (1.0 + mod_ref[4:5, :])) + mod_ref[3:4, :]).astype(BF16)

    vec = _const((1, D_MODEL))
    f32_rows = jax.ShapeDtypeStruct((s_len, D_MODEL), F32)
    bf16_rows = jax.ShapeDtypeStruct((s_len, D_MODEL), BF16)
    return pl.pallas_call(
        body, name="fwd_mix", grid=(s_len // ts,),
        in_specs=[_rows(ts, D_MODEL), _const((N_MOD, D_MODEL)), vec, vec, vec, _VMEM, _VMEM,
                  _const((N_HEADS, CHUNK, CHUNK)), _const((CHUNK, D_A)), _const((1, D_A)), _const((1, D_A)),
                  _const((N_HEADS, GROUP_DIM, GROUP_DIM)), _const((1, D_B)), _const((1, D_B))],
        out_specs=[_rows(ts, D_Z), _rows(ts, D_MODEL), _rows(ts, D_MODEL), _rows(ts, D_MODEL), _rows(ts, D_MODEL)],
        out_shape=[jax.ShapeDtypeStruct((s_len, D_Z), F32), bf16_rows, f32_rows, f32_rows, bf16_rows],
        scratch_shapes=[pltpu.VMEM((ts, D_A), F32), pltpu.VMEM((HALO, D_B), F32), pltpu.VMEM((D_MODEL, D_Z), BF16)],
        compiler_params=_params(),
    )(x, mod6, n1pre, n1post, n2pre, win_g, wout_g, w_spatial, bsp_full, gain, bias, w_pool, b_pool, pool_scale)


def _mixer_forward_tile(za, wc, bsp_ref, gain, bias, mixed_ref):
    ga, dga = _gelu_parts(za)
    u = ga[:, :D_A]
    v = ga[:, D_A:]
    mu = _rowmean(v)
    vc = v - mu
    rstd = lax.rsqrt(_rowmean(vc * vc) + EPS)
    vhat = vc * rstd
    vn = (vhat * gain + bias).astype(BF16)
    ts = za.shape[0]
    for k in range(ts // CHUNK):
        for h in range(N_HEADS):
            blk = vn[k * CHUNK:(k + 1) * CHUNK, h * HEAD_DIM:(h + 1) * HEAD_DIM]
            mixed_ref[k * CHUNK:(k + 1) * CHUNK, h * HEAD_DIM:(h + 1) * HEAD_DIM] = (
                _dot(wc[h], blk) + bsp_ref[:, h * HEAD_DIM:(h + 1) * HEAD_DIM])
    return u, vhat, rstd, vn, dga


def _fwd_fc1(h2, fc1_g, ts):
    s_len = h2.shape[0]
    cs = D_FF // N_CHIPS

    def body(h_ref, w_ref, q_ref):
        hb = h_ref[...]
        for j in range(N_CHIPS):
            p = jnp.maximum(_dot(hb, w_ref[j]), 0.0)
            q_ref[:, j * cs:(j + 1) * cs] = (p * p).astype(BF16)

    return pl.pallas_call(
        body, name="fwd_fc1", grid=(s_len // ts,),
        in_specs=[_rows(ts, D_MODEL), _VMEM],
        out_specs=_rows(ts, D_FF),
        out_shape=jax.ShapeDtypeStruct((s_len, D_FF), BF16),
        compiler_params=_params(),
    )(h2, fc1_g)


def _fwd_fc2_loss(q, x1, target, mod6, n2post, fc2_g, ts):
    s_len = q.shape[0]
    rs = D_FF // N_CHIPS

    def body(q_ref, x1_ref, t_ref, mod_ref, g_ref, w_ref, dy_ref, df_ref, loss_ref, s_ref):
        _zero_on_first_step(loss_ref, s_ref)
        gate_gain = mod_ref[5:6, :] * g_ref[...]
        f = _dot(q_ref[:, 0:rs], w_ref[0])
        for j in range(1, N_CHIPS):
            f = f + _dot(q_ref[:, j * rs:(j + 1) * rs], w_ref[j])
        r4 = lax.rsqrt(_rowmean(f * f) + EPS)
        fh = f * r4
        err = (x1_ref[...] + fh * gate_gain) - t_ref[...]
        loss_ref[...] += 0.5 * jnp.sum(_rowmean(err * err), axis=0, keepdims=True)
        dy = err * (1.0 / D_MODEL)
        dy_ref[...] = dy
        s_ref[...] += _colsum(dy * fh)
        gh = dy * gate_gain
        df_ref[...] = (r4 * (gh - fh * _rowmean(gh * fh))).astype(BF16)

    return pl.pallas_call(
        body, name="fwd_fc2_loss", grid=(s_len // ts,),
        in_specs=[_rows(ts, D_FF), _rows(ts, D_MODEL), _rows(ts, D_MODEL), _const((N_MOD, D_MODEL)),
                  _const((1, D_MODEL)), _VMEM],
        out_specs=[_rows(ts, D_MODEL), _rows(ts, D_MODEL), _const((1, 1)), _const((1, D_MODEL))],
        out_shape=[jax.ShapeDtypeStruct((s_len, D_MODEL), F32), jax.ShapeDtypeStruct((s_len, D_MODEL), BF16),
                   jax.ShapeDtypeStruct((1, 1), F32), jax.ShapeDtypeStruct((1, D_MODEL), F32)],
        compiler_params=_params(),
    )(q, x1, target, mod6, n2post, fc2_g)


def _join_w_in_on_first_step(win_ref, full_ref):
    cs = D_Z // N_CHIPS

    @pl.when(pl.program_id(0) == 0)
    def _():
        for j in range(N_CHIPS):
            full_ref[:, j * cs:(j + 1) * cs] = win_ref[j]


def _zero_on_first_step(*refs):
    @pl.when(pl.program_id(0) == 0)
    def _():
        for ref in refs:
            ref[...] = jnp.zeros_like(ref)


def _on_last_step(fn):
    pl.when(pl.program_id(0) == pl.num_programs(0) - 1)(fn)


def _store_shard_on_last_step(acc_ref, hbm_ref, sem, j):
    _on_last_step(lambda: pltpu.make_async_copy(acc_ref.at[j], hbm_ref.at[j], sem.at[j]).start())


def _wait_stores_on_last_step(*stores):
    def wait_all():
        for acc_ref, hbm_ref, sem in stores:
            for j in range(N_CHIPS):
                pltpu.make_async_copy(acc_ref.at[j], hbm_ref.at[j], sem.at[j]).wait()

    _on_last_step(wait_all)


def _bwd_fc2(df, q, fc2_g, ts):
    s_len = df.shape[0]
    cs = D_FF // N_CHIPS

    def body(df_ref, q_ref, w_ref, dp_ref, dw_hbm, dw_ref, dw_sem):
        _zero_on_first_step(dw_ref)
        dfb = df_ref[...]
        df2 = dfb * 2.0
        for j in range(N_CHIPS):
            qb = q_ref[:, j * cs:(j + 1) * cs]
            dw_ref[j] += _dot_tn(qb, dfb).reshape(2, cs // 2, D_MODEL)
            _store_shard_on_last_step(dw_ref, dw_hbm, dw_sem, j)
            dq2 = _dot_nt(df2, w_ref[j])
            dp_ref[:, j * cs:(j + 1) * cs] = (dq2 * jnp.sqrt(qb.astype(F32))).astype(BF16)
        _wait_stores_on_last_step((dw_ref, dw_hbm, dw_sem))

    dw_shape = (N_CHIPS, 2, cs // 2, D_MODEL)
    return pl.pallas_call(
        body, name="bwd_fc2", grid=(s_len // ts,),
        in_specs=[_rows(ts, D_MODEL), _rows(ts, D_FF), _VMEM],
        out_specs=[_rows(ts, D_FF), _ANY],
        out_shape=[jax.ShapeDtypeStruct((s_len, D_FF), BF16), jax.ShapeDtypeStruct(dw_shape, F32)],
        scratch_shapes=[pltpu.VMEM(dw_shape, F32), pltpu.SemaphoreType.DMA((N_CHIPS,))],
        compiler_params=_params(),
    )(df, q, fc2_g)


def _bwd_fc1_out(dp, dy, x1, mix, h2, ycat, mod6, n2pre, n1post, fc1_g, wout_g, dep, ts):
    s_len = dp.shape[0]
    cs = D_FF // N_CHIPS
    rs = D_MODEL // N_CHIPS

    def body(dp_ref, dy_ref, x1_ref, mix_ref, h2_ref, yc_ref, mod_ref, g2_ref, g1_ref, w1_ref, wo_ref, dep_ref,
             dx1_ref, dyc_ref, dshift2_ref, da2_ref, s1_ref, dw1_hbm, dwo_hbm, dw1_ref, dwo_ref, dw1_sem, dwo_sem):
        _zero_on_first_step(dshift2_ref, da2_ref, s1_ref, dw1_ref, dwo_ref)
        h2b = h2_ref[...]
        dh2 = None
        for j in range(N_CHIPS):
            dpb = dp_ref[:, j * cs:(j + 1) * cs]
            dw1_ref[j] += _dot_tn(h2b, dpb).reshape(2, D_MODEL // 2, cs)
            _store_shard_on_last_step(dw1_ref, dw1_hbm, dw1_sem, j)
            part = _dot_nt(dpb, w1_ref[j])
            dh2 = part if dh2 is None else dh2 + part
        x1 = x1_ref[...]
        r3 = lax.rsqrt(_rowmean(x1 * x1) + EPS)
        xh = x1 * r3
        a2 = g2_ref[...] * (1.0 + mod_ref[4:5, :])
        dshift2_ref[...] += _colsum(dh2)
        da2_ref[...] += _colsum(dh2 * xh)
        dxh = dh2 * a2
        dx1 = dy_ref[...] + r3 * (dxh - xh * _rowmean(dxh * xh))
        dx1_ref[...] = dx1

        mix = mix_ref[...]
        r2 = lax.rsqrt(_rowmean(mix * mix) + EPS)
        mh = mix * r2
        s1_ref[...] += _colsum(dx1 * mh)
        gh = dx1 * (mod_ref[2:3, :] * g1_ref[...])
        dmix = (r2 * (gh - mh * _rowmean(gh * mh))).astype(BF16)
        dwo_ref[...] += _dot_tn(yc_ref[...], dmix).reshape(N_CHIPS, 2, rs // 2, D_MODEL)
        for j in range(N_CHIPS):
            _store_shard_on_last_step(dwo_ref, dwo_hbm, dwo_sem, j)
            dyc_ref[:, j * rs:(j + 1) * rs] = _dot_nt(dmix, wo_ref[j])
        _wait_stores_on_last_step((dw1_ref, dw1_hbm, dw1_sem), (dwo_ref, dwo_hbm, dwo_sem))

    vec = jax.ShapeDtypeStruct((1, D_MODEL), F32)
    dw1_shape = (N_CHIPS, 2, D_MODEL // 2, cs)
    dwo_shape = (N_CHIPS, 2, rs // 2, D_MODEL)
    return pl.pallas_call(
        body, name="bwd_fc1_out", grid=(s_len // ts,),
        in_specs=[_rows(ts, D_FF), _rows(ts, D_MODEL), _rows(ts, D_MODEL), _rows(ts, D_MODEL), _rows(ts, D_MODEL),
                  _rows(ts, D_MODEL), _const((N_MOD, D_MODEL)), _const((1, D_MODEL)), _const((1, D_MODEL)), _VMEM,
                  _VMEM, _ANY],
        out_specs=[_rows(ts, D_MODEL), _rows(ts, D_MODEL)] + [_const((1, D_MODEL))] * 3 + [_ANY, _ANY],
        out_shape=[jax.ShapeDtypeStruct((s_len, D_MODEL), F32), jax.ShapeDtypeStruct((s_len, D_MODEL), F32),
                   vec, vec, vec, jax.ShapeDtypeStruct(dw1_shape, F32), jax.ShapeDtypeStruct(dwo_shape, F32)],
        scratch_shapes=[pltpu.VMEM(dw1_shape, F32), pltpu.VMEM(dwo_shape, F32), pltpu.SemaphoreType.DMA((N_CHIPS,)),
                        pltpu.SemaphoreType.DMA((N_CHIPS,))],
        compiler_params=_params(),
    )(dp, dy, x1, mix, h2, ycat, mod6, n2pre, n1post, fc1_g, wout_g, dep)


def _mixer_bwd(z, dyc, w_spatial, bsp_full, gain, bias, w_pool, b_pool, pool_scale, dep, ts):
    s_len = z.shape[0]
    nb = ts // HALO
    last = s_len // HALO - 1
    te = ts + HALO

    def body(z_ref, zprev_ref, znext_ref, dyc_ref, dynext_ref, ws_ref, bsp_ref, gain_ref, bias_ref, wp_ref, bp_ref,
             ps_ref, dep_ref, dz_ref, dws_ref, dbsp_ref, dgain_ref, dbias_ref, dwp_ref, dbp_ref, dps_ref, mixed_ref,
             dvn_ref):
        i = pl.program_id(0)

        @pl.when(i == 0)
        def _():
            for ref in (dws_ref, dbsp_ref, dgain_ref, dbias_ref, dwp_ref, dbp_ref, dps_ref):
                ref[...] = jnp.zeros_like(ref)

        wc = _tril_weights(ws_ref)
        gain = gain_ref[...]
        u, vhat, rstd, vn, dga = _mixer_forward_tile(z_ref[:, :2 * D_A], wc, bsp_ref, gain, bias_ref[...], mixed_ref)
        dya = dyc_ref[:, :D_A]
        du = dya * mixed_ref[...]
        dmixed = dya * u
        dmb = dmixed.astype(BF16)
        dm_sum = dmixed[0:CHUNK, :]
        for k in range(1, ts // CHUNK):
            dm_sum = dm_sum + dmixed[k * CHUNK:(k + 1) * CHUNK, :]
        r_idx = lax.broadcasted_iota(jnp.int32, (CHUNK, CHUNK), 0)
        s_idx = lax.broadcasted_iota(jnp.int32, (CHUNK, CHUNK), 1)
        causal = (s_idx <= r_idx).astype(F32)
        for h in range(N_HEADS):
            lanes = slice(h * HEAD_DIM, (h + 1) * HEAD_DIM)
            dbsp_ref[h] += jnp.sum(dm_sum[:, lanes], axis=1, keepdims=True)
            acc = None
            for k in range(ts // CHUNK):
                rows = slice(k * CHUNK, (k + 1) * CHUNK)
                t = _dot_nt(dmb[rows, lanes], vn[rows, lanes])
                acc = t if acc is None else acc + t
                dvn_ref[rows, lanes] = _dot_tn(wc[h], dmb[rows, lanes])
            dws_ref[h] += acc * causal
        dvn = dvn_ref[...]
        dgain_ref[...] += _colsum(dvn * vhat)
        dbias_ref[...] += _colsum(dvn)
        dvh = dvn * gain
        dv = rstd * (dvh - _rowmean(dvh) - vhat * _rowmean(dvh * vhat))
        dz_ref[:, :D_A] = (du * dga[:, :D_A]).astype(BF16)
        dz_ref[:, D_A:2 * D_A] = (dv * dga[:, D_A:]).astype(BF16)

        zb = z_ref[:, 2 * D_A:]
        prev = jnp.where(i == 0, 0.0, zprev_ref[...])
        zb_ext = jnp.concatenate([zb, znext_ref[...]], axis=0)
        sums = _causal_window_sums(jnp.concatenate([prev, zb_ext], axis=0))
        pos, inv_counts = _window_counts(i * ts, te)
        dyb_ext = jnp.concatenate([dyc_ref[:, D_A:], dynext_ref[...]], axis=0)
        dlin_ext = dyb_ext * ps_ref[...]
        dbp_ref[...] += _colsum(dlin_ext[:ts, :])
        scaled = []
        ddiffs = []
        lins = []
        for g in range(len(POOL_WINDOWS)):
            lanes = slice(g * GROUP_DIM, (g + 1) * GROUP_DIM)
            diff = (sums[g][HALO:, :] * inv_counts[g] - zb_ext[:, lanes]).astype(BF16)
            wpb = wp_ref[g].astype(BF16)
            dlb = dlin_ext[:, lanes].astype(BF16)
            lins.append(_dot(diff[:ts, :], wpb) + bp_ref[:, lanes])
            dwp_ref[g] += _dot_tn(diff[:ts, :], dlb[:ts, :])
            dd = _dot_nt(dlb, wpb)
            ddiffs.append(dd)
            scaled.append(jnp.where(pos < float(s_len), dd * inv_counts[g], 0.0))
        dps_ref[...] += _colsum(dyb_ext[:ts, :] * jnp.concatenate(lins, axis=1))
        back = _anticausal_window_sums(jnp.concatenate(scaled, axis=1))
        for g in range(len(POOL_WINDOWS)):
            dz_ref[:, 2 * D_A + g * GROUP_DIM:2 * D_A + (g + 1) * GROUP_DIM] = (
                back[g][:ts, :] - ddiffs[g][:ts, :]).astype(BF16)

    sq = jax.ShapeDtypeStruct((N_HEADS, CHUNK, CHUNK), F32)
    vec = jax.ShapeDtypeStruct((1, D_A), F32)
    return pl.pallas_call(
        body, name="mixer_bwd", grid=(s_len // ts,),
        in_specs=[_rows(ts, D_Z),
                  pl.BlockSpec((HALO, D_B), lambda i: (jnp.maximum(i * nb - 1, 0), 2)),
                  pl.BlockSpec((HALO, D_B), lambda i: (jnp.minimum((i + 1) * nb, last), 2)),
                  _rows(ts, D_MODEL),
                  pl.BlockSpec((HALO, D_B), lambda i: (jnp.minimum((i + 1) * nb, last), 1)),
                  _const((N_HEADS, CHUNK, CHUNK)), _const((CHUNK, D_A)), _const((1, D_A)), _const((1, D_A)),
                  _const((N_HEADS, GROUP_DIM, GROUP_DIM)), _const((1, D_B)), _const((1, D_B)), _ANY],
        out_specs=[_rows(ts, D_Z), _const((N_HEADS, CHUNK, CHUNK)), _const((N_HEADS, CHUNK, 1)), _const((1, D_A)),
                   _const((1, D_A)), _const((N_HEADS, GROUP_DIM, GROUP_DIM)), _const((1, D_B)), _const((1, D_B))],
        out_shape=[jax.ShapeDtypeStruct((s_len, D_Z), BF16), sq, jax.ShapeDtypeStruct((N_HEADS, CHUNK, 1), F32), vec,
                   vec, sq, vec, vec],
        scratch_shapes=[pltpu.VMEM((ts, D_A), F32), pltpu.VMEM((ts, D_A), F32)],
        compiler_params=_params(),
    )(z, z, z, dyc, dyc, w_spatial, bsp_full, gain, bias, w_pool, b_pool, pool_scale, dep)


def _bwd_in(dz, dx1, x, mod6, n1pre, win_g, dep, ts):
    s_len = x.shape[0]
    cs = D_Z // N_CHIPS

    def body(dz_ref, dx1_ref, x_ref, mod_ref, g_ref, w_ref, dep_ref, gx_ref, dshift_ref, da_ref, dw_hbm, dw_ref,
             wfull_ref, dw_sem):
        _zero_on_first_step(dshift_ref, da_ref, dw_ref)
        _join_w_in_on_first_step(w_ref, wfull_ref)
        xv = x_ref[...]
        r = lax.rsqrt(_rowmean(xv * xv) + EPS)
        xh = xv * r
        h1b = (xh * (g_ref[...] * (1.0 + mod_ref[1:2, :])) + mod_ref[0:1, :]).astype(BF16)
        dzb = dz_ref[...]
        dw = _dot_tn(h1b, dzb)
        for j in range(N_CHIPS):
            dw_ref[j] += dw[:, j * cs:(j + 1) * cs].reshape(2, D_MODEL // 2, cs)
        dh = _dot_nt(dzb, wfull_ref[...])
        a1 = g_ref[...] * (1.0 + mod_ref[1:2, :])
        dshift_ref[...] += _colsum(dh)
        da_ref[...] += _colsum(dh * xh)
        dxh = dh * a1
        gx_ref[...] = dx1_ref[...] + r * (dxh - xh * _rowmean(dxh * xh))
        for j in range(N_CHIPS):
            _store_shard_on_last_step(dw_ref, dw_hbm, dw_sem, j)
        _wait_stores_on_last_step((dw_ref, dw_hbm, dw_sem))

    vec = jax.ShapeDtypeStruct((1, D_MODEL), F32)
    dw_shape = (N_CHIPS, 2, D_MODEL // 2, cs)
    return pl.pallas_call(
        body, name="bwd_in", grid=(s_len // ts,),
        in_specs=[_rows(ts, D_Z), _rows(ts, D_MODEL), _rows(ts, D_MODEL), _const((N_MOD, D_MODEL)),
                  _const((1, D_MODEL)), _VMEM, _ANY],
        out_specs=[_rows(ts, D_MODEL), _const((1, D_MODEL)), _const((1, D_MODEL)), _ANY],
        out_shape=[jax.ShapeDtypeStruct((s_len, D_MODEL), F32), vec, vec, jax.ShapeDtypeStruct(dw_shape, F32)],
        scratch_shapes=[pltpu.VMEM(dw_shape, F32), pltpu.VMEM((D_MODEL, D_Z), BF16),
                        pltpu.SemaphoreType.DMA((N_CHIPS,))],
        compiler_params=_params(),
    )(dz, dx1, x, mod6, n1pre, win_g, dep)


def _adamw_math(w, g, m, v):
    m = ADAM_B1 * m + (1.0 - ADAM_B1) * g
    v = ADAM_B2 * v + (1.0 - ADAM_B2) * (g * g)
    m_hat = m / (1.0 - ADAM_B1 ** ADAM_STEP)
    v_hat = v / (1.0 - ADAM_B2 ** ADAM_STEP)
    delta = -ADAM_LR * (m_hat / (jnp.sqrt(v_hat) + ADAM_EPS) + ADAM_WD * w)
    return delta, m, v


def _adamw(gs, ws, ms, vs, name, steps):
    n = len(ws)

    def body(*refs):
        for i in range(n):
            g_ref, w_ref, m_ref, v_ref = refs[4 * i:4 * i + 4]
            g = g_ref[...]
            d, nm, nv = _adamw_math(w_ref[...], g, m_ref[...], v_ref[...])
            for ref, val in zip(refs[4 * n + 4 * i:4 * n + 4 * i + 4], (g, d, nm, nv)):
                ref[...] = val

    specs = [_rows(w.shape[0] // steps, w.shape[1]) for w in ws]
    out = pl.pallas_call(
        body, name=name, grid=(steps,),
        in_specs=[s for s in specs for _ in range(4)], out_specs=[s for s in specs for _ in range(4)],
        out_shape=[jax.ShapeDtypeStruct(w.shape, F32) for w in ws for _ in range(4)],
        compiler_params=_params(),
    )(*[a for quad in zip(gs, ws, ms, vs) for a in quad])
    return [out[4 * i:4 * i + 4] for i in range(n)]


def _ada_grad_adamw(sc_t, dmod_all, w, m, v, tr):
    rows, cols = w.shape

    def body(s_ref, dm_ref, w_ref, m_ref, v_ref, g_ref, d_ref, nm_ref, nv_ref):
        x, y, _ = _position()
        mine = pl.ds(pl.multiple_of((2 * x + y) * cols, LANES), cols)
        g = s_ref[:, 0:1] * dm_ref[0:1, mine]
        for b in range(1, N_DEV):
            g = g + s_ref[:, b:b + 1] * dm_ref[b:b + 1, mine]
        g_ref[...] = g
        d, nm, nv = _adamw_math(w_ref[...], g, m_ref[...], v_ref[...])
        d_ref[...] = d
        nm_ref[...] = nm
        nv_ref[...] = nv

    spec = _rows(tr, cols)
    shape = jax.ShapeDtypeStruct((rows, cols), F32)
    return pl.pallas_call(
        body, name="ada_grad_adamw", grid=(rows // tr,),
        in_specs=[_rows(tr, N_DEV), _const(dmod_all.shape), spec, spec, spec],
        out_specs=[spec] * 4, out_shape=[shape] * 4, compiler_params=_params(),
    )(sc_t, dmod_all, w, m, v)


def _mod_grads(da1, dshift1, s1, da2, dshift2, s2, mod6, n1pre, n1post, n2pre, n2post):
    def body(da1_ref, ds1_ref, s1_ref, da2_ref, ds2_ref, s2_ref, mod_ref, n1_ref, p1_ref, n2_ref, p2_ref, dmod_ref,
             dn_ref):
        dmod_ref[0:1, :] = ds1_ref[...]
        dmod_ref[1:2, :] = da1_ref[...] * n1_ref[...]
        dmod_ref[2:3, :] = s1_ref[...] * p1_ref[...]
        dmod_ref[3:4, :] = ds2_ref[...]
        dmod_ref[4:5, :] = da2_ref[...] * n2_ref[...]
        dmod_ref[5:6, :] = s2_ref[...] * p2_ref[...]
        dn_ref[0:1, :] = da1_ref[...] * (1.0 + mod_ref[1:2, :])
        dn_ref[1:2, :] = s1_ref[...] * mod_ref[2:3, :]
        dn_ref[2:3, :] = da2_ref[...] * (1.0 + mod_ref[4:5, :])
        dn_ref[3:4, :] = s2_ref[...] * mod_ref[5:6, :]

    return pl.pallas_call(
        body, name="mod_grads",
        out_shape=[jax.ShapeDtypeStruct((N_MOD, D_MODEL), F32), jax.ShapeDtypeStruct((4, D_MODEL), F32)],
    )(da1, dshift1, s1, da2, dshift2, s2, mod6, n1pre, n1post, n2pre, n2post)


def _position():
    x, y, c = lax.axis_index("x"), lax.axis_index("y"), lax.axis_index("c")
    return x, y, c


def _flip(v, bit):
    return 1 - v if bit else v


def _peer(x, y, c, k):
    return (_flip(x, k & 4), _flip(y, k & 2), _flip(c, k & 1))


def _remote(src, dst, send_sem, recv_sem, device):
    return pltpu.make_async_remote_copy(src_ref=src, dst_ref=dst, send_sem=send_sem, recv_sem=recv_sem,
                                        device_id=device, device_id_type=MESH)


def _mod_exchange(c_row, w_ada_shard, b_ada_row, ws, early):
    cs = w_ada_shard.shape[1]
    n = len(ws)

    def body(c_ref, w_hbm, b_ref, *refs):
        shards, (mod_ref, sc_ref), zones = refs[:n], refs[n:n + 2], refs[n + 2:2 * n + 2]
        rows_ref, w_ref, w_sem, send1, recv1, send2, recv2 = refs[2 * n + 2:2 * n + 9]
        wide, narrow = refs[2 * n + 9:3 * n + 9], refs[3 * n + 9:4 * n + 9]
        cast_load, cast_store = refs[4 * n + 9:]
        x, y, c = _position()
        me = 4 * x + 2 * y + c
        chip = 2 * x + y
        w_load = pltpu.make_async_copy(w_hbm, w_ref, w_sem)
        w_load.start()
        loads = [pltpu.make_async_copy(shards[i], wide[i], cast_load.at[i]) for i in range(n)]
        stores = [pltpu.make_async_copy(narrow[i], zones[i].at[chip], cast_store.at[i]) for i in range(n)]

        def cast(i):
            loads[i].wait()
            narrow[i][...] = wide[i][...].astype(BF16)
            stores[i].start()

        for i in range(early):
            loads[i].start()
        cv = c_ref[...]
        sc_ref[me] = cv * jax.nn.sigmoid(cv)
        gather = [_remote(sc_ref.at[me], sc_ref.at[me], send1.at[k - 1], recv1.at[k - 1], _peer(x, y, c, k))
                  for k in range(1, N_DEV)]
        for cp in gather:
            cp.start()
        for i in range(early):
            cast(i)
        for i in range(early, n):
            loads[i].start()
        for k in range(1, N_DEV):
            px, py, pc = _peer(x, y, c, k)
            src = 4 * px + 2 * py + pc
            _remote(sc_ref.at[src], sc_ref.at[src], send1.at[k - 1], recv1.at[k - 1], (px, py, pc)).wait_recv()
        for cp in gather:
            cp.wait_send()
        sc_all = jnp.concatenate([sc_ref[b] for b in range(N_DEV)], axis=0)
        w_load.wait()
        part = _dot(sc_all.astype(BF16), w_ref[...].astype(BF16))
        part = part + b_ref[:, pl.ds(pl.multiple_of(chip * cs, LANES), cs)]
        for b in range(N_DEV):
            rows_ref[b] = part[b:b + 1, :]
        mod_ref[chip] = rows_ref[me]
        hand = []
        for k in (2, 4, 6):
            px, py, _ = _peer(x, y, c, k)
            hand.append(_remote(rows_ref.at[4 * px + 2 * py + c], mod_ref.at[chip], send2.at[k // 2 - 1],
                                recv2.at[k // 2 - 1], (px, py, c)))
        for cp in hand:
            cp.start()
        for i in range(early, n):
            cast(i)
        for k in (2, 4, 6):
            px, py, _ = _peer(x, y, c, k)
            pchip = 2 * px + py
            _remote(rows_ref.at[me], mod_ref.at[pchip], send2.at[k // 2 - 1], recv2.at[k // 2 - 1],
                    (px, py, c)).wait_recv()
        for cp in hand:
            cp.wait_send()
        for cp in stores:
            cp.wait()

    out = pl.pallas_call(
        body, name="mod_exchange",
        in_specs=[_VMEM, _ANY, _VMEM] + [_ANY] * n, out_specs=[_VMEM, _VMEM] + [_ANY] * n,
        out_shape=[jax.ShapeDtypeStruct((N_CHIPS, 1, cs), F32), jax.ShapeDtypeStruct((N_DEV, 1, D_MODEL), F32)]
        + [jax.ShapeDtypeStruct((N_CHIPS,) + w.shape, BF16) for w in ws],
        scratch_shapes=[pltpu.VMEM((N_DEV, 1, cs), F32), pltpu.VMEM(w_ada_shard.shape, F32), pltpu.SemaphoreType.DMA,
                        pltpu.SemaphoreType.DMA((N_DEV - 1,)),
                        pltpu.SemaphoreType.DMA((N_DEV - 1,)), pltpu.SemaphoreType.DMA((N_CHIPS - 1,)),
                        pltpu.SemaphoreType.DMA((N_CHIPS - 1,))]
        + [pltpu.VMEM(w.shape, F32) for w in ws] + [pltpu.VMEM(w.shape, BF16) for w in ws]
        + [pltpu.SemaphoreType.DMA((n,))] * 2,
        compiler_params=pltpu.CompilerParams(vmem_limit_bytes=VMEM_LIMIT),
    )(c_row, w_ada_shard, b_ada_row, *ws)
    return out[0], out[1], list(out[2:])


_HBM = pl.BlockSpec(memory_space=pltpu.HBM)
_SEM = pl.BlockSpec(memory_space=pltpu.SEMAPHORE)
_EFFECT = pltpu.SideEffectType.DATAFLOW_SIDE_EFFECTING
_CHIP_HOPS = (2, 4, 6)


def _in_hbm(a):
    return pltpu.with_memory_space_constraint(a, pltpu.HBM)


def _sems3():
    return pltpu.SemaphoreType.DMA((len(_CHIP_HOPS),))


def _ag_start(lands, after, name):
    n = len(lands)

    def body(*refs):
        zones = refs[:n]
        sends, recvs = refs[n + 1:2 * n + 1], refs[2 * n + 1:3 * n + 1]
        x, y, c = _position()
        chip = 2 * x + y
        for i in range(n):
            half = zones[i].shape[1] // 2
            mine = zones[i].at[chip, pl.ds(c * half, half)]
            for s, k in enumerate(_CHIP_HOPS):
                px, py, _ = _peer(x, y, c, k)
                _remote(mine, mine, sends[i].at[s], recvs[i].at[s], (px, py, c)).start()

    out = pl.pallas_call(
        body, name=name,
        in_specs=[_HBM] * n + [_ANY],
        out_specs=[_SEM] * (2 * n) + [_HBM] * n,
        out_shape=[_sems3()] * (2 * n) + [pltpu.HBM(z.shape, BF16) for z in lands],
        input_output_aliases={i: 2 * n + i for i in range(n)},
        compiler_params=pltpu.CompilerParams(has_side_effects=_EFFECT),
    )(*[_in_hbm(z) for z in lands], after)
    return [(out[2 * n + i], out[i], out[n + i]) for i in range(n)]


def _ag_pass(group, after, name):
    n = len(group)

    def body(*refs):
        zones = refs[:n]
        sends, recvs = refs[n:2 * n], refs[2 * n:3 * n]
        fsends, frecvs = refs[4 * n + 1:5 * n + 1], refs[5 * n + 1:6 * n + 1]
        x, y, c = _position()
        chip = 2 * x + y
        for i in range(n):
            half = zones[i].shape[1] // 2
            rows = pl.ds(c * half, half)
            for s, k in enumerate(_CHIP_HOPS):
                px, py, _ = _peer(x, y, c, k)
                landed = zones[i].at[2 * px + py, rows]
                _remote(landed, landed, sends[i].at[s], recvs[i].at[s], (px, py, c)).wait_recv()
                _remote(landed, landed, fsends[i].at[s], frecvs[i].at[s], (x, y, 1 - c)).start()
        for i in range(n):
            half = zones[i].shape[1] // 2
            mine = zones[i].at[chip, pl.ds(c * half, half)]
            for s, k in enumerate(_CHIP_HOPS):
                px, py, _ = _peer(x, y, c, k)
                _remote(mine, mine, sends[i].at[s], recvs[i].at[s], (px, py, c)).wait_send()

    out = pl.pallas_call(
        body, name=name,
        in_specs=[_HBM] * n + [_SEM] * (2 * n) + [_ANY],
        out_specs=[_HBM] * n + [_SEM] * (2 * n),
        out_shape=[pltpu.HBM(g[0].shape, BF16) for g in group] + [_sems3()] * (2 * n),
        input_output_aliases={i: i for i in range(n)},
        compiler_params=pltpu.CompilerParams(has_side_effects=_EFFECT),
    )(*[g[0] for g in group], *[g[1] for g in group], *[g[2] for g in group], after)
    return [(out[i], out[n + i], out[2 * n + i]) for i in range(n)]


def _ag_done(group, name):
    n = len(group)

    def body(*refs):
        lands = refs[:n]
        fsends, frecvs = refs[n:2 * n], refs[2 * n:3 * n]
        x, y, c = _position()
        for i in range(n):
            half = lands[i].shape[1] // 2
            for s, k in enumerate(_CHIP_HOPS):
                px, py, _ = _peer(x, y, c, k)
                sent = lands[i].at[2 * px + py, pl.ds(c * half, half)]
                got = lands[i].at[2 * px + py, pl.ds((1 - c) * half, half)]
                cp = _remote(sent, got, fsends[i].at[s], frecvs[i].at[s], (x, y, 1 - c))
                cp.wait_recv()
                cp.wait_send()

    out = pl.pallas_call(
        body, name=name,
        in_specs=[_HBM] * n + [_SEM] * (2 * n),
        out_specs=[_HBM] * n,
        out_shape=[pltpu.HBM(g[0].shape, BF16) for g in group],
        input_output_aliases={i: i for i in range(n)},
        compiler_params=pltpu.CompilerParams(has_side_effects=_EFFECT),
    )(*[g[0] for g in group], *[g[1] for g in group], *[g[2] for g in group])
    return list(out)


def _small_spread_wait(slots, sends, recvs, afters):
    def body(z_ref, sends, recvs, *rest):
        x, y, c = _position()
        mine = z_ref.at[2 * x + y]
        for s, k in enumerate(_CHIP_HOPS):
            px, py, _ = _peer(x, y, c, k)
            cp = _remote(mine, z_ref.at[2 * px + py], sends.at[s], recvs.at[s], (px, py, c))
            cp.wait_recv()
            cp.wait_send()

    return pl.pallas_call(
        body, name="small_spread_wait",
        in_specs=[_HBM, _SEM, _SEM] + [_ANY] * len(afters), out_specs=_HBM, out_shape=pltpu.HBM(slots.shape, F32),
        input_output_aliases={0: 0},
        compiler_params=pltpu.CompilerParams(has_side_effects=_EFFECT),
    )(slots, sends, recvs, *afters)


def _sibling_sum(pgs, name, small=None):
    n = len(pgs)
    k = 0 if small is None else 1
    units = [(i, j) for i in range(n) for j in range(N_CHIPS)]

    def body(*refs):
        refs = list(refs)
        take = lambda count: [refs.pop(0) for _ in range(count)]
        ins, small_in = take(n), take(k)
        qbs, owns, slots_out = take(n), take(n), take(k)
        mine, other, stage, got = take(n), take(n), take(n), take(n)
        load_a, load_b, send, recv, keep_wire, keep_own = take(6)
        x, y, c = _position()
        chip = 2 * x + y
        if k:
            sib_ref, own_ref, pair_send, pair_recv, keep_small, load_small = take(6)
            pair = _remote(small_in[0], sib_ref, pair_send, pair_recv, (x, y, 1 - c))
            pair.start()
            own_load = pltpu.make_async_copy(small_in[0], own_ref, load_small)
            own_load.start()
        loads_a = [pltpu.make_async_copy(ins[i].at[j, 1 - c], other[i].at[j], load_a.at[u])
                   for u, (i, j) in enumerate(units)]
        loads_b = [pltpu.make_async_copy(ins[i].at[j, c], mine[i].at[j], load_b.at[u])
                   for u, (i, j) in enumerate(units)]
        for cp in loads_a + loads_b:
            cp.start()
        sent = []
        for u, (i, j) in enumerate(units):
            loads_a[u].wait()
            stage[i][j] = other[i][j].astype(BF16)
            cp = _remote(stage[i].at[j], got[i].at[j], send.at[u], recv.at[u], (x, y, 1 - c))
            cp.start()
            sent.append(cp)
        stores = []
        for u, (i, j) in enumerate(units):
            loads_b[u].wait()
            sent[u].wait_recv()
            q = mine[i][j] + got[i][j].astype(F32)
            mine[i][j] = q
            got[i][j] = q.astype(BF16)
            stores.append(pltpu.make_async_copy(got[i].at[j], qbs[i].at[j], keep_wire.at[u]))
            stores[-1].start()
        for i in range(n):
            stores.append(pltpu.make_async_copy(mine[i].at[chip], owns[i], keep_own.at[i]))
            stores[-1].start()
        if k:
            pair.wait()
            own_load.wait()
            sib_ref[...] = own_ref[...] + sib_ref[...]
            stores.append(pltpu.make_async_copy(sib_ref, slots_out[0].at[chip], keep_small))
            stores[-1].start()
        for cp in sent:
            cp.wait_send()
        for cp in stores:
            cp.wait()

    wire = [(N_CHIPS,) + p.shape[2:] for p in pgs]
    extra_out, extra_scratch = [], []
    if k:
        extra_out = [jax.ShapeDtypeStruct((N_CHIPS,) + small.shape, F32)]
        extra_scratch = [pltpu.VMEM(small.shape, F32)] * 2 + [pltpu.SemaphoreType.DMA] * 4
    out = pl.pallas_call(
        body, name=name, in_specs=[_ANY] * (n + k), out_specs=[_ANY] * (2 * n + k),
        out_shape=[jax.ShapeDtypeStruct(w, BF16) for w in wire] + [jax.ShapeDtypeStruct(w[1:], F32) for w in wire]
        + extra_out,
        scratch_shapes=[pltpu.VMEM(w, F32) for w in wire] * 2 + [pltpu.VMEM(w, BF16) for w in wire] * 2
        + [pltpu.SemaphoreType.DMA((len(units),))] * 5 + [pltpu.SemaphoreType.DMA((n,))] + extra_scratch,
        compiler_params=pltpu.CompilerParams(vmem_limit_bytes=VMEM_LIMIT),
    )(*pgs, *([small] if k else []))
    return list(out[:n]), list(out[n:2 * n]), list(out[2 * n:])


def _rs_start(qbs, name, small=None):
    n = len(qbs)
    k = 0 if small is None else 1

    def body(*refs):
        outs, inboxes, slots = refs[:n], refs[n:2 * n], refs[2 * n:2 * n + k]
        sems = refs[2 * n + k:4 * n + 3 * k]
        sends, recvs = sems[:n], sems[n:2 * n]
        x, y, c = _position()
        chip = 2 * x + y
        if k:
            mine = slots[0].at[chip]
            for s, hop in enumerate(_CHIP_HOPS):
                px, py, _ = _peer(x, y, c, hop)
                _remote(mine, mine, sems[2 * n].at[s], sems[2 * n + 1].at[s], (px, py, c)).start()
        for i in range(n):
            for s, hop in enumerate(_CHIP_HOPS):
                px, py, _ = _peer(x, y, c, hop)
                _remote(outs[i].at[2 * px + py], inboxes[i].at[chip], sends[i].at[s], recvs[i].at[s], (px, py, c)).start()

    inboxes = [_in_hbm(lax.empty(q.shape, BF16)) for q in qbs]
    n_sems = 2 * n + 2 * k
    out = pl.pallas_call(
        body, name=name,
        in_specs=[_HBM] * (2 * n + k),
        out_specs=[_SEM] * n_sems + [_HBM] * (2 * n + k),
        out_shape=[_sems3()] * n_sems + [pltpu.HBM(q.shape, BF16) for q in qbs] * 2
        + ([pltpu.HBM(small.shape, F32)] if k else []),
        input_output_aliases={i: n_sems + i for i in range(2 * n + k)},
        compiler_params=pltpu.CompilerParams(has_side_effects=_EFFECT),
    )(*[_in_hbm(q) for q in qbs], *inboxes, *([_in_hbm(small)] if k else []))
    states = [(out[n_sems + i], out[n_sems + n + i], out[i], out[n + i]) for i in range(n)]
    return (states, (out[n_sems + 2 * n], out[2 * n], out[2 * n + 1])) if k else states


def _rs_wait(group, after, name):
    n = len(group)

    def body(*refs):
        outs, inboxes = refs[:n], refs[n:2 * n]
        sends, recvs = refs[2 * n:3 * n], refs[3 * n:4 * n]
        x, y, c = _position()
        for i in range(n):
            for s, k in enumerate(_CHIP_HOPS):
                px, py, _ = _peer(x, y, c, k)
                slot = 2 * px + py
                cp = _remote(outs[i].at[slot], inboxes[i].at[slot], sends[i].at[s], recvs[i].at[s], (px, py, c))
                cp.wait_recv()
                cp.wait_send()

    out = pl.pallas_call(
        body, name=name,
        in_specs=[_HBM] * (2 * n) + [_SEM] * (2 * n) + [_ANY],
        out_specs=[_HBM] * n,
        out_shape=[pltpu.HBM(g[1].shape, BF16) for g in group],
        input_output_aliases={n + i: i for i in range(n)},
        compiler_params=pltpu.CompilerParams(has_side_effects=_EFFECT),
    )(*[g[0] for g in group], *[g[1] for g in group], *[g[2] for g in group], *[g[3] for g in group], after)
    return list(out)


def _final_share(inboxes, owns, name):
    n = len(inboxes)
    units = [(i, s) for i in range(n) for s in range(len(_CHIP_HOPS))]

    def body(*refs):
        ins, mine, outs, landed, half = (refs[k * n:(k + 1) * n] for k in range(5))
        load, load_own, keep, send, recv = refs[5 * n:]
        x, y, c = _position()
        loads = []
        for u, (i, s) in enumerate(units):
            px, py, _ = _peer(x, y, c, _CHIP_HOPS[s])
            loads.append(pltpu.make_async_copy(ins[i].at[2 * px + py], landed[i].at[s], load.at[u]))
        loads_own = [pltpu.make_async_copy(mine[i], half[i], load_own.at[i]) for i in range(n)]
        for cp in loads + loads_own:
            cp.start()
        copies = []
        for i in range(n):
            for s in range(len(_CHIP_HOPS)):
                loads[len(_CHIP_HOPS) * i + s].wait()
            loads_own[i].wait()
            total = (landed[i][0].astype(F32) + landed[i][1].astype(F32)) + landed[i][2].astype(F32)
            half[i][...] = total + half[i][...]
            copies.append(pltpu.make_async_copy(half[i], outs[i].at[c], keep.at[i]))
            copies.append(_remote(half[i], outs[i].at[c], send.at[i], recv.at[i], (x, y, 1 - c)))
            for cp in copies[-2:]:
                cp.start()
        for i in range(n):
            theirs = outs[i].at[1 - c]
            _remote(theirs, theirs, send.at[i], recv.at[i], (x, y, 1 - c)).wait_recv()
        for i in range(n):
            copies[2 * i].wait()
            copies[2 * i + 1].wait_send()

    return pl.pallas_call(
        body, name=name, in_specs=[_ANY] * (2 * n), out_specs=[_ANY] * n,
        out_shape=[jax.ShapeDtypeStruct((2,) + o.shape, F32) for o in owns],
        scratch_shapes=[pltpu.VMEM((len(_CHIP_HOPS),) + o.shape, BF16) for o in owns]
        + [pltpu.VMEM(o.shape, F32) for o in owns]
        + [pltpu.SemaphoreType.DMA((len(units),))] + [pltpu.SemaphoreType.DMA((n,))] * 4,
        compiler_params=pltpu.CompilerParams(vmem_limit_bytes=VMEM_LIMIT),
    )(*inboxes, *owns)


_SMALL = (("b_ada", N_MOD * D_MODEL), ("norm1_pre", D_MODEL), ("norm1_post", D_MODEL), ("norm2_pre", D_MODEL),
          ("norm2_post", D_MODEL), ("w_spatial", N_HEADS * CHUNK * CHUNK), ("b_spatial", N_HEADS * CHUNK),
          ("ln_v_gain", D_A), ("ln_v_bias", D_A), ("w_pool", N_HEADS * GROUP_DIM * GROUP_DIM),
          ("b_pool", D_B), ("pool_scale", D_B))
_MOD_ROWS = N_MOD * D_MODEL // LANES


def _packed_rows(size):
    return -(-(size // LANES) // SUBLANES) * SUBLANES


def _pack(parts):
    out = []
    for name, size in _SMALL:
        a = parts[name].reshape(size // LANES, LANES)
        pad = _packed_rows(size) - a.shape[0]
        out.append(jnp.pad(a, ((0, pad), (0, 0))) if pad else a)
    return out


def _small_adamw(slots, ws, ms, vs):
    n = len(_SMALL)
    head = N_DEV * _MOD_ROWS

    def body(*refs):
        s_ref, w, m, v = refs[0], refs[1:1 + n], refs[1 + n:1 + 2 * n], refs[1 + 2 * n:1 + 3 * n]
        outs = refs[1 + 3 * n:1 + 7 * n]
        dmod_ref, loss_ref, t_ref = refs[1 + 7 * n:]
        t_ref[...] = ((s_ref[0] + s_ref[1]) + s_ref[2]) + s_ref[3]
        dmod_ref[...] = t_ref[0:head, :]
        loss_ref[...] = t_ref[t_ref.shape[0] - SUBLANES:t_ref.shape[0] - SUBLANES + 1, 0:1]
        row = head
        for i, (_, size) in enumerate(_SMALL):
            if i == 0:
                g = t_ref[0:_MOD_ROWS, :]
                for b in range(1, N_DEV):
                    g = g + t_ref[b * _MOD_ROWS:(b + 1) * _MOD_ROWS, :]
            else:
                g = t_ref[row:row + size // LANES, :]
                row += _packed_rows(size)
            d, nm, nv = _adamw_math(w[i][...], g, m[i][...], v[i][...])
            for ref, val in zip(outs[4 * i:4 * i + 4], (g, d, nm, nv)):
                ref[...] = val

    each = [jax.ShapeDtypeStruct((size // LANES, LANES), F32) for _, size in _SMALL for _ in range(4)]
    out = pl.pallas_call(
        body, name="small_adamw",
        out_shape=each + [jax.ShapeDtypeStruct((head, LANES), F32), jax.ShapeDtypeStruct((1, 1), F32)],
        scratch_shapes=[pltpu.VMEM(slots.shape[1:], F32)],
        compiler_params=pltpu.CompilerParams(vmem_limit_bytes=VMEM_LIMIT),
    )(slots, *ws, *ms, *vs)
    return [out[4 * i:4 * i + 4] for i in range(n)], out[4 * n], out[4 * n + 1]


def kernel(x, c, w_ada, b_ada, norm1_pre, norm1_post, w_in, w_spatial, b_spatial, ln_v_gain, ln_v_bias, w_pool, b_pool, pool_scale, w_out, norm2_pre, norm2_post, w_fc1, w_fc2, loss_target, m_w_ada, m_b_ada, m_norm1_pre, m_norm1_post, m_w_in, m_w_spatial, m_b_spatial, m_ln_v_gain, m_ln_v_bias, m_w_pool, m_b_pool, m_pool_scale, m_w_out, m_norm2_pre, m_norm2_post, m_w_fc1, m_w_fc2, v_w_ada, v_b_ada, v_norm1_pre, v_norm1_post, v_w_in, v_w_spatial, v_b_spatial, v_ln_v_gain, v_ln_v_bias, v_w_pool, v_b_pool, v_pool_scale, v_w_out, v_norm2_pre, v_norm2_post, v_w_fc1, v_w_fc2):
    weights = dict(w_ada=w_ada, b_ada=b_ada, norm1_pre=norm1_pre, norm1_post=norm1_post, w_in=w_in,
                   w_spatial=w_spatial, b_spatial=b_spatial, ln_v_gain=ln_v_gain, ln_v_bias=ln_v_bias, w_pool=w_pool,
                   b_pool=b_pool, pool_scale=pool_scale, w_out=w_out, norm2_pre=norm2_pre, norm2_post=norm2_post,
                   w_fc1=w_fc1, w_fc2=w_fc2)
    m_old = dict(w_ada=m_w_ada, b_ada=m_b_ada, norm1_pre=m_norm1_pre, norm1_post=m_norm1_post, w_in=m_w_in,
                 w_spatial=m_w_spatial, b_spatial=m_b_spatial, ln_v_gain=m_ln_v_gain, ln_v_bias=m_ln_v_bias,
                 w_pool=m_w_pool, b_pool=m_b_pool, pool_scale=m_pool_scale, w_out=m_w_out, norm2_pre=m_norm2_pre,
                 norm2_post=m_norm2_post, w_fc1=m_w_fc1, w_fc2=m_w_fc2)
    v_old = dict(w_ada=v_w_ada, b_ada=v_b_ada, norm1_pre=v_norm1_pre, norm1_post=v_norm1_post, w_in=v_w_in,
                 w_spatial=v_w_spatial, b_spatial=v_b_spatial, ln_v_gain=v_ln_v_gain, ln_v_bias=v_ln_v_bias,
                 w_pool=v_w_pool, b_pool=v_b_pool, pool_scale=v_pool_scale, w_out=v_w_out, norm2_pre=v_norm2_pre,
                 norm2_post=v_norm2_post, w_fc1=v_w_fc1, w_fc2=v_w_fc2)
    order = ("w_ada", "b_ada", "norm1_pre", "norm1_post", "w_in", "w_spatial", "b_spatial", "ln_v_gain", "ln_v_bias",
             "w_pool", "b_pool", "pool_scale", "w_out", "norm2_pre", "norm2_post", "w_fc1", "w_fc2")
    mx, my, mc = _position()
    me = 4 * mx + 2 * my + mc
    chip = 2 * mx + my
    row = lambda a: a.reshape(1, -1)

    xs, target = x[0], loss_target[0]
    n1pre, n1post, n2pre, n2post = row(norm1_pre), row(norm1_post), row(norm2_pre), row(norm2_post)
    mixer = (w_spatial, jnp.repeat(b_spatial.T, HEAD_DIM, axis=1), row(ln_v_gain), row(ln_v_bias), w_pool,
             row(b_pool), row(pool_scale))
    ts_big, ts_mid = 512, 256

    mod4, sc_all, lands = _mod_exchange(c, w_ada, row(b_ada), [w_in, w_out, w_fc1, w_fc2], 2)
    mod6 = mod4.reshape(N_MOD, D_MODEL)
    ag = _ag_start(lands, mod4, "ag_start")

    win_g, wout_g = _ag_done(_ag_pass([ag[0], ag[1]], ag[2][0], "ag_pass_mix"), "ag_done_mix")
    z, ycat, mix, x1, h2 = _fwd_mix(xs, mod6, n1pre, n1post, n2pre, win_g, wout_g, *mixer, ts_big)
    (fc1_g,) = _ag_done(_ag_pass([ag[2]], h2, "ag_pass_fc1"), "ag_done_fc1")
    q = _fwd_fc1(h2, fc1_g, ts_big)
    (fc2_g,) = _ag_done(_ag_pass([ag[3]], q, "ag_pass_fc2"), "ag_done_fc2")
    dy, df, loss, s2 = _fwd_fc2_loss(q, x1, target, mod6, n2post, fc2_g, ts_big)

    def reduce_start(partials, tag, small=None):
        wire, owns, slots = _sibling_sum(partials, "sibling_sum_" + tag, small)
        return _rs_start(wire, "rs_start_" + tag), owns, slots

    def reduce_finish(state, owns, names, tag, dep):
        inboxes = _rs_wait(state, dep, "rs_wait_" + tag)
        shards = _final_share(inboxes, owns, "final_share_" + tag)
        updates = _adamw([g.reshape(weights[n].shape) for n, g in zip(names, shards)], [weights[n] for n in names],
                         [m_old[n] for n in names], [v_old[n] for n in names], "adamw_" + tag, 4)
        for n, (g, d, nm, nv) in zip(names, updates):
            grads[n], deltas[n], new_m[n], new_v[n] = g, d, nm, nv

    grads, deltas, new_m, new_v = {}, {}, {}, {}
    dp, g_fc2 = _bwd_fc2(df, q, fc2_g, ts_big)
    state_fc2, owns_fc2, _ = reduce_start([g_fc2], "fc2")
    dx1, dyc, dshift2, da2, s1, g_fc1, g_out = _bwd_fc1_out(
        dp, dy, x1, mix, h2, ycat, mod6, n2pre, n1post, fc1_g, wout_g, state_fc2[0][0], ts_mid)
    state_mid, owns_mid, _ = reduce_start([g_fc1, g_out], "mid")
    dz, dws, dbsp, dgain, dbias, dwp, dbp, dps = _mixer_bwd(z, dyc, *mixer, state_mid[0][0], ts_big)
    grad_x, dshift1, da1, g_in = _bwd_in(dz, dx1, xs, mod6, n1pre, win_g, state_mid[0][0], ts_big)
    dmod6, dnorms = _mod_grads(da1, dshift1, s1, da2, dshift2, s2, mod6, n1pre, n1post, n2pre, n2post)

    parts = dict(b_ada=dmod6, norm1_pre=dnorms[0], norm1_post=dnorms[1], norm2_pre=dnorms[2], norm2_post=dnorms[3],
                 w_spatial=dws, b_spatial=dbsp, ln_v_gain=dgain, ln_v_bias=dbias, w_pool=dwp, b_pool=dbp,
                 pool_scale=dps)
    pieces = _pack(parts)
    slots = lax.dynamic_update_slice(jnp.zeros((N_DEV * _MOD_ROWS, LANES), F32), pieces[0], (me * _MOD_ROWS, 0))
    loss_tile = jnp.pad(loss, ((0, SUBLANES - 1), (0, LANES - 1)))
    wire_in, owns_in, pair_sum = _sibling_sum([g_in], "sibling_sum_in",
                                              jnp.concatenate([slots] + pieces[1:] + [loss_tile], axis=0))
    state_in, spread = _rs_start(wire_in, "rs_start_in", pair_sum[0])
    reduce_finish(state_fc2 + state_mid, owns_fc2 + owns_mid, ("w_fc2", "w_fc1", "w_out"), "mlp", state_in[0][0])
    flat = lambda d: [d[n].reshape(size // LANES, LANES) for n, size in _SMALL]
    small_out, dmod_all, loss = _small_adamw(
        _small_spread_wait(*spread, [deltas[n] for n in ("w_fc2", "w_fc1", "w_out")]), flat(weights), flat(m_old),
        flat(v_old))
    loss = loss.reshape(())
    for (n, _), (g, d, nm, nv) in zip(_SMALL, small_out):
        shape = weights[n].shape
        grads[n], deltas[n], new_m[n], new_v[n] = g.reshape(shape), d.reshape(shape), nm.reshape(shape), nv.reshape(shape)

    sc_t = sc_all.reshape(N_DEV, D_MODEL).T
    grads["w_ada"], deltas["w_ada"], new_m["w_ada"], new_v["w_ada"] = _ada_grad_adamw(
        sc_t, dmod_all.reshape(N_DEV, N_MOD * D_MODEL), w_ada, m_w_ada, v_w_ada, 256)

    reduce_finish(state_in, owns_in, ("w_in",), "in", deltas["w_ada"])

    return (loss, grad_x[None], *[grads[n] for n in order], *[deltas[n] for n in order],
            *[new_m[n] for n in order], *[new_v[n] for n in order])
```

```python
import jax
import jax.numpy as jnp
from jax import lax
from jax.experimental import pallas as pl
from jax.experimental.pallas import tpu as pltpu

F32 = jnp.float32
BF16 = jnp.bfloat16
MESH = pl.DeviceIdType.MESH

D_MODEL = 1024
D_A = 512
D_B = 512
D_Z = 2 * D_A + D_B
N_HEADS = 4
HEAD_DIM = 128
CHUNK = 128
POOL_WINDOWS = (2, 4, 8, 16)
GROUP_DIM = 128
D_FF = 4096
N_MOD = 6
EPS = 1e-6
HALO = 16
N_CHIPS = 4
N_DEV = 8

ADAM_LR = 0.001
ADAM_B1 = 0.9
ADAM_B2 = 0.999
ADAM_EPS = 1e-08
ADAM_WD = 0.01
ADAM_STEP = 10

VMEM_LIMIT = 56 * 1024 * 1024
LANES = 128
SUBLANES = 8

_VMEM = pl.BlockSpec(memory_space=pltpu.VMEM)
_ANY = pl.BlockSpec(memory_space=pl.ANY)


def _params(n_grid_axes=1):
    return pltpu.CompilerParams(dimension_semantics=("arbitrary",) * n_grid_axes, vmem_limit_bytes=VMEM_LIMIT)


def _rows(ts, width):
    return pl.BlockSpec((ts, width), lambda i: (i, 0))


def _const(shape):
    return pl.BlockSpec(shape, lambda i: (0,) * len(shape))


def _dot(a, b):
    return jnp.dot(a, b, preferred_element_type=F32)


def _dot_nt(a, b):
    return lax.dot_general(a, b, (((1,), (1,)), ((), ())), preferred_element_type=F32)


def _dot_tn(a, b):
    return lax.dot_general(a, b, (((0,), (0,)), ((), ())), preferred_element_type=F32)


def _rowmean(v):
    return jnp.mean(v, axis=-1, keepdims=True)


def _colsum(v):
    return jnp.sum(v, axis=0, keepdims=True)


def _gelu_parts(z):
    k0 = 0.7978845608028654
    k1 = 0.044715
    z2 = z * z
    t = jnp.tanh(z * (k0 + (k0 * k1) * z2))
    u = 0.5 * t + 0.5
    g = z * u
    dg = u + (0.5 * z) * (1.0 - t * t) * (k0 + (3.0 * k0 * k1) * z2)
    return g, dg


def _tril_weights(ws_ref):
    r = lax.broadcasted_iota(jnp.int32, (CHUNK, CHUNK), 0)
    s = lax.broadcasted_iota(jnp.int32, (CHUNK, CHUNK), 1)
    mask = (s <= r).astype(F32)
    return [(ws_ref[h] * mask).astype(BF16) for h in range(N_HEADS)]


def _window_counts(first_row, n_rows):
    pos = (first_row + lax.broadcasted_iota(jnp.int32, (n_rows, 1), 0)).astype(F32)
    return pos, [1.0 / jnp.minimum(pos + 1.0, float(w)) for w in POOL_WINDOWS]


def _causal_window_sums(ext):
    out = []
    e = ext
    shift = 1
    for g in range(len(POOL_WINDOWS)):
        e = e + pltpu.roll(e, shift, 0)
        shift *= 2
        out.append(e[:, g * GROUP_DIM:(g + 1) * GROUP_DIM])
    return out


def _anticausal_window_sums(ext):
    n = ext.shape[0]
    out = []
    e = ext
    shift = 1
    for g in range(len(POOL_WINDOWS)):
        e = e + pltpu.roll(e, n - shift, 0)
        shift *= 2
        out.append(e[:, g * GROUP_DIM:(g + 1) * GROUP_DIM])
    return out


def _fwd_mix(x, mod6, n1pre, n1post, n2pre, win_g, wout_g, w_spatial, bsp_full, gain, bias, w_pool, b_pool, pool_scale, ts):
    s_len = x.shape[0]
    rs = D_MODEL // N_CHIPS

    def body(x_ref, mod_ref, g1pre_ref, g1post_ref, g2pre_ref, win_ref, wout_ref, ws_ref, bsp_ref, gain_ref,
             bias_ref, wp_ref, bp_ref, ps_ref, z_ref, y_ref, mix_ref, x1_ref, h2_ref, mixed_ref, prev_ref, wfull_ref):
        i = pl.program_id(0)
        _zero_on_first_step(prev_ref)
        _join_w_in_on_first_step(win_ref, wfull_ref)
        xv = x_ref[...]
        r = lax.rsqrt(_rowmean(xv * xv) + EPS)
        hb = ((xv * r) * (g1pre_ref[...] * (1.0 + mod_ref[1:2, :])) + mod_ref[0:1, :]).astype(BF16)
        z_ref[...] = _dot(hb, wfull_ref[...])

        wc = _tril_weights(ws_ref)
        u, _, _, _, _ = _mixer_forward_tile(z_ref[:, :2 * D_A], wc, bsp_ref, gain_ref[...], bias_ref[...], mixed_ref)
        y_ref[:, :D_A] = (u * mixed_ref[...]).astype(BF16)
        zb = z_ref[:, 2 * D_A:]
        sums = _causal_window_sums(jnp.concatenate([prev_ref[...], zb], axis=0))
        prev_ref[...] = zb[ts - HALO:, :]
        _, inv_counts = _window_counts(i * ts, ts)
        for g in range(len(POOL_WINDOWS)):
            lanes = slice(g * GROUP_DIM, (g + 1) * GROUP_DIM)
            diff = sums[g][HALO:, :] * inv_counts[g] - zb[:, lanes]
            lin = _dot(diff.astype(BF16), wp_ref[g].astype(BF16)) + bp_ref[:, lanes]
            y_ref[:, D_A + g * GROUP_DIM:D_A + (g + 1) * GROUP_DIM] = (lin * ps_ref[:, lanes]).astype(BF16)

        mix = None
        for j in range(N_CHIPS):
            part = _dot(y_ref[:, j * rs:(j + 1) * rs], wout_ref[j])
            mix = part if mix is None else mix + part
        mix_ref[...] = mix
        r2 = lax.rsqrt(_rowmean(mix * mix) + EPS)
        x1 = xv + (mix * r2) * (mod_ref[2:3, :] * g1post_ref[...])
        x1_ref[...] = x1
        r3 = lax.rsqrt(_rowmean(x1 * x1) + EPS)
        h2_ref[...] = ((x1 * r3) * (g2pre_ref[...] * (1.0 + mod_ref[4:5, :])) + mod_ref[3:4, :]).astype(BF16)

    vec = _const((1, D_MODEL))
    f32_rows = jax.ShapeDtypeStruct((s_len, D_MODEL), F32)
    bf16_rows = jax.ShapeDtypeStruct((s_len, D_MODEL), BF16)
    return pl.pallas_call(
        body, name="fwd_mix", grid=(s_len // ts,),
        in_specs=[_rows(ts, D_MODEL), _const((N_MOD, D_MODEL)), vec, vec, vec, _VMEM, _VMEM,
                  _const((N_HEADS, CHUNK, CHUNK)), _const((CHUNK, D_A)), _const((1, D_A)), _const((1, D_A)),
                  _const((N_HEADS, GROUP_DIM, GROUP_DIM)), _const((1, D_B)), _const((1, D_B))],
        out_specs=[_rows(ts, D_Z), _rows(ts, D_MODEL), _rows(ts, D_MODEL), _rows(ts, D_MODEL), _rows(ts, D_MODEL)],
        out_shape=[jax.ShapeDtypeStruct((s_len, D_Z), F32), bf16_rows, f32_rows, f32_rows, bf16_rows],
        scratch_shapes=[pltpu.VMEM((ts, D_A), F32), pltpu.VMEM((HALO, D_B), F32), pltpu.VMEM((D_MODEL, D_Z), BF16)],
        compiler_params=_params(),
    )(x, mod6, n1pre, n1post, n2pre, win_g, wout_g, w_spatial, bsp_full, gain, bias, w_pool, b_pool, pool_scale)


def _mixer_forward_tile(za, wc, bsp_ref, gain, bias, mixed_ref):
    ga, dga = _gelu_parts(za)
    u = ga[:, :D_A]
    v = ga[:, D_A:]
    mu = _rowmean(v)
    vc = v - mu
    rstd = lax.rsqrt(_rowmean(vc * vc) + EPS)
    vhat = vc * rstd
    vn = (vhat * gain + bias).astype(BF16)
    ts = za.shape[0]
    for k in range(ts // CHUNK):
        for h in range(N_HEADS):
            blk = vn[k * CHUNK:(k + 1) * CHUNK, h * HEAD_DIM:(h + 1) * HEAD_DIM]
            mixed_ref[k * CHUNK:(k + 1) * CHUNK, h * HEAD_DIM:(h + 1) * HEAD_DIM] = (
                _dot(wc[h], blk) + bsp_ref[:, h * HEAD_DIM:(h + 1) * HEAD_DIM])
    return u, vhat, rstd, vn, dga


def _fwd_fc1(h2, fc1_g, ts):
    s_len = h2.shape[0]
    cs = D_FF // N_CHIPS

    def body(h_ref, w_ref, q_ref):
        hb = h_ref[...]
        for j in range(N_CHIPS):
            p = jnp.maximum(_dot(hb, w_ref[j]), 0.0)
            q_ref[:, j * cs:(j + 1) * cs] = (p * p).astype(BF16)

    return pl.pallas_call(
        body, name="fwd_fc1", grid=(s_len // ts,),
        in_specs=[_rows(ts, D_MODEL), _VMEM],
        out_specs=_rows(ts, D_FF),
        out_shape=jax.ShapeDtypeStruct((s_len, D_FF), BF16),
        compiler_params=_params(),
    )(h2, fc1_g)


def _fwd_fc2_loss(q, x1, target, mod6, n2post, fc2_g, ts):
    s_len = q.shape[0]
    rs = D_FF // N_CHIPS

    def body(q_ref, x1_ref, t_ref, mod_ref, g_ref, w_ref, dy_ref, df_ref, loss_ref, s_ref):
        _zero_on_first_step(loss_ref, s_ref)
        gate_gain = mod_ref[5:6, :] * g_ref[...]
        f = _dot(q_ref[:, 0:rs], w_ref[0])
        for j in range(1, N_CHIPS):
            f = f + _dot(q_ref[:, j * rs:(j + 1) * rs], w_ref[j])
        r4 = lax.rsqrt(_rowmean(f * f) + EPS)
        fh = f * r4
        err = (x1_ref[...] + fh * gate_gain) - t_ref[...]
        loss_ref[...] += 0.5 * jnp.sum(_rowmean(err * err), axis=0, keepdims=True)
        dy = err * (1.0 / D_MODEL)
        dy_ref[...] = dy
        s_ref[...] += _colsum(dy * fh)
        gh = dy * gate_gain
        df_ref[...] = (r4 * (gh - fh * _rowmean(gh * fh))).astype(BF16)

    return pl.pallas_call(
        body, name="fwd_fc2_loss", grid=(s_len // ts,),
        in_specs=[_rows(ts, D_FF), _rows(ts, D_MODEL), _rows(ts, D_MODEL), _const((N_MOD, D_MODEL)),
                  _const((1, D_MODEL)), _VMEM],
        out_specs=[_rows(ts, D_MODEL), _rows(ts, D_MODEL), _const((1, 1)), _const((1, D_MODEL))],
        out_shape=[jax.ShapeDtypeStruct((s_len, D_MODEL), F32), jax.ShapeDtypeStruct((s_len, D_MODEL), BF16),
                   jax.ShapeDtypeStruct((1, 1), F32), jax.ShapeDtypeStruct((1, D_MODEL), F32)],
        compiler_params=_params(),
    )(q, x1, target, mod6, n2post, fc2_g)


def _join_w_in_on_first_step(win_ref, full_ref):
    cs = D_Z // N_CHIPS

    @pl.when(pl.program_id(0) == 0)
    def _():
        for j in range(N_CHIPS):
            full_ref[:, j * cs:(j + 1) * cs] = win_ref[j]


def _zero_on_first_step(*refs):
    @pl.when(pl.program_id(0) == 0)
    def _():
        for ref in refs:
            ref[...] = jnp.zeros_like(ref)


def _on_last_step(fn):
    pl.when(pl.program_id(0) == pl.num_programs(0) - 1)(fn)


def _store_shard_on_last_step(acc_ref, hbm_ref, sem, j):
    _on_last_step(lambda: pltpu.make_async_copy(acc_ref.at[j], hbm_ref.at[j], sem.at[j]).start())


def _wait_stores_on_last_step(*stores):
    def wait_all():
        for acc_ref, hbm_ref, sem in stores:
            for j in range(N_CHIPS):
                pltpu.make_async_copy(acc_ref.at[j], hbm_ref.at[j], sem.at[j]).wait()

    _on_last_step(wait_all)


def _bwd_fc2(df, q, fc2_g, ts):
    s_len = df.shape[0]
    cs = D_FF // N_CHIPS

    def body(df_ref, q_ref, w_ref, dp_ref, dw_hbm, dw_ref, dw_sem):
        _zero_on_first_step(dw_ref)
        dfb = df_ref[...]
        df2 = dfb * 2.0
        for j in range(N_CHIPS):
            qb = q_ref[:, j * cs:(j + 1) * cs]
            dw_ref[j] += _dot_tn(qb, dfb).reshape(2, cs // 2, D_MODEL)
            _store_shard_on_last_step(dw_ref, dw_hbm, dw_sem, j)
            dq2 = _dot_nt(df2, w_ref[j])
            dp_ref[:, j * cs:(j + 1) * cs] = (dq2 * jnp.sqrt(qb.astype(F32))).astype(BF16)
        _wait_stores_on_last_step((dw_ref, dw_hbm, dw_sem))

    dw_shape = (N_CHIPS, 2, cs // 2, D_MODEL)
    return pl.pallas_call(
        body, name="bwd_fc2", grid=(s_len // ts,),
        in_specs=[_rows(ts, D_MODEL), _rows(ts, D_FF), _VMEM],
        out_specs=[_rows(ts, D_FF), _ANY],
        out_shape=[jax.ShapeDtypeStruct((s_len, D_FF), BF16), jax.ShapeDtypeStruct(dw_shape, F32)],
        scratch_shapes=[pltpu.VMEM(dw_shape, F32), pltpu.SemaphoreType.DMA((N_CHIPS,))],
        compiler_params=_params(),
    )(df, q, fc2_g)


def _bwd_fc1_out(dp, dy, x1, mix, h2, ycat, mod6, n2pre, n1post, fc1_g, wout_g, dep, ts):
    s_len = dp.shape[0]
    cs = D_FF // N_CHIPS
    rs = D_MODEL // N_CHIPS

    def body(dp_ref, dy_ref, x1_ref, mix_ref, h2_ref, yc_ref, mod_ref, g2_ref, g1_ref, w1_ref, wo_ref, dep_ref,
             dx1_ref, dyc_ref, dshift2_ref, da2_ref, s1_ref, dw1_hbm, dwo_hbm, dw1_ref, dwo_ref, dw1_sem, dwo_sem):
        _zero_on_first_step(dshift2_ref, da2_ref, s1_ref, dw1_ref, dwo_ref)
        h2b = h2_ref[...]
        dh2 = None
        for j in range(N_CHIPS):
            dpb = dp_ref[:, j * cs:(j + 1) * cs]
            dw1_ref[j] += _dot_tn(h2b, dpb).reshape(2, D_MODEL // 2, cs)
            _store_shard_on_last_step(dw1_ref, dw1_hbm, dw1_sem, j)
            part = _dot_nt(dpb, w1_ref[j])
            dh2 = part if dh2 is None else dh2 + part
        x1 = x1_ref[...]
        r3 = lax.rsqrt(_rowmean(x1 * x1) + EPS)
        xh = x1 * r3
        a2 = g2_ref[...] * (1.0 + mod_ref[4:5, :])
        dshift2_ref[...] += _colsum(dh2)
        da2_ref[...] += _colsum(dh2 * xh)
        dxh = dh2 * a2
        dx1 = dy_ref[...] + r3 * (dxh - xh * _rowmean(dxh * xh))
        dx1_ref[...] = dx1

        mix = mix_ref[...]
        r2 = lax.rsqrt(_rowmean(mix * mix) + EPS)
        mh = mix * r2
        s1_ref[...] += _colsum(dx1 * mh)
        gh = dx1 * (mod_ref[2:3, :] * g1_ref[...])
        dmix = (r2 * (gh - mh * _rowmean(gh * mh))).astype(BF16)
        dwo_ref[...] += _dot_tn(yc_ref[...], dmix).reshape(N_CHIPS, 2, rs // 2, D_MODEL)
        for j in range(N_CHIPS):
            _store_shard_on_last_step(dwo_ref, dwo_hbm, dwo_sem, j)
            dyc_ref[:, j * rs:(j + 1) * rs] = _dot_nt(dmix, wo_ref[j])
        _wait_stores_on_last_step((dw1_ref, dw1_hbm, dw1_sem), (dwo_ref, dwo_hbm, dwo_sem))

    vec = jax.ShapeDtypeStruct((1, D_MODEL), F32)
    dw1_shape = (N_CHIPS, 2, D_MODEL // 2, cs)
    dwo_shape = (N_CHIPS, 2, rs // 2, D_MODEL)
    return pl.pallas_call(
        body, name="bwd_fc1_out", grid=(s_len // ts,),
        in_specs=[_rows(ts, D_FF), _rows(ts, D_MODEL), _rows(ts, D_MODEL), _rows(ts, D_MODEL), _rows(ts, D_MODEL),
                  _rows(ts, D_MODEL), _const((N_MOD, D_MODEL)), _const((1, D_MODEL)), _const((1, D_MODEL)), _VMEM,
                  _VMEM, _ANY],
        out_specs=[_rows(ts, D_MODEL), _rows(ts, D_MODEL)] + [_const((1, D_MODEL))] * 3 + [_ANY, _ANY],
        out_shape=[jax.ShapeDtypeStruct((s_len, D_MODEL), F32), jax.ShapeDtypeStruct((s_len, D_MODEL), F32),
                   vec, vec, vec, jax.ShapeDtypeStruct(dw1_shape, F32), jax.ShapeDtypeStruct(dwo_shape, F32)],
        scratch_shapes=[pltpu.VMEM(dw1_shape, F32), pltpu.VMEM(dwo_shape, F32), pltpu.SemaphoreType.DMA((N_CHIPS,)),
                        pltpu.SemaphoreType.DMA((N_CHIPS,))],
        compiler_params=_params(),
    )(dp, dy, x1, mix, h2, ycat, mod6, n2pre, n1post, fc1_g, wout_g, dep)


def _mixer_bwd(z, dyc, w_spatial, bsp_full, gain, bias, w_pool, b_pool, pool_scale, dep, ts):
    s_len = z.shape[0]
    nb = ts // HALO
    last = s_len // HALO - 1
    te = ts + HALO

    def body(z_ref, zprev_ref, znext_ref, dyc_ref, dynext_ref, ws_ref, bsp_ref, gain_ref, bias_ref, wp_ref, bp_ref,
             ps_ref, dep_ref, dz_ref, dws_ref, dbsp_ref, dgain_ref, dbias_ref, dwp_ref, dbp_ref, dps_ref, mixed_ref,
             dvn_ref):
        i = pl.program_id(0)

        @pl.when(i == 0)
        def _():
            for ref in (dws_ref, dbsp_ref, dgain_ref, dbias_ref, dwp_ref, dbp_ref, dps_ref):
                ref[...] = jnp.zeros_like(ref)

        wc = _tril_weights(ws_ref)
        gain = gain_ref[...]
        u, vhat, rstd, vn, dga = _mixer_forward_tile(z_ref[:, :2 * D_A], wc, bsp_ref, gain, bias_ref[...], mixed_ref)
        dya = dyc_ref[:, :D_A]
        du = dya * mixed_ref[...]
        dmixed = dya * u
        dmb = dmixed.astype(BF16)
        dm_sum = dmixed[0:CHUNK, :]
        for k in range(1, ts // CHUNK):
            dm_sum = dm_sum + dmixed[k * CHUNK:(k + 1) * CHUNK, :]
        r_idx = lax.broadcasted_iota(jnp.int32, (CHUNK, CHUNK), 0)
        s_idx = lax.broadcasted_iota(jnp.int32, (CHUNK, CHUNK), 1)
        causal = (s_idx <= r_idx).astype(F32)
        for h in range(N_HEADS):
            lanes = slice(h * HEAD_DIM, (h + 1) * HEAD_DIM)
            dbsp_ref[h] += jnp.sum(dm_sum[:, lanes], axis=1, keepdims=True)
            acc = None
            for k in range(ts // CHUNK):
                rows = slice(k * CHUNK, (k + 1) * CHUNK)
                t = _dot_nt(dmb[rows, lanes], vn[rows, lanes])
                acc = t if acc is None else acc + t
                dvn_ref[rows, lanes] = _dot_tn(wc[h], dmb[rows, lanes])
            dws_ref[h] += acc * causal
        dvn = dvn_ref[...]
        dgain_ref[...] += _colsum(dvn * vhat)
        dbias_ref[...] += _colsum(dvn)
        dvh = dvn * gain
        dv = rstd * (dvh - _rowmean(dvh) - vhat * _rowmean(dvh * vhat))
        dz_ref[:, :D_A] = (du * dga[:, :D_A]).astype(BF16)
        dz_ref[:, D_A:2 * D_A] = (dv * dga[:, D_A:]).astype(BF16)

        zb = z_ref[:, 2 * D_A:]
        prev = jnp.where(i == 0, 0.0, zprev_ref[...])
        zb_ext = jnp.concatenate([zb, znext_ref[...]], axis=0)
        sums = _causal_window_sums(jnp.concatenate([prev, zb_ext], axis=0))
        pos, inv_counts = _window_counts(i * ts, te)
        dyb_ext = jnp.concatenate([dyc_ref[:, D_A:], dynext_ref[...]], axis=0)
        dlin_ext = dyb_ext * ps_ref[...]
        dbp_ref[...] += _colsum(dlin_ext[:ts, :])
        scaled = []
        ddiffs = []
        lins = []
        for g in range(len(POOL_WINDOWS)):
            lanes = slice(g * GROUP_DIM, (g + 1) * GROUP_DIM)
            diff = (sums[g][HALO:, :] * inv_counts[g] - zb_ext[:, lanes]).astype(BF16)
            wpb = wp_ref[g].astype(BF16)
            dlb = dlin_ext[:, lanes].astype(BF16)
            lins.append(_dot(diff[:ts, :], wpb) + bp_ref[:, lanes])
            dwp_ref[g] += _dot_tn(diff[:ts, :], dlb[:ts, :])
            dd = _dot_nt(dlb, wpb)
            ddiffs.append(dd)
            scaled.append(jnp.where(pos < float(s_len), dd * inv_counts[g], 0.0))
        dps_ref[...] += _colsum(dyb_ext[:ts, :] * jnp.concatenate(lins, axis=1))
        back = _anticausal_window_sums(jnp.concatenate(scaled, axis=1))
        for g in range(len(POOL_WINDOWS)):
            dz_ref[:, 2 * D_A + g * GROUP_DIM:2 * D_A + (g + 1) * GROUP_DIM] = (
                back[g][:ts, :] - ddiffs[g][:ts, :]).astype(BF16)

    sq = jax.ShapeDtypeStruct((N_HEADS, CHUNK, CHUNK), F32)
    vec = jax.ShapeDtypeStruct((1, D_A), F32)
    return pl.pallas_call(
        body, name="mixer_bwd", grid=(s_len // ts,),
        in_specs=[_rows(ts, D_Z),
                  pl.BlockSpec((HALO, D_B), lambda i: (jnp.maximum(i * nb - 1, 0), 2)),
                  pl.BlockSpec((HALO, D_B), lambda i: (jnp.minimum((i + 1) * nb, last), 2)),
                  _rows(ts, D_MODEL),
                  pl.BlockSpec((HALO, D_B), lambda i: (jnp.minimum((i + 1) * nb, last), 1)),
                  _const((N_HEADS, CHUNK, CHUNK)), _const((CHUNK, D_A)), _const((1, D_A)), _const((1, D_A)),
                  _const((N_HEADS, GROUP_DIM, GROUP_DIM)), _const((1, D_B)), _const((1, D_B)), _ANY],
        out_specs=[_rows(ts, D_Z), _const((N_HEADS, CHUNK, CHUNK)), _const((N_HEADS, CHUNK, 1)), _const((1, D_A)),
                   _const((1, D_A)), _const((N_HEADS, GROUP_DIM, GROUP_DIM)), _const((1, D_B)), _const((1, D_B))],
        out_shape=[jax.ShapeDtypeStruct((s_len, D_Z), BF16), sq, jax.ShapeDtypeStruct((N_HEADS, CHUNK, 1), F32), vec,
                   vec, sq, vec, vec],
        scratch_shapes=[pltpu.VMEM((ts, D_A), F32), pltpu.VMEM((ts, D_A), F32)],
        compiler_params=_params(),
    )(z, z, z, dyc, dyc, w_spatial, bsp_full, gain, bias, w_pool, b_pool, pool_scale, dep)


def _bwd_in(dz, dx1, x, mod6, n1pre, win_g, dep, ts):
    s_len = x.shape[0]
    cs = D_Z // N_CHIPS

    def body(dz_ref, dx1_ref, x_ref, mod_ref, g_ref, w_ref, dep_ref, gx_ref, dshift_ref, da_ref, dw_hbm, dw_ref,
             wfull_ref, dw_sem):
        _zero_on_first_step(dshift_ref, da_ref, dw_ref)
        _join_w_in_on_first_step(w_ref, wfull_ref)
        xv = x_ref[...]
        r = lax.rsqrt(_rowmean(xv * xv) + EPS)
        xh = xv * r
        h1b = (xh * (g_ref[...] * (1.0 + mod_ref[1:2, :])) + mod_ref[0:1, :]).astype(BF16)
        dzb = dz_ref[...]
        dw = _dot_tn(h1b, dzb)
        for j in range(N_CHIPS):
            dw_ref[j] += dw[:, j * cs:(j + 1) * cs].reshape(2, D_MODEL // 2, cs)
        dh = _dot_nt(dzb, wfull_ref[...])
        a1 = g_ref[...] * (1.0 + mod_ref[1:2, :])
        dshift_ref[...] += _colsum(dh)
        da_ref[...] += _colsum(dh * xh)
        dxh = dh * a1
        gx_ref[...] = dx1_ref[...] + r * (dxh - xh * _rowmean(dxh * xh))
        for j in range(N_CHIPS):
            _store_shard_on_last_step(dw_ref, dw_hbm, dw_sem, j)
        _wait_stores_on_last_step((dw_ref, dw_hbm, dw_sem))

    vec = jax.ShapeDtypeStruct((1, D_MODEL), F32)
    dw_shape = (N_CHIPS, 2, D_MODEL // 2, cs)
    return pl.pallas_call(
        body, name="bwd_in", grid=(s_len // ts,),
        in_specs=[_rows(ts, D_Z), _rows(ts, D_MODEL), _rows(ts, D_MODEL), _const((N_MOD, D_MODEL)),
                  _const((1, D_MODEL)), _VMEM, _ANY],
        out_specs=[_rows(ts, D_MODEL), _const((1, D_MODEL)), _const((1, D_MODEL)), _ANY],
        out_shape=[jax.ShapeDtypeStruct((s_len, D_MODEL), F32), vec, vec, jax.ShapeDtypeStruct(dw_shape, F32)],
        scratch_shapes=[pltpu.VMEM(dw_shape, F32), pltpu.VMEM((D_MODEL, D_Z), BF16),
                        pltpu.SemaphoreType.DMA((N_CHIPS,))],
        compiler_params=_params(),
    )(dz, dx1, x, mod6, n1pre, win_g, dep)


def _adamw_math(w, g, m, v):
    m = ADAM_B1 * m + (1.0 - ADAM_B1) * g
    v = ADAM_B2 * v + (1.0 - ADAM_B2) * (g * g)
    m_hat = m / (1.0 - ADAM_B1 ** ADAM_STEP)
    v_hat = v / (1.0 - ADAM_B2 ** ADAM_STEP)
    delta = -ADAM_LR * (m_hat / (jnp.sqrt(v_hat) + ADAM_EPS) + ADAM_WD * w)
    return delta, m, v


def _adamw(gs, ws, ms, vs, name, steps):
    n = len(ws)

    def body(*refs):
        for i in range(n):
            g_ref, w_ref, m_ref, v_ref = refs[4 * i:4 * i + 4]
            g = g_ref[...]
            d, nm, nv = _adamw_math(w_ref[...], g, m_ref[...], v_ref[...])
            for ref, val in zip(refs[4 * n + 4 * i:4 * n + 4 * i + 4], (g, d, nm, nv)):
                ref[...] = val

    specs = [_rows(w.shape[0] // steps, w.shape[1]) for w in ws]
    out = pl.pallas_call(
        body, name=name, grid=(steps,),
        in_specs=[s for s in specs for _ in range(4)], out_specs=[s for s in specs for _ in range(4)],
        out_shape=[jax.ShapeDtypeStruct(w.shape, F32) for w in ws for _ in range(4)],
        compiler_params=_params(),
    )(*[a for quad in zip(gs, ws, ms, vs) for a in quad])
    return [out[4 * i:4 * i + 4] for i in range(n)]


def _ada_grad_adamw(sc_t, dmod_all, w, m, v, tr):
    rows, cols = w.shape

    def body(s_ref, dm_ref, w_ref, m_ref, v_ref, g_ref, d_ref, nm_ref, nv_ref):
        x, y, _ = _position()
        mine = pl.ds(pl.multiple_of((2 * x + y) * cols, LANES), cols)
        g = s_ref[:, 0:1] * dm_ref[0:1, mine]
        for b in range(1, N_DEV):
            g = g + s_ref[:, b:b + 1] * dm_ref[b:b + 1, mine]
        g_ref[...] = g
        d, nm, nv = _adamw_math(w_ref[...], g, m_ref[...], v_ref[...])
        d_ref[...] = d
        nm_ref[...] = nm
        nv_ref[...] = nv

    spec = _rows(tr, cols)
    shape = jax.ShapeDtypeStruct((rows, cols), F32)
    return pl.pallas_call(
        body, name="ada_grad_adamw", grid=(rows // tr,),
        in_specs=[_rows(tr, N_DEV), _const(dmod_all.shape), spec, spec, spec],
        out_specs=[spec] * 4, out_shape=[shape] * 4, compiler_params=_params(),
    )(sc_t, dmod_all, w, m, v)


def _mod_grads(da1, dshift1, s1, da2, dshift2, s2, mod6, n1pre, n1post, n2pre, n2post):
    def body(da1_ref, ds1_ref, s1_ref, da2_ref, ds2_ref, s2_ref, mod_ref, n1_ref, p1_ref, n2_ref, p2_ref, dmod_ref,
             dn_ref):
        dmod_ref[0:1, :] = ds1_ref[...]
        dmod_ref[1:2, :] = da1_ref[...] * n1_ref[...]
        dmod_ref[2:3, :] = s1_ref[...] * p1_ref[...]
        dmod_ref[3:4, :] = ds2_ref[...]
        dmod_ref[4:5, :] = da2_ref[...] * n2_ref[...]
        dmod_ref[5:6, :] = s2_ref[...] * p2_ref[...]
        dn_ref[0:1, :] = da1_ref[...] * (1.0 + mod_ref[1:2, :])
        dn_ref[1:2, :] = s1_ref[...] * mod_ref[2:3, :]
        dn_ref[2:3, :] = da2_ref[...] * (1.0 + mod_ref[4:5, :])
        dn_ref[3:4, :] = s2_ref[...] * mod_ref[5:6, :]

    return pl.pallas_call(
        body, name="mod_grads",
        out_shape=[jax.ShapeDtypeStruct((N_MOD, D_MODEL), F32), jax.ShapeDtypeStruct((4, D_MODEL), F32)],
    )(da1, dshift1, s1, da2, dshift2, s2, mod6, n1pre, n1post, n2pre, n2post)


def _position():
    x, y, c = lax.axis_index("x"), lax.axis_index("y"), lax.axis_index("c")
    return x, y, c


def _flip(v, bit):
    return 1 - v if bit else v


def _peer(x, y, c, k):
    return (_flip(x, k & 4), _flip(y, k & 2), _flip(c, k & 1))


def _remote(src, dst, send_sem, recv_sem, device):
    return pltpu.make_async_remote_copy(src_ref=src, dst_ref=dst, send_sem=send_sem, recv_sem=recv_sem,
                                        device_id=device, device_id_type=MESH)


def _mod_exchange(c_row, w_ada_shard, b_ada_row, ws, early):
    cs = w_ada_shard.shape[1]
    n = len(ws)

    def body(c_ref, w_hbm, b_ref, *refs):
        shards, (mod_ref, sc_ref), zones = refs[:n], refs[n:n + 2], refs[n + 2:2 * n + 2]
        rows_ref, w_ref, w_sem, send1, recv1, send2, recv2 = refs[2 * n + 2:2 * n + 9]
        wide, narrow = refs[2 * n + 9:3 * n + 9], refs[3 * n + 9:4 * n + 9]
        cast_load, cast_store = refs[4 * n + 9:]
        x, y, c = _position()
        me = 4 * x + 2 * y + c
        chip = 2 * x + y
        w_load = pltpu.make_async_copy(w_hbm, w_ref, w_sem)
        w_load.start()
        loads = [pltpu.make_async_copy(shards[i], wide[i], cast_load.at[i]) for i in range(n)]
        stores = [pltpu.make_async_copy(narrow[i], zones[i].at[chip], cast_store.at[i]) for i in range(n)]

        def cast(i):
            loads[i].wait()
            narrow[i][...] = wide[i][...].astype(BF16)
            stores[i].start()

        for i in range(early):
            loads[i].start()
        cv = c_ref[...]
        sc_ref[me] = cv * jax.nn.sigmoid(cv)
        gather = [_remote(sc_ref.at[me], sc_ref.at[me], send1.at[k - 1], recv1.at[k - 1], _peer(x, y, c, k))
                  for k in range(1, N_DEV)]
        for cp in gather:
            cp.start()
        for i in range(early):
            cast(i)
        for i in range(early, n):
            loads[i].start()
        for k in range(1, N_DEV):
            px, py, pc = _peer(x, y, c, k)
            src = 4 * px + 2 * py + pc
            _remote(sc_ref.at[src], sc_ref.at[src], send1.at[k - 1], recv1.at[k - 1], (px, py, pc)).wait_recv()
        for cp in gather:
            cp.wait_send()
        sc_all = jnp.concatenate([sc_ref[b] for b in range(N_DEV)], axis=0)
        w_load.wait()
        part = _dot(sc_all.astype(BF16), w_ref[...].astype(BF16))
        part = part + b_ref[:, pl.ds(pl.multiple_of(chip * cs, LANES), cs)]
        for b in range(N_DEV):
            rows_ref[b] = part[b:b + 1, :]
        mod_ref[chip] = rows_ref[me]
        hand = []
        for k in (2, 4, 6):
            px, py, _ = _peer(x, y, c, k)
            hand.append(_remote(rows_ref.at[4 * px + 2 * py + c], mod_ref.at[chip], send2.at[k // 2 - 1],
                                recv2.at[k // 2 - 1], (px, py, c)))
        for cp in hand:
            cp.start()
        for i in range(early, n):
            cast(i)
        for k in (2, 4, 6):
            px, py, _ = _peer(x, y, c, k)
            pchip = 2 * px + py
            _remote(rows_ref.at[me], mod_ref.at[pchip], send2.at[k // 2 - 1], recv2.at[k // 2 - 1],
                    (px, py, c)).wait_recv()
        for cp in hand:
            cp.wait_send()
        for cp in stores:
            cp.wait()

    out = pl.pallas_call(
        body, name="mod_exchange",
        in_specs=[_VMEM, _ANY, _VMEM] + [_ANY] * n, out_specs=[_VMEM, _VMEM] + [_ANY] * n,
        out_shape=[jax.ShapeDtypeStruct((N_CHIPS, 1, cs), F32), jax.ShapeDtypeStruct((N_DEV, 1, D_MODEL), F32)]
        + [jax.ShapeDtypeStruct((N_CHIPS,) + w.shape, BF16) for w in ws],
        scratch_shapes=[pltpu.VMEM((N_DEV, 1, cs), F32), pltpu.VMEM(w_ada_shard.shape, F32), pltpu.SemaphoreType.DMA,
                        pltpu.SemaphoreType.DMA((N_DEV - 1,)),
                        pltpu.SemaphoreType.DMA((N_DEV - 1,)), pltpu.SemaphoreType.DMA((N_CHIPS - 1,)),
                        pltpu.SemaphoreType.DMA((N_CHIPS - 1,))]
        + [pltpu.VMEM(w.shape, F32) for w in ws] + [pltpu.VMEM(w.shape, BF16) for w in ws]
        + [pltpu.SemaphoreType.DMA((n,))] * 2,
        compiler_params=pltpu.CompilerParams(vmem_limit_bytes=VMEM_LIMIT),
    )(c_row, w_ada_shard, b_ada_row, *ws)
    return out[0], out[1], list(out[2:])


_HBM = pl.BlockSpec(memory_space=pltpu.HBM)
_SEM = pl.BlockSpec(memory_space=pltpu.SEMAPHORE)
_EFFECT = pltpu.SideEffectType.DATAFLOW_SIDE_EFFECTING
_CHIP_HOPS = (2, 4, 6)


def _in_hbm(a):
    return pltpu.with_memory_space_constraint(a, pltpu.HBM)


def _sems3():
    return pltpu.SemaphoreType.DMA((len(_CHIP_HOPS),))


def _ag_start(lands, after, name):
    n = len(lands)

    def body(*refs):
        zones = refs[:n]
        sends, recvs = refs[n + 1:2 * n + 1], refs[2 * n + 1:3 * n + 1]
        x, y, c = _position()
        chip = 2 * x + y
        for i in range(n):
            half = zones[i].shape[1] // 2
            mine = zones[i].at[chip, pl.ds(c * half, half)]
            for s, k in enumerate(_CHIP_HOPS):
                px, py, _ = _peer(x, y, c, k)
                _remote(mine, mine, sends[i].at[s], recvs[i].at[s], (px, py, c)).start()

    out = pl.pallas_call(
        body, name=name,
        in_specs=[_HBM] * n + [_ANY],
        out_specs=[_SEM] * (2 * n) + [_HBM] * n,
        out_shape=[_sems3()] * (2 * n) + [pltpu.HBM(z.shape, BF16) for z in lands],
        input_output_aliases={i: 2 * n + i for i in range(n)},
        compiler_params=pltpu.CompilerParams(has_side_effects=_EFFECT),
    )(*[_in_hbm(z) for z in lands], after)
    return [(out[2 * n + i], out[i], out[n + i]) for i in range(n)]


def _ag_pass(group, after, name):
    n = len(group)

    def body(*refs):
        zones = refs[:n]
        sends, recvs = refs[n:2 * n], refs[2 * n:3 * n]
        fsends, frecvs = refs[4 * n + 1:5 * n + 1], refs[5 * n + 1:6 * n + 1]
        x, y, c = _position()
        chip = 2 * x + y
        for i in range(n):
            half = zones[i].shape[1] // 2
            rows = pl.ds(c * half, half)
            for s, k in enumerate(_CHIP_HOPS):
                px, py, _ = _peer(x, y, c, k)
                landed = zones[i].at[2 * px + py, rows]
                _remote(landed, landed, sends[i].at[s], recvs[i].at[s], (px, py, c)).wait_recv()
                _remote(landed, landed, fsends[i].at[s], frecvs[i].at[s], (x, y, 1 - c)).start()
        for i in range(n):
            half = zones[i].shape[1] // 2
            mine = zones[i].at[chip, pl.ds(c * half, half)]
            for s, k in enumerate(_CHIP_HOPS):
                px, py, _ = _peer(x, y, c, k)
                _remote(mine, mine, sends[i].at[s], recvs[i].at[s], (px, py, c)).wait_send()

    out = pl.pallas_call(
        body, name=name,
        in_specs=[_HBM] * n + [_SEM] * (2 * n) + [_ANY],
        out_specs=[_HBM] * n + [_SEM] * (2 * n),
        out_shape=[pltpu.HBM(g[0].shape, BF16) for g in group] + [_sems3()] * (2 * n),
        input_output_aliases={i: i for i in range(n)},
        compiler_params=pltpu.CompilerParams(has_side_effects=_EFFECT),
    )(*[g[0] for g in group], *[g[1] for g in group], *[g[2] for g in group], after)
    return [(out[i], out[n + i], out[2 * n + i]) for i in range(n)]


def _ag_done(group, name):
    n = len(group)

    def body(*refs):
        lands = refs[:n]
        fsends, frecvs = refs[n:2 * n], refs[2 * n:3 * n]
        x, y, c = _position()
        for i in range(n):
            half = lands[i].shape[1] // 2
            for s, k in enumerate(_CHIP_HOPS):
                px, py, _ = _peer(x, y, c, k)
                sent = lands[i].at[2 * px + py, pl.ds(c * half, half)]
                got = lands[i].at[2 * px + py, pl.ds((1 - c) * half, half)]
                cp = _remote(sent, got, fsends[i].at[s], frecvs[i].at[s], (x, y, 1 - c))
                cp.wait_recv()
                cp.wait_send()

    out = pl.pallas_call(
        body, name=name,
        in_specs=[_HBM] * n + [_SEM] * (2 * n),
        out_specs=[_HBM] * n,
        out_shape=[pltpu.HBM(g[0].shape, BF16) for g in group],
        input_output_aliases={i: i for i in range(n)},
        compiler_params=pltpu.CompilerParams(has_side_effects=_EFFECT),
    )(*[g[0] for g in group], *[g[1] for g in group], *[g[2] for g in group])
    return list(out)


def _small_spread_wait(slots, sends, recvs, afters):
    def body(z_ref, sends, recvs, *rest):
        x, y, c = _position()
        mine = z_ref.at[2 * x + y]
        for s, k in enumerate(_CHIP_HOPS):
            px, py, _ = _peer(x, y, c, k)
            cp = _remote(mine, z_ref.at[2 * px + py], sends.at[s], recvs.at[s], (px, py, c))
            cp.wait_recv()
            cp.wait_send()

    return pl.pallas_call(
        body, name="small_spread_wait",
        in_specs=[_HBM, _SEM, _SEM] + [_ANY] * len(afters), out_specs=_HBM, out_shape=pltpu.HBM(slots.shape, F32),
        input_output_aliases={0: 0},
        compiler_params=pltpu.CompilerParams(has_side_effects=_EFFECT),
    )(slots, sends, recvs, *afters)


def _sibling_sum(pgs, name, small=None):
    n = len(pgs)
    k = 0 if small is None else 1
    units = [(i, j) for i in range(n) for j in range(N_CHIPS)]

    def body(*refs):
        refs = list(refs)
        take = lambda count: [refs.pop(0) for _ in range(count)]
        ins, small_in = take(n), take(k)
        qbs, owns, slots_out = take(n), take(n), take(k)
        mine, other, stage, got = take(n), take(n), take(n), take(n)
        load_a, load_b, send, recv, keep_wire, keep_own = take(6)
        x, y, c = _position()
        chip = 2 * x + y
        if k:
            sib_ref, pair_send, pair_recv, keep_small = take(4)
            pair = _remote(small_in[0], sib_ref, pair_send, pair_recv, (x, y, 1 - c))
            pair.start()
        loads_a = [pltpu.make_async_copy(ins[i].at[j, 1 - c], other[i].at[j], load_a.at[u])
                   for u, (i, j) in enumerate(units)]
        loads_b = [pltpu.make_async_copy(ins[i].at[j, c], mine[i].at[j], load_b.at[u])
                   for u, (i, j) in enumerate(units)]
        for cp in loads_a + loads_b:
            cp.start()
        sent = []
        for u, (i, j) in enumerate(units):
            loads_a[u].wait()
            stage[i][j] = other[i][j].astype(BF16)
            cp = _remote(stage[i].at[j], got[i].at[j], send.at[u], recv.at[u], (x, y, 1 - c))
            cp.start()
            sent.append(cp)
        stores = []
        for u, (i, j) in enumerate(units):
            loads_b[u].wait()
            sent[u].wait_recv()
            q = mine[i][j] + got[i][j].astype(F32)
            mine[i][j] = q
            got[i][j] = q.astype(BF16)
            stores.append(pltpu.make_async_copy(got[i].at[j], qbs[i].at[j], keep_wire.at[u]))
            stores[-1].start()
        for i in range(n):
            stores.append(pltpu.make_async_copy(mine[i].at[chip], owns[i], keep_own.at[i]))
            stores[-1].start()
        if k:
            pair.wait()
            sib_ref[...] = small_in[0][...] + sib_ref[...]
            stores.append(pltpu.make_async_copy(sib_ref, slots_out[0].at[chip], keep_small))
            stores[-1].start()
        for cp in sent:
            cp.wait_send()
        for cp in stores:
            cp.wait()

    wire = [(N_CHIPS,) + p.shape[2:] for p in pgs]
    extra_out, extra_scratch = [], []
    if k:
        extra_out = [jax.ShapeDtypeStruct((N_CHIPS,) + small.shape, F32)]
        extra_scratch = [pltpu.VMEM(small.shape, F32)] + [pltpu.SemaphoreType.DMA] * 3
    out = pl.pallas_call(
        body, name=name, in_specs=[_ANY] * n + [_VMEM] * k, out_specs=[_ANY] * (2 * n + k),
        out_shape=[jax.ShapeDtypeStruct(w, BF16) for w in wire] + [jax.ShapeDtypeStruct(w[1:], F32) for w in wire]
        + extra_out,
        scratch_shapes=[pltpu.VMEM(w, F32) for w in wire] * 2 + [pltpu.VMEM(w, BF16) for w in wire] * 2
        + [pltpu.SemaphoreType.DMA((len(units),))] * 5 + [pltpu.SemaphoreType.DMA((n,))] + extra_scratch,
        compiler_params=pltpu.CompilerParams(vmem_limit_bytes=VMEM_LIMIT),
    )(*pgs, *([small] if k else []))
    return list(out[:n]), list(out[n:2 * n]), list(out[2 * n:])


def _rs_start(qbs, name, small=None):
    n = len(qbs)
    k = 0 if small is None else 1

    def body(*refs):
        outs, inboxes, slots = refs[:n], refs[n:2 * n], refs[2 * n:2 * n + k]
        sems = refs[2 * n + k:4 * n + 3 * k]
        sends, recvs = sems[:n], sems[n:2 * n]
        x, y, c = _position()
        chip = 2 * x + y
        if k:
            mine = slots[0].at[chip]
            for s, hop in enumerate(_CHIP_HOPS):
                px, py, _ = _peer(x, y, c, hop)
                _remote(mine, mine, sems[2 * n].at[s], sems[2 * n + 1].at[s], (px, py, c)).start()
        for i in range(n):
            for s, hop in enumerate(_CHIP_HOPS):
                px, py, _ = _peer(x, y, c, hop)
                _remote(outs[i].at[2 * px + py], inboxes[i].at[chip], sends[i].at[s], recvs[i].at[s], (px, py, c)).start()

    inboxes = [_in_hbm(lax.empty(q.shape, BF16)) for q in qbs]
    n_sems = 2 * n + 2 * k
    out = pl.pallas_call(
        body, name=name,
        in_specs=[_HBM] * (2 * n + k),
        out_specs=[_SEM] * n_sems + [_HBM] * (2 * n + k),
        out_shape=[_sems3()] * n_sems + [pltpu.HBM(q.shape, BF16) for q in qbs] * 2
        + ([pltpu.HBM(small.shape, F32)] if k else []),
        input_output_aliases={i: n_sems + i for i in range(2 * n + k)},
        compiler_params=pltpu.CompilerParams(has_side_effects=_EFFECT),
    )(*[_in_hbm(q) for q in qbs], *inboxes, *([_in_hbm(small)] if k else []))
    states = [(out[n_sems + i], out[n_sems + n + i], out[i], out[n + i]) for i in range(n)]
    return (states, (out[n_sems + 2 * n], out[2 * n], out[2 * n + 1])) if k else states


def _rs_wait(group, after, name):
    n = len(group)

    def body(*refs):
        outs, inboxes = refs[:n], refs[n:2 * n]
        sends, recvs = refs[2 * n:3 * n], refs[3 * n:4 * n]
        x, y, c = _position()
        for i in range(n):
            for s, k in enumerate(_CHIP_HOPS):
                px, py, _ = _peer(x, y, c, k)
                slot = 2 * px + py
                cp = _remote(outs[i].at[slot], inboxes[i].at[slot], sends[i].at[s], recvs[i].at[s], (px, py, c))
                cp.wait_recv()
                cp.wait_send()

    out = pl.pallas_call(
        body, name=name,
        in_specs=[_HBM] * (2 * n) + [_SEM] * (2 * n) + [_ANY],
        out_specs=[_HBM] * n,
        out_shape=[pltpu.HBM(g[1].shape, BF16) for g in group],
        input_output_aliases={n + i: i for i in range(n)},
        compiler_params=pltpu.CompilerParams(has_side_effects=_EFFECT),
    )(*[g[0] for g in group], *[g[1] for g in group], *[g[2] for g in group], *[g[3] for g in group], after)
    return list(out)


def _final_share(inboxes, owns, name):
    n = len(inboxes)
    units = [(i, s) for i in range(n) for s in range(len(_CHIP_HOPS))]

    def body(*refs):
        ins, mine, outs, landed, half = (refs[k * n:(k + 1) * n] for k in range(5))
        load, load_own, keep, send, recv = refs[5 * n:]
        x, y, c = _position()
        loads = []
        for u, (i, s) in enumerate(units):
            px, py, _ = _peer(x, y, c, _CHIP_HOPS[s])
            loads.append(pltpu.make_async_copy(ins[i].at[2 * px + py], landed[i].at[s], load.at[u]))
        loads_own = [pltpu.make_async_copy(mine[i], half[i], load_own.at[i]) for i in range(n)]
        for cp in loads + loads_own:
            cp.start()
        copies = []
        for i in range(n):
            for s in range(len(_CHIP_HOPS)):
                loads[len(_CHIP_HOPS) * i + s].wait()
            loads_own[i].wait()
            total = (landed[i][0].astype(F32) + landed[i][1].astype(F32)) + landed[i][2].astype(F32)
            half[i][...] = total + half[i][...]
            copies.append(pltpu.make_async_copy(half[i], outs[i].at[c], keep.at[i]))
            copies.append(_remote(half[i], outs[i].at[c], send.at[i], recv.at[i], (x, y, 1 - c)))
            for cp in copies[-2:]:
                cp.start()
        for i in range(n):
            theirs = outs[i].at[1 - c]
            _remote(theirs, theirs, send.at[i], recv.at[i], (x, y, 1 - c)).wait_recv()
        for i in range(n):
            copies[2 * i].wait()
            copies[2 * i + 1].wait_send()

    return pl.pallas_call(
        body, name=name, in_specs=[_ANY] * (2 * n), out_specs=[_ANY] * n,
        out_shape=[jax.ShapeDtypeStruct((2,) + o.shape, F32) for o in owns],
        scratch_shapes=[pltpu.VMEM((len(_CHIP_HOPS),) + o.shape, BF16) for o in owns]
        + [pltpu.VMEM(o.shape, F32) for o in owns]
        + [pltpu.SemaphoreType.DMA((len(units),))] + [pltpu.SemaphoreType.DMA((n,))] * 4,
        compiler_params=pltpu.CompilerParams(vmem_limit_bytes=VMEM_LIMIT),
    )(*inboxes, *owns)


_SMALL = (("b_ada", N_MOD * D_MODEL), ("norm1_pre", D_MODEL), ("norm1_post", D_MODEL), ("norm2_pre", D_MODEL),
          ("norm2_post", D_MODEL), ("w_spatial", N_HEADS * CHUNK * CHUNK), ("b_spatial", N_HEADS * CHUNK),
          ("ln_v_gain", D_A), ("ln_v_bias", D_A), ("w_pool", N_HEADS * GROUP_DIM * GROUP_DIM),
          ("b_pool", D_B), ("pool_scale", D_B))
_MOD_ROWS = N_MOD * D_MODEL // LANES


def _packed_rows(size):
    return -(-(size // LANES) // SUBLANES) * SUBLANES


def _pack(parts):
    out = []
    for name, size in _SMALL:
        a = parts[name].reshape(size // LANES, LANES)
        pad = _packed_rows(size) - a.shape[0]
        out.append(jnp.pad(a, ((0, pad), (0, 0))) if pad else a)
    return out


def _small_adamw(slots, ws, ms, vs):
    n = len(_SMALL)
    head = N_DEV * _MOD_ROWS

    def body(*refs):
        s_ref, w, m, v = refs[0], refs[1:1 + n], refs[1 + n:1 + 2 * n], refs[1 + 2 * n:1 + 3 * n]
        outs = refs[1 + 3 * n:1 + 7 * n]
        dmod_ref, loss_ref, t_ref = refs[1 + 7 * n:]
        t_ref[...] = ((s_ref[0] + s_ref[1]) + s_ref[2]) + s_ref[3]
        for r in range(head):
            b, k = divmod(r, _MOD_ROWS)
            dmod_ref[b:b + 1, k * LANES:(k + 1) * LANES] = t_ref[r:r + 1, :]
        loss_ref[...] = t_ref[t_ref.shape[0] - SUBLANES:t_ref.shape[0] - SUBLANES + 1, 0:1]
        row = head
        for i, (_, size) in enumerate(_SMALL):
            if i == 0:
                g = t_ref[0:_MOD_ROWS, :]
                for b in range(1, N_DEV):
                    g = g + t_ref[b * _MOD_ROWS:(b + 1) * _MOD_ROWS, :]
            else:
                g = t_ref[row:row + size // LANES, :]
                row += _packed_rows(size)
            d, nm, nv = _adamw_math(w[i][...], g, m[i][...], v[i][...])
            for ref, val in zip(outs[4 * i:4 * i + 4], (g, d, nm, nv)):
                ref[...] = val

    each = [jax.ShapeDtypeStruct((size // LANES, LANES), F32) for _, size in _SMALL for _ in range(4)]
    out = pl.pallas_call(
        body, name="small_adamw",
        out_shape=each + [jax.ShapeDtypeStruct((N_DEV, N_MOD * D_MODEL), F32), jax.ShapeDtypeStruct((1, 1), F32)],
        scratch_shapes=[pltpu.VMEM(slots.shape[1:], F32)],
        compiler_params=pltpu.CompilerParams(vmem_limit_bytes=VMEM_LIMIT),
    )(slots, *ws, *ms, *vs)
    return [out[4 * i:4 * i + 4] for i in range(n)], out[4 * n], out[4 * n + 1]


def kernel(x, c, w_ada, b_ada, norm1_pre, norm1_post, w_in, w_spatial, b_spatial, ln_v_gain, ln_v_bias, w_pool, b_pool, pool_scale, w_out, norm2_pre, norm2_post, w_fc1, w_fc2, loss_target, m_w_ada, m_b_ada, m_norm1_pre, m_norm1_post, m_w_in, m_w_spatial, m_b_spatial, m_ln_v_gain, m_ln_v_bias, m_w_pool, m_b_pool, m_pool_scale, m_w_out, m_norm2_pre, m_norm2_post, m_w_fc1, m_w_fc2, v_w_ada, v_b_ada, v_norm1_pre, v_norm1_post, v_w_in, v_w_spatial, v_b_spatial, v_ln_v_gain, v_ln_v_bias, v_w_pool, v_b_pool, v_pool_scale, v_w_out, v_norm2_pre, v_norm2_post, v_w_fc1, v_w_fc2):
    weights = dict(w_ada=w_ada, b_ada=b_ada, norm1_pre=norm1_pre, norm1_post=norm1_post, w_in=w_in,
                   w_spatial=w_spatial, b_spatial=b_spatial, ln_v_gain=ln_v_gain, ln_v_bias=ln_v_bias, w_pool=w_pool,
                   b_pool=b_pool, pool_scale=pool_scale, w_out=w_out, norm2_pre=norm2_pre, norm2_post=norm2_post,
                   w_fc1=w_fc1, w_fc2=w_fc2)
    m_old = dict(w_ada=m_w_ada, b_ada=m_b_ada, norm1_pre=m_norm1_pre, norm1_post=m_norm1_post, w_in=m_w_in,
                 w_spatial=m_w_spatial, b_spatial=m_b_spatial, ln_v_gain=m_ln_v_gain, ln_v_bias=m_ln_v_bias,
                 w_pool=m_w_pool, b_pool=m_b_pool, pool_scale=m_pool_scale, w_out=m_w_out, norm2_pre=m_norm2_pre,
                 norm2_post=m_norm2_post, w_fc1=m_w_fc1, w_fc2=m_w_fc2)
    v_old = dict(w_ada=v_w_ada, b_ada=v_b_ada, norm1_pre=v_norm1_pre, norm1_post=v_norm1_post, w_in=v_w_in,
                 w_spatial=v_w_spatial, b_spatial=v_b_spatial, ln_v_gain=v_ln_v_gain, ln_v_bias=v_ln_v_bias,
                 w_pool=v_w_pool, b_pool=v_b_pool, pool_scale=v_pool_scale, w_out=v_w_out, norm2_pre=v_norm2_pre,
                 norm2_post=v_norm2_post, w_fc1=v_w_fc1, w_fc2=v_w_fc2)
    order = ("w_ada", "b_ada", "norm1_pre", "norm1_post", "w_in", "w_spatial", "b_spatial", "ln_v_gain", "ln_v_bias",
             "w_pool", "b_pool", "pool_scale", "w_out", "norm2_pre", "norm2_post", "w_fc1", "w_fc2")
    mx, my, mc = _position()
    me = 4 * mx + 2 * my + mc
    chip = 2 * mx + my
    row = lambda a: a.reshape(1, -1)

    xs, target = x[0], loss_target[0]
    n1pre, n1post, n2pre, n2post = row(norm1_pre), row(norm1_post), row(norm2_pre), row(norm2_post)
    mixer = (w_spatial, jnp.repeat(b_spatial.T, HEAD_DIM, axis=1), row(ln_v_gain), row(ln_v_bias), w_pool,
             row(b_pool), row(pool_scale))
    ts_big, ts_mid = 512, 256

    mod4, sc_all, lands = _mod_exchange(c, w_ada, row(b_ada), [w_in, w_out, w_fc1, w_fc2], 2)
    mod6 = mod4.reshape(N_MOD, D_MODEL)
    ag = _ag_start(lands, mod4, "ag_start")

    win_g, wout_g = _ag_done(_ag_pass([ag[0], ag[1]], ag[2][0], "ag_pass_mix"), "ag_done_mix")
    z, ycat, mix, x1, h2 = _fwd_mix(xs, mod6, n1pre, n1post, n2pre, win_g, wout_g, *mixer, ts_big)
    (fc1_g,) = _ag_done(_ag_pass([ag[2]], h2, "ag_pass_fc1"), "ag_done_fc1")
    q = _fwd_fc1(h2, fc1_g, ts_big)
    (fc2_g,) = _ag_done(_ag_pass([ag[3]], q, "ag_pass_fc2"), "ag_done_fc2")
    dy, df, loss, s2 = _fwd_fc2_loss(q, x1, target, mod6, n2post, fc2_g, ts_big)

    def reduce_start(partials, tag, small=None):
        wire, owns, slots = _sibling_sum(partials, "sibling_sum_" + tag, small)
        return _rs_start(wire, "rs_start_" + tag), owns, slots

    def reduce_finish(state, owns, names, tag, dep):
        inboxes = _rs_wait(state, dep, "rs_wait_" + tag)
        shards = _final_share(inboxes, owns, "final_share_" + tag)
        updates = _adamw([g.reshape(weights[n].shape) for n, g in zip(names, shards)], [weights[n] for n in names],
                         [m_old[n] for n in names], [v_old[n] for n in names], "adamw_" + tag, 4)
        for n, (g, d, nm, nv) in zip(names, updates):
            grads[n], deltas[n], new_m[n], new_v[n] = g, d, nm, nv

    grads, deltas, new_m, new_v = {}, {}, {}, {}
    dp, g_fc2 = _bwd_fc2(df, q, fc2_g, ts_big)
    state_fc2, owns_fc2, _ = reduce_start([g_fc2], "fc2")
    dx1, dyc, dshift2, da2, s1, g_fc1, g_out = _bwd_fc1_out(
        dp, dy, x1, mix, h2, ycat, mod6, n2pre, n1post, fc1_g, wout_g, state_fc2[0][0], ts_mid)
    state_mid, owns_mid, _ = reduce_start([g_fc1, g_out], "mid")
    dz, dws, dbsp, dgain, dbias, dwp, dbp, dps = _mixer_bwd(z, dyc, *mixer, state_mid[0][0], ts_big)
    grad_x, dshift1, da1, g_in = _bwd_in(dz, dx1, xs, mod6, n1pre, win_g, state_mid[0][0], ts_big)
    dmod6, dnorms = _mod_grads(da1, dshift1, s1, da2, dshift2, s2, mod6, n1pre, n1post, n2pre, n2post)

    parts = dict(b_ada=dmod6, norm1_pre=dnorms[0], norm1_post=dnorms[1], norm2_pre=dnorms[2], norm2_post=dnorms[3],
                 w_spatial=dws, b_spatial=dbsp, ln_v_gain=dgain, ln_v_bias=dbias, w_pool=dwp, b_pool=dbp,
                 pool_scale=dps)
    pieces = _pack(parts)
    slots = lax.dynamic_update_slice(jnp.zeros((N_DEV * _MOD_ROWS, LANES), F32), pieces[0], (me * _MOD_ROWS, 0))
    loss_tile = jnp.pad(loss, ((0, SUBLANES - 1), (0, LANES - 1)))
    wire_in, owns_in, pair_sum = _sibling_sum([g_in], "sibling_sum_in",
                                              jnp.concatenate([slots] + pieces[1:] + [loss_tile], axis=0))
    state_in, spread = _rs_start(wire_in, "rs_start_in", pair_sum[0])
    reduce_finish(state_fc2 + state_mid, owns_fc2 + owns_mid, ("w_fc2", "w_fc1", "w_out"), "mlp", state_in[0][0])
    flat = lambda d: [d[n].reshape(size // LANES, LANES) for n, size in _SMALL]
    small_out, dmod_all, loss = _small_adamw(
        _small_spread_wait(*spread, [deltas[n] for n in ("w_fc2", "w_fc1", "w_out")]), flat(weights), flat(m_old),
        flat(v_old))
    loss = loss.reshape(())
    for (n, _), (g, d, nm, nv) in zip(_SMALL, small_out):
        shape = weights[n].shape
        grads[n], deltas[n], new_m[n], new_v[n] = g.reshape(shape), d.reshape(shape), nm.reshape(shape), nv.reshape(shape)

    sc_t = sc_all.reshape(N_DEV, D_MODEL).T
    grads["w_ada"], deltas["w_ada"], new_m["w_ada"], new_v["w_ada"] = _ada_grad_adamw(
        sc_t, dmod_all, w_ada, m_w_ada, v_w_ada, 256)

    reduce_finish(state_in, owns_in, ("w_in",), "in", deltas["w_ada"])

    return (loss, grad_x[None], *[grads[n] for n in order], *[deltas[n] for n in order],
            *[new_m[n] for n in order], *[new_v[n] for n in order])
```

```python
import jax
import jax.numpy as jnp
from jax import lax
from jax.experimental import pallas as pl
from jax.experimental.pallas import tpu as pltpu

F32 = jnp.float32
BF16 = jnp.bfloat16
MESH = pl.DeviceIdType.MESH

D_MODEL = 1024
D_A = 512
D_B = 512
D_Z = 2 * D_A + D_B
N_HEADS = 4
HEAD_DIM = 128
CHUNK = 128
POOL_WINDOWS = (2, 4, 8, 16)
GROUP_DIM = 128
D_FF = 4096
N_MOD = 6
EPS = 1e-6
HALO = 16
N_CHIPS = 4
N_DEV = 8

ADAM_LR = 0.001
ADAM_B1 = 0.9
ADAM_B2 = 0.999
ADAM_EPS = 1e-08
ADAM_WD = 0.01
ADAM_STEP = 10

VMEM_LIMIT = 56 * 1024 * 1024
LANES = 128
SUBLANES = 8

_VMEM = pl.BlockSpec(memory_space=pltpu.VMEM)
_ANY = pl.BlockSpec(memory_space=pl.ANY)


def _params(n_grid_axes=1):
    return pltpu.CompilerParams(dimension_semantics=("arbitrary",) * n_grid_axes, vmem_limit_bytes=VMEM_LIMIT)


def _rows(ts, width):
    return pl.BlockSpec((ts, width), lambda i: (i, 0))


def _const(shape):
    return pl.BlockSpec(shape, lambda i: (0,) * len(shape))


def _dot(a, b):
    return jnp.dot(a, b, preferred_element_type=F32)


def _dot_nt(a, b):
    return lax.dot_general(a, b, (((1,), (1,)), ((), ())), preferred_element_type=F32)


def _dot_tn(a, b):
    return lax.dot_general(a, b, (((0,), (0,)), ((), ())), preferred_element_type=F32)


def _rowmean(v):
    return jnp.mean(v, axis=-1, keepdims=True)


def _colsum(v):
    return jnp.sum(v, axis=0, keepdims=True)


def _gelu_parts(z):
    k0 = 0.7978845608028654
    k1 = 0.044715
    z2 = z * z
    t = jnp.tanh(z * (k0 + (k0 * k1) * z2))
    u = 0.5 * t + 0.5
    g = z * u
    dg = u + (0.5 * z) * (1.0 - t * t) * (k0 + (3.0 * k0 * k1) * z2)
    return g, dg


def _tril_weights(ws_ref):
    r = lax.broadcasted_iota(jnp.int32, (CHUNK, CHUNK), 0)
    s = lax.broadcasted_iota(jnp.int32, (CHUNK, CHUNK), 1)
    mask = (s <= r).astype(F32)
    return [(ws_ref[h] * mask).astype(BF16) for h in range(N_HEADS)]


def _window_counts(first_row, n_rows):
    pos = (first_row + lax.broadcasted_iota(jnp.int32, (n_rows, 1), 0)).astype(F32)
    return pos, [1.0 / jnp.minimum(pos + 1.0, float(w)) for w in POOL_WINDOWS]


def _causal_window_sums(ext):
    out = []
    e = ext
    shift = 1
    for g in range(len(POOL_WINDOWS)):
        e = e + pltpu.roll(e, shift, 0)
        shift *= 2
        out.append(e[:, g * GROUP_DIM:(g + 1) * GROUP_DIM])
    return out


def _anticausal_window_sums(ext):
    n = ext.shape[0]
    out = []
    e = ext
    shift = 1
    for g in range(len(POOL_WINDOWS)):
        e = e + pltpu.roll(e, n - shift, 0)
        shift *= 2
        out.append(e[:, g * GROUP_DIM:(g + 1) * GROUP_DIM])
    return out


def _fwd_mix(x, mod6, n1pre, n1post, n2pre, win_g, wout_g, w_spatial, bsp_full, gain, bias, w_pool, b_pool, pool_scale, ts):
    s_len = x.shape[0]
    rs = D_MODEL // N_CHIPS

    def body(x_ref, mod_ref, g1pre_ref, g1post_ref, g2pre_ref, win_ref, wout_ref, ws_ref, bsp_ref, gain_ref,
             bias_ref, wp_ref, bp_ref, ps_ref, z_ref, y_ref, mix_ref, x1_ref, h2_ref, mixed_ref, prev_ref, wfull_ref):
        i = pl.program_id(0)
        _zero_on_first_step(prev_ref)
        _join_w_in_on_first_step(win_ref, wfull_ref)
        xv = x_ref[...]
        r = lax.rsqrt(_rowmean(xv * xv) + EPS)
        hb = ((xv * r) * (g1pre_ref[...] * (1.0 + mod_ref[1:2, :])) + mod_ref[0:1, :]).astype(BF16)
        z_ref[...] = _dot(hb, wfull_ref[...])

        wc = _tril_weights(ws_ref)
        u, _, _, _, _ = _mixer_forward_tile(z_ref[:, :2 * D_A], wc, bsp_ref, gain_ref[...], bias_ref[...], mixed_ref)
        y_ref[:, :D_A] = (u * mixed_ref[...]).astype(BF16)
        zb = z_ref[:, 2 * D_A:]
        sums = _causal_window_sums(jnp.concatenate([prev_ref[...], zb], axis=0))
        prev_ref[...] = zb[ts - HALO:, :]
        _, inv_counts = _window_counts(i * ts, ts)
        for g in range(len(POOL_WINDOWS)):
            lanes = slice(g * GROUP_DIM, (g + 1) * GROUP_DIM)
            diff = sums[g][HALO:, :] * inv_counts[g] - zb[:, lanes]
            lin = _dot(diff.astype(BF16), wp_ref[g].astype(BF16)) + bp_ref[:, lanes]
            y_ref[:, D_A + g * GROUP_DIM:D_A + (g + 1) * GROUP_DIM] = (lin * ps_ref[:, lanes]).astype(BF16)

        mix = None
        for j in range(N_CHIPS):
            part = _dot(y_ref[:, j * rs:(j + 1) * rs], wout_ref[j])
            mix = part if mix is None else mix + part
        mix_ref[...] = mix
        r2 = lax.rsqrt(_rowmean(mix * mix) + EPS)
        x1 = xv + (mix * r2) * (mod_ref[2:3, :] * g1post_ref[...])
        x1_ref[...] = x1
        r3 = lax.rsqrt(_rowmean(x1 * x1) + EPS)
        h2_ref[...] = ((x1 * r3) * (g2pre_ref[...] * (1.0 + mod_ref[4:5, :])) + mod_ref[3:4, :]).astype(BF16)

    vec = _const((1, D_MODEL))
    f32_rows = jax.ShapeDtypeStruct((s_len, D_MODEL), F32)
    bf16_rows = jax.ShapeDtypeStruct((s_len, D_MODEL), BF16)
    return pl.pallas_call(
        body, name="fwd_mix", grid=(s_len // ts,),
        in_specs=[_rows(ts, D_MODEL), _const((N_MOD, D_MODEL)), vec, vec, vec, _VMEM, _VMEM,
                  _const((N_HEADS, CHUNK, CHUNK)), _const((CHUNK, D_A)), _const((1, D_A)), _const((1, D_A)),
                  _const((N_HEADS, GROUP_DIM, GROUP_DIM)), _const((1, D_B)), _const((1, D_B))],
        out_specs=[_rows(ts, D_Z), _rows(ts, D_MODEL), _rows(ts, D_MODEL), _rows(ts, D_MODEL), _rows(ts, D_MODEL)],
        out_shape=[jax.ShapeDtypeStruct((s_len, D_Z), F32), bf16_rows, f32_rows, f32_rows, bf16_rows],
        scratch_shapes=[pltpu.VMEM((ts, D_A), F32), pltpu.VMEM((HALO, D_B), F32), pltpu.VMEM((D_MODEL, D_Z), BF16)],
        compiler_params=_params(),
    )(x, mod6, n1pre, n1post, n2pre, win_g, wout_g, w_spatial, bsp_full, gain, bias, w_pool, b_pool, pool_scale)


def _mixer_forward_tile(za, wc, bsp_ref, gain, bias, mixed_ref):
    ga, dga = _gelu_parts(za)
    u = ga[:, :D_A]
    v = ga[:, D_A:]
    mu = _rowmean(v)
    vc = v - mu
    rstd = lax.rsqrt(_rowmean(vc * vc) + EPS)
    vhat = vc * rstd
    vn = (vhat * gain + bias).astype(BF16)
    ts = za.shape[0]
    for k in range(ts // CHUNK):
        for h in range(N_HEADS):
            blk = vn[k * CHUNK:(k + 1) * CHUNK, h * HEAD_DIM:(h + 1) * HEAD_DIM]
            mixed_ref[k * CHUNK:(k + 1) * CHUNK, h * HEAD_DIM:(h + 1) * HEAD_DIM] = (
                _dot(wc[h], blk) + bsp_ref[:, h * HEAD_DIM:(h + 1) * HEAD_DIM])
    return u, vhat, rstd, vn, dga


def _fwd_fc1(h2, fc1_g, ts):
    s_len = h2.shape[0]
    cs = D_FF // N_CHIPS

    def body(h_ref, w_ref, q_ref):
        hb = h_ref[...]
        for j in range(N_CHIPS):
            p = jnp.maximum(_dot(hb, w_ref[j]), 0.0)
            q_ref[:, j * cs:(j + 1) * cs] = (p * p).astype(BF16)

    return pl.pallas_call(
        body, name="fwd_fc1", grid=(s_len // ts,),
        in_specs=[_rows(ts, D_MODEL), _VMEM],
        out_specs=_rows(ts, D_FF),
        out_shape=jax.ShapeDtypeStruct((s_len, D_FF), BF16),
        compiler_params=_params(),
    )(h2, fc1_g)


def _fwd_fc2_loss(q, x1, target, mod6, n2post, fc2_g, ts):
    s_len = q.shape[0]
    rs = D_FF // N_CHIPS

    def body(q_ref, x1_ref, t_ref, mod_ref, g_ref, w_ref, dy_ref, df_ref, loss_ref, s_ref):
        _zero_on_first_step(loss_ref, s_ref)
        gate_gain = mod_ref[5:6, :] * g_ref[...]
        f = _dot(q_ref[:, 0:rs], w_ref[0])
        for j in range(1, N_CHIPS):
            f = f + _dot(q_ref[:, j * rs:(j + 1) * rs], w_ref[j])
        r4 = lax.rsqrt(_rowmean(f * f) + EPS)
        fh = f * r4
        err = (x1_ref[...] + fh * gate_gain) - t_ref[...]
        loss_ref[...] += 0.5 * jnp.sum(_rowmean(err * err), axis=0, keepdims=True)
        dy = err * (1.0 / D_MODEL)
        dy_ref[...] = dy
        s_ref[...] += _colsum(dy * fh)
        gh = dy * gate_gain
        df_ref[...] = (r4 * (gh - fh * _rowmean(gh * fh))).astype(BF16)

    return pl.pallas_call(
        body, name="fwd_fc2_loss", grid=(s_len // ts,),
        in_specs=[_rows(ts, D_FF), _rows(ts, D_MODEL), _rows(ts, D_MODEL), _const((N_MOD, D_MODEL)),
                  _const((1, D_MODEL)), _VMEM],
        out_specs=[_rows(ts, D_MODEL), _rows(ts, D_MODEL), _const((1, 1)), _const((1, D_MODEL))],
        out_shape=[jax.ShapeDtypeStruct((s_len, D_MODEL), F32), jax.ShapeDtypeStruct((s_len, D_MODEL), BF16),
                   jax.ShapeDtypeStruct((1, 1), F32), jax.ShapeDtypeStruct((1, D_MODEL), F32)],
        compiler_params=_params(),
    )(q, x1, target, mod6, n2post, fc2_g)


def _join_w_in_on_first_step(win_ref, full_ref):
    cs = D_Z // N_CHIPS

    @pl.when(pl.program_id(0) == 0)
    def _():
        for j in range(N_CHIPS):
            full_ref[:, j * cs:(j + 1) * cs] = win_ref[j]


def _zero_on_first_step(*refs):
    @pl.when(pl.program_id(0) == 0)
    def _():
        for ref in refs:
            ref[...] = jnp.zeros_like(ref)


def _on_last_step(fn):
    pl.when(pl.program_id(0) == pl.num_programs(0) - 1)(fn)


def _store_shard_on_last_step(acc_ref, hbm_ref, sem, j):
    _on_last_step(lambda: pltpu.make_async_copy(acc_ref.at[j], hbm_ref.at[j], sem.at[j]).start())


def _wait_stores_on_last_step(*stores):
    def wait_all():
        for acc_ref, hbm_ref, sem in stores:
            for j in range(N_CHIPS):
                pltpu.make_async_copy(acc_ref.at[j], hbm_ref.at[j], sem.at[j]).wait()

    _on_last_step(wait_all)


def _bwd_fc2(df, q, fc2_g, ts):
    s_len = df.shape[0]
    cs = D_FF // N_CHIPS

    def body(df_ref, q_ref, w_ref, dp_ref, dw_hbm, dw_ref, dw_sem):
        _zero_on_first_step(dw_ref)
        dfb = df_ref[...]
        df2 = dfb * 2.0
        for j in range(N_CHIPS):
            qb = q_ref[:, j * cs:(j + 1) * cs]
            dw_ref[j] += _dot_tn(qb, dfb).reshape(2, cs // 2, D_MODEL)
            _store_shard_on_last_step(dw_ref, dw_hbm, dw_sem, j)
            dq2 = _dot_nt(df2, w_ref[j])
            dp_ref[:, j * cs:(j + 1) * cs] = (dq2 * jnp.sqrt(qb.astype(F32))).astype(BF16)
        _wait_stores_on_last_step((dw_ref, dw_hbm, dw_sem))

    dw_shape = (N_CHIPS, 2, cs // 2, D_MODEL)
    return pl.pallas_call(
        body, name="bwd_fc2", grid=(s_len // ts,),
        in_specs=[_rows(ts, D_MODEL), _rows(ts, D_FF), _VMEM],
        out_specs=[_rows(ts, D_FF), _ANY],
        out_shape=[jax.ShapeDtypeStruct((s_len, D_FF), BF16), jax.ShapeDtypeStruct(dw_shape, F32)],
        scratch_shapes=[pltpu.VMEM(dw_shape, F32), pltpu.SemaphoreType.DMA((N_CHIPS,))],
        compiler_params=_params(),
    )(df, q, fc2_g)


def _bwd_fc1_out(dp, dy, x1, mix, h2, ycat, mod6, n2pre, n1post, fc1_g, wout_g, dep, ts):
    s_len = dp.shape[0]
    cs = D_FF // N_CHIPS
    rs = D_MODEL // N_CHIPS

    def body(dp_ref, dy_ref, x1_ref, mix_ref, h2_ref, yc_ref, mod_ref, g2_ref, g1_ref, w1_ref, wo_ref, dep_ref,
             dx1_ref, dyc_ref, dshift2_ref, da2_ref, s1_ref, dw1_hbm, dwo_hbm, dw1_ref, dwo_ref, dw1_sem, dwo_sem):
        _zero_on_first_step(dshift2_ref, da2_ref, s1_ref, dw1_ref, dwo_ref)
        h2b = h2_ref[...]
        dh2 = None
        for j in range(N_CHIPS):
            dpb = dp_ref[:, j * cs:(j + 1) * cs]
            dw1_ref[j] += _dot_tn(h2b, dpb).reshape(2, D_MODEL // 2, cs)
            _store_shard_on_last_step(dw1_ref, dw1_hbm, dw1_sem, j)
            part = _dot_nt(dpb, w1_ref[j])
            dh2 = part if dh2 is None else dh2 + part
        x1 = x1_ref[...]
        r3 = lax.rsqrt(_rowmean(x1 * x1) + EPS)
        xh = x1 * r3
        a2 = g2_ref[...] * (1.0 + mod_ref[4:5, :])
        dshift2_ref[...] += _colsum(dh2)
        da2_ref[...] += _colsum(dh2 * xh)
        dxh = dh2 * a2
        dx1 = dy_ref[...] + r3 * (dxh - xh * _rowmean(dxh * xh))
        dx1_ref[...] = dx1

        mix = mix_ref[...]
        r2 = lax.rsqrt(_rowmean(mix * mix) + EPS)
        mh = mix * r2
        s1_ref[...] += _colsum(dx1 * mh)
        gh = dx1 * (mod_ref[2:3, :] * g1_ref[...])
        dmix = (r2 * (gh - mh * _rowmean(gh * mh))).astype(BF16)
        dwo_ref[...] += _dot_tn(yc_ref[...], dmix).reshape(N_CHIPS, 2, rs // 2, D_MODEL)
        for j in range(N_CHIPS):
            _store_shard_on_last_step(dwo_ref, dwo_hbm, dwo_sem, j)
            dyc_ref[:, j * rs:(j + 1) * rs] = _dot_nt(dmix, wo_ref[j])
        _wait_stores_on_last_step((dw1_ref, dw1_hbm, dw1_sem), (dwo_ref, dwo_hbm, dwo_sem))

    vec = jax.ShapeDtypeStruct((1, D_MODEL), F32)
    dw1_shape = (N_CHIPS, 2, D_MODEL // 2, cs)
    dwo_shape = (N_CHIPS, 2, rs // 2, D_MODEL)
    return pl.pallas_call(
        body, name="bwd_fc1_out", grid=(s_len // ts,),
        in_specs=[_rows(ts, D_FF), _rows(ts, D_MODEL), _rows(ts, D_MODEL), _rows(ts, D_MODEL), _rows(ts, D_MODEL),
                  _rows(ts, D_MODEL), _const((N_MOD, D_MODEL)), _const((1, D_MODEL)), _const((1, D_MODEL)), _VMEM,
                  _VMEM, _ANY],
        out_specs=[_rows(ts, D_MODEL), _rows(ts, D_MODEL)] + [_const((1, D_MODEL))] * 3 + [_ANY, _ANY],
        out_shape=[jax.ShapeDtypeStruct((s_len, D_MODEL), F32), jax.ShapeDtypeStruct((s_len, D_MODEL), F32),
                   vec, vec, vec, jax.ShapeDtypeStruct(dw1_shape, F32), jax.ShapeDtypeStruct(dwo_shape, F32)],
        scratch_shapes=[pltpu.VMEM(dw1_shape, F32), pltpu.VMEM(dwo_shape, F32), pltpu.SemaphoreType.DMA((N_CHIPS,)),
                        pltpu.SemaphoreType.DMA((N_CHIPS,))],
        compiler_params=_params(),
    )(dp, dy, x1, mix, h2, ycat, mod6, n2pre, n1post, fc1_g, wout_g, dep)


def _mixer_bwd(z, dyc, w_spatial, bsp_full, gain, bias, w_pool, b_pool, pool_scale, dep, ts):
    s_len = z.shape[0]
    nb = ts // HALO
    last = s_len // HALO - 1
    te = ts + HALO

    def body(z_ref, zprev_ref, znext_ref, dyc_ref, dynext_ref, ws_ref, bsp_ref, gain_ref, bias_ref, wp_ref, bp_ref,
             ps_ref, dep_ref, dz_ref, dws_ref, dbsp_ref, dgain_ref, dbias_ref, dwp_ref, dbp_ref, dps_ref, mixed_ref,
             dvn_ref):
        i = pl.program_id(0)

        @pl.when(i == 0)
        def _():
            for ref in (dws_ref, dbsp_ref, dgain_ref, dbias_ref, dwp_ref, dbp_ref, dps_ref):
                ref[...] = jnp.zeros_like(ref)

        wc = _tril_weights(ws_ref)
        gain = gain_ref[...]
        u, vhat, rstd, vn, dga = _mixer_forward_tile(z_ref[:, :2 * D_A], wc, bsp_ref, gain, bias_ref[...], mixed_ref)
        dya = dyc_ref[:, :D_A]
        du = dya * mixed_ref[...]
        dmixed = dya * u
        dmb = dmixed.astype(BF16)
        dm_sum = dmixed[0:CHUNK, :]
        for k in range(1, ts // CHUNK):
            dm_sum = dm_sum + dmixed[k * CHUNK:(k + 1) * CHUNK, :]
        r_idx = lax.broadcasted_iota(jnp.int32, (CHUNK, CHUNK), 0)
        s_idx = lax.broadcasted_iota(jnp.int32, (CHUNK, CHUNK), 1)
        causal = (s_idx <= r_idx).astype(F32)
        for h in range(N_HEADS):
            lanes = slice(h * HEAD_DIM, (h + 1) * HEAD_DIM)
            dbsp_ref[h] += jnp.sum(dm_sum[:, lanes], axis=1, keepdims=True)
            acc = None
            for k in range(ts // CHUNK):
                rows = slice(k * CHUNK, (k + 1) * CHUNK)
                t = _dot_nt(dmb[rows, lanes], vn[rows, lanes])
                acc = t if acc is None else acc + t
                dvn_ref[rows, lanes] = _dot_tn(wc[h], dmb[rows, lanes])
            dws_ref[h] += acc * causal
        dvn = dvn_ref[...]
        dgain_ref[...] += _colsum(dvn * vhat)
        dbias_ref[...] += _colsum(dvn)
        dvh = dvn * gain
        dv = rstd * (dvh - _rowmean(dvh) - vhat * _rowmean(dvh * vhat))
        dz_ref[:, :D_A] = (du * dga[:, :D_A]).astype(BF16)
        dz_ref[:, D_A:2 * D_A] = (dv * dga[:, D_A:]).astype(BF16)

        zb = z_ref[:, 2 * D_A:]
        prev = jnp.where(i == 0, 0.0, zprev_ref[...])
        zb_ext = jnp.concatenate([zb, znext_ref[...]], axis=0)
        sums = _causal_window_sums(jnp.concatenate([prev, zb_ext], axis=0))
        pos, inv_counts = _window_counts(i * ts, te)
        dyb_ext = jnp.concatenate([dyc_ref[:, D_A:], dynext_ref[...]], axis=0)
        dlin_ext = dyb_ext * ps_ref[...]
        dbp_ref[...] += _colsum(dlin_ext[:ts, :])
        scaled = []
        ddiffs = []
        lins = []
        for g in range(len(POOL_WINDOWS)):
            lanes = slice(g * GROUP_DIM, (g + 1) * GROUP_DIM)
            diff = (sums[g][HALO:, :] * inv_counts[g] - zb_ext[:, lanes]).astype(BF16)
            wpb = wp_ref[g].astype(BF16)
            dlb = dlin_ext[:, lanes].astype(BF16)
            lins.append(_dot(diff[:ts, :], wpb) + bp_ref[:, lanes])
            dwp_ref[g] += _dot_tn(diff[:ts, :], dlb[:ts, :])
            dd = _dot_nt(dlb, wpb)
            ddiffs.append(dd)
            scaled.append(jnp.where(pos < float(s_len), dd * inv_counts[g], 0.0))
        dps_ref[...] += _colsum(dyb_ext[:ts, :] * jnp.concatenate(lins, axis=1))
        back = _anticausal_window_sums(jnp.concatenate(scaled, axis=1))
        for g in range(len(POOL_WINDOWS)):
            dz_ref[:, 2 * D_A + g * GROUP_DIM:2 * D_A + (g + 1) * GROUP_DIM] = (
                back[g][:ts, :] - ddiffs[g][:ts, :]).astype(BF16)

    sq = jax.ShapeDtypeStruct((N_HEADS, CHUNK, CHUNK), F32)
    vec = jax.ShapeDtypeStruct((1, D_A), F32)
    return pl.pallas_call(
        body, name="mixer_bwd", grid=(s_len // ts,),
        in_specs=[_rows(ts, D_Z),
                  pl.BlockSpec((HALO, D_B), lambda i: (jnp.maximum(i * nb - 1, 0), 2)),
                  pl.BlockSpec((HALO, D_B), lambda i: (jnp.minimum((i + 1) * nb, last), 2)),
                  _rows(ts, D_MODEL),
                  pl.BlockSpec((HALO, D_B), lambda i: (jnp.minimum((i + 1) * nb, last), 1)),
                  _const((N_HEADS, CHUNK, CHUNK)), _const((CHUNK, D_A)), _const((1, D_A)), _const((1, D_A)),
                  _const((N_HEADS, GROUP_DIM, GROUP_DIM)), _const((1, D_B)), _const((1, D_B)), _ANY],
        out_specs=[_rows(ts, D_Z), _const((N_HEADS, CHUNK, CHUNK)), _const((N_HEADS, CHUNK, 1)), _const((1, D_A)),
                   _const((1, D_A)), _const((N_HEADS, GROUP_DIM, GROUP_DIM)), _const((1, D_B)), _const((1, D_B))],
        out_shape=[jax.ShapeDtypeStruct((s_len, D_Z), BF16), sq, jax.ShapeDtypeStruct((N_HEADS, CHUNK, 1), F32), vec,
                   vec, sq, vec, vec],
        scratch_shapes=[pltpu.VMEM((ts, D_A), F32), pltpu.VMEM((ts, D_A), F32)],
        compiler_params=_params(),
    )(z, z, z, dyc, dyc, w_spatial, bsp_full, gain, bias, w_pool, b_pool, pool_scale, dep)


def _bwd_in(dz, dx1, x, mod6, n1pre, win_g, dep, ts):
    s_len = x.shape[0]
    cs = D_Z // N_CHIPS

    def body(dz_ref, dx1_ref, x_ref, mod_ref, g_ref, w_ref, dep_ref, gx_ref, dshift_ref, da_ref, dw_hbm, dw_ref,
             wfull_ref, dw_sem):
        _zero_on_first_step(dshift_ref, da_ref, dw_ref)
        _join_w_in_on_first_step(w_ref, wfull_ref)
        xv = x_ref[...]
        r = lax.rsqrt(_rowmean(xv * xv) + EPS)
        xh = xv * r
        h1b = (xh * (g_ref[...] * (1.0 + mod_ref[1:2, :])) + mod_ref[0:1, :]).astype(BF16)
        dzb = dz_ref[...]
        dw = _dot_tn(h1b, dzb)
        for j in range(N_CHIPS):
            dw_ref[j] += dw[:, j * cs:(j + 1) * cs].reshape(2, D_MODEL // 2, cs)
        dh = _dot_nt(dzb, wfull_ref[...])
        a1 = g_ref[...] * (1.0 + mod_ref[1:2, :])
        dshift_ref[...] += _colsum(dh)
        da_ref[...] += _colsum(dh * xh)
        dxh = dh * a1
        gx_ref[...] = dx1_ref[...] + r * (dxh - xh * _rowmean(dxh * xh))
        for j in range(N_CHIPS):
            _store_shard_on_last_step(dw_ref, dw_hbm, dw_sem, j)
        _wait_stores_on_last_step((dw_ref, dw_hbm, dw_sem))

    vec = jax.ShapeDtypeStruct((1, D_MODEL), F32)
    dw_shape = (N_CHIPS, 2, D_MODEL // 2, cs)
    return pl.pallas_call(
        body, name="bwd_in", grid=(s_len // ts,),
        in_specs=[_rows(ts, D_Z), _rows(ts, D_MODEL), _rows(ts, D_MODEL), _const((N_MOD, D_MODEL)),
                  _const((1, D_MODEL)), _VMEM, _ANY],
        out_specs=[_rows(ts, D_MODEL), _const((1, D_MODEL)), _const((1, D_MODEL)), _ANY],
        out_shape=[jax.ShapeDtypeStruct((s_len, D_MODEL), F32), vec, vec, jax.ShapeDtypeStruct(dw_shape, F32)],
        scratch_shapes=[pltpu.VMEM(dw_shape, F32), pltpu.VMEM((D_MODEL, D_Z), BF16),
                        pltpu.SemaphoreType.DMA((N_CHIPS,))],
        compiler_params=_params(),
    )(dz, dx1, x, mod6, n1pre, win_g, dep)


def _adamw_math(w, g, m, v):
    m = ADAM_B1 * m + (1.0 - ADAM_B1) * g
    v = ADAM_B2 * v + (1.0 - ADAM_B2) * (g * g)
    m_hat = m / (1.0 - ADAM_B1 ** ADAM_STEP)
    v_hat = v / (1.0 - ADAM_B2 ** ADAM_STEP)
    delta = -ADAM_LR * (m_hat / (jnp.sqrt(v_hat) + ADAM_EPS) + ADAM_WD * w)
    return delta, m, v


def _adamw(gs, ws, ms, vs, name, steps):
    n = len(ws)

    def body(*refs):
        for i in range(n):
            g_ref, w_ref, m_ref, v_ref = refs[4 * i:4 * i + 4]
            g = g_ref[...]
            d, nm, nv = _adamw_math(w_ref[...], g, m_ref[...], v_ref[...])
            for ref, val in zip(refs[4 * n + 4 * i:4 * n + 4 * i + 4], (g, d, nm, nv)):
                ref[...] = val

    specs = [_rows(w.shape[0] // steps, w.shape[1]) for w in ws]
    out = pl.pallas_call(
        body, name=name, grid=(steps,),
        in_specs=[s for s in specs for _ in range(4)], out_specs=[s for s in specs for _ in range(4)],
        out_shape=[jax.ShapeDtypeStruct(w.shape, F32) for w in ws for _ in range(4)],
        compiler_params=_params(),
    )(*[a for quad in zip(gs, ws, ms, vs) for a in quad])
    return [out[4 * i:4 * i + 4] for i in range(n)]


def _ada_grad_adamw(sc_t, dmod_all, w, m, v, tr):
    rows, cols = w.shape
    n_steps = rows // tr

    def body(s_ref, dm_ref, w_hbm, m_hbm, v_hbm, g_ref, d_ref, nm_ref, nv_ref, w_ref, m_ref, v_ref, sems):
        i = pl.program_id(0)
        streams = ((w_hbm, w_ref), (m_hbm, m_ref), (v_hbm, v_ref))

        def read(a, step):
            block = pl.ds(step * tr, tr)
            return pltpu.make_async_copy(streams[a][0].at[block], streams[a][1].at[block], sems.at[a, step])

        @pl.when(i == 0)
        def _():
            for step in range(n_steps):
                for a in range(len(streams)):
                    read(a, step).start()

        for a in range(len(streams)):
            read(a, i).wait()
        here = pl.ds(pl.multiple_of(i * tr, tr), tr)
        x, y, _ = _position()
        mine = pl.ds(pl.multiple_of((2 * x + y) * cols, LANES), cols)
        g = s_ref[:, 0:1] * dm_ref[0:1, mine]
        for b in range(1, N_DEV):
            g = g + s_ref[:, b:b + 1] * dm_ref[b:b + 1, mine]
        g_ref[...] = g
        d, nm, nv = _adamw_math(w_ref[here, :], g, m_ref[here, :], v_ref[here, :])
        d_ref[...] = d
        nm_ref[...] = nm
        nv_ref[...] = nv

    spec = _rows(tr, cols)
    shape = jax.ShapeDtypeStruct((rows, cols), F32)
    return pl.pallas_call(
        body, name="ada_grad_adamw", grid=(n_steps,),
        in_specs=[_rows(tr, N_DEV), _const(dmod_all.shape), _ANY, _ANY, _ANY],
        out_specs=[spec] * 4, out_shape=[shape] * 4,
        scratch_shapes=[pltpu.VMEM((rows, cols), F32)] * 3 + [pltpu.SemaphoreType.DMA((3, n_steps))],
        compiler_params=_params(),
    )(sc_t, dmod_all, w, m, v)


def _mod_grads(da1, dshift1, s1, da2, dshift2, s2, mod6, n1pre, n1post, n2pre, n2post):
    def body(da1_ref, ds1_ref, s1_ref, da2_ref, ds2_ref, s2_ref, mod_ref, n1_ref, p1_ref, n2_ref, p2_ref, dmod_ref,
             dn_ref):
        dmod_ref[0:1, :] = ds1_ref[...]
        dmod_ref[1:2, :] = da1_ref[...] * n1_ref[...]
        dmod_ref[2:3, :] = s1_ref[...] * p1_ref[...]
        dmod_ref[3:4, :] = ds2_ref[...]
        dmod_ref[4:5, :] = da2_ref[...] * n2_ref[...]
        dmod_ref[5:6, :] = s2_ref[...] * p2_ref[...]
        dn_ref[0:1, :] = da1_ref[...] * (1.0 + mod_ref[1:2, :])
        dn_ref[1:2, :] = s1_ref[...] * mod_ref[2:3, :]
        dn_ref[2:3, :] = da2_ref[...] * (1.0 + mod_ref[4:5, :])
        dn_ref[3:4, :] = s2_ref[...] * mod_ref[5:6, :]

    return pl.pallas_call(
        body, name="mod_grads",
        out_shape=[jax.ShapeDtypeStruct((N_MOD, D_MODEL), F32), jax.ShapeDtypeStruct((4, D_MODEL), F32)],
    )(da1, dshift1, s1, da2, dshift2, s2, mod6, n1pre, n1post, n2pre, n2post)


def _position():
    x, y, c = lax.axis_index("x"), lax.axis_index("y"), lax.axis_index("c")
    return x, y, c


def _flip(v, bit):
    return 1 - v if bit else v


def _peer(x, y, c, k):
    return (_flip(x, k & 4), _flip(y, k & 2), _flip(c, k & 1))


def _remote(src, dst, send_sem, recv_sem, device):
    return pltpu.make_async_remote_copy(src_ref=src, dst_ref=dst, send_sem=send_sem, recv_sem=recv_sem,
                                        device_id=device, device_id_type=MESH)


def _mod_exchange(c_row, w_ada_shard, b_ada_row, ws, early):
    cs = w_ada_shard.shape[1]
    n = len(ws)

    def body(c_ref, w_hbm, b_ref, *refs):
        shards, (mod_ref, sc_ref), zones = refs[:n], refs[n:n + 2], refs[n + 2:2 * n + 2]
        rows_ref, w_ref, w_sem, send1, recv1, send2, recv2 = refs[2 * n + 2:2 * n + 9]
        wide, narrow = refs[2 * n + 9:3 * n + 9], refs[3 * n + 9:4 * n + 9]
        cast_load, cast_store = refs[4 * n + 9:]
        x, y, c = _position()
        me = 4 * x + 2 * y + c
        chip = 2 * x + y
        w_load = pltpu.make_async_copy(w_hbm, w_ref, w_sem)
        w_load.start()
        loads = [pltpu.make_async_copy(shards[i], wide[i], cast_load.at[i]) for i in range(n)]
        stores = [pltpu.make_async_copy(narrow[i], zones[i].at[chip], cast_store.at[i]) for i in range(n)]

        def cast(i):
            loads[i].wait()
            narrow[i][...] = wide[i][...].astype(BF16)
            stores[i].start()

        for i in range(early):
            loads[i].start()
        cv = c_ref[...]
        sc_ref[me] = cv * jax.nn.sigmoid(cv)
        gather = [_remote(sc_ref.at[me], sc_ref.at[me], send1.at[k - 1], recv1.at[k - 1], _peer(x, y, c, k))
                  for k in range(1, N_DEV)]
        for cp in gather:
            cp.start()
        for i in range(early):
            cast(i)
        for i in range(early, n):
            loads[i].start()
        for k in range(1, N_DEV):
            px, py, pc = _peer(x, y, c, k)
            src = 4 * px + 2 * py + pc
            _remote(sc_ref.at[src], sc_ref.at[src], send1.at[k - 1], recv1.at[k - 1], (px, py, pc)).wait_recv()
        for cp in gather:
            cp.wait_send()
        sc_all = jnp.concatenate([sc_ref[b] for b in range(N_DEV)], axis=0)
        w_load.wait()
        part = _dot(sc_all.astype(BF16), w_ref[...].astype(BF16))
        part = part + b_ref[:, pl.ds(pl.multiple_of(chip * cs, LANES), cs)]
        for b in range(N_DEV):
            rows_ref[b] = part[b:b + 1, :]
        mod_ref[chip] = rows_ref[me]
        hand = []
        for k in (2, 4, 6):
            px, py, _ = _peer(x, y, c, k)
            hand.append(_remote(rows_ref.at[4 * px + 2 * py + c], mod_ref.at[chip], send2.at[k // 2 - 1],
                                recv2.at[k // 2 - 1], (px, py, c)))
        for cp in hand:
            cp.start()
        for i in range(early, n):
            cast(i)
        for k in (2, 4, 6):
            px, py, _ = _peer(x, y, c, k)
            pchip = 2 * px + py
            _remote(rows_ref.at[me], mod_ref.at[pchip], send2.at[k // 2 - 1], recv2.at[k // 2 - 1],
                    (px, py, c)).wait_recv()
        for cp in hand:
            cp.wait_send()
        for cp in stores:
            cp.wait()

    out = pl.pallas_call(
        body, name="mod_exchange",
        in_specs=[_VMEM, _ANY, _VMEM] + [_ANY] * n, out_specs=[_VMEM, _VMEM] + [_ANY] * n,
        out_shape=[jax.ShapeDtypeStruct((N_CHIPS, 1, cs), F32), jax.ShapeDtypeStruct((N_DEV, 1, D_MODEL), F32)]
        + [jax.ShapeDtypeStruct((N_CHIPS,) + w.shape, BF16) for w in ws],
        scratch_shapes=[pltpu.VMEM((N_DEV, 1, cs), F32), pltpu.VMEM(w_ada_shard.shape, F32), pltpu.SemaphoreType.DMA,
                        pltpu.SemaphoreType.DMA((N_DEV - 1,)),
                        pltpu.SemaphoreType.DMA((N_DEV - 1,)), pltpu.SemaphoreType.DMA((N_CHIPS - 1,)),
                        pltpu.SemaphoreType.DMA((N_CHIPS - 1,))]
        + [pltpu.VMEM(w.shape, F32) for w in ws] + [pltpu.VMEM(w.shape, BF16) for w in ws]
        + [pltpu.SemaphoreType.DMA((n,))] * 2,
        compiler_params=pltpu.CompilerParams(vmem_limit_bytes=VMEM_LIMIT),
    )(c_row, w_ada_shard, b_ada_row, *ws)
    return out[0], out[1], list(out[2:])


_HBM = pl.BlockSpec(memory_space=pltpu.HBM)
_SEM = pl.BlockSpec(memory_space=pltpu.SEMAPHORE)
_EFFECT = pltpu.SideEffectType.DATAFLOW_SIDE_EFFECTING
_CHIP_HOPS = (2, 4, 6)


def _in_hbm(a):
    return pltpu.with_memory_space_constraint(a, pltpu.HBM)


def _sems3():
    return pltpu.SemaphoreType.DMA((len(_CHIP_HOPS),))


def _ag_start(lands, after, name):
    n = len(lands)

    def body(*refs):
        zones = refs[:n]
        sends, recvs = refs[n + 1:2 * n + 1], refs[2 * n + 1:3 * n + 1]
        x, y, c = _position()
        chip = 2 * x + y
        for i in range(n):
            half = zones[i].shape[1] // 2
            mine = zones[i].at[chip, pl.ds(c * half, half)]
            for s, k in enumerate(_CHIP_HOPS):
                px, py, _ = _peer(x, y, c, k)
                _remote(mine, mine, sends[i].at[s], recvs[i].at[s], (px, py, c)).start()

    out = pl.pallas_call(
        body, name=name,
        in_specs=[_HBM] * n + [_ANY],
        out_specs=[_SEM] * (2 * n) + [_HBM] * n,
        out_shape=[_sems3()] * (2 * n) + [pltpu.HBM(z.shape, BF16) for z in lands],
        input_output_aliases={i: 2 * n + i for i in range(n)},
        compiler_params=pltpu.CompilerParams(has_side_effects=_EFFECT),
    )(*[_in_hbm(z) for z in lands], after)
    return [(out[2 * n + i], out[i], out[n + i]) for i in range(n)]


def _ag_pass(group, after, name):
    n = len(group)

    def body(*refs):
        zones = refs[:n]
        sends, recvs = refs[n:2 * n], refs[2 * n:3 * n]
        fsends, frecvs = refs[4 * n + 1:5 * n + 1], refs[5 * n + 1:6 * n + 1]
        x, y, c = _position()
        chip = 2 * x + y
        for i in range(n):
            half = zones[i].shape[1] // 2
            rows = pl.ds(c * half, half)
            for s, k in enumerate(_CHIP_HOPS):
                px, py, _ = _peer(x, y, c, k)
                landed = zones[i].at[2 * px + py, rows]
                _remote(landed, landed, sends[i].at[s], recvs[i].at[s], (px, py, c)).wait_recv()
                _remote(landed, landed, fsends[i].at[s], frecvs[i].at[s], (x, y, 1 - c)).start()
        for i in range(n):
            half = zones[i].shape[1] // 2
            mine = zones[i].at[chip, pl.ds(c * half, half)]
            for s, k in enumerate(_CHIP_HOPS):
                px, py, _ = _peer(x, y, c, k)
                _remote(mine, mine, sends[i].at[s], recvs[i].at[s], (px, py, c)).wait_send()

    out = pl.pallas_call(
        body, name=name,
        in_specs=[_HBM] * n + [_SEM] * (2 * n) + [_ANY],
        out_specs=[_HBM] * n + [_SEM] * (2 * n),
        out_shape=[pltpu.HBM(g[0].shape, BF16) for g in group] + [_sems3()] * (2 * n),
        input_output_aliases={i: i for i in range(n)},
        compiler_params=pltpu.CompilerParams(has_side_effects=_EFFECT),
    )(*[g[0] for g in group], *[g[1] for g in group], *[g[2] for g in group], after)
    return [(out[i], out[n + i], out[2 * n + i]) for i in range(n)]


def _ag_done(group, name):
    n = len(group)

    def body(*refs):
        lands = refs[:n]
        fsends, frecvs = refs[n:2 * n], refs[2 * n:3 * n]
        x, y, c = _position()
        for i in range(n):
            half = lands[i].shape[1] // 2
            for s, k in enumerate(_CHIP_HOPS):
                px, py, _ = _peer(x, y, c, k)
                sent = lands[i].at[2 * px + py, pl.ds(c * half, half)]
                got = lands[i].at[2 * px + py, pl.ds((1 - c) * half, half)]
                cp = _remote(sent, got, fsends[i].at[s], frecvs[i].at[s], (x, y, 1 - c))
                cp.wait_recv()
                cp.wait_send()

    out = pl.pallas_call(
        body, name=name,
        in_specs=[_HBM] * n + [_SEM] * (2 * n),
        out_specs=[_HBM] * n,
        out_shape=[pltpu.HBM(g[0].shape, BF16) for g in group],
        input_output_aliases={i: i for i in range(n)},
        compiler_params=pltpu.CompilerParams(has_side_effects=_EFFECT),
    )(*[g[0] for g in group], *[g[1] for g in group], *[g[2] for g in group])
    return list(out)


def _small_spread_wait(slots, sends, recvs, afters):
    def body(z_ref, sends, recvs, *rest):
        x, y, c = _position()
        mine = z_ref.at[2 * x + y]
        for s, k in enumerate(_CHIP_HOPS):
            px, py, _ = _peer(x, y, c, k)
            cp = _remote(mine, z_ref.at[2 * px + py], sends.at[s], recvs.at[s], (px, py, c))
            cp.wait_recv()
            cp.wait_send()

    return pl.pallas_call(
        body, name="small_spread_wait",
        in_specs=[_HBM, _SEM, _SEM] + [_ANY] * len(afters), out_specs=_HBM, out_shape=pltpu.HBM(slots.shape, F32),
        input_output_aliases={0: 0},
        compiler_params=pltpu.CompilerParams(has_side_effects=_EFFECT),
    )(slots, sends, recvs, *afters)


def _sibling_sum(pgs, name, small=None):
    n = len(pgs)
    k = 0 if small is None else 1
    units = [(i, j) for i in range(n) for j in range(N_CHIPS)]

    def body(*refs):
        refs = list(refs)
        take = lambda count: [refs.pop(0) for _ in range(count)]
        ins, small_in = take(n), take(k)
        qbs, owns, slots_out = take(n), take(n), take(k)
        mine, other, stage, got = take(n), take(n), take(n), take(n)
        load_a, load_b, send, recv, keep_wire, keep_own = take(6)
        x, y, c = _position()
        chip = 2 * x + y
        if k:
            sib_ref, pair_send, pair_recv, keep_small = take(4)
            pair = _remote(small_in[0], sib_ref, pair_send, pair_recv, (x, y, 1 - c))
            pair.start()
        loads_a = [pltpu.make_async_copy(ins[i].at[j, 1 - c], other[i].at[j], load_a.at[u])
                   for u, (i, j) in enumerate(units)]
        loads_b = [pltpu.make_async_copy(ins[i].at[j, c], mine[i].at[j], load_b.at[u])
                   for u, (i, j) in enumerate(units)]
        for cp in loads_a + loads_b:
            cp.start()
        sent = []
        for u, (i, j) in enumerate(units):
            loads_a[u].wait()
            stage[i][j] = other[i][j].astype(BF16)
            cp = _remote(stage[i].at[j], got[i].at[j], send.at[u], recv.at[u], (x, y, 1 - c))
            cp.start()
            sent.append(cp)
        stores = []
        for u, (i, j) in enumerate(units):
            loads_b[u].wait()
            sent[u].wait_recv()
            q = mine[i][j] + got[i][j].astype(F32)
            mine[i][j] = q
            got[i][j] = q.astype(BF16)
            stores.append(pltpu.make_async_copy(got[i].at[j], qbs[i].at[j], keep_wire.at[u]))
            stores[-1].start()
        for i in range(n):
            stores.append(pltpu.make_async_copy(mine[i].at[chip], owns[i], keep_own.at[i]))
            stores[-1].start()
        if k:
            pair.wait()
            sib_ref[...] = small_in[0][...] + sib_ref[...]
            stores.append(pltpu.make_async_copy(sib_ref, slots_out[0].at[chip], keep_small))
            stores[-1].start()
        for cp in sent:
            cp.wait_send()
        for cp in stores:
            cp.wait()

    wire = [(N_CHIPS,) + p.shape[2:] for p in pgs]
    extra_out, extra_scratch = [], []
    if k:
        extra_out = [jax.ShapeDtypeStruct((N_CHIPS,) + small.shape, F32)]
        extra_scratch = [pltpu.VMEM(small.shape, F32)] + [pltpu.SemaphoreType.DMA] * 3
    out = pl.pallas_call(
        body, name=name, in_specs=[_ANY] * n + [_VMEM] * k, out_specs=[_ANY] * (2 * n + k),
        out_shape=[jax.ShapeDtypeStruct(w, BF16) for w in wire] + [jax.ShapeDtypeStruct(w[1:], F32) for w in wire]
        + extra_out,
        scratch_shapes=[pltpu.VMEM(w, F32) for w in wire] * 2 + [pltpu.VMEM(w, BF16) for w in wire] * 2
        + [pltpu.SemaphoreType.DMA((len(units),))] * 5 + [pltpu.SemaphoreType.DMA((n,))] + extra_scratch,
        compiler_params=pltpu.CompilerParams(vmem_limit_bytes=VMEM_LIMIT),
    )(*pgs, *([small] if k else []))
    return list(out[:n]), list(out[n:2 * n]), list(out[2 * n:])


def _rs_start(qbs, name, small=None):
    n = len(qbs)
    k = 0 if small is None else 1

    def body(*refs):
        outs, inboxes, slots = refs[:n], refs[n:2 * n], refs[2 * n:2 * n + k]
        sems = refs[2 * n + k:4 * n + 3 * k]
        sends, recvs = sems[:n], sems[n:2 * n]
        x, y, c = _position()
        chip = 2 * x + y
        if k:
            mine = slots[0].at[chip]
            for s, hop in enumerate(_CHIP_HOPS):
                px, py, _ = _peer(x, y, c, hop)
                _remote(mine, mine, sems[2 * n].at[s], sems[2 * n + 1].at[s], (px, py, c)).start()
        for i in range(n):
            for s, hop in enumerate(_CHIP_HOPS):
                px, py, _ = _peer(x, y, c, hop)
                _remote(outs[i].at[2 * px + py], inboxes[i].at[chip], sends[i].at[s], recvs[i].at[s], (px, py, c)).start()

    inboxes = [_in_hbm(lax.empty(q.shape, BF16)) for q in qbs]
    n_sems = 2 * n + 2 * k
    out = pl.pallas_call(
        body, name=name,
        in_specs=[_HBM] * (2 * n + k),
        out_specs=[_SEM] * n_sems + [_HBM] * (2 * n + k),
        out_shape=[_sems3()] * n_sems + [pltpu.HBM(q.shape, BF16) for q in qbs] * 2
        + ([pltpu.HBM(small.shape, F32)] if k else []),
        input_output_aliases={i: n_sems + i for i in range(2 * n + k)},
        compiler_params=pltpu.CompilerParams(has_side_effects=_EFFECT),
    )(*[_in_hbm(q) for q in qbs], *inboxes, *([_in_hbm(small)] if k else []))
    states = [(out[n_sems + i], out[n_sems + n + i], out[i], out[n + i]) for i in range(n)]
    return (states, (out[n_sems + 2 * n], out[2 * n], out[2 * n + 1])) if k else states


def _rs_wait(group, after, name):
    n = len(group)

    def body(*refs):
        outs, inboxes = refs[:n], refs[n:2 * n]
        sends, recvs = refs[2 * n:3 * n], refs[3 * n:4 * n]
        x, y, c = _position()
        for i in range(n):
            for s, k in enumerate(_CHIP_HOPS):
                px, py, _ = _peer(x, y, c, k)
                slot = 2 * px + py
                cp = _remote(outs[i].at[slot], inboxes[i].at[slot], sends[i].at[s], recvs[i].at[s], (px, py, c))
                cp.wait_recv()
                cp.wait_send()

    out = pl.pallas_call(
        body, name=name,
        in_specs=[_HBM] * (2 * n) + [_SEM] * (2 * n) + [_ANY],
        out_specs=[_HBM] * n,
        out_shape=[pltpu.HBM(g[1].shape, BF16) for g in group],
        input_output_aliases={n + i: i for i in range(n)},
        compiler_params=pltpu.CompilerParams(has_side_effects=_EFFECT),
    )(*[g[0] for g in group], *[g[1] for g in group], *[g[2] for g in group], *[g[3] for g in group], after)
    return list(out)


def _final_share(inboxes, owns, name):
    n = len(inboxes)
    units = [(i, s) for i in range(n) for s in range(len(_CHIP_HOPS))]

    def body(*refs):
        ins, mine, outs, landed, half = (refs[k * n:(k + 1) * n] for k in range(5))
        load, load_own, keep, send, recv = refs[5 * n:]
        x, y, c = _position()
        loads = []
        for u, (i, s) in enumerate(units):
            px, py, _ = _peer(x, y, c, _CHIP_HOPS[s])
            loads.append(pltpu.make_async_copy(ins[i].at[2 * px + py], landed[i].at[s], load.at[u]))
        loads_own = [pltpu.make_async_copy(mine[i], half[i], load_own.at[i]) for i in range(n)]
        for cp in loads + loads_own:
            cp.start()
        copies = []
        for i in range(n):
            for s in range(len(_CHIP_HOPS)):
                loads[len(_CHIP_HOPS) * i + s].wait()
            loads_own[i].wait()
            total = (landed[i][0].astype(F32) + landed[i][1].astype(F32)) + landed[i][2].astype(F32)
            half[i][...] = total + half[i][...]
            copies.append(pltpu.make_async_copy(half[i], outs[i].at[c], keep.at[i]))
            copies.append(_remote(half[i], outs[i].at[c], send.at[i], recv.at[i], (x, y, 1 - c)))
            for cp in copies[-2:]:
                cp.start()
        for i in range(n):
            theirs = outs[i].at[1 - c]
            _remote(theirs, theirs, send.at[i], recv.at[i], (x, y, 1 - c)).wait_recv()
        for i in range(n):
            copies[2 * i].wait()
            copies[2 * i + 1].wait_send()

    return pl.pallas_call(
        body, name=name, in_specs=[_ANY] * (2 * n), out_specs=[_ANY] * n,
        out_shape=[jax.ShapeDtypeStruct((2,) + o.shape, F32) for o in owns],
        scratch_shapes=[pltpu.VMEM((len(_CHIP_HOPS),) + o.shape, BF16) for o in owns]
        + [pltpu.VMEM(o.shape, F32) for o in owns]
        + [pltpu.SemaphoreType.DMA((len(units),))] + [pltpu.SemaphoreType.DMA((n,))] * 4,
        compiler_params=pltpu.CompilerParams(vmem_limit_bytes=VMEM_LIMIT),
    )(*inboxes, *owns)


_SMALL = (("b_ada", N_MOD * D_MODEL), ("norm1_pre", D_MODEL), ("norm1_post", D_MODEL), ("norm2_pre", D_MODEL),
          ("norm2_post", D_MODEL), ("w_spatial", N_HEADS * CHUNK * CHUNK), ("b_spatial", N_HEADS * CHUNK),
          ("ln_v_gain", D_A), ("ln_v_bias", D_A), ("w_pool", N_HEADS * GROUP_DIM * GROUP_DIM),
          ("b_pool", D_B), ("pool_scale", D_B))
_MOD_ROWS = N_MOD * D_MODEL // LANES


def _packed_rows(size):
    return -(-(size // LANES) // SUBLANES) * SUBLANES


def _pack(parts):
    out = []
    for name, size in _SMALL:
        a = parts[name].reshape(size // LANES, LANES)
        pad = _packed_rows(size) - a.shape[0]
        out.append(jnp.pad(a, ((0, pad), (0, 0))) if pad else a)
    return out


def _small_adamw(slots, ws, ms, vs):
    n = len(_SMALL)
    head = N_DEV * _MOD_ROWS

    def body(*refs):
        s_ref, w, m, v = refs[0], refs[1:1 + n], refs[1 + n:1 + 2 * n], refs[1 + 2 * n:1 + 3 * n]
        outs = refs[1 + 3 * n:1 + 7 * n]
        dmod_ref, loss_ref, t_ref = refs[1 + 7 * n:]
        t_ref[...] = ((s_ref[0] + s_ref[1]) + s_ref[2]) + s_ref[3]
        for r in range(head):
            b, k = divmod(r, _MOD_ROWS)
            dmod_ref[b:b + 1, k * LANES:(k + 1) * LANES] = t_ref[r:r + 1, :]
        loss_ref[...] = t_ref[t_ref.shape[0] - SUBLANES:t_ref.shape[0] - SUBLANES + 1, 0:1]
        row = head
        for i, (_, size) in enumerate(_SMALL):
            if i == 0:
                g = t_ref[0:_MOD_ROWS, :]
                for b in range(1, N_DEV):
                    g = g + t_ref[b * _MOD_ROWS:(b + 1) * _MOD_ROWS, :]
            else:
                g = t_ref[row:row + size // LANES, :]
                row += _packed_rows(size)
            d, nm, nv = _adamw_math(w[i][...], g, m[i][...], v[i][...])
            for ref, val in zip(outs[4 * i:4 * i + 4], (g, d, nm, nv)):
                ref[...] = val

    each = [jax.ShapeDtypeStruct((size // LANES, LANES), F32) for _, size in _SMALL for _ in range(4)]
    out = pl.pallas_call(
        body, name="small_adamw",
        out_shape=each + [jax.ShapeDtypeStruct((N_DEV, N_MOD * D_MODEL), F32), jax.ShapeDtypeStruct((1, 1), F32)],
        scratch_shapes=[pltpu.VMEM(slots.shape[1:], F32)],
        compiler_params=pltpu.CompilerParams(vmem_limit_bytes=VMEM_LIMIT),
    )(slots, *ws, *ms, *vs)
    return [out[4 * i:4 * i + 4] for i in range(n)], out[4 * n], out[4 * n + 1]


def kernel(x, c, w_ada, b_ada, norm1_pre, norm1_post, w_in, w_spatial, b_spatial, ln_v_gain, ln_v_bias, w_pool, b_pool, pool_scale, w_out, norm2_pre, norm2_post, w_fc1, w_fc2, loss_target, m_w_ada, m_b_ada, m_norm1_pre, m_norm1_post, m_w_in, m_w_spatial, m_b_spatial, m_ln_v_gain, m_ln_v_bias, m_w_pool, m_b_pool, m_pool_scale, m_w_out, m_norm2_pre, m_norm2_post, m_w_fc1, m_w_fc2, v_w_ada, v_b_ada, v_norm1_pre, v_norm1_post, v_w_in, v_w_spatial, v_b_spatial, v_ln_v_gain, v_ln_v_bias, v_w_pool, v_b_pool, v_pool_scale, v_w_out, v_norm2_pre, v_norm2_post, v_w_fc1, v_w_fc2):
    weights = dict(w_ada=w_ada, b_ada=b_ada, norm1_pre=norm1_pre, norm1_post=norm1_post, w_in=w_in,
                   w_spatial=w_spatial, b_spatial=b_spatial, ln_v_gain=ln_v_gain, ln_v_bias=ln_v_bias, w_pool=w_pool,
                   b_pool=b_pool, pool_scale=pool_scale, w_out=w_out, norm2_pre=norm2_pre, norm2_post=norm2_post,
                   w_fc1=w_fc1, w_fc2=w_fc2)
    m_old = dict(w_ada=m_w_ada, b_ada=m_b_ada, norm1_pre=m_norm1_pre, norm1_post=m_norm1_post, w_in=m_w_in,
                 w_spatial=m_w_spatial, b_spatial=m_b_spatial, ln_v_gain=m_ln_v_gain, ln_v_bias=m_ln_v_bias,
                 w_pool=m_w_pool, b_pool=m_b_pool, pool_scale=m_pool_scale, w_out=m_w_out, norm2_pre=m_norm2_pre,
                 norm2_post=m_norm2_post, w_fc1=m_w_fc1, w_fc2=m_w_fc2)
    v_old = dict(w_ada=v_w_ada, b_ada=v_b_ada, norm1_pre=v_norm1_pre, norm1_post=v_norm1_post, w_in=v_w_in,
                 w_spatial=v_w_spatial, b_spatial=v_b_spatial, ln_v_gain=v_ln_v_gain, ln_v_bias=v_ln_v_bias,
                 w_pool=v_w_pool, b_pool=v_b_pool, pool_scale=v_pool_scale, w_out=v_w_out, norm2_pre=v_norm2_pre,
                 norm2_post=v_norm2_post, w_fc1=v_w_fc1, w_fc2=v_w_fc2)
    order = ("w_ada", "b_ada", "norm1_pre", "norm1_post", "w_in", "w_spatial", "b_spatial", "ln_v_gain", "ln_v_bias",
             "w_pool", "b_pool", "pool_scale", "w_out", "norm2_pre", "norm2_post", "w_fc1", "w_fc2")
    mx, my, mc = _position()
    me = 4 * mx + 2 * my + mc
    chip = 2 * mx + my
    row = lambda a: a.reshape(1, -1)

    xs, target = x[0], loss_target[0]
    n1pre, n1post, n2pre, n2post = row(norm1_pre), row(norm1_post), row(norm2_pre), row(norm2_post)
    mixer = (w_spatial, jnp.repeat(b_spatial.T, HEAD_DIM, axis=1), row(ln_v_gain), row(ln_v_bias), w_pool,
             row(b_pool), row(pool_scale))
    ts_big, ts_mid = 512, 256

    mod4, sc_all, lands = _mod_exchange(c, w_ada, row(b_ada), [w_in, w_out, w_fc1, w_fc2], 2)
    mod6 = mod4.reshape(N_MOD, D_MODEL)
    ag = _ag_start(lands, mod4, "ag_start")

    win_g, wout_g = _ag_done(_ag_pass([ag[0], ag[1]], ag[2][0], "ag_pass_mix"), "ag_done_mix")
    z, ycat, mix, x1, h2 = _fwd_mix(xs, mod6, n1pre, n1post, n2pre, win_g, wout_g, *mixer, ts_big)
    (fc1_g,) = _ag_done(_ag_pass([ag[2]], h2, "ag_pass_fc1"), "ag_done_fc1")
    q = _fwd_fc1(h2, fc1_g, ts_big)
    (fc2_g,) = _ag_done(_ag_pass([ag[3]], q, "ag_pass_fc2"), "ag_done_fc2")
    dy, df, loss, s2 = _fwd_fc2_loss(q, x1, target, mod6, n2post, fc2_g, ts_big)

    def reduce_start(partials, tag, small=None):
        wire, owns, slots = _sibling_sum(partials, "sibling_sum_" + tag, small)
        return _rs_start(wire, "rs_start_" + tag), owns, slots

    def reduce_finish(state, owns, names, tag, dep):
        inboxes = _rs_wait(state, dep, "rs_wait_" + tag)
        shards = _final_share(inboxes, owns, "final_share_" + tag)
        updates = _adamw([g.reshape(weights[n].shape) for n, g in zip(names, shards)], [weights[n] for n in names],
                         [m_old[n] for n in names], [v_old[n] for n in names], "adamw_" + tag, 4)
        for n, (g, d, nm, nv) in zip(names, updates):
            grads[n], deltas[n], new_m[n], new_v[n] = g, d, nm, nv

    grads, deltas, new_m, new_v = {}, {}, {}, {}
    dp, g_fc2 = _bwd_fc2(df, q, fc2_g, ts_big)
    state_fc2, owns_fc2, _ = reduce_start([g_fc2], "fc2")
    dx1, dyc, dshift2, da2, s1, g_fc1, g_out = _bwd_fc1_out(
        dp, dy, x1, mix, h2, ycat, mod6, n2pre, n1post, fc1_g, wout_g, state_fc2[0][0], ts_mid)
    state_mid, owns_mid, _ = reduce_start([g_fc1, g_out], "mid")
    dz, dws, dbsp, dgain, dbias, dwp, dbp, dps = _mixer_bwd(z, dyc, *mixer, state_mid[0][0], ts_big)
    grad_x, dshift1, da1, g_in = _bwd_in(dz, dx1, xs, mod6, n1pre, win_g, state_mid[0][0], ts_big)
    dmod6, dnorms = _mod_grads(da1, dshift1, s1, da2, dshift2, s2, mod6, n1pre, n1post, n2pre, n2post)

    parts = dict(b_ada=dmod6, norm1_pre=dnorms[0], norm1_post=dnorms[1], norm2_pre=dnorms[2], norm2_post=dnorms[3],
                 w_spatial=dws, b_spatial=dbsp, ln_v_gain=dgain, ln_v_bias=dbias, w_pool=dwp, b_pool=dbp,
                 pool_scale=dps)
    pieces = _pack(parts)
    slots = lax.dynamic_update_slice(jnp.zeros((N_DEV * _MOD_ROWS, LANES), F32), pieces[0], (me * _MOD_ROWS, 0))
    loss_tile = jnp.pad(loss, ((0, SUBLANES - 1), (0, LANES - 1)))
    wire_in, owns_in, pair_sum = _sibling_sum([g_in], "sibling_sum_in",
                                              jnp.concatenate([slots] + pieces[1:] + [loss_tile], axis=0))
    state_in, spread = _rs_start(wire_in, "rs_start_in", pair_sum[0])
    reduce_finish(state_fc2 + state_mid, owns_fc2 + owns_mid, ("w_fc2", "w_fc1", "w_out"), "mlp", state_in[0][0])
    flat = lambda d: [d[n].reshape(size // LANES, LANES) for n, size in _SMALL]
    small_out, dmod_all, loss = _small_adamw(
        _small_spread_wait(*spread, [deltas[n] for n in ("w_fc2", "w_fc1", "w_out")]), flat(weights), flat(m_old),
        flat(v_old))
    loss = loss.reshape(())
    for (n, _), (g, d, nm, nv) in zip(_SMALL, small_out):
        shape = weights[n].shape
        grads[n], deltas[n], new_m[n], new_v[n] = g.reshape(shape), d.reshape(shape), nm.reshape(shape), nv.reshape(shape)

    sc_t = sc_all.reshape(N_DEV, D_MODEL).T
    grads["w_ada"], deltas["w_ada"], new_m["w_ada"], new_v["w_ada"] = _ada_grad_adamw(
        sc_t, dmod_all, w_ada, m_w_ada, v_w_ada, 256)

    reduce_finish(state_in, owns_in, ("w_in",), "in", deltas["w_ada"])

    return (loss, grad_x[None], *[grads[n] for n in order], *[deltas[n] for n in order],
            *[new_m[n] for n in order], *[new_v[n] for n in order])
```

```python
import jax
import jax.numpy as jnp
from jax import lax
from jax.experimental import pallas as pl
from jax.experimental.pallas import tpu as pltpu

F32 = jnp.float32
BF16 = jnp.bfloat16
MESH = pl.DeviceIdType.MESH

D_MODEL = 1024
D_A = 512
D_B = 512
D_Z = 2 * D_A + D_B
N_HEADS = 4
HEAD_DIM = 128
CHUNK = 128
POOL_WINDOWS = (2, 4, 8, 16)
GROUP_DIM = 128
D_FF = 4096
N_MOD = 6
EPS = 1e-6
HALO = 16
N_CHIPS = 4
N_DEV = 8

ADAM_LR = 0.001
ADAM_B1 = 0.9
ADAM_B2 = 0.999
ADAM_EPS = 1e-08
ADAM_WD = 0.01
ADAM_STEP = 10

VMEM_LIMIT = 56 * 1024 * 1024
LANES = 128
SUBLANES = 8

_VMEM = pl.BlockSpec(memory_space=pltpu.VMEM)
_ANY = pl.BlockSpec(memory_space=pl.ANY)


def _params(n_grid_axes=1):
    return pltpu.CompilerParams(dimension_semantics=("arbitrary",) * n_grid_axes, vmem_limit_bytes=VMEM_LIMIT)


def _rows(ts, width):
    return pl.BlockSpec((ts, width), lambda i: (i, 0))


def _const(shape):
    return pl.BlockSpec(shape, lambda i: (0,) * len(shape))


def _dot(a, b):
    return jnp.dot(a, b, preferred_element_type=F32)


def _dot_nt(a, b):
    return lax.dot_general(a, b, (((1,), (1,)), ((), ())), preferred_element_type=F32)


def _dot_tn(a, b):
    return lax.dot_general(a, b, (((0,), (0,)), ((), ())), preferred_element_type=F32)


def _rowmean(v):
    return jnp.mean(v, axis=-1, keepdims=True)


def _colsum(v):
    return jnp.sum(v, axis=0, keepdims=True)


def _gelu_parts(z):
    k0 = 0.7978845608028654
    k1 = 0.044715
    z2 = z * z
    t = jnp.tanh(z * (k0 + (k0 * k1) * z2))
    u = 0.5 * t + 0.5
    g = z * u
    dg = u + (0.5 * z) * (1.0 - t * t) * (k0 + (3.0 * k0 * k1) * z2)
    return g, dg


def _tril_weights(ws_ref):
    r = lax.broadcasted_iota(jnp.int32, (CHUNK, CHUNK), 0)
    s = lax.broadcasted_iota(jnp.int32, (CHUNK, CHUNK), 1)
    mask = (s <= r).astype(F32)
    return [(ws_ref[h] * mask).astype(BF16) for h in range(N_HEADS)]


def _window_counts(first_row, n_rows):
    pos = (first_row + lax.broadcasted_iota(jnp.int32, (n_rows, 1), 0)).astype(F32)
    return pos, [1.0 / jnp.minimum(pos + 1.0, float(w)) for w in POOL_WINDOWS]


def _causal_window_sums(ext):
    out = []
    e = ext
    shift = 1
    for g in range(len(POOL_WINDOWS)):
        e = e + pltpu.roll(e, shift, 0)
        shift *= 2
        out.append(e[:, g * GROUP_DIM:(g + 1) * GROUP_DIM])
    return out


def _anticausal_window_sums(ext):
    n = ext.shape[0]
    out = []
    e = ext
    shift = 1
    for g in range(len(POOL_WINDOWS)):
        e = e + pltpu.roll(e, n - shift, 0)
        shift *= 2
        out.append(e[:, g * GROUP_DIM:(g + 1) * GROUP_DIM])
    return out


def _fwd_mix(x, mod6, n1pre, n1post, n2pre, win_g, wout_g, w_spatial, bsp_full, gain, bias, w_pool, b_pool, pool_scale, ts):
    s_len = x.shape[0]
    rs = D_MODEL // N_CHIPS

    def body(x_ref, mod_ref, g1pre_ref, g1post_ref, g2pre_ref, win_ref, wout_ref, ws_ref, bsp_ref, gain_ref,
             bias_ref, wp_ref, bp_ref, ps_ref, z_ref, y_ref, mix_ref, x1_ref, h2_ref, mixed_ref, prev_ref, wfull_ref):
        i = pl.program_id(0)
        _zero_on_first_step(prev_ref)
        _join_w_in_on_first_step(win_ref, wfull_ref)
        xv = x_ref[...]
        r = lax.rsqrt(_rowmean(xv * xv) + EPS)
        hb = ((xv * r) * (g1pre_ref[...] * (1.0 + mod_ref[1:2, :])) + mod_ref[0:1, :]).astype(BF16)
        z_ref[...] = _dot(hb, wfull_ref[...])

        wc = _tril_weights(ws_ref)
        u, _, _, _, _ = _mixer_forward_tile(z_ref[:, :2 * D_A], wc, bsp_ref, gain_ref[...], bias_ref[...], mixed_ref)
        y_ref[:, :D_A] = (u * mixed_ref[...]).astype(BF16)
        zb = z_ref[:, 2 * D_A:]
        sums = _causal_window_sums(jnp.concatenate([prev_ref[...], zb], axis=0))
        prev_ref[...] = zb[ts - HALO:, :]
        _, inv_counts = _window_counts(i * ts, ts)
        for g in range(len(POOL_WINDOWS)):
            lanes = slice(g * GROUP_DIM, (g + 1) * GROUP_DIM)
            diff = sums[g][HALO:, :] * inv_counts[g] - zb[:, lanes]
            lin = _dot(diff.astype(BF16), wp_ref[g].astype(BF16)) + bp_ref[:, lanes]
            y_ref[:, D_A + g * GROUP_DIM:D_A + (g + 1) * GROUP_DIM] = (lin * ps_ref[:, lanes]).astype(BF16)

        mix = None
        for j in range(N_CHIPS):
            part = _dot(y_ref[:, j * rs:(j + 1) * rs], wout_ref[j])
            mix = part if mix is None else mix + part
        mix_ref[...] = mix
        r2 = lax.rsqrt(_rowmean(mix * mix) + EPS)
        x1 = xv + (mix * r2) * (mod_ref[2:3, :] * g1post_ref[...])
        x1_ref[...] = x1
        r3 = lax.rsqrt(_rowmean(x1 * x1) + EPS)
        h2_ref[...] = ((x1 * r3) * (g2pre_ref[...] * (1.0 + mod_ref[4:5, :])) + mod_ref[3:4, :]).astype(BF16)

    vec = _const((1, D_MODEL))
    f32_rows = jax.ShapeDtypeStruct((s_len, D_MODEL), F32)
    bf16_rows = jax.ShapeDtypeStruct((s_len, D_MODEL), BF16)
    return pl.pallas_call(
        body, name="fwd_mix", grid=(s_len // ts,),
        in_specs=[_rows(ts, D_MODEL), _const((N_MOD, D_MODEL)), vec, vec, vec, _VMEM, _VMEM,
                  _const((N_HEADS, CHUNK, CHUNK)), _const((CHUNK, D_A)), _const((1, D_A)), _const((1, D_A)),
                  _const((N_HEADS, GROUP_DIM, GROUP_DIM)), _const((1, D_B)), _const((1, D_B))],
        out_specs=[_rows(ts, D_Z), _rows(ts, D_MODEL), _rows(ts, D_MODEL), _rows(ts, D_MODEL), _rows(ts, D_MODEL)],
        out_shape=[jax.ShapeDtypeStruct((s_len, D_Z), F32), bf16_rows, f32_rows, f32_rows, bf16_rows],
        scratch_shapes=[pltpu.VMEM((ts, D_A), F32), pltpu.VMEM((HALO, D_B), F32), pltpu.VMEM((D_MODEL, D_Z), BF16)],
        compiler_params=_params(),
    )(x, mod6, n1pre, n1post, n2pre, win_g, wout_g, w_spatial, bsp_full, gain, bias, w_pool, b_pool, pool_scale)


def _mixer_forward_tile(za, wc, bsp_ref, gain, bias, mixed_ref):
    ga, dga = _gelu_parts(za)
    u = ga[:, :D_A]
    v = ga[:, D_A:]
    mu = _rowmean(v)
    vc = v - mu
    rstd = lax.rsqrt(_rowmean(vc * vc) + EPS)
    vhat = vc * rstd
    vn = (vhat * gain + bias).astype(BF16)
    ts = za.shape[0]
    for k in range(ts // CHUNK):
        for h in range(N_HEADS):
            blk = vn[k * CHUNK:(k + 1) * CHUNK, h * HEAD_DIM:(h + 1) * HEAD_DIM]
            mixed_ref[k * CHUNK:(k + 1) * CHUNK, h * HEAD_DIM:(h + 1) * HEAD_DIM] = (
                _dot(wc[h], blk) + bsp_ref[:, h * HEAD_DIM:(h + 1) * HEAD_DIM])
    return u, vhat, rstd, vn, dga


def _fwd_fc1(h2, fc1_g, ts):
    s_len = h2.shape[0]
    cs = D_FF // N_CHIPS

    def body(h_ref, w_ref, q_ref):
        hb = h_ref[...]
        for j in range(N_CHIPS):
            p = jnp.maximum(_dot(hb, w_ref[j]), 0.0)
            q_ref[:, j * cs:(j + 1) * cs] = (p * p).astype(BF16)

    return pl.pallas_call(
        body, name="fwd_fc1", grid=(s_len // ts,),
        in_specs=[_rows(ts, D_MODEL), _VMEM],
        out_specs=_rows(ts, D_FF),
        out_shape=jax.ShapeDtypeStruct((s_len, D_FF), BF16),
        compiler_params=_params(),
    )(h2, fc1_g)


def _fwd_fc2_loss(q, x1, target, mod6, n2post, fc2_g, ts):
    s_len = q.shape[0]
    rs = D_FF // N_CHIPS

    def body(q_ref, x1_ref, t_ref, mod_ref, g_ref, w_ref, dy_ref, df_ref, loss_ref, s_ref):
        _zero_on_first_step(loss_ref, s_ref)
        gate_gain = mod_ref[5:6, :] * g_ref[...]
        f = _dot(q_ref[:, 0:rs], w_ref[0])
        for j in range(1, N_CHIPS):
            f = f + _dot(q_ref[:, j * rs:(j + 1) * rs], w_ref[j])
        r4 = lax.rsqrt(_rowmean(f * f) + EPS)
        fh = f * r4
        err = (x1_ref[...] + fh * gate_gain) - t_ref[...]
        loss_ref[...] += 0.5 * jnp.sum(_rowmean(err * err), axis=0, keepdims=True)
        dy = err * (1.0 / D_MODEL)
        dy_ref[...] = dy
        s_ref[...] += _colsum(dy * fh)
        gh = dy * gate_gain
        df_ref[...] = (r4 * (gh - fh * _rowmean(gh * fh))).astype(BF16)

    return pl.pallas_call(
        body, name="fwd_fc2_loss", grid=(s_len // ts,),
        in_specs=[_rows(ts, D_FF), _rows(ts, D_MODEL), _rows(ts, D_MODEL), _const((N_MOD, D_MODEL)),
                  _const((1, D_MODEL)), _VMEM],
        out_specs=[_rows(ts, D_MODEL), _rows(ts, D_MODEL), _const((1, 1)), _const((1, D_MODEL))],
        out_shape=[jax.ShapeDtypeStruct((s_len, D_MODEL), F32), jax.ShapeDtypeStruct((s_len, D_MODEL), BF16),
                   jax.ShapeDtypeStruct((1, 1), F32), jax.ShapeDtypeStruct((1, D_MODEL), F32)],
        compiler_params=_params(),
    )(q, x1, target, mod6, n2post, fc2_g)


def _join_w_in_on_first_step(win_ref, full_ref):
    cs = D_Z // N_CHIPS

    @pl.when(pl.program_id(0) == 0)
    def _():
        for j in range(N_CHIPS):
            full_ref[:, j * cs:(j + 1) * cs] = win_ref[j]


def _zero_on_first_step(*refs):
    @pl.when(pl.program_id(0) == 0)
    def _():
        for ref in refs:
            ref[...] = jnp.zeros_like(ref)


def _on_last_step(fn):
    pl.when(pl.program_id(0) == pl.num_programs(0) - 1)(fn)


def _store_shard_on_last_step(acc_ref, hbm_ref, sem, j):
    _on_last_step(lambda: pltpu.make_async_copy(acc_ref.at[j], hbm_ref.at[j], sem.at[j]).start())


def _wait_stores_on_last_step(*stores):
    def wait_all():
        for acc_ref, hbm_ref, sem in stores:
            for j in range(N_CHIPS):
                pltpu.make_async_copy(acc_ref.at[j], hbm_ref.at[j], sem.at[j]).wait()

    _on_last_step(wait_all)


def _bwd_fc2(df, q, fc2_g, ts):
    s_len = df.shape[0]
    cs = D_FF // N_CHIPS

    def body(df_ref, q_ref, w_ref, dp_ref, dw_hbm, dw_ref, dw_sem):
        _zero_on_first_step(dw_ref)
        dfb = df_ref[...]
        df2 = dfb * 2.0
        for j in range(N_CHIPS):
            qb = q_ref[:, j * cs:(j + 1) * cs]
            dw_ref[j] += _dot_tn(qb, dfb).reshape(2, cs // 2, D_MODEL)
            _store_shard_on_last_step(dw_ref, dw_hbm, dw_sem, j)
            dq2 = _dot_nt(df2, w_ref[j])
            dp_ref[:, j * cs:(j + 1) * cs] = (dq2 * jnp.sqrt(qb.astype(F32))).astype(BF16)
        _wait_stores_on_last_step((dw_ref, dw_hbm, dw_sem))

    dw_shape = (N_CHIPS, 2, cs // 2, D_MODEL)
    return pl.pallas_call(
        body, name="bwd_fc2", grid=(s_len // ts,),
        in_specs=[_rows(ts, D_MODEL), _rows(ts, D_FF), _VMEM],
        out_specs=[_rows(ts, D_FF), _ANY],
        out_shape=[jax.ShapeDtypeStruct((s_len, D_FF), BF16), jax.ShapeDtypeStruct(dw_shape, F32)],
        scratch_shapes=[pltpu.VMEM(dw_shape, F32), pltpu.SemaphoreType.DMA((N_CHIPS,))],
        compiler_params=_params(),
    )(df, q, fc2_g)


def _bwd_fc1_out(dp, dy, x1, mix, h2, ycat, mod6, n2pre, n1post, fc1_g, wout_g, dep, ts):
    s_len = dp.shape[0]
    cs = D_FF // N_CHIPS
    rs = D_MODEL // N_CHIPS

    def body(dp_ref, dy_ref, x1_ref, mix_ref, h2_ref, yc_ref, mod_ref, g2_ref, g1_ref, w1_ref, wo_ref, dep_ref,
             dx1_ref, dyc_ref, dshift2_ref, da2_ref, s1_ref, dw1_hbm, dwo_hbm, dw1_ref, dwo_ref, dw1_sem, dwo_sem):
        _zero_on_first_step(dshift2_ref, da2_ref, s1_ref, dw1_ref, dwo_ref)
        h2b = h2_ref[...]
        dh2 = None
        for j in range(N_CHIPS):
            dpb = dp_ref[:, j * cs:(j + 1) * cs]
            dw1_ref[j] += _dot_tn(h2b, dpb).reshape(2, D_MODEL // 2, cs)
            _store_shard_on_last_step(dw1_ref, dw1_hbm, dw1_sem, j)
            part = _dot_nt(dpb, w1_ref[j])
            dh2 = part if dh2 is None else dh2 + part
        x1 = x1_ref[...]
        r3 = lax.rsqrt(_rowmean(x1 * x1) + EPS)
        xh = x1 * r3
        a2 = g2_ref[...] * (1.0 + mod_ref[4:5, :])
        dshift2_ref[...] += _colsum(dh2)
        da2_ref[...] += _colsum(dh2 * xh)
        dxh = dh2 * a2
        dx1 = dy_ref[...] + r3 * (dxh - xh * _rowmean(dxh * xh))
        dx1_ref[...] = dx1

        mix = mix_ref[...]
        r2 = lax.rsqrt(_rowmean(mix * mix) + EPS)
        mh = mix * r2
        s1_ref[...] += _colsum(dx1 * mh)
        gh = dx1 * (mod_ref[2:3, :] * g1_ref[...])
        dmix = (r2 * (gh - mh * _rowmean(gh * mh))).astype(BF16)
        dwo_ref[...] += _dot_tn(yc_ref[...], dmix).reshape(N_CHIPS, 2, rs // 2, D_MODEL)
        for j in range(N_CHIPS):
            _store_shard_on_last_step(dwo_ref, dwo_hbm, dwo_sem, j)
            dyc_ref[:, j * rs:(j + 1) * rs] = _dot_nt(dmix, wo_ref[j])
        _wait_stores_on_last_step((dw1_ref, dw1_hbm, dw1_sem), (dwo_ref, dwo_hbm, dwo_sem))

    vec = jax.ShapeDtypeStruct((1, D_MODEL), F32)
    dw1_shape = (N_CHIPS, 2, D_MODEL // 2, cs)
    dwo_shape = (N_CHIPS, 2, rs // 2, D_MODEL)
    return pl.pallas_call(
        body, name="bwd_fc1_out", grid=(s_len // ts,),
        in_specs=[_rows(ts, D_FF), _rows(ts, D_MODEL), _rows(ts, D_MODEL), _rows(ts, D_MODEL), _rows(ts, D_MODEL),
                  _rows(ts, D_MODEL), _const((N_MOD, D_MODEL)), _const((1, D_MODEL)), _const((1, D_MODEL)), _VMEM,
                  _VMEM, _ANY],
        out_specs=[_rows(ts, D_MODEL), _rows(ts, D_MODEL)] + [_const((1, D_MODEL))] * 3 + [_ANY, _ANY],
        out_shape=[jax.ShapeDtypeStruct((s_len, D_MODEL), F32), jax.ShapeDtypeStruct((s_len, D_MODEL), F32),
                   vec, vec, vec, jax.ShapeDtypeStruct(dw1_shape, F32), jax.ShapeDtypeStruct(dwo_shape, F32)],
        scratch_shapes=[pltpu.VMEM(dw1_shape, F32), pltpu.VMEM(dwo_shape, F32), pltpu.SemaphoreType.DMA((N_CHIPS,)),
                        pltpu.SemaphoreType.DMA((N_CHIPS,))],
        compiler_params=_params(),
    )(dp, dy, x1, mix, h2, ycat, mod6, n2pre, n1post, fc1_g, wout_g, dep)


def _mixer_bwd(z, dyc, w_spatial, bsp_full, gain, bias, w_pool, b_pool, pool_scale, dep, ts):
    s_len = z.shape[0]
    nb = ts // HALO
    last = s_len // HALO - 1
    te = ts + HALO

    def body(z_ref, zprev_ref, znext_ref, dyc_ref, dynext_ref, ws_ref, bsp_ref, gain_ref, bias_ref, wp_ref, bp_ref,
             ps_ref, dep_ref, dz_ref, dws_ref, dbsp_ref, dgain_ref, dbias_ref, dwp_ref, dbp_ref, dps_ref, mixed_ref,
             dvn_ref):
        i = pl.program_id(0)

        @pl.when(i == 0)
        def _():
            for ref in (dws_ref, dbsp_ref, dgain_ref, dbias_ref, dwp_ref, dbp_ref, dps_ref):
                ref[...] = jnp.zeros_like(ref)

        wc = _tril_weights(ws_ref)
        gain = gain_ref[...]
        u, vhat, rstd, vn, dga = _mixer_forward_tile(z_ref[:, :2 * D_A], wc, bsp_ref, gain, bias_ref[...], mixed_ref)
        dya = dyc_ref[:, :D_A]
        du = dya * mixed_ref[...]
        dmixed = dya * u
        dmb = dmixed.astype(BF16)
        dm_sum = dmixed[0:CHUNK, :]
        for k in range(1, ts // CHUNK):
            dm_sum = dm_sum + dmixed[k * CHUNK:(k + 1) * CHUNK, :]
        r_idx = lax.broadcasted_iota(jnp.int32, (CHUNK, CHUNK), 0)
        s_idx = lax.broadcasted_iota(jnp.int32, (CHUNK, CHUNK), 1)
        causal = (s_idx <= r_idx).astype(F32)
        for h in range(N_HEADS):
            lanes = slice(h * HEAD_DIM, (h + 1) * HEAD_DIM)
            dbsp_ref[h] += jnp.sum(dm_sum[:, lanes], axis=1, keepdims=True)
            acc = None
            for k in range(ts // CHUNK):
                rows = slice(k * CHUNK, (k + 1) * CHUNK)
                t = _dot_nt(dmb[rows, lanes], vn[rows, lanes])
                acc = t if acc is None else acc + t
                dvn_ref[rows, lanes] = _dot_tn(wc[h], dmb[rows, lanes])
            dws_ref[h] += acc * causal
        dvn = dvn_ref[...]
        dgain_ref[...] += _colsum(dvn * vhat)
        dbias_ref[...] += _colsum(dvn)
        dvh = dvn * gain
        dv = rstd * (dvh - _rowmean(dvh) - vhat * _rowmean(dvh * vhat))
        dz_ref[:, :D_A] = (du * dga[:, :D_A]).astype(BF16)
        dz_ref[:, D_A:2 * D_A] = (dv * dga[:, D_A:]).astype(BF16)

        zb = z_ref[:, 2 * D_A:]
        prev = jnp.where(i == 0, 0.0, zprev_ref[...])
        zb_ext = jnp.concatenate([zb, znext_ref[...]], axis=0)
        sums = _causal_window_sums(jnp.concatenate([prev, zb_ext], axis=0))
        pos, inv_counts = _window_counts(i * ts, te)
        dyb_ext = jnp.concatenate([dyc_ref[:, D_A:], dynext_ref[...]], axis=0)
        dlin_ext = dyb_ext * ps_ref[...]
        dbp_ref[...] += _colsum(dlin_ext[:ts, :])
        scaled = []
        ddiffs = []
        lins = []
        for g in range(len(POOL_WINDOWS)):
            lanes = slice(g * GROUP_DIM, (g + 1) * GROUP_DIM)
            diff = (sums[g][HALO:, :] * inv_counts[g] - zb_ext[:, lanes]).astype(BF16)
            wpb = wp_ref[g].astype(BF16)
            dlb = dlin_ext[:, lanes].astype(BF16)
            lins.append(_dot(diff[:ts, :], wpb) + bp_ref[:, lanes])
            dwp_ref[g] += _dot_tn(diff[:ts, :], dlb[:ts, :])
            dd = _dot_nt(dlb, wpb)
            ddiffs.append(dd)
            scaled.append(jnp.where(pos < float(s_len), dd * inv_counts[g], 0.0))
        dps_ref[...] += _colsum(dyb_ext[:ts, :] * jnp.concatenate(lins, axis=1))
        back = _anticausal_window_sums(jnp.concatenate(scaled, axis=1))
        for g in range(len(POOL_WINDOWS)):
            dz_ref[:, 2 * D_A + g * GROUP_DIM:2 * D_A + (g + 1) * GROUP_DIM] = (
                back[g][:ts, :] - ddiffs[g][:ts, :]).astype(BF16)

    sq = jax.ShapeDtypeStruct((N_HEADS, CHUNK, CHUNK), F32)
    vec = jax.ShapeDtypeStruct((1, D_A), F32)
    return pl.pallas_call(
        body, name="mixer_bwd", grid=(s_len // ts,),
        in_specs=[_rows(ts, D_Z),
                  pl.BlockSpec((HALO, D_B), lambda i: (jnp.maximum(i * nb - 1, 0), 2)),
                  pl.BlockSpec((HALO, D_B), lambda i: (jnp.minimum((i + 1) * nb, last), 2)),
                  _rows(ts, D_MODEL),
                  pl.BlockSpec((HALO, D_B), lambda i: (jnp.minimum((i + 1) * nb, last), 1)),
                  _const((N_HEADS, CHUNK, CHUNK)), _const((CHUNK, D_A)), _const((1, D_A)), _const((1, D_A)),
                  _const((N_HEADS, GROUP_DIM, GROUP_DIM)), _const((1, D_B)), _const((1, D_B)), _ANY],
        out_specs=[_rows(ts, D_Z), _const((N_HEADS, CHUNK, CHUNK)), _const((N_HEADS, CHUNK, 1)), _const((1, D_A)),
                   _const((1, D_A)), _const((N_HEADS, GROUP_DIM, GROUP_DIM)), _const((1, D_B)), _const((1, D_B))],
        out_shape=[jax.ShapeDtypeStruct((s_len, D_Z), BF16), sq, jax.ShapeDtypeStruct((N_HEADS, CHUNK, 1), F32), vec,
                   vec, sq, vec, vec],
        scratch_shapes=[pltpu.VMEM((ts, D_A), F32), pltpu.VMEM((ts, D_A), F32)],
        compiler_params=_params(),
    )(z, z, z, dyc, dyc, w_spatial, bsp_full, gain, bias, w_pool, b_pool, pool_scale, dep)


def _bwd_in(dz, dx1, x, mod6, n1pre, win_g, dep, ts):
    s_len = x.shape[0]
    cs = D_Z // N_CHIPS

    def body(dz_ref, dx1_ref, x_ref, mod_ref, g_ref, w_ref, dep_ref, gx_ref, dshift_ref, da_ref, dw_hbm, dw_ref,
             wfull_ref, dw_sem):
        _zero_on_first_step(dshift_ref, da_ref, dw_ref)
        _join_w_in_on_first_step(w_ref, wfull_ref)
        xv = x_ref[...]
        r = lax.rsqrt(_rowmean(xv * xv) + EPS)
        xh = xv * r
        h1b = (xh * (g_ref[...] * (1.0 + mod_ref[1:2, :])) + mod_ref[0:1, :]).astype(BF16)
        dzb = dz_ref[...]
        dw = _dot_tn(h1b, dzb)
        for j in range(N_CHIPS):
            dw_ref[j] += dw[:, j * cs:(j + 1) * cs].reshape(2, D_MODEL // 2, cs)
        dh = _dot_nt(dzb, wfull_ref[...])
        a1 = g_ref[...] * (1.0 + mod_ref[1:2, :])
        dshift_ref[...] += _colsum(dh)
        da_ref[...] += _colsum(dh * xh)
        dxh = dh * a1
        gx_ref[...] = dx1_ref[...] + r * (dxh - xh * _rowmean(dxh * xh))
        for j in range(N_CHIPS):
            _store_shard_on_last_step(dw_ref, dw_hbm, dw_sem, j)
        _wait_stores_on_last_step((dw_ref, dw_hbm, dw_sem))

    vec = jax.ShapeDtypeStruct((1, D_MODEL), F32)
    dw_shape = (N_CHIPS, 2, D_MODEL // 2, cs)
    return pl.pallas_call(
        body, name="bwd_in", grid=(s_len // ts,),
        in_specs=[_rows(ts, D_Z), _rows(ts, D_MODEL), _rows(ts, D_MODEL), _const((N_MOD, D_MODEL)),
                  _const((1, D_MODEL)), _VMEM, _ANY],
        out_specs=[_rows(ts, D_MODEL), _const((1, D_MODEL)), _const((1, D_MODEL)), _ANY],
        out_shape=[jax.ShapeDtypeStruct((s_len, D_MODEL), F32), vec, vec, jax.ShapeDtypeStruct(dw_shape, F32)],
        scratch_shapes=[pltpu.VMEM(dw_shape, F32), pltpu.VMEM((D_MODEL, D_Z), BF16),
                        pltpu.SemaphoreType.DMA((N_CHIPS,))],
        compiler_params=_params(),
    )(dz, dx1, x, mod6, n1pre, win_g, dep)


def _adamw_math(w, g, m, v):
    m = ADAM_B1 * m + (1.0 - ADAM_B1) * g
    v = ADAM_B2 * v + (1.0 - ADAM_B2) * (g * g)
    m_hat = m / (1.0 - ADAM_B1 ** ADAM_STEP)
    v_hat = v / (1.0 - ADAM_B2 ** ADAM_STEP)
    delta = -ADAM_LR * (m_hat / (jnp.sqrt(v_hat) + ADAM_EPS) + ADAM_WD * w)
    return delta, m, v


def _adamw(gs, ws, ms, vs, name, steps):
    n = len(ws)

    def body(*refs):
        ins, outs, bufs, sems = refs[:4 * n], refs[4 * n:8 * n], refs[8 * n:11 * n], refs[11 * n]
        step_now = pl.program_id(0)

        def read(i, a, step):
            tr = ws[i].shape[0] // steps
            block = pl.ds(step * tr, tr)
            return pltpu.make_async_copy(ins[4 * i + 1 + a].at[block], bufs[3 * i + a].at[block], sems.at[3 * i + a, step])

        @pl.when(step_now == 0)
        def _():
            for step in range(steps):
                for i in range(n):
                    for a in range(3):
                        read(i, a, step).start()

        for i in range(n):
            for a in range(3):
                read(i, a, step_now).wait()
            tr = ws[i].shape[0] // steps
            here = pl.ds(pl.multiple_of(step_now * tr, tr), tr)
            g = ins[4 * i][...]
            d, nm, nv = _adamw_math(bufs[3 * i][here, :], g, bufs[3 * i + 1][here, :], bufs[3 * i + 2][here, :])
            for ref, val in zip(outs[4 * i:4 * i + 4], (g, d, nm, nv)):
                ref[...] = val

    specs = [_rows(w.shape[0] // steps, w.shape[1]) for w in ws]
    out = pl.pallas_call(
        body, name=name, grid=(steps,),
        in_specs=[s for spec in specs for s in (spec, _ANY, _ANY, _ANY)], out_specs=[s for s in specs for _ in range(4)],
        out_shape=[jax.ShapeDtypeStruct(w.shape, F32) for w in ws for _ in range(4)],
        scratch_shapes=[pltpu.VMEM(w.shape, F32) for w in ws for _ in range(3)]
        + [pltpu.SemaphoreType.DMA((3 * n, steps))],
        compiler_params=_params(),
    )(*[a for quad in zip(gs, ws, ms, vs) for a in quad])
    return [out[4 * i:4 * i + 4] for i in range(n)]


def _ada_grad_adamw(sc_t, dmod_all, w, m, v, tr):
    rows, cols = w.shape
    n_steps = rows // tr

    def body(s_ref, dm_ref, w_hbm, m_hbm, v_hbm, g_ref, d_ref, nm_ref, nv_ref, w_ref, m_ref, v_ref, sems):
        i = pl.program_id(0)
        streams = ((w_hbm, w_ref), (m_hbm, m_ref), (v_hbm, v_ref))

        def read(a, step):
            block = pl.ds(step * tr, tr)
            return pltpu.make_async_copy(streams[a][0].at[block], streams[a][1].at[block], sems.at[a, step])

        @pl.when(i == 0)
        def _():
            for step in range(n_steps):
                for a in range(len(streams)):
                    read(a, step).start()

        for a in range(len(streams)):
            read(a, i).wait()
        here = pl.ds(pl.multiple_of(i * tr, tr), tr)
        x, y, _ = _position()
        mine = pl.ds(pl.multiple_of((2 * x + y) * cols, LANES), cols)
        g = s_ref[:, 0:1] * dm_ref[0:1, mine]
        for b in range(1, N_DEV):
            g = g + s_ref[:, b:b + 1] * dm_ref[b:b + 1, mine]
        g_ref[...] = g
        d, nm, nv = _adamw_math(w_ref[here, :], g, m_ref[here, :], v_ref[here, :])
        d_ref[...] = d
        nm_ref[...] = nm
        nv_ref[...] = nv

    spec = _rows(tr, cols)
    shape = jax.ShapeDtypeStruct((rows, cols), F32)
    return pl.pallas_call(
        body, name="ada_grad_adamw", grid=(n_steps,),
        in_specs=[_rows(tr, N_DEV), _const(dmod_all.shape), _ANY, _ANY, _ANY],
        out_specs=[spec] * 4, out_shape=[shape] * 4,
        scratch_shapes=[pltpu.VMEM((rows, cols), F32)] * 3 + [pltpu.SemaphoreType.DMA((3, n_steps))],
        compiler_params=_params(),
    )(sc_t, dmod_all, w, m, v)


def _mod_grads(da1, dshift1, s1, da2, dshift2, s2, mod6, n1pre, n1post, n2pre, n2post):
    def body(da1_ref, ds1_ref, s1_ref, da2_ref, ds2_ref, s2_ref, mod_ref, n1_ref, p1_ref, n2_ref, p2_ref, dmod_ref,
             dn_ref):
        dmod_ref[0:1, :] = ds1_ref[...]
        dmod_ref[1:2, :] = da1_ref[...] * n1_ref[...]
        dmod_ref[2:3, :] = s1_ref[...] * p1_ref[...]
        dmod_ref[3:4, :] = ds2_ref[...]
        dmod_ref[4:5, :] = da2_ref[...] * n2_ref[...]
        dmod_ref[5:6, :] = s2_ref[...] * p2_ref[...]
        dn_ref[0:1, :] = da1_ref[...] * (1.0 + mod_ref[1:2, :])
        dn_ref[1:2, :] = s1_ref[...] * mod_ref[2:3, :]
        dn_ref[2:3, :] = da2_ref[...] * (1.0 + mod_ref[4:5, :])
        dn_ref[3:4, :] = s2_ref[...] * mod_ref[5:6, :]

    return pl.pallas_call(
        body, name="mod_grads",
        out_shape=[jax.ShapeDtypeStruct((N_MOD, D_MODEL), F32), jax.ShapeDtypeStruct((4, D_MODEL), F32)],
    )(da1, dshift1, s1, da2, dshift2, s2, mod6, n1pre, n1post, n2pre, n2post)


def _position():
    x, y, c = lax.axis_index("x"), lax.axis_index("y"), lax.axis_index("c")
    return x, y, c


def _flip(v, bit):
    return 1 - v if bit else v


def _peer(x, y, c, k):
    return (_flip(x, k & 4), _flip(y, k & 2), _flip(c, k & 1))


def _remote(src, dst, send_sem, recv_sem, device):
    return pltpu.make_async_remote_copy(src_ref=src, dst_ref=dst, send_sem=send_sem, recv_sem=recv_sem,
                                        device_id=device, device_id_type=MESH)


def _mod_exchange(c_row, w_ada_shard, b_ada_row, ws, early):
    cs = w_ada_shard.shape[1]
    n = len(ws)

    def body(c_ref, w_hbm, b_ref, *refs):
        shards, (mod_ref, sc_ref), zones = refs[:n], refs[n:n + 2], refs[n + 2:2 * n + 2]
        rows_ref, w_ref, w_sem, send1, recv1, send2, recv2 = refs[2 * n + 2:2 * n + 9]
        wide, narrow = refs[2 * n + 9:3 * n + 9], refs[3 * n + 9:4 * n + 9]
        cast_load, cast_store = refs[4 * n + 9:]
        x, y, c = _position()
        me = 4 * x + 2 * y + c
        chip = 2 * x + y
        w_load = pltpu.make_async_copy(w_hbm, w_ref, w_sem)
        w_load.start()
        loads = [pltpu.make_async_copy(shards[i], wide[i], cast_load.at[i]) for i in range(n)]
        stores = [pltpu.make_async_copy(narrow[i], zones[i].at[chip], cast_store.at[i]) for i in range(n)]

        def cast(i):
            loads[i].wait()
            narrow[i][...] = wide[i][...].astype(BF16)
            stores[i].start()

        for i in range(early):
            loads[i].start()
        cv = c_ref[...]
        sc_ref[me] = cv * jax.nn.sigmoid(cv)
        gather = [_remote(sc_ref.at[me], sc_ref.at[me], send1.at[k - 1], recv1.at[k - 1], _peer(x, y, c, k))
                  for k in range(1, N_DEV)]
        for cp in gather:
            cp.start()
        for i in range(early):
            cast(i)
        for i in range(early, n):
            loads[i].start()
        for k in range(1, N_DEV):
            px, py, pc = _peer(x, y, c, k)
            src = 4 * px + 2 * py + pc
            _remote(sc_ref.at[src], sc_ref.at[src], send1.at[k - 1], recv1.at[k - 1], (px, py, pc)).wait_recv()
        for cp in gather:
            cp.wait_send()
        sc_all = jnp.concatenate([sc_ref[b] for b in range(N_DEV)], axis=0)
        w_load.wait()
        part = _dot(sc_all.astype(BF16), w_ref[...].astype(BF16))
        part = part + b_ref[:, pl.ds(pl.multiple_of(chip * cs, LANES), cs)]
        for b in range(N_DEV):
            rows_ref[b] = part[b:b + 1, :]
        mod_ref[chip] = rows_ref[me]
        hand = []
        for k in (2, 4, 6):
            px, py, _ = _peer(x, y, c, k)
            hand.append(_remote(rows_ref.at[4 * px + 2 * py + c], mod_ref.at[chip], send2.at[k // 2 - 1],
                                recv2.at[k // 2 - 1], (px, py, c)))
        for cp in hand:
            cp.start()
        for i in range(early, n):
            cast(i)
        for k in (2, 4, 6):
            px, py, _ = _peer(x, y, c, k)
            pchip = 2 * px + py
            _remote(rows_ref.at[me], mod_ref.at[pchip], send2.at[k // 2 - 1], recv2.at[k // 2 - 1],
                    (px, py, c)).wait_recv()
        for cp in hand:
            cp.wait_send()
        for cp in stores:
            cp.wait()

    out = pl.pallas_call(
        body, name="mod_exchange",
        in_specs=[_VMEM, _ANY, _VMEM] + [_ANY] * n, out_specs=[_VMEM, _VMEM] + [_ANY] * n,
        out_shape=[jax.ShapeDtypeStruct((N_CHIPS, 1, cs), F32), jax.ShapeDtypeStruct((N_DEV, 1, D_MODEL), F32)]
        + [jax.ShapeDtypeStruct((N_CHIPS,) + w.shape, BF16) for w in ws],
        scratch_shapes=[pltpu.VMEM((N_DEV, 1, cs), F32), pltpu.VMEM(w_ada_shard.shape, F32), pltpu.SemaphoreType.DMA,
                        pltpu.SemaphoreType.DMA((N_DEV - 1,)),
                        pltpu.SemaphoreType.DMA((N_DEV - 1,)), pltpu.SemaphoreType.DMA((N_CHIPS - 1,)),
                        pltpu.SemaphoreType.DMA((N_CHIPS - 1,))]
        + [pltpu.VMEM(w.shape, F32) for w in ws] + [pltpu.VMEM(w.shape, BF16) for w in ws]
        + [pltpu.SemaphoreType.DMA((n,))] * 2,
        compiler_params=pltpu.CompilerParams(vmem_limit_bytes=VMEM_LIMIT),
    )(c_row, w_ada_shard, b_ada_row, *ws)
    return out[0], out[1], list(out[2:])


_HBM = pl.BlockSpec(memory_space=pltpu.HBM)
_SEM = pl.BlockSpec(memory_space=pltpu.SEMAPHORE)
_EFFECT = pltpu.SideEffectType.DATAFLOW_SIDE_EFFECTING
_CHIP_HOPS = (2, 4, 6)


def _in_hbm(a):
    return pltpu.with_memory_space_constraint(a, pltpu.HBM)


def _sems3():
    return pltpu.SemaphoreType.DMA((len(_CHIP_HOPS),))


def _ag_start(lands, after, name):
    n = len(lands)

    def body(*refs):
        zones = refs[:n]
        sends, recvs = refs[n + 1:2 * n + 1], refs[2 * n + 1:3 * n + 1]
        x, y, c = _position()
        chip = 2 * x + y
        for i in range(n):
            half = zones[i].shape[1] // 2
            mine = zones[i].at[chip, pl.ds(c * half, half)]
            for s, k in enumerate(_CHIP_HOPS):
                px, py, _ = _peer(x, y, c, k)
                _remote(mine, mine, sends[i].at[s], recvs[i].at[s], (px, py, c)).start()

    out = pl.pallas_call(
        body, name=name,
        in_specs=[_HBM] * n + [_ANY],
        out_specs=[_SEM] * (2 * n) + [_HBM] * n,
        out_shape=[_sems3()] * (2 * n) + [pltpu.HBM(z.shape, BF16) for z in lands],
        input_output_aliases={i: 2 * n + i for i in range(n)},
        compiler_params=pltpu.CompilerParams(has_side_effects=_EFFECT),
    )(*[_in_hbm(z) for z in lands], after)
    return [(out[2 * n + i], out[i], out[n + i]) for i in range(n)]


def _ag_pass(group, after, name):
    n = len(group)

    def body(*refs):
        zones = refs[:n]
        sends, recvs = refs[n:2 * n], refs[2 * n:3 * n]
        fsends, frecvs = refs[4 * n + 1:5 * n + 1], refs[5 * n + 1:6 * n + 1]
        x, y, c = _position()
        chip = 2 * x + y
        for i in range(n):
            half = zones[i].shape[1] // 2
            rows = pl.ds(c * half, half)
            for s, k in enumerate(_CHIP_HOPS):
                px, py, _ = _peer(x, y, c, k)
                landed = zones[i].at[2 * px + py, rows]
                _remote(landed, landed, sends[i].at[s], recvs[i].at[s], (px, py, c)).wait_recv()
                _remote(landed, landed, fsends[i].at[s], frecvs[i].at[s], (x, y, 1 - c)).start()
        for i in range(n):
            half = zones[i].shape[1] // 2
            mine = zones[i].at[chip, pl.ds(c * half, half)]
            for s, k in enumerate(_CHIP_HOPS):
                px, py, _ = _peer(x, y, c, k)
                _remote(mine, mine, sends[i].at[s], recvs[i].at[s], (px, py, c)).wait_send()

    out = pl.pallas_call(
        body, name=name,
        in_specs=[_HBM] * n + [_SEM] * (2 * n) + [_ANY],
        out_specs=[_HBM] * n + [_SEM] * (2 * n),
        out_shape=[pltpu.HBM(g[0].shape, BF16) for g in group] + [_sems3()] * (2 * n),
        input_output_aliases={i: i for i in range(n)},
        compiler_params=pltpu.CompilerParams(has_side_effects=_EFFECT),
    )(*[g[0] for g in group], *[g[1] for g in group], *[g[2] for g in group], after)
    return [(out[i], out[n + i], out[2 * n + i]) for i in range(n)]


def _ag_done(group, name):
    n = len(group)

    def body(*refs):
        lands = refs[:n]
        fsends, frecvs = refs[n:2 * n], refs[2 * n:3 * n]
        x, y, c = _position()
        for i in range(n):
            half = lands[i].shape[1] // 2
            for s, k in enumerate(_CHIP_HOPS):
                px, py, _ = _peer(x, y, c, k)
                sent = lands[i].at[2 * px + py, pl.ds(c * half, half)]
                got = lands[i].at[2 * px + py, pl.ds((1 - c) * half, half)]
                cp = _remote(sent, got, fsends[i].at[s], frecvs[i].at[s], (x, y, 1 - c))
                cp.wait_recv()
                cp.wait_send()

    out = pl.pallas_call(
        body, name=name,
        in_specs=[_HBM] * n + [_SEM] * (2 * n),
        out_specs=[_HBM] * n,
        out_shape=[pltpu.HBM(g[0].shape, BF16) for g in group],
        input_output_aliases={i: i for i in range(n)},
        compiler_params=pltpu.CompilerParams(has_side_effects=_EFFECT),
    )(*[g[0] for g in group], *[g[1] for g in group], *[g[2] for g in group])
    return list(out)


def _small_spread_wait(slots, sends, recvs, afters):
    def body(z_ref, sends, recvs, *rest):
        x, y, c = _position()
        mine = z_ref.at[2 * x + y]
        for s, k in enumerate(_CHIP_HOPS):
            px, py, _ = _peer(x, y, c, k)
            cp = _remote(mine, z_ref.at[2 * px + py], sends.at[s], recvs.at[s], (px, py, c))
            cp.wait_recv()
            cp.wait_send()

    return pl.pallas_call(
        body, name="small_spread_wait",
        in_specs=[_HBM, _SEM, _SEM] + [_ANY] * len(afters), out_specs=_HBM, out_shape=pltpu.HBM(slots.shape, F32),
        input_output_aliases={0: 0},
        compiler_params=pltpu.CompilerParams(has_side_effects=_EFFECT),
    )(slots, sends, recvs, *afters)


def _sibling_sum(pgs, name, small=None):
    n = len(pgs)
    k = 0 if small is None else 1
    units = [(i, j) for i in range(n) for j in range(N_CHIPS)]

    def body(*refs):
        refs = list(refs)
        take = lambda count: [refs.pop(0) for _ in range(count)]
        ins, small_in = take(n), take(k)
        qbs, owns, slots_out = take(n), take(n), take(k)
        mine, other, stage, got = take(n), take(n), take(n), take(n)
        load_a, load_b, send, recv, keep_wire, keep_own = take(6)
        x, y, c = _position()
        chip = 2 * x + y
        if k:
            sib_ref, pair_send, pair_recv, keep_small = take(4)
            pair = _remote(small_in[0], sib_ref, pair_send, pair_recv, (x, y, 1 - c))
            pair.start()
        loads_a = [pltpu.make_async_copy(ins[i].at[j, 1 - c], other[i].at[j], load_a.at[u])
                   for u, (i, j) in enumerate(units)]
        loads_b = [pltpu.make_async_copy(ins[i].at[j, c], mine[i].at[j], load_b.at[u])
                   for u, (i, j) in enumerate(units)]
        for cp in loads_a + loads_b:
            cp.start()
        sent = []
        for u, (i, j) in enumerate(units):
            loads_a[u].wait()
            stage[i][j] = other[i][j].astype(BF16)
            cp = _remote(stage[i].at[j], got[i].at[j], send.at[u], recv.at[u], (x, y, 1 - c))
            cp.start()
            sent.append(cp)
        stores = []
        for u, (i, j) in enumerate(units):
            loads_b[u].wait()
            sent[u].wait_recv()
            q = mine[i][j] + got[i][j].astype(F32)
            mine[i][j] = q
            got[i][j] = q.astype(BF16)
            stores.append(pltpu.make_async_copy(got[i].at[j], qbs[i].at[j], keep_wire.at[u]))
            stores[-1].start()
        for i in range(n):
            stores.append(pltpu.make_async_copy(mine[i].at[chip], owns[i], keep_own.at[i]))
            stores[-1].start()
        if k:
            pair.wait()
            sib_ref[...] = small_in[0][...] + sib_ref[...]
            stores.append(pltpu.make_async_copy(sib_ref, slots_out[0].at[chip], keep_small))
            stores[-1].start()
        for cp in sent:
            cp.wait_send()
        for cp in stores:
            cp.wait()

    wire = [(N_CHIPS,) + p.shape[2:] for p in pgs]
    extra_out, extra_scratch = [], []
    if k:
        extra_out = [jax.ShapeDtypeStruct((N_CHIPS,) + small.shape, F32)]
        extra_scratch = [pltpu.VMEM(small.shape, F32)] + [pltpu.SemaphoreType.DMA] * 3
    out = pl.pallas_call(
        body, name=name, in_specs=[_ANY] * n + [_VMEM] * k, out_specs=[_ANY] * (2 * n + k),
        out_shape=[jax.ShapeDtypeStruct(w, BF16) for w in wire] + [jax.ShapeDtypeStruct(w[1:], F32) for w in wire]
        + extra_out,
        scratch_shapes=[pltpu.VMEM(w, F32) for w in wire] * 2 + [pltpu.VMEM(w, BF16) for w in wire] * 2
        + [pltpu.SemaphoreType.DMA((len(units),))] * 5 + [pltpu.SemaphoreType.DMA((n,))] + extra_scratch,
        compiler_params=pltpu.CompilerParams(vmem_limit_bytes=VMEM_LIMIT),
    )(*pgs, *([small] if k else []))
    return list(out[:n]), list(out[n:2 * n]), list(out[2 * n:])


def _rs_start(qbs, name, small=None):
    n = len(qbs)
    k = 0 if small is None else 1

    def body(*refs):
        outs, inboxes, slots = refs[:n], refs[n:2 * n], refs[2 * n:2 * n + k]
        sems = refs[2 * n + k:4 * n + 3 * k]
        sends, recvs = sems[:n], sems[n:2 * n]
        x, y, c = _position()
        chip = 2 * x + y
        if k:
            mine = slots[0].at[chip]
            for s, hop in enumerate(_CHIP_HOPS):
                px, py, _ = _peer(x, y, c, hop)
                _remote(mine, mine, sems[2 * n].at[s], sems[2 * n + 1].at[s], (px, py, c)).start()
        for i in range(n):
            for s, hop in enumerate(_CHIP_HOPS):
                px, py, _ = _peer(x, y, c, hop)
                _remote(outs[i].at[2 * px + py], inboxes[i].at[chip], sends[i].at[s], recvs[i].at[s], (px, py, c)).start()

    inboxes = [_in_hbm(lax.empty(q.shape, BF16)) for q in qbs]
    n_sems = 2 * n + 2 * k
    out = pl.pallas_call(
        body, name=name,
        in_specs=[_HBM] * (2 * n + k),
        out_specs=[_SEM] * n_sems + [_HBM] * (2 * n + k),
        out_shape=[_sems3()] * n_sems + [pltpu.HBM(q.shape, BF16) for q in qbs] * 2
        + ([pltpu.HBM(small.shape, F32)] if k else []),
        input_output_aliases={i: n_sems + i for i in range(2 * n + k)},
        compiler_params=pltpu.CompilerParams(has_side_effects=_EFFECT),
    )(*[_in_hbm(q) for q in qbs], *inboxes, *([_in_hbm(small)] if k else []))
    states = [(out[n_sems + i], out[n_sems + n + i], out[i], out[n + i]) for i in range(n)]
    return (states, (out[n_sems + 2 * n], out[2 * n], out[2 * n + 1])) if k else states


def _rs_wait(group, after, name):
    n = len(group)

    def body(*refs):
        outs, inboxes = refs[:n], refs[n:2 * n]
        sends, recvs = refs[2 * n:3 * n], refs[3 * n:4 * n]
        x, y, c = _position()
        for i in range(n):
            for s, k in enumerate(_CHIP_HOPS):
                px, py, _ = _peer(x, y, c, k)
                slot = 2 * px + py
                cp = _remote(outs[i].at[slot], inboxes[i].at[slot], sends[i].at[s], recvs[i].at[s], (px, py, c))
                cp.wait_recv()
                cp.wait_send()

    out = pl.pallas_call(
        body, name=name,
        in_specs=[_HBM] * (2 * n) + [_SEM] * (2 * n) + [_ANY],
        out_specs=[_HBM] * n,
        out_shape=[pltpu.HBM(g[1].shape, BF16) for g in group],
        input_output_aliases={n + i: i for i in range(n)},
        compiler_params=pltpu.CompilerParams(has_side_effects=_EFFECT),
    )(*[g[0] for g in group], *[g[1] for g in group], *[g[2] for g in group], *[g[3] for g in group], after)
    return list(out)


def _final_share(inboxes, owns, name):
    n = len(inboxes)
    units = [(i, s) for i in range(n) for s in range(len(_CHIP_HOPS))]

    def body(*refs):
        ins, mine, outs, landed, half = (refs[k * n:(k + 1) * n] for k in range(5))
        load, load_own, keep, send, recv = refs[5 * n:]
        x, y, c = _position()
        loads = []
        for u, (i, s) in enumerate(units):
            px, py, _ = _peer(x, y, c, _CHIP_HOPS[s])
            loads.append(pltpu.make_async_copy(ins[i].at[2 * px + py], landed[i].at[s], load.at[u]))
        loads_own = [pltpu.make_async_copy(mine[i], half[i], load_own.at[i]) for i in range(n)]
        for cp in loads + loads_own:
            cp.start()
        copies = []
        for i in range(n):
            for s in range(len(_CHIP_HOPS)):
                loads[len(_CHIP_HOPS) * i + s].wait()
            loads_own[i].wait()
            total = (landed[i][0].astype(F32) + landed[i][1].astype(F32)) + landed[i][2].astype(F32)
            half[i][...] = total + half[i][...]
            copies.append(pltpu.make_async_copy(half[i], outs[i].at[c], keep.at[i]))
            copies.append(_remote(half[i], outs[i].at[c], send.at[i], recv.at[i], (x, y, 1 - c)))
            for cp in copies[-2:]:
                cp.start()
        for i in range(n):
            theirs = outs[i].at[1 - c]
            _remote(theirs, theirs, send.at[i], recv.at[i], (x, y, 1 - c)).wait_recv()
        for i in range(n):
            copies[2 * i].wait()
            copies[2 * i + 1].wait_send()

    return pl.pallas_call(
        body, name=name, in_specs=[_ANY] * (2 * n), out_specs=[_ANY] * n,
        out_shape=[jax.ShapeDtypeStruct((2,) + o.shape, F32) for o in owns],
        scratch_shapes=[pltpu.VMEM((len(_CHIP_HOPS),) + o.shape, BF16) for o in owns]
        + [pltpu.VMEM(o.shape, F32) for o in owns]
        + [pltpu.SemaphoreType.DMA((len(units),))] + [pltpu.SemaphoreType.DMA((n,))] * 4,
        compiler_params=pltpu.CompilerParams(vmem_limit_bytes=VMEM_LIMIT),
    )(*inboxes, *owns)


_SMALL = (("b_ada", N_MOD * D_MODEL), ("norm1_pre", D_MODEL), ("norm1_post", D_MODEL), ("norm2_pre", D_MODEL),
          ("norm2_post", D_MODEL), ("w_spatial", N_HEADS * CHUNK * CHUNK), ("b_spatial", N_HEADS * CHUNK),
          ("ln_v_gain", D_A), ("ln_v_bias", D_A), ("w_pool", N_HEADS * GROUP_DIM * GROUP_DIM),
          ("b_pool", D_B), ("pool_scale", D_B))
_MOD_ROWS = N_MOD * D_MODEL // LANES


def _packed_rows(size):
    return -(-(size // LANES) // SUBLANES) * SUBLANES


def _pack(parts):
    out = []
    for name, size in _SMALL:
        a = parts[name].reshape(size // LANES, LANES)
        pad = _packed_rows(size) - a.shape[0]
        out.append(jnp.pad(a, ((0, pad), (0, 0))) if pad else a)
    return out


def _small_adamw(slots, ws, ms, vs):
    n = len(_SMALL)
    head = N_DEV * _MOD_ROWS

    def body(*refs):
        s_ref, w, m, v = refs[0], refs[1:1 + n], refs[1 + n:1 + 2 * n], refs[1 + 2 * n:1 + 3 * n]
        outs = refs[1 + 3 * n:1 + 7 * n]
        dmod_ref, loss_ref, t_ref = refs[1 + 7 * n:]
        t_ref[...] = ((s_ref[0] + s_ref[1]) + s_ref[2]) + s_ref[3]
        for r in range(head):
            b, k = divmod(r, _MOD_ROWS)
            dmod_ref[b:b + 1, k * LANES:(k + 1) * LANES] = t_ref[r:r + 1, :]
        loss_ref[...] = t_ref[t_ref.shape[0] - SUBLANES:t_ref.shape[0] - SUBLANES + 1, 0:1]
        row = head
        for i, (_, size) in enumerate(_SMALL):
            if i == 0:
                g = t_ref[0:_MOD_ROWS, :]
                for b in range(1, N_DEV):
                    g = g + t_ref[b * _MOD_ROWS:(b + 1) * _MOD_ROWS, :]
            else:
                g = t_ref[row:row + size // LANES, :]
                row += _packed_rows(size)
            d, nm, nv = _adamw_math(w[i][...], g, m[i][...], v[i][...])
            for ref, val in zip(outs[4 * i:4 * i + 4], (g, d, nm, nv)):
                ref[...] = val

    each = [jax.ShapeDtypeStruct((size // LANES, LANES), F32) for _, size in _SMALL for _ in range(4)]
    out = pl.pallas_call(
        body, name="small_adamw",
        out_shape=each + [jax.ShapeDtypeStruct((N_DEV, N_MOD * D_MODEL), F32), jax.ShapeDtypeStruct((1, 1), F32)],
        scratch_shapes=[pltpu.VMEM(slots.shape[1:], F32)],
        compiler_params=pltpu.CompilerParams(vmem_limit_bytes=VMEM_LIMIT),
    )(slots, *ws, *ms, *vs)
    return [out[4 * i:4 * i + 4] for i in range(n)], out[4 * n], out[4 * n + 1]


def kernel(x, c, w_ada, b_ada, norm1_pre, norm1_post, w_in, w_spatial, b_spatial, ln_v_gain, ln_v_bias, w_pool, b_pool, pool_scale, w_out, norm2_pre, norm2_post, w_fc1, w_fc2, loss_target, m_w_ada, m_b_ada, m_norm1_pre, m_norm1_post, m_w_in, m_w_spatial, m_b_spatial, m_ln_v_gain, m_ln_v_bias, m_w_pool, m_b_pool, m_pool_scale, m_w_out, m_norm2_pre, m_norm2_post, m_w_fc1, m_w_fc2, v_w_ada, v_b_ada, v_norm1_pre, v_norm1_post, v_w_in, v_w_spatial, v_b_spatial, v_ln_v_gain, v_ln_v_bias, v_w_pool, v_b_pool, v_pool_scale, v_w_out, v_norm2_pre, v_norm2_post, v_w_fc1, v_w_fc2):
    weights = dict(w_ada=w_ada, b_ada=b_ada, norm1_pre=norm1_pre, norm1_post=norm1_post, w_in=w_in,
                   w_spatial=w_spatial, b_spatial=b_spatial, ln_v_gain=ln_v_gain, ln_v_bias=ln_v_bias, w_pool=w_pool,
                   b_pool=b_pool, pool_scale=pool_scale, w_out=w_out, norm2_pre=norm2_pre, norm2_post=norm2_post,
                   w_fc1=w_fc1, w_fc2=w_fc2)
    m_old = dict(w_ada=m_w_ada, b_ada=m_b_ada, norm1_pre=m_norm1_pre, norm1_post=m_norm1_post, w_in=m_w_in,
                 w_spatial=m_w_spatial, b_spatial=m_b_spatial, ln_v_gain=m_ln_v_gain, ln_v_bias=m_ln_v_bias,
                 w_pool=m_w_pool, b_pool=m_b_pool, pool_scale=m_pool_scale, w_out=m_w_out, norm2_pre=m_norm2_pre,
                 norm2_post=m_norm2_post, w_fc1=m_w_fc1, w_fc2=m_w_fc2)
    v_old = dict(w_ada=v_w_ada, b_ada=v_b_ada, norm1_pre=v_norm1_pre, norm1_post=v_norm1_post, w_in=v_w_in,
                 w_spatial=v_w_spatial, b_spatial=v_b_spatial, ln_v_gain=v_ln_v_gain, ln_v_bias=v_ln_v_bias,
                 w_pool=v_w_pool, b_pool=v_b_pool, pool_scale=v_pool_scale, w_out=v_w_out, norm2_pre=v_norm2_pre,
                 norm2_post=v_norm2_post, w_fc1=v_w_fc1, w_fc2=v_w_fc2)
    order = ("w_ada", "b_ada", "norm1_pre", "norm1_post", "w_in", "w_spatial", "b_spatial", "ln_v_gain", "ln_v_bias",
             "w_pool", "b_pool", "pool_scale", "w_out", "norm2_pre", "norm2_post", "w_fc1", "w_fc2")
    mx, my, mc = _position()
    me = 4 * mx + 2 * my + mc
    chip = 2 * mx + my
    row = lambda a: a.reshape(1, -1)

    xs, target = x[0], loss_target[0]
    n1pre, n1post, n2pre, n2post = row(norm1_pre), row(norm1_post), row(norm2_pre), row(norm2_post)
    mixer = (w_spatial, jnp.repeat(b_spatial.T, HEAD_DIM, axis=1), row(ln_v_gain), row(ln_v_bias), w_pool,
             row(b_pool), row(pool_scale))
    ts_big, ts_mid = 512, 256

    mod4, sc_all, lands = _mod_exchange(c, w_ada, row(b_ada), [w_in, w_out, w_fc1, w_fc2], 2)
    mod6 = mod4.reshape(N_MOD, D_MODEL)
    ag = _ag_start(lands, mod4, "ag_start")

    win_g, wout_g = _ag_done(_ag_pass([ag[0], ag[1]], ag[2][0], "ag_pass_mix"), "ag_done_mix")
    z, ycat, mix, x1, h2 = _fwd_mix(xs, mod6, n1pre, n1post, n2pre, win_g, wout_g, *mixer, ts_big)
    (fc1_g,) = _ag_done(_ag_pass([ag[2]], h2, "ag_pass_fc1"), "ag_done_fc1")
    q = _fwd_fc1(h2, fc1_g, ts_big)
    (fc2_g,) = _ag_done(_ag_pass([ag[3]], q, "ag_pass_fc2"), "ag_done_fc2")
    dy, df, loss, s2 = _fwd_fc2_loss(q, x1, target, mod6, n2post, fc2_g, ts_big)

    def reduce_start(partials, tag, small=None):
        wire, owns, slots = _sibling_sum(partials, "sibling_sum_" + tag, small)
        return _rs_start(wire, "rs_start_" + tag), owns, slots

    def reduce_finish(state, owns, names, tag, dep):
        inboxes = _rs_wait(state, dep, "rs_wait_" + tag)
        shards = _final_share(inboxes, owns, "final_share_" + tag)
        updates = _adamw([g.reshape(weights[n].shape) for n, g in zip(names, shards)], [weights[n] for n in names],
                         [m_old[n] for n in names], [v_old[n] for n in names], "adamw_" + tag, 4)
        for n, (g, d, nm, nv) in zip(names, updates):
            grads[n], deltas[n], new_m[n], new_v[n] = g, d, nm, nv

    grads, deltas, new_m, new_v = {}, {}, {}, {}
    dp, g_fc2 = _bwd_fc2(df, q, fc2_g, ts_big)
    state_fc2, owns_fc2, _ = reduce_start([g_fc2], "fc2")
    dx1, dyc, dshift2, da2, s1, g_fc1, g_out = _bwd_fc1_out(
        dp, dy, x1, mix, h2, ycat, mod6, n2pre, n1post, fc1_g, wout_g, state_fc2[0][0], ts_mid)
    state_mid, owns_mid, _ = reduce_start([g_fc1, g_out], "mid")
    dz, dws, dbsp, dgain, dbias, dwp, dbp, dps = _mixer_bwd(z, dyc, *mixer, state_mid[0][0], ts_big)
    grad_x, dshift1, da1, g_in = _bwd_in(dz, dx1, xs, mod6, n1pre, win_g, state_mid[0][0], ts_big)
    dmod6, dnorms = _mod_grads(da1, dshift1, s1, da2, dshift2, s2, mod6, n1pre, n1post, n2pre, n2post)

    parts = dict(b_ada=dmod6, norm1_pre=dnorms[0], norm1_post=dnorms[1], norm2_pre=dnorms[2], norm2_post=dnorms[3],
                 w_spatial=dws, b_spatial=dbsp, ln_v_gain=dgain, ln_v_bias=dbias, w_pool=dwp, b_pool=dbp,
                 pool_scale=dps)
    pieces = _pack(parts)
    slots = lax.dynamic_update_slice(jnp.zeros((N_DEV * _MOD_ROWS, LANES), F32), pieces[0], (me * _MOD_ROWS, 0))
    loss_tile = jnp.pad(loss, ((0, SUBLANES - 1), (0, LANES - 1)))
    wire_in, owns_in, pair_sum = _sibling_sum([g_in], "sibling_sum_in",
                                              jnp.concatenate([slots] + pieces[1:] + [loss_tile], axis=0))
    state_in, spread = _rs_start(wire_in, "rs_start_in", pair_sum[0])
    reduce_finish(state_fc2 + state_mid, owns_fc2 + owns_mid, ("w_fc2", "w_fc1", "w_out"), "mlp", state_in[0][0])
    flat = lambda d: [d[n].reshape(size // LANES, LANES) for n, size in _SMALL]
    small_out, dmod_all, loss = _small_adamw(
        _small_spread_wait(*spread, [deltas[n] for n in ("w_fc2", "w_fc1", "w_out")]), flat(weights), flat(m_old),
        flat(v_old))
    loss = loss.reshape(())
    for (n, _), (g, d, nm, nv) in zip(_SMALL, small_out):
        shape = weights[n].shape
        grads[n], deltas[n], new_m[n], new_v[n] = g.reshape(shape), d.reshape(shape), nm.reshape(shape), nv.reshape(shape)

    sc_t = sc_all.reshape(N_DEV, D_MODEL).T
    grads["w_ada"], deltas["w_ada"], new_m["w_ada"], new_v["w_ada"] = _ada_grad_adamw(
        sc_t, dmod_all, w_ada, m_w_ada, v_w_ada, 256)

    reduce_finish(state_in, owns_in, ("w_in",), "in", deltas["w_ada"])

    return (loss, grad_x[None], *[grads[n] for n in order], *[deltas[n] for n in order],
            *[new_m[n] for n in order], *[new_v[n] for n in order])
```
